```python
import math
import jax, jax.numpy as jnp
from jax import lax
import numpy as np

D_MODEL = 1024
BATCH = 8
SEQ = 8192
DEPTH = 1

N_META = 16
BLOCK = 128
PAD = BLOCK - N_META
EPS = 1e-6
NEG = -1e30
SSD_INNER = 2 * D_MODEL
SSD_HEADDIM = 64
SSD_HEADS = SSD_INNER // SSD_HEADDIM
SSD_GROUPS = 4
SSD_HPG = SSD_HEADS // SSD_GROUPS
SSD_STATE = 128
SSD_CONV = 4
SSD_XBC = SSD_INNER + 2 * SSD_GROUPS * SSD_STATE
ATT_HEADS = 16
ATT_KV_HEADS = 2
ATT_HEADDIM = 64
ATT_GQ = ATT_HEADS // ATT_KV_HEADS
WINDOW = 128
ATT_Q = ATT_HEADS * ATT_HEADDIM
ATT_KV = ATT_KV_HEADS * ATT_HEADDIM
REL_BUCKETS = 32
REL_MAX_DIST = 128
D_FF = 2816
FFN_CONV = 3
N_BRANCH = 2
IN_WIDTHS = (SSD_INNER, SSD_XBC, SSD_HEADS, ATT_Q, ATT_KV, ATT_KV, D_MODEL, D_MODEL)
D_IN_PROJ = SSD_INNER + SSD_XBC + SSD_HEADS + ATT_Q + 2 * ATT_KV + N_BRANCH * D_MODEL

kernel_name = 'hybrid_ssd_swa_sink_convffn_block'


def rmsnorm(x, w):
    xf = x.astype(jnp.float32)
    y = xf * lax.rsqrt(jnp.mean(xf * xf, axis=-1, keepdims=True) + EPS)
    return (y * w.astype(jnp.float32)).astype(x.dtype)


def split_cols(t, widths):
    idx, acc = [], 0
    for w in widths[:-1]:
        acc += w
        idx.append(acc)
    return jnp.split(t, idx, axis=-1)


def causal_dwconv(x, w, b):
    k = w.shape[0]
    y = lax.conv_general_dilated(x, w[:, None, :].astype(x.dtype), window_strides=(1,),
                                 padding=[(k - 1, 0)], dimension_numbers=('NWC', 'WIO', 'NWC'),
                                 feature_group_count=x.shape[-1])
    return y + b.astype(x.dtype)


def t5_bucket(dist):
    max_exact = REL_BUCKETS // 2
    d = jnp.maximum(dist, 0)
    big = max_exact + (jnp.log(jnp.maximum(d, max_exact).astype(jnp.float32) / max_exact)
                       / math.log(REL_MAX_DIST / max_exact) * (REL_BUCKETS - max_exact)).astype(jnp.int32)
    return jnp.where(d < max_exact, d, jnp.minimum(big, REL_BUCKETS - 1))


def ssd_chunked(x, dt, a, bmat, cmat):
    bsz, lp = x.shape[:2]
    nc = lp // BLOCK
    f32 = jnp.float32
    xc = x.reshape(bsz, nc, BLOCK, SSD_GROUPS, SSD_HPG, SSD_HEADDIM).astype(f32)
    dtc = dt.reshape(bsz, nc, BLOCK, SSD_GROUPS, SSD_HPG)
    bc = bmat.reshape(bsz, nc, BLOCK, SSD_GROUPS, SSD_STATE).astype(f32)
    cc = cmat.reshape(bsz, nc, BLOCK, SSD_GROUPS, SSD_STATE).astype(f32)
    cs = jnp.cumsum(dtc * a.reshape(SSD_GROUPS, SSD_HPG), axis=2)
    xdt = xc * dtc[..., None]
    cs_t = jnp.moveaxis(cs, 2, -1)
    causal = jnp.tril(jnp.ones((BLOCK, BLOCK), dtype=bool))
    lmat = jnp.exp(jnp.where(causal, cs_t[..., :, None] - cs_t[..., None, :], -jnp.inf))
    cb = jnp.einsum('bclgn,bcsgn->bcgls', cc, bc)
    y_diag = jnp.einsum('bcgls,bcgrls,bcsgrp->bclgrp', cb, lmat, xdt)
    decay_states = jnp.exp(cs[:, :, -1:] - cs)
    states = jnp.einsum('bclgn,bclgr,bclgrp->bcgrpn', bc, decay_states, xdt)
    chunk_decay = jnp.exp(cs[:, :, -1])

    def step(hst, inp):
        st, dec = inp
        return dec[..., None, None] * hst + st, hst

    h0 = jnp.zeros((bsz, SSD_GROUPS, SSD_HPG, SSD_HEADDIM, SSD_STATE), f32)
    _, prev = lax.scan(step, h0, (jnp.moveaxis(states, 1, 0), jnp.moveaxis(chunk_decay, 1, 0)))
    prev = jnp.moveaxis(prev, 0, 1)
    y_off = jnp.einsum('bclgn,bcgrpn,bclgr->bclgrp', cc, prev, jnp.exp(cs))
    return (y_diag + y_off).reshape(bsz, lp, SSD_HEADS, SSD_HEADDIM)


def gated_group_rmsnorm(y, z, w):
    g = y * jax.nn.silu(z.astype(jnp.float32))
    gg = g.reshape(g.shape[:-1] + (SSD_GROUPS, SSD_INNER // SSD_GROUPS))
    gg = gg * lax.rsqrt(jnp.mean(gg * gg, axis=-1, keepdims=True) + EPS)
    return gg.reshape(g.shape) * w.astype(jnp.float32)


def swa_sink_attention(q, k, v, rel_bias, sinks):
    bsz, lp = q.shape[:2]
    nb = lp // BLOCK
    f32 = jnp.float32
    scale = ATT_HEADDIM ** -0.5
    qb = q.reshape(bsz, nb, BLOCK, ATT_KV_HEADS, ATT_GQ, ATT_HEADDIM)
    kb = k.reshape(bsz, nb, BLOCK, ATT_KV_HEADS, ATT_HEADDIM)
    vb = v.reshape(bsz, nb, BLOCK, ATT_KV_HEADS, ATT_HEADDIM)

    def with_prev(t):
        prev = jnp.pad(t, ((0, 0), (1, 0), (0, 0), (0, 0), (0, 0)))[:, :-1]
        return jnp.concatenate([prev, t], axis=2)

    kband, vband = with_prev(kb), with_prev(vb)
    pos = jnp.arange(lp, dtype=jnp.int32).reshape(nb, BLOCK)
    k_pos = jnp.concatenate([pos - BLOCK, pos], axis=1)
    dist = pos[:, :, None] - k_pos[:, None, :]
    mask_band = (k_pos[:, None, :] >= BLOCK) & (dist >= 0) & (dist < WINDOW)
    bias_band = rel_bias[t5_bucket(dist[0])].astype(f32)
    bias_band = jnp.transpose(bias_band, (2, 0, 1)).reshape(ATT_KV_HEADS, ATT_GQ, BLOCK, 2 * BLOCK)
    s_band = jnp.einsum('bnqkgd,bnskd->bnkgqs', qb, kband, preferred_element_type=f32) * scale + bias_band
    s_band = jnp.where(mask_band[None, :, None, None], s_band, NEG)
    meta_pos = PAD + jnp.arange(N_META, dtype=jnp.int32)
    k_meta, v_meta = k[:, PAD:BLOCK], v[:, PAD:BLOCK]
    dist_m = pos[:, :, None] - meta_pos[None, None, :]
    bias_m = rel_bias[t5_bucket(dist_m)].astype(f32)
    bias_m = jnp.transpose(bias_m, (0, 3, 1, 2)).reshape(nb, ATT_KV_HEADS, ATT_GQ, BLOCK, N_META)
    s_meta = jnp.einsum('bnqkgd,bmkd->bnkgqm', qb, k_meta, preferred_element_type=f32) * scale + bias_m[None]
    s_meta = jnp.where((dist_m >= 0)[None, :, None, None], s_meta, NEG)
    sink = jnp.broadcast_to(sinks.astype(f32).reshape(1, 1, ATT_KV_HEADS, ATT_GQ, 1, 1),
                            s_meta.shape[:-1] + (1,))
    p = jax.nn.softmax(jnp.concatenate([s_meta, s_band, sink], axis=-1), axis=-1)
    p_meta = p[..., :N_META].astype(v.dtype)
    p_band = p[..., N_META:-1].astype(v.dtype)
    out = (jnp.einsum('bnkgqm,bmkd->bnqkgd', p_meta, v_meta)
           + jnp.einsum('bnkgqs,bnskd->bnqkgd', p_band, vband))
    return out.reshape(bsz, lp, ATT_Q)


def mixer_block(h, valid_f32, norm_w, w_in, conv_w, conv_b, dt_bias, a_log, d_skip, ssd_norm_w,
                w_ssd_branch, w_attn_branch, sinks, rel_bias, gate_b, w_out):
    bsz, lp = h.shape[:2]
    f32 = jnp.float32
    u = rmsnorm(h, norm_w)
    z, xbc, dt_raw, q, k, v, g_ssd, g_att = split_cols(u @ w_in.astype(u.dtype), IN_WIDTHS)
    xbc = jax.nn.silu(causal_dwconv(xbc, conv_w, conv_b))
    xs, bm, cm = split_cols(xbc, (SSD_INNER, SSD_GROUPS * SSD_STATE, SSD_GROUPS * SSD_STATE))
    dt = jax.nn.softplus(dt_raw.astype(f32) + dt_bias.astype(f32)) * valid_f32[None, :, None]
    a = -jnp.exp(a_log.astype(f32))
    xh = xs.reshape(bsz, lp, SSD_HEADS, SSD_HEADDIM)
    y = ssd_chunked(xh, dt, a, bm.reshape(bsz, lp, SSD_GROUPS, SSD_STATE),
                    cm.reshape(bsz, lp, SSD_GROUPS, SSD_STATE))
    y = y + d_skip.astype(f32)[:, None] * xh.astype(f32)
    y = gated_group_rmsnorm(y.reshape(bsz, lp, SSD_INNER), z, ssd_norm_w).astype(h.dtype)
    y_ssd = y @ w_ssd_branch.astype(h.dtype)
    att = swa_sink_attention(q.reshape(bsz, lp, ATT_HEADS, ATT_HEADDIM),
                             k.reshape(bsz, lp, ATT_KV_HEADS, ATT_HEADDIM),
                             v.reshape(bsz, lp, ATT_KV_HEADS, ATT_HEADDIM), rel_bias, sinks)
    y_att = att @ w_attn_branch.astype(h.dtype)
    gb = gate_b.astype(h.dtype)
    merged = jax.nn.sigmoid(g_ssd + gb[0]) * y_ssd + jax.nn.sigmoid(g_att + gb[1]) * y_att
    return merged @ w_out.astype(h.dtype)


def conv_ffn(h, norm_w, w_in, conv_w, conv_b, w_down):
    u = rmsnorm(h, norm_w)
    hid = causal_dwconv(u @ w_in.astype(u.dtype), conv_w, conv_b)
    up, gate = jnp.split(hid, 2, axis=-1)
    return (jax.nn.silu(gate) * up) @ w_down.astype(h.dtype)


def _fwd_setup_inputs(seed: int = 0) -> dict:
    key = jax.random.key(seed)
    ks = jax.random.split(key, 24)
    f32 = jnp.float32
    nrm = lambda k, shape, s: jax.random.normal(k, shape, f32) * s
    dt0 = jnp.exp(jax.random.uniform(ks[5], (DEPTH, SSD_HEADS), f32, math.log(1e-3), math.log(1e-1)))
    return {
        'x': nrm(ks[0], (BATCH, SEQ, D_MODEL), 1.0),
        'meta_tokens': nrm(ks[1], (N_META, D_MODEL), 1.0),
        'norm_mix_w': 1.0 + nrm(ks[2], (DEPTH, D_MODEL), 0.05),
        'w_in': nrm(ks[3], (DEPTH, D_MODEL, D_IN_PROJ), D_MODEL ** -0.5),
        'ssd_conv_w': nrm(ks[4], (DEPTH, SSD_CONV, SSD_XBC), SSD_CONV ** -0.5),
        'ssd_conv_b': nrm(ks[6], (DEPTH, SSD_XBC), 0.01),
        'ssd_dt_bias': dt0 + jnp.log(-jnp.expm1(-dt0)),
        'ssd_a_log': jnp.log(jax.random.uniform(ks[7], (DEPTH, SSD_HEADS), f32, 1.0, 16.0)),
        'ssd_d': 1.0 + nrm(ks[8], (DEPTH, SSD_HEADS), 0.1),
        'ssd_norm_w': 1.0 + nrm(ks[9], (DEPTH, SSD_INNER), 0.05),
        'w_ssd_branch': nrm(ks[10], (DEPTH, SSD_INNER, D_MODEL), SSD_INNER ** -0.5),
        'w_attn_branch': nrm(ks[11], (DEPTH, ATT_Q, D_MODEL), ATT_Q ** -0.5),
        'attn_sinks': nrm(ks[12], (DEPTH, ATT_HEADS), 0.5),
        'rel_bias': nrm(ks[13], (REL_BUCKETS, ATT_HEADS), 0.5),
        'gate_b': nrm(ks[14], (DEPTH, N_BRANCH, D_MODEL), 0.1),
        'w_out': nrm(ks[15], (DEPTH, D_MODEL, D_MODEL), D_MODEL ** -0.5),
        'norm_ffn_w': 1.0 + nrm(ks[16], (DEPTH, D_MODEL), 0.05),
        'w_ffn_in': nrm(ks[17], (DEPTH, D_MODEL, 2 * D_FF), D_MODEL ** -0.5),
        'ffn_conv_w': nrm(ks[18], (DEPTH, FFN_CONV, 2 * D_FF), FFN_CONV ** -0.5),
        'ffn_conv_b': nrm(ks[19], (DEPTH, 2 * D_FF), 0.01),
        'w_ffn_out': nrm(ks[20], (DEPTH, D_FF, D_MODEL), D_FF ** -0.5),
        'norm_final_w': 1.0 + nrm(ks[21], (D_MODEL,), 0.05),
    }


def _fwd_reference(x, meta_tokens, norm_mix_w, w_in, ssd_conv_w, ssd_conv_b, ssd_dt_bias, ssd_a_log, ssd_d,
              ssd_norm_w, w_ssd_branch, w_attn_branch, attn_sinks, rel_bias, gate_b, w_out, norm_ffn_w,
              w_ffn_in, ffn_conv_w, ffn_conv_b, w_ffn_out, norm_final_w):
    bsz, seq = x.shape[:2]
    lp = seq + BLOCK
    meta = jnp.broadcast_to(meta_tokens.astype(x.dtype)[None], (bsz, N_META, D_MODEL))
    h = jnp.concatenate([jnp.zeros((bsz, PAD, D_MODEL), x.dtype), meta, x], axis=1)
    valid = jnp.arange(lp) >= PAD
    valid_f32 = valid.astype(jnp.float32)
    valid_h = valid.astype(x.dtype)[None, :, None]
    for i in range(DEPTH):
        h = h + valid_h * mixer_block(h, valid_f32, norm_mix_w[i], w_in[i], ssd_conv_w[i], ssd_conv_b[i],
                                      ssd_dt_bias[i], ssd_a_log[i], ssd_d[i], ssd_norm_w[i], w_ssd_branch[i],
                                      w_attn_branch[i], attn_sinks[i], rel_bias, gate_b[i], w_out[i])
        h = h + valid_h * conv_ffn(h, norm_ffn_w[i], w_ffn_in[i], ffn_conv_w[i], ffn_conv_b[i], w_ffn_out[i])
    y = rmsnorm(h, norm_final_w)
    return y[:, BLOCK:]


import jax as _jax
import jax.numpy as _jnp

TWIN_FORMAT = 'train_step'
FWD_PARAMS = ['x', 'meta_tokens', 'norm_mix_w', 'w_in', 'ssd_conv_w', 'ssd_conv_b', 'ssd_dt_bias', 'ssd_a_log', 'ssd_d', 'ssd_norm_w', 'w_ssd_branch', 'w_attn_branch', 'attn_sinks', 'rel_bias', 'gate_b', 'w_out', 'norm_ffn_w', 'w_ffn_in', 'ffn_conv_w', 'ffn_conv_b', 'w_ffn_out', 'norm_final_w']
TWIN_WEIGHTS = ['meta_tokens', 'norm_mix_w', 'w_in', 'ssd_conv_w', 'ssd_conv_b', 'ssd_dt_bias', 'ssd_a_log', 'ssd_d', 'ssd_norm_w', 'w_ssd_branch', 'w_attn_branch', 'attn_sinks', 'rel_bias', 'gate_b', 'w_out', 'norm_ffn_w', 'w_ffn_in', 'ffn_conv_w', 'ffn_conv_b', 'w_ffn_out', 'norm_final_w']
TWIN_DIFF_INPUT = 'x'
TWIN_INPUTS = ['x', 'meta_tokens', 'norm_mix_w', 'w_in', 'ssd_conv_w', 'ssd_conv_b', 'ssd_dt_bias', 'ssd_a_log', 'ssd_d', 'ssd_norm_w', 'w_ssd_branch', 'w_attn_branch', 'attn_sinks', 'rel_bias', 'gate_b', 'w_out', 'norm_ffn_w', 'w_ffn_in', 'ffn_conv_w', 'ffn_conv_b', 'w_ffn_out', 'norm_final_w', 'loss_target', 'm_meta_tokens', 'm_norm_mix_w', 'm_w_in', 'm_ssd_conv_w', 'm_ssd_conv_b', 'm_ssd_dt_bias', 'm_ssd_a_log', 'm_ssd_d', 'm_ssd_norm_w', 'm_w_ssd_branch', 'm_w_attn_branch', 'm_attn_sinks', 'm_rel_bias', 'm_gate_b', 'm_w_out', 'm_norm_ffn_w', 'm_w_ffn_in', 'm_ffn_conv_w', 'm_ffn_conv_b', 'm_w_ffn_out', 'm_norm_final_w', 'v_meta_tokens', 'v_norm_mix_w', 'v_w_in', 'v_ssd_conv_w', 'v_ssd_conv_b', 'v_ssd_dt_bias', 'v_ssd_a_log', 'v_ssd_d', 'v_ssd_norm_w', 'v_w_ssd_branch', 'v_w_attn_branch', 'v_attn_sinks', 'v_rel_bias', 'v_gate_b', 'v_w_out', 'v_norm_ffn_w', 'v_w_ffn_in', 'v_ffn_conv_w', 'v_ffn_conv_b', 'v_w_ffn_out', 'v_norm_final_w']
TWIN_OUTPUTS = ['loss', 'grad_x', 'grad_meta_tokens', 'grad_norm_mix_w', 'grad_w_in', 'grad_ssd_conv_w', 'grad_ssd_conv_b', 'grad_ssd_dt_bias', 'grad_ssd_a_log', 'grad_ssd_d', 'grad_ssd_norm_w', 'grad_w_ssd_branch', 'grad_w_attn_branch', 'grad_attn_sinks', 'grad_rel_bias', 'grad_gate_b', 'grad_w_out', 'grad_norm_ffn_w', 'grad_w_ffn_in', 'grad_ffn_conv_w', 'grad_ffn_conv_b', 'grad_w_ffn_out', 'grad_norm_final_w', 'delta_meta_tokens', 'delta_norm_mix_w', 'delta_w_in', 'delta_ssd_conv_w', 'delta_ssd_conv_b', 'delta_ssd_dt_bias', 'delta_ssd_a_log', 'delta_ssd_d', 'delta_ssd_norm_w', 'delta_w_ssd_branch', 'delta_w_attn_branch', 'delta_attn_sinks', 'delta_rel_bias', 'delta_gate_b', 'delta_w_out', 'delta_norm_ffn_w', 'delta_w_ffn_in', 'delta_ffn_conv_w', 'delta_ffn_conv_b', 'delta_w_ffn_out', 'delta_norm_final_w', 'new_m_meta_tokens', 'new_m_norm_mix_w', 'new_m_w_in', 'new_m_ssd_conv_w', 'new_m_ssd_conv_b', 'new_m_ssd_dt_bias', 'new_m_ssd_a_log', 'new_m_ssd_d', 'new_m_ssd_norm_w', 'new_m_w_ssd_branch', 'new_m_w_attn_branch', 'new_m_attn_sinks', 'new_m_rel_bias', 'new_m_gate_b', 'new_m_w_out', 'new_m_norm_ffn_w', 'new_m_w_ffn_in', 'new_m_ffn_conv_w', 'new_m_ffn_conv_b', 'new_m_w_ffn_out', 'new_m_norm_final_w', 'new_v_meta_tokens', 'new_v_norm_mix_w', 'new_v_w_in', 'new_v_ssd_conv_w', 'new_v_ssd_conv_b', 'new_v_ssd_dt_bias', 'new_v_ssd_a_log', 'new_v_ssd_d', 'new_v_ssd_norm_w', 'new_v_w_ssd_branch', 'new_v_w_attn_branch', 'new_v_attn_sinks', 'new_v_rel_bias', 'new_v_gate_b', 'new_v_w_out', 'new_v_norm_ffn_w', 'new_v_w_ffn_in', 'new_v_ffn_conv_w', 'new_v_ffn_conv_b', 'new_v_w_ffn_out', 'new_v_norm_final_w']
TWIN_LEAF_KINDS = {'loss': 'loss', 'grad_x': 'grad_x', 'grad_meta_tokens': 'grad_w', 'grad_norm_mix_w': 'grad_w', 'grad_w_in': 'grad_w', 'grad_ssd_conv_w': 'grad_w', 'grad_ssd_conv_b': 'grad_w', 'grad_ssd_dt_bias': 'grad_w', 'grad_ssd_a_log': 'grad_w', 'grad_ssd_d': 'grad_w', 'grad_ssd_norm_w': 'grad_w', 'grad_w_ssd_branch': 'grad_w', 'grad_w_attn_branch': 'grad_w', 'grad_attn_sinks': 'grad_w', 'grad_rel_bias': 'grad_w', 'grad_gate_b': 'grad_w', 'grad_w_out': 'grad_w', 'grad_norm_ffn_w': 'grad_w', 'grad_w_ffn_in': 'grad_w', 'grad_ffn_conv_w': 'grad_w', 'grad_ffn_conv_b': 'grad_w', 'grad_w_ffn_out': 'grad_w', 'grad_norm_final_w': 'grad_w', 'delta_meta_tokens': 'delta_w', 'delta_norm_mix_w': 'delta_w', 'delta_w_in': 'delta_w', 'delta_ssd_conv_w': 'delta_w', 'delta_ssd_conv_b': 'delta_w', 'delta_ssd_dt_bias': 'delta_w', 'delta_ssd_a_log': 'delta_w', 'delta_ssd_d': 'delta_w', 'delta_ssd_norm_w': 'delta_w', 'delta_w_ssd_branch': 'delta_w', 'delta_w_attn_branch': 'delta_w', 'delta_attn_sinks': 'delta_w', 'delta_rel_bias': 'delta_w', 'delta_gate_b': 'delta_w', 'delta_w_out': 'delta_w', 'delta_norm_ffn_w': 'delta_w', 'delta_w_ffn_in': 'delta_w', 'delta_ffn_conv_w': 'delta_w', 'delta_ffn_conv_b': 'delta_w', 'delta_w_ffn_out': 'delta_w', 'delta_norm_final_w': 'delta_w', 'new_m_meta_tokens': 'new_m', 'new_m_norm_mix_w': 'new_m', 'new_m_w_in': 'new_m', 'new_m_ssd_conv_w': 'new_m', 'new_m_ssd_conv_b': 'new_m', 'new_m_ssd_dt_bias': 'new_m', 'new_m_ssd_a_log': 'new_m', 'new_m_ssd_d': 'new_m', 'new_m_ssd_norm_w': 'new_m', 'new_m_w_ssd_branch': 'new_m', 'new_m_w_attn_branch': 'new_m', 'new_m_attn_sinks': 'new_m', 'new_m_rel_bias': 'new_m', 'new_m_gate_b': 'new_m', 'new_m_w_out': 'new_m', 'new_m_norm_ffn_w': 'new_m', 'new_m_w_ffn_in': 'new_m', 'new_m_ffn_conv_w': 'new_m', 'new_m_ffn_conv_b': 'new_m', 'new_m_w_ffn_out': 'new_m', 'new_m_norm_final_w': 'new_m', 'new_v_meta_tokens': 'new_v', 'new_v_norm_mix_w': 'new_v', 'new_v_w_in': 'new_v', 'new_v_ssd_conv_w': 'new_v', 'new_v_ssd_conv_b': 'new_v', 'new_v_ssd_dt_bias': 'new_v', 'new_v_ssd_a_log': 'new_v', 'new_v_ssd_d': 'new_v', 'new_v_ssd_norm_w': 'new_v', 'new_v_w_ssd_branch': 'new_v', 'new_v_w_attn_branch': 'new_v', 'new_v_attn_sinks': 'new_v', 'new_v_rel_bias': 'new_v', 'new_v_gate_b': 'new_v', 'new_v_w_out': 'new_v', 'new_v_norm_ffn_w': 'new_v', 'new_v_w_ffn_in': 'new_v', 'new_v_ffn_conv_w': 'new_v', 'new_v_ffn_conv_b': 'new_v', 'new_v_w_ffn_out': 'new_v', 'new_v_norm_final_w': 'new_v'}


def _forward(args):
    return _fwd_reference(*[args[k] for k in FWD_PARAMS])


def _output_shape():
    def fwd():
        inp = _fwd_setup_inputs(0)
        return _fwd_reference(*[inp[k] for k in FWD_PARAMS])
    out = _jax.eval_shape(fwd)
    return out.shape, out.dtype

N_MICROBATCH = 1
ADAM_LR = 0.001
ADAM_B1 = 0.9
ADAM_B2 = 0.999
ADAM_EPS = 1e-08
ADAM_WD = 0.01
ADAM_STEP = 10
PER_EXAMPLE_BATCH_AXIS = {'x': 0, 'loss_target': 0}
SHARED_INPUTS = []
_WEIGHT_DTYPES = {'meta_tokens': _jnp.float32, 'norm_mix_w': _jnp.float32, 'w_in': _jnp.float32, 'ssd_conv_w': _jnp.float32, 'ssd_conv_b': _jnp.float32, 'ssd_dt_bias': _jnp.float32, 'ssd_a_log': _jnp.float32, 'ssd_d': _jnp.float32, 'ssd_norm_w': _jnp.float32, 'w_ssd_branch': _jnp.float32, 'w_attn_branch': _jnp.float32, 'attn_sinks': _jnp.float32, 'rel_bias': _jnp.float32, 'gate_b': _jnp.float32, 'w_out': _jnp.float32, 'norm_ffn_w': _jnp.float32, 'w_ffn_in': _jnp.float32, 'ffn_conv_w': _jnp.float32, 'ffn_conv_b': _jnp.float32, 'w_ffn_out': _jnp.float32, 'norm_final_w': _jnp.float32}
MOMENT_SCALE = {'meta_tokens': 6.449281e-03, 'norm_mix_w': 2.134627e-01, 'w_in': 7.388967e-02, 'ssd_conv_w': 8.266863e-02, 'ssd_conv_b': 1.207329e-01, 'ssd_dt_bias': 2.824252e-01, 'ssd_a_log': 3.734278e-01, 'ssd_d': 5.930146e-01, 'ssd_norm_w': 9.707076e-02, 'w_ssd_branch': 1.381582e-01, 'w_attn_branch': 2.493335e-02, 'attn_sinks': 2.969657e-03, 'rel_bias': 2.870703e-02, 'gate_b': 3.896137e-02, 'w_out': 1.388249e-01, 'norm_ffn_w': 1.727848e-01, 'w_ffn_in': 7.242574e-02, 'ffn_conv_w': 7.472036e-02, 'ffn_conv_b': 8.088209e-02, 'w_ffn_out': 1.212038e-01, 'norm_final_w': 6.418765e+01}


def _to_microbatches(a, axis):
    t = _jnp.moveaxis(a, axis, 0)
    t = t.reshape((N_MICROBATCH, t.shape[0] // N_MICROBATCH) + t.shape[1:])
    return _jnp.moveaxis(t, 1, axis + 1)


def setup_inputs(seed: int = 0) -> dict:
    inp = _fwd_setup_inputs(seed)
    key = _jax.random.fold_in(_jax.random.key(seed), 7919)
    shape, _ = _output_shape()
    out = dict(inp)
    out["loss_target"] = _jax.random.normal(_jax.random.fold_in(key, 0), shape, _jnp.float32)
    for i, name in enumerate(TWIN_WEIGHTS):
        w = inp[name].astype(_jnp.float32)
        if MOMENT_SCALE is None:
            s = _jnp.sqrt(_jnp.mean(_jnp.square(w)) + 1e-30)
        else:
            s = MOMENT_SCALE[name]
        km, kv = _jax.random.split(_jax.random.fold_in(key, i + 1))
        out[name] = w
        out["m_" + name] = s * _jax.random.normal(km, w.shape, _jnp.float32)
        out["v_" + name] = (s * s) * _jax.random.uniform(kv, w.shape, _jnp.float32, 0.5, 1.5)
    if N_MICROBATCH > 1:
        for name, axis in PER_EXAMPLE_BATCH_AXIS.items():
            out[name] = _to_microbatches(out[name], axis)
    return {'x': out['x'], 'meta_tokens': out['meta_tokens'], 'norm_mix_w': out['norm_mix_w'], 'w_in': out['w_in'], 'ssd_conv_w': out['ssd_conv_w'], 'ssd_conv_b': out['ssd_conv_b'], 'ssd_dt_bias': out['ssd_dt_bias'], 'ssd_a_log': out['ssd_a_log'], 'ssd_d': out['ssd_d'], 'ssd_norm_w': out['ssd_norm_w'], 'w_ssd_branch': out['w_ssd_branch'], 'w_attn_branch': out['w_attn_branch'], 'attn_sinks': out['attn_sinks'], 'rel_bias': out['rel_bias'], 'gate_b': out['gate_b'], 'w_out': out['w_out'], 'norm_ffn_w': out['norm_ffn_w'], 'w_ffn_in': out['w_ffn_in'], 'ffn_conv_w': out['ffn_conv_w'], 'ffn_conv_b': out['ffn_conv_b'], 'w_ffn_out': out['w_ffn_out'], 'norm_final_w': out['norm_final_w'], 'loss_target': out['loss_target'], 'm_meta_tokens': out['m_meta_tokens'], 'm_norm_mix_w': out['m_norm_mix_w'], 'm_w_in': out['m_w_in'], 'm_ssd_conv_w': out['m_ssd_conv_w'], 'm_ssd_conv_b': out['m_ssd_conv_b'], 'm_ssd_dt_bias': out['m_ssd_dt_bias'], 'm_ssd_a_log': out['m_ssd_a_log'], 'm_ssd_d': out['m_ssd_d'], 'm_ssd_norm_w': out['m_ssd_norm_w'], 'm_w_ssd_branch': out['m_w_ssd_branch'], 'm_w_attn_branch': out['m_w_attn_branch'], 'm_attn_sinks': out['m_attn_sinks'], 'm_rel_bias': out['m_rel_bias'], 'm_gate_b': out['m_gate_b'], 'm_w_out': out['m_w_out'], 'm_norm_ffn_w': out['m_norm_ffn_w'], 'm_w_ffn_in': out['m_w_ffn_in'], 'm_ffn_conv_w': out['m_ffn_conv_w'], 'm_ffn_conv_b': out['m_ffn_conv_b'], 'm_w_ffn_out': out['m_w_ffn_out'], 'm_norm_final_w': out['m_norm_final_w'], 'v_meta_tokens': out['v_meta_tokens'], 'v_norm_mix_w': out['v_norm_mix_w'], 'v_w_in': out['v_w_in'], 'v_ssd_conv_w': out['v_ssd_conv_w'], 'v_ssd_conv_b': out['v_ssd_conv_b'], 'v_ssd_dt_bias': out['v_ssd_dt_bias'], 'v_ssd_a_log': out['v_ssd_a_log'], 'v_ssd_d': out['v_ssd_d'], 'v_ssd_norm_w': out['v_ssd_norm_w'], 'v_w_ssd_branch': out['v_w_ssd_branch'], 'v_w_attn_branch': out['v_w_attn_branch'], 'v_attn_sinks': out['v_attn_sinks'], 'v_rel_bias': out['v_rel_bias'], 'v_gate_b': out['v_gate_b'], 'v_w_out': out['v_w_out'], 'v_norm_ffn_w': out['v_norm_ffn_w'], 'v_w_ffn_in': out['v_w_ffn_in'], 'v_ffn_conv_w': out['v_ffn_conv_w'], 'v_ffn_conv_b': out['v_ffn_conv_b'], 'v_w_ffn_out': out['v_w_ffn_out'], 'v_norm_final_w': out['v_norm_final_w']}


def _loss(weights, diff, rest, loss_target):
    with _jax.named_scope("forward"):
        args = {**rest, TWIN_DIFF_INPUT: diff, **{k: w.astype(_WEIGHT_DTYPES[k]) for k, w in weights.items()}}
        y = _forward(args)
    with _jax.named_scope("loss_head"):
        err = _jnp.square(y.astype(_jnp.float32) - loss_target)
        return 0.5 * _jnp.sum(_jnp.mean(err, axis=-1)) if err.ndim else 0.5 * err


def _adamw(w, g, m, v):
    m = ADAM_B1 * m + (1.0 - ADAM_B1) * g
    v = ADAM_B2 * v + (1.0 - ADAM_B2) * _jnp.square(g)
    m_hat = m / (1.0 - ADAM_B1 ** ADAM_STEP)
    v_hat = v / (1.0 - ADAM_B2 ** ADAM_STEP)
    delta = -ADAM_LR * (m_hat / (_jnp.sqrt(v_hat) + ADAM_EPS) + ADAM_WD * w)
    return delta, m, v


def reference(x, meta_tokens, norm_mix_w, w_in, ssd_conv_w, ssd_conv_b, ssd_dt_bias, ssd_a_log, ssd_d, ssd_norm_w, w_ssd_branch, w_attn_branch, attn_sinks, rel_bias, gate_b, w_out, norm_ffn_w, w_ffn_in, ffn_conv_w, ffn_conv_b, w_ffn_out, norm_final_w, loss_target, m_meta_tokens, m_norm_mix_w, m_w_in, m_ssd_conv_w, m_ssd_conv_b, m_ssd_dt_bias, m_ssd_a_log, m_ssd_d, m_ssd_norm_w, m_w_ssd_branch, m_w_attn_branch, m_attn_sinks, m_rel_bias, m_gate_b, m_w_out, m_norm_ffn_w, m_w_ffn_in, m_ffn_conv_w, m_ffn_conv_b, m_w_ffn_out, m_norm_final_w, v_meta_tokens, v_norm_mix_w, v_w_in, v_ssd_conv_w, v_ssd_conv_b, v_ssd_dt_bias, v_ssd_a_log, v_ssd_d, v_ssd_norm_w, v_w_ssd_branch, v_w_attn_branch, v_attn_sinks, v_rel_bias, v_gate_b, v_w_out, v_norm_ffn_w, v_w_ffn_in, v_ffn_conv_w, v_ffn_conv_b, v_w_ffn_out, v_norm_final_w):
    given = dict(x=x, meta_tokens=meta_tokens, norm_mix_w=norm_mix_w, w_in=w_in, ssd_conv_w=ssd_conv_w, ssd_conv_b=ssd_conv_b, ssd_dt_bias=ssd_dt_bias, ssd_a_log=ssd_a_log, ssd_d=ssd_d, ssd_norm_w=ssd_norm_w, w_ssd_branch=w_ssd_branch, w_attn_branch=w_attn_branch, attn_sinks=attn_sinks, rel_bias=rel_bias, gate_b=gate_b, w_out=w_out, norm_ffn_w=norm_ffn_w, w_ffn_in=w_ffn_in, ffn_conv_w=ffn_conv_w, ffn_conv_b=ffn_conv_b, w_ffn_out=w_ffn_out, norm_final_w=norm_final_w, loss_target=loss_target, m_meta_tokens=m_meta_tokens, m_norm_mix_w=m_norm_mix_w, m_w_in=m_w_in, m_ssd_conv_w=m_ssd_conv_w, m_ssd_conv_b=m_ssd_conv_b, m_ssd_dt_bias=m_ssd_dt_bias, m_ssd_a_log=m_ssd_a_log, m_ssd_d=m_ssd_d, m_ssd_norm_w=m_ssd_norm_w, m_w_ssd_branch=m_w_ssd_branch, m_w_attn_branch=m_w_attn_branch, m_attn_sinks=m_attn_sinks, m_rel_bias=m_rel_bias, m_gate_b=m_gate_b, m_w_out=m_w_out, m_norm_ffn_w=m_norm_ffn_w, m_w_ffn_in=m_w_ffn_in, m_ffn_conv_w=m_ffn_conv_w, m_ffn_conv_b=m_ffn_conv_b, m_w_ffn_out=m_w_ffn_out, m_norm_final_w=m_norm_final_w, v_meta_tokens=v_meta_tokens, v_norm_mix_w=v_norm_mix_w, v_w_in=v_w_in, v_ssd_conv_w=v_ssd_conv_w, v_ssd_conv_b=v_ssd_conv_b, v_ssd_dt_bias=v_ssd_dt_bias, v_ssd_a_log=v_ssd_a_log, v_ssd_d=v_ssd_d, v_ssd_norm_w=v_ssd_norm_w, v_w_ssd_branch=v_w_ssd_branch, v_w_attn_branch=v_w_attn_branch, v_attn_sinks=v_attn_sinks, v_rel_bias=v_rel_bias, v_gate_b=v_gate_b, v_w_out=v_w_out, v_norm_ffn_w=v_norm_ffn_w, v_w_ffn_in=v_w_ffn_in, v_ffn_conv_w=v_ffn_conv_w, v_ffn_conv_b=v_ffn_conv_b, v_w_ffn_out=v_w_ffn_out, v_norm_final_w=v_norm_final_w)
    weights = {n: given[n] for n in TWIN_WEIGHTS}
    shared = {n: given[n] for n in SHARED_INPUTS}
    per_example = {n: given[n] for n in ['x']}
    grad_fn = _jax.value_and_grad(_loss, argnums=(0, 1))

    def one_microbatch(ex, loss_target):
        ex = dict(ex)
        diff = ex.pop(TWIN_DIFF_INPUT)
        return grad_fn(weights, diff, {**shared, **ex}, loss_target)

    if N_MICROBATCH == 1:
        loss, (grad_w, grad_x) = one_microbatch(per_example, given["loss_target"])
    else:
        def body(carry, xs):
            loss_sum, grad_sum = carry
            l_k, (gw_k, gx_k) = one_microbatch(xs[0], xs[1])
            with _jax.named_scope("update"):
                return (loss_sum + l_k, _jax.tree.map(_jnp.add, grad_sum, gw_k)), gx_k

        init = (_jnp.zeros((), _jnp.float32), _jax.tree.map(_jnp.zeros_like, weights))
        (loss, grad_w), grad_x = _jax.lax.scan(body, init, (per_example, given["loss_target"]))
    with _jax.named_scope("update"):
        delta_w, new_m, new_v = {}, {}, {}
        for n in TWIN_WEIGHTS:
            delta_w[n], new_m[n], new_v[n] = _adamw(weights[n], grad_w[n], given["m_" + n], given["v_" + n])
    return (loss, grad_x, *[grad_w[n] for n in TWIN_WEIGHTS], *[delta_w[n] for n in TWIN_WEIGHTS],
            *[new_m[n] for n in TWIN_WEIGHTS], *[new_v[n] for n in TWIN_WEIGHTS])
```

```python
import functools
import math

import numpy as np
import jax
import jax.numpy as jnp
from jax import lax
from jax.experimental import pallas as pl
from jax.experimental.pallas import tpu as pltpu

F32 = jnp.float32
BF16 = jnp.bfloat16
HIGHEST = lax.Precision.HIGHEST

D_MODEL = 1024
N_META = 16
BLOCK = 128
PAD = BLOCK - N_META
EPS = 1e-6
NEG = -1e30
SSD_INNER = 2 * D_MODEL
SSD_HEADDIM = 64
SSD_HEADS = SSD_INNER // SSD_HEADDIM
SSD_GROUPS = 4
SSD_HPG = SSD_HEADS // SSD_GROUPS
SSD_STATE = 128
SSD_CONV = 4
SSD_GW = SSD_HPG * SSD_HEADDIM
SSD_BC = SSD_GROUPS * SSD_STATE
SSD_XBC = SSD_INNER + 2 * SSD_BC
ATT_HEADS = 16
ATT_KV_HEADS = 2
ATT_HEADDIM = 64
ATT_GQ = ATT_HEADS // ATT_KV_HEADS
ATT_Q = ATT_HEADS * ATT_HEADDIM
ATT_KV = ATT_KV_HEADS * ATT_HEADDIM
REL_BUCKETS = 32
REL_MAX_DIST = 128
D_FF = 2816
FFN_CONV = 3
ADAM_LR = 0.001
ADAM_B1 = 0.9
ADAM_B2 = 0.999
ADAM_EPS = 1e-08
ADAM_WD = 0.01
ADAM_STEP = 10

N_DEV = 8
LANES = 128
SUBLANES = 8
DT_W = SSD_GROUPS * LANES
VMEM_LIMIT_BYTES = 56 * 1024 * 1024
MESH = pl.DeviceIdType.MESH

BIG_ROW_MULT = 128
SMALL_ROW_MULT = 16

NT_BAND = BLOCK * 2 * BLOCK
NT_META = BLOCK * BLOCK
NT_ALL = NT_BAND + 3 * NT_META
NT_TILE = 8192


def _cparams(*sem):
    return pltpu.CompilerParams(dimension_semantics=sem, vmem_limit_bytes=VMEM_LIMIT_BYTES)


def _row_tile(n, cap):
    best = None
    for t in range(16, min(n, cap) + 1, 16):
        if n % t == 0:
            best = t
    return best or n


def _col_tile(n, cap):
    for t in (1408, 1280, 1024, 768, 640, 512, 384, 256, 128):
        if t <= cap and n % t == 0:
            return t
    return n


def _silu(x):
    return x * jax.nn.sigmoid(x)


def _dsilu(x):
    s = jax.nn.sigmoid(x)
    return s * (1.0 + x * (1.0 - s))


def _softplus(x):
    return jnp.maximum(x, 0.0) + jnp.log(1.0 + jnp.exp(-jnp.abs(x)))


def _dot_nt(a, b):
    return lax.dot_general(a, b, (((1,), (1,)), ((), ())), preferred_element_type=F32)


def _dot_tn(a, b):
    return lax.dot_general(a, b, (((0,), (0,)), ((), ())), preferred_element_type=F32)


def _dot(a, b):
    return jnp.dot(a, b, preferred_element_type=F32)


def _sum_all(x):
    return jnp.sum(jnp.sum(x, axis=1, keepdims=True), axis=0, keepdims=True)


def _mm(a, b, *, name, ta=False, c=None, mask=False, out_dtype=F32):
    if not ta:
        m, k = a.shape
        n = b.shape[1]
        tm = _row_tile(m, 832)
        tn = _col_tile(n, 512 if k > 3072 else 1024)

        def body(*refs):
            if c is None:
                a_ref, b_ref, o_ref = refs
            else:
                a_ref, b_ref, c_ref, o_ref = refs
            acc = _dot(a_ref[...].astype(BF16), b_ref[...].astype(BF16))
            if mask:
                row = pl.program_id(0) * tm + lax.broadcasted_iota(jnp.int32, (tm, 1), 0)
                acc = jnp.where(row >= PAD, acc, 0.0)
            if c is not None:
                acc = acc + c_ref[...]
            o_ref[...] = acc.astype(out_dtype)

        in_specs = [pl.BlockSpec((tm, k), lambda i, j: (i, 0)), pl.BlockSpec((k, tn), lambda i, j: (0, j))]
        args = [a, b]
        if c is not None:
            in_specs.append(pl.BlockSpec((tm, tn), lambda i, j: (i, j)))
            args.append(c)
        return pl.pallas_call(
            body, name=name, grid=(m // tm, n // tn), in_specs=in_specs,
            out_specs=pl.BlockSpec((tm, tn), lambda i, j: (i, j)),
            out_shape=jax.ShapeDtypeStruct((m, n), out_dtype),
            compiler_params=_cparams("parallel", "parallel"))(*args)

    kc, m = a.shape
    n = b.shape[1]
    tk = _row_tile(kc, 832)
    tm = _col_tile(m, 1408)
    tn = _col_tile(n, 1408)

    def body_t(a_ref, b_ref, o_ref):
        kk = pl.program_id(2)
        bb = b_ref[...]
        if mask:
            row = kk * tk + lax.broadcasted_iota(jnp.int32, (tk, 1), 0)
            bb = jnp.where(row >= PAD, bb, jnp.zeros_like(bb))
        p = _dot_tn(a_ref[...].astype(BF16), bb.astype(BF16))

        @pl.when(kk == 0)
        def _():
            o_ref[...] = p

        @pl.when(kk > 0)
        def _():
            o_ref[...] += p

    return pl.pallas_call(
        body_t, name=name, grid=(m // tm, n // tn, kc // tk),
        in_specs=[pl.BlockSpec((tk, tm), lambda i, j, kk: (kk, i)), pl.BlockSpec((tk, tn), lambda i, j, kk: (kk, j))],
        out_specs=pl.BlockSpec((tm, tn), lambda i, j, kk: (i, j)),
        out_shape=jax.ShapeDtypeStruct((m, n), F32),
        compiler_params=_cparams("parallel", "parallel", "arbitrary"))(a, b)


def _rms_fwd(h, w, *, name):
    n, d = h.shape
    tm = _row_tile(n, 832)

    def body(h_ref, w_ref, o_ref):
        x = h_ref[...]
        r = lax.rsqrt(jnp.mean(x * x, axis=-1, keepdims=True) + EPS)
        o_ref[...] = (x * r * w_ref[...]).astype(BF16)

    return pl.pallas_call(
        body, name=name, grid=(n // tm,),
        in_specs=[pl.BlockSpec((tm, d), lambda i: (i, 0)), pl.BlockSpec((1, d), lambda i: (0, 0))],
        out_specs=pl.BlockSpec((tm, d), lambda i: (i, 0)),
        out_shape=jax.ShapeDtypeStruct((n, d), BF16),
        compiler_params=_cparams("parallel"))(h, w)


def _rms_bwd(x, w, dy, dres, *, name):
    n, d = x.shape
    tm = _row_tile(n, 832)

    def body(x_ref, w_ref, dy_ref, dres_ref, dx_ref, dw_ref):
        i = pl.program_id(0)
        xv = x_ref[...]
        r = lax.rsqrt(jnp.mean(xv * xv, axis=-1, keepdims=True) + EPS)
        xh = xv * r
        dyv = dy_ref[...]
        g = dyv * w_ref[...]
        dx_ref[...] = r * (g - xh * jnp.mean(g * xh, axis=-1, keepdims=True)) + dres_ref[...]
        part = jnp.sum(dyv * xh, axis=0, keepdims=True)

        @pl.when(i == 0)
        def _():
            dw_ref[...] = part

        @pl.when(i > 0)
        def _():
            dw_ref[...] += part

    row = pl.BlockSpec((tm, d), lambda i: (i, 0))
    vec = pl.BlockSpec((1, d), lambda i: (0, 0))
    return pl.pallas_call(
        body, name=name, grid=(n // tm,), in_specs=[row, vec, row, row], out_specs=[row, vec],
        out_shape=[jax.ShapeDtypeStruct((n, d), F32), jax.ShapeDtypeStruct((1, d), F32)],
        compiler_params=_cparams("arbitrary"))(x, w, dy, dres)


def _final_loss(h, w, target):
    n, d = h.shape
    nb = n // BLOCK

    def body(h_ref, w_ref, t_ref, dh_ref, loss_ref, dw_ref):
        i = pl.program_id(0)
        xv = h_ref[...]
        r = lax.rsqrt(jnp.mean(xv * xv, axis=-1, keepdims=True) + EPS)
        xh = xv * r
        wv = w_ref[...]
        err = jnp.where(i >= 1, xh * wv - t_ref[...], 0.0)
        dyv = err * (1.0 / d)
        g = dyv * wv
        dh_ref[...] = r * (g - xh * jnp.mean(g * xh, axis=-1, keepdims=True))
        lpart = jnp.broadcast_to(0.5 * _sum_all(err * err) * (1.0 / d), (1, LANES))
        wpart = jnp.sum(dyv * xh, axis=0, keepdims=True)

        @pl.when(i == 0)
        def _():
            loss_ref[...] = lpart
            dw_ref[...] = wpart

        @pl.when(i > 0)
        def _():
            loss_ref[...] += lpart
            dw_ref[...] += wpart

    row = pl.BlockSpec((BLOCK, d), lambda i: (i, 0))
    vec = pl.BlockSpec((1, d), lambda i: (0, 0))
    return pl.pallas_call(
        body, name="final_loss", grid=(nb,),
        in_specs=[row, vec, pl.BlockSpec((BLOCK, d), lambda i: (jnp.maximum(i - 1, 0), 0))],
        out_specs=[row, pl.BlockSpec((1, LANES), lambda i: (0, 0)), vec],
        out_shape=[jax.ShapeDtypeStruct((n, d), F32), jax.ShapeDtypeStruct((1, LANES), F32),
                   jax.ShapeDtypeStruct((1, d), F32)],
        compiler_params=_cparams("arbitrary"))(h, w, target)


def _main_spec(tm, cb, off=0):
    return pl.BlockSpec((tm, cb), lambda j, i: (i, j + off))


def _prev_spec(tm, cb, off=0):
    r8 = tm // SUBLANES
    return pl.BlockSpec((SUBLANES, cb), lambda j, i: (jnp.maximum(i * r8 - 1, 0), j + off))


def _next_spec(tm, cb, n_rows, off=0):
    r8 = tm // SUBLANES
    last = n_rows // SUBLANES - 1
    return pl.BlockSpec((SUBLANES, cb), lambda j, i: (jnp.minimum((i + 1) * r8, last), j + off))


def _with_prev(prev_ref, main_ref, i):
    prev = jnp.where(i > 0, prev_ref[...], 0.0)
    return jnp.concatenate([prev, main_ref[...]], axis=0)


def _with_next(main, nxt, i, n_tiles):
    return jnp.concatenate([main, jnp.where(i < n_tiles - 1, nxt, 0.0)], axis=0)


def _back(xx, s, tm):
    if s == 0:
        return xx[SUBLANES:SUBLANES + tm]
    return pltpu.roll(xx, s, 0)[SUBLANES:SUBLANES + tm]


def _ahead(xx, s, tm):
    if s == 0:
        return xx[:tm]
    return pltpu.roll(xx, tm + SUBLANES - s, 0)[:tm]


def _conv_fwd(x, w, b, *, name):
    n, cdim = x.shape
    kw = w.shape[0]
    tm = _row_tile(n, 832)
    cb = _col_tile(cdim, 512)

    def body(xp_ref, x_ref, w_ref, b_ref, o_ref):
        xx = _with_prev(xp_ref, x_ref, pl.program_id(1))
        acc = jnp.broadcast_to(b_ref[...], (tm, cb))
        for k in range(kw):
            acc = acc + w_ref[k:k + 1, :] * _back(xx, kw - 1 - k, tm)
        o_ref[...] = acc

    return pl.pallas_call(
        body, name=name, grid=(cdim // cb, n // tm),
        in_specs=[_prev_spec(tm, cb), _main_spec(tm, cb), pl.BlockSpec((kw, cb), lambda j, i: (0, j)),
                  pl.BlockSpec((1, cb), lambda j, i: (0, j))],
        out_specs=_main_spec(tm, cb),
        out_shape=jax.ShapeDtypeStruct((n, cdim), F32),
        compiler_params=_cparams("parallel", "parallel"))(x, x, w, b)


def _conv_bwd_core(dpre_ext, x_ext, w_ref, kw, tm):
    dpre = dpre_ext[:tm]
    dx = None
    dws = []
    for k in range(kw):
        term = w_ref[k:k + 1, :] * _ahead(dpre_ext, kw - 1 - k, tm)
        dx = term if dx is None else dx + term
        dws.append(jnp.sum(dpre * _back(x_ext, kw - 1 - k, tm), axis=0, keepdims=True))
    return dx, dws, jnp.sum(dpre, axis=0, keepdims=True)


def _acc_rows(i, dw_ref, db_ref, dws, db):
    @pl.when(i == 0)
    def _():
        for k, v in enumerate(dws):
            dw_ref[k:k + 1, :] = v
        db_ref[...] = db

    @pl.when(i > 0)
    def _():
        for k, v in enumerate(dws):
            dw_ref[k:k + 1, :] += v
        db_ref[...] += db


def _conv_bwd(dpre, x, w, *, name):
    n, cdim = x.shape
    kw = w.shape[0]
    tm = _row_tile(n, 832)
    cb = _col_tile(cdim, 512)
    nt = n // tm

    def body(d_ref, dn_ref, xp_ref, x_ref, w_ref, dx_ref, dw_ref, db_ref):
        i = pl.program_id(1)
        dpre_ext = _with_next(d_ref[...], dn_ref[...], i, nt)
        x_ext = _with_prev(xp_ref, x_ref, i)
        dx, dws, db = _conv_bwd_core(dpre_ext, x_ext, w_ref, kw, tm)
        dx_ref[...] = dx.astype(BF16)
        _acc_rows(i, dw_ref, db_ref, dws, db)

    wspec = pl.BlockSpec((kw, cb), lambda j, i: (0, j))
    bspec = pl.BlockSpec((1, cb), lambda j, i: (0, j))
    return pl.pallas_call(
        body, name=name, grid=(cdim // cb, nt),
        in_specs=[_main_spec(tm, cb), _next_spec(tm, cb, n), _prev_spec(tm, cb), _main_spec(tm, cb), wspec],
        out_specs=[_main_spec(tm, cb), wspec, bspec],
        out_shape=[jax.ShapeDtypeStruct((n, cdim), BF16), jax.ShapeDtypeStruct((kw, cdim), F32),
                   jax.ShapeDtypeStruct((1, cdim), F32)],
        compiler_params=_cparams("parallel", "arbitrary"))(dpre, dpre, x, x, w)


def _ffn_act_fwd(x, w, b):
    n = x.shape[0]
    kw = w.shape[0]
    tm = _row_tile(n, 832)
    cb = _col_tile(D_FF, 256)
    nc = D_FF // cb

    def body(xpu_ref, xu_ref, xpg_ref, xg_ref, wu_ref, wg_ref, bu_ref, bg_ref, hu_ref, hg_ref, act_ref):
        i = pl.program_id(1)
        outs = []
        for xp_ref, x_ref, w_ref, b_ref in ((xpu_ref, xu_ref, wu_ref, bu_ref), (xpg_ref, xg_ref, wg_ref, bg_ref)):
            xx = _with_prev(xp_ref, x_ref, i)
            acc = jnp.broadcast_to(b_ref[...], (tm, cb))
            for k in range(kw):
                acc = acc + w_ref[k:k + 1, :] * _back(xx, kw - 1 - k, tm)
            outs.append(acc)
        hu_ref[...] = outs[0]
        hg_ref[...] = outs[1]
        act_ref[...] = (_silu(outs[1]) * outs[0]).astype(BF16)

    def wspec(off):
        return pl.BlockSpec((kw, cb), lambda j, i: (0, j + off))

    def bspec(off):
        return pl.BlockSpec((1, cb), lambda j, i: (0, j + off))

    out = _main_spec(tm, cb)
    return pl.pallas_call(
        body, name="ffn_act_fwd", grid=(nc, n // tm),
        in_specs=[_prev_spec(tm, cb), _main_spec(tm, cb), _prev_spec(tm, cb, nc), _main_spec(tm, cb, nc),
                  wspec(0), wspec(nc), bspec(0), bspec(nc)],
        out_specs=[out, out, out],
        out_shape=[jax.ShapeDtypeStruct((n, D_FF), F32), jax.ShapeDtypeStruct((n, D_FF), F32),
                   jax.ShapeDtypeStruct((n, D_FF), BF16)],
        compiler_params=_cparams("parallel", "parallel"))(x, x, x, x, w, w, b, b)


def _ffn_act_bwd(dact, hu, hg, x, w):
    n = x.shape[0]
    kw = w.shape[0]
    tm = _row_tile(n, 832)
    cb = _col_tile(D_FF, 256)
    nc = D_FF // cb
    nt = n // tm

    def body(d_ref, dn_ref, hu_ref, hun_ref, hg_ref, hgn_ref, xpu_ref, xu_ref, xpg_ref, xg_ref, wu_ref, wg_ref,
             dxu_ref, dxg_ref, dwu_ref, dwg_ref, dbu_ref, dbg_ref):
        i = pl.program_id(1)
        dact_e = _with_next(d_ref[...], dn_ref[...], i, nt)
        up_e = _with_next(hu_ref[...], hun_ref[...], i, nt)
        gate_e = _with_next(hg_ref[...], hgn_ref[...], i, nt)
        dup_e = dact_e * _silu(gate_e)
        dgate_e = dact_e * up_e * _dsilu(gate_e)
        dx, dws, db = _conv_bwd_core(dup_e, _with_prev(xpu_ref, xu_ref, i), wu_ref, kw, tm)
        dxu_ref[...] = dx.astype(BF16)
        _acc_rows(i, dwu_ref, dbu_ref, dws, db)
        dx, dws, db = _conv_bwd_core(dgate_e, _with_prev(xpg_ref, xg_ref, i), wg_ref, kw, tm)
        dxg_ref[...] = dx.astype(BF16)
        _acc_rows(i, dwg_ref, dbg_ref, dws, db)

    main, nxt = _main_spec(tm, cb), _next_spec(tm, cb, n)
    wspec0 = pl.BlockSpec((kw, cb), lambda j, i: (0, j))
    wspec1 = pl.BlockSpec((kw, cb), lambda j, i: (0, j + nc))
    bspec = pl.BlockSpec((1, cb), lambda j, i: (0, j))
    return pl.pallas_call(
        body, name="ffn_act_bwd", grid=(nc, nt),
        in_specs=[main, nxt, main, nxt, main, nxt,
                  _prev_spec(tm, cb), _main_spec(tm, cb), _prev_spec(tm, cb, nc), _main_spec(tm, cb, nc),
                  wspec0, wspec1],
        out_specs=[main, main, wspec0, wspec0, bspec, bspec],
        out_shape=[jax.ShapeDtypeStruct((n, D_FF), BF16), jax.ShapeDtypeStruct((n, D_FF), BF16),
                   jax.ShapeDtypeStruct((kw, D_FF), F32), jax.ShapeDtypeStruct((kw, D_FF), F32),
                   jax.ShapeDtypeStruct((1, D_FF), F32), jax.ShapeDtypeStruct((1, D_FF), F32)],
        compiler_params=_cparams("parallel", "arbitrary"))(dact, dact, hu, hu, hg, hg, x, x, x, x, w, w)


def _ssd_prep(pxs_ref, pb_ref, pc_ref, dtr_ref, dtb_ref, alog_ref, c):
    xs = _silu(pxs_ref[...])
    bm = _silu(pb_ref[...])
    cm = _silu(pc_ref[...])
    row = lax.broadcasted_iota(jnp.int32, (BLOCK, 1), 0) + c * BLOCK
    valid = (row >= PAD).astype(F32)
    dtr = dtr_ref[...] + dtb_ref[...]
    dt = _softplus(dtr) * valid
    a = -jnp.exp(alog_ref[...])
    lam = dt * a
    ri = lax.broadcasted_iota(jnp.int32, (BLOCK, BLOCK), 0)
    ci = lax.broadcasted_iota(jnp.int32, (BLOCK, BLOCK), 1)
    causal = ci <= ri
    cs = jnp.dot(causal.astype(F32), lam, precision=HIGHEST, preferred_element_type=F32)
    return xs, bm, cm, valid, dtr, dt, a, lam, cs, causal


def _head_cols(r):
    return slice(SSD_HEADDIM * r, SSD_HEADDIM * (r + 1))


def _ssd_specs(nc, rev):
    def cidx(c):
        return nc - 1 - c if rev else c

    xs = pl.BlockSpec((BLOCK, SSD_GW), lambda g, c: (cidx(c), g))
    bspec = pl.BlockSpec((BLOCK, SSD_STATE), lambda g, c: (cidx(c), SSD_INNER // SSD_STATE + g))
    cspec = pl.BlockSpec((BLOCK, SSD_STATE), lambda g, c: (cidx(c), (SSD_INNER + SSD_BC) // SSD_STATE + g))
    lane = pl.BlockSpec((BLOCK, LANES), lambda g, c: (cidx(c), g))
    vec = pl.BlockSpec((1, LANES), lambda g, c: (0, g))
    wide_vec = pl.BlockSpec((1, SSD_GW), lambda g, c: (0, g))
    hsave = pl.BlockSpec((1, 1, SSD_GW, SSD_STATE), lambda g, c: (cidx(c), g, 0, 0))
    return xs, bspec, cspec, lane, vec, wide_vec, hsave


def _ssd_fwd(pre, dt_raw, z, dtb, alog, dskip, norm_w):
    n = pre.shape[0]
    nc = n // BLOCK
    xs_s, b_s, c_s, lane_s, vec_s, wide_s, hs_s = _ssd_specs(nc, False)

    def body(pxs_ref, pb_ref, pc_ref, dtr_ref, z_ref, dtb_ref, alog_ref, dsk_ref, nw_ref,
             y_ref, yn_ref, hs_ref, h_scr):
        c = pl.program_id(1)

        @pl.when(c == 0)
        def _():
            h_scr[...] = jnp.zeros_like(h_scr)

        xs, bm, cm, _, _, dt, _, _, cs, causal = _ssd_prep(pxs_ref, pb_ref, pc_ref, dtr_ref, dtb_ref, alog_ref, c)
        cst = cs.T
        cs_last = cs[BLOCK - 1:BLOCK, :]
        bmb = bm.astype(BF16)
        cmb = cm.astype(BF16)
        cb = _dot_nt(cmb, bmb)
        hg = h_scr[...]
        hs_ref[0, 0] = hg
        yoff = _dot_nt(cmb, hg.astype(BF16))
        ys, xds = [], []
        for r in range(SSD_HPG):
            csc = cs[:, r:r + 1]
            lm = jnp.exp(jnp.where(causal, csc - cst[r:r + 1, :], NEG))
            x_r = xs[:, _head_cols(r)]
            xdt = x_r * dt[:, r:r + 1]
            ys.append(_dot((cb * lm).astype(BF16), xdt.astype(BF16)) + yoff[:, _head_cols(r)] * jnp.exp(csc)
                      + dsk_ref[0:1, r:r + 1] * x_r)
            xds.append(xdt * jnp.exp(cs_last[:, r:r + 1] - csc))
        y = jnp.concatenate(ys, axis=1)
        st = _dot_tn(jnp.concatenate(xds, axis=1).astype(BF16), bmb)
        for r in range(SSD_HPG):
            rows = _head_cols(r)
            h_scr[rows, :] = hg[rows, :] * jnp.exp(cs_last[:, r:r + 1]) + st[rows, :]
        y_ref[...] = y
        gts = y * _silu(z_ref[...])
        rr = lax.rsqrt(jnp.mean(gts * gts, axis=-1, keepdims=True) + EPS)
        yn_ref[...] = (gts * rr * nw_ref[...]).astype(BF16)

    return pl.pallas_call(
        body, name="ssd_fwd", grid=(SSD_GROUPS, nc),
        in_specs=[xs_s, b_s, c_s, lane_s, xs_s, vec_s, vec_s, vec_s, wide_s],
        out_specs=[xs_s, xs_s, hs_s],
        out_shape=[jax.ShapeDtypeStruct((n, SSD_INNER), F32), jax.ShapeDtypeStruct((n, SSD_INNER), BF16),
                   jax.ShapeDtypeStruct((nc, SSD_GROUPS, SSD_GW, SSD_STATE), F32)],
        scratch_shapes=[pltpu.VMEM((SSD_GW, SSD_STATE), F32)],
        compiler_params=_cparams("parallel", "arbitrary"))(pre, pre, pre, dt_raw, z, dtb, alog, dskip, norm_w)


def _lane_put(acc, col, r):
    lane = lax.broadcasted_iota(jnp.int32, acc.shape, 1)
    return jnp.where(lane == r, col, acc)


def _ssd_bwd(dyn, y, z, pre, dt_raw, hsave, dtb, alog, dskip, norm_w):
    n = pre.shape[0]
    nc = n // BLOCK
    xs_s, b_s, c_s, lane_s, vec_s, wide_s, hs_s = _ssd_specs(nc, True)
    bc_out = pl.BlockSpec((BLOCK, SSD_STATE), lambda g, c: (nc - 1 - c, g))

    def body(dyn_ref, y_ref, z_ref, pxs_ref, pb_ref, pc_ref, dtr_ref, hs_ref, dtb_ref, alog_ref, dsk_ref, nw_ref,
             dz_ref, dxs_ref, dbm_ref, dcm_ref, ddt_ref, dnw_ref, ddtb_ref, dalog_ref, ddsk_ref, g_scr):
        step = pl.program_id(1)
        c = nc - 1 - step

        @pl.when(step == 0)
        def _():
            g_scr[...] = jnp.zeros_like(g_scr)

        xs, bm, cm, valid, dtr, dt, a, lam, cs, causal = _ssd_prep(
            pxs_ref, pb_ref, pc_ref, dtr_ref, dtb_ref, alog_ref, c)
        cst = cs.T
        cs_last = cs[BLOCK - 1:BLOCK, :]
        bmb = bm.astype(BF16)
        cmb = cm.astype(BF16)
        cb = _dot_nt(cmb, bmb)
        hg = hs_ref[0, 0]
        hgb = hg.astype(BF16)
        yoff = _dot_nt(cmb, hgb)
        gn = g_scr[...]
        gnb = gn.astype(BF16)

        zv = z_ref[...]
        yv = y_ref[...]
        sz = _silu(zv)
        gts = yv * sz
        rr = lax.rsqrt(jnp.mean(gts * gts, axis=-1, keepdims=True) + EPS)
        xh = gts * rr
        dynv = dyn_ref[...]
        gg = dynv * nw_ref[...]
        dgts = rr * (gg - xh * jnp.mean(gg * xh, axis=-1, keepdims=True))
        dnw = jnp.sum(dynv * xh, axis=0, keepdims=True)
        dy = dgts * sz
        dz_ref[...] = (dgts * yv * _dsilu(zv)).astype(BF16)

        q_all = _dot_nt(bmb, gnb)
        zero_l = jnp.zeros((BLOCK, LANES), F32)
        dcs_col, ddt_x = zero_l, zero_l
        dcs_row = jnp.zeros((SUBLANES, BLOCK), F32)
        dcs_last = jnp.zeros((1, LANES), F32)
        ddsk = jnp.zeros((1, LANES), F32)
        dcb = jnp.zeros((BLOCK, BLOCK), F32)
        ws, xds, dxdts, decs = [], [], [], []
        for r in range(SSD_HPG):
            cols = _head_cols(r)
            csc = cs[:, r:r + 1]
            ecs = jnp.exp(csc)
            lm = jnp.exp(jnp.where(causal, csc - cst[r:r + 1, :], NEG))
            x_r = xs[:, cols]
            xdt = x_r * dt[:, r:r + 1]
            dy_r = dy[:, cols]
            dyb = dy_r.astype(BF16)
            dec = jnp.exp(cs_last[:, r:r + 1] - csc)
            ws.append(dy_r * ecs)
            col = jnp.sum(dy_r * yoff[:, cols], axis=1, keepdims=True) * ecs
            q_r = q_all[:, cols]
            e_r = jnp.sum(q_r * xdt, axis=1, keepdims=True) * dec
            col = col - e_r
            last = jnp.sum(e_r, axis=0, keepdims=True)
            rows = cols
            eh = jnp.exp(cs_last[:, r:r + 1])
            last = last + eh * _sum_all(gn[rows, :] * hg[rows, :])
            gm = _dot_nt(dyb, xdt.astype(BF16)) * lm
            dcb = dcb + gm
            mm_ = gm * cb
            col = col + jnp.sum(mm_, axis=1, keepdims=True)
            rowv = jnp.sum(mm_, axis=0, keepdims=True)
            sub = lax.broadcasted_iota(jnp.int32, (SUBLANES, BLOCK), 0)
            dcs_row = jnp.where(sub == r, rowv, dcs_row)
            dxdt = _dot_tn((cb * lm).astype(BF16), dyb) + q_r * dec
            dcs_col = _lane_put(dcs_col, col, r)
            dcs_last = _lane_put(dcs_last, last, r)
            ddt_x = _lane_put(ddt_x, jnp.sum(dxdt * x_r, axis=1, keepdims=True), r)
            ddsk = _lane_put(ddsk, _sum_all(dy_r * x_r), r)
            xds.append(xdt * dec)
            dxdts.append(dxdt * dt[:, r:r + 1] + dsk_ref[0:1, r:r + 1] * dy_r)
            decs.append(eh)
        w_all = jnp.concatenate(ws, axis=1).astype(BF16)
        xd_all = jnp.concatenate(xds, axis=1).astype(BF16)
        dcbb = dcb.astype(BF16)
        dcm = _dot(w_all, hgb) + _dot(dcbb, bmb)
        dbm = _dot(xd_all, gnb) + _dot_tn(dcbb, cmb)
        dh_off = _dot_tn(w_all, cmb)
        for r in range(SSD_HPG):
            rows = _head_cols(r)
            g_scr[rows, :] = gn[rows, :] * decs[r] + dh_off[rows, :]

        pad_rows = jnp.zeros((BLOCK - SUBLANES, BLOCK), F32)
        dcs = dcs_col - jnp.concatenate([dcs_row, pad_rows], axis=0).T
        rsel = lax.broadcasted_iota(jnp.int32, (BLOCK, LANES), 0)
        dcs = dcs + jnp.where(rsel == BLOCK - 1, dcs_last, 0.0)
        ri = lax.broadcasted_iota(jnp.int32, (BLOCK, BLOCK), 0)
        ci = lax.broadcasted_iota(jnp.int32, (BLOCK, BLOCK), 1)
        dlam = jnp.dot((ci >= ri).astype(F32), dcs, precision=HIGHEST, preferred_element_type=F32)
        lane = lax.broadcasted_iota(jnp.int32, (BLOCK, LANES), 1)
        head = lane < SSD_HPG
        ddt = dlam * a + ddt_x
        ddtr = jnp.where(head, ddt * jax.nn.sigmoid(dtr) * valid, 0.0)
        ddt_ref[...] = ddtr.astype(BF16)
        dalog = jnp.sum(jnp.where(head, dlam * lam, 0.0), axis=0, keepdims=True)
        ddtb = jnp.sum(ddtr, axis=0, keepdims=True)

        dxs_ref[...] = jnp.concatenate(dxdts, axis=1) * _dsilu(pxs_ref[...])
        dbm_ref[...] = dbm * _dsilu(pb_ref[...])
        dcm_ref[...] = dcm * _dsilu(pc_ref[...])

        @pl.when(step == 0)
        def _():
            dnw_ref[...] = dnw
            ddtb_ref[...] = ddtb
            dalog_ref[...] = dalog
            ddsk_ref[...] = ddsk

        @pl.when(step > 0)
        def _():
            dnw_ref[...] += dnw
            ddtb_ref[...] += ddtb
            dalog_ref[...] += dalog
            ddsk_ref[...] += ddsk

    return pl.pallas_call(
        body, name="ssd_bwd", grid=(SSD_GROUPS, nc),
        in_specs=[xs_s, xs_s, xs_s, xs_s, b_s, c_s, lane_s, hs_s, vec_s, vec_s, vec_s, wide_s],
        out_specs=[xs_s, xs_s, bc_out, bc_out, lane_s, wide_s, vec_s, vec_s, vec_s],
        out_shape=[jax.ShapeDtypeStruct((n, SSD_INNER), BF16), jax.ShapeDtypeStruct((n, SSD_INNER), F32),
                   jax.ShapeDtypeStruct((n, SSD_BC), F32), jax.ShapeDtypeStruct((n, SSD_BC), F32),
                   jax.ShapeDtypeStruct((n, DT_W), BF16), jax.ShapeDtypeStruct((1, SSD_INNER), F32),
                   jax.ShapeDtypeStruct((1, DT_W), F32), jax.ShapeDtypeStruct((1, DT_W), F32),
                   jax.ShapeDtypeStruct((1, DT_W), F32)],
        scratch_shapes=[pltpu.VMEM((SSD_GW, SSD_STATE), F32)],
        compiler_params=_cparams("parallel", "arbitrary"))(
            dyn, y, z, pre, pre, pre, dt_raw, hsave, dtb, alog, dskip, norm_w)


def _bucket_table():
    def bucket(dist):
        d = np.maximum(dist, 0)
        half = REL_BUCKETS // 2
        big = half + (np.log(np.maximum(d, half).astype(np.float32) / np.float32(half))
                      / np.float32(math.log(REL_MAX_DIST / half)) * np.float32(REL_BUCKETS - half)).astype(np.int32)
        return np.where(d < half, d, np.minimum(big, REL_BUCKETS - 1)).astype(np.int32)

    l = np.arange(BLOCK)[:, None]
    band = bucket(l + BLOCK - np.arange(2 * BLOCK)[None, :])
    j = np.arange(BLOCK)[None, :]
    metas = [bucket(l - j), bucket(BLOCK + l - j), bucket(2 * BLOCK + l - j)]
    return np.concatenate([band.reshape(-1)] + [m.reshape(-1) for m in metas])


def _onehot_t():
    buckets = jnp.asarray(_bucket_table())
    return (buckets[None, :] == jnp.arange(REL_BUCKETS, dtype=jnp.int32)[:, None]).astype(F32)


def _bias_tables(rel_t, onehot_t):
    def body(r_ref, oh_ref, o_ref):
        o_ref[...] = jnp.dot(r_ref[...], oh_ref[...], precision=HIGHEST, preferred_element_type=F32)

    return pl.pallas_call(
        body, name="bias_tables", grid=(NT_ALL // NT_TILE,),
        in_specs=[pl.BlockSpec((ATT_HEADS, REL_BUCKETS), lambda i: (0, 0)),
                  pl.BlockSpec((REL_BUCKETS, NT_TILE), lambda i: (0, i))],
        out_specs=pl.BlockSpec((ATT_HEADS, NT_TILE), lambda i: (0, i)),
        out_shape=jax.ShapeDtypeStruct((ATT_HEADS, NT_ALL), F32),
        compiler_params=_cparams("parallel"))(rel_t, onehot_t)


def _bias_grad(dtab, onehot_t):
    def body(d_ref, oh_ref, o_ref):
        i = pl.program_id(0)
        p = lax.dot_general(d_ref[...], oh_ref[...], (((1,), (1,)), ((), ())), precision=HIGHEST,
                            preferred_element_type=F32)

        @pl.when(i == 0)
        def _():
            o_ref[...] = p

        @pl.when(i > 0)
        def _():
            o_ref[...] += p

    return pl.pallas_call(
        body, name="bias_grad", grid=(NT_ALL // NT_TILE,),
        in_specs=[pl.BlockSpec((ATT_HEADS, NT_TILE), lambda i: (0, i)),
                  pl.BlockSpec((REL_BUCKETS, NT_TILE), lambda i: (0, i))],
        out_specs=pl.BlockSpec((ATT_HEADS, REL_BUCKETS), lambda i: (0, 0)),
        out_shape=jax.ShapeDtypeStruct((ATT_HEADS, REL_BUCKETS), F32),
        compiler_params=_cparams("arbitrary"))(dtab, onehot_t)


def _att_masks(n):
    far = 4 * BLOCK
    li = lax.broadcasted_iota(jnp.int32, (BLOCK, BLOCK), 0)
    ki = lax.broadcasted_iota(jnp.int32, (BLOCK, BLOCK), 1)
    m_meta = (ki >= PAD) & (li + jnp.where(n >= 1, far, 0) >= ki)
    li2 = lax.broadcasted_iota(jnp.int32, (BLOCK, 2 * BLOCK), 0)
    ki2 = lax.broadcasted_iota(jnp.int32, (BLOCK, 2 * BLOCK), 1)
    prev_ok = (ki2 < BLOCK) & (ki2 > li2 + jnp.where(n >= 2, 0, far))
    cur_ok = (ki2 >= BLOCK) & (ki2 - BLOCK <= li2 - jnp.where(n >= 1, 0, far))
    return m_meta, prev_ok | cur_ok


def _att_probs(qh, k_meta, k_band, b_meta, b_band, m_meta, m_band, sink):
    scale = ATT_HEADDIM ** -0.5
    s_m = jnp.where(m_meta, _dot_nt(qh, k_meta) * scale + b_meta, NEG)
    s_b = jnp.where(m_band, _dot_nt(qh, k_band) * scale + b_band, NEG)
    mx = jnp.maximum(jnp.maximum(jnp.max(s_m, axis=1, keepdims=True), jnp.max(s_b, axis=1, keepdims=True)), sink)
    p_m = jnp.exp(s_m - mx)
    p_b = jnp.exp(s_b - mx)
    p_s = jnp.exp(sink - mx)
    inv = 1.0 / (jnp.sum(p_m, axis=1, keepdims=True) + jnp.sum(p_b, axis=1, keepdims=True) + p_s)
    return p_m * inv, p_b * inv, p_s * inv


def _kv_cols(kind, kh):
    base = ATT_KV * kind + ATT_HEADDIM * kh
    return slice(base, base + ATT_HEADDIM)


def _att_specs(nb, rev):
    def nidx(i):
        return nb - 1 - i if rev else i

    kvb = ATT_Q // (2 * ATT_KV)
    q_s = pl.BlockSpec((BLOCK, ATT_Q), lambda i: (nidx(i), 0))
    cur = pl.BlockSpec((BLOCK, 2 * ATT_KV), lambda i: (nidx(i), kvb))
    prev = pl.BlockSpec((BLOCK, 2 * ATT_KV), lambda i: (jnp.maximum(nidx(i) - 1, 0), kvb))
    meta = pl.BlockSpec((BLOCK, 2 * ATT_KV), lambda i: (0, kvb))
    tmeta = pl.BlockSpec((1, ATT_HEADS, BLOCK, BLOCK), lambda i: (jnp.minimum(nidx(i), 2), 0, 0, 0))
    tband = pl.BlockSpec((ATT_HEADS, BLOCK, 2 * BLOCK), lambda i: (0, 0, 0))
    sink = pl.BlockSpec((1, LANES), lambda i: (0, 0))
    return q_s, cur, prev, meta, tmeta, tband, sink


def _attn_fwd(qkv, t_meta, t_band, sinks):
    n = qkv.shape[0]
    nb = n // BLOCK
    q_s, cur_s, prev_s, meta_s, tm_s, tb_s, sink_s = _att_specs(nb, False)

    def body(q_ref, cur_ref, prev_ref, meta_ref, tm_ref, tb_ref, sink_ref, o_ref):
        blk = pl.program_id(0)
        m_meta, m_band = _att_masks(blk)
        kv_band = jnp.concatenate([prev_ref[...], cur_ref[...]], axis=0).astype(BF16)
        kv_meta = meta_ref[...].astype(BF16)
        for kh in range(ATT_KV_HEADS):
            k_meta, v_meta = kv_meta[:, _kv_cols(0, kh)], kv_meta[:, _kv_cols(1, kh)]
            k_band, v_band = kv_band[:, _kv_cols(0, kh)], kv_band[:, _kv_cols(1, kh)]
            for gq in range(ATT_GQ):
                h = kh * ATT_GQ + gq
                cols = slice(ATT_HEADDIM * h, ATT_HEADDIM * (h + 1))
                qh = q_ref[:, cols].astype(BF16)
                p_m, p_b, _ = _att_probs(qh, k_meta, k_band, tm_ref[0, h], tb_ref[h], m_meta, m_band,
                                         sink_ref[0:1, h:h + 1])
                o_ref[:, cols] = (_dot(p_m.astype(BF16), v_meta) + _dot(p_b.astype(BF16), v_band)).astype(BF16)

    return pl.pallas_call(
        body, name="attn_fwd", grid=(nb,),
        in_specs=[q_s, cur_s, prev_s, meta_s, tm_s, tb_s, sink_s],
        out_specs=q_s,
        out_shape=jax.ShapeDtypeStruct((n, ATT_Q), BF16),
        compiler_params=_cparams("parallel"))(qkv, qkv, qkv, qkv, t_meta, t_band, sinks)


def _attn_bwd(datt, qkv, t_meta, t_band, sinks):
    n = qkv.shape[0]
    nb = n // BLOCK
    q_s, cur_s, prev_s, meta_s, tm_s, tb_s, sink_s = _att_specs(nb, True)
    dqkv_s = pl.BlockSpec((BLOCK, ATT_Q + 2 * ATT_KV), lambda i: (nb - 1 - i, 0))
    scale = ATT_HEADDIM ** -0.5

    def body(do_ref, q_ref, cur_ref, prev_ref, meta_ref, tm_ref, tb_ref, sink_ref,
             dqkv_ref, dtm_ref, dtb_ref, dsink_ref, carry_scr, meta_scr):
        step = pl.program_id(0)
        blk = nb - 1 - step
        m_meta, m_band = _att_masks(blk)
        kv_band = jnp.concatenate([prev_ref[...], cur_ref[...]], axis=0).astype(BF16)
        kv_meta = meta_ref[...].astype(BF16)

        @pl.when(step == 0)
        def _():
            carry_scr[...] = jnp.zeros_like(carry_scr)
            meta_scr[...] = jnp.zeros_like(meta_scr)
            dtb_ref[...] = jnp.zeros_like(dtb_ref)
            dsink_ref[...] = jnp.zeros_like(dsink_ref)

        @pl.when((step == 0) | (blk <= 1))
        def _():
            dtm_ref[...] = jnp.zeros_like(dtm_ref)

        dsink = jnp.zeros((1, LANES), F32)
        dkv_band = [None] * (2 * ATT_KV_HEADS)
        dkv_meta = [None] * (2 * ATT_KV_HEADS)
        for kh in range(ATT_KV_HEADS):
            k_meta, v_meta = kv_meta[:, _kv_cols(0, kh)], kv_meta[:, _kv_cols(1, kh)]
            k_band, v_band = kv_band[:, _kv_cols(0, kh)], kv_band[:, _kv_cols(1, kh)]
            dk_m = dv_m = dk_b = dv_b = None
            for gq in range(ATT_GQ):
                h = kh * ATT_GQ + gq
                cols = slice(ATT_HEADDIM * h, ATT_HEADDIM * (h + 1))
                qh = q_ref[:, cols].astype(BF16)
                doh = do_ref[:, cols].astype(BF16)
                p_m, p_b, p_s = _att_probs(qh, k_meta, k_band, tm_ref[0, h], tb_ref[h], m_meta, m_band,
                                           sink_ref[0:1, h:h + 1])
                dp_m = _dot_nt(doh, v_meta)
                dp_b = _dot_nt(doh, v_band)
                delta = jnp.sum(p_m * dp_m, axis=1, keepdims=True) + jnp.sum(p_b * dp_b, axis=1, keepdims=True)
                ds_m = p_m * (dp_m - delta)
                ds_b = p_b * (dp_b - delta)
                dsink = _lane_put(dsink, dsink[0:1, h:h + 1] - jnp.sum(p_s * delta, axis=0, keepdims=True), h)
                dtm_ref[0, h] += ds_m
                dtb_ref[h] += ds_b
                ds_mb, ds_bb = ds_m.astype(BF16), ds_b.astype(BF16)
                dqkv_ref[:, cols] = ((_dot(ds_mb, k_meta) + _dot(ds_bb, k_band)) * scale).astype(BF16)
                parts = (_dot_tn(ds_mb, qh) * scale, _dot_tn(p_m.astype(BF16), doh),
                         _dot_tn(ds_bb, qh) * scale, _dot_tn(p_b.astype(BF16), doh))
                if gq == 0:
                    dk_m, dv_m, dk_b, dv_b = parts
                else:
                    dk_m, dv_m, dk_b, dv_b = dk_m + parts[0], dv_m + parts[1], dk_b + parts[2], dv_b + parts[3]
            dkv_meta[kh], dkv_meta[ATT_KV_HEADS + kh] = dk_m, dv_m
            dkv_band[kh], dkv_band[ATT_KV_HEADS + kh] = dk_b, dv_b
        dsink_ref[...] += dsink
        band = jnp.concatenate(dkv_band, axis=1)
        meta_scr[...] += jnp.concatenate(dkv_meta, axis=1)
        own = band[BLOCK:, :] + carry_scr[...]
        carry_scr[...] = band[:BLOCK, :]

        @pl.when(blk > 0)
        def _():
            dqkv_ref[:, ATT_Q:] = own.astype(BF16)

        @pl.when(blk == 0)
        def _():
            dqkv_ref[:, ATT_Q:] = (own + meta_scr[...]).astype(BF16)

    return pl.pallas_call(
        body, name="attn_bwd", grid=(nb,),
        in_specs=[q_s, q_s, cur_s, prev_s, meta_s, tm_s, tb_s, sink_s],
        out_specs=[dqkv_s, tm_s, tb_s, sink_s],
        out_shape=[jax.ShapeDtypeStruct((n, ATT_Q + 2 * ATT_KV), BF16),
                   jax.ShapeDtypeStruct((3, ATT_HEADS, BLOCK, BLOCK), F32),
                   jax.ShapeDtypeStruct((ATT_HEADS, BLOCK, 2 * BLOCK), F32),
                   jax.ShapeDtypeStruct((1, LANES), F32)],
        scratch_shapes=[pltpu.VMEM((BLOCK, 2 * ATT_KV), F32), pltpu.VMEM((BLOCK, 2 * ATT_KV), F32)],
        compiler_params=_cparams("arbitrary"))(datt, qkv, qkv, qkv, qkv, t_meta, t_band, sinks)


def _merge_fwd(gates, y_ssd, y_att, gate_b):
    n = gates.shape[0]
    tm = _row_tile(n, 832)

    def body(gs_ref, ga_ref, ys_ref, ya_ref, gb_ref, o_ref):
        o_ref[...] = (jax.nn.sigmoid(gs_ref[...] + gb_ref[0:1, :]) * ys_ref[...]
                      + jax.nn.sigmoid(ga_ref[...] + gb_ref[1:2, :]) * ya_ref[...]).astype(BF16)

    row = pl.BlockSpec((tm, D_MODEL), lambda i: (i, 0))
    return pl.pallas_call(
        body, name="merge_fwd", grid=(n // tm,),
        in_specs=[row, pl.BlockSpec((tm, D_MODEL), lambda i: (i, 1)), row, row,
                  pl.BlockSpec((2, D_MODEL), lambda i: (0, 0))],
        out_specs=row, out_shape=jax.ShapeDtypeStruct((n, D_MODEL), BF16),
        compiler_params=_cparams("parallel"))(gates, gates, y_ssd, y_att, gate_b)


def _merge_bwd(dm, gates, y_ssd, y_att, gate_b):
    n = gates.shape[0]
    tm = _row_tile(n, 832)

    def body(dm_ref, gs_ref, ga_ref, ys_ref, ya_ref, gb_ref, dys_ref, dya_ref, dg_ref, dgb_ref):
        i = pl.program_id(0)
        dmv = dm_ref[...]
        ss = jax.nn.sigmoid(gs_ref[...] + gb_ref[0:1, :])
        sa = jax.nn.sigmoid(ga_ref[...] + gb_ref[1:2, :])
        dys_ref[...] = (dmv * ss).astype(BF16)
        dya_ref[...] = (dmv * sa).astype(BF16)
        dgs = dmv * ys_ref[...] * ss * (1.0 - ss)
        dga = dmv * ya_ref[...] * sa * (1.0 - sa)
        dg_ref[:, :D_MODEL] = dgs.astype(BF16)
        dg_ref[:, D_MODEL:] = dga.astype(BF16)
        part = jnp.concatenate([jnp.sum(dgs, axis=0, keepdims=True), jnp.sum(dga, axis=0, keepdims=True)], axis=0)

        @pl.when(i == 0)
        def _():
            dgb_ref[...] = part

        @pl.when(i > 0)
        def _():
            dgb_ref[...] += part

    row = pl.BlockSpec((tm, D_MODEL), lambda i: (i, 0))
    gb = pl.BlockSpec((2, D_MODEL), lambda i: (0, 0))
    return pl.pallas_call(
        body, name="merge_bwd", grid=(n // tm,),
        in_specs=[row, row, pl.BlockSpec((tm, D_MODEL), lambda i: (i, 1)), row, row, gb],
        out_specs=[row, row, pl.BlockSpec((tm, 2 * D_MODEL), lambda i: (i, 0)), gb],
        out_shape=[jax.ShapeDtypeStruct((n, D_MODEL), BF16), jax.ShapeDtypeStruct((n, D_MODEL), BF16),
                   jax.ShapeDtypeStruct((n, 2 * D_MODEL), BF16), jax.ShapeDtypeStruct((2, D_MODEL), F32)],
        compiler_params=_cparams("arbitrary"))(dm, gates, gates, y_ssd, y_att, gate_b)


_RELATIONS = [(dx, dy, dc) for dx in (0, 1) for dy in (0, 1) for dc in (0, 1)][1:]


def _exchange(arrays, scatter, *, name):
    n_arr = len(arrays)
    n_rel = len(_RELATIONS)

    def body(*refs):
        ins, outs = refs[:n_arr], refs[n_arr:2 * n_arr]
        send_sems, recv_sems, local_sems = refs[2 * n_arr:]
        x, y, c = lax.axis_index("x"), lax.axis_index("y"), lax.axis_index("c")
        me = 4 * x + 2 * y + c
        copies = []
        for a in range(n_arr):
            src = ins[a].at[me] if scatter[a] else ins[a]
            local = pltpu.make_async_copy(src, outs[a].at[me], local_sems.at[a])
            local.start()
            copies.append(local)
        remote = []
        for k, (dx, dy, dc) in enumerate(_RELATIONS):
            px, py, pc = x ^ dx, y ^ dy, c ^ dc
            peer = 4 * px + 2 * py + pc
            for a in range(n_arr):
                src = ins[a].at[peer] if scatter[a] else ins[a]
                cp = pltpu.make_async_remote_copy(
                    src_ref=src, dst_ref=outs[a].at[me], send_sem=send_sems.at[a * n_rel + k],
                    recv_sem=recv_sems.at[a * n_rel + k], device_id=(px, py, pc), device_id_type=MESH)
                cp.start()
                remote.append((cp, a, k, peer))
        for cp, a, k, peer in remote:
            cp.wait_send()
        for cp, a, k, peer in remote:
            src = ins[a].at[peer] if scatter[a] else ins[a]
            pltpu.make_async_remote_copy(
                src_ref=src, dst_ref=outs[a].at[peer], send_sem=send_sems.at[a * n_rel + k],
                recv_sem=recv_sems.at[a * n_rel + k], device_id=(x, y, c), device_id_type=MESH).wait_recv()
        for local in copies:
            local.wait()

    out_shape = [jax.ShapeDtypeStruct((N_DEV,) + (a.shape[1:] if s else a.shape), a.dtype)
                 for a, s in zip(arrays, scatter)]
    any_spec = pl.BlockSpec(memory_space=pl.ANY)
    return pl.pallas_call(
        body, name=name, in_specs=[any_spec] * n_arr, out_specs=[any_spec] * n_arr, out_shape=out_shape,
        scratch_shapes=[pltpu.SemaphoreType.DMA((n_arr * n_rel,)), pltpu.SemaphoreType.DMA((n_arr * n_rel,)),
                        pltpu.SemaphoreType.DMA((n_arr,))],
    )(*arrays)


def _adamw(w, gslots, m, v, *, name):
    rows, cols = w.shape
    tr = _row_tile(rows, 128) if rows % 16 == 0 else rows
    c1 = 1.0 / (1.0 - ADAM_B1 ** ADAM_STEP)
    c2 = 1.0 / (1.0 - ADAM_B2 ** ADAM_STEP)

    def body(w_ref, g_ref, m_ref, v_ref, go_ref, d_ref, mo_ref, vo_ref):
        g = g_ref[0]
        for s in range(1, N_DEV):
            g = g + g_ref[s]
        mn = ADAM_B1 * m_ref[...] + (1.0 - ADAM_B1) * g
        vn = ADAM_B2 * v_ref[...] + (1.0 - ADAM_B2) * (g * g)
        go_ref[...] = g
        mo_ref[...] = mn
        vo_ref[...] = vn
        d_ref[...] = -ADAM_LR * ((mn * c1) / (jnp.sqrt(vn * c2) + ADAM_EPS) + ADAM_WD * w_ref[...])

    blk = pl.BlockSpec((tr, cols), lambda i: (i, 0))
    shp = jax.ShapeDtypeStruct((rows, cols), F32)
    return pl.pallas_call(
        body, name=name, grid=(rows // tr,),
        in_specs=[blk, pl.BlockSpec((N_DEV, tr, cols), lambda i: (0, i, 0)), blk, blk],
        out_specs=[blk] * 4, out_shape=[shp] * 4,
        compiler_params=_cparams("parallel"))(w, gslots, m, v)


_BIG = ("w_in", "w_ssd_branch", "w_attn_branch", "w_out", "w_ffn_in", "w_ffn_out")
_SMALL_SHARDED = ("meta_tokens", "ssd_conv_w", "gate_b", "ffn_conv_w")
_SMALL_REPLICATED = ("norm_mix_w", "ssd_conv_b", "ssd_dt_bias", "ssd_a_log", "ssd_d", "ssd_norm_w", "attn_sinks",
                     "rel_bias", "norm_ffn_w", "ffn_conv_b", "norm_final_w")
_WEIGHTS = ("meta_tokens", "norm_mix_w", "w_in", "ssd_conv_w", "ssd_conv_b", "ssd_dt_bias", "ssd_a_log", "ssd_d",
            "ssd_norm_w", "w_ssd_branch", "w_attn_branch", "attn_sinks", "rel_bias", "gate_b", "w_out", "norm_ffn_w",
            "w_ffn_in", "ffn_conv_w", "ffn_conv_b", "w_ffn_out", "norm_final_w")
_COL_SHARDED = ("w_in", "w_ffn_in", "meta_tokens", "ssd_conv_w", "gate_b", "ffn_conv_w")
_IN_SEGS = (("z", SSD_INNER), ("xbc", SSD_XBC), ("dt", SSD_HEADS), ("qkv", ATT_Q + 2 * ATT_KV), ("g", 2 * D_MODEL))


def _pack_rows(flat_parts, width, row_mult):
    flat = jnp.concatenate([p.reshape(-1) for p in flat_parts])
    pad = (-flat.shape[0]) % (width * row_mult)
    if pad:
        flat = jnp.concatenate([flat, jnp.zeros((pad,), flat.dtype)])
    return flat.reshape(-1, width)


def _unpack(flat, shapes):
    out, off = [], 0
    for shp in shapes:
        size = int(np.prod(shp))
        out.append(flat[off:off + size].reshape(shp))
        off += size
    return out


def _gather_full(stack, name, shard_shape):
    if name in _COL_SHARDED:
        return jnp.transpose(stack, (1, 0, 2)).reshape(shard_shape[0], N_DEV * shard_shape[1])
    return stack.reshape(N_DEV * shard_shape[0], shard_shape[1])


def _to_shards(full, name):
    r, c = full.shape
    if name in _COL_SHARDED:
        return jnp.transpose(full.reshape(r, N_DEV, c // N_DEV), (1, 0, 2))
    return full.reshape(N_DEV, r // N_DEV, c)


def _dt_spread(w_dt):
    k = w_dt.shape[0]
    w4 = w_dt.reshape(k, SSD_GROUPS, SSD_HPG)
    return jnp.pad(w4, ((0, 0), (0, 0), (0, LANES - SSD_HPG))).reshape(k, DT_W)


def _dt_gather(w_wide):
    k = w_wide.shape[0]
    return w_wide.reshape(k, SSD_GROUPS, LANES)[:, :, :SSD_HPG].reshape(k, SSD_HEADS)


def _local_step(x, target, w):
    seq = x.shape[0]
    n = seq + BLOCK
    h0 = jnp.concatenate([jnp.zeros((PAD, D_MODEL), F32), w["meta_tokens"], x], axis=0)

    segs, off = {}, 0
    for nm, width in _IN_SEGS:
        segs[nm] = w["w_in"][:, off:off + width]
        off += width
    segs["dt"] = _dt_spread(segs["dt"])
    segs_t = {nm: s.T for nm, s in segs.items()}
    w_ffn_up, w_ffn_gate = w["w_ffn_in"][:, :D_FF], w["w_ffn_in"][:, D_FF:]

    dtb = _dt_spread(w["ssd_dt_bias"])
    alog = _dt_spread(w["ssd_a_log"])
    dskip = _dt_spread(w["ssd_d"])
    sinks = jnp.pad(w["attn_sinks"], ((0, 0), (0, LANES - ATT_HEADS)))
    onehot_t = _onehot_t()
    tabs = _bias_tables(w["rel_bias"].T, onehot_t)
    t_band = tabs[:, :NT_BAND].reshape(ATT_HEADS, BLOCK, 2 * BLOCK)
    t_meta = jnp.transpose(tabs[:, NT_BAND:].reshape(ATT_HEADS, 3, BLOCK, BLOCK), (1, 0, 2, 3))

    u = _rms_fwd(h0, w["norm_mix_w"], name="rms_mix_fwd")
    z = _mm(u, segs["z"], name="in_z")
    xbc = _mm(u, segs["xbc"], name="in_xbc")
    dt_raw = _mm(u, segs["dt"], name="in_dt")
    qkv = _mm(u, segs["qkv"], name="in_qkv")
    gates = _mm(u, segs["g"], name="in_g")
    pre = _conv_fwd(xbc, w["ssd_conv_w"], w["ssd_conv_b"], name="ssd_conv_fwd")
    y, yn, hsave = _ssd_fwd(pre, dt_raw, z, dtb, alog, dskip, w["ssd_norm_w"])
    y_ssd = _mm(yn, w["w_ssd_branch"], name="ssd_out")
    att = _attn_fwd(qkv, t_meta, t_band, sinks)
    y_att = _mm(att, w["w_attn_branch"], name="att_out")
    merged = _merge_fwd(gates, y_ssd, y_att, w["gate_b"])
    h1 = _mm(merged, w["w_out"], c=h0, mask=True, name="mix_out")
    u2 = _rms_fwd(h1, w["norm_ffn_w"], name="rms_ffn_fwd")
    hid_raw = _mm(u2, w["w_ffn_in"], name="ffn_in")
    hid_up, hid_gate, act = _ffn_act_fwd(hid_raw, w["ffn_conv_w"], w["ffn_conv_b"])
    h2 = _mm(act, w["w_ffn_out"], c=h1, mask=True, name="ffn_out")
    dh2, loss_row, g_norm_final = _final_loss(h2, w["norm_final_w"], target)

    grads = {"norm_final_w": g_norm_final}
    dact = _mm(dh2, w["w_ffn_out"].T, mask=True, name="d_act")
    grads["w_ffn_out"] = _mm(act, dh2, ta=True, mask=True, name="g_w_ffn_out")
    dx_up, dx_gate, dcw_up, dcw_gate, dcb_up, dcb_gate = _ffn_act_bwd(dact, hid_up, hid_gate, hid_raw, w["ffn_conv_w"])
    grads["ffn_conv_w"] = jnp.concatenate([dcw_up, dcw_gate], axis=1)
    grads["ffn_conv_b"] = jnp.concatenate([dcb_up, dcb_gate], axis=1)
    du2 = _mm(dx_up, w_ffn_up.T, name="d_u2_up")
    du2 = _mm(dx_gate, w_ffn_gate.T, c=du2, name="d_u2_gate")
    grads["w_ffn_in"] = jnp.concatenate([_mm(u2, dx_up, ta=True, name="g_w_ffn_up"),
                                         _mm(u2, dx_gate, ta=True, name="g_w_ffn_gate")], axis=1)
    dh1, grads["norm_ffn_w"] = _rms_bwd(h1, w["norm_ffn_w"], du2, dh2, name="rms_ffn_bwd")

    dmerged = _mm(dh1, w["w_out"].T, mask=True, name="d_merged")
    grads["w_out"] = _mm(merged, dh1, ta=True, mask=True, name="g_w_out")
    dy_ssd, dy_att, dgates, grads["gate_b"] = _merge_bwd(dmerged, gates, y_ssd, y_att, w["gate_b"])
    dyn = _mm(dy_ssd, w["w_ssd_branch"].T, name="d_yn")
    grads["w_ssd_branch"] = _mm(yn, dy_ssd, ta=True, name="g_w_ssd")
    datt = _mm(dy_att, w["w_attn_branch"].T, name="d_att")
    grads["w_attn_branch"] = _mm(att, dy_att, ta=True, name="g_w_att")
    dz, dpxs, dpb, dpc, ddt, grads["ssd_norm_w"], g_dtb, g_alog, g_dskip = _ssd_bwd(
        dyn, y, z, pre, dt_raw, hsave, dtb, alog, dskip, w["ssd_norm_w"])
    grads["ssd_dt_bias"] = _dt_gather(g_dtb)
    grads["ssd_a_log"] = _dt_gather(g_alog)
    grads["ssd_d"] = _dt_gather(g_dskip)
    dpre = jnp.concatenate([dpxs, dpb, dpc], axis=1)
    dxbc, grads["ssd_conv_w"], grads["ssd_conv_b"] = _conv_bwd(dpre, xbc, w["ssd_conv_w"], name="ssd_conv_bwd")
    dqkv, d_tmeta, d_tband, d_sinks = _attn_bwd(datt, qkv, t_meta, t_band, sinks)
    grads["attn_sinks"] = d_sinks[:, :ATT_HEADS]
    dtab = jnp.concatenate([d_tband.reshape(ATT_HEADS, NT_BAND),
                            jnp.transpose(d_tmeta, (1, 0, 2, 3)).reshape(ATT_HEADS, 3 * NT_META)], axis=1)
    grads["rel_bias"] = _bias_grad(dtab, onehot_t).T
    dsegs = {"z": dz, "xbc": dxbc, "dt": ddt, "qkv": dqkv, "g": dgates}
    du, g_in = None, []
    for nm, _ in _IN_SEGS:
        du = _mm(dsegs[nm], segs_t[nm], c=du, name="d_u_" + nm)
        gw = _mm(u, dsegs[nm], ta=True, name="g_w_in_" + nm)
        g_in.append(_dt_gather(gw) if nm == "dt" else gw)
    grads["w_in"] = jnp.concatenate(g_in, axis=1)
    dh0, grads["norm_mix_w"] = _rms_bwd(h0, w["norm_mix_w"], du, dh1, name="rms_mix_bwd")
    grads["meta_tokens"] = dh0[PAD:BLOCK]
    return loss_row[0, 0], dh0[BLOCK:], grads


def kernel(x, meta_tokens, norm_mix_w, w_in, ssd_conv_w, ssd_conv_b, ssd_dt_bias, ssd_a_log, ssd_d, ssd_norm_w, w_ssd_branch, w_attn_branch, attn_sinks, rel_bias, gate_b, w_out, norm_ffn_w, w_ffn_in, ffn_conv_w, ffn_conv_b, w_ffn_out, norm_final_w, loss_target, m_meta_tokens, m_norm_mix_w, m_w_in, m_ssd_conv_w, m_ssd_conv_b, m_ssd_dt_bias, m_ssd_a_log, m_ssd_d, m_ssd_norm_w, m_w_ssd_branch, m_w_attn_branch, m_attn_sinks, m_rel_bias, m_gate_b, m_w_out, m_norm_ffn_w, m_w_ffn_in, m_ffn_conv_w, m_ffn_conv_b, m_w_ffn_out, m_norm_final_w, v_meta_tokens, v_norm_mix_w, v_w_in, v_ssd_conv_w, v_ssd_conv_b, v_ssd_dt_bias, v_ssd_a_log, v_ssd_d, v_ssd_norm_w, v_w_ssd_branch, v_w_attn_branch, v_attn_sinks, v_rel_bias, v_gate_b, v_w_out, v_norm_ffn_w, v_w_ffn_in, v_ffn_conv_w, v_ffn_conv_b, v_w_ffn_out, v_norm_final_w):
    shard = dict(meta_tokens=meta_tokens, norm_mix_w=norm_mix_w, w_in=w_in, ssd_conv_w=ssd_conv_w,
                 ssd_conv_b=ssd_conv_b, ssd_dt_bias=ssd_dt_bias, ssd_a_log=ssd_a_log, ssd_d=ssd_d,
                 ssd_norm_w=ssd_norm_w, w_ssd_branch=w_ssd_branch, w_attn_branch=w_attn_branch,
                 attn_sinks=attn_sinks, rel_bias=rel_bias, gate_b=gate_b, w_out=w_out, norm_ffn_w=norm_ffn_w,
                 w_ffn_in=w_ffn_in, ffn_conv_w=ffn_conv_w, ffn_conv_b=ffn_conv_b, w_ffn_out=w_ffn_out,
                 norm_final_w=norm_final_w)
    mom_m = dict(zip(_WEIGHTS, (m_meta_tokens, m_norm_mix_w, m_w_in, m_ssd_conv_w, m_ssd_conv_b, m_ssd_dt_bias,
                                m_ssd_a_log, m_ssd_d, m_ssd_norm_w, m_w_ssd_branch, m_w_attn_branch, m_attn_sinks,
                                m_rel_bias, m_gate_b, m_w_out, m_norm_ffn_w, m_w_ffn_in, m_ffn_conv_w, m_ffn_conv_b,
                                m_w_ffn_out, m_norm_final_w)))
    mom_v = dict(zip(_WEIGHTS, (v_meta_tokens, v_norm_mix_w, v_w_in, v_ssd_conv_w, v_ssd_conv_b, v_ssd_dt_bias,
                                v_ssd_a_log, v_ssd_d, v_ssd_norm_w, v_w_ssd_branch, v_w_attn_branch, v_attn_sinks,
                                v_rel_bias, v_gate_b, v_w_out, v_norm_ffn_w, v_w_ffn_in, v_ffn_conv_w, v_ffn_conv_b,
                                v_w_ffn_out, v_norm_final_w)))
    orig_shape = {k: a.shape for k, a in shard.items()}
    two_d = {k: a.reshape(a.shape[-2:]) if a.ndim >= 2 else a.reshape(1, -1) for k, a in shard.items()}
    shape2 = {k: a.shape for k, a in two_d.items()}

    def as2d(tree):
        return {k: tree[k].reshape(shape2[k]) for k in _WEIGHTS}

    mom_m, mom_v = as2d(mom_m), as2d(mom_v)

    big_pack = _pack_rows([two_d[k].astype(BF16) for k in _BIG], D_MODEL, BIG_ROW_MULT)
    small_pack = _pack_rows([two_d[k] for k in _SMALL_SHARDED], LANES, SMALL_ROW_MULT)
    big_all, small_all = _exchange([big_pack, small_pack], [False, False], name="gather_weights")
    full = {k: two_d[k] for k in _SMALL_REPLICATED}
    big_flat = big_all.reshape(N_DEV, -1)
    small_flat = small_all.reshape(N_DEV, -1)
    off = 0
    for k in _BIG:
        size = int(np.prod(shape2[k]))
        full[k] = _gather_full(big_flat[:, off:off + size].reshape((N_DEV,) + shape2[k]), k, shape2[k])
        off += size
    off = 0
    for k in _SMALL_SHARDED:
        size = int(np.prod(shape2[k]))
        full[k] = _gather_full(small_flat[:, off:off + size].reshape((N_DEV,) + shape2[k]), k, shape2[k])
        off += size

    loss_local, grad_x, grads = _local_step(x[0], loss_target[0], full)

    big_send = jnp.concatenate([_to_shards(grads[k], k).reshape(N_DEV, -1) for k in _BIG], axis=1)
    big_send = jnp.pad(big_send, ((0, 0), (0, (-big_send.shape[1]) % (D_MODEL * BIG_ROW_MULT))))
    big_send = big_send.reshape(N_DEV, -1, D_MODEL)
    small_names = _SMALL_SHARDED + _SMALL_REPLICATED
    small_send = _pack_rows([grads[k] for k in small_names] + [loss_local.reshape(1)], LANES, SMALL_ROW_MULT)
    big_recv, small_recv = _exchange([big_send, small_send], [True, False], name="exchange_grads")

    def packed_big(tree):
        return _pack_rows([tree[k] for k in _BIG], D_MODEL, BIG_ROW_MULT)

    g_big, d_big, m_big, v_big = _adamw(packed_big(two_d), big_recv, packed_big(mom_m), packed_big(mom_v),
                                        name="adamw_big")
    me = 4 * lax.axis_index("x") + 2 * lax.axis_index("y") + lax.axis_index("c")
    small_full_shapes = [grads[k].shape for k in small_names]
    n_small = sum(int(np.prod(s)) for s in small_full_shapes)

    def packed_small(tree):
        parts = []
        for k in small_names:
            a = tree[k]
            if k in _SMALL_SHARDED:
                fullw = jnp.zeros(grads[k].shape, F32)
                a = lax.dynamic_update_slice(fullw, a, (0, me * a.shape[1]))
            parts.append(a)
        return _pack_rows(parts + [jnp.zeros((1,), F32)], LANES, SMALL_ROW_MULT)

    g_small, d_small, m_small, v_small = _adamw(packed_small(two_d), small_recv, packed_small(mom_m),
                                                packed_small(mom_v), name="adamw_small")

    def unpack_all(big, small):
        out = dict(zip(_BIG, _unpack(big.reshape(-1), [shape2[k] for k in _BIG])))
        flat = small.reshape(-1)
        for k, a in zip(small_names, _unpack(flat, small_full_shapes)):
            if k in _SMALL_SHARDED:
                a = lax.dynamic_slice(a, (0, me * shape2[k][1]), shape2[k])
            out[k] = a
        return out, flat[n_small]

    g_all, loss = unpack_all(g_big, g_small)
    d_all, _ = unpack_all(d_big, d_small)
    m_all, _ = unpack_all(m_big, m_small)
    v_all, _ = unpack_all(v_big, v_small)

    def final(tree):
        return [tree[k].reshape(orig_shape[k]) for k in _WEIGHTS]

    return (loss, grad_x[None], *final(g_all), *final(d_all), *final(m_all), *final(v_all))
```

```python
import functools
import math

import numpy as np
import jax
import jax.numpy as jnp
from jax import lax
from jax.experimental import pallas as pl
from jax.experimental.pallas import tpu as pltpu

F32 = jnp.float32
BF16 = jnp.bfloat16
HIGHEST = lax.Precision.HIGHEST

D_MODEL = 1024
N_META = 16
BLOCK = 128
PAD = BLOCK - N_META
EPS = 1e-6
NEG = -1e30
SSD_INNER = 2 * D_MODEL
SSD_HEADDIM = 64
SSD_HEADS = SSD_INNER // SSD_HEADDIM
SSD_GROUPS = 4
SSD_HPG = SSD_HEADS // SSD_GROUPS
SSD_STATE = 128
SSD_CONV = 4
SSD_GW = SSD_HPG * SSD_HEADDIM
SSD_BC = SSD_GROUPS * SSD_STATE
SSD_XBC = SSD_INNER + 2 * SSD_BC
ATT_HEADS = 16
ATT_KV_HEADS = 2
ATT_HEADDIM = 64
ATT_GQ = ATT_HEADS // ATT_KV_HEADS
ATT_Q = ATT_HEADS * ATT_HEADDIM
ATT_KV = ATT_KV_HEADS * ATT_HEADDIM
REL_BUCKETS = 32
REL_MAX_DIST = 128
D_FF = 2816
FFN_CONV = 3
ADAM_LR = 0.001
ADAM_B1 = 0.9
ADAM_B2 = 0.999
ADAM_EPS = 1e-08
ADAM_WD = 0.01
ADAM_STEP = 10

N_DEV = 8
LANES = 128
SUBLANES = 8
DT_W = SSD_GROUPS * LANES
VMEM_LIMIT_BYTES = 56 * 1024 * 1024
MESH = pl.DeviceIdType.MESH

SMALL_ROW_MULT = 16

NT_BAND = BLOCK * 2 * BLOCK
NT_META = BLOCK * BLOCK
NT_ALL = NT_BAND + 3 * NT_META
NT_TILE = 8192


def _cparams(*sem):
    return pltpu.CompilerParams(dimension_semantics=sem, vmem_limit_bytes=VMEM_LIMIT_BYTES)


def _row_tile(n, cap):
    best = None
    for t in range(16, min(n, cap) + 1, 16):
        if n % t == 0:
            best = t
    return best or n


def _col_tile(n, cap):
    for t in (1408, 1280, 1024, 768, 640, 512, 384, 256, 128):
        if t <= cap and n % t == 0:
            return t
    return n


def _silu(x):
    return x * jax.nn.sigmoid(x)


def _dsilu(x):
    s = jax.nn.sigmoid(x)
    return s * (1.0 + x * (1.0 - s))


def _softplus(x):
    return jnp.maximum(x, 0.0) + jnp.log(1.0 + jnp.exp(-jnp.abs(x)))


def _dot_nt(a, b):
    return lax.dot_general(a, b, (((1,), (1,)), ((), ())), preferred_element_type=F32)


def _dot_tn(a, b):
    return lax.dot_general(a, b, (((0,), (0,)), ((), ())), preferred_element_type=F32)


def _dot(a, b):
    return jnp.dot(a, b, preferred_element_type=F32)


def _sum_all(x):
    return jnp.sum(jnp.sum(x, axis=1, keepdims=True), axis=0, keepdims=True)


def _mm(a, b, *, name, ta=False, tb=False, c=None, mask=False, out_dtype=F32):
    if not ta:
        m, k = a.shape
        n = b.shape[0] if tb else b.shape[1]
        tm = _row_tile(m, 832)
        tn = _col_tile(n, 512 if k > 3072 else 1024)

        def body(*refs):
            if c is None:
                a_ref, b_ref, o_ref = refs
            else:
                a_ref, b_ref, c_ref, o_ref = refs
            acc = (_dot_nt if tb else _dot)(a_ref[...].astype(BF16), b_ref[...].astype(BF16))
            if mask:
                row = pl.program_id(0) * tm + lax.broadcasted_iota(jnp.int32, (tm, 1), 0)
                acc = jnp.where(row >= PAD, acc, 0.0)
            if c is not None:
                acc = acc + c_ref[...]
            o_ref[...] = acc.astype(out_dtype)

        b_spec = pl.BlockSpec((tn, k), lambda i, j: (j, 0)) if tb else pl.BlockSpec((k, tn), lambda i, j: (0, j))
        in_specs = [pl.BlockSpec((tm, k), lambda i, j: (i, 0)), b_spec]
        args = [a, b]
        if c is not None:
            in_specs.append(pl.BlockSpec((tm, tn), lambda i, j: (i, j)))
            args.append(c)
        return pl.pallas_call(
            body, name=name, grid=(m // tm, n // tn), in_specs=in_specs,
            out_specs=pl.BlockSpec((tm, tn), lambda i, j: (i, j)),
            out_shape=jax.ShapeDtypeStruct((m, n), out_dtype),
            compiler_params=_cparams("parallel", "parallel"))(*args)

    kc, m = a.shape
    n = b.shape[1]
    tk = _row_tile(kc, 832)
    tm = _col_tile(m, 1408)
    tn = _col_tile(n, 1408)

    def body_t(a_ref, b_ref, o_ref):
        kk = pl.program_id(2)
        bb = b_ref[...]
        if mask:
            row = kk * tk + lax.broadcasted_iota(jnp.int32, (tk, 1), 0)
            bb = jnp.where(row >= PAD, bb, jnp.zeros_like(bb))
        p = _dot_tn(a_ref[...].astype(BF16), bb.astype(BF16))

        @pl.when(kk == 0)
        def _():
            o_ref[...] = p

        @pl.when(kk > 0)
        def _():
            o_ref[...] += p

    return pl.pallas_call(
        body_t, name=name, grid=(m // tm, n // tn, kc // tk),
        in_specs=[pl.BlockSpec((tk, tm), lambda i, j, kk: (kk, i)), pl.BlockSpec((tk, tn), lambda i, j, kk: (kk, j))],
        out_specs=pl.BlockSpec((tm, tn), lambda i, j, kk: (i, j)),
        out_shape=jax.ShapeDtypeStruct((m, n), F32),
        compiler_params=_cparams("parallel", "parallel", "arbitrary"))(a, b)


def _rms_fwd(h, w, *, name):
    n, d = h.shape
    tm = _row_tile(n, 832)

    def body(h_ref, w_ref, o_ref):
        x = h_ref[...]
        r = lax.rsqrt(jnp.mean(x * x, axis=-1, keepdims=True) + EPS)
        o_ref[...] = (x * r * w_ref[...]).astype(BF16)

    return pl.pallas_call(
        body, name=name, grid=(n // tm,),
        in_specs=[pl.BlockSpec((tm, d), lambda i: (i, 0)), pl.BlockSpec((1, d), lambda i: (0, 0))],
        out_specs=pl.BlockSpec((tm, d), lambda i: (i, 0)),
        out_shape=jax.ShapeDtypeStruct((n, d), BF16),
        compiler_params=_cparams("parallel"))(h, w)


def _rms_bwd(x, w, dy, dres, *, name):
    n, d = x.shape
    tm = _row_tile(n, 832)

    def body(x_ref, w_ref, dy_ref, dres_ref, dx_ref, dw_ref):
        i = pl.program_id(0)
        xv = x_ref[...]
        r = lax.rsqrt(jnp.mean(xv * xv, axis=-1, keepdims=True) + EPS)
        xh = xv * r
        dyv = dy_ref[...]
        g = dyv * w_ref[...]
        dx_ref[...] = r * (g - xh * jnp.mean(g * xh, axis=-1, keepdims=True)) + dres_ref[...]
        part = jnp.sum(dyv * xh, axis=0, keepdims=True)

        @pl.when(i == 0)
        def _():
            dw_ref[...] = part

        @pl.when(i > 0)
        def _():
            dw_ref[...] += part

    row = pl.BlockSpec((tm, d), lambda i: (i, 0))
    vec = pl.BlockSpec((1, d), lambda i: (0, 0))
    return pl.pallas_call(
        body, name=name, grid=(n // tm,), in_specs=[row, vec, row, row], out_specs=[row, vec],
        out_shape=[jax.ShapeDtypeStruct((n, d), F32), jax.ShapeDtypeStruct((1, d), F32)],
        compiler_params=_cparams("arbitrary"))(x, w, dy, dres)


def _final_loss(h, w, target):
    n, d = h.shape
    nb = n // BLOCK

    def body(h_ref, w_ref, t_ref, dh_ref, loss_ref, dw_ref):
        i = pl.program_id(0)
        xv = h_ref[...]
        r = lax.rsqrt(jnp.mean(xv * xv, axis=-1, keepdims=True) + EPS)
        xh = xv * r
        wv = w_ref[...]
        err = jnp.where(i >= 1, xh * wv - t_ref[...], 0.0)
        dyv = err * (1.0 / d)
        g = dyv * wv
        dh_ref[...] = r * (g - xh * jnp.mean(g * xh, axis=-1, keepdims=True))
        lpart = jnp.broadcast_to(0.5 * _sum_all(err * err) * (1.0 / d), (1, LANES))
        wpart = jnp.sum(dyv * xh, axis=0, keepdims=True)

        @pl.when(i == 0)
        def _():
            loss_ref[...] = lpart
            dw_ref[...] = wpart

        @pl.when(i > 0)
        def _():
            loss_ref[...] += lpart
            dw_ref[...] += wpart

    row = pl.BlockSpec((BLOCK, d), lambda i: (i, 0))
    vec = pl.BlockSpec((1, d), lambda i: (0, 0))
    return pl.pallas_call(
        body, name="final_loss", grid=(nb,),
        in_specs=[row, vec, pl.BlockSpec((BLOCK, d), lambda i: (jnp.maximum(i - 1, 0), 0))],
        out_specs=[row, pl.BlockSpec((1, LANES), lambda i: (0, 0)), vec],
        out_shape=[jax.ShapeDtypeStruct((n, d), F32), jax.ShapeDtypeStruct((1, LANES), F32),
                   jax.ShapeDtypeStruct((1, d), F32)],
        compiler_params=_cparams("arbitrary"))(h, w, target)


def _main_spec(tm, cb, off=0):
    return pl.BlockSpec((tm, cb), lambda j, i: (i, j + off))


def _prev_spec(tm, cb, off=0):
    r8 = tm // SUBLANES
    return pl.BlockSpec((SUBLANES, cb), lambda j, i: (jnp.maximum(i * r8 - 1, 0), j + off))


def _next_spec(tm, cb, n_rows, off=0):
    r8 = tm // SUBLANES
    last = n_rows // SUBLANES - 1
    return pl.BlockSpec((SUBLANES, cb), lambda j, i: (jnp.minimum((i + 1) * r8, last), j + off))


def _with_prev(prev_ref, main_ref, i):
    prev = jnp.where(i > 0, prev_ref[...], 0.0)
    return jnp.concatenate([prev, main_ref[...]], axis=0)


def _with_next(main, nxt, i, n_tiles):
    return jnp.concatenate([main, jnp.where(i < n_tiles - 1, nxt, 0.0)], axis=0)


def _back(xx, s, tm):
    if s == 0:
        return xx[SUBLANES:SUBLANES + tm]
    return pltpu.roll(xx, s, 0)[SUBLANES:SUBLANES + tm]


def _ahead(xx, s, tm):
    if s == 0:
        return xx[:tm]
    return pltpu.roll(xx, tm + SUBLANES - s, 0)[:tm]


def _conv_fwd(x, w, b, *, name):
    n, cdim = x.shape
    kw = w.shape[0]
    tm = _row_tile(n, 832)
    cb = _col_tile(cdim, 512)

    def body(xp_ref, x_ref, w_ref, b_ref, o_ref):
        xx = _with_prev(xp_ref, x_ref, pl.program_id(1))
        acc = jnp.broadcast_to(b_ref[...], (tm, cb))
        for k in range(kw):
            acc = acc + w_ref[k:k + 1, :] * _back(xx, kw - 1 - k, tm)
        o_ref[...] = acc

    return pl.pallas_call(
        body, name=name, grid=(cdim // cb, n // tm),
        in_specs=[_prev_spec(tm, cb), _main_spec(tm, cb), pl.BlockSpec((kw, cb), lambda j, i: (0, j)),
                  pl.BlockSpec((1, cb), lambda j, i: (0, j))],
        out_specs=_main_spec(tm, cb),
        out_shape=jax.ShapeDtypeStruct((n, cdim), F32),
        compiler_params=_cparams("parallel", "parallel"))(x, x, w, b)


def _conv_bwd_core(dpre_ext, x_ext, w_ref, kw, tm):
    dpre = dpre_ext[:tm]
    dx = None
    dws = []
    for k in range(kw):
        term = w_ref[k:k + 1, :] * _ahead(dpre_ext, kw - 1 - k, tm)
        dx = term if dx is None else dx + term
        dws.append(jnp.sum(dpre * _back(x_ext, kw - 1 - k, tm), axis=0, keepdims=True))
    return dx, dws, jnp.sum(dpre, axis=0, keepdims=True)


def _acc_rows(i, dw_ref, db_ref, dws, db):
    @pl.when(i == 0)
    def _():
        for k, v in enumerate(dws):
            dw_ref[k:k + 1, :] = v
        db_ref[...] = db

    @pl.when(i > 0)
    def _():
        for k, v in enumerate(dws):
            dw_ref[k:k + 1, :] += v
        db_ref[...] += db


def _conv_bwd(dpre, x, w, *, name):
    n, cdim = x.shape
    kw = w.shape[0]
    tm = _row_tile(n, 832)
    cb = _col_tile(cdim, 512)
    nt = n // tm

    def body(d_ref, dn_ref, xp_ref, x_ref, w_ref, dx_ref, dw_ref, db_ref):
        i = pl.program_id(1)
        dpre_ext = _with_next(d_ref[...], dn_ref[...], i, nt)
        x_ext = _with_prev(xp_ref, x_ref, i)
        dx, dws, db = _conv_bwd_core(dpre_ext, x_ext, w_ref, kw, tm)
        dx_ref[...] = dx.astype(BF16)
        _acc_rows(i, dw_ref, db_ref, dws, db)

    wspec = pl.BlockSpec((kw, cb), lambda j, i: (0, j))
    bspec = pl.BlockSpec((1, cb), lambda j, i: (0, j))
    return pl.pallas_call(
        body, name=name, grid=(cdim // cb, nt),
        in_specs=[_main_spec(tm, cb), _next_spec(tm, cb, n), _prev_spec(tm, cb), _main_spec(tm, cb), wspec],
        out_specs=[_main_spec(tm, cb), wspec, bspec],
        out_shape=[jax.ShapeDtypeStruct((n, cdim), BF16), jax.ShapeDtypeStruct((kw, cdim), F32),
                   jax.ShapeDtypeStruct((1, cdim), F32)],
        compiler_params=_cparams("parallel", "arbitrary"))(dpre, dpre, x, x, w)


def _ffn_act_fwd(x, w, b):
    n = x.shape[0]
    kw = w.shape[0]
    tm = _row_tile(n, 832)
    cb = _col_tile(D_FF, 256)
    nc = D_FF // cb

    def body(xpu_ref, xu_ref, xpg_ref, xg_ref, wu_ref, wg_ref, bu_ref, bg_ref, hu_ref, hg_ref, act_ref):
        i = pl.program_id(1)
        outs = []
        for xp_ref, x_ref, w_ref, b_ref in ((xpu_ref, xu_ref, wu_ref, bu_ref), (xpg_ref, xg_ref, wg_ref, bg_ref)):
            xx = _with_prev(xp_ref, x_ref, i)
            acc = jnp.broadcast_to(b_ref[...], (tm, cb))
            for k in range(kw):
                acc = acc + w_ref[k:k + 1, :] * _back(xx, kw - 1 - k, tm)
            outs.append(acc)
        hu_ref[...] = outs[0]
        hg_ref[...] = outs[1]
        act_ref[...] = (_silu(outs[1]) * outs[0]).astype(BF16)

    def wspec(off):
        return pl.BlockSpec((kw, cb), lambda j, i: (0, j + off))

    def bspec(off):
        return pl.BlockSpec((1, cb), lambda j, i: (0, j + off))

    out = _main_spec(tm, cb)
    return pl.pallas_call(
        body, name="ffn_act_fwd", grid=(nc, n // tm),
        in_specs=[_prev_spec(tm, cb), _main_spec(tm, cb), _prev_spec(tm, cb, nc), _main_spec(tm, cb, nc),
                  wspec(0), wspec(nc), bspec(0), bspec(nc)],
        out_specs=[out, out, out],
        out_shape=[jax.ShapeDtypeStruct((n, D_FF), F32), jax.ShapeDtypeStruct((n, D_FF), F32),
                   jax.ShapeDtypeStruct((n, D_FF), BF16)],
        compiler_params=_cparams("parallel", "parallel"))(x, x, x, x, w, w, b, b)


def _ffn_act_bwd(dact, hu, hg, x, w):
    n = x.shape[0]
    kw = w.shape[0]
    tm = _row_tile(n, 832)
    cb = _col_tile(D_FF, 256)
    nc = D_FF // cb
    nt = n // tm

    def body(d_ref, dn_ref, hu_ref, hun_ref, hg_ref, hgn_ref, xpu_ref, xu_ref, xpg_ref, xg_ref, wu_ref, wg_ref,
             dxu_ref, dxg_ref, dwu_ref, dwg_ref, dbu_ref, dbg_ref):
        i = pl.program_id(1)
        dact_e = _with_next(d_ref[...], dn_ref[...], i, nt)
        up_e = _with_next(hu_ref[...], hun_ref[...], i, nt)
        gate_e = _with_next(hg_ref[...], hgn_ref[...], i, nt)
        dup_e = dact_e * _silu(gate_e)
        dgate_e = dact_e * up_e * _dsilu(gate_e)
        dx, dws, db = _conv_bwd_core(dup_e, _with_prev(xpu_ref, xu_ref, i), wu_ref, kw, tm)
        dxu_ref[...] = dx.astype(BF16)
        _acc_rows(i, dwu_ref, dbu_ref, dws, db)
        dx, dws, db = _conv_bwd_core(dgate_e, _with_prev(xpg_ref, xg_ref, i), wg_ref, kw, tm)
        dxg_ref[...] = dx.astype(BF16)
        _acc_rows(i, dwg_ref, dbg_ref, dws, db)

    main, nxt = _main_spec(tm, cb), _next_spec(tm, cb, n)
    wspec0 = pl.BlockSpec((kw, cb), lambda j, i: (0, j))
    wspec1 = pl.BlockSpec((kw, cb), lambda j, i: (0, j + nc))
    bspec = pl.BlockSpec((1, cb), lambda j, i: (0, j))
    return pl.pallas_call(
        body, name="ffn_act_bwd", grid=(nc, nt),
        in_specs=[main, nxt, main, nxt, main, nxt,
                  _prev_spec(tm, cb), _main_spec(tm, cb), _prev_spec(tm, cb, nc), _main_spec(tm, cb, nc),
                  wspec0, wspec1],
        out_specs=[main, main, wspec0, wspec0, bspec, bspec],
        out_shape=[jax.ShapeDtypeStruct((n, D_FF), BF16), jax.ShapeDtypeStruct((n, D_FF), BF16),
                   jax.ShapeDtypeStruct((kw, D_FF), F32), jax.ShapeDtypeStruct((kw, D_FF), F32),
                   jax.ShapeDtypeStruct((1, D_FF), F32), jax.ShapeDtypeStruct((1, D_FF), F32)],
        compiler_params=_cparams("parallel", "arbitrary"))(dact, dact, hu, hu, hg, hg, x, x, x, x, w, w)


def _ssd_prep(pxs_ref, pb_ref, pc_ref, dtr_ref, dtb_ref, alog_ref, c):
    xs = _silu(pxs_ref[...])
    bm = _silu(pb_ref[...])
    cm = _silu(pc_ref[...])
    row = lax.broadcasted_iota(jnp.int32, (BLOCK, 1), 0) + c * BLOCK
    valid = (row >= PAD).astype(F32)
    dtr = dtr_ref[...] + dtb_ref[...]
    dt = _softplus(dtr) * valid
    a = -jnp.exp(alog_ref[...])
    lam = dt * a
    ri = lax.broadcasted_iota(jnp.int32, (BLOCK, BLOCK), 0)
    ci = lax.broadcasted_iota(jnp.int32, (BLOCK, BLOCK), 1)
    causal = ci <= ri
    cs = jnp.dot(causal.astype(F32), lam, precision=HIGHEST, preferred_element_type=F32)
    return xs, bm, cm, valid, dtr, dt, a, lam, cs, causal


def _head_cols(r):
    return slice(SSD_HEADDIM * r, SSD_HEADDIM * (r + 1))


def _ssd_specs(nc, rev):
    def cidx(c):
        return nc - 1 - c if rev else c

    xs = pl.BlockSpec((BLOCK, SSD_GW), lambda g, c: (cidx(c), g))
    bspec = pl.BlockSpec((BLOCK, SSD_STATE), lambda g, c: (cidx(c), SSD_INNER // SSD_STATE + g))
    cspec = pl.BlockSpec((BLOCK, SSD_STATE), lambda g, c: (cidx(c), (SSD_INNER + SSD_BC) // SSD_STATE + g))
    lane = pl.BlockSpec((BLOCK, LANES), lambda g, c: (cidx(c), g))
    vec = pl.BlockSpec((1, LANES), lambda g, c: (0, g))
    wide_vec = pl.BlockSpec((1, SSD_GW), lambda g, c: (0, g))
    hsave = pl.BlockSpec((1, 1, SSD_GW, SSD_STATE), lambda g, c: (cidx(c), g, 0, 0))
    return xs, bspec, cspec, lane, vec, wide_vec, hsave


def _ssd_fwd(pre, dt_raw, z, dtb, alog, dskip, norm_w):
    n = pre.shape[0]
    nc = n // BLOCK
    xs_s, b_s, c_s, lane_s, vec_s, wide_s, hs_s = _ssd_specs(nc, False)

    def body(pxs_ref, pb_ref, pc_ref, dtr_ref, z_ref, dtb_ref, alog_ref, dsk_ref, nw_ref,
             y_ref, yn_ref, hs_ref, h_scr):
        c = pl.program_id(1)

        @pl.when(c == 0)
        def _():
            h_scr[...] = jnp.zeros_like(h_scr)

        xs, bm, cm, _, _, dt, _, _, cs, causal = _ssd_prep(pxs_ref, pb_ref, pc_ref, dtr_ref, dtb_ref, alog_ref, c)
        cst = cs.T
        cs_last = cs[BLOCK - 1:BLOCK, :]
        bmb = bm.astype(BF16)
        cmb = cm.astype(BF16)
        cb = _dot_nt(cmb, bmb)
        hg = h_scr[...]
        hs_ref[0, 0] = hg
        yoff = _dot_nt(cmb, hg.astype(BF16))
        ys, xds = [], []
        for r in range(SSD_HPG):
            csc = cs[:, r:r + 1]
            lm = jnp.exp(jnp.where(causal, csc - cst[r:r + 1, :], NEG))
            x_r = xs[:, _head_cols(r)]
            xdt = x_r * dt[:, r:r + 1]
            ys.append(_dot((cb * lm).astype(BF16), xdt.astype(BF16)) + yoff[:, _head_cols(r)] * jnp.exp(csc)
                      + dsk_ref[0:1, r:r + 1] * x_r)
            xds.append(xdt * jnp.exp(cs_last[:, r:r + 1] - csc))
        y = jnp.concatenate(ys, axis=1)
        st = _dot_tn(jnp.concatenate(xds, axis=1).astype(BF16), bmb)
        for r in range(SSD_HPG):
            rows = _head_cols(r)
            h_scr[rows, :] = hg[rows, :] * jnp.exp(cs_last[:, r:r + 1]) + st[rows, :]
        y_ref[...] = y
        gts = y * _silu(z_ref[...])
        rr = lax.rsqrt(jnp.mean(gts * gts, axis=-1, keepdims=True) + EPS)
        yn_ref[...] = (gts * rr * nw_ref[...]).astype(BF16)

    return pl.pallas_call(
        body, name="ssd_fwd", grid=(SSD_GROUPS, nc),
        in_specs=[xs_s, b_s, c_s, lane_s, xs_s, vec_s, vec_s, vec_s, wide_s],
        out_specs=[xs_s, xs_s, hs_s],
        out_shape=[jax.ShapeDtypeStruct((n, SSD_INNER), F32), jax.ShapeDtypeStruct((n, SSD_INNER), BF16),
                   jax.ShapeDtypeStruct((nc, SSD_GROUPS, SSD_GW, SSD_STATE), F32)],
        scratch_shapes=[pltpu.VMEM((SSD_GW, SSD_STATE), F32)],
        compiler_params=_cparams("parallel", "arbitrary"))(pre, pre, pre, dt_raw, z, dtb, alog, dskip, norm_w)


def _lane_put(acc, col, r):
    lane = lax.broadcasted_iota(jnp.int32, acc.shape, 1)
    return jnp.where(lane == r, col, acc)


def _ssd_bwd(dyn, y, z, pre, dt_raw, hsave, dtb, alog, dskip, norm_w):
    n = pre.shape[0]
    nc = n // BLOCK
    xs_s, b_s, c_s, lane_s, vec_s, wide_s, hs_s = _ssd_specs(nc, True)
    bc_out = pl.BlockSpec((BLOCK, SSD_STATE), lambda g, c: (nc - 1 - c, g))

    def body(dyn_ref, y_ref, z_ref, pxs_ref, pb_ref, pc_ref, dtr_ref, hs_ref, dtb_ref, alog_ref, dsk_ref, nw_ref,
             dz_ref, dxs_ref, dbm_ref, dcm_ref, ddt_ref, dnw_ref, ddtb_ref, dalog_ref, ddsk_ref, g_scr):
        step = pl.program_id(1)
        c = nc - 1 - step

        @pl.when(step == 0)
        def _():
            g_scr[...] = jnp.zeros_like(g_scr)

        xs, bm, cm, valid, dtr, dt, a, lam, cs, causal = _ssd_prep(
            pxs_ref, pb_ref, pc_ref, dtr_ref, dtb_ref, alog_ref, c)
        cst = cs.T
        cs_last = cs[BLOCK - 1:BLOCK, :]
        bmb = bm.astype(BF16)
        cmb = cm.astype(BF16)
        cb = _dot_nt(cmb, bmb)
        hg = hs_ref[0, 0]
        hgb = hg.astype(BF16)
        yoff = _dot_nt(cmb, hgb)
        gn = g_scr[...]
        gnb = gn.astype(BF16)

        zv = z_ref[...]
        yv = y_ref[...]
        sz = _silu(zv)
        gts = yv * sz
        rr = lax.rsqrt(jnp.mean(gts * gts, axis=-1, keepdims=True) + EPS)
        xh = gts * rr
        dynv = dyn_ref[...]
        gg = dynv * nw_ref[...]
        dgts = rr * (gg - xh * jnp.mean(gg * xh, axis=-1, keepdims=True))
        dnw = jnp.sum(dynv * xh, axis=0, keepdims=True)
        dy = dgts * sz
        dz_ref[...] = (dgts * yv * _dsilu(zv)).astype(BF16)

        q_all = _dot_nt(bmb, gnb)
        zero_l = jnp.zeros((BLOCK, LANES), F32)
        dcs_col, ddt_x = zero_l, zero_l
        dcs_row = jnp.zeros((SUBLANES, BLOCK), F32)
        dcs_last = jnp.zeros((1, LANES), F32)
        ddsk = jnp.zeros((1, LANES), F32)
        dcb = jnp.zeros((BLOCK, BLOCK), F32)
        ws, xds, dxdts, decs = [], [], [], []
        for r in range(SSD_HPG):
            cols = _head_cols(r)
            csc = cs[:, r:r + 1]
            ecs = jnp.exp(csc)
            lm = jnp.exp(jnp.where(causal, csc - cst[r:r + 1, :], NEG))
            x_r = xs[:, cols]
            xdt = x_r * dt[:, r:r + 1]
            dy_r = dy[:, cols]
            dyb = dy_r.astype(BF16)
            dec = jnp.exp(cs_last[:, r:r + 1] - csc)
            ws.append(dy_r * ecs)
            col = jnp.sum(dy_r * yoff[:, cols], axis=1, keepdims=True) * ecs
            q_r = q_all[:, cols]
            e_r = jnp.sum(q_r * xdt, axis=1, keepdims=True) * dec
            col = col - e_r
            last = jnp.sum(e_r, axis=0, keepdims=True)
            rows = cols
            eh = jnp.exp(cs_last[:, r:r + 1])
            last = last + eh * _sum_all(gn[rows, :] * hg[rows, :])
            gm = _dot_nt(dyb, xdt.astype(BF16)) * lm
            dcb = dcb + gm
            mm_ = gm * cb
            col = col + jnp.sum(mm_, axis=1, keepdims=True)
            rowv = jnp.sum(mm_, axis=0, keepdims=True)
            sub = lax.broadcasted_iota(jnp.int32, (SUBLANES, BLOCK), 0)
            dcs_row = jnp.where(sub == r, rowv, dcs_row)
            dxdt = _dot_tn((cb * lm).astype(BF16), dyb) + q_r * dec
            dcs_col = _lane_put(dcs_col, col, r)
            dcs_last = _lane_put(dcs_last, last, r)
            ddt_x = _lane_put(ddt_x, jnp.sum(dxdt * x_r, axis=1, keepdims=True), r)
            ddsk = _lane_put(ddsk, _sum_all(dy_r * x_r), r)
            xds.append(xdt * dec)
            dxdts.append(dxdt * dt[:, r:r + 1] + dsk_ref[0:1, r:r + 1] * dy_r)
            decs.append(eh)
        w_all = jnp.concatenate(ws, axis=1).astype(BF16)
        xd_all = jnp.concatenate(xds, axis=1).astype(BF16)
        dcbb = dcb.astype(BF16)
        dcm = _dot(w_all, hgb) + _dot(dcbb, bmb)
        dbm = _dot(xd_all, gnb) + _dot_tn(dcbb, cmb)
        dh_off = _dot_tn(w_all, cmb)
        for r in range(SSD_HPG):
            rows = _head_cols(r)
            g_scr[rows, :] = gn[rows, :] * decs[r] + dh_off[rows, :]

        pad_rows = jnp.zeros((BLOCK - SUBLANES, BLOCK), F32)
        dcs = dcs_col - jnp.concatenate([dcs_row, pad_rows], axis=0).T
        rsel = lax.broadcasted_iota(jnp.int32, (BLOCK, LANES), 0)
        dcs = dcs + jnp.where(rsel == BLOCK - 1, dcs_last, 0.0)
        ri = lax.broadcasted_iota(jnp.int32, (BLOCK, BLOCK), 0)
        ci = lax.broadcasted_iota(jnp.int32, (BLOCK, BLOCK), 1)
        dlam = jnp.dot((ci >= ri).astype(F32), dcs, precision=HIGHEST, preferred_element_type=F32)
        lane = lax.broadcasted_iota(jnp.int32, (BLOCK, LANES), 1)
        head = lane < SSD_HPG
        ddt = dlam * a + ddt_x
        ddtr = jnp.where(head, ddt * jax.nn.sigmoid(dtr) * valid, 0.0)
        ddt_ref[...] = ddtr.astype(BF16)
        dalog = jnp.sum(jnp.where(head, dlam * lam, 0.0), axis=0, keepdims=True)
        ddtb = jnp.sum(ddtr, axis=0, keepdims=True)

        dxs_ref[...] = jnp.concatenate(dxdts, axis=1) * _dsilu(pxs_ref[...])
        dbm_ref[...] = dbm * _dsilu(pb_ref[...])
        dcm_ref[...] = dcm * _dsilu(pc_ref[...])

        @pl.when(step == 0)
        def _():
            dnw_ref[...] = dnw
            ddtb_ref[...] = ddtb
            dalog_ref[...] = dalog
            ddsk_ref[...] = ddsk

        @pl.when(step > 0)
        def _():
            dnw_ref[...] += dnw
            ddtb_ref[...] += ddtb
            dalog_ref[...] += dalog
            ddsk_ref[...] += ddsk

    return pl.pallas_call(
        body, name="ssd_bwd", grid=(SSD_GROUPS, nc),
        in_specs=[xs_s, xs_s, xs_s, xs_s, b_s, c_s, lane_s, hs_s, vec_s, vec_s, vec_s, wide_s],
        out_specs=[xs_s, xs_s, bc_out, bc_out, lane_s, wide_s, vec_s, vec_s, vec_s],
        out_shape=[jax.ShapeDtypeStruct((n, SSD_INNER), BF16), jax.ShapeDtypeStruct((n, SSD_INNER), F32),
                   jax.ShapeDtypeStruct((n, SSD_BC), F32), jax.ShapeDtypeStruct((n, SSD_BC), F32),
                   jax.ShapeDtypeStruct((n, DT_W), BF16), jax.ShapeDtypeStruct((1, SSD_INNER), F32),
                   jax.ShapeDtypeStruct((1, DT_W), F32), jax.ShapeDtypeStruct((1, DT_W), F32),
                   jax.ShapeDtypeStruct((1, DT_W), F32)],
        scratch_shapes=[pltpu.VMEM((SSD_GW, SSD_STATE), F32)],
        compiler_params=_cparams("parallel", "arbitrary"))(
            dyn, y, z, pre, pre, pre, dt_raw, hsave, dtb, alog, dskip, norm_w)


def _bucket_table():
    def bucket(dist):
        d = np.maximum(dist, 0)
        half = REL_BUCKETS // 2
        big = half + (np.log(np.maximum(d, half).astype(np.float32) / np.float32(half))
                      / np.float32(math.log(REL_MAX_DIST / half)) * np.float32(REL_BUCKETS - half)).astype(np.int32)
        return np.where(d < half, d, np.minimum(big, REL_BUCKETS - 1)).astype(np.int32)

    l = np.arange(BLOCK)[:, None]
    band = bucket(l + BLOCK - np.arange(2 * BLOCK)[None, :])
    j = np.arange(BLOCK)[None, :]
    metas = [bucket(l - j), bucket(BLOCK + l - j), bucket(2 * BLOCK + l - j)]
    return np.concatenate([band.reshape(-1)] + [m.reshape(-1) for m in metas])


def _onehot_t():
    buckets = jnp.asarray(_bucket_table())
    return (buckets[None, :] == jnp.arange(REL_BUCKETS, dtype=jnp.int32)[:, None]).astype(F32)


def _bias_tables(rel_t, onehot_t):
    def body(r_ref, oh_ref, o_ref):
        o_ref[...] = jnp.dot(r_ref[...], oh_ref[...], precision=HIGHEST, preferred_element_type=F32)

    return pl.pallas_call(
        body, name="bias_tables", grid=(NT_ALL // NT_TILE,),
        in_specs=[pl.BlockSpec((ATT_HEADS, REL_BUCKETS), lambda i: (0, 0)),
                  pl.BlockSpec((REL_BUCKETS, NT_TILE), lambda i: (0, i))],
        out_specs=pl.BlockSpec((ATT_HEADS, NT_TILE), lambda i: (0, i)),
        out_shape=jax.ShapeDtypeStruct((ATT_HEADS, NT_ALL), F32),
        compiler_params=_cparams("parallel"))(rel_t, onehot_t)


def _bias_grad(dtab, onehot_t):
    def body(d_ref, oh_ref, o_ref):
        i = pl.program_id(0)
        p = lax.dot_general(d_ref[...], oh_ref[...], (((1,), (1,)), ((), ())), precision=HIGHEST,
                            preferred_element_type=F32)

        @pl.when(i == 0)
        def _():
            o_ref[...] = p

        @pl.when(i > 0)
        def _():
            o_ref[...] += p

    return pl.pallas_call(
        body, name="bias_grad", grid=(NT_ALL // NT_TILE,),
        in_specs=[pl.BlockSpec((ATT_HEADS, NT_TILE), lambda i: (0, i)),
                  pl.BlockSpec((REL_BUCKETS, NT_TILE), lambda i: (0, i))],
        out_specs=pl.BlockSpec((ATT_HEADS, REL_BUCKETS), lambda i: (0, 0)),
        out_shape=jax.ShapeDtypeStruct((ATT_HEADS, REL_BUCKETS), F32),
        compiler_params=_cparams("arbitrary"))(dtab, onehot_t)


def _att_masks(n):
    far = 4 * BLOCK
    li = lax.broadcasted_iota(jnp.int32, (BLOCK, BLOCK), 0)
    ki = lax.broadcasted_iota(jnp.int32, (BLOCK, BLOCK), 1)
    m_meta = (ki >= PAD) & (li + jnp.where(n >= 1, far, 0) >= ki)
    li2 = lax.broadcasted_iota(jnp.int32, (BLOCK, 2 * BLOCK), 0)
    ki2 = lax.broadcasted_iota(jnp.int32, (BLOCK, 2 * BLOCK), 1)
    prev_ok = (ki2 < BLOCK) & (ki2 > li2 + jnp.where(n >= 2, 0, far))
    cur_ok = (ki2 >= BLOCK) & (ki2 - BLOCK <= li2 - jnp.where(n >= 1, 0, far))
    return m_meta, prev_ok | cur_ok


def _att_probs(qh, k_meta, k_band, b_meta, b_band, m_meta, m_band, sink):
    scale = ATT_HEADDIM ** -0.5
    s_m = jnp.where(m_meta, _dot_nt(qh, k_meta) * scale + b_meta, NEG)
    s_b = jnp.where(m_band, _dot_nt(qh, k_band) * scale + b_band, NEG)
    mx = jnp.maximum(jnp.maximum(jnp.max(s_m, axis=1, keepdims=True), jnp.max(s_b, axis=1, keepdims=True)), sink)
    p_m = jnp.exp(s_m - mx)
    p_b = jnp.exp(s_b - mx)
    p_s = jnp.exp(sink - mx)
    inv = 1.0 / (jnp.sum(p_m, axis=1, keepdims=True) + jnp.sum(p_b, axis=1, keepdims=True) + p_s)
    return p_m * inv, p_b * inv, p_s * inv


def _kv_cols(kind, kh):
    base = ATT_KV * kind + ATT_HEADDIM * kh
    return slice(base, base + ATT_HEADDIM)


def _att_specs(nb, rev):
    def nidx(i):
        return nb - 1 - i if rev else i

    kvb = ATT_Q // (2 * ATT_KV)
    q_s = pl.BlockSpec((BLOCK, ATT_Q), lambda i: (nidx(i), 0))
    cur = pl.BlockSpec((BLOCK, 2 * ATT_KV), lambda i: (nidx(i), kvb))
    prev = pl.BlockSpec((BLOCK, 2 * ATT_KV), lambda i: (jnp.maximum(nidx(i) - 1, 0), kvb))
    meta = pl.BlockSpec((BLOCK, 2 * ATT_KV), lambda i: (0, kvb))
    tmeta = pl.BlockSpec((1, ATT_HEADS, BLOCK, BLOCK), lambda i: (jnp.minimum(nidx(i), 2), 0, 0, 0))
    tband = pl.BlockSpec((ATT_HEADS, BLOCK, 2 * BLOCK), lambda i: (0, 0, 0))
    sink = pl.BlockSpec((1, LANES), lambda i: (0, 0))
    return q_s, cur, prev, meta, tmeta, tband, sink


def _attn_fwd(qkv, t_meta, t_band, sinks):
    n = qkv.shape[0]
    nb = n // BLOCK
    q_s, cur_s, prev_s, meta_s, tm_s, tb_s, sink_s = _att_specs(nb, False)

    def body(q_ref, cur_ref, prev_ref, meta_ref, tm_ref, tb_ref, sink_ref, o_ref):
        blk = pl.program_id(0)
        m_meta, m_band = _att_masks(blk)
        kv_band = jnp.concatenate([prev_ref[...], cur_ref[...]], axis=0).astype(BF16)
        kv_meta = meta_ref[...].astype(BF16)
        for kh in range(ATT_KV_HEADS):
            k_meta, v_meta = kv_meta[:, _kv_cols(0, kh)], kv_meta[:, _kv_cols(1, kh)]
            k_band, v_band = kv_band[:, _kv_cols(0, kh)], kv_band[:, _kv_cols(1, kh)]
            for gq in range(ATT_GQ):
                h = kh * ATT_GQ + gq
                cols = slice(ATT_HEADDIM * h, ATT_HEADDIM * (h + 1))
                qh = q_ref[:, cols].astype(BF16)
                p_m, p_b, _ = _att_probs(qh, k_meta, k_band, tm_ref[0, h], tb_ref[h], m_meta, m_band,
                                         sink_ref[0:1, h:h + 1])
                o_ref[:, cols] = (_dot(p_m.astype(BF16), v_meta) + _dot(p_b.astype(BF16), v_band)).astype(BF16)

    return pl.pallas_call(
        body, name="attn_fwd", grid=(nb,),
        in_specs=[q_s, cur_s, prev_s, meta_s, tm_s, tb_s, sink_s],
        out_specs=q_s,
        out_shape=jax.ShapeDtypeStruct((n, ATT_Q), BF16),
        compiler_params=_cparams("parallel"))(qkv, qkv, qkv, qkv, t_meta, t_band, sinks)


def _attn_bwd(datt, qkv, t_meta, t_band, sinks):
    n = qkv.shape[0]
    nb = n // BLOCK
    q_s, cur_s, prev_s, meta_s, tm_s, tb_s, sink_s = _att_specs(nb, True)
    dqkv_s = pl.BlockSpec((BLOCK, ATT_Q + 2 * ATT_KV), lambda i: (nb - 1 - i, 0))
    scale = ATT_HEADDIM ** -0.5

    def body(do_ref, q_ref, cur_ref, prev_ref, meta_ref, tm_ref, tb_ref, sink_ref,
             dqkv_ref, dtm_ref, dtb_ref, dsink_ref, carry_scr, meta_scr):
        step = pl.program_id(0)
        blk = nb - 1 - step
        m_meta, m_band = _att_masks(blk)
        kv_band = jnp.concatenate([prev_ref[...], cur_ref[...]], axis=0).astype(BF16)
        kv_meta = meta_ref[...].astype(BF16)

        @pl.when(step == 0)
        def _():
            carry_scr[...] = jnp.zeros_like(carry_scr)
            meta_scr[...] = jnp.zeros_like(meta_scr)
            dtb_ref[...] = jnp.zeros_like(dtb_ref)
            dsink_ref[...] = jnp.zeros_like(dsink_ref)

        @pl.when((step == 0) | (blk <= 1))
        def _():
            dtm_ref[...] = jnp.zeros_like(dtm_ref)

        dsink = jnp.zeros((1, LANES), F32)
        dkv_band = [None] * (2 * ATT_KV_HEADS)
        dkv_meta = [None] * (2 * ATT_KV_HEADS)
        for kh in range(ATT_KV_HEADS):
            k_meta, v_meta = kv_meta[:, _kv_cols(0, kh)], kv_meta[:, _kv_cols(1, kh)]
            k_band, v_band = kv_band[:, _kv_cols(0, kh)], kv_band[:, _kv_cols(1, kh)]
            dk_m = dv_m = dk_b = dv_b = None
            for gq in range(ATT_GQ):
                h = kh * ATT_GQ + gq
                cols = slice(ATT_HEADDIM * h, ATT_HEADDIM * (h + 1))
                qh = q_ref[:, cols].astype(BF16)
                doh = do_ref[:, cols].astype(BF16)
                p_m, p_b, p_s = _att_probs(qh, k_meta, k_band, tm_ref[0, h], tb_ref[h], m_meta, m_band,
                                           sink_ref[0:1, h:h + 1])
                dp_m = _dot_nt(doh, v_meta)
                dp_b = _dot_nt(doh, v_band)
                delta = jnp.sum(p_m * dp_m, axis=1, keepdims=True) + jnp.sum(p_b * dp_b, axis=1, keepdims=True)
                ds_m = p_m * (dp_m - delta)
                ds_b = p_b * (dp_b - delta)
                dsink = _lane_put(dsink, dsink[0:1, h:h + 1] - jnp.sum(p_s * delta, axis=0, keepdims=True), h)
                dtm_ref[0, h] += ds_m
                dtb_ref[h] += ds_b
                ds_mb, ds_bb = ds_m.astype(BF16), ds_b.astype(BF16)
                dqkv_ref[:, cols] = ((_dot(ds_mb, k_meta) + _dot(ds_bb, k_band)) * scale).astype(BF16)
                parts = (_dot_tn(ds_mb, qh) * scale, _dot_tn(p_m.astype(BF16), doh),
                         _dot_tn(ds_bb, qh) * scale, _dot_tn(p_b.astype(BF16), doh))
                if gq == 0:
                    dk_m, dv_m, dk_b, dv_b = parts
                else:
                    dk_m, dv_m, dk_b, dv_b = dk_m + parts[0], dv_m + parts[1], dk_b + parts[2], dv_b + parts[3]
            dkv_meta[kh], dkv_meta[ATT_KV_HEADS + kh] = dk_m, dv_m
            dkv_band[kh], dkv_band[ATT_KV_HEADS + kh] = dk_b, dv_b
        dsink_ref[...] += dsink
        band = jnp.concatenate(dkv_band, axis=1)
        meta_scr[...] += jnp.concatenate(dkv_meta, axis=1)
        own = band[BLOCK:, :] + carry_scr[...]
        carry_scr[...] = band[:BLOCK, :]

        @pl.when(blk > 0)
        def _():
            dqkv_ref[:, ATT_Q:] = own.astype(BF16)

        @pl.when(blk == 0)
        def _():
            dqkv_ref[:, ATT_Q:] = (own + meta_scr[...]).astype(BF16)

    return pl.pallas_call(
        body, name="attn_bwd", grid=(nb,),
        in_specs=[q_s, q_s, cur_s, prev_s, meta_s, tm_s, tb_s, sink_s],
        out_specs=[dqkv_s, tm_s, tb_s, sink_s],
        out_shape=[jax.ShapeDtypeStruct((n, ATT_Q + 2 * ATT_KV), BF16),
                   jax.ShapeDtypeStruct((3, ATT_HEADS, BLOCK, BLOCK), F32),
                   jax.ShapeDtypeStruct((ATT_HEADS, BLOCK, 2 * BLOCK), F32),
                   jax.ShapeDtypeStruct((1, LANES), F32)],
        scratch_shapes=[pltpu.VMEM((BLOCK, 2 * ATT_KV), F32), pltpu.VMEM((BLOCK, 2 * ATT_KV), F32)],
        compiler_params=_cparams("arbitrary"))(datt, qkv, qkv, qkv, qkv, t_meta, t_band, sinks)


def _merge_fwd(gates, y_ssd, y_att, gate_b):
    n = gates.shape[0]
    tm = _row_tile(n, 832)

    def body(gs_ref, ga_ref, ys_ref, ya_ref, gb_ref, o_ref):
        o_ref[...] = (jax.nn.sigmoid(gs_ref[...] + gb_ref[0:1, :]) * ys_ref[...]
                      + jax.nn.sigmoid(ga_ref[...] + gb_ref[1:2, :]) * ya_ref[...]).astype(BF16)

    row = pl.BlockSpec((tm, D_MODEL), lambda i: (i, 0))
    return pl.pallas_call(
        body, name="merge_fwd", grid=(n // tm,),
        in_specs=[row, pl.BlockSpec((tm, D_MODEL), lambda i: (i, 1)), row, row,
                  pl.BlockSpec((2, D_MODEL), lambda i: (0, 0))],
        out_specs=row, out_shape=jax.ShapeDtypeStruct((n, D_MODEL), BF16),
        compiler_params=_cparams("parallel"))(gates, gates, y_ssd, y_att, gate_b)


def _merge_bwd(dm, gates, y_ssd, y_att, gate_b):
    n = gates.shape[0]
    tm = _row_tile(n, 832)

    def body(dm_ref, gs_ref, ga_ref, ys_ref, ya_ref, gb_ref, dys_ref, dya_ref, dg_ref, dgb_ref):
        i = pl.program_id(0)
        dmv = dm_ref[...]
        ss = jax.nn.sigmoid(gs_ref[...] + gb_ref[0:1, :])
        sa = jax.nn.sigmoid(ga_ref[...] + gb_ref[1:2, :])
        dys_ref[...] = (dmv * ss).astype(BF16)
        dya_ref[...] = (dmv * sa).astype(BF16)
        dgs = dmv * ys_ref[...] * ss * (1.0 - ss)
        dga = dmv * ya_ref[...] * sa * (1.0 - sa)
        dg_ref[:, :D_MODEL] = dgs.astype(BF16)
        dg_ref[:, D_MODEL:] = dga.astype(BF16)
        part = jnp.concatenate([jnp.sum(dgs, axis=0, keepdims=True), jnp.sum(dga, axis=0, keepdims=True)], axis=0)

        @pl.when(i == 0)
        def _():
            dgb_ref[...] = part

        @pl.when(i > 0)
        def _():
            dgb_ref[...] += part

    row = pl.BlockSpec((tm, D_MODEL), lambda i: (i, 0))
    gb = pl.BlockSpec((2, D_MODEL), lambda i: (0, 0))
    return pl.pallas_call(
        body, name="merge_bwd", grid=(n // tm,),
        in_specs=[row, row, pl.BlockSpec((tm, D_MODEL), lambda i: (i, 1)), row, row, gb],
        out_specs=[row, row, pl.BlockSpec((tm, 2 * D_MODEL), lambda i: (i, 0)), gb],
        out_shape=[jax.ShapeDtypeStruct((n, D_MODEL), BF16), jax.ShapeDtypeStruct((n, D_MODEL), BF16),
                   jax.ShapeDtypeStruct((n, 2 * D_MODEL), BF16), jax.ShapeDtypeStruct((2, D_MODEL), F32)],
        compiler_params=_cparams("arbitrary"))(dm, gates, gates, y_ssd, y_att, gate_b)


def _col_move(srcs, outs, pieces, *, name):
    rows = srcs[0].shape[-2]
    tr = _row_tile(rows, 128)
    n_src = len(srcs)
    covered = [sum(p[6] for p in pieces if p[0] == o) for o in range(len(outs))]
    total = [int(np.prod(shp)) // rows for shp, _ in outs]

    def body(*refs):
        in_refs, out_refs = refs[:n_src], refs[n_src:]
        for o, ref in enumerate(out_refs):
            if covered[o] != total[o]:
                ref[...] = jnp.zeros_like(ref)
        for o, ol, oc, s, sl, sc, width in pieces:
            val = in_refs[s][:, sc:sc + width] if sl is None else in_refs[s][sl, :, sc:sc + width]
            val = val.astype(outs[o][1])
            if ol is None:
                out_refs[o][:, oc:oc + width] = val
            else:
                out_refs[o][ol, :, oc:oc + width] = val

    def spec(shape):
        if len(shape) == 2:
            return pl.BlockSpec((tr, shape[1]), lambda i: (i, 0))
        return pl.BlockSpec((shape[0], tr, shape[2]), lambda i: (0, i, 0))

    return pl.pallas_call(
        body, name=name, grid=(rows // tr,),
        in_specs=[spec(a.shape) for a in srcs], out_specs=[spec(shp) for shp, _ in outs],
        out_shape=[jax.ShapeDtypeStruct(shp, dt) for shp, dt in outs],
        compiler_params=_cparams("parallel"))(*srcs)


def _shard_pieces(seg_ranges, shard_w):
    out = []
    for seg, runs in enumerate(seg_ranges):
        for g0, width, s0 in runs:
            done = 0
            while done < width:
                dev, col = divmod(g0 + done, shard_w)
                take = min(width - done, shard_w - col)
                out.append((seg, s0 + done, dev, col, take))
                done += take
    return out


_RELATIONS =[(dx, dy, dc) for dx in (0, 1) for dy in (0, 1) for dc in (0, 1)][1:]


def _exchange(arrays, scatter, *, name):
    n_arr = len(arrays)
    n_rel = len(_RELATIONS)

    def body(*refs):
        ins, outs = refs[:n_arr], refs[n_arr:2 * n_arr]
        send_sems, recv_sems, local_sems = refs[2 * n_arr:]
        x, y, c = lax.axis_index("x"), lax.axis_index("y"), lax.axis_index("c")
        me = 4 * x + 2 * y + c
        copies = []
        for a in range(n_arr):
            src = ins[a].at[me] if scatter[a] else ins[a]
            local = pltpu.make_async_copy(src, outs[a].at[me], local_sems.at[a])
            local.start()
            copies.append(local)
        remote = []
        for k, (dx, dy, dc) in enumerate(_RELATIONS):
            px, py, pc = x ^ dx, y ^ dy, c ^ dc
            peer = 4 * px + 2 * py + pc
            for a in range(n_arr):
                src = ins[a].at[peer] if scatter[a] else ins[a]
                cp = pltpu.make_async_remote_copy(
                    src_ref=src, dst_ref=outs[a].at[me], send_sem=send_sems.at[a * n_rel + k],
                    recv_sem=recv_sems.at[a * n_rel + k], device_id=(px, py, pc), device_id_type=MESH)
                cp.start()
                remote.append((cp, a, k, peer))
        for cp, a, k, peer in remote:
            cp.wait_send()
        for cp, a, k, peer in remote:
            src = ins[a].at[peer] if scatter[a] else ins[a]
            pltpu.make_async_remote_copy(
                src_ref=src, dst_ref=outs[a].at[peer], send_sem=send_sems.at[a * n_rel + k],
                recv_sem=recv_sems.at[a * n_rel + k], device_id=(x, y, c), device_id_type=MESH).wait_recv()
        for local in copies:
            local.wait()

    out_shape = [jax.ShapeDtypeStruct((N_DEV,) + (a.shape[1:] if s else a.shape), a.dtype)
                 for a, s in zip(arrays, scatter)]
    any_spec = pl.BlockSpec(memory_space=pl.ANY)
    return pl.pallas_call(
        body, name=name, in_specs=[any_spec] * n_arr, out_specs=[any_spec] * n_arr, out_shape=out_shape,
        scratch_shapes=[pltpu.SemaphoreType.DMA((n_arr * n_rel,)), pltpu.SemaphoreType.DMA((n_arr * n_rel,)),
                        pltpu.SemaphoreType.DMA((n_arr,))],
    )(*arrays)


def _adamw(w, gslots, m, v, *, name):
    rows, cols = w.shape
    tr = _row_tile(rows, 128) if rows % 16 == 0 else rows
    c1 = 1.0 / (1.0 - ADAM_B1 ** ADAM_STEP)
    c2 = 1.0 / (1.0 - ADAM_B2 ** ADAM_STEP)

    def body(w_ref, g_ref, m_ref, v_ref, go_ref, d_ref, mo_ref, vo_ref):
        g = g_ref[0]
        for s in range(1, N_DEV):
            g = g + g_ref[s]
        mn = ADAM_B1 * m_ref[...] + (1.0 - ADAM_B1) * g
        vn = ADAM_B2 * v_ref[...] + (1.0 - ADAM_B2) * (g * g)
        go_ref[...] = g
        mo_ref[...] = mn
        vo_ref[...] = vn
        d_ref[...] = -ADAM_LR * ((mn * c1) / (jnp.sqrt(vn * c2) + ADAM_EPS) + ADAM_WD * w_ref[...])

    blk = pl.BlockSpec((tr, cols), lambda i: (i, 0))
    shp = jax.ShapeDtypeStruct((rows, cols), F32)
    return pl.pallas_call(
        body, name=name, grid=(rows // tr,),
        in_specs=[blk, pl.BlockSpec((N_DEV, tr, cols), lambda i: (0, i, 0)), blk, blk],
        out_specs=[blk] * 4, out_shape=[shp] * 4,
        compiler_params=_cparams("parallel"))(w, gslots, m, v)


_BIG = ("w_in", "w_ssd_branch", "w_attn_branch", "w_out", "w_ffn_in", "w_ffn_out")
_SMALL_SHARDED = ("meta_tokens", "ssd_conv_w", "gate_b", "ffn_conv_w")
_SMALL_REPLICATED = ("norm_mix_w", "ssd_conv_b", "ssd_dt_bias", "ssd_a_log", "ssd_d", "ssd_norm_w", "attn_sinks",
                     "rel_bias", "norm_ffn_w", "ffn_conv_b", "norm_final_w")
_WEIGHTS = ("meta_tokens", "norm_mix_w", "w_in", "ssd_conv_w", "ssd_conv_b", "ssd_dt_bias", "ssd_a_log", "ssd_d",
            "ssd_norm_w", "w_ssd_branch", "w_attn_branch", "attn_sinks", "rel_bias", "gate_b", "w_out", "norm_ffn_w",
            "w_ffn_in", "ffn_conv_w", "ffn_conv_b", "w_ffn_out", "norm_final_w")
_ROW_SHARDED = ("w_ssd_branch", "w_attn_branch", "w_out", "w_ffn_out")
_COL_SHARDED = ("w_in", "w_ffn_in", "meta_tokens", "ssd_conv_w", "gate_b", "ffn_conv_w")
_IN_SEGS = (("z", SSD_INNER), ("xbc", SSD_XBC), ("dt", SSD_HEADS), ("qkv", ATT_Q + 2 * ATT_KV), ("g", 2 * D_MODEL))


def _pack_rows(flat_parts, width, row_mult):
    flat = jnp.concatenate([p.reshape(-1) for p in flat_parts])
    pad = (-flat.shape[0]) % (width * row_mult)
    if pad:
        flat = jnp.concatenate([flat, jnp.zeros((pad,), flat.dtype)])
    return flat.reshape(-1, width)


def _unpack(flat, shapes):
    out, off = [], 0
    for shp in shapes:
        size = int(np.prod(shp))
        out.append(flat[off:off + size].reshape(shp))
        off += size
    return out


def _gather_full(stack, name, shard_shape):
    if name in _COL_SHARDED:
        return jnp.transpose(stack, (1, 0, 2)).reshape(shard_shape[0], N_DEV * shard_shape[1])
    return stack.reshape(N_DEV * shard_shape[0], shard_shape[1])


_IN_SEG_W = {"z": SSD_INNER, "xbc": SSD_XBC, "dt": DT_W, "qkv": ATT_Q + 2 * ATT_KV, "g": 2 * D_MODEL}
_IN_SHARD_W = (SSD_INNER + SSD_XBC + SSD_HEADS + ATT_Q + 2 * ATT_KV + 2 * D_MODEL) // N_DEV
_FFN_SHARD_W = 2 * D_FF // N_DEV


def _in_seg_runs():
    runs, off = [], 0
    for nm, width in _IN_SEGS:
        if nm == "dt":
            runs.append([(off + SSD_HPG * g, SSD_HPG, LANES * g) for g in range(SSD_GROUPS)])
        else:
            runs.append([(off, width, 0)])
        off += width
    return runs


def _w_in_to_segments(stack):
    pieces = [(seg, None, scol, 0, dev, col, w) for seg, scol, dev, col, w in _shard_pieces(_in_seg_runs(), _IN_SHARD_W)]
    outs = [((D_MODEL, _IN_SEG_W[nm]), stack.dtype) for nm, _ in _IN_SEGS]
    return dict(zip([nm for nm, _ in _IN_SEGS], _col_move([stack], outs, pieces, name="w_in_segments")))


def _segments_to_w_in_shards(seg_grads):
    pieces = [(0, dev, col, seg, None, scol, w) for seg, scol, dev, col, w in _shard_pieces(_in_seg_runs(), _IN_SHARD_W)]
    return _col_move(seg_grads, [((N_DEV, D_MODEL, _IN_SHARD_W), F32)], pieces, name="g_w_in_shards")[0]


def _ffn_in_from_shards(stack):
    pieces = [(0, None, scol, 0, dev, col, w)
              for _, scol, dev, col, w in _shard_pieces([[(0, 2 * D_FF, 0)]], _FFN_SHARD_W)]
    return _col_move([stack], [((D_MODEL, 2 * D_FF), stack.dtype)], pieces, name="w_ffn_in_full")[0]


def _ffn_in_to_shards(g_up, g_gate):
    pieces = [(0, dev, col, seg, None, scol, w)
              for seg, scol, dev, col, w in _shard_pieces([[(0, D_FF, 0)], [(D_FF, D_FF, 0)]], _FFN_SHARD_W)]
    return _col_move([g_up, g_gate], [((N_DEV, D_MODEL, _FFN_SHARD_W), F32)], pieces, name="g_w_ffn_in_shards")[0]


def _dt_spread(w_dt):
    k = w_dt.shape[0]
    w4 = w_dt.reshape(k, SSD_GROUPS, SSD_HPG)
    return jnp.pad(w4, ((0, 0), (0, 0), (0, LANES - SSD_HPG))).reshape(k, DT_W)


def _dt_gather(w_wide):
    k = w_wide.shape[0]
    return w_wide.reshape(k, SSD_GROUPS, LANES)[:, :, :SSD_HPG].reshape(k, SSD_HEADS)


def _local_step(x, target, w):
    h0 = jnp.concatenate([jnp.zeros((PAD, D_MODEL), F32), w["meta_tokens"], x], axis=0)
    segs = w["in_segs"]
    w_ffn_up, w_ffn_gate = w["w_ffn_in"][:, :D_FF], w["w_ffn_in"][:, D_FF:]

    dtb = _dt_spread(w["ssd_dt_bias"])
    alog = _dt_spread(w["ssd_a_log"])
    dskip = _dt_spread(w["ssd_d"])
    sinks = jnp.pad(w["attn_sinks"], ((0, 0), (0, LANES - ATT_HEADS)))
    onehot_t = _onehot_t()
    tabs = _bias_tables(w["rel_bias"].T, onehot_t)
    t_band = tabs[:, :NT_BAND].reshape(ATT_HEADS, BLOCK, 2 * BLOCK)
    t_meta = jnp.transpose(tabs[:, NT_BAND:].reshape(ATT_HEADS, 3, BLOCK, BLOCK), (1, 0, 2, 3))

    u = _rms_fwd(h0, w["norm_mix_w"], name="rms_mix_fwd")
    z = _mm(u, segs["z"], name="in_z")
    xbc = _mm(u, segs["xbc"], name="in_xbc")
    dt_raw = _mm(u, segs["dt"], name="in_dt")
    qkv = _mm(u, segs["qkv"], name="in_qkv")
    gates = _mm(u, segs["g"], name="in_g")
    pre = _conv_fwd(xbc, w["ssd_conv_w"], w["ssd_conv_b"], name="ssd_conv_fwd")
    y, yn, hsave = _ssd_fwd(pre, dt_raw, z, dtb, alog, dskip, w["ssd_norm_w"])
    y_ssd = _mm(yn, w["w_ssd_branch"], name="ssd_out")
    att = _attn_fwd(qkv, t_meta, t_band, sinks)
    y_att = _mm(att, w["w_attn_branch"], name="att_out")
    merged = _merge_fwd(gates, y_ssd, y_att, w["gate_b"])
    h1 = _mm(merged, w["w_out"], c=h0, mask=True, name="mix_out")
    u2 = _rms_fwd(h1, w["norm_ffn_w"], name="rms_ffn_fwd")
    hid_raw = _mm(u2, w["w_ffn_in"], name="ffn_in")
    hid_up, hid_gate, act = _ffn_act_fwd(hid_raw, w["ffn_conv_w"], w["ffn_conv_b"])
    h2 = _mm(act, w["w_ffn_out"], c=h1, mask=True, name="ffn_out")
    dh2, loss_row, g_norm_final = _final_loss(h2, w["norm_final_w"], target)

    grads = {"norm_final_w": g_norm_final}
    dact = _mm(dh2, w["w_ffn_out"], tb=True, mask=True, name="d_act")
    grads["w_ffn_out"] = _mm(act, dh2, ta=True, mask=True, name="g_w_ffn_out")
    dx_up, dx_gate, dcw_up, dcw_gate, dcb_up, dcb_gate = _ffn_act_bwd(dact, hid_up, hid_gate, hid_raw, w["ffn_conv_w"])
    grads["ffn_conv_w"] = jnp.concatenate([dcw_up, dcw_gate], axis=1)
    grads["ffn_conv_b"] = jnp.concatenate([dcb_up, dcb_gate], axis=1)
    du2 = _mm(dx_up, w_ffn_up, tb=True, name="d_u2_up")
    du2 = _mm(dx_gate, w_ffn_gate, tb=True, c=du2, name="d_u2_gate")
    grads["w_ffn_in"] = (_mm(u2, dx_up, ta=True, name="g_w_ffn_up"), _mm(u2, dx_gate, ta=True, name="g_w_ffn_gate"))
    dh1, grads["norm_ffn_w"] = _rms_bwd(h1, w["norm_ffn_w"], du2, dh2, name="rms_ffn_bwd")

    dmerged = _mm(dh1, w["w_out"], tb=True, mask=True, name="d_merged")
    grads["w_out"] = _mm(merged, dh1, ta=True, mask=True, name="g_w_out")
    dy_ssd, dy_att, dgates, grads["gate_b"] = _merge_bwd(dmerged, gates, y_ssd, y_att, w["gate_b"])
    dyn = _mm(dy_ssd, w["w_ssd_branch"], tb=True, name="d_yn")
    grads["w_ssd_branch"] = _mm(yn, dy_ssd, ta=True, name="g_w_ssd")
    datt = _mm(dy_att, w["w_attn_branch"], tb=True, name="d_att")
    grads["w_attn_branch"] = _mm(att, dy_att, ta=True, name="g_w_att")
    dz, dpxs, dpb, dpc, ddt, grads["ssd_norm_w"], g_dtb, g_alog, g_dskip = _ssd_bwd(
        dyn, y, z, pre, dt_raw, hsave, dtb, alog, dskip, w["ssd_norm_w"])
    grads["ssd_dt_bias"] = _dt_gather(g_dtb)
    grads["ssd_a_log"] = _dt_gather(g_alog)
    grads["ssd_d"] = _dt_gather(g_dskip)
    dpre = jnp.concatenate([dpxs, dpb, dpc], axis=1)
    dxbc, grads["ssd_conv_w"], grads["ssd_conv_b"] = _conv_bwd(dpre, xbc, w["ssd_conv_w"], name="ssd_conv_bwd")
    dqkv, d_tmeta, d_tband, d_sinks = _attn_bwd(datt, qkv, t_meta, t_band, sinks)
    grads["attn_sinks"] = d_sinks[:, :ATT_HEADS]
    dtab = jnp.concatenate([d_tband.reshape(ATT_HEADS, NT_BAND),
                            jnp.transpose(d_tmeta, (1, 0, 2, 3)).reshape(ATT_HEADS, 3 * NT_META)], axis=1)
    grads["rel_bias"] = _bias_grad(dtab, onehot_t).T
    dsegs = {"z": dz, "xbc": dxbc, "dt": ddt, "qkv": dqkv, "g": dgates}
    du, g_in = None, []
    for nm, _ in _IN_SEGS:
        du = _mm(dsegs[nm], segs[nm], tb=True, c=du, name="d_u_" + nm)
        g_in.append(_mm(u, dsegs[nm], ta=True, name="g_w_in_" + nm))
    grads["in_segs"] = g_in
    dh0, grads["norm_mix_w"] = _rms_bwd(h0, w["norm_mix_w"], du, dh1, name="rms_mix_bwd")
    grads["meta_tokens"] = dh0[PAD:BLOCK]
    return loss_row[0, 0], dh0[BLOCK:], grads


def kernel(x, meta_tokens, norm_mix_w, w_in, ssd_conv_w, ssd_conv_b, ssd_dt_bias, ssd_a_log, ssd_d, ssd_norm_w, w_ssd_branch, w_attn_branch, attn_sinks, rel_bias, gate_b, w_out, norm_ffn_w, w_ffn_in, ffn_conv_w, ffn_conv_b, w_ffn_out, norm_final_w, loss_target, m_meta_tokens, m_norm_mix_w, m_w_in, m_ssd_conv_w, m_ssd_conv_b, m_ssd_dt_bias, m_ssd_a_log, m_ssd_d, m_ssd_norm_w, m_w_ssd_branch, m_w_attn_branch, m_attn_sinks, m_rel_bias, m_gate_b, m_w_out, m_norm_ffn_w, m_w_ffn_in, m_ffn_conv_w, m_ffn_conv_b, m_w_ffn_out, m_norm_final_w, v_meta_tokens, v_norm_mix_w, v_w_in, v_ssd_conv_w, v_ssd_conv_b, v_ssd_dt_bias, v_ssd_a_log, v_ssd_d, v_ssd_norm_w, v_w_ssd_branch, v_w_attn_branch, v_attn_sinks, v_rel_bias, v_gate_b, v_w_out, v_norm_ffn_w, v_w_ffn_in, v_ffn_conv_w, v_ffn_conv_b, v_w_ffn_out, v_norm_final_w):
    shard = dict(meta_tokens=meta_tokens, norm_mix_w=norm_mix_w, w_in=w_in, ssd_conv_w=ssd_conv_w,
                 ssd_conv_b=ssd_conv_b, ssd_dt_bias=ssd_dt_bias, ssd_a_log=ssd_a_log, ssd_d=ssd_d,
                 ssd_norm_w=ssd_norm_w, w_ssd_branch=w_ssd_branch, w_attn_branch=w_attn_branch,
                 attn_sinks=attn_sinks, rel_bias=rel_bias, gate_b=gate_b, w_out=w_out, norm_ffn_w=norm_ffn_w,
                 w_ffn_in=w_ffn_in, ffn_conv_w=ffn_conv_w, ffn_conv_b=ffn_conv_b, w_ffn_out=w_ffn_out,
                 norm_final_w=norm_final_w)
    mom_m = dict(zip(_WEIGHTS, (m_meta_tokens, m_norm_mix_w, m_w_in, m_ssd_conv_w, m_ssd_conv_b, m_ssd_dt_bias,
                                m_ssd_a_log, m_ssd_d, m_ssd_norm_w, m_w_ssd_branch, m_w_attn_branch, m_attn_sinks,
                                m_rel_bias, m_gate_b, m_w_out, m_norm_ffn_w, m_w_ffn_in, m_ffn_conv_w, m_ffn_conv_b,
                                m_w_ffn_out, m_norm_final_w)))
    mom_v = dict(zip(_WEIGHTS, (v_meta_tokens, v_norm_mix_w, v_w_in, v_ssd_conv_w, v_ssd_conv_b, v_ssd_dt_bias,
                                v_ssd_a_log, v_ssd_d, v_ssd_norm_w, v_w_ssd_branch, v_w_attn_branch, v_attn_sinks,
                                v_rel_bias, v_gate_b, v_w_out, v_norm_ffn_w, v_w_ffn_in, v_ffn_conv_w, v_ffn_conv_b,
                                v_w_ffn_out, v_norm_final_w)))
    orig_shape = {k: a.shape for k, a in shard.items()}
    two_d = {k: a.reshape(a.shape[-2:]) if a.ndim >= 2 else a.reshape(1, -1) for k, a in shard.items()}
    shape2 = {k: a.shape for k, a in two_d.items()}

    def as2d(tree):
        return {k: tree[k].reshape(shape2[k]) for k in _WEIGHTS}

    mom_m, mom_v = as2d(mom_m), as2d(mom_v)

    def row_pack(tree):
        return jnp.concatenate([tree[k] for k in _ROW_SHARDED], axis=0)

    small_pack = _pack_rows([two_d[k] for k in _SMALL_SHARDED], LANES, SMALL_ROW_MULT)
    w_in_all, w_ffn_in_all, rows_all, small_all = _exchange(
        [two_d["w_in"].astype(BF16), two_d["w_ffn_in"].astype(BF16), row_pack(two_d).astype(BF16), small_pack],
        [False] * 4, name="gather_weights")
    full = {k: two_d[k] for k in _SMALL_REPLICATED}
    full["in_segs"] = _w_in_to_segments(w_in_all)
    full["w_ffn_in"] = _ffn_in_from_shards(w_ffn_in_all)
    off = 0
    for k in _ROW_SHARDED:
        r = shape2[k][0]
        full[k] = rows_all[:, off:off + r].reshape(N_DEV * r, D_MODEL)
        off += r
    small_flat = small_all.reshape(N_DEV, -1)
    off = 0
    for k in _SMALL_SHARDED:
        size = int(np.prod(shape2[k]))
        full[k] = _gather_full(small_flat[:, off:off + size].reshape((N_DEV,) + shape2[k]), k, shape2[k])
        off += size

    loss_local, grad_x, grads = _local_step(x[0], loss_target[0], full)

    rows_send = jnp.concatenate([grads[k].reshape(N_DEV, shape2[k][0], D_MODEL) for k in _ROW_SHARDED], axis=1)
    small_names = _SMALL_SHARDED + _SMALL_REPLICATED
    small_send = _pack_rows([grads[k] for k in small_names] + [loss_local.reshape(1)], LANES, SMALL_ROW_MULT)
    in_recv, ffn_recv, rows_recv, small_recv = _exchange(
        [_segments_to_w_in_shards(grads["in_segs"]), _ffn_in_to_shards(*grads["w_ffn_in"]), rows_send, small_send],
        [True, True, True, False], name="exchange_grads")

    big = {"w_in": _adamw(two_d["w_in"], in_recv, mom_m["w_in"], mom_v["w_in"], name="adamw_w_in"),
           "w_ffn_in": _adamw(two_d["w_ffn_in"], ffn_recv, mom_m["w_ffn_in"], mom_v["w_ffn_in"], name="adamw_w_ffn_in")}
    rows_out = _adamw(row_pack(two_d), rows_recv, row_pack(mom_m), row_pack(mom_v), name="adamw_rows")
    off = 0
    for k in _ROW_SHARDED:
        r = shape2[k][0]
        big[k] = [a[off:off + r] for a in rows_out]
        off += r
    me =4 * lax.axis_index("x") + 2 * lax.axis_index("y") + lax.axis_index("c")
    small_full_shapes = [grads[k].shape for k in small_names]
    n_small = sum(int(np.prod(s)) for s in small_full_shapes)

    def packed_small(tree):
        parts = []
        for k in small_names:
            a = tree[k]
            if k in _SMALL_SHARDED:
                fullw = jnp.zeros(grads[k].shape, F32)
                a = lax.dynamic_update_slice(fullw, a, (0, me * a.shape[1]))
            parts.append(a)
        return _pack_rows(parts + [jnp.zeros((1,), F32)], LANES, SMALL_ROW_MULT)

    g_small, d_small, m_small, v_small = _adamw(packed_small(two_d), small_recv, packed_small(mom_m),
                                                packed_small(mom_v), name="adamw_small")

    def unpack_all(which, small):
        out = {k: big[k][which] for k in _BIG}
        flat = small.reshape(-1)
        for k, a in zip(small_names, _unpack(flat, small_full_shapes)):
            if k in _SMALL_SHARDED:
                a = lax.dynamic_slice(a, (0, me * shape2[k][1]), shape2[k])
            out[k] = a
        return out, flat[n_small]

    g_all, loss = unpack_all(0, g_small)
    d_all, _ = unpack_all(1, d_small)
    m_all, _ = unpack_all(2, m_small)
    v_all, _ = unpack_all(3, v_small)

    def final(tree):
        return [tree[k].reshape(orig_shape[k]) for k in _WEIGHTS]

    return (loss, grad_x[None], *final(g_all), *final(d_all), *final(m_all), *final(v_all))
```

```python
import functools
import math

import numpy as np
import jax
import jax.numpy as jnp
from jax import lax
from jax.experimental import pallas as pl
from jax.experimental.pallas import tpu as pltpu

F32 = jnp.float32
BF16 = jnp.bfloat16
HIGHEST = lax.Precision.HIGHEST

D_MODEL = 1024
N_META = 16
BLOCK = 128
PAD = BLOCK - N_META
EPS = 1e-6
NEG = -1e30
SSD_INNER = 2 * D_MODEL
SSD_HEADDIM = 64
SSD_HEADS = SSD_INNER // SSD_HEADDIM
SSD_GROUPS = 4
SSD_HPG = SSD_HEADS // SSD_GROUPS
SSD_STATE = 128
SSD_CONV = 4
SSD_GW = SSD_HPG * SSD_HEADDIM
SSD_BC = SSD_GROUPS * SSD_STATE
SSD_XBC = SSD_INNER + 2 * SSD_BC
ATT_HEADS = 16
ATT_KV_HEADS = 2
ATT_HEADDIM = 64
ATT_GQ = ATT_HEADS // ATT_KV_HEADS
ATT_Q = ATT_HEADS * ATT_HEADDIM
ATT_KV = ATT_KV_HEADS * ATT_HEADDIM
REL_BUCKETS = 32
REL_MAX_DIST = 128
D_FF = 2816
FFN_CONV = 3
ADAM_LR = 0.001
ADAM_B1 = 0.9
ADAM_B2 = 0.999
ADAM_EPS = 1e-08
ADAM_WD = 0.01
ADAM_STEP = 10

N_DEV = 8
LANES = 128
SUBLANES = 8
DT_W = SSD_GROUPS * LANES
VMEM_LIMIT_BYTES = 56 * 1024 * 1024
MESH = pl.DeviceIdType.MESH

SMALL_ROW_MULT = 16

NT_BAND = BLOCK * 2 * BLOCK
NT_META = BLOCK * BLOCK
NT_ALL = NT_BAND + 3 * NT_META
NT_TILE = 8192


def _cparams(*sem):
    return pltpu.CompilerParams(dimension_semantics=sem, vmem_limit_bytes=VMEM_LIMIT_BYTES)


def _row_tile(n, cap):
    best = None
    for t in range(16, min(n, cap) + 1, 16):
        if n % t == 0:
            best = t
    return best or n


def _col_tile(n, cap):
    for t in (1408, 1280, 1024, 768, 640, 512, 384, 256, 128):
        if t <= cap and n % t == 0:
            return t
    return n


def _silu(x):
    return x * jax.nn.sigmoid(x)


def _dsilu(x):
    s = jax.nn.sigmoid(x)
    return s * (1.0 + x * (1.0 - s))


def _softplus(x):
    return jnp.maximum(x, 0.0) + jnp.log(1.0 + jnp.exp(-jnp.abs(x)))


def _dot_nt(a, b):
    return lax.dot_general(a, b, (((1,), (1,)), ((), ())), preferred_element_type=F32)


def _dot_tn(a, b):
    return lax.dot_general(a, b, (((0,), (0,)), ((), ())), preferred_element_type=F32)


def _dot(a, b):
    return jnp.dot(a, b, preferred_element_type=F32)


def _sum_all(x):
    return jnp.sum(jnp.sum(x, axis=1, keepdims=True), axis=0, keepdims=True)


def _mm(a, b, *, name, ta=False, tb=False, c=None, mask=False, out_dtype=F32):
    if not ta:
        m, k = a.shape
        n = b.shape[0] if tb else b.shape[1]
        tm = _row_tile(m, 832)
        tn = _col_tile(n, 512 if k > 3072 else 1024)

        def body(*refs):
            if c is None:
                a_ref, b_ref, o_ref = refs
            else:
                a_ref, b_ref, c_ref, o_ref = refs
            acc = (_dot_nt if tb else _dot)(a_ref[...].astype(BF16), b_ref[...].astype(BF16))
            if mask:
                row = pl.program_id(0) * tm + lax.broadcasted_iota(jnp.int32, (tm, 1), 0)
                acc = jnp.where(row >= PAD, acc, 0.0)
            if c is not None:
                acc = acc + c_ref[...]
            o_ref[...] = acc.astype(out_dtype)

        b_spec = pl.BlockSpec((tn, k), lambda i, j: (j, 0)) if tb else pl.BlockSpec((k, tn), lambda i, j: (0, j))
        in_specs = [pl.BlockSpec((tm, k), lambda i, j: (i, 0)), b_spec]
        args = [a, b]
        if c is not None:
            in_specs.append(pl.BlockSpec((tm, tn), lambda i, j: (i, j)))
            args.append(c)
        return pl.pallas_call(
            body, name=name, grid=(m // tm, n // tn), in_specs=in_specs,
            out_specs=pl.BlockSpec((tm, tn), lambda i, j: (i, j)),
            out_shape=jax.ShapeDtypeStruct((m, n), out_dtype),
            compiler_params=_cparams("parallel", "parallel"))(*args)

    kc, m = a.shape
    n = b.shape[1]
    tk = _row_tile(kc, 832)
    tm = _col_tile(m, 1408)
    tn = _col_tile(n, 1408)

    def body_t(a_ref, b_ref, o_ref):
        kk = pl.program_id(2)
        bb = b_ref[...]
        if mask:
            row = kk * tk + lax.broadcasted_iota(jnp.int32, (tk, 1), 0)
            bb = jnp.where(row >= PAD, bb, jnp.zeros_like(bb))
        p = _dot_tn(a_ref[...].astype(BF16), bb.astype(BF16))

        @pl.when(kk == 0)
        def _():
            o_ref[...] = p

        @pl.when(kk > 0)
        def _():
            o_ref[...] += p

    return pl.pallas_call(
        body_t, name=name, grid=(m // tm, n // tn, kc // tk),
        in_specs=[pl.BlockSpec((tk, tm), lambda i, j, kk: (kk, i)), pl.BlockSpec((tk, tn), lambda i, j, kk: (kk, j))],
        out_specs=pl.BlockSpec((tm, tn), lambda i, j, kk: (i, j)),
        out_shape=jax.ShapeDtypeStruct((m, n), F32),
        compiler_params=_cparams("parallel", "parallel", "arbitrary"))(a, b)


def _rms_fwd(h, w, *, name):
    n, d = h.shape
    tm = _row_tile(n, 832)

    def body(h_ref, w_ref, o_ref):
        x = h_ref[...]
        r = lax.rsqrt(jnp.mean(x * x, axis=-1, keepdims=True) + EPS)
        o_ref[...] = (x * r * w_ref[...]).astype(BF16)

    return pl.pallas_call(
        body, name=name, grid=(n // tm,),
        in_specs=[pl.BlockSpec((tm, d), lambda i: (i, 0)), pl.BlockSpec((1, d), lambda i: (0, 0))],
        out_specs=pl.BlockSpec((tm, d), lambda i: (i, 0)),
        out_shape=jax.ShapeDtypeStruct((n, d), BF16),
        compiler_params=_cparams("parallel"))(h, w)


def _rms_bwd(x, w, dy, dres, *, name):
    n, d = x.shape
    tm = _row_tile(n, 832)

    def body(x_ref, w_ref, dy_ref, dres_ref, dx_ref, dw_ref):
        i = pl.program_id(0)
        xv = x_ref[...]
        r = lax.rsqrt(jnp.mean(xv * xv, axis=-1, keepdims=True) + EPS)
        xh = xv * r
        dyv = dy_ref[...]
        g = dyv * w_ref[...]
        dx_ref[...] = r * (g - xh * jnp.mean(g * xh, axis=-1, keepdims=True)) + dres_ref[...]
        part = jnp.sum(dyv * xh, axis=0, keepdims=True)

        @pl.when(i == 0)
        def _():
            dw_ref[...] = part

        @pl.when(i > 0)
        def _():
            dw_ref[...] += part

    row = pl.BlockSpec((tm, d), lambda i: (i, 0))
    vec = pl.BlockSpec((1, d), lambda i: (0, 0))
    return pl.pallas_call(
        body, name=name, grid=(n // tm,), in_specs=[row, vec, row, row], out_specs=[row, vec],
        out_shape=[jax.ShapeDtypeStruct((n, d), F32), jax.ShapeDtypeStruct((1, d), F32)],
        compiler_params=_cparams("arbitrary"))(x, w, dy, dres)


def _final_loss(h, w, target):
    n, d = h.shape
    nb = n // BLOCK

    def body(h_ref, w_ref, t_ref, dh_ref, loss_ref, dw_ref):
        i = pl.program_id(0)
        xv = h_ref[...]
        r = lax.rsqrt(jnp.mean(xv * xv, axis=-1, keepdims=True) + EPS)
        xh = xv * r
        wv = w_ref[...]
        err = jnp.where(i >= 1, xh * wv - t_ref[...], 0.0)
        dyv = err * (1.0 / d)
        g = dyv * wv
        dh_ref[...] = r * (g - xh * jnp.mean(g * xh, axis=-1, keepdims=True))
        lpart = jnp.broadcast_to(0.5 * _sum_all(err * err) * (1.0 / d), (1, LANES))
        wpart = jnp.sum(dyv * xh, axis=0, keepdims=True)

        @pl.when(i == 0)
        def _():
            loss_ref[...] = lpart
            dw_ref[...] = wpart

        @pl.when(i > 0)
        def _():
            loss_ref[...] += lpart
            dw_ref[...] += wpart

    row = pl.BlockSpec((BLOCK, d), lambda i: (i, 0))
    vec = pl.BlockSpec((1, d), lambda i: (0, 0))
    return pl.pallas_call(
        body, name="final_loss", grid=(nb,),
        in_specs=[row, vec, pl.BlockSpec((BLOCK, d), lambda i: (jnp.maximum(i - 1, 0), 0))],
        out_specs=[row, pl.BlockSpec((1, LANES), lambda i: (0, 0)), vec],
        out_shape=[jax.ShapeDtypeStruct((n, d), F32), jax.ShapeDtypeStruct((1, LANES), F32),
                   jax.ShapeDtypeStruct((1, d), F32)],
        compiler_params=_cparams("arbitrary"))(h, w, target)


def _main_spec(tm, cb, off=0):
    return pl.BlockSpec((tm, cb), lambda j, i: (i, j + off))


def _prev_spec(tm, cb, off=0):
    r8 = tm // SUBLANES
    return pl.BlockSpec((SUBLANES, cb), lambda j, i: (jnp.maximum(i * r8 - 1, 0), j + off))


def _next_spec(tm, cb, n_rows, off=0):
    r8 = tm // SUBLANES
    last = n_rows // SUBLANES - 1
    return pl.BlockSpec((SUBLANES, cb), lambda j, i: (jnp.minimum((i + 1) * r8, last), j + off))


def _with_prev(prev_ref, main_ref, i):
    prev = jnp.where(i > 0, prev_ref[...], 0.0)
    return jnp.concatenate([prev, main_ref[...]], axis=0)


def _with_next(main, nxt, i, n_tiles):
    return jnp.concatenate([main, jnp.where(i < n_tiles - 1, nxt, 0.0)], axis=0)


def _back(xx, s, tm):
    if s == 0:
        return xx[SUBLANES:SUBLANES + tm]
    return pltpu.roll(xx, s, 0)[SUBLANES:SUBLANES + tm]


def _ahead(xx, s, tm):
    if s == 0:
        return xx[:tm]
    return pltpu.roll(xx, tm + SUBLANES - s, 0)[:tm]


def _conv_fwd(x, w, b, *, name):
    n, cdim = x.shape
    kw = w.shape[0]
    tm = _row_tile(n, 832)
    cb = _col_tile(cdim, 512)

    def body(xp_ref, x_ref, w_ref, b_ref, o_ref):
        xx = _with_prev(xp_ref, x_ref, pl.program_id(1))
        acc = jnp.broadcast_to(b_ref[...], (tm, cb))
        for k in range(kw):
            acc = acc + w_ref[k:k + 1, :] * _back(xx, kw - 1 - k, tm)
        o_ref[...] = acc

    return pl.pallas_call(
        body, name=name, grid=(cdim // cb, n // tm),
        in_specs=[_prev_spec(tm, cb), _main_spec(tm, cb), pl.BlockSpec((kw, cb), lambda j, i: (0, j)),
                  pl.BlockSpec((1, cb), lambda j, i: (0, j))],
        out_specs=_main_spec(tm, cb),
        out_shape=jax.ShapeDtypeStruct((n, cdim), F32),
        compiler_params=_cparams("parallel", "parallel"))(x, x, w, b)


def _conv_bwd_core(dpre_ext, x_ext, w_ref, kw, tm):
    dpre = dpre_ext[:tm]
    dx = None
    dws = []
    for k in range(kw):
        term = w_ref[k:k + 1, :] * _ahead(dpre_ext, kw - 1 - k, tm)
        dx = term if dx is None else dx + term
        dws.append(jnp.sum(dpre * _back(x_ext, kw - 1 - k, tm), axis=0, keepdims=True))
    return dx, dws, jnp.sum(dpre, axis=0, keepdims=True)


def _acc_rows(i, dw_ref, db_ref, dws, db):
    @pl.when(i == 0)
    def _():
        for k, v in enumerate(dws):
            dw_ref[k:k + 1, :] = v
        db_ref[...] = db

    @pl.when(i > 0)
    def _():
        for k, v in enumerate(dws):
            dw_ref[k:k + 1, :] += v
        db_ref[...] += db


def _conv_bwd(dpre, x, w, *, name):
    n, cdim = x.shape
    kw = w.shape[0]
    tm = _row_tile(n, 832)
    cb = _col_tile(cdim, 512)
    nt = n // tm

    def body(d_ref, dn_ref, xp_ref, x_ref, w_ref, dx_ref, dw_ref, db_ref):
        i = pl.program_id(1)
        dpre_ext = _with_next(d_ref[...], dn_ref[...], i, nt)
        x_ext = _with_prev(xp_ref, x_ref, i)
        dx, dws, db = _conv_bwd_core(dpre_ext, x_ext, w_ref, kw, tm)
        dx_ref[...] = dx.astype(BF16)
        _acc_rows(i, dw_ref, db_ref, dws, db)

    wspec = pl.BlockSpec((kw, cb), lambda j, i: (0, j))
    bspec = pl.BlockSpec((1, cb), lambda j, i: (0, j))
    return pl.pallas_call(
        body, name=name, grid=(cdim // cb, nt),
        in_specs=[_main_spec(tm, cb), _next_spec(tm, cb, n), _prev_spec(tm, cb), _main_spec(tm, cb), wspec],
        out_specs=[_main_spec(tm, cb), wspec, bspec],
        out_shape=[jax.ShapeDtypeStruct((n, cdim), BF16), jax.ShapeDtypeStruct((kw, cdim), F32),
                   jax.ShapeDtypeStruct((1, cdim), F32)],
        compiler_params=_cparams("parallel", "arbitrary"))(dpre, dpre, x, x, w)


def _ffn_act_fwd(x, w, b):
    n = x.shape[0]
    kw = w.shape[0]
    tm = _row_tile(n, 832)
    cb = _col_tile(D_FF, 256)
    nc = D_FF // cb

    def body(xpu_ref, xu_ref, xpg_ref, xg_ref, wu_ref, wg_ref, bu_ref, bg_ref, hu_ref, hg_ref, act_ref):
        i = pl.program_id(1)
        outs = []
        for xp_ref, x_ref, w_ref, b_ref in ((xpu_ref, xu_ref, wu_ref, bu_ref), (xpg_ref, xg_ref, wg_ref, bg_ref)):
            xx = _with_prev(xp_ref, x_ref, i)
            acc = jnp.broadcast_to(b_ref[...], (tm, cb))
            for k in range(kw):
                acc = acc + w_ref[k:k + 1, :] * _back(xx, kw - 1 - k, tm)
            outs.append(acc)
        hu_ref[...] = outs[0]
        hg_ref[...] = outs[1]
        act_ref[...] = (_silu(outs[1]) * outs[0]).astype(BF16)

    def wspec(off):
        return pl.BlockSpec((kw, cb), lambda j, i: (0, j + off))

    def bspec(off):
        return pl.BlockSpec((1, cb), lambda j, i: (0, j + off))

    out = _main_spec(tm, cb)
    return pl.pallas_call(
        body, name="ffn_act_fwd", grid=(nc, n // tm),
        in_specs=[_prev_spec(tm, cb), _main_spec(tm, cb), _prev_spec(tm, cb, nc), _main_spec(tm, cb, nc),
                  wspec(0), wspec(nc), bspec(0), bspec(nc)],
        out_specs=[out, out, out],
        out_shape=[jax.ShapeDtypeStruct((n, D_FF), F32), jax.ShapeDtypeStruct((n, D_FF), F32),
                   jax.ShapeDtypeStruct((n, D_FF), BF16)],
        compiler_params=_cparams("parallel", "parallel"))(x, x, x, x, w, w, b, b)


def _ffn_act_bwd(dact, hu, hg, x, w):
    n = x.shape[0]
    kw = w.shape[0]
    tm = _row_tile(n, 832)
    cb = _col_tile(D_FF, 256)
    nc = D_FF // cb
    nt = n // tm

    def body(d_ref, dn_ref, hu_ref, hun_ref, hg_ref, hgn_ref, xpu_ref, xu_ref, xpg_ref, xg_ref, wu_ref, wg_ref,
             dxu_ref, dxg_ref, dwu_ref, dwg_ref, dbu_ref, dbg_ref):
        i = pl.program_id(1)
        dact_e = _with_next(d_ref[...], dn_ref[...], i, nt)
        up_e = _with_next(hu_ref[...], hun_ref[...], i, nt)
        gate_e = _with_next(hg_ref[...], hgn_ref[...], i, nt)
        dup_e = dact_e * _silu(gate_e)
        dgate_e = dact_e * up_e * _dsilu(gate_e)
        dx, dws, db = _conv_bwd_core(dup_e, _with_prev(xpu_ref, xu_ref, i), wu_ref, kw, tm)
        dxu_ref[...] = dx.astype(BF16)
        _acc_rows(i, dwu_ref, dbu_ref, dws, db)
        dx, dws, db = _conv_bwd_core(dgate_e, _with_prev(xpg_ref, xg_ref, i), wg_ref, kw, tm)
        dxg_ref[...] = dx.astype(BF16)
        _acc_rows(i, dwg_ref, dbg_ref, dws, db)

    main, nxt = _main_spec(tm, cb), _next_spec(tm, cb, n)
    wspec0 = pl.BlockSpec((kw, cb), lambda j, i: (0, j))
    wspec1 = pl.BlockSpec((kw, cb), lambda j, i: (0, j + nc))
    bspec = pl.BlockSpec((1, cb), lambda j, i: (0, j))
    return pl.pallas_call(
        body, name="ffn_act_bwd", grid=(nc, nt),
        in_specs=[main, nxt, main, nxt, main, nxt,
                  _prev_spec(tm, cb), _main_spec(tm, cb), _prev_spec(tm, cb, nc), _main_spec(tm, cb, nc),
                  wspec0, wspec1],
        out_specs=[main, main, wspec0, wspec0, bspec, bspec],
        out_shape=[jax.ShapeDtypeStruct((n, D_FF), BF16), jax.ShapeDtypeStruct((n, D_FF), BF16),
                   jax.ShapeDtypeStruct((kw, D_FF), F32), jax.ShapeDtypeStruct((kw, D_FF), F32),
                   jax.ShapeDtypeStruct((1, D_FF), F32), jax.ShapeDtypeStruct((1, D_FF), F32)],
        compiler_params=_cparams("parallel", "arbitrary"))(dact, dact, hu, hu, hg, hg, x, x, x, x, w, w)


def _ssd_prep(pxs_ref, pb_ref, pc_ref, dtr_ref, dtb_ref, alog_ref, c):
    xs = _silu(pxs_ref[...])
    bm = _silu(pb_ref[...])
    cm = _silu(pc_ref[...])
    row = lax.broadcasted_iota(jnp.int32, (BLOCK, 1), 0) + c * BLOCK
    valid = (row >= PAD).astype(F32)
    dtr = dtr_ref[...] + dtb_ref[...]
    dt = _softplus(dtr) * valid
    a = -jnp.exp(alog_ref[...])
    lam = dt * a
    ri = lax.broadcasted_iota(jnp.int32, (BLOCK, BLOCK), 0)
    ci = lax.broadcasted_iota(jnp.int32, (BLOCK, BLOCK), 1)
    causal = ci <= ri
    cs = jnp.dot(causal.astype(F32), lam, precision=HIGHEST, preferred_element_type=F32)
    return xs, bm, cm, valid, dtr, dt, a, lam, cs, causal


def _head_cols(r):
    return slice(SSD_HEADDIM * r, SSD_HEADDIM * (r + 1))


def _ssd_specs(nc, rev):
    def cidx(c):
        return nc - 1 - c if rev else c

    xs = pl.BlockSpec((BLOCK, SSD_GW), lambda g, c: (cidx(c), g))
    bspec = pl.BlockSpec((BLOCK, SSD_STATE), lambda g, c: (cidx(c), SSD_INNER // SSD_STATE + g))
    cspec = pl.BlockSpec((BLOCK, SSD_STATE), lambda g, c: (cidx(c), (SSD_INNER + SSD_BC) // SSD_STATE + g))
    lane = pl.BlockSpec((BLOCK, LANES), lambda g, c: (cidx(c), g))
    vec = pl.BlockSpec((1, LANES), lambda g, c: (0, g))
    wide_vec = pl.BlockSpec((1, SSD_GW), lambda g, c: (0, g))
    hsave = pl.BlockSpec((1, 1, SSD_GW, SSD_STATE), lambda g, c: (cidx(c), g, 0, 0))
    return xs, bspec, cspec, lane, vec, wide_vec, hsave


def _ssd_fwd(pre, dt_raw, z, dtb, alog, dskip, norm_w):
    n = pre.shape[0]
    nc = n // BLOCK
    xs_s, b_s, c_s, lane_s, vec_s, wide_s, hs_s = _ssd_specs(nc, False)

    def body(pxs_ref, pb_ref, pc_ref, dtr_ref, z_ref, dtb_ref, alog_ref, dsk_ref, nw_ref,
             y_ref, yn_ref, hs_ref, h_scr):
        c = pl.program_id(1)

        @pl.when(c == 0)
        def _():
            h_scr[...] = jnp.zeros_like(h_scr)

        xs, bm, cm, _, _, dt, _, _, cs, causal = _ssd_prep(pxs_ref, pb_ref, pc_ref, dtr_ref, dtb_ref, alog_ref, c)
        cst = cs.T
        cs_last = cs[BLOCK - 1:BLOCK, :]
        bmb = bm.astype(BF16)
        cmb = cm.astype(BF16)
        cb = _dot_nt(cmb, bmb)
        hg = h_scr[...]
        hs_ref[0, 0] = hg
        yoff = _dot_nt(cmb, hg.astype(BF16))
        ys, xds = [], []
        for r in range(SSD_HPG):
            csc = cs[:, r:r + 1]
            lm = jnp.exp(jnp.where(causal, csc - cst[r:r + 1, :], NEG))
            x_r = xs[:, _head_cols(r)]
            xdt = x_r * dt[:, r:r + 1]
            ys.append(_dot((cb * lm).astype(BF16), xdt.astype(BF16)) + yoff[:, _head_cols(r)] * jnp.exp(csc)
                      + dsk_ref[0:1, r:r + 1] * x_r)
            xds.append(xdt * jnp.exp(cs_last[:, r:r + 1] - csc))
        y = jnp.concatenate(ys, axis=1)
        st = _dot_tn(jnp.concatenate(xds, axis=1).astype(BF16), bmb)
        for r in range(SSD_HPG):
            rows = _head_cols(r)
            h_scr[rows, :] = hg[rows, :] * jnp.exp(cs_last[:, r:r + 1]) + st[rows, :]
        y_ref[...] = y
        gts = y * _silu(z_ref[...])
        rr = lax.rsqrt(jnp.mean(gts * gts, axis=-1, keepdims=True) + EPS)
        yn_ref[...] = (gts * rr * nw_ref[...]).astype(BF16)

    return pl.pallas_call(
        body, name="ssd_fwd", grid=(SSD_GROUPS, nc),
        in_specs=[xs_s, b_s, c_s, lane_s, xs_s, vec_s, vec_s, vec_s, wide_s],
        out_specs=[xs_s, xs_s, hs_s],
        out_shape=[jax.ShapeDtypeStruct((n, SSD_INNER), F32), jax.ShapeDtypeStruct((n, SSD_INNER), BF16),
                   jax.ShapeDtypeStruct((nc, SSD_GROUPS, SSD_GW, SSD_STATE), F32)],
        scratch_shapes=[pltpu.VMEM((SSD_GW, SSD_STATE), F32)],
        compiler_params=_cparams("parallel", "arbitrary"))(pre, pre, pre, dt_raw, z, dtb, alog, dskip, norm_w)


def _lane_put(acc, col, r):
    lane = lax.broadcasted_iota(jnp.int32, acc.shape, 1)
    return jnp.where(lane == r, col, acc)


def _ssd_bwd(dyn, y, z, pre, dt_raw, hsave, dtb, alog, dskip, norm_w):
    n = pre.shape[0]
    nc = n // BLOCK
    xs_s, b_s, c_s, lane_s, vec_s, wide_s, hs_s = _ssd_specs(nc, True)
    bc_out = pl.BlockSpec((BLOCK, SSD_STATE), lambda g, c: (nc - 1 - c, g))

    def body(dyn_ref, y_ref, z_ref, pxs_ref, pb_ref, pc_ref, dtr_ref, hs_ref, dtb_ref, alog_ref, dsk_ref, nw_ref,
             dz_ref, dxs_ref, dbm_ref, dcm_ref, ddt_ref, dnw_ref, ddtb_ref, dalog_ref, ddsk_ref, g_scr):
        step = pl.program_id(1)
        c = nc - 1 - step

        @pl.when(step == 0)
        def _():
            g_scr[...] = jnp.zeros_like(g_scr)

        xs, bm, cm, valid, dtr, dt, a, lam, cs, causal = _ssd_prep(
            pxs_ref, pb_ref, pc_ref, dtr_ref, dtb_ref, alog_ref, c)
        cst = cs.T
        cs_last = cs[BLOCK - 1:BLOCK, :]
        bmb = bm.astype(BF16)
        cmb = cm.astype(BF16)
        cb = _dot_nt(cmb, bmb)
        hg = hs_ref[0, 0]
        hgb = hg.astype(BF16)
        yoff = _dot_nt(cmb, hgb)
        gn = g_scr[...]
        gnb = gn.astype(BF16)

        zv = z_ref[...]
        yv = y_ref[...]
        sz = _silu(zv)
        gts = yv * sz
        rr = lax.rsqrt(jnp.mean(gts * gts, axis=-1, keepdims=True) + EPS)
        xh = gts * rr
        dynv = dyn_ref[...]
        gg = dynv * nw_ref[...]
        dgts = rr * (gg - xh * jnp.mean(gg * xh, axis=-1, keepdims=True))
        dnw = jnp.sum(dynv * xh, axis=0, keepdims=True)
        dy = dgts * sz
        dz_ref[...] = (dgts * yv * _dsilu(zv)).astype(BF16)

        q_all = _dot_nt(bmb, gnb)
        zero_l = jnp.zeros((BLOCK, LANES), F32)
        dcs_col, ddt_x = zero_l, zero_l
        dcs_row = jnp.zeros((SUBLANES, BLOCK), F32)
        dcs_last = jnp.zeros((1, LANES), F32)
        ddsk = jnp.zeros((1, LANES), F32)
        dcb = jnp.zeros((BLOCK, BLOCK), F32)
        ws, xds, dxdts, decs = [], [], [], []
        for r in range(SSD_HPG):
            cols = _head_cols(r)
            csc = cs[:, r:r + 1]
            ecs = jnp.exp(csc)
            lm = jnp.exp(jnp.where(causal, csc - cst[r:r + 1, :], NEG))
            x_r = xs[:, cols]
            xdt = x_r * dt[:, r:r + 1]
            dy_r = dy[:, cols]
            dyb = dy_r.astype(BF16)
            dec = jnp.exp(cs_last[:, r:r + 1] - csc)
            ws.append(dy_r * ecs)
            col = jnp.sum(dy_r * yoff[:, cols], axis=1, keepdims=True) * ecs
            q_r = q_all[:, cols]
            e_r = jnp.sum(q_r * xdt, axis=1, keepdims=True) * dec
            col = col - e_r
            last = jnp.sum(e_r, axis=0, keepdims=True)
            rows = cols
            eh = jnp.exp(cs_last[:, r:r + 1])
            last = last + eh * _sum_all(gn[rows, :] * hg[rows, :])
            gm = _dot_nt(dyb, xdt.astype(BF16)) * lm
            dcb = dcb + gm
            mm_ = gm * cb
            col = col + jnp.sum(mm_, axis=1, keepdims=True)
            rowv = jnp.sum(mm_, axis=0, keepdims=True)
            sub = lax.broadcasted_iota(jnp.int32, (SUBLANES, BLOCK), 0)
            dcs_row = jnp.where(sub == r, rowv, dcs_row)
            dxdt = _dot_tn((cb * lm).astype(BF16), dyb) + q_r * dec
            dcs_col = _lane_put(dcs_col, col, r)
            dcs_last = _lane_put(dcs_last, last, r)
            ddt_x = _lane_put(ddt_x, jnp.sum(dxdt * x_r, axis=1, keepdims=True), r)
            ddsk = _lane_put(ddsk, _sum_all(dy_r * x_r), r)
            xds.append(xdt * dec)
            dxdts.append(dxdt * dt[:, r:r + 1] + dsk_ref[0:1, r:r + 1] * dy_r)
            decs.append(eh)
        w_all = jnp.concatenate(ws, axis=1).astype(BF16)
        xd_all = jnp.concatenate(xds, axis=1).astype(BF16)
        dcbb = dcb.astype(BF16)
        dcm = _dot(w_all, hgb) + _dot(dcbb, bmb)
        dbm = _dot(xd_all, gnb) + _dot_tn(dcbb, cmb)
        dh_off = _dot_tn(w_all, cmb)
        for r in range(SSD_HPG):
            rows = _head_cols(r)
            g_scr[rows, :] = gn[rows, :] * decs[r] + dh_off[rows, :]

        pad_rows = jnp.zeros((BLOCK - SUBLANES, BLOCK), F32)
        dcs = dcs_col - jnp.concatenate([dcs_row, pad_rows], axis=0).T
        rsel = lax.broadcasted_iota(jnp.int32, (BLOCK, LANES), 0)
        dcs = dcs + jnp.where(rsel == BLOCK - 1, dcs_last, 0.0)
        ri = lax.broadcasted_iota(jnp.int32, (BLOCK, BLOCK), 0)
        ci = lax.broadcasted_iota(jnp.int32, (BLOCK, BLOCK), 1)
        dlam = jnp.dot((ci >= ri).astype(F32), dcs, precision=HIGHEST, preferred_element_type=F32)
        lane = lax.broadcasted_iota(jnp.int32, (BLOCK, LANES), 1)
        head = lane < SSD_HPG
        ddt = dlam * a + ddt_x
        ddtr = jnp.where(head, ddt * jax.nn.sigmoid(dtr) * valid, 0.0)
        ddt_ref[...] = ddtr.astype(BF16)
        dalog = jnp.sum(jnp.where(head, dlam * lam, 0.0), axis=0, keepdims=True)
        ddtb = jnp.sum(ddtr, axis=0, keepdims=True)

        dxs_ref[...] = jnp.concatenate(dxdts, axis=1) * _dsilu(pxs_ref[...])
        dbm_ref[...] = dbm * _dsilu(pb_ref[...])
        dcm_ref[...] = dcm * _dsilu(pc_ref[...])

        @pl.when(step == 0)
        def _():
            dnw_ref[...] = dnw
            ddtb_ref[...] = ddtb
            dalog_ref[...] = dalog
            ddsk_ref[...] = ddsk

        @pl.when(step > 0)
        def _():
            dnw_ref[...] += dnw
            ddtb_ref[...] += ddtb
            dalog_ref[...] += dalog
            ddsk_ref[...] += ddsk

    return pl.pallas_call(
        body, name="ssd_bwd", grid=(SSD_GROUPS, nc),
        in_specs=[xs_s, xs_s, xs_s, xs_s, b_s, c_s, lane_s, hs_s, vec_s, vec_s, vec_s, wide_s],
        out_specs=[xs_s, xs_s, bc_out, bc_out, lane_s, wide_s, vec_s, vec_s, vec_s],
        out_shape=[jax.ShapeDtypeStruct((n, SSD_INNER), BF16), jax.ShapeDtypeStruct((n, SSD_INNER), F32),
                   jax.ShapeDtypeStruct((n, SSD_BC), F32), jax.ShapeDtypeStruct((n, SSD_BC), F32),
                   jax.ShapeDtypeStruct((n, DT_W), BF16), jax.ShapeDtypeStruct((1, SSD_INNER), F32),
                   jax.ShapeDtypeStruct((1, DT_W), F32), jax.ShapeDtypeStruct((1, DT_W), F32),
                   jax.ShapeDtypeStruct((1, DT_W), F32)],
        scratch_shapes=[pltpu.VMEM((SSD_GW, SSD_STATE), F32)],
        compiler_params=_cparams("parallel", "arbitrary"))(
            dyn, y, z, pre, pre, pre, dt_raw, hsave, dtb, alog, dskip, norm_w)


def _bucket_table():
    def bucket(dist):
        d = np.maximum(dist, 0)
        half = REL_BUCKETS // 2
        big = half + (np.log(np.maximum(d, half).astype(np.float32) / np.float32(half))
                      / np.float32(math.log(REL_MAX_DIST / half)) * np.float32(REL_BUCKETS - half)).astype(np.int32)
        return np.where(d < half, d, np.minimum(big, REL_BUCKETS - 1)).astype(np.int32)

    l = np.arange(BLOCK)[:, None]
    band = bucket(l + BLOCK - np.arange(2 * BLOCK)[None, :])
    j = np.arange(BLOCK)[None, :]
    metas = [bucket(l - j), bucket(BLOCK + l - j), bucket(2 * BLOCK + l - j)]
    return np.concatenate([band.reshape(-1)] + [m.reshape(-1) for m in metas])


def _onehot_t():
    buckets = jnp.asarray(_bucket_table())
    return (buckets[None, :] == jnp.arange(REL_BUCKETS, dtype=jnp.int32)[:, None]).astype(F32)


def _bias_tables(rel_t, onehot_t):
    def body(r_ref, oh_ref, o_ref):
        o_ref[...] = jnp.dot(r_ref[...], oh_ref[...], precision=HIGHEST, preferred_element_type=F32)

    return pl.pallas_call(
        body, name="bias_tables", grid=(NT_ALL // NT_TILE,),
        in_specs=[pl.BlockSpec((ATT_HEADS, REL_BUCKETS), lambda i: (0, 0)),
                  pl.BlockSpec((REL_BUCKETS, NT_TILE), lambda i: (0, i))],
        out_specs=pl.BlockSpec((ATT_HEADS, NT_TILE), lambda i: (0, i)),
        out_shape=jax.ShapeDtypeStruct((ATT_HEADS, NT_ALL), F32),
        compiler_params=_cparams("parallel"))(rel_t, onehot_t)


def _bias_grad(dtab, onehot_t):
    def body(d_ref, oh_ref, o_ref):
        i = pl.program_id(0)
        p = lax.dot_general(d_ref[...], oh_ref[...], (((1,), (1,)), ((), ())), precision=HIGHEST,
                            preferred_element_type=F32)

        @pl.when(i == 0)
        def _():
            o_ref[...] = p

        @pl.when(i > 0)
        def _():
            o_ref[...] += p

    return pl.pallas_call(
        body, name="bias_grad", grid=(NT_ALL // NT_TILE,),
        in_specs=[pl.BlockSpec((ATT_HEADS, NT_TILE), lambda i: (0, i)),
                  pl.BlockSpec((REL_BUCKETS, NT_TILE), lambda i: (0, i))],
        out_specs=pl.BlockSpec((ATT_HEADS, REL_BUCKETS), lambda i: (0, 0)),
        out_shape=jax.ShapeDtypeStruct((ATT_HEADS, REL_BUCKETS), F32),
        compiler_params=_cparams("arbitrary"))(dtab, onehot_t)


def _att_masks(n):
    far = 4 * BLOCK
    li = lax.broadcasted_iota(jnp.int32, (BLOCK, BLOCK), 0)
    ki = lax.broadcasted_iota(jnp.int32, (BLOCK, BLOCK), 1)
    m_meta = (ki >= PAD) & (li + jnp.where(n >= 1, far, 0) >= ki)
    li2 = lax.broadcasted_iota(jnp.int32, (BLOCK, 2 * BLOCK), 0)
    ki2 = lax.broadcasted_iota(jnp.int32, (BLOCK, 2 * BLOCK), 1)
    prev_ok = (ki2 < BLOCK) & (ki2 > li2 + jnp.where(n >= 2, 0, far))
    cur_ok = (ki2 >= BLOCK) & (ki2 - BLOCK <= li2 - jnp.where(n >= 1, 0, far))
    return m_meta, prev_ok | cur_ok


def _att_probs(qh, k_meta, k_band, b_meta, b_band, m_meta, m_band, sink):
    scale = ATT_HEADDIM ** -0.5
    s_m = jnp.where(m_meta, _dot_nt(qh, k_meta) * scale + b_meta, NEG)
    s_b = jnp.where(m_band, _dot_nt(qh, k_band) * scale + b_band, NEG)
    mx = jnp.maximum(jnp.maximum(jnp.max(s_m, axis=1, keepdims=True), jnp.max(s_b, axis=1, keepdims=True)), sink)
    p_m = jnp.exp(s_m - mx)
    p_b = jnp.exp(s_b - mx)
    p_s = jnp.exp(sink - mx)
    inv = 1.0 / (jnp.sum(p_m, axis=1, keepdims=True) + jnp.sum(p_b, axis=1, keepdims=True) + p_s)
    return p_m * inv, p_b * inv, p_s * inv


def _kv_cols(kind, kh):
    base = ATT_KV * kind + ATT_HEADDIM * kh
    return slice(base, base + ATT_HEADDIM)


def _att_specs(nb, rev):
    def nidx(i):
        return nb - 1 - i if rev else i

    kvb = ATT_Q // (2 * ATT_KV)
    q_s = pl.BlockSpec((BLOCK, ATT_Q), lambda i: (nidx(i), 0))
    cur = pl.BlockSpec((BLOCK, 2 * ATT_KV), lambda i: (nidx(i), kvb))
    prev = pl.BlockSpec((BLOCK, 2 * ATT_KV), lambda i: (jnp.maximum(nidx(i) - 1, 0), kvb))
    meta = pl.BlockSpec((BLOCK, 2 * ATT_KV), lambda i: (0, kvb))
    tmeta = pl.BlockSpec((1, ATT_HEADS, BLOCK, BLOCK), lambda i: (jnp.minimum(nidx(i), 2), 0, 0, 0))
    tband = pl.BlockSpec((ATT_HEADS, BLOCK, 2 * BLOCK), lambda i: (0, 0, 0))
    sink = pl.BlockSpec((1, LANES), lambda i: (0, 0))
    return q_s, cur, prev, meta, tmeta, tband, sink


def _attn_fwd(qkv, t_meta, t_band, sinks):
    n = qkv.shape[0]
    nb = n // BLOCK
    q_s, cur_s, prev_s, meta_s, tm_s, tb_s, sink_s = _att_specs(nb, False)

    def body(q_ref, cur_ref, prev_ref, meta_ref, tm_ref, tb_ref, sink_ref, o_ref):
        blk = pl.program_id(0)
        m_meta, m_band = _att_masks(blk)
        kv_band = jnp.concatenate([prev_ref[...], cur_ref[...]], axis=0).astype(BF16)
        kv_meta = meta_ref[...].astype(BF16)
        for kh in range(ATT_KV_HEADS):
            k_meta, v_meta = kv_meta[:, _kv_cols(0, kh)], kv_meta[:, _kv_cols(1, kh)]
            k_band, v_band = kv_band[:, _kv_cols(0, kh)], kv_band[:, _kv_cols(1, kh)]
            for gq in range(ATT_GQ):
                h = kh * ATT_GQ + gq
                cols = slice(ATT_HEADDIM * h, ATT_HEADDIM * (h + 1))
                qh = q_ref[:, cols].astype(BF16)
                p_m, p_b, _ = _att_probs(qh, k_meta, k_band, tm_ref[0, h], tb_ref[h], m_meta, m_band,
                                         sink_ref[0:1, h:h + 1])
                o_ref[:, cols] = (_dot(p_m.astype(BF16), v_meta) + _dot(p_b.astype(BF16), v_band)).astype(BF16)

    return pl.pallas_call(
        body, name="attn_fwd", grid=(nb,),
        in_specs=[q_s, cur_s, prev_s, meta_s, tm_s, tb_s, sink_s],
        out_specs=q_s,
        out_shape=jax.ShapeDtypeStruct((n, ATT_Q), BF16),
        compiler_params=_cparams("parallel"))(qkv, qkv, qkv, qkv, t_meta, t_band, sinks)


def _attn_bwd(datt, qkv, t_meta, t_band, sinks):
    n = qkv.shape[0]
    nb = n // BLOCK
    q_s, cur_s, prev_s, meta_s, tm_s, tb_s, sink_s = _att_specs(nb, True)
    dqkv_s = pl.BlockSpec((BLOCK, ATT_Q + 2 * ATT_KV), lambda i: (nb - 1 - i, 0))
    scale = ATT_HEADDIM ** -0.5

    def body(do_ref, q_ref, cur_ref, prev_ref, meta_ref, tm_ref, tb_ref, sink_ref,
             dqkv_ref, dtm_ref, dtb_ref, dsink_ref, carry_scr, meta_scr):
        step = pl.program_id(0)
        blk = nb - 1 - step
        m_meta, m_band = _att_masks(blk)
        kv_band = jnp.concatenate([prev_ref[...], cur_ref[...]], axis=0).astype(BF16)
        kv_meta = meta_ref[...].astype(BF16)

        @pl.when(step == 0)
        def _():
            carry_scr[...] = jnp.zeros_like(carry_scr)
            meta_scr[...] = jnp.zeros_like(meta_scr)
            dtb_ref[...] = jnp.zeros_like(dtb_ref)
            dsink_ref[...] = jnp.zeros_like(dsink_ref)

        @pl.when((step == 0) | (blk <= 1))
        def _():
            dtm_ref[...] = jnp.zeros_like(dtm_ref)

        dsink = jnp.zeros((1, LANES), F32)
        dkv_band = [None] * (2 * ATT_KV_HEADS)
        dkv_meta = [None] * (2 * ATT_KV_HEADS)
        for kh in range(ATT_KV_HEADS):
            k_meta, v_meta = kv_meta[:, _kv_cols(0, kh)], kv_meta[:, _kv_cols(1, kh)]
            k_band, v_band = kv_band[:, _kv_cols(0, kh)], kv_band[:, _kv_cols(1, kh)]
            dk_m = dv_m = dk_b = dv_b = None
            for gq in range(ATT_GQ):
                h = kh * ATT_GQ + gq
                cols = slice(ATT_HEADDIM * h, ATT_HEADDIM * (h + 1))
                qh = q_ref[:, cols].astype(BF16)
                doh = do_ref[:, cols].astype(BF16)
                p_m, p_b, p_s = _att_probs(qh, k_meta, k_band, tm_ref[0, h], tb_ref[h], m_meta, m_band,
                                           sink_ref[0:1, h:h + 1])
                dp_m = _dot_nt(doh, v_meta)
                dp_b = _dot_nt(doh, v_band)
                delta = jnp.sum(p_m * dp_m, axis=1, keepdims=True) + jnp.sum(p_b * dp_b, axis=1, keepdims=True)
                ds_m = p_m * (dp_m - delta)
                ds_b = p_b * (dp_b - delta)
                dsink = _lane_put(dsink, dsink[0:1, h:h + 1] - jnp.sum(p_s * delta, axis=0, keepdims=True), h)
                dtm_ref[0, h] += ds_m
                dtb_ref[h] += ds_b
                ds_mb, ds_bb = ds_m.astype(BF16), ds_b.astype(BF16)
                dqkv_ref[:, cols] = ((_dot(ds_mb, k_meta) + _dot(ds_bb, k_band)) * scale).astype(BF16)
                parts = (_dot_tn(ds_mb, qh) * scale, _dot_tn(p_m.astype(BF16), doh),
                         _dot_tn(ds_bb, qh) * scale, _dot_tn(p_b.astype(BF16), doh))
                if gq == 0:
                    dk_m, dv_m, dk_b, dv_b = parts
                else:
                    dk_m, dv_m, dk_b, dv_b = dk_m + parts[0], dv_m + parts[1], dk_b + parts[2], dv_b + parts[3]
            dkv_meta[kh], dkv_meta[ATT_KV_HEADS + kh] = dk_m, dv_m
            dkv_band[kh], dkv_band[ATT_KV_HEADS + kh] = dk_b, dv_b
        dsink_ref[...] += dsink
        band = jnp.concatenate(dkv_band, axis=1)
        meta_scr[...] += jnp.concatenate(dkv_meta, axis=1)
        own = band[BLOCK:, :] + carry_scr[...]
        carry_scr[...] = band[:BLOCK, :]

        @pl.when(blk > 0)
        def _():
            dqkv_ref[:, ATT_Q:] = own.astype(BF16)

        @pl.when(blk == 0)
        def _():
            dqkv_ref[:, ATT_Q:] = (own + meta_scr[...]).astype(BF16)

    return pl.pallas_call(
        body, name="attn_bwd", grid=(nb,),
        in_specs=[q_s, q_s, cur_s, prev_s, meta_s, tm_s, tb_s, sink_s],
        out_specs=[dqkv_s, tm_s, tb_s, sink_s],
        out_shape=[jax.ShapeDtypeStruct((n, ATT_Q + 2 * ATT_KV), BF16),
                   jax.ShapeDtypeStruct((3, ATT_HEADS, BLOCK, BLOCK), F32),
                   jax.ShapeDtypeStruct((ATT_HEADS, BLOCK, 2 * BLOCK), F32),
                   jax.ShapeDtypeStruct((1, LANES), F32)],
        scratch_shapes=[pltpu.VMEM((BLOCK, 2 * ATT_KV), F32), pltpu.VMEM((BLOCK, 2 * ATT_KV), F32)],
        compiler_params=_cparams("arbitrary"))(datt, qkv, qkv, qkv, qkv, t_meta, t_band, sinks)


def _merge_fwd(gates, y_ssd, y_att, gate_b):
    n = gates.shape[0]
    tm = _row_tile(n, 832)

    def body(gs_ref, ga_ref, ys_ref, ya_ref, gb_ref, o_ref):
        o_ref[...] = (jax.nn.sigmoid(gs_ref[...] + gb_ref[0:1, :]) * ys_ref[...]
                      + jax.nn.sigmoid(ga_ref[...] + gb_ref[1:2, :]) * ya_ref[...]).astype(BF16)

    row = pl.BlockSpec((tm, D_MODEL), lambda i: (i, 0))
    return pl.pallas_call(
        body, name="merge_fwd", grid=(n // tm,),
        in_specs=[row, pl.BlockSpec((tm, D_MODEL), lambda i: (i, 1)), row, row,
                  pl.BlockSpec((2, D_MODEL), lambda i: (0, 0))],
        out_specs=row, out_shape=jax.ShapeDtypeStruct((n, D_MODEL), BF16),
        compiler_params=_cparams("parallel"))(gates, gates, y_ssd, y_att, gate_b)


def _merge_bwd(dm, gates, y_ssd, y_att, gate_b):
    n = gates.shape[0]
    tm = _row_tile(n, 832)

    def body(dm_ref, gs_ref, ga_ref, ys_ref, ya_ref, gb_ref, dys_ref, dya_ref, dg_ref, dgb_ref):
        i = pl.program_id(0)
        dmv = dm_ref[...]
        ss = jax.nn.sigmoid(gs_ref[...] + gb_ref[0:1, :])
        sa = jax.nn.sigmoid(ga_ref[...] + gb_ref[1:2, :])
        dys_ref[...] = (dmv * ss).astype(BF16)
        dya_ref[...] = (dmv * sa).astype(BF16)
        dgs = dmv * ys_ref[...] * ss * (1.0 - ss)
        dga = dmv * ya_ref[...] * sa * (1.0 - sa)
        dg_ref[:, :D_MODEL] = dgs.astype(BF16)
        dg_ref[:, D_MODEL:] = dga.astype(BF16)
        part = jnp.concatenate([jnp.sum(dgs, axis=0, keepdims=True), jnp.sum(dga, axis=0, keepdims=True)], axis=0)

        @pl.when(i == 0)
        def _():
            dgb_ref[...] = part

        @pl.when(i > 0)
        def _():
            dgb_ref[...] += part

    row = pl.BlockSpec((tm, D_MODEL), lambda i: (i, 0))
    gb = pl.BlockSpec((2, D_MODEL), lambda i: (0, 0))
    return pl.pallas_call(
        body, name="merge_bwd", grid=(n // tm,),
        in_specs=[row, row, pl.BlockSpec((tm, D_MODEL), lambda i: (i, 1)), row, row, gb],
        out_specs=[row, row, pl.BlockSpec((tm, 2 * D_MODEL), lambda i: (i, 0)), gb],
        out_shape=[jax.ShapeDtypeStruct((n, D_MODEL), BF16), jax.ShapeDtypeStruct((n, D_MODEL), BF16),
                   jax.ShapeDtypeStruct((n, 2 * D_MODEL), BF16), jax.ShapeDtypeStruct((2, D_MODEL), F32)],
        compiler_params=_cparams("arbitrary"))(dm, gates, gates, y_ssd, y_att, gate_b)


def _col_move(srcs, outs, pieces, *, name):
    rows = srcs[0].shape[-2]
    tr = _row_tile(rows, 128)
    n_src = len(srcs)
    covered = [sum(p[6] for p in pieces if p[0] == o) for o in range(len(outs))]
    total = [int(np.prod(shp)) // rows for shp, _ in outs]

    def body(*refs):
        in_refs, out_refs = refs[:n_src], refs[n_src:]
        for o, ref in enumerate(out_refs):
            if covered[o] != total[o]:
                ref[...] = jnp.zeros_like(ref)
        for o, ol, oc, s, sl, sc, width in pieces:
            val = in_refs[s][:, sc:sc + width] if sl is None else in_refs[s][sl, :, sc:sc + width]
            val = val.astype(outs[o][1])
            if ol is None:
                out_refs[o][:, oc:oc + width] = val
            else:
                out_refs[o][ol, :, oc:oc + width] = val

    def spec(shape):
        if len(shape) == 2:
            return pl.BlockSpec((tr, shape[1]), lambda i: (i, 0))
        return pl.BlockSpec((shape[0], tr, shape[2]), lambda i: (0, i, 0))

    return pl.pallas_call(
        body, name=name, grid=(rows // tr,),
        in_specs=[spec(a.shape) for a in srcs], out_specs=[spec(shp) for shp, _ in outs],
        out_shape=[jax.ShapeDtypeStruct(shp, dt) for shp, dt in outs],
        compiler_params=_cparams("parallel"))(*srcs)


def _shard_pieces(seg_ranges, shard_w):
    out = []
    for seg, runs in enumerate(seg_ranges):
        for g0, width, s0 in runs:
            done = 0
            while done < width:
                dev, col = divmod(g0 + done, shard_w)
                take = min(width - done, shard_w - col)
                out.append((seg, s0 + done, dev, col, take))
                done += take
    return out


_CHIP_RELATIONS = [(1, 0, 0), (0, 1, 0), (1, 1, 0)]
N_CHIPS = 4


def _gather_two_level(arrays, *, name):
    n_arr = len(arrays)
    n_pair = 1 + 2 * len(_CHIP_RELATIONS)

    def body(*refs):
        ins, outs = refs[:n_arr], refs[n_arr:2 * n_arr]
        send_sems, recv_sems, local_sems = refs[2 * n_arr:]
        x, y, c = lax.axis_index("x"), lax.axis_index("y"), lax.axis_index("c")
        sibling = (x, y, 1 - c)
        chips = [(x ^ dx, y ^ dy) for dx, dy, _ in _CHIP_RELATIONS]

        def copy(a, k, block, to, src=None):
            slot = outs[a].at[2 * block[0] + block[1], block[2]]
            return pltpu.make_async_remote_copy(
                src_ref=slot if src is None else src, dst_ref=slot, send_sem=send_sems.at[a * n_pair + k],
                recv_sem=recv_sems.at[a * n_pair + k], device_id=to, device_id_type=MESH)

        sends, locals_ = [], []
        for a in range(n_arr):
            mine = pltpu.make_async_copy(ins[a], outs[a].at[2 * x + y, c], local_sems.at[a])
            mine.start()
            locals_.append(mine)
            first = [copy(a, 0, (x, y, c), sibling, src=ins[a])]
            first += [copy(a, 1 + j, (x, y, c), (*chip, c), src=ins[a]) for j, chip in enumerate(chips)]
            for cp in first:
                cp.start()
            sends += first
        for j, chip in enumerate(chips):
            for a in range(n_arr):
                copy(a, 1 + j, (*chip, c), (x, y, c)).wait_recv()
                passed = copy(a, 1 + len(chips) + j, (*chip, c), sibling)
                passed.start()
                sends.append(passed)
        for a in range(n_arr):
            copy(a, 0, (x, y, 1 - c), (x, y, c)).wait_recv()
            for j, chip in enumerate(chips):
                copy(a, 1 + len(chips) + j, (*chip, 1 - c), (x, y, c)).wait_recv()
        for cp in sends:
            cp.wait_send()
        for mine in locals_:
            mine.wait()

    any_spec = pl.BlockSpec(memory_space=pl.ANY)
    outs = pl.pallas_call(
        body, name=name, in_specs=[any_spec] * n_arr, out_specs=[any_spec] * n_arr,
        out_shape=[jax.ShapeDtypeStruct((N_CHIPS, 2) + a.shape, a.dtype) for a in arrays],
        scratch_shapes=[pltpu.SemaphoreType.DMA((n_arr * n_pair,)), pltpu.SemaphoreType.DMA((n_arr * n_pair,)),
                        pltpu.SemaphoreType.DMA((n_arr,))],
    )(*arrays)
    return [o.reshape((N_DEV,) + a.shape) for o, a in zip(outs, arrays)]


def _sibling_exchange(arrays, scatter, *, name):
    n_arr = len(arrays)

    def body(*refs):
        ins, outs = refs[:n_arr], refs[n_arr:2 * n_arr]
        send_sems, recv_sems = refs[2 * n_arr:]
        x, y, c = lax.axis_index("x"), lax.axis_index("y"), lax.axis_index("c")
        copies = []
        for a in range(n_arr):
            for q in range(N_CHIPS if scatter[a] else 1):
                src = ins[a].at[2 * q + 1 - c] if scatter[a] else ins[a]
                dst = outs[a].at[q] if scatter[a] else outs[a]
                cp = pltpu.make_async_remote_copy(
                    src_ref=src, dst_ref=dst, send_sem=send_sems.at[a * N_CHIPS + q],
                    recv_sem=recv_sems.at[a * N_CHIPS + q], device_id=(x, y, 1 - c), device_id_type=MESH)
                cp.start()
                copies.append(cp)
        for cp in copies:
            cp.wait_send()
        for cp in copies:
            cp.wait_recv()

    any_spec = pl.BlockSpec(memory_space=pl.ANY)
    return pl.pallas_call(
        body, name=name, in_specs=[any_spec] * n_arr, out_specs=[any_spec] * n_arr,
        out_shape=[jax.ShapeDtypeStruct(((N_CHIPS,) + a.shape[1:]) if s else a.shape, a.dtype)
                   for a, s in zip(arrays, scatter)],
        scratch_shapes=[pltpu.SemaphoreType.DMA((n_arr * N_CHIPS,)), pltpu.SemaphoreType.DMA((n_arr * N_CHIPS,))],
    )(*arrays)


def _pair_sum(mine, sib, *, name, out_dtype):
    _, rows, cols = mine.shape
    tr = _row_tile(rows, 128)

    def body(m_ref, s_ref, o_ref):
        c = lax.axis_index("c")
        o_ref[0] = (m_ref[0, c] + s_ref[0]).astype(out_dtype)

    return pl.pallas_call(
        body, name=name, grid=(N_CHIPS, rows // tr),
        in_specs=[pl.BlockSpec((1, 2, tr, cols), lambda q, i: (q, 0, i, 0)),
                  pl.BlockSpec((1, tr, cols), lambda q, i: (q, i, 0))],
        out_specs=pl.BlockSpec((1, tr, cols), lambda q, i: (q, i, 0)),
        out_shape=jax.ShapeDtypeStruct((N_CHIPS, rows, cols), out_dtype),
        compiler_params=_cparams("parallel", "parallel"))(mine.reshape(N_CHIPS, 2, rows, cols), sib)


def _add(a, b, *, name):
    rows, cols = a.shape
    tr = _row_tile(rows, 256)

    def body(a_ref, b_ref, o_ref):
        o_ref[...] = a_ref[...] + b_ref[...]

    blk = pl.BlockSpec((tr, cols), lambda i: (i, 0))
    return pl.pallas_call(body, name=name, grid=(rows // tr,), in_specs=[blk, blk], out_specs=blk,
                          out_shape=jax.ShapeDtypeStruct(a.shape, a.dtype), compiler_params=_cparams("parallel"))(a, b)


def _chip_exchange(arrays, scatter, *, name):
    n_arr = len(arrays)
    n_rel = len(_CHIP_RELATIONS)

    def body(*refs):
        ins, outs = refs[:n_arr], refs[n_arr:2 * n_arr]
        send_sems, recv_sems, local_sems = refs[2 * n_arr:]
        x, y, c = lax.axis_index("x"), lax.axis_index("y"), lax.axis_index("c")
        me = 2 * x + y
        copies = []
        for a in range(n_arr):
            src = ins[a].at[me] if scatter[a] else ins[a]
            local = pltpu.make_async_copy(src, outs[a].at[me], local_sems.at[a])
            local.start()
            copies.append(local)
        remote = []
        for k, (dx, dy, dc) in enumerate(_CHIP_RELATIONS):
            px, py, pc = x ^ dx, y ^ dy, c ^ dc
            peer = 2 * px + py
            for a in range(n_arr):
                src = ins[a].at[peer] if scatter[a] else ins[a]
                cp = pltpu.make_async_remote_copy(
                    src_ref=src, dst_ref=outs[a].at[me], send_sem=send_sems.at[a * n_rel + k],
                    recv_sem=recv_sems.at[a * n_rel + k], device_id=(px, py, pc), device_id_type=MESH)
                cp.start()
                remote.append((cp, a, k, peer))
        for cp, a, k, peer in remote:
            cp.wait_send()
        for cp, a, k, peer in remote:
            src = ins[a].at[peer] if scatter[a] else ins[a]
            pltpu.make_async_remote_copy(
                src_ref=src, dst_ref=outs[a].at[peer], send_sem=send_sems.at[a * n_rel + k],
                recv_sem=recv_sems.at[a * n_rel + k], device_id=(x, y, c), device_id_type=MESH).wait_recv()
        for local in copies:
            local.wait()

    out_shape = [jax.ShapeDtypeStruct((N_CHIPS,) + (a.shape[1:] if s else a.shape), a.dtype)
                 for a, s in zip(arrays, scatter)]
    any_spec = pl.BlockSpec(memory_space=pl.ANY)
    return pl.pallas_call(
        body, name=name, in_specs=[any_spec] * n_arr, out_specs=[any_spec] * n_arr, out_shape=out_shape,
        scratch_shapes=[pltpu.SemaphoreType.DMA((n_arr * n_rel,)), pltpu.SemaphoreType.DMA((n_arr * n_rel,)),
                        pltpu.SemaphoreType.DMA((n_arr,))],
    )(*arrays)


def _adamw(w, gslots, m, v, *, name):
    rows, cols = w.shape
    n_slots = gslots.shape[0]
    tr = _row_tile(rows, 128) if rows % 16 == 0 else rows

    def body(w_ref, g_ref, m_ref, v_ref, go_ref, d_ref, mo_ref, vo_ref):
        g = g_ref[0].astype(F32)
        for s in range(1, n_slots):
            g = g + g_ref[s].astype(F32)
        mn = ADAM_B1 * m_ref[...] + (1.0 - ADAM_B1) * g
        vn = ADAM_B2 * v_ref[...] + (1.0 - ADAM_B2) * (g * g)
        go_ref[...] = g
        mo_ref[...] = mn
        vo_ref[...] = vn
        m_hat = mn / (1.0 - ADAM_B1 ** ADAM_STEP)
        v_hat = vn / (1.0 - ADAM_B2 ** ADAM_STEP)
        d_ref[...] = -ADAM_LR * (m_hat / (jnp.sqrt(v_hat) + ADAM_EPS) + ADAM_WD * w_ref[...])

    blk = pl.BlockSpec((tr, cols), lambda i: (i, 0))
    shp = jax.ShapeDtypeStruct((rows, cols), F32)
    return pl.pallas_call(
        body, name=name, grid=(rows // tr,),
        in_specs=[blk, pl.BlockSpec((n_slots, tr, cols), lambda i: (0, i, 0)), blk, blk],
        out_specs=[blk] * 4, out_shape=[shp] * 4,
        compiler_params=_cparams("parallel"))(w, gslots, m, v)


_BIG = ("w_in", "w_ssd_branch", "w_attn_branch", "w_out", "w_ffn_in", "w_ffn_out")
_SMALL_SHARDED = ("meta_tokens", "ssd_conv_w", "gate_b", "ffn_conv_w")
_SMALL_REPLICATED = ("norm_mix_w", "ssd_conv_b", "ssd_dt_bias", "ssd_a_log", "ssd_d", "ssd_norm_w", "attn_sinks",
                     "rel_bias", "norm_ffn_w", "ffn_conv_b", "norm_final_w")
_WEIGHTS = ("meta_tokens", "norm_mix_w", "w_in", "ssd_conv_w", "ssd_conv_b", "ssd_dt_bias", "ssd_a_log", "ssd_d",
            "ssd_norm_w", "w_ssd_branch", "w_attn_branch", "attn_sinks", "rel_bias", "gate_b", "w_out", "norm_ffn_w",
            "w_ffn_in", "ffn_conv_w", "ffn_conv_b", "w_ffn_out", "norm_final_w")
_ROW_SHARDED = ("w_ssd_branch", "w_attn_branch", "w_out", "w_ffn_out")
_COL_SHARDED = ("w_in", "w_ffn_in", "meta_tokens", "ssd_conv_w", "gate_b", "ffn_conv_w")
_IN_SEGS = (("z", SSD_INNER), ("xbc", SSD_XBC), ("dt", SSD_HEADS), ("qkv", ATT_Q + 2 * ATT_KV), ("g", 2 * D_MODEL))


def _pack_rows(flat_parts, width, row_mult):
    flat = jnp.concatenate([p.reshape(-1) for p in flat_parts])
    pad = (-flat.shape[0]) % (width * row_mult)
    if pad:
        flat = jnp.concatenate([flat, jnp.zeros((pad,), flat.dtype)])
    return flat.reshape(-1, width)


def _unpack(flat, shapes):
    out, off = [], 0
    for shp in shapes:
        size = int(np.prod(shp))
        out.append(flat[off:off + size].reshape(shp))
        off += size
    return out


def _gather_full(stack, name, shard_shape):
    if name in _COL_SHARDED:
        return jnp.transpose(stack, (1, 0, 2)).reshape(shard_shape[0], N_DEV * shard_shape[1])
    return stack.reshape(N_DEV * shard_shape[0], shard_shape[1])


_IN_SEG_W = {"z": SSD_INNER, "xbc": SSD_XBC, "dt": DT_W, "qkv": ATT_Q + 2 * ATT_KV, "g": 2 * D_MODEL}
_IN_SHARD_W = (SSD_INNER + SSD_XBC + SSD_HEADS + ATT_Q + 2 * ATT_KV + 2 * D_MODEL) // N_DEV
_FFN_SHARD_W = 2 * D_FF // N_DEV


def _in_seg_runs():
    runs, off = [], 0
    for nm, width in _IN_SEGS:
        if nm == "dt":
            runs.append([(off + SSD_HPG * g, SSD_HPG, LANES * g) for g in range(SSD_GROUPS)])
        else:
            runs.append([(off, width, 0)])
        off += width
    return runs


def _w_in_to_segments(stack):
    pieces = [(seg, None, scol, 0, dev, col, w) for seg, scol, dev, col, w in _shard_pieces(_in_seg_runs(), _IN_SHARD_W)]
    outs = [((D_MODEL, _IN_SEG_W[nm]), stack.dtype) for nm, _ in _IN_SEGS]
    return dict(zip([nm for nm, _ in _IN_SEGS], _col_move([stack], outs, pieces, name="w_in_segments")))


def _segments_to_w_in_shards(seg_grads):
    pieces = [(0, dev, col, seg, None, scol, w) for seg, scol, dev, col, w in _shard_pieces(_in_seg_runs(), _IN_SHARD_W)]
    return _col_move(seg_grads, [((N_DEV, D_MODEL, _IN_SHARD_W), F32)], pieces, name="g_w_in_shards")[0]


def _ffn_in_from_shards(stack):
    pieces = [(0, None, scol, 0, dev, col, w)
              for _, scol, dev, col, w in _shard_pieces([[(0, 2 * D_FF, 0)]], _FFN_SHARD_W)]
    return _col_move([stack], [((D_MODEL, 2 * D_FF), stack.dtype)], pieces, name="w_ffn_in_full")[0]


def _ffn_in_to_shards(g_up, g_gate):
    pieces = [(0, dev, col, seg, None, scol, w)
              for seg, scol, dev, col, w in _shard_pieces([[(0, D_FF, 0)], [(D_FF, D_FF, 0)]], _FFN_SHARD_W)]
    return _col_move([g_up, g_gate], [((N_DEV, D_MODEL, _FFN_SHARD_W), F32)], pieces, name="g_w_ffn_in_shards")[0]


def _dt_spread(w_dt):
    k = w_dt.shape[0]
    w4 = w_dt.reshape(k, SSD_GROUPS, SSD_HPG)
    return jnp.pad(w4, ((0, 0), (0, 0), (0, LANES - SSD_HPG))).reshape(k, DT_W)


def _dt_gather(w_wide):
    k = w_wide.shape[0]
    return w_wide.reshape(k, SSD_GROUPS, LANES)[:, :, :SSD_HPG].reshape(k, SSD_HEADS)


def _local_step(x, target, w):
    h0 = jnp.concatenate([jnp.zeros((PAD, D_MODEL), F32), w["meta_tokens"], x], axis=0)
    segs = w["in_segs"]
    w_ffn_up, w_ffn_gate = w["w_ffn_in"][:, :D_FF], w["w_ffn_in"][:, D_FF:]

    dtb = _dt_spread(w["ssd_dt_bias"])
    alog = _dt_spread(w["ssd_a_log"])
    dskip = _dt_spread(w["ssd_d"])
    sinks = jnp.pad(w["attn_sinks"], ((0, 0), (0, LANES - ATT_HEADS)))
    onehot_t = _onehot_t()
    tabs = _bias_tables(w["rel_bias"].T, onehot_t)
    t_band = tabs[:, :NT_BAND].reshape(ATT_HEADS, BLOCK, 2 * BLOCK)
    t_meta = jnp.transpose(tabs[:, NT_BAND:].reshape(ATT_HEADS, 3, BLOCK, BLOCK), (1, 0, 2, 3))

    u = _rms_fwd(h0, w["norm_mix_w"], name="rms_mix_fwd")
    z = _mm(u, segs["z"], name="in_z")
    xbc = _mm(u, segs["xbc"], name="in_xbc")
    dt_raw = _mm(u, segs["dt"], name="in_dt")
    qkv = _mm(u, segs["qkv"], name="in_qkv")
    gates = _mm(u, segs["g"], name="in_g")
    pre = _conv_fwd(xbc, w["ssd_conv_w"], w["ssd_conv_b"], name="ssd_conv_fwd")
    y, yn, hsave = _ssd_fwd(pre, dt_raw, z, dtb, alog, dskip, w["ssd_norm_w"])
    y_ssd = _mm(yn, w["w_ssd_branch"], name="ssd_out")
    att = _attn_fwd(qkv, t_meta, t_band, sinks)
    y_att = _mm(att, w["w_attn_branch"], name="att_out")
    merged = _merge_fwd(gates, y_ssd, y_att, w["gate_b"])
    h1 = _mm(merged, w["w_out"], c=h0, mask=True, name="mix_out")
    u2 = _rms_fwd(h1, w["norm_ffn_w"], name="rms_ffn_fwd")
    hid_raw = _mm(u2, w["w_ffn_in"], name="ffn_in")
    hid_up, hid_gate, act = _ffn_act_fwd(hid_raw, w["ffn_conv_w"], w["ffn_conv_b"])
    h2 = _mm(act, w["w_ffn_out"], c=h1, mask=True, name="ffn_out")
    dh2, loss_row, g_norm_final = _final_loss(h2, w["norm_final_w"], target)

    grads = {"norm_final_w": g_norm_final}
    dact = _mm(dh2, w["w_ffn_out"], tb=True, mask=True, name="d_act")
    grads["w_ffn_out"] = _mm(act, dh2, ta=True, mask=True, name="g_w_ffn_out")
    dx_up, dx_gate, dcw_up, dcw_gate, dcb_up, dcb_gate = _ffn_act_bwd(dact, hid_up, hid_gate, hid_raw, w["ffn_conv_w"])
    grads["ffn_conv_w"] = jnp.concatenate([dcw_up, dcw_gate], axis=1)
    grads["ffn_conv_b"] = jnp.concatenate([dcb_up, dcb_gate], axis=1)
    du2 = _mm(dx_up, w_ffn_up, tb=True, name="d_u2_up")
    du2 = _mm(dx_gate, w_ffn_gate, tb=True, c=du2, name="d_u2_gate")
    grads["w_ffn_in"] = (_mm(u2, dx_up, ta=True, name="g_w_ffn_up"), _mm(u2, dx_gate, ta=True, name="g_w_ffn_gate"))
    dh1, grads["norm_ffn_w"] = _rms_bwd(h1, w["norm_ffn_w"], du2, dh2, name="rms_ffn_bwd")

    dmerged = _mm(dh1, w["w_out"], tb=True, mask=True, name="d_merged")
    grads["w_out"] = _mm(merged, dh1, ta=True, mask=True, name="g_w_out")
    dy_ssd, dy_att, dgates, grads["gate_b"] = _merge_bwd(dmerged, gates, y_ssd, y_att, w["gate_b"])
    dyn = _mm(dy_ssd, w["w_ssd_branch"], tb=True, name="d_yn")
    grads["w_ssd_branch"] = _mm(yn, dy_ssd, ta=True, name="g_w_ssd")
    datt = _mm(dy_att, w["w_attn_branch"], tb=True, name="d_att")
    grads["w_attn_branch"] = _mm(att, dy_att, ta=True, name="g_w_att")
    dz, dpxs, dpb, dpc, ddt, grads["ssd_norm_w"], g_dtb, g_alog, g_dskip = _ssd_bwd(
        dyn, y, z, pre, dt_raw, hsave, dtb, alog, dskip, w["ssd_norm_w"])
    grads["ssd_dt_bias"] = _dt_gather(g_dtb)
    grads["ssd_a_log"] = _dt_gather(g_alog)
    grads["ssd_d"] = _dt_gather(g_dskip)
    dpre = jnp.concatenate([dpxs, dpb, dpc], axis=1)
    dxbc, grads["ssd_conv_w"], grads["ssd_conv_b"] = _conv_bwd(dpre, xbc, w["ssd_conv_w"], name="ssd_conv_bwd")
    dqkv, d_tmeta, d_tband, d_sinks = _attn_bwd(datt, qkv, t_meta, t_band, sinks)
    grads["attn_sinks"] = d_sinks[:, :ATT_HEADS]
    dtab = jnp.concatenate([d_tband.reshape(ATT_HEADS, NT_BAND),
                            jnp.transpose(d_tmeta, (1, 0, 2, 3)).reshape(ATT_HEADS, 3 * NT_META)], axis=1)
    grads["rel_bias"] = _bias_grad(dtab, onehot_t).T
    dsegs = {"z": dz, "xbc": dxbc, "dt": ddt, "qkv": dqkv, "g": dgates}
    du, g_in = None, []
    for nm, _ in _IN_SEGS:
        du = _mm(dsegs[nm], segs[nm], tb=True, c=du, name="d_u_" + nm)
        g_in.append(_mm(u, dsegs[nm], ta=True, name="g_w_in_" + nm))
    grads["in_segs"] = g_in
    dh0, grads["norm_mix_w"] = _rms_bwd(h0, w["norm_mix_w"], du, dh1, name="rms_mix_bwd")
    grads["meta_tokens"] = dh0[PAD:BLOCK]
    return loss_row[0, 0], dh0[BLOCK:], grads


def kernel(x, meta_tokens, norm_mix_w, w_in, ssd_conv_w, ssd_conv_b, ssd_dt_bias, ssd_a_log, ssd_d, ssd_norm_w, w_ssd_branch, w_attn_branch, attn_sinks, rel_bias, gate_b, w_out, norm_ffn_w, w_ffn_in, ffn_conv_w, ffn_conv_b, w_ffn_out, norm_final_w, loss_target, m_meta_tokens, m_norm_mix_w, m_w_in, m_ssd_conv_w, m_ssd_conv_b, m_ssd_dt_bias, m_ssd_a_log, m_ssd_d, m_ssd_norm_w, m_w_ssd_branch, m_w_attn_branch, m_attn_sinks, m_rel_bias, m_gate_b, m_w_out, m_norm_ffn_w, m_w_ffn_in, m_ffn_conv_w, m_ffn_conv_b, m_w_ffn_out, m_norm_final_w, v_meta_tokens, v_norm_mix_w, v_w_in, v_ssd_conv_w, v_ssd_conv_b, v_ssd_dt_bias, v_ssd_a_log, v_ssd_d, v_ssd_norm_w, v_w_ssd_branch, v_w_attn_branch, v_attn_sinks, v_rel_bias, v_gate_b, v_w_out, v_norm_ffn_w, v_w_ffn_in, v_ffn_conv_w, v_ffn_conv_b, v_w_ffn_out, v_norm_final_w):
    shard = dict(meta_tokens=meta_tokens, norm_mix_w=norm_mix_w, w_in=w_in, ssd_conv_w=ssd_conv_w,
                 ssd_conv_b=ssd_conv_b, ssd_dt_bias=ssd_dt_bias, ssd_a_log=ssd_a_log, ssd_d=ssd_d,
                 ssd_norm_w=ssd_norm_w, w_ssd_branch=w_ssd_branch, w_attn_branch=w_attn_branch,
                 attn_sinks=attn_sinks, rel_bias=rel_bias, gate_b=gate_b, w_out=w_out, norm_ffn_w=norm_ffn_w,
                 w_ffn_in=w_ffn_in, ffn_conv_w=ffn_conv_w, ffn_conv_b=ffn_conv_b, w_ffn_out=w_ffn_out,
                 norm_final_w=norm_final_w)
    mom_m = dict(zip(_WEIGHTS, (m_meta_tokens, m_norm_mix_w, m_w_in, m_ssd_conv_w, m_ssd_conv_b, m_ssd_dt_bias,
                                m_ssd_a_log, m_ssd_d, m_ssd_norm_w, m_w_ssd_branch, m_w_attn_branch, m_attn_sinks,
                                m_rel_bias, m_gate_b, m_w_out, m_norm_ffn_w, m_w_ffn_in, m_ffn_conv_w, m_ffn_conv_b,
                                m_w_ffn_out, m_norm_final_w)))
    mom_v = dict(zip(_WEIGHTS, (v_meta_tokens, v_norm_mix_w, v_w_in, v_ssd_conv_w, v_ssd_conv_b, v_ssd_dt_bias,
                                v_ssd_a_log, v_ssd_d, v_ssd_norm_w, v_w_ssd_branch, v_w_attn_branch, v_attn_sinks,
                                v_rel_bias, v_gate_b, v_w_out, v_norm_ffn_w, v_w_ffn_in, v_ffn_conv_w, v_ffn_conv_b,
                                v_w_ffn_out, v_norm_final_w)))
    orig_shape = {k: a.shape for k, a in shard.items()}
    two_d = {k: a.reshape(a.shape[-2:]) if a.ndim >= 2 else a.reshape(1, -1) for k, a in shard.items()}
    shape2 = {k: a.shape for k, a in two_d.items()}

    def as2d(tree):
        return {k: tree[k].reshape(shape2[k]) for k in _WEIGHTS}

    mom_m, mom_v = as2d(mom_m), as2d(mom_v)

    def row_pack(tree):
        return jnp.concatenate([tree[k] for k in _ROW_SHARDED], axis=0)

    small_pack = _pack_rows([two_d[k] for k in _SMALL_SHARDED], LANES, SMALL_ROW_MULT)
    w_in_all, w_ffn_in_all, rows_all, small_all = _gather_two_level(
        [two_d["w_in"].astype(BF16), two_d["w_ffn_in"].astype(BF16), row_pack(two_d).astype(BF16), small_pack],
        name="gather_weights")
    full = {k: two_d[k] for k in _SMALL_REPLICATED}
    full["in_segs"] = _w_in_to_segments(w_in_all)
    full["w_ffn_in"] = _ffn_in_from_shards(w_ffn_in_all)
    off = 0
    for k in _ROW_SHARDED:
        r = shape2[k][0]
        full[k] = rows_all[:, off:off + r].reshape(N_DEV * r, D_MODEL)
        off += r
    small_flat = small_all.reshape(N_DEV, -1)
    off = 0
    for k in _SMALL_SHARDED:
        size = int(np.prod(shape2[k]))
        full[k] = _gather_full(small_flat[:, off:off + size].reshape((N_DEV,) + shape2[k]), k, shape2[k])
        off += size

    loss_local, grad_x, grads = _local_step(x[0], loss_target[0], full)

    rows_send = jnp.concatenate([grads[k].reshape(N_DEV, shape2[k][0], D_MODEL) for k in _ROW_SHARDED], axis=1)
    small_names = _SMALL_SHARDED + _SMALL_REPLICATED
    small_send = _pack_rows([grads[k] for k in small_names] + [loss_local.reshape(1)], LANES, SMALL_ROW_MULT)
    big_send = [_segments_to_w_in_shards(grads["in_segs"]), _ffn_in_to_shards(*grads["w_ffn_in"]), rows_send]
    from_sib = _sibling_exchange(big_send + [small_send], [True, True, True, False], name="grads_to_sibling")
    parts = [_pair_sum(mine, sib, name="pair_sum_" + nm, out_dtype=BF16)
             for nm, mine, sib in zip(("w_in", "w_ffn_in", "rows"), big_send, from_sib)]
    parts.append(_add(small_send, from_sib[3], name="pair_sum_small"))
    in_recv, ffn_recv, rows_recv, small_recv = _chip_exchange(parts, [True, True, True, False], name="exchange_grads")

    big = {"w_in": _adamw(two_d["w_in"], in_recv, mom_m["w_in"], mom_v["w_in"], name="adamw_w_in"),
           "w_ffn_in": _adamw(two_d["w_ffn_in"], ffn_recv, mom_m["w_ffn_in"], mom_v["w_ffn_in"], name="adamw_w_ffn_in")}
    rows_out = _adamw(row_pack(two_d), rows_recv, row_pack(mom_m), row_pack(mom_v), name="adamw_rows")
    off = 0
    for k in _ROW_SHARDED:
        r = shape2[k][0]
        big[k] = [a[off:off + r] for a in rows_out]
        off += r
    me =4 * lax.axis_index("x") + 2 * lax.axis_index("y") + lax.axis_index("c")
    small_full_shapes = [grads[k].shape for k in small_names]
    n_small = sum(int(np.prod(s)) for s in small_full_shapes)

    def packed_small(tree):
        parts = []
        for k in small_names:
            a = tree[k]
            if k in _SMALL_SHARDED:
                fullw = jnp.zeros(grads[k].shape, F32)
                a = lax.dynamic_update_slice(fullw, a, (0, me * a.shape[1]))
            parts.append(a)
        return _pack_rows(parts + [jnp.zeros((1,), F32)], LANES, SMALL_ROW_MULT)

    g_small, d_small, m_small, v_small = _adamw(packed_small(two_d), small_recv, packed_small(mom_m),
                                                packed_small(mom_v), name="adamw_small")

    def unpack_all(which, small):
        out = {k: big[k][which] for k in _BIG}
        flat = small.reshape(-1)
        for k, a in zip(small_names, _unpack(flat, small_full_shapes)):
            if k in _SMALL_SHARDED:
                a = lax.dynamic_slice(a, (0, me * shape2[k][1]), shape2[k])
            out[k] = a
        return out, flat[n_small]

    g_all, loss = unpack_all(0, g_small)
    d_all, _ = unpack_all(1, d_small)
    m_all, _ = unpack_all(2, m_small)
    v_all, _ = unpack_all(3, v_small)

    def final(tree):
        return [tree[k].reshape(orig_shape[k]) for k in _WEIGHTS]

    return (loss, grad_x[None], *final(g_all), *final(d_all), *final(m_all), *final(v_all))
```

```python
import functools
import math

import numpy as np
import jax
import jax.numpy as jnp
from jax import lax
from jax.experimental import pallas as pl
from jax.experimental.pallas import tpu as pltpu

F32 = jnp.float32
BF16 = jnp.bfloat16
HIGHEST = lax.Precision.HIGHEST

D_MODEL = 1024
N_META = 16
BLOCK = 128
PAD = BLOCK - N_META
EPS = 1e-6
NEG = -1e30
SSD_INNER = 2 * D_MODEL
SSD_HEADDIM = 64
SSD_HEADS = SSD_INNER // SSD_HEADDIM
SSD_GROUPS = 4
SSD_HPG = SSD_HEADS // SSD_GROUPS
SSD_STATE = 128
SSD_CONV = 4
SSD_GW = SSD_HPG * SSD_HEADDIM
SSD_BC = SSD_GROUPS * SSD_STATE
SSD_XBC = SSD_INNER + 2 * SSD_BC
ATT_HEADS = 16
ATT_KV_HEADS = 2
ATT_HEADDIM = 64
ATT_GQ = ATT_HEADS // ATT_KV_HEADS
ATT_Q = ATT_HEADS * ATT_HEADDIM
ATT_KV = ATT_KV_HEADS * ATT_HEADDIM
REL_BUCKETS = 32
REL_MAX_DIST = 128
D_FF = 2816
FFN_CONV = 3
ADAM_LR = 0.001
ADAM_B1 = 0.9
ADAM_B2 = 0.999
ADAM_EPS = 1e-08
ADAM_WD = 0.01
ADAM_STEP = 10

N_DEV = 8
LANES = 128
SUBLANES = 8
DT_W = SSD_GROUPS * LANES
VMEM_LIMIT_BYTES = 56 * 1024 * 1024
MESH = pl.DeviceIdType.MESH

SMALL_ROW_MULT = 16

NT_BAND = BLOCK * 2 * BLOCK
NT_META = BLOCK * BLOCK
NT_ALL = NT_BAND + 3 * NT_META
NT_TILE = 8192


def _cparams(*sem):
    return pltpu.CompilerParams(dimension_semantics=sem, vmem_limit_bytes=VMEM_LIMIT_BYTES)


def _row_tile(n, cap):
    best = None
    for t in range(16, min(n, cap) + 1, 16):
        if n % t == 0:
            best = t
    return best or n


def _col_tile(n, cap):
    for t in (1408, 1280, 1024, 768, 640, 512, 384, 256, 128):
        if t <= cap and n % t == 0:
            return t
    return n


def _silu(x):
    return x * jax.nn.sigmoid(x)


def _dsilu(x):
    s = jax.nn.sigmoid(x)
    return s * (1.0 + x * (1.0 - s))


def _softplus(x):
    return jnp.maximum(x, 0.0) + jnp.log(1.0 + jnp.exp(-jnp.abs(x)))


def _dot_nt(a, b):
    return lax.dot_general(a, b, (((1,), (1,)), ((), ())), preferred_element_type=F32)


def _dot_tn(a, b):
    return lax.dot_general(a, b, (((0,), (0,)), ((), ())), preferred_element_type=F32)


def _dot(a, b):
    return jnp.dot(a, b, preferred_element_type=F32)


def _sum_all(x):
    return jnp.sum(jnp.sum(x, axis=1, keepdims=True), axis=0, keepdims=True)


def _mm(a, b, *, name, ta=False, tb=False, c=None, mask=False, out_dtype=F32):
    if not ta:
        m, k = a.shape
        n = b.shape[0] if tb else b.shape[1]
        tm = _row_tile(m, 832)
        tn = _col_tile(n, 512 if k > 3072 else 1024)

        def body(*refs):
            if c is None:
                a_ref, b_ref, o_ref = refs
            else:
                a_ref, b_ref, c_ref, o_ref = refs
            acc = (_dot_nt if tb else _dot)(a_ref[...].astype(BF16), b_ref[...].astype(BF16))
            if mask:
                row = pl.program_id(0) * tm + lax.broadcasted_iota(jnp.int32, (tm, 1), 0)
                acc = jnp.where(row >= PAD, acc, 0.0)
            if c is not None:
                acc = acc + c_ref[...]
            o_ref[...] = acc.astype(out_dtype)

        b_spec = pl.BlockSpec((tn, k), lambda i, j: (j, 0)) if tb else pl.BlockSpec((k, tn), lambda i, j: (0, j))
        in_specs = [pl.BlockSpec((tm, k), lambda i, j: (i, 0)), b_spec]
        args = [a, b]
        if c is not None:
            in_specs.append(pl.BlockSpec((tm, tn), lambda i, j: (i, j)))
            args.append(c)
        return pl.pallas_call(
            body, name=name, grid=(m // tm, n // tn), in_specs=in_specs,
            out_specs=pl.BlockSpec((tm, tn), lambda i, j: (i, j)),
            out_shape=jax.ShapeDtypeStruct((m, n), out_dtype),
            compiler_params=_cparams("parallel", "parallel"))(*args)

    kc, m = a.shape
    n = b.shape[1]
    tk = _row_tile(kc, 832)
    tm = _col_tile(m, 1408)
    tn = _col_tile(n, 1408)

    def body_t(a_ref, b_ref, o_ref):
        kk = pl.program_id(2)
        bb = b_ref[...]
        if mask:
            row = kk * tk + lax.broadcasted_iota(jnp.int32, (tk, 1), 0)
            bb = jnp.where(row >= PAD, bb, jnp.zeros_like(bb))
        p = _dot_tn(a_ref[...].astype(BF16), bb.astype(BF16))

        @pl.when(kk == 0)
        def _():
            o_ref[...] = p

        @pl.when(kk > 0)
        def _():
            o_ref[...] += p

    return pl.pallas_call(
        body_t, name=name, grid=(m // tm, n // tn, kc // tk),
        in_specs=[pl.BlockSpec((tk, tm), lambda i, j, kk: (kk, i)), pl.BlockSpec((tk, tn), lambda i, j, kk: (kk, j))],
        out_specs=pl.BlockSpec((tm, tn), lambda i, j, kk: (i, j)),
        out_shape=jax.ShapeDtypeStruct((m, n), F32),
        compiler_params=_cparams("parallel", "parallel", "arbitrary"))(a, b)


def _rms_fwd(h, w, *, name):
    n, d = h.shape
    tm = _row_tile(n, 832)

    def body(h_ref, w_ref, o_ref):
        x = h_ref[...]
        r = lax.rsqrt(jnp.mean(x * x, axis=-1, keepdims=True) + EPS)
        o_ref[...] = (x * r * w_ref[...]).astype(BF16)

    return pl.pallas_call(
        body, name=name, grid=(n // tm,),
        in_specs=[pl.BlockSpec((tm, d), lambda i: (i, 0)), pl.BlockSpec((1, d), lambda i: (0, 0))],
        out_specs=pl.BlockSpec((tm, d), lambda i: (i, 0)),
        out_shape=jax.ShapeDtypeStruct((n, d), BF16),
        compiler_params=_cparams("parallel"))(h, w)


def _rms_bwd(x, w, dy, dres, *, name):
    n, d = x.shape
    tm = _row_tile(n, 832)

    def body(x_ref, w_ref, dy_ref, dres_ref, dx_ref, dw_ref):
        i = pl.program_id(0)
        xv = x_ref[...]
        r = lax.rsqrt(jnp.mean(xv * xv, axis=-1, keepdims=True) + EPS)
        xh = xv * r
        dyv = dy_ref[...]
        g = dyv * w_ref[...]
        dx_ref[...] = r * (g - xh * jnp.mean(g * xh, axis=-1, keepdims=True)) + dres_ref[...]
        part = jnp.sum(dyv * xh, axis=0, keepdims=True)

        @pl.when(i == 0)
        def _():
            dw_ref[...] = part

        @pl.when(i > 0)
        def _():
            dw_ref[...] += part

    row = pl.BlockSpec((tm, d), lambda i: (i, 0))
    vec = pl.BlockSpec((1, d), lambda i: (0, 0))
    return pl.pallas_call(
        body, name=name, grid=(n // tm,), in_specs=[row, vec, row, row], out_specs=[row, vec],
        out_shape=[jax.ShapeDtypeStruct((n, d), F32), jax.ShapeDtypeStruct((1, d), F32)],
        compiler_params=_cparams("arbitrary"))(x, w, dy, dres)


def _final_loss(h, w, target):
    n, d = h.shape
    nb = n // BLOCK

    def body(h_ref, w_ref, t_ref, dh_ref, loss_ref, dw_ref):
        i = pl.program_id(0)
        xv = h_ref[...]
        r = lax.rsqrt(jnp.mean(xv * xv, axis=-1, keepdims=True) + EPS)
        xh = xv * r
        wv = w_ref[...]
        err = jnp.where(i >= 1, xh * wv - t_ref[...], 0.0)
        dyv = err * (1.0 / d)
        g = dyv * wv
        dh_ref[...] = r * (g - xh * jnp.mean(g * xh, axis=-1, keepdims=True))
        lpart = jnp.broadcast_to(0.5 * _sum_all(err * err) * (1.0 / d), (1, LANES))
        wpart = jnp.sum(dyv * xh, axis=0, keepdims=True)

        @pl.when(i == 0)
        def _():
            loss_ref[...] = lpart
            dw_ref[...] = wpart

        @pl.when(i > 0)
        def _():
            loss_ref[...] += lpart
            dw_ref[...] += wpart

    row = pl.BlockSpec((BLOCK, d), lambda i: (i, 0))
    vec = pl.BlockSpec((1, d), lambda i: (0, 0))
    return pl.pallas_call(
        body, name="final_loss", grid=(nb,),
        in_specs=[row, vec, pl.BlockSpec((BLOCK, d), lambda i: (jnp.maximum(i - 1, 0), 0))],
        out_specs=[row, pl.BlockSpec((1, LANES), lambda i: (0, 0)), vec],
        out_shape=[jax.ShapeDtypeStruct((n, d), F32), jax.ShapeDtypeStruct((1, LANES), F32),
                   jax.ShapeDtypeStruct((1, d), F32)],
        compiler_params=_cparams("arbitrary"))(h, w, target)


def _main_spec(tm, cb, off=0):
    return pl.BlockSpec((tm, cb), lambda j, i: (i, j + off))


def _prev_spec(tm, cb, off=0):
    r8 = tm // SUBLANES
    return pl.BlockSpec((SUBLANES, cb), lambda j, i: (jnp.maximum(i * r8 - 1, 0), j + off))


def _next_spec(tm, cb, n_rows, off=0):
    r8 = tm // SUBLANES
    last = n_rows // SUBLANES - 1
    return pl.BlockSpec((SUBLANES, cb), lambda j, i: (jnp.minimum((i + 1) * r8, last), j + off))


def _with_prev(prev_ref, main_ref, i):
    prev = jnp.where(i > 0, prev_ref[...], 0.0)
    return jnp.concatenate([prev, main_ref[...]], axis=0)


def _with_next(main, nxt, i, n_tiles):
    return jnp.concatenate([main, jnp.where(i < n_tiles - 1, nxt, 0.0)], axis=0)


def _back(xx, s, tm):
    if s == 0:
        return xx[SUBLANES:SUBLANES + tm]
    return pltpu.roll(xx, s, 0)[SUBLANES:SUBLANES + tm]


def _ahead(xx, s, tm):
    if s == 0:
        return xx[:tm]
    return pltpu.roll(xx, tm + SUBLANES - s, 0)[:tm]


def _conv_fwd(x, w, b, *, name):
    n, cdim = x.shape
    kw = w.shape[0]
    tm = _row_tile(n, 832)
    cb = _col_tile(cdim, 512)

    def body(xp_ref, x_ref, w_ref, b_ref, o_ref):
        xx = _with_prev(xp_ref, x_ref, pl.program_id(1))
        acc = jnp.broadcast_to(b_ref[...], (tm, cb))
        for k in range(kw):
            acc = acc + w_ref[k:k + 1, :] * _back(xx, kw - 1 - k, tm)
        o_ref[...] = acc

    return pl.pallas_call(
        body, name=name, grid=(cdim // cb, n // tm),
        in_specs=[_prev_spec(tm, cb), _main_spec(tm, cb), pl.BlockSpec((kw, cb), lambda j, i: (0, j)),
                  pl.BlockSpec((1, cb), lambda j, i: (0, j))],
        out_specs=_main_spec(tm, cb),
        out_shape=jax.ShapeDtypeStruct((n, cdim), F32),
        compiler_params=_cparams("parallel", "parallel"))(x, x, w, b)


def _conv_bwd_core(dpre_ext, x_ext, w_ref, kw, tm):
    dpre = dpre_ext[:tm]
    dx = None
    dws = []
    for k in range(kw):
        term = w_ref[k:k + 1, :] * _ahead(dpre_ext, kw - 1 - k, tm)
        dx = term if dx is None else dx + term
        dws.append(jnp.sum(dpre * _back(x_ext, kw - 1 - k, tm), axis=0, keepdims=True))
    return dx, dws, jnp.sum(dpre, axis=0, keepdims=True)


def _acc_rows(i, dw_ref, db_ref, dws, db):
    @pl.when(i == 0)
    def _():
        for k, v in enumerate(dws):
            dw_ref[k:k + 1, :] = v
        db_ref[...] = db

    @pl.when(i > 0)
    def _():
        for k, v in enumerate(dws):
            dw_ref[k:k + 1, :] += v
        db_ref[...] += db


def _conv_bwd(dpre, x, w, *, name):
    n, cdim = x.shape
    kw = w.shape[0]
    tm = _row_tile(n, 832)
    cb = _col_tile(cdim, 512)
    nt = n // tm

    def body(d_ref, dn_ref, xp_ref, x_ref, w_ref, dx_ref, dw_ref, db_ref):
        i = pl.program_id(1)
        dpre_ext = _with_next(d_ref[...], dn_ref[...], i, nt)
        x_ext = _with_prev(xp_ref, x_ref, i)
        dx, dws, db = _conv_bwd_core(dpre_ext, x_ext, w_ref, kw, tm)
        dx_ref[...] = dx.astype(BF16)
        _acc_rows(i, dw_ref, db_ref, dws, db)

    wspec = pl.BlockSpec((kw, cb), lambda j, i: (0, j))
    bspec = pl.BlockSpec((1, cb), lambda j, i: (0, j))
    return pl.pallas_call(
        body, name=name, grid=(cdim // cb, nt),
        in_specs=[_main_spec(tm, cb), _next_spec(tm, cb, n), _prev_spec(tm, cb), _main_spec(tm, cb), wspec],
        out_specs=[_main_spec(tm, cb), wspec, bspec],
        out_shape=[jax.ShapeDtypeStruct((n, cdim), BF16), jax.ShapeDtypeStruct((kw, cdim), F32),
                   jax.ShapeDtypeStruct((1, cdim), F32)],
        compiler_params=_cparams("parallel", "arbitrary"))(dpre, dpre, x, x, w)


def _ffn_act_fwd(x, w, b):
    n = x.shape[0]
    kw = w.shape[0]
    tm = _row_tile(n, 832)
    cb = _col_tile(D_FF, 256)
    nc = D_FF // cb

    def body(xpu_ref, xu_ref, xpg_ref, xg_ref, wu_ref, wg_ref, bu_ref, bg_ref, hu_ref, hg_ref, act_ref):
        i = pl.program_id(1)
        outs = []
        for xp_ref, x_ref, w_ref, b_ref in ((xpu_ref, xu_ref, wu_ref, bu_ref), (xpg_ref, xg_ref, wg_ref, bg_ref)):
            xx = _with_prev(xp_ref, x_ref, i)
            acc = jnp.broadcast_to(b_ref[...], (tm, cb))
            for k in range(kw):
                acc = acc + w_ref[k:k + 1, :] * _back(xx, kw - 1 - k, tm)
            outs.append(acc)
        hu_ref[...] = outs[0]
        hg_ref[...] = outs[1]
        act_ref[...] = (_silu(outs[1]) * outs[0]).astype(BF16)

    def wspec(off):
        return pl.BlockSpec((kw, cb), lambda j, i: (0, j + off))

    def bspec(off):
        return pl.BlockSpec((1, cb), lambda j, i: (0, j + off))

    out = _main_spec(tm, cb)
    return pl.pallas_call(
        body, name="ffn_act_fwd", grid=(nc, n // tm),
        in_specs=[_prev_spec(tm, cb), _main_spec(tm, cb), _prev_spec(tm, cb, nc), _main_spec(tm, cb, nc),
                  wspec(0), wspec(nc), bspec(0), bspec(nc)],
        out_specs=[out, out, out],
        out_shape=[jax.ShapeDtypeStruct((n, D_FF), F32), jax.ShapeDtypeStruct((n, D_FF), F32),
                   jax.ShapeDtypeStruct((n, D_FF), BF16)],
        compiler_params=_cparams("parallel", "parallel"))(x, x, x, x, w, w, b, b)


def _ffn_act_bwd(dact, hu, hg, x, w):
    n = x.shape[0]
    kw = w.shape[0]
    tm = _row_tile(n, 832)
    cb = _col_tile(D_FF, 256)
    nc = D_FF // cb
    nt = n // tm

    def body(d_ref, dn_ref, hu_ref, hun_ref, hg_ref, hgn_ref, xpu_ref, xu_ref, xpg_ref, xg_ref, wu_ref, wg_ref,
             dxu_ref, dxg_ref, dwu_ref, dwg_ref, dbu_ref, dbg_ref):
        i = pl.program_id(1)
        dact_e = _with_next(d_ref[...], dn_ref[...], i, nt)
        up_e = _with_next(hu_ref[...], hun_ref[...], i, nt)
        gate_e = _with_next(hg_ref[...], hgn_ref[...], i, nt)
        dup_e = dact_e * _silu(gate_e)
        dgate_e = dact_e * up_e * _dsilu(gate_e)
        dx, dws, db = _conv_bwd_core(dup_e, _with_prev(xpu_ref, xu_ref, i), wu_ref, kw, tm)
        dxu_ref[...] = dx.astype(BF16)
        _acc_rows(i, dwu_ref, dbu_ref, dws, db)
        dx, dws, db = _conv_bwd_core(dgate_e, _with_prev(xpg_ref, xg_ref, i), wg_ref, kw, tm)
        dxg_ref[...] = dx.astype(BF16)
        _acc_rows(i, dwg_ref, dbg_ref, dws, db)

    main, nxt = _main_spec(tm, cb), _next_spec(tm, cb, n)
    wspec0 = pl.BlockSpec((kw, cb), lambda j, i: (0, j))
    wspec1 = pl.BlockSpec((kw, cb), lambda j, i: (0, j + nc))
    bspec = pl.BlockSpec((1, cb), lambda j, i: (0, j))
    return pl.pallas_call(
        body, name="ffn_act_bwd", grid=(nc, nt),
        in_specs=[main, nxt, main, nxt, main, nxt,
                  _prev_spec(tm, cb), _main_spec(tm, cb), _prev_spec(tm, cb, nc), _main_spec(tm, cb, nc),
                  wspec0, wspec1],
        out_specs=[main, main, wspec0, wspec0, bspec, bspec],
        out_shape=[jax.ShapeDtypeStruct((n, D_FF), BF16), jax.ShapeDtypeStruct((n, D_FF), BF16),
                   jax.ShapeDtypeStruct((kw, D_FF), F32), jax.ShapeDtypeStruct((kw, D_FF), F32),
                   jax.ShapeDtypeStruct((1, D_FF), F32), jax.ShapeDtypeStruct((1, D_FF), F32)],
        compiler_params=_cparams("parallel", "arbitrary"))(dact, dact, hu, hu, hg, hg, x, x, x, x, w, w)


def _ssd_prep(pxs_ref, pb_ref, pc_ref, dtr_ref, dtb_ref, alog_ref, c):
    xs = _silu(pxs_ref[...])
    bm = _silu(pb_ref[...])
    cm = _silu(pc_ref[...])
    row = lax.broadcasted_iota(jnp.int32, (BLOCK, 1), 0) + c * BLOCK
    valid = (row >= PAD).astype(F32)
    dtr = dtr_ref[...] + dtb_ref[...]
    dt = _softplus(dtr) * valid
    a = -jnp.exp(alog_ref[...])
    lam = dt * a
    ri = lax.broadcasted_iota(jnp.int32, (BLOCK, BLOCK), 0)
    ci = lax.broadcasted_iota(jnp.int32, (BLOCK, BLOCK), 1)
    causal = ci <= ri
    cs = jnp.dot(causal.astype(F32), lam, precision=HIGHEST, preferred_element_type=F32)
    return xs, bm, cm, valid, dtr, dt, a, lam, cs, causal


def _head_cols(r):
    return slice(SSD_HEADDIM * r, SSD_HEADDIM * (r + 1))


def _ssd_specs(nc, rev):
    def cidx(c):
        return nc - 1 - c if rev else c

    xs = pl.BlockSpec((BLOCK, SSD_GW), lambda g, c: (cidx(c), g))
    bspec = pl.BlockSpec((BLOCK, SSD_STATE), lambda g, c: (cidx(c), SSD_INNER // SSD_STATE + g))
    cspec = pl.BlockSpec((BLOCK, SSD_STATE), lambda g, c: (cidx(c), (SSD_INNER + SSD_BC) // SSD_STATE + g))
    lane = pl.BlockSpec((BLOCK, LANES), lambda g, c: (cidx(c), g))
    vec = pl.BlockSpec((1, LANES), lambda g, c: (0, g))
    wide_vec = pl.BlockSpec((1, SSD_GW), lambda g, c: (0, g))
    hsave = pl.BlockSpec((1, 1, SSD_GW, SSD_STATE), lambda g, c: (cidx(c), g, 0, 0))
    return xs, bspec, cspec, lane, vec, wide_vec, hsave


def _head_spread_matrix():
    r = lax.broadcasted_iota(jnp.int32, (LANES, SSD_GW), 0)
    col = lax.broadcasted_iota(jnp.int32, (LANES, SSD_GW), 1)
    return (col // SSD_HEADDIM == r).astype(F32)


def _const_spec(shape):
    return pl.BlockSpec(shape, lambda g, c: (0,) * len(shape))


def _spread_heads(per_head, e_ref):
    wide = jnp.dot(jnp.concatenate(per_head, axis=0), e_ref[...], precision=HIGHEST, preferred_element_type=F32)
    return [wide[BLOCK * k:BLOCK * (k + 1)] for k in range(len(per_head))]


def _ssd_fwd(pre, dt_raw, z, dtb, alog, dskip_w, norm_w):
    n = pre.shape[0]
    nc = n // BLOCK
    xs_s, b_s, c_s, lane_s, vec_s, wide_s, hs_s = _ssd_specs(nc, False)

    def body(pxs_ref, pb_ref, pc_ref, dtr_ref, z_ref, dtb_ref, alog_ref, dskw_ref, nw_ref, e_ref,
             y_ref, yn_ref, hs_ref, h_scr):
        c = pl.program_id(1)

        @pl.when(c == 0)
        def _():
            h_scr[...] = jnp.zeros_like(h_scr)

        xs, bm, cm, _, _, dt, _, _, cs, causal = _ssd_prep(pxs_ref, pb_ref, pc_ref, dtr_ref, dtb_ref, alog_ref, c)
        cst = cs.T
        cs_last = cs[BLOCK - 1:BLOCK, :]
        dt_w, ecs_w, dec_w = _spread_heads([dt, jnp.exp(cs), jnp.exp(cs_last - cs)], e_ref)
        xdt = xs * dt_w
        bmb = bm.astype(BF16)
        cmb = cm.astype(BF16)
        cb = _dot_nt(cmb, bmb)
        hg = h_scr[...]
        hs_ref[0, 0] = hg
        y = _dot_nt(cmb, hg.astype(BF16)) * ecs_w + dskw_ref[...] * xs
        first = lax.broadcasted_iota(jnp.int32, (BLOCK, LANES), 1) < SSD_HEADDIM
        diag = []
        for j in range(SSD_HPG // 2):
            xp = xdt[:, LANES * j:LANES * (j + 1)].astype(BF16)
            res = []
            for r in (2 * j, 2 * j + 1):
                lm = jnp.exp(jnp.where(causal, cs[:, r:r + 1] - cst[r:r + 1, :], NEG))
                res.append(_dot((cb * lm).astype(BF16), xp))
            diag.append(jnp.where(first, res[0], res[1]))
        y = y + jnp.concatenate(diag, axis=1)
        st = _dot_tn((xdt * dec_w).astype(BF16), bmb)
        eh = jnp.exp(cs_last)
        for r in range(SSD_HPG):
            rows = _head_cols(r)
            h_scr[rows, :] = hg[rows, :] * eh[:, r:r + 1] + st[rows, :]
        y_ref[...] = y
        gts = y * _silu(z_ref[...])
        rr = lax.rsqrt(jnp.mean(gts * gts, axis=-1, keepdims=True) + EPS)
        yn_ref[...] = (gts * rr * nw_ref[...]).astype(BF16)

    return pl.pallas_call(
        body, name="ssd_fwd", grid=(SSD_GROUPS, nc),
        in_specs=[xs_s, b_s, c_s, lane_s, xs_s, vec_s, vec_s, wide_s, wide_s, _const_spec((LANES, SSD_GW))],
        out_specs=[xs_s, xs_s, hs_s],
        out_shape=[jax.ShapeDtypeStruct((n, SSD_INNER), F32), jax.ShapeDtypeStruct((n, SSD_INNER), BF16),
                   jax.ShapeDtypeStruct((nc, SSD_GROUPS, SSD_GW, SSD_STATE), F32)],
        scratch_shapes=[pltpu.VMEM((SSD_GW, SSD_STATE), F32)],
        compiler_params=_cparams("parallel", "arbitrary"))(
            pre, pre, pre, dt_raw, z, dtb, alog, dskip_w, norm_w, _head_spread_matrix())


def _lane_put(acc, col, r):
    lane = lax.broadcasted_iota(jnp.int32, acc.shape, 1)
    return jnp.where(lane == r, col, acc)


def _ssd_bwd(dyn, y, z, pre, dt_raw, hsave, dtb, alog, dskip_w, norm_w):
    n = pre.shape[0]
    nc = n // BLOCK
    spread = _head_spread_matrix()
    xs_s, b_s, c_s, lane_s, vec_s, wide_s, hs_s = _ssd_specs(nc, True)
    bc_out =pl.BlockSpec((BLOCK, SSD_STATE), lambda g, c: (nc - 1 - c, g))

    def body(dyn_ref, y_ref, z_ref, pxs_ref, pb_ref, pc_ref, dtr_ref, hs_ref, dtb_ref, alog_ref, dskw_ref, nw_ref,
             e_ref, r_ref,
             dz_ref, dxs_ref, dbm_ref, dcm_ref, ddt_ref, dnw_ref, ddtb_ref, dalog_ref, ddsk_ref, g_scr):
        step = pl.program_id(1)
        c = nc - 1 - step

        @pl.when(step == 0)
        def _():
            g_scr[...] = jnp.zeros_like(g_scr)

        xs, bm, cm, valid, dtr, dt, a, lam, cs, causal = _ssd_prep(
            pxs_ref, pb_ref, pc_ref, dtr_ref, dtb_ref, alog_ref, c)
        cst = cs.T
        cs_last = cs[BLOCK - 1:BLOCK, :]
        bmb = bm.astype(BF16)
        cmb = cm.astype(BF16)
        cb = _dot_nt(cmb, bmb)
        hg = hs_ref[0, 0]
        hgb = hg.astype(BF16)
        yoff = _dot_nt(cmb, hgb)
        gn = g_scr[...]
        gnb = gn.astype(BF16)

        zv = z_ref[...]
        yv = y_ref[...]
        sz = _silu(zv)
        gts = yv * sz
        rr = lax.rsqrt(jnp.mean(gts * gts, axis=-1, keepdims=True) + EPS)
        xh = gts * rr
        dynv = dyn_ref[...]
        gg = dynv * nw_ref[...]
        dgts = rr * (gg - xh * jnp.mean(gg * xh, axis=-1, keepdims=True))
        dnw = jnp.sum(dynv * xh, axis=0, keepdims=True)
        dy = dgts * sz
        dz_ref[...] = (dgts * yv * _dsilu(zv)).astype(BF16)

        ecs = jnp.exp(cs)
        dec = jnp.exp(cs_last - cs)
        eh = jnp.exp(cs_last)
        dt_w, ecs_w, dec_w = _spread_heads([dt, ecs, dec], e_ref)
        red_m = r_ref[...]

        def head_sums(v):
            return jnp.dot(v, red_m, precision=lax.Precision.HIGH, preferred_element_type=F32)

        xdt = xs * dt_w
        q_all = _dot_nt(bmb, gnb)
        w_all = (dy * ecs_w).astype(BF16)
        e_hl = head_sums(q_all * xdt) * dec
        dcs_col = head_sums(dy * yoff) * ecs - e_hl
        gh = jnp.zeros((1, LANES), F32)
        prod = gn * hg
        for r in range(SSD_HPG):
            gh = _lane_put(gh, _sum_all(prod[_head_cols(r), :]), r)
        dcs_last = jnp.sum(e_hl, axis=0, keepdims=True) + eh * gh
        ddsk = jnp.sum(head_sums(dy * xs), axis=0, keepdims=True)
        cbt = _dot_nt(bmb, cmb)
        lane = lax.broadcasted_iota(jnp.int32, (BLOCK, LANES), 1)
        first = lane < SSD_HEADDIM
        causal_t = lax.broadcasted_iota(jnp.int32, (BLOCK, BLOCK), 1) >= lax.broadcasted_iota(
            jnp.int32, (BLOCK, BLOCK), 0)
        sub = lax.broadcasted_iota(jnp.int32, (SUBLANES, BLOCK), 0)
        dcs_row = jnp.zeros((SUBLANES, BLOCK), F32)
        dcb = jnp.zeros((BLOCK, BLOCK), F32)
        dxdt_pairs = []
        for j in range(SSD_HPG // 2):
            tile = slice(LANES * j, LANES * (j + 1))
            dy_p = dy[:, tile]
            dyb = dy_p.astype(BF16)
            xdtb = xdt[:, tile].astype(BF16)
            res = []
            for half, r in enumerate((2 * j, 2 * j + 1)):
                csc, csr = cs[:, r:r + 1], cst[r:r + 1, :]
                lm = jnp.exp(jnp.where(causal, csc - csr, NEG))
                lmt = jnp.exp(jnp.where(causal_t, csr - csc, NEG))
                keep = first if half == 0 else jnp.logical_not(first)
                gm = _dot_nt(jnp.where(keep, dy_p, 0.0).astype(BF16), xdtb) * lm
                dcb = dcb + gm
                mm_ = gm * cb
                dcs_col = dcs_col + jnp.where(lane == r, jnp.sum(mm_, axis=1, keepdims=True), 0.0)
                dcs_row = jnp.where(sub == r, jnp.sum(mm_, axis=0, keepdims=True), dcs_row)
                res.append(_dot((cbt * lmt).astype(BF16), dyb))
            dxdt_pairs.append(jnp.where(first, res[0], res[1]))
        dxdt = jnp.concatenate(dxdt_pairs, axis=1) + q_all * dec_w
        ddt_x = head_sums(dxdt * xs)
        dxs = dxdt * dt_w + dskw_ref[...] * dy
        dcbb = dcb.astype(BF16)
        dcm = _dot(w_all, hgb) + _dot(dcbb, bmb)
        dbm = _dot((xdt * dec_w).astype(BF16), gnb) + _dot_tn(dcbb, cmb)
        dh_off = _dot_tn(w_all, cmb)
        for r in range(SSD_HPG):
            rows = _head_cols(r)
            g_scr[rows, :] = gn[rows, :] * eh[:, r:r + 1] + dh_off[rows, :]

        pad_rows = jnp.zeros((BLOCK - SUBLANES, BLOCK), F32)
        dcs = dcs_col - jnp.concatenate([dcs_row, pad_rows], axis=0).T
        rsel = lax.broadcasted_iota(jnp.int32, (BLOCK, LANES), 0)
        dcs = dcs + jnp.where(rsel == BLOCK - 1, dcs_last, 0.0)
        ri = lax.broadcasted_iota(jnp.int32, (BLOCK, BLOCK), 0)
        ci = lax.broadcasted_iota(jnp.int32, (BLOCK, BLOCK), 1)
        dlam = jnp.dot((ci >= ri).astype(F32), dcs, precision=HIGHEST, preferred_element_type=F32)
        head = lane < SSD_HPG
        ddt = dlam * a + ddt_x
        ddtr = jnp.where(head, ddt * jax.nn.sigmoid(dtr) * valid, 0.0)
        ddt_ref[...] = ddtr.astype(BF16)
        dalog = jnp.sum(jnp.where(head, dlam * lam, 0.0), axis=0, keepdims=True)
        ddtb = jnp.sum(ddtr, axis=0, keepdims=True)

        dxs_ref[...] = dxs * _dsilu(pxs_ref[...])
        dbm_ref[...] = dbm * _dsilu(pb_ref[...])
        dcm_ref[...] = dcm * _dsilu(pc_ref[...])

        @pl.when(step == 0)
        def _():
            dnw_ref[...] = dnw
            ddtb_ref[...] = ddtb
            dalog_ref[...] = dalog
            ddsk_ref[...] = ddsk

        @pl.when(step > 0)
        def _():
            dnw_ref[...] += dnw
            ddtb_ref[...] += ddtb
            dalog_ref[...] += dalog
            ddsk_ref[...] += ddsk

    return pl.pallas_call(
        body, name="ssd_bwd", grid=(SSD_GROUPS, nc),
        in_specs=[xs_s, xs_s, xs_s, xs_s, b_s, c_s, lane_s, hs_s, vec_s, vec_s, wide_s, wide_s,
                  _const_spec((LANES, SSD_GW)), _const_spec((SSD_GW, LANES))],
        out_specs=[xs_s, xs_s, bc_out, bc_out, lane_s, wide_s, vec_s, vec_s, vec_s],
        out_shape=[jax.ShapeDtypeStruct((n, SSD_INNER), BF16), jax.ShapeDtypeStruct((n, SSD_INNER), F32),
                   jax.ShapeDtypeStruct((n, SSD_BC), F32), jax.ShapeDtypeStruct((n, SSD_BC), F32),
                   jax.ShapeDtypeStruct((n, DT_W), BF16), jax.ShapeDtypeStruct((1, SSD_INNER), F32),
                   jax.ShapeDtypeStruct((1, DT_W), F32), jax.ShapeDtypeStruct((1, DT_W), F32),
                   jax.ShapeDtypeStruct((1, DT_W), F32)],
        scratch_shapes=[pltpu.VMEM((SSD_GW, SSD_STATE), F32)],
        compiler_params=_cparams("parallel", "arbitrary"))(
            dyn, y, z, pre, pre, pre, dt_raw, hsave, dtb, alog, dskip_w, norm_w, spread, spread.T)


def _bucket_table():
    def bucket(dist):
        d = np.maximum(dist, 0)
        half = REL_BUCKETS // 2
        big = half + (np.log(np.maximum(d, half).astype(np.float32) / np.float32(half))
                      / np.float32(math.log(REL_MAX_DIST / half)) * np.float32(REL_BUCKETS - half)).astype(np.int32)
        return np.where(d < half, d, np.minimum(big, REL_BUCKETS - 1)).astype(np.int32)

    l = np.arange(BLOCK)[:, None]
    band = bucket(l + BLOCK - np.arange(2 * BLOCK)[None, :])
    j = np.arange(BLOCK)[None, :]
    metas = [bucket(l - j), bucket(BLOCK + l - j), bucket(2 * BLOCK + l - j)]
    return np.concatenate([band.reshape(-1)] + [m.reshape(-1) for m in metas])


def _onehot_t():
    buckets = jnp.asarray(_bucket_table())
    return (buckets[None, :] == jnp.arange(REL_BUCKETS, dtype=jnp.int32)[:, None]).astype(F32)


def _bias_tables(rel_t, onehot_t):
    def body(r_ref, oh_ref, o_ref):
        o_ref[...] = jnp.dot(r_ref[...], oh_ref[...], precision=HIGHEST, preferred_element_type=F32)

    return pl.pallas_call(
        body, name="bias_tables", grid=(NT_ALL // NT_TILE,),
        in_specs=[pl.BlockSpec((ATT_HEADS, REL_BUCKETS), lambda i: (0, 0)),
                  pl.BlockSpec((REL_BUCKETS, NT_TILE), lambda i: (0, i))],
        out_specs=pl.BlockSpec((ATT_HEADS, NT_TILE), lambda i: (0, i)),
        out_shape=jax.ShapeDtypeStruct((ATT_HEADS, NT_ALL), F32),
        compiler_params=_cparams("parallel"))(rel_t, onehot_t)


def _bias_grad(dtab, onehot_t):
    def body(d_ref, oh_ref, o_ref):
        i = pl.program_id(0)
        p = lax.dot_general(d_ref[...], oh_ref[...], (((1,), (1,)), ((), ())), precision=HIGHEST,
                            preferred_element_type=F32)

        @pl.when(i == 0)
        def _():
            o_ref[...] = p

        @pl.when(i > 0)
        def _():
            o_ref[...] += p

    return pl.pallas_call(
        body, name="bias_grad", grid=(NT_ALL // NT_TILE,),
        in_specs=[pl.BlockSpec((ATT_HEADS, NT_TILE), lambda i: (0, i)),
                  pl.BlockSpec((REL_BUCKETS, NT_TILE), lambda i: (0, i))],
        out_specs=pl.BlockSpec((ATT_HEADS, REL_BUCKETS), lambda i: (0, 0)),
        out_shape=jax.ShapeDtypeStruct((ATT_HEADS, REL_BUCKETS), F32),
        compiler_params=_cparams("arbitrary"))(dtab, onehot_t)


def _att_masks(n):
    far = 4 * BLOCK
    li = lax.broadcasted_iota(jnp.int32, (BLOCK, BLOCK), 0)
    ki = lax.broadcasted_iota(jnp.int32, (BLOCK, BLOCK), 1)
    m_meta = (ki >= PAD) & (li + jnp.where(n >= 1, far, 0) >= ki)
    li2 = lax.broadcasted_iota(jnp.int32, (BLOCK, 2 * BLOCK), 0)
    ki2 = lax.broadcasted_iota(jnp.int32, (BLOCK, 2 * BLOCK), 1)
    prev_ok = (ki2 < BLOCK) & (ki2 > li2 + jnp.where(n >= 2, 0, far))
    cur_ok = (ki2 >= BLOCK) & (ki2 - BLOCK <= li2 - jnp.where(n >= 1, 0, far))
    return m_meta, prev_ok | cur_ok


def _att_probs(qh, k_meta, k_band, b_meta, b_band, m_meta, m_band, sink):
    scale = ATT_HEADDIM ** -0.5
    s_m = jnp.where(m_meta, _dot_nt(qh, k_meta) * scale + b_meta, NEG)
    s_b = jnp.where(m_band, _dot_nt(qh, k_band) * scale + b_band, NEG)
    mx = jnp.maximum(jnp.maximum(jnp.max(s_m, axis=1, keepdims=True), jnp.max(s_b, axis=1, keepdims=True)), sink)
    p_m = jnp.exp(s_m - mx)
    p_b = jnp.exp(s_b - mx)
    p_s = jnp.exp(sink - mx)
    inv = 1.0 / (jnp.sum(p_m, axis=1, keepdims=True) + jnp.sum(p_b, axis=1, keepdims=True) + p_s)
    return p_m * inv, p_b * inv, p_s * inv


def _kv_cols(kind, kh):
    base = ATT_KV * kind + ATT_HEADDIM * kh
    return slice(base, base + ATT_HEADDIM)


def _att_specs(nb, rev):
    def nidx(i):
        return nb - 1 - i if rev else i

    kvb = ATT_Q // (2 * ATT_KV)
    q_s = pl.BlockSpec((BLOCK, ATT_Q), lambda i: (nidx(i), 0))
    cur = pl.BlockSpec((BLOCK, 2 * ATT_KV), lambda i: (nidx(i), kvb))
    prev = pl.BlockSpec((BLOCK, 2 * ATT_KV), lambda i: (jnp.maximum(nidx(i) - 1, 0), kvb))
    meta = pl.BlockSpec((BLOCK, 2 * ATT_KV), lambda i: (0, kvb))
    tmeta = pl.BlockSpec((1, ATT_HEADS, BLOCK, BLOCK), lambda i: (jnp.minimum(nidx(i), 2), 0, 0, 0))
    tband = pl.BlockSpec((ATT_HEADS, BLOCK, 2 * BLOCK), lambda i: (0, 0, 0))
    sink = pl.BlockSpec((1, LANES), lambda i: (0, 0))
    return q_s, cur, prev, meta, tmeta, tband, sink


def _attn_fwd(qkv, t_meta, t_band, sinks):
    n = qkv.shape[0]
    nb = n // BLOCK
    q_s, cur_s, prev_s, meta_s, tm_s, tb_s, sink_s = _att_specs(nb, False)

    def body(q_ref, cur_ref, prev_ref, meta_ref, tm_ref, tb_ref, sink_ref, o_ref):
        blk = pl.program_id(0)
        m_meta, m_band = _att_masks(blk)
        kv_band = jnp.concatenate([prev_ref[...], cur_ref[...]], axis=0).astype(BF16)
        kv_meta = meta_ref[...].astype(BF16)
        for kh in range(ATT_KV_HEADS):
            k_meta, v_meta = kv_meta[:, _kv_cols(0, kh)], kv_meta[:, _kv_cols(1, kh)]
            k_band, v_band = kv_band[:, _kv_cols(0, kh)], kv_band[:, _kv_cols(1, kh)]
            for gq in range(ATT_GQ):
                h = kh * ATT_GQ + gq
                cols = slice(ATT_HEADDIM * h, ATT_HEADDIM * (h + 1))
                qh = q_ref[:, cols].astype(BF16)
                p_m, p_b, _ = _att_probs(qh, k_meta, k_band, tm_ref[0, h], tb_ref[h], m_meta, m_band,
                                         sink_ref[0:1, h:h + 1])
                o_ref[:, cols] = (_dot(p_m.astype(BF16), v_meta) + _dot(p_b.astype(BF16), v_band)).astype(BF16)

    return pl.pallas_call(
        body, name="attn_fwd", grid=(nb,),
        in_specs=[q_s, cur_s, prev_s, meta_s, tm_s, tb_s, sink_s],
        out_specs=q_s,
        out_shape=jax.ShapeDtypeStruct((n, ATT_Q), BF16),
        compiler_params=_cparams("parallel"))(qkv, qkv, qkv, qkv, t_meta, t_band, sinks)


def _attn_bwd(datt, qkv, t_meta, t_band, sinks):
    n = qkv.shape[0]
    nb = n // BLOCK
    q_s, cur_s, prev_s, meta_s, tm_s, tb_s, sink_s = _att_specs(nb, True)
    dqkv_s = pl.BlockSpec((BLOCK, ATT_Q + 2 * ATT_KV), lambda i: (nb - 1 - i, 0))
    scale = ATT_HEADDIM ** -0.5

    def body(do_ref, q_ref, cur_ref, prev_ref, meta_ref, tm_ref, tb_ref, sink_ref,
             dqkv_ref, dtm_ref, dtb_ref, dsink_ref, carry_scr, meta_scr):
        step = pl.program_id(0)
        blk = nb - 1 - step
        m_meta, m_band = _att_masks(blk)
        kv_band = jnp.concatenate([prev_ref[...], cur_ref[...]], axis=0).astype(BF16)
        kv_meta = meta_ref[...].astype(BF16)

        @pl.when(step == 0)
        def _():
            carry_scr[...] = jnp.zeros_like(carry_scr)
            meta_scr[...] = jnp.zeros_like(meta_scr)
            dtb_ref[...] = jnp.zeros_like(dtb_ref)
            dsink_ref[...] = jnp.zeros_like(dsink_ref)

        @pl.when((step == 0) | (blk <= 1))
        def _():
            dtm_ref[...] = jnp.zeros_like(dtm_ref)

        dsink = jnp.zeros((1, LANES), F32)
        dkv_band = [None] * (2 * ATT_KV_HEADS)
        dkv_meta = [None] * (2 * ATT_KV_HEADS)
        for kh in range(ATT_KV_HEADS):
            k_meta, v_meta = kv_meta[:, _kv_cols(0, kh)], kv_meta[:, _kv_cols(1, kh)]
            k_band, v_band = kv_band[:, _kv_cols(0, kh)], kv_band[:, _kv_cols(1, kh)]
            dk_m = dv_m = dk_b = dv_b = None
            for gq in range(ATT_GQ):
                h = kh * ATT_GQ + gq
                cols = slice(ATT_HEADDIM * h, ATT_HEADDIM * (h + 1))
                qh = q_ref[:, cols].astype(BF16)
                doh = do_ref[:, cols].astype(BF16)
                p_m, p_b, p_s = _att_probs(qh, k_meta, k_band, tm_ref[0, h], tb_ref[h], m_meta, m_band,
                                           sink_ref[0:1, h:h + 1])
                dp_m = _dot_nt(doh, v_meta)
                dp_b = _dot_nt(doh, v_band)
                delta = jnp.sum(p_m * dp_m, axis=1, keepdims=True) + jnp.sum(p_b * dp_b, axis=1, keepdims=True)
                ds_m = p_m * (dp_m - delta)
                ds_b = p_b * (dp_b - delta)
                dsink = _lane_put(dsink, dsink[0:1, h:h + 1] - jnp.sum(p_s * delta, axis=0, keepdims=True), h)
                dtm_ref[0, h] += ds_m
                dtb_ref[h] += ds_b
                ds_mb, ds_bb = ds_m.astype(BF16), ds_b.astype(BF16)
                dqkv_ref[:, cols] = ((_dot(ds_mb, k_meta) + _dot(ds_bb, k_band)) * scale).astype(BF16)
                parts = (_dot_tn(ds_mb, qh) * scale, _dot_tn(p_m.astype(BF16), doh),
                         _dot_tn(ds_bb, qh) * scale, _dot_tn(p_b.astype(BF16), doh))
                if gq == 0:
                    dk_m, dv_m, dk_b, dv_b = parts
                else:
                    dk_m, dv_m, dk_b, dv_b = dk_m + parts[0], dv_m + parts[1], dk_b + parts[2], dv_b + parts[3]
            dkv_meta[kh], dkv_meta[ATT_KV_HEADS + kh] = dk_m, dv_m
            dkv_band[kh], dkv_band[ATT_KV_HEADS + kh] = dk_b, dv_b
        dsink_ref[...] += dsink
        band = jnp.concatenate(dkv_band, axis=1)
        meta_scr[...] += jnp.concatenate(dkv_meta, axis=1)
        own = band[BLOCK:, :] + carry_scr[...]
        carry_scr[...] = band[:BLOCK, :]

        @pl.when(blk > 0)
        def _():
            dqkv_ref[:, ATT_Q:] = own.astype(BF16)

        @pl.when(blk == 0)
        def _():
            dqkv_ref[:, ATT_Q:] = (own + meta_scr[...]).astype(BF16)

    return pl.pallas_call(
        body, name="attn_bwd", grid=(nb,),
        in_specs=[q_s, q_s, cur_s, prev_s, meta_s, tm_s, tb_s, sink_s],
        out_specs=[dqkv_s, tm_s, tb_s, sink_s],
        out_shape=[jax.ShapeDtypeStruct((n, ATT_Q + 2 * ATT_KV), BF16),
                   jax.ShapeDtypeStruct((3, ATT_HEADS, BLOCK, BLOCK), F32),
                   jax.ShapeDtypeStruct((ATT_HEADS, BLOCK, 2 * BLOCK), F32),
                   jax.ShapeDtypeStruct((1, LANES), F32)],
        scratch_shapes=[pltpu.VMEM((BLOCK, 2 * ATT_KV), F32), pltpu.VMEM((BLOCK, 2 * ATT_KV), F32)],
        compiler_params=_cparams("arbitrary"))(datt, qkv, qkv, qkv, qkv, t_meta, t_band, sinks)


def _merge_fwd(gates, y_ssd, y_att, gate_b):
    n = gates.shape[0]
    tm = _row_tile(n, 832)

    def body(gs_ref, ga_ref, ys_ref, ya_ref, gb_ref, o_ref):
        o_ref[...] = (jax.nn.sigmoid(gs_ref[...] + gb_ref[0:1, :]) * ys_ref[...]
                      + jax.nn.sigmoid(ga_ref[...] + gb_ref[1:2, :]) * ya_ref[...]).astype(BF16)

    row = pl.BlockSpec((tm, D_MODEL), lambda i: (i, 0))
    return pl.pallas_call(
        body, name="merge_fwd", grid=(n // tm,),
        in_specs=[row, pl.BlockSpec((tm, D_MODEL), lambda i: (i, 1)), row, row,
                  pl.BlockSpec((2, D_MODEL), lambda i: (0, 0))],
        out_specs=row, out_shape=jax.ShapeDtypeStruct((n, D_MODEL), BF16),
        compiler_params=_cparams("parallel"))(gates, gates, y_ssd, y_att, gate_b)


def _merge_bwd(dm, gates, y_ssd, y_att, gate_b):
    n = gates.shape[0]
    tm = _row_tile(n, 832)

    def body(dm_ref, gs_ref, ga_ref, ys_ref, ya_ref, gb_ref, dys_ref, dya_ref, dg_ref, dgb_ref):
        i = pl.program_id(0)
        dmv = dm_ref[...]
        ss = jax.nn.sigmoid(gs_ref[...] + gb_ref[0:1, :])
        sa = jax.nn.sigmoid(ga_ref[...] + gb_ref[1:2, :])
        dys_ref[...] = (dmv * ss).astype(BF16)
        dya_ref[...] = (dmv * sa).astype(BF16)
        dgs = dmv * ys_ref[...] * ss * (1.0 - ss)
        dga = dmv * ya_ref[...] * sa * (1.0 - sa)
        dg_ref[:, :D_MODEL] = dgs.astype(BF16)
        dg_ref[:, D_MODEL:] = dga.astype(BF16)
        part = jnp.concatenate([jnp.sum(dgs, axis=0, keepdims=True), jnp.sum(dga, axis=0, keepdims=True)], axis=0)

        @pl.when(i == 0)
        def _():
            dgb_ref[...] = part

        @pl.when(i > 0)
        def _():
            dgb_ref[...] += part

    row = pl.BlockSpec((tm, D_MODEL), lambda i: (i, 0))
    gb = pl.BlockSpec((2, D_MODEL), lambda i: (0, 0))
    return pl.pallas_call(
        body, name="merge_bwd", grid=(n // tm,),
        in_specs=[row, row, pl.BlockSpec((tm, D_MODEL), lambda i: (i, 1)), row, row, gb],
        out_specs=[row, row, pl.BlockSpec((tm, 2 * D_MODEL), lambda i: (i, 0)), gb],
        out_shape=[jax.ShapeDtypeStruct((n, D_MODEL), BF16), jax.ShapeDtypeStruct((n, D_MODEL), BF16),
                   jax.ShapeDtypeStruct((n, 2 * D_MODEL), BF16), jax.ShapeDtypeStruct((2, D_MODEL), F32)],
        compiler_params=_cparams("arbitrary"))(dm, gates, gates, y_ssd, y_att, gate_b)


def _col_move(srcs, outs, pieces, *, name):
    rows = srcs[0].shape[-2]
    tr = _row_tile(rows, 128)
    n_src = len(srcs)
    covered = [sum(p[6] for p in pieces if p[0] == o) for o in range(len(outs))]
    total = [int(np.prod(shp)) // rows for shp, _ in outs]

    def body(*refs):
        in_refs, out_refs = refs[:n_src], refs[n_src:]
        for o, ref in enumerate(out_refs):
            if covered[o] != total[o]:
                ref[...] = jnp.zeros_like(ref)
        for o, ol, oc, s, sl, sc, width in pieces:
            val = in_refs[s][:, sc:sc + width] if sl is None else in_refs[s][sl, :, sc:sc + width]
            val = val.astype(outs[o][1])
            if ol is None:
                out_refs[o][:, oc:oc + width] = val
            else:
                out_refs[o][ol, :, oc:oc + width] = val

    def spec(shape):
        if len(shape) == 2:
            return pl.BlockSpec((tr, shape[1]), lambda i: (i, 0))
        return pl.BlockSpec((shape[0], tr, shape[2]), lambda i: (0, i, 0))

    return pl.pallas_call(
        body, name=name, grid=(rows // tr,),
        in_specs=[spec(a.shape) for a in srcs], out_specs=[spec(shp) for shp, _ in outs],
        out_shape=[jax.ShapeDtypeStruct(shp, dt) for shp, dt in outs],
        compiler_params=_cparams("parallel"))(*srcs)


def _shard_pieces(seg_ranges, shard_w):
    out = []
    for seg, runs in enumerate(seg_ranges):
        for g0, width, s0 in runs:
            done = 0
            while done < width:
                dev, col = divmod(g0 + done, shard_w)
                take = min(width - done, shard_w - col)
                out.append((seg, s0 + done, dev, col, take))
                done += take
    return out


_CHIP_RELATIONS = [(1, 0, 0), (0, 1, 0), (1, 1, 0)]
N_CHIPS = 4


def _gather_two_level(arrays, *, name):
    n_arr = len(arrays)
    n_pair = 1 + 2 * len(_CHIP_RELATIONS)

    def body(*refs):
        ins, outs = refs[:n_arr], refs[n_arr:2 * n_arr]
        send_sems, recv_sems, local_sems = refs[2 * n_arr:]
        x, y, c = lax.axis_index("x"), lax.axis_index("y"), lax.axis_index("c")
        sibling = (x, y, 1 - c)
        chips = [(x ^ dx, y ^ dy) for dx, dy, _ in _CHIP_RELATIONS]

        def copy(a, k, block, to, src=None):
            slot = outs[a].at[2 * block[0] + block[1], block[2]]
            return pltpu.make_async_remote_copy(
                src_ref=slot if src is None else src, dst_ref=slot, send_sem=send_sems.at[a * n_pair + k],
                recv_sem=recv_sems.at[a * n_pair + k], device_id=to, device_id_type=MESH)

        sends, locals_ = [], []
        for a in range(n_arr):
            mine = pltpu.make_async_copy(ins[a], outs[a].at[2 * x + y, c], local_sems.at[a])
            mine.start()
            locals_.append(mine)
            first = [copy(a, 0, (x, y, c), sibling, src=ins[a])]
            first += [copy(a, 1 + j, (x, y, c), (*chip, c), src=ins[a]) for j, chip in enumerate(chips)]
            for cp in first:
                cp.start()
            sends += first
        for j, chip in enumerate(chips):
            for a in range(n_arr):
                copy(a, 1 + j, (*chip, c), (x, y, c)).wait_recv()
                passed = copy(a, 1 + len(chips) + j, (*chip, c), sibling)
                passed.start()
                sends.append(passed)
        for a in range(n_arr):
            copy(a, 0, (x, y, 1 - c), (x, y, c)).wait_recv()
            for j, chip in enumerate(chips):
                copy(a, 1 + len(chips) + j, (*chip, 1 - c), (x, y, c)).wait_recv()
        for cp in sends:
            cp.wait_send()
        for mine in locals_:
            mine.wait()

    any_spec = pl.BlockSpec(memory_space=pl.ANY)
    outs = pl.pallas_call(
        body, name=name, in_specs=[any_spec] * n_arr, out_specs=[any_spec] * n_arr,
        out_shape=[jax.ShapeDtypeStruct((N_CHIPS, 2) + a.shape, a.dtype) for a in arrays],
        scratch_shapes=[pltpu.SemaphoreType.DMA((n_arr * n_pair,)), pltpu.SemaphoreType.DMA((n_arr * n_pair,)),
                        pltpu.SemaphoreType.DMA((n_arr,))],
    )(*arrays)
    return [o.reshape((N_DEV,) + a.shape) for o, a in zip(outs, arrays)]


def _sibling_exchange(arrays, scatter, *, name):
    n_arr = len(arrays)

    def body(*refs):
        ins, outs = refs[:n_arr], refs[n_arr:2 * n_arr]
        send_sems, recv_sems = refs[2 * n_arr:]
        x, y, c = lax.axis_index("x"), lax.axis_index("y"), lax.axis_index("c")
        copies = []
        for a in range(n_arr):
            for q in range(N_CHIPS if scatter[a] else 1):
                src = ins[a].at[2 * q + 1 - c] if scatter[a] else ins[a]
                dst = outs[a].at[q] if scatter[a] else outs[a]
                cp = pltpu.make_async_remote_copy(
                    src_ref=src, dst_ref=dst, send_sem=send_sems.at[a * N_CHIPS + q],
                    recv_sem=recv_sems.at[a * N_CHIPS + q], device_id=(x, y, 1 - c), device_id_type=MESH)
                cp.start()
                copies.append(cp)
        for cp in copies:
            cp.wait_send()
        for cp in copies:
            cp.wait_recv()

    any_spec = pl.BlockSpec(memory_space=pl.ANY)
    return pl.pallas_call(
        body, name=name, in_specs=[any_spec] * n_arr, out_specs=[any_spec] * n_arr,
        out_shape=[jax.ShapeDtypeStruct(((N_CHIPS,) + a.shape[1:]) if s else a.shape, a.dtype)
                   for a, s in zip(arrays, scatter)],
        scratch_shapes=[pltpu.SemaphoreType.DMA((n_arr * N_CHIPS,)), pltpu.SemaphoreType.DMA((n_arr * N_CHIPS,))],
    )(*arrays)


def _pair_sum(mine, sib, *, name, out_dtype):
    _, rows, cols = mine.shape
    tr = _row_tile(rows, 128)

    def body(m_ref, s_ref, o_ref):
        c = lax.axis_index("c")
        o_ref[0] = (m_ref[0, c] + s_ref[0]).astype(out_dtype)

    return pl.pallas_call(
        body, name=name, grid=(N_CHIPS, rows // tr),
        in_specs=[pl.BlockSpec((1, 2, tr, cols), lambda q, i: (q, 0, i, 0)),
                  pl.BlockSpec((1, tr, cols), lambda q, i: (q, i, 0))],
        out_specs=pl.BlockSpec((1, tr, cols), lambda q, i: (q, i, 0)),
        out_shape=jax.ShapeDtypeStruct((N_CHIPS, rows, cols), out_dtype),
        compiler_params=_cparams("parallel", "parallel"))(mine.reshape(N_CHIPS, 2, rows, cols), sib)


def _add(a, b, *, name):
    rows, cols = a.shape
    tr = _row_tile(rows, 256)

    def body(a_ref, b_ref, o_ref):
        o_ref[...] = a_ref[...] + b_ref[...]

    blk = pl.BlockSpec((tr, cols), lambda i: (i, 0))
    return pl.pallas_call(body, name=name, grid=(rows // tr,), in_specs=[blk, blk], out_specs=blk,
                          out_shape=jax.ShapeDtypeStruct(a.shape, a.dtype), compiler_params=_cparams("parallel"))(a, b)


def _chip_exchange(arrays, scatter, *, name):
    n_arr = len(arrays)
    n_rel = len(_CHIP_RELATIONS)

    def body(*refs):
        ins, outs = refs[:n_arr], refs[n_arr:2 * n_arr]
        send_sems, recv_sems, local_sems = refs[2 * n_arr:]
        x, y, c = lax.axis_index("x"), lax.axis_index("y"), lax.axis_index("c")
        me = 2 * x + y
        copies = []
        for a in range(n_arr):
            src = ins[a].at[me] if scatter[a] else ins[a]
            local = pltpu.make_async_copy(src, outs[a].at[me], local_sems.at[a])
            local.start()
            copies.append(local)
        remote = []
        for k, (dx, dy, dc) in enumerate(_CHIP_RELATIONS):
            px, py, pc = x ^ dx, y ^ dy, c ^ dc
            peer = 2 * px + py
            for a in range(n_arr):
                src = ins[a].at[peer] if scatter[a] else ins[a]
                cp = pltpu.make_async_remote_copy(
                    src_ref=src, dst_ref=outs[a].at[me], send_sem=send_sems.at[a * n_rel + k],
                    recv_sem=recv_sems.at[a * n_rel + k], device_id=(px, py, pc), device_id_type=MESH)
                cp.start()
                remote.append((cp, a, k, peer))
        for cp, a, k, peer in remote:
            cp.wait_send()
        for cp, a, k, peer in remote:
            src = ins[a].at[peer] if scatter[a] else ins[a]
            pltpu.make_async_remote_copy(
                src_ref=src, dst_ref=outs[a].at[peer], send_sem=send_sems.at[a * n_rel + k],
                recv_sem=recv_sems.at[a * n_rel + k], device_id=(x, y, c), device_id_type=MESH).wait_recv()
        for local in copies:
            local.wait()

    out_shape = [jax.ShapeDtypeStruct((N_CHIPS,) + (a.shape[1:] if s else a.shape), a.dtype)
                 for a, s in zip(arrays, scatter)]
    any_spec = pl.BlockSpec(memory_space=pl.ANY)
    return pl.pallas_call(
        body, name=name, in_specs=[any_spec] * n_arr, out_specs=[any_spec] * n_arr, out_shape=out_shape,
        scratch_shapes=[pltpu.SemaphoreType.DMA((n_arr * n_rel,)), pltpu.SemaphoreType.DMA((n_arr * n_rel,)),
                        pltpu.SemaphoreType.DMA((n_arr,))],
    )(*arrays)


def _adamw(w, gslots, m, v, *, name):
    rows, cols = w.shape
    n_slots = gslots.shape[0]
    tr = _row_tile(rows, 128) if rows % 16 == 0 else rows

    def body(w_ref, g_ref, m_ref, v_ref, go_ref, d_ref, mo_ref, vo_ref):
        g = g_ref[0].astype(F32)
        for s in range(1, n_slots):
            g = g + g_ref[s].astype(F32)
        mn = ADAM_B1 * m_ref[...] + (1.0 - ADAM_B1) * g
        vn = ADAM_B2 * v_ref[...] + (1.0 - ADAM_B2) * (g * g)
        go_ref[...] = g
        mo_ref[...] = mn
        vo_ref[...] = vn
        m_hat = mn / (1.0 - ADAM_B1 ** ADAM_STEP)
        v_hat = vn / (1.0 - ADAM_B2 ** ADAM_STEP)
        d_ref[...] = -ADAM_LR * (m_hat / (jnp.sqrt(v_hat) + ADAM_EPS) + ADAM_WD * w_ref[...])

    blk = pl.BlockSpec((tr, cols), lambda i: (i, 0))
    shp = jax.ShapeDtypeStruct((rows, cols), F32)
    return pl.pallas_call(
        body, name=name, grid=(rows // tr,),
        in_specs=[blk, pl.BlockSpec((n_slots, tr, cols), lambda i: (0, i, 0)), blk, blk],
        out_specs=[blk] * 4, out_shape=[shp] * 4,
        compiler_params=_cparams("parallel"))(w, gslots, m, v)


_BIG = ("w_in", "w_ssd_branch", "w_attn_branch", "w_out", "w_ffn_in", "w_ffn_out")
_SMALL_SHARDED = ("meta_tokens", "ssd_conv_w", "gate_b", "ffn_conv_w")
_SMALL_REPLICATED = ("norm_mix_w", "ssd_conv_b", "ssd_dt_bias", "ssd_a_log", "ssd_d", "ssd_norm_w", "attn_sinks",
                     "rel_bias", "norm_ffn_w", "ffn_conv_b", "norm_final_w")
_WEIGHTS = ("meta_tokens", "norm_mix_w", "w_in", "ssd_conv_w", "ssd_conv_b", "ssd_dt_bias", "ssd_a_log", "ssd_d",
            "ssd_norm_w", "w_ssd_branch", "w_attn_branch", "attn_sinks", "rel_bias", "gate_b", "w_out", "norm_ffn_w",
            "w_ffn_in", "ffn_conv_w", "ffn_conv_b", "w_ffn_out", "norm_final_w")
_ROW_SHARDED = ("w_ssd_branch", "w_attn_branch", "w_out", "w_ffn_out")
_COL_SHARDED = ("w_in", "w_ffn_in", "meta_tokens", "ssd_conv_w", "gate_b", "ffn_conv_w")
_IN_SEGS = (("z", SSD_INNER), ("xbc", SSD_XBC), ("dt", SSD_HEADS), ("qkv", ATT_Q + 2 * ATT_KV), ("g", 2 * D_MODEL))


def _pack_rows(flat_parts, width, row_mult):
    flat = jnp.concatenate([p.reshape(-1) for p in flat_parts])
    pad = (-flat.shape[0]) % (width * row_mult)
    if pad:
        flat = jnp.concatenate([flat, jnp.zeros((pad,), flat.dtype)])
    return flat.reshape(-1, width)


def _unpack(flat, shapes):
    out, off = [], 0
    for shp in shapes:
        size = int(np.prod(shp))
        out.append(flat[off:off + size].reshape(shp))
        off += size
    return out


def _gather_full(stack, name, shard_shape):
    if name in _COL_SHARDED:
        return jnp.transpose(stack, (1, 0, 2)).reshape(shard_shape[0], N_DEV * shard_shape[1])
    return stack.reshape(N_DEV * shard_shape[0], shard_shape[1])


_IN_SEG_W = {"z": SSD_INNER, "xbc": SSD_XBC, "dt": DT_W, "qkv": ATT_Q + 2 * ATT_KV, "g": 2 * D_MODEL}
_IN_SHARD_W = (SSD_INNER + SSD_XBC + SSD_HEADS + ATT_Q + 2 * ATT_KV + 2 * D_MODEL) // N_DEV
_FFN_SHARD_W = 2 * D_FF // N_DEV


def _in_seg_runs():
    runs, off = [], 0
    for nm, width in _IN_SEGS:
        if nm == "dt":
            runs.append([(off + SSD_HPG * g, SSD_HPG, LANES * g) for g in range(SSD_GROUPS)])
        else:
            runs.append([(off, width, 0)])
        off += width
    return runs


def _w_in_to_segments(stack):
    pieces = [(seg, None, scol, 0, dev, col, w) for seg, scol, dev, col, w in _shard_pieces(_in_seg_runs(), _IN_SHARD_W)]
    outs = [((D_MODEL, _IN_SEG_W[nm]), stack.dtype) for nm, _ in _IN_SEGS]
    return dict(zip([nm for nm, _ in _IN_SEGS], _col_move([stack], outs, pieces, name="w_in_segments")))


def _segments_to_w_in_shards(seg_grads):
    pieces = [(0, dev, col, seg, None, scol, w) for seg, scol, dev, col, w in _shard_pieces(_in_seg_runs(), _IN_SHARD_W)]
    return _col_move(seg_grads, [((N_DEV, D_MODEL, _IN_SHARD_W), F32)], pieces, name="g_w_in_shards")[0]


def _ffn_in_from_shards(stack):
    pieces = [(0, None, scol, 0, dev, col, w)
              for _, scol, dev, col, w in _shard_pieces([[(0, 2 * D_FF, 0)]], _FFN_SHARD_W)]
    return _col_move([stack], [((D_MODEL, 2 * D_FF), stack.dtype)], pieces, name="w_ffn_in_full")[0]


def _ffn_in_to_shards(g_up, g_gate):
    pieces = [(0, dev, col, seg, None, scol, w)
              for seg, scol, dev, col, w in _shard_pieces([[(0, D_FF, 0)], [(D_FF, D_FF, 0)]], _FFN_SHARD_W)]
    return _col_move([g_up, g_gate], [((N_DEV, D_MODEL, _FFN_SHARD_W), F32)], pieces, name="g_w_ffn_in_shards")[0]


def _dt_spread(w_dt):
    k = w_dt.shape[0]
    w4 = w_dt.reshape(k, SSD_GROUPS, SSD_HPG)
    return jnp.pad(w4, ((0, 0), (0, 0), (0, LANES - SSD_HPG))).reshape(k, DT_W)


def _dt_gather(w_wide):
    k = w_wide.shape[0]
    return w_wide.reshape(k, SSD_GROUPS, LANES)[:, :, :SSD_HPG].reshape(k, SSD_HEADS)


def _local_step(x, target, w):
    h0 = jnp.concatenate([jnp.zeros((PAD, D_MODEL), F32), w["meta_tokens"], x], axis=0)
    segs = w["in_segs"]
    w_ffn_up, w_ffn_gate = w["w_ffn_in"][:, :D_FF], w["w_ffn_in"][:, D_FF:]

    dtb = _dt_spread(w["ssd_dt_bias"])
    alog = _dt_spread(w["ssd_a_log"])
    dskip_w = jnp.repeat(w["ssd_d"], SSD_HEADDIM, axis=1)
    sinks = jnp.pad(w["attn_sinks"], ((0, 0), (0, LANES - ATT_HEADS)))
    onehot_t = _onehot_t()
    tabs = _bias_tables(w["rel_bias"].T, onehot_t)
    t_band = tabs[:, :NT_BAND].reshape(ATT_HEADS, BLOCK, 2 * BLOCK)
    t_meta = jnp.transpose(tabs[:, NT_BAND:].reshape(ATT_HEADS, 3, BLOCK, BLOCK), (1, 0, 2, 3))

    u = _rms_fwd(h0, w["norm_mix_w"], name="rms_mix_fwd")
    z = _mm(u, segs["z"], name="in_z")
    xbc = _mm(u, segs["xbc"], name="in_xbc")
    dt_raw = _mm(u, segs["dt"], name="in_dt")
    qkv = _mm(u, segs["qkv"], name="in_qkv")
    gates = _mm(u, segs["g"], name="in_g")
    pre = _conv_fwd(xbc, w["ssd_conv_w"], w["ssd_conv_b"], name="ssd_conv_fwd")
    y, yn, hsave = _ssd_fwd(pre, dt_raw, z, dtb, alog, dskip_w, w["ssd_norm_w"])
    y_ssd = _mm(yn, w["w_ssd_branch"], name="ssd_out")
    att = _attn_fwd(qkv, t_meta, t_band, sinks)
    y_att = _mm(att, w["w_attn_branch"], name="att_out")
    merged = _merge_fwd(gates, y_ssd, y_att, w["gate_b"])
    h1 = _mm(merged, w["w_out"], c=h0, mask=True, name="mix_out")
    u2 = _rms_fwd(h1, w["norm_ffn_w"], name="rms_ffn_fwd")
    hid_raw = _mm(u2, w["w_ffn_in"], name="ffn_in")
    hid_up, hid_gate, act = _ffn_act_fwd(hid_raw, w["ffn_conv_w"], w["ffn_conv_b"])
    h2 = _mm(act, w["w_ffn_out"], c=h1, mask=True, name="ffn_out")
    dh2, loss_row, g_norm_final = _final_loss(h2, w["norm_final_w"], target)

    grads = {"norm_final_w": g_norm_final}
    dact = _mm(dh2, w["w_ffn_out"], tb=True, mask=True, name="d_act")
    grads["w_ffn_out"] = _mm(act, dh2, ta=True, mask=True, name="g_w_ffn_out")
    dx_up, dx_gate, dcw_up, dcw_gate, dcb_up, dcb_gate = _ffn_act_bwd(dact, hid_up, hid_gate, hid_raw, w["ffn_conv_w"])
    grads["ffn_conv_w"] = jnp.concatenate([dcw_up, dcw_gate], axis=1)
    grads["ffn_conv_b"] = jnp.concatenate([dcb_up, dcb_gate], axis=1)
    du2 = _mm(dx_up, w_ffn_up, tb=True, name="d_u2_up")
    du2 = _mm(dx_gate, w_ffn_gate, tb=True, c=du2, name="d_u2_gate")
    grads["w_ffn_in"] = (_mm(u2, dx_up, ta=True, name="g_w_ffn_up"), _mm(u2, dx_gate, ta=True, name="g_w_ffn_gate"))
    dh1, grads["norm_ffn_w"] = _rms_bwd(h1, w["norm_ffn_w"], du2, dh2, name="rms_ffn_bwd")

    dmerged = _mm(dh1, w["w_out"], tb=True, mask=True, name="d_merged")
    grads["w_out"] = _mm(merged, dh1, ta=True, mask=True, name="g_w_out")
    dy_ssd, dy_att, dgates, grads["gate_b"] = _merge_bwd(dmerged, gates, y_ssd, y_att, w["gate_b"])
    dyn = _mm(dy_ssd, w["w_ssd_branch"], tb=True, name="d_yn")
    grads["w_ssd_branch"] = _mm(yn, dy_ssd, ta=True, name="g_w_ssd")
    datt = _mm(dy_att, w["w_attn_branch"], tb=True, name="d_att")
    grads["w_attn_branch"] = _mm(att, dy_att, ta=True, name="g_w_att")
    dz, dpxs, dpb, dpc, ddt, grads["ssd_norm_w"], g_dtb, g_alog, g_dskip = _ssd_bwd(
        dyn, y, z, pre, dt_raw, hsave, dtb, alog, dskip_w, w["ssd_norm_w"])
    grads["ssd_dt_bias"] = _dt_gather(g_dtb)
    grads["ssd_a_log"] = _dt_gather(g_alog)
    grads["ssd_d"] = _dt_gather(g_dskip)
    dpre = jnp.concatenate([dpxs, dpb, dpc], axis=1)
    dxbc, grads["ssd_conv_w"], grads["ssd_conv_b"] = _conv_bwd(dpre, xbc, w["ssd_conv_w"], name="ssd_conv_bwd")
    dqkv, d_tmeta, d_tband, d_sinks = _attn_bwd(datt, qkv, t_meta, t_band, sinks)
    grads["attn_sinks"] = d_sinks[:, :ATT_HEADS]
    dtab = jnp.concatenate([d_tband.reshape(ATT_HEADS, NT_BAND),
                            jnp.transpose(d_tmeta, (1, 0, 2, 3)).reshape(ATT_HEADS, 3 * NT_META)], axis=1)
    grads["rel_bias"] = _bias_grad(dtab, onehot_t).T
    dsegs = {"z": dz, "xbc": dxbc, "dt": ddt, "qkv": dqkv, "g": dgates}
    du, g_in = None, []
    for nm, _ in _IN_SEGS:
        du = _mm(dsegs[nm], segs[nm], tb=True, c=du, name="d_u_" + nm)
        g_in.append(_mm(u, dsegs[nm], ta=True, name="g_w_in_" + nm))
    grads["in_segs"] = g_in
    dh0, grads["norm_mix_w"] = _rms_bwd(h0, w["norm_mix_w"], du, dh1, name="rms_mix_bwd")
    grads["meta_tokens"] = dh0[PAD:BLOCK]
    return loss_row[0, 0], dh0[BLOCK:], grads


def kernel(x, meta_tokens, norm_mix_w, w_in, ssd_conv_w, ssd_conv_b, ssd_dt_bias, ssd_a_log, ssd_d, ssd_norm_w, w_ssd_branch, w_attn_branch, attn_sinks, rel_bias, gate_b, w_out, norm_ffn_w, w_ffn_in, ffn_conv_w, ffn_conv_b, w_ffn_out, norm_final_w, loss_target, m_meta_tokens, m_norm_mix_w, m_w_in, m_ssd_conv_w, m_ssd_conv_b, m_ssd_dt_bias, m_ssd_a_log, m_ssd_d, m_ssd_norm_w, m_w_ssd_branch, m_w_attn_branch, m_attn_sinks, m_rel_bias, m_gate_b, m_w_out, m_norm_ffn_w, m_w_ffn_in, m_ffn_conv_w, m_ffn_conv_b, m_w_ffn_out, m_norm_final_w, v_meta_tokens, v_norm_mix_w, v_w_in, v_ssd_conv_w, v_ssd_conv_b, v_ssd_dt_bias, v_ssd_a_log, v_ssd_d, v_ssd_norm_w, v_w_ssd_branch, v_w_attn_branch, v_attn_sinks, v_rel_bias, v_gate_b, v_w_out, v_norm_ffn_w, v_w_ffn_in, v_ffn_conv_w, v_ffn_conv_b, v_w_ffn_out, v_norm_final_w):
    shard = dict(meta_tokens=meta_tokens, norm_mix_w=norm_mix_w, w_in=w_in, ssd_conv_w=ssd_conv_w,
                 ssd_conv_b=ssd_conv_b, ssd_dt_bias=ssd_dt_bias, ssd_a_log=ssd_a_log, ssd_d=ssd_d,
                 ssd_norm_w=ssd_norm_w, w_ssd_branch=w_ssd_branch, w_attn_branch=w_attn_branch,
                 attn_sinks=attn_sinks, rel_bias=rel_bias, gate_b=gate_b, w_out=w_out, norm_ffn_w=norm_ffn_w,
                 w_ffn_in=w_ffn_in, ffn_conv_w=ffn_conv_w, ffn_conv_b=ffn_conv_b, w_ffn_out=w_ffn_out,
                 norm_final_w=norm_final_w)
    mom_m = dict(zip(_WEIGHTS, (m_meta_tokens, m_norm_mix_w, m_w_in, m_ssd_conv_w, m_ssd_conv_b, m_ssd_dt_bias,
                                m_ssd_a_log, m_ssd_d, m_ssd_norm_w, m_w_ssd_branch, m_w_attn_branch, m_attn_sinks,
                                m_rel_bias, m_gate_b, m_w_out, m_norm_ffn_w, m_w_ffn_in, m_ffn_conv_w, m_ffn_conv_b,
                                m_w_ffn_out, m_norm_final_w)))
    mom_v = dict(zip(_WEIGHTS, (v_meta_tokens, v_norm_mix_w, v_w_in, v_ssd_conv_w, v_ssd_conv_b, v_ssd_dt_bias,
                                v_ssd_a_log, v_ssd_d, v_ssd_norm_w, v_w_ssd_branch, v_w_attn_branch, v_attn_sinks,
                                v_rel_bias, v_gate_b, v_w_out, v_norm_ffn_w, v_w_ffn_in, v_ffn_conv_w, v_ffn_conv_b,
                                v_w_ffn_out, v_norm_final_w)))
    orig_shape = {k: a.shape for k, a in shard.items()}
    two_d = {k: a.reshape(a.shape[-2:]) if a.ndim >= 2 else a.reshape(1, -1) for k, a in shard.items()}
    shape2 = {k: a.shape for k, a in two_d.items()}

    def as2d(tree):
        return {k: tree[k].reshape(shape2[k]) for k in _WEIGHTS}

    mom_m, mom_v = as2d(mom_m), as2d(mom_v)

    def row_pack(tree):
        return jnp.concatenate([tree[k] for k in _ROW_SHARDED], axis=0)

    small_pack = _pack_rows([two_d[k] for k in _SMALL_SHARDED], LANES, SMALL_ROW_MULT)
    w_in_all, w_ffn_in_all, rows_all, small_all = _gather_two_level(
        [two_d["w_in"].astype(BF16), two_d["w_ffn_in"].astype(BF16), row_pack(two_d).astype(BF16), small_pack],
        name="gather_weights")
    full = {k: two_d[k] for k in _SMALL_REPLICATED}
    full["in_segs"] = _w_in_to_segments(w_in_all)
    full["w_ffn_in"] = _ffn_in_from_shards(w_ffn_in_all)
    off = 0
    for k in _ROW_SHARDED:
        r = shape2[k][0]
        full[k] = rows_all[:, off:off + r].reshape(N_DEV * r, D_MODEL)
        off += r
    small_flat = small_all.reshape(N_DEV, -1)
    off = 0
    for k in _SMALL_SHARDED:
        size = int(np.prod(shape2[k]))
        full[k] = _gather_full(small_flat[:, off:off + size].reshape((N_DEV,) + shape2[k]), k, shape2[k])
        off += size

    loss_local, grad_x, grads = _local_step(x[0], loss_target[0], full)

    rows_send = jnp.concatenate([grads[k].reshape(N_DEV, shape2[k][0], D_MODEL) for k in _ROW_SHARDED], axis=1)
    small_names = _SMALL_SHARDED + _SMALL_REPLICATED
    small_send = _pack_rows([grads[k] for k in small_names] + [loss_local.reshape(1)], LANES, SMALL_ROW_MULT)
    big_send = [_segments_to_w_in_shards(grads["in_segs"]), _ffn_in_to_shards(*grads["w_ffn_in"]), rows_send]
    from_sib = _sibling_exchange(big_send + [small_send], [True, True, True, False], name="grads_to_sibling")
    parts = [_pair_sum(mine, sib, name="pair_sum_" + nm, out_dtype=BF16)
             for nm, mine, sib in zip(("w_in", "w_ffn_in", "rows"), big_send, from_sib)]
    parts.append(_add(small_send, from_sib[3], name="pair_sum_small"))
    in_recv, ffn_recv, rows_recv, small_recv = _chip_exchange(parts, [True, True, True, False], name="exchange_grads")

    big = {"w_in": _adamw(two_d["w_in"], in_recv, mom_m["w_in"], mom_v["w_in"], name="adamw_w_in"),
           "w_ffn_in": _adamw(two_d["w_ffn_in"], ffn_recv, mom_m["w_ffn_in"], mom_v["w_ffn_in"], name="adamw_w_ffn_in")}
    rows_out = _adamw(row_pack(two_d), rows_recv, row_pack(mom_m), row_pack(mom_v), name="adamw_rows")
    off = 0
    for k in _ROW_SHARDED:
        r = shape2[k][0]
        big[k] = [a[off:off + r] for a in rows_out]
        off += r
    me =4 * lax.axis_index("x") + 2 * lax.axis_index("y") + lax.axis_index("c")
    small_full_shapes = [grads[k].shape for k in small_names]
    n_small = sum(int(np.prod(s)) for s in small_full_shapes)

    def packed_small(tree):
        parts = []
        for k in small_names:
            a = tree[k]
            if k in _SMALL_SHARDED:
                fullw = jnp.zeros(grads[k].shape, F32)
                a = lax.dynamic_update_slice(fullw, a, (0, me * a.shape[1]))
            parts.append(a)
        return _pack_rows(parts + [jnp.zeros((1,), F32)], LANES, SMALL_ROW_MULT)

    g_small, d_small, m_small, v_small = _adamw(packed_small(two_d), small_recv, packed_small(mom_m),
                                                packed_small(mom_v), name="adamw_small")

    def unpack_all(which, small):
        out = {k: big[k][which] for k in _BIG}
        flat = small.reshape(-1)
        for k, a in zip(small_names, _unpack(flat, small_full_shapes)):
            if k in _SMALL_SHARDED:
                a = lax.dynamic_slice(a, (0, me * shape2[k][1]), shape2[k])
            out[k] = a
        return out, flat[n_small]

    g_all, loss = unpack_all(0, g_small)
    d_all, _ = unpack_all(1, d_small)
    m_all, _ = unpack_all(2, m_small)
    v_all, _ = unpack_all(3, v_small)

    def final(tree):
        return [tree[k].reshape(orig_shape[k]) for k in _WEIGHTS]

    return (loss, grad_x[None], *final(g_all), *final(d_all), *final(m_all), *final(v_all))
```

```python
import functools
import math

import numpy as np
import jax
import jax.numpy as jnp
from jax import lax
from jax.experimental import pallas as pl
from jax.experimental.pallas import tpu as pltpu

F32 = jnp.float32
BF16 = jnp.bfloat16
HIGHEST = lax.Precision.HIGHEST

D_MODEL = 1024
N_META = 16
BLOCK = 128
PAD = BLOCK - N_META
EPS = 1e-6
NEG = -1e30
SSD_INNER = 2 * D_MODEL
SSD_HEADDIM = 64
SSD_HEADS = SSD_INNER // SSD_HEADDIM
SSD_GROUPS = 4
SSD_HPG = SSD_HEADS // SSD_GROUPS
SSD_STATE = 128
SSD_CONV = 4
SSD_GW = SSD_HPG * SSD_HEADDIM
SSD_BC = SSD_GROUPS * SSD_STATE
SSD_XBC = SSD_INNER + 2 * SSD_BC
ATT_HEADS = 16
ATT_KV_HEADS = 2
ATT_HEADDIM = 64
ATT_GQ = ATT_HEADS // ATT_KV_HEADS
ATT_Q = ATT_HEADS * ATT_HEADDIM
ATT_KV = ATT_KV_HEADS * ATT_HEADDIM
REL_BUCKETS = 32
REL_MAX_DIST = 128
D_FF = 2816
FFN_CONV = 3
ADAM_LR = 0.001
ADAM_B1 = 0.9
ADAM_B2 = 0.999
ADAM_EPS = 1e-08
ADAM_WD = 0.01
ADAM_STEP = 10

N_DEV = 8
LANES = 128
SUBLANES = 8
DT_W = SSD_GROUPS * LANES
VMEM_LIMIT_BYTES = 56 * 1024 * 1024
MESH = pl.DeviceIdType.MESH

SMALL_ROW_MULT = 16

N_KEYS = 3 * BLOCK
NT_ALL = 3 * N_KEYS * BLOCK
NT_TILE = 8192


def _cparams(*sem):
    return pltpu.CompilerParams(dimension_semantics=sem, vmem_limit_bytes=VMEM_LIMIT_BYTES)


def _row_tile(n, cap):
    best = None
    for t in range(16, min(n, cap) + 1, 16):
        if n % t == 0:
            best = t
    return best or n


def _col_tile(n, cap):
    for t in (1408, 1280, 1024, 768, 640, 512, 384, 256, 128):
        if t <= cap and n % t == 0:
            return t
    return n


def _silu(x):
    return x * jax.nn.sigmoid(x)


def _dsilu(x):
    s = jax.nn.sigmoid(x)
    return s * (1.0 + x * (1.0 - s))


def _softplus(x):
    return jnp.maximum(x, 0.0) + jnp.log(1.0 + jnp.exp(-jnp.abs(x)))


def _dot_nt(a, b):
    return lax.dot_general(a, b, (((1,), (1,)), ((), ())), preferred_element_type=F32)


def _dot_tn(a, b):
    return lax.dot_general(a, b, (((0,), (0,)), ((), ())), preferred_element_type=F32)


def _dot(a, b):
    return jnp.dot(a, b, preferred_element_type=F32)


def _sum_all(x):
    return jnp.sum(jnp.sum(x, axis=1, keepdims=True), axis=0, keepdims=True)


def _mm(a, b, *, name, ta=False, tb=False, c=None, mask=False, out_dtype=F32):
    if not ta:
        m, k = a.shape
        n = b.shape[0] if tb else b.shape[1]
        tm = _row_tile(m, 832)
        tn = _col_tile(n, 512 if k > 3072 else 1024)

        def body(*refs):
            if c is None:
                a_ref, b_ref, o_ref = refs
            else:
                a_ref, b_ref, c_ref, o_ref = refs
            acc = (_dot_nt if tb else _dot)(a_ref[...].astype(BF16), b_ref[...].astype(BF16))
            if mask:
                row = pl.program_id(0) * tm + lax.broadcasted_iota(jnp.int32, (tm, 1), 0)
                acc = jnp.where(row >= PAD, acc, 0.0)
            if c is not None:
                acc = acc + c_ref[...]
            o_ref[...] = acc.astype(out_dtype)

        b_spec = pl.BlockSpec((tn, k), lambda i, j: (j, 0)) if tb else pl.BlockSpec((k, tn), lambda i, j: (0, j))
        in_specs = [pl.BlockSpec((tm, k), lambda i, j: (i, 0)), b_spec]
        args = [a, b]
        if c is not None:
            in_specs.append(pl.BlockSpec((tm, tn), lambda i, j: (i, j)))
            args.append(c)
        return pl.pallas_call(
            body, name=name, grid=(m // tm, n // tn), in_specs=in_specs,
            out_specs=pl.BlockSpec((tm, tn), lambda i, j: (i, j)),
            out_shape=jax.ShapeDtypeStruct((m, n), out_dtype),
            compiler_params=_cparams("parallel", "parallel"))(*args)

    kc, m = a.shape
    n = b.shape[1]
    tk = _row_tile(kc, 832)
    tm = _col_tile(m, 1408)
    tn = _col_tile(n, 1408)

    def body_t(a_ref, b_ref, o_ref):
        kk = pl.program_id(2)
        bb = b_ref[...]
        if mask:
            row = kk * tk + lax.broadcasted_iota(jnp.int32, (tk, 1), 0)
            bb = jnp.where(row >= PAD, bb, jnp.zeros_like(bb))
        p = _dot_tn(a_ref[...].astype(BF16), bb.astype(BF16))

        @pl.when(kk == 0)
        def _():
            o_ref[...] = p

        @pl.when(kk > 0)
        def _():
            o_ref[...] += p

    return pl.pallas_call(
        body_t, name=name, grid=(m // tm, n // tn, kc // tk),
        in_specs=[pl.BlockSpec((tk, tm), lambda i, j, kk: (kk, i)), pl.BlockSpec((tk, tn), lambda i, j, kk: (kk, j))],
        out_specs=pl.BlockSpec((tm, tn), lambda i, j, kk: (i, j)),
        out_shape=jax.ShapeDtypeStruct((m, n), F32),
        compiler_params=_cparams("parallel", "parallel", "arbitrary"))(a, b)


def _rms_fwd(h, w, *, name):
    n, d = h.shape
    tm = _row_tile(n, 832)

    def body(h_ref, w_ref, o_ref):
        x = h_ref[...]
        r = lax.rsqrt(jnp.mean(x * x, axis=-1, keepdims=True) + EPS)
        o_ref[...] = (x * r * w_ref[...]).astype(BF16)

    return pl.pallas_call(
        body, name=name, grid=(n // tm,),
        in_specs=[pl.BlockSpec((tm, d), lambda i: (i, 0)), pl.BlockSpec((1, d), lambda i: (0, 0))],
        out_specs=pl.BlockSpec((tm, d), lambda i: (i, 0)),
        out_shape=jax.ShapeDtypeStruct((n, d), BF16),
        compiler_params=_cparams("parallel"))(h, w)


def _rms_bwd(x, w, dy, dres, *, name):
    n, d = x.shape
    tm = _row_tile(n, 832)

    def body(x_ref, w_ref, dy_ref, dres_ref, dx_ref, dw_ref):
        i = pl.program_id(0)
        xv = x_ref[...]
        r = lax.rsqrt(jnp.mean(xv * xv, axis=-1, keepdims=True) + EPS)
        xh = xv * r
        dyv = dy_ref[...]
        g = dyv * w_ref[...]
        dx_ref[...] = r * (g - xh * jnp.mean(g * xh, axis=-1, keepdims=True)) + dres_ref[...]
        part = jnp.sum(dyv * xh, axis=0, keepdims=True)

        @pl.when(i == 0)
        def _():
            dw_ref[...] = part

        @pl.when(i > 0)
        def _():
            dw_ref[...] += part

    row = pl.BlockSpec((tm, d), lambda i: (i, 0))
    vec = pl.BlockSpec((1, d), lambda i: (0, 0))
    return pl.pallas_call(
        body, name=name, grid=(n // tm,), in_specs=[row, vec, row, row], out_specs=[row, vec],
        out_shape=[jax.ShapeDtypeStruct((n, d), F32), jax.ShapeDtypeStruct((1, d), F32)],
        compiler_params=_cparams("arbitrary"))(x, w, dy, dres)


def _final_loss(h, w, target):
    n, d = h.shape
    nb = n // BLOCK

    def body(h_ref, w_ref, t_ref, dh_ref, loss_ref, dw_ref):
        i = pl.program_id(0)
        xv = h_ref[...]
        r = lax.rsqrt(jnp.mean(xv * xv, axis=-1, keepdims=True) + EPS)
        xh = xv * r
        wv = w_ref[...]
        err = jnp.where(i >= 1, xh * wv - t_ref[...], 0.0)
        dyv = err * (1.0 / d)
        g = dyv * wv
        dh_ref[...] = r * (g - xh * jnp.mean(g * xh, axis=-1, keepdims=True))
        lpart = jnp.broadcast_to(0.5 * _sum_all(err * err) * (1.0 / d), (1, LANES))
        wpart = jnp.sum(dyv * xh, axis=0, keepdims=True)

        @pl.when(i == 0)
        def _():
            loss_ref[...] = lpart
            dw_ref[...] = wpart

        @pl.when(i > 0)
        def _():
            loss_ref[...] += lpart
            dw_ref[...] += wpart

    row = pl.BlockSpec((BLOCK, d), lambda i: (i, 0))
    vec = pl.BlockSpec((1, d), lambda i: (0, 0))
    return pl.pallas_call(
        body, name="final_loss", grid=(nb,),
        in_specs=[row, vec, pl.BlockSpec((BLOCK, d), lambda i: (jnp.maximum(i - 1, 0), 0))],
        out_specs=[row, pl.BlockSpec((1, LANES), lambda i: (0, 0)), vec],
        out_shape=[jax.ShapeDtypeStruct((n, d), F32), jax.ShapeDtypeStruct((1, LANES), F32),
                   jax.ShapeDtypeStruct((1, d), F32)],
        compiler_params=_cparams("arbitrary"))(h, w, target)


def _main_spec(tm, cb, off=0):
    return pl.BlockSpec((tm, cb), lambda j, i: (i, j + off))


def _prev_spec(tm, cb, off=0):
    r8 = tm // SUBLANES
    return pl.BlockSpec((SUBLANES, cb), lambda j, i: (jnp.maximum(i * r8 - 1, 0), j + off))


def _next_spec(tm, cb, n_rows, off=0):
    r8 = tm // SUBLANES
    last = n_rows // SUBLANES - 1
    return pl.BlockSpec((SUBLANES, cb), lambda j, i: (jnp.minimum((i + 1) * r8, last), j + off))


def _with_prev(prev_ref, main_ref, i):
    prev = jnp.where(i > 0, prev_ref[...], 0.0)
    return jnp.concatenate([prev, main_ref[...]], axis=0)


def _with_next(main, nxt, i, n_tiles):
    return jnp.concatenate([main, jnp.where(i < n_tiles - 1, nxt, 0.0)], axis=0)


def _back(xx, s, tm):
    if s == 0:
        return xx[SUBLANES:SUBLANES + tm]
    return pltpu.roll(xx, s, 0)[SUBLANES:SUBLANES + tm]


def _ahead(xx, s, tm):
    if s == 0:
        return xx[:tm]
    return pltpu.roll(xx, tm + SUBLANES - s, 0)[:tm]


def _conv_fwd(x, w, b, *, name):
    n, cdim = x.shape
    kw = w.shape[0]
    tm = _row_tile(n, 832)
    cb = _col_tile(cdim, 512)

    def body(xp_ref, x_ref, w_ref, b_ref, o_ref):
        xx = _with_prev(xp_ref, x_ref, pl.program_id(1))
        acc = jnp.broadcast_to(b_ref[...], (tm, cb))
        for k in range(kw):
            acc = acc + w_ref[k:k + 1, :] * _back(xx, kw - 1 - k, tm)
        o_ref[...] = acc

    return pl.pallas_call(
        body, name=name, grid=(cdim // cb, n // tm),
        in_specs=[_prev_spec(tm, cb), _main_spec(tm, cb), pl.BlockSpec((kw, cb), lambda j, i: (0, j)),
                  pl.BlockSpec((1, cb), lambda j, i: (0, j))],
        out_specs=_main_spec(tm, cb),
        out_shape=jax.ShapeDtypeStruct((n, cdim), F32),
        compiler_params=_cparams("parallel", "parallel"))(x, x, w, b)


def _conv_bwd_core(dpre_ext, x_ext, w_ref, kw, tm):
    dpre = dpre_ext[:tm]
    dx = None
    dws = []
    for k in range(kw):
        term = w_ref[k:k + 1, :] * _ahead(dpre_ext, kw - 1 - k, tm)
        dx = term if dx is None else dx + term
        dws.append(jnp.sum(dpre * _back(x_ext, kw - 1 - k, tm), axis=0, keepdims=True))
    return dx, dws, jnp.sum(dpre, axis=0, keepdims=True)


def _acc_rows(i, dw_ref, db_ref, dws, db):
    @pl.when(i == 0)
    def _():
        for k, v in enumerate(dws):
            dw_ref[k:k + 1, :] = v
        db_ref[...] = db

    @pl.when(i > 0)
    def _():
        for k, v in enumerate(dws):
            dw_ref[k:k + 1, :] += v
        db_ref[...] += db


def _conv_bwd(dpre, x, w, *, name):
    n, cdim = x.shape
    kw = w.shape[0]
    tm = _row_tile(n, 832)
    cb = _col_tile(cdim, 512)
    nt = n // tm

    def body(d_ref, dn_ref, xp_ref, x_ref, w_ref, dx_ref, dw_ref, db_ref):
        i = pl.program_id(1)
        dpre_ext = _with_next(d_ref[...], dn_ref[...], i, nt)
        x_ext = _with_prev(xp_ref, x_ref, i)
        dx, dws, db = _conv_bwd_core(dpre_ext, x_ext, w_ref, kw, tm)
        dx_ref[...] = dx.astype(BF16)
        _acc_rows(i, dw_ref, db_ref, dws, db)

    wspec = pl.BlockSpec((kw, cb), lambda j, i: (0, j))
    bspec = pl.BlockSpec((1, cb), lambda j, i: (0, j))
    return pl.pallas_call(
        body, name=name, grid=(cdim // cb, nt),
        in_specs=[_main_spec(tm, cb), _next_spec(tm, cb, n), _prev_spec(tm, cb), _main_spec(tm, cb), wspec],
        out_specs=[_main_spec(tm, cb), wspec, bspec],
        out_shape=[jax.ShapeDtypeStruct((n, cdim), BF16), jax.ShapeDtypeStruct((kw, cdim), F32),
                   jax.ShapeDtypeStruct((1, cdim), F32)],
        compiler_params=_cparams("parallel", "arbitrary"))(dpre, dpre, x, x, w)


def _ffn_act_fwd(x, w, b):
    n = x.shape[0]
    kw = w.shape[0]
    tm = _row_tile(n, 832)
    cb = _col_tile(D_FF, 256)
    nc = D_FF // cb

    def body(xpu_ref, xu_ref, xpg_ref, xg_ref, wu_ref, wg_ref, bu_ref, bg_ref, hu_ref, hg_ref, act_ref):
        i = pl.program_id(1)
        outs = []
        for xp_ref, x_ref, w_ref, b_ref in ((xpu_ref, xu_ref, wu_ref, bu_ref), (xpg_ref, xg_ref, wg_ref, bg_ref)):
            xx = _with_prev(xp_ref, x_ref, i)
            acc = jnp.broadcast_to(b_ref[...], (tm, cb))
            for k in range(kw):
                acc = acc + w_ref[k:k + 1, :] * _back(xx, kw - 1 - k, tm)
            outs.append(acc)
        hu_ref[...] = outs[0]
        hg_ref[...] = outs[1]
        act_ref[...] = (_silu(outs[1]) * outs[0]).astype(BF16)

    def wspec(off):
        return pl.BlockSpec((kw, cb), lambda j, i: (0, j + off))

    def bspec(off):
        return pl.BlockSpec((1, cb), lambda j, i: (0, j + off))

    out = _main_spec(tm, cb)
    return pl.pallas_call(
        body, name="ffn_act_fwd", grid=(nc, n // tm),
        in_specs=[_prev_spec(tm, cb), _main_spec(tm, cb), _prev_spec(tm, cb, nc), _main_spec(tm, cb, nc),
                  wspec(0), wspec(nc), bspec(0), bspec(nc)],
        out_specs=[out, out, out],
        out_shape=[jax.ShapeDtypeStruct((n, D_FF), F32), jax.ShapeDtypeStruct((n, D_FF), F32),
                   jax.ShapeDtypeStruct((n, D_FF), BF16)],
        compiler_params=_cparams("parallel", "parallel"))(x, x, x, x, w, w, b, b)


def _ffn_act_bwd(dact, hu, hg, x, w):
    n = x.shape[0]
    kw = w.shape[0]
    tm = _row_tile(n, 832)
    cb = _col_tile(D_FF, 256)
    nc = D_FF // cb
    nt = n // tm

    def body(d_ref, dn_ref, hu_ref, hun_ref, hg_ref, hgn_ref, xpu_ref, xu_ref, xpg_ref, xg_ref, wu_ref, wg_ref,
             dxu_ref, dxg_ref, dwu_ref, dwg_ref, dbu_ref, dbg_ref):
        i = pl.program_id(1)
        dact_e = _with_next(d_ref[...], dn_ref[...], i, nt)
        up_e = _with_next(hu_ref[...], hun_ref[...], i, nt)
        gate_e = _with_next(hg_ref[...], hgn_ref[...], i, nt)
        dup_e = dact_e * _silu(gate_e)
        dgate_e = dact_e * up_e * _dsilu(gate_e)
        dx, dws, db = _conv_bwd_core(dup_e, _with_prev(xpu_ref, xu_ref, i), wu_ref, kw, tm)
        dxu_ref[...] = dx.astype(BF16)
        _acc_rows(i, dwu_ref, dbu_ref, dws, db)
        dx, dws, db = _conv_bwd_core(dgate_e, _with_prev(xpg_ref, xg_ref, i), wg_ref, kw, tm)
        dxg_ref[...] = dx.astype(BF16)
        _acc_rows(i, dwg_ref, dbg_ref, dws, db)

    main, nxt = _main_spec(tm, cb), _next_spec(tm, cb, n)
    wspec0 = pl.BlockSpec((kw, cb), lambda j, i: (0, j))
    wspec1 = pl.BlockSpec((kw, cb), lambda j, i: (0, j + nc))
    bspec = pl.BlockSpec((1, cb), lambda j, i: (0, j))
    return pl.pallas_call(
        body, name="ffn_act_bwd", grid=(nc, nt),
        in_specs=[main, nxt, main, nxt, main, nxt,
                  _prev_spec(tm, cb), _main_spec(tm, cb), _prev_spec(tm, cb, nc), _main_spec(tm, cb, nc),
                  wspec0, wspec1],
        out_specs=[main, main, wspec0, wspec0, bspec, bspec],
        out_shape=[jax.ShapeDtypeStruct((n, D_FF), BF16), jax.ShapeDtypeStruct((n, D_FF), BF16),
                   jax.ShapeDtypeStruct((kw, D_FF), F32), jax.ShapeDtypeStruct((kw, D_FF), F32),
                   jax.ShapeDtypeStruct((1, D_FF), F32), jax.ShapeDtypeStruct((1, D_FF), F32)],
        compiler_params=_cparams("parallel", "arbitrary"))(dact, dact, hu, hu, hg, hg, x, x, x, x, w, w)


def _ssd_prep(pxs_ref, pb_ref, pc_ref, dtr_ref, dtb_ref, alog_ref, c):
    xs = _silu(pxs_ref[...])
    bm = _silu(pb_ref[...])
    cm = _silu(pc_ref[...])
    row = lax.broadcasted_iota(jnp.int32, (BLOCK, 1), 0) + c * BLOCK
    valid = (row >= PAD).astype(F32)
    dtr = dtr_ref[...] + dtb_ref[...]
    dt = _softplus(dtr) * valid
    a = -jnp.exp(alog_ref[...])
    lam = dt * a
    ri = lax.broadcasted_iota(jnp.int32, (BLOCK, BLOCK), 0)
    ci = lax.broadcasted_iota(jnp.int32, (BLOCK, BLOCK), 1)
    causal = ci <= ri
    cs = jnp.dot(causal.astype(F32), lam, precision=HIGHEST, preferred_element_type=F32)
    return xs, bm, cm, valid, dtr, dt, a, lam, cs, causal


def _head_cols(r):
    return slice(SSD_HEADDIM * r, SSD_HEADDIM * (r + 1))


def _ssd_specs(nc, rev):
    def cidx(c):
        return nc - 1 - c if rev else c

    xs = pl.BlockSpec((BLOCK, SSD_GW), lambda g, c: (cidx(c), g))
    bspec = pl.BlockSpec((BLOCK, SSD_STATE), lambda g, c: (cidx(c), SSD_INNER // SSD_STATE + g))
    cspec = pl.BlockSpec((BLOCK, SSD_STATE), lambda g, c: (cidx(c), (SSD_INNER + SSD_BC) // SSD_STATE + g))
    lane = pl.BlockSpec((BLOCK, LANES), lambda g, c: (cidx(c), g))
    vec = pl.BlockSpec((1, LANES), lambda g, c: (0, g))
    wide_vec = pl.BlockSpec((1, SSD_GW), lambda g, c: (0, g))
    hsave = pl.BlockSpec((1, 1, SSD_GW, SSD_STATE), lambda g, c: (cidx(c), g, 0, 0))
    return xs, bspec, cspec, lane, vec, wide_vec, hsave


def _head_spread_matrix():
    r = lax.broadcasted_iota(jnp.int32, (LANES, SSD_GW), 0)
    col = lax.broadcasted_iota(jnp.int32, (LANES, SSD_GW), 1)
    return (col // SSD_HEADDIM == r).astype(F32)


def _const_spec(shape):
    return pl.BlockSpec(shape, lambda g, c: (0,) * len(shape))


def _spread_heads(per_head, e_ref):
    wide = jnp.dot(jnp.concatenate(per_head, axis=0), e_ref[...], precision=HIGHEST, preferred_element_type=F32)
    return [wide[BLOCK * k:BLOCK * (k + 1)] for k in range(len(per_head))]


def _ssd_fwd(pre, dt_raw, z, dtb, alog, dskip_w, norm_w):
    n = pre.shape[0]
    nc = n // BLOCK
    xs_s, b_s, c_s, lane_s, vec_s, wide_s, hs_s = _ssd_specs(nc, False)

    def body(pxs_ref, pb_ref, pc_ref, dtr_ref, z_ref, dtb_ref, alog_ref, dskw_ref, nw_ref, e_ref,
             y_ref, yn_ref, hs_ref, h_scr):
        c = pl.program_id(1)

        @pl.when(c == 0)
        def _():
            h_scr[...] = jnp.zeros_like(h_scr)

        xs, bm, cm, _, _, dt, _, _, cs, causal = _ssd_prep(pxs_ref, pb_ref, pc_ref, dtr_ref, dtb_ref, alog_ref, c)
        cst = cs.T
        cs_last = cs[BLOCK - 1:BLOCK, :]
        dt_w, ecs_w, dec_w = _spread_heads([dt, jnp.exp(cs), jnp.exp(cs_last - cs)], e_ref)
        xdt = xs * dt_w
        bmb = bm.astype(BF16)
        cmb = cm.astype(BF16)
        cb = _dot_nt(cmb, bmb)
        hg = h_scr[...]
        hs_ref[0, 0] = hg
        y = _dot_nt(cmb, hg.astype(BF16)) * ecs_w + dskw_ref[...] * xs
        first = lax.broadcasted_iota(jnp.int32, (BLOCK, LANES), 1) < SSD_HEADDIM
        diag = []
        for j in range(SSD_HPG // 2):
            xp = xdt[:, LANES * j:LANES * (j + 1)].astype(BF16)
            res = []
            for r in (2 * j, 2 * j + 1):
                lm = jnp.exp(jnp.where(causal, cs[:, r:r + 1] - cst[r:r + 1, :], NEG))
                res.append(_dot((cb * lm).astype(BF16), xp))
            diag.append(jnp.where(first, res[0], res[1]))
        y = y + jnp.concatenate(diag, axis=1)
        st = _dot_tn((xdt * dec_w).astype(BF16), bmb)
        eh = jnp.exp(cs_last)
        for r in range(SSD_HPG):
            rows = _head_cols(r)
            h_scr[rows, :] = hg[rows, :] * eh[:, r:r + 1] + st[rows, :]
        y_ref[...] = y
        gts = y * _silu(z_ref[...])
        rr = lax.rsqrt(jnp.mean(gts * gts, axis=-1, keepdims=True) + EPS)
        yn_ref[...] = (gts * rr * nw_ref[...]).astype(BF16)

    return pl.pallas_call(
        body, name="ssd_fwd", grid=(SSD_GROUPS, nc),
        in_specs=[xs_s, b_s, c_s, lane_s, xs_s, vec_s, vec_s, wide_s, wide_s, _const_spec((LANES, SSD_GW))],
        out_specs=[xs_s, xs_s, hs_s],
        out_shape=[jax.ShapeDtypeStruct((n, SSD_INNER), F32), jax.ShapeDtypeStruct((n, SSD_INNER), BF16),
                   jax.ShapeDtypeStruct((nc, SSD_GROUPS, SSD_GW, SSD_STATE), F32)],
        scratch_shapes=[pltpu.VMEM((SSD_GW, SSD_STATE), F32)],
        compiler_params=_cparams("parallel", "arbitrary"))(
            pre, pre, pre, dt_raw, z, dtb, alog, dskip_w, norm_w, _head_spread_matrix())


def _lane_put(acc, col, r):
    lane = lax.broadcasted_iota(jnp.int32, acc.shape, 1)
    return jnp.where(lane == r, col, acc)


def _ssd_bwd(dyn, y, z, pre, dt_raw, hsave, dtb, alog, dskip_w, norm_w):
    n = pre.shape[0]
    nc = n // BLOCK
    spread = _head_spread_matrix()
    xs_s, b_s, c_s, lane_s, vec_s, wide_s, hs_s = _ssd_specs(nc, True)
    bc_out =pl.BlockSpec((BLOCK, SSD_STATE), lambda g, c: (nc - 1 - c, g))

    def body(dyn_ref, y_ref, z_ref, pxs_ref, pb_ref, pc_ref, dtr_ref, hs_ref, dtb_ref, alog_ref, dskw_ref, nw_ref,
             e_ref, r_ref,
             dz_ref, dxs_ref, dbm_ref, dcm_ref, ddt_ref, dnw_ref, ddtb_ref, dalog_ref, ddsk_ref, g_scr):
        step = pl.program_id(1)
        c = nc - 1 - step

        @pl.when(step == 0)
        def _():
            g_scr[...] = jnp.zeros_like(g_scr)

        xs, bm, cm, valid, dtr, dt, a, lam, cs, causal = _ssd_prep(
            pxs_ref, pb_ref, pc_ref, dtr_ref, dtb_ref, alog_ref, c)
        cst = cs.T
        cs_last = cs[BLOCK - 1:BLOCK, :]
        bmb = bm.astype(BF16)
        cmb = cm.astype(BF16)
        cb = _dot_nt(cmb, bmb)
        hg = hs_ref[0, 0]
        hgb = hg.astype(BF16)
        yoff = _dot_nt(cmb, hgb)
        gn = g_scr[...]
        gnb = gn.astype(BF16)

        zv = z_ref[...]
        yv = y_ref[...]
        sz = _silu(zv)
        gts = yv * sz
        rr = lax.rsqrt(jnp.mean(gts * gts, axis=-1, keepdims=True) + EPS)
        xh = gts * rr
        dynv = dyn_ref[...]
        gg = dynv * nw_ref[...]
        dgts = rr * (gg - xh * jnp.mean(gg * xh, axis=-1, keepdims=True))
        dnw = jnp.sum(dynv * xh, axis=0, keepdims=True)
        dy = dgts * sz
        dz_ref[...] = (dgts * yv * _dsilu(zv)).astype(BF16)

        ecs = jnp.exp(cs)
        dec = jnp.exp(cs_last - cs)
        eh = jnp.exp(cs_last)
        dt_w, ecs_w, dec_w = _spread_heads([dt, ecs, dec], e_ref)
        red_m = r_ref[...]

        def head_sums(v):
            return jnp.dot(v, red_m, precision=lax.Precision.HIGH, preferred_element_type=F32)

        xdt = xs * dt_w
        q_all = _dot_nt(bmb, gnb)
        w_all = (dy * ecs_w).astype(BF16)
        e_hl = head_sums(q_all * xdt) * dec
        dcs_col = head_sums(dy * yoff) * ecs - e_hl
        gh = jnp.zeros((1, LANES), F32)
        prod = gn * hg
        for r in range(SSD_HPG):
            gh = _lane_put(gh, _sum_all(prod[_head_cols(r), :]), r)
        dcs_last = jnp.sum(e_hl, axis=0, keepdims=True) + eh * gh
        ddsk = jnp.sum(head_sums(dy * xs), axis=0, keepdims=True)
        cbt = _dot_nt(bmb, cmb)
        lane = lax.broadcasted_iota(jnp.int32, (BLOCK, LANES), 1)
        first = lane < SSD_HEADDIM
        causal_t = lax.broadcasted_iota(jnp.int32, (BLOCK, BLOCK), 1) >= lax.broadcasted_iota(
            jnp.int32, (BLOCK, BLOCK), 0)
        sub = lax.broadcasted_iota(jnp.int32, (SUBLANES, BLOCK), 0)
        dcs_row = jnp.zeros((SUBLANES, BLOCK), F32)
        dcb = jnp.zeros((BLOCK, BLOCK), F32)
        dxdt_pairs = []
        for j in range(SSD_HPG // 2):
            tile = slice(LANES * j, LANES * (j + 1))
            dy_p = dy[:, tile]
            dyb = dy_p.astype(BF16)
            xdtb = xdt[:, tile].astype(BF16)
            res = []
            for half, r in enumerate((2 * j, 2 * j + 1)):
                csc, csr = cs[:, r:r + 1], cst[r:r + 1, :]
                lm = jnp.exp(jnp.where(causal, csc - csr, NEG))
                lmt = jnp.exp(jnp.where(causal_t, csr - csc, NEG))
                keep = first if half == 0 else jnp.logical_not(first)
                gm = _dot_nt(jnp.where(keep, dy_p, 0.0).astype(BF16), xdtb) * lm
                dcb = dcb + gm
                mm_ = gm * cb
                dcs_col = dcs_col + jnp.where(lane == r, jnp.sum(mm_, axis=1, keepdims=True), 0.0)
                dcs_row = jnp.where(sub == r, jnp.sum(mm_, axis=0, keepdims=True), dcs_row)
                res.append(_dot((cbt * lmt).astype(BF16), dyb))
            dxdt_pairs.append(jnp.where(first, res[0], res[1]))
        dxdt = jnp.concatenate(dxdt_pairs, axis=1) + q_all * dec_w
        ddt_x = head_sums(dxdt * xs)
        dxs = dxdt * dt_w + dskw_ref[...] * dy
        dcbb = dcb.astype(BF16)
        dcm = _dot(w_all, hgb) + _dot(dcbb, bmb)
        dbm = _dot((xdt * dec_w).astype(BF16), gnb) + _dot_tn(dcbb, cmb)
        dh_off = _dot_tn(w_all, cmb)
        for r in range(SSD_HPG):
            rows = _head_cols(r)
            g_scr[rows, :] = gn[rows, :] * eh[:, r:r + 1] + dh_off[rows, :]

        pad_rows = jnp.zeros((BLOCK - SUBLANES, BLOCK), F32)
        dcs = dcs_col - jnp.concatenate([dcs_row, pad_rows], axis=0).T
        rsel = lax.broadcasted_iota(jnp.int32, (BLOCK, LANES), 0)
        dcs = dcs + jnp.where(rsel == BLOCK - 1, dcs_last, 0.0)
        ri = lax.broadcasted_iota(jnp.int32, (BLOCK, BLOCK), 0)
        ci = lax.broadcasted_iota(jnp.int32, (BLOCK, BLOCK), 1)
        dlam = jnp.dot((ci >= ri).astype(F32), dcs, precision=HIGHEST, preferred_element_type=F32)
        head = lane < SSD_HPG
        ddt = dlam * a + ddt_x
        ddtr = jnp.where(head, ddt * jax.nn.sigmoid(dtr) * valid, 0.0)
        ddt_ref[...] = ddtr.astype(BF16)
        dalog = jnp.sum(jnp.where(head, dlam * lam, 0.0), axis=0, keepdims=True)
        ddtb = jnp.sum(ddtr, axis=0, keepdims=True)

        dxs_ref[...] = dxs * _dsilu(pxs_ref[...])
        dbm_ref[...] = dbm * _dsilu(pb_ref[...])
        dcm_ref[...] = dcm * _dsilu(pc_ref[...])

        @pl.when(step == 0)
        def _():
            dnw_ref[...] = dnw
            ddtb_ref[...] = ddtb
            dalog_ref[...] = dalog
            ddsk_ref[...] = ddsk

        @pl.when(step > 0)
        def _():
            dnw_ref[...] += dnw
            ddtb_ref[...] += ddtb
            dalog_ref[...] += dalog
            ddsk_ref[...] += ddsk

    return pl.pallas_call(
        body, name="ssd_bwd", grid=(SSD_GROUPS, nc),
        in_specs=[xs_s, xs_s, xs_s, xs_s, b_s, c_s, lane_s, hs_s, vec_s, vec_s, wide_s, wide_s,
                  _const_spec((LANES, SSD_GW)), _const_spec((SSD_GW, LANES))],
        out_specs=[xs_s, xs_s, bc_out, bc_out, lane_s, wide_s, vec_s, vec_s, vec_s],
        out_shape=[jax.ShapeDtypeStruct((n, SSD_INNER), BF16), jax.ShapeDtypeStruct((n, SSD_INNER), F32),
                   jax.ShapeDtypeStruct((n, SSD_BC), F32), jax.ShapeDtypeStruct((n, SSD_BC), F32),
                   jax.ShapeDtypeStruct((n, DT_W), BF16), jax.ShapeDtypeStruct((1, SSD_INNER), F32),
                   jax.ShapeDtypeStruct((1, DT_W), F32), jax.ShapeDtypeStruct((1, DT_W), F32),
                   jax.ShapeDtypeStruct((1, DT_W), F32)],
        scratch_shapes=[pltpu.VMEM((SSD_GW, SSD_STATE), F32)],
        compiler_params=_cparams("parallel", "arbitrary"))(
            dyn, y, z, pre, pre, pre, dt_raw, hsave, dtb, alog, dskip_w, norm_w, spread, spread.T)


def _bucket_table():
    def bucket(dist):
        d = np.maximum(dist, 0)
        half = REL_BUCKETS // 2
        big = half + (np.log(np.maximum(d, half).astype(np.float32) / np.float32(half))
                      / np.float32(math.log(REL_MAX_DIST / half)) * np.float32(REL_BUCKETS - half)).astype(np.int32)
        return np.where(d < half, d, np.minimum(big, REL_BUCKETS - 1)).astype(np.int32)

    l = np.arange(BLOCK)[None, :]
    band = bucket(l + BLOCK - np.arange(2 * BLOCK)[:, None])
    j = np.arange(BLOCK)[:, None]
    tables = [np.concatenate([bucket(v * BLOCK + l - j), band], axis=0) for v in range(3)]
    return np.concatenate([t.reshape(-1) for t in tables])


def _onehot_t():
    buckets = jnp.asarray(_bucket_table())
    return (buckets[None, :] == jnp.arange(REL_BUCKETS, dtype=jnp.int32)[:, None]).astype(F32)


def _bias_tables(rel_t, onehot_t):
    def body(r_ref, oh_ref, o_ref):
        o_ref[...] = jnp.dot(r_ref[...], oh_ref[...], precision=HIGHEST, preferred_element_type=F32)

    return pl.pallas_call(
        body, name="bias_tables", grid=(NT_ALL // NT_TILE,),
        in_specs=[pl.BlockSpec((ATT_HEADS, REL_BUCKETS), lambda i: (0, 0)),
                  pl.BlockSpec((REL_BUCKETS, NT_TILE), lambda i: (0, i))],
        out_specs=pl.BlockSpec((ATT_HEADS, NT_TILE), lambda i: (0, i)),
        out_shape=jax.ShapeDtypeStruct((ATT_HEADS, NT_ALL), F32),
        compiler_params=_cparams("parallel"))(rel_t, onehot_t)


def _bias_grad(dtab, onehot_t):
    def body(d_ref, oh_ref, o_ref):
        i = pl.program_id(0)
        p = lax.dot_general(d_ref[...], oh_ref[...], (((1,), (1,)), ((), ())), precision=HIGHEST,
                            preferred_element_type=F32)

        @pl.when(i == 0)
        def _():
            o_ref[...] = p

        @pl.when(i > 0)
        def _():
            o_ref[...] += p

    return pl.pallas_call(
        body, name="bias_grad", grid=(NT_ALL // NT_TILE,),
        in_specs=[pl.BlockSpec((ATT_HEADS, NT_TILE), lambda i: (0, i)),
                  pl.BlockSpec((REL_BUCKETS, NT_TILE), lambda i: (0, i))],
        out_specs=pl.BlockSpec((ATT_HEADS, REL_BUCKETS), lambda i: (0, 0)),
        out_shape=jax.ShapeDtypeStruct((ATT_HEADS, REL_BUCKETS), F32),
        compiler_params=_cparams("arbitrary"))(dtab, onehot_t)


def _att_mask_t(n):
    far = 4 * BLOCK
    kk = lax.broadcasted_iota(jnp.int32, (N_KEYS, BLOCK), 0)
    li = lax.broadcasted_iota(jnp.int32, (N_KEYS, BLOCK), 1)
    meta_ok = (kk >= PAD) & (kk < BLOCK) & (li + jnp.where(n >= 1, far, 0) >= kk)
    prev_ok = (kk >= BLOCK) & (kk < 2 * BLOCK) & (kk - BLOCK > li + jnp.where(n >= 2, 0, far))
    cur_ok = (kk >= 2 * BLOCK) & (kk - 2 * BLOCK <= li - jnp.where(n >= 1, 0, far))
    return meta_ok | prev_ok | cur_ok


def _att_kv(meta_ref, prev_ref, cur_ref):
    kv = jnp.concatenate([meta_ref[...], prev_ref[...], cur_ref[...]], axis=0)
    first = lax.broadcasted_iota(jnp.int32, (N_KEYS, LANES), 1) < ATT_HEADDIM
    out = []
    for pair in (kv[:, :LANES], kv[:, LANES:]):
        swapped = pltpu.roll(pair, ATT_HEADDIM, 1)
        out.append([jnp.where(first, pair, swapped).astype(BF16), jnp.where(first, swapped, pair).astype(BF16)])
    return out[0], out[1]


def _att_probs_t(qm, k_dup, bias_t, mask_t, sink):
    scale = ATT_HEADDIM ** -0.5
    s_t = jnp.where(mask_t, _dot_nt(k_dup, qm) * scale + bias_t, NEG)
    mx = jnp.maximum(jnp.max(s_t, axis=0, keepdims=True), sink)
    p_t = jnp.exp(s_t - mx)
    p_s = jnp.exp(sink - mx)
    inv = 1.0 / (jnp.sum(p_t, axis=0, keepdims=True) + p_s)
    return p_t * inv, p_s * inv


def _att_specs(nb, rev):
    def nidx(i):
        return nb - 1 - i if rev else i

    kvb = ATT_Q // (2 * ATT_KV)
    q_s = pl.BlockSpec((BLOCK, ATT_Q), lambda i: (nidx(i), 0))
    cur = pl.BlockSpec((BLOCK, 2 * ATT_KV), lambda i: (nidx(i), kvb))
    prev = pl.BlockSpec((BLOCK, 2 * ATT_KV), lambda i: (jnp.maximum(nidx(i) - 1, 0), kvb))
    meta = pl.BlockSpec((BLOCK, 2 * ATT_KV), lambda i: (0, kvb))
    table = pl.BlockSpec((1, ATT_HEADS, N_KEYS, BLOCK), lambda i: (jnp.minimum(nidx(i), 2), 0, 0, 0))
    sink = pl.BlockSpec((1, LANES), lambda i: (0, 0))
    return q_s, cur, prev, meta, table, sink


def _attn_fwd(qkv, tables, sinks):
    n = qkv.shape[0]
    nb = n // BLOCK
    q_s, cur_s, prev_s, meta_s, t_s, sink_s = _att_specs(nb, False)

    def body(q_ref, cur_ref, prev_ref, meta_ref, t_ref, sink_ref, o_ref):
        blk = pl.program_id(0)
        mask_t = _att_mask_t(blk)
        k_dup, v_dup = _att_kv(meta_ref, prev_ref, cur_ref)
        v_dup_t = [v.T for v in v_dup]
        first = lax.broadcasted_iota(jnp.int32, (BLOCK, LANES), 1) < ATT_HEADDIM
        top = lax.broadcasted_iota(jnp.int32, (LANES, BLOCK), 0) < ATT_HEADDIM
        for j in range(ATT_HEADS // 2):
            kh = 2 * j // ATT_GQ
            tile = slice(LANES * j, LANES * (j + 1))
            q_p = q_ref[:, tile]
            res = []
            for half, h in enumerate((2 * j, 2 * j + 1)):
                keep = first if half == 0 else jnp.logical_not(first)
                qm = jnp.where(keep, q_p, 0.0).astype(BF16)
                p_t, _ = _att_probs_t(qm, k_dup[kh], t_ref[0, h], mask_t, sink_ref[0:1, h:h + 1])
                res.append(_dot(v_dup_t[kh], p_t.astype(BF16)))
            o_ref[:, tile] = jnp.where(top, res[0], res[1]).T.astype(BF16)

    return pl.pallas_call(
        body, name="attn_fwd", grid=(nb,),
        in_specs=[q_s, cur_s, prev_s, meta_s, t_s, sink_s],
        out_specs=q_s,
        out_shape=jax.ShapeDtypeStruct((n, ATT_Q), BF16),
        compiler_params=_cparams("parallel"))(qkv, qkv, qkv, qkv, tables, sinks)


def _attn_bwd(datt, qkv, tables, sinks):
    n = qkv.shape[0]
    nb = n // BLOCK
    q_s, cur_s, prev_s, meta_s, t_s, sink_s = _att_specs(nb, True)
    dqkv_s = pl.BlockSpec((BLOCK, ATT_Q + 2 * ATT_KV), lambda i: (nb - 1 - i, 0))
    scale = ATT_HEADDIM ** -0.5

    def body(do_ref, q_ref, cur_ref, prev_ref, meta_ref, t_ref, sink_ref,
             dqkv_ref, dt_ref, dsink_ref, carry_scr, meta_scr):
        step = pl.program_id(0)
        blk = nb - 1 - step
        mask_t = _att_mask_t(blk)
        k_dup, v_dup = _att_kv(meta_ref, prev_ref, cur_ref)
        k_dup_t = [k.T for k in k_dup]

        @pl.when(step == 0)
        def _():
            carry_scr[...] = jnp.zeros_like(carry_scr)
            meta_scr[...] = jnp.zeros_like(meta_scr)
            dsink_ref[...] = jnp.zeros_like(dsink_ref)

        @pl.when((step == 0) | (blk <= 1))
        def _():
            dt_ref[...] = jnp.zeros_like(dt_ref)

        first = lax.broadcasted_iota(jnp.int32, (BLOCK, LANES), 1) < ATT_HEADDIM
        top = lax.broadcasted_iota(jnp.int32, (LANES, BLOCK), 0) < ATT_HEADDIM
        first_k = lax.broadcasted_iota(jnp.int32, (N_KEYS, LANES), 1) < ATT_HEADDIM
        dsink = jnp.zeros((1, LANES), F32)
        dk_acc = [None] * ATT_KV_HEADS
        dv_acc = [None] * ATT_KV_HEADS
        for j in range(ATT_HEADS // 2):
            kh = 2 * j // ATT_GQ
            tile = slice(LANES * j, LANES * (j + 1))
            q_p = q_ref[:, tile]
            do_p = do_ref[:, tile]
            res = []
            for half, h in enumerate((2 * j, 2 * j + 1)):
                keep = first if half == 0 else jnp.logical_not(first)
                qm = jnp.where(keep, q_p, 0.0).astype(BF16)
                dom = jnp.where(keep, do_p, 0.0).astype(BF16)
                p_t, p_s = _att_probs_t(qm, k_dup[kh], t_ref[0, h], mask_t, sink_ref[0:1, h:h + 1])
                dp_t = _dot_nt(v_dup[kh], dom)
                delta = jnp.sum(p_t * dp_t, axis=0, keepdims=True)
                ds_t = p_t * (dp_t - delta)
                dsink = _lane_put(dsink, -jnp.sum(p_s * delta, axis=1, keepdims=True), h)
                dt_ref[0, h] += ds_t
                ds_tb = ds_t.astype(BF16)
                res.append(_dot(k_dup_t[kh], ds_tb))
                dk_part, dv_part = _dot(ds_tb, qm), _dot(p_t.astype(BF16), dom)
                dk_acc[kh] = dk_part if dk_acc[kh] is None else dk_acc[kh] + dk_part
                dv_acc[kh] = dv_part if dv_acc[kh] is None else dv_acc[kh] + dv_part
            dqkv_ref[:, tile] = (jnp.where(top, res[0], res[1]).T * scale).astype(BF16)
        dsink_ref[...] += dsink
        folded = [a + pltpu.roll(a, ATT_HEADDIM, 1) for a in dk_acc + dv_acc]
        dkv = jnp.concatenate([jnp.where(first_k, folded[0], folded[1]) * scale,
                               jnp.where(first_k, folded[2], folded[3])], axis=1)
        meta_scr[...] += dkv[:BLOCK, :]
        own = dkv[2 * BLOCK:, :] + carry_scr[...]
        carry_scr[...] = dkv[BLOCK:2 * BLOCK, :]

        @pl.when(blk > 0)
        def _():
            dqkv_ref[:, ATT_Q:] = own.astype(BF16)

        @pl.when(blk == 0)
        def _():
            dqkv_ref[:, ATT_Q:] = (own + meta_scr[...]).astype(BF16)

    return pl.pallas_call(
        body, name="attn_bwd", grid=(nb,),
        in_specs=[q_s, q_s, cur_s, prev_s, meta_s, t_s, sink_s],
        out_specs=[dqkv_s, t_s, sink_s],
        out_shape=[jax.ShapeDtypeStruct((n, ATT_Q + 2 * ATT_KV), BF16),
                   jax.ShapeDtypeStruct((3, ATT_HEADS, N_KEYS, BLOCK), F32),
                   jax.ShapeDtypeStruct((1, LANES), F32)],
        scratch_shapes=[pltpu.VMEM((BLOCK, 2 * ATT_KV), F32), pltpu.VMEM((BLOCK, 2 * ATT_KV), F32)],
        compiler_params=_cparams("arbitrary"))(datt, qkv, qkv, qkv, qkv, tables, sinks)


def _merge_fwd(gates, y_ssd, y_att, gate_b):
    n = gates.shape[0]
    tm = _row_tile(n, 832)

    def body(gs_ref, ga_ref, ys_ref, ya_ref, gb_ref, o_ref):
        o_ref[...] = (jax.nn.sigmoid(gs_ref[...] + gb_ref[0:1, :]) * ys_ref[...]
                      + jax.nn.sigmoid(ga_ref[...] + gb_ref[1:2, :]) * ya_ref[...]).astype(BF16)

    row = pl.BlockSpec((tm, D_MODEL), lambda i: (i, 0))
    return pl.pallas_call(
        body, name="merge_fwd", grid=(n // tm,),
        in_specs=[row, pl.BlockSpec((tm, D_MODEL), lambda i: (i, 1)), row, row,
                  pl.BlockSpec((2, D_MODEL), lambda i: (0, 0))],
        out_specs=row, out_shape=jax.ShapeDtypeStruct((n, D_MODEL), BF16),
        compiler_params=_cparams("parallel"))(gates, gates, y_ssd, y_att, gate_b)


def _merge_bwd(dm, gates, y_ssd, y_att, gate_b):
    n = gates.shape[0]
    tm = _row_tile(n, 832)

    def body(dm_ref, gs_ref, ga_ref, ys_ref, ya_ref, gb_ref, dys_ref, dya_ref, dg_ref, dgb_ref):
        i = pl.program_id(0)
        dmv = dm_ref[...]
        ss = jax.nn.sigmoid(gs_ref[...] + gb_ref[0:1, :])
        sa = jax.nn.sigmoid(ga_ref[...] + gb_ref[1:2, :])
        dys_ref[...] = (dmv * ss).astype(BF16)
        dya_ref[...] = (dmv * sa).astype(BF16)
        dgs = dmv * ys_ref[...] * ss * (1.0 - ss)
        dga = dmv * ya_ref[...] * sa * (1.0 - sa)
        dg_ref[:, :D_MODEL] = dgs.astype(BF16)
        dg_ref[:, D_MODEL:] = dga.astype(BF16)
        part = jnp.concatenate([jnp.sum(dgs, axis=0, keepdims=True), jnp.sum(dga, axis=0, keepdims=True)], axis=0)

        @pl.when(i == 0)
        def _():
            dgb_ref[...] = part

        @pl.when(i > 0)
        def _():
            dgb_ref[...] += part

    row = pl.BlockSpec((tm, D_MODEL), lambda i: (i, 0))
    gb = pl.BlockSpec((2, D_MODEL), lambda i: (0, 0))
    return pl.pallas_call(
        body, name="merge_bwd", grid=(n // tm,),
        in_specs=[row, row, pl.BlockSpec((tm, D_MODEL), lambda i: (i, 1)), row, row, gb],
        out_specs=[row, row, pl.BlockSpec((tm, 2 * D_MODEL), lambda i: (i, 0)), gb],
        out_shape=[jax.ShapeDtypeStruct((n, D_MODEL), BF16), jax.ShapeDtypeStruct((n, D_MODEL), BF16),
                   jax.ShapeDtypeStruct((n, 2 * D_MODEL), BF16), jax.ShapeDtypeStruct((2, D_MODEL), F32)],
        compiler_params=_cparams("arbitrary"))(dm, gates, gates, y_ssd, y_att, gate_b)


def _col_move(srcs, outs, pieces, *, name):
    rows = srcs[0].shape[-2]
    tr = _row_tile(rows, 128)
    n_src = len(srcs)
    covered = [sum(p[6] for p in pieces if p[0] == o) for o in range(len(outs))]
    total = [int(np.prod(shp)) // rows for shp, _ in outs]

    def body(*refs):
        in_refs, out_refs = refs[:n_src], refs[n_src:]
        for o, ref in enumerate(out_refs):
            if covered[o] != total[o]:
                ref[...] = jnp.zeros_like(ref)
        for o, ol, oc, s, sl, sc, width in pieces:
            val = in_refs[s][:, sc:sc + width] if sl is None else in_refs[s][sl, :, sc:sc + width]
            val = val.astype(outs[o][1])
            if ol is None:
                out_refs[o][:, oc:oc + width] = val
            else:
                out_refs[o][ol, :, oc:oc + width] = val

    def spec(shape):
        if len(shape) == 2:
            return pl.BlockSpec((tr, shape[1]), lambda i: (i, 0))
        return pl.BlockSpec((shape[0], tr, shape[2]), lambda i: (0, i, 0))

    return pl.pallas_call(
        body, name=name, grid=(rows // tr,),
        in_specs=[spec(a.shape) for a in srcs], out_specs=[spec(shp) for shp, _ in outs],
        out_shape=[jax.ShapeDtypeStruct(shp, dt) for shp, dt in outs],
        compiler_params=_cparams("parallel"))(*srcs)


def _shard_pieces(seg_ranges, shard_w):
    out = []
    for seg, runs in enumerate(seg_ranges):
        for g0, width, s0 in runs:
            done = 0
            while done < width:
                dev, col = divmod(g0 + done, shard_w)
                take = min(width - done, shard_w - col)
                out.append((seg, s0 + done, dev, col, take))
                done += take
    return out


_CHIP_RELATIONS = [(1, 0, 0), (0, 1, 0), (1, 1, 0)]
N_CHIPS = 4


def _gather_two_level(arrays, *, name):
    n_arr = len(arrays)
    n_pair = 1 + 2 * len(_CHIP_RELATIONS)

    def body(*refs):
        ins, outs = refs[:n_arr], refs[n_arr:2 * n_arr]
        send_sems, recv_sems, local_sems = refs[2 * n_arr:]
        x, y, c = lax.axis_index("x"), lax.axis_index("y"), lax.axis_index("c")
        sibling = (x, y, 1 - c)
        chips = [(x ^ dx, y ^ dy) for dx, dy, _ in _CHIP_RELATIONS]

        def copy(a, k, block, to, src=None):
            slot = outs[a].at[2 * block[0] + block[1], block[2]]
            return pltpu.make_async_remote_copy(
                src_ref=slot if src is None else src, dst_ref=slot, send_sem=send_sems.at[a * n_pair + k],
                recv_sem=recv_sems.at[a * n_pair + k], device_id=to, device_id_type=MESH)

        sends, locals_ = [], []
        for a in range(n_arr):
            mine = pltpu.make_async_copy(ins[a], outs[a].at[2 * x + y, c], local_sems.at[a])
            mine.start()
            locals_.append(mine)
            first = [copy(a, 0, (x, y, c), sibling, src=ins[a])]
            first += [copy(a, 1 + j, (x, y, c), (*chip, c), src=ins[a]) for j, chip in enumerate(chips)]
            for cp in first:
                cp.start()
            sends += first
        for j, chip in enumerate(chips):
            for a in range(n_arr):
                copy(a, 1 + j, (*chip, c), (x, y, c)).wait_recv()
                passed = copy(a, 1 + len(chips) + j, (*chip, c), sibling)
                passed.start()
                sends.append(passed)
        for a in range(n_arr):
            copy(a, 0, (x, y, 1 - c), (x, y, c)).wait_recv()
            for j, chip in enumerate(chips):
                copy(a, 1 + len(chips) + j, (*chip, 1 - c), (x, y, c)).wait_recv()
        for cp in sends:
            cp.wait_send()
        for mine in locals_:
            mine.wait()

    any_spec = pl.BlockSpec(memory_space=pl.ANY)
    outs = pl.pallas_call(
        body, name=name, in_specs=[any_spec] * n_arr, out_specs=[any_spec] * n_arr,
        out_shape=[jax.ShapeDtypeStruct((N_CHIPS, 2) + a.shape, a.dtype) for a in arrays],
        scratch_shapes=[pltpu.SemaphoreType.DMA((n_arr * n_pair,)), pltpu.SemaphoreType.DMA((n_arr * n_pair,)),
                        pltpu.SemaphoreType.DMA((n_arr,))],
    )(*arrays)
    return [o.reshape((N_DEV,) + a.shape) for o, a in zip(outs, arrays)]


def _sibling_exchange(arrays, scatter, *, name):
    n_arr = len(arrays)

    def body(*refs):
        ins, outs = refs[:n_arr], refs[n_arr:2 * n_arr]
        send_sems, recv_sems = refs[2 * n_arr:]
        x, y, c = lax.axis_index("x"), lax.axis_index("y"), lax.axis_index("c")
        copies = []
        for a in range(n_arr):
            for q in range(N_CHIPS if scatter[a] else 1):
                src = ins[a].at[2 * q + 1 - c] if scatter[a] else ins[a]
                dst = outs[a].at[q] if scatter[a] else outs[a]
                cp = pltpu.make_async_remote_copy(
                    src_ref=src, dst_ref=dst, send_sem=send_sems.at[a * N_CHIPS + q],
                    recv_sem=recv_sems.at[a * N_CHIPS + q], device_id=(x, y, 1 - c), device_id_type=MESH)
                cp.start()
                copies.append(cp)
        for cp in copies:
            cp.wait_send()
        for cp in copies:
            cp.wait_recv()

    any_spec = pl.BlockSpec(memory_space=pl.ANY)
    return pl.pallas_call(
        body, name=name, in_specs=[any_spec] * n_arr, out_specs=[any_spec] * n_arr,
        out_shape=[jax.ShapeDtypeStruct(((N_CHIPS,) + a.shape[1:]) if s else a.shape, a.dtype)
                   for a, s in zip(arrays, scatter)],
        scratch_shapes=[pltpu.SemaphoreType.DMA((n_arr * N_CHIPS,)), pltpu.SemaphoreType.DMA((n_arr * N_CHIPS,))],
    )(*arrays)


def _pair_sum(mine, sib, *, name, out_dtype):
    _, rows, cols = mine.shape
    tr = _row_tile(rows, 128)

    def body(m_ref, s_ref, o_ref):
        c = lax.axis_index("c")
        o_ref[0] = (m_ref[0, c] + s_ref[0]).astype(out_dtype)

    return pl.pallas_call(
        body, name=name, grid=(N_CHIPS, rows // tr),
        in_specs=[pl.BlockSpec((1, 2, tr, cols), lambda q, i: (q, 0, i, 0)),
                  pl.BlockSpec((1, tr, cols), lambda q, i: (q, i, 0))],
        out_specs=pl.BlockSpec((1, tr, cols), lambda q, i: (q, i, 0)),
        out_shape=jax.ShapeDtypeStruct((N_CHIPS, rows, cols), out_dtype),
        compiler_params=_cparams("parallel", "parallel"))(mine.reshape(N_CHIPS, 2, rows, cols), sib)


def _add(a, b, *, name):
    rows, cols = a.shape
    tr = _row_tile(rows, 256)

    def body(a_ref, b_ref, o_ref):
        o_ref[...] = a_ref[...] + b_ref[...]

    blk = pl.BlockSpec((tr, cols), lambda i: (i, 0))
    return pl.pallas_call(body, name=name, grid=(rows // tr,), in_specs=[blk, blk], out_specs=blk,
                          out_shape=jax.ShapeDtypeStruct(a.shape, a.dtype), compiler_params=_cparams("parallel"))(a, b)


def _chip_exchange(arrays, scatter, *, name):
    n_arr = len(arrays)
    n_rel = len(_CHIP_RELATIONS)

    def body(*refs):
        ins, outs = refs[:n_arr], refs[n_arr:2 * n_arr]
        send_sems, recv_sems, local_sems = refs[2 * n_arr:]
        x, y, c = lax.axis_index("x"), lax.axis_index("y"), lax.axis_index("c")
        me = 2 * x + y
        copies = []
        for a in range(n_arr):
            src = ins[a].at[me] if scatter[a] else ins[a]
            local = pltpu.make_async_copy(src, outs[a].at[me], local_sems.at[a])
            local.start()
            copies.append(local)
        remote = []
        for k, (dx, dy, dc) in enumerate(_CHIP_RELATIONS):
            px, py, pc = x ^ dx, y ^ dy, c ^ dc
            peer = 2 * px + py
            for a in range(n_arr):
                src = ins[a].at[peer] if scatter[a] else ins[a]
                cp = pltpu.make_async_remote_copy(
                    src_ref=src, dst_ref=outs[a].at[me], send_sem=send_sems.at[a * n_rel + k],
                    recv_sem=recv_sems.at[a * n_rel + k], device_id=(px, py, pc), device_id_type=MESH)
                cp.start()
                remote.append((cp, a, k, peer))
        for cp, a, k, peer in remote:
            cp.wait_send()
        for cp, a, k, peer in remote:
            src = ins[a].at[peer] if scatter[a] else ins[a]
            pltpu.make_async_remote_copy(
                src_ref=src, dst_ref=outs[a].at[peer], send_sem=send_sems.at[a * n_rel + k],
                recv_sem=recv_sems.at[a * n_rel + k], device_id=(x, y, c), device_id_type=MESH).wait_recv()
        for local in copies:
            local.wait()

    out_shape = [jax.ShapeDtypeStruct((N_CHIPS,) + (a.shape[1:] if s else a.shape), a.dtype)
                 for a, s in zip(arrays, scatter)]
    any_spec = pl.BlockSpec(memory_space=pl.ANY)
    return pl.pallas_call(
        body, name=name, in_specs=[any_spec] * n_arr, out_specs=[any_spec] * n_arr, out_shape=out_shape,
        scratch_shapes=[pltpu.SemaphoreType.DMA((n_arr * n_rel,)), pltpu.SemaphoreType.DMA((n_arr * n_rel,)),
                        pltpu.SemaphoreType.DMA((n_arr,))],
    )(*arrays)


def _adamw(w, gslots, m, v, *, name):
    rows, cols = w.shape
    n_slots = gslots.shape[0]
    tr = _row_tile(rows, 128) if rows % 16 == 0 else rows

    def body(w_ref, g_ref, m_ref, v_ref, go_ref, d_ref, mo_ref, vo_ref):
        g = g_ref[0].astype(F32)
        for s in range(1, n_slots):
            g = g + g_ref[s].astype(F32)
        mn = ADAM_B1 * m_ref[...] + (1.0 - ADAM_B1) * g
        vn = ADAM_B2 * v_ref[...] + (1.0 - ADAM_B2) * (g * g)
        go_ref[...] = g
        mo_ref[...] = mn
        vo_ref[...] = vn
        m_hat = mn / (1.0 - ADAM_B1 ** ADAM_STEP)
        v_hat = vn / (1.0 - ADAM_B2 ** ADAM_STEP)
        d_ref[...] = -ADAM_LR * (m_hat / (jnp.sqrt(v_hat) + ADAM_EPS) + ADAM_WD * w_ref[...])

    blk = pl.BlockSpec((tr, cols), lambda i: (i, 0))
    shp = jax.ShapeDtypeStruct((rows, cols), F32)
    return pl.pallas_call(
        body, name=name, grid=(rows // tr,),
        in_specs=[blk, pl.BlockSpec((n_slots, tr, cols), lambda i: (0, i, 0)), blk, blk],
        out_specs=[blk] * 4, out_shape=[shp] * 4,
        compiler_params=_cparams("parallel"))(w, gslots, m, v)


_BIG = ("w_in", "w_ssd_branch", "w_attn_branch", "w_out", "w_ffn_in", "w_ffn_out")
_SMALL_SHARDED = ("meta_tokens", "ssd_conv_w", "gate_b", "ffn_conv_w")
_SMALL_REPLICATED = ("norm_mix_w", "ssd_conv_b", "ssd_dt_bias", "ssd_a_log", "ssd_d", "ssd_norm_w", "attn_sinks",
                     "rel_bias", "norm_ffn_w", "ffn_conv_b", "norm_final_w")
_WEIGHTS = ("meta_tokens", "norm_mix_w", "w_in", "ssd_conv_w", "ssd_conv_b", "ssd_dt_bias", "ssd_a_log", "ssd_d",
            "ssd_norm_w", "w_ssd_branch", "w_attn_branch", "attn_sinks", "rel_bias", "gate_b", "w_out", "norm_ffn_w",
            "w_ffn_in", "ffn_conv_w", "ffn_conv_b", "w_ffn_out", "norm_final_w")
_ROW_SHARDED = ("w_ssd_branch", "w_attn_branch", "w_out", "w_ffn_out")
_COL_SHARDED = ("w_in", "w_ffn_in", "meta_tokens", "ssd_conv_w", "gate_b", "ffn_conv_w")
_IN_SEGS = (("z", SSD_INNER), ("xbc", SSD_XBC), ("dt", SSD_HEADS), ("qkv", ATT_Q + 2 * ATT_KV), ("g", 2 * D_MODEL))


def _pack_rows(flat_parts, width, row_mult):
    flat = jnp.concatenate([p.reshape(-1) for p in flat_parts])
    pad = (-flat.shape[0]) % (width * row_mult)
    if pad:
        flat = jnp.concatenate([flat, jnp.zeros((pad,), flat.dtype)])
    return flat.reshape(-1, width)


def _unpack(flat, shapes):
    out, off = [], 0
    for shp in shapes:
        size = int(np.prod(shp))
        out.append(flat[off:off + size].reshape(shp))
        off += size
    return out


def _gather_full(stack, name, shard_shape):
    if name in _COL_SHARDED:
        return jnp.transpose(stack, (1, 0, 2)).reshape(shard_shape[0], N_DEV * shard_shape[1])
    return stack.reshape(N_DEV * shard_shape[0], shard_shape[1])


_IN_SEG_W = {"z": SSD_INNER, "xbc": SSD_XBC, "dt": DT_W, "qkv": ATT_Q + 2 * ATT_KV, "g": 2 * D_MODEL}
_IN_SHARD_W = (SSD_INNER + SSD_XBC + SSD_HEADS + ATT_Q + 2 * ATT_KV + 2 * D_MODEL) // N_DEV
_FFN_SHARD_W = 2 * D_FF // N_DEV


def _in_seg_runs():
    runs, off = [], 0
    for nm, width in _IN_SEGS:
        if nm == "dt":
            runs.append([(off + SSD_HPG * g, SSD_HPG, LANES * g) for g in range(SSD_GROUPS)])
        else:
            runs.append([(off, width, 0)])
        off += width
    return runs


def _w_in_to_segments(stack):
    pieces = [(seg, None, scol, 0, dev, col, w) for seg, scol, dev, col, w in _shard_pieces(_in_seg_runs(), _IN_SHARD_W)]
    outs = [((D_MODEL, _IN_SEG_W[nm]), stack.dtype) for nm, _ in _IN_SEGS]
    return dict(zip([nm for nm, _ in _IN_SEGS], _col_move([stack], outs, pieces, name="w_in_segments")))


def _segments_to_w_in_shards(seg_grads):
    pieces = [(0, dev, col, seg, None, scol, w) for seg, scol, dev, col, w in _shard_pieces(_in_seg_runs(), _IN_SHARD_W)]
    return _col_move(seg_grads, [((N_DEV, D_MODEL, _IN_SHARD_W), F32)], pieces, name="g_w_in_shards")[0]


def _ffn_in_from_shards(stack):
    pieces = [(0, None, scol, 0, dev, col, w)
              for _, scol, dev, col, w in _shard_pieces([[(0, 2 * D_FF, 0)]], _FFN_SHARD_W)]
    return _col_move([stack], [((D_MODEL, 2 * D_FF), stack.dtype)], pieces, name="w_ffn_in_full")[0]


def _ffn_in_to_shards(g_up, g_gate):
    pieces = [(0, dev, col, seg, None, scol, w)
              for seg, scol, dev, col, w in _shard_pieces([[(0, D_FF, 0)], [(D_FF, D_FF, 0)]], _FFN_SHARD_W)]
    return _col_move([g_up, g_gate], [((N_DEV, D_MODEL, _FFN_SHARD_W), F32)], pieces, name="g_w_ffn_in_shards")[0]


def _dt_spread(w_dt):
    k = w_dt.shape[0]
    w4 = w_dt.reshape(k, SSD_GROUPS, SSD_HPG)
    return jnp.pad(w4, ((0, 0), (0, 0), (0, LANES - SSD_HPG))).reshape(k, DT_W)


def _dt_gather(w_wide):
    k = w_wide.shape[0]
    return w_wide.reshape(k, SSD_GROUPS, LANES)[:, :, :SSD_HPG].reshape(k, SSD_HEADS)


def _local_step(x, target, w):
    h0 = jnp.concatenate([jnp.zeros((PAD, D_MODEL), F32), w["meta_tokens"], x], axis=0)
    segs = w["in_segs"]
    w_ffn_up, w_ffn_gate = w["w_ffn_in"][:, :D_FF], w["w_ffn_in"][:, D_FF:]

    dtb = _dt_spread(w["ssd_dt_bias"])
    alog = _dt_spread(w["ssd_a_log"])
    dskip_w = jnp.repeat(w["ssd_d"], SSD_HEADDIM, axis=1)
    sinks = jnp.pad(w["attn_sinks"], ((0, 0), (0, LANES - ATT_HEADS)))
    onehot_t = _onehot_t()
    tables = jnp.transpose(_bias_tables(w["rel_bias"].T, onehot_t).reshape(ATT_HEADS, 3, N_KEYS, BLOCK), (1, 0, 2, 3))

    u = _rms_fwd(h0, w["norm_mix_w"], name="rms_mix_fwd")
    z = _mm(u, segs["z"], name="in_z")
    xbc = _mm(u, segs["xbc"], name="in_xbc")
    dt_raw = _mm(u, segs["dt"], name="in_dt")
    qkv = _mm(u, segs["qkv"], name="in_qkv")
    gates = _mm(u, segs["g"], name="in_g")
    pre = _conv_fwd(xbc, w["ssd_conv_w"], w["ssd_conv_b"], name="ssd_conv_fwd")
    y, yn, hsave = _ssd_fwd(pre, dt_raw, z, dtb, alog, dskip_w, w["ssd_norm_w"])
    y_ssd = _mm(yn, w["w_ssd_branch"], name="ssd_out")
    att = _attn_fwd(qkv, tables, sinks)
    y_att = _mm(att, w["w_attn_branch"], name="att_out")
    merged = _merge_fwd(gates, y_ssd, y_att, w["gate_b"])
    h1 = _mm(merged, w["w_out"], c=h0, mask=True, name="mix_out")
    u2 = _rms_fwd(h1, w["norm_ffn_w"], name="rms_ffn_fwd")
    hid_raw = _mm(u2, w["w_ffn_in"], name="ffn_in")
    hid_up, hid_gate, act = _ffn_act_fwd(hid_raw, w["ffn_conv_w"], w["ffn_conv_b"])
    h2 = _mm(act, w["w_ffn_out"], c=h1, mask=True, name="ffn_out")
    dh2, loss_row, g_norm_final = _final_loss(h2, w["norm_final_w"], target)

    grads = {"norm_final_w": g_norm_final}
    dact = _mm(dh2, w["w_ffn_out"], tb=True, mask=True, name="d_act")
    grads["w_ffn_out"] = _mm(act, dh2, ta=True, mask=True, name="g_w_ffn_out")
    dx_up, dx_gate, dcw_up, dcw_gate, dcb_up, dcb_gate = _ffn_act_bwd(dact, hid_up, hid_gate, hid_raw, w["ffn_conv_w"])
    grads["ffn_conv_w"] = jnp.concatenate([dcw_up, dcw_gate], axis=1)
    grads["ffn_conv_b"] = jnp.concatenate([dcb_up, dcb_gate], axis=1)
    du2 = _mm(dx_up, w_ffn_up, tb=True, name="d_u2_up")
    du2 = _mm(dx_gate, w_ffn_gate, tb=True, c=du2, name="d_u2_gate")
    grads["w_ffn_in"] = (_mm(u2, dx_up, ta=True, name="g_w_ffn_up"), _mm(u2, dx_gate, ta=True, name="g_w_ffn_gate"))
    dh1, grads["norm_ffn_w"] = _rms_bwd(h1, w["norm_ffn_w"], du2, dh2, name="rms_ffn_bwd")

    dmerged = _mm(dh1, w["w_out"], tb=True, mask=True, name="d_merged")
    grads["w_out"] = _mm(merged, dh1, ta=True, mask=True, name="g_w_out")
    dy_ssd, dy_att, dgates, grads["gate_b"] = _merge_bwd(dmerged, gates, y_ssd, y_att, w["gate_b"])
    dyn = _mm(dy_ssd, w["w_ssd_branch"], tb=True, name="d_yn")
    grads["w_ssd_branch"] = _mm(yn, dy_ssd, ta=True, name="g_w_ssd")
    datt = _mm(dy_att, w["w_attn_branch"], tb=True, name="d_att")
    grads["w_attn_branch"] = _mm(att, dy_att, ta=True, name="g_w_att")
    dz, dpxs, dpb, dpc, ddt, grads["ssd_norm_w"], g_dtb, g_alog, g_dskip = _ssd_bwd(
        dyn, y, z, pre, dt_raw, hsave, dtb, alog, dskip_w, w["ssd_norm_w"])
    grads["ssd_dt_bias"] = _dt_gather(g_dtb)
    grads["ssd_a_log"] = _dt_gather(g_alog)
    grads["ssd_d"] = _dt_gather(g_dskip)
    dpre = jnp.concatenate([dpxs, dpb, dpc], axis=1)
    dxbc, grads["ssd_conv_w"], grads["ssd_conv_b"] = _conv_bwd(dpre, xbc, w["ssd_conv_w"], name="ssd_conv_bwd")
    dqkv, d_tables, d_sinks = _attn_bwd(datt, qkv, tables, sinks)
    grads["attn_sinks"] = d_sinks[:, :ATT_HEADS]
    dtab = jnp.transpose(d_tables, (1, 0, 2, 3)).reshape(ATT_HEADS, NT_ALL)
    grads["rel_bias"] = _bias_grad(dtab, onehot_t).T
    dsegs = {"z": dz, "xbc": dxbc, "dt": ddt, "qkv": dqkv, "g": dgates}
    du, g_in = None, []
    for nm, _ in _IN_SEGS:
        du = _mm(dsegs[nm], segs[nm], tb=True, c=du, name="d_u_" + nm)
        g_in.append(_mm(u, dsegs[nm], ta=True, name="g_w_in_" + nm))
    grads["in_segs"] = g_in
    dh0, grads["norm_mix_w"] = _rms_bwd(h0, w["norm_mix_w"], du, dh1, name="rms_mix_bwd")
    grads["meta_tokens"] = dh0[PAD:BLOCK]
    return loss_row[0, 0], dh0[BLOCK:], grads


def kernel(x, meta_tokens, norm_mix_w, w_in, ssd_conv_w, ssd_conv_b, ssd_dt_bias, ssd_a_log, ssd_d, ssd_norm_w, w_ssd_branch, w_attn_branch, attn_sinks, rel_bias, gate_b, w_out, norm_ffn_w, w_ffn_in, ffn_conv_w, ffn_conv_b, w_ffn_out, norm_final_w, loss_target, m_meta_tokens, m_norm_mix_w, m_w_in, m_ssd_conv_w, m_ssd_conv_b, m_ssd_dt_bias, m_ssd_a_log, m_ssd_d, m_ssd_norm_w, m_w_ssd_branch, m_w_attn_branch, m_attn_sinks, m_rel_bias, m_gate_b, m_w_out, m_norm_ffn_w, m_w_ffn_in, m_ffn_conv_w, m_ffn_conv_b, m_w_ffn_out, m_norm_final_w, v_meta_tokens, v_norm_mix_w, v_w_in, v_ssd_conv_w, v_ssd_conv_b, v_ssd_dt_bias, v_ssd_a_log, v_ssd_d, v_ssd_norm_w, v_w_ssd_branch, v_w_attn_branch, v_attn_sinks, v_rel_bias, v_gate_b, v_w_out, v_norm_ffn_w, v_w_ffn_in, v_ffn_conv_w, v_ffn_conv_b, v_w_ffn_out, v_norm_final_w):
    shard = dict(meta_tokens=meta_tokens, norm_mix_w=norm_mix_w, w_in=w_in, ssd_conv_w=ssd_conv_w,
                 ssd_conv_b=ssd_conv_b, ssd_dt_bias=ssd_dt_bias, ssd_a_log=ssd_a_log, ssd_d=ssd_d,
                 ssd_norm_w=ssd_norm_w, w_ssd_branch=w_ssd_branch, w_attn_branch=w_attn_branch,
                 attn_sinks=attn_sinks, rel_bias=rel_bias, gate_b=gate_b, w_out=w_out, norm_ffn_w=norm_ffn_w,
                 w_ffn_in=w_ffn_in, ffn_conv_w=ffn_conv_w, ffn_conv_b=ffn_conv_b, w_ffn_out=w_ffn_out,
                 norm_final_w=norm_final_w)
    mom_m = dict(zip(_WEIGHTS, (m_meta_tokens, m_norm_mix_w, m_w_in, m_ssd_conv_w, m_ssd_conv_b, m_ssd_dt_bias,
                                m_ssd_a_log, m_ssd_d, m_ssd_norm_w, m_w_ssd_branch, m_w_attn_branch, m_attn_sinks,
                                m_rel_bias, m_gate_b, m_w_out, m_norm_ffn_w, m_w_ffn_in, m_ffn_conv_w, m_ffn_conv_b,
                                m_w_ffn_out, m_norm_final_w)))
    mom_v = dict(zip(_WEIGHTS, (v_meta_tokens, v_norm_mix_w, v_w_in, v_ssd_conv_w, v_ssd_conv_b, v_ssd_dt_bias,
                                v_ssd_a_log, v_ssd_d, v_ssd_norm_w, v_w_ssd_branch, v_w_attn_branch, v_attn_sinks,
                                v_rel_bias, v_gate_b, v_w_out, v_norm_ffn_w, v_w_ffn_in, v_ffn_conv_w, v_ffn_conv_b,
                                v_w_ffn_out, v_norm_final_w)))
    orig_shape = {k: a.shape for k, a in shard.items()}
    two_d = {k: a.reshape(a.shape[-2:]) if a.ndim >= 2 else a.reshape(1, -1) for k, a in shard.items()}
    shape2 = {k: a.shape for k, a in two_d.items()}

    def as2d(tree):
        return {k: tree[k].reshape(shape2[k]) for k in _WEIGHTS}

    mom_m, mom_v = as2d(mom_m), as2d(mom_v)

    def row_pack(tree):
        return jnp.concatenate([tree[k] for k in _ROW_SHARDED], axis=0)

    small_pack = _pack_rows([two_d[k] for k in _SMALL_SHARDED], LANES, SMALL_ROW_MULT)
    w_in_all, w_ffn_in_all, rows_all, small_all = _gather_two_level(
        [two_d["w_in"].astype(BF16), two_d["w_ffn_in"].astype(BF16), row_pack(two_d).astype(BF16), small_pack],
        name="gather_weights")
    full = {k: two_d[k] for k in _SMALL_REPLICATED}
    full["in_segs"] = _w_in_to_segments(w_in_all)
    full["w_ffn_in"] = _ffn_in_from_shards(w_ffn_in_all)
    off = 0
    for k in _ROW_SHARDED:
        r = shape2[k][0]
        full[k] = rows_all[:, off:off + r].reshape(N_DEV * r, D_MODEL)
        off += r
    small_flat = small_all.reshape(N_DEV, -1)
    off = 0
    for k in _SMALL_SHARDED:
        size = int(np.prod(shape2[k]))
        full[k] = _gather_full(small_flat[:, off:off + size].reshape((N_DEV,) + shape2[k]), k, shape2[k])
        off += size

    loss_local, grad_x, grads = _local_step(x[0], loss_target[0], full)

    rows_send = jnp.concatenate([grads[k].reshape(N_DEV, shape2[k][0], D_MODEL) for k in _ROW_SHARDED], axis=1)
    small_names = _SMALL_SHARDED + _SMALL_REPLICATED
    small_send = _pack_rows([grads[k] for k in small_names] + [loss_local.reshape(1)], LANES, SMALL_ROW_MULT)
    big_send = [_segments_to_w_in_shards(grads["in_segs"]), _ffn_in_to_shards(*grads["w_ffn_in"]), rows_send]
    from_sib = _sibling_exchange(big_send + [small_send], [True, True, True, False], name="grads_to_sibling")
    parts = [_pair_sum(mine, sib, name="pair_sum_" + nm, out_dtype=BF16)
             for nm, mine, sib in zip(("w_in", "w_ffn_in", "rows"), big_send, from_sib)]
    parts.append(_add(small_send, from_sib[3], name="pair_sum_small"))
    in_recv, ffn_recv, rows_recv, small_recv = _chip_exchange(parts, [True, True, True, False], name="exchange_grads")

    big = {"w_in": _adamw(two_d["w_in"], in_recv, mom_m["w_in"], mom_v["w_in"], name="adamw_w_in"),
           "w_ffn_in": _adamw(two_d["w_ffn_in"], ffn_recv, mom_m["w_ffn_in"], mom_v["w_ffn_in"], name="adamw_w_ffn_in")}
    rows_out = _adamw(row_pack(two_d), rows_recv, row_pack(mom_m), row_pack(mom_v), name="adamw_rows")
    off = 0
    for k in _ROW_SHARDED:
        r = shape2[k][0]
        big[k] = [a[off:off + r] for a in rows_out]
        off += r
    me =4 * lax.axis_index("x") + 2 * lax.axis_index("y") + lax.axis_index("c")
    small_full_shapes = [grads[k].shape for k in small_names]
    n_small = sum(int(np.prod(s)) for s in small_full_shapes)

    def packed_small(tree):
        parts = []
        for k in small_names:
            a = tree[k]
            if k in _SMALL_SHARDED:
                fullw = jnp.zeros(grads[k].shape, F32)
                a = lax.dynamic_update_slice(fullw, a, (0, me * a.shape[1]))
            parts.append(a)
        return _pack_rows(parts + [jnp.zeros((1,), F32)], LANES, SMALL_ROW_MULT)

    g_small, d_small, m_small, v_small = _adamw(packed_small(two_d), small_recv, packed_small(mom_m),
                                                packed_small(mom_v), name="adamw_small")

    def unpack_all(which, small):
        out = {k: big[k][which] for k in _BIG}
        flat = small.reshape(-1)
        for k, a in zip(small_names, _unpack(flat, small_full_shapes)):
            if k in _SMALL_SHARDED:
                a = lax.dynamic_slice(a, (0, me * shape2[k][1]), shape2[k])
            out[k] = a
        return out, flat[n_small]

    g_all, loss = unpack_all(0, g_small)
    d_all, _ = unpack_all(1, d_small)
    m_all, _ = unpack_all(2, m_small)
    v_all, _ = unpack_all(3, v_small)

    def final(tree):
        return [tree[k].reshape(orig_shape[k]) for k in _WEIGHTS]

    return (loss, grad_x[None], *final(g_all), *final(d_all), *final(m_all), *final(v_all))
```

```python
import functools
import math

import numpy as np
import jax
import jax.numpy as jnp
from jax import lax
from jax.experimental import pallas as pl
from jax.experimental.pallas import tpu as pltpu

F32 = jnp.float32
BF16 = jnp.bfloat16
HIGHEST = lax.Precision.HIGHEST

D_MODEL = 1024
N_META = 16
BLOCK = 128
PAD = BLOCK - N_META
EPS = 1e-6
NEG = -1e30
SSD_INNER = 2 * D_MODEL
SSD_HEADDIM = 64
SSD_HEADS = SSD_INNER // SSD_HEADDIM
SSD_GROUPS = 4
SSD_HPG = SSD_HEADS // SSD_GROUPS
SSD_STATE = 128
SSD_CONV = 4
SSD_GW = SSD_HPG * SSD_HEADDIM
SSD_BC = SSD_GROUPS * SSD_STATE
SSD_XBC = SSD_INNER + 2 * SSD_BC
ATT_HEADS = 16
ATT_KV_HEADS = 2
ATT_HEADDIM = 64
ATT_GQ = ATT_HEADS // ATT_KV_HEADS
ATT_Q = ATT_HEADS * ATT_HEADDIM
ATT_KV = ATT_KV_HEADS * ATT_HEADDIM
REL_BUCKETS = 32
REL_MAX_DIST = 128
D_FF = 2816
FFN_CONV = 3
ADAM_LR = 0.001
ADAM_B1 = 0.9
ADAM_B2 = 0.999
ADAM_EPS = 1e-08
ADAM_WD = 0.01
ADAM_STEP = 10

N_DEV = 8
LANES = 128
SUBLANES = 8
DT_W = SSD_GROUPS * LANES
VMEM_LIMIT_BYTES = 56 * 1024 * 1024
MESH = pl.DeviceIdType.MESH

SMALL_ROW_MULT = 16

N_KEYS = 3 * BLOCK
NT_ALL = 3 * N_KEYS * BLOCK
NT_TILE = 8192


def _cparams(*sem):
    return pltpu.CompilerParams(dimension_semantics=sem, vmem_limit_bytes=VMEM_LIMIT_BYTES)


def _row_tile(n, cap):
    best = None
    for t in range(16, min(n, cap) + 1, 16):
        if n % t == 0:
            best = t
    return best or n


def _col_tile(n, cap):
    for t in (1408, 1280, 1024, 768, 640, 512, 384, 256, 128):
        if t <= cap and n % t == 0:
            return t
    return n


def _silu(x):
    return x * jax.nn.sigmoid(x)


def _dsilu(x):
    s = jax.nn.sigmoid(x)
    return s * (1.0 + x * (1.0 - s))


def _softplus(x):
    return jnp.maximum(x, 0.0) + jnp.log(1.0 + jnp.exp(-jnp.abs(x)))


def _dot_nt(a, b):
    return lax.dot_general(a, b, (((1,), (1,)), ((), ())), preferred_element_type=F32)


def _dot_tn(a, b):
    return lax.dot_general(a, b, (((0,), (0,)), ((), ())), preferred_element_type=F32)


def _dot(a, b):
    return jnp.dot(a, b, preferred_element_type=F32)


def _bf16_terms(x, terms):
    out, rest = [], x
    for _ in range(terms):
        part = rest.astype(BF16)
        out.append(part)
        rest = rest - part.astype(F32)
    return out


def _dot_sel(x, sel, terms=3):
    return sum(_dot(part, sel) for part in _bf16_terms(x, terms))


def _sel_dot(sel, x, terms=3):
    return sum(_dot(sel, part) for part in _bf16_terms(x, terms))


def _sum_all(x):
    return jnp.sum(jnp.sum(x, axis=1, keepdims=True), axis=0, keepdims=True)


MM_ROW_CAPS = (1664, 832, 416)
MM_COL_CAP = 1408
MM_VMEM_BUDGET = 44 * 1024 * 1024


def _mm_tiles(rows, cols, vmem_bytes):
    col_cands = [t for t in (1408, 1280, 1024, 768, 640, 512, 384, 256, 128) if cols % t == 0]
    if cols <= 2 * MM_COL_CAP:
        col_cands.append(cols)
    best = None
    for cap in MM_ROW_CAPS:
        tr = _row_tile(rows, cap)
        for tc in col_cands:
            if vmem_bytes(tr, tc) <= MM_VMEM_BUDGET and (best is None or tr * tc > best[0] * best[1]):
                best = (tr, tc)
    assert best is not None, (rows, cols)
    return best


def _mm(a, b, *, name, ta=False, tb=False, c=None, mask=False, out_dtype=F32):
    if not ta:
        m, k = a.shape
        n = b.shape[0] if tb else b.shape[1]
        tm, tn = _mm_tiles(m, n, lambda t_m, t_n: 2 * (t_m * k * a.dtype.itemsize + k * t_n * b.dtype.itemsize
                                                       + t_m * t_n * (jnp.dtype(out_dtype).itemsize
                                                                      + (0 if c is None else c.dtype.itemsize)))
                           + 4 * t_m * t_n)

        def body(*refs):
            if c is None:
                a_ref, b_ref, o_ref = refs
            else:
                a_ref, b_ref, c_ref, o_ref = refs
            acc = (_dot_nt if tb else _dot)(a_ref[...].astype(BF16), b_ref[...].astype(BF16))
            if mask:
                row = pl.program_id(0) * tm + lax.broadcasted_iota(jnp.int32, (tm, 1), 0)
                acc = jnp.where(row >= PAD, acc, 0.0)
            if c is not None:
                acc = acc + c_ref[...]
            o_ref[...] = acc.astype(out_dtype)

        b_spec = pl.BlockSpec((tn, k), lambda i, j: (j, 0)) if tb else pl.BlockSpec((k, tn), lambda i, j: (0, j))
        in_specs = [pl.BlockSpec((tm, k), lambda i, j: (i, 0)), b_spec]
        args = [a, b]
        if c is not None:
            in_specs.append(pl.BlockSpec((tm, tn), lambda i, j: (i, j)))
            args.append(c)
        return pl.pallas_call(
            body, name=name, grid=(m // tm, n // tn), in_specs=in_specs,
            out_specs=pl.BlockSpec((tm, tn), lambda i, j: (i, j)),
            out_shape=jax.ShapeDtypeStruct((m, n), out_dtype),
            compiler_params=_cparams("parallel", "parallel"))(*args)

    kc, m = a.shape
    n = b.shape[1]
    tm = _col_tile(m, MM_COL_CAP)
    tk, tn = _mm_tiles(kc, n, lambda t_k, t_n: 2 * (t_k * tm * a.dtype.itemsize + t_k * t_n * b.dtype.itemsize
                                                    + 4 * tm * t_n) + 4 * tm * t_n)

    def body_t(a_ref, b_ref, o_ref):
        kk = pl.program_id(2)
        bb = b_ref[...]
        if mask:
            row = kk * tk + lax.broadcasted_iota(jnp.int32, (tk, 1), 0)
            bb = jnp.where(row >= PAD, bb, jnp.zeros_like(bb))
        p = _dot_tn(a_ref[...].astype(BF16), bb.astype(BF16))

        @pl.when(kk == 0)
        def _():
            o_ref[...] = p

        @pl.when(kk > 0)
        def _():
            o_ref[...] += p

    return pl.pallas_call(
        body_t, name=name, grid=(m // tm, n // tn, kc // tk),
        in_specs=[pl.BlockSpec((tk, tm), lambda i, j, kk: (kk, i)), pl.BlockSpec((tk, tn), lambda i, j, kk: (kk, j))],
        out_specs=pl.BlockSpec((tm, tn), lambda i, j, kk: (i, j)),
        out_shape=jax.ShapeDtypeStruct((m, n), F32),
        compiler_params=_cparams("parallel", "parallel", "arbitrary"))(a, b)


def _rms_fwd(h, w, *, name):
    n, d = h.shape
    tm = _row_tile(n, 832)

    def body(h_ref, w_ref, o_ref):
        x = h_ref[...]
        r = lax.rsqrt(jnp.mean(x * x, axis=-1, keepdims=True) + EPS)
        o_ref[...] = (x * r * w_ref[...]).astype(BF16)

    return pl.pallas_call(
        body, name=name, grid=(n // tm,),
        in_specs=[pl.BlockSpec((tm, d), lambda i: (i, 0)), pl.BlockSpec((1, d), lambda i: (0, 0))],
        out_specs=pl.BlockSpec((tm, d), lambda i: (i, 0)),
        out_shape=jax.ShapeDtypeStruct((n, d), BF16),
        compiler_params=_cparams("parallel"))(h, w)


def _rms_bwd(x, w, dy, dres, *, name):
    n, d = x.shape
    tm = _row_tile(n, 832)

    def body(x_ref, w_ref, dy_ref, dres_ref, dx_ref, dw_ref):
        i = pl.program_id(0)
        xv = x_ref[...]
        r = lax.rsqrt(jnp.mean(xv * xv, axis=-1, keepdims=True) + EPS)
        xh = xv * r
        dyv = dy_ref[...]
        g = dyv * w_ref[...]
        dx_ref[...] = r * (g - xh * jnp.mean(g * xh, axis=-1, keepdims=True)) + dres_ref[...]
        part = jnp.sum(dyv * xh, axis=0, keepdims=True)

        @pl.when(i == 0)
        def _():
            dw_ref[...] = part

        @pl.when(i > 0)
        def _():
            dw_ref[...] += part

    row = pl.BlockSpec((tm, d), lambda i: (i, 0))
    vec = pl.BlockSpec((1, d), lambda i: (0, 0))
    return pl.pallas_call(
        body, name=name, grid=(n // tm,), in_specs=[row, vec, row, row], out_specs=[row, vec],
        out_shape=[jax.ShapeDtypeStruct((n, d), F32), jax.ShapeDtypeStruct((1, d), F32)],
        compiler_params=_cparams("arbitrary"))(x, w, dy, dres)


def _final_loss(h, w, target):
    n, d = h.shape
    nb = n // BLOCK

    def body(h_ref, w_ref, t_ref, dh_ref, loss_ref, dw_ref):
        i = pl.program_id(0)
        xv = h_ref[...]
        r = lax.rsqrt(jnp.mean(xv * xv, axis=-1, keepdims=True) + EPS)
        xh = xv * r
        wv = w_ref[...]
        err = jnp.where(i >= 1, xh * wv - t_ref[...], 0.0)
        dyv = err * (1.0 / d)
        g = dyv * wv
        dh_ref[...] = r * (g - xh * jnp.mean(g * xh, axis=-1, keepdims=True))
        lpart = jnp.broadcast_to(0.5 * _sum_all(err * err) * (1.0 / d), (1, LANES))
        wpart = jnp.sum(dyv * xh, axis=0, keepdims=True)

        @pl.when(i == 0)
        def _():
            loss_ref[...] = lpart
            dw_ref[...] = wpart

        @pl.when(i > 0)
        def _():
            loss_ref[...] += lpart
            dw_ref[...] += wpart

    row = pl.BlockSpec((BLOCK, d), lambda i: (i, 0))
    vec = pl.BlockSpec((1, d), lambda i: (0, 0))
    return pl.pallas_call(
        body, name="final_loss", grid=(nb,),
        in_specs=[row, vec, pl.BlockSpec((BLOCK, d), lambda i: (jnp.maximum(i - 1, 0), 0))],
        out_specs=[row, pl.BlockSpec((1, LANES), lambda i: (0, 0)), vec],
        out_shape=[jax.ShapeDtypeStruct((n, d), F32), jax.ShapeDtypeStruct((1, LANES), F32),
                   jax.ShapeDtypeStruct((1, d), F32)],
        compiler_params=_cparams("arbitrary"))(h, w, target)


def _main_spec(tm, cb, off=0):
    return pl.BlockSpec((tm, cb), lambda j, i: (i, j + off))


def _prev_spec(tm, cb, off=0):
    r8 = tm // SUBLANES
    return pl.BlockSpec((SUBLANES, cb), lambda j, i: (jnp.maximum(i * r8 - 1, 0), j + off))


def _next_spec(tm, cb, n_rows, off=0):
    r8 = tm // SUBLANES
    last = n_rows // SUBLANES - 1
    return pl.BlockSpec((SUBLANES, cb), lambda j, i: (jnp.minimum((i + 1) * r8, last), j + off))


def _with_prev(prev_ref, main_ref, i):
    prev = jnp.where(i > 0, prev_ref[...], 0.0)
    return jnp.concatenate([prev, main_ref[...]], axis=0)


def _with_next(main, nxt, i, n_tiles):
    return jnp.concatenate([main, jnp.where(i < n_tiles - 1, nxt, 0.0)], axis=0)


def _back(xx, s, tm):
    if s == 0:
        return xx[SUBLANES:SUBLANES + tm]
    return pltpu.roll(xx, s, 0)[SUBLANES:SUBLANES + tm]


def _ahead(xx, s, tm):
    if s == 0:
        return xx[:tm]
    return pltpu.roll(xx, tm + SUBLANES - s, 0)[:tm]


def _conv_fwd(x, w, b, *, name):
    n, cdim = x.shape
    kw = w.shape[0]
    tm = _row_tile(n, 832)
    cb = _col_tile(cdim, 512)

    def body(xp_ref, x_ref, w_ref, b_ref, o_ref):
        xx = _with_prev(xp_ref, x_ref, pl.program_id(1))
        acc = jnp.broadcast_to(b_ref[...], (tm, cb))
        for k in range(kw):
            acc = acc + w_ref[k:k + 1, :] * _back(xx, kw - 1 - k, tm)
        o_ref[...] = acc

    return pl.pallas_call(
        body, name=name, grid=(cdim // cb, n // tm),
        in_specs=[_prev_spec(tm, cb), _main_spec(tm, cb), pl.BlockSpec((kw, cb), lambda j, i: (0, j)),
                  pl.BlockSpec((1, cb), lambda j, i: (0, j))],
        out_specs=_main_spec(tm, cb),
        out_shape=jax.ShapeDtypeStruct((n, cdim), F32),
        compiler_params=_cparams("parallel", "parallel"))(x, x, w, b)


def _conv_bwd_core(dpre_ext, x_ext, w_ref, kw, tm):
    dpre = dpre_ext[:tm]
    dx = None
    dws = []
    for k in range(kw):
        term = w_ref[k:k + 1, :] * _ahead(dpre_ext, kw - 1 - k, tm)
        dx = term if dx is None else dx + term
        dws.append(jnp.sum(dpre * _back(x_ext, kw - 1 - k, tm), axis=0, keepdims=True))
    return dx, dws, jnp.sum(dpre, axis=0, keepdims=True)


def _acc_rows(i, dw_ref, db_ref, dws, db):
    @pl.when(i == 0)
    def _():
        for k, v in enumerate(dws):
            dw_ref[k:k + 1, :] = v
        db_ref[...] = db

    @pl.when(i > 0)
    def _():
        for k, v in enumerate(dws):
            dw_ref[k:k + 1, :] += v
        db_ref[...] += db


def _conv_bwd(dpre, x, w, *, name):
    n, cdim = x.shape
    kw = w.shape[0]
    tm = _row_tile(n, 832)
    cb = _col_tile(cdim, 512)
    nt = n // tm

    def body(d_ref, dn_ref, xp_ref, x_ref, w_ref, dx_ref, dw_ref, db_ref):
        i = pl.program_id(1)
        dpre_ext = _with_next(d_ref[...], dn_ref[...], i, nt)
        x_ext = _with_prev(xp_ref, x_ref, i)
        dx, dws, db = _conv_bwd_core(dpre_ext, x_ext, w_ref, kw, tm)
        dx_ref[...] = dx.astype(BF16)
        _acc_rows(i, dw_ref, db_ref, dws, db)

    wspec = pl.BlockSpec((kw, cb), lambda j, i: (0, j))
    bspec = pl.BlockSpec((1, cb), lambda j, i: (0, j))
    return pl.pallas_call(
        body, name=name, grid=(cdim // cb, nt),
        in_specs=[_main_spec(tm, cb), _next_spec(tm, cb, n), _prev_spec(tm, cb), _main_spec(tm, cb), wspec],
        out_specs=[_main_spec(tm, cb), wspec, bspec],
        out_shape=[jax.ShapeDtypeStruct((n, cdim), BF16), jax.ShapeDtypeStruct((kw, cdim), F32),
                   jax.ShapeDtypeStruct((1, cdim), F32)],
        compiler_params=_cparams("parallel", "arbitrary"))(dpre, dpre, x, x, w)


def _ffn_act_fwd(x, w, b):
    n = x.shape[0]
    kw = w.shape[0]
    tm = _row_tile(n, 832)
    cb = _col_tile(D_FF, 256)
    nc = D_FF // cb

    def body(xpu_ref, xu_ref, xpg_ref, xg_ref, wu_ref, wg_ref, bu_ref, bg_ref, hu_ref, hg_ref, act_ref):
        i = pl.program_id(1)
        outs = []
        for xp_ref, x_ref, w_ref, b_ref in ((xpu_ref, xu_ref, wu_ref, bu_ref), (xpg_ref, xg_ref, wg_ref, bg_ref)):
            xx = _with_prev(xp_ref, x_ref, i)
            acc = jnp.broadcast_to(b_ref[...], (tm, cb))
            for k in range(kw):
                acc = acc + w_ref[k:k + 1, :] * _back(xx, kw - 1 - k, tm)
            outs.append(acc)
        hu_ref[...] = outs[0]
        hg_ref[...] = outs[1]
        act_ref[...] = (_silu(outs[1]) * outs[0]).astype(BF16)

    def wspec(off):
        return pl.BlockSpec((kw, cb), lambda j, i: (0, j + off))

    def bspec(off):
        return pl.BlockSpec((1, cb), lambda j, i: (0, j + off))

    out = _main_spec(tm, cb)
    return pl.pallas_call(
        body, name="ffn_act_fwd", grid=(nc, n // tm),
        in_specs=[_prev_spec(tm, cb), _main_spec(tm, cb), _prev_spec(tm, cb, nc), _main_spec(tm, cb, nc),
                  wspec(0), wspec(nc), bspec(0), bspec(nc)],
        out_specs=[out, out, out],
        out_shape=[jax.ShapeDtypeStruct((n, D_FF), F32), jax.ShapeDtypeStruct((n, D_FF), F32),
                   jax.ShapeDtypeStruct((n, D_FF), BF16)],
        compiler_params=_cparams("parallel", "parallel"))(x, x, x, x, w, w, b, b)


def _ffn_act_bwd(dact, hu, hg, x, w):
    n = x.shape[0]
    kw = w.shape[0]
    tm = _row_tile(n, 832)
    cb = _col_tile(D_FF, 256)
    nc = D_FF // cb
    nt = n // tm

    def body(d_ref, dn_ref, hu_ref, hun_ref, hg_ref, hgn_ref, xpu_ref, xu_ref, xpg_ref, xg_ref, wu_ref, wg_ref,
             dxu_ref, dxg_ref, dwu_ref, dwg_ref, dbu_ref, dbg_ref):
        i = pl.program_id(1)
        dact_e = _with_next(d_ref[...], dn_ref[...], i, nt)
        up_e = _with_next(hu_ref[...], hun_ref[...], i, nt)
        gate_e = _with_next(hg_ref[...], hgn_ref[...], i, nt)
        dup_e = dact_e * _silu(gate_e)
        dgate_e = dact_e * up_e * _dsilu(gate_e)
        dx, dws, db = _conv_bwd_core(dup_e, _with_prev(xpu_ref, xu_ref, i), wu_ref, kw, tm)
        dxu_ref[...] = dx.astype(BF16)
        _acc_rows(i, dwu_ref, dbu_ref, dws, db)
        dx, dws, db = _conv_bwd_core(dgate_e, _with_prev(xpg_ref, xg_ref, i), wg_ref, kw, tm)
        dxg_ref[...] = dx.astype(BF16)
        _acc_rows(i, dwg_ref, dbg_ref, dws, db)

    main, nxt = _main_spec(tm, cb), _next_spec(tm, cb, n)
    wspec0 = pl.BlockSpec((kw, cb), lambda j, i: (0, j))
    wspec1 = pl.BlockSpec((kw, cb), lambda j, i: (0, j + nc))
    bspec = pl.BlockSpec((1, cb), lambda j, i: (0, j))
    return pl.pallas_call(
        body, name="ffn_act_bwd", grid=(nc, nt),
        in_specs=[main, nxt, main, nxt, main, nxt,
                  _prev_spec(tm, cb), _main_spec(tm, cb), _prev_spec(tm, cb, nc), _main_spec(tm, cb, nc),
                  wspec0, wspec1],
        out_specs=[main, main, wspec0, wspec0, bspec, bspec],
        out_shape=[jax.ShapeDtypeStruct((n, D_FF), BF16), jax.ShapeDtypeStruct((n, D_FF), BF16),
                   jax.ShapeDtypeStruct((kw, D_FF), F32), jax.ShapeDtypeStruct((kw, D_FF), F32),
                   jax.ShapeDtypeStruct((1, D_FF), F32), jax.ShapeDtypeStruct((1, D_FF), F32)],
        compiler_params=_cparams("parallel", "arbitrary"))(dact, dact, hu, hu, hg, hg, x, x, x, x, w, w)


def _ssd_prep(pxs_ref, pb_ref, pc_ref, dtr_ref, dtb_ref, alog_ref, c):
    xs = _silu(pxs_ref[...])
    bm = _silu(pb_ref[...])
    cm = _silu(pc_ref[...])
    row = lax.broadcasted_iota(jnp.int32, (BLOCK, 1), 0) + c * BLOCK
    valid = (row >= PAD).astype(F32)
    dtr = dtr_ref[...] + dtb_ref[...]
    dt = _softplus(dtr) * valid
    a = -jnp.exp(alog_ref[...])
    lam = dt * a
    ri = lax.broadcasted_iota(jnp.int32, (BLOCK, BLOCK), 0)
    ci = lax.broadcasted_iota(jnp.int32, (BLOCK, BLOCK), 1)
    causal = ci <= ri
    cs = _sel_dot(causal.astype(BF16), lam)
    return xs, bm, cm, valid, dtr, dt, a, lam, cs, causal


def _head_cols(r):
    return slice(SSD_HEADDIM * r, SSD_HEADDIM * (r + 1))


def _ssd_specs(nc, rev):
    def cidx(c):
        return nc - 1 - c if rev else c

    xs = pl.BlockSpec((BLOCK, SSD_GW), lambda g, c: (cidx(c), g))
    bspec = pl.BlockSpec((BLOCK, SSD_STATE), lambda g, c: (cidx(c), SSD_INNER // SSD_STATE + g))
    cspec = pl.BlockSpec((BLOCK, SSD_STATE), lambda g, c: (cidx(c), (SSD_INNER + SSD_BC) // SSD_STATE + g))
    lane = pl.BlockSpec((BLOCK, LANES), lambda g, c: (cidx(c), g))
    vec = pl.BlockSpec((1, LANES), lambda g, c: (0, g))
    wide_vec = pl.BlockSpec((1, SSD_GW), lambda g, c: (0, g))
    hsave = pl.BlockSpec((1, 1, SSD_GW, SSD_STATE), lambda g, c: (cidx(c), g, 0, 0))
    return xs, bspec, cspec, lane, vec, wide_vec, hsave


def _head_spread_matrix():
    r = lax.broadcasted_iota(jnp.int32, (LANES, SSD_GW), 0)
    col = lax.broadcasted_iota(jnp.int32, (LANES, SSD_GW), 1)
    return (col // SSD_HEADDIM == r).astype(BF16)


def _const_spec(shape):
    return pl.BlockSpec(shape, lambda g, c: (0,) * len(shape))


def _spread_heads(per_head, e_ref):
    wide = _dot_sel(jnp.concatenate(per_head, axis=0), e_ref[...])
    return [wide[BLOCK * k:BLOCK * (k + 1)] for k in range(len(per_head))]


def _ssd_fwd(pre, dt_raw, z, dtb, alog, dskip_w, norm_w):
    n = pre.shape[0]
    nc = n // BLOCK
    xs_s, b_s, c_s, lane_s, vec_s, wide_s, hs_s = _ssd_specs(nc, False)

    def body(pxs_ref, pb_ref, pc_ref, dtr_ref, z_ref, dtb_ref, alog_ref, dskw_ref, nw_ref, e_ref,
             y_ref, yn_ref, hs_ref, h_scr):
        c = pl.program_id(1)

        @pl.when(c == 0)
        def _():
            h_scr[...] = jnp.zeros_like(h_scr)

        xs, bm, cm, _, _, dt, _, _, cs, causal = _ssd_prep(pxs_ref, pb_ref, pc_ref, dtr_ref, dtb_ref, alog_ref, c)
        cst = cs.T
        cs_last = cs[BLOCK - 1:BLOCK, :]
        dt_w, ecs_w, dec_w = _spread_heads([dt, jnp.exp(cs), jnp.exp(cs_last - cs)], e_ref)
        xdt = xs * dt_w
        bmb = bm.astype(BF16)
        cmb = cm.astype(BF16)
        cb = _dot_nt(cmb, bmb)
        hg = h_scr[...]
        hs_ref[0, 0] = hg
        y = _dot_nt(cmb, hg.astype(BF16)) * ecs_w + dskw_ref[...] * xs
        first = lax.broadcasted_iota(jnp.int32, (BLOCK, LANES), 1) < SSD_HEADDIM
        diag = []
        for j in range(SSD_HPG // 2):
            xp = xdt[:, LANES * j:LANES * (j + 1)].astype(BF16)
            res = []
            for r in (2 * j, 2 * j + 1):
                lm = jnp.exp(jnp.where(causal, cs[:, r:r + 1] - cst[r:r + 1, :], NEG))
                res.append(_dot((cb * lm).astype(BF16), xp))
            diag.append(jnp.where(first, res[0], res[1]))
        y = y + jnp.concatenate(diag, axis=1)
        st = _dot_tn((xdt * dec_w).astype(BF16), bmb)
        eh = jnp.exp(cs_last)
        for r in range(SSD_HPG):
            rows = _head_cols(r)
            h_scr[rows, :] = hg[rows, :] * eh[:, r:r + 1] + st[rows, :]
        y_ref[...] = y
        gts = y * _silu(z_ref[...])
        rr = lax.rsqrt(jnp.mean(gts * gts, axis=-1, keepdims=True) + EPS)
        yn_ref[...] = (gts * rr * nw_ref[...]).astype(BF16)

    return pl.pallas_call(
        body, name="ssd_fwd", grid=(SSD_GROUPS, nc),
        in_specs=[xs_s, b_s, c_s, lane_s, xs_s, vec_s, vec_s, wide_s, wide_s, _const_spec((LANES, SSD_GW))],
        out_specs=[xs_s, xs_s, hs_s],
        out_shape=[jax.ShapeDtypeStruct((n, SSD_INNER), F32), jax.ShapeDtypeStruct((n, SSD_INNER), BF16),
                   jax.ShapeDtypeStruct((nc, SSD_GROUPS, SSD_GW, SSD_STATE), F32)],
        scratch_shapes=[pltpu.VMEM((SSD_GW, SSD_STATE), F32)],
        compiler_params=_cparams("parallel", "arbitrary"))(
            pre, pre, pre, dt_raw, z, dtb, alog, dskip_w, norm_w, _head_spread_matrix())


def _lane_put(acc, col, r):
    lane = lax.broadcasted_iota(jnp.int32, acc.shape, 1)
    return jnp.where(lane == r, col, acc)


def _ssd_bwd(dyn, y, z, pre, dt_raw, hsave, dtb, alog, dskip_w, norm_w):
    n = pre.shape[0]
    nc = n // BLOCK
    spread = _head_spread_matrix()
    xs_s, b_s, c_s, lane_s, vec_s, wide_s, hs_s = _ssd_specs(nc, True)
    bc_out =pl.BlockSpec((BLOCK, SSD_STATE), lambda g, c: (nc - 1 - c, g))

    def body(dyn_ref, y_ref, z_ref, pxs_ref, pb_ref, pc_ref, dtr_ref, hs_ref, dtb_ref, alog_ref, dskw_ref, nw_ref,
             e_ref, r_ref,
             dz_ref, dxs_ref, dbm_ref, dcm_ref, ddt_ref, dnw_ref, ddtb_ref, dalog_ref, ddsk_ref, g_scr):
        step = pl.program_id(1)
        c = nc - 1 - step

        @pl.when(step == 0)
        def _():
            g_scr[...] = jnp.zeros_like(g_scr)

        xs, bm, cm, valid, dtr, dt, a, lam, cs, causal = _ssd_prep(
            pxs_ref, pb_ref, pc_ref, dtr_ref, dtb_ref, alog_ref, c)
        cst = cs.T
        cs_last = cs[BLOCK - 1:BLOCK, :]
        bmb = bm.astype(BF16)
        cmb = cm.astype(BF16)
        cb = _dot_nt(cmb, bmb)
        hg = hs_ref[0, 0]
        hgb = hg.astype(BF16)
        yoff = _dot_nt(cmb, hgb)
        gn = g_scr[...]
        gnb = gn.astype(BF16)

        zv = z_ref[...]
        yv = y_ref[...]
        sz = _silu(zv)
        gts = yv * sz
        rr = lax.rsqrt(jnp.mean(gts * gts, axis=-1, keepdims=True) + EPS)
        xh = gts * rr
        dynv = dyn_ref[...]
        gg = dynv * nw_ref[...]
        dgts = rr * (gg - xh * jnp.mean(gg * xh, axis=-1, keepdims=True))
        dnw = jnp.sum(dynv * xh, axis=0, keepdims=True)
        dy = dgts * sz
        dz_ref[...] = (dgts * yv * _dsilu(zv)).astype(BF16)

        ecs = jnp.exp(cs)
        dec = jnp.exp(cs_last - cs)
        eh = jnp.exp(cs_last)
        dt_w, ecs_w, dec_w = _spread_heads([dt, ecs, dec], e_ref)
        red_m = r_ref[...]

        def head_sums(v):
            return _dot_sel(v, red_m, terms=2)

        xdt = xs * dt_w
        q_all = _dot_nt(bmb, gnb)
        w_all = (dy * ecs_w).astype(BF16)
        e_hl = head_sums(q_all * xdt) * dec
        dcs_col = head_sums(dy * yoff) * ecs - e_hl
        gh = jnp.zeros((1, LANES), F32)
        prod = gn * hg
        for r in range(SSD_HPG):
            gh = _lane_put(gh, _sum_all(prod[_head_cols(r), :]), r)
        dcs_last = jnp.sum(e_hl, axis=0, keepdims=True) + eh * gh
        ddsk = jnp.sum(head_sums(dy * xs), axis=0, keepdims=True)
        cbt = _dot_nt(bmb, cmb)
        lane = lax.broadcasted_iota(jnp.int32, (BLOCK, LANES), 1)
        first = lane < SSD_HEADDIM
        causal_t = lax.broadcasted_iota(jnp.int32, (BLOCK, BLOCK), 1) >= lax.broadcasted_iota(
            jnp.int32, (BLOCK, BLOCK), 0)
        sub = lax.broadcasted_iota(jnp.int32, (SUBLANES, BLOCK), 0)
        dcs_row = jnp.zeros((SUBLANES, BLOCK), F32)
        dcb = jnp.zeros((BLOCK, BLOCK), F32)
        dxdt_pairs = []
        for j in range(SSD_HPG // 2):
            tile = slice(LANES * j, LANES * (j + 1))
            dy_p = dy[:, tile]
            dyb = dy_p.astype(BF16)
            xdtb = xdt[:, tile].astype(BF16)
            res = []
            for half, r in enumerate((2 * j, 2 * j + 1)):
                csc, csr = cs[:, r:r + 1], cst[r:r + 1, :]
                lm = jnp.exp(jnp.where(causal, csc - csr, NEG))
                lmt = jnp.exp(jnp.where(causal_t, csr - csc, NEG))
                keep = first if half == 0 else jnp.logical_not(first)
                gm = _dot_nt(jnp.where(keep, dy_p, 0.0).astype(BF16), xdtb) * lm
                dcb = dcb + gm
                mm_ = gm * cb
                dcs_col = dcs_col + jnp.where(lane == r, jnp.sum(mm_, axis=1, keepdims=True), 0.0)
                dcs_row = jnp.where(sub == r, jnp.sum(mm_, axis=0, keepdims=True), dcs_row)
                res.append(_dot((cbt * lmt).astype(BF16), dyb))
            dxdt_pairs.append(jnp.where(first, res[0], res[1]))
        dxdt = jnp.concatenate(dxdt_pairs, axis=1) + q_all * dec_w
        ddt_x = head_sums(dxdt * xs)
        dxs = dxdt * dt_w + dskw_ref[...] * dy
        dcbb = dcb.astype(BF16)
        dcm = _dot(w_all, hgb) + _dot(dcbb, bmb)
        dbm = _dot((xdt * dec_w).astype(BF16), gnb) + _dot_tn(dcbb, cmb)
        dh_off = _dot_tn(w_all, cmb)
        for r in range(SSD_HPG):
            rows = _head_cols(r)
            g_scr[rows, :] = gn[rows, :] * eh[:, r:r + 1] + dh_off[rows, :]

        pad_rows = jnp.zeros((BLOCK - SUBLANES, BLOCK), F32)
        dcs = dcs_col - jnp.concatenate([dcs_row, pad_rows], axis=0).T
        rsel = lax.broadcasted_iota(jnp.int32, (BLOCK, LANES), 0)
        dcs = dcs + jnp.where(rsel == BLOCK - 1, dcs_last, 0.0)
        ri = lax.broadcasted_iota(jnp.int32, (BLOCK, BLOCK), 0)
        ci = lax.broadcasted_iota(jnp.int32, (BLOCK, BLOCK), 1)
        dlam = _sel_dot((ci >= ri).astype(BF16), dcs)
        head = lane < SSD_HPG
        ddt = dlam * a + ddt_x
        ddtr = jnp.where(head, ddt * jax.nn.sigmoid(dtr) * valid, 0.0)
        ddt_ref[...] = ddtr.astype(BF16)
        dalog = jnp.sum(jnp.where(head, dlam * lam, 0.0), axis=0, keepdims=True)
        ddtb = jnp.sum(ddtr, axis=0, keepdims=True)

        dxs_ref[...] = dxs * _dsilu(pxs_ref[...])
        dbm_ref[...] = dbm * _dsilu(pb_ref[...])
        dcm_ref[...] = dcm * _dsilu(pc_ref[...])

        @pl.when(step == 0)
        def _():
            dnw_ref[...] = dnw
            ddtb_ref[...] = ddtb
            dalog_ref[...] = dalog
            ddsk_ref[...] = ddsk

        @pl.when(step > 0)
        def _():
            dnw_ref[...] += dnw
            ddtb_ref[...] += ddtb
            dalog_ref[...] += dalog
            ddsk_ref[...] += ddsk

    return pl.pallas_call(
        body, name="ssd_bwd", grid=(SSD_GROUPS, nc),
        in_specs=[xs_s, xs_s, xs_s, xs_s, b_s, c_s, lane_s, hs_s, vec_s, vec_s, wide_s, wide_s,
                  _const_spec((LANES, SSD_GW)), _const_spec((SSD_GW, LANES))],
        out_specs=[xs_s, xs_s, bc_out, bc_out, lane_s, wide_s, vec_s, vec_s, vec_s],
        out_shape=[jax.ShapeDtypeStruct((n, SSD_INNER), BF16), jax.ShapeDtypeStruct((n, SSD_INNER), F32),
                   jax.ShapeDtypeStruct((n, SSD_BC), F32), jax.ShapeDtypeStruct((n, SSD_BC), F32),
                   jax.ShapeDtypeStruct((n, DT_W), BF16), jax.ShapeDtypeStruct((1, SSD_INNER), F32),
                   jax.ShapeDtypeStruct((1, DT_W), F32), jax.ShapeDtypeStruct((1, DT_W), F32),
                   jax.ShapeDtypeStruct((1, DT_W), F32)],
        scratch_shapes=[pltpu.VMEM((SSD_GW, SSD_STATE), F32)],
        compiler_params=_cparams("parallel", "arbitrary"))(
            dyn, y, z, pre, pre, pre, dt_raw, hsave, dtb, alog, dskip_w, norm_w, spread, spread.T)


def _bucket_table():
    def bucket(dist):
        d = np.maximum(dist, 0)
        half = REL_BUCKETS // 2
        big = half + (np.log(np.maximum(d, half).astype(np.float32) / np.float32(half))
                      / np.float32(math.log(REL_MAX_DIST / half)) * np.float32(REL_BUCKETS - half)).astype(np.int32)
        return np.where(d < half, d, np.minimum(big, REL_BUCKETS - 1)).astype(np.int32)

    l = np.arange(BLOCK)[None, :]
    band = bucket(l + BLOCK - np.arange(2 * BLOCK)[:, None])
    j = np.arange(BLOCK)[:, None]
    tables = [np.concatenate([bucket(v * BLOCK + l - j), band], axis=0) for v in range(3)]
    return np.concatenate([t.reshape(-1) for t in tables])


def _onehot_t():
    buckets = jnp.asarray(_bucket_table())
    return (buckets[None, :] == jnp.arange(REL_BUCKETS, dtype=jnp.int32)[:, None]).astype(F32)


def _bias_tables(rel_t, onehot_t):
    def body(r_ref, oh_ref, o_ref):
        o_ref[...] = jnp.dot(r_ref[...], oh_ref[...], precision=HIGHEST, preferred_element_type=F32)

    return pl.pallas_call(
        body, name="bias_tables", grid=(NT_ALL // NT_TILE,),
        in_specs=[pl.BlockSpec((ATT_HEADS, REL_BUCKETS), lambda i: (0, 0)),
                  pl.BlockSpec((REL_BUCKETS, NT_TILE), lambda i: (0, i))],
        out_specs=pl.BlockSpec((ATT_HEADS, NT_TILE), lambda i: (0, i)),
        out_shape=jax.ShapeDtypeStruct((ATT_HEADS, NT_ALL), F32),
        compiler_params=_cparams("parallel"))(rel_t, onehot_t)


def _bias_grad(dtab, onehot_t):
    def body(d_ref, oh_ref, o_ref):
        i = pl.program_id(0)
        p = lax.dot_general(d_ref[...], oh_ref[...], (((1,), (1,)), ((), ())), precision=HIGHEST,
                            preferred_element_type=F32)

        @pl.when(i == 0)
        def _():
            o_ref[...] = p

        @pl.when(i > 0)
        def _():
            o_ref[...] += p

    return pl.pallas_call(
        body, name="bias_grad", grid=(NT_ALL // NT_TILE,),
        in_specs=[pl.BlockSpec((ATT_HEADS, NT_TILE), lambda i: (0, i)),
                  pl.BlockSpec((REL_BUCKETS, NT_TILE), lambda i: (0, i))],
        out_specs=pl.BlockSpec((ATT_HEADS, REL_BUCKETS), lambda i: (0, 0)),
        out_shape=jax.ShapeDtypeStruct((ATT_HEADS, REL_BUCKETS), F32),
        compiler_params=_cparams("arbitrary"))(dtab, onehot_t)


def _att_mask_t(n, copies):
    far = 4 * BLOCK
    kk = lax.broadcasted_iota(jnp.int32, (N_KEYS, copies * BLOCK), 0)
    li = lax.broadcasted_iota(jnp.int32, (N_KEYS, copies * BLOCK), 1) & (BLOCK - 1)
    meta_ok = (kk >= PAD) & (kk < BLOCK) & (li + jnp.where(n >= 1, far, 0) >= kk)
    prev_ok = (kk >= BLOCK) & (kk < 2 * BLOCK) & (kk - BLOCK > li + jnp.where(n >= 2, 0, far))
    cur_ok = (kk >= 2 * BLOCK) & (kk - 2 * BLOCK <= li - jnp.where(n >= 1, 0, far))
    return meta_ok | prev_ok | cur_ok


def _att_kv(meta_ref, prev_ref, cur_ref):
    kv = jnp.concatenate([meta_ref[...], prev_ref[...], cur_ref[...]], axis=0)
    first = lax.broadcasted_iota(jnp.int32, (N_KEYS, LANES), 1) < ATT_HEADDIM
    out = []
    for pair in (kv[:, :LANES], kv[:, LANES:]):
        swapped = pltpu.roll(pair, ATT_HEADDIM, 1)
        out.append([jnp.where(first, pair, swapped).astype(BF16), jnp.where(first, swapped, pair).astype(BF16)])
    return out[0], out[1]


def _split_heads(x_pair, first):
    return jnp.concatenate([jnp.where(first, x_pair, 0.0), jnp.where(first, 0.0, x_pair)], axis=0).astype(BF16)


def _att_probs_t(qm2, k_dup, t_ref, j, mask2, sink_ref):
    scale = ATT_HEADDIM ** -0.5
    bias2 = jnp.concatenate([t_ref[0, 2 * j], t_ref[0, 2 * j + 1]], axis=1)
    second = lax.broadcasted_iota(jnp.int32, (1, 2 * BLOCK), 1) >= BLOCK
    sink2 = jnp.where(second, sink_ref[0:1, 2 * j + 1:2 * j + 2], sink_ref[0:1, 2 * j:2 * j + 1])
    s_t = jnp.where(mask2, _dot_nt(k_dup, qm2) * scale + bias2, NEG)
    mx = jnp.maximum(jnp.max(s_t, axis=0, keepdims=True), sink2)
    p_t = jnp.exp(s_t - mx)
    p_s = jnp.exp(sink2 - mx)
    inv = 1.0 / (jnp.sum(p_t, axis=0, keepdims=True) + p_s)
    return p_t * inv, p_s * inv


def _att_specs(nb, rev):
    def nidx(i):
        return nb - 1 - i if rev else i

    kvb = ATT_Q // (2 * ATT_KV)
    q_s = pl.BlockSpec((BLOCK, ATT_Q), lambda i: (nidx(i), 0))
    cur = pl.BlockSpec((BLOCK, 2 * ATT_KV), lambda i: (nidx(i), kvb))
    prev = pl.BlockSpec((BLOCK, 2 * ATT_KV), lambda i: (jnp.maximum(nidx(i) - 1, 0), kvb))
    meta = pl.BlockSpec((BLOCK, 2 * ATT_KV), lambda i: (0, kvb))
    table = pl.BlockSpec((1, ATT_HEADS, N_KEYS, BLOCK), lambda i: (jnp.minimum(nidx(i), 2), 0, 0, 0))
    sink = pl.BlockSpec((1, LANES), lambda i: (0, 0))
    return q_s, cur, prev, meta, table, sink


def _attn_fwd(qkv, tables, sinks):
    n = qkv.shape[0]
    nb = n // BLOCK
    q_s, cur_s, prev_s, meta_s, t_s, sink_s = _att_specs(nb, False)

    def body(q_ref, cur_ref, prev_ref, meta_ref, t_ref, sink_ref, o_ref):
        blk = pl.program_id(0)
        mask_t = _att_mask_t(blk, 1)
        k_dup, v_dup = _att_kv(meta_ref, prev_ref, cur_ref)
        v_dup_t = [v.T for v in v_dup]
        first = lax.broadcasted_iota(jnp.int32, (BLOCK, LANES), 1) < ATT_HEADDIM
        top = lax.broadcasted_iota(jnp.int32, (LANES, BLOCK), 0) < ATT_HEADDIM
        scale = ATT_HEADDIM ** -0.5
        for j in range(ATT_HEADS // 2):
            kh = 2 * j // ATT_GQ
            tile = slice(LANES * j, LANES * (j + 1))
            q_p = q_ref[:, tile]
            res = []
            for half, h in enumerate((2 * j, 2 * j + 1)):
                qm = jnp.where(first if half == 0 else jnp.logical_not(first), q_p, 0.0).astype(BF16)
                sink = sink_ref[0:1, h:h + 1]
                s_t = jnp.where(mask_t, _dot_nt(k_dup[kh], qm) * scale + t_ref[0, h], NEG)
                mx = jnp.maximum(jnp.max(s_t, axis=0, keepdims=True), sink)
                p_t = jnp.exp(s_t - mx)
                inv = 1.0 / (jnp.sum(p_t, axis=0, keepdims=True) + jnp.exp(sink - mx))
                res.append(_dot(v_dup_t[kh], (p_t * inv).astype(BF16)))
            o_ref[:, tile] = jnp.where(top, res[0], res[1]).T.astype(BF16)

    return pl.pallas_call(
        body, name="attn_fwd", grid=(nb,),
        in_specs=[q_s, cur_s, prev_s, meta_s, t_s, sink_s],
        out_specs=q_s,
        out_shape=jax.ShapeDtypeStruct((n, ATT_Q), BF16),
        compiler_params=_cparams("parallel"))(qkv, qkv, qkv, qkv, tables, sinks)


def _attn_bwd(datt, qkv, tables, sinks):
    n = qkv.shape[0]
    nb = n // BLOCK
    q_s, cur_s, prev_s, meta_s, t_s, sink_s = _att_specs(nb, True)
    dqkv_s = pl.BlockSpec((BLOCK, ATT_Q + 2 * ATT_KV), lambda i: (nb - 1 - i, 0))
    scale = ATT_HEADDIM ** -0.5

    def body(do_ref, q_ref, cur_ref, prev_ref, meta_ref, t_ref, sink_ref,
             dqkv_ref, dt_ref, dsink_ref, carry_scr, meta_scr):
        step = pl.program_id(0)
        blk = nb - 1 - step
        mask2 = _att_mask_t(blk, 2)
        k_dup, v_dup = _att_kv(meta_ref, prev_ref, cur_ref)
        k_dup_t = [k.T for k in k_dup]

        @pl.when(step == 0)
        def _():
            carry_scr[...] = jnp.zeros_like(carry_scr)
            meta_scr[...] = jnp.zeros_like(meta_scr)
            dsink_ref[...] = jnp.zeros_like(dsink_ref)

        @pl.when((step == 0) | (blk <= 1))
        def _():
            dt_ref[...] = jnp.zeros_like(dt_ref)

        first = lax.broadcasted_iota(jnp.int32, (BLOCK, LANES), 1) < ATT_HEADDIM
        top = lax.broadcasted_iota(jnp.int32, (LANES, BLOCK), 0) < ATT_HEADDIM
        first_k = lax.broadcasted_iota(jnp.int32, (N_KEYS, LANES), 1) < ATT_HEADDIM
        dsink = jnp.zeros((1, LANES), F32)
        dk_acc = [None] * ATT_KV_HEADS
        dv_acc = [None] * ATT_KV_HEADS
        for j in range(ATT_HEADS // 2):
            kh = 2 * j // ATT_GQ
            tile = slice(LANES * j, LANES * (j + 1))
            qm2 = _split_heads(q_ref[:, tile], first)
            dom2 = _split_heads(do_ref[:, tile], first)
            p_t, p_s = _att_probs_t(qm2, k_dup[kh], t_ref, j, mask2, sink_ref)
            dp_t = _dot_nt(v_dup[kh], dom2)
            delta = jnp.sum(p_t * dp_t, axis=0, keepdims=True)
            ds_t = p_t * (dp_t - delta)
            sink_terms = p_s * delta
            for half in range(2):
                cols = slice(BLOCK * half, BLOCK * (half + 1))
                dsink = _lane_put(dsink, -jnp.sum(sink_terms[:, cols], axis=1, keepdims=True), 2 * j + half)
                dt_ref[0, 2 * j + half] += ds_t[:, cols]
            ds_tb = ds_t.astype(BF16)
            dq_t = _dot(k_dup_t[kh], ds_tb)
            dqkv_ref[:, tile] = (jnp.where(top, dq_t[:, :BLOCK], dq_t[:, BLOCK:]).T * scale).astype(BF16)
            dk_part, dv_part = _dot(ds_tb, qm2), _dot(p_t.astype(BF16), dom2)
            dk_acc[kh] = dk_part if dk_acc[kh] is None else dk_acc[kh] + dk_part
            dv_acc[kh] = dv_part if dv_acc[kh] is None else dv_acc[kh] + dv_part
        dsink_ref[...] += dsink
        folded = [a + pltpu.roll(a, ATT_HEADDIM, 1) for a in dk_acc + dv_acc]
        dkv = jnp.concatenate([jnp.where(first_k, folded[0], folded[1]) * scale,
                               jnp.where(first_k, folded[2], folded[3])], axis=1)
        meta_scr[...] += dkv[:BLOCK, :]
        own = dkv[2 * BLOCK:, :] + carry_scr[...]
        carry_scr[...] = dkv[BLOCK:2 * BLOCK, :]

        @pl.when(blk > 0)
        def _():
            dqkv_ref[:, ATT_Q:] = own.astype(BF16)

        @pl.when(blk == 0)
        def _():
            dqkv_ref[:, ATT_Q:] = (own + meta_scr[...]).astype(BF16)

    return pl.pallas_call(
        body, name="attn_bwd", grid=(nb,),
        in_specs=[q_s, q_s, cur_s, prev_s, meta_s, t_s, sink_s],
        out_specs=[dqkv_s, t_s, sink_s],
        out_shape=[jax.ShapeDtypeStruct((n, ATT_Q + 2 * ATT_KV), BF16),
                   jax.ShapeDtypeStruct((3, ATT_HEADS, N_KEYS, BLOCK), F32),
                   jax.ShapeDtypeStruct((1, LANES), F32)],
        scratch_shapes=[pltpu.VMEM((BLOCK, 2 * ATT_KV), F32), pltpu.VMEM((BLOCK, 2 * ATT_KV), F32)],
        compiler_params=_cparams("arbitrary"))(datt, qkv, qkv, qkv, qkv, tables, sinks)


def _merge_fwd(gates, y_ssd, y_att, gate_b):
    n = gates.shape[0]
    tm = _row_tile(n, 832)

    def body(gs_ref, ga_ref, ys_ref, ya_ref, gb_ref, o_ref):
        o_ref[...] = (jax.nn.sigmoid(gs_ref[...] + gb_ref[0:1, :]) * ys_ref[...]
                      + jax.nn.sigmoid(ga_ref[...] + gb_ref[1:2, :]) * ya_ref[...]).astype(BF16)

    row = pl.BlockSpec((tm, D_MODEL), lambda i: (i, 0))
    return pl.pallas_call(
        body, name="merge_fwd", grid=(n // tm,),
        in_specs=[row, pl.BlockSpec((tm, D_MODEL), lambda i: (i, 1)), row, row,
                  pl.BlockSpec((2, D_MODEL), lambda i: (0, 0))],
        out_specs=row, out_shape=jax.ShapeDtypeStruct((n, D_MODEL), BF16),
        compiler_params=_cparams("parallel"))(gates, gates, y_ssd, y_att, gate_b)


def _merge_bwd(dm, gates, y_ssd, y_att, gate_b):
    n = gates.shape[0]
    tm = _row_tile(n, 832)

    def body(dm_ref, gs_ref, ga_ref, ys_ref, ya_ref, gb_ref, dys_ref, dya_ref, dg_ref, dgb_ref):
        i = pl.program_id(0)
        dmv = dm_ref[...]
        ss = jax.nn.sigmoid(gs_ref[...] + gb_ref[0:1, :])
        sa = jax.nn.sigmoid(ga_ref[...] + gb_ref[1:2, :])
        dys_ref[...] = (dmv * ss).astype(BF16)
        dya_ref[...] = (dmv * sa).astype(BF16)
        dgs = dmv * ys_ref[...] * ss * (1.0 - ss)
        dga = dmv * ya_ref[...] * sa * (1.0 - sa)
        dg_ref[:, :D_MODEL] = dgs.astype(BF16)
        dg_ref[:, D_MODEL:] = dga.astype(BF16)
        part = jnp.concatenate([jnp.sum(dgs, axis=0, keepdims=True), jnp.sum(dga, axis=0, keepdims=True)], axis=0)

        @pl.when(i == 0)
        def _():
            dgb_ref[...] = part

        @pl.when(i > 0)
        def _():
            dgb_ref[...] += part

    row = pl.BlockSpec((tm, D_MODEL), lambda i: (i, 0))
    gb = pl.BlockSpec((2, D_MODEL), lambda i: (0, 0))
    return pl.pallas_call(
        body, name="merge_bwd", grid=(n // tm,),
        in_specs=[row, row, pl.BlockSpec((tm, D_MODEL), lambda i: (i, 1)), row, row, gb],
        out_specs=[row, row, pl.BlockSpec((tm, 2 * D_MODEL), lambda i: (i, 0)), gb],
        out_shape=[jax.ShapeDtypeStruct((n, D_MODEL), BF16), jax.ShapeDtypeStruct((n, D_MODEL), BF16),
                   jax.ShapeDtypeStruct((n, 2 * D_MODEL), BF16), jax.ShapeDtypeStruct((2, D_MODEL), F32)],
        compiler_params=_cparams("arbitrary"))(dm, gates, gates, y_ssd, y_att, gate_b)


def _col_move(srcs, outs, pieces, *, name):
    rows = srcs[0].shape[-2]
    tr = _row_tile(rows, 128)
    n_src = len(srcs)
    covered = [sum(p[6] for p in pieces if p[0] == o) for o in range(len(outs))]
    total = [int(np.prod(shp)) // rows for shp, _ in outs]

    def body(*refs):
        in_refs, out_refs = refs[:n_src], refs[n_src:]
        for o, ref in enumerate(out_refs):
            if covered[o] != total[o]:
                ref[...] = jnp.zeros_like(ref)
        for o, ol, oc, s, sl, sc, width in pieces:
            val = in_refs[s][:, sc:sc + width] if sl is None else in_refs[s][sl, :, sc:sc + width]
            val = val.astype(outs[o][1])
            if ol is None:
                out_refs[o][:, oc:oc + width] = val
            else:
                out_refs[o][ol, :, oc:oc + width] = val

    def spec(shape):
        if len(shape) == 2:
            return pl.BlockSpec((tr, shape[1]), lambda i: (i, 0))
        return pl.BlockSpec((shape[0], tr, shape[2]), lambda i: (0, i, 0))

    return pl.pallas_call(
        body, name=name, grid=(rows // tr,),
        in_specs=[spec(a.shape) for a in srcs], out_specs=[spec(shp) for shp, _ in outs],
        out_shape=[jax.ShapeDtypeStruct(shp, dt) for shp, dt in outs],
        compiler_params=_cparams("parallel"))(*srcs)


def _shard_pieces(seg_ranges, shard_w):
    out = []
    for seg, runs in enumerate(seg_ranges):
        for g0, width, s0 in runs:
            done = 0
            while done < width:
                dev, col = divmod(g0 + done, shard_w)
                take = min(width - done, shard_w - col)
                out.append((seg, s0 + done, dev, col, take))
                done += take
    return out


_CHIP_RELATIONS = [(1, 0, 0), (0, 1, 0), (1, 1, 0)]
N_CHIPS = 4


def _gather_two_level(arrays, *, name):
    n_arr = len(arrays)
    n_pair = 1 + 2 * len(_CHIP_RELATIONS)

    def body(*refs):
        ins, outs = refs[:n_arr], refs[n_arr:2 * n_arr]
        send_sems, recv_sems, local_sems = refs[2 * n_arr:]
        x, y, c = lax.axis_index("x"), lax.axis_index("y"), lax.axis_index("c")
        sibling = (x, y, 1 - c)
        chips = [(x ^ dx, y ^ dy) for dx, dy, _ in _CHIP_RELATIONS]

        def copy(a, k, block, to, src=None):
            slot = outs[a].at[2 * block[0] + block[1], block[2]]
            return pltpu.make_async_remote_copy(
                src_ref=slot if src is None else src, dst_ref=slot, send_sem=send_sems.at[a * n_pair + k],
                recv_sem=recv_sems.at[a * n_pair + k], device_id=to, device_id_type=MESH)

        sends, locals_ = [], []
        for a in range(n_arr):
            mine = pltpu.make_async_copy(ins[a], outs[a].at[2 * x + y, c], local_sems.at[a])
            mine.start()
            locals_.append(mine)
            first = [copy(a, 0, (x, y, c), sibling, src=ins[a])]
            first += [copy(a, 1 + j, (x, y, c), (*chip, c), src=ins[a]) for j, chip in enumerate(chips)]
            for cp in first:
                cp.start()
            sends += first
        for j, chip in enumerate(chips):
            for a in range(n_arr):
                copy(a, 1 + j, (*chip, c), (x, y, c)).wait_recv()
                passed = copy(a, 1 + len(chips) + j, (*chip, c), sibling)
                passed.start()
                sends.append(passed)
        for a in range(n_arr):
            copy(a, 0, (x, y, 1 - c), (x, y, c)).wait_recv()
            for j, chip in enumerate(chips):
                copy(a, 1 + len(chips) + j, (*chip, 1 - c), (x, y, c)).wait_recv()
        for cp in sends:
            cp.wait_send()
        for mine in locals_:
            mine.wait()

    any_spec = pl.BlockSpec(memory_space=pl.ANY)
    outs = pl.pallas_call(
        body, name=name, in_specs=[any_spec] * n_arr, out_specs=[any_spec] * n_arr,
        out_shape=[jax.ShapeDtypeStruct((N_CHIPS, 2) + a.shape, a.dtype) for a in arrays],
        scratch_shapes=[pltpu.SemaphoreType.DMA((n_arr * n_pair,)), pltpu.SemaphoreType.DMA((n_arr * n_pair,)),
                        pltpu.SemaphoreType.DMA((n_arr,))],
    )(*arrays)
    return [o.reshape((N_DEV,) + a.shape) for o, a in zip(outs, arrays)]


def _sibling_exchange(arrays, scatter, *, name):
    n_arr = len(arrays)

    def body(*refs):
        ins, outs = refs[:n_arr], refs[n_arr:2 * n_arr]
        send_sems, recv_sems = refs[2 * n_arr:]
        x, y, c = lax.axis_index("x"), lax.axis_index("y"), lax.axis_index("c")
        copies = []
        for a in range(n_arr):
            for q in range(N_CHIPS if scatter[a] else 1):
                src = ins[a].at[2 * q + 1 - c] if scatter[a] else ins[a]
                dst = outs[a].at[q] if scatter[a] else outs[a]
                cp = pltpu.make_async_remote_copy(
                    src_ref=src, dst_ref=dst, send_sem=send_sems.at[a * N_CHIPS + q],
                    recv_sem=recv_sems.at[a * N_CHIPS + q], device_id=(x, y, 1 - c), device_id_type=MESH)
                cp.start()
                copies.append(cp)
        for cp in copies:
            cp.wait_send()
        for cp in copies:
            cp.wait_recv()

    any_spec = pl.BlockSpec(memory_space=pl.ANY)
    return pl.pallas_call(
        body, name=name, in_specs=[any_spec] * n_arr, out_specs=[any_spec] * n_arr,
        out_shape=[jax.ShapeDtypeStruct(((N_CHIPS,) + a.shape[1:]) if s else a.shape, a.dtype)
                   for a, s in zip(arrays, scatter)],
        scratch_shapes=[pltpu.SemaphoreType.DMA((n_arr * N_CHIPS,)), pltpu.SemaphoreType.DMA((n_arr * N_CHIPS,))],
    )(*arrays)


def _pair_sum(mine, sib, *, name, out_dtype):
    _, rows, cols = mine.shape
    tr = _row_tile(rows, 128)

    def body(m_ref, s_ref, o_ref):
        c = lax.axis_index("c")
        o_ref[0] = (m_ref[0, c] + s_ref[0]).astype(out_dtype)

    return pl.pallas_call(
        body, name=name, grid=(N_CHIPS, rows // tr),
        in_specs=[pl.BlockSpec((1, 2, tr, cols), lambda q, i: (q, 0, i, 0)),
                  pl.BlockSpec((1, tr, cols), lambda q, i: (q, i, 0))],
        out_specs=pl.BlockSpec((1, tr, cols), lambda q, i: (q, i, 0)),
        out_shape=jax.ShapeDtypeStruct((N_CHIPS, rows, cols), out_dtype),
        compiler_params=_cparams("parallel", "parallel"))(mine.reshape(N_CHIPS, 2, rows, cols), sib)


def _add(a, b, *, name):
    rows, cols = a.shape
    tr = _row_tile(rows, 256)

    def body(a_ref, b_ref, o_ref):
        o_ref[...] = a_ref[...] + b_ref[...]

    blk = pl.BlockSpec((tr, cols), lambda i: (i, 0))
    return pl.pallas_call(body, name=name, grid=(rows // tr,), in_specs=[blk, blk], out_specs=blk,
                          out_shape=jax.ShapeDtypeStruct(a.shape, a.dtype), compiler_params=_cparams("parallel"))(a, b)


def _chip_exchange(arrays, scatter, *, name):
    n_arr = len(arrays)
    n_rel = len(_CHIP_RELATIONS)

    def body(*refs):
        ins, outs = refs[:n_arr], refs[n_arr:2 * n_arr]
        send_sems, recv_sems, local_sems = refs[2 * n_arr:]
        x, y, c = lax.axis_index("x"), lax.axis_index("y"), lax.axis_index("c")
        me = 2 * x + y
        copies = []
        for a in range(n_arr):
            src = ins[a].at[me] if scatter[a] else ins[a]
            local = pltpu.make_async_copy(src, outs[a].at[me], local_sems.at[a])
            local.start()
            copies.append(local)
        remote = []
        for k, (dx, dy, dc) in enumerate(_CHIP_RELATIONS):
            px, py, pc = x ^ dx, y ^ dy, c ^ dc
            peer = 2 * px + py
            for a in range(n_arr):
                src = ins[a].at[peer] if scatter[a] else ins[a]
                cp = pltpu.make_async_remote_copy(
                    src_ref=src, dst_ref=outs[a].at[me], send_sem=send_sems.at[a * n_rel + k],
                    recv_sem=recv_sems.at[a * n_rel + k], device_id=(px, py, pc), device_id_type=MESH)
                cp.start()
                remote.append((cp, a, k, peer))
        for cp, a, k, peer in remote:
            cp.wait_send()
        for cp, a, k, peer in remote:
            src = ins[a].at[peer] if scatter[a] else ins[a]
            pltpu.make_async_remote_copy(
                src_ref=src, dst_ref=outs[a].at[peer], send_sem=send_sems.at[a * n_rel + k],
                recv_sem=recv_sems.at[a * n_rel + k], device_id=(x, y, c), device_id_type=MESH).wait_recv()
        for local in copies:
            local.wait()

    out_shape = [jax.ShapeDtypeStruct((N_CHIPS,) + (a.shape[1:] if s else a.shape), a.dtype)
                 for a, s in zip(arrays, scatter)]
    any_spec = pl.BlockSpec(memory_space=pl.ANY)
    return pl.pallas_call(
        body, name=name, in_specs=[any_spec] * n_arr, out_specs=[any_spec] * n_arr, out_shape=out_shape,
        scratch_shapes=[pltpu.SemaphoreType.DMA((n_arr * n_rel,)), pltpu.SemaphoreType.DMA((n_arr * n_rel,)),
                        pltpu.SemaphoreType.DMA((n_arr,))],
    )(*arrays)


def _adamw(w, gslots, m, v, *, name):
    rows, cols = w.shape
    n_slots = gslots.shape[0]
    tr = _row_tile(rows, 128) if rows % 16 == 0 else rows

    def body(w_ref, g_ref, m_ref, v_ref, go_ref, d_ref, mo_ref, vo_ref):
        g = g_ref[0].astype(F32)
        for s in range(1, n_slots):
            g = g + g_ref[s].astype(F32)
        mn = ADAM_B1 * m_ref[...] + (1.0 - ADAM_B1) * g
        vn = ADAM_B2 * v_ref[...] + (1.0 - ADAM_B2) * (g * g)
        go_ref[...] = g
        mo_ref[...] = mn
        vo_ref[...] = vn
        m_hat = mn / (1.0 - ADAM_B1 ** ADAM_STEP)
        v_hat = vn / (1.0 - ADAM_B2 ** ADAM_STEP)
        d_ref[...] = -ADAM_LR * (m_hat / (jnp.sqrt(v_hat) + ADAM_EPS) + ADAM_WD * w_ref[...])

    blk = pl.BlockSpec((tr, cols), lambda i: (i, 0))
    shp = jax.ShapeDtypeStruct((rows, cols), F32)
    return pl.pallas_call(
        body, name=name, grid=(rows // tr,),
        in_specs=[blk, pl.BlockSpec((n_slots, tr, cols), lambda i: (0, i, 0)), blk, blk],
        out_specs=[blk] * 4, out_shape=[shp] * 4,
        compiler_params=_cparams("parallel"))(w, gslots, m, v)


_BIG = ("w_in", "w_ssd_branch", "w_attn_branch", "w_out", "w_ffn_in", "w_ffn_out")
_SMALL_SHARDED = ("meta_tokens", "ssd_conv_w", "gate_b", "ffn_conv_w")
_SMALL_REPLICATED = ("norm_mix_w", "ssd_conv_b", "ssd_dt_bias", "ssd_a_log", "ssd_d", "ssd_norm_w", "attn_sinks",
                     "rel_bias", "norm_ffn_w", "ffn_conv_b", "norm_final_w")
_WEIGHTS = ("meta_tokens", "norm_mix_w", "w_in", "ssd_conv_w", "ssd_conv_b", "ssd_dt_bias", "ssd_a_log", "ssd_d",
            "ssd_norm_w", "w_ssd_branch", "w_attn_branch", "attn_sinks", "rel_bias", "gate_b", "w_out", "norm_ffn_w",
            "w_ffn_in", "ffn_conv_w", "ffn_conv_b", "w_ffn_out", "norm_final_w")
_ROW_SHARDED = ("w_ssd_branch", "w_attn_branch", "w_out", "w_ffn_out")
_COL_SHARDED = ("w_in", "w_ffn_in", "meta_tokens", "ssd_conv_w", "gate_b", "ffn_conv_w")
_IN_SEGS = (("z", SSD_INNER), ("xbc", SSD_XBC), ("dt", SSD_HEADS), ("qkv", ATT_Q + 2 * ATT_KV), ("g", 2 * D_MODEL))


def _pack_rows(flat_parts, width, row_mult):
    flat = jnp.concatenate([p.reshape(-1) for p in flat_parts])
    pad = (-flat.shape[0]) % (width * row_mult)
    if pad:
        flat = jnp.concatenate([flat, jnp.zeros((pad,), flat.dtype)])
    return flat.reshape(-1, width)


def _unpack(flat, shapes):
    out, off = [], 0
    for shp in shapes:
        size = int(np.prod(shp))
        out.append(flat[off:off + size].reshape(shp))
        off += size
    return out


def _gather_full(stack, name, shard_shape):
    if name in _COL_SHARDED:
        return jnp.transpose(stack, (1, 0, 2)).reshape(shard_shape[0], N_DEV * shard_shape[1])
    return stack.reshape(N_DEV * shard_shape[0], shard_shape[1])


_IN_SEG_W = {"z": SSD_INNER, "xbc": SSD_XBC, "dt": DT_W, "qkv": ATT_Q + 2 * ATT_KV, "g": 2 * D_MODEL}
_IN_SHARD_W = (SSD_INNER + SSD_XBC + SSD_HEADS + ATT_Q + 2 * ATT_KV + 2 * D_MODEL) // N_DEV
_FFN_SHARD_W = 2 * D_FF // N_DEV


def _in_seg_runs():
    runs, off = [], 0
    for nm, width in _IN_SEGS:
        if nm == "dt":
            runs.append([(off + SSD_HPG * g, SSD_HPG, LANES * g) for g in range(SSD_GROUPS)])
        else:
            runs.append([(off, width, 0)])
        off += width
    return runs


def _w_in_to_segments(stack):
    pieces = [(seg, None, scol, 0, dev, col, w) for seg, scol, dev, col, w in _shard_pieces(_in_seg_runs(), _IN_SHARD_W)]
    outs = [((D_MODEL, _IN_SEG_W[nm]), stack.dtype) for nm, _ in _IN_SEGS]
    return dict(zip([nm for nm, _ in _IN_SEGS], _col_move([stack], outs, pieces, name="w_in_segments")))


def _segments_to_w_in_shards(seg_grads):
    pieces = [(0, dev, col, seg, None, scol, w) for seg, scol, dev, col, w in _shard_pieces(_in_seg_runs(), _IN_SHARD_W)]
    return _col_move(seg_grads, [((N_DEV, D_MODEL, _IN_SHARD_W), F32)], pieces, name="g_w_in_shards")[0]


def _ffn_in_from_shards(stack):
    pieces = [(0, None, scol, 0, dev, col, w)
              for _, scol, dev, col, w in _shard_pieces([[(0, 2 * D_FF, 0)]], _FFN_SHARD_W)]
    return _col_move([stack], [((D_MODEL, 2 * D_FF), stack.dtype)], pieces, name="w_ffn_in_full")[0]


def _ffn_in_to_shards(g_up, g_gate):
    pieces = [(0, dev, col, seg, None, scol, w)
              for seg, scol, dev, col, w in _shard_pieces([[(0, D_FF, 0)], [(D_FF, D_FF, 0)]], _FFN_SHARD_W)]
    return _col_move([g_up, g_gate], [((N_DEV, D_MODEL, _FFN_SHARD_W), F32)], pieces, name="g_w_ffn_in_shards")[0]


def _dt_spread(w_dt):
    k = w_dt.shape[0]
    w4 = w_dt.reshape(k, SSD_GROUPS, SSD_HPG)
    return jnp.pad(w4, ((0, 0), (0, 0), (0, LANES - SSD_HPG))).reshape(k, DT_W)


def _dt_gather(w_wide):
    k = w_wide.shape[0]
    return w_wide.reshape(k, SSD_GROUPS, LANES)[:, :, :SSD_HPG].reshape(k, SSD_HEADS)


def _local_step(x, target, w):
    h0 = jnp.concatenate([jnp.zeros((PAD, D_MODEL), F32), w["meta_tokens"], x], axis=0)
    segs = w["in_segs"]
    w_ffn_up, w_ffn_gate = w["w_ffn_in"][:, :D_FF], w["w_ffn_in"][:, D_FF:]

    dtb = _dt_spread(w["ssd_dt_bias"])
    alog = _dt_spread(w["ssd_a_log"])
    dskip_w = jnp.repeat(w["ssd_d"], SSD_HEADDIM, axis=1)
    sinks = jnp.pad(w["attn_sinks"], ((0, 0), (0, LANES - ATT_HEADS)))
    onehot_t = _onehot_t()
    tables = jnp.transpose(_bias_tables(w["rel_bias"].T, onehot_t).reshape(ATT_HEADS, 3, N_KEYS, BLOCK), (1, 0, 2, 3))

    u = _rms_fwd(h0, w["norm_mix_w"], name="rms_mix_fwd")
    z = _mm(u, segs["z"], name="in_z")
    xbc = _mm(u, segs["xbc"], name="in_xbc")
    dt_raw = _mm(u, segs["dt"], name="in_dt")
    qkv = _mm(u, segs["qkv"], name="in_qkv")
    gates = _mm(u, segs["g"], name="in_g")
    pre = _conv_fwd(xbc, w["ssd_conv_w"], w["ssd_conv_b"], name="ssd_conv_fwd")
    y, yn, hsave = _ssd_fwd(pre, dt_raw, z, dtb, alog, dskip_w, w["ssd_norm_w"])
    y_ssd = _mm(yn, w["w_ssd_branch"], name="ssd_out")
    att = _attn_fwd(qkv, tables, sinks)
    y_att = _mm(att, w["w_attn_branch"], name="att_out")
    merged = _merge_fwd(gates, y_ssd, y_att, w["gate_b"])
    h1 = _mm(merged, w["w_out"], c=h0, mask=True, name="mix_out")
    u2 = _rms_fwd(h1, w["norm_ffn_w"], name="rms_ffn_fwd")
    hid_raw = _mm(u2, w["w_ffn_in"], name="ffn_in")
    hid_up, hid_gate, act = _ffn_act_fwd(hid_raw, w["ffn_conv_w"], w["ffn_conv_b"])
    h2 = _mm(act, w["w_ffn_out"], c=h1, mask=True, name="ffn_out")
    dh2, loss_row, g_norm_final = _final_loss(h2, w["norm_final_w"], target)

    grads = {"norm_final_w": g_norm_final}
    dact = _mm(dh2, w["w_ffn_out"], tb=True, mask=True, name="d_act")
    grads["w_ffn_out"] = _mm(act, dh2, ta=True, mask=True, name="g_w_ffn_out")
    dx_up, dx_gate, dcw_up, dcw_gate, dcb_up, dcb_gate = _ffn_act_bwd(dact, hid_up, hid_gate, hid_raw, w["ffn_conv_w"])
    grads["ffn_conv_w"] = jnp.concatenate([dcw_up, dcw_gate], axis=1)
    grads["ffn_conv_b"] = jnp.concatenate([dcb_up, dcb_gate], axis=1)
    du2 = _mm(dx_up, w_ffn_up, tb=True, name="d_u2_up")
    du2 = _mm(dx_gate, w_ffn_gate, tb=True, c=du2, name="d_u2_gate")
    grads["w_ffn_in"] = (_mm(u2, dx_up, ta=True, name="g_w_ffn_up"), _mm(u2, dx_gate, ta=True, name="g_w_ffn_gate"))
    dh1, grads["norm_ffn_w"] = _rms_bwd(h1, w["norm_ffn_w"], du2, dh2, name="rms_ffn_bwd")

    dmerged = _mm(dh1, w["w_out"], tb=True, mask=True, name="d_merged")
    grads["w_out"] = _mm(merged, dh1, ta=True, mask=True, name="g_w_out")
    dy_ssd, dy_att, dgates, grads["gate_b"] = _merge_bwd(dmerged, gates, y_ssd, y_att, w["gate_b"])
    dyn = _mm(dy_ssd, w["w_ssd_branch"], tb=True, name="d_yn")
    grads["w_ssd_branch"] = _mm(yn, dy_ssd, ta=True, name="g_w_ssd")
    datt = _mm(dy_att, w["w_attn_branch"], tb=True, name="d_att")
    grads["w_attn_branch"] = _mm(att, dy_att, ta=True, name="g_w_att")
    dz, dpxs, dpb, dpc, ddt, grads["ssd_norm_w"], g_dtb, g_alog, g_dskip = _ssd_bwd(
        dyn, y, z, pre, dt_raw, hsave, dtb, alog, dskip_w, w["ssd_norm_w"])
    grads["ssd_dt_bias"] = _dt_gather(g_dtb)
    grads["ssd_a_log"] = _dt_gather(g_alog)
    grads["ssd_d"] = _dt_gather(g_dskip)
    dpre = jnp.concatenate([dpxs, dpb, dpc], axis=1)
    dxbc, grads["ssd_conv_w"], grads["ssd_conv_b"] = _conv_bwd(dpre, xbc, w["ssd_conv_w"], name="ssd_conv_bwd")
    dqkv, d_tables, d_sinks = _attn_bwd(datt, qkv, tables, sinks)
    grads["attn_sinks"] = d_sinks[:, :ATT_HEADS]
    dtab = jnp.transpose(d_tables, (1, 0, 2, 3)).reshape(ATT_HEADS, NT_ALL)
    grads["rel_bias"] = _bias_grad(dtab, onehot_t).T
    dsegs = {"z": dz, "xbc": dxbc, "dt": ddt, "qkv": dqkv, "g": dgates}
    du, g_in = None, []
    for nm, _ in _IN_SEGS:
        du = _mm(dsegs[nm], segs[nm], tb=True, c=du, name="d_u_" + nm)
        g_in.append(_mm(u, dsegs[nm], ta=True, name="g_w_in_" + nm))
    grads["in_segs"] = g_in
    dh0, grads["norm_mix_w"] = _rms_bwd(h0, w["norm_mix_w"], du, dh1, name="rms_mix_bwd")
    grads["meta_tokens"] = dh0[PAD:BLOCK]
    return loss_row[0, 0], dh0[BLOCK:], grads


def kernel(x, meta_tokens, norm_mix_w, w_in, ssd_conv_w, ssd_conv_b, ssd_dt_bias, ssd_a_log, ssd_d, ssd_norm_w, w_ssd_branch, w_attn_branch, attn_sinks, rel_bias, gate_b, w_out, norm_ffn_w, w_ffn_in, ffn_conv_w, ffn_conv_b, w_ffn_out, norm_final_w, loss_target, m_meta_tokens, m_norm_mix_w, m_w_in, m_ssd_conv_w, m_ssd_conv_b, m_ssd_dt_bias, m_ssd_a_log, m_ssd_d, m_ssd_norm_w, m_w_ssd_branch, m_w_attn_branch, m_attn_sinks, m_rel_bias, m_gate_b, m_w_out, m_norm_ffn_w, m_w_ffn_in, m_ffn_conv_w, m_ffn_conv_b, m_w_ffn_out, m_norm_final_w, v_meta_tokens, v_norm_mix_w, v_w_in, v_ssd_conv_w, v_ssd_conv_b, v_ssd_dt_bias, v_ssd_a_log, v_ssd_d, v_ssd_norm_w, v_w_ssd_branch, v_w_attn_branch, v_attn_sinks, v_rel_bias, v_gate_b, v_w_out, v_norm_ffn_w, v_w_ffn_in, v_ffn_conv_w, v_ffn_conv_b, v_w_ffn_out, v_norm_final_w):
    shard = dict(meta_tokens=meta_tokens, norm_mix_w=norm_mix_w, w_in=w_in, ssd_conv_w=ssd_conv_w,
                 ssd_conv_b=ssd_conv_b, ssd_dt_bias=ssd_dt_bias, ssd_a_log=ssd_a_log, ssd_d=ssd_d,
                 ssd_norm_w=ssd_norm_w, w_ssd_branch=w_ssd_branch, w_attn_branch=w_attn_branch,
                 attn_sinks=attn_sinks, rel_bias=rel_bias, gate_b=gate_b, w_out=w_out, norm_ffn_w=norm_ffn_w,
                 w_ffn_in=w_ffn_in, ffn_conv_w=ffn_conv_w, ffn_conv_b=ffn_conv_b, w_ffn_out=w_ffn_out,
                 norm_final_w=norm_final_w)
    mom_m = dict(zip(_WEIGHTS, (m_meta_tokens, m_norm_mix_w, m_w_in, m_ssd_conv_w, m_ssd_conv_b, m_ssd_dt_bias,
                                m_ssd_a_log, m_ssd_d, m_ssd_norm_w, m_w_ssd_branch, m_w_attn_branch, m_attn_sinks,
                                m_rel_bias, m_gate_b, m_w_out, m_norm_ffn_w, m_w_ffn_in, m_ffn_conv_w, m_ffn_conv_b,
                                m_w_ffn_out, m_norm_final_w)))
    mom_v = dict(zip(_WEIGHTS, (v_meta_tokens, v_norm_mix_w, v_w_in, v_ssd_conv_w, v_ssd_conv_b, v_ssd_dt_bias,
                                v_ssd_a_log, v_ssd_d, v_ssd_norm_w, v_w_ssd_branch, v_w_attn_branch, v_attn_sinks,
                                v_rel_bias, v_gate_b, v_w_out, v_norm_ffn_w, v_w_ffn_in, v_ffn_conv_w, v_ffn_conv_b,
                                v_w_ffn_out, v_norm_final_w)))
    orig_shape = {k: a.shape for k, a in shard.items()}
    two_d = {k: a.reshape(a.shape[-2:]) if a.ndim >= 2 else a.reshape(1, -1) for k, a in shard.items()}
    shape2 = {k: a.shape for k, a in two_d.items()}

    def as2d(tree):
        return {k: tree[k].reshape(shape2[k]) for k in _WEIGHTS}

    mom_m, mom_v = as2d(mom_m), as2d(mom_v)

    def row_pack(tree):
        return jnp.concatenate([tree[k] for k in _ROW_SHARDED], axis=0)

    small_pack = _pack_rows([two_d[k] for k in _SMALL_SHARDED], LANES, SMALL_ROW_MULT)
    w_in_all, w_ffn_in_all, rows_all, small_all = _gather_two_level(
        [two_d["w_in"].astype(BF16), two_d["w_ffn_in"].astype(BF16), row_pack(two_d).astype(BF16), small_pack],
        name="gather_weights")
    full = {k: two_d[k] for k in _SMALL_REPLICATED}
    full["in_segs"] = _w_in_to_segments(w_in_all)
    full["w_ffn_in"] = _ffn_in_from_shards(w_ffn_in_all)
    off = 0
    for k in _ROW_SHARDED:
        r = shape2[k][0]
        full[k] = rows_all[:, off:off + r].reshape(N_DEV * r, D_MODEL)
        off += r
    small_flat = small_all.reshape(N_DEV, -1)
    off = 0
    for k in _SMALL_SHARDED:
        size = int(np.prod(shape2[k]))
        full[k] = _gather_full(small_flat[:, off:off + size].reshape((N_DEV,) + shape2[k]), k, shape2[k])
        off += size

    loss_local, grad_x, grads = _local_step(x[0], loss_target[0], full)

    rows_send = jnp.concatenate([grads[k].reshape(N_DEV, shape2[k][0], D_MODEL) for k in _ROW_SHARDED], axis=1)
    small_names = _SMALL_SHARDED + _SMALL_REPLICATED
    small_send = _pack_rows([grads[k] for k in small_names] + [loss_local.reshape(1)], LANES, SMALL_ROW_MULT)
    big_send = [_segments_to_w_in_shards(grads["in_segs"]), _ffn_in_to_shards(*grads["w_ffn_in"]), rows_send]
    from_sib = _sibling_exchange(big_send + [small_send], [True, True, True, False], name="grads_to_sibling")
    parts = [_pair_sum(mine, sib, name="pair_sum_" + nm, out_dtype=BF16)
             for nm, mine, sib in zip(("w_in", "w_ffn_in", "rows"), big_send, from_sib)]
    parts.append(_add(small_send, from_sib[3], name="pair_sum_small"))
    in_recv, ffn_recv, rows_recv, small_recv = _chip_exchange(parts, [True, True, True, False], name="exchange_grads")

    big = {"w_in": _adamw(two_d["w_in"], in_recv, mom_m["w_in"], mom_v["w_in"], name="adamw_w_in"),
           "w_ffn_in": _adamw(two_d["w_ffn_in"], ffn_recv, mom_m["w_ffn_in"], mom_v["w_ffn_in"], name="adamw_w_ffn_in")}
    rows_out = _adamw(row_pack(two_d), rows_recv, row_pack(mom_m), row_pack(mom_v), name="adamw_rows")
    off = 0
    for k in _ROW_SHARDED:
        r = shape2[k][0]
        big[k] = [a[off:off + r] for a in rows_out]
        off += r
    me =4 * lax.axis_index("x") + 2 * lax.axis_index("y") + lax.axis_index("c")
    small_full_shapes = [grads[k].shape for k in small_names]
    n_small = sum(int(np.prod(s)) for s in small_full_shapes)

    def packed_small(tree):
        parts = []
        for k in small_names:
            a = tree[k]
            if k in _SMALL_SHARDED:
                fullw = jnp.zeros(grads[k].shape, F32)
                a = lax.dynamic_update_slice(fullw, a, (0, me * a.shape[1]))
            parts.append(a)
        return _pack_rows(parts + [jnp.zeros((1,), F32)], LANES, SMALL_ROW_MULT)

    g_small, d_small, m_small, v_small = _adamw(packed_small(two_d), small_recv, packed_small(mom_m),
                                                packed_small(mom_v), name="adamw_small")

    def unpack_all(which, small):
        out = {k: big[k][which] for k in _BIG}
        flat = small.reshape(-1)
        for k, a in zip(small_names, _unpack(flat, small_full_shapes)):
            if k in _SMALL_SHARDED:
                a = lax.dynamic_slice(a, (0, me * shape2[k][1]), shape2[k])
            out[k] = a
        return out, flat[n_small]

    g_all, loss = unpack_all(0, g_small)
    d_all, _ = unpack_all(1, d_small)
    m_all, _ = unpack_all(2, m_small)
    v_all, _ = unpack_all(3, v_small)

    def final(tree):
        return [tree[k].reshape(orig_shape[k]) for k in _WEIGHTS]

    return (loss, grad_x[None], *final(g_all), *final(d_all), *final(m_all), *final(v_all))
```

```python
import functools
import math

import numpy as np
import jax
import jax.numpy as jnp
from jax import lax
from jax.experimental import pallas as pl
from jax.experimental.pallas import tpu as pltpu

F32 = jnp.float32
BF16 = jnp.bfloat16
HIGHEST = lax.Precision.HIGHEST

D_MODEL = 1024
N_META = 16
BLOCK = 128
PAD = BLOCK - N_META
EPS = 1e-6
NEG = -1e30
SSD_INNER = 2 * D_MODEL
SSD_HEADDIM = 64
SSD_HEADS = SSD_INNER // SSD_HEADDIM
SSD_GROUPS = 4
SSD_HPG = SSD_HEADS // SSD_GROUPS
SSD_STATE = 128
SSD_CONV = 4
SSD_GW = SSD_HPG * SSD_HEADDIM
SSD_BC = SSD_GROUPS * SSD_STATE
SSD_XBC = SSD_INNER + 2 * SSD_BC
ATT_HEADS = 16
ATT_KV_HEADS = 2
ATT_HEADDIM = 64
ATT_GQ = ATT_HEADS // ATT_KV_HEADS
ATT_Q = ATT_HEADS * ATT_HEADDIM
ATT_KV = ATT_KV_HEADS * ATT_HEADDIM
REL_BUCKETS = 32
REL_MAX_DIST = 128
D_FF = 2816
FFN_CONV = 3
ADAM_LR = 0.001
ADAM_B1 = 0.9
ADAM_B2 = 0.999
ADAM_EPS = 1e-08
ADAM_WD = 0.01
ADAM_STEP = 10

N_DEV = 8
LANES = 128
SUBLANES = 8
DT_W = SSD_GROUPS * LANES
VMEM_LIMIT_BYTES = 56 * 1024 * 1024
MESH = pl.DeviceIdType.MESH

SMALL_ROW_MULT = 16

N_KEYS = 3 * BLOCK
NT_ALL = 3 * N_KEYS * BLOCK
NT_TILE = 8192


def _cparams(*sem):
    return pltpu.CompilerParams(dimension_semantics=sem, vmem_limit_bytes=VMEM_LIMIT_BYTES)


def _row_tile(n, cap):
    best = None
    for t in range(16, min(n, cap) + 1, 16):
        if n % t == 0:
            best = t
    return best or n


def _col_tile(n, cap):
    for t in (1408, 1280, 1024, 768, 640, 512, 384, 256, 128):
        if t <= cap and n % t == 0:
            return t
    return n


def _sigmoid(x):
    return 0.5 * jnp.tanh(0.5 * x) + 0.5


def _silu(x):
    return x * _sigmoid(x)


def _softplus(x):
    return jnp.maximum(x, 0.0) + jnp.log(1.0 + jnp.exp(-jnp.abs(x)))


def _dot_nt(a, b):
    return lax.dot_general(a, b, (((1,), (1,)), ((), ())), preferred_element_type=F32)


def _dot_tn(a, b):
    return lax.dot_general(a, b, (((0,), (0,)), ((), ())), preferred_element_type=F32)


def _dot(a, b):
    return jnp.dot(a, b, preferred_element_type=F32)


def _bf16_terms(x, terms):
    out, rest = [], x
    for _ in range(terms):
        part = rest.astype(BF16)
        out.append(part)
        rest = rest - part.astype(F32)
    return out


def _dot_sel(x, sel, terms=3):
    return sum(_dot(part, sel) for part in _bf16_terms(x, terms))


def _sel_dot(sel, x, terms=3):
    return sum(_dot(sel, part) for part in _bf16_terms(x, terms))


def _sum_all(x):
    return jnp.sum(jnp.sum(x, axis=1, keepdims=True), axis=0, keepdims=True)


MM_ROW_CAPS = (1664, 832, 416)
MM_COL_CAP = 1408
MM_VMEM_BUDGET = 44 * 1024 * 1024


def _mm_tiles(rows, cols, vmem_bytes):
    col_cands = [t for t in (1408, 1280, 1024, 768, 640, 512, 384, 256, 128) if cols % t == 0]
    if cols <= 2 * MM_COL_CAP:
        col_cands.append(cols)
    best = None
    for cap in MM_ROW_CAPS:
        tr = _row_tile(rows, cap)
        for tc in col_cands:
            if vmem_bytes(tr, tc) <= MM_VMEM_BUDGET and (best is None or tr * tc > best[0] * best[1]):
                best = (tr, tc)
    assert best is not None, (rows, cols)
    return best


def _mm(a, b, *, name, ta=False, tb=False, c=None, mask=False, out_dtype=F32):
    if not ta:
        m, k = a.shape
        n = b.shape[0] if tb else b.shape[1]
        tm, tn = _mm_tiles(m, n, lambda t_m, t_n: 2 * (t_m * k * a.dtype.itemsize + k * t_n * b.dtype.itemsize
                                                       + t_m * t_n * (jnp.dtype(out_dtype).itemsize
                                                                      + (0 if c is None else c.dtype.itemsize)))
                           + 4 * t_m * t_n)

        def body(*refs):
            if c is None:
                a_ref, b_ref, o_ref = refs
            else:
                a_ref, b_ref, c_ref, o_ref = refs
            acc = (_dot_nt if tb else _dot)(a_ref[...].astype(BF16), b_ref[...].astype(BF16))
            if mask:
                row = pl.program_id(0) * tm + lax.broadcasted_iota(jnp.int32, (tm, 1), 0)
                acc = jnp.where(row >= PAD, acc, 0.0)
            if c is not None:
                acc = acc + c_ref[...]
            o_ref[...] = acc.astype(out_dtype)

        b_spec = pl.BlockSpec((tn, k), lambda i, j: (j, 0)) if tb else pl.BlockSpec((k, tn), lambda i, j: (0, j))
        in_specs = [pl.BlockSpec((tm, k), lambda i, j: (i, 0)), b_spec]
        args = [a, b]
        if c is not None:
            in_specs.append(pl.BlockSpec((tm, tn), lambda i, j: (i, j)))
            args.append(c)
        return pl.pallas_call(
            body, name=name, grid=(m // tm, n // tn), in_specs=in_specs,
            out_specs=pl.BlockSpec((tm, tn), lambda i, j: (i, j)),
            out_shape=jax.ShapeDtypeStruct((m, n), out_dtype),
            compiler_params=_cparams("parallel", "parallel"))(*args)

    kc, m = a.shape
    n = b.shape[1]
    tm = _col_tile(m, MM_COL_CAP)
    tk, tn = _mm_tiles(kc, n, lambda t_k, t_n: 2 * (t_k * tm * a.dtype.itemsize + t_k * t_n * b.dtype.itemsize
                                                    + 4 * tm * t_n) + 4 * tm * t_n)

    def body_t(a_ref, b_ref, o_ref):
        kk = pl.program_id(2)
        bb = b_ref[...]
        if mask:
            row = kk * tk + lax.broadcasted_iota(jnp.int32, (tk, 1), 0)
            bb = jnp.where(row >= PAD, bb, jnp.zeros_like(bb))
        p = _dot_tn(a_ref[...].astype(BF16), bb.astype(BF16))

        @pl.when(kk == 0)
        def _():
            o_ref[...] = p

        @pl.when(kk > 0)
        def _():
            o_ref[...] += p

    return pl.pallas_call(
        body_t, name=name, grid=(m // tm, n // tn, kc // tk),
        in_specs=[pl.BlockSpec((tk, tm), lambda i, j, kk: (kk, i)), pl.BlockSpec((tk, tn), lambda i, j, kk: (kk, j))],
        out_specs=pl.BlockSpec((tm, tn), lambda i, j, kk: (i, j)),
        out_shape=jax.ShapeDtypeStruct((m, n), F32),
        compiler_params=_cparams("parallel", "parallel", "arbitrary"))(a, b)


def _rms_fwd(h, w, *, name):
    n, d = h.shape
    tm = _row_tile(n, 832)

    def body(h_ref, w_ref, o_ref):
        x = h_ref[...]
        r = lax.rsqrt(jnp.mean(x * x, axis=-1, keepdims=True) + EPS)
        o_ref[...] = (x * r * w_ref[...]).astype(BF16)

    return pl.pallas_call(
        body, name=name, grid=(n // tm,),
        in_specs=[pl.BlockSpec((tm, d), lambda i: (i, 0)), pl.BlockSpec((1, d), lambda i: (0, 0))],
        out_specs=pl.BlockSpec((tm, d), lambda i: (i, 0)),
        out_shape=jax.ShapeDtypeStruct((n, d), BF16),
        compiler_params=_cparams("parallel"))(h, w)


def _rms_bwd(x, w, dy, dres, *, name):
    n, d = x.shape
    tm = _row_tile(n, 832)

    def body(x_ref, w_ref, dy_ref, dres_ref, dx_ref, dw_ref):
        i = pl.program_id(0)
        xv = x_ref[...]
        r = lax.rsqrt(jnp.mean(xv * xv, axis=-1, keepdims=True) + EPS)
        xh = xv * r
        dyv = dy_ref[...]
        g = dyv * w_ref[...]
        dx_ref[...] = r * (g - xh * jnp.mean(g * xh, axis=-1, keepdims=True)) + dres_ref[...]
        part = jnp.sum(dyv * xh, axis=0, keepdims=True)

        @pl.when(i == 0)
        def _():
            dw_ref[...] = part

        @pl.when(i > 0)
        def _():
            dw_ref[...] += part

    row = pl.BlockSpec((tm, d), lambda i: (i, 0))
    vec = pl.BlockSpec((1, d), lambda i: (0, 0))
    return pl.pallas_call(
        body, name=name, grid=(n // tm,), in_specs=[row, vec, row, row], out_specs=[row, vec],
        out_shape=[jax.ShapeDtypeStruct((n, d), F32), jax.ShapeDtypeStruct((1, d), F32)],
        compiler_params=_cparams("arbitrary"))(x, w, dy, dres)


def _final_loss(h, w, target):
    n, d = h.shape
    nb = n // BLOCK

    def body(h_ref, w_ref, t_ref, dh_ref, loss_ref, dw_ref):
        i = pl.program_id(0)
        xv = h_ref[...]
        r = lax.rsqrt(jnp.mean(xv * xv, axis=-1, keepdims=True) + EPS)
        xh = xv * r
        wv = w_ref[...]
        err = jnp.where(i >= 1, xh * wv - t_ref[...], 0.0)
        dyv = err * (1.0 / d)
        g = dyv * wv
        dh_ref[...] = r * (g - xh * jnp.mean(g * xh, axis=-1, keepdims=True))
        lpart = jnp.broadcast_to(0.5 * _sum_all(err * err) * (1.0 / d), (1, LANES))
        wpart = jnp.sum(dyv * xh, axis=0, keepdims=True)

        @pl.when(i == 0)
        def _():
            loss_ref[...] = lpart
            dw_ref[...] = wpart

        @pl.when(i > 0)
        def _():
            loss_ref[...] += lpart
            dw_ref[...] += wpart

    row = pl.BlockSpec((BLOCK, d), lambda i: (i, 0))
    vec = pl.BlockSpec((1, d), lambda i: (0, 0))
    return pl.pallas_call(
        body, name="final_loss", grid=(nb,),
        in_specs=[row, vec, pl.BlockSpec((BLOCK, d), lambda i: (jnp.maximum(i - 1, 0), 0))],
        out_specs=[row, pl.BlockSpec((1, LANES), lambda i: (0, 0)), vec],
        out_shape=[jax.ShapeDtypeStruct((n, d), F32), jax.ShapeDtypeStruct((1, LANES), F32),
                   jax.ShapeDtypeStruct((1, d), F32)],
        compiler_params=_cparams("arbitrary"))(h, w, target)


def _main_spec(tm, cb, off=0):
    return pl.BlockSpec((tm, cb), lambda j, i: (i, j + off))


def _prev_spec(tm, cb, off=0):
    r8 = tm // SUBLANES
    return pl.BlockSpec((SUBLANES, cb), lambda j, i: (jnp.maximum(i * r8 - 1, 0), j + off))


def _next_spec(tm, cb, n_rows, off=0):
    r8 = tm // SUBLANES
    last = n_rows // SUBLANES - 1
    return pl.BlockSpec((SUBLANES, cb), lambda j, i: (jnp.minimum((i + 1) * r8, last), j + off))


def _with_prev(prev_ref, main_ref, i):
    prev = jnp.where(i > 0, prev_ref[...], 0.0)
    return jnp.concatenate([prev, main_ref[...]], axis=0)


def _with_next(main, nxt, i, n_tiles):
    return jnp.concatenate([main, jnp.where(i < n_tiles - 1, nxt, 0.0)], axis=0)


def _back(xx, s, tm):
    if s == 0:
        return xx[SUBLANES:SUBLANES + tm]
    return pltpu.roll(xx, s, 0)[SUBLANES:SUBLANES + tm]


def _ahead(xx, s, tm):
    if s == 0:
        return xx[:tm]
    return pltpu.roll(xx, tm + SUBLANES - s, 0)[:tm]


def _conv_fwd(x, w, b, *, name):
    n, cdim = x.shape
    kw = w.shape[0]
    tm = _row_tile(n, 832)
    cb = _col_tile(cdim, 512)

    def body(xp_ref, x_ref, w_ref, b_ref, o_ref):
        xx = _with_prev(xp_ref, x_ref, pl.program_id(1))
        acc = jnp.broadcast_to(b_ref[...], (tm, cb))
        for k in range(kw):
            acc = acc + w_ref[k:k + 1, :] * _back(xx, kw - 1 - k, tm)
        o_ref[...] = acc

    return pl.pallas_call(
        body, name=name, grid=(cdim // cb, n // tm),
        in_specs=[_prev_spec(tm, cb), _main_spec(tm, cb), pl.BlockSpec((kw, cb), lambda j, i: (0, j)),
                  pl.BlockSpec((1, cb), lambda j, i: (0, j))],
        out_specs=_main_spec(tm, cb),
        out_shape=jax.ShapeDtypeStruct((n, cdim), F32),
        compiler_params=_cparams("parallel", "parallel"))(x, x, w, b)


def _conv_bwd_core(dpre_ext, x, w_ref, kw, tm):
    dx = None
    dws = []
    for k in range(kw):
        shifted = _ahead(dpre_ext, kw - 1 - k, tm)
        term = w_ref[k:k + 1, :] * shifted
        dx = term if dx is None else dx + term
        dws.append(jnp.sum(shifted * x, axis=0, keepdims=True))
    return dx, dws, jnp.sum(dpre_ext[:tm], axis=0, keepdims=True)


def _acc_rows(i, dw_ref, db_ref, dws, db):
    @pl.when(i == 0)
    def _():
        for k, v in enumerate(dws):
            dw_ref[k:k + 1, :] = v
        db_ref[...] = db

    @pl.when(i > 0)
    def _():
        for k, v in enumerate(dws):
            dw_ref[k:k + 1, :] += v
        db_ref[...] += db


def _conv_bwd(dpre, x, w, *, name):
    n, cdim = x.shape
    kw = w.shape[0]
    tm = _row_tile(n, 832)
    cb = _col_tile(cdim, 512)
    nt = n // tm

    def body(d_ref, dn_ref, x_ref, w_ref, dx_ref, dw_ref, db_ref):
        i = pl.program_id(1)
        dpre_ext = _with_next(d_ref[...], dn_ref[...], i, nt)
        dx, dws, db = _conv_bwd_core(dpre_ext, x_ref[...], w_ref, kw, tm)
        dx_ref[...] = dx.astype(BF16)
        _acc_rows(i, dw_ref, db_ref, dws, db)

    wspec = pl.BlockSpec((kw, cb), lambda j, i: (0, j))
    bspec = pl.BlockSpec((1, cb), lambda j, i: (0, j))
    return pl.pallas_call(
        body, name=name, grid=(cdim // cb, nt),
        in_specs=[_main_spec(tm, cb), _next_spec(tm, cb, n), _main_spec(tm, cb), wspec],
        out_specs=[_main_spec(tm, cb), wspec, bspec],
        out_shape=[jax.ShapeDtypeStruct((n, cdim), BF16), jax.ShapeDtypeStruct((kw, cdim), F32),
                   jax.ShapeDtypeStruct((1, cdim), F32)],
        compiler_params=_cparams("parallel", "arbitrary"))(dpre, dpre, x, w)


def _ffn_act_fwd(x, w, b):
    n = x.shape[0]
    kw = w.shape[0]
    tm = _row_tile(n, 832)
    cb = _col_tile(D_FF, 256)
    nc = D_FF // cb

    def body(xpu_ref, xu_ref, xpg_ref, xg_ref, wu_ref, wg_ref, bu_ref, bg_ref, hu_ref, hg_ref, act_ref):
        i = pl.program_id(1)
        outs = []
        for xp_ref, x_ref, w_ref, b_ref in ((xpu_ref, xu_ref, wu_ref, bu_ref), (xpg_ref, xg_ref, wg_ref, bg_ref)):
            xx = _with_prev(xp_ref, x_ref, i)
            acc = jnp.broadcast_to(b_ref[...], (tm, cb))
            for k in range(kw):
                acc = acc + w_ref[k:k + 1, :] * _back(xx, kw - 1 - k, tm)
            outs.append(acc)
        hu_ref[...] = outs[0]
        hg_ref[...] = outs[1]
        act_ref[...] = (_silu(outs[1]) * outs[0]).astype(BF16)

    def wspec(off):
        return pl.BlockSpec((kw, cb), lambda j, i: (0, j + off))

    def bspec(off):
        return pl.BlockSpec((1, cb), lambda j, i: (0, j + off))

    out = _main_spec(tm, cb)
    return pl.pallas_call(
        body, name="ffn_act_fwd", grid=(nc, n // tm),
        in_specs=[_prev_spec(tm, cb), _main_spec(tm, cb), _prev_spec(tm, cb, nc), _main_spec(tm, cb, nc),
                  wspec(0), wspec(nc), bspec(0), bspec(nc)],
        out_specs=[out, out, out],
        out_shape=[jax.ShapeDtypeStruct((n, D_FF), F32), jax.ShapeDtypeStruct((n, D_FF), F32),
                   jax.ShapeDtypeStruct((n, D_FF), BF16)],
        compiler_params=_cparams("parallel", "parallel"))(x, x, x, x, w, w, b, b)


def _ffn_act_bwd(dact, hu, hg, x, w):
    n = x.shape[0]
    kw = w.shape[0]
    tm = _row_tile(n, 832)
    cb = _col_tile(D_FF, 256)
    nc = D_FF // cb
    nt = n // tm

    def body(d_ref, dn_ref, hu_ref, hun_ref, hg_ref, hgn_ref, xu_ref, xg_ref, wu_ref, wg_ref,
             dxu_ref, dxg_ref, dwu_ref, dwg_ref, dbu_ref, dbg_ref):
        i = pl.program_id(1)
        dact_e = _with_next(d_ref[...], dn_ref[...], i, nt)
        up_e = _with_next(hu_ref[...], hun_ref[...], i, nt)
        gate_e = _with_next(hg_ref[...], hgn_ref[...], i, nt)
        sg = _sigmoid(gate_e)
        dup_e = dact_e * (gate_e * sg)
        dgate_e = dact_e * up_e * (sg * (1.0 + gate_e * (1.0 - sg)))
        dx, dws, db = _conv_bwd_core(dup_e, xu_ref[...], wu_ref, kw, tm)
        dxu_ref[...] = dx.astype(BF16)
        _acc_rows(i, dwu_ref, dbu_ref, dws, db)
        dx, dws, db = _conv_bwd_core(dgate_e, xg_ref[...], wg_ref, kw, tm)
        dxg_ref[...] = dx.astype(BF16)
        _acc_rows(i, dwg_ref, dbg_ref, dws, db)

    main, nxt = _main_spec(tm, cb), _next_spec(tm, cb, n)
    wspec0 = pl.BlockSpec((kw, cb), lambda j, i: (0, j))
    wspec1 = pl.BlockSpec((kw, cb), lambda j, i: (0, j + nc))
    bspec = pl.BlockSpec((1, cb), lambda j, i: (0, j))
    return pl.pallas_call(
        body, name="ffn_act_bwd", grid=(nc, nt),
        in_specs=[main, nxt, main, nxt, main, nxt, _main_spec(tm, cb), _main_spec(tm, cb, nc), wspec0, wspec1],
        out_specs=[main, main, wspec0, wspec0, bspec, bspec],
        out_shape=[jax.ShapeDtypeStruct((n, D_FF), BF16), jax.ShapeDtypeStruct((n, D_FF), BF16),
                   jax.ShapeDtypeStruct((kw, D_FF), F32), jax.ShapeDtypeStruct((kw, D_FF), F32),
                   jax.ShapeDtypeStruct((1, D_FF), F32), jax.ShapeDtypeStruct((1, D_FF), F32)],
        compiler_params=_cparams("parallel", "arbitrary"))(dact, dact, hu, hu, hg, hg, x, x, w, w)


def _ssd_prep(pxs_ref, pb_ref, pc_ref, dtr_ref, dtb_ref, alog_ref, c):
    xs = _silu(pxs_ref[...])
    bm = _silu(pb_ref[...])
    cm = _silu(pc_ref[...])
    return (xs, bm, cm) + _ssd_decay(dtr_ref, dtb_ref, alog_ref, c)


def _ssd_decay(dtr_ref, dtb_ref, alog_ref, c):
    row =lax.broadcasted_iota(jnp.int32, (BLOCK, 1), 0) + c * BLOCK
    valid = (row >= PAD).astype(F32)
    dtr = dtr_ref[...] + dtb_ref[...]
    dt = _softplus(dtr) * valid
    a = -jnp.exp(alog_ref[...])
    lam = dt * a
    ri = lax.broadcasted_iota(jnp.int32, (BLOCK, BLOCK), 0)
    ci = lax.broadcasted_iota(jnp.int32, (BLOCK, BLOCK), 1)
    causal = ci <= ri
    cs = _sel_dot(causal.astype(BF16), lam)
    return valid, dtr, dt, a, lam, cs, causal


def _head_cols(r):
    return slice(SSD_HEADDIM * r, SSD_HEADDIM * (r + 1))


def _ssd_specs(nc, rev):
    def cidx(c):
        return nc - 1 - c if rev else c

    xs = pl.BlockSpec((BLOCK, SSD_GW), lambda g, c: (cidx(c), g))
    bspec = pl.BlockSpec((BLOCK, SSD_STATE), lambda g, c: (cidx(c), SSD_INNER // SSD_STATE + g))
    cspec = pl.BlockSpec((BLOCK, SSD_STATE), lambda g, c: (cidx(c), (SSD_INNER + SSD_BC) // SSD_STATE + g))
    lane = pl.BlockSpec((BLOCK, LANES), lambda g, c: (cidx(c), g))
    vec = pl.BlockSpec((1, LANES), lambda g, c: (0, g))
    wide_vec = pl.BlockSpec((1, SSD_GW), lambda g, c: (0, g))
    hsave = pl.BlockSpec((1, 1, SSD_GW, SSD_STATE), lambda g, c: (cidx(c), g, 0, 0))
    return xs, bspec, cspec, lane, vec, wide_vec, hsave


def _head_spread_matrix():
    r = lax.broadcasted_iota(jnp.int32, (LANES, SSD_GW), 0)
    col = lax.broadcasted_iota(jnp.int32, (LANES, SSD_GW), 1)
    return (col // SSD_HEADDIM == r).astype(BF16)


def _const_spec(shape):
    return pl.BlockSpec(shape, lambda g, c: (0,) * len(shape))


def _spread_heads(per_head, e_ref):
    wide = _dot_sel(jnp.concatenate(per_head, axis=0), e_ref[...])
    return [wide[BLOCK * k:BLOCK * (k + 1)] for k in range(len(per_head))]


def _call_with_side(body, side, *, name, grid, in_specs, out_specs, out_shape, scratch_shapes, args):
    if side is None:
        outs = pl.pallas_call(body, name=name, grid=grid, in_specs=in_specs, out_specs=out_specs, out_shape=out_shape,
                              scratch_shapes=scratch_shapes, compiler_params=_cparams("parallel", "arbitrary"))(*args)
        return outs, []
    n_in, n_out, n_scr, n_side = len(in_specs), len(out_specs), len(scratch_shapes), len(side.arrays)

    def body_with_side(*refs):
        ins, rest = refs[:n_in + n_side], refs[n_in + n_side:]
        outs, scratch = rest[:n_out + n_side], rest[n_out + n_side:]
        side_refs = (ins[n_in:], outs[n_out:], scratch[n_scr:])
        g, c = pl.program_id(0), pl.program_id(1)

        @pl.when((g == 0) & (c == 0))
        def _():
            side.phases[0](*side_refs)

        body(*ins[:n_in], *outs[:n_out], *scratch[:n_scr])

        @pl.when((g == grid[0] // 2) & (c == 0))
        def _():
            side.phases[1](*side_refs)

        @pl.when((g == grid[0] - 1) & (c == grid[1] - 1))
        def _():
            side.phases[2](*side_refs)

    any_spec = pl.BlockSpec(memory_space=pl.ANY)
    outs = pl.pallas_call(
        body_with_side, name=name, grid=grid, in_specs=list(in_specs) + [any_spec] * n_side,
        out_specs=list(out_specs) + [any_spec] * n_side, out_shape=list(out_shape) + list(side.out_shape),
        scratch_shapes=list(scratch_shapes) + list(side.scratch_shapes),
        compiler_params=_cparams("arbitrary", "arbitrary"))(*args, *side.arrays)
    return outs[:n_out], outs[n_out:]


def _ssd_fwd(pre, dt_raw, z, dtb, alog, dskip_w, norm_w, side=None):
    n = pre.shape[0]
    nc = n // BLOCK
    xs_s, b_s, c_s, lane_s, vec_s, wide_s, hs_s = _ssd_specs(nc, False)

    def body(pxs_ref, pb_ref, pc_ref, dtr_ref, z_ref, dtb_ref, alog_ref, dskw_ref, nw_ref, e_ref,
             y_ref, yn_ref, hs_ref, h_scr):
        c = pl.program_id(1)

        @pl.when(c == 0)
        def _():
            h_scr[...] = jnp.zeros_like(h_scr)

        xs, bm, cm, _, _, dt, _, _, cs, causal = _ssd_prep(pxs_ref, pb_ref, pc_ref, dtr_ref, dtb_ref, alog_ref, c)
        cst = cs.T
        cs_last = cs[BLOCK - 1:BLOCK, :]
        dt_w, ecs_w, dec_w = _spread_heads([dt, jnp.exp(cs), jnp.exp(cs_last - cs)], e_ref)
        xdt = xs * dt_w
        bmb = bm.astype(BF16)
        cmb = cm.astype(BF16)
        cb = _dot_nt(cmb, bmb)
        hg = h_scr[...]
        hs_ref[0, 0] = hg
        y = _dot_nt(cmb, hg.astype(BF16)) * ecs_w + dskw_ref[...] * xs
        first = lax.broadcasted_iota(jnp.int32, (BLOCK, LANES), 1) < SSD_HEADDIM
        diag = []
        for j in range(SSD_HPG // 2):
            xp = xdt[:, LANES * j:LANES * (j + 1)].astype(BF16)
            res = []
            for r in (2 * j, 2 * j + 1):
                lm = jnp.exp(jnp.where(causal, cs[:, r:r + 1] - cst[r:r + 1, :], NEG))
                res.append(_dot((cb * lm).astype(BF16), xp))
            diag.append(jnp.where(first, res[0], res[1]))
        y = y + jnp.concatenate(diag, axis=1)
        st = _dot_tn((xdt * dec_w).astype(BF16), bmb)
        eh = jnp.exp(cs_last)
        for r in range(SSD_HPG):
            rows = _head_cols(r)
            h_scr[rows, :] = hg[rows, :] * eh[:, r:r + 1] + st[rows, :]
        y_ref[...] = y
        gts = y * _silu(z_ref[...])
        rr = lax.rsqrt(jnp.mean(gts * gts, axis=-1, keepdims=True) + EPS)
        yn_ref[...] = (gts * rr * nw_ref[...]).astype(BF16)

    return _call_with_side(
        body, side, name="ssd_fwd", grid=(SSD_GROUPS, nc),
        in_specs=[xs_s, b_s, c_s, lane_s, xs_s, vec_s, vec_s, wide_s, wide_s, _const_spec((LANES, SSD_GW))],
        out_specs=[xs_s, xs_s, hs_s],
        out_shape=[jax.ShapeDtypeStruct((n, SSD_INNER), F32), jax.ShapeDtypeStruct((n, SSD_INNER), BF16),
                   jax.ShapeDtypeStruct((nc, SSD_GROUPS, SSD_GW, SSD_STATE), F32)],
        scratch_shapes=[pltpu.VMEM((SSD_GW, SSD_STATE), F32)],
        args=(pre, pre, pre, dt_raw, z, dtb, alog, dskip_w, norm_w, _head_spread_matrix()))


def _lane_put(acc, col, r):
    lane = lax.broadcasted_iota(jnp.int32, acc.shape, 1)
    return jnp.where(lane == r, col, acc)


def _ssd_bwd(dyn, y, z, pre, dt_raw, hsave, dtb, alog, dskip_w, norm_w, side=None):
    n = pre.shape[0]
    nc = n // BLOCK
    spread = _head_spread_matrix()
    xs_s, b_s, c_s, lane_s, vec_s, wide_s, hs_s = _ssd_specs(nc, True)
    bc_out =pl.BlockSpec((BLOCK, SSD_STATE), lambda g, c: (nc - 1 - c, g))

    def body(dyn_ref, y_ref, z_ref, pxs_ref, pb_ref, pc_ref, dtr_ref, hs_ref, dtb_ref, alog_ref, dskw_ref, nw_ref,
             e_ref, r_ref,
             dz_ref, dxs_ref, dbm_ref, dcm_ref, ddt_ref, dnw_ref, ddtb_ref, dalog_ref, ddsk_ref, g_scr):
        step = pl.program_id(1)
        c = nc - 1 - step

        @pl.when(step == 0)
        def _():
            g_scr[...] = jnp.zeros_like(g_scr)

        pxs, pb, pc = pxs_ref[...], pb_ref[...], pc_ref[...]
        sx, sb, sc = _sigmoid(pxs), _sigmoid(pb), _sigmoid(pc)
        xs, bm, cm = pxs * sx, pb * sb, pc * sc
        valid, dtr, dt, a, lam, cs, causal = _ssd_decay(dtr_ref, dtb_ref, alog_ref, c)
        cst = cs.T
        cs_last = cs[BLOCK - 1:BLOCK, :]
        bmb = bm.astype(BF16)
        cmb = cm.astype(BF16)
        cb = _dot_nt(cmb, bmb)
        hg = hs_ref[0, 0]
        hgb = hg.astype(BF16)
        yoff = _dot_nt(cmb, hgb)
        gn = g_scr[...]
        gnb = gn.astype(BF16)

        zv = z_ref[...]
        yv = y_ref[...]
        sgz = _sigmoid(zv)
        sz = zv * sgz
        gts = yv * sz
        rr = lax.rsqrt(jnp.mean(gts * gts, axis=-1, keepdims=True) + EPS)
        xh = gts * rr
        dynv = dyn_ref[...]
        gg = dynv * nw_ref[...]
        dgts = rr * (gg - xh * jnp.mean(gg * xh, axis=-1, keepdims=True))
        dnw = jnp.sum(dynv * xh, axis=0, keepdims=True)
        dy = dgts * sz
        dz_ref[...] = (dgts * yv * (sgz * (1.0 + zv * (1.0 - sgz)))).astype(BF16)

        ecs = jnp.exp(cs)
        dec = jnp.exp(cs_last - cs)
        eh = jnp.exp(cs_last)
        dt_w, ecs_w, dec_w = _spread_heads([dt, ecs, dec], e_ref)
        red_m = r_ref[...]

        def head_sums(v):
            return _dot_sel(v, red_m, terms=2)

        xdt = xs * dt_w
        q_all = _dot_nt(bmb, gnb)
        w_all = (dy * ecs_w).astype(BF16)
        e_hl = head_sums(q_all * xdt) * dec
        dcs_col = head_sums(dy * yoff) * ecs - e_hl
        gh = jnp.zeros((1, LANES), F32)
        prod = gn * hg
        for r in range(SSD_HPG):
            gh = _lane_put(gh, _sum_all(prod[_head_cols(r), :]), r)
        dcs_last = jnp.sum(e_hl, axis=0, keepdims=True) + eh * gh
        ddsk = jnp.sum(head_sums(dy * xs), axis=0, keepdims=True)
        cbt = _dot_nt(bmb, cmb)
        lane = lax.broadcasted_iota(jnp.int32, (BLOCK, LANES), 1)
        first = lane < SSD_HEADDIM
        causal_t = lax.broadcasted_iota(jnp.int32, (BLOCK, BLOCK), 1) >= lax.broadcasted_iota(
            jnp.int32, (BLOCK, BLOCK), 0)
        sub = lax.broadcasted_iota(jnp.int32, (SUBLANES, BLOCK), 0)
        dcs_row = jnp.zeros((SUBLANES, BLOCK), F32)
        dcb = jnp.zeros((BLOCK, BLOCK), F32)
        dxdt_pairs = []
        for j in range(SSD_HPG // 2):
            tile = slice(LANES * j, LANES * (j + 1))
            dy_p = dy[:, tile]
            dyb = dy_p.astype(BF16)
            xdtb = xdt[:, tile].astype(BF16)
            res = []
            for half, r in enumerate((2 * j, 2 * j + 1)):
                csc, csr = cs[:, r:r + 1], cst[r:r + 1, :]
                lm = jnp.exp(jnp.where(causal, csc - csr, NEG))
                lmt = jnp.exp(jnp.where(causal_t, csr - csc, NEG))
                keep = first if half == 0 else jnp.logical_not(first)
                gm = _dot_nt(jnp.where(keep, dy_p, 0.0).astype(BF16), xdtb) * lm
                dcb = dcb + gm
                mm_ = gm * cb
                dcs_col = dcs_col + jnp.where(lane == r, jnp.sum(mm_, axis=1, keepdims=True), 0.0)
                dcs_row = jnp.where(sub == r, jnp.sum(mm_, axis=0, keepdims=True), dcs_row)
                res.append(_dot((cbt * lmt).astype(BF16), dyb))
            dxdt_pairs.append(jnp.where(first, res[0], res[1]))
        dxdt = jnp.concatenate(dxdt_pairs, axis=1) + q_all * dec_w
        ddt_x = head_sums(dxdt * xs)
        dxs = dxdt * dt_w + dskw_ref[...] * dy
        dcbb = dcb.astype(BF16)
        dcm = _dot(w_all, hgb) + _dot(dcbb, bmb)
        dbm = _dot((xdt * dec_w).astype(BF16), gnb) + _dot_tn(dcbb, cmb)
        dh_off = _dot_tn(w_all, cmb)
        for r in range(SSD_HPG):
            rows = _head_cols(r)
            g_scr[rows, :] = gn[rows, :] * eh[:, r:r + 1] + dh_off[rows, :]

        pad_rows = jnp.zeros((BLOCK - SUBLANES, BLOCK), F32)
        dcs = dcs_col - jnp.concatenate([dcs_row, pad_rows], axis=0).T
        rsel = lax.broadcasted_iota(jnp.int32, (BLOCK, LANES), 0)
        dcs = dcs + jnp.where(rsel == BLOCK - 1, dcs_last, 0.0)
        ri = lax.broadcasted_iota(jnp.int32, (BLOCK, BLOCK), 0)
        ci = lax.broadcasted_iota(jnp.int32, (BLOCK, BLOCK), 1)
        dlam = _sel_dot((ci >= ri).astype(BF16), dcs)
        head = lane < SSD_HPG
        ddt = dlam * a + ddt_x
        ddtr = jnp.where(head, ddt * _sigmoid(dtr) * valid, 0.0)
        ddt_ref[...] = ddtr.astype(BF16)
        dalog = jnp.sum(jnp.where(head, dlam * lam, 0.0), axis=0, keepdims=True)
        ddtb = jnp.sum(ddtr, axis=0, keepdims=True)

        dxs_ref[...] = dxs * (sx * (1.0 + pxs * (1.0 - sx)))
        dbm_ref[...] = dbm * (sb * (1.0 + pb * (1.0 - sb)))
        dcm_ref[...] = dcm * (sc * (1.0 + pc * (1.0 - sc)))

        @pl.when(step == 0)
        def _():
            dnw_ref[...] = dnw
            ddtb_ref[...] = ddtb
            dalog_ref[...] = dalog
            ddsk_ref[...] = ddsk

        @pl.when(step > 0)
        def _():
            dnw_ref[...] += dnw
            ddtb_ref[...] += ddtb
            dalog_ref[...] += dalog
            ddsk_ref[...] += ddsk

    return _call_with_side(
        body, side, name="ssd_bwd", grid=(SSD_GROUPS, nc),
        in_specs=[xs_s, xs_s, xs_s, xs_s, b_s, c_s, lane_s, hs_s, vec_s, vec_s, wide_s, wide_s,
                  _const_spec((LANES, SSD_GW)), _const_spec((SSD_GW, LANES))],
        out_specs=[xs_s, xs_s, bc_out, bc_out, lane_s, wide_s, vec_s, vec_s, vec_s],
        out_shape=[jax.ShapeDtypeStruct((n, SSD_INNER), BF16), jax.ShapeDtypeStruct((n, SSD_INNER), F32),
                   jax.ShapeDtypeStruct((n, SSD_BC), F32), jax.ShapeDtypeStruct((n, SSD_BC), F32),
                   jax.ShapeDtypeStruct((n, DT_W), BF16), jax.ShapeDtypeStruct((1, SSD_INNER), F32),
                   jax.ShapeDtypeStruct((1, DT_W), F32), jax.ShapeDtypeStruct((1, DT_W), F32),
                   jax.ShapeDtypeStruct((1, DT_W), F32)],
        scratch_shapes=[pltpu.VMEM((SSD_GW, SSD_STATE), F32)],
        args=(dyn, y, z, pre, pre, pre, dt_raw, hsave, dtb, alog, dskip_w, norm_w, spread, spread.T))


def _bucket_table():
    def bucket(dist):
        d = np.maximum(dist, 0)
        half = REL_BUCKETS // 2
        big = half + (np.log(np.maximum(d, half).astype(np.float32) / np.float32(half))
                      / np.float32(math.log(REL_MAX_DIST / half)) * np.float32(REL_BUCKETS - half)).astype(np.int32)
        return np.where(d < half, d, np.minimum(big, REL_BUCKETS - 1)).astype(np.int32)

    l = np.arange(BLOCK)[None, :]
    band = bucket(l + BLOCK - np.arange(2 * BLOCK)[:, None])
    j = np.arange(BLOCK)[:, None]
    tables = [np.concatenate([bucket(v * BLOCK + l - j), band], axis=0) for v in range(3)]
    return np.concatenate([t.reshape(-1) for t in tables])


def _onehot_t():
    buckets = jnp.asarray(_bucket_table())
    return (buckets[None, :] == jnp.arange(REL_BUCKETS, dtype=jnp.int32)[:, None]).astype(F32)


def _bias_tables(rel_t, onehot_t):
    def body(r_ref, oh_ref, o_ref):
        o_ref[...] = jnp.dot(r_ref[...], oh_ref[...], precision=HIGHEST, preferred_element_type=F32)

    return pl.pallas_call(
        body, name="bias_tables", grid=(NT_ALL // NT_TILE,),
        in_specs=[pl.BlockSpec((ATT_HEADS, REL_BUCKETS), lambda i: (0, 0)),
                  pl.BlockSpec((REL_BUCKETS, NT_TILE), lambda i: (0, i))],
        out_specs=pl.BlockSpec((ATT_HEADS, NT_TILE), lambda i: (0, i)),
        out_shape=jax.ShapeDtypeStruct((ATT_HEADS, NT_ALL), F32),
        compiler_params=_cparams("parallel"))(rel_t, onehot_t)


def _bias_grad(dtab, onehot_t):
    def body(d_ref, oh_ref, o_ref):
        i = pl.program_id(0)
        p = lax.dot_general(d_ref[...], oh_ref[...], (((1,), (1,)), ((), ())), precision=HIGHEST,
                            preferred_element_type=F32)

        @pl.when(i == 0)
        def _():
            o_ref[...] = p

        @pl.when(i > 0)
        def _():
            o_ref[...] += p

    return pl.pallas_call(
        body, name="bias_grad", grid=(NT_ALL // NT_TILE,),
        in_specs=[pl.BlockSpec((ATT_HEADS, NT_TILE), lambda i: (0, i)),
                  pl.BlockSpec((REL_BUCKETS, NT_TILE), lambda i: (0, i))],
        out_specs=pl.BlockSpec((ATT_HEADS, REL_BUCKETS), lambda i: (0, 0)),
        out_shape=jax.ShapeDtypeStruct((ATT_HEADS, REL_BUCKETS), F32),
        compiler_params=_cparams("arbitrary"))(dtab, onehot_t)


def _att_mask_t(n, copies):
    far = 4 * BLOCK
    kk = lax.broadcasted_iota(jnp.int32, (N_KEYS, copies * BLOCK), 0)
    li = lax.broadcasted_iota(jnp.int32, (N_KEYS, copies * BLOCK), 1) & (BLOCK - 1)
    meta_ok = (kk >= PAD) & (kk < BLOCK) & (li + jnp.where(n >= 1, far, 0) >= kk)
    prev_ok = (kk >= BLOCK) & (kk < 2 * BLOCK) & (kk - BLOCK > li + jnp.where(n >= 2, 0, far))
    cur_ok = (kk >= 2 * BLOCK) & (kk - 2 * BLOCK <= li - jnp.where(n >= 1, 0, far))
    return meta_ok | prev_ok | cur_ok


def _att_kv(meta_ref, prev_ref, cur_ref):
    kv = jnp.concatenate([meta_ref[...], prev_ref[...], cur_ref[...]], axis=0)
    first = lax.broadcasted_iota(jnp.int32, (N_KEYS, LANES), 1) < ATT_HEADDIM
    out = []
    for pair in (kv[:, :LANES], kv[:, LANES:]):
        swapped = pltpu.roll(pair, ATT_HEADDIM, 1)
        out.append([jnp.where(first, pair, swapped).astype(BF16), jnp.where(first, swapped, pair).astype(BF16)])
    return out[0], out[1]


def _split_heads(x_pair, first):
    return jnp.concatenate([jnp.where(first, x_pair, 0.0), jnp.where(first, 0.0, x_pair)], axis=0).astype(BF16)


def _att_probs_t(qm2, k_dup, t_ref, j, mask2, sink_ref):
    scale = ATT_HEADDIM ** -0.5
    bias2 = jnp.concatenate([t_ref[0, 2 * j], t_ref[0, 2 * j + 1]], axis=1)
    second = lax.broadcasted_iota(jnp.int32, (1, 2 * BLOCK), 1) >= BLOCK
    sink2 = jnp.where(second, sink_ref[0:1, 2 * j + 1:2 * j + 2], sink_ref[0:1, 2 * j:2 * j + 1])
    s_t = jnp.where(mask2, _dot_nt(k_dup, qm2) * scale + bias2, NEG)
    mx = jnp.maximum(jnp.max(s_t, axis=0, keepdims=True), sink2)
    p_t = jnp.exp(s_t - mx)
    p_s = jnp.exp(sink2 - mx)
    inv = 1.0 / (jnp.sum(p_t, axis=0, keepdims=True) + p_s)
    return p_t * inv, p_s * inv


def _att_specs(nb, rev):
    def nidx(i):
        return nb - 1 - i if rev else i

    kvb = ATT_Q // (2 * ATT_KV)
    q_s = pl.BlockSpec((BLOCK, ATT_Q), lambda i: (nidx(i), 0))
    cur = pl.BlockSpec((BLOCK, 2 * ATT_KV), lambda i: (nidx(i), kvb))
    prev = pl.BlockSpec((BLOCK, 2 * ATT_KV), lambda i: (jnp.maximum(nidx(i) - 1, 0), kvb))
    meta = pl.BlockSpec((BLOCK, 2 * ATT_KV), lambda i: (0, kvb))
    table = pl.BlockSpec((1, ATT_HEADS, N_KEYS, BLOCK), lambda i: (jnp.minimum(nidx(i), 2), 0, 0, 0))
    sink = pl.BlockSpec((1, LANES), lambda i: (0, 0))
    return q_s, cur, prev, meta, table, sink


def _attn_fwd(qkv, tables, sinks):
    n = qkv.shape[0]
    nb = n // BLOCK
    q_s, cur_s, prev_s, meta_s, t_s, sink_s = _att_specs(nb, False)

    def body(q_ref, cur_ref, prev_ref, meta_ref, t_ref, sink_ref, o_ref):
        blk = pl.program_id(0)
        mask_t = _att_mask_t(blk, 1)
        k_dup, v_dup = _att_kv(meta_ref, prev_ref, cur_ref)
        v_dup_t = [v.T for v in v_dup]
        first = lax.broadcasted_iota(jnp.int32, (BLOCK, LANES), 1) < ATT_HEADDIM
        top = lax.broadcasted_iota(jnp.int32, (LANES, BLOCK), 0) < ATT_HEADDIM
        scale = ATT_HEADDIM ** -0.5
        for j in range(ATT_HEADS // 2):
            kh = 2 * j // ATT_GQ
            tile = slice(LANES * j, LANES * (j + 1))
            q_p = q_ref[:, tile]
            res = []
            for half, h in enumerate((2 * j, 2 * j + 1)):
                qm = jnp.where(first if half == 0 else jnp.logical_not(first), q_p, 0.0).astype(BF16)
                sink = sink_ref[0:1, h:h + 1]
                s_t = jnp.where(mask_t, _dot_nt(k_dup[kh], qm) * scale + t_ref[0, h], NEG)
                mx = jnp.maximum(jnp.max(s_t, axis=0, keepdims=True), sink)
                p_t = jnp.exp(s_t - mx)
                inv = 1.0 / (jnp.sum(p_t, axis=0, keepdims=True) + jnp.exp(sink - mx))
                res.append(_dot(v_dup_t[kh], (p_t * inv).astype(BF16)))
            o_ref[:, tile] = jnp.where(top, res[0], res[1]).T.astype(BF16)

    return pl.pallas_call(
        body, name="attn_fwd", grid=(nb,),
        in_specs=[q_s, cur_s, prev_s, meta_s, t_s, sink_s],
        out_specs=q_s,
        out_shape=jax.ShapeDtypeStruct((n, ATT_Q), BF16),
        compiler_params=_cparams("parallel"))(qkv, qkv, qkv, qkv, tables, sinks)


def _attn_bwd(datt, qkv, tables, sinks):
    n = qkv.shape[0]
    nb = n // BLOCK
    q_s, cur_s, prev_s, meta_s, t_s, sink_s = _att_specs(nb, True)
    dqkv_s = pl.BlockSpec((BLOCK, ATT_Q + 2 * ATT_KV), lambda i: (nb - 1 - i, 0))
    scale = ATT_HEADDIM ** -0.5

    def body(do_ref, q_ref, cur_ref, prev_ref, meta_ref, t_ref, sink_ref,
             dqkv_ref, dt_ref, dsink_ref, carry_scr, meta_scr):
        step = pl.program_id(0)
        blk = nb - 1 - step
        mask2 = _att_mask_t(blk, 2)
        k_dup, v_dup = _att_kv(meta_ref, prev_ref, cur_ref)
        k_dup_t = [k.T for k in k_dup]

        @pl.when(step == 0)
        def _():
            carry_scr[...] = jnp.zeros_like(carry_scr)
            meta_scr[...] = jnp.zeros_like(meta_scr)
            dsink_ref[...] = jnp.zeros_like(dsink_ref)

        @pl.when((step == 0) | (blk <= 1))
        def _():
            dt_ref[...] = jnp.zeros_like(dt_ref)

        first = lax.broadcasted_iota(jnp.int32, (BLOCK, LANES), 1) < ATT_HEADDIM
        top = lax.broadcasted_iota(jnp.int32, (LANES, BLOCK), 0) < ATT_HEADDIM
        first_k = lax.broadcasted_iota(jnp.int32, (N_KEYS, LANES), 1) < ATT_HEADDIM
        dsink = jnp.zeros((1, LANES), F32)
        dk_acc = [None] * ATT_KV_HEADS
        dv_acc = [None] * ATT_KV_HEADS
        for j in range(ATT_HEADS // 2):
            kh = 2 * j // ATT_GQ
            tile = slice(LANES * j, LANES * (j + 1))
            qm2 = _split_heads(q_ref[:, tile], first)
            dom2 = _split_heads(do_ref[:, tile], first)
            p_t, p_s = _att_probs_t(qm2, k_dup[kh], t_ref, j, mask2, sink_ref)
            dp_t = _dot_nt(v_dup[kh], dom2)
            delta = jnp.sum(p_t * dp_t, axis=0, keepdims=True)
            ds_t = p_t * (dp_t - delta)
            sink_terms = p_s * delta
            for half in range(2):
                cols = slice(BLOCK * half, BLOCK * (half + 1))
                dsink = _lane_put(dsink, -jnp.sum(sink_terms[:, cols], axis=1, keepdims=True), 2 * j + half)
                dt_ref[0, 2 * j + half] += ds_t[:, cols]
            ds_tb = ds_t.astype(BF16)
            dq_t = _dot(k_dup_t[kh], ds_tb)
            dqkv_ref[:, tile] = (jnp.where(top, dq_t[:, :BLOCK], dq_t[:, BLOCK:]).T * scale).astype(BF16)
            dk_part, dv_part = _dot(ds_tb, qm2), _dot(p_t.astype(BF16), dom2)
            dk_acc[kh] = dk_part if dk_acc[kh] is None else dk_acc[kh] + dk_part
            dv_acc[kh] = dv_part if dv_acc[kh] is None else dv_acc[kh] + dv_part
        dsink_ref[...] += dsink
        folded = [a + pltpu.roll(a, ATT_HEADDIM, 1) for a in dk_acc + dv_acc]
        dkv = jnp.concatenate([jnp.where(first_k, folded[0], folded[1]) * scale,
                               jnp.where(first_k, folded[2], folded[3])], axis=1)
        meta_scr[...] += dkv[:BLOCK, :]
        own = dkv[2 * BLOCK:, :] + carry_scr[...]
        carry_scr[...] = dkv[BLOCK:2 * BLOCK, :]

        @pl.when(blk > 0)
        def _():
            dqkv_ref[:, ATT_Q:] = own.astype(BF16)

        @pl.when(blk == 0)
        def _():
            dqkv_ref[:, ATT_Q:] = (own + meta_scr[...]).astype(BF16)

    return pl.pallas_call(
        body, name="attn_bwd", grid=(nb,),
        in_specs=[q_s, q_s, cur_s, prev_s, meta_s, t_s, sink_s],
        out_specs=[dqkv_s, t_s, sink_s],
        out_shape=[jax.ShapeDtypeStruct((n, ATT_Q + 2 * ATT_KV), BF16),
                   jax.ShapeDtypeStruct((3, ATT_HEADS, N_KEYS, BLOCK), F32),
                   jax.ShapeDtypeStruct((1, LANES), F32)],
        scratch_shapes=[pltpu.VMEM((BLOCK, 2 * ATT_KV), F32), pltpu.VMEM((BLOCK, 2 * ATT_KV), F32)],
        compiler_params=_cparams("arbitrary"))(datt, qkv, qkv, qkv, qkv, tables, sinks)


def _merge_fwd(gates, y_ssd, y_att, gate_b):
    n = gates.shape[0]
    tm = _row_tile(n, 832)

    def body(gs_ref, ga_ref, ys_ref, ya_ref, gb_ref, o_ref):
        o_ref[...] = (_sigmoid(gs_ref[...] + gb_ref[0:1, :]) * ys_ref[...]
                      + _sigmoid(ga_ref[...] + gb_ref[1:2, :]) * ya_ref[...]).astype(BF16)

    row = pl.BlockSpec((tm, D_MODEL), lambda i: (i, 0))
    return pl.pallas_call(
        body, name="merge_fwd", grid=(n // tm,),
        in_specs=[row, pl.BlockSpec((tm, D_MODEL), lambda i: (i, 1)), row, row,
                  pl.BlockSpec((2, D_MODEL), lambda i: (0, 0))],
        out_specs=row, out_shape=jax.ShapeDtypeStruct((n, D_MODEL), BF16),
        compiler_params=_cparams("parallel"))(gates, gates, y_ssd, y_att, gate_b)


def _merge_bwd(dm, gates, y_ssd, y_att, gate_b):
    n = gates.shape[0]
    tm = _row_tile(n, 832)

    def body(dm_ref, gs_ref, ga_ref, ys_ref, ya_ref, gb_ref, dys_ref, dya_ref, dg_ref, dgb_ref):
        i = pl.program_id(0)
        dmv = dm_ref[...]
        ss = _sigmoid(gs_ref[...] + gb_ref[0:1, :])
        sa = _sigmoid(ga_ref[...] + gb_ref[1:2, :])
        dys_ref[...] = (dmv * ss).astype(BF16)
        dya_ref[...] = (dmv * sa).astype(BF16)
        dgs = dmv * ys_ref[...] * ss * (1.0 - ss)
        dga = dmv * ya_ref[...] * sa * (1.0 - sa)
        dg_ref[:, :D_MODEL] = dgs.astype(BF16)
        dg_ref[:, D_MODEL:] = dga.astype(BF16)
        part = jnp.concatenate([jnp.sum(dgs, axis=0, keepdims=True), jnp.sum(dga, axis=0, keepdims=True)], axis=0)

        @pl.when(i == 0)
        def _():
            dgb_ref[...] = part

        @pl.when(i > 0)
        def _():
            dgb_ref[...] += part

    row = pl.BlockSpec((tm, D_MODEL), lambda i: (i, 0))
    gb = pl.BlockSpec((2, D_MODEL), lambda i: (0, 0))
    return pl.pallas_call(
        body, name="merge_bwd", grid=(n // tm,),
        in_specs=[row, row, pl.BlockSpec((tm, D_MODEL), lambda i: (i, 1)), row, row, gb],
        out_specs=[row, row, pl.BlockSpec((tm, 2 * D_MODEL), lambda i: (i, 0)), gb],
        out_shape=[jax.ShapeDtypeStruct((n, D_MODEL), BF16), jax.ShapeDtypeStruct((n, D_MODEL), BF16),
                   jax.ShapeDtypeStruct((n, 2 * D_MODEL), BF16), jax.ShapeDtypeStruct((2, D_MODEL), F32)],
        compiler_params=_cparams("arbitrary"))(dm, gates, gates, y_ssd, y_att, gate_b)


def _col_move(srcs, outs, pieces, *, name):
    rows = srcs[0].shape[-2]
    tr = _row_tile(rows, 128)
    n_src = len(srcs)
    covered = [sum(p[6] for p in pieces if p[0] == o) for o in range(len(outs))]
    total = [int(np.prod(shp)) // rows for shp, _ in outs]

    def body(*refs):
        in_refs, out_refs = refs[:n_src], refs[n_src:]
        for o, ref in enumerate(out_refs):
            if covered[o] != total[o]:
                ref[...] = jnp.zeros_like(ref)
        for o, ol, oc, s, sl, sc, width in pieces:
            val = in_refs[s][:, sc:sc + width] if sl is None else in_refs[s][sl, :, sc:sc + width]
            val = val.astype(outs[o][1])
            if ol is None:
                out_refs[o][:, oc:oc + width] = val
            else:
                out_refs[o][ol, :, oc:oc + width] = val

    def spec(shape):
        if len(shape) == 2:
            return pl.BlockSpec((tr, shape[1]), lambda i: (i, 0))
        return pl.BlockSpec((shape[0], tr, shape[2]), lambda i: (0, i, 0))

    return pl.pallas_call(
        body, name=name, grid=(rows // tr,),
        in_specs=[spec(a.shape) for a in srcs], out_specs=[spec(shp) for shp, _ in outs],
        out_shape=[jax.ShapeDtypeStruct(shp, dt) for shp, dt in outs],
        compiler_params=_cparams("parallel"))(*srcs)


def _shard_pieces(seg_ranges, shard_w):
    out = []
    for seg, runs in enumerate(seg_ranges):
        for g0, width, s0 in runs:
            done = 0
            while done < width:
                dev, col = divmod(g0 + done, shard_w)
                take = min(width - done, shard_w - col)
                out.append((seg, s0 + done, dev, col, take))
                done += take
    return out


_CHIP_RELATIONS = [(1, 0, 0), (0, 1, 0), (1, 1, 0)]
N_CHIPS = 4


def _gather_two_level(arrays, *, name):
    outs = _run_plan(_gather_plan(arrays), name)
    return [o.reshape((N_DEV,) + a.shape) for o, a in zip(outs, arrays)]


class _CommPlan:
    def __init__(self, arrays, out_shape, scratch_shapes, phases):
        self.arrays, self.out_shape, self.scratch_shapes, self.phases = arrays, out_shape, scratch_shapes, phases


def _run_plan(plan, name):
    n_arr = len(plan.arrays)

    def body(*refs):
        ins, outs, sems = refs[:n_arr], refs[n_arr:2 * n_arr], refs[2 * n_arr:]
        for phase in plan.phases:
            phase(ins, outs, sems)

    any_spec = pl.BlockSpec(memory_space=pl.ANY)
    return pl.pallas_call(
        body, name=name, in_specs=[any_spec] * n_arr, out_specs=[any_spec] * n_arr, out_shape=plan.out_shape,
        scratch_shapes=plan.scratch_shapes)(*plan.arrays)


def _gather_plan(arrays):
    n_arr = len(arrays)
    n_chips = len(_CHIP_RELATIONS)
    n_pair = 1 + 2 * n_chips

    def where():
        x, y, c = lax.axis_index("x"), lax.axis_index("y"), lax.axis_index("c")
        return x, y, c, (x, y, 1 - c), [(x ^ dx, y ^ dy) for dx, dy, _ in _CHIP_RELATIONS]

    def copy(outs, sems, a, k, block, to, src=None):
        slot = outs[a].at[2 * block[0] + block[1], block[2]]
        return pltpu.make_async_remote_copy(
            src_ref=slot if src is None else src, dst_ref=slot, send_sem=sems[0].at[a * n_pair + k],
            recv_sem=sems[1].at[a * n_pair + k], device_id=to, device_id_type=MESH)

    def mine(ins, outs, sems, a, x, y, c):
        return pltpu.make_async_copy(ins[a], outs[a].at[2 * x + y, c], sems[2].at[a])

    def first_copies(ins, outs, sems, a, x, y, c, sibling, chips):
        return ([copy(outs, sems, a, 0, (x, y, c), sibling, src=ins[a])]
                + [copy(outs, sems, a, 1 + j, (x, y, c), (*chip, c), src=ins[a]) for j, chip in enumerate(chips)])

    def start(ins, outs, sems):
        x, y, c, sibling, chips = where()
        for a in range(n_arr):
            mine(ins, outs, sems, a, x, y, c).start()
            for cp in first_copies(ins, outs, sems, a, x, y, c, sibling, chips):
                cp.start()

    def pass_on(ins, outs, sems):
        x, y, c, sibling, chips = where()
        for j, chip in enumerate(chips):
            for a in range(n_arr):
                copy(outs, sems, a, 1 + j, (*chip, c), (x, y, c)).wait_recv()
                copy(outs, sems, a, 1 + n_chips + j, (*chip, c), sibling).start()

    def finish(ins, outs, sems):
        x, y, c, sibling, chips = where()
        for a in range(n_arr):
            copy(outs, sems, a, 0, (x, y, 1 - c), (x, y, c)).wait_recv()
            for j, chip in enumerate(chips):
                copy(outs, sems, a, 1 + n_chips + j, (*chip, 1 - c), (x, y, c)).wait_recv()
        for a in range(n_arr):
            for cp in first_copies(ins, outs, sems, a, x, y, c, sibling, chips):
                cp.wait_send()
            for j, chip in enumerate(chips):
                copy(outs, sems, a, 1 + n_chips + j, (*chip, c), sibling).wait_send()
            mine(ins, outs, sems, a, x, y, c).wait()

    return _CommPlan(
        arrays, [jax.ShapeDtypeStruct((N_CHIPS, 2) + a.shape, a.dtype) for a in arrays],
        [pltpu.SemaphoreType.DMA((n_arr * n_pair,)), pltpu.SemaphoreType.DMA((n_arr * n_pair,)),
         pltpu.SemaphoreType.DMA((n_arr,))],
        (start, pass_on, finish))


def _sibling_exchange(arrays, scatter, *, name):
    n_arr = len(arrays)

    def body(*refs):
        ins, outs = refs[:n_arr], refs[n_arr:2 * n_arr]
        send_sems, recv_sems = refs[2 * n_arr:]
        x, y, c = lax.axis_index("x"), lax.axis_index("y"), lax.axis_index("c")
        copies = []
        for a in range(n_arr):
            for q in range(N_CHIPS if scatter[a] else 1):
                src = ins[a].at[2 * q + 1 - c] if scatter[a] else ins[a]
                dst = outs[a].at[q] if scatter[a] else outs[a]
                cp = pltpu.make_async_remote_copy(
                    src_ref=src, dst_ref=dst, send_sem=send_sems.at[a * N_CHIPS + q],
                    recv_sem=recv_sems.at[a * N_CHIPS + q], device_id=(x, y, 1 - c), device_id_type=MESH)
                cp.start()
                copies.append(cp)
        for cp in copies:
            cp.wait_send()
        for cp in copies:
            cp.wait_recv()

    any_spec = pl.BlockSpec(memory_space=pl.ANY)
    return pl.pallas_call(
        body, name=name, in_specs=[any_spec] * n_arr, out_specs=[any_spec] * n_arr,
        out_shape=[jax.ShapeDtypeStruct(((N_CHIPS,) + a.shape[1:]) if s else a.shape, a.dtype)
                   for a, s in zip(arrays, scatter)],
        scratch_shapes=[pltpu.SemaphoreType.DMA((n_arr * N_CHIPS,)), pltpu.SemaphoreType.DMA((n_arr * N_CHIPS,))],
    )(*arrays)


def _pair_sum(mine, sib, *, name, out_dtype):
    _, rows, cols = mine.shape
    tr = _row_tile(rows, 128)

    def body(m_ref, s_ref, o_ref):
        c = lax.axis_index("c")
        o_ref[0] = (m_ref[0, c] + s_ref[0]).astype(out_dtype)

    return pl.pallas_call(
        body, name=name, grid=(N_CHIPS, rows // tr),
        in_specs=[pl.BlockSpec((1, 2, tr, cols), lambda q, i: (q, 0, i, 0)),
                  pl.BlockSpec((1, tr, cols), lambda q, i: (q, i, 0))],
        out_specs=pl.BlockSpec((1, tr, cols), lambda q, i: (q, i, 0)),
        out_shape=jax.ShapeDtypeStruct((N_CHIPS, rows, cols), out_dtype),
        compiler_params=_cparams("parallel", "parallel"))(mine.reshape(N_CHIPS, 2, rows, cols), sib)


def _add(a, b, *, name):
    rows, cols = a.shape
    tr = _row_tile(rows, 256)

    def body(a_ref, b_ref, o_ref):
        o_ref[...] = a_ref[...] + b_ref[...]

    blk = pl.BlockSpec((tr, cols), lambda i: (i, 0))
    return pl.pallas_call(body, name=name, grid=(rows // tr,), in_specs=[blk, blk], out_specs=blk,
                          out_shape=jax.ShapeDtypeStruct(a.shape, a.dtype), compiler_params=_cparams("parallel"))(a, b)


def _chip_exchange(arrays, scatter, *, name):
    return _run_plan(_chip_exchange_plan(arrays, scatter), name)


def _chip_exchange_plan(arrays, scatter):
    n_arr = len(arrays)
    n_rel = len(_CHIP_RELATIONS)

    def local_copies(ins, outs, sems):
        me = 2 * lax.axis_index("x") + lax.axis_index("y")
        return [pltpu.make_async_copy(ins[a].at[me] if scatter[a] else ins[a], outs[a].at[me], sems[2].at[a])
                for a in range(n_arr)]

    def remote_copies(ins, outs, sems, arrivals):
        x, y, c = lax.axis_index("x"), lax.axis_index("y"), lax.axis_index("c")
        me = 2 * x + y
        out = []
        for k, (dx, dy, _) in enumerate(_CHIP_RELATIONS):
            px, py = x ^ dx, y ^ dy
            peer = 2 * px + py
            for a in range(n_arr):
                out.append(pltpu.make_async_remote_copy(
                    src_ref=ins[a].at[peer] if scatter[a] else ins[a], dst_ref=outs[a].at[peer if arrivals else me],
                    send_sem=sems[0].at[a * n_rel + k], recv_sem=sems[1].at[a * n_rel + k],
                    device_id=(x, y, c) if arrivals else (px, py, c), device_id_type=MESH))
        return out

    def start(ins, outs, sems):
        for cp in local_copies(ins, outs, sems) + remote_copies(ins, outs, sems, False):
            cp.start()

    def pass_on(ins, outs, sems):
        pass

    def finish(ins, outs, sems):
        for send in remote_copies(ins, outs, sems, False):
            send.wait_send()
        for arrival in remote_copies(ins, outs, sems, True):
            arrival.wait_recv()
        for cp in local_copies(ins, outs, sems):
            cp.wait()

    out_shape = [jax.ShapeDtypeStruct((N_CHIPS,) + (a.shape[1:] if s else a.shape), a.dtype)
                 for a, s in zip(arrays, scatter)]
    return _CommPlan(
        arrays, out_shape,
        [pltpu.SemaphoreType.DMA((n_arr * n_rel,)), pltpu.SemaphoreType.DMA((n_arr * n_rel,)),
         pltpu.SemaphoreType.DMA((n_arr,))],
        (start, pass_on, finish))


def _adamw(w, gslots, m, v, *, name):
    rows, cols = w.shape
    n_slots = gslots.shape[0]
    tr = _row_tile(rows, 128) if rows % 16 == 0 else rows

    def body(w_ref, g_ref, m_ref, v_ref, go_ref, d_ref, mo_ref, vo_ref):
        g = g_ref[0].astype(F32)
        for s in range(1, n_slots):
            g = g + g_ref[s].astype(F32)
        mn = ADAM_B1 * m_ref[...] + (1.0 - ADAM_B1) * g
        vn = ADAM_B2 * v_ref[...] + (1.0 - ADAM_B2) * (g * g)
        go_ref[...] = g
        mo_ref[...] = mn
        vo_ref[...] = vn
        m_hat = mn / (1.0 - ADAM_B1 ** ADAM_STEP)
        v_hat = vn / (1.0 - ADAM_B2 ** ADAM_STEP)
        d_ref[...] = -ADAM_LR * (m_hat / (jnp.sqrt(v_hat) + ADAM_EPS) + ADAM_WD * w_ref[...])

    blk = pl.BlockSpec((tr, cols), lambda i: (i, 0))
    shp = jax.ShapeDtypeStruct((rows, cols), F32)
    return pl.pallas_call(
        body, name=name, grid=(rows // tr,),
        in_specs=[blk, pl.BlockSpec((n_slots, tr, cols), lambda i: (0, i, 0)), blk, blk],
        out_specs=[blk] * 4, out_shape=[shp] * 4,
        compiler_params=_cparams("parallel"))(w, gslots, m, v)


_BIG = ("w_in", "w_ssd_branch", "w_attn_branch", "w_out", "w_ffn_in", "w_ffn_out")
_SMALL_SHARDED = ("meta_tokens", "ssd_conv_w", "gate_b", "ffn_conv_w")
_SMALL_REPLICATED = ("norm_mix_w", "ssd_conv_b", "ssd_dt_bias", "ssd_a_log", "ssd_d", "ssd_norm_w", "attn_sinks",
                     "rel_bias", "norm_ffn_w", "ffn_conv_b", "norm_final_w")
_WEIGHTS = ("meta_tokens", "norm_mix_w", "w_in", "ssd_conv_w", "ssd_conv_b", "ssd_dt_bias", "ssd_a_log", "ssd_d",
            "ssd_norm_w", "w_ssd_branch", "w_attn_branch", "attn_sinks", "rel_bias", "gate_b", "w_out", "norm_ffn_w",
            "w_ffn_in", "ffn_conv_w", "ffn_conv_b", "w_ffn_out", "norm_final_w")
_ROW_SHARDED = ("w_ssd_branch", "w_attn_branch", "w_out", "w_ffn_out")
_COL_SHARDED = ("w_in", "w_ffn_in", "meta_tokens", "ssd_conv_w", "gate_b", "ffn_conv_w")
_IN_SEGS = (("z", SSD_INNER), ("xbc", SSD_XBC), ("dt", SSD_HEADS), ("qkv", ATT_Q + 2 * ATT_KV), ("g", 2 * D_MODEL))


def _pack_rows(flat_parts, width, row_mult):
    flat = jnp.concatenate([p.reshape(-1) for p in flat_parts])
    pad = (-flat.shape[0]) % (width * row_mult)
    if pad:
        flat = jnp.concatenate([flat, jnp.zeros((pad,), flat.dtype)])
    return flat.reshape(-1, width)


def _unpack(flat, shapes):
    out, off = [], 0
    for shp in shapes:
        size = int(np.prod(shp))
        out.append(flat[off:off + size].reshape(shp))
        off += size
    return out


def _gather_full(stack, name, shard_shape):
    if name in _COL_SHARDED:
        return jnp.transpose(stack, (1, 0, 2)).reshape(shard_shape[0], N_DEV * shard_shape[1])
    return stack.reshape(N_DEV * shard_shape[0], shard_shape[1])


_IN_SEG_W = {"z": SSD_INNER, "xbc": SSD_XBC, "dt": DT_W, "qkv": ATT_Q + 2 * ATT_KV, "g": 2 * D_MODEL}
_IN_SHARD_W = (SSD_INNER + SSD_XBC + SSD_HEADS + ATT_Q + 2 * ATT_KV + 2 * D_MODEL) // N_DEV
_FFN_SHARD_W = 2 * D_FF // N_DEV


def _in_seg_runs():
    runs, off = [], 0
    for nm, width in _IN_SEGS:
        if nm == "dt":
            runs.append([(off + SSD_HPG * g, SSD_HPG, LANES * g) for g in range(SSD_GROUPS)])
        else:
            runs.append([(off, width, 0)])
        off += width
    return runs


def _w_in_to_segments(stack):
    pieces = [(seg, None, scol, 0, dev, col, w) for seg, scol, dev, col, w in _shard_pieces(_in_seg_runs(), _IN_SHARD_W)]
    outs = [((D_MODEL, _IN_SEG_W[nm]), stack.dtype) for nm, _ in _IN_SEGS]
    return dict(zip([nm for nm, _ in _IN_SEGS], _col_move([stack], outs, pieces, name="w_in_segments")))


def _segments_to_w_in_shards(seg_grads):
    pieces = [(0, dev, col, seg, None, scol, w) for seg, scol, dev, col, w in _shard_pieces(_in_seg_runs(), _IN_SHARD_W)]
    return _col_move(seg_grads, [((N_DEV, D_MODEL, _IN_SHARD_W), F32)], pieces, name="g_w_in_shards")[0]


def _ffn_in_from_shards(stack):
    pieces = [(0, None, scol, 0, dev, col, w)
              for _, scol, dev, col, w in _shard_pieces([[(0, 2 * D_FF, 0)]], _FFN_SHARD_W)]
    return _col_move([stack], [((D_MODEL, 2 * D_FF), stack.dtype)], pieces, name="w_ffn_in_full")[0]


def _ffn_in_to_shards(g_up, g_gate):
    pieces = [(0, dev, col, seg, None, scol, w)
              for seg, scol, dev, col, w in _shard_pieces([[(0, D_FF, 0)], [(D_FF, D_FF, 0)]], _FFN_SHARD_W)]
    return _col_move([g_up, g_gate], [((N_DEV, D_MODEL, _FFN_SHARD_W), F32)], pieces, name="g_w_ffn_in_shards")[0]


def _dt_spread(w_dt):
    k = w_dt.shape[0]
    w4 = w_dt.reshape(k, SSD_GROUPS, SSD_HPG)
    return jnp.pad(w4, ((0, 0), (0, 0), (0, LANES - SSD_HPG))).reshape(k, DT_W)


def _dt_gather(w_wide):
    k = w_wide.shape[0]
    return w_wide.reshape(k, SSD_GROUPS, LANES)[:, :, :SSD_HPG].reshape(k, SSD_HEADS)


class _LateExchanges:
    def __init__(self, two_d, shape2):
        self.two_d, self.shape2 = two_d, shape2
        self.early_grads_received = None

    def row_pack(self, tree):
        return jnp.concatenate([tree[k] for k in _ROW_SHARDED], axis=0)

    def late_weights_plan(self):
        return _gather_plan([self.two_d["w_ffn_in"].astype(BF16), self.row_pack(self.two_d).astype(BF16)])

    def late_weights(self, gathered):
        w_ffn_in_all, rows_all = [g.reshape((N_DEV,) + g.shape[2:]) for g in gathered]
        out = {"w_ffn_in": _ffn_in_from_shards(w_ffn_in_all)}
        off = 0
        for k in _ROW_SHARDED:
            r = self.shape2[k][0]
            out[k] = rows_all[:, off:off + r].reshape(N_DEV * r, D_MODEL)
            off += r
        return out

    def early_grads_plan(self, grads):
        rows_send = jnp.concatenate([grads[k].reshape(N_DEV, self.shape2[k][0], D_MODEL) for k in _ROW_SHARDED], axis=1)
        send = [_ffn_in_to_shards(*grads["w_ffn_in"]), rows_send]
        from_sib = _sibling_exchange(send, [True, True], name="early_grads_to_sibling")
        parts = [_pair_sum(mine, sib, name="pair_sum_" + nm, out_dtype=BF16)
                 for nm, mine, sib in zip(("w_ffn_in", "rows"), send, from_sib)]
        return _chip_exchange_plan(parts, [True, True])


def _local_step(x, target, w, exchanges=None):
    h0 = jnp.concatenate([jnp.zeros((PAD, D_MODEL), F32), w["meta_tokens"], x], axis=0)
    segs = w["in_segs"]

    dtb = _dt_spread(w["ssd_dt_bias"])
    alog = _dt_spread(w["ssd_a_log"])
    dskip_w = jnp.repeat(w["ssd_d"], SSD_HEADDIM, axis=1)
    sinks = jnp.pad(w["attn_sinks"], ((0, 0), (0, LANES - ATT_HEADS)))
    onehot_t = _onehot_t()
    tables = jnp.transpose(_bias_tables(w["rel_bias"].T, onehot_t).reshape(ATT_HEADS, 3, N_KEYS, BLOCK), (1, 0, 2, 3))

    u = _rms_fwd(h0, w["norm_mix_w"], name="rms_mix_fwd")
    z = _mm(u, segs["z"], name="in_z")
    xbc = _mm(u, segs["xbc"], name="in_xbc")
    dt_raw = _mm(u, segs["dt"], name="in_dt")
    qkv = _mm(u, segs["qkv"], name="in_qkv")
    gates = _mm(u, segs["g"], name="in_g")
    pre = _conv_fwd(xbc, w["ssd_conv_w"], w["ssd_conv_b"], name="ssd_conv_fwd")
    (y, yn, hsave), gathered = _ssd_fwd(pre, dt_raw, z, dtb, alog, dskip_w, w["ssd_norm_w"],
                                        side=None if exchanges is None else exchanges.late_weights_plan())
    if exchanges is not None:
        w = {**w, **exchanges.late_weights(gathered)}
    w_ffn_up, w_ffn_gate = w["w_ffn_in"][:, :D_FF], w["w_ffn_in"][:, D_FF:]
    y_ssd = _mm(yn, w["w_ssd_branch"], name="ssd_out")
    att = _attn_fwd(qkv, tables, sinks)
    y_att = _mm(att, w["w_attn_branch"], name="att_out")
    merged = _merge_fwd(gates, y_ssd, y_att, w["gate_b"])
    h1 = _mm(merged, w["w_out"], c=h0, mask=True, name="mix_out")
    u2 = _rms_fwd(h1, w["norm_ffn_w"], name="rms_ffn_fwd")
    hid_raw = _mm(u2, w["w_ffn_in"], name="ffn_in")
    hid_up, hid_gate, act = _ffn_act_fwd(hid_raw, w["ffn_conv_w"], w["ffn_conv_b"])
    h2 = _mm(act, w["w_ffn_out"], c=h1, mask=True, name="ffn_out")
    dh2, loss_row, g_norm_final = _final_loss(h2, w["norm_final_w"], target)

    grads = {"norm_final_w": g_norm_final}
    dact = _mm(dh2, w["w_ffn_out"], tb=True, mask=True, name="d_act")
    grads["w_ffn_out"] = _mm(act, dh2, ta=True, mask=True, name="g_w_ffn_out")
    dx_up, dx_gate, dcw_up, dcw_gate, dcb_up, dcb_gate = _ffn_act_bwd(dact, hid_up, hid_gate, hid_raw, w["ffn_conv_w"])
    grads["ffn_conv_w"] = jnp.concatenate([dcw_up, dcw_gate], axis=1)
    grads["ffn_conv_b"] = jnp.concatenate([dcb_up, dcb_gate], axis=1)
    du2 = _mm(dx_up, w_ffn_up, tb=True, name="d_u2_up")
    du2 = _mm(dx_gate, w_ffn_gate, tb=True, c=du2, name="d_u2_gate")
    grads["w_ffn_in"] = (_mm(u2, dx_up, ta=True, name="g_w_ffn_up"), _mm(u2, dx_gate, ta=True, name="g_w_ffn_gate"))
    dh1, grads["norm_ffn_w"] = _rms_bwd(h1, w["norm_ffn_w"], du2, dh2, name="rms_ffn_bwd")

    dmerged = _mm(dh1, w["w_out"], tb=True, mask=True, name="d_merged")
    grads["w_out"] = _mm(merged, dh1, ta=True, mask=True, name="g_w_out")
    dy_ssd, dy_att, dgates, grads["gate_b"] = _merge_bwd(dmerged, gates, y_ssd, y_att, w["gate_b"])
    dyn = _mm(dy_ssd, w["w_ssd_branch"], tb=True, name="d_yn")
    grads["w_ssd_branch"] = _mm(yn, dy_ssd, ta=True, name="g_w_ssd")
    datt = _mm(dy_att, w["w_attn_branch"], tb=True, name="d_att")
    grads["w_attn_branch"] = _mm(att, dy_att, ta=True, name="g_w_att")
    (dz, dpxs, dpb, dpc, ddt, grads["ssd_norm_w"], g_dtb, g_alog, g_dskip), received = _ssd_bwd(
        dyn, y, z, pre, dt_raw, hsave, dtb, alog, dskip_w, w["ssd_norm_w"],
        side=None if exchanges is None else exchanges.early_grads_plan(grads))
    if exchanges is not None:
        exchanges.early_grads_received = received
    grads["ssd_dt_bias"] = _dt_gather(g_dtb)
    grads["ssd_a_log"] = _dt_gather(g_alog)
    grads["ssd_d"] = _dt_gather(g_dskip)
    dpre = jnp.concatenate([dpxs, dpb, dpc], axis=1)
    dxbc, grads["ssd_conv_w"], grads["ssd_conv_b"] = _conv_bwd(dpre, xbc, w["ssd_conv_w"], name="ssd_conv_bwd")
    dqkv, d_tables, d_sinks = _attn_bwd(datt, qkv, tables, sinks)
    grads["attn_sinks"] = d_sinks[:, :ATT_HEADS]
    dtab = jnp.transpose(d_tables, (1, 0, 2, 3)).reshape(ATT_HEADS, NT_ALL)
    grads["rel_bias"] = _bias_grad(dtab, onehot_t).T
    dsegs = {"z": dz, "xbc": dxbc, "dt": ddt, "qkv": dqkv, "g": dgates}
    du, g_in = None, []
    for nm, _ in _IN_SEGS:
        du = _mm(dsegs[nm], segs[nm], tb=True, c=du, name="d_u_" + nm)
        g_in.append(_mm(u, dsegs[nm], ta=True, name="g_w_in_" + nm))
    grads["in_segs"] = g_in
    dh0, grads["norm_mix_w"] = _rms_bwd(h0, w["norm_mix_w"], du, dh1, name="rms_mix_bwd")
    grads["meta_tokens"] = dh0[PAD:BLOCK]
    return loss_row[0, 0], dh0[BLOCK:], grads


def kernel(x, meta_tokens, norm_mix_w, w_in, ssd_conv_w, ssd_conv_b, ssd_dt_bias, ssd_a_log, ssd_d, ssd_norm_w, w_ssd_branch, w_attn_branch, attn_sinks, rel_bias, gate_b, w_out, norm_ffn_w, w_ffn_in, ffn_conv_w, ffn_conv_b, w_ffn_out, norm_final_w, loss_target, m_meta_tokens, m_norm_mix_w, m_w_in, m_ssd_conv_w, m_ssd_conv_b, m_ssd_dt_bias, m_ssd_a_log, m_ssd_d, m_ssd_norm_w, m_w_ssd_branch, m_w_attn_branch, m_attn_sinks, m_rel_bias, m_gate_b, m_w_out, m_norm_ffn_w, m_w_ffn_in, m_ffn_conv_w, m_ffn_conv_b, m_w_ffn_out, m_norm_final_w, v_meta_tokens, v_norm_mix_w, v_w_in, v_ssd_conv_w, v_ssd_conv_b, v_ssd_dt_bias, v_ssd_a_log, v_ssd_d, v_ssd_norm_w, v_w_ssd_branch, v_w_attn_branch, v_attn_sinks, v_rel_bias, v_gate_b, v_w_out, v_norm_ffn_w, v_w_ffn_in, v_ffn_conv_w, v_ffn_conv_b, v_w_ffn_out, v_norm_final_w):
    shard = dict(meta_tokens=meta_tokens, norm_mix_w=norm_mix_w, w_in=w_in, ssd_conv_w=ssd_conv_w,
                 ssd_conv_b=ssd_conv_b, ssd_dt_bias=ssd_dt_bias, ssd_a_log=ssd_a_log, ssd_d=ssd_d,
                 ssd_norm_w=ssd_norm_w, w_ssd_branch=w_ssd_branch, w_attn_branch=w_attn_branch,
                 attn_sinks=attn_sinks, rel_bias=rel_bias, gate_b=gate_b, w_out=w_out, norm_ffn_w=norm_ffn_w,
                 w_ffn_in=w_ffn_in, ffn_conv_w=ffn_conv_w, ffn_conv_b=ffn_conv_b, w_ffn_out=w_ffn_out,
                 norm_final_w=norm_final_w)
    mom_m = dict(zip(_WEIGHTS, (m_meta_tokens, m_norm_mix_w, m_w_in, m_ssd_conv_w, m_ssd_conv_b, m_ssd_dt_bias,
                                m_ssd_a_log, m_ssd_d, m_ssd_norm_w, m_w_ssd_branch, m_w_attn_branch, m_attn_sinks,
                                m_rel_bias, m_gate_b, m_w_out, m_norm_ffn_w, m_w_ffn_in, m_ffn_conv_w, m_ffn_conv_b,
                                m_w_ffn_out, m_norm_final_w)))
    mom_v = dict(zip(_WEIGHTS, (v_meta_tokens, v_norm_mix_w, v_w_in, v_ssd_conv_w, v_ssd_conv_b, v_ssd_dt_bias,
                                v_ssd_a_log, v_ssd_d, v_ssd_norm_w, v_w_ssd_branch, v_w_attn_branch, v_attn_sinks,
                                v_rel_bias, v_gate_b, v_w_out, v_norm_ffn_w, v_w_ffn_in, v_ffn_conv_w, v_ffn_conv_b,
                                v_w_ffn_out, v_norm_final_w)))
    orig_shape = {k: a.shape for k, a in shard.items()}
    two_d = {k: a.reshape(a.shape[-2:]) if a.ndim >= 2 else a.reshape(1, -1) for k, a in shard.items()}
    shape2 = {k: a.shape for k, a in two_d.items()}

    def as2d(tree):
        return {k: tree[k].reshape(shape2[k]) for k in _WEIGHTS}

    mom_m, mom_v = as2d(mom_m), as2d(mom_v)

    exchanges = _LateExchanges(two_d, shape2)
    row_pack = exchanges.row_pack
    small_pack = _pack_rows([two_d[k] for k in _SMALL_SHARDED], LANES, SMALL_ROW_MULT)
    w_in_all, small_all = _gather_two_level([two_d["w_in"].astype(BF16), small_pack], name="gather_weights")
    full = {k: two_d[k] for k in _SMALL_REPLICATED}
    full["in_segs"] = _w_in_to_segments(w_in_all)
    small_flat = small_all.reshape(N_DEV, -1)
    off = 0
    for k in _SMALL_SHARDED:
        size = int(np.prod(shape2[k]))
        full[k] = _gather_full(small_flat[:, off:off + size].reshape((N_DEV,) + shape2[k]), k, shape2[k])
        off += size

    loss_local, grad_x, grads = _local_step(x[0], loss_target[0], full, exchanges)

    small_names = _SMALL_SHARDED + _SMALL_REPLICATED
    small_send = _pack_rows([grads[k] for k in small_names] + [loss_local.reshape(1)], LANES, SMALL_ROW_MULT)
    in_send = _segments_to_w_in_shards(grads["in_segs"])
    from_sib = _sibling_exchange([in_send, small_send], [True, False], name="grads_to_sibling")
    parts = [_pair_sum(in_send, from_sib[0], name="pair_sum_w_in", out_dtype=BF16),
             _add(small_send, from_sib[1], name="pair_sum_small")]
    in_recv, small_recv = _chip_exchange(parts, [True, False], name="exchange_grads")
    ffn_recv, rows_recv = exchanges.early_grads_received

    big = {"w_in": _adamw(two_d["w_in"], in_recv, mom_m["w_in"], mom_v["w_in"], name="adamw_w_in"),
           "w_ffn_in": _adamw(two_d["w_ffn_in"], ffn_recv, mom_m["w_ffn_in"], mom_v["w_ffn_in"], name="adamw_w_ffn_in")}
    rows_out = _adamw(row_pack(two_d), rows_recv, row_pack(mom_m), row_pack(mom_v), name="adamw_rows")
    off = 0
    for k in _ROW_SHARDED:
        r = shape2[k][0]
        big[k] = [a[off:off + r] for a in rows_out]
        off += r
    me =4 * lax.axis_index("x") + 2 * lax.axis_index("y") + lax.axis_index("c")
    small_full_shapes = [grads[k].shape for k in small_names]
    n_small = sum(int(np.prod(s)) for s in small_full_shapes)

    def packed_small(tree):
        parts = []
        for k in small_names:
            a = tree[k]
            if k in _SMALL_SHARDED:
                fullw = jnp.zeros(grads[k].shape, F32)
                a = lax.dynamic_update_slice(fullw, a, (0, me * a.shape[1]))
            parts.append(a)
        return _pack_rows(parts + [jnp.zeros((1,), F32)], LANES, SMALL_ROW_MULT)

    g_small, d_small, m_small, v_small = _adamw(packed_small(two_d), small_recv, packed_small(mom_m),
                                                packed_small(mom_v), name="adamw_small")

    def unpack_all(which, small):
        out = {k: big[k][which] for k in _BIG}
        flat = small.reshape(-1)
        for k, a in zip(small_names, _unpack(flat, small_full_shapes)):
            if k in _SMALL_SHARDED:
                a = lax.dynamic_slice(a, (0, me * shape2[k][1]), shape2[k])
            out[k] = a
        return out, flat[n_small]

    g_all, loss = unpack_all(0, g_small)
    d_all, _ = unpack_all(1, d_small)
    m_all, _ = unpack_all(2, m_small)
    v_all, _ = unpack_all(3, v_small)

    def final(tree):
        return [tree[k].reshape(orig_shape[k]) for k in _WEIGHTS]

    return (loss, grad_x[None], *final(g_all), *final(d_all), *final(m_all), *final(v_all))
```

```python
import functools
import math

import numpy as np
import jax
import jax.numpy as jnp
from jax import lax
from jax.experimental import pallas as pl
from jax.experimental.pallas import tpu as pltpu

F32 = jnp.float32
BF16 = jnp.bfloat16
HIGHEST = lax.Precision.HIGHEST

D_MODEL = 1024
N_META = 16
BLOCK = 128
PAD = BLOCK - N_META
EPS = 1e-6
NEG = -1e30
SSD_INNER = 2 * D_MODEL
SSD_HEADDIM = 64
SSD_HEADS = SSD_INNER // SSD_HEADDIM
SSD_GROUPS = 4
SSD_HPG = SSD_HEADS // SSD_GROUPS
SSD_STATE = 128
SSD_CONV = 4
SSD_GW = SSD_HPG * SSD_HEADDIM
SSD_BC = SSD_GROUPS * SSD_STATE
SSD_XBC = SSD_INNER + 2 * SSD_BC
ATT_HEADS = 16
ATT_KV_HEADS = 2
ATT_HEADDIM = 64
ATT_GQ = ATT_HEADS // ATT_KV_HEADS
ATT_Q = ATT_HEADS * ATT_HEADDIM
ATT_KV = ATT_KV_HEADS * ATT_HEADDIM
REL_BUCKETS = 32
REL_MAX_DIST = 128
D_FF = 2816
FFN_CONV = 3
ADAM_LR = 0.001
ADAM_B1 = 0.9
ADAM_B2 = 0.999
ADAM_EPS = 1e-08
ADAM_WD = 0.01
ADAM_STEP = 10

N_DEV = 8
LANES = 128
SUBLANES = 8
DT_W = SSD_GROUPS * LANES
VMEM_LIMIT_BYTES = 56 * 1024 * 1024
MESH = pl.DeviceIdType.MESH

SMALL_ROW_MULT = 16

N_KEYS = 3 * BLOCK
NT_ALL = 3 * N_KEYS * BLOCK
NT_TILE = 8192


def _cparams(*sem):
    return pltpu.CompilerParams(dimension_semantics=sem, vmem_limit_bytes=VMEM_LIMIT_BYTES)


def _row_tile(n, cap):
    best = None
    for t in range(16, min(n, cap) + 1, 16):
        if n % t == 0:
            best = t
    return best or n


def _col_tile(n, cap):
    for t in (1408, 1280, 1024, 768, 640, 512, 384, 256, 128):
        if t <= cap and n % t == 0:
            return t
    return n


def _sigmoid(x):
    return 0.5 * jnp.tanh(0.5 * x) + 0.5


def _silu(x):
    return x * _sigmoid(x)


def _softplus(x):
    return jnp.maximum(x, 0.0) + jnp.log(1.0 + jnp.exp(-jnp.abs(x)))


def _dot_nt(a, b):
    return lax.dot_general(a, b, (((1,), (1,)), ((), ())), preferred_element_type=F32)


def _dot_tn(a, b):
    return lax.dot_general(a, b, (((0,), (0,)), ((), ())), preferred_element_type=F32)


def _dot(a, b):
    return jnp.dot(a, b, preferred_element_type=F32)


def _bf16_terms(x, terms):
    out, rest = [], x
    for _ in range(terms):
        part = rest.astype(BF16)
        out.append(part)
        rest = rest - part.astype(F32)
    return out


def _dot_sel(x, sel, terms=3):
    return sum(_dot(part, sel) for part in _bf16_terms(x, terms))


def _sel_dot(sel, x, terms=3):
    return sum(_dot(sel, part) for part in _bf16_terms(x, terms))


def _sum_all(x):
    return jnp.sum(jnp.sum(x, axis=1, keepdims=True), axis=0, keepdims=True)


MM_ROW_CAPS = (1664, 832, 416)
MM_COL_CAP = 1408
MM_VMEM_BUDGET = 44 * 1024 * 1024


def _mm_tiles(rows, cols, vmem_bytes):
    col_cands = [t for t in (1408, 1280, 1024, 768, 640, 512, 384, 256, 128) if cols % t == 0]
    if cols <= 2 * MM_COL_CAP:
        col_cands.append(cols)
    best = None
    for cap in MM_ROW_CAPS:
        tr = _row_tile(rows, cap)
        for tc in col_cands:
            if vmem_bytes(tr, tc) <= MM_VMEM_BUDGET and (best is None or tr * tc > best[0] * best[1]):
                best = (tr, tc)
    assert best is not None, (rows, cols)
    return best


def _mm(a, b, *, name, ta=False, tb=False, c=None, mask=False, out_dtype=F32):
    if not ta:
        m, k = a.shape
        n = b.shape[0] if tb else b.shape[1]
        tm, tn = _mm_tiles(m, n, lambda t_m, t_n: 2 * (t_m * k * a.dtype.itemsize + k * t_n * b.dtype.itemsize
                                                       + t_m * t_n * (jnp.dtype(out_dtype).itemsize
                                                                      + (0 if c is None else c.dtype.itemsize)))
                           + 4 * t_m * t_n)

        def body(*refs):
            if c is None:
                a_ref, b_ref, o_ref = refs
            else:
                a_ref, b_ref, c_ref, o_ref = refs
            acc = (_dot_nt if tb else _dot)(a_ref[...].astype(BF16), b_ref[...].astype(BF16))
            if mask:
                row = pl.program_id(0) * tm + lax.broadcasted_iota(jnp.int32, (tm, 1), 0)
                acc = jnp.where(row >= PAD, acc, 0.0)
            if c is not None:
                acc = acc + c_ref[...]
            o_ref[...] = acc.astype(out_dtype)

        b_spec = pl.BlockSpec((tn, k), lambda i, j: (j, 0)) if tb else pl.BlockSpec((k, tn), lambda i, j: (0, j))
        in_specs = [pl.BlockSpec((tm, k), lambda i, j: (i, 0)), b_spec]
        args = [a, b]
        if c is not None:
            in_specs.append(pl.BlockSpec((tm, tn), lambda i, j: (i, j)))
            args.append(c)
        return pl.pallas_call(
            body, name=name, grid=(m // tm, n // tn), in_specs=in_specs,
            out_specs=pl.BlockSpec((tm, tn), lambda i, j: (i, j)),
            out_shape=jax.ShapeDtypeStruct((m, n), out_dtype),
            compiler_params=_cparams("parallel", "parallel"))(*args)

    kc, m = a.shape
    n = b.shape[1]
    tm = _col_tile(m, MM_COL_CAP)
    tk, tn = _mm_tiles(kc, n, lambda t_k, t_n: 2 * (t_k * tm * a.dtype.itemsize + t_k * t_n * b.dtype.itemsize
                                                    + 4 * tm * t_n) + 4 * tm * t_n)

    def body_t(a_ref, b_ref, o_ref):
        kk = pl.program_id(2)
        bb = b_ref[...]
        if mask:
            row = kk * tk + lax.broadcasted_iota(jnp.int32, (tk, 1), 0)
            bb = jnp.where(row >= PAD, bb, jnp.zeros_like(bb))
        p = _dot_tn(a_ref[...].astype(BF16), bb.astype(BF16))

        @pl.when(kk == 0)
        def _():
            o_ref[...] = p

        @pl.when(kk > 0)
        def _():
            o_ref[...] += p

    return pl.pallas_call(
        body_t, name=name, grid=(m // tm, n // tn, kc // tk),
        in_specs=[pl.BlockSpec((tk, tm), lambda i, j, kk: (kk, i)), pl.BlockSpec((tk, tn), lambda i, j, kk: (kk, j))],
        out_specs=pl.BlockSpec((tm, tn), lambda i, j, kk: (i, j)),
        out_shape=jax.ShapeDtypeStruct((m, n), F32),
        compiler_params=_cparams("parallel", "parallel", "arbitrary"))(a, b)


def _mm_rms_bwd(a, b, c, x, w, dres, *, name):
    m, k = a.shape
    d = b.shape[0]
    tm = _row_tile(m, 416)

    def body(a_ref, b_ref, c_ref, x_ref, w_ref, dres_ref, dx_ref, dw_ref):
        i = pl.program_id(0)
        dyv = _dot_nt(a_ref[...].astype(BF16), b_ref[...].astype(BF16)) + c_ref[...]
        xv = x_ref[...]
        r = lax.rsqrt(jnp.mean(xv * xv, axis=-1, keepdims=True) + EPS)
        xh = xv * r
        g = dyv * w_ref[...]
        dx_ref[...] = r * (g - xh * jnp.mean(g * xh, axis=-1, keepdims=True)) + dres_ref[...]
        part = jnp.sum(dyv * xh, axis=0, keepdims=True)

        @pl.when(i == 0)
        def _():
            dw_ref[...] = part

        @pl.when(i > 0)
        def _():
            dw_ref[...] += part

    row = pl.BlockSpec((tm, d), lambda i: (i, 0))
    vec = pl.BlockSpec((1, d), lambda i: (0, 0))
    return pl.pallas_call(
        body, name=name, grid=(m // tm,),
        in_specs=[pl.BlockSpec((tm, k), lambda i: (i, 0)), pl.BlockSpec((d, k), lambda i: (0, 0)), row, row, vec, row],
        out_specs=[row, vec],
        out_shape=[jax.ShapeDtypeStruct((m, d), F32), jax.ShapeDtypeStruct((1, d), F32)],
        compiler_params=_cparams("arbitrary"))(a, b, c, x, w, dres)


def _rms_fwd(h, w, *, name):
    n, d = h.shape
    tm = _row_tile(n, 832)

    def body(h_ref, w_ref, o_ref):
        x = h_ref[...]
        r = lax.rsqrt(jnp.mean(x * x, axis=-1, keepdims=True) + EPS)
        o_ref[...] = (x * r * w_ref[...]).astype(BF16)

    return pl.pallas_call(
        body, name=name, grid=(n // tm,),
        in_specs=[pl.BlockSpec((tm, d), lambda i: (i, 0)), pl.BlockSpec((1, d), lambda i: (0, 0))],
        out_specs=pl.BlockSpec((tm, d), lambda i: (i, 0)),
        out_shape=jax.ShapeDtypeStruct((n, d), BF16),
        compiler_params=_cparams("parallel"))(h, w)


def _final_loss(h, w, target):
    n, d = h.shape
    nb = n // BLOCK

    def body(h_ref, w_ref, t_ref, dh_ref, loss_ref, dw_ref):
        i = pl.program_id(0)
        xv = h_ref[...]
        r = lax.rsqrt(jnp.mean(xv * xv, axis=-1, keepdims=True) + EPS)
        xh = xv * r
        wv = w_ref[...]
        err = jnp.where(i >= 1, xh * wv - t_ref[...], 0.0)
        dyv = err * (1.0 / d)
        g = dyv * wv
        dh_ref[...] = r * (g - xh * jnp.mean(g * xh, axis=-1, keepdims=True))
        lpart = jnp.broadcast_to(0.5 * _sum_all(err * err) * (1.0 / d), (1, LANES))
        wpart = jnp.sum(dyv * xh, axis=0, keepdims=True)

        @pl.when(i == 0)
        def _():
            loss_ref[...] = lpart
            dw_ref[...] = wpart

        @pl.when(i > 0)
        def _():
            loss_ref[...] += lpart
            dw_ref[...] += wpart

    row = pl.BlockSpec((BLOCK, d), lambda i: (i, 0))
    vec = pl.BlockSpec((1, d), lambda i: (0, 0))
    return pl.pallas_call(
        body, name="final_loss", grid=(nb,),
        in_specs=[row, vec, pl.BlockSpec((BLOCK, d), lambda i: (jnp.maximum(i - 1, 0), 0))],
        out_specs=[row, pl.BlockSpec((1, LANES), lambda i: (0, 0)), vec],
        out_shape=[jax.ShapeDtypeStruct((n, d), F32), jax.ShapeDtypeStruct((1, LANES), F32),
                   jax.ShapeDtypeStruct((1, d), F32)],
        compiler_params=_cparams("arbitrary"))(h, w, target)


def _main_spec(tm, cb, off=0):
    return pl.BlockSpec((tm, cb), lambda j, i: (i, j + off))


def _prev_spec(tm, cb, off=0):
    r8 = tm // SUBLANES
    return pl.BlockSpec((SUBLANES, cb), lambda j, i: (jnp.maximum(i * r8 - 1, 0), j + off))


def _next_spec(tm, cb, n_rows, off=0):
    r8 = tm // SUBLANES
    last = n_rows // SUBLANES - 1
    return pl.BlockSpec((SUBLANES, cb), lambda j, i: (jnp.minimum((i + 1) * r8, last), j + off))


def _with_prev(prev_ref, main_ref, i):
    prev = jnp.where(i > 0, prev_ref[...], 0.0)
    return jnp.concatenate([prev, main_ref[...]], axis=0)


def _with_next(main, nxt, i, n_tiles):
    return jnp.concatenate([main, jnp.where(i < n_tiles - 1, nxt, 0.0)], axis=0)


def _back(xx, s, tm):
    if s == 0:
        return xx[SUBLANES:SUBLANES + tm]
    return pltpu.roll(xx, s, 0)[SUBLANES:SUBLANES + tm]


def _ahead(xx, s, tm):
    if s == 0:
        return xx[:tm]
    return pltpu.roll(xx, tm + SUBLANES - s, 0)[:tm]


def _conv_fwd(x, w, b, *, name):
    n, cdim = x.shape
    kw = w.shape[0]
    tm = _row_tile(n, 832)
    cb = _col_tile(cdim, 512)

    def body(xp_ref, x_ref, w_ref, b_ref, o_ref):
        xx = _with_prev(xp_ref, x_ref, pl.program_id(1))
        acc = jnp.broadcast_to(b_ref[...], (tm, cb))
        for k in range(kw):
            acc = acc + w_ref[k:k + 1, :] * _back(xx, kw - 1 - k, tm)
        o_ref[...] = acc

    return pl.pallas_call(
        body, name=name, grid=(cdim // cb, n // tm),
        in_specs=[_prev_spec(tm, cb), _main_spec(tm, cb), pl.BlockSpec((kw, cb), lambda j, i: (0, j)),
                  pl.BlockSpec((1, cb), lambda j, i: (0, j))],
        out_specs=_main_spec(tm, cb),
        out_shape=jax.ShapeDtypeStruct((n, cdim), F32),
        compiler_params=_cparams("parallel", "parallel"))(x, x, w, b)


def _conv_bwd_core(dpre_ext, x, w_ref, kw, tm):
    dx = None
    dws = []
    for k in range(kw):
        shifted = _ahead(dpre_ext, kw - 1 - k, tm)
        term = w_ref[k:k + 1, :] * shifted
        dx = term if dx is None else dx + term
        dws.append(jnp.sum(shifted * x, axis=0, keepdims=True))
    return dx, dws, jnp.sum(dpre_ext[:tm], axis=0, keepdims=True)


def _acc_rows(i, dw_ref, db_ref, dws, db):
    @pl.when(i == 0)
    def _():
        for k, v in enumerate(dws):
            dw_ref[k:k + 1, :] = v
        db_ref[...] = db

    @pl.when(i > 0)
    def _():
        for k, v in enumerate(dws):
            dw_ref[k:k + 1, :] += v
        db_ref[...] += db


def _conv_bwd(dpre, x, w, *, name, col0=0, into=None):
    n, cdim = x.shape
    kw = w.shape[0]
    tm = _row_tile(n, 832)
    cb = _col_tile(cdim, 512)
    nt = n // tm
    off = col0 // cb
    n_alias = 0 if into is None else 3

    def body(d_ref, dn_ref, x_ref, w_ref, *rest):
        dx_ref, dw_ref, db_ref = rest[n_alias:]
        i = pl.program_id(1)
        dpre_ext = _with_next(d_ref[...], dn_ref[...], i, nt)
        dx, dws, db = _conv_bwd_core(dpre_ext, x_ref[...], w_ref, kw, tm)
        dx_ref[...] = dx.astype(BF16)
        _acc_rows(i, dw_ref, db_ref, dws, db)

    wspec = pl.BlockSpec((kw, cb), lambda j, i: (0, j + off))
    bspec = pl.BlockSpec((1, cb), lambda j, i: (0, j + off))
    return pl.pallas_call(
        body, name=name, grid=(dpre.shape[1] // cb, nt),
        in_specs=[_main_spec(tm, cb), _next_spec(tm, cb, n), _main_spec(tm, cb, off), wspec]
        + [pl.BlockSpec(memory_space=pl.ANY)] * n_alias,
        out_specs=[_main_spec(tm, cb, off), wspec, bspec],
        out_shape=[jax.ShapeDtypeStruct((n, cdim), BF16), jax.ShapeDtypeStruct((kw, cdim), F32),
                   jax.ShapeDtypeStruct((1, cdim), F32)],
        input_output_aliases={4 + k: k for k in range(n_alias)},
        compiler_params=_cparams("parallel", "arbitrary"))(dpre, dpre, x, w, *(into or ()))


def _ffn_act_fwd(x, w, b):
    n = x.shape[0]
    kw = w.shape[0]
    tm = _row_tile(n, 832)
    cb = _col_tile(D_FF, 256)
    nc = D_FF // cb

    def body(xpu_ref, xu_ref, xpg_ref, xg_ref, wu_ref, wg_ref, bu_ref, bg_ref, hu_ref, hg_ref, act_ref):
        i = pl.program_id(1)
        outs = []
        for xp_ref, x_ref, w_ref, b_ref in ((xpu_ref, xu_ref, wu_ref, bu_ref), (xpg_ref, xg_ref, wg_ref, bg_ref)):
            xx = _with_prev(xp_ref, x_ref, i)
            acc = jnp.broadcast_to(b_ref[...], (tm, cb))
            for k in range(kw):
                acc = acc + w_ref[k:k + 1, :] * _back(xx, kw - 1 - k, tm)
            outs.append(acc)
        hu_ref[...] = outs[0]
        hg_ref[...] = outs[1]
        act_ref[...] = (_silu(outs[1]) * outs[0]).astype(BF16)

    def wspec(off):
        return pl.BlockSpec((kw, cb), lambda j, i: (0, j + off))

    def bspec(off):
        return pl.BlockSpec((1, cb), lambda j, i: (0, j + off))

    out = _main_spec(tm, cb)
    return pl.pallas_call(
        body, name="ffn_act_fwd", grid=(nc, n // tm),
        in_specs=[_prev_spec(tm, cb), _main_spec(tm, cb), _prev_spec(tm, cb, nc), _main_spec(tm, cb, nc),
                  wspec(0), wspec(nc), bspec(0), bspec(nc)],
        out_specs=[out, out, out],
        out_shape=[jax.ShapeDtypeStruct((n, D_FF), F32), jax.ShapeDtypeStruct((n, D_FF), F32),
                   jax.ShapeDtypeStruct((n, D_FF), BF16)],
        compiler_params=_cparams("parallel", "parallel"))(x, x, x, x, w, w, b, b)


def _ffn_act_bwd(dact, hu, hg, x, w):
    n = x.shape[0]
    kw = w.shape[0]
    tm = _row_tile(n, 832)
    cb = _col_tile(D_FF, 256)
    nc = D_FF // cb
    nt = n // tm

    def body(d_ref, dn_ref, hu_ref, hun_ref, hg_ref, hgn_ref, xu_ref, xg_ref, wu_ref, wg_ref,
             dxu_ref, dxg_ref, dwu_ref, dwg_ref, dbu_ref, dbg_ref):
        i = pl.program_id(1)
        dact_e = _with_next(d_ref[...], dn_ref[...], i, nt)
        up_e = _with_next(hu_ref[...], hun_ref[...], i, nt)
        gate_e = _with_next(hg_ref[...], hgn_ref[...], i, nt)
        sg = _sigmoid(gate_e)
        dup_e = dact_e * (gate_e * sg)
        dgate_e = dact_e * up_e * (sg * (1.0 + gate_e * (1.0 - sg)))
        dx, dws, db = _conv_bwd_core(dup_e, xu_ref[...], wu_ref, kw, tm)
        dxu_ref[...] = dx.astype(BF16)
        _acc_rows(i, dwu_ref, dbu_ref, dws, db)
        dx, dws, db = _conv_bwd_core(dgate_e, xg_ref[...], wg_ref, kw, tm)
        dxg_ref[...] = dx.astype(BF16)
        _acc_rows(i, dwg_ref, dbg_ref, dws, db)

    main, nxt = _main_spec(tm, cb), _next_spec(tm, cb, n)
    wspec0 = pl.BlockSpec((kw, cb), lambda j, i: (0, j))
    wspec1 = pl.BlockSpec((kw, cb), lambda j, i: (0, j + nc))
    bspec = pl.BlockSpec((1, cb), lambda j, i: (0, j))
    return pl.pallas_call(
        body, name="ffn_act_bwd", grid=(nc, nt),
        in_specs=[main, nxt, main, nxt, main, nxt, _main_spec(tm, cb), _main_spec(tm, cb, nc), wspec0, wspec1],
        out_specs=[main, main, wspec0, wspec0, bspec, bspec],
        out_shape=[jax.ShapeDtypeStruct((n, D_FF), BF16), jax.ShapeDtypeStruct((n, D_FF), BF16),
                   jax.ShapeDtypeStruct((kw, D_FF), F32), jax.ShapeDtypeStruct((kw, D_FF), F32),
                   jax.ShapeDtypeStruct((1, D_FF), F32), jax.ShapeDtypeStruct((1, D_FF), F32)],
        compiler_params=_cparams("parallel", "arbitrary"))(dact, dact, hu, hu, hg, hg, x, x, w, w)


def _ssd_prep(pxs_ref, pb_ref, pc_ref, dtr_ref, dtb_ref, alog_ref, c):
    xs = _silu(pxs_ref[...])
    bm = _silu(pb_ref[...])
    cm = _silu(pc_ref[...])
    return (xs, bm, cm) + _ssd_decay(dtr_ref, dtb_ref, alog_ref, c)


def _ssd_decay(dtr_ref, dtb_ref, alog_ref, c):
    row =lax.broadcasted_iota(jnp.int32, (BLOCK, 1), 0) + c * BLOCK
    valid = (row >= PAD).astype(F32)
    dtr = dtr_ref[...] + dtb_ref[...]
    dt = _softplus(dtr) * valid
    a = -jnp.exp(alog_ref[...])
    lam = dt * a
    ri = lax.broadcasted_iota(jnp.int32, (BLOCK, BLOCK), 0)
    ci = lax.broadcasted_iota(jnp.int32, (BLOCK, BLOCK), 1)
    causal = ci <= ri
    cs = _sel_dot(causal.astype(BF16), lam)
    return valid, dtr, dt, a, lam, cs, causal


def _head_cols(r):
    return slice(SSD_HEADDIM * r, SSD_HEADDIM * (r + 1))


def _ssd_specs(nc, rev):
    def cidx(c):
        return nc - 1 - c if rev else c

    xs = pl.BlockSpec((BLOCK, SSD_GW), lambda g, c: (cidx(c), g))
    bspec = pl.BlockSpec((BLOCK, SSD_STATE), lambda g, c: (cidx(c), SSD_INNER // SSD_STATE + g))
    cspec = pl.BlockSpec((BLOCK, SSD_STATE), lambda g, c: (cidx(c), (SSD_INNER + SSD_BC) // SSD_STATE + g))
    lane = pl.BlockSpec((BLOCK, LANES), lambda g, c: (cidx(c), g))
    vec = pl.BlockSpec((1, LANES), lambda g, c: (0, g))
    wide_vec = pl.BlockSpec((1, SSD_GW), lambda g, c: (0, g))
    hsave = pl.BlockSpec((1, 1, SSD_GW, SSD_STATE), lambda g, c: (cidx(c), g, 0, 0))
    return xs, bspec, cspec, lane, vec, wide_vec, hsave


def _head_spread_matrix():
    r = lax.broadcasted_iota(jnp.int32, (LANES, SSD_GW), 0)
    col = lax.broadcasted_iota(jnp.int32, (LANES, SSD_GW), 1)
    return (col // SSD_HEADDIM == r).astype(BF16)


def _const_spec(shape):
    return pl.BlockSpec(shape, lambda g, c: (0,) * len(shape))


def _spread_heads(per_head, e_ref):
    wide = _dot_sel(jnp.concatenate(per_head, axis=0), e_ref[...])
    return [wide[BLOCK * k:BLOCK * (k + 1)] for k in range(len(per_head))]


def _call_with_side(body, side, *, name, grid, in_specs, out_specs, out_shape, scratch_shapes, args):
    if side is None:
        outs = pl.pallas_call(body, name=name, grid=grid, in_specs=in_specs, out_specs=out_specs, out_shape=out_shape,
                              scratch_shapes=scratch_shapes, compiler_params=_cparams("parallel", "arbitrary"))(*args)
        return outs, []
    n_in, n_out, n_scr, n_side = len(in_specs), len(out_specs), len(scratch_shapes), len(side.arrays)

    def body_with_side(*refs):
        ins, rest = refs[:n_in + n_side], refs[n_in + n_side:]
        outs, scratch = rest[:n_out + n_side], rest[n_out + n_side:]
        side_refs = (ins[n_in:], outs[n_out:], scratch[n_scr:])
        g, c = pl.program_id(0), pl.program_id(1)

        @pl.when((g == 0) & (c == 0))
        def _():
            side.phases[0](*side_refs)

        body(*ins[:n_in], *outs[:n_out], *scratch[:n_scr])

        @pl.when((g == grid[0] // 2) & (c == 0))
        def _():
            side.phases[1](*side_refs)

        @pl.when((g == grid[0] - 1) & (c == grid[1] - 1))
        def _():
            side.phases[2](*side_refs)

    any_spec = pl.BlockSpec(memory_space=pl.ANY)
    outs = pl.pallas_call(
        body_with_side, name=name, grid=grid, in_specs=list(in_specs) + [any_spec] * n_side,
        out_specs=list(out_specs) + [any_spec] * n_side, out_shape=list(out_shape) + list(side.out_shape),
        scratch_shapes=list(scratch_shapes) + list(side.scratch_shapes),
        compiler_params=_cparams("arbitrary", "arbitrary"))(*args, *side.arrays)
    return outs[:n_out], outs[n_out:]


def _ssd_fwd(pre, dt_raw, z, dtb, alog, dskip_w, norm_w, side=None):
    n = pre.shape[0]
    nc = n // BLOCK
    xs_s, b_s, c_s, lane_s, vec_s, wide_s, hs_s = _ssd_specs(nc, False)

    def body(pxs_ref, pb_ref, pc_ref, dtr_ref, z_ref, dtb_ref, alog_ref, dskw_ref, nw_ref, e_ref,
             y_ref, yn_ref, hs_ref, h_scr):
        c = pl.program_id(1)

        @pl.when(c == 0)
        def _():
            h_scr[...] = jnp.zeros_like(h_scr)

        xs, bm, cm, _, _, dt, _, _, cs, causal = _ssd_prep(pxs_ref, pb_ref, pc_ref, dtr_ref, dtb_ref, alog_ref, c)
        cst = cs.T
        cs_last = cs[BLOCK - 1:BLOCK, :]
        dt_w, ecs_w, dec_w = _spread_heads([dt, jnp.exp(cs), jnp.exp(cs_last - cs)], e_ref)
        xdt = xs * dt_w
        bmb = bm.astype(BF16)
        cmb = cm.astype(BF16)
        cb = _dot_nt(cmb, bmb)
        hg = h_scr[...]
        hs_ref[0, 0] = hg
        y = _dot_nt(cmb, hg.astype(BF16)) * ecs_w + dskw_ref[...] * xs
        first = lax.broadcasted_iota(jnp.int32, (BLOCK, LANES), 1) < SSD_HEADDIM
        diag = []
        for j in range(SSD_HPG // 2):
            xp = xdt[:, LANES * j:LANES * (j + 1)].astype(BF16)
            res = []
            for r in (2 * j, 2 * j + 1):
                lm = jnp.exp(jnp.where(causal, cs[:, r:r + 1] - cst[r:r + 1, :], NEG))
                res.append(_dot((cb * lm).astype(BF16), xp))
            diag.append(jnp.where(first, res[0], res[1]))
        y = y + jnp.concatenate(diag, axis=1)
        st = _dot_tn((xdt * dec_w).astype(BF16), bmb)
        eh = jnp.exp(cs_last)
        for r in range(SSD_HPG):
            rows = _head_cols(r)
            h_scr[rows, :] = hg[rows, :] * eh[:, r:r + 1] + st[rows, :]
        y_ref[...] = y
        gts = y * _silu(z_ref[...])
        rr = lax.rsqrt(jnp.mean(gts * gts, axis=-1, keepdims=True) + EPS)
        yn_ref[...] = (gts * rr * nw_ref[...]).astype(BF16)

    return _call_with_side(
        body, side, name="ssd_fwd", grid=(SSD_GROUPS, nc),
        in_specs=[xs_s, b_s, c_s, lane_s, xs_s, vec_s, vec_s, wide_s, wide_s, _const_spec((LANES, SSD_GW))],
        out_specs=[xs_s, xs_s, hs_s],
        out_shape=[jax.ShapeDtypeStruct((n, SSD_INNER), F32), jax.ShapeDtypeStruct((n, SSD_INNER), BF16),
                   jax.ShapeDtypeStruct((nc, SSD_GROUPS, SSD_GW, SSD_STATE), F32)],
        scratch_shapes=[pltpu.VMEM((SSD_GW, SSD_STATE), F32)],
        args=(pre, pre, pre, dt_raw, z, dtb, alog, dskip_w, norm_w, _head_spread_matrix()))


def _lane_put(acc, col, r):
    lane = lax.broadcasted_iota(jnp.int32, acc.shape, 1)
    return jnp.where(lane == r, col, acc)


def _ssd_bwd(dyn, y, z, pre, dt_raw, hsave, dtb, alog, dskip_w, norm_w, side=None):
    n = pre.shape[0]
    nc = n // BLOCK
    spread = _head_spread_matrix()
    xs_s, b_s, c_s, lane_s, vec_s, wide_s, hs_s = _ssd_specs(nc, True)
    bc_out =pl.BlockSpec((BLOCK, SSD_STATE), lambda g, c: (nc - 1 - c, g))

    def body(dyn_ref, y_ref, z_ref, pxs_ref, pb_ref, pc_ref, dtr_ref, hs_ref, dtb_ref, alog_ref, dskw_ref, nw_ref,
             e_ref, r_ref,
             dz_ref, dxs_ref, dbm_ref, dcm_ref, ddt_ref, dnw_ref, ddtb_ref, dalog_ref, ddsk_ref, g_scr):
        step = pl.program_id(1)
        c = nc - 1 - step

        @pl.when(step == 0)
        def _():
            g_scr[...] = jnp.zeros_like(g_scr)

        pxs, pb, pc = pxs_ref[...], pb_ref[...], pc_ref[...]
        sx, sb, sc = _sigmoid(pxs), _sigmoid(pb), _sigmoid(pc)
        xs, bm, cm = pxs * sx, pb * sb, pc * sc
        valid, dtr, dt, a, lam, cs, causal = _ssd_decay(dtr_ref, dtb_ref, alog_ref, c)
        cst = cs.T
        cs_last = cs[BLOCK - 1:BLOCK, :]
        bmb = bm.astype(BF16)
        cmb = cm.astype(BF16)
        cb = _dot_nt(cmb, bmb)
        hg = hs_ref[0, 0]
        hgb = hg.astype(BF16)
        yoff = _dot_nt(cmb, hgb)
        gn = g_scr[...]
        gnb = gn.astype(BF16)

        zv = z_ref[...]
        yv = y_ref[...]
        sgz = _sigmoid(zv)
        sz = zv * sgz
        gts = yv * sz
        rr = lax.rsqrt(jnp.mean(gts * gts, axis=-1, keepdims=True) + EPS)
        xh = gts * rr
        dynv = dyn_ref[...]
        gg = dynv * nw_ref[...]
        dgts = rr * (gg - xh * jnp.mean(gg * xh, axis=-1, keepdims=True))
        dnw = jnp.sum(dynv * xh, axis=0, keepdims=True)
        dy = dgts * sz
        dz_ref[...] = (dgts * yv * (sgz * (1.0 + zv * (1.0 - sgz)))).astype(BF16)

        ecs = jnp.exp(cs)
        dec = jnp.exp(cs_last - cs)
        eh = jnp.exp(cs_last)
        dt_w, ecs_w, dec_w = _spread_heads([dt, ecs, dec], e_ref)
        red_m = r_ref[...]

        def head_sums(v):
            return _dot_sel(v, red_m, terms=2)

        xdt = xs * dt_w
        q_all = _dot_nt(bmb, gnb)
        w_all = (dy * ecs_w).astype(BF16)
        e_hl = head_sums(q_all * xdt) * dec
        dcs_col = head_sums(dy * yoff) * ecs - e_hl
        gh = jnp.zeros((1, LANES), F32)
        prod = gn * hg
        for r in range(SSD_HPG):
            gh = _lane_put(gh, _sum_all(prod[_head_cols(r), :]), r)
        dcs_last = jnp.sum(e_hl, axis=0, keepdims=True) + eh * gh
        ddsk = jnp.sum(head_sums(dy * xs), axis=0, keepdims=True)
        cbt = _dot_nt(bmb, cmb)
        lane = lax.broadcasted_iota(jnp.int32, (BLOCK, LANES), 1)
        first = lane < SSD_HEADDIM
        causal_t = lax.broadcasted_iota(jnp.int32, (BLOCK, BLOCK), 1) >= lax.broadcasted_iota(
            jnp.int32, (BLOCK, BLOCK), 0)
        sub = lax.broadcasted_iota(jnp.int32, (SUBLANES, BLOCK), 0)
        dcs_row = jnp.zeros((SUBLANES, BLOCK), F32)
        dcb = jnp.zeros((BLOCK, BLOCK), F32)
        dxdt_pairs = []
        for j in range(SSD_HPG // 2):
            tile = slice(LANES * j, LANES * (j + 1))
            dy_p = dy[:, tile]
            dyb = dy_p.astype(BF16)
            xdtb = xdt[:, tile].astype(BF16)
            res = []
            for half, r in enumerate((2 * j, 2 * j + 1)):
                csc, csr = cs[:, r:r + 1], cst[r:r + 1, :]
                lm = jnp.exp(jnp.where(causal, csc - csr, NEG))
                lmt = jnp.exp(jnp.where(causal_t, csr - csc, NEG))
                keep = first if half == 0 else jnp.logical_not(first)
                gm = _dot_nt(jnp.where(keep, dy_p, 0.0).astype(BF16), xdtb) * lm
                dcb = dcb + gm
                mm_ = gm * cb
                dcs_col = dcs_col + jnp.where(lane == r, jnp.sum(mm_, axis=1, keepdims=True), 0.0)
                dcs_row = jnp.where(sub == r, jnp.sum(mm_, axis=0, keepdims=True), dcs_row)
                res.append(_dot((cbt * lmt).astype(BF16), dyb))
            dxdt_pairs.append(jnp.where(first, res[0], res[1]))
        dxdt = jnp.concatenate(dxdt_pairs, axis=1) + q_all * dec_w
        ddt_x = head_sums(dxdt * xs)
        dxs = dxdt * dt_w + dskw_ref[...] * dy
        dcbb = dcb.astype(BF16)
        dcm = _dot(w_all, hgb) + _dot(dcbb, bmb)
        dbm = _dot((xdt * dec_w).astype(BF16), gnb) + _dot_tn(dcbb, cmb)
        dh_off = _dot_tn(w_all, cmb)
        for r in range(SSD_HPG):
            rows = _head_cols(r)
            g_scr[rows, :] = gn[rows, :] * eh[:, r:r + 1] + dh_off[rows, :]

        pad_rows = jnp.zeros((BLOCK - SUBLANES, BLOCK), F32)
        dcs = dcs_col - jnp.concatenate([dcs_row, pad_rows], axis=0).T
        rsel = lax.broadcasted_iota(jnp.int32, (BLOCK, LANES), 0)
        dcs = dcs + jnp.where(rsel == BLOCK - 1, dcs_last, 0.0)
        ri = lax.broadcasted_iota(jnp.int32, (BLOCK, BLOCK), 0)
        ci = lax.broadcasted_iota(jnp.int32, (BLOCK, BLOCK), 1)
        dlam = _sel_dot((ci >= ri).astype(BF16), dcs)
        head = lane < SSD_HPG
        ddt = dlam * a + ddt_x
        ddtr = jnp.where(head, ddt * _sigmoid(dtr) * valid, 0.0)
        ddt_ref[...] = ddtr.astype(BF16)
        dalog = jnp.sum(jnp.where(head, dlam * lam, 0.0), axis=0, keepdims=True)
        ddtb = jnp.sum(ddtr, axis=0, keepdims=True)

        dxs_ref[...] = dxs * (sx * (1.0 + pxs * (1.0 - sx)))
        dbm_ref[...] = dbm * (sb * (1.0 + pb * (1.0 - sb)))
        dcm_ref[...] = dcm * (sc * (1.0 + pc * (1.0 - sc)))

        @pl.when(step == 0)
        def _():
            dnw_ref[...] = dnw
            ddtb_ref[...] = ddtb
            dalog_ref[...] = dalog
            ddsk_ref[...] = ddsk

        @pl.when(step > 0)
        def _():
            dnw_ref[...] += dnw
            ddtb_ref[...] += ddtb
            dalog_ref[...] += dalog
            ddsk_ref[...] += ddsk

    return _call_with_side(
        body, side, name="ssd_bwd", grid=(SSD_GROUPS, nc),
        in_specs=[xs_s, xs_s, xs_s, xs_s, b_s, c_s, lane_s, hs_s, vec_s, vec_s, wide_s, wide_s,
                  _const_spec((LANES, SSD_GW)), _const_spec((SSD_GW, LANES))],
        out_specs=[xs_s, xs_s, bc_out, bc_out, lane_s, wide_s, vec_s, vec_s, vec_s],
        out_shape=[jax.ShapeDtypeStruct((n, SSD_INNER), BF16), jax.ShapeDtypeStruct((n, SSD_INNER), F32),
                   jax.ShapeDtypeStruct((n, SSD_BC), F32), jax.ShapeDtypeStruct((n, SSD_BC), F32),
                   jax.ShapeDtypeStruct((n, DT_W), BF16), jax.ShapeDtypeStruct((1, SSD_INNER), F32),
                   jax.ShapeDtypeStruct((1, DT_W), F32), jax.ShapeDtypeStruct((1, DT_W), F32),
                   jax.ShapeDtypeStruct((1, DT_W), F32)],
        scratch_shapes=[pltpu.VMEM((SSD_GW, SSD_STATE), F32)],
        args=(dyn, y, z, pre, pre, pre, dt_raw, hsave, dtb, alog, dskip_w, norm_w, spread, spread.T))


def _bucket_table():
    def bucket(dist):
        d = np.maximum(dist, 0)
        half = REL_BUCKETS // 2
        big = half + (np.log(np.maximum(d, half).astype(np.float32) / np.float32(half))
                      / np.float32(math.log(REL_MAX_DIST / half)) * np.float32(REL_BUCKETS - half)).astype(np.int32)
        return np.where(d < half, d, np.minimum(big, REL_BUCKETS - 1)).astype(np.int32)

    l = np.arange(BLOCK)[None, :]
    band = bucket(l + BLOCK - np.arange(2 * BLOCK)[:, None])
    j = np.arange(BLOCK)[:, None]
    tables = [np.concatenate([bucket(v * BLOCK + l - j), band], axis=0) for v in range(3)]
    return np.concatenate([t.reshape(-1) for t in tables])


def _onehot_t():
    buckets = jnp.asarray(_bucket_table())
    return (buckets[None, :] == jnp.arange(REL_BUCKETS, dtype=jnp.int32)[:, None]).astype(F32)


def _bias_tables(rel_t, onehot_t):
    def body(r_ref, oh_ref, o_ref):
        o_ref[...] = jnp.dot(r_ref[...], oh_ref[...], precision=HIGHEST, preferred_element_type=F32)

    return pl.pallas_call(
        body, name="bias_tables", grid=(NT_ALL // NT_TILE,),
        in_specs=[pl.BlockSpec((ATT_HEADS, REL_BUCKETS), lambda i: (0, 0)),
                  pl.BlockSpec((REL_BUCKETS, NT_TILE), lambda i: (0, i))],
        out_specs=pl.BlockSpec((ATT_HEADS, NT_TILE), lambda i: (0, i)),
        out_shape=jax.ShapeDtypeStruct((ATT_HEADS, NT_ALL), F32),
        compiler_params=_cparams("parallel"))(rel_t, onehot_t)


def _bias_grad(dtab, onehot_t):
    def body(d_ref, oh_ref, o_ref):
        i = pl.program_id(0)
        p = lax.dot_general(d_ref[...], oh_ref[...], (((1,), (1,)), ((), ())), precision=HIGHEST,
                            preferred_element_type=F32)

        @pl.when(i == 0)
        def _():
            o_ref[...] = p

        @pl.when(i > 0)
        def _():
            o_ref[...] += p

    return pl.pallas_call(
        body, name="bias_grad", grid=(NT_ALL // NT_TILE,),
        in_specs=[pl.BlockSpec((ATT_HEADS, NT_TILE), lambda i: (0, i)),
                  pl.BlockSpec((REL_BUCKETS, NT_TILE), lambda i: (0, i))],
        out_specs=pl.BlockSpec((ATT_HEADS, REL_BUCKETS), lambda i: (0, 0)),
        out_shape=jax.ShapeDtypeStruct((ATT_HEADS, REL_BUCKETS), F32),
        compiler_params=_cparams("arbitrary"))(dtab, onehot_t)


def _att_mask_t(n, copies):
    far = 4 * BLOCK
    kk = lax.broadcasted_iota(jnp.int32, (N_KEYS, copies * BLOCK), 0)
    li = lax.broadcasted_iota(jnp.int32, (N_KEYS, copies * BLOCK), 1) & (BLOCK - 1)
    meta_ok = (kk >= PAD) & (kk < BLOCK) & (li + jnp.where(n >= 1, far, 0) >= kk)
    prev_ok = (kk >= BLOCK) & (kk < 2 * BLOCK) & (kk - BLOCK > li + jnp.where(n >= 2, 0, far))
    cur_ok = (kk >= 2 * BLOCK) & (kk - 2 * BLOCK <= li - jnp.where(n >= 1, 0, far))
    return meta_ok | prev_ok | cur_ok


def _att_kv(meta_ref, prev_ref, cur_ref):
    kv = jnp.concatenate([meta_ref[...], prev_ref[...], cur_ref[...]], axis=0)
    first = lax.broadcasted_iota(jnp.int32, (N_KEYS, LANES), 1) < ATT_HEADDIM
    out = []
    for pair in (kv[:, :LANES], kv[:, LANES:]):
        swapped = pltpu.roll(pair, ATT_HEADDIM, 1)
        out.append([jnp.where(first, pair, swapped).astype(BF16), jnp.where(first, swapped, pair).astype(BF16)])
    return out[0], out[1]


def _split_heads(x_pair, first):
    return jnp.concatenate([jnp.where(first, x_pair, 0.0), jnp.where(first, 0.0, x_pair)], axis=0).astype(BF16)


def _att_probs_t(qm2, k_dup, t_ref, j, mask2, sink_ref):
    scale = ATT_HEADDIM ** -0.5
    bias2 = jnp.concatenate([t_ref[0, 2 * j], t_ref[0, 2 * j + 1]], axis=1)
    second = lax.broadcasted_iota(jnp.int32, (1, 2 * BLOCK), 1) >= BLOCK
    sink2 = jnp.where(second, sink_ref[0:1, 2 * j + 1:2 * j + 2], sink_ref[0:1, 2 * j:2 * j + 1])
    s_t = jnp.where(mask2, _dot_nt(k_dup, qm2) * scale + bias2, NEG)
    mx = jnp.maximum(jnp.max(s_t, axis=0, keepdims=True), sink2)
    p_t = jnp.exp(s_t - mx)
    p_s = jnp.exp(sink2 - mx)
    inv = 1.0 / (jnp.sum(p_t, axis=0, keepdims=True) + p_s)
    return p_t * inv, p_s * inv


def _att_specs(nb, rev):
    def nidx(i):
        return nb - 1 - i if rev else i

    kvb = ATT_Q // (2 * ATT_KV)
    q_s = pl.BlockSpec((BLOCK, ATT_Q), lambda i: (nidx(i), 0))
    cur = pl.BlockSpec((BLOCK, 2 * ATT_KV), lambda i: (nidx(i), kvb))
    prev = pl.BlockSpec((BLOCK, 2 * ATT_KV), lambda i: (jnp.maximum(nidx(i) - 1, 0), kvb))
    meta = pl.BlockSpec((BLOCK, 2 * ATT_KV), lambda i: (0, kvb))
    table = pl.BlockSpec((1, ATT_HEADS, N_KEYS, BLOCK), lambda i: (jnp.minimum(nidx(i), 2), 0, 0, 0))
    sink = pl.BlockSpec((1, LANES), lambda i: (0, 0))
    return q_s, cur, prev, meta, table, sink


def _attn_fwd(qkv, tables, sinks):
    n = qkv.shape[0]
    nb = n // BLOCK
    q_s, cur_s, prev_s, meta_s, t_s, sink_s = _att_specs(nb, False)

    def body(q_ref, cur_ref, prev_ref, meta_ref, t_ref, sink_ref, o_ref):
        blk = pl.program_id(0)
        mask_t = _att_mask_t(blk, 1)
        k_dup, v_dup = _att_kv(meta_ref, prev_ref, cur_ref)
        v_dup_t = [v.T for v in v_dup]
        first = lax.broadcasted_iota(jnp.int32, (BLOCK, LANES), 1) < ATT_HEADDIM
        top = lax.broadcasted_iota(jnp.int32, (LANES, BLOCK), 0) < ATT_HEADDIM
        scale = ATT_HEADDIM ** -0.5
        for j in range(ATT_HEADS // 2):
            kh = 2 * j // ATT_GQ
            tile = slice(LANES * j, LANES * (j + 1))
            q_p = q_ref[:, tile]
            res = []
            for half, h in enumerate((2 * j, 2 * j + 1)):
                qm = jnp.where(first if half == 0 else jnp.logical_not(first), q_p, 0.0).astype(BF16)
                sink = sink_ref[0:1, h:h + 1]
                s_t = jnp.where(mask_t, _dot_nt(k_dup[kh], qm) * scale + t_ref[0, h], NEG)
                mx = jnp.maximum(jnp.max(s_t, axis=0, keepdims=True), sink)
                p_t = jnp.exp(s_t - mx)
                inv = 1.0 / (jnp.sum(p_t, axis=0, keepdims=True) + jnp.exp(sink - mx))
                res.append(_dot(v_dup_t[kh], (p_t * inv).astype(BF16)))
            o_ref[:, tile] = jnp.where(top, res[0], res[1]).T.astype(BF16)

    return pl.pallas_call(
        body, name="attn_fwd", grid=(nb,),
        in_specs=[q_s, cur_s, prev_s, meta_s, t_s, sink_s],
        out_specs=q_s,
        out_shape=jax.ShapeDtypeStruct((n, ATT_Q), BF16),
        compiler_params=_cparams("parallel"))(qkv, qkv, qkv, qkv, tables, sinks)


def _attn_bwd(datt, qkv, tables, sinks):
    n = qkv.shape[0]
    nb = n // BLOCK
    q_s, cur_s, prev_s, meta_s, t_s, sink_s = _att_specs(nb, True)
    dqkv_s = pl.BlockSpec((BLOCK, ATT_Q + 2 * ATT_KV), lambda i: (nb - 1 - i, 0))
    scale = ATT_HEADDIM ** -0.5

    def body(do_ref, q_ref, cur_ref, prev_ref, meta_ref, t_ref, sink_ref,
             dqkv_ref, dt_ref, dsink_ref, carry_scr, meta_scr):
        step = pl.program_id(0)
        blk = nb - 1 - step
        mask2 = _att_mask_t(blk, 2)
        k_dup, v_dup = _att_kv(meta_ref, prev_ref, cur_ref)
        k_dup_t = [k.T for k in k_dup]

        @pl.when(step == 0)
        def _():
            carry_scr[...] = jnp.zeros_like(carry_scr)
            meta_scr[...] = jnp.zeros_like(meta_scr)
            dsink_ref[...] = jnp.zeros_like(dsink_ref)

        @pl.when((step == 0) | (blk <= 1))
        def _():
            dt_ref[...] = jnp.zeros_like(dt_ref)

        first = lax.broadcasted_iota(jnp.int32, (BLOCK, LANES), 1) < ATT_HEADDIM
        top = lax.broadcasted_iota(jnp.int32, (LANES, BLOCK), 0) < ATT_HEADDIM
        first_k = lax.broadcasted_iota(jnp.int32, (N_KEYS, LANES), 1) < ATT_HEADDIM
        dsink = jnp.zeros((1, LANES), F32)
        dk_acc = [None] * ATT_KV_HEADS
        dv_acc = [None] * ATT_KV_HEADS
        for j in range(ATT_HEADS // 2):
            kh = 2 * j // ATT_GQ
            tile = slice(LANES * j, LANES * (j + 1))
            qm2 = _split_heads(q_ref[:, tile], first)
            dom2 = _split_heads(do_ref[:, tile], first)
            p_t, p_s = _att_probs_t(qm2, k_dup[kh], t_ref, j, mask2, sink_ref)
            dp_t = _dot_nt(v_dup[kh], dom2)
            delta = jnp.sum(p_t * dp_t, axis=0, keepdims=True)
            ds_t = p_t * (dp_t - delta)
            sink_terms = p_s * delta
            for half in range(2):
                cols = slice(BLOCK * half, BLOCK * (half + 1))
                dsink = _lane_put(dsink, -jnp.sum(sink_terms[:, cols], axis=1, keepdims=True), 2 * j + half)
                dt_ref[0, 2 * j + half] += ds_t[:, cols]
            ds_tb = ds_t.astype(BF16)
            dq_t = _dot(k_dup_t[kh], ds_tb)
            dqkv_ref[:, tile] = (jnp.where(top, dq_t[:, :BLOCK], dq_t[:, BLOCK:]).T * scale).astype(BF16)
            dk_part, dv_part = _dot(ds_tb, qm2), _dot(p_t.astype(BF16), dom2)
            dk_acc[kh] = dk_part if dk_acc[kh] is None else dk_acc[kh] + dk_part
            dv_acc[kh] = dv_part if dv_acc[kh] is None else dv_acc[kh] + dv_part
        dsink_ref[...] += dsink
        folded = [a + pltpu.roll(a, ATT_HEADDIM, 1) for a in dk_acc + dv_acc]
        dkv = jnp.concatenate([jnp.where(first_k, folded[0], folded[1]) * scale,
                               jnp.where(first_k, folded[2], folded[3])], axis=1)
        meta_scr[...] += dkv[:BLOCK, :]
        own = dkv[2 * BLOCK:, :] + carry_scr[...]
        carry_scr[...] = dkv[BLOCK:2 * BLOCK, :]

        @pl.when(blk > 0)
        def _():
            dqkv_ref[:, ATT_Q:] = own.astype(BF16)

        @pl.when(blk == 0)
        def _():
            dqkv_ref[:, ATT_Q:] = (own + meta_scr[...]).astype(BF16)

    return pl.pallas_call(
        body, name="attn_bwd", grid=(nb,),
        in_specs=[q_s, q_s, cur_s, prev_s, meta_s, t_s, sink_s],
        out_specs=[dqkv_s, t_s, sink_s],
        out_shape=[jax.ShapeDtypeStruct((n, ATT_Q + 2 * ATT_KV), BF16),
                   jax.ShapeDtypeStruct((3, ATT_HEADS, N_KEYS, BLOCK), F32),
                   jax.ShapeDtypeStruct((1, LANES), F32)],
        scratch_shapes=[pltpu.VMEM((BLOCK, 2 * ATT_KV), F32), pltpu.VMEM((BLOCK, 2 * ATT_KV), F32)],
        compiler_params=_cparams("arbitrary"))(datt, qkv, qkv, qkv, qkv, tables, sinks)


def _merge_out_fwd(gates, y_ssd, y_att, gate_b, w_out, h):
    n = gates.shape[0]
    tm = _row_tile(n, 416)

    def body(gs_ref, ga_ref, ys_ref, ya_ref, gb_ref, w_ref, h_ref, m_ref, o_ref):
        merged = (_sigmoid(gs_ref[...] + gb_ref[0:1, :]) * ys_ref[...]
                  + _sigmoid(ga_ref[...] + gb_ref[1:2, :]) * ya_ref[...]).astype(BF16)
        m_ref[...] = merged
        row = pl.program_id(0) * tm + lax.broadcasted_iota(jnp.int32, (tm, 1), 0)
        o_ref[...] = jnp.where(row >= PAD, _dot(merged, w_ref[...]), 0.0) + h_ref[...]

    row = pl.BlockSpec((tm, D_MODEL), lambda i: (i, 0))
    return pl.pallas_call(
        body, name="merge_out_fwd", grid=(n // tm,),
        in_specs=[row, pl.BlockSpec((tm, D_MODEL), lambda i: (i, 1)), row, row,
                  pl.BlockSpec((2, D_MODEL), lambda i: (0, 0)), pl.BlockSpec((D_MODEL, D_MODEL), lambda i: (0, 0)), row],
        out_specs=[row, row],
        out_shape=[jax.ShapeDtypeStruct((n, D_MODEL), BF16), jax.ShapeDtypeStruct((n, D_MODEL), F32)],
        compiler_params=_cparams("parallel"))(gates, gates, y_ssd, y_att, gate_b, w_out, h)


def _merge_out_bwd(dh, w_out, gates, y_ssd, y_att, gate_b):
    n = gates.shape[0]
    tm = _row_tile(n, 416)

    def body(dh_ref, w_ref, gs_ref, ga_ref, ys_ref, ya_ref, gb_ref, dys_ref, dya_ref, dg_ref, dgb_ref):
        i = pl.program_id(0)
        row = i * tm + lax.broadcasted_iota(jnp.int32, (tm, 1), 0)
        dmv = jnp.where(row >= PAD, _dot_nt(dh_ref[...].astype(BF16), w_ref[...]), 0.0)
        ss =_sigmoid(gs_ref[...] + gb_ref[0:1, :])
        sa = _sigmoid(ga_ref[...] + gb_ref[1:2, :])
        dys_ref[...] = (dmv * ss).astype(BF16)
        dya_ref[...] = (dmv * sa).astype(BF16)
        dgs = dmv * ys_ref[...] * ss * (1.0 - ss)
        dga = dmv * ya_ref[...] * sa * (1.0 - sa)
        dg_ref[:, :D_MODEL] = dgs.astype(BF16)
        dg_ref[:, D_MODEL:] = dga.astype(BF16)
        part = jnp.concatenate([jnp.sum(dgs, axis=0, keepdims=True), jnp.sum(dga, axis=0, keepdims=True)], axis=0)

        @pl.when(i == 0)
        def _():
            dgb_ref[...] = part

        @pl.when(i > 0)
        def _():
            dgb_ref[...] += part

    row = pl.BlockSpec((tm, D_MODEL), lambda i: (i, 0))
    gb = pl.BlockSpec((2, D_MODEL), lambda i: (0, 0))
    return pl.pallas_call(
        body, name="merge_out_bwd", grid=(n // tm,),
        in_specs=[row, pl.BlockSpec((D_MODEL, D_MODEL), lambda i: (0, 0)), row,
                  pl.BlockSpec((tm, D_MODEL), lambda i: (i, 1)), row, row, gb],
        out_specs=[row, row, pl.BlockSpec((tm, 2 * D_MODEL), lambda i: (i, 0)), gb],
        out_shape=[jax.ShapeDtypeStruct((n, D_MODEL), BF16), jax.ShapeDtypeStruct((n, D_MODEL), BF16),
                   jax.ShapeDtypeStruct((n, 2 * D_MODEL), BF16), jax.ShapeDtypeStruct((2, D_MODEL), F32)],
        compiler_params=_cparams("arbitrary"))(dh, w_out, gates, gates, y_ssd, y_att, gate_b)


def _col_move(srcs, outs, pieces, *, name):
    rows = srcs[0].shape[-2]
    tr = _row_tile(rows, 128)
    n_src = len(srcs)
    covered = [sum(p[6] for p in pieces if p[0] == o) for o in range(len(outs))]
    total = [int(np.prod(shp)) // rows for shp, _ in outs]

    def body(*refs):
        in_refs, out_refs = refs[:n_src], refs[n_src:]
        for o, ref in enumerate(out_refs):
            if covered[o] != total[o]:
                ref[...] = jnp.zeros_like(ref)
        for o, ol, oc, s, sl, sc, width in pieces:
            val = in_refs[s][:, sc:sc + width] if sl is None else in_refs[s][sl, :, sc:sc + width]
            val = val.astype(outs[o][1])
            if ol is None:
                out_refs[o][:, oc:oc + width] = val
            else:
                out_refs[o][ol, :, oc:oc + width] = val

    def spec(shape):
        if len(shape) == 2:
            return pl.BlockSpec((tr, shape[1]), lambda i: (i, 0))
        return pl.BlockSpec((shape[0], tr, shape[2]), lambda i: (0, i, 0))

    return pl.pallas_call(
        body, name=name, grid=(rows // tr,),
        in_specs=[spec(a.shape) for a in srcs], out_specs=[spec(shp) for shp, _ in outs],
        out_shape=[jax.ShapeDtypeStruct(shp, dt) for shp, dt in outs],
        compiler_params=_cparams("parallel"))(*srcs)


def _shard_pieces(seg_ranges, shard_w):
    out = []
    for seg, runs in enumerate(seg_ranges):
        for g0, width, s0 in runs:
            done = 0
            while done < width:
                dev, col = divmod(g0 + done, shard_w)
                take = min(width - done, shard_w - col)
                out.append((seg, s0 + done, dev, col, take))
                done += take
    return out


_CHIP_RELATIONS = [(1, 0, 0), (0, 1, 0), (1, 1, 0)]
N_CHIPS = 4


def _gather_two_level(arrays, *, name):
    outs = _run_plan(_gather_plan(arrays), name)
    return [o.reshape((N_DEV,) + a.shape) for o, a in zip(outs, arrays)]


class _CommPlan:
    def __init__(self, arrays, out_shape, scratch_shapes, phases):
        self.arrays, self.out_shape, self.scratch_shapes, self.phases = arrays, out_shape, scratch_shapes, phases


def _run_plan(plan, name):
    n_arr = len(plan.arrays)

    def body(*refs):
        ins, outs, sems = refs[:n_arr], refs[n_arr:2 * n_arr], refs[2 * n_arr:]
        for phase in plan.phases:
            phase(ins, outs, sems)

    any_spec = pl.BlockSpec(memory_space=pl.ANY)
    return pl.pallas_call(
        body, name=name, in_specs=[any_spec] * n_arr, out_specs=[any_spec] * n_arr, out_shape=plan.out_shape,
        scratch_shapes=plan.scratch_shapes)(*plan.arrays)


def _gather_plan(arrays):
    n_arr = len(arrays)
    n_chips = len(_CHIP_RELATIONS)
    n_pair = 1 + 2 * n_chips

    def where():
        x, y, c = lax.axis_index("x"), lax.axis_index("y"), lax.axis_index("c")
        return x, y, c, (x, y, 1 - c), [(x ^ dx, y ^ dy) for dx, dy, _ in _CHIP_RELATIONS]

    def copy(outs, sems, a, k, block, to, src=None):
        slot = outs[a].at[2 * block[0] + block[1], block[2]]
        return pltpu.make_async_remote_copy(
            src_ref=slot if src is None else src, dst_ref=slot, send_sem=sems[0].at[a * n_pair + k],
            recv_sem=sems[1].at[a * n_pair + k], device_id=to, device_id_type=MESH)

    def mine(ins, outs, sems, a, x, y, c):
        return pltpu.make_async_copy(ins[a], outs[a].at[2 * x + y, c], sems[2].at[a])

    def first_copies(ins, outs, sems, a, x, y, c, sibling, chips):
        return ([copy(outs, sems, a, 0, (x, y, c), sibling, src=ins[a])]
                + [copy(outs, sems, a, 1 + j, (x, y, c), (*chip, c), src=ins[a]) for j, chip in enumerate(chips)])

    def start(ins, outs, sems):
        x, y, c, sibling, chips = where()
        for a in range(n_arr):
            mine(ins, outs, sems, a, x, y, c).start()
            for cp in first_copies(ins, outs, sems, a, x, y, c, sibling, chips):
                cp.start()

    def pass_on(ins, outs, sems):
        x, y, c, sibling, chips = where()
        for j, chip in enumerate(chips):
            for a in range(n_arr):
                copy(outs, sems, a, 1 + j, (*chip, c), (x, y, c)).wait_recv()
                copy(outs, sems, a, 1 + n_chips + j, (*chip, c), sibling).start()

    def finish(ins, outs, sems):
        x, y, c, sibling, chips = where()
        for a in range(n_arr):
            copy(outs, sems, a, 0, (x, y, 1 - c), (x, y, c)).wait_recv()
            for j, chip in enumerate(chips):
                copy(outs, sems, a, 1 + n_chips + j, (*chip, 1 - c), (x, y, c)).wait_recv()
        for a in range(n_arr):
            for cp in first_copies(ins, outs, sems, a, x, y, c, sibling, chips):
                cp.wait_send()
            for j, chip in enumerate(chips):
                copy(outs, sems, a, 1 + n_chips + j, (*chip, c), sibling).wait_send()
            mine(ins, outs, sems, a, x, y, c).wait()

    return _CommPlan(
        arrays, [jax.ShapeDtypeStruct((N_CHIPS, 2) + a.shape, a.dtype) for a in arrays],
        [pltpu.SemaphoreType.DMA((n_arr * n_pair,)), pltpu.SemaphoreType.DMA((n_arr * n_pair,)),
         pltpu.SemaphoreType.DMA((n_arr,))],
        (start, pass_on, finish))


def _sibling_exchange(arrays, scatter, *, name):
    n_arr = len(arrays)

    def body(*refs):
        ins, outs = refs[:n_arr], refs[n_arr:2 * n_arr]
        send_sems, recv_sems = refs[2 * n_arr:]
        x, y, c = lax.axis_index("x"), lax.axis_index("y"), lax.axis_index("c")
        copies = []
        for a in range(n_arr):
            for q in range(N_CHIPS if scatter[a] else 1):
                src = ins[a].at[2 * q + 1 - c] if scatter[a] else ins[a]
                dst = outs[a].at[q] if scatter[a] else outs[a]
                cp = pltpu.make_async_remote_copy(
                    src_ref=src, dst_ref=dst, send_sem=send_sems.at[a * N_CHIPS + q],
                    recv_sem=recv_sems.at[a * N_CHIPS + q], device_id=(x, y, 1 - c), device_id_type=MESH)
                cp.start()
                copies.append(cp)
        for cp in copies:
            cp.wait_send()
        for cp in copies:
            cp.wait_recv()

    any_spec = pl.BlockSpec(memory_space=pl.ANY)
    return pl.pallas_call(
        body, name=name, in_specs=[any_spec] * n_arr, out_specs=[any_spec] * n_arr,
        out_shape=[jax.ShapeDtypeStruct(((N_CHIPS,) + a.shape[1:]) if s else a.shape, a.dtype)
                   for a, s in zip(arrays, scatter)],
        scratch_shapes=[pltpu.SemaphoreType.DMA((n_arr * N_CHIPS,)), pltpu.SemaphoreType.DMA((n_arr * N_CHIPS,))],
    )(*arrays)


def _pair_sum(mine, sib, *, name, out_dtype):
    _, rows, cols = mine.shape
    tr = _row_tile(rows, 128)

    def body(c_ref, m_ref, s_ref, o_ref):
        del c_ref
        o_ref[0] = (m_ref[0, 0] + s_ref[0]).astype(out_dtype)

    grid_spec = pltpu.PrefetchScalarGridSpec(
        num_scalar_prefetch=1, grid=(N_CHIPS, rows // tr),
        in_specs=[pl.BlockSpec((1, 1, tr, cols), lambda q, i, c_ref: (q, c_ref[0], i, 0)),
                  pl.BlockSpec((1, tr, cols), lambda q, i, c_ref: (q, i, 0))],
        out_specs=pl.BlockSpec((1, tr, cols), lambda q, i, c_ref: (q, i, 0)))
    my_side = lax.axis_index("c").astype(jnp.int32).reshape(1)
    return pl.pallas_call(
        body, name=name, grid_spec=grid_spec,
        out_shape=jax.ShapeDtypeStruct((N_CHIPS, rows, cols), out_dtype),
        compiler_params=_cparams("parallel", "parallel"))(my_side, mine.reshape(N_CHIPS, 2, rows, cols), sib)


def _add(a, b, *, name):
    rows, cols = a.shape
    tr = _row_tile(rows, 256)

    def body(a_ref, b_ref, o_ref):
        o_ref[...] = a_ref[...] + b_ref[...]

    blk = pl.BlockSpec((tr, cols), lambda i: (i, 0))
    return pl.pallas_call(body, name=name, grid=(rows // tr,), in_specs=[blk, blk], out_specs=blk,
                          out_shape=jax.ShapeDtypeStruct(a.shape, a.dtype), compiler_params=_cparams("parallel"))(a, b)


def _chip_exchange(arrays, scatter, *, name):
    return _run_plan(_chip_exchange_plan(arrays, scatter), name)


def _chip_exchange_plan(arrays, scatter):
    n_arr = len(arrays)
    n_rel = len(_CHIP_RELATIONS)

    def local_copies(ins, outs, sems):
        me = 2 * lax.axis_index("x") + lax.axis_index("y")
        return [pltpu.make_async_copy(ins[a].at[me] if scatter[a] else ins[a], outs[a].at[me], sems[2].at[a])
                for a in range(n_arr)]

    def remote_copies(ins, outs, sems, arrivals):
        x, y, c = lax.axis_index("x"), lax.axis_index("y"), lax.axis_index("c")
        me = 2 * x + y
        out = []
        for k, (dx, dy, _) in enumerate(_CHIP_RELATIONS):
            px, py = x ^ dx, y ^ dy
            peer = 2 * px + py
            for a in range(n_arr):
                out.append(pltpu.make_async_remote_copy(
                    src_ref=ins[a].at[peer] if scatter[a] else ins[a], dst_ref=outs[a].at[peer if arrivals else me],
                    send_sem=sems[0].at[a * n_rel + k], recv_sem=sems[1].at[a * n_rel + k],
                    device_id=(x, y, c) if arrivals else (px, py, c), device_id_type=MESH))
        return out

    def start(ins, outs, sems):
        for cp in local_copies(ins, outs, sems) + remote_copies(ins, outs, sems, False):
            cp.start()

    def pass_on(ins, outs, sems):
        pass

    def finish(ins, outs, sems):
        for send in remote_copies(ins, outs, sems, False):
            send.wait_send()
        for arrival in remote_copies(ins, outs, sems, True):
            arrival.wait_recv()
        for cp in local_copies(ins, outs, sems):
            cp.wait()

    out_shape = [jax.ShapeDtypeStruct((N_CHIPS,) + (a.shape[1:] if s else a.shape), a.dtype)
                 for a, s in zip(arrays, scatter)]
    return _CommPlan(
        arrays, out_shape,
        [pltpu.SemaphoreType.DMA((n_arr * n_rel,)), pltpu.SemaphoreType.DMA((n_arr * n_rel,)),
         pltpu.SemaphoreType.DMA((n_arr,))],
        (start, pass_on, finish))


def _adamw(w, gslots, m, v, *, name):
    rows, cols = w.shape
    n_slots = gslots.shape[0]
    tr = _row_tile(rows, 128) if rows % 16 == 0 else rows

    def body(w_ref, g_ref, m_ref, v_ref, go_ref, d_ref, mo_ref, vo_ref):
        g = g_ref[0].astype(F32)
        for s in range(1, n_slots):
            g = g + g_ref[s].astype(F32)
        mn = ADAM_B1 * m_ref[...] + (1.0 - ADAM_B1) * g
        vn = ADAM_B2 * v_ref[...] + (1.0 - ADAM_B2) * (g * g)
        go_ref[...] = g
        mo_ref[...] = mn
        vo_ref[...] = vn
        m_hat = mn / (1.0 - ADAM_B1 ** ADAM_STEP)
        v_hat = vn / (1.0 - ADAM_B2 ** ADAM_STEP)
        d_ref[...] = -ADAM_LR * (m_hat / (jnp.sqrt(v_hat) + ADAM_EPS) + ADAM_WD * w_ref[...])

    blk = pl.BlockSpec((tr, cols), lambda i: (i, 0))
    shp = jax.ShapeDtypeStruct((rows, cols), F32)
    return pl.pallas_call(
        body, name=name, grid=(rows // tr,),
        in_specs=[blk, pl.BlockSpec((n_slots, tr, cols), lambda i: (0, i, 0)), blk, blk],
        out_specs=[blk] * 4, out_shape=[shp] * 4,
        compiler_params=_cparams("parallel"))(w, gslots, m, v)


_BIG = ("w_in", "w_ssd_branch", "w_attn_branch", "w_out", "w_ffn_in", "w_ffn_out")
_SMALL_SHARDED = ("meta_tokens", "ssd_conv_w", "gate_b", "ffn_conv_w")
_SMALL_REPLICATED = ("norm_mix_w", "ssd_conv_b", "ssd_dt_bias", "ssd_a_log", "ssd_d", "ssd_norm_w", "attn_sinks",
                     "rel_bias", "norm_ffn_w", "ffn_conv_b", "norm_final_w")
_WEIGHTS = ("meta_tokens", "norm_mix_w", "w_in", "ssd_conv_w", "ssd_conv_b", "ssd_dt_bias", "ssd_a_log", "ssd_d",
            "ssd_norm_w", "w_ssd_branch", "w_attn_branch", "attn_sinks", "rel_bias", "gate_b", "w_out", "norm_ffn_w",
            "w_ffn_in", "ffn_conv_w", "ffn_conv_b", "w_ffn_out", "norm_final_w")
_ROW_SHARDED = ("w_ssd_branch", "w_attn_branch", "w_out", "w_ffn_out")
_COL_SHARDED = ("w_in", "w_ffn_in", "meta_tokens", "ssd_conv_w", "gate_b", "ffn_conv_w")
_IN_SEGS = (("z", SSD_INNER), ("xbc", SSD_XBC), ("dt", SSD_HEADS), ("qkv", ATT_Q + 2 * ATT_KV), ("g", 2 * D_MODEL))


def _pack_rows(flat_parts, width, row_mult):
    flat = jnp.concatenate([p.reshape(-1) for p in flat_parts])
    pad = (-flat.shape[0]) % (width * row_mult)
    if pad:
        flat = jnp.concatenate([flat, jnp.zeros((pad,), flat.dtype)])
    return flat.reshape(-1, width)


def _unpack(flat, shapes):
    out, off = [], 0
    for shp in shapes:
        size = int(np.prod(shp))
        out.append(flat[off:off + size].reshape(shp))
        off += size
    return out


def _gather_full(stack, name, shard_shape):
    if name in _COL_SHARDED:
        return jnp.transpose(stack, (1, 0, 2)).reshape(shard_shape[0], N_DEV * shard_shape[1])
    return stack.reshape(N_DEV * shard_shape[0], shard_shape[1])


_IN_SEG_W = {"z": SSD_INNER, "xbc": SSD_XBC, "dt": DT_W, "qkv": ATT_Q + 2 * ATT_KV, "g": 2 * D_MODEL}
_IN_SHARD_W = (SSD_INNER + SSD_XBC + SSD_HEADS + ATT_Q + 2 * ATT_KV + 2 * D_MODEL) // N_DEV
_FFN_SHARD_W = 2 * D_FF // N_DEV


def _in_seg_runs():
    runs, off = [], 0
    for nm, width in _IN_SEGS:
        if nm == "dt":
            runs.append([(off + SSD_HPG * g, SSD_HPG, LANES * g) for g in range(SSD_GROUPS)])
        else:
            runs.append([(off, width, 0)])
        off += width
    return runs


def _w_in_to_segments(stack):
    pieces = [(seg, None, scol, 0, dev, col, w) for seg, scol, dev, col, w in _shard_pieces(_in_seg_runs(), _IN_SHARD_W)]
    outs = [((D_MODEL, _IN_SEG_W[nm]), stack.dtype) for nm, _ in _IN_SEGS]
    return dict(zip([nm for nm, _ in _IN_SEGS], _col_move([stack], outs, pieces, name="w_in_segments")))


def _segments_to_w_in_shards(seg_grads):
    pieces = [(0, dev, col, seg, None, scol, w) for seg, scol, dev, col, w in _shard_pieces(_in_seg_runs(), _IN_SHARD_W)]
    return _col_move(seg_grads, [((N_DEV, D_MODEL, _IN_SHARD_W), F32)], pieces, name="g_w_in_shards")[0]


def _ffn_in_from_shards(stack):
    pieces = [(0, None, scol, 0, dev, col, w)
              for _, scol, dev, col, w in _shard_pieces([[(0, 2 * D_FF, 0)]], _FFN_SHARD_W)]
    return _col_move([stack], [((D_MODEL, 2 * D_FF), stack.dtype)], pieces, name="w_ffn_in_full")[0]


def _ffn_in_to_shards(g_up, g_gate):
    pieces = [(0, dev, col, seg, None, scol, w)
              for seg, scol, dev, col, w in _shard_pieces([[(0, D_FF, 0)], [(D_FF, D_FF, 0)]], _FFN_SHARD_W)]
    return _col_move([g_up, g_gate], [((N_DEV, D_MODEL, _FFN_SHARD_W), F32)], pieces, name="g_w_ffn_in_shards")[0]


def _dt_spread(w_dt):
    k = w_dt.shape[0]
    w4 = w_dt.reshape(k, SSD_GROUPS, SSD_HPG)
    return jnp.pad(w4, ((0, 0), (0, 0), (0, LANES - SSD_HPG))).reshape(k, DT_W)


def _dt_gather(w_wide):
    k = w_wide.shape[0]
    return w_wide.reshape(k, SSD_GROUPS, LANES)[:, :, :SSD_HPG].reshape(k, SSD_HEADS)


class _LateExchanges:
    def __init__(self, two_d, shape2):
        self.two_d, self.shape2 = two_d, shape2
        self.early_grads_received = None

    def row_pack(self, tree):
        return jnp.concatenate([tree[k] for k in _ROW_SHARDED], axis=0)

    def late_weights_plan(self):
        return _gather_plan([self.two_d["w_ffn_in"].astype(BF16), self.row_pack(self.two_d).astype(BF16)])

    def late_weights(self, gathered):
        w_ffn_in_all, rows_all = [g.reshape((N_DEV,) + g.shape[2:]) for g in gathered]
        out = {"w_ffn_in": _ffn_in_from_shards(w_ffn_in_all)}
        off = 0
        for k in _ROW_SHARDED:
            r = self.shape2[k][0]
            out[k] = rows_all[:, off:off + r].reshape(N_DEV * r, D_MODEL)
            off += r
        return out

    def early_grads_plan(self, grads):
        rows_send = jnp.concatenate([grads[k].reshape(N_DEV, self.shape2[k][0], D_MODEL) for k in _ROW_SHARDED], axis=1)
        send = [_ffn_in_to_shards(*grads["w_ffn_in"]), rows_send]
        from_sib = _sibling_exchange(send, [True, True], name="early_grads_to_sibling")
        parts = [_pair_sum(mine, sib, name="pair_sum_" + nm, out_dtype=BF16)
                 for nm, mine, sib in zip(("w_ffn_in", "rows"), send, from_sib)]
        return _chip_exchange_plan(parts, [True, True])


def _local_step(x, target, w, exchanges=None):
    h0 = jnp.concatenate([jnp.zeros((PAD, D_MODEL), F32), w["meta_tokens"], x], axis=0)
    segs = w["in_segs"]

    dtb = _dt_spread(w["ssd_dt_bias"])
    alog = _dt_spread(w["ssd_a_log"])
    dskip_w = jnp.repeat(w["ssd_d"], SSD_HEADDIM, axis=1)
    sinks = jnp.pad(w["attn_sinks"], ((0, 0), (0, LANES - ATT_HEADS)))
    onehot_t = _onehot_t()
    tables = jnp.transpose(_bias_tables(w["rel_bias"].T, onehot_t).reshape(ATT_HEADS, 3, N_KEYS, BLOCK), (1, 0, 2, 3))

    u = _rms_fwd(h0, w["norm_mix_w"], name="rms_mix_fwd")
    z = _mm(u, segs["z"], name="in_z")
    xbc = _mm(u, segs["xbc"], name="in_xbc")
    dt_raw = _mm(u, segs["dt"], name="in_dt")
    qkv = _mm(u, segs["qkv"], name="in_qkv")
    gates = _mm(u, segs["g"], name="in_g")
    pre = _conv_fwd(xbc, w["ssd_conv_w"], w["ssd_conv_b"], name="ssd_conv_fwd")
    (y, yn, hsave), gathered = _ssd_fwd(pre, dt_raw, z, dtb, alog, dskip_w, w["ssd_norm_w"],
                                        side=None if exchanges is None else exchanges.late_weights_plan())
    if exchanges is not None:
        w = {**w, **exchanges.late_weights(gathered)}
    w_ffn_up, w_ffn_gate = w["w_ffn_in"][:, :D_FF], w["w_ffn_in"][:, D_FF:]
    y_ssd = _mm(yn, w["w_ssd_branch"], name="ssd_out")
    att = _attn_fwd(qkv, tables, sinks)
    y_att = _mm(att, w["w_attn_branch"], name="att_out")
    merged, h1 = _merge_out_fwd(gates, y_ssd, y_att, w["gate_b"], w["w_out"], h0)
    u2 = _rms_fwd(h1, w["norm_ffn_w"], name="rms_ffn_fwd")
    hid_raw = _mm(u2, w["w_ffn_in"], name="ffn_in")
    hid_up, hid_gate, act = _ffn_act_fwd(hid_raw, w["ffn_conv_w"], w["ffn_conv_b"])
    h2 = _mm(act, w["w_ffn_out"], c=h1, mask=True, name="ffn_out")
    dh2, loss_row, g_norm_final = _final_loss(h2, w["norm_final_w"], target)

    grads = {"norm_final_w": g_norm_final}
    dact = _mm(dh2, w["w_ffn_out"], tb=True, mask=True, name="d_act")
    grads["w_ffn_out"] = _mm(act, dh2, ta=True, mask=True, name="g_w_ffn_out")
    dx_up, dx_gate, dcw_up, dcw_gate, dcb_up, dcb_gate = _ffn_act_bwd(dact, hid_up, hid_gate, hid_raw, w["ffn_conv_w"])
    grads["ffn_conv_w"] = jnp.concatenate([dcw_up, dcw_gate], axis=1)
    grads["ffn_conv_b"] = jnp.concatenate([dcb_up, dcb_gate], axis=1)
    du2 = _mm(dx_up, w_ffn_up, tb=True, name="d_u2_up")
    dh1, grads["norm_ffn_w"] = _mm_rms_bwd(dx_gate, w_ffn_gate, du2, h1, w["norm_ffn_w"], dh2, name="d_u2_gate_rms_bwd")
    grads["w_ffn_in"] = (_mm(u2, dx_up, ta=True, name="g_w_ffn_up"), _mm(u2, dx_gate, ta=True, name="g_w_ffn_gate"))

    grads["w_out"] = _mm(merged, dh1, ta=True, mask=True, name="g_w_out")
    dy_ssd, dy_att, dgates, grads["gate_b"] = _merge_out_bwd(dh1, w["w_out"], gates, y_ssd, y_att, w["gate_b"])
    dyn = _mm(dy_ssd, w["w_ssd_branch"], tb=True, name="d_yn")
    grads["w_ssd_branch"] = _mm(yn, dy_ssd, ta=True, name="g_w_ssd")
    datt = _mm(dy_att, w["w_attn_branch"], tb=True, name="d_att")
    grads["w_attn_branch"] = _mm(att, dy_att, ta=True, name="g_w_att")
    (dz, dpxs, dpb, dpc, ddt, grads["ssd_norm_w"], g_dtb, g_alog, g_dskip), received = _ssd_bwd(
        dyn, y, z, pre, dt_raw, hsave, dtb, alog, dskip_w, w["ssd_norm_w"],
        side=None if exchanges is None else exchanges.early_grads_plan(grads))
    if exchanges is not None:
        exchanges.early_grads_received = received
    grads["ssd_dt_bias"] = _dt_gather(g_dtb)
    grads["ssd_a_log"] = _dt_gather(g_alog)
    grads["ssd_d"] = _dt_gather(g_dskip)
    conv_g = _conv_bwd(dpxs, xbc, w["ssd_conv_w"], name="ssd_conv_bwd_x")
    conv_g = _conv_bwd(dpb, xbc, w["ssd_conv_w"], name="ssd_conv_bwd_b", col0=SSD_INNER, into=conv_g)
    dxbc, grads["ssd_conv_w"], grads["ssd_conv_b"] = _conv_bwd(
        dpc, xbc, w["ssd_conv_w"], name="ssd_conv_bwd_c", col0=SSD_INNER + SSD_BC, into=conv_g)
    dqkv, d_tables, d_sinks = _attn_bwd(datt, qkv, tables, sinks)
    grads["attn_sinks"] = d_sinks[:, :ATT_HEADS]
    dtab = jnp.transpose(d_tables, (1, 0, 2, 3)).reshape(ATT_HEADS, NT_ALL)
    grads["rel_bias"] = _bias_grad(dtab, onehot_t).T
    dsegs = {"z": dz, "xbc": dxbc, "dt": ddt, "qkv": dqkv, "g": dgates}
    du, g_in = None, []
    for nm, _ in _IN_SEGS:
        g_in.append(_mm(u, dsegs[nm], ta=True, name="g_w_in_" + nm))
        if nm != _IN_SEGS[-1][0]:
            du = _mm(dsegs[nm], segs[nm], tb=True, c=du, name="d_u_" + nm)
    grads["in_segs"] = g_in
    dh0, grads["norm_mix_w"] = _mm_rms_bwd(dsegs[nm], segs[nm], du, h0, w["norm_mix_w"], dh1, name="d_u_g_rms_bwd")
    grads["meta_tokens"] = dh0[PAD:BLOCK]
    return loss_row[0, 0], dh0[BLOCK:], grads


def kernel(x, meta_tokens, norm_mix_w, w_in, ssd_conv_w, ssd_conv_b, ssd_dt_bias, ssd_a_log, ssd_d, ssd_norm_w, w_ssd_branch, w_attn_branch, attn_sinks, rel_bias, gate_b, w_out, norm_ffn_w, w_ffn_in, ffn_conv_w, ffn_conv_b, w_ffn_out, norm_final_w, loss_target, m_meta_tokens, m_norm_mix_w, m_w_in, m_ssd_conv_w, m_ssd_conv_b, m_ssd_dt_bias, m_ssd_a_log, m_ssd_d, m_ssd_norm_w, m_w_ssd_branch, m_w_attn_branch, m_attn_sinks, m_rel_bias, m_gate_b, m_w_out, m_norm_ffn_w, m_w_ffn_in, m_ffn_conv_w, m_ffn_conv_b, m_w_ffn_out, m_norm_final_w, v_meta_tokens, v_norm_mix_w, v_w_in, v_ssd_conv_w, v_ssd_conv_b, v_ssd_dt_bias, v_ssd_a_log, v_ssd_d, v_ssd_norm_w, v_w_ssd_branch, v_w_attn_branch, v_attn_sinks, v_rel_bias, v_gate_b, v_w_out, v_norm_ffn_w, v_w_ffn_in, v_ffn_conv_w, v_ffn_conv_b, v_w_ffn_out, v_norm_final_w):
    shard = dict(meta_tokens=meta_tokens, norm_mix_w=norm_mix_w, w_in=w_in, ssd_conv_w=ssd_conv_w,
                 ssd_conv_b=ssd_conv_b, ssd_dt_bias=ssd_dt_bias, ssd_a_log=ssd_a_log, ssd_d=ssd_d,
                 ssd_norm_w=ssd_norm_w, w_ssd_branch=w_ssd_branch, w_attn_branch=w_attn_branch,
                 attn_sinks=attn_sinks, rel_bias=rel_bias, gate_b=gate_b, w_out=w_out, norm_ffn_w=norm_ffn_w,
                 w_ffn_in=w_ffn_in, ffn_conv_w=ffn_conv_w, ffn_conv_b=ffn_conv_b, w_ffn_out=w_ffn_out,
                 norm_final_w=norm_final_w)
    mom_m = dict(zip(_WEIGHTS, (m_meta_tokens, m_norm_mix_w, m_w_in, m_ssd_conv_w, m_ssd_conv_b, m_ssd_dt_bias,
                                m_ssd_a_log, m_ssd_d, m_ssd_norm_w, m_w_ssd_branch, m_w_attn_branch, m_attn_sinks,
                                m_rel_bias, m_gate_b, m_w_out, m_norm_ffn_w, m_w_ffn_in, m_ffn_conv_w, m_ffn_conv_b,
                                m_w_ffn_out, m_norm_final_w)))
    mom_v = dict(zip(_WEIGHTS, (v_meta_tokens, v_norm_mix_w, v_w_in, v_ssd_conv_w, v_ssd_conv_b, v_ssd_dt_bias,
                                v_ssd_a_log, v_ssd_d, v_ssd_norm_w, v_w_ssd_branch, v_w_attn_branch, v_attn_sinks,
                                v_rel_bias, v_gate_b, v_w_out, v_norm_ffn_w, v_w_ffn_in, v_ffn_conv_w, v_ffn_conv_b,
                                v_w_ffn_out, v_norm_final_w)))
    orig_shape = {k: a.shape for k, a in shard.items()}
    two_d = {k: a.reshape(a.shape[-2:]) if a.ndim >= 2 else a.reshape(1, -1) for k, a in shard.items()}
    shape2 = {k: a.shape for k, a in two_d.items()}

    def as2d(tree):
        return {k: tree[k].reshape(shape2[k]) for k in _WEIGHTS}

    mom_m, mom_v = as2d(mom_m), as2d(mom_v)

    exchanges = _LateExchanges(two_d, shape2)
    row_pack = exchanges.row_pack
    small_pack = _pack_rows([two_d[k] for k in _SMALL_SHARDED], LANES, SMALL_ROW_MULT)
    w_in_all, small_all = _gather_two_level([two_d["w_in"].astype(BF16), small_pack], name="gather_weights")
    full = {k: two_d[k] for k in _SMALL_REPLICATED}
    full["in_segs"] = _w_in_to_segments(w_in_all)
    small_flat = small_all.reshape(N_DEV, -1)
    off = 0
    for k in _SMALL_SHARDED:
        size = int(np.prod(shape2[k]))
        full[k] = _gather_full(small_flat[:, off:off + size].reshape((N_DEV,) + shape2[k]), k, shape2[k])
        off += size

    loss_local, grad_x, grads = _local_step(x[0], loss_target[0], full, exchanges)

    small_names = _SMALL_SHARDED + _SMALL_REPLICATED
    small_send = _pack_rows([grads[k] for k in small_names] + [loss_local.reshape(1)], LANES, SMALL_ROW_MULT)
    in_send = _segments_to_w_in_shards(grads["in_segs"])
    from_sib = _sibling_exchange([in_send, small_send], [True, False], name="grads_to_sibling")
    parts = [_pair_sum(in_send, from_sib[0], name="pair_sum_w_in", out_dtype=BF16),
             _add(small_send, from_sib[1], name="pair_sum_small")]
    in_recv, small_recv = _chip_exchange(parts, [True, False], name="exchange_grads")
    ffn_recv, rows_recv = exchanges.early_grads_received

    big = {"w_in": _adamw(two_d["w_in"], in_recv, mom_m["w_in"], mom_v["w_in"], name="adamw_w_in"),
           "w_ffn_in": _adamw(two_d["w_ffn_in"], ffn_recv, mom_m["w_ffn_in"], mom_v["w_ffn_in"], name="adamw_w_ffn_in")}
    rows_out = _adamw(row_pack(two_d), rows_recv, row_pack(mom_m), row_pack(mom_v), name="adamw_rows")
    off = 0
    for k in _ROW_SHARDED:
        r = shape2[k][0]
        big[k] = [a[off:off + r] for a in rows_out]
        off += r
    me =4 * lax.axis_index("x") + 2 * lax.axis_index("y") + lax.axis_index("c")
    small_full_shapes = [grads[k].shape for k in small_names]
    n_small = sum(int(np.prod(s)) for s in small_full_shapes)

    def packed_small(tree):
        parts = []
        for k in small_names:
            a = tree[k]
            if k in _SMALL_SHARDED:
                fullw = jnp.zeros(grads[k].shape, F32)
                a = lax.dynamic_update_slice(fullw, a, (0, me * a.shape[1]))
            parts.append(a)
        return _pack_rows(parts + [jnp.zeros((1,), F32)], LANES, SMALL_ROW_MULT)

    g_small, d_small, m_small, v_small = _adamw(packed_small(two_d), small_recv, packed_small(mom_m),
                                                packed_small(mom_v), name="adamw_small")

    def unpack_all(which, small):
        out = {k: big[k][which] for k in _BIG}
        flat = small.reshape(-1)
        for k, a in zip(small_names, _unpack(flat, small_full_shapes)):
            if k in _SMALL_SHARDED:
                a = lax.dynamic_slice(a, (0, me * shape2[k][1]), shape2[k])
            out[k] = a
        return out, flat[n_small]

    g_all, loss = unpack_all(0, g_small)
    d_all, _ = unpack_all(1, d_small)
    m_all, _ = unpack_all(2, m_small)
    v_all, _ = unpack_all(3, v_small)

    def final(tree):
        return [tree[k].reshape(orig_shape[k]) for k in _WEIGHTS]

    return (loss, grad_x[None], *final(g_all), *final(d_all), *final(m_all), *final(v_all))
```

```python
import functools
import math

import numpy as np
import jax
import jax.numpy as jnp
from jax import lax
from jax.experimental import pallas as pl
from jax.experimental.pallas import tpu as pltpu

F32 = jnp.float32
BF16 = jnp.bfloat16
HIGHEST = lax.Precision.HIGHEST

D_MODEL = 1024
N_META = 16
BLOCK = 128
PAD = BLOCK - N_META
EPS = 1e-6
NEG = -1e30
SSD_INNER = 2 * D_MODEL
SSD_HEADDIM = 64
SSD_HEADS = SSD_INNER // SSD_HEADDIM
SSD_GROUPS = 4
SSD_HPG = SSD_HEADS // SSD_GROUPS
SSD_STATE = 128
SSD_CONV = 4
SSD_GW = SSD_HPG * SSD_HEADDIM
SSD_BC = SSD_GROUPS * SSD_STATE
SSD_XBC = SSD_INNER + 2 * SSD_BC
ATT_HEADS = 16
ATT_KV_HEADS = 2
ATT_HEADDIM = 64
ATT_GQ = ATT_HEADS // ATT_KV_HEADS
ATT_Q = ATT_HEADS * ATT_HEADDIM
ATT_KV = ATT_KV_HEADS * ATT_HEADDIM
REL_BUCKETS = 32
REL_MAX_DIST = 128
D_FF = 2816
FFN_CONV = 3
ADAM_LR = 0.001
ADAM_B1 = 0.9
ADAM_B2 = 0.999
ADAM_EPS = 1e-08
ADAM_WD = 0.01
ADAM_STEP = 10

N_DEV = 8
LANES = 128
SUBLANES = 8
DT_W = SSD_GROUPS * LANES
VMEM_LIMIT_BYTES = 56 * 1024 * 1024
MESH = pl.DeviceIdType.MESH

SMALL_ROW_MULT = 16

N_KEYS = 3 * BLOCK
NT_ALL = 3 * N_KEYS * BLOCK
NT_TILE = 8192


def _cparams(*sem):
    return pltpu.CompilerParams(dimension_semantics=sem, vmem_limit_bytes=VMEM_LIMIT_BYTES)


def _row_tile(n, cap):
    best = None
    for t in range(16, min(n, cap) + 1, 16):
        if n % t == 0:
            best = t
    return best or n


def _col_tile(n, cap):
    for t in (1408, 1280, 1024, 768, 640, 512, 384, 256, 128):
        if t <= cap and n % t == 0:
            return t
    return n


def _sigmoid(x):
    return 0.5 * jnp.tanh(0.5 * x) + 0.5


def _silu(x):
    return x * _sigmoid(x)


def _softplus(x):
    return jnp.maximum(x, 0.0) + jnp.log(1.0 + jnp.exp(-jnp.abs(x)))


def _dot_nt(a, b):
    return lax.dot_general(a, b, (((1,), (1,)), ((), ())), preferred_element_type=F32)


def _dot_tn(a, b):
    return lax.dot_general(a, b, (((0,), (0,)), ((), ())), preferred_element_type=F32)


def _dot(a, b):
    return jnp.dot(a, b, preferred_element_type=F32)


def _bf16_terms(x, terms):
    out, rest = [], x
    for _ in range(terms):
        part = rest.astype(BF16)
        out.append(part)
        rest = rest - part.astype(F32)
    return out


def _dot_sel(x, sel, terms=3):
    return sum(_dot(part, sel) for part in _bf16_terms(x, terms))


def _sel_dot(sel, x, terms=3):
    return sum(_dot(sel, part) for part in _bf16_terms(x, terms))


def _sum_all(x):
    return jnp.sum(jnp.sum(x, axis=1, keepdims=True), axis=0, keepdims=True)


MM_ROW_CAPS = (1664, 832, 416)
MM_COL_CAP = 1408
MM_VMEM_BUDGET = 44 * 1024 * 1024


def _mm_tiles(rows, cols, vmem_bytes):
    col_cands = [t for t in (1408, 1280, 1024, 768, 640, 512, 384, 256, 128) if cols % t == 0]
    if cols <= 2 * MM_COL_CAP:
        col_cands.append(cols)
    best = None
    for cap in MM_ROW_CAPS:
        tr = _row_tile(rows, cap)
        for tc in col_cands:
            if vmem_bytes(tr, tc) <= MM_VMEM_BUDGET and (best is None or tr * tc > best[0] * best[1]):
                best = (tr, tc)
    assert best is not None, (rows, cols)
    return best


def _mm(a, b, *, name, ta=False, tb=False, c=None, mask=False, out_dtype=F32):
    if not ta:
        m, k = a.shape
        n = b.shape[0] if tb else b.shape[1]
        tm, tn = _mm_tiles(m, n, lambda t_m, t_n: 2 * (t_m * k * a.dtype.itemsize + k * t_n * b.dtype.itemsize
                                                       + t_m * t_n * (jnp.dtype(out_dtype).itemsize
                                                                      + (0 if c is None else c.dtype.itemsize)))
                           + 4 * t_m * t_n)

        def body(*refs):
            if c is None:
                a_ref, b_ref, o_ref = refs
            else:
                a_ref, b_ref, c_ref, o_ref = refs
            acc = (_dot_nt if tb else _dot)(a_ref[...].astype(BF16), b_ref[...].astype(BF16))
            if mask:
                row = pl.program_id(0) * tm + lax.broadcasted_iota(jnp.int32, (tm, 1), 0)
                acc = jnp.where(row >= PAD, acc, 0.0)
            if c is not None:
                acc = acc + c_ref[...]
            o_ref[...] = acc.astype(out_dtype)

        b_spec = pl.BlockSpec((tn, k), lambda i, j: (j, 0)) if tb else pl.BlockSpec((k, tn), lambda i, j: (0, j))
        in_specs = [pl.BlockSpec((tm, k), lambda i, j: (i, 0)), b_spec]
        args = [a, b]
        if c is not None:
            in_specs.append(pl.BlockSpec((tm, tn), lambda i, j: (i, j)))
            args.append(c)
        return pl.pallas_call(
            body, name=name, grid=(m // tm, n // tn), in_specs=in_specs,
            out_specs=pl.BlockSpec((tm, tn), lambda i, j: (i, j)),
            out_shape=jax.ShapeDtypeStruct((m, n), out_dtype),
            compiler_params=_cparams("parallel", "parallel"))(*args)

    kc, m = a.shape
    n = b.shape[1]
    tm = _col_tile(m, MM_COL_CAP)
    tk, tn = _mm_tiles(kc, n, lambda t_k, t_n: 2 * (t_k * tm * a.dtype.itemsize + t_k * t_n * b.dtype.itemsize
                                                    + 4 * tm * t_n) + 4 * tm * t_n)

    def body_t(a_ref, b_ref, o_ref):
        kk = pl.program_id(2)
        bb = b_ref[...]
        if mask:
            row = kk * tk + lax.broadcasted_iota(jnp.int32, (tk, 1), 0)
            bb = jnp.where(row >= PAD, bb, jnp.zeros_like(bb))
        p = _dot_tn(a_ref[...].astype(BF16), bb.astype(BF16))

        @pl.when(kk == 0)
        def _():
            o_ref[...] = p

        @pl.when(kk > 0)
        def _():
            o_ref[...] += p

    return pl.pallas_call(
        body_t, name=name, grid=(m // tm, n // tn, kc // tk),
        in_specs=[pl.BlockSpec((tk, tm), lambda i, j, kk: (kk, i)), pl.BlockSpec((tk, tn), lambda i, j, kk: (kk, j))],
        out_specs=pl.BlockSpec((tm, tn), lambda i, j, kk: (i, j)),
        out_shape=jax.ShapeDtypeStruct((m, n), F32),
        compiler_params=_cparams("parallel", "parallel", "arbitrary"))(a, b)


def _mm_rms_bwd(pairs, x, w, dres, *, name, side=None):
    m, d = x.shape
    tm = _row_tile(m, 416)
    n_pairs = len(pairs)

    def body(*refs):
        a_refs, b_refs = refs[:n_pairs], refs[n_pairs:2 * n_pairs]
        x_ref, w_ref, dres_ref, dx_ref, dw_ref = refs[2 * n_pairs:]
        i = pl.program_id(0)
        dyv = None
        for a_ref, b_ref in zip(a_refs, b_refs):
            term = _dot_nt(a_ref[...].astype(BF16), b_ref[...])
            dyv = term if dyv is None else dyv + term
        xv = x_ref[...]
        r = lax.rsqrt(jnp.mean(xv * xv, axis=-1, keepdims=True) + EPS)
        xh = xv * r
        g = dyv * w_ref[...]
        dx_ref[...] = r * (g - xh * jnp.mean(g * xh, axis=-1, keepdims=True)) + dres_ref[...]
        part = jnp.sum(dyv * xh, axis=0, keepdims=True)

        @pl.when(i == 0)
        def _():
            dw_ref[...] = part

        @pl.when(i > 0)
        def _():
            dw_ref[...] += part

    row = pl.BlockSpec((tm, d), lambda i: (i, 0))
    vec = pl.BlockSpec((1, d), lambda i: (0, 0))
    in_specs = ([pl.BlockSpec((tm, a.shape[1]), lambda i: (i, 0)) for a, _ in pairs]
                + [pl.BlockSpec(b.shape, lambda i: (0, 0), pipeline_mode=pl.Buffered(1)) for _, b in pairs]
                + [row, vec, row])
    return _call_with_side(
        body, side, name=name, grid=(m // tm,), in_specs=in_specs, out_specs=[row, vec],
        out_shape=[jax.ShapeDtypeStruct((m, d), F32), jax.ShapeDtypeStruct((1, d), F32)], scratch_shapes=[],
        args=[a for a, _ in pairs] + [b for _, b in pairs] + [x, w, dres], semantics=("arbitrary",))


def _rms_fwd(h, w, *, name):
    n, d = h.shape
    tm = _row_tile(n, 832)

    def body(h_ref, w_ref, o_ref):
        x = h_ref[...]
        r = lax.rsqrt(jnp.mean(x * x, axis=-1, keepdims=True) + EPS)
        o_ref[...] = (x * r * w_ref[...]).astype(BF16)

    return pl.pallas_call(
        body, name=name, grid=(n // tm,),
        in_specs=[pl.BlockSpec((tm, d), lambda i: (i, 0)), pl.BlockSpec((1, d), lambda i: (0, 0))],
        out_specs=pl.BlockSpec((tm, d), lambda i: (i, 0)),
        out_shape=jax.ShapeDtypeStruct((n, d), BF16),
        compiler_params=_cparams("parallel"))(h, w)


def _final_loss(h, w, target):
    n, d = h.shape
    nb = n // BLOCK

    def body(h_ref, w_ref, t_ref, dh_ref, loss_ref, dw_ref):
        i = pl.program_id(0)
        xv = h_ref[...]
        r = lax.rsqrt(jnp.mean(xv * xv, axis=-1, keepdims=True) + EPS)
        xh = xv * r
        wv = w_ref[...]
        err = jnp.where(i >= 1, xh * wv - t_ref[...], 0.0)
        dyv = err * (1.0 / d)
        g = dyv * wv
        dh_ref[...] = r * (g - xh * jnp.mean(g * xh, axis=-1, keepdims=True))
        lpart = jnp.broadcast_to(0.5 * _sum_all(err * err) * (1.0 / d), (1, LANES))
        wpart = jnp.sum(dyv * xh, axis=0, keepdims=True)

        @pl.when(i == 0)
        def _():
            loss_ref[...] = lpart
            dw_ref[...] = wpart

        @pl.when(i > 0)
        def _():
            loss_ref[...] += lpart
            dw_ref[...] += wpart

    row = pl.BlockSpec((BLOCK, d), lambda i: (i, 0))
    vec = pl.BlockSpec((1, d), lambda i: (0, 0))
    return pl.pallas_call(
        body, name="final_loss", grid=(nb,),
        in_specs=[row, vec, pl.BlockSpec((BLOCK, d), lambda i: (jnp.maximum(i - 1, 0), 0))],
        out_specs=[row, pl.BlockSpec((1, LANES), lambda i: (0, 0)), vec],
        out_shape=[jax.ShapeDtypeStruct((n, d), F32), jax.ShapeDtypeStruct((1, LANES), F32),
                   jax.ShapeDtypeStruct((1, d), F32)],
        compiler_params=_cparams("arbitrary"))(h, w, target)


def _main_spec(tm, cb, off=0):
    return pl.BlockSpec((tm, cb), lambda j, i: (i, j + off))


def _prev_spec(tm, cb, off=0):
    r8 = tm // SUBLANES
    return pl.BlockSpec((SUBLANES, cb), lambda j, i: (jnp.maximum(i * r8 - 1, 0), j + off))


def _next_spec(tm, cb, n_rows, off=0):
    r8 = tm // SUBLANES
    last = n_rows // SUBLANES - 1
    return pl.BlockSpec((SUBLANES, cb), lambda j, i: (jnp.minimum((i + 1) * r8, last), j + off))


def _with_prev(prev_ref, main_ref, i):
    prev = jnp.where(i > 0, prev_ref[...], 0.0)
    return jnp.concatenate([prev, main_ref[...]], axis=0)


def _with_next(main, nxt, i, n_tiles):
    return jnp.concatenate([main, jnp.where(i < n_tiles - 1, nxt, 0.0)], axis=0)


def _back(xx, s, tm):
    if s == 0:
        return xx[SUBLANES:SUBLANES + tm]
    return pltpu.roll(xx, s, 0)[SUBLANES:SUBLANES + tm]


def _ahead(xx, s, tm):
    if s == 0:
        return xx[:tm]
    return pltpu.roll(xx, tm + SUBLANES - s, 0)[:tm]


def _conv_fwd(x, w, b, *, name):
    n, cdim = x.shape
    kw = w.shape[0]
    tm = _row_tile(n, 832)
    cb = _col_tile(cdim, 512)

    def body(xp_ref, x_ref, w_ref, b_ref, o_ref):
        xx = _with_prev(xp_ref, x_ref, pl.program_id(1))
        acc = jnp.broadcast_to(b_ref[...], (tm, cb))
        for k in range(kw):
            acc = acc + w_ref[k:k + 1, :] * _back(xx, kw - 1 - k, tm)
        o_ref[...] = acc

    return pl.pallas_call(
        body, name=name, grid=(cdim // cb, n // tm),
        in_specs=[_prev_spec(tm, cb), _main_spec(tm, cb), pl.BlockSpec((kw, cb), lambda j, i: (0, j)),
                  pl.BlockSpec((1, cb), lambda j, i: (0, j))],
        out_specs=_main_spec(tm, cb),
        out_shape=jax.ShapeDtypeStruct((n, cdim), F32),
        compiler_params=_cparams("parallel", "parallel"))(x, x, w, b)


def _conv_bwd_core(dpre_ext, x, w_ref, kw, tm):
    dx = None
    dws = []
    for k in range(kw):
        shifted = _ahead(dpre_ext, kw - 1 - k, tm)
        term = w_ref[k:k + 1, :] * shifted
        dx = term if dx is None else dx + term
        dws.append(jnp.sum(shifted * x, axis=0, keepdims=True))
    return dx, dws, jnp.sum(dpre_ext[:tm], axis=0, keepdims=True)


def _acc_rows(i, dw_ref, db_ref, dws, db):
    @pl.when(i == 0)
    def _():
        for k, v in enumerate(dws):
            dw_ref[k:k + 1, :] = v
        db_ref[...] = db

    @pl.when(i > 0)
    def _():
        for k, v in enumerate(dws):
            dw_ref[k:k + 1, :] += v
        db_ref[...] += db


def _conv_bwd(dpre, x, w, *, name, col0=0, into=None):
    n, cdim = x.shape
    kw = w.shape[0]
    tm = _row_tile(n, 832)
    cb = _col_tile(cdim, 512)
    nt = n // tm
    off = col0 // cb
    n_alias = 0 if into is None else 3

    def body(d_ref, dn_ref, x_ref, w_ref, *rest):
        dx_ref, dw_ref, db_ref = rest[n_alias:]
        i = pl.program_id(1)
        dpre_ext = _with_next(d_ref[...], dn_ref[...], i, nt)
        dx, dws, db = _conv_bwd_core(dpre_ext, x_ref[...], w_ref, kw, tm)
        dx_ref[...] = dx.astype(BF16)
        _acc_rows(i, dw_ref, db_ref, dws, db)

    wspec = pl.BlockSpec((kw, cb), lambda j, i: (0, j + off))
    bspec = pl.BlockSpec((1, cb), lambda j, i: (0, j + off))
    return pl.pallas_call(
        body, name=name, grid=(dpre.shape[1] // cb, nt),
        in_specs=[_main_spec(tm, cb), _next_spec(tm, cb, n), _main_spec(tm, cb, off), wspec]
        + [pl.BlockSpec(memory_space=pl.ANY)] * n_alias,
        out_specs=[_main_spec(tm, cb, off), wspec, bspec],
        out_shape=[jax.ShapeDtypeStruct((n, cdim), BF16), jax.ShapeDtypeStruct((kw, cdim), F32),
                   jax.ShapeDtypeStruct((1, cdim), F32)],
        input_output_aliases={4 + k: k for k in range(n_alias)},
        compiler_params=_cparams("parallel", "arbitrary"))(dpre, dpre, x, w, *(into or ()))


def _ffn_act_fwd(x, w, b):
    n = x.shape[0]
    kw = w.shape[0]
    tm = _row_tile(n, 832)
    cb = _col_tile(D_FF, 256)
    nc = D_FF // cb

    def body(xpu_ref, xu_ref, xpg_ref, xg_ref, wu_ref, wg_ref, bu_ref, bg_ref, hu_ref, hg_ref, act_ref):
        i = pl.program_id(1)
        outs = []
        for xp_ref, x_ref, w_ref, b_ref in ((xpu_ref, xu_ref, wu_ref, bu_ref), (xpg_ref, xg_ref, wg_ref, bg_ref)):
            xx = _with_prev(xp_ref, x_ref, i)
            acc = jnp.broadcast_to(b_ref[...], (tm, cb))
            for k in range(kw):
                acc = acc + w_ref[k:k + 1, :] * _back(xx, kw - 1 - k, tm)
            outs.append(acc)
        hu_ref[...] = outs[0]
        hg_ref[...] = outs[1]
        act_ref[...] = (_silu(outs[1]) * outs[0]).astype(BF16)

    def wspec(off):
        return pl.BlockSpec((kw, cb), lambda j, i: (0, j + off))

    def bspec(off):
        return pl.BlockSpec((1, cb), lambda j, i: (0, j + off))

    out = _main_spec(tm, cb)
    return pl.pallas_call(
        body, name="ffn_act_fwd", grid=(nc, n // tm),
        in_specs=[_prev_spec(tm, cb), _main_spec(tm, cb), _prev_spec(tm, cb, nc), _main_spec(tm, cb, nc),
                  wspec(0), wspec(nc), bspec(0), bspec(nc)],
        out_specs=[out, out, out],
        out_shape=[jax.ShapeDtypeStruct((n, D_FF), F32), jax.ShapeDtypeStruct((n, D_FF), F32),
                   jax.ShapeDtypeStruct((n, D_FF), BF16)],
        compiler_params=_cparams("parallel", "parallel"))(x, x, x, x, w, w, b, b)


def _ffn_act_bwd(dact, hu, hg, x, w):
    n = x.shape[0]
    kw = w.shape[0]
    tm = _row_tile(n, 832)
    cb = _col_tile(D_FF, 256)
    nc = D_FF // cb
    nt = n // tm

    def body(d_ref, dn_ref, hu_ref, hun_ref, hg_ref, hgn_ref, xu_ref, xg_ref, wu_ref, wg_ref,
             dxu_ref, dxg_ref, dwu_ref, dwg_ref, dbu_ref, dbg_ref):
        i = pl.program_id(1)
        dact_e = _with_next(d_ref[...], dn_ref[...], i, nt)
        up_e = _with_next(hu_ref[...], hun_ref[...], i, nt)
        gate_e = _with_next(hg_ref[...], hgn_ref[...], i, nt)
        sg = _sigmoid(gate_e)
        dup_e = dact_e * (gate_e * sg)
        dgate_e = dact_e * up_e * (sg * (1.0 + gate_e * (1.0 - sg)))
        dx, dws, db = _conv_bwd_core(dup_e, xu_ref[...], wu_ref, kw, tm)
        dxu_ref[...] = dx.astype(BF16)
        _acc_rows(i, dwu_ref, dbu_ref, dws, db)
        dx, dws, db = _conv_bwd_core(dgate_e, xg_ref[...], wg_ref, kw, tm)
        dxg_ref[...] = dx.astype(BF16)
        _acc_rows(i, dwg_ref, dbg_ref, dws, db)

    main, nxt = _main_spec(tm, cb), _next_spec(tm, cb, n)
    wspec0 = pl.BlockSpec((kw, cb), lambda j, i: (0, j))
    wspec1 = pl.BlockSpec((kw, cb), lambda j, i: (0, j + nc))
    bspec = pl.BlockSpec((1, cb), lambda j, i: (0, j))
    return pl.pallas_call(
        body, name="ffn_act_bwd", grid=(nc, nt),
        in_specs=[main, nxt, main, nxt, main, nxt, _main_spec(tm, cb), _main_spec(tm, cb, nc), wspec0, wspec1],
        out_specs=[main, main, wspec0, wspec0, bspec, bspec],
        out_shape=[jax.ShapeDtypeStruct((n, D_FF), BF16), jax.ShapeDtypeStruct((n, D_FF), BF16),
                   jax.ShapeDtypeStruct((kw, D_FF), F32), jax.ShapeDtypeStruct((kw, D_FF), F32),
                   jax.ShapeDtypeStruct((1, D_FF), F32), jax.ShapeDtypeStruct((1, D_FF), F32)],
        compiler_params=_cparams("parallel", "arbitrary"))(dact, dact, hu, hu, hg, hg, x, x, w, w)


def _ssd_prep(pxs_ref, pb_ref, pc_ref, dtr_ref, dtb_ref, alog_ref, c):
    xs = _silu(pxs_ref[...])
    bm = _silu(pb_ref[...])
    cm = _silu(pc_ref[...])
    return (xs, bm, cm) + _ssd_decay(dtr_ref, dtb_ref, alog_ref, c)


def _ssd_decay(dtr_ref, dtb_ref, alog_ref, c):
    row =lax.broadcasted_iota(jnp.int32, (BLOCK, 1), 0) + c * BLOCK
    valid = (row >= PAD).astype(F32)
    dtr = dtr_ref[...] + dtb_ref[...]
    dt = _softplus(dtr) * valid
    a = -jnp.exp(alog_ref[...])
    lam = dt * a
    ri = lax.broadcasted_iota(jnp.int32, (BLOCK, BLOCK), 0)
    ci = lax.broadcasted_iota(jnp.int32, (BLOCK, BLOCK), 1)
    causal = ci <= ri
    cs = _sel_dot(causal.astype(BF16), lam)
    return valid, dtr, dt, a, lam, cs, causal


def _head_cols(r):
    return slice(SSD_HEADDIM * r, SSD_HEADDIM * (r + 1))


def _ssd_specs(nc, rev):
    def cidx(c):
        return nc - 1 - c if rev else c

    xs = pl.BlockSpec((BLOCK, SSD_GW), lambda g, c: (cidx(c), g))
    bspec = pl.BlockSpec((BLOCK, SSD_STATE), lambda g, c: (cidx(c), SSD_INNER // SSD_STATE + g))
    cspec = pl.BlockSpec((BLOCK, SSD_STATE), lambda g, c: (cidx(c), (SSD_INNER + SSD_BC) // SSD_STATE + g))
    lane = pl.BlockSpec((BLOCK, LANES), lambda g, c: (cidx(c), g))
    vec = pl.BlockSpec((1, LANES), lambda g, c: (0, g))
    wide_vec = pl.BlockSpec((1, SSD_GW), lambda g, c: (0, g))
    hsave = pl.BlockSpec((1, 1, SSD_GW, SSD_STATE), lambda g, c: (cidx(c), g, 0, 0))
    return xs, bspec, cspec, lane, vec, wide_vec, hsave


def _head_spread_matrix():
    r = lax.broadcasted_iota(jnp.int32, (LANES, SSD_GW), 0)
    col = lax.broadcasted_iota(jnp.int32, (LANES, SSD_GW), 1)
    return (col // SSD_HEADDIM == r).astype(BF16)


def _const_spec(shape):
    return pl.BlockSpec(shape, lambda g, c: (0,) * len(shape))


def _spread_heads(per_head, e_ref):
    wide = _dot_sel(jnp.concatenate(per_head, axis=0), e_ref[...])
    return [wide[BLOCK * k:BLOCK * (k + 1)] for k in range(len(per_head))]


def _call_with_side(body, side, *, name, grid, in_specs, out_specs, out_shape, scratch_shapes, args,
                    semantics=("parallel", "arbitrary")):
    if side is None:
        outs = pl.pallas_call(body, name=name, grid=grid, in_specs=in_specs, out_specs=out_specs, out_shape=out_shape,
                              scratch_shapes=scratch_shapes, compiler_params=_cparams(*semantics))(*args)
        return outs, []
    n_in, n_out, n_scr, n_side = len(in_specs), len(out_specs), len(scratch_shapes), len(side.arrays)

    def body_with_side(*refs):
        ins, rest = refs[:n_in + n_side], refs[n_in + n_side:]
        outs, scratch = rest[:n_out + n_side], rest[n_out + n_side:]
        side_refs = (ins[n_in:], outs[n_out:], scratch[n_scr:])
        ids = [pl.program_id(k) for k in range(len(grid))]
        inner_first = functools.reduce(jnp.logical_and, [i == 0 for i in ids[1:]], True)

        @pl.when((ids[0] == 0) & inner_first)
        def _():
            side.phases[0](*side_refs)

        body(*ins[:n_in], *outs[:n_out], *scratch[:n_scr])

        @pl.when((ids[0] == grid[0] // 2) & inner_first)
        def _():
            side.phases[1](*side_refs)

        @pl.when(functools.reduce(jnp.logical_and, [i == n - 1 for i, n in zip(ids, grid)]))
        def _():
            side.phases[2](*side_refs)

    any_spec = pl.BlockSpec(memory_space=pl.ANY)
    outs = pl.pallas_call(
        body_with_side, name=name, grid=grid, in_specs=list(in_specs) + [any_spec] * n_side,
        out_specs=list(out_specs) + [any_spec] * n_side, out_shape=list(out_shape) + list(side.out_shape),
        scratch_shapes=list(scratch_shapes) + list(side.scratch_shapes),
        compiler_params=_cparams(*["arbitrary"] * len(grid)))(*args, *side.arrays)
    return outs[:n_out], outs[n_out:]


def _ssd_fwd(pre, dt_raw, z, dtb, alog, dskip_w, norm_w, side=None):
    n = pre.shape[0]
    nc = n // BLOCK
    xs_s, b_s, c_s, lane_s, vec_s, wide_s, hs_s = _ssd_specs(nc, False)

    def body(pxs_ref, pb_ref, pc_ref, dtr_ref, z_ref, dtb_ref, alog_ref, dskw_ref, nw_ref, e_ref,
             y_ref, yn_ref, hs_ref, h_scr):
        c = pl.program_id(1)

        @pl.when(c == 0)
        def _():
            h_scr[...] = jnp.zeros_like(h_scr)

        xs, bm, cm, _, _, dt, _, _, cs, causal = _ssd_prep(pxs_ref, pb_ref, pc_ref, dtr_ref, dtb_ref, alog_ref, c)
        cst = cs.T
        cs_last = cs[BLOCK - 1:BLOCK, :]
        dt_w, ecs_w, dec_w = _spread_heads([dt, jnp.exp(cs), jnp.exp(cs_last - cs)], e_ref)
        xdt = xs * dt_w
        bmb = bm.astype(BF16)
        cmb = cm.astype(BF16)
        cb = _dot_nt(cmb, bmb)
        hg = h_scr[...]
        hs_ref[0, 0] = hg
        y = _dot_nt(cmb, hg.astype(BF16)) * ecs_w + dskw_ref[...] * xs
        first = lax.broadcasted_iota(jnp.int32, (BLOCK, LANES), 1) < SSD_HEADDIM
        diag = []
        for j in range(SSD_HPG // 2):
            xp = xdt[:, LANES * j:LANES * (j + 1)].astype(BF16)
            res = []
            for r in (2 * j, 2 * j + 1):
                lm = jnp.exp(jnp.where(causal, cs[:, r:r + 1] - cst[r:r + 1, :], NEG))
                res.append(_dot((cb * lm).astype(BF16), xp))
            diag.append(jnp.where(first, res[0], res[1]))
        y = y + jnp.concatenate(diag, axis=1)
        st = _dot_tn((xdt * dec_w).astype(BF16), bmb)
        eh = jnp.exp(cs_last)
        for r in range(SSD_HPG):
            rows = _head_cols(r)
            h_scr[rows, :] = hg[rows, :] * eh[:, r:r + 1] + st[rows, :]
        y_ref[...] = y
        gts = y * _silu(z_ref[...])
        rr = lax.rsqrt(jnp.mean(gts * gts, axis=-1, keepdims=True) + EPS)
        yn_ref[...] = (gts * rr * nw_ref[...]).astype(BF16)

    return _call_with_side(
        body, side, name="ssd_fwd", grid=(SSD_GROUPS, nc),
        in_specs=[xs_s, b_s, c_s, lane_s, xs_s, vec_s, vec_s, wide_s, wide_s, _const_spec((LANES, SSD_GW))],
        out_specs=[xs_s, xs_s, hs_s],
        out_shape=[jax.ShapeDtypeStruct((n, SSD_INNER), F32), jax.ShapeDtypeStruct((n, SSD_INNER), BF16),
                   jax.ShapeDtypeStruct((nc, SSD_GROUPS, SSD_GW, SSD_STATE), F32)],
        scratch_shapes=[pltpu.VMEM((SSD_GW, SSD_STATE), F32)],
        args=(pre, pre, pre, dt_raw, z, dtb, alog, dskip_w, norm_w, _head_spread_matrix()))


def _lane_put(acc, col, r):
    lane = lax.broadcasted_iota(jnp.int32, acc.shape, 1)
    return jnp.where(lane == r, col, acc)


def _ssd_bwd(dyn, y, z, pre, dt_raw, hsave, dtb, alog, dskip_w, norm_w, side=None):
    n = pre.shape[0]
    nc = n // BLOCK
    spread = _head_spread_matrix()
    xs_s, b_s, c_s, lane_s, vec_s, wide_s, hs_s = _ssd_specs(nc, True)
    bc_out =pl.BlockSpec((BLOCK, SSD_STATE), lambda g, c: (nc - 1 - c, g))

    def body(dyn_ref, y_ref, z_ref, pxs_ref, pb_ref, pc_ref, dtr_ref, hs_ref, dtb_ref, alog_ref, dskw_ref, nw_ref,
             e_ref, r_ref,
             dz_ref, dxs_ref, dbm_ref, dcm_ref, ddt_ref, dnw_ref, ddtb_ref, dalog_ref, ddsk_ref, g_scr):
        step = pl.program_id(1)
        c = nc - 1 - step

        @pl.when(step == 0)
        def _():
            g_scr[...] = jnp.zeros_like(g_scr)

        pxs, pb, pc = pxs_ref[...], pb_ref[...], pc_ref[...]
        sx, sb, sc = _sigmoid(pxs), _sigmoid(pb), _sigmoid(pc)
        xs, bm, cm = pxs * sx, pb * sb, pc * sc
        valid, dtr, dt, a, lam, cs, causal = _ssd_decay(dtr_ref, dtb_ref, alog_ref, c)
        cst = cs.T
        cs_last = cs[BLOCK - 1:BLOCK, :]
        bmb = bm.astype(BF16)
        cmb = cm.astype(BF16)
        cb = _dot_nt(cmb, bmb)
        hg = hs_ref[0, 0]
        hgb = hg.astype(BF16)
        yoff = _dot_nt(cmb, hgb)
        gn = g_scr[...]
        gnb = gn.astype(BF16)

        zv = z_ref[...]
        yv = y_ref[...]
        sgz = _sigmoid(zv)
        sz = zv * sgz
        gts = yv * sz
        rr = lax.rsqrt(jnp.mean(gts * gts, axis=-1, keepdims=True) + EPS)
        xh = gts * rr
        dynv = dyn_ref[...]
        gg = dynv * nw_ref[...]
        dgts = rr * (gg - xh * jnp.mean(gg * xh, axis=-1, keepdims=True))
        dnw = jnp.sum(dynv * xh, axis=0, keepdims=True)
        dy = dgts * sz
        dz_ref[...] = (dgts * yv * (sgz * (1.0 + zv * (1.0 - sgz)))).astype(BF16)

        ecs = jnp.exp(cs)
        dec = jnp.exp(cs_last - cs)
        eh = jnp.exp(cs_last)
        dt_w, ecs_w, dec_w = _spread_heads([dt, ecs, dec], e_ref)
        red_m = r_ref[...]

        def head_sums(v):
            return _dot_sel(v, red_m, terms=2)

        xdt = xs * dt_w
        q_all = _dot_nt(bmb, gnb)
        w_all = (dy * ecs_w).astype(BF16)
        e_hl = head_sums(q_all * xdt) * dec
        dcs_col = head_sums(dy * yoff) * ecs - e_hl
        gh = jnp.zeros((1, LANES), F32)
        prod = gn * hg
        for r in range(SSD_HPG):
            gh = _lane_put(gh, _sum_all(prod[_head_cols(r), :]), r)
        dcs_last = jnp.sum(e_hl, axis=0, keepdims=True) + eh * gh
        ddsk = jnp.sum(head_sums(dy * xs), axis=0, keepdims=True)
        cbt = _dot_nt(bmb, cmb)
        lane = lax.broadcasted_iota(jnp.int32, (BLOCK, LANES), 1)
        first = lane < SSD_HEADDIM
        causal_t = lax.broadcasted_iota(jnp.int32, (BLOCK, BLOCK), 1) >= lax.broadcasted_iota(
            jnp.int32, (BLOCK, BLOCK), 0)
        sub = lax.broadcasted_iota(jnp.int32, (SUBLANES, BLOCK), 0)
        dcs_row = jnp.zeros((SUBLANES, BLOCK), F32)
        dcb = jnp.zeros((BLOCK, BLOCK), F32)
        dxdt_pairs = []
        for j in range(SSD_HPG // 2):
            tile = slice(LANES * j, LANES * (j + 1))
            dy_p = dy[:, tile]
            dyb = dy_p.astype(BF16)
            xdtb = xdt[:, tile].astype(BF16)
            res = []
            for half, r in enumerate((2 * j, 2 * j + 1)):
                csc, csr = cs[:, r:r + 1], cst[r:r + 1, :]
                lm = jnp.exp(jnp.where(causal, csc - csr, NEG))
                lmt = jnp.exp(jnp.where(causal_t, csr - csc, NEG))
                keep = first if half == 0 else jnp.logical_not(first)
                gm = _dot_nt(jnp.where(keep, dy_p, 0.0).astype(BF16), xdtb) * lm
                dcb = dcb + gm
                mm_ = gm * cb
                dcs_col = dcs_col + jnp.where(lane == r, jnp.sum(mm_, axis=1, keepdims=True), 0.0)
                dcs_row = jnp.where(sub == r, jnp.sum(mm_, axis=0, keepdims=True), dcs_row)
                res.append(_dot((cbt * lmt).astype(BF16), dyb))
            dxdt_pairs.append(jnp.where(first, res[0], res[1]))
        dxdt = jnp.concatenate(dxdt_pairs, axis=1) + q_all * dec_w
        ddt_x = head_sums(dxdt * xs)
        dxs = dxdt * dt_w + dskw_ref[...] * dy
        dcbb = dcb.astype(BF16)
        dcm = _dot(w_all, hgb) + _dot(dcbb, bmb)
        dbm = _dot((xdt * dec_w).astype(BF16), gnb) + _dot_tn(dcbb, cmb)
        dh_off = _dot_tn(w_all, cmb)
        for r in range(SSD_HPG):
            rows = _head_cols(r)
            g_scr[rows, :] = gn[rows, :] * eh[:, r:r + 1] + dh_off[rows, :]

        pad_rows = jnp.zeros((BLOCK - SUBLANES, BLOCK), F32)
        dcs = dcs_col - jnp.concatenate([dcs_row, pad_rows], axis=0).T
        rsel = lax.broadcasted_iota(jnp.int32, (BLOCK, LANES), 0)
        dcs = dcs + jnp.where(rsel == BLOCK - 1, dcs_last, 0.0)
        ri = lax.broadcasted_iota(jnp.int32, (BLOCK, BLOCK), 0)
        ci = lax.broadcasted_iota(jnp.int32, (BLOCK, BLOCK), 1)
        dlam = _sel_dot((ci >= ri).astype(BF16), dcs)
        head = lane < SSD_HPG
        ddt = dlam * a + ddt_x
        ddtr = jnp.where(head, ddt * _sigmoid(dtr) * valid, 0.0)
        ddt_ref[...] = ddtr.astype(BF16)
        dalog = jnp.sum(jnp.where(head, dlam * lam, 0.0), axis=0, keepdims=True)
        ddtb = jnp.sum(ddtr, axis=0, keepdims=True)

        dxs_ref[...] = dxs * (sx * (1.0 + pxs * (1.0 - sx)))
        dbm_ref[...] = dbm * (sb * (1.0 + pb * (1.0 - sb)))
        dcm_ref[...] = dcm * (sc * (1.0 + pc * (1.0 - sc)))

        @pl.when(step == 0)
        def _():
            dnw_ref[...] = dnw
            ddtb_ref[...] = ddtb
            dalog_ref[...] = dalog
            ddsk_ref[...] = ddsk

        @pl.when(step > 0)
        def _():
            dnw_ref[...] += dnw
            ddtb_ref[...] += ddtb
            dalog_ref[...] += dalog
            ddsk_ref[...] += ddsk

    return _call_with_side(
        body, side, name="ssd_bwd", grid=(SSD_GROUPS, nc),
        in_specs=[xs_s, xs_s, xs_s, xs_s, b_s, c_s, lane_s, hs_s, vec_s, vec_s, wide_s, wide_s,
                  _const_spec((LANES, SSD_GW)), _const_spec((SSD_GW, LANES))],
        out_specs=[xs_s, xs_s, bc_out, bc_out, lane_s, wide_s, vec_s, vec_s, vec_s],
        out_shape=[jax.ShapeDtypeStruct((n, SSD_INNER), BF16), jax.ShapeDtypeStruct((n, SSD_INNER), F32),
                   jax.ShapeDtypeStruct((n, SSD_BC), F32), jax.ShapeDtypeStruct((n, SSD_BC), F32),
                   jax.ShapeDtypeStruct((n, DT_W), BF16), jax.ShapeDtypeStruct((1, SSD_INNER), F32),
                   jax.ShapeDtypeStruct((1, DT_W), F32), jax.ShapeDtypeStruct((1, DT_W), F32),
                   jax.ShapeDtypeStruct((1, DT_W), F32)],
        scratch_shapes=[pltpu.VMEM((SSD_GW, SSD_STATE), F32)],
        args=(dyn, y, z, pre, pre, pre, dt_raw, hsave, dtb, alog, dskip_w, norm_w, spread, spread.T))


def _bucket_table():
    def bucket(dist):
        d = np.maximum(dist, 0)
        half = REL_BUCKETS // 2
        big = half + (np.log(np.maximum(d, half).astype(np.float32) / np.float32(half))
                      / np.float32(math.log(REL_MAX_DIST / half)) * np.float32(REL_BUCKETS - half)).astype(np.int32)
        return np.where(d < half, d, np.minimum(big, REL_BUCKETS - 1)).astype(np.int32)

    l = np.arange(BLOCK)[None, :]
    band = bucket(l + BLOCK - np.arange(2 * BLOCK)[:, None])
    j = np.arange(BLOCK)[:, None]
    tables = [np.concatenate([bucket(v * BLOCK + l - j), band], axis=0) for v in range(3)]
    return np.concatenate([t.reshape(-1) for t in tables])


def _onehot_t():
    buckets = jnp.asarray(_bucket_table())
    return (buckets[None, :] == jnp.arange(REL_BUCKETS, dtype=jnp.int32)[:, None]).astype(F32)


def _bias_tables(rel_t, onehot_t):
    def body(r_ref, oh_ref, o_ref):
        o_ref[...] = jnp.dot(r_ref[...], oh_ref[...], precision=HIGHEST, preferred_element_type=F32)

    return pl.pallas_call(
        body, name="bias_tables", grid=(NT_ALL // NT_TILE,),
        in_specs=[pl.BlockSpec((ATT_HEADS, REL_BUCKETS), lambda i: (0, 0)),
                  pl.BlockSpec((REL_BUCKETS, NT_TILE), lambda i: (0, i))],
        out_specs=pl.BlockSpec((ATT_HEADS, NT_TILE), lambda i: (0, i)),
        out_shape=jax.ShapeDtypeStruct((ATT_HEADS, NT_ALL), F32),
        compiler_params=_cparams("parallel"))(rel_t, onehot_t)


def _bias_grad(dtab, onehot_t):
    def body(d_ref, oh_ref, o_ref):
        i = pl.program_id(0)
        p = lax.dot_general(d_ref[...], oh_ref[...], (((1,), (1,)), ((), ())), precision=HIGHEST,
                            preferred_element_type=F32)

        @pl.when(i == 0)
        def _():
            o_ref[...] = p

        @pl.when(i > 0)
        def _():
            o_ref[...] += p

    return pl.pallas_call(
        body, name="bias_grad", grid=(NT_ALL // NT_TILE,),
        in_specs=[pl.BlockSpec((ATT_HEADS, NT_TILE), lambda i: (0, i)),
                  pl.BlockSpec((REL_BUCKETS, NT_TILE), lambda i: (0, i))],
        out_specs=pl.BlockSpec((ATT_HEADS, REL_BUCKETS), lambda i: (0, 0)),
        out_shape=jax.ShapeDtypeStruct((ATT_HEADS, REL_BUCKETS), F32),
        compiler_params=_cparams("arbitrary"))(dtab, onehot_t)


def _att_mask_t(n, copies):
    far = 4 * BLOCK
    kk = lax.broadcasted_iota(jnp.int32, (N_KEYS, copies * BLOCK), 0)
    li = lax.broadcasted_iota(jnp.int32, (N_KEYS, copies * BLOCK), 1) & (BLOCK - 1)
    meta_ok = (kk >= PAD) & (kk < BLOCK) & (li + jnp.where(n >= 1, far, 0) >= kk)
    prev_ok = (kk >= BLOCK) & (kk < 2 * BLOCK) & (kk - BLOCK > li + jnp.where(n >= 2, 0, far))
    cur_ok = (kk >= 2 * BLOCK) & (kk - 2 * BLOCK <= li - jnp.where(n >= 1, 0, far))
    return meta_ok | prev_ok | cur_ok


def _att_kv(meta_ref, prev_ref, cur_ref):
    kv = jnp.concatenate([meta_ref[...], prev_ref[...], cur_ref[...]], axis=0)
    first = lax.broadcasted_iota(jnp.int32, (N_KEYS, LANES), 1) < ATT_HEADDIM
    out = []
    for pair in (kv[:, :LANES], kv[:, LANES:]):
        swapped = pltpu.roll(pair, ATT_HEADDIM, 1)
        out.append([jnp.where(first, pair, swapped).astype(BF16), jnp.where(first, swapped, pair).astype(BF16)])
    return out[0], out[1]


def _split_heads(x_pair, first):
    return jnp.concatenate([jnp.where(first, x_pair, 0.0), jnp.where(first, 0.0, x_pair)], axis=0).astype(BF16)


def _att_probs_t(qm2, k_dup, t_ref, j, mask2, sink_ref):
    scale = ATT_HEADDIM ** -0.5
    bias2 = jnp.concatenate([t_ref[0, 2 * j], t_ref[0, 2 * j + 1]], axis=1)
    second = lax.broadcasted_iota(jnp.int32, (1, 2 * BLOCK), 1) >= BLOCK
    sink2 = jnp.where(second, sink_ref[0:1, 2 * j + 1:2 * j + 2], sink_ref[0:1, 2 * j:2 * j + 1])
    s_t = jnp.where(mask2, _dot_nt(k_dup, qm2) * scale + bias2, NEG)
    mx = jnp.maximum(jnp.max(s_t, axis=0, keepdims=True), sink2)
    p_t = jnp.exp(s_t - mx)
    p_s = jnp.exp(sink2 - mx)
    inv = 1.0 / (jnp.sum(p_t, axis=0, keepdims=True) + p_s)
    return p_t * inv, p_s * inv


def _att_specs(nb, rev):
    def nidx(i):
        return nb - 1 - i if rev else i

    kvb = ATT_Q // (2 * ATT_KV)
    q_s = pl.BlockSpec((BLOCK, ATT_Q), lambda i: (nidx(i), 0))
    cur = pl.BlockSpec((BLOCK, 2 * ATT_KV), lambda i: (nidx(i), kvb))
    prev = pl.BlockSpec((BLOCK, 2 * ATT_KV), lambda i: (jnp.maximum(nidx(i) - 1, 0), kvb))
    meta = pl.BlockSpec((BLOCK, 2 * ATT_KV), lambda i: (0, kvb))
    table = pl.BlockSpec((1, ATT_HEADS, N_KEYS, BLOCK), lambda i: (jnp.minimum(nidx(i), 2), 0, 0, 0))
    sink = pl.BlockSpec((1, LANES), lambda i: (0, 0))
    return q_s, cur, prev, meta, table, sink


def _attn_fwd(qkv, tables, sinks):
    n = qkv.shape[0]
    nb = n // BLOCK
    q_s, cur_s, prev_s, meta_s, t_s, sink_s = _att_specs(nb, False)

    def body(q_ref, cur_ref, prev_ref, meta_ref, t_ref, sink_ref, o_ref):
        blk = pl.program_id(0)
        mask_t = _att_mask_t(blk, 1)
        k_dup, v_dup = _att_kv(meta_ref, prev_ref, cur_ref)
        v_dup_t = [v.T for v in v_dup]
        first = lax.broadcasted_iota(jnp.int32, (BLOCK, LANES), 1) < ATT_HEADDIM
        top = lax.broadcasted_iota(jnp.int32, (LANES, BLOCK), 0) < ATT_HEADDIM
        scale = ATT_HEADDIM ** -0.5
        for j in range(ATT_HEADS // 2):
            kh = 2 * j // ATT_GQ
            tile = slice(LANES * j, LANES * (j + 1))
            q_p = q_ref[:, tile]
            res = []
            for half, h in enumerate((2 * j, 2 * j + 1)):
                qm = jnp.where(first if half == 0 else jnp.logical_not(first), q_p, 0.0).astype(BF16)
                sink = sink_ref[0:1, h:h + 1]
                s_t = jnp.where(mask_t, _dot_nt(k_dup[kh], qm) * scale + t_ref[0, h], NEG)
                mx = jnp.maximum(jnp.max(s_t, axis=0, keepdims=True), sink)
                p_t = jnp.exp(s_t - mx)
                inv = 1.0 / (jnp.sum(p_t, axis=0, keepdims=True) + jnp.exp(sink - mx))
                res.append(_dot(v_dup_t[kh], (p_t * inv).astype(BF16)))
            o_ref[:, tile] = jnp.where(top, res[0], res[1]).T.astype(BF16)

    return pl.pallas_call(
        body, name="attn_fwd", grid=(nb,),
        in_specs=[q_s, cur_s, prev_s, meta_s, t_s, sink_s],
        out_specs=q_s,
        out_shape=jax.ShapeDtypeStruct((n, ATT_Q), BF16),
        compiler_params=_cparams("parallel"))(qkv, qkv, qkv, qkv, tables, sinks)


def _attn_bwd(datt, qkv, tables, sinks):
    n = qkv.shape[0]
    nb = n // BLOCK
    q_s, cur_s, prev_s, meta_s, t_s, sink_s = _att_specs(nb, True)
    dqkv_s = pl.BlockSpec((BLOCK, ATT_Q + 2 * ATT_KV), lambda i: (nb - 1 - i, 0))
    scale = ATT_HEADDIM ** -0.5

    def body(do_ref, q_ref, cur_ref, prev_ref, meta_ref, t_ref, sink_ref,
             dqkv_ref, dt_ref, dsink_ref, carry_scr, meta_scr):
        step = pl.program_id(0)
        blk = nb - 1 - step
        mask2 = _att_mask_t(blk, 2)
        k_dup, v_dup = _att_kv(meta_ref, prev_ref, cur_ref)
        k_dup_t = [k.T for k in k_dup]

        @pl.when(step == 0)
        def _():
            carry_scr[...] = jnp.zeros_like(carry_scr)
            meta_scr[...] = jnp.zeros_like(meta_scr)
            dsink_ref[...] = jnp.zeros_like(dsink_ref)

        @pl.when((step == 0) | (blk <= 1))
        def _():
            dt_ref[...] = jnp.zeros_like(dt_ref)

        first = lax.broadcasted_iota(jnp.int32, (BLOCK, LANES), 1) < ATT_HEADDIM
        top = lax.broadcasted_iota(jnp.int32, (LANES, BLOCK), 0) < ATT_HEADDIM
        first_k = lax.broadcasted_iota(jnp.int32, (N_KEYS, LANES), 1) < ATT_HEADDIM
        dsink = jnp.zeros((1, LANES), F32)
        dk_acc = [None] * ATT_KV_HEADS
        dv_acc = [None] * ATT_KV_HEADS
        for j in range(ATT_HEADS // 2):
            kh = 2 * j // ATT_GQ
            tile = slice(LANES * j, LANES * (j + 1))
            qm2 = _split_heads(q_ref[:, tile], first)
            dom2 = _split_heads(do_ref[:, tile], first)
            p_t, p_s = _att_probs_t(qm2, k_dup[kh], t_ref, j, mask2, sink_ref)
            dp_t = _dot_nt(v_dup[kh], dom2)
            delta = jnp.sum(p_t * dp_t, axis=0, keepdims=True)
            ds_t = p_t * (dp_t - delta)
            sink_terms = p_s * delta
            for half in range(2):
                cols = slice(BLOCK * half, BLOCK * (half + 1))
                dsink = _lane_put(dsink, -jnp.sum(sink_terms[:, cols], axis=1, keepdims=True), 2 * j + half)
                dt_ref[0, 2 * j + half] += ds_t[:, cols]
            ds_tb = ds_t.astype(BF16)
            dq_t = _dot(k_dup_t[kh], ds_tb)
            dqkv_ref[:, tile] = (jnp.where(top, dq_t[:, :BLOCK], dq_t[:, BLOCK:]).T * scale).astype(BF16)
            dk_part, dv_part = _dot(ds_tb, qm2), _dot(p_t.astype(BF16), dom2)
            dk_acc[kh] = dk_part if dk_acc[kh] is None else dk_acc[kh] + dk_part
            dv_acc[kh] = dv_part if dv_acc[kh] is None else dv_acc[kh] + dv_part
        dsink_ref[...] += dsink
        folded = [a + pltpu.roll(a, ATT_HEADDIM, 1) for a in dk_acc + dv_acc]
        dkv = jnp.concatenate([jnp.where(first_k, folded[0], folded[1]) * scale,
                               jnp.where(first_k, folded[2], folded[3])], axis=1)
        meta_scr[...] += dkv[:BLOCK, :]
        own = dkv[2 * BLOCK:, :] + carry_scr[...]
        carry_scr[...] = dkv[BLOCK:2 * BLOCK, :]

        @pl.when(blk > 0)
        def _():
            dqkv_ref[:, ATT_Q:] = own.astype(BF16)

        @pl.when(blk == 0)
        def _():
            dqkv_ref[:, ATT_Q:] = (own + meta_scr[...]).astype(BF16)

    return pl.pallas_call(
        body, name="attn_bwd", grid=(nb,),
        in_specs=[q_s, q_s, cur_s, prev_s, meta_s, t_s, sink_s],
        out_specs=[dqkv_s, t_s, sink_s],
        out_shape=[jax.ShapeDtypeStruct((n, ATT_Q + 2 * ATT_KV), BF16),
                   jax.ShapeDtypeStruct((3, ATT_HEADS, N_KEYS, BLOCK), F32),
                   jax.ShapeDtypeStruct((1, LANES), F32)],
        scratch_shapes=[pltpu.VMEM((BLOCK, 2 * ATT_KV), F32), pltpu.VMEM((BLOCK, 2 * ATT_KV), F32)],
        compiler_params=_cparams("arbitrary"))(datt, qkv, qkv, qkv, qkv, tables, sinks)


def _merge_out_fwd(gates, y_ssd, y_att, gate_b, w_out, h):
    n = gates.shape[0]
    tm = _row_tile(n, 416)

    def body(gs_ref, ga_ref, ys_ref, ya_ref, gb_ref, w_ref, h_ref, m_ref, o_ref):
        merged = (_sigmoid(gs_ref[...] + gb_ref[0:1, :]) * ys_ref[...]
                  + _sigmoid(ga_ref[...] + gb_ref[1:2, :]) * ya_ref[...]).astype(BF16)
        m_ref[...] = merged
        row = pl.program_id(0) * tm + lax.broadcasted_iota(jnp.int32, (tm, 1), 0)
        o_ref[...] = jnp.where(row >= PAD, _dot(merged, w_ref[...]), 0.0) + h_ref[...]

    row = pl.BlockSpec((tm, D_MODEL), lambda i: (i, 0))
    return pl.pallas_call(
        body, name="merge_out_fwd", grid=(n // tm,),
        in_specs=[row, pl.BlockSpec((tm, D_MODEL), lambda i: (i, 1)), row, row,
                  pl.BlockSpec((2, D_MODEL), lambda i: (0, 0)), pl.BlockSpec((D_MODEL, D_MODEL), lambda i: (0, 0)), row],
        out_specs=[row, row],
        out_shape=[jax.ShapeDtypeStruct((n, D_MODEL), BF16), jax.ShapeDtypeStruct((n, D_MODEL), F32)],
        compiler_params=_cparams("parallel"))(gates, gates, y_ssd, y_att, gate_b, w_out, h)


def _merge_out_bwd(dh, w_out, gates, y_ssd, y_att, gate_b):
    n = gates.shape[0]
    tm = _row_tile(n, 416)

    def body(dh_ref, w_ref, gs_ref, ga_ref, ys_ref, ya_ref, gb_ref, dys_ref, dya_ref, dg_ref, dgb_ref):
        i = pl.program_id(0)
        row = i * tm + lax.broadcasted_iota(jnp.int32, (tm, 1), 0)
        dmv = jnp.where(row >= PAD, _dot_nt(dh_ref[...].astype(BF16), w_ref[...]), 0.0)
        ss =_sigmoid(gs_ref[...] + gb_ref[0:1, :])
        sa = _sigmoid(ga_ref[...] + gb_ref[1:2, :])
        dys_ref[...] = (dmv * ss).astype(BF16)
        dya_ref[...] = (dmv * sa).astype(BF16)
        dgs = dmv * ys_ref[...] * ss * (1.0 - ss)
        dga = dmv * ya_ref[...] * sa * (1.0 - sa)
        dg_ref[:, :D_MODEL] = dgs.astype(BF16)
        dg_ref[:, D_MODEL:] = dga.astype(BF16)
        part = jnp.concatenate([jnp.sum(dgs, axis=0, keepdims=True), jnp.sum(dga, axis=0, keepdims=True)], axis=0)

        @pl.when(i == 0)
        def _():
            dgb_ref[...] = part

        @pl.when(i > 0)
        def _():
            dgb_ref[...] += part

    row = pl.BlockSpec((tm, D_MODEL), lambda i: (i, 0))
    gb = pl.BlockSpec((2, D_MODEL), lambda i: (0, 0))
    return pl.pallas_call(
        body, name="merge_out_bwd", grid=(n // tm,),
        in_specs=[row, pl.BlockSpec((D_MODEL, D_MODEL), lambda i: (0, 0)), row,
                  pl.BlockSpec((tm, D_MODEL), lambda i: (i, 1)), row, row, gb],
        out_specs=[row, row, pl.BlockSpec((tm, 2 * D_MODEL), lambda i: (i, 0)), gb],
        out_shape=[jax.ShapeDtypeStruct((n, D_MODEL), BF16), jax.ShapeDtypeStruct((n, D_MODEL), BF16),
                   jax.ShapeDtypeStruct((n, 2 * D_MODEL), BF16), jax.ShapeDtypeStruct((2, D_MODEL), F32)],
        compiler_params=_cparams("arbitrary"))(dh, w_out, gates, gates, y_ssd, y_att, gate_b)


def _col_move(srcs, outs, pieces, *, name):
    rows = srcs[0].shape[-2]
    tr = _row_tile(rows, 128)
    n_src = len(srcs)
    covered = [sum(p[6] for p in pieces if p[0] == o) for o in range(len(outs))]
    total = [int(np.prod(shp)) // rows for shp, _ in outs]

    def body(*refs):
        in_refs, out_refs = refs[:n_src], refs[n_src:]
        for o, ref in enumerate(out_refs):
            if covered[o] != total[o]:
                ref[...] = jnp.zeros_like(ref)
        for o, ol, oc, s, sl, sc, width in pieces:
            val = in_refs[s][:, sc:sc + width] if sl is None else in_refs[s][sl, :, sc:sc + width]
            val = val.astype(outs[o][1])
            if ol is None:
                out_refs[o][:, oc:oc + width] = val
            else:
                out_refs[o][ol, :, oc:oc + width] = val

    def spec(shape):
        if len(shape) == 2:
            return pl.BlockSpec((tr, shape[1]), lambda i: (i, 0))
        return pl.BlockSpec((shape[0], tr, shape[2]), lambda i: (0, i, 0))

    return pl.pallas_call(
        body, name=name, grid=(rows // tr,),
        in_specs=[spec(a.shape) for a in srcs], out_specs=[spec(shp) for shp, _ in outs],
        out_shape=[jax.ShapeDtypeStruct(shp, dt) for shp, dt in outs],
        compiler_params=_cparams("parallel"))(*srcs)


def _shard_pieces(seg_ranges, shard_w):
    out = []
    for seg, runs in enumerate(seg_ranges):
        for g0, width, s0 in runs:
            done = 0
            while done < width:
                dev, col = divmod(g0 + done, shard_w)
                take = min(width - done, shard_w - col)
                out.append((seg, s0 + done, dev, col, take))
                done += take
    return out


_CHIP_RELATIONS = [(1, 0, 0), (0, 1, 0), (1, 1, 0)]
N_CHIPS = 4


def _gather_two_level(arrays, *, name):
    outs = _run_plan(_gather_plan(arrays), name)
    return [o.reshape((N_DEV,) + a.shape) for o, a in zip(outs, arrays)]


class _CommPlan:
    def __init__(self, arrays, out_shape, scratch_shapes, phases):
        self.arrays, self.out_shape, self.scratch_shapes, self.phases = arrays, out_shape, scratch_shapes, phases


def _run_plan(plan, name):
    n_arr = len(plan.arrays)

    def body(*refs):
        ins, outs, sems = refs[:n_arr], refs[n_arr:2 * n_arr], refs[2 * n_arr:]
        for phase in plan.phases:
            phase(ins, outs, sems)

    any_spec = pl.BlockSpec(memory_space=pl.ANY)
    return pl.pallas_call(
        body, name=name, in_specs=[any_spec] * n_arr, out_specs=[any_spec] * n_arr, out_shape=plan.out_shape,
        scratch_shapes=plan.scratch_shapes)(*plan.arrays)


def _gather_plan(arrays):
    n_arr = len(arrays)
    n_chips = len(_CHIP_RELATIONS)
    n_pair = 1 + 2 * n_chips

    def where():
        x, y, c = lax.axis_index("x"), lax.axis_index("y"), lax.axis_index("c")
        return x, y, c, (x, y, 1 - c), [(x ^ dx, y ^ dy) for dx, dy, _ in _CHIP_RELATIONS]

    def copy(outs, sems, a, k, block, to, src=None):
        slot = outs[a].at[2 * block[0] + block[1], block[2]]
        return pltpu.make_async_remote_copy(
            src_ref=slot if src is None else src, dst_ref=slot, send_sem=sems[0].at[a * n_pair + k],
            recv_sem=sems[1].at[a * n_pair + k], device_id=to, device_id_type=MESH)

    def mine(ins, outs, sems, a, x, y, c):
        return pltpu.make_async_copy(ins[a], outs[a].at[2 * x + y, c], sems[2].at[a])

    def first_copies(ins, outs, sems, a, x, y, c, sibling, chips):
        return ([copy(outs, sems, a, 0, (x, y, c), sibling, src=ins[a])]
                + [copy(outs, sems, a, 1 + j, (x, y, c), (*chip, c), src=ins[a]) for j, chip in enumerate(chips)])

    def start(ins, outs, sems):
        x, y, c, sibling, chips = where()
        for a in range(n_arr):
            mine(ins, outs, sems, a, x, y, c).start()
            for cp in first_copies(ins, outs, sems, a, x, y, c, sibling, chips):
                cp.start()

    def pass_on(ins, outs, sems):
        x, y, c, sibling, chips = where()
        for j, chip in enumerate(chips):
            for a in range(n_arr):
                copy(outs, sems, a, 1 + j, (*chip, c), (x, y, c)).wait_recv()
                copy(outs, sems, a, 1 + n_chips + j, (*chip, c), sibling).start()

    def finish(ins, outs, sems):
        x, y, c, sibling, chips = where()
        for a in range(n_arr):
            copy(outs, sems, a, 0, (x, y, 1 - c), (x, y, c)).wait_recv()
            for j, chip in enumerate(chips):
                copy(outs, sems, a, 1 + n_chips + j, (*chip, 1 - c), (x, y, c)).wait_recv()
        for a in range(n_arr):
            for cp in first_copies(ins, outs, sems, a, x, y, c, sibling, chips):
                cp.wait_send()
            for j, chip in enumerate(chips):
                copy(outs, sems, a, 1 + n_chips + j, (*chip, c), sibling).wait_send()
            mine(ins, outs, sems, a, x, y, c).wait()

    return _CommPlan(
        arrays, [jax.ShapeDtypeStruct((N_CHIPS, 2) + a.shape, a.dtype) for a in arrays],
        [pltpu.SemaphoreType.DMA((n_arr * n_pair,)), pltpu.SemaphoreType.DMA((n_arr * n_pair,)),
         pltpu.SemaphoreType.DMA((n_arr,))],
        (start, pass_on, finish))


def _sibling_exchange(arrays, scatter, *, name):
    n_arr = len(arrays)

    def body(*refs):
        ins, outs = refs[:n_arr], refs[n_arr:2 * n_arr]
        send_sems, recv_sems = refs[2 * n_arr:]
        x, y, c = lax.axis_index("x"), lax.axis_index("y"), lax.axis_index("c")
        copies = []
        for a in range(n_arr):
            for q in range(N_CHIPS if scatter[a] else 1):
                src = ins[a].at[2 * q + 1 - c] if scatter[a] else ins[a]
                dst = outs[a].at[q] if scatter[a] else outs[a]
                cp = pltpu.make_async_remote_copy(
                    src_ref=src, dst_ref=dst, send_sem=send_sems.at[a * N_CHIPS + q],
                    recv_sem=recv_sems.at[a * N_CHIPS + q], device_id=(x, y, 1 - c), device_id_type=MESH)
                cp.start()
                copies.append(cp)
        for cp in copies:
            cp.wait_send()
        for cp in copies:
            cp.wait_recv()

    any_spec = pl.BlockSpec(memory_space=pl.ANY)
    return pl.pallas_call(
        body, name=name, in_specs=[any_spec] * n_arr, out_specs=[any_spec] * n_arr,
        out_shape=[jax.ShapeDtypeStruct(((N_CHIPS,) + a.shape[1:]) if s else a.shape, a.dtype)
                   for a, s in zip(arrays, scatter)],
        scratch_shapes=[pltpu.SemaphoreType.DMA((n_arr * N_CHIPS,)), pltpu.SemaphoreType.DMA((n_arr * N_CHIPS,))],
    )(*arrays)


def _pair_sum(mine, sib, *, name, out_dtype):
    _, rows, cols = mine.shape
    tr = _row_tile(rows, 128)

    def body(c_ref, m_ref, s_ref, o_ref):
        del c_ref
        o_ref[0] = (m_ref[0, 0] + s_ref[0]).astype(out_dtype)

    grid_spec = pltpu.PrefetchScalarGridSpec(
        num_scalar_prefetch=1, grid=(N_CHIPS, rows // tr),
        in_specs=[pl.BlockSpec((1, 1, tr, cols), lambda q, i, c_ref: (q, c_ref[0], i, 0)),
                  pl.BlockSpec((1, tr, cols), lambda q, i, c_ref: (q, i, 0))],
        out_specs=pl.BlockSpec((1, tr, cols), lambda q, i, c_ref: (q, i, 0)))
    my_side = lax.axis_index("c").astype(jnp.int32).reshape(1)
    return pl.pallas_call(
        body, name=name, grid_spec=grid_spec,
        out_shape=jax.ShapeDtypeStruct((N_CHIPS, rows, cols), out_dtype),
        compiler_params=_cparams("parallel", "parallel"))(my_side, mine.reshape(N_CHIPS, 2, rows, cols), sib)


def _add(a, b, *, name):
    rows, cols = a.shape
    tr = _row_tile(rows, 256)

    def body(a_ref, b_ref, o_ref):
        o_ref[...] = a_ref[...] + b_ref[...]

    blk = pl.BlockSpec((tr, cols), lambda i: (i, 0))
    return pl.pallas_call(body, name=name, grid=(rows // tr,), in_specs=[blk, blk], out_specs=blk,
                          out_shape=jax.ShapeDtypeStruct(a.shape, a.dtype), compiler_params=_cparams("parallel"))(a, b)


def _chip_exchange(arrays, scatter, *, name):
    return _run_plan(_chip_exchange_plan(arrays, scatter), name)


def _chip_exchange_plan(arrays, scatter):
    n_arr = len(arrays)
    n_rel = len(_CHIP_RELATIONS)

    def local_copies(ins, outs, sems):
        me = 2 * lax.axis_index("x") + lax.axis_index("y")
        return [pltpu.make_async_copy(ins[a].at[me] if scatter[a] else ins[a], outs[a].at[me], sems[2].at[a])
                for a in range(n_arr)]

    def remote_copies(ins, outs, sems, arrivals):
        x, y, c = lax.axis_index("x"), lax.axis_index("y"), lax.axis_index("c")
        me = 2 * x + y
        out = []
        for k, (dx, dy, _) in enumerate(_CHIP_RELATIONS):
            px, py = x ^ dx, y ^ dy
            peer = 2 * px + py
            for a in range(n_arr):
                out.append(pltpu.make_async_remote_copy(
                    src_ref=ins[a].at[peer] if scatter[a] else ins[a], dst_ref=outs[a].at[peer if arrivals else me],
                    send_sem=sems[0].at[a * n_rel + k], recv_sem=sems[1].at[a * n_rel + k],
                    device_id=(x, y, c) if arrivals else (px, py, c), device_id_type=MESH))
        return out

    def start(ins, outs, sems):
        for cp in local_copies(ins, outs, sems) + remote_copies(ins, outs, sems, False):
            cp.start()

    def pass_on(ins, outs, sems):
        pass

    def finish(ins, outs, sems):
        for send in remote_copies(ins, outs, sems, False):
            send.wait_send()
        for arrival in remote_copies(ins, outs, sems, True):
            arrival.wait_recv()
        for cp in local_copies(ins, outs, sems):
            cp.wait()

    out_shape = [jax.ShapeDtypeStruct((N_CHIPS,) + (a.shape[1:] if s else a.shape), a.dtype)
                 for a, s in zip(arrays, scatter)]
    return _CommPlan(
        arrays, out_shape,
        [pltpu.SemaphoreType.DMA((n_arr * n_rel,)), pltpu.SemaphoreType.DMA((n_arr * n_rel,)),
         pltpu.SemaphoreType.DMA((n_arr,))],
        (start, pass_on, finish))


def _adamw(w, gslots, m, v, *, name):
    rows, cols = w.shape
    n_slots = gslots.shape[0]
    tr = _row_tile(rows, 128) if rows % 16 == 0 else rows

    def body(w_ref, g_ref, m_ref, v_ref, go_ref, d_ref, mo_ref, vo_ref):
        g = g_ref[0].astype(F32)
        for s in range(1, n_slots):
            g = g + g_ref[s].astype(F32)
        mn = ADAM_B1 * m_ref[...] + (1.0 - ADAM_B1) * g
        vn = ADAM_B2 * v_ref[...] + (1.0 - ADAM_B2) * (g * g)
        go_ref[...] = g
        mo_ref[...] = mn
        vo_ref[...] = vn
        m_hat = mn / (1.0 - ADAM_B1 ** ADAM_STEP)
        v_hat = vn / (1.0 - ADAM_B2 ** ADAM_STEP)
        d_ref[...] = -ADAM_LR * (m_hat / (jnp.sqrt(v_hat) + ADAM_EPS) + ADAM_WD * w_ref[...])

    blk = pl.BlockSpec((tr, cols), lambda i: (i, 0))
    shp = jax.ShapeDtypeStruct((rows, cols), F32)
    return pl.pallas_call(
        body, name=name, grid=(rows // tr,),
        in_specs=[blk, pl.BlockSpec((n_slots, tr, cols), lambda i: (0, i, 0)), blk, blk],
        out_specs=[blk] * 4, out_shape=[shp] * 4,
        compiler_params=_cparams("parallel"))(w, gslots, m, v)


_BIG = ("w_in", "w_ssd_branch", "w_attn_branch", "w_out", "w_ffn_in", "w_ffn_out")
_SMALL_SHARDED = ("meta_tokens", "ssd_conv_w", "gate_b", "ffn_conv_w")
_SMALL_REPLICATED = ("norm_mix_w", "ssd_conv_b", "ssd_dt_bias", "ssd_a_log", "ssd_d", "ssd_norm_w", "attn_sinks",
                     "rel_bias", "norm_ffn_w", "ffn_conv_b", "norm_final_w")
_WEIGHTS = ("meta_tokens", "norm_mix_w", "w_in", "ssd_conv_w", "ssd_conv_b", "ssd_dt_bias", "ssd_a_log", "ssd_d",
            "ssd_norm_w", "w_ssd_branch", "w_attn_branch", "attn_sinks", "rel_bias", "gate_b", "w_out", "norm_ffn_w",
            "w_ffn_in", "ffn_conv_w", "ffn_conv_b", "w_ffn_out", "norm_final_w")
_ROW_SHARDED = ("w_ssd_branch", "w_attn_branch", "w_out", "w_ffn_out")
_COL_SHARDED = ("w_in", "w_ffn_in", "meta_tokens", "ssd_conv_w", "gate_b", "ffn_conv_w")
_IN_SEGS = (("z", SSD_INNER), ("xbc", SSD_XBC), ("dt", SSD_HEADS), ("qkv", ATT_Q + 2 * ATT_KV), ("g", 2 * D_MODEL))


def _pack_rows(flat_parts, width, row_mult):
    flat = jnp.concatenate([p.reshape(-1) for p in flat_parts])
    pad = (-flat.shape[0]) % (width * row_mult)
    if pad:
        flat = jnp.concatenate([flat, jnp.zeros((pad,), flat.dtype)])
    return flat.reshape(-1, width)


def _unpack(flat, shapes):
    out, off = [], 0
    for shp in shapes:
        size = int(np.prod(shp))
        out.append(flat[off:off + size].reshape(shp))
        off += size
    return out


def _gather_full(stack, name, shard_shape):
    if name in _COL_SHARDED:
        return jnp.transpose(stack, (1, 0, 2)).reshape(shard_shape[0], N_DEV * shard_shape[1])
    return stack.reshape(N_DEV * shard_shape[0], shard_shape[1])


_IN_SEG_W = {"z": SSD_INNER, "xbc": SSD_XBC, "dt": DT_W, "qkv": ATT_Q + 2 * ATT_KV, "g": 2 * D_MODEL}
_IN_SHARD_W = (SSD_INNER + SSD_XBC + SSD_HEADS + ATT_Q + 2 * ATT_KV + 2 * D_MODEL) // N_DEV
_FFN_SHARD_W = 2 * D_FF // N_DEV


def _in_seg_runs():
    runs, off = [], 0
    for nm, width in _IN_SEGS:
        if nm == "dt":
            runs.append([(off + SSD_HPG * g, SSD_HPG, LANES * g) for g in range(SSD_GROUPS)])
        else:
            runs.append([(off, width, 0)])
        off += width
    return runs


def _w_in_to_segments(stack):
    pieces = [(seg, None, scol, 0, dev, col, w) for seg, scol, dev, col, w in _shard_pieces(_in_seg_runs(), _IN_SHARD_W)]
    outs = [((D_MODEL, _IN_SEG_W[nm]), stack.dtype) for nm, _ in _IN_SEGS]
    return dict(zip([nm for nm, _ in _IN_SEGS], _col_move([stack], outs, pieces, name="w_in_segments")))


def _segments_to_w_in_shards(seg_grads):
    pieces = [(0, dev, col, seg, None, scol, w) for seg, scol, dev, col, w in _shard_pieces(_in_seg_runs(), _IN_SHARD_W)]
    return _col_move(seg_grads, [((N_DEV, D_MODEL, _IN_SHARD_W), F32)], pieces, name="g_w_in_shards")[0]


def _ffn_in_from_shards(stack):
    pieces = [(0, None, scol, 0, dev, col, w)
              for _, scol, dev, col, w in _shard_pieces([[(0, 2 * D_FF, 0)]], _FFN_SHARD_W)]
    return _col_move([stack], [((D_MODEL, 2 * D_FF), stack.dtype)], pieces, name="w_ffn_in_full")[0]


def _ffn_in_to_shards(g_up, g_gate):
    pieces = [(0, dev, col, seg, None, scol, w)
              for seg, scol, dev, col, w in _shard_pieces([[(0, D_FF, 0)], [(D_FF, D_FF, 0)]], _FFN_SHARD_W)]
    return _col_move([g_up, g_gate], [((N_DEV, D_MODEL, _FFN_SHARD_W), F32)], pieces, name="g_w_ffn_in_shards")[0]


def _dt_spread(w_dt):
    k = w_dt.shape[0]
    w4 = w_dt.reshape(k, SSD_GROUPS, SSD_HPG)
    return jnp.pad(w4, ((0, 0), (0, 0), (0, LANES - SSD_HPG))).reshape(k, DT_W)


def _dt_gather(w_wide):
    k = w_wide.shape[0]
    return w_wide.reshape(k, SSD_GROUPS, LANES)[:, :, :SSD_HPG].reshape(k, SSD_HEADS)


class _LateExchanges:
    def __init__(self, two_d, shape2):
        self.two_d, self.shape2 = two_d, shape2
        self.early_grads_received = None
        self.w_in_grads_received = None

    def row_pack(self, tree):
        return jnp.concatenate([tree[k] for k in _ROW_SHARDED], axis=0)

    def late_weights_plan(self):
        return _gather_plan([self.two_d["w_ffn_in"].astype(BF16), self.row_pack(self.two_d).astype(BF16)])

    def late_weights(self, gathered):
        w_ffn_in_all, rows_all = [g.reshape((N_DEV,) + g.shape[2:]) for g in gathered]
        out = {"w_ffn_in": _ffn_in_from_shards(w_ffn_in_all)}
        off = 0
        for k in _ROW_SHARDED:
            r = self.shape2[k][0]
            out[k] = rows_all[:, off:off + r].reshape(N_DEV * r, D_MODEL)
            off += r
        return out

    def early_grads_plan(self, grads):
        rows_send = jnp.concatenate([grads[k].reshape(N_DEV, self.shape2[k][0], D_MODEL) for k in _ROW_SHARDED], axis=1)
        send = [_ffn_in_to_shards(*grads["w_ffn_in"]), rows_send]
        from_sib = _sibling_exchange(send, [True, True], name="early_grads_to_sibling")
        parts = [_pair_sum(mine, sib, name="pair_sum_" + nm, out_dtype=BF16)
                 for nm, mine, sib in zip(("w_ffn_in", "rows"), send, from_sib)]
        return _chip_exchange_plan(parts, [True, True])

    def w_in_grads_plan(self, seg_grads):
        send = _segments_to_w_in_shards(seg_grads)
        from_sib = _sibling_exchange([send], [True], name="w_in_grads_to_sibling")
        return _chip_exchange_plan([_pair_sum(send, from_sib[0], name="pair_sum_w_in", out_dtype=BF16)], [True])


def _local_step(x, target, w, exchanges=None):
    h0 = jnp.concatenate([jnp.zeros((PAD, D_MODEL), F32), w["meta_tokens"], x], axis=0)
    segs = w["in_segs"]

    dtb = _dt_spread(w["ssd_dt_bias"])
    alog = _dt_spread(w["ssd_a_log"])
    dskip_w = jnp.repeat(w["ssd_d"], SSD_HEADDIM, axis=1)
    sinks = jnp.pad(w["attn_sinks"], ((0, 0), (0, LANES - ATT_HEADS)))
    onehot_t = _onehot_t()
    tables = jnp.transpose(_bias_tables(w["rel_bias"].T, onehot_t).reshape(ATT_HEADS, 3, N_KEYS, BLOCK), (1, 0, 2, 3))

    u = _rms_fwd(h0, w["norm_mix_w"], name="rms_mix_fwd")
    z = _mm(u, segs["z"], name="in_z")
    xbc = _mm(u, segs["xbc"], name="in_xbc")
    dt_raw = _mm(u, segs["dt"], name="in_dt")
    qkv = _mm(u, segs["qkv"], name="in_qkv")
    gates = _mm(u, segs["g"], name="in_g")
    pre = _conv_fwd(xbc, w["ssd_conv_w"], w["ssd_conv_b"], name="ssd_conv_fwd")
    (y, yn, hsave), gathered = _ssd_fwd(pre, dt_raw, z, dtb, alog, dskip_w, w["ssd_norm_w"],
                                        side=None if exchanges is None else exchanges.late_weights_plan())
    if exchanges is not None:
        w = {**w, **exchanges.late_weights(gathered)}
    w_ffn_up, w_ffn_gate = w["w_ffn_in"][:, :D_FF], w["w_ffn_in"][:, D_FF:]
    y_ssd = _mm(yn, w["w_ssd_branch"], name="ssd_out")
    att = _attn_fwd(qkv, tables, sinks)
    y_att = _mm(att, w["w_attn_branch"], name="att_out")
    merged, h1 = _merge_out_fwd(gates, y_ssd, y_att, w["gate_b"], w["w_out"], h0)
    u2 = _rms_fwd(h1, w["norm_ffn_w"], name="rms_ffn_fwd")
    hid_raw = _mm(u2, w["w_ffn_in"], name="ffn_in")
    hid_up, hid_gate, act = _ffn_act_fwd(hid_raw, w["ffn_conv_w"], w["ffn_conv_b"])
    h2 = _mm(act, w["w_ffn_out"], c=h1, mask=True, name="ffn_out")
    dh2, loss_row, g_norm_final = _final_loss(h2, w["norm_final_w"], target)

    grads = {"norm_final_w": g_norm_final}
    dact = _mm(dh2, w["w_ffn_out"], tb=True, mask=True, name="d_act")
    grads["w_ffn_out"] = _mm(act, dh2, ta=True, mask=True, name="g_w_ffn_out")
    dx_up, dx_gate, dcw_up, dcw_gate, dcb_up, dcb_gate = _ffn_act_bwd(dact, hid_up, hid_gate, hid_raw, w["ffn_conv_w"])
    grads["ffn_conv_w"] = jnp.concatenate([dcw_up, dcw_gate], axis=1)
    grads["ffn_conv_b"] = jnp.concatenate([dcb_up, dcb_gate], axis=1)
    (dh1, grads["norm_ffn_w"]), _ = _mm_rms_bwd([(dx_up, w_ffn_up), (dx_gate, w_ffn_gate)], h1, w["norm_ffn_w"], dh2,
                                                name="d_u2_rms_bwd")
    grads["w_ffn_in"] = (_mm(u2, dx_up, ta=True, name="g_w_ffn_up"), _mm(u2, dx_gate, ta=True, name="g_w_ffn_gate"))

    grads["w_out"] = _mm(merged, dh1, ta=True, mask=True, name="g_w_out")
    dy_ssd, dy_att, dgates, grads["gate_b"] = _merge_out_bwd(dh1, w["w_out"], gates, y_ssd, y_att, w["gate_b"])
    dyn = _mm(dy_ssd, w["w_ssd_branch"], tb=True, name="d_yn")
    grads["w_ssd_branch"] = _mm(yn, dy_ssd, ta=True, name="g_w_ssd")
    datt = _mm(dy_att, w["w_attn_branch"], tb=True, name="d_att")
    grads["w_attn_branch"] = _mm(att, dy_att, ta=True, name="g_w_att")
    (dz, dpxs, dpb, dpc, ddt, grads["ssd_norm_w"], g_dtb, g_alog, g_dskip), received = _ssd_bwd(
        dyn, y, z, pre, dt_raw, hsave, dtb, alog, dskip_w, w["ssd_norm_w"],
        side=None if exchanges is None else exchanges.early_grads_plan(grads))
    if exchanges is not None:
        exchanges.early_grads_received = received
    grads["ssd_dt_bias"] = _dt_gather(g_dtb)
    grads["ssd_a_log"] = _dt_gather(g_alog)
    grads["ssd_d"] = _dt_gather(g_dskip)
    conv_g = _conv_bwd(dpxs, xbc, w["ssd_conv_w"], name="ssd_conv_bwd_x")
    conv_g = _conv_bwd(dpb, xbc, w["ssd_conv_w"], name="ssd_conv_bwd_b", col0=SSD_INNER, into=conv_g)
    dxbc, grads["ssd_conv_w"], grads["ssd_conv_b"] = _conv_bwd(
        dpc, xbc, w["ssd_conv_w"], name="ssd_conv_bwd_c", col0=SSD_INNER + SSD_BC, into=conv_g)
    dqkv, d_tables, d_sinks = _attn_bwd(datt, qkv, tables, sinks)
    grads["attn_sinks"] = d_sinks[:, :ATT_HEADS]
    dtab = jnp.transpose(d_tables, (1, 0, 2, 3)).reshape(ATT_HEADS, NT_ALL)
    grads["rel_bias"] = _bias_grad(dtab, onehot_t).T
    dsegs = {"z": dz, "xbc": dxbc, "dt": ddt, "qkv": dqkv, "g": dgates}
    grads["in_segs"] = [_mm(u, dsegs[nm], ta=True, name="g_w_in_" + nm) for nm, _ in _IN_SEGS]
    (dh0, grads["norm_mix_w"]), received = _mm_rms_bwd(
        [(dsegs[nm], segs[nm]) for nm, _ in _IN_SEGS], h0, w["norm_mix_w"], dh1, name="d_u_rms_bwd",
        side=None if exchanges is None else exchanges.w_in_grads_plan(grads["in_segs"]))
    if exchanges is not None:
        exchanges.w_in_grads_received = received[0]
    grads["meta_tokens"] = dh0[PAD:BLOCK]
    return loss_row[0, 0], dh0[BLOCK:], grads


def kernel(x, meta_tokens, norm_mix_w, w_in, ssd_conv_w, ssd_conv_b, ssd_dt_bias, ssd_a_log, ssd_d, ssd_norm_w, w_ssd_branch, w_attn_branch, attn_sinks, rel_bias, gate_b, w_out, norm_ffn_w, w_ffn_in, ffn_conv_w, ffn_conv_b, w_ffn_out, norm_final_w, loss_target, m_meta_tokens, m_norm_mix_w, m_w_in, m_ssd_conv_w, m_ssd_conv_b, m_ssd_dt_bias, m_ssd_a_log, m_ssd_d, m_ssd_norm_w, m_w_ssd_branch, m_w_attn_branch, m_attn_sinks, m_rel_bias, m_gate_b, m_w_out, m_norm_ffn_w, m_w_ffn_in, m_ffn_conv_w, m_ffn_conv_b, m_w_ffn_out, m_norm_final_w, v_meta_tokens, v_norm_mix_w, v_w_in, v_ssd_conv_w, v_ssd_conv_b, v_ssd_dt_bias, v_ssd_a_log, v_ssd_d, v_ssd_norm_w, v_w_ssd_branch, v_w_attn_branch, v_attn_sinks, v_rel_bias, v_gate_b, v_w_out, v_norm_ffn_w, v_w_ffn_in, v_ffn_conv_w, v_ffn_conv_b, v_w_ffn_out, v_norm_final_w):
    shard = dict(meta_tokens=meta_tokens, norm_mix_w=norm_mix_w, w_in=w_in, ssd_conv_w=ssd_conv_w,
                 ssd_conv_b=ssd_conv_b, ssd_dt_bias=ssd_dt_bias, ssd_a_log=ssd_a_log, ssd_d=ssd_d,
                 ssd_norm_w=ssd_norm_w, w_ssd_branch=w_ssd_branch, w_attn_branch=w_attn_branch,
                 attn_sinks=attn_sinks, rel_bias=rel_bias, gate_b=gate_b, w_out=w_out, norm_ffn_w=norm_ffn_w,
                 w_ffn_in=w_ffn_in, ffn_conv_w=ffn_conv_w, ffn_conv_b=ffn_conv_b, w_ffn_out=w_ffn_out,
                 norm_final_w=norm_final_w)
    mom_m = dict(zip(_WEIGHTS, (m_meta_tokens, m_norm_mix_w, m_w_in, m_ssd_conv_w, m_ssd_conv_b, m_ssd_dt_bias,
                                m_ssd_a_log, m_ssd_d, m_ssd_norm_w, m_w_ssd_branch, m_w_attn_branch, m_attn_sinks,
                                m_rel_bias, m_gate_b, m_w_out, m_norm_ffn_w, m_w_ffn_in, m_ffn_conv_w, m_ffn_conv_b,
                                m_w_ffn_out, m_norm_final_w)))
    mom_v = dict(zip(_WEIGHTS, (v_meta_tokens, v_norm_mix_w, v_w_in, v_ssd_conv_w, v_ssd_conv_b, v_ssd_dt_bias,
                                v_ssd_a_log, v_ssd_d, v_ssd_norm_w, v_w_ssd_branch, v_w_attn_branch, v_attn_sinks,
                                v_rel_bias, v_gate_b, v_w_out, v_norm_ffn_w, v_w_ffn_in, v_ffn_conv_w, v_ffn_conv_b,
                                v_w_ffn_out, v_norm_final_w)))
    orig_shape = {k: a.shape for k, a in shard.items()}
    two_d = {k: a.reshape(a.shape[-2:]) if a.ndim >= 2 else a.reshape(1, -1) for k, a in shard.items()}
    shape2 = {k: a.shape for k, a in two_d.items()}

    def as2d(tree):
        return {k: tree[k].reshape(shape2[k]) for k in _WEIGHTS}

    mom_m, mom_v = as2d(mom_m), as2d(mom_v)

    exchanges = _LateExchanges(two_d, shape2)
    row_pack = exchanges.row_pack
    small_pack = _pack_rows([two_d[k] for k in _SMALL_SHARDED], LANES, SMALL_ROW_MULT)
    w_in_all, small_all = _gather_two_level([two_d["w_in"].astype(BF16), small_pack], name="gather_weights")
    full = {k: two_d[k] for k in _SMALL_REPLICATED}
    full["in_segs"] = _w_in_to_segments(w_in_all)
    small_flat = small_all.reshape(N_DEV, -1)
    off = 0
    for k in _SMALL_SHARDED:
        size = int(np.prod(shape2[k]))
        full[k] = _gather_full(small_flat[:, off:off + size].reshape((N_DEV,) + shape2[k]), k, shape2[k])
        off += size

    loss_local, grad_x, grads = _local_step(x[0], loss_target[0], full, exchanges)

    small_names = _SMALL_SHARDED + _SMALL_REPLICATED
    small_send = _pack_rows([grads[k] for k in small_names] + [loss_local.reshape(1)], LANES, SMALL_ROW_MULT)
    from_sib = _sibling_exchange([small_send], [False], name="small_grads_to_sibling")
    small_recv, = _chip_exchange([_add(small_send, from_sib[0], name="pair_sum_small")], [False],
                                 name="exchange_small_grads")
    in_recv = exchanges.w_in_grads_received
    ffn_recv, rows_recv = exchanges.early_grads_received

    big = {"w_in": _adamw(two_d["w_in"], in_recv, mom_m["w_in"], mom_v["w_in"], name="adamw_w_in"),
           "w_ffn_in": _adamw(two_d["w_ffn_in"], ffn_recv, mom_m["w_ffn_in"], mom_v["w_ffn_in"], name="adamw_w_ffn_in")}
    rows_out = _adamw(row_pack(two_d), rows_recv, row_pack(mom_m), row_pack(mom_v), name="adamw_rows")
    off = 0
    for k in _ROW_SHARDED:
        r = shape2[k][0]
        big[k] = [a[off:off + r] for a in rows_out]
        off += r
    me =4 * lax.axis_index("x") + 2 * lax.axis_index("y") + lax.axis_index("c")
    small_full_shapes = [grads[k].shape for k in small_names]
    n_small = sum(int(np.prod(s)) for s in small_full_shapes)

    def packed_small(tree):
        parts = []
        for k in small_names:
            a = tree[k]
            if k in _SMALL_SHARDED:
                fullw = jnp.zeros(grads[k].shape, F32)
                a = lax.dynamic_update_slice(fullw, a, (0, me * a.shape[1]))
            parts.append(a)
        return _pack_rows(parts + [jnp.zeros((1,), F32)], LANES, SMALL_ROW_MULT)

    g_small, d_small, m_small, v_small = _adamw(packed_small(two_d), small_recv, packed_small(mom_m),
                                                packed_small(mom_v), name="adamw_small")

    def unpack_all(which, small):
        out = {k: big[k][which] for k in _BIG}
        flat = small.reshape(-1)
        for k, a in zip(small_names, _unpack(flat, small_full_shapes)):
            if k in _SMALL_SHARDED:
                a = lax.dynamic_slice(a, (0, me * shape2[k][1]), shape2[k])
            out[k] = a
        return out, flat[n_small]

    g_all, loss = unpack_all(0, g_small)
    d_all, _ = unpack_all(1, d_small)
    m_all, _ = unpack_all(2, m_small)
    v_all, _ = unpack_all(3, v_small)

    def final(tree):
        return [tree[k].reshape(orig_shape[k]) for k in _WEIGHTS]

    return (loss, grad_x[None], *final(g_all), *final(d_all), *final(m_all), *final(v_all))
```

```python
import functools
import math

import numpy as np
import jax
import jax.numpy as jnp
from jax import lax
from jax.experimental import pallas as pl
from jax.experimental.pallas import tpu as pltpu

F32 = jnp.float32
BF16 = jnp.bfloat16
HIGHEST = lax.Precision.HIGHEST

D_MODEL = 1024
N_META = 16
BLOCK = 128
PAD = BLOCK - N_META
EPS = 1e-6
NEG = -1e30
SSD_INNER = 2 * D_MODEL
SSD_HEADDIM = 64
SSD_HEADS = SSD_INNER // SSD_HEADDIM
SSD_GROUPS = 4
SSD_HPG = SSD_HEADS // SSD_GROUPS
SSD_STATE = 128
SSD_CONV = 4
SSD_GW = SSD_HPG * SSD_HEADDIM
SSD_BC = SSD_GROUPS * SSD_STATE
SSD_XBC = SSD_INNER + 2 * SSD_BC
ATT_HEADS = 16
ATT_KV_HEADS = 2
ATT_HEADDIM = 64
ATT_GQ = ATT_HEADS // ATT_KV_HEADS
ATT_Q = ATT_HEADS * ATT_HEADDIM
ATT_KV = ATT_KV_HEADS * ATT_HEADDIM
REL_BUCKETS = 32
REL_MAX_DIST = 128
D_FF = 2816
FFN_CONV = 3
ADAM_LR = 0.001
ADAM_B1 = 0.9
ADAM_B2 = 0.999
ADAM_EPS = 1e-08
ADAM_WD = 0.01
ADAM_STEP = 10

N_DEV = 8
LANES = 128
SUBLANES = 8
DT_W = SSD_GROUPS * LANES
VMEM_LIMIT_BYTES = 56 * 1024 * 1024
MESH = pl.DeviceIdType.MESH

SMALL_ROW_MULT = 16

N_KEYS = 3 * BLOCK
NT_ALL = 3 * N_KEYS * BLOCK
NT_TILE = 8192


def _cparams(*sem):
    return pltpu.CompilerParams(dimension_semantics=sem, vmem_limit_bytes=VMEM_LIMIT_BYTES)


def _row_tile(n, cap):
    best = None
    for t in range(16, min(n, cap) + 1, 16):
        if n % t == 0:
            best = t
    return best or n


def _col_tile(n, cap):
    for t in (1408, 1280, 1024, 768, 640, 512, 384, 256, 128):
        if t <= cap and n % t == 0:
            return t
    return n


def _sigmoid(x):
    return 0.5 * jnp.tanh(0.5 * x) + 0.5


def _silu(x):
    return x * _sigmoid(x)


def _softplus(x):
    return jnp.maximum(x, 0.0) + jnp.log(1.0 + jnp.exp(-jnp.abs(x)))


def _dot_nt(a, b):
    return lax.dot_general(a, b, (((1,), (1,)), ((), ())), preferred_element_type=F32)


def _dot_tn(a, b):
    return lax.dot_general(a, b, (((0,), (0,)), ((), ())), preferred_element_type=F32)


def _dot(a, b):
    return jnp.dot(a, b, preferred_element_type=F32)


def _bf16_terms(x, terms):
    out, rest = [], x
    for _ in range(terms):
        part = rest.astype(BF16)
        out.append(part)
        rest = rest - part.astype(F32)
    return out


def _dot_sel(x, sel, terms=3):
    return sum(_dot(part, sel) for part in _bf16_terms(x, terms))


def _sel_dot(sel, x, terms=3):
    return sum(_dot(sel, part) for part in _bf16_terms(x, terms))


def _sum_all(x):
    return jnp.sum(jnp.sum(x, axis=1, keepdims=True), axis=0, keepdims=True)


MM_ROW_CAPS = (1664, 832, 416)
MM_COL_CAP = 1408
MM_VMEM_BUDGET = 44 * 1024 * 1024


def _mm_tiles(rows, cols, vmem_bytes):
    col_cands = [t for t in (1408, 1280, 1024, 768, 640, 512, 384, 256, 128) if cols % t == 0]
    if cols <= 2 * MM_COL_CAP:
        col_cands.append(cols)
    best = None
    for cap in MM_ROW_CAPS:
        tr = _row_tile(rows, cap)
        for tc in col_cands:
            if vmem_bytes(tr, tc) <= MM_VMEM_BUDGET and (best is None or tr * tc > best[0] * best[1]):
                best = (tr, tc)
    assert best is not None, (rows, cols)
    return best


def _mm(a, b, *, name, ta=False, tb=False, c=None, mask=False, out_dtype=F32):
    if not ta:
        m, k = a.shape
        n = b.shape[0] if tb else b.shape[1]
        tm, tn = _mm_tiles(m, n, lambda t_m, t_n: 2 * (t_m * k * a.dtype.itemsize + k * t_n * b.dtype.itemsize
                                                       + t_m * t_n * (jnp.dtype(out_dtype).itemsize
                                                                      + (0 if c is None else c.dtype.itemsize)))
                           + 4 * t_m * t_n)

        def body(*refs):
            if c is None:
                a_ref, b_ref, o_ref = refs
            else:
                a_ref, b_ref, c_ref, o_ref = refs
            acc = (_dot_nt if tb else _dot)(a_ref[...].astype(BF16), b_ref[...].astype(BF16))
            if mask:
                row = pl.program_id(0) * tm + lax.broadcasted_iota(jnp.int32, (tm, 1), 0)
                acc = jnp.where(row >= PAD, acc, 0.0)
            if c is not None:
                acc = acc + c_ref[...]
            o_ref[...] = acc.astype(out_dtype)

        b_spec = pl.BlockSpec((tn, k), lambda i, j: (j, 0)) if tb else pl.BlockSpec((k, tn), lambda i, j: (0, j))
        in_specs = [pl.BlockSpec((tm, k), lambda i, j: (i, 0)), b_spec]
        args = [a, b]
        if c is not None:
            in_specs.append(pl.BlockSpec((tm, tn), lambda i, j: (i, j)))
            args.append(c)
        return pl.pallas_call(
            body, name=name, grid=(m // tm, n // tn), in_specs=in_specs,
            out_specs=pl.BlockSpec((tm, tn), lambda i, j: (i, j)),
            out_shape=jax.ShapeDtypeStruct((m, n), out_dtype),
            compiler_params=_cparams("parallel", "parallel"))(*args)

    kc, m = a.shape
    n = b.shape[1]
    tm = _col_tile(m, MM_COL_CAP)
    tk, tn = _mm_tiles(kc, n, lambda t_k, t_n: 2 * (t_k * tm * a.dtype.itemsize + t_k * t_n * b.dtype.itemsize
                                                    + 4 * tm * t_n) + 4 * tm * t_n)

    n_k = kc // tk

    def body_t(a_ref, b_ref, o_ref, acc_ref):
        kk = pl.program_id(2)
        bb = b_ref[...]
        if mask:
            row = kk * tk + lax.broadcasted_iota(jnp.int32, (tk, 1), 0)
            bb = jnp.where(row >= PAD, bb, jnp.zeros_like(bb))
        p = _dot_tn(a_ref[...].astype(BF16), bb.astype(BF16))

        @pl.when(kk == 0)
        def _():
            acc_ref[...] = p

        @pl.when(kk > 0)
        def _():
            acc_ref[...] += p

        @pl.when(kk == n_k - 1)
        def _():
            o_ref[...] = acc_ref[...].astype(out_dtype)

    return pl.pallas_call(
        body_t, name=name, grid=(m // tm, n // tn, n_k),
        in_specs=[pl.BlockSpec((tk, tm), lambda i, j, kk: (kk, i)), pl.BlockSpec((tk, tn), lambda i, j, kk: (kk, j))],
        out_specs=pl.BlockSpec((tm, tn), lambda i, j, kk: (i, j)),
        out_shape=jax.ShapeDtypeStruct((m, n), out_dtype),
        scratch_shapes=[pltpu.VMEM((tm, tn), F32)],
        compiler_params=_cparams("parallel", "parallel", "arbitrary"))(a, b)


def _mm_rms_bwd(pairs, x, w, dres, *, name, side=None):
    m, d = x.shape
    tm = _row_tile(m, 416)
    n_pairs = len(pairs)

    def body(*refs):
        a_refs, b_refs = refs[:n_pairs], refs[n_pairs:2 * n_pairs]
        x_ref, w_ref, dres_ref, dx_ref, dw_ref = refs[2 * n_pairs:]
        i = pl.program_id(0)
        dyv = None
        for a_ref, b_ref in zip(a_refs, b_refs):
            term = _dot_nt(a_ref[...].astype(BF16), b_ref[...])
            dyv = term if dyv is None else dyv + term
        xv = x_ref[...]
        r = lax.rsqrt(jnp.mean(xv * xv, axis=-1, keepdims=True) + EPS)
        xh = xv * r
        g = dyv * w_ref[...]
        dx_ref[...] = r * (g - xh * jnp.mean(g * xh, axis=-1, keepdims=True)) + dres_ref[...]
        part = jnp.sum(dyv * xh, axis=0, keepdims=True)

        @pl.when(i == 0)
        def _():
            dw_ref[...] = part

        @pl.when(i > 0)
        def _():
            dw_ref[...] += part

    row = pl.BlockSpec((tm, d), lambda i: (i, 0))
    vec = pl.BlockSpec((1, d), lambda i: (0, 0))
    in_specs = ([pl.BlockSpec((tm, a.shape[1]), lambda i: (i, 0)) for a, _ in pairs]
                + [pl.BlockSpec(b.shape, lambda i: (0, 0), pipeline_mode=pl.Buffered(1)) for _, b in pairs]
                + [row, vec, row])
    return _call_with_side(
        body, side, name=name, grid=(m // tm,), in_specs=in_specs, out_specs=[row, vec],
        out_shape=[jax.ShapeDtypeStruct((m, d), F32), jax.ShapeDtypeStruct((1, d), F32)], scratch_shapes=[],
        args=[a for a, _ in pairs] + [b for _, b in pairs] + [x, w, dres], semantics=("arbitrary",))


def _rms_fwd(h, w, *, name):
    n, d = h.shape
    tm = _row_tile(n, 832)

    def body(h_ref, w_ref, o_ref):
        x = h_ref[...]
        r = lax.rsqrt(jnp.mean(x * x, axis=-1, keepdims=True) + EPS)
        o_ref[...] = (x * r * w_ref[...]).astype(BF16)

    return pl.pallas_call(
        body, name=name, grid=(n // tm,),
        in_specs=[pl.BlockSpec((tm, d), lambda i: (i, 0)), pl.BlockSpec((1, d), lambda i: (0, 0))],
        out_specs=pl.BlockSpec((tm, d), lambda i: (i, 0)),
        out_shape=jax.ShapeDtypeStruct((n, d), BF16),
        compiler_params=_cparams("parallel"))(h, w)


def _final_loss(h, w, target):
    n, d = h.shape
    nb = n // BLOCK

    def body(h_ref, w_ref, t_ref, dh_ref, loss_ref, dw_ref):
        i = pl.program_id(0)
        xv = h_ref[...]
        r = lax.rsqrt(jnp.mean(xv * xv, axis=-1, keepdims=True) + EPS)
        xh = xv * r
        wv = w_ref[...]
        err = jnp.where(i >= 1, xh * wv - t_ref[...], 0.0)
        dyv = err * (1.0 / d)
        g = dyv * wv
        dh_ref[...] = r * (g - xh * jnp.mean(g * xh, axis=-1, keepdims=True))
        lpart = jnp.broadcast_to(0.5 * _sum_all(err * err) * (1.0 / d), (1, LANES))
        wpart = jnp.sum(dyv * xh, axis=0, keepdims=True)

        @pl.when(i == 0)
        def _():
            loss_ref[...] = lpart
            dw_ref[...] = wpart

        @pl.when(i > 0)
        def _():
            loss_ref[...] += lpart
            dw_ref[...] += wpart

    row = pl.BlockSpec((BLOCK, d), lambda i: (i, 0))
    vec = pl.BlockSpec((1, d), lambda i: (0, 0))
    return pl.pallas_call(
        body, name="final_loss", grid=(nb,),
        in_specs=[row, vec, pl.BlockSpec((BLOCK, d), lambda i: (jnp.maximum(i - 1, 0), 0))],
        out_specs=[row, pl.BlockSpec((1, LANES), lambda i: (0, 0)), vec],
        out_shape=[jax.ShapeDtypeStruct((n, d), F32), jax.ShapeDtypeStruct((1, LANES), F32),
                   jax.ShapeDtypeStruct((1, d), F32)],
        compiler_params=_cparams("arbitrary"))(h, w, target)


def _main_spec(tm, cb, off=0):
    return pl.BlockSpec((tm, cb), lambda j, i: (i, j + off))


def _prev_spec(tm, cb, off=0):
    r8 = tm // SUBLANES
    return pl.BlockSpec((SUBLANES, cb), lambda j, i: (jnp.maximum(i * r8 - 1, 0), j + off))


def _next_spec(tm, cb, n_rows, off=0):
    r8 = tm // SUBLANES
    last = n_rows // SUBLANES - 1
    return pl.BlockSpec((SUBLANES, cb), lambda j, i: (jnp.minimum((i + 1) * r8, last), j + off))


def _with_prev(prev_ref, main_ref, i):
    prev = jnp.where(i > 0, prev_ref[...], 0.0)
    return jnp.concatenate([prev, main_ref[...]], axis=0)


def _with_next(main, nxt, i, n_tiles):
    return jnp.concatenate([main, jnp.where(i < n_tiles - 1, nxt, 0.0)], axis=0)


def _back(xx, s, tm):
    if s == 0:
        return xx[SUBLANES:SUBLANES + tm]
    return pltpu.roll(xx, s, 0)[SUBLANES:SUBLANES + tm]


def _ahead(xx, s, tm):
    if s == 0:
        return xx[:tm]
    return pltpu.roll(xx, tm + SUBLANES - s, 0)[:tm]


def _conv_fwd(x, w, b, *, name):
    n, cdim = x.shape
    kw = w.shape[0]
    tm = _row_tile(n, 832)
    cb = _col_tile(cdim, 512)

    def body(xp_ref, x_ref, w_ref, b_ref, o_ref):
        xx = _with_prev(xp_ref, x_ref, pl.program_id(1))
        acc = jnp.broadcast_to(b_ref[...], (tm, cb))
        for k in range(kw):
            acc = acc + w_ref[k:k + 1, :] * _back(xx, kw - 1 - k, tm)
        o_ref[...] = acc

    return pl.pallas_call(
        body, name=name, grid=(cdim // cb, n // tm),
        in_specs=[_prev_spec(tm, cb), _main_spec(tm, cb), pl.BlockSpec((kw, cb), lambda j, i: (0, j)),
                  pl.BlockSpec((1, cb), lambda j, i: (0, j))],
        out_specs=_main_spec(tm, cb),
        out_shape=jax.ShapeDtypeStruct((n, cdim), F32),
        compiler_params=_cparams("parallel", "parallel"))(x, x, w, b)


def _conv_bwd_core(dpre_ext, x, w_ref, kw, tm):
    dx = None
    dws = []
    for k in range(kw):
        shifted = _ahead(dpre_ext, kw - 1 - k, tm)
        term = w_ref[k:k + 1, :] * shifted
        dx = term if dx is None else dx + term
        dws.append(jnp.sum(shifted * x, axis=0, keepdims=True))
    return dx, dws, jnp.sum(dpre_ext[:tm], axis=0, keepdims=True)


def _acc_rows(i, dw_ref, db_ref, dws, db):
    @pl.when(i == 0)
    def _():
        for k, v in enumerate(dws):
            dw_ref[k:k + 1, :] = v
        db_ref[...] = db

    @pl.when(i > 0)
    def _():
        for k, v in enumerate(dws):
            dw_ref[k:k + 1, :] += v
        db_ref[...] += db


def _conv_bwd(dpre, x, w, *, name, col0=0, into=None):
    n, cdim = x.shape
    kw = w.shape[0]
    tm = _row_tile(n, 832)
    cb = _col_tile(cdim, 512)
    nt = n // tm
    off = col0 // cb
    n_alias = 0 if into is None else 3

    def body(d_ref, dn_ref, x_ref, w_ref, *rest):
        dx_ref, dw_ref, db_ref = rest[n_alias:]
        i = pl.program_id(1)
        dpre_ext = _with_next(d_ref[...], dn_ref[...], i, nt)
        dx, dws, db = _conv_bwd_core(dpre_ext, x_ref[...], w_ref, kw, tm)
        dx_ref[...] = dx.astype(BF16)
        _acc_rows(i, dw_ref, db_ref, dws, db)

    wspec = pl.BlockSpec((kw, cb), lambda j, i: (0, j + off))
    bspec = pl.BlockSpec((1, cb), lambda j, i: (0, j + off))
    return pl.pallas_call(
        body, name=name, grid=(dpre.shape[1] // cb, nt),
        in_specs=[_main_spec(tm, cb), _next_spec(tm, cb, n), _main_spec(tm, cb, off), wspec]
        + [pl.BlockSpec(memory_space=pl.ANY)] * n_alias,
        out_specs=[_main_spec(tm, cb, off), wspec, bspec],
        out_shape=[jax.ShapeDtypeStruct((n, cdim), BF16), jax.ShapeDtypeStruct((kw, cdim), F32),
                   jax.ShapeDtypeStruct((1, cdim), F32)],
        input_output_aliases={4 + k: k for k in range(n_alias)},
        compiler_params=_cparams("parallel", "arbitrary"))(dpre, dpre, x, w, *(into or ()))


def _ffn_act_fwd(x, w, b):
    n = x.shape[0]
    kw = w.shape[0]
    tm = _row_tile(n, 832)
    cb = _col_tile(D_FF, 256)
    nc = D_FF // cb

    def body(xpu_ref, xu_ref, xpg_ref, xg_ref, wu_ref, wg_ref, bu_ref, bg_ref, hu_ref, hg_ref, act_ref):
        i = pl.program_id(1)
        outs = []
        for xp_ref, x_ref, w_ref, b_ref in ((xpu_ref, xu_ref, wu_ref, bu_ref), (xpg_ref, xg_ref, wg_ref, bg_ref)):
            xx = _with_prev(xp_ref, x_ref, i)
            acc = jnp.broadcast_to(b_ref[...], (tm, cb))
            for k in range(kw):
                acc = acc + w_ref[k:k + 1, :] * _back(xx, kw - 1 - k, tm)
            outs.append(acc)
        hu_ref[...] = outs[0]
        hg_ref[...] = outs[1]
        act_ref[...] = (_silu(outs[1]) * outs[0]).astype(BF16)

    def wspec(off):
        return pl.BlockSpec((kw, cb), lambda j, i: (0, j + off))

    def bspec(off):
        return pl.BlockSpec((1, cb), lambda j, i: (0, j + off))

    out = _main_spec(tm, cb)
    return pl.pallas_call(
        body, name="ffn_act_fwd", grid=(nc, n // tm),
        in_specs=[_prev_spec(tm, cb), _main_spec(tm, cb), _prev_spec(tm, cb, nc), _main_spec(tm, cb, nc),
                  wspec(0), wspec(nc), bspec(0), bspec(nc)],
        out_specs=[out, out, out],
        out_shape=[jax.ShapeDtypeStruct((n, D_FF), F32), jax.ShapeDtypeStruct((n, D_FF), F32),
                   jax.ShapeDtypeStruct((n, D_FF), BF16)],
        compiler_params=_cparams("parallel", "parallel"))(x, x, x, x, w, w, b, b)


def _ffn_act_bwd(dact, hu, hg, x, w):
    n = x.shape[0]
    kw = w.shape[0]
    tm = _row_tile(n, 832)
    cb = _col_tile(D_FF, 256)
    nc = D_FF // cb
    nt = n // tm

    def body(d_ref, dn_ref, hu_ref, hun_ref, hg_ref, hgn_ref, xu_ref, xg_ref, wu_ref, wg_ref,
             dxu_ref, dxg_ref, dwu_ref, dwg_ref, dbu_ref, dbg_ref):
        i = pl.program_id(1)
        dact_e = _with_next(d_ref[...], dn_ref[...], i, nt)
        up_e = _with_next(hu_ref[...], hun_ref[...], i, nt)
        gate_e = _with_next(hg_ref[...], hgn_ref[...], i, nt)
        sg = _sigmoid(gate_e)
        dup_e = dact_e * (gate_e * sg)
        dgate_e = dact_e * up_e * (sg * (1.0 + gate_e * (1.0 - sg)))
        dx, dws, db = _conv_bwd_core(dup_e, xu_ref[...], wu_ref, kw, tm)
        dxu_ref[...] = dx.astype(BF16)
        _acc_rows(i, dwu_ref, dbu_ref, dws, db)
        dx, dws, db = _conv_bwd_core(dgate_e, xg_ref[...], wg_ref, kw, tm)
        dxg_ref[...] = dx.astype(BF16)
        _acc_rows(i, dwg_ref, dbg_ref, dws, db)

    main, nxt = _main_spec(tm, cb), _next_spec(tm, cb, n)
    wspec0 = pl.BlockSpec((kw, cb), lambda j, i: (0, j))
    wspec1 = pl.BlockSpec((kw, cb), lambda j, i: (0, j + nc))
    bspec = pl.BlockSpec((1, cb), lambda j, i: (0, j))
    return pl.pallas_call(
        body, name="ffn_act_bwd", grid=(nc, nt),
        in_specs=[main, nxt, main, nxt, main, nxt, _main_spec(tm, cb), _main_spec(tm, cb, nc), wspec0, wspec1],
        out_specs=[main, main, wspec0, wspec0, bspec, bspec],
        out_shape=[jax.ShapeDtypeStruct((n, D_FF), BF16), jax.ShapeDtypeStruct((n, D_FF), BF16),
                   jax.ShapeDtypeStruct((kw, D_FF), F32), jax.ShapeDtypeStruct((kw, D_FF), F32),
                   jax.ShapeDtypeStruct((1, D_FF), F32), jax.ShapeDtypeStruct((1, D_FF), F32)],
        compiler_params=_cparams("parallel", "arbitrary"))(dact, dact, hu, hu, hg, hg, x, x, w, w)


def _ssd_prep(pxs_ref, pb_ref, pc_ref, dtr_ref, dtb_ref, alog_ref, c):
    xs = _silu(pxs_ref[...])
    bm = _silu(pb_ref[...])
    cm = _silu(pc_ref[...])
    return (xs, bm, cm) + _ssd_decay(dtr_ref, dtb_ref, alog_ref, c)


def _ssd_decay(dtr_ref, dtb_ref, alog_ref, c):
    row =lax.broadcasted_iota(jnp.int32, (BLOCK, 1), 0) + c * BLOCK
    valid = (row >= PAD).astype(F32)
    dtr = dtr_ref[...] + dtb_ref[...]
    dt = _softplus(dtr) * valid
    a = -jnp.exp(alog_ref[...])
    lam = dt * a
    ri = lax.broadcasted_iota(jnp.int32, (BLOCK, BLOCK), 0)
    ci = lax.broadcasted_iota(jnp.int32, (BLOCK, BLOCK), 1)
    causal = ci <= ri
    cs = _sel_dot(causal.astype(BF16), lam)
    return valid, dtr, dt, a, lam, cs, causal


def _head_cols(r):
    return slice(SSD_HEADDIM * r, SSD_HEADDIM * (r + 1))


def _ssd_specs(nc, rev):
    def cidx(c):
        return nc - 1 - c if rev else c

    xs = pl.BlockSpec((BLOCK, SSD_GW), lambda g, c: (cidx(c), g))
    bspec = pl.BlockSpec((BLOCK, SSD_STATE), lambda g, c: (cidx(c), SSD_INNER // SSD_STATE + g))
    cspec = pl.BlockSpec((BLOCK, SSD_STATE), lambda g, c: (cidx(c), (SSD_INNER + SSD_BC) // SSD_STATE + g))
    lane = pl.BlockSpec((BLOCK, LANES), lambda g, c: (cidx(c), g))
    vec = pl.BlockSpec((1, LANES), lambda g, c: (0, g))
    wide_vec = pl.BlockSpec((1, SSD_GW), lambda g, c: (0, g))
    hsave = pl.BlockSpec((1, 1, SSD_GW, SSD_STATE), lambda g, c: (cidx(c), g, 0, 0))
    return xs, bspec, cspec, lane, vec, wide_vec, hsave


def _head_spread_matrix():
    r = lax.broadcasted_iota(jnp.int32, (LANES, SSD_GW), 0)
    col = lax.broadcasted_iota(jnp.int32, (LANES, SSD_GW), 1)
    return (col // SSD_HEADDIM == r).astype(BF16)


def _const_spec(shape):
    return pl.BlockSpec(shape, lambda g, c: (0,) * len(shape))


def _spread_heads(per_head, e_ref):
    wide = _dot_sel(jnp.concatenate(per_head, axis=0), e_ref[...])
    return [wide[BLOCK * k:BLOCK * (k + 1)] for k in range(len(per_head))]


def _call_with_side(body, side, *, name, grid, in_specs, out_specs, out_shape, scratch_shapes, args,
                    semantics=("parallel", "arbitrary")):
    if side is None:
        outs = pl.pallas_call(body, name=name, grid=grid, in_specs=in_specs, out_specs=out_specs, out_shape=out_shape,
                              scratch_shapes=scratch_shapes, compiler_params=_cparams(*semantics))(*args)
        return outs, []
    n_in, n_out, n_scr, n_side = len(in_specs), len(out_specs), len(scratch_shapes), len(side.arrays)

    def body_with_side(*refs):
        ins, rest = refs[:n_in + n_side], refs[n_in + n_side:]
        outs, scratch = rest[:n_out + n_side], rest[n_out + n_side:]
        side_refs = (ins[n_in:], outs[n_out:], scratch[n_scr:])
        ids = [pl.program_id(k) for k in range(len(grid))]
        inner_first = functools.reduce(jnp.logical_and, [i == 0 for i in ids[1:]], True)

        @pl.when((ids[0] == 0) & inner_first)
        def _():
            side.phases[0](*side_refs)

        body(*ins[:n_in], *outs[:n_out], *scratch[:n_scr])

        @pl.when((ids[0] == grid[0] // 2) & inner_first)
        def _():
            side.phases[1](*side_refs)

        @pl.when(functools.reduce(jnp.logical_and, [i == n - 1 for i, n in zip(ids, grid)]))
        def _():
            side.phases[2](*side_refs)

    any_spec = pl.BlockSpec(memory_space=pl.ANY)
    outs = pl.pallas_call(
        body_with_side, name=name, grid=grid, in_specs=list(in_specs) + [any_spec] * n_side,
        out_specs=list(out_specs) + [any_spec] * n_side, out_shape=list(out_shape) + list(side.out_shape),
        scratch_shapes=list(scratch_shapes) + list(side.scratch_shapes),
        compiler_params=_cparams(*["arbitrary"] * len(grid)))(*args, *side.arrays)
    return outs[:n_out], outs[n_out:]


def _ssd_fwd(pre, dt_raw, z, dtb, alog, dskip_w, norm_w, side=None):
    n = pre.shape[0]
    nc = n // BLOCK
    xs_s, b_s, c_s, lane_s, vec_s, wide_s, hs_s = _ssd_specs(nc, False)

    def body(pxs_ref, pb_ref, pc_ref, dtr_ref, z_ref, dtb_ref, alog_ref, dskw_ref, nw_ref, e_ref,
             y_ref, yn_ref, hs_ref, h_scr):
        c = pl.program_id(1)

        @pl.when(c == 0)
        def _():
            h_scr[...] = jnp.zeros_like(h_scr)

        xs, bm, cm, _, _, dt, _, _, cs, causal = _ssd_prep(pxs_ref, pb_ref, pc_ref, dtr_ref, dtb_ref, alog_ref, c)
        cst = cs.T
        cs_last = cs[BLOCK - 1:BLOCK, :]
        dt_w, ecs_w, dec_w = _spread_heads([dt, jnp.exp(cs), jnp.exp(cs_last - cs)], e_ref)
        xdt = xs * dt_w
        bmb = bm.astype(BF16)
        cmb = cm.astype(BF16)
        cb = _dot_nt(cmb, bmb)
        hg = h_scr[...]
        hs_ref[0, 0] = hg
        y = _dot_nt(cmb, hg.astype(BF16)) * ecs_w + dskw_ref[...] * xs
        first = lax.broadcasted_iota(jnp.int32, (BLOCK, LANES), 1) < SSD_HEADDIM
        diag = []
        for j in range(SSD_HPG // 2):
            xp = xdt[:, LANES * j:LANES * (j + 1)].astype(BF16)
            res = []
            for r in (2 * j, 2 * j + 1):
                lm = jnp.exp(jnp.where(causal, cs[:, r:r + 1] - cst[r:r + 1, :], NEG))
                res.append(_dot((cb * lm).astype(BF16), xp))
            diag.append(jnp.where(first, res[0], res[1]))
        y = y + jnp.concatenate(diag, axis=1)
        st = _dot_tn((xdt * dec_w).astype(BF16), bmb)
        eh = jnp.exp(cs_last)
        for r in range(SSD_HPG):
            rows = _head_cols(r)
            h_scr[rows, :] = hg[rows, :] * eh[:, r:r + 1] + st[rows, :]
        y_ref[...] = y
        gts = y * _silu(z_ref[...])
        rr = lax.rsqrt(jnp.mean(gts * gts, axis=-1, keepdims=True) + EPS)
        yn_ref[...] = (gts * rr * nw_ref[...]).astype(BF16)

    return _call_with_side(
        body, side, name="ssd_fwd", grid=(SSD_GROUPS, nc),
        in_specs=[xs_s, b_s, c_s, lane_s, xs_s, vec_s, vec_s, wide_s, wide_s, _const_spec((LANES, SSD_GW))],
        out_specs=[xs_s, xs_s, hs_s],
        out_shape=[jax.ShapeDtypeStruct((n, SSD_INNER), F32), jax.ShapeDtypeStruct((n, SSD_INNER), BF16),
                   jax.ShapeDtypeStruct((nc, SSD_GROUPS, SSD_GW, SSD_STATE), F32)],
        scratch_shapes=[pltpu.VMEM((SSD_GW, SSD_STATE), F32)],
        args=(pre, pre, pre, dt_raw, z, dtb, alog, dskip_w, norm_w, _head_spread_matrix()))


def _lane_put(acc, col, r):
    lane = lax.broadcasted_iota(jnp.int32, acc.shape, 1)
    return jnp.where(lane == r, col, acc)


def _ssd_bwd(dyn, y, z, pre, dt_raw, hsave, dtb, alog, dskip_w, norm_w, side=None):
    n = pre.shape[0]
    nc = n // BLOCK
    spread = _head_spread_matrix()
    xs_s, b_s, c_s, lane_s, vec_s, wide_s, hs_s = _ssd_specs(nc, True)
    bc_out =pl.BlockSpec((BLOCK, SSD_STATE), lambda g, c: (nc - 1 - c, g))

    def body(dyn_ref, y_ref, z_ref, pxs_ref, pb_ref, pc_ref, dtr_ref, hs_ref, dtb_ref, alog_ref, dskw_ref, nw_ref,
             e_ref, r_ref,
             dz_ref, dxs_ref, dbm_ref, dcm_ref, ddt_ref, dnw_ref, ddtb_ref, dalog_ref, ddsk_ref, g_scr):
        step = pl.program_id(1)
        c = nc - 1 - step

        @pl.when(step == 0)
        def _():
            g_scr[...] = jnp.zeros_like(g_scr)

        pxs, pb, pc = pxs_ref[...], pb_ref[...], pc_ref[...]
        sx, sb, sc = _sigmoid(pxs), _sigmoid(pb), _sigmoid(pc)
        xs, bm, cm = pxs * sx, pb * sb, pc * sc
        valid, dtr, dt, a, lam, cs, causal = _ssd_decay(dtr_ref, dtb_ref, alog_ref, c)
        cst = cs.T
        cs_last = cs[BLOCK - 1:BLOCK, :]
        bmb = bm.astype(BF16)
        cmb = cm.astype(BF16)
        cb = _dot_nt(cmb, bmb)
        hg = hs_ref[0, 0]
        hgb = hg.astype(BF16)
        yoff = _dot_nt(cmb, hgb)
        gn = g_scr[...]
        gnb = gn.astype(BF16)

        zv = z_ref[...]
        yv = y_ref[...]
        sgz = _sigmoid(zv)
        sz = zv * sgz
        gts = yv * sz
        rr = lax.rsqrt(jnp.mean(gts * gts, axis=-1, keepdims=True) + EPS)
        xh = gts * rr
        dynv = dyn_ref[...]
        gg = dynv * nw_ref[...]
        dgts = rr * (gg - xh * jnp.mean(gg * xh, axis=-1, keepdims=True))
        dnw = jnp.sum(dynv * xh, axis=0, keepdims=True)
        dy = dgts * sz
        dz_ref[...] = (dgts * yv * (sgz * (1.0 + zv * (1.0 - sgz)))).astype(BF16)

        ecs = jnp.exp(cs)
        dec = jnp.exp(cs_last - cs)
        eh = jnp.exp(cs_last)
        dt_w, ecs_w, dec_w = _spread_heads([dt, ecs, dec], e_ref)
        red_m = r_ref[...]

        def head_sums(v):
            return _dot_sel(v, red_m, terms=2)

        xdt = xs * dt_w
        q_all = _dot_nt(bmb, gnb)
        w_all = (dy * ecs_w).astype(BF16)
        e_hl = head_sums(q_all * xdt) * dec
        dcs_col = head_sums(dy * yoff) * ecs - e_hl
        gh = jnp.zeros((1, LANES), F32)
        prod = gn * hg
        for r in range(SSD_HPG):
            gh = _lane_put(gh, _sum_all(prod[_head_cols(r), :]), r)
        dcs_last = jnp.sum(e_hl, axis=0, keepdims=True) + eh * gh
        ddsk = jnp.sum(head_sums(dy * xs), axis=0, keepdims=True)
        cbt = _dot_nt(bmb, cmb)
        lane = lax.broadcasted_iota(jnp.int32, (BLOCK, LANES), 1)
        first = lane < SSD_HEADDIM
        causal_t = lax.broadcasted_iota(jnp.int32, (BLOCK, BLOCK), 1) >= lax.broadcasted_iota(
            jnp.int32, (BLOCK, BLOCK), 0)
        sub = lax.broadcasted_iota(jnp.int32, (SUBLANES, BLOCK), 0)
        dcs_row = jnp.zeros((SUBLANES, BLOCK), F32)
        dcb = jnp.zeros((BLOCK, BLOCK), F32)
        dxdt_pairs = []
        for j in range(SSD_HPG // 2):
            tile = slice(LANES * j, LANES * (j + 1))
            dy_p = dy[:, tile]
            dyb = dy_p.astype(BF16)
            xdtb = xdt[:, tile].astype(BF16)
            res = []
            for half, r in enumerate((2 * j, 2 * j + 1)):
                csc, csr = cs[:, r:r + 1], cst[r:r + 1, :]
                lm = jnp.exp(jnp.where(causal, csc - csr, NEG))
                lmt = jnp.exp(jnp.where(causal_t, csr - csc, NEG))
                keep = first if half == 0 else jnp.logical_not(first)
                gm = _dot_nt(jnp.where(keep, dy_p, 0.0).astype(BF16), xdtb) * lm
                dcb = dcb + gm
                mm_ = gm * cb
                dcs_col = dcs_col + jnp.where(lane == r, jnp.sum(mm_, axis=1, keepdims=True), 0.0)
                dcs_row = jnp.where(sub == r, jnp.sum(mm_, axis=0, keepdims=True), dcs_row)
                res.append(_dot((cbt * lmt).astype(BF16), dyb))
            dxdt_pairs.append(jnp.where(first, res[0], res[1]))
        dxdt = jnp.concatenate(dxdt_pairs, axis=1) + q_all * dec_w
        ddt_x = head_sums(dxdt * xs)
        dxs = dxdt * dt_w + dskw_ref[...] * dy
        dcbb = dcb.astype(BF16)
        dcm = _dot(w_all, hgb) + _dot(dcbb, bmb)
        dbm = _dot((xdt * dec_w).astype(BF16), gnb) + _dot_tn(dcbb, cmb)
        dh_off = _dot_tn(w_all, cmb)
        for r in range(SSD_HPG):
            rows = _head_cols(r)
            g_scr[rows, :] = gn[rows, :] * eh[:, r:r + 1] + dh_off[rows, :]

        pad_rows = jnp.zeros((BLOCK - SUBLANES, BLOCK), F32)
        dcs = dcs_col - jnp.concatenate([dcs_row, pad_rows], axis=0).T
        rsel = lax.broadcasted_iota(jnp.int32, (BLOCK, LANES), 0)
        dcs = dcs + jnp.where(rsel == BLOCK - 1, dcs_last, 0.0)
        ri = lax.broadcasted_iota(jnp.int32, (BLOCK, BLOCK), 0)
        ci = lax.broadcasted_iota(jnp.int32, (BLOCK, BLOCK), 1)
        dlam = _sel_dot((ci >= ri).astype(BF16), dcs)
        head = lane < SSD_HPG
        ddt = dlam * a + ddt_x
        ddtr = jnp.where(head, ddt * _sigmoid(dtr) * valid, 0.0)
        ddt_ref[...] = ddtr.astype(BF16)
        dalog = jnp.sum(jnp.where(head, dlam * lam, 0.0), axis=0, keepdims=True)
        ddtb = jnp.sum(ddtr, axis=0, keepdims=True)

        dxs_ref[...] = dxs * (sx * (1.0 + pxs * (1.0 - sx)))
        dbm_ref[...] = dbm * (sb * (1.0 + pb * (1.0 - sb)))
        dcm_ref[...] = dcm * (sc * (1.0 + pc * (1.0 - sc)))

        @pl.when(step == 0)
        def _():
            dnw_ref[...] = dnw
            ddtb_ref[...] = ddtb
            dalog_ref[...] = dalog
            ddsk_ref[...] = ddsk

        @pl.when(step > 0)
        def _():
            dnw_ref[...] += dnw
            ddtb_ref[...] += ddtb
            dalog_ref[...] += dalog
            ddsk_ref[...] += ddsk

    return _call_with_side(
        body, side, name="ssd_bwd", grid=(SSD_GROUPS, nc),
        in_specs=[xs_s, xs_s, xs_s, xs_s, b_s, c_s, lane_s, hs_s, vec_s, vec_s, wide_s, wide_s,
                  _const_spec((LANES, SSD_GW)), _const_spec((SSD_GW, LANES))],
        out_specs=[xs_s, xs_s, bc_out, bc_out, lane_s, wide_s, vec_s, vec_s, vec_s],
        out_shape=[jax.ShapeDtypeStruct((n, SSD_INNER), BF16), jax.ShapeDtypeStruct((n, SSD_INNER), F32),
                   jax.ShapeDtypeStruct((n, SSD_BC), F32), jax.ShapeDtypeStruct((n, SSD_BC), F32),
                   jax.ShapeDtypeStruct((n, DT_W), BF16), jax.ShapeDtypeStruct((1, SSD_INNER), F32),
                   jax.ShapeDtypeStruct((1, DT_W), F32), jax.ShapeDtypeStruct((1, DT_W), F32),
                   jax.ShapeDtypeStruct((1, DT_W), F32)],
        scratch_shapes=[pltpu.VMEM((SSD_GW, SSD_STATE), F32)],
        args=(dyn, y, z, pre, pre, pre, dt_raw, hsave, dtb, alog, dskip_w, norm_w, spread, spread.T))


def _bucket_table():
    def bucket(dist):
        d = np.maximum(dist, 0)
        half = REL_BUCKETS // 2
        big = half + (np.log(np.maximum(d, half).astype(np.float32) / np.float32(half))
                      / np.float32(math.log(REL_MAX_DIST / half)) * np.float32(REL_BUCKETS - half)).astype(np.int32)
        return np.where(d < half, d, np.minimum(big, REL_BUCKETS - 1)).astype(np.int32)

    l = np.arange(BLOCK)[None, :]
    band = bucket(l + BLOCK - np.arange(2 * BLOCK)[:, None])
    j = np.arange(BLOCK)[:, None]
    tables = [np.concatenate([bucket(v * BLOCK + l - j), band], axis=0) for v in range(3)]
    return np.concatenate([t.reshape(-1) for t in tables])


def _onehot_t():
    buckets = jnp.asarray(_bucket_table())
    return (buckets[None, :] == jnp.arange(REL_BUCKETS, dtype=jnp.int32)[:, None]).astype(F32)


def _bias_tables(rel_t, onehot_t):
    def body(r_ref, oh_ref, o_ref):
        o_ref[...] = jnp.dot(r_ref[...], oh_ref[...], precision=HIGHEST, preferred_element_type=F32)

    return pl.pallas_call(
        body, name="bias_tables", grid=(NT_ALL // NT_TILE,),
        in_specs=[pl.BlockSpec((ATT_HEADS, REL_BUCKETS), lambda i: (0, 0)),
                  pl.BlockSpec((REL_BUCKETS, NT_TILE), lambda i: (0, i))],
        out_specs=pl.BlockSpec((ATT_HEADS, NT_TILE), lambda i: (0, i)),
        out_shape=jax.ShapeDtypeStruct((ATT_HEADS, NT_ALL), F32),
        compiler_params=_cparams("parallel"))(rel_t, onehot_t)


def _bias_grad(dtab, onehot_t):
    def body(d_ref, oh_ref, o_ref):
        i = pl.program_id(0)
        p = lax.dot_general(d_ref[...], oh_ref[...], (((1,), (1,)), ((), ())), precision=HIGHEST,
                            preferred_element_type=F32)

        @pl.when(i == 0)
        def _():
            o_ref[...] = p

        @pl.when(i > 0)
        def _():
            o_ref[...] += p

    return pl.pallas_call(
        body, name="bias_grad", grid=(NT_ALL // NT_TILE,),
        in_specs=[pl.BlockSpec((ATT_HEADS, NT_TILE), lambda i: (0, i)),
                  pl.BlockSpec((REL_BUCKETS, NT_TILE), lambda i: (0, i))],
        out_specs=pl.BlockSpec((ATT_HEADS, REL_BUCKETS), lambda i: (0, 0)),
        out_shape=jax.ShapeDtypeStruct((ATT_HEADS, REL_BUCKETS), F32),
        compiler_params=_cparams("arbitrary"))(dtab, onehot_t)


def _att_mask_t(n, copies):
    far = 4 * BLOCK
    kk = lax.broadcasted_iota(jnp.int32, (N_KEYS, copies * BLOCK), 0)
    li = lax.broadcasted_iota(jnp.int32, (N_KEYS, copies * BLOCK), 1) & (BLOCK - 1)
    meta_ok = (kk >= PAD) & (kk < BLOCK) & (li + jnp.where(n >= 1, far, 0) >= kk)
    prev_ok = (kk >= BLOCK) & (kk < 2 * BLOCK) & (kk - BLOCK > li + jnp.where(n >= 2, 0, far))
    cur_ok = (kk >= 2 * BLOCK) & (kk - 2 * BLOCK <= li - jnp.where(n >= 1, 0, far))
    return meta_ok | prev_ok | cur_ok


def _att_kv(meta_ref, prev_ref, cur_ref):
    kv = jnp.concatenate([meta_ref[...], prev_ref[...], cur_ref[...]], axis=0)
    first = lax.broadcasted_iota(jnp.int32, (N_KEYS, LANES), 1) < ATT_HEADDIM
    out = []
    for pair in (kv[:, :LANES], kv[:, LANES:]):
        swapped = pltpu.roll(pair, ATT_HEADDIM, 1)
        out.append([jnp.where(first, pair, swapped).astype(BF16), jnp.where(first, swapped, pair).astype(BF16)])
    return out[0], out[1]


def _split_heads(x_pair, first):
    return jnp.concatenate([jnp.where(first, x_pair, 0.0), jnp.where(first, 0.0, x_pair)], axis=0).astype(BF16)


def _att_probs_t(qm2, k_dup, t_ref, j, mask2, sink_ref):
    scale = ATT_HEADDIM ** -0.5
    bias2 = jnp.concatenate([t_ref[0, 2 * j], t_ref[0, 2 * j + 1]], axis=1)
    second = lax.broadcasted_iota(jnp.int32, (1, 2 * BLOCK), 1) >= BLOCK
    sink2 = jnp.where(second, sink_ref[0:1, 2 * j + 1:2 * j + 2], sink_ref[0:1, 2 * j:2 * j + 1])
    s_t = jnp.where(mask2, _dot_nt(k_dup, qm2) * scale + bias2, NEG)
    mx = jnp.maximum(jnp.max(s_t, axis=0, keepdims=True), sink2)
    p_t = jnp.exp(s_t - mx)
    p_s = jnp.exp(sink2 - mx)
    inv = 1.0 / (jnp.sum(p_t, axis=0, keepdims=True) + p_s)
    return p_t * inv, p_s * inv


def _att_specs(nb, rev):
    def nidx(i):
        return nb - 1 - i if rev else i

    kvb = ATT_Q // (2 * ATT_KV)
    q_s = pl.BlockSpec((BLOCK, ATT_Q), lambda i: (nidx(i), 0))
    cur = pl.BlockSpec((BLOCK, 2 * ATT_KV), lambda i: (nidx(i), kvb))
    prev = pl.BlockSpec((BLOCK, 2 * ATT_KV), lambda i: (jnp.maximum(nidx(i) - 1, 0), kvb))
    meta = pl.BlockSpec((BLOCK, 2 * ATT_KV), lambda i: (0, kvb))
    table = pl.BlockSpec((1, ATT_HEADS, N_KEYS, BLOCK), lambda i: (jnp.minimum(nidx(i), 2), 0, 0, 0))
    sink = pl.BlockSpec((1, LANES), lambda i: (0, 0))
    return q_s, cur, prev, meta, table, sink


def _attn_fwd(qkv, tables, sinks):
    n = qkv.shape[0]
    nb = n // BLOCK
    q_s, cur_s, prev_s, meta_s, t_s, sink_s = _att_specs(nb, False)

    def body(q_ref, cur_ref, prev_ref, meta_ref, t_ref, sink_ref, o_ref):
        blk = pl.program_id(0)
        mask_t = _att_mask_t(blk, 1)
        k_dup, v_dup = _att_kv(meta_ref, prev_ref, cur_ref)
        v_dup_t = [v.T for v in v_dup]
        first = lax.broadcasted_iota(jnp.int32, (BLOCK, LANES), 1) < ATT_HEADDIM
        top = lax.broadcasted_iota(jnp.int32, (LANES, BLOCK), 0) < ATT_HEADDIM
        scale = ATT_HEADDIM ** -0.5
        for j in range(ATT_HEADS // 2):
            kh = 2 * j // ATT_GQ
            tile = slice(LANES * j, LANES * (j + 1))
            q_p = q_ref[:, tile]
            res = []
            for half, h in enumerate((2 * j, 2 * j + 1)):
                qm = jnp.where(first if half == 0 else jnp.logical_not(first), q_p, 0.0).astype(BF16)
                sink = sink_ref[0:1, h:h + 1]
                s_t = jnp.where(mask_t, _dot_nt(k_dup[kh], qm) * scale + t_ref[0, h], NEG)
                mx = jnp.maximum(jnp.max(s_t, axis=0, keepdims=True), sink)
                p_t = jnp.exp(s_t - mx)
                inv = 1.0 / (jnp.sum(p_t, axis=0, keepdims=True) + jnp.exp(sink - mx))
                res.append(_dot(v_dup_t[kh], (p_t * inv).astype(BF16)))
            o_ref[:, tile] = jnp.where(top, res[0], res[1]).T.astype(BF16)

    return pl.pallas_call(
        body, name="attn_fwd", grid=(nb,),
        in_specs=[q_s, cur_s, prev_s, meta_s, t_s, sink_s],
        out_specs=q_s,
        out_shape=jax.ShapeDtypeStruct((n, ATT_Q), BF16),
        compiler_params=_cparams("parallel"))(qkv, qkv, qkv, qkv, tables, sinks)


def _attn_bwd(datt, qkv, tables, sinks):
    n = qkv.shape[0]
    nb = n // BLOCK
    q_s, cur_s, prev_s, meta_s, t_s, sink_s = _att_specs(nb, True)
    dqkv_s = pl.BlockSpec((BLOCK, ATT_Q + 2 * ATT_KV), lambda i: (nb - 1 - i, 0))
    scale = ATT_HEADDIM ** -0.5

    def body(do_ref, q_ref, cur_ref, prev_ref, meta_ref, t_ref, sink_ref,
             dqkv_ref, dt_ref, dsink_ref, carry_scr, meta_scr):
        step = pl.program_id(0)
        blk = nb - 1 - step
        mask2 = _att_mask_t(blk, 2)
        k_dup, v_dup = _att_kv(meta_ref, prev_ref, cur_ref)
        k_dup_t = [k.T for k in k_dup]

        @pl.when(step == 0)
        def _():
            carry_scr[...] = jnp.zeros_like(carry_scr)
            meta_scr[...] = jnp.zeros_like(meta_scr)
            dsink_ref[...] = jnp.zeros_like(dsink_ref)

        @pl.when((step == 0) | (blk <= 1))
        def _():
            dt_ref[...] = jnp.zeros_like(dt_ref)

        first = lax.broadcasted_iota(jnp.int32, (BLOCK, LANES), 1) < ATT_HEADDIM
        top = lax.broadcasted_iota(jnp.int32, (LANES, BLOCK), 0) < ATT_HEADDIM
        first_k = lax.broadcasted_iota(jnp.int32, (N_KEYS, LANES), 1) < ATT_HEADDIM
        dsink = jnp.zeros((1, LANES), F32)
        dk_acc = [None] * ATT_KV_HEADS
        dv_acc = [None] * ATT_KV_HEADS
        for j in range(ATT_HEADS // 2):
            kh = 2 * j // ATT_GQ
            tile = slice(LANES * j, LANES * (j + 1))
            qm2 = _split_heads(q_ref[:, tile], first)
            dom2 = _split_heads(do_ref[:, tile], first)
            p_t, p_s = _att_probs_t(qm2, k_dup[kh], t_ref, j, mask2, sink_ref)
            dp_t = _dot_nt(v_dup[kh], dom2)
            delta = jnp.sum(p_t * dp_t, axis=0, keepdims=True)
            ds_t = p_t * (dp_t - delta)
            sink_terms = p_s * delta
            for half in range(2):
                cols = slice(BLOCK * half, BLOCK * (half + 1))
                dsink = _lane_put(dsink, -jnp.sum(sink_terms[:, cols], axis=1, keepdims=True), 2 * j + half)
                dt_ref[0, 2 * j + half] += ds_t[:, cols]
            ds_tb = ds_t.astype(BF16)
            dq_t = _dot(k_dup_t[kh], ds_tb)
            dqkv_ref[:, tile] = (jnp.where(top, dq_t[:, :BLOCK], dq_t[:, BLOCK:]).T * scale).astype(BF16)
            dk_part, dv_part = _dot(ds_tb, qm2), _dot(p_t.astype(BF16), dom2)
            dk_acc[kh] = dk_part if dk_acc[kh] is None else dk_acc[kh] + dk_part
            dv_acc[kh] = dv_part if dv_acc[kh] is None else dv_acc[kh] + dv_part
        dsink_ref[...] += dsink
        folded = [a + pltpu.roll(a, ATT_HEADDIM, 1) for a in dk_acc + dv_acc]
        dkv = jnp.concatenate([jnp.where(first_k, folded[0], folded[1]) * scale,
                               jnp.where(first_k, folded[2], folded[3])], axis=1)
        meta_scr[...] += dkv[:BLOCK, :]
        own = dkv[2 * BLOCK:, :] + carry_scr[...]
        carry_scr[...] = dkv[BLOCK:2 * BLOCK, :]

        @pl.when(blk > 0)
        def _():
            dqkv_ref[:, ATT_Q:] = own.astype(BF16)

        @pl.when(blk == 0)
        def _():
            dqkv_ref[:, ATT_Q:] = (own + meta_scr[...]).astype(BF16)

    return pl.pallas_call(
        body, name="attn_bwd", grid=(nb,),
        in_specs=[q_s, q_s, cur_s, prev_s, meta_s, t_s, sink_s],
        out_specs=[dqkv_s, t_s, sink_s],
        out_shape=[jax.ShapeDtypeStruct((n, ATT_Q + 2 * ATT_KV), BF16),
                   jax.ShapeDtypeStruct((3, ATT_HEADS, N_KEYS, BLOCK), F32),
                   jax.ShapeDtypeStruct((1, LANES), F32)],
        scratch_shapes=[pltpu.VMEM((BLOCK, 2 * ATT_KV), F32), pltpu.VMEM((BLOCK, 2 * ATT_KV), F32)],
        compiler_params=_cparams("arbitrary"))(datt, qkv, qkv, qkv, qkv, tables, sinks)


def _merge_out_fwd(gates, y_ssd, y_att, gate_b, w_out, h):
    n = gates.shape[0]
    tm = _row_tile(n, 416)

    def body(gs_ref, ga_ref, ys_ref, ya_ref, gb_ref, w_ref, h_ref, m_ref, o_ref):
        merged = (_sigmoid(gs_ref[...] + gb_ref[0:1, :]) * ys_ref[...]
                  + _sigmoid(ga_ref[...] + gb_ref[1:2, :]) * ya_ref[...]).astype(BF16)
        m_ref[...] = merged
        row = pl.program_id(0) * tm + lax.broadcasted_iota(jnp.int32, (tm, 1), 0)
        o_ref[...] = jnp.where(row >= PAD, _dot(merged, w_ref[...]), 0.0) + h_ref[...]

    row = pl.BlockSpec((tm, D_MODEL), lambda i: (i, 0))
    return pl.pallas_call(
        body, name="merge_out_fwd", grid=(n // tm,),
        in_specs=[row, pl.BlockSpec((tm, D_MODEL), lambda i: (i, 1)), row, row,
                  pl.BlockSpec((2, D_MODEL), lambda i: (0, 0)), pl.BlockSpec((D_MODEL, D_MODEL), lambda i: (0, 0)), row],
        out_specs=[row, row],
        out_shape=[jax.ShapeDtypeStruct((n, D_MODEL), BF16), jax.ShapeDtypeStruct((n, D_MODEL), F32)],
        compiler_params=_cparams("parallel"))(gates, gates, y_ssd, y_att, gate_b, w_out, h)


def _merge_out_bwd(dh, w_out, gates, y_ssd, y_att, gate_b):
    n = gates.shape[0]
    tm = _row_tile(n, 416)

    def body(dh_ref, w_ref, gs_ref, ga_ref, ys_ref, ya_ref, gb_ref, dys_ref, dya_ref, dg_ref, dgb_ref):
        i = pl.program_id(0)
        row = i * tm + lax.broadcasted_iota(jnp.int32, (tm, 1), 0)
        dmv = jnp.where(row >= PAD, _dot_nt(dh_ref[...].astype(BF16), w_ref[...]), 0.0)
        ss =_sigmoid(gs_ref[...] + gb_ref[0:1, :])
        sa = _sigmoid(ga_ref[...] + gb_ref[1:2, :])
        dys_ref[...] = (dmv * ss).astype(BF16)
        dya_ref[...] = (dmv * sa).astype(BF16)
        dgs = dmv * ys_ref[...] * ss * (1.0 - ss)
        dga = dmv * ya_ref[...] * sa * (1.0 - sa)
        dg_ref[:, :D_MODEL] = dgs.astype(BF16)
        dg_ref[:, D_MODEL:] = dga.astype(BF16)
        part = jnp.concatenate([jnp.sum(dgs, axis=0, keepdims=True), jnp.sum(dga, axis=0, keepdims=True)], axis=0)

        @pl.when(i == 0)
        def _():
            dgb_ref[...] = part

        @pl.when(i > 0)
        def _():
            dgb_ref[...] += part

    row = pl.BlockSpec((tm, D_MODEL), lambda i: (i, 0))
    gb = pl.BlockSpec((2, D_MODEL), lambda i: (0, 0))
    return pl.pallas_call(
        body, name="merge_out_bwd", grid=(n // tm,),
        in_specs=[row, pl.BlockSpec((D_MODEL, D_MODEL), lambda i: (0, 0)), row,
                  pl.BlockSpec((tm, D_MODEL), lambda i: (i, 1)), row, row, gb],
        out_specs=[row, row, pl.BlockSpec((tm, 2 * D_MODEL), lambda i: (i, 0)), gb],
        out_shape=[jax.ShapeDtypeStruct((n, D_MODEL), BF16), jax.ShapeDtypeStruct((n, D_MODEL), BF16),
                   jax.ShapeDtypeStruct((n, 2 * D_MODEL), BF16), jax.ShapeDtypeStruct((2, D_MODEL), F32)],
        compiler_params=_cparams("arbitrary"))(dh, w_out, gates, gates, y_ssd, y_att, gate_b)


def _col_move(srcs, outs, pieces, *, name):
    rows = srcs[0].shape[-2]
    tr = _row_tile(rows, 128)
    n_src = len(srcs)
    covered = [sum(p[6] for p in pieces if p[0] == o) for o in range(len(outs))]
    total = [int(np.prod(shp)) // rows for shp, _ in outs]

    def body(*refs):
        in_refs, out_refs = refs[:n_src], refs[n_src:]
        for o, ref in enumerate(out_refs):
            if covered[o] != total[o]:
                ref[...] = jnp.zeros_like(ref)
        for o, ol, oc, s, sl, sc, width in pieces:
            val = in_refs[s][:, sc:sc + width] if sl is None else in_refs[s][sl, :, sc:sc + width]
            val = val.astype(outs[o][1])
            if ol is None:
                out_refs[o][:, oc:oc + width] = val
            else:
                out_refs[o][ol, :, oc:oc + width] = val

    def spec(shape):
        if len(shape) == 2:
            return pl.BlockSpec((tr, shape[1]), lambda i: (i, 0))
        return pl.BlockSpec((shape[0], tr, shape[2]), lambda i: (0, i, 0))

    return pl.pallas_call(
        body, name=name, grid=(rows // tr,),
        in_specs=[spec(a.shape) for a in srcs], out_specs=[spec(shp) for shp, _ in outs],
        out_shape=[jax.ShapeDtypeStruct(shp, dt) for shp, dt in outs],
        compiler_params=_cparams("parallel"))(*srcs)


def _shard_pieces(seg_ranges, shard_w):
    out = []
    for seg, runs in enumerate(seg_ranges):
        for g0, width, s0 in runs:
            done = 0
            while done < width:
                dev, col = divmod(g0 + done, shard_w)
                take = min(width - done, shard_w - col)
                out.append((seg, s0 + done, dev, col, take))
                done += take
    return out


_CHIP_RELATIONS = [(1, 0, 0), (0, 1, 0), (1, 1, 0)]
N_CHIPS = 4


def _gather_two_level(arrays, *, name):
    outs = _run_plan(_gather_plan(arrays), name)
    return [o.reshape((N_DEV,) + a.shape) for o, a in zip(outs, arrays)]


class _CommPlan:
    def __init__(self, arrays, out_shape, scratch_shapes, phases):
        self.arrays, self.out_shape, self.scratch_shapes, self.phases = arrays, out_shape, scratch_shapes, phases


def _run_plan(plan, name):
    n_arr = len(plan.arrays)

    def body(*refs):
        ins, outs, sems = refs[:n_arr], refs[n_arr:2 * n_arr], refs[2 * n_arr:]
        for phase in plan.phases:
            phase(ins, outs, sems)

    any_spec = pl.BlockSpec(memory_space=pl.ANY)
    return pl.pallas_call(
        body, name=name, in_specs=[any_spec] * n_arr, out_specs=[any_spec] * n_arr, out_shape=plan.out_shape,
        scratch_shapes=plan.scratch_shapes)(*plan.arrays)


def _gather_plan(arrays):
    n_arr = len(arrays)
    n_chips = len(_CHIP_RELATIONS)
    n_pair = 1 + 2 * n_chips

    def where():
        x, y, c = lax.axis_index("x"), lax.axis_index("y"), lax.axis_index("c")
        return x, y, c, (x, y, 1 - c), [(x ^ dx, y ^ dy) for dx, dy, _ in _CHIP_RELATIONS]

    def copy(outs, sems, a, k, block, to, src=None):
        slot = outs[a].at[2 * block[0] + block[1], block[2]]
        return pltpu.make_async_remote_copy(
            src_ref=slot if src is None else src, dst_ref=slot, send_sem=sems[0].at[a * n_pair + k],
            recv_sem=sems[1].at[a * n_pair + k], device_id=to, device_id_type=MESH)

    def mine(ins, outs, sems, a, x, y, c):
        return pltpu.make_async_copy(ins[a], outs[a].at[2 * x + y, c], sems[2].at[a])

    def first_copies(ins, outs, sems, a, x, y, c, sibling, chips):
        return ([copy(outs, sems, a, 0, (x, y, c), sibling, src=ins[a])]
                + [copy(outs, sems, a, 1 + j, (x, y, c), (*chip, c), src=ins[a]) for j, chip in enumerate(chips)])

    def start(ins, outs, sems):
        x, y, c, sibling, chips = where()
        for a in range(n_arr):
            mine(ins, outs, sems, a, x, y, c).start()
            for cp in first_copies(ins, outs, sems, a, x, y, c, sibling, chips):
                cp.start()

    def pass_on(ins, outs, sems):
        x, y, c, sibling, chips = where()
        for j, chip in enumerate(chips):
            for a in range(n_arr):
                copy(outs, sems, a, 1 + j, (*chip, c), (x, y, c)).wait_recv()
                copy(outs, sems, a, 1 + n_chips + j, (*chip, c), sibling).start()

    def finish(ins, outs, sems):
        x, y, c, sibling, chips = where()
        for a in range(n_arr):
            copy(outs, sems, a, 0, (x, y, 1 - c), (x, y, c)).wait_recv()
            for j, chip in enumerate(chips):
                copy(outs, sems, a, 1 + n_chips + j, (*chip, 1 - c), (x, y, c)).wait_recv()
        for a in range(n_arr):
            for cp in first_copies(ins, outs, sems, a, x, y, c, sibling, chips):
                cp.wait_send()
            for j, chip in enumerate(chips):
                copy(outs, sems, a, 1 + n_chips + j, (*chip, c), sibling).wait_send()
            mine(ins, outs, sems, a, x, y, c).wait()

    return _CommPlan(
        arrays, [jax.ShapeDtypeStruct((N_CHIPS, 2) + a.shape, a.dtype) for a in arrays],
        [pltpu.SemaphoreType.DMA((n_arr * n_pair,)), pltpu.SemaphoreType.DMA((n_arr * n_pair,)),
         pltpu.SemaphoreType.DMA((n_arr,))],
        (start, pass_on, finish))


def _sibling_exchange(arrays, scatter, *, name):
    n_arr = len(arrays)

    def body(*refs):
        ins, outs = refs[:n_arr], refs[n_arr:2 * n_arr]
        send_sems, recv_sems = refs[2 * n_arr:]
        x, y, c = lax.axis_index("x"), lax.axis_index("y"), lax.axis_index("c")
        copies = []
        for a in range(n_arr):
            for q in range(N_CHIPS if scatter[a] else 1):
                src = ins[a].at[2 * q + 1 - c] if scatter[a] else ins[a]
                dst = outs[a].at[q] if scatter[a] else outs[a]
                cp = pltpu.make_async_remote_copy(
                    src_ref=src, dst_ref=dst, send_sem=send_sems.at[a * N_CHIPS + q],
                    recv_sem=recv_sems.at[a * N_CHIPS + q], device_id=(x, y, 1 - c), device_id_type=MESH)
                cp.start()
                copies.append(cp)
        for cp in copies:
            cp.wait_send()
        for cp in copies:
            cp.wait_recv()

    any_spec = pl.BlockSpec(memory_space=pl.ANY)
    return pl.pallas_call(
        body, name=name, in_specs=[any_spec] * n_arr, out_specs=[any_spec] * n_arr,
        out_shape=[jax.ShapeDtypeStruct(((N_CHIPS,) + a.shape[1:]) if s else a.shape, a.dtype)
                   for a, s in zip(arrays, scatter)],
        scratch_shapes=[pltpu.SemaphoreType.DMA((n_arr * N_CHIPS,)), pltpu.SemaphoreType.DMA((n_arr * N_CHIPS,))],
    )(*arrays)


def _add(a, b, *, name):
    rows, cols = a.shape
    tr = _row_tile(rows, 256)

    def body(a_ref, b_ref, o_ref):
        o_ref[...] = a_ref[...] + b_ref[...]

    blk = pl.BlockSpec((tr, cols), lambda i: (i, 0))
    return pl.pallas_call(body, name=name, grid=(rows // tr,), in_specs=[blk, blk], out_specs=blk,
                          out_shape=jax.ShapeDtypeStruct(a.shape, a.dtype), compiler_params=_cparams("parallel"))(a, b)


def _chip_exchange(arrays, scatter, *, name):
    return _run_plan(_chip_exchange_plan(arrays, scatter), name)


_ALL_RELATIONS = [(dx, dy, dc) for dx in (0, 1) for dy in (0, 1) for dc in (0, 1)][1:]


def _all_to_all_plan(arrays):
    n_arr = len(arrays)
    n_rel = len(_ALL_RELATIONS)

    def local_copies(ins, outs, sems):
        me = 4 * lax.axis_index("x") + 2 * lax.axis_index("y") + lax.axis_index("c")
        return [pltpu.make_async_copy(ins[a].at[me], outs[a].at[me], sems[2].at[a]) for a in range(n_arr)]

    def remote_copies(ins, outs, sems, arrivals):
        x, y, c = lax.axis_index("x"), lax.axis_index("y"), lax.axis_index("c")
        me = 4 * x + 2 * y + c
        out = []
        for k, (dx, dy, dc) in enumerate(_ALL_RELATIONS):
            px, py, pc = x ^ dx, y ^ dy, c ^ dc
            peer = 4 * px + 2 * py + pc
            for a in range(n_arr):
                out.append(pltpu.make_async_remote_copy(
                    src_ref=ins[a].at[peer], dst_ref=outs[a].at[peer if arrivals else me],
                    send_sem=sems[0].at[a * n_rel + k], recv_sem=sems[1].at[a * n_rel + k],
                    device_id=(x, y, c) if arrivals else (px, py, pc), device_id_type=MESH))
        return out

    def start(ins, outs, sems):
        for cp in local_copies(ins, outs, sems) + remote_copies(ins, outs, sems, False):
            cp.start()

    def pass_on(ins, outs, sems):
        pass

    def finish(ins, outs, sems):
        for send in remote_copies(ins, outs, sems, False):
            send.wait_send()
        for arrival in remote_copies(ins, outs, sems, True):
            arrival.wait_recv()
        for cp in local_copies(ins, outs, sems):
            cp.wait()

    return _CommPlan(
        arrays, [jax.ShapeDtypeStruct(a.shape, a.dtype) for a in arrays],
        [pltpu.SemaphoreType.DMA((n_arr * n_rel,)), pltpu.SemaphoreType.DMA((n_arr * n_rel,)),
         pltpu.SemaphoreType.DMA((n_arr,))],
        (start, pass_on, finish))


def _chip_exchange_plan(arrays, scatter):
    n_arr = len(arrays)
    n_rel = len(_CHIP_RELATIONS)

    def local_copies(ins, outs, sems):
        me = 2 * lax.axis_index("x") + lax.axis_index("y")
        return [pltpu.make_async_copy(ins[a].at[me] if scatter[a] else ins[a], outs[a].at[me], sems[2].at[a])
                for a in range(n_arr)]

    def remote_copies(ins, outs, sems, arrivals):
        x, y, c = lax.axis_index("x"), lax.axis_index("y"), lax.axis_index("c")
        me = 2 * x + y
        out = []
        for k, (dx, dy, _) in enumerate(_CHIP_RELATIONS):
            px, py = x ^ dx, y ^ dy
            peer = 2 * px + py
            for a in range(n_arr):
                out.append(pltpu.make_async_remote_copy(
                    src_ref=ins[a].at[peer] if scatter[a] else ins[a], dst_ref=outs[a].at[peer if arrivals else me],
                    send_sem=sems[0].at[a * n_rel + k], recv_sem=sems[1].at[a * n_rel + k],
                    device_id=(x, y, c) if arrivals else (px, py, c), device_id_type=MESH))
        return out

    def start(ins, outs, sems):
        for cp in local_copies(ins, outs, sems) + remote_copies(ins, outs, sems, False):
            cp.start()

    def pass_on(ins, outs, sems):
        pass

    def finish(ins, outs, sems):
        for send in remote_copies(ins, outs, sems, False):
            send.wait_send()
        for arrival in remote_copies(ins, outs, sems, True):
            arrival.wait_recv()
        for cp in local_copies(ins, outs, sems):
            cp.wait()

    out_shape = [jax.ShapeDtypeStruct((N_CHIPS,) + (a.shape[1:] if s else a.shape), a.dtype)
                 for a, s in zip(arrays, scatter)]
    return _CommPlan(
        arrays, out_shape,
        [pltpu.SemaphoreType.DMA((n_arr * n_rel,)), pltpu.SemaphoreType.DMA((n_arr * n_rel,)),
         pltpu.SemaphoreType.DMA((n_arr,))],
        (start, pass_on, finish))


def _adamw(w, gslots, m, v, *, name):
    rows, cols = w.shape
    n_slots = gslots.shape[0]
    tr = _row_tile(rows, 128) if rows % 16 == 0 else rows

    def body(w_ref, g_ref, m_ref, v_ref, go_ref, d_ref, mo_ref, vo_ref):
        g = g_ref[0].astype(F32)
        for s in range(1, n_slots):
            g = g + g_ref[s].astype(F32)
        mn = ADAM_B1 * m_ref[...] + (1.0 - ADAM_B1) * g
        vn = ADAM_B2 * v_ref[...] + (1.0 - ADAM_B2) * (g * g)
        go_ref[...] = g
        mo_ref[...] = mn
        vo_ref[...] = vn
        m_hat = mn / (1.0 - ADAM_B1 ** ADAM_STEP)
        v_hat = vn / (1.0 - ADAM_B2 ** ADAM_STEP)
        d_ref[...] = -ADAM_LR * (m_hat / (jnp.sqrt(v_hat) + ADAM_EPS) + ADAM_WD * w_ref[...])

    blk = pl.BlockSpec((tr, cols), lambda i: (i, 0))
    shp = jax.ShapeDtypeStruct((rows, cols), F32)
    return pl.pallas_call(
        body, name=name, grid=(rows // tr,),
        in_specs=[blk, pl.BlockSpec((n_slots, tr, cols), lambda i: (0, i, 0)), blk, blk],
        out_specs=[blk] * 4, out_shape=[shp] * 4,
        compiler_params=_cparams("parallel"))(w, gslots, m, v)


_BIG = ("w_in", "w_ssd_branch", "w_attn_branch", "w_out", "w_ffn_in", "w_ffn_out")
_SMALL_SHARDED = ("meta_tokens", "ssd_conv_w", "gate_b", "ffn_conv_w")
_SMALL_REPLICATED = ("norm_mix_w", "ssd_conv_b", "ssd_dt_bias", "ssd_a_log", "ssd_d", "ssd_norm_w", "attn_sinks",
                     "rel_bias", "norm_ffn_w", "ffn_conv_b", "norm_final_w")
_WEIGHTS = ("meta_tokens", "norm_mix_w", "w_in", "ssd_conv_w", "ssd_conv_b", "ssd_dt_bias", "ssd_a_log", "ssd_d",
            "ssd_norm_w", "w_ssd_branch", "w_attn_branch", "attn_sinks", "rel_bias", "gate_b", "w_out", "norm_ffn_w",
            "w_ffn_in", "ffn_conv_w", "ffn_conv_b", "w_ffn_out", "norm_final_w")
_ROW_SHARDED = ("w_ssd_branch", "w_attn_branch", "w_out", "w_ffn_out")
_COL_SHARDED = ("w_in", "w_ffn_in", "meta_tokens", "ssd_conv_w", "gate_b", "ffn_conv_w")
_IN_SEGS = (("z", SSD_INNER), ("xbc", SSD_XBC), ("dt", SSD_HEADS), ("qkv", ATT_Q + 2 * ATT_KV), ("g", 2 * D_MODEL))


def _pack_rows(flat_parts, width, row_mult):
    flat = jnp.concatenate([p.reshape(-1) for p in flat_parts])
    pad = (-flat.shape[0]) % (width * row_mult)
    if pad:
        flat = jnp.concatenate([flat, jnp.zeros((pad,), flat.dtype)])
    return flat.reshape(-1, width)


def _unpack(flat, shapes):
    out, off = [], 0
    for shp in shapes:
        size = int(np.prod(shp))
        out.append(flat[off:off + size].reshape(shp))
        off += size
    return out


def _gather_full(stack, name, shard_shape):
    if name in _COL_SHARDED:
        return jnp.transpose(stack, (1, 0, 2)).reshape(shard_shape[0], N_DEV * shard_shape[1])
    return stack.reshape(N_DEV * shard_shape[0], shard_shape[1])


_IN_SEG_W = {"z": SSD_INNER, "xbc": SSD_XBC, "dt": DT_W, "qkv": ATT_Q + 2 * ATT_KV, "g": 2 * D_MODEL}
_IN_SHARD_W = (SSD_INNER + SSD_XBC + SSD_HEADS + ATT_Q + 2 * ATT_KV + 2 * D_MODEL) // N_DEV
_FFN_SHARD_W = 2 * D_FF // N_DEV


def _in_seg_runs():
    runs, off = [], 0
    for nm, width in _IN_SEGS:
        if nm == "dt":
            runs.append([(off + SSD_HPG * g, SSD_HPG, LANES * g) for g in range(SSD_GROUPS)])
        else:
            runs.append([(off, width, 0)])
        off += width
    return runs


def _w_in_to_segments(stack):
    pieces = [(seg, None, scol, 0, dev, col, w) for seg, scol, dev, col, w in _shard_pieces(_in_seg_runs(), _IN_SHARD_W)]
    outs = [((D_MODEL, _IN_SEG_W[nm]), stack.dtype) for nm, _ in _IN_SEGS]
    return dict(zip([nm for nm, _ in _IN_SEGS], _col_move([stack], outs, pieces, name="w_in_segments")))


def _segments_to_w_in_shards(seg_grads):
    pieces = [(0, dev, col, seg, None, scol, w) for seg, scol, dev, col, w in _shard_pieces(_in_seg_runs(), _IN_SHARD_W)]
    return _col_move(seg_grads, [((N_DEV, D_MODEL, _IN_SHARD_W), seg_grads[0].dtype)], pieces, name="g_w_in_shards")[0]


def _ffn_in_from_shards(stack):
    pieces = [(0, None, scol, 0, dev, col, w)
              for _, scol, dev, col, w in _shard_pieces([[(0, 2 * D_FF, 0)]], _FFN_SHARD_W)]
    return _col_move([stack], [((D_MODEL, 2 * D_FF), stack.dtype)], pieces, name="w_ffn_in_full")[0]


def _ffn_in_to_shards(g_up, g_gate):
    pieces = [(0, dev, col, seg, None, scol, w)
              for seg, scol, dev, col, w in _shard_pieces([[(0, D_FF, 0)], [(D_FF, D_FF, 0)]], _FFN_SHARD_W)]
    return _col_move([g_up, g_gate], [((N_DEV, D_MODEL, _FFN_SHARD_W), g_up.dtype)], pieces, name="g_w_ffn_in_shards")[0]


def _dt_spread(w_dt):
    k = w_dt.shape[0]
    w4 = w_dt.reshape(k, SSD_GROUPS, SSD_HPG)
    return jnp.pad(w4, ((0, 0), (0, 0), (0, LANES - SSD_HPG))).reshape(k, DT_W)


def _dt_gather(w_wide):
    k = w_wide.shape[0]
    return w_wide.reshape(k, SSD_GROUPS, LANES)[:, :, :SSD_HPG].reshape(k, SSD_HEADS)


class _LateExchanges:
    def __init__(self, two_d, shape2):
        self.two_d, self.shape2 = two_d, shape2
        self.early_grads_received = None
        self.w_in_grads_received = None

    def row_pack(self, tree):
        return jnp.concatenate([tree[k] for k in _ROW_SHARDED], axis=0)

    def late_weights_plan(self):
        return _gather_plan([self.two_d["w_ffn_in"].astype(BF16), self.row_pack(self.two_d).astype(BF16)])

    def late_weights(self, gathered):
        w_ffn_in_all, rows_all = [g.reshape((N_DEV,) + g.shape[2:]) for g in gathered]
        out = {"w_ffn_in": _ffn_in_from_shards(w_ffn_in_all)}
        off = 0
        for k in _ROW_SHARDED:
            r = self.shape2[k][0]
            out[k] = rows_all[:, off:off + r].reshape(N_DEV * r, D_MODEL)
            off += r
        return out

    def early_grads_plan(self, grads):
        rows_send = jnp.concatenate([grads[k].reshape(N_DEV, self.shape2[k][0], D_MODEL) for k in _ROW_SHARDED], axis=1)
        return _all_to_all_plan([_ffn_in_to_shards(*grads["w_ffn_in"]), rows_send])

    def w_in_grads_plan(self, seg_grads):
        return _all_to_all_plan([_segments_to_w_in_shards(seg_grads)])


def _local_step(x, target, w, exchanges=None):
    h0 = jnp.concatenate([jnp.zeros((PAD, D_MODEL), F32), w["meta_tokens"], x], axis=0)
    segs = w["in_segs"]

    dtb = _dt_spread(w["ssd_dt_bias"])
    alog = _dt_spread(w["ssd_a_log"])
    dskip_w = jnp.repeat(w["ssd_d"], SSD_HEADDIM, axis=1)
    sinks = jnp.pad(w["attn_sinks"], ((0, 0), (0, LANES - ATT_HEADS)))
    onehot_t = _onehot_t()
    tables = jnp.transpose(_bias_tables(w["rel_bias"].T, onehot_t).reshape(ATT_HEADS, 3, N_KEYS, BLOCK), (1, 0, 2, 3))

    u = _rms_fwd(h0, w["norm_mix_w"], name="rms_mix_fwd")
    z = _mm(u, segs["z"], name="in_z")
    xbc = _mm(u, segs["xbc"], name="in_xbc")
    dt_raw = _mm(u, segs["dt"], name="in_dt")
    qkv = _mm(u, segs["qkv"], name="in_qkv")
    gates = _mm(u, segs["g"], name="in_g")
    pre = _conv_fwd(xbc, w["ssd_conv_w"], w["ssd_conv_b"], name="ssd_conv_fwd")
    (y, yn, hsave), gathered = _ssd_fwd(pre, dt_raw, z, dtb, alog, dskip_w, w["ssd_norm_w"],
                                        side=None if exchanges is None else exchanges.late_weights_plan())
    if exchanges is not None:
        w = {**w, **exchanges.late_weights(gathered)}
    w_ffn_up, w_ffn_gate = w["w_ffn_in"][:, :D_FF], w["w_ffn_in"][:, D_FF:]
    y_ssd = _mm(yn, w["w_ssd_branch"], name="ssd_out")
    att = _attn_fwd(qkv, tables, sinks)
    y_att = _mm(att, w["w_attn_branch"], name="att_out")
    merged, h1 = _merge_out_fwd(gates, y_ssd, y_att, w["gate_b"], w["w_out"], h0)
    u2 = _rms_fwd(h1, w["norm_ffn_w"], name="rms_ffn_fwd")
    hid_raw = _mm(u2, w["w_ffn_in"], name="ffn_in")
    hid_up, hid_gate, act = _ffn_act_fwd(hid_raw, w["ffn_conv_w"], w["ffn_conv_b"])
    h2 = _mm(act, w["w_ffn_out"], c=h1, mask=True, name="ffn_out")
    dh2, loss_row, g_norm_final = _final_loss(h2, w["norm_final_w"], target)

    grads = {"norm_final_w": g_norm_final}
    dact = _mm(dh2, w["w_ffn_out"], tb=True, mask=True, name="d_act")
    grads["w_ffn_out"] = _mm(act, dh2, ta=True, mask=True, out_dtype=BF16, name="g_w_ffn_out")
    dx_up, dx_gate, dcw_up, dcw_gate, dcb_up, dcb_gate = _ffn_act_bwd(dact, hid_up, hid_gate, hid_raw, w["ffn_conv_w"])
    grads["ffn_conv_w"] = jnp.concatenate([dcw_up, dcw_gate], axis=1)
    grads["ffn_conv_b"] = jnp.concatenate([dcb_up, dcb_gate], axis=1)
    (dh1, grads["norm_ffn_w"]), _ = _mm_rms_bwd([(dx_up, w_ffn_up), (dx_gate, w_ffn_gate)], h1, w["norm_ffn_w"], dh2,
                                                name="d_u2_rms_bwd")
    grads["w_ffn_in"] = (_mm(u2, dx_up, ta=True, out_dtype=BF16, name="g_w_ffn_up"),
                         _mm(u2, dx_gate, ta=True, out_dtype=BF16, name="g_w_ffn_gate"))

    grads["w_out"] = _mm(merged, dh1, ta=True, mask=True, out_dtype=BF16, name="g_w_out")
    dy_ssd, dy_att, dgates, grads["gate_b"] = _merge_out_bwd(dh1, w["w_out"], gates, y_ssd, y_att, w["gate_b"])
    dyn = _mm(dy_ssd, w["w_ssd_branch"], tb=True, name="d_yn")
    grads["w_ssd_branch"] = _mm(yn, dy_ssd, ta=True, out_dtype=BF16, name="g_w_ssd")
    datt = _mm(dy_att, w["w_attn_branch"], tb=True, name="d_att")
    grads["w_attn_branch"] = _mm(att, dy_att, ta=True, out_dtype=BF16, name="g_w_att")
    (dz, dpxs, dpb, dpc, ddt, grads["ssd_norm_w"], g_dtb, g_alog, g_dskip), received = _ssd_bwd(
        dyn, y, z, pre, dt_raw, hsave, dtb, alog, dskip_w, w["ssd_norm_w"],
        side=None if exchanges is None else exchanges.early_grads_plan(grads))
    if exchanges is not None:
        exchanges.early_grads_received = received
    grads["ssd_dt_bias"] = _dt_gather(g_dtb)
    grads["ssd_a_log"] = _dt_gather(g_alog)
    grads["ssd_d"] = _dt_gather(g_dskip)
    conv_g = _conv_bwd(dpxs, xbc, w["ssd_conv_w"], name="ssd_conv_bwd_x")
    conv_g = _conv_bwd(dpb, xbc, w["ssd_conv_w"], name="ssd_conv_bwd_b", col0=SSD_INNER, into=conv_g)
    dxbc, grads["ssd_conv_w"], grads["ssd_conv_b"] = _conv_bwd(
        dpc, xbc, w["ssd_conv_w"], name="ssd_conv_bwd_c", col0=SSD_INNER + SSD_BC, into=conv_g)
    dqkv, d_tables, d_sinks = _attn_bwd(datt, qkv, tables, sinks)
    grads["attn_sinks"] = d_sinks[:, :ATT_HEADS]
    dtab = jnp.transpose(d_tables, (1, 0, 2, 3)).reshape(ATT_HEADS, NT_ALL)
    grads["rel_bias"] = _bias_grad(dtab, onehot_t).T
    dsegs = {"z": dz, "xbc": dxbc, "dt": ddt, "qkv": dqkv, "g": dgates}
    grads["in_segs"] = [_mm(u, dsegs[nm], ta=True, out_dtype=BF16, name="g_w_in_" + nm) for nm, _ in _IN_SEGS]
    (dh0, grads["norm_mix_w"]), received = _mm_rms_bwd(
        [(dsegs[nm], segs[nm]) for nm, _ in _IN_SEGS], h0, w["norm_mix_w"], dh1, name="d_u_rms_bwd",
        side=None if exchanges is None else exchanges.w_in_grads_plan(grads["in_segs"]))
    if exchanges is not None:
        exchanges.w_in_grads_received = received[0]
    grads["meta_tokens"] = dh0[PAD:BLOCK]
    return loss_row[0, 0], dh0[BLOCK:], grads


def kernel(x, meta_tokens, norm_mix_w, w_in, ssd_conv_w, ssd_conv_b, ssd_dt_bias, ssd_a_log, ssd_d, ssd_norm_w, w_ssd_branch, w_attn_branch, attn_sinks, rel_bias, gate_b, w_out, norm_ffn_w, w_ffn_in, ffn_conv_w, ffn_conv_b, w_ffn_out, norm_final_w, loss_target, m_meta_tokens, m_norm_mix_w, m_w_in, m_ssd_conv_w, m_ssd_conv_b, m_ssd_dt_bias, m_ssd_a_log, m_ssd_d, m_ssd_norm_w, m_w_ssd_branch, m_w_attn_branch, m_attn_sinks, m_rel_bias, m_gate_b, m_w_out, m_norm_ffn_w, m_w_ffn_in, m_ffn_conv_w, m_ffn_conv_b, m_w_ffn_out, m_norm_final_w, v_meta_tokens, v_norm_mix_w, v_w_in, v_ssd_conv_w, v_ssd_conv_b, v_ssd_dt_bias, v_ssd_a_log, v_ssd_d, v_ssd_norm_w, v_w_ssd_branch, v_w_attn_branch, v_attn_sinks, v_rel_bias, v_gate_b, v_w_out, v_norm_ffn_w, v_w_ffn_in, v_ffn_conv_w, v_ffn_conv_b, v_w_ffn_out, v_norm_final_w):
    shard = dict(meta_tokens=meta_tokens, norm_mix_w=norm_mix_w, w_in=w_in, ssd_conv_w=ssd_conv_w,
                 ssd_conv_b=ssd_conv_b, ssd_dt_bias=ssd_dt_bias, ssd_a_log=ssd_a_log, ssd_d=ssd_d,
                 ssd_norm_w=ssd_norm_w, w_ssd_branch=w_ssd_branch, w_attn_branch=w_attn_branch,
                 attn_sinks=attn_sinks, rel_bias=rel_bias, gate_b=gate_b, w_out=w_out, norm_ffn_w=norm_ffn_w,
                 w_ffn_in=w_ffn_in, ffn_conv_w=ffn_conv_w, ffn_conv_b=ffn_conv_b, w_ffn_out=w_ffn_out,
                 norm_final_w=norm_final_w)
    mom_m = dict(zip(_WEIGHTS, (m_meta_tokens, m_norm_mix_w, m_w_in, m_ssd_conv_w, m_ssd_conv_b, m_ssd_dt_bias,
                                m_ssd_a_log, m_ssd_d, m_ssd_norm_w, m_w_ssd_branch, m_w_attn_branch, m_attn_sinks,
                                m_rel_bias, m_gate_b, m_w_out, m_norm_ffn_w, m_w_ffn_in, m_ffn_conv_w, m_ffn_conv_b,
                                m_w_ffn_out, m_norm_final_w)))
    mom_v = dict(zip(_WEIGHTS, (v_meta_tokens, v_norm_mix_w, v_w_in, v_ssd_conv_w, v_ssd_conv_b, v_ssd_dt_bias,
                                v_ssd_a_log, v_ssd_d, v_ssd_norm_w, v_w_ssd_branch, v_w_attn_branch, v_attn_sinks,
                                v_rel_bias, v_gate_b, v_w_out, v_norm_ffn_w, v_w_ffn_in, v_ffn_conv_w, v_ffn_conv_b,
                                v_w_ffn_out, v_norm_final_w)))
    orig_shape = {k: a.shape for k, a in shard.items()}
    two_d = {k: a.reshape(a.shape[-2:]) if a.ndim >= 2 else a.reshape(1, -1) for k, a in shard.items()}
    shape2 = {k: a.shape for k, a in two_d.items()}

    def as2d(tree):
        return {k: tree[k].reshape(shape2[k]) for k in _WEIGHTS}

    mom_m, mom_v = as2d(mom_m), as2d(mom_v)

    exchanges = _LateExchanges(two_d, shape2)
    row_pack = exchanges.row_pack
    small_pack = _pack_rows([two_d[k] for k in _SMALL_SHARDED], LANES, SMALL_ROW_MULT)
    w_in_all, small_all = _gather_two_level([two_d["w_in"].astype(BF16), small_pack], name="gather_weights")
    full = {k: two_d[k] for k in _SMALL_REPLICATED}
    full["in_segs"] = _w_in_to_segments(w_in_all)
    small_flat = small_all.reshape(N_DEV, -1)
    off = 0
    for k in _SMALL_SHARDED:
        size = int(np.prod(shape2[k]))
        full[k] = _gather_full(small_flat[:, off:off + size].reshape((N_DEV,) + shape2[k]), k, shape2[k])
        off += size

    loss_local, grad_x, grads = _local_step(x[0], loss_target[0], full, exchanges)

    small_names = _SMALL_SHARDED + _SMALL_REPLICATED
    small_send = _pack_rows([grads[k] for k in small_names] + [loss_local.reshape(1)], LANES, SMALL_ROW_MULT)
    from_sib = _sibling_exchange([small_send], [False], name="small_grads_to_sibling")
    small_recv, = _chip_exchange([_add(small_send, from_sib[0], name="pair_sum_small")], [False],
                                 name="exchange_small_grads")
    in_recv = exchanges.w_in_grads_received
    ffn_recv, rows_recv = exchanges.early_grads_received

    big = {"w_in": _adamw(two_d["w_in"], in_recv, mom_m["w_in"], mom_v["w_in"], name="adamw_w_in"),
           "w_ffn_in": _adamw(two_d["w_ffn_in"], ffn_recv, mom_m["w_ffn_in"], mom_v["w_ffn_in"], name="adamw_w_ffn_in")}
    rows_out = _adamw(row_pack(two_d), rows_recv, row_pack(mom_m), row_pack(mom_v), name="adamw_rows")
    off = 0
    for k in _ROW_SHARDED:
        r = shape2[k][0]
        big[k] = [a[off:off + r] for a in rows_out]
        off += r
    me =4 * lax.axis_index("x") + 2 * lax.axis_index("y") + lax.axis_index("c")
    small_full_shapes = [grads[k].shape for k in small_names]
    n_small = sum(int(np.prod(s)) for s in small_full_shapes)

    def packed_small(tree):
        parts = []
        for k in small_names:
            a = tree[k]
            if k in _SMALL_SHARDED:
                fullw = jnp.zeros(grads[k].shape, F32)
                a = lax.dynamic_update_slice(fullw, a, (0, me * a.shape[1]))
            parts.append(a)
        return _pack_rows(parts + [jnp.zeros((1,), F32)], LANES, SMALL_ROW_MULT)

    g_small, d_small, m_small, v_small = _adamw(packed_small(two_d), small_recv, packed_small(mom_m),
                                                packed_small(mom_v), name="adamw_small")

    def unpack_all(which, small):
        out = {k: big[k][which] for k in _BIG}
        flat = small.reshape(-1)
        for k, a in zip(small_names, _unpack(flat, small_full_shapes)):
            if k in _SMALL_SHARDED:
                a = lax.dynamic_slice(a, (0, me * shape2[k][1]), shape2[k])
            out[k] = a
        return out, flat[n_small]

    g_all, loss = unpack_all(0, g_small)
    d_all, _ = unpack_all(1, d_small)
    m_all, _ = unpack_all(2, m_small)
    v_all, _ = unpack_all(3, v_small)

    def final(tree):
        return [tree[k].reshape(orig_shape[k]) for k in _WEIGHTS]

    return (loss, grad_x[None], *final(g_all), *final(d_all), *final(m_all), *final(v_all))
```

```python
import functools
import math

import numpy as np
import jax
import jax.numpy as jnp
from jax import lax
from jax.experimental import pallas as pl
from jax.experimental.pallas import tpu as pltpu

F32 = jnp.float32
BF16 = jnp.bfloat16
HIGHEST = lax.Precision.HIGHEST

D_MODEL = 1024
N_META = 16
BLOCK = 128
PAD = BLOCK - N_META
EPS = 1e-6
NEG = -1e30
SSD_INNER = 2 * D_MODEL
SSD_HEADDIM = 64
SSD_HEADS = SSD_INNER // SSD_HEADDIM
SSD_GROUPS = 4
SSD_HPG = SSD_HEADS // SSD_GROUPS
SSD_STATE = 128
SSD_CONV = 4
SSD_GW = SSD_HPG * SSD_HEADDIM
SSD_BC = SSD_GROUPS * SSD_STATE
SSD_XBC = SSD_INNER + 2 * SSD_BC
ATT_HEADS = 16
ATT_KV_HEADS = 2
ATT_HEADDIM = 64
ATT_GQ = ATT_HEADS // ATT_KV_HEADS
ATT_Q = ATT_HEADS * ATT_HEADDIM
ATT_KV = ATT_KV_HEADS * ATT_HEADDIM
REL_BUCKETS = 32
REL_MAX_DIST = 128
D_FF = 2816
FFN_CONV = 3
ADAM_LR = 0.001
ADAM_B1 = 0.9
ADAM_B2 = 0.999
ADAM_EPS = 1e-08
ADAM_WD = 0.01
ADAM_STEP = 10

N_DEV = 8
LANES = 128
SUBLANES = 8
DT_W = SSD_GROUPS * LANES
VMEM_LIMIT_BYTES = 56 * 1024 * 1024
MESH = pl.DeviceIdType.MESH

SMALL_ROW_MULT = 16

N_KEYS = 3 * BLOCK
NT_ALL = 3 * N_KEYS * BLOCK
NT_TILE = 8192


def _cparams(*sem):
    return pltpu.CompilerParams(dimension_semantics=sem, vmem_limit_bytes=VMEM_LIMIT_BYTES)


def _row_tile(n, cap):
    best = None
    for t in range(16, min(n, cap) + 1, 16):
        if n % t == 0:
            best = t
    return best or n


def _col_tile(n, cap):
    for t in (1408, 1280, 1024, 768, 640, 512, 384, 256, 128):
        if t <= cap and n % t == 0:
            return t
    return n


def _sigmoid(x):
    return 0.5 * jnp.tanh(0.5 * x) + 0.5


def _silu(x):
    return x * _sigmoid(x)


def _softplus(x):
    return jnp.maximum(x, 0.0) + jnp.log(1.0 + jnp.exp(-jnp.abs(x)))


def _dot_nt(a, b):
    return lax.dot_general(a, b, (((1,), (1,)), ((), ())), preferred_element_type=F32)


def _dot_tn(a, b):
    return lax.dot_general(a, b, (((0,), (0,)), ((), ())), preferred_element_type=F32)


def _dot(a, b):
    return jnp.dot(a, b, preferred_element_type=F32)


def _bf16_terms(x, terms):
    out, rest = [], x
    for _ in range(terms):
        part = rest.astype(BF16)
        out.append(part)
        rest = rest - part.astype(F32)
    return out


def _dot_sel(x, sel, terms=3):
    return sum(_dot(part, sel) for part in _bf16_terms(x, terms))


def _sel_dot(sel, x, terms=3):
    return sum(_dot(sel, part) for part in _bf16_terms(x, terms))


def _sum_all(x):
    return jnp.sum(jnp.sum(x, axis=1, keepdims=True), axis=0, keepdims=True)


MM_ROW_CAPS = (2080, 1664, 832, 416)
MM_COL_CAP = 1408
MM_VMEM_BUDGET = 44 * 1024 * 1024


def _mm_tiles(rows, cols, vmem_bytes):
    col_cands = [t for t in (2048, 1536, 1408, 1280, 1024, 768, 640, 512, 384, 256, 128) if cols % t == 0]
    if cols <= 2 * MM_COL_CAP:
        col_cands.append(cols)
    best = None
    for cap in MM_ROW_CAPS:
        tr = _row_tile(rows, cap)
        for tc in col_cands:
            if vmem_bytes(tr, tc) <= MM_VMEM_BUDGET and (best is None or tr * tc > best[0] * best[1]):
                best = (tr, tc)
    assert best is not None, (rows, cols)
    return best


def _mm(a, b, *, name, ta=False, tb=False, c=None, mask=False, out_dtype=F32):
    if not ta:
        m, k = a.shape
        n = b.shape[0] if tb else b.shape[1]
        tm, tn = _mm_tiles(m, n, lambda t_m, t_n: 2 * (t_m * k * a.dtype.itemsize + k * t_n * b.dtype.itemsize
                                                       + t_m * t_n * (jnp.dtype(out_dtype).itemsize
                                                                      + (0 if c is None else c.dtype.itemsize)))
                           + 4 * t_m * t_n)

        def body(*refs):
            if c is None:
                a_ref, b_ref, o_ref = refs
            else:
                a_ref, b_ref, c_ref, o_ref = refs
            acc = (_dot_nt if tb else _dot)(a_ref[...].astype(BF16), b_ref[...].astype(BF16))
            if mask:
                row = pl.program_id(0) * tm + lax.broadcasted_iota(jnp.int32, (tm, 1), 0)
                acc = jnp.where(row >= PAD, acc, 0.0)
            if c is not None:
                acc = acc + c_ref[...]
            o_ref[...] = acc.astype(out_dtype)

        b_spec = pl.BlockSpec((tn, k), lambda i, j: (j, 0)) if tb else pl.BlockSpec((k, tn), lambda i, j: (0, j))
        in_specs = [pl.BlockSpec((tm, k), lambda i, j: (i, 0)), b_spec]
        args = [a, b]
        if c is not None:
            in_specs.append(pl.BlockSpec((tm, tn), lambda i, j: (i, j)))
            args.append(c)
        return pl.pallas_call(
            body, name=name, grid=(m // tm, n // tn), in_specs=in_specs,
            out_specs=pl.BlockSpec((tm, tn), lambda i, j: (i, j)),
            out_shape=jax.ShapeDtypeStruct((m, n), out_dtype),
            compiler_params=_cparams("parallel", "parallel"))(*args)

    kc, m = a.shape
    n = b.shape[1]
    tm = _col_tile(m, MM_COL_CAP)
    tk, tn = _mm_tiles(kc, n, lambda t_k, t_n: 2 * (t_k * tm * a.dtype.itemsize + t_k * t_n * b.dtype.itemsize
                                                    + tm * t_n * jnp.dtype(out_dtype).itemsize) + 8 * tm * t_n)

    n_k = kc // tk

    def body_t(a_ref, b_ref, o_ref, acc_ref):
        kk = pl.program_id(2)
        bb = b_ref[...]
        if mask:
            row = kk * tk + lax.broadcasted_iota(jnp.int32, (tk, 1), 0)
            bb = jnp.where(row >= PAD, bb, jnp.zeros_like(bb))
        p = _dot_tn(a_ref[...].astype(BF16), bb.astype(BF16))

        @pl.when(kk == 0)
        def _():
            acc_ref[...] = p

        @pl.when(kk > 0)
        def _():
            acc_ref[...] += p

        @pl.when(kk == n_k - 1)
        def _():
            o_ref[...] = acc_ref[...].astype(out_dtype)

    return pl.pallas_call(
        body_t, name=name, grid=(m // tm, n // tn, n_k),
        in_specs=[pl.BlockSpec((tk, tm), lambda i, j, kk: (kk, i)), pl.BlockSpec((tk, tn), lambda i, j, kk: (kk, j))],
        out_specs=pl.BlockSpec((tm, tn), lambda i, j, kk: (i, j)),
        out_shape=jax.ShapeDtypeStruct((m, n), out_dtype),
        scratch_shapes=[pltpu.VMEM((tm, tn), F32)],
        compiler_params=_cparams("parallel", "parallel", "arbitrary"))(a, b)


def _mm_rms_bwd(pairs, x, w, dres, *, name, side=None):
    m, d = x.shape
    tm = _row_tile(m, 416)
    n_pairs = len(pairs)

    def body(*refs):
        a_refs, b_refs = refs[:n_pairs], refs[n_pairs:2 * n_pairs]
        x_ref, w_ref, dres_ref, dx_ref, dw_ref = refs[2 * n_pairs:]
        i = pl.program_id(0)
        dyv = None
        for a_ref, b_ref in zip(a_refs, b_refs):
            term = _dot_nt(a_ref[...].astype(BF16), b_ref[...])
            dyv = term if dyv is None else dyv + term
        xv = x_ref[...]
        r = lax.rsqrt(jnp.mean(xv * xv, axis=-1, keepdims=True) + EPS)
        xh = xv * r
        g = dyv * w_ref[...]
        dx_ref[...] = r * (g - xh * jnp.mean(g * xh, axis=-1, keepdims=True)) + dres_ref[...]
        part = jnp.sum(dyv * xh, axis=0, keepdims=True)

        @pl.when(i == 0)
        def _():
            dw_ref[...] = part

        @pl.when(i > 0)
        def _():
            dw_ref[...] += part

    row = pl.BlockSpec((tm, d), lambda i: (i, 0))
    vec = pl.BlockSpec((1, d), lambda i: (0, 0))
    in_specs = ([pl.BlockSpec((tm, a.shape[1]), lambda i: (i, 0)) for a, _ in pairs]
                + [pl.BlockSpec(b.shape, lambda i: (0, 0), pipeline_mode=pl.Buffered(1)) for _, b in pairs]
                + [row, vec, row])
    return _call_with_side(
        body, side, name=name, grid=(m // tm,), in_specs=in_specs, out_specs=[row, vec],
        out_shape=[jax.ShapeDtypeStruct((m, d), F32), jax.ShapeDtypeStruct((1, d), F32)], scratch_shapes=[],
        args=[a for a, _ in pairs] + [b for _, b in pairs] + [x, w, dres], semantics=("arbitrary",))


def _rms_fwd(h, w, *, name):
    n, d = h.shape
    tm = _row_tile(n, 832)

    def body(h_ref, w_ref, o_ref):
        x = h_ref[...]
        r = lax.rsqrt(jnp.mean(x * x, axis=-1, keepdims=True) + EPS)
        o_ref[...] = (x * r * w_ref[...]).astype(BF16)

    return pl.pallas_call(
        body, name=name, grid=(n // tm,),
        in_specs=[pl.BlockSpec((tm, d), lambda i: (i, 0)), pl.BlockSpec((1, d), lambda i: (0, 0))],
        out_specs=pl.BlockSpec((tm, d), lambda i: (i, 0)),
        out_shape=jax.ShapeDtypeStruct((n, d), BF16),
        compiler_params=_cparams("parallel"))(h, w)


def _final_loss(h, w, target):
    n, d = h.shape
    nb = n // BLOCK

    def body(h_ref, w_ref, t_ref, dh_ref, loss_ref, dw_ref):
        i = pl.program_id(0)
        xv = h_ref[...]
        r = lax.rsqrt(jnp.mean(xv * xv, axis=-1, keepdims=True) + EPS)
        xh = xv * r
        wv = w_ref[...]
        err = jnp.where(i >= 1, xh * wv - t_ref[...], 0.0)
        dyv = err * (1.0 / d)
        g = dyv * wv
        dh_ref[...] = r * (g - xh * jnp.mean(g * xh, axis=-1, keepdims=True))
        lpart = jnp.broadcast_to(0.5 * _sum_all(err * err) * (1.0 / d), (1, LANES))
        wpart = jnp.sum(dyv * xh, axis=0, keepdims=True)

        @pl.when(i == 0)
        def _():
            loss_ref[...] = lpart
            dw_ref[...] = wpart

        @pl.when(i > 0)
        def _():
            loss_ref[...] += lpart
            dw_ref[...] += wpart

    row = pl.BlockSpec((BLOCK, d), lambda i: (i, 0))
    vec = pl.BlockSpec((1, d), lambda i: (0, 0))
    return pl.pallas_call(
        body, name="final_loss", grid=(nb,),
        in_specs=[row, vec, pl.BlockSpec((BLOCK, d), lambda i: (jnp.maximum(i - 1, 0), 0))],
        out_specs=[row, pl.BlockSpec((1, LANES), lambda i: (0, 0)), vec],
        out_shape=[jax.ShapeDtypeStruct((n, d), F32), jax.ShapeDtypeStruct((1, LANES), F32),
                   jax.ShapeDtypeStruct((1, d), F32)],
        compiler_params=_cparams("arbitrary"))(h, w, target)


def _main_spec(tm, cb, off=0):
    return pl.BlockSpec((tm, cb), lambda j, i: (i, j + off))


def _prev_spec(tm, cb, off=0):
    r8 = tm // SUBLANES
    return pl.BlockSpec((SUBLANES, cb), lambda j, i: (jnp.maximum(i * r8 - 1, 0), j + off))


def _next_spec(tm, cb, n_rows, off=0):
    r8 = tm // SUBLANES
    last = n_rows // SUBLANES - 1
    return pl.BlockSpec((SUBLANES, cb), lambda j, i: (jnp.minimum((i + 1) * r8, last), j + off))


def _with_prev(prev_ref, main_ref, i):
    prev = jnp.where(i > 0, prev_ref[...], 0.0)
    return jnp.concatenate([prev, main_ref[...]], axis=0)


def _with_next(main, nxt, i, n_tiles):
    return jnp.concatenate([main, jnp.where(i < n_tiles - 1, nxt, 0.0)], axis=0)


def _back(xx, s, tm):
    if s == 0:
        return xx[SUBLANES:SUBLANES + tm]
    return pltpu.roll(xx, s, 0)[SUBLANES:SUBLANES + tm]


def _ahead(xx, s, tm):
    if s == 0:
        return xx[:tm]
    return pltpu.roll(xx, tm + SUBLANES - s, 0)[:tm]


def _conv_fwd(x, w, b, *, name):
    n, cdim = x.shape
    kw = w.shape[0]
    tm = _row_tile(n, 832)
    cb = _col_tile(cdim, 512)

    def body(xp_ref, x_ref, w_ref, b_ref, o_ref):
        xx = _with_prev(xp_ref, x_ref, pl.program_id(1))
        acc = jnp.broadcast_to(b_ref[...], (tm, cb))
        for k in range(kw):
            acc = acc + w_ref[k:k + 1, :] * _back(xx, kw - 1 - k, tm)
        o_ref[...] = acc

    return pl.pallas_call(
        body, name=name, grid=(cdim // cb, n // tm),
        in_specs=[_prev_spec(tm, cb), _main_spec(tm, cb), pl.BlockSpec((kw, cb), lambda j, i: (0, j)),
                  pl.BlockSpec((1, cb), lambda j, i: (0, j))],
        out_specs=_main_spec(tm, cb),
        out_shape=jax.ShapeDtypeStruct((n, cdim), F32),
        compiler_params=_cparams("parallel", "parallel"))(x, x, w, b)


def _conv_bwd_core(dpre_ext, x, w_ref, kw, tm):
    dx = None
    dws = []
    for k in range(kw):
        shifted = _ahead(dpre_ext, kw - 1 - k, tm)
        term = w_ref[k:k + 1, :] * shifted
        dx = term if dx is None else dx + term
        dws.append(jnp.sum(shifted * x, axis=0, keepdims=True))
    return dx, dws, jnp.sum(dpre_ext[:tm], axis=0, keepdims=True)


def _acc_rows(i, dw_ref, db_ref, dws, db):
    @pl.when(i == 0)
    def _():
        for k, v in enumerate(dws):
            dw_ref[k:k + 1, :] = v
        db_ref[...] = db

    @pl.when(i > 0)
    def _():
        for k, v in enumerate(dws):
            dw_ref[k:k + 1, :] += v
        db_ref[...] += db


def _conv_bwd(dpre, x, w, *, name, col0=0, into=None):
    n, cdim = x.shape
    kw = w.shape[0]
    tm = _row_tile(n, 832)
    cb = _col_tile(cdim, 512)
    nt = n // tm
    off = col0 // cb
    n_alias = 0 if into is None else 3

    def body(d_ref, dn_ref, x_ref, w_ref, *rest):
        dx_ref, dw_ref, db_ref = rest[n_alias:]
        i = pl.program_id(1)
        dpre_ext = _with_next(d_ref[...], dn_ref[...], i, nt)
        dx, dws, db = _conv_bwd_core(dpre_ext, x_ref[...], w_ref, kw, tm)
        dx_ref[...] = dx.astype(BF16)
        _acc_rows(i, dw_ref, db_ref, dws, db)

    wspec = pl.BlockSpec((kw, cb), lambda j, i: (0, j + off))
    bspec = pl.BlockSpec((1, cb), lambda j, i: (0, j + off))
    return pl.pallas_call(
        body, name=name, grid=(dpre.shape[1] // cb, nt),
        in_specs=[_main_spec(tm, cb), _next_spec(tm, cb, n), _main_spec(tm, cb, off), wspec]
        + [pl.BlockSpec(memory_space=pl.ANY)] * n_alias,
        out_specs=[_main_spec(tm, cb, off), wspec, bspec],
        out_shape=[jax.ShapeDtypeStruct((n, cdim), BF16), jax.ShapeDtypeStruct((kw, cdim), F32),
                   jax.ShapeDtypeStruct((1, cdim), F32)],
        input_output_aliases={4 + k: k for k in range(n_alias)},
        compiler_params=_cparams("parallel", "arbitrary"))(dpre, dpre, x, w, *(into or ()))


def _ffn_act_fwd(x, w, b):
    n = x.shape[0]
    kw = w.shape[0]
    tm = _row_tile(n, 832)
    cb = _col_tile(D_FF, 256)
    nc = D_FF // cb

    def body(xpu_ref, xu_ref, xpg_ref, xg_ref, wu_ref, wg_ref, bu_ref, bg_ref, hu_ref, hg_ref, act_ref):
        i = pl.program_id(1)
        outs = []
        for xp_ref, x_ref, w_ref, b_ref in ((xpu_ref, xu_ref, wu_ref, bu_ref), (xpg_ref, xg_ref, wg_ref, bg_ref)):
            xx = _with_prev(xp_ref, x_ref, i)
            acc = jnp.broadcast_to(b_ref[...], (tm, cb))
            for k in range(kw):
                acc = acc + w_ref[k:k + 1, :] * _back(xx, kw - 1 - k, tm)
            outs.append(acc)
        hu_ref[...] = outs[0]
        hg_ref[...] = outs[1]
        act_ref[...] = (_silu(outs[1]) * outs[0]).astype(BF16)

    def wspec(off):
        return pl.BlockSpec((kw, cb), lambda j, i: (0, j + off))

    def bspec(off):
        return pl.BlockSpec((1, cb), lambda j, i: (0, j + off))

    out = _main_spec(tm, cb)
    return pl.pallas_call(
        body, name="ffn_act_fwd", grid=(nc, n // tm),
        in_specs=[_prev_spec(tm, cb), _main_spec(tm, cb), _prev_spec(tm, cb, nc), _main_spec(tm, cb, nc),
                  wspec(0), wspec(nc), bspec(0), bspec(nc)],
        out_specs=[out, out, out],
        out_shape=[jax.ShapeDtypeStruct((n, D_FF), F32), jax.ShapeDtypeStruct((n, D_FF), F32),
                   jax.ShapeDtypeStruct((n, D_FF), BF16)],
        compiler_params=_cparams("parallel", "parallel"))(x, x, x, x, w, w, b, b)


def _ffn_act_bwd(dact, hu, hg, x, w):
    n = x.shape[0]
    kw = w.shape[0]
    tm = _row_tile(n, 832)
    cb = _col_tile(D_FF, 256)
    nc = D_FF // cb
    nt = n // tm

    def body(d_ref, dn_ref, hu_ref, hun_ref, hg_ref, hgn_ref, xu_ref, xg_ref, wu_ref, wg_ref,
             dxu_ref, dxg_ref, dwu_ref, dwg_ref, dbu_ref, dbg_ref):
        i = pl.program_id(1)
        dact_e = _with_next(d_ref[...], dn_ref[...], i, nt)
        up_e = _with_next(hu_ref[...], hun_ref[...], i, nt)
        gate_e = _with_next(hg_ref[...], hgn_ref[...], i, nt)
        sg = _sigmoid(gate_e)
        dup_e = dact_e * (gate_e * sg)
        dgate_e = dact_e * up_e * (sg * (1.0 + gate_e * (1.0 - sg)))
        dx, dws, db = _conv_bwd_core(dup_e, xu_ref[...], wu_ref, kw, tm)
        dxu_ref[...] = dx.astype(BF16)
        _acc_rows(i, dwu_ref, dbu_ref, dws, db)
        dx, dws, db = _conv_bwd_core(dgate_e, xg_ref[...], wg_ref, kw, tm)
        dxg_ref[...] = dx.astype(BF16)
        _acc_rows(i, dwg_ref, dbg_ref, dws, db)

    main, nxt = _main_spec(tm, cb), _next_spec(tm, cb, n)
    wspec0 = pl.BlockSpec((kw, cb), lambda j, i: (0, j))
    wspec1 = pl.BlockSpec((kw, cb), lambda j, i: (0, j + nc))
    bspec = pl.BlockSpec((1, cb), lambda j, i: (0, j))
    return pl.pallas_call(
        body, name="ffn_act_bwd", grid=(nc, nt),
        in_specs=[main, nxt, main, nxt, main, nxt, _main_spec(tm, cb), _main_spec(tm, cb, nc), wspec0, wspec1],
        out_specs=[main, main, wspec0, wspec0, bspec, bspec],
        out_shape=[jax.ShapeDtypeStruct((n, D_FF), BF16), jax.ShapeDtypeStruct((n, D_FF), BF16),
                   jax.ShapeDtypeStruct((kw, D_FF), F32), jax.ShapeDtypeStruct((kw, D_FF), F32),
                   jax.ShapeDtypeStruct((1, D_FF), F32), jax.ShapeDtypeStruct((1, D_FF), F32)],
        compiler_params=_cparams("parallel", "arbitrary"))(dact, dact, hu, hu, hg, hg, x, x, w, w)


def _ssd_prep(pxs_ref, pb_ref, pc_ref, dtr_ref, dtb_ref, alog_ref, c):
    xs = _silu(pxs_ref[...])
    bm = _silu(pb_ref[...])
    cm = _silu(pc_ref[...])
    return (xs, bm, cm) + _ssd_decay(dtr_ref, dtb_ref, alog_ref, c)


def _ssd_decay(dtr_ref, dtb_ref, alog_ref, c):
    row =lax.broadcasted_iota(jnp.int32, (BLOCK, 1), 0) + c * BLOCK
    valid = (row >= PAD).astype(F32)
    dtr = dtr_ref[...] + dtb_ref[...]
    dt = _softplus(dtr) * valid
    a = -jnp.exp(alog_ref[...])
    lam = dt * a
    ri = lax.broadcasted_iota(jnp.int32, (BLOCK, BLOCK), 0)
    ci = lax.broadcasted_iota(jnp.int32, (BLOCK, BLOCK), 1)
    causal = ci <= ri
    cs = _sel_dot(causal.astype(BF16), lam)
    return valid, dtr, dt, a, lam, cs, causal


def _head_cols(r):
    return slice(SSD_HEADDIM * r, SSD_HEADDIM * (r + 1))


def _ssd_specs(nc, rev):
    def cidx(c):
        return nc - 1 - c if rev else c

    xs = pl.BlockSpec((BLOCK, SSD_GW), lambda g, c: (cidx(c), g))
    bspec = pl.BlockSpec((BLOCK, SSD_STATE), lambda g, c: (cidx(c), SSD_INNER // SSD_STATE + g))
    cspec = pl.BlockSpec((BLOCK, SSD_STATE), lambda g, c: (cidx(c), (SSD_INNER + SSD_BC) // SSD_STATE + g))
    lane = pl.BlockSpec((BLOCK, LANES), lambda g, c: (cidx(c), g))
    vec = pl.BlockSpec((1, LANES), lambda g, c: (0, g))
    wide_vec = pl.BlockSpec((1, SSD_GW), lambda g, c: (0, g))
    hsave = pl.BlockSpec((1, 1, SSD_GW, SSD_STATE), lambda g, c: (cidx(c), g, 0, 0))
    return xs, bspec, cspec, lane, vec, wide_vec, hsave


def _head_spread_matrix():
    r = lax.broadcasted_iota(jnp.int32, (LANES, SSD_GW), 0)
    col = lax.broadcasted_iota(jnp.int32, (LANES, SSD_GW), 1)
    return (col // SSD_HEADDIM == r).astype(BF16)


def _const_spec(shape):
    return pl.BlockSpec(shape, lambda g, c: (0,) * len(shape))


def _spread_heads(per_head, e_ref):
    wide = _dot_sel(jnp.concatenate(per_head, axis=0), e_ref[...])
    return [wide[BLOCK * k:BLOCK * (k + 1)] for k in range(len(per_head))]


def _call_with_side(body, side, *, name, grid, in_specs, out_specs, out_shape, scratch_shapes, args,
                    semantics=("parallel", "arbitrary")):
    if side is None:
        outs = pl.pallas_call(body, name=name, grid=grid, in_specs=in_specs, out_specs=out_specs, out_shape=out_shape,
                              scratch_shapes=scratch_shapes, compiler_params=_cparams(*semantics))(*args)
        return outs, []
    n_in, n_out, n_scr, n_side = len(in_specs), len(out_specs), len(scratch_shapes), len(side.arrays)

    def body_with_side(*refs):
        ins, rest = refs[:n_in + n_side], refs[n_in + n_side:]
        outs, scratch = rest[:n_out + n_side], rest[n_out + n_side:]
        side_refs = (ins[n_in:], outs[n_out:], scratch[n_scr:])
        ids = [pl.program_id(k) for k in range(len(grid))]
        inner_first = functools.reduce(jnp.logical_and, [i == 0 for i in ids[1:]], True)

        @pl.when((ids[0] == 0) & inner_first)
        def _():
            side.phases[0](*side_refs)

        body(*ins[:n_in], *outs[:n_out], *scratch[:n_scr])

        @pl.when((ids[0] == grid[0] // 2) & inner_first)
        def _():
            side.phases[1](*side_refs)

        @pl.when(functools.reduce(jnp.logical_and, [i == n - 1 for i, n in zip(ids, grid)]))
        def _():
            side.phases[2](*side_refs)

    any_spec = pl.BlockSpec(memory_space=pl.ANY)
    outs = pl.pallas_call(
        body_with_side, name=name, grid=grid, in_specs=list(in_specs) + [any_spec] * n_side,
        out_specs=list(out_specs) + [any_spec] * n_side, out_shape=list(out_shape) + list(side.out_shape),
        scratch_shapes=list(scratch_shapes) + list(side.scratch_shapes),
        compiler_params=_cparams(*["arbitrary"] * len(grid)))(*args, *side.arrays)
    return outs[:n_out], outs[n_out:]


def _ssd_fwd(pre, dt_raw, z, dtb, alog, dskip_w, norm_w, side=None):
    n = pre.shape[0]
    nc = n // BLOCK
    xs_s, b_s, c_s, lane_s, vec_s, wide_s, hs_s = _ssd_specs(nc, False)

    def body(pxs_ref, pb_ref, pc_ref, dtr_ref, z_ref, dtb_ref, alog_ref, dskw_ref, nw_ref, e_ref,
             y_ref, yn_ref, hs_ref, h_scr):
        c = pl.program_id(1)

        @pl.when(c == 0)
        def _():
            h_scr[...] = jnp.zeros_like(h_scr)

        xs, bm, cm, _, _, dt, _, _, cs, causal = _ssd_prep(pxs_ref, pb_ref, pc_ref, dtr_ref, dtb_ref, alog_ref, c)
        cst = cs.T
        cs_last = cs[BLOCK - 1:BLOCK, :]
        dt_w, ecs_w, dec_w = _spread_heads([dt, jnp.exp(cs), jnp.exp(cs_last - cs)], e_ref)
        xdt = xs * dt_w
        bmb = bm.astype(BF16)
        cmb = cm.astype(BF16)
        cb = _dot_nt(cmb, bmb)
        hg = h_scr[...]
        hs_ref[0, 0] = hg
        y = _dot_nt(cmb, hg.astype(BF16)) * ecs_w + dskw_ref[...] * xs
        first = lax.broadcasted_iota(jnp.int32, (BLOCK, LANES), 1) < SSD_HEADDIM
        diag = []
        for j in range(SSD_HPG // 2):
            xp = xdt[:, LANES * j:LANES * (j + 1)].astype(BF16)
            res = []
            for r in (2 * j, 2 * j + 1):
                lm = jnp.exp(jnp.where(causal, cs[:, r:r + 1] - cst[r:r + 1, :], NEG))
                res.append(_dot((cb * lm).astype(BF16), xp))
            diag.append(jnp.where(first, res[0], res[1]))
        y = y + jnp.concatenate(diag, axis=1)
        st = _dot_tn((xdt * dec_w).astype(BF16), bmb)
        eh = jnp.exp(cs_last)
        for r in range(SSD_HPG):
            rows = _head_cols(r)
            h_scr[rows, :] = hg[rows, :] * eh[:, r:r + 1] + st[rows, :]
        y_ref[...] = y
        gts = y * _silu(z_ref[...])
        rr = lax.rsqrt(jnp.mean(gts * gts, axis=-1, keepdims=True) + EPS)
        yn_ref[...] = (gts * rr * nw_ref[...]).astype(BF16)

    return _call_with_side(
        body, side, name="ssd_fwd", grid=(SSD_GROUPS, nc),
        in_specs=[xs_s, b_s, c_s, lane_s, xs_s, vec_s, vec_s, wide_s, wide_s, _const_spec((LANES, SSD_GW))],
        out_specs=[xs_s, xs_s, hs_s],
        out_shape=[jax.ShapeDtypeStruct((n, SSD_INNER), F32), jax.ShapeDtypeStruct((n, SSD_INNER), BF16),
                   jax.ShapeDtypeStruct((nc, SSD_GROUPS, SSD_GW, SSD_STATE), F32)],
        scratch_shapes=[pltpu.VMEM((SSD_GW, SSD_STATE), F32)],
        args=(pre, pre, pre, dt_raw, z, dtb, alog, dskip_w, norm_w, _head_spread_matrix()))


def _lane_put(acc, col, r):
    lane = lax.broadcasted_iota(jnp.int32, acc.shape, 1)
    return jnp.where(lane == r, col, acc)


def _ssd_bwd(dyn, y, z, pre, dt_raw, hsave, dtb, alog, dskip_w, norm_w, side=None):
    n = pre.shape[0]
    nc = n // BLOCK
    spread = _head_spread_matrix()
    xs_s, b_s, c_s, lane_s, vec_s, wide_s, hs_s = _ssd_specs(nc, True)
    bc_out =pl.BlockSpec((BLOCK, SSD_STATE), lambda g, c: (nc - 1 - c, g))

    def body(dyn_ref, y_ref, z_ref, pxs_ref, pb_ref, pc_ref, dtr_ref, hs_ref, dtb_ref, alog_ref, dskw_ref, nw_ref,
             e_ref, r_ref,
             dz_ref, dxs_ref, dbm_ref, dcm_ref, ddt_ref, dnw_ref, ddtb_ref, dalog_ref, ddsk_ref, g_scr):
        step = pl.program_id(1)
        c = nc - 1 - step

        @pl.when(step == 0)
        def _():
            g_scr[...] = jnp.zeros_like(g_scr)

        pxs, pb, pc = pxs_ref[...], pb_ref[...], pc_ref[...]
        sx, sb, sc = _sigmoid(pxs), _sigmoid(pb), _sigmoid(pc)
        xs, bm, cm = pxs * sx, pb * sb, pc * sc
        valid, dtr, dt, a, lam, cs, causal = _ssd_decay(dtr_ref, dtb_ref, alog_ref, c)
        cst = cs.T
        cs_last = cs[BLOCK - 1:BLOCK, :]
        bmb = bm.astype(BF16)
        cmb = cm.astype(BF16)
        cb = _dot_nt(cmb, bmb)
        hg = hs_ref[0, 0]
        hgb = hg.astype(BF16)
        yoff = _dot_nt(cmb, hgb)
        gn = g_scr[...]
        gnb = gn.astype(BF16)

        zv = z_ref[...]
        yv = y_ref[...]
        sgz = _sigmoid(zv)
        sz = zv * sgz
        gts = yv * sz
        rr = lax.rsqrt(jnp.mean(gts * gts, axis=-1, keepdims=True) + EPS)
        xh = gts * rr
        dynv = dyn_ref[...]
        gg = dynv * nw_ref[...]
        dgts = rr * (gg - xh * jnp.mean(gg * xh, axis=-1, keepdims=True))
        dnw = jnp.sum(dynv * xh, axis=0, keepdims=True)
        dy = dgts * sz
        dz_ref[...] = (dgts * yv * (sgz * (1.0 + zv * (1.0 - sgz)))).astype(BF16)

        ecs = jnp.exp(cs)
        dec = jnp.exp(cs_last - cs)
        eh = jnp.exp(cs_last)
        dt_w, ecs_w, dec_w = _spread_heads([dt, ecs, dec], e_ref)
        red_m = r_ref[...]

        def head_sums(v):
            return _dot_sel(v, red_m, terms=2)

        xdt = xs * dt_w
        q_all = _dot_nt(bmb, gnb)
        w_all = (dy * ecs_w).astype(BF16)
        e_hl = head_sums(q_all * xdt) * dec
        dcs_col = head_sums(dy * yoff) * ecs - e_hl
        gh = jnp.zeros((1, LANES), F32)
        prod = gn * hg
        for r in range(SSD_HPG):
            gh = _lane_put(gh, _sum_all(prod[_head_cols(r), :]), r)
        dcs_last = jnp.sum(e_hl, axis=0, keepdims=True) + eh * gh
        ddsk = jnp.sum(head_sums(dy * xs), axis=0, keepdims=True)
        cbt = _dot_nt(bmb, cmb)
        lane = lax.broadcasted_iota(jnp.int32, (BLOCK, LANES), 1)
        first = lane < SSD_HEADDIM
        causal_t = lax.broadcasted_iota(jnp.int32, (BLOCK, BLOCK), 1) >= lax.broadcasted_iota(
            jnp.int32, (BLOCK, BLOCK), 0)
        sub = lax.broadcasted_iota(jnp.int32, (SUBLANES, BLOCK), 0)
        dcs_row = jnp.zeros((SUBLANES, BLOCK), F32)
        dcb = jnp.zeros((BLOCK, BLOCK), F32)
        dxdt_pairs = []
        for j in range(SSD_HPG // 2):
            tile = slice(LANES * j, LANES * (j + 1))
            dy_p = dy[:, tile]
            dyb = dy_p.astype(BF16)
            xdtb = xdt[:, tile].astype(BF16)
            res = []
            for half, r in enumerate((2 * j, 2 * j + 1)):
                csc, csr = cs[:, r:r + 1], cst[r:r + 1, :]
                lm = jnp.exp(jnp.where(causal, csc - csr, NEG))
                lmt = jnp.exp(jnp.where(causal_t, csr - csc, NEG))
                keep = first if half == 0 else jnp.logical_not(first)
                gm = _dot_nt(jnp.where(keep, dy_p, 0.0).astype(BF16), xdtb) * lm
                dcb = dcb + gm
                mm_ = gm * cb
                dcs_col = dcs_col + jnp.where(lane == r, jnp.sum(mm_, axis=1, keepdims=True), 0.0)
                dcs_row = jnp.where(sub == r, jnp.sum(mm_, axis=0, keepdims=True), dcs_row)
                res.append(_dot((cbt * lmt).astype(BF16), dyb))
            dxdt_pairs.append(jnp.where(first, res[0], res[1]))
        dxdt = jnp.concatenate(dxdt_pairs, axis=1) + q_all * dec_w
        ddt_x = head_sums(dxdt * xs)
        dxs = dxdt * dt_w + dskw_ref[...] * dy
        dcbb = dcb.astype(BF16)
        dcm = _dot(w_all, hgb) + _dot(dcbb, bmb)
        dbm = _dot((xdt * dec_w).astype(BF16), gnb) + _dot_tn(dcbb, cmb)
        dh_off = _dot_tn(w_all, cmb)
        for r in range(SSD_HPG):
            rows = _head_cols(r)
            g_scr[rows, :] = gn[rows, :] * eh[:, r:r + 1] + dh_off[rows, :]

        pad_rows = jnp.zeros((BLOCK - SUBLANES, BLOCK), F32)
        dcs = dcs_col - jnp.concatenate([dcs_row, pad_rows], axis=0).T
        rsel = lax.broadcasted_iota(jnp.int32, (BLOCK, LANES), 0)
        dcs = dcs + jnp.where(rsel == BLOCK - 1, dcs_last, 0.0)
        ri = lax.broadcasted_iota(jnp.int32, (BLOCK, BLOCK), 0)
        ci = lax.broadcasted_iota(jnp.int32, (BLOCK, BLOCK), 1)
        dlam = _sel_dot((ci >= ri).astype(BF16), dcs)
        head = lane < SSD_HPG
        ddt = dlam * a + ddt_x
        ddtr = jnp.where(head, ddt * _sigmoid(dtr) * valid, 0.0)
        ddt_ref[...] = ddtr.astype(BF16)
        dalog = jnp.sum(jnp.where(head, dlam * lam, 0.0), axis=0, keepdims=True)
        ddtb = jnp.sum(ddtr, axis=0, keepdims=True)

        dxs_ref[...] = dxs * (sx * (1.0 + pxs * (1.0 - sx)))
        dbm_ref[...] = dbm * (sb * (1.0 + pb * (1.0 - sb)))
        dcm_ref[...] = dcm * (sc * (1.0 + pc * (1.0 - sc)))

        @pl.when(step == 0)
        def _():
            dnw_ref[...] = dnw
            ddtb_ref[...] = ddtb
            dalog_ref[...] = dalog
            ddsk_ref[...] = ddsk

        @pl.when(step > 0)
        def _():
            dnw_ref[...] += dnw
            ddtb_ref[...] += ddtb
            dalog_ref[...] += dalog
            ddsk_ref[...] += ddsk

    return _call_with_side(
        body, side, name="ssd_bwd", grid=(SSD_GROUPS, nc),
        in_specs=[xs_s, xs_s, xs_s, xs_s, b_s, c_s, lane_s, hs_s, vec_s, vec_s, wide_s, wide_s,
                  _const_spec((LANES, SSD_GW)), _const_spec((SSD_GW, LANES))],
        out_specs=[xs_s, xs_s, bc_out, bc_out, lane_s, wide_s, vec_s, vec_s, vec_s],
        out_shape=[jax.ShapeDtypeStruct((n, SSD_INNER), BF16), jax.ShapeDtypeStruct((n, SSD_INNER), F32),
                   jax.ShapeDtypeStruct((n, SSD_BC), F32), jax.ShapeDtypeStruct((n, SSD_BC), F32),
                   jax.ShapeDtypeStruct((n, DT_W), BF16), jax.ShapeDtypeStruct((1, SSD_INNER), F32),
                   jax.ShapeDtypeStruct((1, DT_W), F32), jax.ShapeDtypeStruct((1, DT_W), F32),
                   jax.ShapeDtypeStruct((1, DT_W), F32)],
        scratch_shapes=[pltpu.VMEM((SSD_GW, SSD_STATE), F32)],
        args=(dyn, y, z, pre, pre, pre, dt_raw, hsave, dtb, alog, dskip_w, norm_w, spread, spread.T))


def _bucket_table():
    def bucket(dist):
        d = np.maximum(dist, 0)
        half = REL_BUCKETS // 2
        big = half + (np.log(np.maximum(d, half).astype(np.float32) / np.float32(half))
                      / np.float32(math.log(REL_MAX_DIST / half)) * np.float32(REL_BUCKETS - half)).astype(np.int32)
        return np.where(d < half, d, np.minimum(big, REL_BUCKETS - 1)).astype(np.int32)

    l = np.arange(BLOCK)[None, :]
    band = bucket(l + BLOCK - np.arange(2 * BLOCK)[:, None])
    j = np.arange(BLOCK)[:, None]
    tables = [np.concatenate([bucket(v * BLOCK + l - j), band], axis=0) for v in range(3)]
    return np.concatenate([t.reshape(-1) for t in tables])


def _onehot_t():
    buckets = jnp.asarray(_bucket_table())
    return (buckets[None, :] == jnp.arange(REL_BUCKETS, dtype=jnp.int32)[:, None]).astype(F32)


def _bias_tables(rel_t, onehot_t):
    def body(r_ref, oh_ref, o_ref):
        o_ref[...] = jnp.dot(r_ref[...], oh_ref[...], precision=HIGHEST, preferred_element_type=F32)

    return pl.pallas_call(
        body, name="bias_tables", grid=(NT_ALL // NT_TILE,),
        in_specs=[pl.BlockSpec((ATT_HEADS, REL_BUCKETS), lambda i: (0, 0)),
                  pl.BlockSpec((REL_BUCKETS, NT_TILE), lambda i: (0, i))],
        out_specs=pl.BlockSpec((ATT_HEADS, NT_TILE), lambda i: (0, i)),
        out_shape=jax.ShapeDtypeStruct((ATT_HEADS, NT_ALL), F32),
        compiler_params=_cparams("parallel"))(rel_t, onehot_t)


def _bias_grad(dtab, onehot_t):
    def body(d_ref, oh_ref, o_ref):
        i = pl.program_id(0)
        p = lax.dot_general(d_ref[...], oh_ref[...], (((1,), (1,)), ((), ())), precision=HIGHEST,
                            preferred_element_type=F32)

        @pl.when(i == 0)
        def _():
            o_ref[...] = p

        @pl.when(i > 0)
        def _():
            o_ref[...] += p

    return pl.pallas_call(
        body, name="bias_grad", grid=(NT_ALL // NT_TILE,),
        in_specs=[pl.BlockSpec((ATT_HEADS, NT_TILE), lambda i: (0, i)),
                  pl.BlockSpec((REL_BUCKETS, NT_TILE), lambda i: (0, i))],
        out_specs=pl.BlockSpec((ATT_HEADS, REL_BUCKETS), lambda i: (0, 0)),
        out_shape=jax.ShapeDtypeStruct((ATT_HEADS, REL_BUCKETS), F32),
        compiler_params=_cparams("arbitrary"))(dtab, onehot_t)


def _att_mask_t(n, copies):
    far = 4 * BLOCK
    kk = lax.broadcasted_iota(jnp.int32, (N_KEYS, copies * BLOCK), 0)
    li = lax.broadcasted_iota(jnp.int32, (N_KEYS, copies * BLOCK), 1) & (BLOCK - 1)
    meta_ok = (kk >= PAD) & (kk < BLOCK) & (li + jnp.where(n >= 1, far, 0) >= kk)
    prev_ok = (kk >= BLOCK) & (kk < 2 * BLOCK) & (kk - BLOCK > li + jnp.where(n >= 2, 0, far))
    cur_ok = (kk >= 2 * BLOCK) & (kk - 2 * BLOCK <= li - jnp.where(n >= 1, 0, far))
    return meta_ok | prev_ok | cur_ok


def _att_kv(meta_ref, prev_ref, cur_ref):
    kv = jnp.concatenate([meta_ref[...], prev_ref[...], cur_ref[...]], axis=0)
    first = lax.broadcasted_iota(jnp.int32, (N_KEYS, LANES), 1) < ATT_HEADDIM
    out = []
    for pair in (kv[:, :LANES], kv[:, LANES:]):
        swapped = pltpu.roll(pair, ATT_HEADDIM, 1)
        out.append([jnp.where(first, pair, swapped).astype(BF16), jnp.where(first, swapped, pair).astype(BF16)])
    return out[0], out[1]


def _split_heads(x_pair, first):
    return jnp.concatenate([jnp.where(first, x_pair, 0.0), jnp.where(first, 0.0, x_pair)], axis=0).astype(BF16)


def _att_probs_t(qm2, k_dup, t_ref, j, mask2, sink_ref):
    scale = ATT_HEADDIM ** -0.5
    bias2 = jnp.concatenate([t_ref[0, 2 * j], t_ref[0, 2 * j + 1]], axis=1)
    second = lax.broadcasted_iota(jnp.int32, (1, 2 * BLOCK), 1) >= BLOCK
    sink2 = jnp.where(second, sink_ref[0:1, 2 * j + 1:2 * j + 2], sink_ref[0:1, 2 * j:2 * j + 1])
    s_t = jnp.where(mask2, _dot_nt(k_dup, qm2) * scale + bias2, NEG)
    mx = jnp.maximum(jnp.max(s_t, axis=0, keepdims=True), sink2)
    p_t = jnp.exp(s_t - mx)
    p_s = jnp.exp(sink2 - mx)
    inv = 1.0 / (jnp.sum(p_t, axis=0, keepdims=True) + p_s)
    return p_t * inv, p_s * inv


def _att_specs(nb, rev):
    def nidx(i):
        return nb - 1 - i if rev else i

    kvb = ATT_Q // (2 * ATT_KV)
    q_s = pl.BlockSpec((BLOCK, ATT_Q), lambda i: (nidx(i), 0))
    cur = pl.BlockSpec((BLOCK, 2 * ATT_KV), lambda i: (nidx(i), kvb))
    prev = pl.BlockSpec((BLOCK, 2 * ATT_KV), lambda i: (jnp.maximum(nidx(i) - 1, 0), kvb))
    meta = pl.BlockSpec((BLOCK, 2 * ATT_KV), lambda i: (0, kvb))
    table = pl.BlockSpec((1, ATT_HEADS, N_KEYS, BLOCK), lambda i: (jnp.minimum(nidx(i), 2), 0, 0, 0))
    sink = pl.BlockSpec((1, LANES), lambda i: (0, 0))
    return q_s, cur, prev, meta, table, sink


def _attn_fwd(qkv, tables, sinks):
    n = qkv.shape[0]
    nb = n // BLOCK
    q_s, cur_s, prev_s, meta_s, t_s, sink_s = _att_specs(nb, False)

    def body(q_ref, cur_ref, prev_ref, meta_ref, t_ref, sink_ref, o_ref):
        blk = pl.program_id(0)
        mask_t = _att_mask_t(blk, 1)
        k_dup, v_dup = _att_kv(meta_ref, prev_ref, cur_ref)
        v_dup_t = [v.T for v in v_dup]
        first = lax.broadcasted_iota(jnp.int32, (BLOCK, LANES), 1) < ATT_HEADDIM
        top = lax.broadcasted_iota(jnp.int32, (LANES, BLOCK), 0) < ATT_HEADDIM
        scale = ATT_HEADDIM ** -0.5
        for j in range(ATT_HEADS // 2):
            kh = 2 * j // ATT_GQ
            tile = slice(LANES * j, LANES * (j + 1))
            q_p = q_ref[:, tile]
            res = []
            for half, h in enumerate((2 * j, 2 * j + 1)):
                qm = jnp.where(first if half == 0 else jnp.logical_not(first), q_p, 0.0).astype(BF16)
                sink = sink_ref[0:1, h:h + 1]
                s_t = jnp.where(mask_t, _dot_nt(k_dup[kh], qm) * scale + t_ref[0, h], NEG)
                mx = jnp.maximum(jnp.max(s_t, axis=0, keepdims=True), sink)
                p_t = jnp.exp(s_t - mx)
                inv = 1.0 / (jnp.sum(p_t, axis=0, keepdims=True) + jnp.exp(sink - mx))
                res.append(_dot(v_dup_t[kh], (p_t * inv).astype(BF16)))
            o_ref[:, tile] = jnp.where(top, res[0], res[1]).T.astype(BF16)

    return pl.pallas_call(
        body, name="attn_fwd", grid=(nb,),
        in_specs=[q_s, cur_s, prev_s, meta_s, t_s, sink_s],
        out_specs=q_s,
        out_shape=jax.ShapeDtypeStruct((n, ATT_Q), BF16),
        compiler_params=_cparams("parallel"))(qkv, qkv, qkv, qkv, tables, sinks)


def _attn_bwd(datt, qkv, tables, sinks):
    n = qkv.shape[0]
    nb = n // BLOCK
    q_s, cur_s, prev_s, meta_s, t_s, sink_s = _att_specs(nb, True)
    dqkv_s = pl.BlockSpec((BLOCK, ATT_Q + 2 * ATT_KV), lambda i: (nb - 1 - i, 0))
    scale = ATT_HEADDIM ** -0.5

    def body(do_ref, q_ref, cur_ref, prev_ref, meta_ref, t_ref, sink_ref,
             dqkv_ref, dt_ref, dsink_ref, carry_scr, meta_scr):
        step = pl.program_id(0)
        blk = nb - 1 - step
        mask2 = _att_mask_t(blk, 2)
        k_dup, v_dup = _att_kv(meta_ref, prev_ref, cur_ref)
        k_dup_t = [k.T for k in k_dup]

        @pl.when(step == 0)
        def _():
            carry_scr[...] = jnp.zeros_like(carry_scr)
            meta_scr[...] = jnp.zeros_like(meta_scr)
            dsink_ref[...] = jnp.zeros_like(dsink_ref)

        @pl.when((step == 0) | (blk <= 1))
        def _():
            dt_ref[...] = jnp.zeros_like(dt_ref)

        first = lax.broadcasted_iota(jnp.int32, (BLOCK, LANES), 1) < ATT_HEADDIM
        top = lax.broadcasted_iota(jnp.int32, (LANES, BLOCK), 0) < ATT_HEADDIM
        first_k = lax.broadcasted_iota(jnp.int32, (N_KEYS, LANES), 1) < ATT_HEADDIM
        dsink = jnp.zeros((1, LANES), F32)
        dk_acc = [None] * ATT_KV_HEADS
        dv_acc = [None] * ATT_KV_HEADS
        for j in range(ATT_HEADS // 2):
            kh = 2 * j // ATT_GQ
            tile = slice(LANES * j, LANES * (j + 1))
            qm2 = _split_heads(q_ref[:, tile], first)
            dom2 = _split_heads(do_ref[:, tile], first)
            p_t, p_s = _att_probs_t(qm2, k_dup[kh], t_ref, j, mask2, sink_ref)
            dp_t = _dot_nt(v_dup[kh], dom2)
            delta = jnp.sum(p_t * dp_t, axis=0, keepdims=True)
            ds_t = p_t * (dp_t - delta)
            sink_terms = p_s * delta
            for half in range(2):
                cols = slice(BLOCK * half, BLOCK * (half + 1))
                dsink = _lane_put(dsink, -jnp.sum(sink_terms[:, cols], axis=1, keepdims=True), 2 * j + half)
                dt_ref[0, 2 * j + half] += ds_t[:, cols]
            ds_tb = ds_t.astype(BF16)
            dq_t = _dot(k_dup_t[kh], ds_tb)
            dqkv_ref[:, tile] = (jnp.where(top, dq_t[:, :BLOCK], dq_t[:, BLOCK:]).T * scale).astype(BF16)
            dk_part, dv_part = _dot(ds_tb, qm2), _dot(p_t.astype(BF16), dom2)
            dk_acc[kh] = dk_part if dk_acc[kh] is None else dk_acc[kh] + dk_part
            dv_acc[kh] = dv_part if dv_acc[kh] is None else dv_acc[kh] + dv_part
        dsink_ref[...] += dsink
        folded = [a + pltpu.roll(a, ATT_HEADDIM, 1) for a in dk_acc + dv_acc]
        dkv = jnp.concatenate([jnp.where(first_k, folded[0], folded[1]) * scale,
                               jnp.where(first_k, folded[2], folded[3])], axis=1)
        meta_scr[...] += dkv[:BLOCK, :]
        own = dkv[2 * BLOCK:, :] + carry_scr[...]
        carry_scr[...] = dkv[BLOCK:2 * BLOCK, :]

        @pl.when(blk > 0)
        def _():
            dqkv_ref[:, ATT_Q:] = own.astype(BF16)

        @pl.when(blk == 0)
        def _():
            dqkv_ref[:, ATT_Q:] = (own + meta_scr[...]).astype(BF16)

    return pl.pallas_call(
        body, name="attn_bwd", grid=(nb,),
        in_specs=[q_s, q_s, cur_s, prev_s, meta_s, t_s, sink_s],
        out_specs=[dqkv_s, t_s, sink_s],
        out_shape=[jax.ShapeDtypeStruct((n, ATT_Q + 2 * ATT_KV), BF16),
                   jax.ShapeDtypeStruct((3, ATT_HEADS, N_KEYS, BLOCK), F32),
                   jax.ShapeDtypeStruct((1, LANES), F32)],
        scratch_shapes=[pltpu.VMEM((BLOCK, 2 * ATT_KV), F32), pltpu.VMEM((BLOCK, 2 * ATT_KV), F32)],
        compiler_params=_cparams("arbitrary"))(datt, qkv, qkv, qkv, qkv, tables, sinks)


def _merge_out_fwd(gates, y_ssd, y_att, gate_b, w_out, h):
    n = gates.shape[0]
    tm = _row_tile(n, 416)

    def body(gs_ref, ga_ref, ys_ref, ya_ref, gb_ref, w_ref, h_ref, m_ref, o_ref):
        merged = (_sigmoid(gs_ref[...] + gb_ref[0:1, :]) * ys_ref[...]
                  + _sigmoid(ga_ref[...] + gb_ref[1:2, :]) * ya_ref[...]).astype(BF16)
        m_ref[...] = merged
        row = pl.program_id(0) * tm + lax.broadcasted_iota(jnp.int32, (tm, 1), 0)
        o_ref[...] = jnp.where(row >= PAD, _dot(merged, w_ref[...]), 0.0) + h_ref[...]

    row = pl.BlockSpec((tm, D_MODEL), lambda i: (i, 0))
    return pl.pallas_call(
        body, name="merge_out_fwd", grid=(n // tm,),
        in_specs=[row, pl.BlockSpec((tm, D_MODEL), lambda i: (i, 1)), row, row,
                  pl.BlockSpec((2, D_MODEL), lambda i: (0, 0)), pl.BlockSpec((D_MODEL, D_MODEL), lambda i: (0, 0)), row],
        out_specs=[row, row],
        out_shape=[jax.ShapeDtypeStruct((n, D_MODEL), BF16), jax.ShapeDtypeStruct((n, D_MODEL), F32)],
        compiler_params=_cparams("parallel"))(gates, gates, y_ssd, y_att, gate_b, w_out, h)


def _merge_out_bwd(dh, w_out, gates, y_ssd, y_att, gate_b):
    n = gates.shape[0]
    tm = _row_tile(n, 416)

    def body(dh_ref, w_ref, gs_ref, ga_ref, ys_ref, ya_ref, gb_ref, dys_ref, dya_ref, dg_ref, dgb_ref):
        i = pl.program_id(0)
        row = i * tm + lax.broadcasted_iota(jnp.int32, (tm, 1), 0)
        dmv = jnp.where(row >= PAD, _dot_nt(dh_ref[...].astype(BF16), w_ref[...]), 0.0)
        ss =_sigmoid(gs_ref[...] + gb_ref[0:1, :])
        sa = _sigmoid(ga_ref[...] + gb_ref[1:2, :])
        dys_ref[...] = (dmv * ss).astype(BF16)
        dya_ref[...] = (dmv * sa).astype(BF16)
        dgs = dmv * ys_ref[...] * ss * (1.0 - ss)
        dga = dmv * ya_ref[...] * sa * (1.0 - sa)
        dg_ref[:, :D_MODEL] = dgs.astype(BF16)
        dg_ref[:, D_MODEL:] = dga.astype(BF16)
        part = jnp.concatenate([jnp.sum(dgs, axis=0, keepdims=True), jnp.sum(dga, axis=0, keepdims=True)], axis=0)

        @pl.when(i == 0)
        def _():
            dgb_ref[...] = part

        @pl.when(i > 0)
        def _():
            dgb_ref[...] += part

    row = pl.BlockSpec((tm, D_MODEL), lambda i: (i, 0))
    gb = pl.BlockSpec((2, D_MODEL), lambda i: (0, 0))
    return pl.pallas_call(
        body, name="merge_out_bwd", grid=(n // tm,),
        in_specs=[row, pl.BlockSpec((D_MODEL, D_MODEL), lambda i: (0, 0)), row,
                  pl.BlockSpec((tm, D_MODEL), lambda i: (i, 1)), row, row, gb],
        out_specs=[row, row, pl.BlockSpec((tm, 2 * D_MODEL), lambda i: (i, 0)), gb],
        out_shape=[jax.ShapeDtypeStruct((n, D_MODEL), BF16), jax.ShapeDtypeStruct((n, D_MODEL), BF16),
                   jax.ShapeDtypeStruct((n, 2 * D_MODEL), BF16), jax.ShapeDtypeStruct((2, D_MODEL), F32)],
        compiler_params=_cparams("arbitrary"))(dh, w_out, gates, gates, y_ssd, y_att, gate_b)


def _col_move(srcs, outs, pieces, *, name):
    rows = srcs[0].shape[-2]
    tr = _row_tile(rows, 128)
    n_src = len(srcs)
    covered = [sum(p[6] for p in pieces if p[0] == o) for o in range(len(outs))]
    total = [int(np.prod(shp)) // rows for shp, _ in outs]

    def body(*refs):
        in_refs, out_refs = refs[:n_src], refs[n_src:]
        for o, ref in enumerate(out_refs):
            if covered[o] != total[o]:
                ref[...] = jnp.zeros_like(ref)
        for o, ol, oc, s, sl, sc, width in pieces:
            val = in_refs[s][:, sc:sc + width] if sl is None else in_refs[s][sl, :, sc:sc + width]
            val = val.astype(outs[o][1])
            if ol is None:
                out_refs[o][:, oc:oc + width] = val
            else:
                out_refs[o][ol, :, oc:oc + width] = val

    def spec(shape):
        if len(shape) == 2:
            return pl.BlockSpec((tr, shape[1]), lambda i: (i, 0))
        return pl.BlockSpec((shape[0], tr, shape[2]), lambda i: (0, i, 0))

    return pl.pallas_call(
        body, name=name, grid=(rows // tr,),
        in_specs=[spec(a.shape) for a in srcs], out_specs=[spec(shp) for shp, _ in outs],
        out_shape=[jax.ShapeDtypeStruct(shp, dt) for shp, dt in outs],
        compiler_params=_cparams("parallel"))(*srcs)


def _shard_pieces(seg_ranges, shard_w):
    out = []
    for seg, runs in enumerate(seg_ranges):
        for g0, width, s0 in runs:
            done = 0
            while done < width:
                dev, col = divmod(g0 + done, shard_w)
                take = min(width - done, shard_w - col)
                out.append((seg, s0 + done, dev, col, take))
                done += take
    return out


_CHIP_RELATIONS = [(1, 0, 0), (0, 1, 0), (1, 1, 0)]
N_CHIPS = 4


def _gather_two_level(arrays, *, name):
    outs = _run_plan(_gather_plan(arrays), name)
    return [o.reshape((N_DEV,) + a.shape) for o, a in zip(outs, arrays)]


class _CommPlan:
    def __init__(self, arrays, out_shape, scratch_shapes, phases):
        self.arrays, self.out_shape, self.scratch_shapes, self.phases = arrays, out_shape, scratch_shapes, phases


def _run_plan(plan, name):
    n_arr = len(plan.arrays)

    def body(*refs):
        ins, outs, sems = refs[:n_arr], refs[n_arr:2 * n_arr], refs[2 * n_arr:]
        for phase in plan.phases:
            phase(ins, outs, sems)

    any_spec = pl.BlockSpec(memory_space=pl.ANY)
    return pl.pallas_call(
        body, name=name, in_specs=[any_spec] * n_arr, out_specs=[any_spec] * n_arr, out_shape=plan.out_shape,
        scratch_shapes=plan.scratch_shapes)(*plan.arrays)


def _gather_plan(arrays):
    n_arr = len(arrays)
    n_chips = len(_CHIP_RELATIONS)
    n_pair = 1 + 2 * n_chips

    def where():
        x, y, c = lax.axis_index("x"), lax.axis_index("y"), lax.axis_index("c")
        return x, y, c, (x, y, 1 - c), [(x ^ dx, y ^ dy) for dx, dy, _ in _CHIP_RELATIONS]

    def copy(outs, sems, a, k, block, to, src=None):
        slot = outs[a].at[2 * block[0] + block[1], block[2]]
        return pltpu.make_async_remote_copy(
            src_ref=slot if src is None else src, dst_ref=slot, send_sem=sems[0].at[a * n_pair + k],
            recv_sem=sems[1].at[a * n_pair + k], device_id=to, device_id_type=MESH)

    def mine(ins, outs, sems, a, x, y, c):
        return pltpu.make_async_copy(ins[a], outs[a].at[2 * x + y, c], sems[2].at[a])

    def first_copies(ins, outs, sems, a, x, y, c, sibling, chips):
        return ([copy(outs, sems, a, 0, (x, y, c), sibling, src=ins[a])]
                + [copy(outs, sems, a, 1 + j, (x, y, c), (*chip, c), src=ins[a]) for j, chip in enumerate(chips)])

    def start(ins, outs, sems):
        x, y, c, sibling, chips = where()
        for a in range(n_arr):
            mine(ins, outs, sems, a, x, y, c).start()
            for cp in first_copies(ins, outs, sems, a, x, y, c, sibling, chips):
                cp.start()

    def pass_on(ins, outs, sems):
        x, y, c, sibling, chips = where()
        for j, chip in enumerate(chips):
            for a in range(n_arr):
                copy(outs, sems, a, 1 + j, (*chip, c), (x, y, c)).wait_recv()
                copy(outs, sems, a, 1 + n_chips + j, (*chip, c), sibling).start()

    def finish(ins, outs, sems):
        x, y, c, sibling, chips = where()
        for a in range(n_arr):
            copy(outs, sems, a, 0, (x, y, 1 - c), (x, y, c)).wait_recv()
            for j, chip in enumerate(chips):
                copy(outs, sems, a, 1 + n_chips + j, (*chip, 1 - c), (x, y, c)).wait_recv()
        for a in range(n_arr):
            for cp in first_copies(ins, outs, sems, a, x, y, c, sibling, chips):
                cp.wait_send()
            for j, chip in enumerate(chips):
                copy(outs, sems, a, 1 + n_chips + j, (*chip, c), sibling).wait_send()
            mine(ins, outs, sems, a, x, y, c).wait()

    return _CommPlan(
        arrays, [jax.ShapeDtypeStruct((N_CHIPS, 2) + a.shape, a.dtype) for a in arrays],
        [pltpu.SemaphoreType.DMA((n_arr * n_pair,)), pltpu.SemaphoreType.DMA((n_arr * n_pair,)),
         pltpu.SemaphoreType.DMA((n_arr,))],
        (start, pass_on, finish))


def _sibling_exchange(arrays, scatter, *, name):
    n_arr = len(arrays)

    def body(*refs):
        ins, outs = refs[:n_arr], refs[n_arr:2 * n_arr]
        send_sems, recv_sems = refs[2 * n_arr:]
        x, y, c = lax.axis_index("x"), lax.axis_index("y"), lax.axis_index("c")
        copies = []
        for a in range(n_arr):
            for q in range(N_CHIPS if scatter[a] else 1):
                src = ins[a].at[2 * q + 1 - c] if scatter[a] else ins[a]
                dst = outs[a].at[q] if scatter[a] else outs[a]
                cp = pltpu.make_async_remote_copy(
                    src_ref=src, dst_ref=dst, send_sem=send_sems.at[a * N_CHIPS + q],
                    recv_sem=recv_sems.at[a * N_CHIPS + q], device_id=(x, y, 1 - c), device_id_type=MESH)
                cp.start()
                copies.append(cp)
        for cp in copies:
            cp.wait_send()
        for cp in copies:
            cp.wait_recv()

    any_spec = pl.BlockSpec(memory_space=pl.ANY)
    return pl.pallas_call(
        body, name=name, in_specs=[any_spec] * n_arr, out_specs=[any_spec] * n_arr,
        out_shape=[jax.ShapeDtypeStruct(((N_CHIPS,) + a.shape[1:]) if s else a.shape, a.dtype)
                   for a, s in zip(arrays, scatter)],
        scratch_shapes=[pltpu.SemaphoreType.DMA((n_arr * N_CHIPS,)), pltpu.SemaphoreType.DMA((n_arr * N_CHIPS,))],
    )(*arrays)


def _add(a, b, *, name):
    rows, cols = a.shape
    tr = _row_tile(rows, 256)

    def body(a_ref, b_ref, o_ref):
        o_ref[...] = a_ref[...] + b_ref[...]

    blk = pl.BlockSpec((tr, cols), lambda i: (i, 0))
    return pl.pallas_call(body, name=name, grid=(rows // tr,), in_specs=[blk, blk], out_specs=blk,
                          out_shape=jax.ShapeDtypeStruct(a.shape, a.dtype), compiler_params=_cparams("parallel"))(a, b)


def _chip_exchange(arrays, scatter, *, name):
    return _run_plan(_chip_exchange_plan(arrays, scatter), name)


_ALL_RELATIONS = [(dx, dy, dc) for dx in (0, 1) for dy in (0, 1) for dc in (0, 1)][1:]


def _all_to_all_plan(arrays, scatter=None):
    n_arr = len(arrays)
    n_rel = len(_ALL_RELATIONS)
    scatter = scatter or [True] * n_arr

    def block(ins, a, p):
        return ins[a].at[p] if scatter[a] else ins[a]

    def local_copies(ins, outs, sems):
        me = 4 * lax.axis_index("x") + 2 * lax.axis_index("y") + lax.axis_index("c")
        return [pltpu.make_async_copy(block(ins, a, me), outs[a].at[me], sems[2].at[a]) for a in range(n_arr)]

    def remote_copies(ins, outs, sems, arrivals):
        x, y, c = lax.axis_index("x"), lax.axis_index("y"), lax.axis_index("c")
        me = 4 * x + 2 * y + c
        out = []
        for k, (dx, dy, dc) in enumerate(_ALL_RELATIONS):
            px, py, pc = x ^ dx, y ^ dy, c ^ dc
            peer = 4 * px + 2 * py + pc
            for a in range(n_arr):
                out.append(pltpu.make_async_remote_copy(
                    src_ref=block(ins, a, peer), dst_ref=outs[a].at[peer if arrivals else me],
                    send_sem=sems[0].at[a * n_rel + k], recv_sem=sems[1].at[a * n_rel + k],
                    device_id=(x, y, c) if arrivals else (px, py, pc), device_id_type=MESH))
        return out

    def start(ins, outs, sems):
        for cp in local_copies(ins, outs, sems) + remote_copies(ins, outs, sems, False):
            cp.start()

    def pass_on(ins, outs, sems):
        pass

    def finish(ins, outs, sems):
        for send in remote_copies(ins, outs, sems, False):
            send.wait_send()
        for arrival in remote_copies(ins, outs, sems, True):
            arrival.wait_recv()
        for cp in local_copies(ins, outs, sems):
            cp.wait()

    return _CommPlan(
        arrays, [jax.ShapeDtypeStruct(a.shape if s else (N_DEV,) + a.shape, a.dtype) for a, s in zip(arrays, scatter)],
        [pltpu.SemaphoreType.DMA((n_arr * n_rel,)), pltpu.SemaphoreType.DMA((n_arr * n_rel,)),
         pltpu.SemaphoreType.DMA((n_arr,))],
        (start, pass_on, finish))


def _chip_exchange_plan(arrays, scatter):
    n_arr = len(arrays)
    n_rel = len(_CHIP_RELATIONS)

    def local_copies(ins, outs, sems):
        me = 2 * lax.axis_index("x") + lax.axis_index("y")
        return [pltpu.make_async_copy(ins[a].at[me] if scatter[a] else ins[a], outs[a].at[me], sems[2].at[a])
                for a in range(n_arr)]

    def remote_copies(ins, outs, sems, arrivals):
        x, y, c = lax.axis_index("x"), lax.axis_index("y"), lax.axis_index("c")
        me = 2 * x + y
        out = []
        for k, (dx, dy, _) in enumerate(_CHIP_RELATIONS):
            px, py = x ^ dx, y ^ dy
            peer = 2 * px + py
            for a in range(n_arr):
                out.append(pltpu.make_async_remote_copy(
                    src_ref=ins[a].at[peer] if scatter[a] else ins[a], dst_ref=outs[a].at[peer if arrivals else me],
                    send_sem=sems[0].at[a * n_rel + k], recv_sem=sems[1].at[a * n_rel + k],
                    device_id=(x, y, c) if arrivals else (px, py, c), device_id_type=MESH))
        return out

    def start(ins, outs, sems):
        for cp in local_copies(ins, outs, sems) + remote_copies(ins, outs, sems, False):
            cp.start()

    def pass_on(ins, outs, sems):
        pass

    def finish(ins, outs, sems):
        for send in remote_copies(ins, outs, sems, False):
            send.wait_send()
        for arrival in remote_copies(ins, outs, sems, True):
            arrival.wait_recv()
        for cp in local_copies(ins, outs, sems):
            cp.wait()

    out_shape = [jax.ShapeDtypeStruct((N_CHIPS,) + (a.shape[1:] if s else a.shape), a.dtype)
                 for a, s in zip(arrays, scatter)]
    return _CommPlan(
        arrays, out_shape,
        [pltpu.SemaphoreType.DMA((n_arr * n_rel,)), pltpu.SemaphoreType.DMA((n_arr * n_rel,)),
         pltpu.SemaphoreType.DMA((n_arr,))],
        (start, pass_on, finish))


def _adamw(w, gslots, m, v, *, name):
    rows, cols = w.shape
    n_slots = gslots.shape[0]
    tr = _row_tile(rows, 128) if rows % 16 == 0 else rows

    def body(w_ref, g_ref, m_ref, v_ref, go_ref, d_ref, mo_ref, vo_ref):
        g = g_ref[0].astype(F32)
        for s in range(1, n_slots):
            g = g + g_ref[s].astype(F32)
        mn = ADAM_B1 * m_ref[...] + (1.0 - ADAM_B1) * g
        vn = ADAM_B2 * v_ref[...] + (1.0 - ADAM_B2) * (g * g)
        go_ref[...] = g
        mo_ref[...] = mn
        vo_ref[...] = vn
        m_hat = mn / (1.0 - ADAM_B1 ** ADAM_STEP)
        v_hat = vn / (1.0 - ADAM_B2 ** ADAM_STEP)
        d_ref[...] = -ADAM_LR * (m_hat / (jnp.sqrt(v_hat) + ADAM_EPS) + ADAM_WD * w_ref[...])

    blk = pl.BlockSpec((tr, cols), lambda i: (i, 0))
    shp = jax.ShapeDtypeStruct((rows, cols), F32)
    return pl.pallas_call(
        body, name=name, grid=(rows // tr,),
        in_specs=[blk, pl.BlockSpec((n_slots, tr, cols), lambda i: (0, i, 0)), blk, blk],
        out_specs=[blk] * 4, out_shape=[shp] * 4,
        compiler_params=_cparams("parallel"))(w, gslots, m, v)


_BIG = ("w_in", "w_ssd_branch", "w_attn_branch", "w_out", "w_ffn_in", "w_ffn_out")
_SMALL_SHARDED = ("meta_tokens", "ssd_conv_w", "gate_b", "ffn_conv_w")
_SMALL_REPLICATED = ("norm_mix_w", "ssd_conv_b", "ssd_dt_bias", "ssd_a_log", "ssd_d", "ssd_norm_w", "attn_sinks",
                     "rel_bias", "norm_ffn_w", "ffn_conv_b", "norm_final_w")
_WEIGHTS = ("meta_tokens", "norm_mix_w", "w_in", "ssd_conv_w", "ssd_conv_b", "ssd_dt_bias", "ssd_a_log", "ssd_d",
            "ssd_norm_w", "w_ssd_branch", "w_attn_branch", "attn_sinks", "rel_bias", "gate_b", "w_out", "norm_ffn_w",
            "w_ffn_in", "ffn_conv_w", "ffn_conv_b", "w_ffn_out", "norm_final_w")
_ROW_SHARDED = ("w_ssd_branch", "w_attn_branch", "w_out", "w_ffn_out")
_COL_SHARDED = ("w_in", "w_ffn_in", "meta_tokens", "ssd_conv_w", "gate_b", "ffn_conv_w")
_IN_SEGS = (("z", SSD_INNER), ("xbc", SSD_XBC), ("dt", SSD_HEADS), ("qkv", ATT_Q + 2 * ATT_KV), ("g", 2 * D_MODEL))


def _pack_rows(flat_parts, width, row_mult):
    flat = jnp.concatenate([p.reshape(-1) for p in flat_parts])
    pad = (-flat.shape[0]) % (width * row_mult)
    if pad:
        flat = jnp.concatenate([flat, jnp.zeros((pad,), flat.dtype)])
    return flat.reshape(-1, width)


def _unpack(flat, shapes):
    out, off = [], 0
    for shp in shapes:
        size = int(np.prod(shp))
        out.append(flat[off:off + size].reshape(shp))
        off += size
    return out


def _gather_full(stack, name, shard_shape):
    if name in _COL_SHARDED:
        return jnp.transpose(stack, (1, 0, 2)).reshape(shard_shape[0], N_DEV * shard_shape[1])
    return stack.reshape(N_DEV * shard_shape[0], shard_shape[1])


_IN_SEG_W = {"z": SSD_INNER, "xbc": SSD_XBC, "dt": DT_W, "qkv": ATT_Q + 2 * ATT_KV, "g": 2 * D_MODEL}
_IN_SHARD_W = (SSD_INNER + SSD_XBC + SSD_HEADS + ATT_Q + 2 * ATT_KV + 2 * D_MODEL) // N_DEV
_FFN_SHARD_W = 2 * D_FF // N_DEV


def _in_seg_runs():
    runs, off = [], 0
    for nm, width in _IN_SEGS:
        if nm == "dt":
            runs.append([(off + SSD_HPG * g, SSD_HPG, LANES * g) for g in range(SSD_GROUPS)])
        else:
            runs.append([(off, width, 0)])
        off += width
    return runs


def _w_in_to_segments(stack):
    pieces = [(seg, None, scol, 0, dev, col, w) for seg, scol, dev, col, w in _shard_pieces(_in_seg_runs(), _IN_SHARD_W)]
    outs = [((D_MODEL, _IN_SEG_W[nm]), stack.dtype) for nm, _ in _IN_SEGS]
    return dict(zip([nm for nm, _ in _IN_SEGS], _col_move([stack], outs, pieces, name="w_in_segments")))


def _segments_to_w_in_shards(seg_grads):
    pieces = [(0, dev, col, seg, None, scol, w) for seg, scol, dev, col, w in _shard_pieces(_in_seg_runs(), _IN_SHARD_W)]
    return _col_move(seg_grads, [((N_DEV, D_MODEL, _IN_SHARD_W), seg_grads[0].dtype)], pieces, name="g_w_in_shards")[0]


def _ffn_in_from_shards(stack):
    pieces = [(0, None, scol, 0, dev, col, w)
              for _, scol, dev, col, w in _shard_pieces([[(0, 2 * D_FF, 0)]], _FFN_SHARD_W)]
    return _col_move([stack], [((D_MODEL, 2 * D_FF), stack.dtype)], pieces, name="w_ffn_in_full")[0]


def _ffn_in_to_shards(g_up, g_gate):
    pieces = [(0, dev, col, seg, None, scol, w)
              for seg, scol, dev, col, w in _shard_pieces([[(0, D_FF, 0)], [(D_FF, D_FF, 0)]], _FFN_SHARD_W)]
    return _col_move([g_up, g_gate], [((N_DEV, D_MODEL, _FFN_SHARD_W), g_up.dtype)], pieces, name="g_w_ffn_in_shards")[0]


def _dt_spread(w_dt):
    k = w_dt.shape[0]
    w4 = w_dt.reshape(k, SSD_GROUPS, SSD_HPG)
    return jnp.pad(w4, ((0, 0), (0, 0), (0, LANES - SSD_HPG))).reshape(k, DT_W)


def _dt_gather(w_wide):
    k = w_wide.shape[0]
    return w_wide.reshape(k, SSD_GROUPS, LANES)[:, :, :SSD_HPG].reshape(k, SSD_HEADS)


class _LateExchanges:
    def __init__(self, two_d, shape2):
        self.two_d, self.shape2 = two_d, shape2
        self.early_grads_received = None
        self.w_in_grads_received = None

    def row_pack(self, tree):
        return jnp.concatenate([tree[k] for k in _ROW_SHARDED], axis=0)

    def late_weights_plan(self):
        return _gather_plan([self.two_d["w_ffn_in"].astype(BF16), self.row_pack(self.two_d).astype(BF16)])

    def late_weights(self, gathered):
        w_ffn_in_all, rows_all = [g.reshape((N_DEV,) + g.shape[2:]) for g in gathered]
        out = {"w_ffn_in": _ffn_in_from_shards(w_ffn_in_all)}
        off = 0
        for k in _ROW_SHARDED:
            r = self.shape2[k][0]
            out[k] = rows_all[:, off:off + r].reshape(N_DEV * r, D_MODEL)
            off += r
        return out

    def early_grads_plan(self, grads):
        rows_send = jnp.concatenate([grads[k].reshape(N_DEV, self.shape2[k][0], D_MODEL) for k in _ROW_SHARDED], axis=1)
        return _all_to_all_plan([_ffn_in_to_shards(*grads["w_ffn_in"]), rows_send])

    def w_in_grads_plan(self, seg_grads):
        return _all_to_all_plan([_segments_to_w_in_shards(seg_grads)])


def _local_step(x, target, w, exchanges=None):
    h0 = jnp.concatenate([jnp.zeros((PAD, D_MODEL), F32), w["meta_tokens"], x], axis=0)
    segs = w["in_segs"]

    dtb = _dt_spread(w["ssd_dt_bias"])
    alog = _dt_spread(w["ssd_a_log"])
    dskip_w = jnp.repeat(w["ssd_d"], SSD_HEADDIM, axis=1)
    sinks = jnp.pad(w["attn_sinks"], ((0, 0), (0, LANES - ATT_HEADS)))
    onehot_t = _onehot_t()
    tables = jnp.transpose(_bias_tables(w["rel_bias"].T, onehot_t).reshape(ATT_HEADS, 3, N_KEYS, BLOCK), (1, 0, 2, 3))

    u = _rms_fwd(h0, w["norm_mix_w"], name="rms_mix_fwd")
    z = _mm(u, segs["z"], name="in_z")
    xbc = _mm(u, segs["xbc"], name="in_xbc")
    dt_raw = _mm(u, segs["dt"], name="in_dt")
    qkv = _mm(u, segs["qkv"], name="in_qkv")
    gates = _mm(u, segs["g"], name="in_g")
    pre = _conv_fwd(xbc, w["ssd_conv_w"], w["ssd_conv_b"], name="ssd_conv_fwd")
    (y, yn, hsave), gathered = _ssd_fwd(pre, dt_raw, z, dtb, alog, dskip_w, w["ssd_norm_w"],
                                        side=None if exchanges is None else exchanges.late_weights_plan())
    if exchanges is not None:
        w = {**w, **exchanges.late_weights(gathered)}
    w_ffn_up, w_ffn_gate = w["w_ffn_in"][:, :D_FF], w["w_ffn_in"][:, D_FF:]
    y_ssd = _mm(yn, w["w_ssd_branch"], name="ssd_out")
    att = _attn_fwd(qkv, tables, sinks)
    y_att = _mm(att, w["w_attn_branch"], name="att_out")
    merged, h1 = _merge_out_fwd(gates, y_ssd, y_att, w["gate_b"], w["w_out"], h0)
    u2 = _rms_fwd(h1, w["norm_ffn_w"], name="rms_ffn_fwd")
    hid_raw = _mm(u2, w["w_ffn_in"], name="ffn_in")
    hid_up, hid_gate, act = _ffn_act_fwd(hid_raw, w["ffn_conv_w"], w["ffn_conv_b"])
    h2 = _mm(act, w["w_ffn_out"], c=h1, mask=True, name="ffn_out")
    dh2, loss_row, g_norm_final = _final_loss(h2, w["norm_final_w"], target)

    grads = {"norm_final_w": g_norm_final}
    dact = _mm(dh2, w["w_ffn_out"], tb=True, mask=True, name="d_act")
    grads["w_ffn_out"] = _mm(act, dh2, ta=True, mask=True, out_dtype=BF16, name="g_w_ffn_out")
    dx_up, dx_gate, dcw_up, dcw_gate, dcb_up, dcb_gate = _ffn_act_bwd(dact, hid_up, hid_gate, hid_raw, w["ffn_conv_w"])
    grads["ffn_conv_w"] = jnp.concatenate([dcw_up, dcw_gate], axis=1)
    grads["ffn_conv_b"] = jnp.concatenate([dcb_up, dcb_gate], axis=1)
    (dh1, grads["norm_ffn_w"]), _ = _mm_rms_bwd([(dx_up, w_ffn_up), (dx_gate, w_ffn_gate)], h1, w["norm_ffn_w"], dh2,
                                                name="d_u2_rms_bwd")
    grads["w_ffn_in"] = (_mm(u2, dx_up, ta=True, out_dtype=BF16, name="g_w_ffn_up"),
                         _mm(u2, dx_gate, ta=True, out_dtype=BF16, name="g_w_ffn_gate"))

    grads["w_out"] = _mm(merged, dh1, ta=True, mask=True, out_dtype=BF16, name="g_w_out")
    dy_ssd, dy_att, dgates, grads["gate_b"] = _merge_out_bwd(dh1, w["w_out"], gates, y_ssd, y_att, w["gate_b"])
    dyn = _mm(dy_ssd, w["w_ssd_branch"], tb=True, name="d_yn")
    grads["w_ssd_branch"] = _mm(yn, dy_ssd, ta=True, out_dtype=BF16, name="g_w_ssd")
    datt = _mm(dy_att, w["w_attn_branch"], tb=True, name="d_att")
    grads["w_attn_branch"] = _mm(att, dy_att, ta=True, out_dtype=BF16, name="g_w_att")
    (dz, dpxs, dpb, dpc, ddt, grads["ssd_norm_w"], g_dtb, g_alog, g_dskip), received = _ssd_bwd(
        dyn, y, z, pre, dt_raw, hsave, dtb, alog, dskip_w, w["ssd_norm_w"],
        side=None if exchanges is None else exchanges.early_grads_plan(grads))
    if exchanges is not None:
        exchanges.early_grads_received = received
    grads["ssd_dt_bias"] = _dt_gather(g_dtb)
    grads["ssd_a_log"] = _dt_gather(g_alog)
    grads["ssd_d"] = _dt_gather(g_dskip)
    conv_g = _conv_bwd(dpxs, xbc, w["ssd_conv_w"], name="ssd_conv_bwd_x")
    conv_g = _conv_bwd(dpb, xbc, w["ssd_conv_w"], name="ssd_conv_bwd_b", col0=SSD_INNER, into=conv_g)
    dxbc, grads["ssd_conv_w"], grads["ssd_conv_b"] = _conv_bwd(
        dpc, xbc, w["ssd_conv_w"], name="ssd_conv_bwd_c", col0=SSD_INNER + SSD_BC, into=conv_g)
    dqkv, d_tables, d_sinks = _attn_bwd(datt, qkv, tables, sinks)
    grads["attn_sinks"] = d_sinks[:, :ATT_HEADS]
    dtab = jnp.transpose(d_tables, (1, 0, 2, 3)).reshape(ATT_HEADS, NT_ALL)
    grads["rel_bias"] = _bias_grad(dtab, onehot_t).T
    dsegs = {"z": dz, "xbc": dxbc, "dt": ddt, "qkv": dqkv, "g": dgates}
    grads["in_segs"] = [_mm(u, dsegs[nm], ta=True, out_dtype=BF16, name="g_w_in_" + nm) for nm, _ in _IN_SEGS]
    (dh0, grads["norm_mix_w"]), received = _mm_rms_bwd(
        [(dsegs[nm], segs[nm]) for nm, _ in _IN_SEGS], h0, w["norm_mix_w"], dh1, name="d_u_rms_bwd",
        side=None if exchanges is None else exchanges.w_in_grads_plan(grads["in_segs"]))
    if exchanges is not None:
        exchanges.w_in_grads_received = received[0]
    grads["meta_tokens"] = dh0[PAD:BLOCK]
    return loss_row[0, 0], dh0[BLOCK:], grads


def kernel(x, meta_tokens, norm_mix_w, w_in, ssd_conv_w, ssd_conv_b, ssd_dt_bias, ssd_a_log, ssd_d, ssd_norm_w, w_ssd_branch, w_attn_branch, attn_sinks, rel_bias, gate_b, w_out, norm_ffn_w, w_ffn_in, ffn_conv_w, ffn_conv_b, w_ffn_out, norm_final_w, loss_target, m_meta_tokens, m_norm_mix_w, m_w_in, m_ssd_conv_w, m_ssd_conv_b, m_ssd_dt_bias, m_ssd_a_log, m_ssd_d, m_ssd_norm_w, m_w_ssd_branch, m_w_attn_branch, m_attn_sinks, m_rel_bias, m_gate_b, m_w_out, m_norm_ffn_w, m_w_ffn_in, m_ffn_conv_w, m_ffn_conv_b, m_w_ffn_out, m_norm_final_w, v_meta_tokens, v_norm_mix_w, v_w_in, v_ssd_conv_w, v_ssd_conv_b, v_ssd_dt_bias, v_ssd_a_log, v_ssd_d, v_ssd_norm_w, v_w_ssd_branch, v_w_attn_branch, v_attn_sinks, v_rel_bias, v_gate_b, v_w_out, v_norm_ffn_w, v_w_ffn_in, v_ffn_conv_w, v_ffn_conv_b, v_w_ffn_out, v_norm_final_w):
    shard = dict(meta_tokens=meta_tokens, norm_mix_w=norm_mix_w, w_in=w_in, ssd_conv_w=ssd_conv_w,
                 ssd_conv_b=ssd_conv_b, ssd_dt_bias=ssd_dt_bias, ssd_a_log=ssd_a_log, ssd_d=ssd_d,
                 ssd_norm_w=ssd_norm_w, w_ssd_branch=w_ssd_branch, w_attn_branch=w_attn_branch,
                 attn_sinks=attn_sinks, rel_bias=rel_bias, gate_b=gate_b, w_out=w_out, norm_ffn_w=norm_ffn_w,
                 w_ffn_in=w_ffn_in, ffn_conv_w=ffn_conv_w, ffn_conv_b=ffn_conv_b, w_ffn_out=w_ffn_out,
                 norm_final_w=norm_final_w)
    mom_m = dict(zip(_WEIGHTS, (m_meta_tokens, m_norm_mix_w, m_w_in, m_ssd_conv_w, m_ssd_conv_b, m_ssd_dt_bias,
                                m_ssd_a_log, m_ssd_d, m_ssd_norm_w, m_w_ssd_branch, m_w_attn_branch, m_attn_sinks,
                                m_rel_bias, m_gate_b, m_w_out, m_norm_ffn_w, m_w_ffn_in, m_ffn_conv_w, m_ffn_conv_b,
                                m_w_ffn_out, m_norm_final_w)))
    mom_v = dict(zip(_WEIGHTS, (v_meta_tokens, v_norm_mix_w, v_w_in, v_ssd_conv_w, v_ssd_conv_b, v_ssd_dt_bias,
                                v_ssd_a_log, v_ssd_d, v_ssd_norm_w, v_w_ssd_branch, v_w_attn_branch, v_attn_sinks,
                                v_rel_bias, v_gate_b, v_w_out, v_norm_ffn_w, v_w_ffn_in, v_ffn_conv_w, v_ffn_conv_b,
                                v_w_ffn_out, v_norm_final_w)))
    orig_shape = {k: a.shape for k, a in shard.items()}
    two_d = {k: a.reshape(a.shape[-2:]) if a.ndim >= 2 else a.reshape(1, -1) for k, a in shard.items()}
    shape2 = {k: a.shape for k, a in two_d.items()}

    def as2d(tree):
        return {k: tree[k].reshape(shape2[k]) for k in _WEIGHTS}

    mom_m, mom_v = as2d(mom_m), as2d(mom_v)

    exchanges = _LateExchanges(two_d, shape2)
    row_pack = exchanges.row_pack
    small_pack = _pack_rows([two_d[k] for k in _SMALL_SHARDED], LANES, SMALL_ROW_MULT)
    w_in_all, small_all = _gather_two_level([two_d["w_in"].astype(BF16), small_pack], name="gather_weights")
    full = {k: two_d[k] for k in _SMALL_REPLICATED}
    full["in_segs"] = _w_in_to_segments(w_in_all)
    small_flat = small_all.reshape(N_DEV, -1)
    off = 0
    for k in _SMALL_SHARDED:
        size = int(np.prod(shape2[k]))
        full[k] = _gather_full(small_flat[:, off:off + size].reshape((N_DEV,) + shape2[k]), k, shape2[k])
        off += size

    loss_local, grad_x, grads = _local_step(x[0], loss_target[0], full, exchanges)

    small_names = _SMALL_SHARDED + _SMALL_REPLICATED
    small_send = _pack_rows([grads[k] for k in small_names] + [loss_local.reshape(1)], LANES, SMALL_ROW_MULT)
    small_recv, = _run_plan(_all_to_all_plan([small_send], [False]), "exchange_small_grads")
    in_recv = exchanges.w_in_grads_received
    ffn_recv, rows_recv = exchanges.early_grads_received

    big = {"w_in": _adamw(two_d["w_in"], in_recv, mom_m["w_in"], mom_v["w_in"], name="adamw_w_in"),
           "w_ffn_in": _adamw(two_d["w_ffn_in"], ffn_recv, mom_m["w_ffn_in"], mom_v["w_ffn_in"], name="adamw_w_ffn_in")}
    rows_out = _adamw(row_pack(two_d), rows_recv, row_pack(mom_m), row_pack(mom_v), name="adamw_rows")
    off = 0
    for k in _ROW_SHARDED:
        r = shape2[k][0]
        big[k] = [a[off:off + r] for a in rows_out]
        off += r
    me =4 * lax.axis_index("x") + 2 * lax.axis_index("y") + lax.axis_index("c")
    small_full_shapes = [grads[k].shape for k in small_names]
    n_small = sum(int(np.prod(s)) for s in small_full_shapes)

    def packed_small(tree):
        parts = []
        for k in small_names:
            a = tree[k]
            if k in _SMALL_SHARDED:
                fullw = jnp.zeros(grads[k].shape, F32)
                a = lax.dynamic_update_slice(fullw, a, (0, me * a.shape[1]))
            parts.append(a)
        return _pack_rows(parts + [jnp.zeros((1,), F32)], LANES, SMALL_ROW_MULT)

    g_small, d_small, m_small, v_small = _adamw(packed_small(two_d), small_recv, packed_small(mom_m),
                                                packed_small(mom_v), name="adamw_small")

    def unpack_all(which, small):
        out = {k: big[k][which] for k in _BIG}
        flat = small.reshape(-1)
        for k, a in zip(small_names, _unpack(flat, small_full_shapes)):
            if k in _SMALL_SHARDED:
                a = lax.dynamic_slice(a, (0, me * shape2[k][1]), shape2[k])
            out[k] = a
        return out, flat[n_small]

    g_all, loss = unpack_all(0, g_small)
    d_all, _ = unpack_all(1, d_small)
    m_all, _ = unpack_all(2, m_small)
    v_all, _ = unpack_all(3, v_small)

    def final(tree):
        return [tree[k].reshape(orig_shape[k]) for k in _WEIGHTS]

    return (loss, grad_x[None], *final(g_all), *final(d_all), *final(m_all), *final(v_all))
```

```python
import functools
import math

import numpy as np
import jax
import jax.numpy as jnp
from jax import lax
from jax.experimental import pallas as pl
from jax.experimental.pallas import tpu as pltpu

F32 = jnp.float32
BF16 = jnp.bfloat16
HIGHEST = lax.Precision.HIGHEST

D_MODEL = 1024
N_META = 16
BLOCK = 128
PAD = BLOCK - N_META
EPS = 1e-6
NEG = -1e30
SSD_INNER = 2 * D_MODEL
SSD_HEADDIM = 64
SSD_HEADS = SSD_INNER // SSD_HEADDIM
SSD_GROUPS = 4
SSD_HPG = SSD_HEADS // SSD_GROUPS
SSD_STATE = 128
SSD_CONV = 4
SSD_GW = SSD_HPG * SSD_HEADDIM
SSD_BC = SSD_GROUPS * SSD_STATE
SSD_XBC = SSD_INNER + 2 * SSD_BC
ATT_HEADS = 16
ATT_KV_HEADS = 2
ATT_HEADDIM = 64
ATT_GQ = ATT_HEADS // ATT_KV_HEADS
ATT_Q = ATT_HEADS * ATT_HEADDIM
ATT_KV = ATT_KV_HEADS * ATT_HEADDIM
REL_BUCKETS = 32
REL_MAX_DIST = 128
D_FF = 2816
FFN_CONV = 3
ADAM_LR = 0.001
ADAM_B1 = 0.9
ADAM_B2 = 0.999
ADAM_EPS = 1e-08
ADAM_WD = 0.01
ADAM_STEP = 10

N_DEV = 8
LANES = 128
SUBLANES = 8
DT_W = SSD_GROUPS * LANES
VMEM_LIMIT_BYTES = 56 * 1024 * 1024
MESH = pl.DeviceIdType.MESH

SMALL_ROW_MULT = 16

N_KEYS = 3 * BLOCK
NT_ALL = 3 * N_KEYS * BLOCK
NT_TILE = 8192


def _cparams(*sem):
    return pltpu.CompilerParams(dimension_semantics=sem, vmem_limit_bytes=VMEM_LIMIT_BYTES)


def _row_tile(n, cap):
    best = None
    for t in range(16, min(n, cap) + 1, 16):
        if n % t == 0:
            best = t
    return best or n


def _col_tile(n, cap):
    for t in (1408, 1280, 1024, 768, 640, 512, 384, 256, 128):
        if t <= cap and n % t == 0:
            return t
    return n


def _sigmoid(x):
    return 0.5 * jnp.tanh(0.5 * x) + 0.5


def _silu(x):
    return x * _sigmoid(x)


def _softplus(x):
    return jnp.maximum(x, 0.0) + jnp.log(1.0 + jnp.exp(-jnp.abs(x)))


def _dot_nt(a, b):
    return lax.dot_general(a, b, (((1,), (1,)), ((), ())), preferred_element_type=F32)


def _dot_tn(a, b):
    return lax.dot_general(a, b, (((0,), (0,)), ((), ())), preferred_element_type=F32)


def _dot(a, b):
    return jnp.dot(a, b, preferred_element_type=F32)


def _bf16_terms(x, terms):
    out, rest = [], x
    for _ in range(terms):
        part = rest.astype(BF16)
        out.append(part)
        rest = rest - part.astype(F32)
    return out


def _dot_sel(x, sel, terms=3):
    return sum(_dot(part, sel) for part in _bf16_terms(x, terms))


def _sel_dot(sel, x, terms=3):
    return sum(_dot(sel, part) for part in _bf16_terms(x, terms))


def _sum_all(x):
    return jnp.sum(jnp.sum(x, axis=1, keepdims=True), axis=0, keepdims=True)


MM_ROW_CAPS = (2080, 1664, 832, 416)
MM_COL_CAP = 1408
MM_VMEM_BUDGET = 44 * 1024 * 1024


def _mm_tiles(rows, cols, vmem_bytes):
    col_cands = [t for t in (2048, 1536, 1408, 1280, 1024, 768, 640, 512, 384, 256, 128) if cols % t == 0]
    if cols <= 2 * MM_COL_CAP:
        col_cands.append(cols)
    best = None
    for cap in MM_ROW_CAPS:
        tr = _row_tile(rows, cap)
        for tc in col_cands:
            if vmem_bytes(tr, tc) <= MM_VMEM_BUDGET and (best is None or tr * tc > best[0] * best[1]):
                best = (tr, tc)
    assert best is not None, (rows, cols)
    return best


def _mm(a, b, *, name, ta=False, tb=False, c=None, mask=False, out_dtype=F32):
    if not ta:
        m, k = a.shape
        n = b.shape[0] if tb else b.shape[1]
        tm, tn = _mm_tiles(m, n, lambda t_m, t_n: 2 * (t_m * k * a.dtype.itemsize + k * t_n * b.dtype.itemsize
                                                       + t_m * t_n * (jnp.dtype(out_dtype).itemsize
                                                                      + (0 if c is None else c.dtype.itemsize)))
                           + 4 * t_m * t_n)

        def body(*refs):
            if c is None:
                a_ref, b_ref, o_ref = refs
            else:
                a_ref, b_ref, c_ref, o_ref = refs
            acc = (_dot_nt if tb else _dot)(a_ref[...].astype(BF16), b_ref[...].astype(BF16))
            if mask:
                row = pl.program_id(0) * tm + lax.broadcasted_iota(jnp.int32, (tm, 1), 0)
                acc = jnp.where(row >= PAD, acc, 0.0)
            if c is not None:
                acc = acc + c_ref[...]
            o_ref[...] = acc.astype(out_dtype)

        b_spec = pl.BlockSpec((tn, k), lambda i, j: (j, 0)) if tb else pl.BlockSpec((k, tn), lambda i, j: (0, j))
        in_specs = [pl.BlockSpec((tm, k), lambda i, j: (i, 0)), b_spec]
        args = [a, b]
        if c is not None:
            in_specs.append(pl.BlockSpec((tm, tn), lambda i, j: (i, j)))
            args.append(c)
        return pl.pallas_call(
            body, name=name, grid=(m // tm, n // tn), in_specs=in_specs,
            out_specs=pl.BlockSpec((tm, tn), lambda i, j: (i, j)),
            out_shape=jax.ShapeDtypeStruct((m, n), out_dtype),
            compiler_params=_cparams("parallel", "parallel"))(*args)

    kc, m = a.shape
    n = b.shape[1]
    tm = _col_tile(m, MM_COL_CAP)
    tk, tn = _mm_tiles(kc, n, lambda t_k, t_n: 2 * (t_k * tm * a.dtype.itemsize + t_k * t_n * b.dtype.itemsize
                                                    + tm * t_n * jnp.dtype(out_dtype).itemsize) + 8 * tm * t_n)

    n_k = kc // tk

    def body_t(a_ref, b_ref, o_ref, acc_ref):
        kk = pl.program_id(2)
        bb = b_ref[...]
        if mask:
            row = kk * tk + lax.broadcasted_iota(jnp.int32, (tk, 1), 0)
            bb = jnp.where(row >= PAD, bb, jnp.zeros_like(bb))
        p = _dot_tn(a_ref[...].astype(BF16), bb.astype(BF16))

        @pl.when(kk == 0)
        def _():
            acc_ref[...] = p

        @pl.when(kk > 0)
        def _():
            acc_ref[...] += p

        @pl.when(kk == n_k - 1)
        def _():
            o_ref[...] = acc_ref[...].astype(out_dtype)

    return pl.pallas_call(
        body_t, name=name, grid=(m // tm, n // tn, n_k),
        in_specs=[pl.BlockSpec((tk, tm), lambda i, j, kk: (kk, i)), pl.BlockSpec((tk, tn), lambda i, j, kk: (kk, j))],
        out_specs=pl.BlockSpec((tm, tn), lambda i, j, kk: (i, j)),
        out_shape=jax.ShapeDtypeStruct((m, n), out_dtype),
        scratch_shapes=[pltpu.VMEM((tm, tn), F32)],
        compiler_params=_cparams("parallel", "parallel", "arbitrary"))(a, b)


def _mm_rms_bwd(pairs, x, w, dres, *, name, side=None):
    m, d = x.shape
    tm = _row_tile(m, 416)
    n_pairs = len(pairs)

    def body(*refs):
        a_refs, b_refs = refs[:n_pairs], refs[n_pairs:2 * n_pairs]
        x_ref, w_ref, dres_ref, dx_ref, dw_ref = refs[2 * n_pairs:]
        i = pl.program_id(0)
        dyv = None
        for a_ref, b_ref in zip(a_refs, b_refs):
            term = _dot_nt(a_ref[...].astype(BF16), b_ref[...])
            dyv = term if dyv is None else dyv + term
        xv = x_ref[...]
        r = lax.rsqrt(jnp.mean(xv * xv, axis=-1, keepdims=True) + EPS)
        xh = xv * r
        g = dyv * w_ref[...]
        dx_ref[...] = r * (g - xh * jnp.mean(g * xh, axis=-1, keepdims=True)) + dres_ref[...]
        part = jnp.sum(dyv * xh, axis=0, keepdims=True)

        @pl.when(i == 0)
        def _():
            dw_ref[...] = part

        @pl.when(i > 0)
        def _():
            dw_ref[...] += part

    row = pl.BlockSpec((tm, d), lambda i: (i, 0))
    vec = pl.BlockSpec((1, d), lambda i: (0, 0))
    in_specs = ([pl.BlockSpec((tm, a.shape[1]), lambda i: (i, 0)) for a, _ in pairs]
                + [pl.BlockSpec(b.shape, lambda i: (0, 0), pipeline_mode=pl.Buffered(1)) for _, b in pairs]
                + [row, vec, row])
    return _call_with_side(
        body, side, name=name, grid=(m // tm,), in_specs=in_specs, out_specs=[row, vec],
        out_shape=[jax.ShapeDtypeStruct((m, d), F32), jax.ShapeDtypeStruct((1, d), F32)], scratch_shapes=[],
        args=[a for a, _ in pairs] + [b for _, b in pairs] + [x, w, dres], semantics=("arbitrary",))


def _rms_fwd(h, w, *, name):
    n, d = h.shape
    tm = _row_tile(n, 832)

    def body(h_ref, w_ref, o_ref):
        x = h_ref[...]
        r = lax.rsqrt(jnp.mean(x * x, axis=-1, keepdims=True) + EPS)
        o_ref[...] = (x * r * w_ref[...]).astype(BF16)

    return pl.pallas_call(
        body, name=name, grid=(n // tm,),
        in_specs=[pl.BlockSpec((tm, d), lambda i: (i, 0)), pl.BlockSpec((1, d), lambda i: (0, 0))],
        out_specs=pl.BlockSpec((tm, d), lambda i: (i, 0)),
        out_shape=jax.ShapeDtypeStruct((n, d), BF16),
        compiler_params=_cparams("parallel"))(h, w)


def _final_loss(h, w, target):
    n, d = h.shape
    nb = n // BLOCK

    def body(h_ref, w_ref, t_ref, dh_ref, loss_ref, dw_ref):
        i = pl.program_id(0)
        xv = h_ref[...]
        r = lax.rsqrt(jnp.mean(xv * xv, axis=-1, keepdims=True) + EPS)
        xh = xv * r
        wv = w_ref[...]
        err = jnp.where(i >= 1, xh * wv - t_ref[...], 0.0)
        dyv = err * (1.0 / d)
        g = dyv * wv
        dh_ref[...] = r * (g - xh * jnp.mean(g * xh, axis=-1, keepdims=True))
        lpart = jnp.broadcast_to(0.5 * _sum_all(err * err) * (1.0 / d), (1, LANES))
        wpart = jnp.sum(dyv * xh, axis=0, keepdims=True)

        @pl.when(i == 0)
        def _():
            loss_ref[...] = lpart
            dw_ref[...] = wpart

        @pl.when(i > 0)
        def _():
            loss_ref[...] += lpart
            dw_ref[...] += wpart

    row = pl.BlockSpec((BLOCK, d), lambda i: (i, 0))
    vec = pl.BlockSpec((1, d), lambda i: (0, 0))
    return pl.pallas_call(
        body, name="final_loss", grid=(nb,),
        in_specs=[row, vec, pl.BlockSpec((BLOCK, d), lambda i: (jnp.maximum(i - 1, 0), 0))],
        out_specs=[row, pl.BlockSpec((1, LANES), lambda i: (0, 0)), vec],
        out_shape=[jax.ShapeDtypeStruct((n, d), F32), jax.ShapeDtypeStruct((1, LANES), F32),
                   jax.ShapeDtypeStruct((1, d), F32)],
        compiler_params=_cparams("arbitrary"))(h, w, target)


def _main_spec(tm, cb, off=0):
    return pl.BlockSpec((tm, cb), lambda j, i: (i, j + off))


def _prev_spec(tm, cb, off=0):
    r8 = tm // SUBLANES
    return pl.BlockSpec((SUBLANES, cb), lambda j, i: (jnp.maximum(i * r8 - 1, 0), j + off))


def _next_spec(tm, cb, n_rows, off=0):
    r8 = tm // SUBLANES
    last = n_rows // SUBLANES - 1
    return pl.BlockSpec((SUBLANES, cb), lambda j, i: (jnp.minimum((i + 1) * r8, last), j + off))


def _with_prev(prev_ref, main_ref, i):
    prev = jnp.where(i > 0, prev_ref[...], 0.0)
    return jnp.concatenate([prev, main_ref[...]], axis=0)


def _with_next(main, nxt, i, n_tiles):
    return jnp.concatenate([main, jnp.where(i < n_tiles - 1, nxt, 0.0)], axis=0)


def _back(xx, s, tm):
    if s == 0:
        return xx[SUBLANES:SUBLANES + tm]
    return pltpu.roll(xx, s, 0)[SUBLANES:SUBLANES + tm]


def _ahead(xx, s, tm):
    if s == 0:
        return xx[:tm]
    return pltpu.roll(xx, tm + SUBLANES - s, 0)[:tm]


def _conv_fwd(x, w, b, *, name):
    n, cdim = x.shape
    kw = w.shape[0]
    tm = _row_tile(n, 832)
    cb = _col_tile(cdim, 512)

    def body(xp_ref, x_ref, w_ref, b_ref, o_ref):
        xx = _with_prev(xp_ref, x_ref, pl.program_id(1))
        acc = jnp.broadcast_to(b_ref[...], (tm, cb))
        for k in range(kw):
            acc = acc + w_ref[k:k + 1, :] * _back(xx, kw - 1 - k, tm)
        o_ref[...] = acc

    return pl.pallas_call(
        body, name=name, grid=(cdim // cb, n // tm),
        in_specs=[_prev_spec(tm, cb), _main_spec(tm, cb), pl.BlockSpec((kw, cb), lambda j, i: (0, j)),
                  pl.BlockSpec((1, cb), lambda j, i: (0, j))],
        out_specs=_main_spec(tm, cb),
        out_shape=jax.ShapeDtypeStruct((n, cdim), F32),
        compiler_params=_cparams("parallel", "parallel"))(x, x, w, b)


def _conv_bwd_core(dpre_ext, x, w_ref, kw, tm):
    dx = None
    dws = []
    for k in range(kw):
        shifted = _ahead(dpre_ext, kw - 1 - k, tm)
        term = w_ref[k:k + 1, :] * shifted
        dx = term if dx is None else dx + term
        dws.append(jnp.sum(shifted * x, axis=0, keepdims=True))
    return dx, dws, jnp.sum(dpre_ext[:tm], axis=0, keepdims=True)


def _acc_rows(i, dw_ref, db_ref, dws, db):
    @pl.when(i == 0)
    def _():
        for k, v in enumerate(dws):
            dw_ref[k:k + 1, :] = v
        db_ref[...] = db

    @pl.when(i > 0)
    def _():
        for k, v in enumerate(dws):
            dw_ref[k:k + 1, :] += v
        db_ref[...] += db


def _conv_bwd(dpre, x, w, *, name, col0=0, into=None):
    n, cdim = x.shape
    kw = w.shape[0]
    tm = _row_tile(n, 832)
    cb = _col_tile(cdim, 512)
    nt = n // tm
    off = col0 // cb
    n_alias = 0 if into is None else 3

    def body(d_ref, dn_ref, x_ref, w_ref, *rest):
        dx_ref, dw_ref, db_ref = rest[n_alias:]
        i = pl.program_id(1)
        dpre_ext = _with_next(d_ref[...], dn_ref[...], i, nt)
        dx, dws, db = _conv_bwd_core(dpre_ext, x_ref[...], w_ref, kw, tm)
        dx_ref[...] = dx.astype(BF16)
        _acc_rows(i, dw_ref, db_ref, dws, db)

    wspec = pl.BlockSpec((kw, cb), lambda j, i: (0, j + off))
    bspec = pl.BlockSpec((1, cb), lambda j, i: (0, j + off))
    return pl.pallas_call(
        body, name=name, grid=(dpre.shape[1] // cb, nt),
        in_specs=[_main_spec(tm, cb), _next_spec(tm, cb, n), _main_spec(tm, cb, off), wspec]
        + [pl.BlockSpec(memory_space=pl.ANY)] * n_alias,
        out_specs=[_main_spec(tm, cb, off), wspec, bspec],
        out_shape=[jax.ShapeDtypeStruct((n, cdim), BF16), jax.ShapeDtypeStruct((kw, cdim), F32),
                   jax.ShapeDtypeStruct((1, cdim), F32)],
        input_output_aliases={4 + k: k for k in range(n_alias)},
        compiler_params=_cparams("parallel", "arbitrary"))(dpre, dpre, x, w, *(into or ()))


def _ffn_act_fwd(x, w, b):
    n = x.shape[0]
    kw = w.shape[0]
    tm = _row_tile(n, 832)
    cb = _col_tile(D_FF, 256)
    nc = D_FF // cb

    def body(xpu_ref, xu_ref, xpg_ref, xg_ref, wu_ref, wg_ref, bu_ref, bg_ref, hu_ref, hg_ref, act_ref):
        i = pl.program_id(1)
        outs = []
        for xp_ref, x_ref, w_ref, b_ref in ((xpu_ref, xu_ref, wu_ref, bu_ref), (xpg_ref, xg_ref, wg_ref, bg_ref)):
            xx = _with_prev(xp_ref, x_ref, i)
            acc = jnp.broadcast_to(b_ref[...], (tm, cb))
            for k in range(kw):
                acc = acc + w_ref[k:k + 1, :] * _back(xx, kw - 1 - k, tm)
            outs.append(acc)
        hu_ref[...] = outs[0]
        hg_ref[...] = outs[1]
        act_ref[...] = (_silu(outs[1]) * outs[0]).astype(BF16)

    def wspec(off):
        return pl.BlockSpec((kw, cb), lambda j, i: (0, j + off))

    def bspec(off):
        return pl.BlockSpec((1, cb), lambda j, i: (0, j + off))

    out = _main_spec(tm, cb)
    return pl.pallas_call(
        body, name="ffn_act_fwd", grid=(nc, n // tm),
        in_specs=[_prev_spec(tm, cb), _main_spec(tm, cb), _prev_spec(tm, cb, nc), _main_spec(tm, cb, nc),
                  wspec(0), wspec(nc), bspec(0), bspec(nc)],
        out_specs=[out, out, out],
        out_shape=[jax.ShapeDtypeStruct((n, D_FF), F32), jax.ShapeDtypeStruct((n, D_FF), F32),
                   jax.ShapeDtypeStruct((n, D_FF), BF16)],
        compiler_params=_cparams("parallel", "parallel"))(x, x, x, x, w, w, b, b)


def _ffn_act_bwd(dact, hu, hg, x, w):
    n = x.shape[0]
    kw = w.shape[0]
    tm = _row_tile(n, 832)
    cb = _col_tile(D_FF, 256)
    nc = D_FF // cb
    nt = n // tm

    def body(d_ref, dn_ref, hu_ref, hun_ref, hg_ref, hgn_ref, xu_ref, xg_ref, wu_ref, wg_ref,
             dxu_ref, dxg_ref, dwu_ref, dwg_ref, dbu_ref, dbg_ref):
        i = pl.program_id(1)
        dact_e = _with_next(d_ref[...], dn_ref[...], i, nt)
        up_e = _with_next(hu_ref[...], hun_ref[...], i, nt)
        gate_e = _with_next(hg_ref[...], hgn_ref[...], i, nt)
        sg = _sigmoid(gate_e)
        dup_e = dact_e * (gate_e * sg)
        dgate_e = dact_e * up_e * (sg * (1.0 + gate_e * (1.0 - sg)))
        dx, dws, db = _conv_bwd_core(dup_e, xu_ref[...], wu_ref, kw, tm)
        dxu_ref[...] = dx.astype(BF16)
        _acc_rows(i, dwu_ref, dbu_ref, dws, db)
        dx, dws, db = _conv_bwd_core(dgate_e, xg_ref[...], wg_ref, kw, tm)
        dxg_ref[...] = dx.astype(BF16)
        _acc_rows(i, dwg_ref, dbg_ref, dws, db)

    main, nxt = _main_spec(tm, cb), _next_spec(tm, cb, n)
    wspec0 = pl.BlockSpec((kw, cb), lambda j, i: (0, j))
    wspec1 = pl.BlockSpec((kw, cb), lambda j, i: (0, j + nc))
    bspec = pl.BlockSpec((1, cb), lambda j, i: (0, j))
    return pl.pallas_call(
        body, name="ffn_act_bwd", grid=(nc, nt),
        in_specs=[main, nxt, main, nxt, main, nxt, _main_spec(tm, cb), _main_spec(tm, cb, nc), wspec0, wspec1],
        out_specs=[main, main, wspec0, wspec0, bspec, bspec],
        out_shape=[jax.ShapeDtypeStruct((n, D_FF), BF16), jax.ShapeDtypeStruct((n, D_FF), BF16),
                   jax.ShapeDtypeStruct((kw, D_FF), F32), jax.ShapeDtypeStruct((kw, D_FF), F32),
                   jax.ShapeDtypeStruct((1, D_FF), F32), jax.ShapeDtypeStruct((1, D_FF), F32)],
        compiler_params=_cparams("parallel", "arbitrary"))(dact, dact, hu, hu, hg, hg, x, x, w, w)


def _ssd_prep(pxs_ref, pb_ref, pc_ref, dtr_ref, dtb_ref, alog_ref, c):
    xs = _silu(pxs_ref[...])
    bm = _silu(pb_ref[...])
    cm = _silu(pc_ref[...])
    return (xs, bm, cm) + _ssd_decay(dtr_ref, dtb_ref, alog_ref, c)


def _ssd_decay(dtr_ref, dtb_ref, alog_ref, c):
    row =lax.broadcasted_iota(jnp.int32, (BLOCK, 1), 0) + c * BLOCK
    valid = (row >= PAD).astype(F32)
    dtr = dtr_ref[...] + dtb_ref[...]
    dt = _softplus(dtr) * valid
    a = -jnp.exp(alog_ref[...])
    lam = dt * a
    ri = lax.broadcasted_iota(jnp.int32, (BLOCK, BLOCK), 0)
    ci = lax.broadcasted_iota(jnp.int32, (BLOCK, BLOCK), 1)
    causal = ci <= ri
    cs = _sel_dot(causal.astype(BF16), lam)
    return valid, dtr, dt, a, lam, cs, causal


def _head_cols(r):
    return slice(SSD_HEADDIM * r, SSD_HEADDIM * (r + 1))


def _ssd_specs(nc, rev):
    def cidx(c):
        return nc - 1 - c if rev else c

    xs = pl.BlockSpec((BLOCK, SSD_GW), lambda g, c: (cidx(c), g))
    bspec = pl.BlockSpec((BLOCK, SSD_STATE), lambda g, c: (cidx(c), SSD_INNER // SSD_STATE + g))
    cspec = pl.BlockSpec((BLOCK, SSD_STATE), lambda g, c: (cidx(c), (SSD_INNER + SSD_BC) // SSD_STATE + g))
    lane = pl.BlockSpec((BLOCK, LANES), lambda g, c: (cidx(c), g))
    vec = pl.BlockSpec((1, LANES), lambda g, c: (0, g))
    wide_vec = pl.BlockSpec((1, SSD_GW), lambda g, c: (0, g))
    hsave = pl.BlockSpec((1, 1, SSD_GW, SSD_STATE), lambda g, c: (cidx(c), g, 0, 0))
    return xs, bspec, cspec, lane, vec, wide_vec, hsave


def _head_spread_matrix():
    r = lax.broadcasted_iota(jnp.int32, (LANES, SSD_GW), 0)
    col = lax.broadcasted_iota(jnp.int32, (LANES, SSD_GW), 1)
    return (col // SSD_HEADDIM == r).astype(BF16)


def _const_spec(shape):
    return pl.BlockSpec(shape, lambda g, c: (0,) * len(shape))


def _spread_heads(per_head, e_ref):
    wide = _dot_sel(jnp.concatenate(per_head, axis=0), e_ref[...])
    return [wide[BLOCK * k:BLOCK * (k + 1)] for k in range(len(per_head))]


def _call_with_side(body, side, *, name, grid, in_specs, out_specs, out_shape, scratch_shapes, args,
                    semantics=("parallel", "arbitrary")):
    if side is None:
        outs = pl.pallas_call(body, name=name, grid=grid, in_specs=in_specs, out_specs=out_specs, out_shape=out_shape,
                              scratch_shapes=scratch_shapes, compiler_params=_cparams(*semantics))(*args)
        return outs, []
    n_in, n_out, n_scr, n_side = len(in_specs), len(out_specs), len(scratch_shapes), len(side.arrays)

    def body_with_side(*refs):
        ins, rest = refs[:n_in + n_side], refs[n_in + n_side:]
        outs, scratch = rest[:n_out + n_side], rest[n_out + n_side:]
        side_refs = (ins[n_in:], outs[n_out:], scratch[n_scr:])
        ids = [pl.program_id(k) for k in range(len(grid))]
        inner_first = functools.reduce(jnp.logical_and, [i == 0 for i in ids[1:]], True)

        @pl.when((ids[0] == 0) & inner_first)
        def _():
            side.phases[0](*side_refs)

        body(*ins[:n_in], *outs[:n_out], *scratch[:n_scr])

        @pl.when((ids[0] == grid[0] // 2) & inner_first)
        def _():
            side.phases[1](*side_refs)

        @pl.when(functools.reduce(jnp.logical_and, [i == n - 1 for i, n in zip(ids, grid)]))
        def _():
            side.phases[2](*side_refs)

    any_spec = pl.BlockSpec(memory_space=pl.ANY)
    outs = pl.pallas_call(
        body_with_side, name=name, grid=grid, in_specs=list(in_specs) + [any_spec] * n_side,
        out_specs=list(out_specs) + [any_spec] * n_side, out_shape=list(out_shape) + list(side.out_shape),
        scratch_shapes=list(scratch_shapes) + list(side.scratch_shapes),
        compiler_params=_cparams(*["arbitrary"] * len(grid)))(*args, *side.arrays)
    return outs[:n_out], outs[n_out:]


def _ssd_fwd(pre, dt_raw, z, dtb, alog, dskip_w, norm_w, side=None):
    n = pre.shape[0]
    nc = n // BLOCK
    xs_s, b_s, c_s, lane_s, vec_s, wide_s, hs_s = _ssd_specs(nc, False)

    def body(pxs_ref, pb_ref, pc_ref, dtr_ref, z_ref, dtb_ref, alog_ref, dskw_ref, nw_ref, e_ref,
             y_ref, yn_ref, hs_ref, h_scr):
        c = pl.program_id(1)

        @pl.when(c == 0)
        def _():
            h_scr[...] = jnp.zeros_like(h_scr)

        xs, bm, cm, _, _, dt, _, _, cs, causal = _ssd_prep(pxs_ref, pb_ref, pc_ref, dtr_ref, dtb_ref, alog_ref, c)
        cst = cs.T
        cs_last = cs[BLOCK - 1:BLOCK, :]
        dt_w, ecs_w, dec_w = _spread_heads([dt, jnp.exp(cs), jnp.exp(cs_last - cs)], e_ref)
        xdt = xs * dt_w
        bmb = bm.astype(BF16)
        cmb = cm.astype(BF16)
        cb = _dot_nt(cmb, bmb)
        hg = h_scr[...]
        hs_ref[0, 0] = hg
        y = _dot_nt(cmb, hg.astype(BF16)) * ecs_w + dskw_ref[...] * xs
        first = lax.broadcasted_iota(jnp.int32, (BLOCK, LANES), 1) < SSD_HEADDIM
        diag = []
        for j in range(SSD_HPG // 2):
            xp = xdt[:, LANES * j:LANES * (j + 1)].astype(BF16)
            res = []
            for r in (2 * j, 2 * j + 1):
                lm = jnp.exp(jnp.where(causal, cs[:, r:r + 1] - cst[r:r + 1, :], NEG))
                res.append(_dot((cb * lm).astype(BF16), xp))
            diag.append(jnp.where(first, res[0], res[1]))
        y = y + jnp.concatenate(diag, axis=1)
        st = _dot_tn((xdt * dec_w).astype(BF16), bmb)
        eh = jnp.exp(cs_last)
        for r in range(SSD_HPG):
            rows = _head_cols(r)
            h_scr[rows, :] = hg[rows, :] * eh[:, r:r + 1] + st[rows, :]
        y_ref[...] = y
        gts = y * _silu(z_ref[...])
        rr = lax.rsqrt(jnp.mean(gts * gts, axis=-1, keepdims=True) + EPS)
        yn_ref[...] = (gts * rr * nw_ref[...]).astype(BF16)

    return _call_with_side(
        body, side, name="ssd_fwd", grid=(SSD_GROUPS, nc),
        in_specs=[xs_s, b_s, c_s, lane_s, xs_s, vec_s, vec_s, wide_s, wide_s, _const_spec((LANES, SSD_GW))],
        out_specs=[xs_s, xs_s, hs_s],
        out_shape=[jax.ShapeDtypeStruct((n, SSD_INNER), F32), jax.ShapeDtypeStruct((n, SSD_INNER), BF16),
                   jax.ShapeDtypeStruct((nc, SSD_GROUPS, SSD_GW, SSD_STATE), F32)],
        scratch_shapes=[pltpu.VMEM((SSD_GW, SSD_STATE), F32)],
        args=(pre, pre, pre, dt_raw, z, dtb, alog, dskip_w, norm_w, _head_spread_matrix()))


def _lane_put(acc, col, r):
    lane = lax.broadcasted_iota(jnp.int32, acc.shape, 1)
    return jnp.where(lane == r, col, acc)


def _ssd_bwd(dyn, y, z, pre, dt_raw, hsave, dtb, alog, dskip_w, norm_w, side=None):
    n = pre.shape[0]
    nc = n // BLOCK
    spread = _head_spread_matrix()
    xs_s, b_s, c_s, lane_s, vec_s, wide_s, hs_s = _ssd_specs(nc, True)
    bc_out =pl.BlockSpec((BLOCK, SSD_STATE), lambda g, c: (nc - 1 - c, g))

    def body(dyn_ref, y_ref, z_ref, pxs_ref, pb_ref, pc_ref, dtr_ref, hs_ref, dtb_ref, alog_ref, dskw_ref, nw_ref,
             e_ref, r_ref,
             dz_ref, dxs_ref, dbm_ref, dcm_ref, ddt_ref, dnw_ref, ddtb_ref, dalog_ref, ddsk_ref, g_scr):
        step = pl.program_id(1)
        c = nc - 1 - step

        @pl.when(step == 0)
        def _():
            g_scr[...] = jnp.zeros_like(g_scr)

        pxs, pb, pc = pxs_ref[...], pb_ref[...], pc_ref[...]
        sx, sb, sc = _sigmoid(pxs), _sigmoid(pb), _sigmoid(pc)
        xs, bm, cm = pxs * sx, pb * sb, pc * sc
        valid, dtr, dt, a, lam, cs, causal = _ssd_decay(dtr_ref, dtb_ref, alog_ref, c)
        cst = cs.T
        cs_last = cs[BLOCK - 1:BLOCK, :]
        bmb = bm.astype(BF16)
        cmb = cm.astype(BF16)
        cb = _dot_nt(cmb, bmb)
        hg = hs_ref[0, 0]
        hgb = hg.astype(BF16)
        yoff = _dot_nt(cmb, hgb)
        gn = g_scr[...]
        gnb = gn.astype(BF16)

        zv = z_ref[...]
        yv = y_ref[...]
        sgz = _sigmoid(zv)
        sz = zv * sgz
        gts = yv * sz
        rr = lax.rsqrt(jnp.mean(gts * gts, axis=-1, keepdims=True) + EPS)
        xh = gts * rr
        dynv = dyn_ref[...]
        gg = dynv * nw_ref[...]
        dgts = rr * (gg - xh * jnp.mean(gg * xh, axis=-1, keepdims=True))
        dnw = jnp.sum(dynv * xh, axis=0, keepdims=True)
        dy = dgts * sz
        dz_ref[...] = (dgts * yv * (sgz * (1.0 + zv * (1.0 - sgz)))).astype(BF16)

        ecs = jnp.exp(cs)
        dec = jnp.exp(cs_last - cs)
        eh = jnp.exp(cs_last)
        dt_w, ecs_w, dec_w = _spread_heads([dt, ecs, dec], e_ref)
        red_m = r_ref[...]

        def head_sums(v):
            return _dot_sel(v, red_m, terms=2)

        xdt = xs * dt_w
        q_all = _dot_nt(bmb, gnb)
        w_all = (dy * ecs_w).astype(BF16)
        e_hl = head_sums(q_all * xdt) * dec
        dcs_col = head_sums(dy * yoff) * ecs - e_hl
        gh = jnp.zeros((1, LANES), F32)
        prod = gn * hg
        for r in range(SSD_HPG):
            gh = _lane_put(gh, _sum_all(prod[_head_cols(r), :]), r)
        dcs_last = jnp.sum(e_hl, axis=0, keepdims=True) + eh * gh
        ddsk = jnp.sum(head_sums(dy * xs), axis=0, keepdims=True)
        cbt = _dot_nt(bmb, cmb)
        lane = lax.broadcasted_iota(jnp.int32, (BLOCK, LANES), 1)
        first = lane < SSD_HEADDIM
        causal_t = lax.broadcasted_iota(jnp.int32, (BLOCK, BLOCK), 1) >= lax.broadcasted_iota(
            jnp.int32, (BLOCK, BLOCK), 0)
        sub = lax.broadcasted_iota(jnp.int32, (SUBLANES, BLOCK), 0)
        dcs_row = jnp.zeros((SUBLANES, BLOCK), F32)
        dcb = jnp.zeros((BLOCK, BLOCK), F32)
        dxdt_pairs = []
        for j in range(SSD_HPG // 2):
            tile = slice(LANES * j, LANES * (j + 1))
            dy_p = dy[:, tile]
            dyb = dy_p.astype(BF16)
            xdtb = xdt[:, tile].astype(BF16)
            res = []
            for half, r in enumerate((2 * j, 2 * j + 1)):
                csc, csr = cs[:, r:r + 1], cst[r:r + 1, :]
                lm = jnp.exp(jnp.where(causal, csc - csr, NEG))
                lmt = jnp.exp(jnp.where(causal_t, csr - csc, NEG))
                keep = first if half == 0 else jnp.logical_not(first)
                gm = _dot_nt(jnp.where(keep, dy_p, 0.0).astype(BF16), xdtb) * lm
                dcb = dcb + gm
                mm_ = gm * cb
                dcs_col = dcs_col + jnp.where(lane == r, jnp.sum(mm_, axis=1, keepdims=True), 0.0)
                dcs_row = jnp.where(sub == r, jnp.sum(mm_, axis=0, keepdims=True), dcs_row)
                res.append(_dot((cbt * lmt).astype(BF16), dyb))
            dxdt_pairs.append(jnp.where(first, res[0], res[1]))
        dxdt = jnp.concatenate(dxdt_pairs, axis=1) + q_all * dec_w
        ddt_x = head_sums(dxdt * xs)
        dxs = dxdt * dt_w + dskw_ref[...] * dy
        dcbb = dcb.astype(BF16)
        dcm = _dot(w_all, hgb) + _dot(dcbb, bmb)
        dbm = _dot((xdt * dec_w).astype(BF16), gnb) + _dot_tn(dcbb, cmb)
        dh_off = _dot_tn(w_all, cmb)
        for r in range(SSD_HPG):
            rows = _head_cols(r)
            g_scr[rows, :] = gn[rows, :] * eh[:, r:r + 1] + dh_off[rows, :]

        pad_rows = jnp.zeros((BLOCK - SUBLANES, BLOCK), F32)
        dcs = dcs_col - jnp.concatenate([dcs_row, pad_rows], axis=0).T
        rsel = lax.broadcasted_iota(jnp.int32, (BLOCK, LANES), 0)
        dcs = dcs + jnp.where(rsel == BLOCK - 1, dcs_last, 0.0)
        ri = lax.broadcasted_iota(jnp.int32, (BLOCK, BLOCK), 0)
        ci = lax.broadcasted_iota(jnp.int32, (BLOCK, BLOCK), 1)
        dlam = _sel_dot((ci >= ri).astype(BF16), dcs)
        head = lane < SSD_HPG
        ddt = dlam * a + ddt_x
        ddtr = jnp.where(head, ddt * _sigmoid(dtr) * valid, 0.0)
        ddt_ref[...] = ddtr.astype(BF16)
        dalog = jnp.sum(jnp.where(head, dlam * lam, 0.0), axis=0, keepdims=True)
        ddtb = jnp.sum(ddtr, axis=0, keepdims=True)

        dxs_ref[...] = dxs * (sx * (1.0 + pxs * (1.0 - sx)))
        dbm_ref[...] = dbm * (sb * (1.0 + pb * (1.0 - sb)))
        dcm_ref[...] = dcm * (sc * (1.0 + pc * (1.0 - sc)))

        @pl.when(step == 0)
        def _():
            dnw_ref[...] = dnw
            ddtb_ref[...] = ddtb
            dalog_ref[...] = dalog
            ddsk_ref[...] = ddsk

        @pl.when(step > 0)
        def _():
            dnw_ref[...] += dnw
            ddtb_ref[...] += ddtb
            dalog_ref[...] += dalog
            ddsk_ref[...] += ddsk

    return _call_with_side(
        body, side, name="ssd_bwd", grid=(SSD_GROUPS, nc),
        in_specs=[xs_s, xs_s, xs_s, xs_s, b_s, c_s, lane_s, hs_s, vec_s, vec_s, wide_s, wide_s,
                  _const_spec((LANES, SSD_GW)), _const_spec((SSD_GW, LANES))],
        out_specs=[xs_s, xs_s, bc_out, bc_out, lane_s, wide_s, vec_s, vec_s, vec_s],
        out_shape=[jax.ShapeDtypeStruct((n, SSD_INNER), BF16), jax.ShapeDtypeStruct((n, SSD_INNER), F32),
                   jax.ShapeDtypeStruct((n, SSD_BC), F32), jax.ShapeDtypeStruct((n, SSD_BC), F32),
                   jax.ShapeDtypeStruct((n, DT_W), BF16), jax.ShapeDtypeStruct((1, SSD_INNER), F32),
                   jax.ShapeDtypeStruct((1, DT_W), F32), jax.ShapeDtypeStruct((1, DT_W), F32),
                   jax.ShapeDtypeStruct((1, DT_W), F32)],
        scratch_shapes=[pltpu.VMEM((SSD_GW, SSD_STATE), F32)],
        args=(dyn, y, z, pre, pre, pre, dt_raw, hsave, dtb, alog, dskip_w, norm_w, spread, spread.T))


def _bucket_table():
    def bucket(dist):
        d = np.maximum(dist, 0)
        half = REL_BUCKETS // 2
        big = half + (np.log(np.maximum(d, half).astype(np.float32) / np.float32(half))
                      / np.float32(math.log(REL_MAX_DIST / half)) * np.float32(REL_BUCKETS - half)).astype(np.int32)
        return np.where(d < half, d, np.minimum(big, REL_BUCKETS - 1)).astype(np.int32)

    l = np.arange(BLOCK)[None, :]
    band = bucket(l + BLOCK - np.arange(2 * BLOCK)[:, None])
    j = np.arange(BLOCK)[:, None]
    tables = [np.concatenate([bucket(v * BLOCK + l - j), band], axis=0) for v in range(3)]
    return np.concatenate([t.reshape(-1) for t in tables])


def _onehot_t():
    buckets = jnp.asarray(_bucket_table())
    return (buckets[None, :] == jnp.arange(REL_BUCKETS, dtype=jnp.int32)[:, None]).astype(F32)


def _bias_tables(rel_t, onehot_t):
    def body(r_ref, oh_ref, o_ref):
        o_ref[...] = jnp.dot(r_ref[...], oh_ref[...], precision=HIGHEST, preferred_element_type=F32)

    return pl.pallas_call(
        body, name="bias_tables", grid=(NT_ALL // NT_TILE,),
        in_specs=[pl.BlockSpec((ATT_HEADS, REL_BUCKETS), lambda i: (0, 0)),
                  pl.BlockSpec((REL_BUCKETS, NT_TILE), lambda i: (0, i))],
        out_specs=pl.BlockSpec((ATT_HEADS, NT_TILE), lambda i: (0, i)),
        out_shape=jax.ShapeDtypeStruct((ATT_HEADS, NT_ALL), F32),
        compiler_params=_cparams("parallel"))(rel_t, onehot_t)


def _bias_grad(dtab, onehot_t):
    def body(d_ref, oh_ref, o_ref):
        i = pl.program_id(0)
        p = lax.dot_general(d_ref[...], oh_ref[...], (((1,), (1,)), ((), ())), precision=HIGHEST,
                            preferred_element_type=F32)

        @pl.when(i == 0)
        def _():
            o_ref[...] = p

        @pl.when(i > 0)
        def _():
            o_ref[...] += p

    return pl.pallas_call(
        body, name="bias_grad", grid=(NT_ALL // NT_TILE,),
        in_specs=[pl.BlockSpec((ATT_HEADS, NT_TILE), lambda i: (0, i)),
                  pl.BlockSpec((REL_BUCKETS, NT_TILE), lambda i: (0, i))],
        out_specs=pl.BlockSpec((ATT_HEADS, REL_BUCKETS), lambda i: (0, 0)),
        out_shape=jax.ShapeDtypeStruct((ATT_HEADS, REL_BUCKETS), F32),
        compiler_params=_cparams("arbitrary"))(dtab, onehot_t)


def _att_mask_t(n, copies):
    far = 4 * BLOCK
    kk = lax.broadcasted_iota(jnp.int32, (N_KEYS, copies * BLOCK), 0)
    li = lax.broadcasted_iota(jnp.int32, (N_KEYS, copies * BLOCK), 1) & (BLOCK - 1)
    meta_ok = (kk >= PAD) & (kk < BLOCK) & (li + jnp.where(n >= 1, far, 0) >= kk)
    prev_ok = (kk >= BLOCK) & (kk < 2 * BLOCK) & (kk - BLOCK > li + jnp.where(n >= 2, 0, far))
    cur_ok = (kk >= 2 * BLOCK) & (kk - 2 * BLOCK <= li - jnp.where(n >= 1, 0, far))
    return meta_ok | prev_ok | cur_ok


def _att_kv(meta_ref, prev_ref, cur_ref):
    kv = jnp.concatenate([meta_ref[...], prev_ref[...], cur_ref[...]], axis=0)
    first = lax.broadcasted_iota(jnp.int32, (N_KEYS, LANES), 1) < ATT_HEADDIM
    out = []
    for pair in (kv[:, :LANES], kv[:, LANES:]):
        swapped = pltpu.roll(pair, ATT_HEADDIM, 1)
        out.append([jnp.where(first, pair, swapped).astype(BF16), jnp.where(first, swapped, pair).astype(BF16)])
    return out[0], out[1]


def _split_heads(x_pair, first):
    return jnp.concatenate([jnp.where(first, x_pair, 0.0), jnp.where(first, 0.0, x_pair)], axis=0).astype(BF16)


def _att_probs_t(qm2, k_dup, t_ref, j, mask2, sink_ref):
    scale = ATT_HEADDIM ** -0.5
    bias2 = jnp.concatenate([t_ref[0, 2 * j], t_ref[0, 2 * j + 1]], axis=1)
    second = lax.broadcasted_iota(jnp.int32, (1, 2 * BLOCK), 1) >= BLOCK
    sink2 = jnp.where(second, sink_ref[0:1, 2 * j + 1:2 * j + 2], sink_ref[0:1, 2 * j:2 * j + 1])
    s_t = jnp.where(mask2, _dot_nt(k_dup, qm2) * scale + bias2, NEG)
    mx = jnp.maximum(jnp.max(s_t, axis=0, keepdims=True), sink2)
    p_t = jnp.exp(s_t - mx)
    p_s = jnp.exp(sink2 - mx)
    inv = 1.0 / (jnp.sum(p_t, axis=0, keepdims=True) + p_s)
    return p_t * inv, p_s * inv


def _att_specs(nb, rev):
    def nidx(i):
        return nb - 1 - i if rev else i

    kvb = ATT_Q // (2 * ATT_KV)
    q_s = pl.BlockSpec((BLOCK, ATT_Q), lambda i: (nidx(i), 0))
    cur = pl.BlockSpec((BLOCK, 2 * ATT_KV), lambda i: (nidx(i), kvb))
    prev = pl.BlockSpec((BLOCK, 2 * ATT_KV), lambda i: (jnp.maximum(nidx(i) - 1, 0), kvb))
    meta = pl.BlockSpec((BLOCK, 2 * ATT_KV), lambda i: (0, kvb))
    table = pl.BlockSpec((1, ATT_HEADS, N_KEYS, BLOCK), lambda i: (jnp.minimum(nidx(i), 2), 0, 0, 0))
    sink = pl.BlockSpec((1, LANES), lambda i: (0, 0))
    return q_s, cur, prev, meta, table, sink


def _attn_fwd(qkv, tables, sinks):
    n = qkv.shape[0]
    nb = n // BLOCK
    q_s, cur_s, prev_s, meta_s, t_s, sink_s = _att_specs(nb, False)

    def body(q_ref, cur_ref, prev_ref, meta_ref, t_ref, sink_ref, o_ref):
        blk = pl.program_id(0)
        mask_t = _att_mask_t(blk, 1)
        k_dup, v_dup = _att_kv(meta_ref, prev_ref, cur_ref)
        v_dup_t = [v.T for v in v_dup]
        first = lax.broadcasted_iota(jnp.int32, (BLOCK, LANES), 1) < ATT_HEADDIM
        top = lax.broadcasted_iota(jnp.int32, (LANES, BLOCK), 0) < ATT_HEADDIM
        scale = ATT_HEADDIM ** -0.5
        for j in range(ATT_HEADS // 2):
            kh = 2 * j // ATT_GQ
            tile = slice(LANES * j, LANES * (j + 1))
            q_p = q_ref[:, tile]
            res = []
            for half, h in enumerate((2 * j, 2 * j + 1)):
                qm = jnp.where(first if half == 0 else jnp.logical_not(first), q_p, 0.0).astype(BF16)
                sink = sink_ref[0:1, h:h + 1]
                s_t = jnp.where(mask_t, _dot_nt(k_dup[kh], qm) * scale + t_ref[0, h], NEG)
                mx = jnp.maximum(jnp.max(s_t, axis=0, keepdims=True), sink)
                p_t = jnp.exp(s_t - mx)
                inv = 1.0 / (jnp.sum(p_t, axis=0, keepdims=True) + jnp.exp(sink - mx))
                res.append(_dot(v_dup_t[kh], (p_t * inv).astype(BF16)))
            o_ref[:, tile] = jnp.where(top, res[0], res[1]).T.astype(BF16)

    return pl.pallas_call(
        body, name="attn_fwd", grid=(nb,),
        in_specs=[q_s, cur_s, prev_s, meta_s, t_s, sink_s],
        out_specs=q_s,
        out_shape=jax.ShapeDtypeStruct((n, ATT_Q), BF16),
        compiler_params=_cparams("parallel"))(qkv, qkv, qkv, qkv, tables, sinks)


def _attn_bwd(datt, qkv, tables, sinks):
    n = qkv.shape[0]
    nb = n // BLOCK
    q_s, cur_s, prev_s, meta_s, t_s, sink_s = _att_specs(nb, True)
    dqkv_s = pl.BlockSpec((BLOCK, ATT_Q + 2 * ATT_KV), lambda i: (nb - 1 - i, 0))
    scale = ATT_HEADDIM ** -0.5

    def body(do_ref, q_ref, cur_ref, prev_ref, meta_ref, t_ref, sink_ref,
             dqkv_ref, dt_ref, dsink_ref, carry_scr, meta_scr):
        step = pl.program_id(0)
        blk = nb - 1 - step
        mask2 = _att_mask_t(blk, 2)
        k_dup, v_dup = _att_kv(meta_ref, prev_ref, cur_ref)
        k_dup_t = [k.T for k in k_dup]

        @pl.when(step == 0)
        def _():
            carry_scr[...] = jnp.zeros_like(carry_scr)
            meta_scr[...] = jnp.zeros_like(meta_scr)
            dsink_ref[...] = jnp.zeros_like(dsink_ref)

        @pl.when((step == 0) | (blk <= 1))
        def _():
            dt_ref[...] = jnp.zeros_like(dt_ref)

        first = lax.broadcasted_iota(jnp.int32, (BLOCK, LANES), 1) < ATT_HEADDIM
        top = lax.broadcasted_iota(jnp.int32, (LANES, BLOCK), 0) < ATT_HEADDIM
        first_k = lax.broadcasted_iota(jnp.int32, (N_KEYS, LANES), 1) < ATT_HEADDIM
        dsink = jnp.zeros((1, LANES), F32)
        dk_acc = [None] * ATT_KV_HEADS
        dv_acc = [None] * ATT_KV_HEADS
        for j in range(ATT_HEADS // 2):
            kh = 2 * j // ATT_GQ
            tile = slice(LANES * j, LANES * (j + 1))
            qm2 = _split_heads(q_ref[:, tile], first)
            dom2 = _split_heads(do_ref[:, tile], first)
            p_t, p_s = _att_probs_t(qm2, k_dup[kh], t_ref, j, mask2, sink_ref)
            dp_t = _dot_nt(v_dup[kh], dom2)
            delta = jnp.sum(p_t * dp_t, axis=0, keepdims=True)
            ds_t = p_t * (dp_t - delta)
            sink_terms = p_s * delta
            for half in range(2):
                cols = slice(BLOCK * half, BLOCK * (half + 1))
                dsink = _lane_put(dsink, -jnp.sum(sink_terms[:, cols], axis=1, keepdims=True), 2 * j + half)
                dt_ref[0, 2 * j + half] += ds_t[:, cols]
            ds_tb = ds_t.astype(BF16)
            dq_t = _dot(k_dup_t[kh], ds_tb)
            dqkv_ref[:, tile] = (jnp.where(top, dq_t[:, :BLOCK], dq_t[:, BLOCK:]).T * scale).astype(BF16)
            dk_part, dv_part = _dot(ds_tb, qm2), _dot(p_t.astype(BF16), dom2)
            dk_acc[kh] = dk_part if dk_acc[kh] is None else dk_acc[kh] + dk_part
            dv_acc[kh] = dv_part if dv_acc[kh] is None else dv_acc[kh] + dv_part
        dsink_ref[...] += dsink
        folded = [a + pltpu.roll(a, ATT_HEADDIM, 1) for a in dk_acc + dv_acc]
        dkv = jnp.concatenate([jnp.where(first_k, folded[0], folded[1]) * scale,
                               jnp.where(first_k, folded[2], folded[3])], axis=1)
        meta_scr[...] += dkv[:BLOCK, :]
        own = dkv[2 * BLOCK:, :] + carry_scr[...]
        carry_scr[...] = dkv[BLOCK:2 * BLOCK, :]

        @pl.when(blk > 0)
        def _():
            dqkv_ref[:, ATT_Q:] = own.astype(BF16)

        @pl.when(blk == 0)
        def _():
            dqkv_ref[:, ATT_Q:] = (own + meta_scr[...]).astype(BF16)

    return pl.pallas_call(
        body, name="attn_bwd", grid=(nb,),
        in_specs=[q_s, q_s, cur_s, prev_s, meta_s, t_s, sink_s],
        out_specs=[dqkv_s, t_s, sink_s],
        out_shape=[jax.ShapeDtypeStruct((n, ATT_Q + 2 * ATT_KV), BF16),
                   jax.ShapeDtypeStruct((3, ATT_HEADS, N_KEYS, BLOCK), F32),
                   jax.ShapeDtypeStruct((1, LANES), F32)],
        scratch_shapes=[pltpu.VMEM((BLOCK, 2 * ATT_KV), F32), pltpu.VMEM((BLOCK, 2 * ATT_KV), F32)],
        compiler_params=_cparams("arbitrary"))(datt, qkv, qkv, qkv, qkv, tables, sinks)


def _merge_out_fwd(gates, y_ssd, y_att, gate_b, w_out, h):
    n = gates.shape[0]
    tm = _row_tile(n, 416)

    def body(gs_ref, ga_ref, ys_ref, ya_ref, gb_ref, w_ref, h_ref, m_ref, o_ref):
        merged = (_sigmoid(gs_ref[...] + gb_ref[0:1, :]) * ys_ref[...]
                  + _sigmoid(ga_ref[...] + gb_ref[1:2, :]) * ya_ref[...]).astype(BF16)
        m_ref[...] = merged
        row = pl.program_id(0) * tm + lax.broadcasted_iota(jnp.int32, (tm, 1), 0)
        o_ref[...] = jnp.where(row >= PAD, _dot(merged, w_ref[...]), 0.0) + h_ref[...]

    row = pl.BlockSpec((tm, D_MODEL), lambda i: (i, 0))
    return pl.pallas_call(
        body, name="merge_out_fwd", grid=(n // tm,),
        in_specs=[row, pl.BlockSpec((tm, D_MODEL), lambda i: (i, 1)), row, row,
                  pl.BlockSpec((2, D_MODEL), lambda i: (0, 0)), pl.BlockSpec((D_MODEL, D_MODEL), lambda i: (0, 0)), row],
        out_specs=[row, row],
        out_shape=[jax.ShapeDtypeStruct((n, D_MODEL), BF16), jax.ShapeDtypeStruct((n, D_MODEL), F32)],
        compiler_params=_cparams("parallel"))(gates, gates, y_ssd, y_att, gate_b, w_out, h)


def _merge_out_bwd(dh, w_out, gates, y_ssd, y_att, gate_b):
    n = gates.shape[0]
    tm = _row_tile(n, 416)

    def body(dh_ref, w_ref, gs_ref, ga_ref, ys_ref, ya_ref, gb_ref, dys_ref, dya_ref, dg_ref, dgb_ref):
        i = pl.program_id(0)
        row = i * tm + lax.broadcasted_iota(jnp.int32, (tm, 1), 0)
        dmv = jnp.where(row >= PAD, _dot_nt(dh_ref[...].astype(BF16), w_ref[...]), 0.0)
        ss =_sigmoid(gs_ref[...] + gb_ref[0:1, :])
        sa = _sigmoid(ga_ref[...] + gb_ref[1:2, :])
        dys_ref[...] = (dmv * ss).astype(BF16)
        dya_ref[...] = (dmv * sa).astype(BF16)
        dgs = dmv * ys_ref[...] * ss * (1.0 - ss)
        dga = dmv * ya_ref[...] * sa * (1.0 - sa)
        dg_ref[:, :D_MODEL] = dgs.astype(BF16)
        dg_ref[:, D_MODEL:] = dga.astype(BF16)
        part = jnp.concatenate([jnp.sum(dgs, axis=0, keepdims=True), jnp.sum(dga, axis=0, keepdims=True)], axis=0)

        @pl.when(i == 0)
        def _():
            dgb_ref[...] = part

        @pl.when(i > 0)
        def _():
            dgb_ref[...] += part

    row = pl.BlockSpec((tm, D_MODEL), lambda i: (i, 0))
    gb = pl.BlockSpec((2, D_MODEL), lambda i: (0, 0))
    return pl.pallas_call(
        body, name="merge_out_bwd", grid=(n // tm,),
        in_specs=[row, pl.BlockSpec((D_MODEL, D_MODEL), lambda i: (0, 0)), row,
                  pl.BlockSpec((tm, D_MODEL), lambda i: (i, 1)), row, row, gb],
        out_specs=[row, row, pl.BlockSpec((tm, 2 * D_MODEL), lambda i: (i, 0)), gb],
        out_shape=[jax.ShapeDtypeStruct((n, D_MODEL), BF16), jax.ShapeDtypeStruct((n, D_MODEL), BF16),
                   jax.ShapeDtypeStruct((n, 2 * D_MODEL), BF16), jax.ShapeDtypeStruct((2, D_MODEL), F32)],
        compiler_params=_cparams("arbitrary"))(dh, w_out, gates, gates, y_ssd, y_att, gate_b)


def _col_move(srcs, outs, pieces, *, name):
    rows = srcs[0].shape[-2]
    tr = _row_tile(rows, 128)
    n_src = len(srcs)
    covered = [sum(p[6] for p in pieces if p[0] == o) for o in range(len(outs))]
    total = [int(np.prod(shp)) // rows for shp, _ in outs]

    def body(*refs):
        in_refs, out_refs = refs[:n_src], refs[n_src:]
        for o, ref in enumerate(out_refs):
            if covered[o] != total[o]:
                ref[...] = jnp.zeros_like(ref)
        for o, ol, oc, s, sl, sc, width in pieces:
            val = in_refs[s][:, sc:sc + width] if sl is None else in_refs[s][sl, :, sc:sc + width]
            val = val.astype(outs[o][1])
            if ol is None:
                out_refs[o][:, oc:oc + width] = val
            else:
                out_refs[o][ol, :, oc:oc + width] = val

    def spec(shape):
        if len(shape) == 2:
            return pl.BlockSpec((tr, shape[1]), lambda i: (i, 0))
        return pl.BlockSpec((shape[0], tr, shape[2]), lambda i: (0, i, 0))

    return pl.pallas_call(
        body, name=name, grid=(rows // tr,),
        in_specs=[spec(a.shape) for a in srcs], out_specs=[spec(shp) for shp, _ in outs],
        out_shape=[jax.ShapeDtypeStruct(shp, dt) for shp, dt in outs],
        compiler_params=_cparams("parallel"))(*srcs)


def _shard_pieces(seg_ranges, shard_w):
    out = []
    for seg, runs in enumerate(seg_ranges):
        for g0, width, s0 in runs:
            done = 0
            while done < width:
                dev, col = divmod(g0 + done, shard_w)
                take = min(width - done, shard_w - col)
                out.append((seg, s0 + done, dev, col, take))
                done += take
    return out


_CHIP_RELATIONS = [(1, 0, 0), (0, 1, 0), (1, 1, 0)]
N_CHIPS = 4


def _gather_two_level(arrays, *, name):
    outs = _run_plan(_gather_plan(arrays), name)
    return [o.reshape((N_DEV,) + a.shape) for o, a in zip(outs, arrays)]


class _CommPlan:
    def __init__(self, arrays, out_shape, scratch_shapes, phases):
        self.arrays, self.out_shape, self.scratch_shapes, self.phases = arrays, out_shape, scratch_shapes, phases


def _run_plan(plan, name):
    n_arr = len(plan.arrays)

    def body(*refs):
        ins, outs, sems = refs[:n_arr], refs[n_arr:2 * n_arr], refs[2 * n_arr:]
        for phase in plan.phases:
            phase(ins, outs, sems)

    any_spec = pl.BlockSpec(memory_space=pl.ANY)
    return pl.pallas_call(
        body, name=name, in_specs=[any_spec] * n_arr, out_specs=[any_spec] * n_arr, out_shape=plan.out_shape,
        scratch_shapes=plan.scratch_shapes)(*plan.arrays)


def _gather_plan(arrays):
    n_arr = len(arrays)
    n_chips = len(_CHIP_RELATIONS)
    n_pair = 1 + 2 * n_chips

    def where():
        x, y, c = lax.axis_index("x"), lax.axis_index("y"), lax.axis_index("c")
        return x, y, c, (x, y, 1 - c), [(x ^ dx, y ^ dy) for dx, dy, _ in _CHIP_RELATIONS]

    def copy(outs, sems, a, k, block, to, src=None):
        slot = outs[a].at[2 * block[0] + block[1], block[2]]
        return pltpu.make_async_remote_copy(
            src_ref=slot if src is None else src, dst_ref=slot, send_sem=sems[0].at[a * n_pair + k],
            recv_sem=sems[1].at[a * n_pair + k], device_id=to, device_id_type=MESH)

    def mine(ins, outs, sems, a, x, y, c):
        return pltpu.make_async_copy(ins[a], outs[a].at[2 * x + y, c], sems[2].at[a])

    def first_copies(ins, outs, sems, a, x, y, c, sibling, chips):
        return ([copy(outs, sems, a, 0, (x, y, c), sibling, src=ins[a])]
                + [copy(outs, sems, a, 1 + j, (x, y, c), (*chip, c), src=ins[a]) for j, chip in enumerate(chips)])

    def start(ins, outs, sems):
        x, y, c, sibling, chips = where()
        for a in range(n_arr):
            mine(ins, outs, sems, a, x, y, c).start()
            for cp in first_copies(ins, outs, sems, a, x, y, c, sibling, chips):
                cp.start()

    def pass_on(ins, outs, sems):
        x, y, c, sibling, chips = where()
        for j, chip in enumerate(chips):
            for a in range(n_arr):
                copy(outs, sems, a, 1 + j, (*chip, c), (x, y, c)).wait_recv()
                copy(outs, sems, a, 1 + n_chips + j, (*chip, c), sibling).start()

    def finish(ins, outs, sems):
        x, y, c, sibling, chips = where()
        for a in range(n_arr):
            copy(outs, sems, a, 0, (x, y, 1 - c), (x, y, c)).wait_recv()
            for j, chip in enumerate(chips):
                copy(outs, sems, a, 1 + n_chips + j, (*chip, 1 - c), (x, y, c)).wait_recv()
        for a in range(n_arr):
            for cp in first_copies(ins, outs, sems, a, x, y, c, sibling, chips):
                cp.wait_send()
            for j, chip in enumerate(chips):
                copy(outs, sems, a, 1 + n_chips + j, (*chip, c), sibling).wait_send()
            mine(ins, outs, sems, a, x, y, c).wait()

    return _CommPlan(
        arrays, [jax.ShapeDtypeStruct((N_CHIPS, 2) + a.shape, a.dtype) for a in arrays],
        [pltpu.SemaphoreType.DMA((n_arr * n_pair,)), pltpu.SemaphoreType.DMA((n_arr * n_pair,)),
         pltpu.SemaphoreType.DMA((n_arr,))],
        (start, pass_on, finish))


def _sibling_exchange(arrays, scatter, *, name):
    n_arr = len(arrays)

    def body(*refs):
        ins, outs = refs[:n_arr], refs[n_arr:2 * n_arr]
        send_sems, recv_sems = refs[2 * n_arr:]
        x, y, c = lax.axis_index("x"), lax.axis_index("y"), lax.axis_index("c")
        copies = []
        for a in range(n_arr):
            for q in range(N_CHIPS if scatter[a] else 1):
                src = ins[a].at[2 * q + 1 - c] if scatter[a] else ins[a]
                dst = outs[a].at[q] if scatter[a] else outs[a]
                cp = pltpu.make_async_remote_copy(
                    src_ref=src, dst_ref=dst, send_sem=send_sems.at[a * N_CHIPS + q],
                    recv_sem=recv_sems.at[a * N_CHIPS + q], device_id=(x, y, 1 - c), device_id_type=MESH)
                cp.start()
                copies.append(cp)
        for cp in copies:
            cp.wait_send()
        for cp in copies:
            cp.wait_recv()

    any_spec = pl.BlockSpec(memory_space=pl.ANY)
    return pl.pallas_call(
        body, name=name, in_specs=[any_spec] * n_arr, out_specs=[any_spec] * n_arr,
        out_shape=[jax.ShapeDtypeStruct(((N_CHIPS,) + a.shape[1:]) if s else a.shape, a.dtype)
                   for a, s in zip(arrays, scatter)],
        scratch_shapes=[pltpu.SemaphoreType.DMA((n_arr * N_CHIPS,)), pltpu.SemaphoreType.DMA((n_arr * N_CHIPS,))],
    )(*arrays)


def _add(a, b, *, name):
    rows, cols = a.shape
    tr = _row_tile(rows, 256)

    def body(a_ref, b_ref, o_ref):
        o_ref[...] = a_ref[...] + b_ref[...]

    blk = pl.BlockSpec((tr, cols), lambda i: (i, 0))
    return pl.pallas_call(body, name=name, grid=(rows // tr,), in_specs=[blk, blk], out_specs=blk,
                          out_shape=jax.ShapeDtypeStruct(a.shape, a.dtype), compiler_params=_cparams("parallel"))(a, b)


def _chip_exchange(arrays, scatter, *, name):
    return _run_plan(_chip_exchange_plan(arrays, scatter), name)


_ALL_RELATIONS = [(dx, dy, dc) for dx in (0, 1) for dy in (0, 1) for dc in (0, 1)][1:]


def _all_to_all_plan(arrays, scatter=None):
    n_arr = len(arrays)
    n_rel = len(_ALL_RELATIONS)
    scatter = scatter or [True] * n_arr

    def block(ins, a, p):
        return ins[a].at[p] if scatter[a] else ins[a]

    def local_copies(ins, outs, sems):
        me = 4 * lax.axis_index("x") + 2 * lax.axis_index("y") + lax.axis_index("c")
        return [pltpu.make_async_copy(block(ins, a, me), outs[a].at[me], sems[2].at[a]) for a in range(n_arr)]

    def remote_copies(ins, outs, sems, arrivals):
        x, y, c = lax.axis_index("x"), lax.axis_index("y"), lax.axis_index("c")
        me = 4 * x + 2 * y + c
        out = []
        for k, (dx, dy, dc) in enumerate(_ALL_RELATIONS):
            px, py, pc = x ^ dx, y ^ dy, c ^ dc
            peer = 4 * px + 2 * py + pc
            for a in range(n_arr):
                out.append(pltpu.make_async_remote_copy(
                    src_ref=block(ins, a, peer), dst_ref=outs[a].at[peer if arrivals else me],
                    send_sem=sems[0].at[a * n_rel + k], recv_sem=sems[1].at[a * n_rel + k],
                    device_id=(x, y, c) if arrivals else (px, py, pc), device_id_type=MESH))
        return out

    def start(ins, outs, sems):
        for cp in local_copies(ins, outs, sems) + remote_copies(ins, outs, sems, False):
            cp.start()

    def pass_on(ins, outs, sems):
        pass

    def finish(ins, outs, sems):
        for send in remote_copies(ins, outs, sems, False):
            send.wait_send()
        for arrival in remote_copies(ins, outs, sems, True):
            arrival.wait_recv()
        for cp in local_copies(ins, outs, sems):
            cp.wait()

    return _CommPlan(
        arrays, [jax.ShapeDtypeStruct(a.shape if s else (N_DEV,) + a.shape, a.dtype) for a, s in zip(arrays, scatter)],
        [pltpu.SemaphoreType.DMA((n_arr * n_rel,)), pltpu.SemaphoreType.DMA((n_arr * n_rel,)),
         pltpu.SemaphoreType.DMA((n_arr,))],
        (start, pass_on, finish))


def _chip_exchange_plan(arrays, scatter):
    n_arr = len(arrays)
    n_rel = len(_CHIP_RELATIONS)

    def local_copies(ins, outs, sems):
        me = 2 * lax.axis_index("x") + lax.axis_index("y")
        return [pltpu.make_async_copy(ins[a].at[me] if scatter[a] else ins[a], outs[a].at[me], sems[2].at[a])
                for a in range(n_arr)]

    def remote_copies(ins, outs, sems, arrivals):
        x, y, c = lax.axis_index("x"), lax.axis_index("y"), lax.axis_index("c")
        me = 2 * x + y
        out = []
        for k, (dx, dy, _) in enumerate(_CHIP_RELATIONS):
            px, py = x ^ dx, y ^ dy
            peer = 2 * px + py
            for a in range(n_arr):
                out.append(pltpu.make_async_remote_copy(
                    src_ref=ins[a].at[peer] if scatter[a] else ins[a], dst_ref=outs[a].at[peer if arrivals else me],
                    send_sem=sems[0].at[a * n_rel + k], recv_sem=sems[1].at[a * n_rel + k],
                    device_id=(x, y, c) if arrivals else (px, py, c), device_id_type=MESH))
        return out

    def start(ins, outs, sems):
        for cp in local_copies(ins, outs, sems) + remote_copies(ins, outs, sems, False):
            cp.start()

    def pass_on(ins, outs, sems):
        pass

    def finish(ins, outs, sems):
        for send in remote_copies(ins, outs, sems, False):
            send.wait_send()
        for arrival in remote_copies(ins, outs, sems, True):
            arrival.wait_recv()
        for cp in local_copies(ins, outs, sems):
            cp.wait()

    out_shape = [jax.ShapeDtypeStruct((N_CHIPS,) + (a.shape[1:] if s else a.shape), a.dtype)
                 for a, s in zip(arrays, scatter)]
    return _CommPlan(
        arrays, out_shape,
        [pltpu.SemaphoreType.DMA((n_arr * n_rel,)), pltpu.SemaphoreType.DMA((n_arr * n_rel,)),
         pltpu.SemaphoreType.DMA((n_arr,))],
        (start, pass_on, finish))


def _adamw(w, gslots, m, v, *, name):
    rows, cols = w.shape
    n_slots = gslots.shape[0]
    tr = _row_tile(rows, 128) if rows % 16 == 0 else rows

    def body(w_ref, g_ref, m_ref, v_ref, go_ref, d_ref, mo_ref, vo_ref):
        g = g_ref[0].astype(F32)
        for s in range(1, n_slots):
            g = g + g_ref[s].astype(F32)
        mn = ADAM_B1 * m_ref[...] + (1.0 - ADAM_B1) * g
        vn = ADAM_B2 * v_ref[...] + (1.0 - ADAM_B2) * (g * g)
        go_ref[...] = g
        mo_ref[...] = mn
        vo_ref[...] = vn
        m_hat = mn / (1.0 - ADAM_B1 ** ADAM_STEP)
        v_hat = vn / (1.0 - ADAM_B2 ** ADAM_STEP)
        d_ref[...] = -ADAM_LR * (m_hat / (jnp.sqrt(v_hat) + ADAM_EPS) + ADAM_WD * w_ref[...])

    blk = pl.BlockSpec((tr, cols), lambda i: (i, 0))
    shp = jax.ShapeDtypeStruct((rows, cols), F32)
    return pl.pallas_call(
        body, name=name, grid=(rows // tr,),
        in_specs=[blk, pl.BlockSpec((n_slots, tr, cols), lambda i: (0, i, 0)), blk, blk],
        out_specs=[blk] * 4, out_shape=[shp] * 4,
        compiler_params=_cparams("parallel"))(w, gslots, m, v)


_BIG = ("w_in", "w_ssd_branch", "w_attn_branch", "w_out", "w_ffn_in", "w_ffn_out")
_SMALL_SHARDED = ("meta_tokens", "ssd_conv_w", "gate_b", "ffn_conv_w")
_SMALL_REPLICATED = ("norm_mix_w", "ssd_conv_b", "ssd_dt_bias", "ssd_a_log", "ssd_d", "ssd_norm_w", "attn_sinks",
                     "rel_bias", "norm_ffn_w", "ffn_conv_b", "norm_final_w")
_WEIGHTS = ("meta_tokens", "norm_mix_w", "w_in", "ssd_conv_w", "ssd_conv_b", "ssd_dt_bias", "ssd_a_log", "ssd_d",
            "ssd_norm_w", "w_ssd_branch", "w_attn_branch", "attn_sinks", "rel_bias", "gate_b", "w_out", "norm_ffn_w",
            "w_ffn_in", "ffn_conv_w", "ffn_conv_b", "w_ffn_out", "norm_final_w")
_ROW_SHARDED = ("w_ssd_branch", "w_attn_branch", "w_out", "w_ffn_out")
_COL_SHARDED = ("w_in", "w_ffn_in", "meta_tokens", "ssd_conv_w", "gate_b", "ffn_conv_w")
_IN_SEGS = (("z", SSD_INNER), ("xbc", SSD_XBC), ("dt", SSD_HEADS), ("qkv", ATT_Q + 2 * ATT_KV), ("g", 2 * D_MODEL))


def _pack_rows(flat_parts, width, row_mult):
    flat = jnp.concatenate([p.reshape(-1) for p in flat_parts])
    pad = (-flat.shape[0]) % (width * row_mult)
    if pad:
        flat = jnp.concatenate([flat, jnp.zeros((pad,), flat.dtype)])
    return flat.reshape(-1, width)


def _unpack(flat, shapes):
    out, off = [], 0
    for shp in shapes:
        size = int(np.prod(shp))
        out.append(flat[off:off + size].reshape(shp))
        off += size
    return out


def _gather_full(stack, name, shard_shape):
    if name in _COL_SHARDED:
        return jnp.transpose(stack, (1, 0, 2)).reshape(shard_shape[0], N_DEV * shard_shape[1])
    return stack.reshape(N_DEV * shard_shape[0], shard_shape[1])


_IN_SEG_W = {"z": SSD_INNER, "xbc": SSD_XBC, "dt": DT_W, "qkv": ATT_Q + 2 * ATT_KV, "g": 2 * D_MODEL}
_IN_SHARD_W = (SSD_INNER + SSD_XBC + SSD_HEADS + ATT_Q + 2 * ATT_KV + 2 * D_MODEL) // N_DEV
_FFN_SHARD_W = 2 * D_FF // N_DEV


def _in_seg_runs():
    runs, off = [], 0
    for nm, width in _IN_SEGS:
        if nm == "dt":
            runs.append([(off + SSD_HPG * g, SSD_HPG, LANES * g) for g in range(SSD_GROUPS)])
        else:
            runs.append([(off, width, 0)])
        off += width
    return runs


def _w_in_to_segments(stack):
    pieces = [(seg, None, scol, 0, dev, col, w) for seg, scol, dev, col, w in _shard_pieces(_in_seg_runs(), _IN_SHARD_W)]
    outs = [((D_MODEL, _IN_SEG_W[nm]), stack.dtype) for nm, _ in _IN_SEGS]
    return dict(zip([nm for nm, _ in _IN_SEGS], _col_move([stack], outs, pieces, name="w_in_segments")))


def _segments_to_w_in_shards(seg_grads):
    pieces = [(0, dev, col, seg, None, scol, w) for seg, scol, dev, col, w in _shard_pieces(_in_seg_runs(), _IN_SHARD_W)]
    return _col_move(seg_grads, [((N_DEV, D_MODEL, _IN_SHARD_W), seg_grads[0].dtype)], pieces, name="g_w_in_shards")[0]


def _ffn_in_from_shards(stack):
    pieces = [(0, None, scol, 0, dev, col, w)
              for _, scol, dev, col, w in _shard_pieces([[(0, 2 * D_FF, 0)]], _FFN_SHARD_W)]
    return _col_move([stack], [((D_MODEL, 2 * D_FF), stack.dtype)], pieces, name="w_ffn_in_full")[0]


def _ffn_in_to_shards(g_up, g_gate):
    pieces = [(0, dev, col, seg, None, scol, w)
              for seg, scol, dev, col, w in _shard_pieces([[(0, D_FF, 0)], [(D_FF, D_FF, 0)]], _FFN_SHARD_W)]
    return _col_move([g_up, g_gate], [((N_DEV, D_MODEL, _FFN_SHARD_W), g_up.dtype)], pieces, name="g_w_ffn_in_shards")[0]


def _dt_spread(w_dt):
    k = w_dt.shape[0]
    w4 = w_dt.reshape(k, SSD_GROUPS, SSD_HPG)
    return jnp.pad(w4, ((0, 0), (0, 0), (0, LANES - SSD_HPG))).reshape(k, DT_W)


def _dt_gather(w_wide):
    k = w_wide.shape[0]
    return w_wide.reshape(k, SSD_GROUPS, LANES)[:, :, :SSD_HPG].reshape(k, SSD_HEADS)


class _LateExchanges:
    def __init__(self, two_d, shape2):
        self.two_d, self.shape2 = two_d, shape2
        self.early_grads_received = None
        self.w_in_grads_received = None

    def row_pack(self, tree):
        return jnp.concatenate([tree[k] for k in _ROW_SHARDED], axis=0)

    def late_weights_plan(self):
        return _gather_plan([self.two_d["w_ffn_in"].astype(BF16), self.row_pack(self.two_d).astype(BF16)])

    def late_weights(self, gathered):
        w_ffn_in_all, rows_all = [g.reshape((N_DEV,) + g.shape[2:]) for g in gathered]
        out = {"w_ffn_in": _ffn_in_from_shards(w_ffn_in_all)}
        off = 0
        for k in _ROW_SHARDED:
            r = self.shape2[k][0]
            out[k] = rows_all[:, off:off + r].reshape(N_DEV * r, D_MODEL)
            off += r
        return out

    def early_grads_plan(self, grads):
        rows_send = jnp.concatenate([grads[k].reshape(N_DEV, self.shape2[k][0], D_MODEL) for k in _ROW_SHARDED], axis=1)
        return _all_to_all_plan([_ffn_in_to_shards(*grads["w_ffn_in"]), rows_send])

    def w_in_grads_plan(self, seg_grads):
        return _all_to_all_plan([_segments_to_w_in_shards(seg_grads)])


def _local_step(x, target, w, exchanges=None):
    h0 = jnp.concatenate([jnp.zeros((PAD, D_MODEL), F32), w["meta_tokens"], x], axis=0)
    segs = w["in_segs"]

    dtb = _dt_spread(w["ssd_dt_bias"])
    alog = _dt_spread(w["ssd_a_log"])
    dskip_w = jnp.repeat(w["ssd_d"], SSD_HEADDIM, axis=1)
    sinks = jnp.pad(w["attn_sinks"], ((0, 0), (0, LANES - ATT_HEADS)))
    onehot_t = _onehot_t()
    tables = jnp.transpose(_bias_tables(w["rel_bias"].T, onehot_t).reshape(ATT_HEADS, 3, N_KEYS, BLOCK), (1, 0, 2, 3))

    u = _rms_fwd(h0, w["norm_mix_w"], name="rms_mix_fwd")
    z = _mm(u, segs["z"], name="in_z")
    xbc = _mm(u, segs["xbc"], name="in_xbc")
    dt_raw = _mm(u, segs["dt"], name="in_dt")
    qkv = _mm(u, segs["qkv"], out_dtype=BF16, name="in_qkv")
    gates = _mm(u, segs["g"], name="in_g")
    pre = _conv_fwd(xbc, w["ssd_conv_w"], w["ssd_conv_b"], name="ssd_conv_fwd")
    (y, yn, hsave), gathered = _ssd_fwd(pre, dt_raw, z, dtb, alog, dskip_w, w["ssd_norm_w"],
                                        side=None if exchanges is None else exchanges.late_weights_plan())
    if exchanges is not None:
        w = {**w, **exchanges.late_weights(gathered)}
    w_ffn_up, w_ffn_gate = w["w_ffn_in"][:, :D_FF], w["w_ffn_in"][:, D_FF:]
    y_ssd = _mm(yn, w["w_ssd_branch"], out_dtype=BF16, name="ssd_out")
    att = _attn_fwd(qkv, tables, sinks)
    y_att = _mm(att, w["w_attn_branch"], out_dtype=BF16, name="att_out")
    merged, h1 = _merge_out_fwd(gates, y_ssd, y_att, w["gate_b"], w["w_out"], h0)
    u2 = _rms_fwd(h1, w["norm_ffn_w"], name="rms_ffn_fwd")
    hid_raw = _mm(u2, w["w_ffn_in"], name="ffn_in")
    hid_up, hid_gate, act = _ffn_act_fwd(hid_raw, w["ffn_conv_w"], w["ffn_conv_b"])
    h2 = _mm(act, w["w_ffn_out"], c=h1, mask=True, name="ffn_out")
    dh2, loss_row, g_norm_final = _final_loss(h2, w["norm_final_w"], target)

    grads = {"norm_final_w": g_norm_final}
    dact = _mm(dh2, w["w_ffn_out"], tb=True, mask=True, name="d_act")
    grads["w_ffn_out"] = _mm(act, dh2, ta=True, mask=True, out_dtype=BF16, name="g_w_ffn_out")
    dx_up, dx_gate, dcw_up, dcw_gate, dcb_up, dcb_gate = _ffn_act_bwd(dact, hid_up, hid_gate, hid_raw, w["ffn_conv_w"])
    grads["ffn_conv_w"] = jnp.concatenate([dcw_up, dcw_gate], axis=1)
    grads["ffn_conv_b"] = jnp.concatenate([dcb_up, dcb_gate], axis=1)
    (dh1, grads["norm_ffn_w"]), _ = _mm_rms_bwd([(dx_up, w_ffn_up), (dx_gate, w_ffn_gate)], h1, w["norm_ffn_w"], dh2,
                                                name="d_u2_rms_bwd")
    grads["w_ffn_in"] = (_mm(u2, dx_up, ta=True, out_dtype=BF16, name="g_w_ffn_up"),
                         _mm(u2, dx_gate, ta=True, out_dtype=BF16, name="g_w_ffn_gate"))

    grads["w_out"] = _mm(merged, dh1, ta=True, mask=True, out_dtype=BF16, name="g_w_out")
    dy_ssd, dy_att, dgates, grads["gate_b"] = _merge_out_bwd(dh1, w["w_out"], gates, y_ssd, y_att, w["gate_b"])
    dyn = _mm(dy_ssd, w["w_ssd_branch"], tb=True, name="d_yn")
    grads["w_ssd_branch"] = _mm(yn, dy_ssd, ta=True, out_dtype=BF16, name="g_w_ssd")
    datt = _mm(dy_att, w["w_attn_branch"], tb=True, out_dtype=BF16, name="d_att")
    grads["w_attn_branch"] = _mm(att, dy_att, ta=True, out_dtype=BF16, name="g_w_att")
    (dz, dpxs, dpb, dpc, ddt, grads["ssd_norm_w"], g_dtb, g_alog, g_dskip), received = _ssd_bwd(
        dyn, y, z, pre, dt_raw, hsave, dtb, alog, dskip_w, w["ssd_norm_w"],
        side=None if exchanges is None else exchanges.early_grads_plan(grads))
    if exchanges is not None:
        exchanges.early_grads_received = received
    grads["ssd_dt_bias"] = _dt_gather(g_dtb)
    grads["ssd_a_log"] = _dt_gather(g_alog)
    grads["ssd_d"] = _dt_gather(g_dskip)
    conv_g = _conv_bwd(dpxs, xbc, w["ssd_conv_w"], name="ssd_conv_bwd_x")
    conv_g = _conv_bwd(dpb, xbc, w["ssd_conv_w"], name="ssd_conv_bwd_b", col0=SSD_INNER, into=conv_g)
    dxbc, grads["ssd_conv_w"], grads["ssd_conv_b"] = _conv_bwd(
        dpc, xbc, w["ssd_conv_w"], name="ssd_conv_bwd_c", col0=SSD_INNER + SSD_BC, into=conv_g)
    dqkv, d_tables, d_sinks = _attn_bwd(datt, qkv, tables, sinks)
    grads["attn_sinks"] = d_sinks[:, :ATT_HEADS]
    dtab = jnp.transpose(d_tables, (1, 0, 2, 3)).reshape(ATT_HEADS, NT_ALL)
    grads["rel_bias"] = _bias_grad(dtab, onehot_t).T
    dsegs = {"z": dz, "xbc": dxbc, "dt": ddt, "qkv": dqkv, "g": dgates}
    grads["in_segs"] = [_mm(u, dsegs[nm], ta=True, out_dtype=BF16, name="g_w_in_" + nm) for nm, _ in _IN_SEGS]
    (dh0, grads["norm_mix_w"]), received = _mm_rms_bwd(
        [(dsegs[nm], segs[nm]) for nm, _ in _IN_SEGS], h0, w["norm_mix_w"], dh1, name="d_u_rms_bwd",
        side=None if exchanges is None else exchanges.w_in_grads_plan(grads["in_segs"]))
    if exchanges is not None:
        exchanges.w_in_grads_received = received[0]
    grads["meta_tokens"] = dh0[PAD:BLOCK]
    return loss_row[0, 0], dh0[BLOCK:], grads


def kernel(x, meta_tokens, norm_mix_w, w_in, ssd_conv_w, ssd_conv_b, ssd_dt_bias, ssd_a_log, ssd_d, ssd_norm_w, w_ssd_branch, w_attn_branch, attn_sinks, rel_bias, gate_b, w_out, norm_ffn_w, w_ffn_in, ffn_conv_w, ffn_conv_b, w_ffn_out, norm_final_w, loss_target, m_meta_tokens, m_norm_mix_w, m_w_in, m_ssd_conv_w, m_ssd_conv_b, m_ssd_dt_bias, m_ssd_a_log, m_ssd_d, m_ssd_norm_w, m_w_ssd_branch, m_w_attn_branch, m_attn_sinks, m_rel_bias, m_gate_b, m_w_out, m_norm_ffn_w, m_w_ffn_in, m_ffn_conv_w, m_ffn_conv_b, m_w_ffn_out, m_norm_final_w, v_meta_tokens, v_norm_mix_w, v_w_in, v_ssd_conv_w, v_ssd_conv_b, v_ssd_dt_bias, v_ssd_a_log, v_ssd_d, v_ssd_norm_w, v_w_ssd_branch, v_w_attn_branch, v_attn_sinks, v_rel_bias, v_gate_b, v_w_out, v_norm_ffn_w, v_w_ffn_in, v_ffn_conv_w, v_ffn_conv_b, v_w_ffn_out, v_norm_final_w):
    shard = dict(meta_tokens=meta_tokens, norm_mix_w=norm_mix_w, w_in=w_in, ssd_conv_w=ssd_conv_w,
                 ssd_conv_b=ssd_conv_b, ssd_dt_bias=ssd_dt_bias, ssd_a_log=ssd_a_log, ssd_d=ssd_d,
                 ssd_norm_w=ssd_norm_w, w_ssd_branch=w_ssd_branch, w_attn_branch=w_attn_branch,
                 attn_sinks=attn_sinks, rel_bias=rel_bias, gate_b=gate_b, w_out=w_out, norm_ffn_w=norm_ffn_w,
                 w_ffn_in=w_ffn_in, ffn_conv_w=ffn_conv_w, ffn_conv_b=ffn_conv_b, w_ffn_out=w_ffn_out,
                 norm_final_w=norm_final_w)
    mom_m = dict(zip(_WEIGHTS, (m_meta_tokens, m_norm_mix_w, m_w_in, m_ssd_conv_w, m_ssd_conv_b, m_ssd_dt_bias,
                                m_ssd_a_log, m_ssd_d, m_ssd_norm_w, m_w_ssd_branch, m_w_attn_branch, m_attn_sinks,
                                m_rel_bias, m_gate_b, m_w_out, m_norm_ffn_w, m_w_ffn_in, m_ffn_conv_w, m_ffn_conv_b,
                                m_w_ffn_out, m_norm_final_w)))
    mom_v = dict(zip(_WEIGHTS, (v_meta_tokens, v_norm_mix_w, v_w_in, v_ssd_conv_w, v_ssd_conv_b, v_ssd_dt_bias,
                                v_ssd_a_log, v_ssd_d, v_ssd_norm_w, v_w_ssd_branch, v_w_attn_branch, v_attn_sinks,
                                v_rel_bias, v_gate_b, v_w_out, v_norm_ffn_w, v_w_ffn_in, v_ffn_conv_w, v_ffn_conv_b,
                                v_w_ffn_out, v_norm_final_w)))
    orig_shape = {k: a.shape for k, a in shard.items()}
    two_d = {k: a.reshape(a.shape[-2:]) if a.ndim >= 2 else a.reshape(1, -1) for k, a in shard.items()}
    shape2 = {k: a.shape for k, a in two_d.items()}

    def as2d(tree):
        return {k: tree[k].reshape(shape2[k]) for k in _WEIGHTS}

    mom_m, mom_v = as2d(mom_m), as2d(mom_v)

    exchanges = _LateExchanges(two_d, shape2)
    row_pack = exchanges.row_pack
    small_pack = _pack_rows([two_d[k] for k in _SMALL_SHARDED], LANES, SMALL_ROW_MULT)
    w_in_all, small_all = _gather_two_level([two_d["w_in"].astype(BF16), small_pack], name="gather_weights")
    full = {k: two_d[k] for k in _SMALL_REPLICATED}
    full["in_segs"] = _w_in_to_segments(w_in_all)
    small_flat = small_all.reshape(N_DEV, -1)
    off = 0
    for k in _SMALL_SHARDED:
        size = int(np.prod(shape2[k]))
        full[k] = _gather_full(small_flat[:, off:off + size].reshape((N_DEV,) + shape2[k]), k, shape2[k])
        off += size

    loss_local, grad_x, grads = _local_step(x[0], loss_target[0], full, exchanges)

    small_names = _SMALL_SHARDED + _SMALL_REPLICATED
    small_send = _pack_rows([grads[k] for k in small_names] + [loss_local.reshape(1)], LANES, SMALL_ROW_MULT)
    small_recv, = _run_plan(_all_to_all_plan([small_send], [False]), "exchange_small_grads")
    in_recv = exchanges.w_in_grads_received
    ffn_recv, rows_recv = exchanges.early_grads_received

    big = {"w_in": _adamw(two_d["w_in"], in_recv, mom_m["w_in"], mom_v["w_in"], name="adamw_w_in"),
           "w_ffn_in": _adamw(two_d["w_ffn_in"], ffn_recv, mom_m["w_ffn_in"], mom_v["w_ffn_in"], name="adamw_w_ffn_in")}
    rows_out = _adamw(row_pack(two_d), rows_recv, row_pack(mom_m), row_pack(mom_v), name="adamw_rows")
    off = 0
    for k in _ROW_SHARDED:
        r = shape2[k][0]
        big[k] = [a[off:off + r] for a in rows_out]
        off += r
    me =4 * lax.axis_index("x") + 2 * lax.axis_index("y") + lax.axis_index("c")
    small_full_shapes = [grads[k].shape for k in small_names]
    n_small = sum(int(np.prod(s)) for s in small_full_shapes)

    def packed_small(tree):
        parts = []
        for k in small_names:
            a = tree[k]
            if k in _SMALL_SHARDED:
                fullw = jnp.zeros(grads[k].shape, F32)
                a = lax.dynamic_update_slice(fullw, a, (0, me * a.shape[1]))
            parts.append(a)
        return _pack_rows(parts + [jnp.zeros((1,), F32)], LANES, SMALL_ROW_MULT)

    g_small, d_small, m_small, v_small = _adamw(packed_small(two_d), small_recv, packed_small(mom_m),
                                                packed_small(mom_v), name="adamw_small")

    def unpack_all(which, small):
        out = {k: big[k][which] for k in _BIG}
        flat = small.reshape(-1)
        for k, a in zip(small_names, _unpack(flat, small_full_shapes)):
            if k in _SMALL_SHARDED:
                a = lax.dynamic_slice(a, (0, me * shape2[k][1]), shape2[k])
            out[k] = a
        return out, flat[n_small]

    g_all, loss = unpack_all(0, g_small)
    d_all, _ = unpack_all(1, d_small)
    m_all, _ = unpack_all(2, m_small)
    v_all, _ = unpack_all(3, v_small)

    def final(tree):
        return [tree[k].reshape(orig_shape[k]) for k in _WEIGHTS]

    return (loss, grad_x[None], *final(g_all), *final(d_all), *final(m_all), *final(v_all))
```

```python
import functools
import math

import numpy as np
import jax
import jax.numpy as jnp
from jax import lax
from jax.experimental import pallas as pl
from jax.experimental.pallas import tpu as pltpu

F32 = jnp.float32
BF16 = jnp.bfloat16
HIGHEST = lax.Precision.HIGHEST

D_MODEL = 1024
N_META = 16
BLOCK = 128
PAD = BLOCK - N_META
EPS = 1e-6
NEG = -1e30
SSD_INNER = 2 * D_MODEL
SSD_HEADDIM = 64
SSD_HEADS = SSD_INNER // SSD_HEADDIM
SSD_GROUPS = 4
SSD_HPG = SSD_HEADS // SSD_GROUPS
SSD_STATE = 128
SSD_CONV = 4
SSD_GW = SSD_HPG * SSD_HEADDIM
SSD_BC = SSD_GROUPS * SSD_STATE
SSD_XBC = SSD_INNER + 2 * SSD_BC
ATT_HEADS = 16
ATT_KV_HEADS = 2
ATT_HEADDIM = 64
ATT_GQ = ATT_HEADS // ATT_KV_HEADS
ATT_Q = ATT_HEADS * ATT_HEADDIM
ATT_KV = ATT_KV_HEADS * ATT_HEADDIM
REL_BUCKETS = 32
REL_MAX_DIST = 128
D_FF = 2816
FFN_CONV = 3
ADAM_LR = 0.001
ADAM_B1 = 0.9
ADAM_B2 = 0.999
ADAM_EPS = 1e-08
ADAM_WD = 0.01
ADAM_STEP = 10

N_DEV = 8
LANES = 128
SUBLANES = 8
DT_W = SSD_GROUPS * LANES
VMEM_LIMIT_BYTES = 56 * 1024 * 1024
MESH = pl.DeviceIdType.MESH

SMALL_ROW_MULT = 16

N_KEYS = 3 * BLOCK
NT_ALL = 3 * N_KEYS * BLOCK
NT_TILE = 8192


def _cparams(*sem):
    return pltpu.CompilerParams(dimension_semantics=sem, vmem_limit_bytes=VMEM_LIMIT_BYTES)


def _row_tile(n, cap):
    best = None
    for t in range(16, min(n, cap) + 1, 16):
        if n % t == 0:
            best = t
    return best or n


def _col_tile(n, cap):
    for t in (1408, 1280, 1024, 768, 640, 512, 384, 256, 128):
        if t <= cap and n % t == 0:
            return t
    return n


def _sigmoid(x):
    return 0.5 * jnp.tanh(0.5 * x) + 0.5


def _silu(x):
    return x * _sigmoid(x)


def _softplus(x):
    return jnp.maximum(x, 0.0) + jnp.log(1.0 + jnp.exp(-jnp.abs(x)))


def _dot_nt(a, b):
    return lax.dot_general(a, b, (((1,), (1,)), ((), ())), preferred_element_type=F32)


def _dot_tn(a, b):
    return lax.dot_general(a, b, (((0,), (0,)), ((), ())), preferred_element_type=F32)


def _dot(a, b):
    return jnp.dot(a, b, preferred_element_type=F32)


def _bf16_terms(x, terms):
    out, rest = [], x
    for _ in range(terms):
        part = rest.astype(BF16)
        out.append(part)
        rest = rest - part.astype(F32)
    return out


def _dot_sel(x, sel, terms=3):
    return sum(_dot(part, sel) for part in _bf16_terms(x, terms))


def _sel_dot(sel, x, terms=3):
    return sum(_dot(sel, part) for part in _bf16_terms(x, terms))


def _sum_all(x):
    return jnp.sum(jnp.sum(x, axis=1, keepdims=True), axis=0, keepdims=True)


MM_ROW_CAPS = (2080, 1664, 832, 416)
MM_COL_CAP = 1408
MM_VMEM_BUDGET = 44 * 1024 * 1024


def _mm_tiles(rows, cols, vmem_bytes):
    col_cands = [t for t in (2048, 1536, 1408, 1280, 1024, 768, 640, 512, 384, 256, 128) if cols % t == 0]
    if cols <= 2 * MM_COL_CAP:
        col_cands.append(cols)
    best = None
    for cap in MM_ROW_CAPS:
        tr = _row_tile(rows, cap)
        for tc in col_cands:
            if vmem_bytes(tr, tc) <= MM_VMEM_BUDGET and (best is None or tr * tc > best[0] * best[1]):
                best = (tr, tc)
    assert best is not None, (rows, cols)
    return best


def _mm(a, b, *, name, ta=False, tb=False, c=None, mask=False, out_dtype=F32):
    if not ta:
        m, k = a.shape
        n = b.shape[0] if tb else b.shape[1]
        tm, tn = _mm_tiles(m, n, lambda t_m, t_n: 2 * (t_m * k * a.dtype.itemsize + k * t_n * b.dtype.itemsize
                                                       + t_m * t_n * (jnp.dtype(out_dtype).itemsize
                                                                      + (0 if c is None else c.dtype.itemsize)))
                           + 4 * t_m * t_n)

        def body(*refs):
            if c is None:
                a_ref, b_ref, o_ref = refs
            else:
                a_ref, b_ref, c_ref, o_ref = refs
            acc = (_dot_nt if tb else _dot)(a_ref[...].astype(BF16), b_ref[...].astype(BF16))
            if mask:
                row = pl.program_id(0) * tm + lax.broadcasted_iota(jnp.int32, (tm, 1), 0)
                acc = jnp.where(row >= PAD, acc, 0.0)
            if c is not None:
                acc = acc + c_ref[...]
            o_ref[...] = acc.astype(out_dtype)

        b_spec = pl.BlockSpec((tn, k), lambda i, j: (j, 0)) if tb else pl.BlockSpec((k, tn), lambda i, j: (0, j))
        in_specs = [pl.BlockSpec((tm, k), lambda i, j: (i, 0)), b_spec]
        args = [a, b]
        if c is not None:
            in_specs.append(pl.BlockSpec((tm, tn), lambda i, j: (i, j)))
            args.append(c)
        return pl.pallas_call(
            body, name=name, grid=(m // tm, n // tn), in_specs=in_specs,
            out_specs=pl.BlockSpec((tm, tn), lambda i, j: (i, j)),
            out_shape=jax.ShapeDtypeStruct((m, n), out_dtype),
            compiler_params=_cparams("parallel", "parallel"))(*args)

    kc, m = a.shape
    n = b.shape[1]
    tm = _col_tile(m, MM_COL_CAP)
    tk, tn = _mm_tiles(kc, n, lambda t_k, t_n: 2 * (t_k * tm * a.dtype.itemsize + t_k * t_n * b.dtype.itemsize
                                                    + tm * t_n * jnp.dtype(out_dtype).itemsize) + 8 * tm * t_n)

    n_k = kc // tk

    def body_t(a_ref, b_ref, o_ref, acc_ref):
        kk = pl.program_id(2)
        bb = b_ref[...]
        if mask:
            row = kk * tk + lax.broadcasted_iota(jnp.int32, (tk, 1), 0)
            bb = jnp.where(row >= PAD, bb, jnp.zeros_like(bb))
        p = _dot_tn(a_ref[...].astype(BF16), bb.astype(BF16))

        @pl.when(kk == 0)
        def _():
            acc_ref[...] = p

        @pl.when(kk > 0)
        def _():
            acc_ref[...] += p

        @pl.when(kk == n_k - 1)
        def _():
            o_ref[...] = acc_ref[...].astype(out_dtype)

    return pl.pallas_call(
        body_t, name=name, grid=(m // tm, n // tn, n_k),
        in_specs=[pl.BlockSpec((tk, tm), lambda i, j, kk: (kk, i)), pl.BlockSpec((tk, tn), lambda i, j, kk: (kk, j))],
        out_specs=pl.BlockSpec((tm, tn), lambda i, j, kk: (i, j)),
        out_shape=jax.ShapeDtypeStruct((m, n), out_dtype),
        scratch_shapes=[pltpu.VMEM((tm, tn), F32)],
        compiler_params=_cparams("parallel", "parallel", "arbitrary"))(a, b)


def _mm_rms_bwd(pairs, x, w, dres, *, name, side=None):
    m, d = x.shape
    tm = _row_tile(m, 416)
    n_pairs = len(pairs)

    def body(*refs):
        a_refs, b_refs = refs[:n_pairs], refs[n_pairs:2 * n_pairs]
        x_ref, w_ref, dres_ref, dx_ref, dw_ref = refs[2 * n_pairs:]
        i = pl.program_id(0)
        dyv = None
        for a_ref, b_ref in zip(a_refs, b_refs):
            term = _dot_nt(a_ref[...].astype(BF16), b_ref[...])
            dyv = term if dyv is None else dyv + term
        xv = x_ref[...]
        r = lax.rsqrt(jnp.mean(xv * xv, axis=-1, keepdims=True) + EPS)
        xh = xv * r
        g = dyv * w_ref[...]
        dx_ref[...] = r * (g - xh * jnp.mean(g * xh, axis=-1, keepdims=True)) + dres_ref[...]
        part = jnp.sum(dyv * xh, axis=0, keepdims=True)

        @pl.when(i == 0)
        def _():
            dw_ref[...] = part

        @pl.when(i > 0)
        def _():
            dw_ref[...] += part

    row = pl.BlockSpec((tm, d), lambda i: (i, 0))
    vec = pl.BlockSpec((1, d), lambda i: (0, 0))
    in_specs = ([pl.BlockSpec((tm, a.shape[1]), lambda i: (i, 0)) for a, _ in pairs]
                + [pl.BlockSpec(b.shape, lambda i: (0, 0), pipeline_mode=pl.Buffered(1)) for _, b in pairs]
                + [row, vec, row])
    return _call_with_side(
        body, side, name=name, grid=(m // tm,), in_specs=in_specs, out_specs=[row, vec],
        out_shape=[jax.ShapeDtypeStruct((m, d), F32), jax.ShapeDtypeStruct((1, d), F32)], scratch_shapes=[],
        args=[a for a, _ in pairs] + [b for _, b in pairs] + [x, w, dres], semantics=("arbitrary",))


def _rms_fwd(h, w, *, name):
    n, d = h.shape
    tm = _row_tile(n, 832)

    def body(h_ref, w_ref, o_ref):
        x = h_ref[...]
        r = lax.rsqrt(jnp.mean(x * x, axis=-1, keepdims=True) + EPS)
        o_ref[...] = (x * r * w_ref[...]).astype(BF16)

    return pl.pallas_call(
        body, name=name, grid=(n // tm,),
        in_specs=[pl.BlockSpec((tm, d), lambda i: (i, 0)), pl.BlockSpec((1, d), lambda i: (0, 0))],
        out_specs=pl.BlockSpec((tm, d), lambda i: (i, 0)),
        out_shape=jax.ShapeDtypeStruct((n, d), BF16),
        compiler_params=_cparams("parallel"))(h, w)


def _final_loss(h, w, target):
    n, d = h.shape
    nb = n // BLOCK

    def body(h_ref, w_ref, t_ref, dh_ref, loss_ref, dw_ref):
        i = pl.program_id(0)
        xv = h_ref[...]
        r = lax.rsqrt(jnp.mean(xv * xv, axis=-1, keepdims=True) + EPS)
        xh = xv * r
        wv = w_ref[...]
        err = jnp.where(i >= 1, xh * wv - t_ref[...], 0.0)
        dyv = err * (1.0 / d)
        g = dyv * wv
        dh_ref[...] = r * (g - xh * jnp.mean(g * xh, axis=-1, keepdims=True))
        lpart = jnp.broadcast_to(0.5 * _sum_all(err * err) * (1.0 / d), (1, LANES))
        wpart = jnp.sum(dyv * xh, axis=0, keepdims=True)

        @pl.when(i == 0)
        def _():
            loss_ref[...] = lpart
            dw_ref[...] = wpart

        @pl.when(i > 0)
        def _():
            loss_ref[...] += lpart
            dw_ref[...] += wpart

    row = pl.BlockSpec((BLOCK, d), lambda i: (i, 0))
    vec = pl.BlockSpec((1, d), lambda i: (0, 0))
    return pl.pallas_call(
        body, name="final_loss", grid=(nb,),
        in_specs=[row, vec, pl.BlockSpec((BLOCK, d), lambda i: (jnp.maximum(i - 1, 0), 0))],
        out_specs=[row, pl.BlockSpec((1, LANES), lambda i: (0, 0)), vec],
        out_shape=[jax.ShapeDtypeStruct((n, d), F32), jax.ShapeDtypeStruct((1, LANES), F32),
                   jax.ShapeDtypeStruct((1, d), F32)],
        compiler_params=_cparams("arbitrary"))(h, w, target)


def _main_spec(tm, cb, off=0):
    return pl.BlockSpec((tm, cb), lambda j, i: (i, j + off))


def _prev_spec(tm, cb, off=0):
    r8 = tm // SUBLANES
    return pl.BlockSpec((SUBLANES, cb), lambda j, i: (jnp.maximum(i * r8 - 1, 0), j + off))


def _next_spec(tm, cb, n_rows, off=0):
    r8 = tm // SUBLANES
    last = n_rows // SUBLANES - 1
    return pl.BlockSpec((SUBLANES, cb), lambda j, i: (jnp.minimum((i + 1) * r8, last), j + off))


def _with_prev(prev_ref, main_ref, i):
    prev = jnp.where(i > 0, prev_ref[...], 0.0)
    return jnp.concatenate([prev, main_ref[...]], axis=0)


def _with_next(main, nxt, i, n_tiles):
    return jnp.concatenate([main, jnp.where(i < n_tiles - 1, nxt, 0.0)], axis=0)


def _back(xx, s, tm):
    if s == 0:
        return xx[SUBLANES:SUBLANES + tm]
    return pltpu.roll(xx, s, 0)[SUBLANES:SUBLANES + tm]


def _ahead(xx, s, tm):
    if s == 0:
        return xx[:tm]
    return pltpu.roll(xx, tm + SUBLANES - s, 0)[:tm]


def _mm_conv_fwd(u, w_in, w, b, *, name):
    n = u.shape[0]
    cdim = w_in.shape[1]
    kw = w.shape[0]
    tm = _row_tile(n, 832)
    cb = _col_tile(cdim, 512)
    nt = n // tm

    def body(u_ref, w_in_ref, w_ref, b_ref, x_ref, o_ref, acc_scr, halo_scr):
        j, i = pl.program_id(0), pl.program_id(1)

        @pl.when((j == 0) & (i == 0))
        def _():
            acc_scr[...] = jnp.zeros_like(acc_scr)
            halo_scr[...] = jnp.zeros_like(halo_scr)

        new = _dot(u_ref[...], w_in_ref[...])
        prev = acc_scr[...]
        xx = jnp.concatenate([jnp.where(i >= 2, halo_scr[...], 0.0), prev], axis=0)
        acc = jnp.broadcast_to(b_ref[...], (tm, cb))
        for k in range(kw):
            acc = acc + w_ref[k:k + 1, :] * _back(xx, kw - 1 - k, tm)
        x_ref[...] = prev
        o_ref[...] = acc
        halo_scr[...] = prev[tm - SUBLANES:, :]
        acc_scr[...] = new

    out = pl.BlockSpec((tm, cb), lambda j, i: (jnp.maximum(i - 1, 0), j))
    shp = jax.ShapeDtypeStruct((n, cdim), F32)
    return pl.pallas_call(
        body, name=name, grid=(cdim // cb, nt + 1),
        in_specs=[pl.BlockSpec((tm, u.shape[1]), lambda j, i: (jnp.minimum(i, nt - 1), 0)),
                  pl.BlockSpec((w_in.shape[0], cb), lambda j, i: (0, j)),
                  pl.BlockSpec((kw, cb), lambda j, i: (0, j)), pl.BlockSpec((1, cb), lambda j, i: (0, j))],
        out_specs=[out, out], out_shape=[shp, shp],
        scratch_shapes=[pltpu.VMEM((tm, cb), F32), pltpu.VMEM((SUBLANES, cb), F32)],
        compiler_params=_cparams("arbitrary", "arbitrary"))(u, w_in, w, b)


def _conv_bwd_core(dpre_ext, x, w_ref, kw, tm):
    dx = None
    dws = []
    for k in range(kw):
        shifted = _ahead(dpre_ext, kw - 1 - k, tm)
        term = w_ref[k:k + 1, :] * shifted
        dx = term if dx is None else dx + term
        dws.append(jnp.sum(shifted * x, axis=0, keepdims=True))
    return dx, dws, jnp.sum(dpre_ext[:tm], axis=0, keepdims=True)


def _acc_rows(i, dw_ref, db_ref, dws, db):
    @pl.when(i == 0)
    def _():
        for k, v in enumerate(dws):
            dw_ref[k:k + 1, :] = v
        db_ref[...] = db

    @pl.when(i > 0)
    def _():
        for k, v in enumerate(dws):
            dw_ref[k:k + 1, :] += v
        db_ref[...] += db


def _conv_bwd(dpre, x, w, *, name, col0=0, into=None):
    n, cdim = x.shape
    kw = w.shape[0]
    tm = _row_tile(n, 832)
    cb = _col_tile(cdim, 512)
    nt = n // tm
    off = col0 // cb
    n_alias = 0 if into is None else 3

    def body(d_ref, dn_ref, x_ref, w_ref, *rest):
        dx_ref, dw_ref, db_ref = rest[n_alias:]
        i = pl.program_id(1)
        dpre_ext = _with_next(d_ref[...], dn_ref[...], i, nt)
        dx, dws, db = _conv_bwd_core(dpre_ext, x_ref[...], w_ref, kw, tm)
        dx_ref[...] = dx.astype(BF16)
        _acc_rows(i, dw_ref, db_ref, dws, db)

    wspec = pl.BlockSpec((kw, cb), lambda j, i: (0, j + off))
    bspec = pl.BlockSpec((1, cb), lambda j, i: (0, j + off))
    return pl.pallas_call(
        body, name=name, grid=(dpre.shape[1] // cb, nt),
        in_specs=[_main_spec(tm, cb), _next_spec(tm, cb, n), _main_spec(tm, cb, off), wspec]
        + [pl.BlockSpec(memory_space=pl.ANY)] * n_alias,
        out_specs=[_main_spec(tm, cb, off), wspec, bspec],
        out_shape=[jax.ShapeDtypeStruct((n, cdim), BF16), jax.ShapeDtypeStruct((kw, cdim), F32),
                   jax.ShapeDtypeStruct((1, cdim), F32)],
        input_output_aliases={4 + k: k for k in range(n_alias)},
        compiler_params=_cparams("parallel", "arbitrary"))(dpre, dpre, x, w, *(into or ()))


def _ffn_in_act_fwd(u, w_in, w, b):
    n = u.shape[0]
    kw = w.shape[0]
    tm = _row_tile(n, 832)
    cb = _col_tile(D_FF, 256)
    nc = D_FF // cb
    nt = n // tm

    def body(u_ref, wu_in_ref, wg_in_ref, wu_ref, wg_ref, bu_ref, bg_ref,
             xu_ref, xg_ref, hu_ref, hg_ref, act_ref, acc_scr, halo_scr):
        j, i = pl.program_id(0), pl.program_id(1)

        @pl.when((j == 0) & (i == 0))
        def _():
            acc_scr[...] = jnp.zeros_like(acc_scr)
            halo_scr[...] = jnp.zeros_like(halo_scr)

        ub = u_ref[...]
        new = [_dot(ub, wu_in_ref[...]), _dot(ub, wg_in_ref[...])]
        hid = []
        for half, (x_ref, w_ref, b_ref) in enumerate(((xu_ref, wu_ref, bu_ref), (xg_ref, wg_ref, bg_ref))):
            prev = acc_scr[half]
            xx = jnp.concatenate([jnp.where(i >= 2, halo_scr[half], 0.0), prev], axis=0)
            acc = jnp.broadcast_to(b_ref[...], (tm, cb))
            for k in range(kw):
                acc = acc + w_ref[k:k + 1, :] * _back(xx, kw - 1 - k, tm)
            x_ref[...] = prev
            hid.append(acc)
            halo_scr[half] = prev[tm - SUBLANES:, :]
            acc_scr[half] = new[half]
        hu_ref[...] = hid[0]
        hg_ref[...] = hid[1]
        act_ref[...] = (_silu(hid[1]) * hid[0]).astype(BF16)

    def wspec(off):
        return pl.BlockSpec((kw, cb), lambda j, i: (0, j + off))

    def bspec(off):
        return pl.BlockSpec((1, cb), lambda j, i: (0, j + off))

    def in_w(off):
        return pl.BlockSpec((w_in.shape[0], cb), lambda j, i: (0, j + off))

    out = pl.BlockSpec((tm, cb), lambda j, i: (jnp.maximum(i - 1, 0), j))
    f32_out = jax.ShapeDtypeStruct((n, D_FF), F32)
    return pl.pallas_call(
        body, name="ffn_in_act_fwd", grid=(nc, nt + 1),
        in_specs=[pl.BlockSpec((tm, u.shape[1]), lambda j, i: (jnp.minimum(i, nt - 1), 0)), in_w(0), in_w(nc),
                  wspec(0), wspec(nc), bspec(0), bspec(nc)],
        out_specs=[out] * 5,
        out_shape=[f32_out, f32_out, f32_out, f32_out, jax.ShapeDtypeStruct((n, D_FF), BF16)],
        scratch_shapes=[pltpu.VMEM((2, tm, cb), F32), pltpu.VMEM((2, SUBLANES, cb), F32)],
        compiler_params=_cparams("arbitrary", "arbitrary"))(u, w_in, w_in, w, w, b, b)


def _ffn_out_act_bwd(dh, w_out, hu, hg, x_up, x_gate, w):
    n = x_up.shape[0]
    kw = w.shape[0]
    tm = _row_tile(n, 832)
    cb = _col_tile(D_FF, 256)
    nc = D_FF // cb
    nt = n // tm

    def body(dh_ref, wo_ref, hu_ref, hun_ref, hg_ref, hgn_ref, xu_ref, xg_ref, wu_ref, wg_ref,
             dxu_ref, dxg_ref, dwu_ref, dwg_ref, dbu_ref, dbg_ref, acc_scr, halo_scr):
        j, i = pl.program_id(0), pl.program_id(1)

        @pl.when((j == 0) & (i == 0))
        def _():
            acc_scr[...] = jnp.zeros_like(acc_scr)
            halo_scr[...] = jnp.zeros_like(halo_scr)

        tile = jnp.maximum(nt - 1 - i, 0)
        row = tile * tm + lax.broadcasted_iota(jnp.int32, (tm, 1), 0)
        new = jnp.where(row >= PAD, _dot_nt(dh_ref[...].astype(BF16), wo_ref[...]), 0.0)
        prev = jnp.where(i >= 1, acc_scr[...], 0.0)
        dact_e = jnp.concatenate([prev, jnp.where(i >= 2, halo_scr[...], 0.0)], axis=0)
        last = nt - i >= nt - 1
        up_e = jnp.concatenate([hu_ref[...], jnp.where(last, 0.0, hun_ref[...])], axis=0)
        gate_e = jnp.concatenate([hg_ref[...], jnp.where(last, 0.0, hgn_ref[...])], axis=0)
        halo_scr[...] = prev[:SUBLANES, :]
        acc_scr[...] = new
        sg = _sigmoid(gate_e)
        dup_e = dact_e * (gate_e * sg)
        dgate_e = dact_e * up_e * (sg * (1.0 + gate_e * (1.0 - sg)))
        dx, dws, db = _conv_bwd_core(dup_e, xu_ref[...], wu_ref, kw, tm)
        dxu_ref[...] = dx.astype(BF16)
        _acc_rows(i, dwu_ref, dbu_ref, dws, db)
        dx, dws, db = _conv_bwd_core(dgate_e, xg_ref[...], wg_ref, kw, tm)
        dxg_ref[...] = dx.astype(BF16)
        _acc_rows(i, dwg_ref, dbg_ref, dws, db)

    def done_tile(i):
        return jnp.minimum(nt - i, nt - 1)

    r8 = tm // SUBLANES
    main = pl.BlockSpec((tm, cb), lambda j, i: (done_tile(i), j))
    nxt = pl.BlockSpec((SUBLANES, cb), lambda j, i: (jnp.minimum((done_tile(i) + 1) * r8, n // SUBLANES - 1), j))
    wspec0 = pl.BlockSpec((kw, cb), lambda j, i: (0, j))
    wspec1 = pl.BlockSpec((kw, cb), lambda j, i: (0, j + nc))
    bspec = pl.BlockSpec((1, cb), lambda j, i: (0, j))
    return pl.pallas_call(
        body, name="ffn_out_act_bwd", grid=(nc, nt + 1),
        in_specs=[pl.BlockSpec((tm, dh.shape[1]), lambda j, i: (jnp.maximum(nt - 1 - i, 0), 0)),
                  pl.BlockSpec((cb, w_out.shape[1]), lambda j, i: (j, 0)),
                  main, nxt, main, nxt, main, main, wspec0, wspec1],
        out_specs=[main, main, wspec0, wspec0, bspec, bspec],
        out_shape=[jax.ShapeDtypeStruct((n, D_FF), BF16), jax.ShapeDtypeStruct((n, D_FF), BF16),
                   jax.ShapeDtypeStruct((kw, D_FF), F32), jax.ShapeDtypeStruct((kw, D_FF), F32),
                   jax.ShapeDtypeStruct((1, D_FF), F32), jax.ShapeDtypeStruct((1, D_FF), F32)],
        scratch_shapes=[pltpu.VMEM((tm, cb), F32), pltpu.VMEM((SUBLANES, cb), F32)],
        compiler_params=_cparams("arbitrary", "arbitrary"))(dh, w_out, hu, hu, hg, hg, x_up, x_gate, w, w)


def _ssd_prep(pxs_ref, pb_ref, pc_ref, dtr_ref, dtb_ref, alog_ref, c):
    xs = _silu(pxs_ref[...])
    bm = _silu(pb_ref[...])
    cm = _silu(pc_ref[...])
    return (xs, bm, cm) + _ssd_decay(dtr_ref, dtb_ref, alog_ref, c)


def _ssd_decay(dtr_ref, dtb_ref, alog_ref, c):
    row =lax.broadcasted_iota(jnp.int32, (BLOCK, 1), 0) + c * BLOCK
    valid = (row >= PAD).astype(F32)
    dtr = dtr_ref[...] + dtb_ref[...]
    dt = _softplus(dtr) * valid
    a = -jnp.exp(alog_ref[...])
    lam = dt * a
    ri = lax.broadcasted_iota(jnp.int32, (BLOCK, BLOCK), 0)
    ci = lax.broadcasted_iota(jnp.int32, (BLOCK, BLOCK), 1)
    causal = ci <= ri
    cs = _sel_dot(causal.astype(BF16), lam)
    return valid, dtr, dt, a, lam, cs, causal


def _head_cols(r):
    return slice(SSD_HEADDIM * r, SSD_HEADDIM * (r + 1))


def _ssd_specs(nc, rev):
    def cidx(c):
        return nc - 1 - c if rev else c

    xs = pl.BlockSpec((BLOCK, SSD_GW), lambda g, c: (cidx(c), g))
    bspec = pl.BlockSpec((BLOCK, SSD_STATE), lambda g, c: (cidx(c), SSD_INNER // SSD_STATE + g))
    cspec = pl.BlockSpec((BLOCK, SSD_STATE), lambda g, c: (cidx(c), (SSD_INNER + SSD_BC) // SSD_STATE + g))
    lane = pl.BlockSpec((BLOCK, LANES), lambda g, c: (cidx(c), g))
    vec = pl.BlockSpec((1, LANES), lambda g, c: (0, g))
    wide_vec = pl.BlockSpec((1, SSD_GW), lambda g, c: (0, g))
    hsave = pl.BlockSpec((1, 1, SSD_GW, SSD_STATE), lambda g, c: (cidx(c), g, 0, 0))
    return xs, bspec, cspec, lane, vec, wide_vec, hsave


def _head_spread_matrix():
    r = lax.broadcasted_iota(jnp.int32, (LANES, SSD_GW), 0)
    col = lax.broadcasted_iota(jnp.int32, (LANES, SSD_GW), 1)
    return (col // SSD_HEADDIM == r).astype(BF16)


def _const_spec(shape):
    return pl.BlockSpec(shape, lambda g, c: (0,) * len(shape))


def _spread_heads(per_head, e_ref):
    wide = _dot_sel(jnp.concatenate(per_head, axis=0), e_ref[...])
    return [wide[BLOCK * k:BLOCK * (k + 1)] for k in range(len(per_head))]


def _call_with_side(body, side, *, name, grid, in_specs, out_specs, out_shape, scratch_shapes, args,
                    semantics=("parallel", "arbitrary")):
    if side is None:
        outs = pl.pallas_call(body, name=name, grid=grid, in_specs=in_specs, out_specs=out_specs, out_shape=out_shape,
                              scratch_shapes=scratch_shapes, compiler_params=_cparams(*semantics))(*args)
        return outs, []
    n_in, n_out, n_scr, n_side = len(in_specs), len(out_specs), len(scratch_shapes), len(side.arrays)

    def body_with_side(*refs):
        ins, rest = refs[:n_in + n_side], refs[n_in + n_side:]
        outs, scratch = rest[:n_out + n_side], rest[n_out + n_side:]
        side_refs = (ins[n_in:], outs[n_out:], scratch[n_scr:])
        ids = [pl.program_id(k) for k in range(len(grid))]
        inner_first = functools.reduce(jnp.logical_and, [i == 0 for i in ids[1:]], True)

        @pl.when((ids[0] == 0) & inner_first)
        def _():
            side.phases[0](*side_refs)

        body(*ins[:n_in], *outs[:n_out], *scratch[:n_scr])

        @pl.when((ids[0] == grid[0] // 2) & inner_first)
        def _():
            side.phases[1](*side_refs)

        @pl.when(functools.reduce(jnp.logical_and, [i == n - 1 for i, n in zip(ids, grid)]))
        def _():
            side.phases[2](*side_refs)

    any_spec = pl.BlockSpec(memory_space=pl.ANY)
    outs = pl.pallas_call(
        body_with_side, name=name, grid=grid, in_specs=list(in_specs) + [any_spec] * n_side,
        out_specs=list(out_specs) + [any_spec] * n_side, out_shape=list(out_shape) + list(side.out_shape),
        scratch_shapes=list(scratch_shapes) + list(side.scratch_shapes),
        compiler_params=_cparams(*["arbitrary"] * len(grid)))(*args, *side.arrays)
    return outs[:n_out], outs[n_out:]


def _ssd_fwd(pre, dt_raw, z, dtb, alog, dskip_w, norm_w, side=None):
    n = pre.shape[0]
    nc = n // BLOCK
    xs_s, b_s, c_s, lane_s, vec_s, wide_s, hs_s = _ssd_specs(nc, False)

    def body(pxs_ref, pb_ref, pc_ref, dtr_ref, z_ref, dtb_ref, alog_ref, dskw_ref, nw_ref, e_ref,
             y_ref, yn_ref, hs_ref, h_scr):
        c = pl.program_id(1)

        @pl.when(c == 0)
        def _():
            h_scr[...] = jnp.zeros_like(h_scr)

        xs, bm, cm, _, _, dt, _, _, cs, causal = _ssd_prep(pxs_ref, pb_ref, pc_ref, dtr_ref, dtb_ref, alog_ref, c)
        cst = cs.T
        cs_last = cs[BLOCK - 1:BLOCK, :]
        dt_w, ecs_w, dec_w = _spread_heads([dt, jnp.exp(cs), jnp.exp(cs_last - cs)], e_ref)
        xdt = xs * dt_w
        bmb = bm.astype(BF16)
        cmb = cm.astype(BF16)
        cb = _dot_nt(cmb, bmb)
        hg = h_scr[...]
        hs_ref[0, 0] = hg
        y = _dot_nt(cmb, hg.astype(BF16)) * ecs_w + dskw_ref[...] * xs
        first = lax.broadcasted_iota(jnp.int32, (BLOCK, LANES), 1) < SSD_HEADDIM
        diag = []
        for j in range(SSD_HPG // 2):
            xp = xdt[:, LANES * j:LANES * (j + 1)].astype(BF16)
            res = []
            for r in (2 * j, 2 * j + 1):
                lm = jnp.exp(jnp.where(causal, cs[:, r:r + 1] - cst[r:r + 1, :], NEG))
                res.append(_dot((cb * lm).astype(BF16), xp))
            diag.append(jnp.where(first, res[0], res[1]))
        y = y + jnp.concatenate(diag, axis=1)
        st = _dot_tn((xdt * dec_w).astype(BF16), bmb)
        eh = jnp.exp(cs_last)
        for r in range(SSD_HPG):
            rows = _head_cols(r)
            h_scr[rows, :] = hg[rows, :] * eh[:, r:r + 1] + st[rows, :]
        y_ref[...] = y
        gts = y * _silu(z_ref[...])
        rr = lax.rsqrt(jnp.mean(gts * gts, axis=-1, keepdims=True) + EPS)
        yn_ref[...] = (gts * rr * nw_ref[...]).astype(BF16)

    return _call_with_side(
        body, side, name="ssd_fwd", grid=(SSD_GROUPS, nc),
        in_specs=[xs_s, b_s, c_s, lane_s, xs_s, vec_s, vec_s, wide_s, wide_s, _const_spec((LANES, SSD_GW))],
        out_specs=[xs_s, xs_s, hs_s],
        out_shape=[jax.ShapeDtypeStruct((n, SSD_INNER), F32), jax.ShapeDtypeStruct((n, SSD_INNER), BF16),
                   jax.ShapeDtypeStruct((nc, SSD_GROUPS, SSD_GW, SSD_STATE), F32)],
        scratch_shapes=[pltpu.VMEM((SSD_GW, SSD_STATE), F32)],
        args=(pre, pre, pre, dt_raw, z, dtb, alog, dskip_w, norm_w, _head_spread_matrix()))


def _lane_put(acc, col, r):
    lane = lax.broadcasted_iota(jnp.int32, acc.shape, 1)
    return jnp.where(lane == r, col, acc)


def _ssd_bwd(dyn, y, z, pre, dt_raw, hsave, dtb, alog, dskip_w, norm_w, side=None):
    n = pre.shape[0]
    nc = n // BLOCK
    spread = _head_spread_matrix()
    xs_s, b_s, c_s, lane_s, vec_s, wide_s, hs_s = _ssd_specs(nc, True)
    bc_out =pl.BlockSpec((BLOCK, SSD_STATE), lambda g, c: (nc - 1 - c, g))

    def body(dyn_ref, y_ref, z_ref, pxs_ref, pb_ref, pc_ref, dtr_ref, hs_ref, dtb_ref, alog_ref, dskw_ref, nw_ref,
             e_ref, r_ref,
             dz_ref, dxs_ref, dbm_ref, dcm_ref, ddt_ref, dnw_ref, ddtb_ref, dalog_ref, ddsk_ref, g_scr):
        step = pl.program_id(1)
        c = nc - 1 - step

        @pl.when(step == 0)
        def _():
            g_scr[...] = jnp.zeros_like(g_scr)

        pxs, pb, pc = pxs_ref[...], pb_ref[...], pc_ref[...]
        sx, sb, sc = _sigmoid(pxs), _sigmoid(pb), _sigmoid(pc)
        xs, bm, cm = pxs * sx, pb * sb, pc * sc
        valid, dtr, dt, a, lam, cs, causal = _ssd_decay(dtr_ref, dtb_ref, alog_ref, c)
        cst = cs.T
        cs_last = cs[BLOCK - 1:BLOCK, :]
        bmb = bm.astype(BF16)
        cmb = cm.astype(BF16)
        cb = _dot_nt(cmb, bmb)
        hg = hs_ref[0, 0]
        hgb = hg.astype(BF16)
        yoff = _dot_nt(cmb, hgb)
        gn = g_scr[...]
        gnb = gn.astype(BF16)

        zv = z_ref[...]
        yv = y_ref[...]
        sgz = _sigmoid(zv)
        sz = zv * sgz
        gts = yv * sz
        rr = lax.rsqrt(jnp.mean(gts * gts, axis=-1, keepdims=True) + EPS)
        xh = gts * rr
        dynv = dyn_ref[...]
        gg = dynv * nw_ref[...]
        dgts = rr * (gg - xh * jnp.mean(gg * xh, axis=-1, keepdims=True))
        dnw = jnp.sum(dynv * xh, axis=0, keepdims=True)
        dy = dgts * sz
        dz_ref[...] = (dgts * yv * (sgz * (1.0 + zv * (1.0 - sgz)))).astype(BF16)

        ecs = jnp.exp(cs)
        dec = jnp.exp(cs_last - cs)
        eh = jnp.exp(cs_last)
        dt_w, ecs_w, dec_w = _spread_heads([dt, ecs, dec], e_ref)
        red_m = r_ref[...]

        def head_sums(v):
            return _dot_sel(v, red_m, terms=2)

        xdt = xs * dt_w
        q_all = _dot_nt(bmb, gnb)
        w_all = (dy * ecs_w).astype(BF16)
        e_hl = head_sums(q_all * xdt) * dec
        dcs_col = head_sums(dy * yoff) * ecs - e_hl
        gh = jnp.zeros((1, LANES), F32)
        prod = gn * hg
        for r in range(SSD_HPG):
            gh = _lane_put(gh, _sum_all(prod[_head_cols(r), :]), r)
        dcs_last = jnp.sum(e_hl, axis=0, keepdims=True) + eh * gh
        ddsk = jnp.sum(head_sums(dy * xs), axis=0, keepdims=True)
        cbt = _dot_nt(bmb, cmb)
        lane = lax.broadcasted_iota(jnp.int32, (BLOCK, LANES), 1)
        first = lane < SSD_HEADDIM
        causal_t = lax.broadcasted_iota(jnp.int32, (BLOCK, BLOCK), 1) >= lax.broadcasted_iota(
            jnp.int32, (BLOCK, BLOCK), 0)
        sub = lax.broadcasted_iota(jnp.int32, (SUBLANES, BLOCK), 0)
        dcs_row = jnp.zeros((SUBLANES, BLOCK), F32)
        dcb = jnp.zeros((BLOCK, BLOCK), F32)
        dxdt_pairs = []
        for j in range(SSD_HPG // 2):
            tile = slice(LANES * j, LANES * (j + 1))
            dy_p = dy[:, tile]
            dyb = dy_p.astype(BF16)
            xdtb = xdt[:, tile].astype(BF16)
            res = []
            for half, r in enumerate((2 * j, 2 * j + 1)):
                csc, csr = cs[:, r:r + 1], cst[r:r + 1, :]
                lm = jnp.exp(jnp.where(causal, csc - csr, NEG))
                lmt = jnp.exp(jnp.where(causal_t, csr - csc, NEG))
                keep = first if half == 0 else jnp.logical_not(first)
                gm = _dot_nt(jnp.where(keep, dy_p, 0.0).astype(BF16), xdtb) * lm
                dcb = dcb + gm
                mm_ = gm * cb
                dcs_col = dcs_col + jnp.where(lane == r, jnp.sum(mm_, axis=1, keepdims=True), 0.0)
                dcs_row = jnp.where(sub == r, jnp.sum(mm_, axis=0, keepdims=True), dcs_row)
                res.append(_dot((cbt * lmt).astype(BF16), dyb))
            dxdt_pairs.append(jnp.where(first, res[0], res[1]))
        dxdt = jnp.concatenate(dxdt_pairs, axis=1) + q_all * dec_w
        ddt_x = head_sums(dxdt * xs)
        dxs = dxdt * dt_w + dskw_ref[...] * dy
        dcbb = dcb.astype(BF16)
        dcm = _dot(w_all, hgb) + _dot(dcbb, bmb)
        dbm = _dot((xdt * dec_w).astype(BF16), gnb) + _dot_tn(dcbb, cmb)
        dh_off = _dot_tn(w_all, cmb)
        for r in range(SSD_HPG):
            rows = _head_cols(r)
            g_scr[rows, :] = gn[rows, :] * eh[:, r:r + 1] + dh_off[rows, :]

        pad_rows = jnp.zeros((BLOCK - SUBLANES, BLOCK), F32)
        dcs = dcs_col - jnp.concatenate([dcs_row, pad_rows], axis=0).T
        rsel = lax.broadcasted_iota(jnp.int32, (BLOCK, LANES), 0)
        dcs = dcs + jnp.where(rsel == BLOCK - 1, dcs_last, 0.0)
        ri = lax.broadcasted_iota(jnp.int32, (BLOCK, BLOCK), 0)
        ci = lax.broadcasted_iota(jnp.int32, (BLOCK, BLOCK), 1)
        dlam = _sel_dot((ci >= ri).astype(BF16), dcs)
        head = lane < SSD_HPG
        ddt = dlam * a + ddt_x
        ddtr = jnp.where(head, ddt * _sigmoid(dtr) * valid, 0.0)
        ddt_ref[...] = ddtr.astype(BF16)
        dalog = jnp.sum(jnp.where(head, dlam * lam, 0.0), axis=0, keepdims=True)
        ddtb = jnp.sum(ddtr, axis=0, keepdims=True)

        dxs_ref[...] = dxs * (sx * (1.0 + pxs * (1.0 - sx)))
        dbm_ref[...] = dbm * (sb * (1.0 + pb * (1.0 - sb)))
        dcm_ref[...] = dcm * (sc * (1.0 + pc * (1.0 - sc)))

        @pl.when(step == 0)
        def _():
            dnw_ref[...] = dnw
            ddtb_ref[...] = ddtb
            dalog_ref[...] = dalog
            ddsk_ref[...] = ddsk

        @pl.when(step > 0)
        def _():
            dnw_ref[...] += dnw
            ddtb_ref[...] += ddtb
            dalog_ref[...] += dalog
            ddsk_ref[...] += ddsk

    return _call_with_side(
        body, side, name="ssd_bwd", grid=(SSD_GROUPS, nc),
        in_specs=[xs_s, xs_s, xs_s, xs_s, b_s, c_s, lane_s, hs_s, vec_s, vec_s, wide_s, wide_s,
                  _const_spec((LANES, SSD_GW)), _const_spec((SSD_GW, LANES))],
        out_specs=[xs_s, xs_s, bc_out, bc_out, lane_s, wide_s, vec_s, vec_s, vec_s],
        out_shape=[jax.ShapeDtypeStruct((n, SSD_INNER), BF16), jax.ShapeDtypeStruct((n, SSD_INNER), F32),
                   jax.ShapeDtypeStruct((n, SSD_BC), F32), jax.ShapeDtypeStruct((n, SSD_BC), F32),
                   jax.ShapeDtypeStruct((n, DT_W), BF16), jax.ShapeDtypeStruct((1, SSD_INNER), F32),
                   jax.ShapeDtypeStruct((1, DT_W), F32), jax.ShapeDtypeStruct((1, DT_W), F32),
                   jax.ShapeDtypeStruct((1, DT_W), F32)],
        scratch_shapes=[pltpu.VMEM((SSD_GW, SSD_STATE), F32)],
        args=(dyn, y, z, pre, pre, pre, dt_raw, hsave, dtb, alog, dskip_w, norm_w, spread, spread.T))


def _bucket_table():
    def bucket(dist):
        d = np.maximum(dist, 0)
        half = REL_BUCKETS // 2
        big = half + (np.log(np.maximum(d, half).astype(np.float32) / np.float32(half))
                      / np.float32(math.log(REL_MAX_DIST / half)) * np.float32(REL_BUCKETS - half)).astype(np.int32)
        return np.where(d < half, d, np.minimum(big, REL_BUCKETS - 1)).astype(np.int32)

    l = np.arange(BLOCK)[None, :]
    band = bucket(l + BLOCK - np.arange(2 * BLOCK)[:, None])
    j = np.arange(BLOCK)[:, None]
    tables = [np.concatenate([bucket(v * BLOCK + l - j), band], axis=0) for v in range(3)]
    return np.concatenate([t.reshape(-1) for t in tables])


def _onehot_t():
    buckets = jnp.asarray(_bucket_table())
    return (buckets[None, :] == jnp.arange(REL_BUCKETS, dtype=jnp.int32)[:, None]).astype(F32)


def _bias_tables(rel_t, onehot_t):
    def body(r_ref, oh_ref, o_ref):
        o_ref[...] = jnp.dot(r_ref[...], oh_ref[...], precision=HIGHEST, preferred_element_type=F32)

    return pl.pallas_call(
        body, name="bias_tables", grid=(NT_ALL // NT_TILE,),
        in_specs=[pl.BlockSpec((ATT_HEADS, REL_BUCKETS), lambda i: (0, 0)),
                  pl.BlockSpec((REL_BUCKETS, NT_TILE), lambda i: (0, i))],
        out_specs=pl.BlockSpec((ATT_HEADS, NT_TILE), lambda i: (0, i)),
        out_shape=jax.ShapeDtypeStruct((ATT_HEADS, NT_ALL), F32),
        compiler_params=_cparams("parallel"))(rel_t, onehot_t)


def _bias_grad(dtab, onehot_t):
    def body(d_ref, oh_ref, o_ref):
        i = pl.program_id(0)
        p = lax.dot_general(d_ref[...], oh_ref[...], (((1,), (1,)), ((), ())), precision=HIGHEST,
                            preferred_element_type=F32)

        @pl.when(i == 0)
        def _():
            o_ref[...] = p

        @pl.when(i > 0)
        def _():
            o_ref[...] += p

    return pl.pallas_call(
        body, name="bias_grad", grid=(NT_ALL // NT_TILE,),
        in_specs=[pl.BlockSpec((ATT_HEADS, NT_TILE), lambda i: (0, i)),
                  pl.BlockSpec((REL_BUCKETS, NT_TILE), lambda i: (0, i))],
        out_specs=pl.BlockSpec((ATT_HEADS, REL_BUCKETS), lambda i: (0, 0)),
        out_shape=jax.ShapeDtypeStruct((ATT_HEADS, REL_BUCKETS), F32),
        compiler_params=_cparams("arbitrary"))(dtab, onehot_t)


def _att_mask_t(n, copies):
    far = 4 * BLOCK
    kk = lax.broadcasted_iota(jnp.int32, (N_KEYS, copies * BLOCK), 0)
    li = lax.broadcasted_iota(jnp.int32, (N_KEYS, copies * BLOCK), 1) & (BLOCK - 1)
    meta_ok = (kk >= PAD) & (kk < BLOCK) & (li + jnp.where(n >= 1, far, 0) >= kk)
    prev_ok = (kk >= BLOCK) & (kk < 2 * BLOCK) & (kk - BLOCK > li + jnp.where(n >= 2, 0, far))
    cur_ok = (kk >= 2 * BLOCK) & (kk - 2 * BLOCK <= li - jnp.where(n >= 1, 0, far))
    return meta_ok | prev_ok | cur_ok


def _att_kv(meta_ref, prev_ref, cur_ref):
    kv = jnp.concatenate([meta_ref[...], prev_ref[...], cur_ref[...]], axis=0)
    first = lax.broadcasted_iota(jnp.int32, (N_KEYS, LANES), 1) < ATT_HEADDIM
    out = []
    for pair in (kv[:, :LANES], kv[:, LANES:]):
        swapped = pltpu.roll(pair, ATT_HEADDIM, 1)
        out.append([jnp.where(first, pair, swapped).astype(BF16), jnp.where(first, swapped, pair).astype(BF16)])
    return out[0], out[1]


def _split_heads(x_pair, first):
    return jnp.concatenate([jnp.where(first, x_pair, 0.0), jnp.where(first, 0.0, x_pair)], axis=0).astype(BF16)


def _att_probs_t(qm2, k_dup, t_ref, j, mask2, sink_ref):
    scale = ATT_HEADDIM ** -0.5
    bias2 = jnp.concatenate([t_ref[0, 2 * j], t_ref[0, 2 * j + 1]], axis=1)
    second = lax.broadcasted_iota(jnp.int32, (1, 2 * BLOCK), 1) >= BLOCK
    sink2 = jnp.where(second, sink_ref[0:1, 2 * j + 1:2 * j + 2], sink_ref[0:1, 2 * j:2 * j + 1])
    s_t = jnp.where(mask2, _dot_nt(k_dup, qm2) * scale + bias2, NEG)
    mx = jnp.maximum(jnp.max(s_t, axis=0, keepdims=True), sink2)
    p_t = jnp.exp(s_t - mx)
    p_s = jnp.exp(sink2 - mx)
    inv = 1.0 / (jnp.sum(p_t, axis=0, keepdims=True) + p_s)
    return p_t * inv, p_s * inv


def _att_specs(nb, rev):
    def nidx(i):
        return nb - 1 - i if rev else i

    kvb = ATT_Q // (2 * ATT_KV)
    q_s = pl.BlockSpec((BLOCK, ATT_Q), lambda i: (nidx(i), 0))
    cur = pl.BlockSpec((BLOCK, 2 * ATT_KV), lambda i: (nidx(i), kvb))
    prev = pl.BlockSpec((BLOCK, 2 * ATT_KV), lambda i: (jnp.maximum(nidx(i) - 1, 0), kvb))
    meta = pl.BlockSpec((BLOCK, 2 * ATT_KV), lambda i: (0, kvb))
    table = pl.BlockSpec((1, ATT_HEADS, N_KEYS, BLOCK), lambda i: (jnp.minimum(nidx(i), 2), 0, 0, 0))
    sink = pl.BlockSpec((1, LANES), lambda i: (0, 0))
    return q_s, cur, prev, meta, table, sink


def _attn_fwd(qkv, tables, sinks):
    n = qkv.shape[0]
    nb = n // BLOCK
    q_s, cur_s, prev_s, meta_s, t_s, sink_s = _att_specs(nb, False)

    def body(q_ref, cur_ref, prev_ref, meta_ref, t_ref, sink_ref, o_ref):
        blk = pl.program_id(0)
        mask_t = _att_mask_t(blk, 1)
        k_dup, v_dup = _att_kv(meta_ref, prev_ref, cur_ref)
        v_dup_t = [v.T for v in v_dup]
        first = lax.broadcasted_iota(jnp.int32, (BLOCK, LANES), 1) < ATT_HEADDIM
        top = lax.broadcasted_iota(jnp.int32, (LANES, BLOCK), 0) < ATT_HEADDIM
        scale = ATT_HEADDIM ** -0.5
        for j in range(ATT_HEADS // 2):
            kh = 2 * j // ATT_GQ
            tile = slice(LANES * j, LANES * (j + 1))
            q_p = q_ref[:, tile]
            res = []
            for half, h in enumerate((2 * j, 2 * j + 1)):
                qm = jnp.where(first if half == 0 else jnp.logical_not(first), q_p, 0.0).astype(BF16)
                sink = sink_ref[0:1, h:h + 1]
                s_t = jnp.where(mask_t, _dot_nt(k_dup[kh], qm) * scale + t_ref[0, h], NEG)
                mx = jnp.maximum(jnp.max(s_t, axis=0, keepdims=True), sink)
                p_t = jnp.exp(s_t - mx)
                inv = 1.0 / (jnp.sum(p_t, axis=0, keepdims=True) + jnp.exp(sink - mx))
                res.append(_dot(v_dup_t[kh], (p_t * inv).astype(BF16)))
            o_ref[:, tile] = jnp.where(top, res[0], res[1]).T.astype(BF16)

    return pl.pallas_call(
        body, name="attn_fwd", grid=(nb,),
        in_specs=[q_s, cur_s, prev_s, meta_s, t_s, sink_s],
        out_specs=q_s,
        out_shape=jax.ShapeDtypeStruct((n, ATT_Q), BF16),
        compiler_params=_cparams("parallel"))(qkv, qkv, qkv, qkv, tables, sinks)


def _attn_bwd(datt, qkv, tables, sinks):
    n = qkv.shape[0]
    nb = n // BLOCK
    q_s, cur_s, prev_s, meta_s, t_s, sink_s = _att_specs(nb, True)
    dqkv_s = pl.BlockSpec((BLOCK, ATT_Q + 2 * ATT_KV), lambda i: (nb - 1 - i, 0))
    scale = ATT_HEADDIM ** -0.5

    def body(do_ref, q_ref, cur_ref, prev_ref, meta_ref, t_ref, sink_ref,
             dqkv_ref, dt_ref, dsink_ref, carry_scr, meta_scr):
        step = pl.program_id(0)
        blk = nb - 1 - step
        mask2 = _att_mask_t(blk, 2)
        k_dup, v_dup = _att_kv(meta_ref, prev_ref, cur_ref)
        k_dup_t = [k.T for k in k_dup]

        @pl.when(step == 0)
        def _():
            carry_scr[...] = jnp.zeros_like(carry_scr)
            meta_scr[...] = jnp.zeros_like(meta_scr)
            dsink_ref[...] = jnp.zeros_like(dsink_ref)

        @pl.when((step == 0) | (blk <= 1))
        def _():
            dt_ref[...] = jnp.zeros_like(dt_ref)

        first = lax.broadcasted_iota(jnp.int32, (BLOCK, LANES), 1) < ATT_HEADDIM
        top = lax.broadcasted_iota(jnp.int32, (LANES, BLOCK), 0) < ATT_HEADDIM
        first_k = lax.broadcasted_iota(jnp.int32, (N_KEYS, LANES), 1) < ATT_HEADDIM
        dsink = jnp.zeros((1, LANES), F32)
        dk_acc = [None] * ATT_KV_HEADS
        dv_acc = [None] * ATT_KV_HEADS
        for j in range(ATT_HEADS // 2):
            kh = 2 * j // ATT_GQ
            tile = slice(LANES * j, LANES * (j + 1))
            qm2 = _split_heads(q_ref[:, tile], first)
            dom2 = _split_heads(do_ref[:, tile], first)
            p_t, p_s = _att_probs_t(qm2, k_dup[kh], t_ref, j, mask2, sink_ref)
            dp_t = _dot_nt(v_dup[kh], dom2)
            delta = jnp.sum(p_t * dp_t, axis=0, keepdims=True)
            ds_t = p_t * (dp_t - delta)
            sink_terms = p_s * delta
            for half in range(2):
                cols = slice(BLOCK * half, BLOCK * (half + 1))
                dsink = _lane_put(dsink, -jnp.sum(sink_terms[:, cols], axis=1, keepdims=True), 2 * j + half)
                dt_ref[0, 2 * j + half] += ds_t[:, cols]
            ds_tb = ds_t.astype(BF16)
            dq_t = _dot(k_dup_t[kh], ds_tb)
            dqkv_ref[:, tile] = (jnp.where(top, dq_t[:, :BLOCK], dq_t[:, BLOCK:]).T * scale).astype(BF16)
            dk_part, dv_part = _dot(ds_tb, qm2), _dot(p_t.astype(BF16), dom2)
            dk_acc[kh] = dk_part if dk_acc[kh] is None else dk_acc[kh] + dk_part
            dv_acc[kh] = dv_part if dv_acc[kh] is None else dv_acc[kh] + dv_part
        dsink_ref[...] += dsink
        folded = [a + pltpu.roll(a, ATT_HEADDIM, 1) for a in dk_acc + dv_acc]
        dkv = jnp.concatenate([jnp.where(first_k, folded[0], folded[1]) * scale,
                               jnp.where(first_k, folded[2], folded[3])], axis=1)
        meta_scr[...] += dkv[:BLOCK, :]
        own = dkv[2 * BLOCK:, :] + carry_scr[...]
        carry_scr[...] = dkv[BLOCK:2 * BLOCK, :]

        @pl.when(blk > 0)
        def _():
            dqkv_ref[:, ATT_Q:] = own.astype(BF16)

        @pl.when(blk == 0)
        def _():
            dqkv_ref[:, ATT_Q:] = (own + meta_scr[...]).astype(BF16)

    return pl.pallas_call(
        body, name="attn_bwd", grid=(nb,),
        in_specs=[q_s, q_s, cur_s, prev_s, meta_s, t_s, sink_s],
        out_specs=[dqkv_s, t_s, sink_s],
        out_shape=[jax.ShapeDtypeStruct((n, ATT_Q + 2 * ATT_KV), BF16),
                   jax.ShapeDtypeStruct((3, ATT_HEADS, N_KEYS, BLOCK), F32),
                   jax.ShapeDtypeStruct((1, LANES), F32)],
        scratch_shapes=[pltpu.VMEM((BLOCK, 2 * ATT_KV), F32), pltpu.VMEM((BLOCK, 2 * ATT_KV), F32)],
        compiler_params=_cparams("arbitrary"))(datt, qkv, qkv, qkv, qkv, tables, sinks)


def _merge_out_fwd(gates, y_ssd, y_att, gate_b, w_out, h):
    n = gates.shape[0]
    tm = _row_tile(n, 416)

    def body(gs_ref, ga_ref, ys_ref, ya_ref, gb_ref, w_ref, h_ref, m_ref, o_ref):
        merged = (_sigmoid(gs_ref[...] + gb_ref[0:1, :]) * ys_ref[...]
                  + _sigmoid(ga_ref[...] + gb_ref[1:2, :]) * ya_ref[...]).astype(BF16)
        m_ref[...] = merged
        row = pl.program_id(0) * tm + lax.broadcasted_iota(jnp.int32, (tm, 1), 0)
        o_ref[...] = jnp.where(row >= PAD, _dot(merged, w_ref[...]), 0.0) + h_ref[...]

    row = pl.BlockSpec((tm, D_MODEL), lambda i: (i, 0))
    return pl.pallas_call(
        body, name="merge_out_fwd", grid=(n // tm,),
        in_specs=[row, pl.BlockSpec((tm, D_MODEL), lambda i: (i, 1)), row, row,
                  pl.BlockSpec((2, D_MODEL), lambda i: (0, 0)), pl.BlockSpec((D_MODEL, D_MODEL), lambda i: (0, 0)), row],
        out_specs=[row, row],
        out_shape=[jax.ShapeDtypeStruct((n, D_MODEL), BF16), jax.ShapeDtypeStruct((n, D_MODEL), F32)],
        compiler_params=_cparams("parallel"))(gates, gates, y_ssd, y_att, gate_b, w_out, h)


def _merge_out_bwd(dh, w_out, gates, y_ssd, y_att, gate_b):
    n = gates.shape[0]
    tm = _row_tile(n, 416)

    def body(dh_ref, w_ref, gs_ref, ga_ref, ys_ref, ya_ref, gb_ref, dys_ref, dya_ref, dg_ref, dgb_ref):
        i = pl.program_id(0)
        row = i * tm + lax.broadcasted_iota(jnp.int32, (tm, 1), 0)
        dmv = jnp.where(row >= PAD, _dot_nt(dh_ref[...].astype(BF16), w_ref[...]), 0.0)
        ss =_sigmoid(gs_ref[...] + gb_ref[0:1, :])
        sa = _sigmoid(ga_ref[...] + gb_ref[1:2, :])
        dys_ref[...] = (dmv * ss).astype(BF16)
        dya_ref[...] = (dmv * sa).astype(BF16)
        dgs = dmv * ys_ref[...] * ss * (1.0 - ss)
        dga = dmv * ya_ref[...] * sa * (1.0 - sa)
        dg_ref[:, :D_MODEL] = dgs.astype(BF16)
        dg_ref[:, D_MODEL:] = dga.astype(BF16)
        part = jnp.concatenate([jnp.sum(dgs, axis=0, keepdims=True), jnp.sum(dga, axis=0, keepdims=True)], axis=0)

        @pl.when(i == 0)
        def _():
            dgb_ref[...] = part

        @pl.when(i > 0)
        def _():
            dgb_ref[...] += part

    row = pl.BlockSpec((tm, D_MODEL), lambda i: (i, 0))
    gb = pl.BlockSpec((2, D_MODEL), lambda i: (0, 0))
    return pl.pallas_call(
        body, name="merge_out_bwd", grid=(n // tm,),
        in_specs=[row, pl.BlockSpec((D_MODEL, D_MODEL), lambda i: (0, 0)), row,
                  pl.BlockSpec((tm, D_MODEL), lambda i: (i, 1)), row, row, gb],
        out_specs=[row, row, pl.BlockSpec((tm, 2 * D_MODEL), lambda i: (i, 0)), gb],
        out_shape=[jax.ShapeDtypeStruct((n, D_MODEL), BF16), jax.ShapeDtypeStruct((n, D_MODEL), BF16),
                   jax.ShapeDtypeStruct((n, 2 * D_MODEL), BF16), jax.ShapeDtypeStruct((2, D_MODEL), F32)],
        compiler_params=_cparams("arbitrary"))(dh, w_out, gates, gates, y_ssd, y_att, gate_b)


def _col_move(srcs, outs, pieces, *, name):
    rows = srcs[0].shape[-2]
    tr = _row_tile(rows, 128)
    n_src = len(srcs)
    covered = [sum(p[6] for p in pieces if p[0] == o) for o in range(len(outs))]
    total = [int(np.prod(shp)) // rows for shp, _ in outs]

    def body(*refs):
        in_refs, out_refs = refs[:n_src], refs[n_src:]
        for o, ref in enumerate(out_refs):
            if covered[o] != total[o]:
                ref[...] = jnp.zeros_like(ref)
        for o, ol, oc, s, sl, sc, width in pieces:
            val = in_refs[s][:, sc:sc + width] if sl is None else in_refs[s][sl, :, sc:sc + width]
            val = val.astype(outs[o][1])
            if ol is None:
                out_refs[o][:, oc:oc + width] = val
            else:
                out_refs[o][ol, :, oc:oc + width] = val

    def spec(shape):
        if len(shape) == 2:
            return pl.BlockSpec((tr, shape[1]), lambda i: (i, 0))
        return pl.BlockSpec((shape[0], tr, shape[2]), lambda i: (0, i, 0))

    return pl.pallas_call(
        body, name=name, grid=(rows // tr,),
        in_specs=[spec(a.shape) for a in srcs], out_specs=[spec(shp) for shp, _ in outs],
        out_shape=[jax.ShapeDtypeStruct(shp, dt) for shp, dt in outs],
        compiler_params=_cparams("parallel"))(*srcs)


def _shard_pieces(seg_ranges, shard_w):
    out = []
    for seg, runs in enumerate(seg_ranges):
        for g0, width, s0 in runs:
            done = 0
            while done < width:
                dev, col = divmod(g0 + done, shard_w)
                take = min(width - done, shard_w - col)
                out.append((seg, s0 + done, dev, col, take))
                done += take
    return out


_CHIP_RELATIONS = [(1, 0, 0), (0, 1, 0), (1, 1, 0)]
N_CHIPS = 4


def _gather_two_level(arrays, *, name):
    outs = _run_plan(_gather_plan(arrays), name)
    return [o.reshape((N_DEV,) + a.shape) for o, a in zip(outs, arrays)]


class _CommPlan:
    def __init__(self, arrays, out_shape, scratch_shapes, phases):
        self.arrays, self.out_shape, self.scratch_shapes, self.phases = arrays, out_shape, scratch_shapes, phases


def _run_plan(plan, name):
    n_arr = len(plan.arrays)

    def body(*refs):
        ins, outs, sems = refs[:n_arr], refs[n_arr:2 * n_arr], refs[2 * n_arr:]
        for phase in plan.phases:
            phase(ins, outs, sems)

    any_spec = pl.BlockSpec(memory_space=pl.ANY)
    return pl.pallas_call(
        body, name=name, in_specs=[any_spec] * n_arr, out_specs=[any_spec] * n_arr, out_shape=plan.out_shape,
        scratch_shapes=plan.scratch_shapes)(*plan.arrays)


def _gather_plan(arrays):
    n_arr = len(arrays)
    n_chips = len(_CHIP_RELATIONS)
    n_pair = 1 + 2 * n_chips

    def where():
        x, y, c = lax.axis_index("x"), lax.axis_index("y"), lax.axis_index("c")
        return x, y, c, (x, y, 1 - c), [(x ^ dx, y ^ dy) for dx, dy, _ in _CHIP_RELATIONS]

    def copy(outs, sems, a, k, block, to, src=None):
        slot = outs[a].at[2 * block[0] + block[1], block[2]]
        return pltpu.make_async_remote_copy(
            src_ref=slot if src is None else src, dst_ref=slot, send_sem=sems[0].at[a * n_pair + k],
            recv_sem=sems[1].at[a * n_pair + k], device_id=to, device_id_type=MESH)

    def mine(ins, outs, sems, a, x, y, c):
        return pltpu.make_async_copy(ins[a], outs[a].at[2 * x + y, c], sems[2].at[a])

    def first_copies(ins, outs, sems, a, x, y, c, sibling, chips):
        return ([copy(outs, sems, a, 0, (x, y, c), sibling, src=ins[a])]
                + [copy(outs, sems, a, 1 + j, (x, y, c), (*chip, c), src=ins[a]) for j, chip in enumerate(chips)])

    def start(ins, outs, sems):
        x, y, c, sibling, chips = where()
        for a in range(n_arr):
            mine(ins, outs, sems, a, x, y, c).start()
            for cp in first_copies(ins, outs, sems, a, x, y, c, sibling, chips):
                cp.start()

    def pass_on(ins, outs, sems):
        x, y, c, sibling, chips = where()
        for j, chip in enumerate(chips):
            for a in range(n_arr):
                copy(outs, sems, a, 1 + j, (*chip, c), (x, y, c)).wait_recv()
                copy(outs, sems, a, 1 + n_chips + j, (*chip, c), sibling).start()

    def finish(ins, outs, sems):
        x, y, c, sibling, chips = where()
        for a in range(n_arr):
            copy(outs, sems, a, 0, (x, y, 1 - c), (x, y, c)).wait_recv()
            for j, chip in enumerate(chips):
                copy(outs, sems, a, 1 + n_chips + j, (*chip, 1 - c), (x, y, c)).wait_recv()
        for a in range(n_arr):
            for cp in first_copies(ins, outs, sems, a, x, y, c, sibling, chips):
                cp.wait_send()
            for j, chip in enumerate(chips):
                copy(outs, sems, a, 1 + n_chips + j, (*chip, c), sibling).wait_send()
            mine(ins, outs, sems, a, x, y, c).wait()

    return _CommPlan(
        arrays, [jax.ShapeDtypeStruct((N_CHIPS, 2) + a.shape, a.dtype) for a in arrays],
        [pltpu.SemaphoreType.DMA((n_arr * n_pair,)), pltpu.SemaphoreType.DMA((n_arr * n_pair,)),
         pltpu.SemaphoreType.DMA((n_arr,))],
        (start, pass_on, finish))


def _sibling_exchange(arrays, scatter, *, name):
    n_arr = len(arrays)

    def body(*refs):
        ins, outs = refs[:n_arr], refs[n_arr:2 * n_arr]
        send_sems, recv_sems = refs[2 * n_arr:]
        x, y, c = lax.axis_index("x"), lax.axis_index("y"), lax.axis_index("c")
        copies = []
        for a in range(n_arr):
            for q in range(N_CHIPS if scatter[a] else 1):
                src = ins[a].at[2 * q + 1 - c] if scatter[a] else ins[a]
                dst = outs[a].at[q] if scatter[a] else outs[a]
                cp = pltpu.make_async_remote_copy(
                    src_ref=src, dst_ref=dst, send_sem=send_sems.at[a * N_CHIPS + q],
                    recv_sem=recv_sems.at[a * N_CHIPS + q], device_id=(x, y, 1 - c), device_id_type=MESH)
                cp.start()
                copies.append(cp)
        for cp in copies:
            cp.wait_send()
        for cp in copies:
            cp.wait_recv()

    any_spec = pl.BlockSpec(memory_space=pl.ANY)
    return pl.pallas_call(
        body, name=name, in_specs=[any_spec] * n_arr, out_specs=[any_spec] * n_arr,
        out_shape=[jax.ShapeDtypeStruct(((N_CHIPS,) + a.shape[1:]) if s else a.shape, a.dtype)
                   for a, s in zip(arrays, scatter)],
        scratch_shapes=[pltpu.SemaphoreType.DMA((n_arr * N_CHIPS,)), pltpu.SemaphoreType.DMA((n_arr * N_CHIPS,))],
    )(*arrays)


def _add(a, b, *, name):
    rows, cols = a.shape
    tr = _row_tile(rows, 256)

    def body(a_ref, b_ref, o_ref):
        o_ref[...] = a_ref[...] + b_ref[...]

    blk = pl.BlockSpec((tr, cols), lambda i: (i, 0))
    return pl.pallas_call(body, name=name, grid=(rows // tr,), in_specs=[blk, blk], out_specs=blk,
                          out_shape=jax.ShapeDtypeStruct(a.shape, a.dtype), compiler_params=_cparams("parallel"))(a, b)


def _chip_exchange(arrays, scatter, *, name):
    return _run_plan(_chip_exchange_plan(arrays, scatter), name)


_ALL_RELATIONS = [(dx, dy, dc) for dx in (0, 1) for dy in (0, 1) for dc in (0, 1)][1:]


def _all_to_all_plan(arrays, scatter=None):
    n_arr = len(arrays)
    n_rel = len(_ALL_RELATIONS)
    scatter = scatter or [True] * n_arr

    def block(ins, a, p):
        return ins[a].at[p] if scatter[a] else ins[a]

    def local_copies(ins, outs, sems):
        me = 4 * lax.axis_index("x") + 2 * lax.axis_index("y") + lax.axis_index("c")
        return [pltpu.make_async_copy(block(ins, a, me), outs[a].at[me], sems[2].at[a]) for a in range(n_arr)]

    def remote_copies(ins, outs, sems, arrivals):
        x, y, c = lax.axis_index("x"), lax.axis_index("y"), lax.axis_index("c")
        me = 4 * x + 2 * y + c
        out = []
        for k, (dx, dy, dc) in enumerate(_ALL_RELATIONS):
            px, py, pc = x ^ dx, y ^ dy, c ^ dc
            peer = 4 * px + 2 * py + pc
            for a in range(n_arr):
                out.append(pltpu.make_async_remote_copy(
                    src_ref=block(ins, a, peer), dst_ref=outs[a].at[peer if arrivals else me],
                    send_sem=sems[0].at[a * n_rel + k], recv_sem=sems[1].at[a * n_rel + k],
                    device_id=(x, y, c) if arrivals else (px, py, pc), device_id_type=MESH))
        return out

    def start(ins, outs, sems):
        for cp in local_copies(ins, outs, sems) + remote_copies(ins, outs, sems, False):
            cp.start()

    def pass_on(ins, outs, sems):
        pass

    def finish(ins, outs, sems):
        for send in remote_copies(ins, outs, sems, False):
            send.wait_send()
        for arrival in remote_copies(ins, outs, sems, True):
            arrival.wait_recv()
        for cp in local_copies(ins, outs, sems):
            cp.wait()

    return _CommPlan(
        arrays, [jax.ShapeDtypeStruct(a.shape if s else (N_DEV,) + a.shape, a.dtype) for a, s in zip(arrays, scatter)],
        [pltpu.SemaphoreType.DMA((n_arr * n_rel,)), pltpu.SemaphoreType.DMA((n_arr * n_rel,)),
         pltpu.SemaphoreType.DMA((n_arr,))],
        (start, pass_on, finish))


def _chip_exchange_plan(arrays, scatter):
    n_arr = len(arrays)
    n_rel = len(_CHIP_RELATIONS)

    def local_copies(ins, outs, sems):
        me = 2 * lax.axis_index("x") + lax.axis_index("y")
        return [pltpu.make_async_copy(ins[a].at[me] if scatter[a] else ins[a], outs[a].at[me], sems[2].at[a])
                for a in range(n_arr)]

    def remote_copies(ins, outs, sems, arrivals):
        x, y, c = lax.axis_index("x"), lax.axis_index("y"), lax.axis_index("c")
        me = 2 * x + y
        out = []
        for k, (dx, dy, _) in enumerate(_CHIP_RELATIONS):
            px, py = x ^ dx, y ^ dy
            peer = 2 * px + py
            for a in range(n_arr):
                out.append(pltpu.make_async_remote_copy(
                    src_ref=ins[a].at[peer] if scatter[a] else ins[a], dst_ref=outs[a].at[peer if arrivals else me],
                    send_sem=sems[0].at[a * n_rel + k], recv_sem=sems[1].at[a * n_rel + k],
                    device_id=(x, y, c) if arrivals else (px, py, c), device_id_type=MESH))
        return out

    def start(ins, outs, sems):
        for cp in local_copies(ins, outs, sems) + remote_copies(ins, outs, sems, False):
            cp.start()

    def pass_on(ins, outs, sems):
        pass

    def finish(ins, outs, sems):
        for send in remote_copies(ins, outs, sems, False):
            send.wait_send()
        for arrival in remote_copies(ins, outs, sems, True):
            arrival.wait_recv()
        for cp in local_copies(ins, outs, sems):
            cp.wait()

    out_shape = [jax.ShapeDtypeStruct((N_CHIPS,) + (a.shape[1:] if s else a.shape), a.dtype)
                 for a, s in zip(arrays, scatter)]
    return _CommPlan(
        arrays, out_shape,
        [pltpu.SemaphoreType.DMA((n_arr * n_rel,)), pltpu.SemaphoreType.DMA((n_arr * n_rel,)),
         pltpu.SemaphoreType.DMA((n_arr,))],
        (start, pass_on, finish))


def _adamw(w, gslots, m, v, *, name):
    rows, cols = w.shape
    n_slots = gslots.shape[0]
    tr = _row_tile(rows, 128) if rows % 16 == 0 else rows

    def body(w_ref, g_ref, m_ref, v_ref, go_ref, d_ref, mo_ref, vo_ref):
        g = g_ref[0].astype(F32)
        for s in range(1, n_slots):
            g = g + g_ref[s].astype(F32)
        mn = ADAM_B1 * m_ref[...] + (1.0 - ADAM_B1) * g
        vn = ADAM_B2 * v_ref[...] + (1.0 - ADAM_B2) * (g * g)
        go_ref[...] = g
        mo_ref[...] = mn
        vo_ref[...] = vn
        m_hat = mn / (1.0 - ADAM_B1 ** ADAM_STEP)
        v_hat = vn / (1.0 - ADAM_B2 ** ADAM_STEP)
        d_ref[...] = -ADAM_LR * (m_hat / (jnp.sqrt(v_hat) + ADAM_EPS) + ADAM_WD * w_ref[...])

    blk = pl.BlockSpec((tr, cols), lambda i: (i, 0))
    shp = jax.ShapeDtypeStruct((rows, cols), F32)
    return pl.pallas_call(
        body, name=name, grid=(rows // tr,),
        in_specs=[blk, pl.BlockSpec((n_slots, tr, cols), lambda i: (0, i, 0)), blk, blk],
        out_specs=[blk] * 4, out_shape=[shp] * 4,
        compiler_params=_cparams("parallel"))(w, gslots, m, v)


_BIG = ("w_in", "w_ssd_branch", "w_attn_branch", "w_out", "w_ffn_in", "w_ffn_out")
_SMALL_SHARDED = ("meta_tokens", "ssd_conv_w", "gate_b", "ffn_conv_w")
_SMALL_REPLICATED = ("norm_mix_w", "ssd_conv_b", "ssd_dt_bias", "ssd_a_log", "ssd_d", "ssd_norm_w", "attn_sinks",
                     "rel_bias", "norm_ffn_w", "ffn_conv_b", "norm_final_w")
_WEIGHTS = ("meta_tokens", "norm_mix_w", "w_in", "ssd_conv_w", "ssd_conv_b", "ssd_dt_bias", "ssd_a_log", "ssd_d",
            "ssd_norm_w", "w_ssd_branch", "w_attn_branch", "attn_sinks", "rel_bias", "gate_b", "w_out", "norm_ffn_w",
            "w_ffn_in", "ffn_conv_w", "ffn_conv_b", "w_ffn_out", "norm_final_w")
_ROW_SHARDED = ("w_ssd_branch", "w_attn_branch", "w_out", "w_ffn_out")
_COL_SHARDED = ("w_in", "w_ffn_in", "meta_tokens", "ssd_conv_w", "gate_b", "ffn_conv_w")
_IN_SEGS = (("z", SSD_INNER), ("xbc", SSD_XBC), ("dt", SSD_HEADS), ("qkv", ATT_Q + 2 * ATT_KV), ("g", 2 * D_MODEL))


def _pack_rows(flat_parts, width, row_mult):
    flat = jnp.concatenate([p.reshape(-1) for p in flat_parts])
    pad = (-flat.shape[0]) % (width * row_mult)
    if pad:
        flat = jnp.concatenate([flat, jnp.zeros((pad,), flat.dtype)])
    return flat.reshape(-1, width)


def _unpack(flat, shapes):
    out, off = [], 0
    for shp in shapes:
        size = int(np.prod(shp))
        out.append(flat[off:off + size].reshape(shp))
        off += size
    return out


def _gather_full(stack, name, shard_shape):
    if name in _COL_SHARDED:
        return jnp.transpose(stack, (1, 0, 2)).reshape(shard_shape[0], N_DEV * shard_shape[1])
    return stack.reshape(N_DEV * shard_shape[0], shard_shape[1])


_IN_SEG_W = {"z": SSD_INNER, "xbc": SSD_XBC, "dt": DT_W, "qkv": ATT_Q + 2 * ATT_KV, "g": 2 * D_MODEL}
_IN_SHARD_W = (SSD_INNER + SSD_XBC + SSD_HEADS + ATT_Q + 2 * ATT_KV + 2 * D_MODEL) // N_DEV
_FFN_SHARD_W = 2 * D_FF // N_DEV


def _in_seg_runs():
    runs, off = [], 0
    for nm, width in _IN_SEGS:
        if nm == "dt":
            runs.append([(off + SSD_HPG * g, SSD_HPG, LANES * g) for g in range(SSD_GROUPS)])
        else:
            runs.append([(off, width, 0)])
        off += width
    return runs


def _w_in_to_segments(stack):
    pieces = [(seg, None, scol, 0, dev, col, w) for seg, scol, dev, col, w in _shard_pieces(_in_seg_runs(), _IN_SHARD_W)]
    outs = [((D_MODEL, _IN_SEG_W[nm]), stack.dtype) for nm, _ in _IN_SEGS]
    return dict(zip([nm for nm, _ in _IN_SEGS], _col_move([stack], outs, pieces, name="w_in_segments")))


def _segments_to_w_in_shards(seg_grads):
    pieces = [(0, dev, col, seg, None, scol, w) for seg, scol, dev, col, w in _shard_pieces(_in_seg_runs(), _IN_SHARD_W)]
    return _col_move(seg_grads, [((N_DEV, D_MODEL, _IN_SHARD_W), seg_grads[0].dtype)], pieces, name="g_w_in_shards")[0]


def _ffn_in_from_shards(stack):
    pieces = [(0, None, scol, 0, dev, col, w)
              for _, scol, dev, col, w in _shard_pieces([[(0, 2 * D_FF, 0)]], _FFN_SHARD_W)]
    return _col_move([stack], [((D_MODEL, 2 * D_FF), stack.dtype)], pieces, name="w_ffn_in_full")[0]


def _ffn_in_to_shards(g_up, g_gate):
    pieces = [(0, dev, col, seg, None, scol, w)
              for seg, scol, dev, col, w in _shard_pieces([[(0, D_FF, 0)], [(D_FF, D_FF, 0)]], _FFN_SHARD_W)]
    return _col_move([g_up, g_gate], [((N_DEV, D_MODEL, _FFN_SHARD_W), g_up.dtype)], pieces, name="g_w_ffn_in_shards")[0]


def _dt_spread(w_dt):
    k = w_dt.shape[0]
    w4 = w_dt.reshape(k, SSD_GROUPS, SSD_HPG)
    return jnp.pad(w4, ((0, 0), (0, 0), (0, LANES - SSD_HPG))).reshape(k, DT_W)


def _dt_gather(w_wide):
    k = w_wide.shape[0]
    return w_wide.reshape(k, SSD_GROUPS, LANES)[:, :, :SSD_HPG].reshape(k, SSD_HEADS)


class _LateExchanges:
    def __init__(self, two_d, shape2):
        self.two_d, self.shape2 = two_d, shape2
        self.early_grads_received = None
        self.w_in_grads_received = None

    def row_pack(self, tree):
        return jnp.concatenate([tree[k] for k in _ROW_SHARDED], axis=0)

    def late_weights_plan(self):
        return _gather_plan([self.two_d["w_ffn_in"].astype(BF16), self.row_pack(self.two_d).astype(BF16)])

    def late_weights(self, gathered):
        w_ffn_in_all, rows_all = [g.reshape((N_DEV,) + g.shape[2:]) for g in gathered]
        out = {"w_ffn_in": _ffn_in_from_shards(w_ffn_in_all)}
        off = 0
        for k in _ROW_SHARDED:
            r = self.shape2[k][0]
            out[k] = rows_all[:, off:off + r].reshape(N_DEV * r, D_MODEL)
            off += r
        return out

    def early_grads_plan(self, grads):
        rows_send = jnp.concatenate([grads[k].reshape(N_DEV, self.shape2[k][0], D_MODEL) for k in _ROW_SHARDED], axis=1)
        return _all_to_all_plan([_ffn_in_to_shards(*grads["w_ffn_in"]), rows_send])

    def w_in_grads_plan(self, seg_grads):
        return _all_to_all_plan([_segments_to_w_in_shards(seg_grads)])


def _local_step(x, target, w, exchanges=None):
    h0 = jnp.concatenate([jnp.zeros((PAD, D_MODEL), F32), w["meta_tokens"], x], axis=0)
    segs = w["in_segs"]

    dtb = _dt_spread(w["ssd_dt_bias"])
    alog = _dt_spread(w["ssd_a_log"])
    dskip_w = jnp.repeat(w["ssd_d"], SSD_HEADDIM, axis=1)
    sinks = jnp.pad(w["attn_sinks"], ((0, 0), (0, LANES - ATT_HEADS)))
    onehot_t = _onehot_t()
    tables = jnp.transpose(_bias_tables(w["rel_bias"].T, onehot_t).reshape(ATT_HEADS, 3, N_KEYS, BLOCK), (1, 0, 2, 3))

    u = _rms_fwd(h0, w["norm_mix_w"], name="rms_mix_fwd")
    z = _mm(u, segs["z"], name="in_z")
    xbc, pre = _mm_conv_fwd(u, segs["xbc"], w["ssd_conv_w"], w["ssd_conv_b"], name="in_xbc_conv_fwd")
    dt_raw = _mm(u, segs["dt"], name="in_dt")
    qkv = _mm(u, segs["qkv"], out_dtype=BF16, name="in_qkv")
    gates = _mm(u, segs["g"], name="in_g")
    (y, yn, hsave), gathered = _ssd_fwd(pre, dt_raw, z, dtb, alog, dskip_w, w["ssd_norm_w"],
                                        side=None if exchanges is None else exchanges.late_weights_plan())
    if exchanges is not None:
        w = {**w, **exchanges.late_weights(gathered)}
    w_ffn_up, w_ffn_gate = w["w_ffn_in"][:, :D_FF], w["w_ffn_in"][:, D_FF:]
    y_ssd = _mm(yn, w["w_ssd_branch"], out_dtype=BF16, name="ssd_out")
    att = _attn_fwd(qkv, tables, sinks)
    y_att = _mm(att, w["w_attn_branch"], out_dtype=BF16, name="att_out")
    merged, h1 = _merge_out_fwd(gates, y_ssd, y_att, w["gate_b"], w["w_out"], h0)
    u2 = _rms_fwd(h1, w["norm_ffn_w"], name="rms_ffn_fwd")
    x_up, x_gate, hid_up, hid_gate, act = _ffn_in_act_fwd(u2, w["w_ffn_in"], w["ffn_conv_w"], w["ffn_conv_b"])
    h2 = _mm(act, w["w_ffn_out"], c=h1, mask=True, name="ffn_out")
    dh2, loss_row, g_norm_final = _final_loss(h2, w["norm_final_w"], target)

    grads = {"norm_final_w": g_norm_final}
    grads["w_ffn_out"] = _mm(act, dh2, ta=True, mask=True, out_dtype=BF16, name="g_w_ffn_out")
    dx_up, dx_gate, dcw_up, dcw_gate, dcb_up, dcb_gate = _ffn_out_act_bwd(
        dh2, w["w_ffn_out"], hid_up, hid_gate, x_up, x_gate, w["ffn_conv_w"])
    grads["ffn_conv_w"] = jnp.concatenate([dcw_up, dcw_gate], axis=1)
    grads["ffn_conv_b"] = jnp.concatenate([dcb_up, dcb_gate], axis=1)
    (dh1, grads["norm_ffn_w"]), _ = _mm_rms_bwd([(dx_up, w_ffn_up), (dx_gate, w_ffn_gate)], h1, w["norm_ffn_w"], dh2,
                                                name="d_u2_rms_bwd")
    grads["w_ffn_in"] = (_mm(u2, dx_up, ta=True, out_dtype=BF16, name="g_w_ffn_up"),
                         _mm(u2, dx_gate, ta=True, out_dtype=BF16, name="g_w_ffn_gate"))

    grads["w_out"] = _mm(merged, dh1, ta=True, mask=True, out_dtype=BF16, name="g_w_out")
    dy_ssd, dy_att, dgates, grads["gate_b"] = _merge_out_bwd(dh1, w["w_out"], gates, y_ssd, y_att, w["gate_b"])
    dyn = _mm(dy_ssd, w["w_ssd_branch"], tb=True, name="d_yn")
    grads["w_ssd_branch"] = _mm(yn, dy_ssd, ta=True, out_dtype=BF16, name="g_w_ssd")
    datt = _mm(dy_att, w["w_attn_branch"], tb=True, out_dtype=BF16, name="d_att")
    grads["w_attn_branch"] = _mm(att, dy_att, ta=True, out_dtype=BF16, name="g_w_att")
    (dz, dpxs, dpb, dpc, ddt, grads["ssd_norm_w"], g_dtb, g_alog, g_dskip), received = _ssd_bwd(
        dyn, y, z, pre, dt_raw, hsave, dtb, alog, dskip_w, w["ssd_norm_w"],
        side=None if exchanges is None else exchanges.early_grads_plan(grads))
    if exchanges is not None:
        exchanges.early_grads_received = received
    grads["ssd_dt_bias"] = _dt_gather(g_dtb)
    grads["ssd_a_log"] = _dt_gather(g_alog)
    grads["ssd_d"] = _dt_gather(g_dskip)
    conv_g = _conv_bwd(dpxs, xbc, w["ssd_conv_w"], name="ssd_conv_bwd_x")
    conv_g = _conv_bwd(dpb, xbc, w["ssd_conv_w"], name="ssd_conv_bwd_b", col0=SSD_INNER, into=conv_g)
    dxbc, grads["ssd_conv_w"], grads["ssd_conv_b"] = _conv_bwd(
        dpc, xbc, w["ssd_conv_w"], name="ssd_conv_bwd_c", col0=SSD_INNER + SSD_BC, into=conv_g)
    dqkv, d_tables, d_sinks = _attn_bwd(datt, qkv, tables, sinks)
    grads["attn_sinks"] = d_sinks[:, :ATT_HEADS]
    dtab = jnp.transpose(d_tables, (1, 0, 2, 3)).reshape(ATT_HEADS, NT_ALL)
    grads["rel_bias"] = _bias_grad(dtab, onehot_t).T
    dsegs = {"z": dz, "xbc": dxbc, "dt": ddt, "qkv": dqkv, "g": dgates}
    grads["in_segs"] = [_mm(u, dsegs[nm], ta=True, out_dtype=BF16, name="g_w_in_" + nm) for nm, _ in _IN_SEGS]
    (dh0, grads["norm_mix_w"]), received = _mm_rms_bwd(
        [(dsegs[nm], segs[nm]) for nm, _ in _IN_SEGS], h0, w["norm_mix_w"], dh1, name="d_u_rms_bwd",
        side=None if exchanges is None else exchanges.w_in_grads_plan(grads["in_segs"]))
    if exchanges is not None:
        exchanges.w_in_grads_received = received[0]
    grads["meta_tokens"] = dh0[PAD:BLOCK]
    return loss_row[0, 0], dh0[BLOCK:], grads


def kernel(x, meta_tokens, norm_mix_w, w_in, ssd_conv_w, ssd_conv_b, ssd_dt_bias, ssd_a_log, ssd_d, ssd_norm_w, w_ssd_branch, w_attn_branch, attn_sinks, rel_bias, gate_b, w_out, norm_ffn_w, w_ffn_in, ffn_conv_w, ffn_conv_b, w_ffn_out, norm_final_w, loss_target, m_meta_tokens, m_norm_mix_w, m_w_in, m_ssd_conv_w, m_ssd_conv_b, m_ssd_dt_bias, m_ssd_a_log, m_ssd_d, m_ssd_norm_w, m_w_ssd_branch, m_w_attn_branch, m_attn_sinks, m_rel_bias, m_gate_b, m_w_out, m_norm_ffn_w, m_w_ffn_in, m_ffn_conv_w, m_ffn_conv_b, m_w_ffn_out, m_norm_final_w, v_meta_tokens, v_norm_mix_w, v_w_in, v_ssd_conv_w, v_ssd_conv_b, v_ssd_dt_bias, v_ssd_a_log, v_ssd_d, v_ssd_norm_w, v_w_ssd_branch, v_w_attn_branch, v_attn_sinks, v_rel_bias, v_gate_b, v_w_out, v_norm_ffn_w, v_w_ffn_in, v_ffn_conv_w, v_ffn_conv_b, v_w_ffn_out, v_norm_final_w):
    shard = dict(meta_tokens=meta_tokens, norm_mix_w=norm_mix_w, w_in=w_in, ssd_conv_w=ssd_conv_w,
                 ssd_conv_b=ssd_conv_b, ssd_dt_bias=ssd_dt_bias, ssd_a_log=ssd_a_log, ssd_d=ssd_d,
                 ssd_norm_w=ssd_norm_w, w_ssd_branch=w_ssd_branch, w_attn_branch=w_attn_branch,
                 attn_sinks=attn_sinks, rel_bias=rel_bias, gate_b=gate_b, w_out=w_out, norm_ffn_w=norm_ffn_w,
                 w_ffn_in=w_ffn_in, ffn_conv_w=ffn_conv_w, ffn_conv_b=ffn_conv_b, w_ffn_out=w_ffn_out,
                 norm_final_w=norm_final_w)
    mom_m = dict(zip(_WEIGHTS, (m_meta_tokens, m_norm_mix_w, m_w_in, m_ssd_conv_w, m_ssd_conv_b, m_ssd_dt_bias,
                                m_ssd_a_log, m_ssd_d, m_ssd_norm_w, m_w_ssd_branch, m_w_attn_branch, m_attn_sinks,
                                m_rel_bias, m_gate_b, m_w_out, m_norm_ffn_w, m_w_ffn_in, m_ffn_conv_w, m_ffn_conv_b,
                                m_w_ffn_out, m_norm_final_w)))
    mom_v = dict(zip(_WEIGHTS, (v_meta_tokens, v_norm_mix_w, v_w_in, v_ssd_conv_w, v_ssd_conv_b, v_ssd_dt_bias,
                                v_ssd_a_log, v_ssd_d, v_ssd_norm_w, v_w_ssd_branch, v_w_attn_branch, v_attn_sinks,
                                v_rel_bias, v_gate_b, v_w_out, v_norm_ffn_w, v_w_ffn_in, v_ffn_conv_w, v_ffn_conv_b,
                                v_w_ffn_out, v_norm_final_w)))
    orig_shape = {k: a.shape for k, a in shard.items()}
    two_d = {k: a.reshape(a.shape[-2:]) if a.ndim >= 2 else a.reshape(1, -1) for k, a in shard.items()}
    shape2 = {k: a.shape for k, a in two_d.items()}

    def as2d(tree):
        return {k: tree[k].reshape(shape2[k]) for k in _WEIGHTS}

    mom_m, mom_v = as2d(mom_m), as2d(mom_v)

    exchanges = _LateExchanges(two_d, shape2)
    row_pack = exchanges.row_pack
    small_pack = _pack_rows([two_d[k] for k in _SMALL_SHARDED], LANES, SMALL_ROW_MULT)
    w_in_all, small_all = _gather_two_level([two_d["w_in"].astype(BF16), small_pack], name="gather_weights")
    full = {k: two_d[k] for k in _SMALL_REPLICATED}
    full["in_segs"] = _w_in_to_segments(w_in_all)
    small_flat = small_all.reshape(N_DEV, -1)
    off = 0
    for k in _SMALL_SHARDED:
        size = int(np.prod(shape2[k]))
        full[k] = _gather_full(small_flat[:, off:off + size].reshape((N_DEV,) + shape2[k]), k, shape2[k])
        off += size

    loss_local, grad_x, grads = _local_step(x[0], loss_target[0], full, exchanges)

    small_names = _SMALL_SHARDED + _SMALL_REPLICATED
    small_send = _pack_rows([grads[k] for k in small_names] + [loss_local.reshape(1)], LANES, SMALL_ROW_MULT)
    small_recv, = _run_plan(_all_to_all_plan([small_send], [False]), "exchange_small_grads")
    in_recv = exchanges.w_in_grads_received
    ffn_recv, rows_recv = exchanges.early_grads_received

    big = {"w_in": _adamw(two_d["w_in"], in_recv, mom_m["w_in"], mom_v["w_in"], name="adamw_w_in"),
           "w_ffn_in": _adamw(two_d["w_ffn_in"], ffn_recv, mom_m["w_ffn_in"], mom_v["w_ffn_in"], name="adamw_w_ffn_in")}
    rows_out = _adamw(row_pack(two_d), rows_recv, row_pack(mom_m), row_pack(mom_v), name="adamw_rows")
    off = 0
    for k in _ROW_SHARDED:
        r = shape2[k][0]
        big[k] = [a[off:off + r] for a in rows_out]
        off += r
    me =4 * lax.axis_index("x") + 2 * lax.axis_index("y") + lax.axis_index("c")
    small_full_shapes = [grads[k].shape for k in small_names]
    n_small = sum(int(np.prod(s)) for s in small_full_shapes)

    def packed_small(tree):
        parts = []
        for k in small_names:
            a = tree[k]
            if k in _SMALL_SHARDED:
                fullw = jnp.zeros(grads[k].shape, F32)
                a = lax.dynamic_update_slice(fullw, a, (0, me * a.shape[1]))
            parts.append(a)
        return _pack_rows(parts + [jnp.zeros((1,), F32)], LANES, SMALL_ROW_MULT)

    g_small, d_small, m_small, v_small = _adamw(packed_small(two_d), small_recv, packed_small(mom_m),
                                                packed_small(mom_v), name="adamw_small")

    def unpack_all(which, small):
        out = {k: big[k][which] for k in _BIG}
        flat = small.reshape(-1)
        for k, a in zip(small_names, _unpack(flat, small_full_shapes)):
            if k in _SMALL_SHARDED:
                a = lax.dynamic_slice(a, (0, me * shape2[k][1]), shape2[k])
            out[k] = a
        return out, flat[n_small]

    g_all, loss = unpack_all(0, g_small)
    d_all, _ = unpack_all(1, d_small)
    m_all, _ = unpack_all(2, m_small)
    v_all, _ = unpack_all(3, v_small)

    def final(tree):
        return [tree[k].reshape(orig_shape[k]) for k in _WEIGHTS]

    return (loss, grad_x[None], *final(g_all), *final(d_all), *final(m_all), *final(v_all))
```

```python
import functools
import math

import numpy as np
import jax
import jax.numpy as jnp
from jax import lax
from jax.experimental import pallas as pl
from jax.experimental.pallas import tpu as pltpu

F32 = jnp.float32
BF16 = jnp.bfloat16
HIGHEST = lax.Precision.HIGHEST

D_MODEL = 1024
N_META = 16
BLOCK = 128
PAD = BLOCK - N_META
EPS = 1e-6
NEG = -1e30
SSD_INNER = 2 * D_MODEL
SSD_HEADDIM = 64
SSD_HEADS = SSD_INNER // SSD_HEADDIM
SSD_GROUPS = 4
SSD_HPG = SSD_HEADS // SSD_GROUPS
SSD_STATE = 128
SSD_CONV = 4
SSD_GW = SSD_HPG * SSD_HEADDIM
SSD_BC = SSD_GROUPS * SSD_STATE
SSD_XBC = SSD_INNER + 2 * SSD_BC
ATT_HEADS = 16
ATT_KV_HEADS = 2
ATT_HEADDIM = 64
ATT_GQ = ATT_HEADS // ATT_KV_HEADS
ATT_Q = ATT_HEADS * ATT_HEADDIM
ATT_KV = ATT_KV_HEADS * ATT_HEADDIM
REL_BUCKETS = 32
REL_MAX_DIST = 128
D_FF = 2816
FFN_CONV = 3
ADAM_LR = 0.001
ADAM_B1 = 0.9
ADAM_B2 = 0.999
ADAM_EPS = 1e-08
ADAM_WD = 0.01
ADAM_STEP = 10

N_DEV = 8
LANES = 128
SUBLANES = 8
DT_W = SSD_GROUPS * LANES
VMEM_LIMIT_BYTES = 56 * 1024 * 1024
MESH = pl.DeviceIdType.MESH

SMALL_ROW_MULT = 16

N_KEYS = 3 * BLOCK
NT_ALL = 3 * N_KEYS * BLOCK
NT_TILE = 8192


def _cparams(*sem):
    return pltpu.CompilerParams(dimension_semantics=sem, vmem_limit_bytes=VMEM_LIMIT_BYTES)


def _row_tile(n, cap):
    best = None
    for t in range(16, min(n, cap) + 1, 16):
        if n % t == 0:
            best = t
    return best or n


def _col_tile(n, cap):
    for t in (1408, 1280, 1024, 768, 640, 512, 384, 256, 128):
        if t <= cap and n % t == 0:
            return t
    return n


def _sigmoid(x):
    return 0.5 * jnp.tanh(0.5 * x) + 0.5


def _silu(x):
    return x * _sigmoid(x)


def _softplus(x):
    return jnp.maximum(x, 0.0) + jnp.log(1.0 + jnp.exp(-jnp.abs(x)))


def _dot_nt(a, b):
    return lax.dot_general(a, b, (((1,), (1,)), ((), ())), preferred_element_type=F32)


def _dot_tn(a, b):
    return lax.dot_general(a, b, (((0,), (0,)), ((), ())), preferred_element_type=F32)


def _dot(a, b):
    return jnp.dot(a, b, preferred_element_type=F32)


def _bf16_terms(x, terms):
    out, rest = [], x
    for _ in range(terms):
        part = rest.astype(BF16)
        out.append(part)
        rest = rest - part.astype(F32)
    return out


def _dot_sel(x, sel, terms=3):
    return sum(_dot(part, sel) for part in _bf16_terms(x, terms))


def _sel_dot(sel, x, terms=3):
    return sum(_dot(sel, part) for part in _bf16_terms(x, terms))


def _sum_all(x):
    return jnp.sum(jnp.sum(x, axis=1, keepdims=True), axis=0, keepdims=True)


MM_ROW_CAPS = (2080, 1664, 832, 416)
MM_COL_CAP = 1408
MM_VMEM_BUDGET = 44 * 1024 * 1024


def _mm_tiles(rows, cols, vmem_bytes):
    col_cands = [t for t in (2048, 1536, 1408, 1280, 1024, 768, 640, 512, 384, 256, 128) if cols % t == 0]
    if cols <= 2 * MM_COL_CAP:
        col_cands.append(cols)
    best = None
    for cap in MM_ROW_CAPS:
        tr = _row_tile(rows, cap)
        for tc in col_cands:
            if vmem_bytes(tr, tc) <= MM_VMEM_BUDGET and (best is None or tr * tc > best[0] * best[1]):
                best = (tr, tc)
    assert best is not None, (rows, cols)
    return best


def _mm(a, b, *, name, ta=False, tb=False, c=None, mask=False, out_dtype=F32):
    if not ta:
        m, k = a.shape
        n = b.shape[0] if tb else b.shape[1]
        tm, tn = _mm_tiles(m, n, lambda t_m, t_n: 2 * (t_m * k * a.dtype.itemsize + k * t_n * b.dtype.itemsize
                                                       + t_m * t_n * (jnp.dtype(out_dtype).itemsize
                                                                      + (0 if c is None else c.dtype.itemsize)))
                           + 4 * t_m * t_n)

        def body(*refs):
            if c is None:
                a_ref, b_ref, o_ref = refs
            else:
                a_ref, b_ref, c_ref, o_ref = refs
            acc = (_dot_nt if tb else _dot)(a_ref[...].astype(BF16), b_ref[...].astype(BF16))
            if mask:
                row = pl.program_id(0) * tm + lax.broadcasted_iota(jnp.int32, (tm, 1), 0)
                acc = jnp.where(row >= PAD, acc, 0.0)
            if c is not None:
                acc = acc + c_ref[...]
            o_ref[...] = acc.astype(out_dtype)

        b_spec = pl.BlockSpec((tn, k), lambda i, j: (j, 0)) if tb else pl.BlockSpec((k, tn), lambda i, j: (0, j))
        in_specs = [pl.BlockSpec((tm, k), lambda i, j: (i, 0)), b_spec]
        args = [a, b]
        if c is not None:
            in_specs.append(pl.BlockSpec((tm, tn), lambda i, j: (i, j)))
            args.append(c)
        return pl.pallas_call(
            body, name=name, grid=(m // tm, n // tn), in_specs=in_specs,
            out_specs=pl.BlockSpec((tm, tn), lambda i, j: (i, j)),
            out_shape=jax.ShapeDtypeStruct((m, n), out_dtype),
            compiler_params=_cparams("parallel", "parallel"))(*args)

    kc, m = a.shape
    n = b.shape[1]
    tm = _col_tile(m, MM_COL_CAP)
    tk, tn = _mm_tiles(kc, n, lambda t_k, t_n: 2 * (t_k * tm * a.dtype.itemsize + t_k * t_n * b.dtype.itemsize
                                                    + tm * t_n * jnp.dtype(out_dtype).itemsize) + 8 * tm * t_n)

    n_k = kc // tk

    def body_t(a_ref, b_ref, o_ref, acc_ref):
        kk = pl.program_id(2)
        bb = b_ref[...]
        if mask:
            row = kk * tk + lax.broadcasted_iota(jnp.int32, (tk, 1), 0)
            bb = jnp.where(row >= PAD, bb, jnp.zeros_like(bb))
        p = _dot_tn(a_ref[...].astype(BF16), bb.astype(BF16))

        @pl.when(kk == 0)
        def _():
            acc_ref[...] = p

        @pl.when(kk > 0)
        def _():
            acc_ref[...] += p

        @pl.when(kk == n_k - 1)
        def _():
            o_ref[...] = acc_ref[...].astype(out_dtype)

    return pl.pallas_call(
        body_t, name=name, grid=(m // tm, n // tn, n_k),
        in_specs=[pl.BlockSpec((tk, tm), lambda i, j, kk: (kk, i)), pl.BlockSpec((tk, tn), lambda i, j, kk: (kk, j))],
        out_specs=pl.BlockSpec((tm, tn), lambda i, j, kk: (i, j)),
        out_shape=jax.ShapeDtypeStruct((m, n), out_dtype),
        scratch_shapes=[pltpu.VMEM((tm, tn), F32)],
        compiler_params=_cparams("parallel", "parallel", "arbitrary"))(a, b)


def _mm_rms_bwd(pairs, x, w, dres, *, name, side=None):
    m, d = x.shape
    tm = _row_tile(m, 416)
    n_pairs = len(pairs)

    def body(*refs):
        a_refs, b_refs = refs[:n_pairs], refs[n_pairs:2 * n_pairs]
        x_ref, w_ref, dres_ref, dx_ref, dw_ref = refs[2 * n_pairs:]
        i = pl.program_id(0)
        dyv = None
        for a_ref, b_ref in zip(a_refs, b_refs):
            term = _dot_nt(a_ref[...].astype(BF16), b_ref[...])
            dyv = term if dyv is None else dyv + term
        xv = x_ref[...]
        r = lax.rsqrt(jnp.mean(xv * xv, axis=-1, keepdims=True) + EPS)
        xh = xv * r
        g = dyv * w_ref[...]
        dx_ref[...] = r * (g - xh * jnp.mean(g * xh, axis=-1, keepdims=True)) + dres_ref[...]
        part = jnp.sum(dyv * xh, axis=0, keepdims=True)

        @pl.when(i == 0)
        def _():
            dw_ref[...] = part

        @pl.when(i > 0)
        def _():
            dw_ref[...] += part

    row = pl.BlockSpec((tm, d), lambda i: (i, 0))
    vec = pl.BlockSpec((1, d), lambda i: (0, 0))
    in_specs = ([pl.BlockSpec((tm, a.shape[1]), lambda i: (i, 0)) for a, _ in pairs]
                + [pl.BlockSpec(b.shape, lambda i: (0, 0), pipeline_mode=pl.Buffered(1)) for _, b in pairs]
                + [row, vec, row])
    return _call_with_side(
        body, side, name=name, grid=(m // tm,), in_specs=in_specs, out_specs=[row, vec],
        out_shape=[jax.ShapeDtypeStruct((m, d), F32), jax.ShapeDtypeStruct((1, d), F32)], scratch_shapes=[],
        args=[a for a, _ in pairs] + [b for _, b in pairs] + [x, w, dres], semantics=("arbitrary",))


def _rms_fwd(h, w, *, name):
    n, d = h.shape
    tm = _row_tile(n, 832)

    def body(h_ref, w_ref, o_ref):
        x = h_ref[...]
        r = lax.rsqrt(jnp.mean(x * x, axis=-1, keepdims=True) + EPS)
        o_ref[...] = (x * r * w_ref[...]).astype(BF16)

    return pl.pallas_call(
        body, name=name, grid=(n // tm,),
        in_specs=[pl.BlockSpec((tm, d), lambda i: (i, 0)), pl.BlockSpec((1, d), lambda i: (0, 0))],
        out_specs=pl.BlockSpec((tm, d), lambda i: (i, 0)),
        out_shape=jax.ShapeDtypeStruct((n, d), BF16),
        compiler_params=_cparams("parallel"))(h, w)


def _final_loss(h, w, target):
    n, d = h.shape
    nb = n // BLOCK

    def body(h_ref, w_ref, t_ref, dh_ref, dhb_ref, loss_ref, dw_ref):
        i = pl.program_id(0)
        xv = h_ref[...]
        r = lax.rsqrt(jnp.mean(xv * xv, axis=-1, keepdims=True) + EPS)
        xh = xv * r
        wv = w_ref[...]
        err = jnp.where(i >= 1, xh * wv - t_ref[...], 0.0)
        dyv = err * (1.0 / d)
        g = dyv * wv
        dh = r * (g - xh * jnp.mean(g * xh, axis=-1, keepdims=True))
        dh_ref[...] = dh
        dhb_ref[...] = dh.astype(BF16)
        lpart = jnp.broadcast_to(0.5 * _sum_all(err * err) * (1.0 / d), (1, LANES))
        wpart = jnp.sum(dyv * xh, axis=0, keepdims=True)

        @pl.when(i == 0)
        def _():
            loss_ref[...] = lpart
            dw_ref[...] = wpart

        @pl.when(i > 0)
        def _():
            loss_ref[...] += lpart
            dw_ref[...] += wpart

    row = pl.BlockSpec((BLOCK, d), lambda i: (i, 0))
    vec = pl.BlockSpec((1, d), lambda i: (0, 0))
    return pl.pallas_call(
        body, name="final_loss", grid=(nb,),
        in_specs=[row, vec, pl.BlockSpec((BLOCK, d), lambda i: (jnp.maximum(i - 1, 0), 0))],
        out_specs=[row, row, pl.BlockSpec((1, LANES), lambda i: (0, 0)), vec],
        out_shape=[jax.ShapeDtypeStruct((n, d), F32), jax.ShapeDtypeStruct((n, d), BF16),
                   jax.ShapeDtypeStruct((1, LANES), F32), jax.ShapeDtypeStruct((1, d), F32)],
        compiler_params=_cparams("arbitrary"))(h, w, target)


def _main_spec(tm, cb, off=0):
    return pl.BlockSpec((tm, cb), lambda j, i: (i, j + off))


def _prev_spec(tm, cb, off=0):
    r8 = tm // SUBLANES
    return pl.BlockSpec((SUBLANES, cb), lambda j, i: (jnp.maximum(i * r8 - 1, 0), j + off))


def _next_spec(tm, cb, n_rows, off=0):
    r8 = tm // SUBLANES
    last = n_rows // SUBLANES - 1
    return pl.BlockSpec((SUBLANES, cb), lambda j, i: (jnp.minimum((i + 1) * r8, last), j + off))


def _with_prev(prev_ref, main_ref, i):
    prev = jnp.where(i > 0, prev_ref[...], 0.0)
    return jnp.concatenate([prev, main_ref[...]], axis=0)


def _with_next(main, nxt, i, n_tiles):
    return jnp.concatenate([main, jnp.where(i < n_tiles - 1, nxt, 0.0)], axis=0)


def _back(xx, s, tm):
    if s == 0:
        return xx[SUBLANES:SUBLANES + tm]
    return pltpu.roll(xx, s, 0)[SUBLANES:SUBLANES + tm]


def _ahead(xx, s, tm):
    if s == 0:
        return xx[:tm]
    return pltpu.roll(xx, tm + SUBLANES - s, 0)[:tm]


def _mm_conv_fwd(u, w_in, w, b, *, name):
    n = u.shape[0]
    cdim = w_in.shape[1]
    kw = w.shape[0]
    tm = _row_tile(n, 832)
    cb = _col_tile(cdim, 512)
    nt = n // tm

    def body(u_ref, w_in_ref, w_ref, b_ref, x_ref, o_ref, acc_scr, halo_scr):
        j, i = pl.program_id(0), pl.program_id(1)

        @pl.when((j == 0) & (i == 0))
        def _():
            acc_scr[...] = jnp.zeros_like(acc_scr)
            halo_scr[...] = jnp.zeros_like(halo_scr)

        new = _dot(u_ref[...], w_in_ref[...])
        prev = acc_scr[...]
        xx = jnp.concatenate([jnp.where(i >= 2, halo_scr[...], 0.0), prev], axis=0)
        acc = jnp.broadcast_to(b_ref[...], (tm, cb))
        for k in range(kw):
            acc = acc + w_ref[k:k + 1, :] * _back(xx, kw - 1 - k, tm)
        x_ref[...] = prev
        o_ref[...] = acc
        halo_scr[...] = prev[tm - SUBLANES:, :]
        acc_scr[...] = new

    out = pl.BlockSpec((tm, cb), lambda j, i: (jnp.maximum(i - 1, 0), j))
    shp = jax.ShapeDtypeStruct((n, cdim), F32)
    return pl.pallas_call(
        body, name=name, grid=(cdim // cb, nt + 1),
        in_specs=[pl.BlockSpec((tm, u.shape[1]), lambda j, i: (jnp.minimum(i, nt - 1), 0)),
                  pl.BlockSpec((w_in.shape[0], cb), lambda j, i: (0, j)),
                  pl.BlockSpec((kw, cb), lambda j, i: (0, j)), pl.BlockSpec((1, cb), lambda j, i: (0, j))],
        out_specs=[out, out], out_shape=[shp, shp],
        scratch_shapes=[pltpu.VMEM((tm, cb), F32), pltpu.VMEM((SUBLANES, cb), F32)],
        compiler_params=_cparams("arbitrary", "arbitrary"))(u, w_in, w, b)


def _conv_bwd_core(dpre_ext, x, w_ref, kw, tm):
    dx = None
    dws = []
    for k in range(kw):
        shifted = _ahead(dpre_ext, kw - 1 - k, tm)
        term = w_ref[k:k + 1, :] * shifted
        dx = term if dx is None else dx + term
        dws.append(jnp.sum(shifted * x, axis=0, keepdims=True))
    return dx, dws, jnp.sum(dpre_ext[:tm], axis=0, keepdims=True)


def _acc_rows(i, dw_ref, db_ref, dws, db):
    @pl.when(i == 0)
    def _():
        for k, v in enumerate(dws):
            dw_ref[k:k + 1, :] = v
        db_ref[...] = db

    @pl.when(i > 0)
    def _():
        for k, v in enumerate(dws):
            dw_ref[k:k + 1, :] += v
        db_ref[...] += db


def _conv_bwd(dpre, x, w, *, name, col0=0, into=None):
    n, cdim = x.shape
    kw = w.shape[0]
    tm = _row_tile(n, 832)
    cb = _col_tile(cdim, 512)
    nt = n // tm
    off = col0 // cb
    n_alias = 0 if into is None else 3

    def body(d_ref, dn_ref, x_ref, w_ref, *rest):
        dx_ref, dw_ref, db_ref = rest[n_alias:]
        i = pl.program_id(1)
        dpre_ext = _with_next(d_ref[...], dn_ref[...], i, nt)
        dx, dws, db = _conv_bwd_core(dpre_ext, x_ref[...], w_ref, kw, tm)
        dx_ref[...] = dx.astype(BF16)
        _acc_rows(i, dw_ref, db_ref, dws, db)

    wspec = pl.BlockSpec((kw, cb), lambda j, i: (0, j + off))
    bspec = pl.BlockSpec((1, cb), lambda j, i: (0, j + off))
    return pl.pallas_call(
        body, name=name, grid=(dpre.shape[1] // cb, nt),
        in_specs=[_main_spec(tm, cb), _next_spec(tm, cb, n), _main_spec(tm, cb, off), wspec]
        + [pl.BlockSpec(memory_space=pl.ANY)] * n_alias,
        out_specs=[_main_spec(tm, cb, off), wspec, bspec],
        out_shape=[jax.ShapeDtypeStruct((n, cdim), BF16), jax.ShapeDtypeStruct((kw, cdim), F32),
                   jax.ShapeDtypeStruct((1, cdim), F32)],
        input_output_aliases={4 + k: k for k in range(n_alias)},
        compiler_params=_cparams("parallel", "arbitrary"))(dpre, dpre, x, w, *(into or ()))


def _ffn_in_act_fwd(u, w_in, w, b):
    n = u.shape[0]
    kw = w.shape[0]
    tm = _row_tile(n, 832)
    cb = _col_tile(D_FF, 256)
    nc = D_FF // cb
    nt = n // tm

    def body(u_ref, wu_in_ref, wg_in_ref, wu_ref, wg_ref, bu_ref, bg_ref,
             xu_ref, xg_ref, hu_ref, hg_ref, act_ref, acc_scr, halo_scr):
        j, i = pl.program_id(0), pl.program_id(1)

        @pl.when((j == 0) & (i == 0))
        def _():
            acc_scr[...] = jnp.zeros_like(acc_scr)
            halo_scr[...] = jnp.zeros_like(halo_scr)

        ub = u_ref[...]
        new = [_dot(ub, wu_in_ref[...]), _dot(ub, wg_in_ref[...])]
        hid = []
        for half, (x_ref, w_ref, b_ref) in enumerate(((xu_ref, wu_ref, bu_ref), (xg_ref, wg_ref, bg_ref))):
            prev = acc_scr[half]
            xx = jnp.concatenate([jnp.where(i >= 2, halo_scr[half], 0.0), prev], axis=0)
            acc = jnp.broadcast_to(b_ref[...], (tm, cb))
            for k in range(kw):
                acc = acc + w_ref[k:k + 1, :] * _back(xx, kw - 1 - k, tm)
            x_ref[...] = prev
            hid.append(acc)
            halo_scr[half] = prev[tm - SUBLANES:, :]
            acc_scr[half] = new[half]
        hu_ref[...] = hid[0]
        hg_ref[...] = hid[1]
        act_ref[...] = (_silu(hid[1]) * hid[0]).astype(BF16)

    def wspec(off):
        return pl.BlockSpec((kw, cb), lambda j, i: (0, j + off))

    def bspec(off):
        return pl.BlockSpec((1, cb), lambda j, i: (0, j + off))

    def in_w(off):
        return pl.BlockSpec((w_in.shape[0], cb), lambda j, i: (0, j + off))

    out = pl.BlockSpec((tm, cb), lambda j, i: (jnp.maximum(i - 1, 0), j))
    f32_out = jax.ShapeDtypeStruct((n, D_FF), F32)
    return pl.pallas_call(
        body, name="ffn_in_act_fwd", grid=(nc, nt + 1),
        in_specs=[pl.BlockSpec((tm, u.shape[1]), lambda j, i: (jnp.minimum(i, nt - 1), 0)), in_w(0), in_w(nc),
                  wspec(0), wspec(nc), bspec(0), bspec(nc)],
        out_specs=[out] * 5,
        out_shape=[f32_out, f32_out, f32_out, f32_out, jax.ShapeDtypeStruct((n, D_FF), BF16)],
        scratch_shapes=[pltpu.VMEM((2, tm, cb), F32), pltpu.VMEM((2, SUBLANES, cb), F32)],
        compiler_params=_cparams("arbitrary", "arbitrary"))(u, w_in, w_in, w, w, b, b)


def _ffn_out_act_bwd(dh, w_out, hu, hg, x_up, x_gate, w):
    n = x_up.shape[0]
    kw = w.shape[0]
    tm = _row_tile(n, 832)
    cb = _col_tile(D_FF, 256)
    nc = D_FF // cb
    nt = n // tm

    def body(dh_ref, wo_ref, hu_ref, hun_ref, hg_ref, hgn_ref, xu_ref, xg_ref, wu_ref, wg_ref,
             dxu_ref, dxg_ref, dwu_ref, dwg_ref, dbu_ref, dbg_ref, acc_scr, halo_scr):
        j, i = pl.program_id(0), pl.program_id(1)

        @pl.when((j == 0) & (i == 0))
        def _():
            acc_scr[...] = jnp.zeros_like(acc_scr)
            halo_scr[...] = jnp.zeros_like(halo_scr)

        tile = jnp.maximum(nt - 1 - i, 0)
        row = tile * tm + lax.broadcasted_iota(jnp.int32, (tm, 1), 0)
        new = jnp.where(row >= PAD, _dot_nt(dh_ref[...].astype(BF16), wo_ref[...]), 0.0)
        prev = jnp.where(i >= 1, acc_scr[...], 0.0)
        dact_e = jnp.concatenate([prev, jnp.where(i >= 2, halo_scr[...], 0.0)], axis=0)
        last = nt - i >= nt - 1
        up_e = jnp.concatenate([hu_ref[...], jnp.where(last, 0.0, hun_ref[...])], axis=0)
        gate_e = jnp.concatenate([hg_ref[...], jnp.where(last, 0.0, hgn_ref[...])], axis=0)
        halo_scr[...] = prev[:SUBLANES, :]
        acc_scr[...] = new
        sg = _sigmoid(gate_e)
        dup_e = dact_e * (gate_e * sg)
        dgate_e = dact_e * up_e * (sg * (1.0 + gate_e * (1.0 - sg)))
        dx, dws, db = _conv_bwd_core(dup_e, xu_ref[...], wu_ref, kw, tm)
        dxu_ref[...] = dx.astype(BF16)
        _acc_rows(i, dwu_ref, dbu_ref, dws, db)
        dx, dws, db = _conv_bwd_core(dgate_e, xg_ref[...], wg_ref, kw, tm)
        dxg_ref[...] = dx.astype(BF16)
        _acc_rows(i, dwg_ref, dbg_ref, dws, db)

    def done_tile(i):
        return jnp.minimum(nt - i, nt - 1)

    r8 = tm // SUBLANES
    main = pl.BlockSpec((tm, cb), lambda j, i: (done_tile(i), j))
    nxt = pl.BlockSpec((SUBLANES, cb), lambda j, i: (jnp.minimum((done_tile(i) + 1) * r8, n // SUBLANES - 1), j))
    wspec0 = pl.BlockSpec((kw, cb), lambda j, i: (0, j))
    wspec1 = pl.BlockSpec((kw, cb), lambda j, i: (0, j + nc))
    bspec = pl.BlockSpec((1, cb), lambda j, i: (0, j))
    return pl.pallas_call(
        body, name="ffn_out_act_bwd", grid=(nc, nt + 1),
        in_specs=[pl.BlockSpec((tm, dh.shape[1]), lambda j, i: (jnp.maximum(nt - 1 - i, 0), 0)),
                  pl.BlockSpec((cb, w_out.shape[1]), lambda j, i: (j, 0)),
                  main, nxt, main, nxt, main, main, wspec0, wspec1],
        out_specs=[main, main, wspec0, wspec0, bspec, bspec],
        out_shape=[jax.ShapeDtypeStruct((n, D_FF), BF16), jax.ShapeDtypeStruct((n, D_FF), BF16),
                   jax.ShapeDtypeStruct((kw, D_FF), F32), jax.ShapeDtypeStruct((kw, D_FF), F32),
                   jax.ShapeDtypeStruct((1, D_FF), F32), jax.ShapeDtypeStruct((1, D_FF), F32)],
        scratch_shapes=[pltpu.VMEM((tm, cb), F32), pltpu.VMEM((SUBLANES, cb), F32)],
        compiler_params=_cparams("arbitrary", "arbitrary"))(dh, w_out, hu, hu, hg, hg, x_up, x_gate, w, w)


def _ssd_prep(pxs_ref, pb_ref, pc_ref, dtr_ref, dtb_ref, alog_ref, c):
    xs = _silu(pxs_ref[...])
    bm = _silu(pb_ref[...])
    cm = _silu(pc_ref[...])
    return (xs, bm, cm) + _ssd_decay(dtr_ref, dtb_ref, alog_ref, c)


def _ssd_decay(dtr_ref, dtb_ref, alog_ref, c):
    row =lax.broadcasted_iota(jnp.int32, (BLOCK, 1), 0) + c * BLOCK
    valid = (row >= PAD).astype(F32)
    dtr = dtr_ref[...] + dtb_ref[...]
    dt = _softplus(dtr) * valid
    a = -jnp.exp(alog_ref[...])
    lam = dt * a
    ri = lax.broadcasted_iota(jnp.int32, (BLOCK, BLOCK), 0)
    ci = lax.broadcasted_iota(jnp.int32, (BLOCK, BLOCK), 1)
    causal = ci <= ri
    cs = _sel_dot(causal.astype(BF16), lam)
    return valid, dtr, dt, a, lam, cs, causal


def _head_cols(r):
    return slice(SSD_HEADDIM * r, SSD_HEADDIM * (r + 1))


def _ssd_specs(nc, rev):
    def cidx(c):
        return nc - 1 - c if rev else c

    xs = pl.BlockSpec((BLOCK, SSD_GW), lambda g, c: (cidx(c), g))
    bspec = pl.BlockSpec((BLOCK, SSD_STATE), lambda g, c: (cidx(c), SSD_INNER // SSD_STATE + g))
    cspec = pl.BlockSpec((BLOCK, SSD_STATE), lambda g, c: (cidx(c), (SSD_INNER + SSD_BC) // SSD_STATE + g))
    lane = pl.BlockSpec((BLOCK, LANES), lambda g, c: (cidx(c), g))
    vec = pl.BlockSpec((1, LANES), lambda g, c: (0, g))
    wide_vec = pl.BlockSpec((1, SSD_GW), lambda g, c: (0, g))
    hsave = pl.BlockSpec((1, 1, SSD_GW, SSD_STATE), lambda g, c: (cidx(c), g, 0, 0))
    return xs, bspec, cspec, lane, vec, wide_vec, hsave


def _head_spread_matrix():
    r = lax.broadcasted_iota(jnp.int32, (LANES, SSD_GW), 0)
    col = lax.broadcasted_iota(jnp.int32, (LANES, SSD_GW), 1)
    return (col // SSD_HEADDIM == r).astype(BF16)


def _const_spec(shape):
    return pl.BlockSpec(shape, lambda g, c: (0,) * len(shape))


def _spread_heads(per_head, e_ref):
    wide = _dot_sel(jnp.concatenate(per_head, axis=0), e_ref[...])
    return [wide[BLOCK * k:BLOCK * (k + 1)] for k in range(len(per_head))]


def _call_with_side(body, side, *, name, grid, in_specs, out_specs, out_shape, scratch_shapes, args,
                    semantics=("parallel", "arbitrary")):
    if side is None:
        outs = pl.pallas_call(body, name=name, grid=grid, in_specs=in_specs, out_specs=out_specs, out_shape=out_shape,
                              scratch_shapes=scratch_shapes, compiler_params=_cparams(*semantics))(*args)
        return outs, []
    n_in, n_out, n_scr, n_side = len(in_specs), len(out_specs), len(scratch_shapes), len(side.arrays)

    def body_with_side(*refs):
        ins, rest = refs[:n_in + n_side], refs[n_in + n_side:]
        outs, scratch = rest[:n_out + n_side], rest[n_out + n_side:]
        side_refs = (ins[n_in:], outs[n_out:], scratch[n_scr:])
        ids = [pl.program_id(k) for k in range(len(grid))]
        inner_first = functools.reduce(jnp.logical_and, [i == 0 for i in ids[1:]], True)

        @pl.when((ids[0] == 0) & inner_first)
        def _():
            side.phases[0](*side_refs)

        body(*ins[:n_in], *outs[:n_out], *scratch[:n_scr])

        @pl.when((ids[0] == grid[0] // 2) & inner_first)
        def _():
            side.phases[1](*side_refs)

        @pl.when(functools.reduce(jnp.logical_and, [i == n - 1 for i, n in zip(ids, grid)]))
        def _():
            side.phases[2](*side_refs)

    any_spec = pl.BlockSpec(memory_space=pl.ANY)
    outs = pl.pallas_call(
        body_with_side, name=name, grid=grid, in_specs=list(in_specs) + [any_spec] * n_side,
        out_specs=list(out_specs) + [any_spec] * n_side, out_shape=list(out_shape) + list(side.out_shape),
        scratch_shapes=list(scratch_shapes) + list(side.scratch_shapes),
        compiler_params=_cparams(*["arbitrary"] * len(grid)))(*args, *side.arrays)
    return outs[:n_out], outs[n_out:]


def _ssd_fwd(pre, dt_raw, z, dtb, alog, dskip_w, norm_w, side=None):
    n = pre.shape[0]
    nc = n // BLOCK
    xs_s, b_s, c_s, lane_s, vec_s, wide_s, hs_s = _ssd_specs(nc, False)

    def body(pxs_ref, pb_ref, pc_ref, dtr_ref, z_ref, dtb_ref, alog_ref, dskw_ref, nw_ref, e_ref,
             y_ref, yn_ref, hs_ref, h_scr):
        c = pl.program_id(1)

        @pl.when(c == 0)
        def _():
            h_scr[...] = jnp.zeros_like(h_scr)

        xs, bm, cm, _, _, dt, _, _, cs, causal = _ssd_prep(pxs_ref, pb_ref, pc_ref, dtr_ref, dtb_ref, alog_ref, c)
        cst = cs.T
        cs_last = cs[BLOCK - 1:BLOCK, :]
        dt_w, ecs_w, dec_w = _spread_heads([dt, jnp.exp(cs), jnp.exp(cs_last - cs)], e_ref)
        xdt = xs * dt_w
        bmb = bm.astype(BF16)
        cmb = cm.astype(BF16)
        cb = _dot_nt(cmb, bmb)
        hg = h_scr[...]
        hs_ref[0, 0] = hg
        y = _dot_nt(cmb, hg.astype(BF16)) * ecs_w + dskw_ref[...] * xs
        first = lax.broadcasted_iota(jnp.int32, (BLOCK, LANES), 1) < SSD_HEADDIM
        diag = []
        for j in range(SSD_HPG // 2):
            xp = xdt[:, LANES * j:LANES * (j + 1)].astype(BF16)
            res = []
            for r in (2 * j, 2 * j + 1):
                lm = jnp.exp(jnp.where(causal, cs[:, r:r + 1] - cst[r:r + 1, :], NEG))
                res.append(_dot((cb * lm).astype(BF16), xp))
            diag.append(jnp.where(first, res[0], res[1]))
        y = y + jnp.concatenate(diag, axis=1)
        st = _dot_tn((xdt * dec_w).astype(BF16), bmb)
        eh = jnp.exp(cs_last)
        for r in range(SSD_HPG):
            rows = _head_cols(r)
            h_scr[rows, :] = hg[rows, :] * eh[:, r:r + 1] + st[rows, :]
        y_ref[...] = y
        gts = y * _silu(z_ref[...])
        rr = lax.rsqrt(jnp.mean(gts * gts, axis=-1, keepdims=True) + EPS)
        yn_ref[...] = (gts * rr * nw_ref[...]).astype(BF16)

    return _call_with_side(
        body, side, name="ssd_fwd", grid=(SSD_GROUPS, nc),
        in_specs=[xs_s, b_s, c_s, lane_s, xs_s, vec_s, vec_s, wide_s, wide_s, _const_spec((LANES, SSD_GW))],
        out_specs=[xs_s, xs_s, hs_s],
        out_shape=[jax.ShapeDtypeStruct((n, SSD_INNER), F32), jax.ShapeDtypeStruct((n, SSD_INNER), BF16),
                   jax.ShapeDtypeStruct((nc, SSD_GROUPS, SSD_GW, SSD_STATE), F32)],
        scratch_shapes=[pltpu.VMEM((SSD_GW, SSD_STATE), F32)],
        args=(pre, pre, pre, dt_raw, z, dtb, alog, dskip_w, norm_w, _head_spread_matrix()))


def _lane_put(acc, col, r):
    lane = lax.broadcasted_iota(jnp.int32, acc.shape, 1)
    return jnp.where(lane == r, col, acc)


def _ssd_bwd(dyn, y, z, pre, dt_raw, hsave, dtb, alog, dskip_w, norm_w, side=None):
    n = pre.shape[0]
    nc = n // BLOCK
    spread = _head_spread_matrix()
    xs_s, b_s, c_s, lane_s, vec_s, wide_s, hs_s = _ssd_specs(nc, True)
    bc_out =pl.BlockSpec((BLOCK, SSD_STATE), lambda g, c: (nc - 1 - c, g))

    def body(dyn_ref, y_ref, z_ref, pxs_ref, pb_ref, pc_ref, dtr_ref, hs_ref, dtb_ref, alog_ref, dskw_ref, nw_ref,
             e_ref, r_ref,
             dz_ref, dxs_ref, dbm_ref, dcm_ref, ddt_ref, dnw_ref, ddtb_ref, dalog_ref, ddsk_ref, g_scr):
        step = pl.program_id(1)
        c = nc - 1 - step

        @pl.when(step == 0)
        def _():
            g_scr[...] = jnp.zeros_like(g_scr)

        pxs, pb, pc = pxs_ref[...], pb_ref[...], pc_ref[...]
        sx, sb, sc = _sigmoid(pxs), _sigmoid(pb), _sigmoid(pc)
        xs, bm, cm = pxs * sx, pb * sb, pc * sc
        valid, dtr, dt, a, lam, cs, causal = _ssd_decay(dtr_ref, dtb_ref, alog_ref, c)
        cst = cs.T
        cs_last = cs[BLOCK - 1:BLOCK, :]
        bmb = bm.astype(BF16)
        cmb = cm.astype(BF16)
        cb = _dot_nt(cmb, bmb)
        hg = hs_ref[0, 0]
        hgb = hg.astype(BF16)
        yoff = _dot_nt(cmb, hgb)
        gn = g_scr[...]
        gnb = gn.astype(BF16)

        zv = z_ref[...]
        yv = y_ref[...]
        sgz = _sigmoid(zv)
        sz = zv * sgz
        gts = yv * sz
        rr = lax.rsqrt(jnp.mean(gts * gts, axis=-1, keepdims=True) + EPS)
        xh = gts * rr
        dynv = dyn_ref[...]
        gg = dynv * nw_ref[...]
        dgts = rr * (gg - xh * jnp.mean(gg * xh, axis=-1, keepdims=True))
        dnw = jnp.sum(dynv * xh, axis=0, keepdims=True)
        dy = dgts * sz
        dz_ref[...] = (dgts * yv * (sgz * (1.0 + zv * (1.0 - sgz)))).astype(BF16)

        ecs = jnp.exp(cs)
        dec = jnp.exp(cs_last - cs)
        eh = jnp.exp(cs_last)
        dt_w, ecs_w, dec_w = _spread_heads([dt, ecs, dec], e_ref)
        red_m = r_ref[...]

        def head_sums(v):
            return _dot_sel(v, red_m, terms=2)

        xdt = xs * dt_w
        q_all = _dot_nt(bmb, gnb)
        w_all = (dy * ecs_w).astype(BF16)
        e_hl = head_sums(q_all * xdt) * dec
        dcs_col = head_sums(dy * yoff) * ecs - e_hl
        gh = jnp.zeros((1, LANES), F32)
        prod = gn * hg
        for r in range(SSD_HPG):
            gh = _lane_put(gh, _sum_all(prod[_head_cols(r), :]), r)
        dcs_last = jnp.sum(e_hl, axis=0, keepdims=True) + eh * gh
        ddsk = jnp.sum(head_sums(dy * xs), axis=0, keepdims=True)
        cbt = _dot_nt(bmb, cmb)
        lane = lax.broadcasted_iota(jnp.int32, (BLOCK, LANES), 1)
        first = lane < SSD_HEADDIM
        causal_t = lax.broadcasted_iota(jnp.int32, (BLOCK, BLOCK), 1) >= lax.broadcasted_iota(
            jnp.int32, (BLOCK, BLOCK), 0)
        sub = lax.broadcasted_iota(jnp.int32, (SUBLANES, BLOCK), 0)
        dcs_row = jnp.zeros((SUBLANES, BLOCK), F32)
        dcb = jnp.zeros((BLOCK, BLOCK), F32)
        dxdt_pairs = []
        for j in range(SSD_HPG // 2):
            tile = slice(LANES * j, LANES * (j + 1))
            dy_p = dy[:, tile]
            dyb = dy_p.astype(BF16)
            xdtb = xdt[:, tile].astype(BF16)
            res = []
            for half, r in enumerate((2 * j, 2 * j + 1)):
                csc, csr = cs[:, r:r + 1], cst[r:r + 1, :]
                lm = jnp.exp(jnp.where(causal, csc - csr, NEG))
                lmt = jnp.exp(jnp.where(causal_t, csr - csc, NEG))
                keep = first if half == 0 else jnp.logical_not(first)
                gm = _dot_nt(jnp.where(keep, dy_p, 0.0).astype(BF16), xdtb) * lm
                dcb = dcb + gm
                mm_ = gm * cb
                dcs_col = dcs_col + jnp.where(lane == r, jnp.sum(mm_, axis=1, keepdims=True), 0.0)
                dcs_row = jnp.where(sub == r, jnp.sum(mm_, axis=0, keepdims=True), dcs_row)
                res.append(_dot((cbt * lmt).astype(BF16), dyb))
            dxdt_pairs.append(jnp.where(first, res[0], res[1]))
        dxdt = jnp.concatenate(dxdt_pairs, axis=1) + q_all * dec_w
        ddt_x = head_sums(dxdt * xs)
        dxs = dxdt * dt_w + dskw_ref[...] * dy
        dcbb = dcb.astype(BF16)
        dcm = _dot(w_all, hgb) + _dot(dcbb, bmb)
        dbm = _dot((xdt * dec_w).astype(BF16), gnb) + _dot_tn(dcbb, cmb)
        dh_off = _dot_tn(w_all, cmb)
        for r in range(SSD_HPG):
            rows = _head_cols(r)
            g_scr[rows, :] = gn[rows, :] * eh[:, r:r + 1] + dh_off[rows, :]

        pad_rows = jnp.zeros((BLOCK - SUBLANES, BLOCK), F32)
        dcs = dcs_col - jnp.concatenate([dcs_row, pad_rows], axis=0).T
        rsel = lax.broadcasted_iota(jnp.int32, (BLOCK, LANES), 0)
        dcs = dcs + jnp.where(rsel == BLOCK - 1, dcs_last, 0.0)
        ri = lax.broadcasted_iota(jnp.int32, (BLOCK, BLOCK), 0)
        ci = lax.broadcasted_iota(jnp.int32, (BLOCK, BLOCK), 1)
        dlam = _sel_dot((ci >= ri).astype(BF16), dcs)
        head = lane < SSD_HPG
        ddt = dlam * a + ddt_x
        ddtr = jnp.where(head, ddt * _sigmoid(dtr) * valid, 0.0)
        ddt_ref[...] = ddtr.astype(BF16)
        dalog = jnp.sum(jnp.where(head, dlam * lam, 0.0), axis=0, keepdims=True)
        ddtb = jnp.sum(ddtr, axis=0, keepdims=True)

        dxs_ref[...] = dxs * (sx * (1.0 + pxs * (1.0 - sx)))
        dbm_ref[...] = dbm * (sb * (1.0 + pb * (1.0 - sb)))
        dcm_ref[...] = dcm * (sc * (1.0 + pc * (1.0 - sc)))

        @pl.when(step == 0)
        def _():
            dnw_ref[...] = dnw
            ddtb_ref[...] = ddtb
            dalog_ref[...] = dalog
            ddsk_ref[...] = ddsk

        @pl.when(step > 0)
        def _():
            dnw_ref[...] += dnw
            ddtb_ref[...] += ddtb
            dalog_ref[...] += dalog
            ddsk_ref[...] += ddsk

    return _call_with_side(
        body, side, name="ssd_bwd", grid=(SSD_GROUPS, nc),
        in_specs=[xs_s, xs_s, xs_s, xs_s, b_s, c_s, lane_s, hs_s, vec_s, vec_s, wide_s, wide_s,
                  _const_spec((LANES, SSD_GW)), _const_spec((SSD_GW, LANES))],
        out_specs=[xs_s, xs_s, bc_out, bc_out, lane_s, wide_s, vec_s, vec_s, vec_s],
        out_shape=[jax.ShapeDtypeStruct((n, SSD_INNER), BF16), jax.ShapeDtypeStruct((n, SSD_INNER), F32),
                   jax.ShapeDtypeStruct((n, SSD_BC), F32), jax.ShapeDtypeStruct((n, SSD_BC), F32),
                   jax.ShapeDtypeStruct((n, DT_W), BF16), jax.ShapeDtypeStruct((1, SSD_INNER), F32),
                   jax.ShapeDtypeStruct((1, DT_W), F32), jax.ShapeDtypeStruct((1, DT_W), F32),
                   jax.ShapeDtypeStruct((1, DT_W), F32)],
        scratch_shapes=[pltpu.VMEM((SSD_GW, SSD_STATE), F32)],
        args=(dyn, y, z, pre, pre, pre, dt_raw, hsave, dtb, alog, dskip_w, norm_w, spread, spread.T))


def _bucket_table():
    def bucket(dist):
        d = np.maximum(dist, 0)
        half = REL_BUCKETS // 2
        big = half + (np.log(np.maximum(d, half).astype(np.float32) / np.float32(half))
                      / np.float32(math.log(REL_MAX_DIST / half)) * np.float32(REL_BUCKETS - half)).astype(np.int32)
        return np.where(d < half, d, np.minimum(big, REL_BUCKETS - 1)).astype(np.int32)

    l = np.arange(BLOCK)[None, :]
    band = bucket(l + BLOCK - np.arange(2 * BLOCK)[:, None])
    j = np.arange(BLOCK)[:, None]
    tables = [np.concatenate([bucket(v * BLOCK + l - j), band], axis=0) for v in range(3)]
    return np.concatenate([t.reshape(-1) for t in tables])


def _onehot_t():
    buckets = jnp.asarray(_bucket_table())
    return (buckets[None, :] == jnp.arange(REL_BUCKETS, dtype=jnp.int32)[:, None]).astype(F32)


def _bias_tables(rel_t, onehot_t):
    def body(r_ref, oh_ref, o_ref):
        o_ref[...] = jnp.dot(r_ref[...], oh_ref[...], precision=HIGHEST, preferred_element_type=F32)

    return pl.pallas_call(
        body, name="bias_tables", grid=(NT_ALL // NT_TILE,),
        in_specs=[pl.BlockSpec((ATT_HEADS, REL_BUCKETS), lambda i: (0, 0)),
                  pl.BlockSpec((REL_BUCKETS, NT_TILE), lambda i: (0, i))],
        out_specs=pl.BlockSpec((ATT_HEADS, NT_TILE), lambda i: (0, i)),
        out_shape=jax.ShapeDtypeStruct((ATT_HEADS, NT_ALL), F32),
        compiler_params=_cparams("parallel"))(rel_t, onehot_t)


def _bias_grad(dtab, onehot_t):
    def body(d_ref, oh_ref, o_ref):
        i = pl.program_id(0)
        p = lax.dot_general(d_ref[...], oh_ref[...], (((1,), (1,)), ((), ())), precision=HIGHEST,
                            preferred_element_type=F32)

        @pl.when(i == 0)
        def _():
            o_ref[...] = p

        @pl.when(i > 0)
        def _():
            o_ref[...] += p

    return pl.pallas_call(
        body, name="bias_grad", grid=(NT_ALL // NT_TILE,),
        in_specs=[pl.BlockSpec((ATT_HEADS, NT_TILE), lambda i: (0, i)),
                  pl.BlockSpec((REL_BUCKETS, NT_TILE), lambda i: (0, i))],
        out_specs=pl.BlockSpec((ATT_HEADS, REL_BUCKETS), lambda i: (0, 0)),
        out_shape=jax.ShapeDtypeStruct((ATT_HEADS, REL_BUCKETS), F32),
        compiler_params=_cparams("arbitrary"))(dtab, onehot_t)


def _att_mask_t(n, copies):
    far = 4 * BLOCK
    kk = lax.broadcasted_iota(jnp.int32, (N_KEYS, copies * BLOCK), 0)
    li = lax.broadcasted_iota(jnp.int32, (N_KEYS, copies * BLOCK), 1) & (BLOCK - 1)
    meta_ok = (kk >= PAD) & (kk < BLOCK) & (li + jnp.where(n >= 1, far, 0) >= kk)
    prev_ok = (kk >= BLOCK) & (kk < 2 * BLOCK) & (kk - BLOCK > li + jnp.where(n >= 2, 0, far))
    cur_ok = (kk >= 2 * BLOCK) & (kk - 2 * BLOCK <= li - jnp.where(n >= 1, 0, far))
    return meta_ok | prev_ok | cur_ok


def _att_kv(meta_ref, prev_ref, cur_ref):
    kv = jnp.concatenate([meta_ref[...], prev_ref[...], cur_ref[...]], axis=0)
    first = lax.broadcasted_iota(jnp.int32, (N_KEYS, LANES), 1) < ATT_HEADDIM
    out = []
    for pair in (kv[:, :LANES], kv[:, LANES:]):
        swapped = pltpu.roll(pair, ATT_HEADDIM, 1)
        out.append([jnp.where(first, pair, swapped).astype(BF16), jnp.where(first, swapped, pair).astype(BF16)])
    return out[0], out[1]


def _split_heads(x_pair, first):
    return jnp.concatenate([jnp.where(first, x_pair, 0.0), jnp.where(first, 0.0, x_pair)], axis=0).astype(BF16)


def _att_probs_t(qm2, k_dup, t_ref, j, mask2, sink_ref):
    scale = ATT_HEADDIM ** -0.5
    bias2 = jnp.concatenate([t_ref[0, 2 * j], t_ref[0, 2 * j + 1]], axis=1)
    second = lax.broadcasted_iota(jnp.int32, (1, 2 * BLOCK), 1) >= BLOCK
    sink2 = jnp.where(second, sink_ref[0:1, 2 * j + 1:2 * j + 2], sink_ref[0:1, 2 * j:2 * j + 1])
    s_t = jnp.where(mask2, _dot_nt(k_dup, qm2) * scale + bias2, NEG)
    mx = jnp.maximum(jnp.max(s_t, axis=0, keepdims=True), sink2)
    p_t = jnp.exp(s_t - mx)
    p_s = jnp.exp(sink2 - mx)
    inv = 1.0 / (jnp.sum(p_t, axis=0, keepdims=True) + p_s)
    return p_t * inv, p_s * inv


def _att_specs(nb, rev):
    def nidx(i):
        return nb - 1 - i if rev else i

    kvb = ATT_Q // (2 * ATT_KV)
    q_s = pl.BlockSpec((BLOCK, ATT_Q), lambda i: (nidx(i), 0))
    cur = pl.BlockSpec((BLOCK, 2 * ATT_KV), lambda i: (nidx(i), kvb))
    prev = pl.BlockSpec((BLOCK, 2 * ATT_KV), lambda i: (jnp.maximum(nidx(i) - 1, 0), kvb))
    meta = pl.BlockSpec((BLOCK, 2 * ATT_KV), lambda i: (0, kvb))
    table = pl.BlockSpec((1, ATT_HEADS, N_KEYS, BLOCK), lambda i: (jnp.minimum(nidx(i), 2), 0, 0, 0))
    sink = pl.BlockSpec((1, LANES), lambda i: (0, 0))
    return q_s, cur, prev, meta, table, sink


def _attn_fwd(qkv, tables, sinks):
    n = qkv.shape[0]
    nb = n // BLOCK
    q_s, cur_s, prev_s, meta_s, t_s, sink_s = _att_specs(nb, False)

    def body(q_ref, cur_ref, prev_ref, meta_ref, t_ref, sink_ref, o_ref):
        blk = pl.program_id(0)
        mask_t = _att_mask_t(blk, 1)
        k_dup, v_dup = _att_kv(meta_ref, prev_ref, cur_ref)
        v_dup_t = [v.T for v in v_dup]
        first = lax.broadcasted_iota(jnp.int32, (BLOCK, LANES), 1) < ATT_HEADDIM
        top = lax.broadcasted_iota(jnp.int32, (LANES, BLOCK), 0) < ATT_HEADDIM
        scale = ATT_HEADDIM ** -0.5
        for j in range(ATT_HEADS // 2):
            kh = 2 * j // ATT_GQ
            tile = slice(LANES * j, LANES * (j + 1))
            q_p = q_ref[:, tile]
            res = []
            for half, h in enumerate((2 * j, 2 * j + 1)):
                qm = jnp.where(first if half == 0 else jnp.logical_not(first), q_p, 0.0).astype(BF16)
                sink = sink_ref[0:1, h:h + 1]
                s_t = jnp.where(mask_t, _dot_nt(k_dup[kh], qm) * scale + t_ref[0, h], NEG)
                mx = jnp.maximum(jnp.max(s_t, axis=0, keepdims=True), sink)
                p_t = jnp.exp(s_t - mx)
                inv = 1.0 / (jnp.sum(p_t, axis=0, keepdims=True) + jnp.exp(sink - mx))
                res.append(_dot(v_dup_t[kh], (p_t * inv).astype(BF16)))
            o_ref[:, tile] = jnp.where(top, res[0], res[1]).T.astype(BF16)

    return pl.pallas_call(
        body, name="attn_fwd", grid=(nb,),
        in_specs=[q_s, cur_s, prev_s, meta_s, t_s, sink_s],
        out_specs=q_s,
        out_shape=jax.ShapeDtypeStruct((n, ATT_Q), BF16),
        compiler_params=_cparams("parallel"))(qkv, qkv, qkv, qkv, tables, sinks)


def _attn_bwd(datt, qkv, tables, sinks):
    n = qkv.shape[0]
    nb = n // BLOCK
    q_s, cur_s, prev_s, meta_s, t_s, sink_s = _att_specs(nb, True)
    dqkv_s = pl.BlockSpec((BLOCK, ATT_Q + 2 * ATT_KV), lambda i: (nb - 1 - i, 0))
    scale = ATT_HEADDIM ** -0.5

    def body(do_ref, q_ref, cur_ref, prev_ref, meta_ref, t_ref, sink_ref,
             dqkv_ref, dt_ref, dsink_ref, carry_scr, meta_scr):
        step = pl.program_id(0)
        blk = nb - 1 - step
        mask2 = _att_mask_t(blk, 2)
        k_dup, v_dup = _att_kv(meta_ref, prev_ref, cur_ref)
        k_dup_t = [k.T for k in k_dup]

        @pl.when(step == 0)
        def _():
            carry_scr[...] = jnp.zeros_like(carry_scr)
            meta_scr[...] = jnp.zeros_like(meta_scr)
            dsink_ref[...] = jnp.zeros_like(dsink_ref)

        @pl.when((step == 0) | (blk <= 1))
        def _():
            dt_ref[...] = jnp.zeros_like(dt_ref)

        first = lax.broadcasted_iota(jnp.int32, (BLOCK, LANES), 1) < ATT_HEADDIM
        top = lax.broadcasted_iota(jnp.int32, (LANES, BLOCK), 0) < ATT_HEADDIM
        first_k = lax.broadcasted_iota(jnp.int32, (N_KEYS, LANES), 1) < ATT_HEADDIM
        dsink = jnp.zeros((1, LANES), F32)
        dk_acc = [None] * ATT_KV_HEADS
        dv_acc = [None] * ATT_KV_HEADS
        for j in range(ATT_HEADS // 2):
            kh = 2 * j // ATT_GQ
            tile = slice(LANES * j, LANES * (j + 1))
            qm2 = _split_heads(q_ref[:, tile], first)
            dom2 = _split_heads(do_ref[:, tile], first)
            p_t, p_s = _att_probs_t(qm2, k_dup[kh], t_ref, j, mask2, sink_ref)
            dp_t = _dot_nt(v_dup[kh], dom2)
            delta = jnp.sum(p_t * dp_t, axis=0, keepdims=True)
            ds_t = p_t * (dp_t - delta)
            sink_terms = p_s * delta
            for half in range(2):
                cols = slice(BLOCK * half, BLOCK * (half + 1))
                dsink = _lane_put(dsink, -jnp.sum(sink_terms[:, cols], axis=1, keepdims=True), 2 * j + half)
                dt_ref[0, 2 * j + half] += ds_t[:, cols]
            ds_tb = ds_t.astype(BF16)
            dq_t = _dot(k_dup_t[kh], ds_tb)
            dqkv_ref[:, tile] = (jnp.where(top, dq_t[:, :BLOCK], dq_t[:, BLOCK:]).T * scale).astype(BF16)
            dk_part, dv_part = _dot(ds_tb, qm2), _dot(p_t.astype(BF16), dom2)
            dk_acc[kh] = dk_part if dk_acc[kh] is None else dk_acc[kh] + dk_part
            dv_acc[kh] = dv_part if dv_acc[kh] is None else dv_acc[kh] + dv_part
        dsink_ref[...] += dsink
        folded = [a + pltpu.roll(a, ATT_HEADDIM, 1) for a in dk_acc + dv_acc]
        dkv = jnp.concatenate([jnp.where(first_k, folded[0], folded[1]) * scale,
                               jnp.where(first_k, folded[2], folded[3])], axis=1)
        meta_scr[...] += dkv[:BLOCK, :]
        own = dkv[2 * BLOCK:, :] + carry_scr[...]
        carry_scr[...] = dkv[BLOCK:2 * BLOCK, :]

        @pl.when(blk > 0)
        def _():
            dqkv_ref[:, ATT_Q:] = own.astype(BF16)

        @pl.when(blk == 0)
        def _():
            dqkv_ref[:, ATT_Q:] = (own + meta_scr[...]).astype(BF16)

    return pl.pallas_call(
        body, name="attn_bwd", grid=(nb,),
        in_specs=[q_s, q_s, cur_s, prev_s, meta_s, t_s, sink_s],
        out_specs=[dqkv_s, t_s, sink_s],
        out_shape=[jax.ShapeDtypeStruct((n, ATT_Q + 2 * ATT_KV), BF16),
                   jax.ShapeDtypeStruct((3, ATT_HEADS, N_KEYS, BLOCK), F32),
                   jax.ShapeDtypeStruct((1, LANES), F32)],
        scratch_shapes=[pltpu.VMEM((BLOCK, 2 * ATT_KV), F32), pltpu.VMEM((BLOCK, 2 * ATT_KV), F32)],
        compiler_params=_cparams("arbitrary"))(datt, qkv, qkv, qkv, qkv, tables, sinks)


def _merge_out_fwd(gates, y_ssd, y_att, gate_b, w_out, h):
    n = gates.shape[0]
    tm = _row_tile(n, 416)

    def body(gs_ref, ga_ref, ys_ref, ya_ref, gb_ref, w_ref, h_ref, m_ref, o_ref):
        merged = (_sigmoid(gs_ref[...] + gb_ref[0:1, :]) * ys_ref[...]
                  + _sigmoid(ga_ref[...] + gb_ref[1:2, :]) * ya_ref[...]).astype(BF16)
        m_ref[...] = merged
        row = pl.program_id(0) * tm + lax.broadcasted_iota(jnp.int32, (tm, 1), 0)
        o_ref[...] = jnp.where(row >= PAD, _dot(merged, w_ref[...]), 0.0) + h_ref[...]

    row = pl.BlockSpec((tm, D_MODEL), lambda i: (i, 0))
    return pl.pallas_call(
        body, name="merge_out_fwd", grid=(n // tm,),
        in_specs=[row, pl.BlockSpec((tm, D_MODEL), lambda i: (i, 1)), row, row,
                  pl.BlockSpec((2, D_MODEL), lambda i: (0, 0)), pl.BlockSpec((D_MODEL, D_MODEL), lambda i: (0, 0)), row],
        out_specs=[row, row],
        out_shape=[jax.ShapeDtypeStruct((n, D_MODEL), BF16), jax.ShapeDtypeStruct((n, D_MODEL), F32)],
        compiler_params=_cparams("parallel"))(gates, gates, y_ssd, y_att, gate_b, w_out, h)


def _merge_out_bwd(dh, w_out, gates, y_ssd, y_att, gate_b):
    n = gates.shape[0]
    tm = _row_tile(n, 416)

    def body(dh_ref, w_ref, gs_ref, ga_ref, ys_ref, ya_ref, gb_ref, dys_ref, dya_ref, dg_ref, dgb_ref):
        i = pl.program_id(0)
        row = i * tm + lax.broadcasted_iota(jnp.int32, (tm, 1), 0)
        dmv = jnp.where(row >= PAD, _dot_nt(dh_ref[...].astype(BF16), w_ref[...]), 0.0)
        ss =_sigmoid(gs_ref[...] + gb_ref[0:1, :])
        sa = _sigmoid(ga_ref[...] + gb_ref[1:2, :])
        dys_ref[...] = (dmv * ss).astype(BF16)
        dya_ref[...] = (dmv * sa).astype(BF16)
        dgs = dmv * ys_ref[...] * ss * (1.0 - ss)
        dga = dmv * ya_ref[...] * sa * (1.0 - sa)
        dg_ref[:, :D_MODEL] = dgs.astype(BF16)
        dg_ref[:, D_MODEL:] = dga.astype(BF16)
        part = jnp.concatenate([jnp.sum(dgs, axis=0, keepdims=True), jnp.sum(dga, axis=0, keepdims=True)], axis=0)

        @pl.when(i == 0)
        def _():
            dgb_ref[...] = part

        @pl.when(i > 0)
        def _():
            dgb_ref[...] += part

    row = pl.BlockSpec((tm, D_MODEL), lambda i: (i, 0))
    gb = pl.BlockSpec((2, D_MODEL), lambda i: (0, 0))
    return pl.pallas_call(
        body, name="merge_out_bwd", grid=(n // tm,),
        in_specs=[row, pl.BlockSpec((D_MODEL, D_MODEL), lambda i: (0, 0)), row,
                  pl.BlockSpec((tm, D_MODEL), lambda i: (i, 1)), row, row, gb],
        out_specs=[row, row, pl.BlockSpec((tm, 2 * D_MODEL), lambda i: (i, 0)), gb],
        out_shape=[jax.ShapeDtypeStruct((n, D_MODEL), BF16), jax.ShapeDtypeStruct((n, D_MODEL), BF16),
                   jax.ShapeDtypeStruct((n, 2 * D_MODEL), BF16), jax.ShapeDtypeStruct((2, D_MODEL), F32)],
        compiler_params=_cparams("arbitrary"))(dh, w_out, gates, gates, y_ssd, y_att, gate_b)


def _col_move(srcs, outs, pieces, *, name):
    rows = srcs[0].shape[-2]
    tr = _row_tile(rows, 128)
    n_src = len(srcs)
    covered = [sum(p[6] for p in pieces if p[0] == o) for o in range(len(outs))]
    total = [int(np.prod(shp)) // rows for shp, _ in outs]

    def body(*refs):
        in_refs, out_refs = refs[:n_src], refs[n_src:]
        for o, ref in enumerate(out_refs):
            if covered[o] != total[o]:
                ref[...] = jnp.zeros_like(ref)
        for o, ol, oc, s, sl, sc, width in pieces:
            val = in_refs[s][:, sc:sc + width] if sl is None else in_refs[s][sl, :, sc:sc + width]
            val = val.astype(outs[o][1])
            if ol is None:
                out_refs[o][:, oc:oc + width] = val
            else:
                out_refs[o][ol, :, oc:oc + width] = val

    def spec(shape):
        if len(shape) == 2:
            return pl.BlockSpec((tr, shape[1]), lambda i: (i, 0))
        return pl.BlockSpec((shape[0], tr, shape[2]), lambda i: (0, i, 0))

    return pl.pallas_call(
        body, name=name, grid=(rows // tr,),
        in_specs=[spec(a.shape) for a in srcs], out_specs=[spec(shp) for shp, _ in outs],
        out_shape=[jax.ShapeDtypeStruct(shp, dt) for shp, dt in outs],
        compiler_params=_cparams("parallel"))(*srcs)


def _shard_pieces(seg_ranges, shard_w):
    out = []
    for seg, runs in enumerate(seg_ranges):
        for g0, width, s0 in runs:
            done = 0
            while done < width:
                dev, col = divmod(g0 + done, shard_w)
                take = min(width - done, shard_w - col)
                out.append((seg, s0 + done, dev, col, take))
                done += take
    return out


_CHIP_RELATIONS = [(1, 0, 0), (0, 1, 0), (1, 1, 0)]
N_CHIPS = 4


def _gather_two_level(arrays, *, name):
    outs = _run_plan(_gather_plan(arrays), name)
    return [o.reshape((N_DEV,) + a.shape) for o, a in zip(outs, arrays)]


class _CommPlan:
    def __init__(self, arrays, out_shape, scratch_shapes, phases):
        self.arrays, self.out_shape, self.scratch_shapes, self.phases = arrays, out_shape, scratch_shapes, phases


def _run_plan(plan, name):
    n_arr = len(plan.arrays)

    def body(*refs):
        ins, outs, sems = refs[:n_arr], refs[n_arr:2 * n_arr], refs[2 * n_arr:]
        for phase in plan.phases:
            phase(ins, outs, sems)

    any_spec = pl.BlockSpec(memory_space=pl.ANY)
    return pl.pallas_call(
        body, name=name, in_specs=[any_spec] * n_arr, out_specs=[any_spec] * n_arr, out_shape=plan.out_shape,
        scratch_shapes=plan.scratch_shapes)(*plan.arrays)


def _gather_plan(arrays):
    n_arr = len(arrays)
    n_chips = len(_CHIP_RELATIONS)
    n_pair = 1 + 2 * n_chips

    def where():
        x, y, c = lax.axis_index("x"), lax.axis_index("y"), lax.axis_index("c")
        return x, y, c, (x, y, 1 - c), [(x ^ dx, y ^ dy) for dx, dy, _ in _CHIP_RELATIONS]

    def copy(outs, sems, a, k, block, to, src=None):
        slot = outs[a].at[2 * block[0] + block[1], block[2]]
        return pltpu.make_async_remote_copy(
            src_ref=slot if src is None else src, dst_ref=slot, send_sem=sems[0].at[a * n_pair + k],
            recv_sem=sems[1].at[a * n_pair + k], device_id=to, device_id_type=MESH)

    def mine(ins, outs, sems, a, x, y, c):
        return pltpu.make_async_copy(ins[a], outs[a].at[2 * x + y, c], sems[2].at[a])

    def first_copies(ins, outs, sems, a, x, y, c, sibling, chips):
        return ([copy(outs, sems, a, 0, (x, y, c), sibling, src=ins[a])]
                + [copy(outs, sems, a, 1 + j, (x, y, c), (*chip, c), src=ins[a]) for j, chip in enumerate(chips)])

    def start(ins, outs, sems):
        x, y, c, sibling, chips = where()
        for a in range(n_arr):
            mine(ins, outs, sems, a, x, y, c).start()
            for cp in first_copies(ins, outs, sems, a, x, y, c, sibling, chips):
                cp.start()

    def pass_on(ins, outs, sems):
        x, y, c, sibling, chips = where()
        for j, chip in enumerate(chips):
            for a in range(n_arr):
                copy(outs, sems, a, 1 + j, (*chip, c), (x, y, c)).wait_recv()
                copy(outs, sems, a, 1 + n_chips + j, (*chip, c), sibling).start()

    def finish(ins, outs, sems):
        x, y, c, sibling, chips = where()
        for a in range(n_arr):
            copy(outs, sems, a, 0, (x, y, 1 - c), (x, y, c)).wait_recv()
            for j, chip in enumerate(chips):
                copy(outs, sems, a, 1 + n_chips + j, (*chip, 1 - c), (x, y, c)).wait_recv()
        for a in range(n_arr):
            for cp in first_copies(ins, outs, sems, a, x, y, c, sibling, chips):
                cp.wait_send()
            for j, chip in enumerate(chips):
                copy(outs, sems, a, 1 + n_chips + j, (*chip, c), sibling).wait_send()
            mine(ins, outs, sems, a, x, y, c).wait()

    return _CommPlan(
        arrays, [jax.ShapeDtypeStruct((N_CHIPS, 2) + a.shape, a.dtype) for a in arrays],
        [pltpu.SemaphoreType.DMA((n_arr * n_pair,)), pltpu.SemaphoreType.DMA((n_arr * n_pair,)),
         pltpu.SemaphoreType.DMA((n_arr,))],
        (start, pass_on, finish))


def _sibling_exchange(arrays, scatter, *, name):
    n_arr = len(arrays)

    def body(*refs):
        ins, outs = refs[:n_arr], refs[n_arr:2 * n_arr]
        send_sems, recv_sems = refs[2 * n_arr:]
        x, y, c = lax.axis_index("x"), lax.axis_index("y"), lax.axis_index("c")
        copies = []
        for a in range(n_arr):
            for q in range(N_CHIPS if scatter[a] else 1):
                src = ins[a].at[2 * q + 1 - c] if scatter[a] else ins[a]
                dst = outs[a].at[q] if scatter[a] else outs[a]
                cp = pltpu.make_async_remote_copy(
                    src_ref=src, dst_ref=dst, send_sem=send_sems.at[a * N_CHIPS + q],
                    recv_sem=recv_sems.at[a * N_CHIPS + q], device_id=(x, y, 1 - c), device_id_type=MESH)
                cp.start()
                copies.append(cp)
        for cp in copies:
            cp.wait_send()
        for cp in copies:
            cp.wait_recv()

    any_spec = pl.BlockSpec(memory_space=pl.ANY)
    return pl.pallas_call(
        body, name=name, in_specs=[any_spec] * n_arr, out_specs=[any_spec] * n_arr,
        out_shape=[jax.ShapeDtypeStruct(((N_CHIPS,) + a.shape[1:]) if s else a.shape, a.dtype)
                   for a, s in zip(arrays, scatter)],
        scratch_shapes=[pltpu.SemaphoreType.DMA((n_arr * N_CHIPS,)), pltpu.SemaphoreType.DMA((n_arr * N_CHIPS,))],
    )(*arrays)


def _add(a, b, *, name):
    rows, cols = a.shape
    tr = _row_tile(rows, 256)

    def body(a_ref, b_ref, o_ref):
        o_ref[...] = a_ref[...] + b_ref[...]

    blk = pl.BlockSpec((tr, cols), lambda i: (i, 0))
    return pl.pallas_call(body, name=name, grid=(rows // tr,), in_specs=[blk, blk], out_specs=blk,
                          out_shape=jax.ShapeDtypeStruct(a.shape, a.dtype), compiler_params=_cparams("parallel"))(a, b)


def _chip_exchange(arrays, scatter, *, name):
    return _run_plan(_chip_exchange_plan(arrays, scatter), name)


_ALL_RELATIONS = [(dx, dy, dc) for dx in (0, 1) for dy in (0, 1) for dc in (0, 1)][1:]


def _all_to_all_plan(arrays, scatter=None):
    n_arr = len(arrays)
    n_rel = len(_ALL_RELATIONS)
    scatter = scatter or [True] * n_arr

    def block(ins, a, p):
        return ins[a].at[p] if scatter[a] else ins[a]

    def local_copies(ins, outs, sems):
        me = 4 * lax.axis_index("x") + 2 * lax.axis_index("y") + lax.axis_index("c")
        return [pltpu.make_async_copy(block(ins, a, me), outs[a].at[me], sems[2].at[a]) for a in range(n_arr)]

    def remote_copies(ins, outs, sems, arrivals):
        x, y, c = lax.axis_index("x"), lax.axis_index("y"), lax.axis_index("c")
        me = 4 * x + 2 * y + c
        out = []
        for k, (dx, dy, dc) in enumerate(_ALL_RELATIONS):
            px, py, pc = x ^ dx, y ^ dy, c ^ dc
            peer = 4 * px + 2 * py + pc
            for a in range(n_arr):
                out.append(pltpu.make_async_remote_copy(
                    src_ref=block(ins, a, peer), dst_ref=outs[a].at[peer if arrivals else me],
                    send_sem=sems[0].at[a * n_rel + k], recv_sem=sems[1].at[a * n_rel + k],
                    device_id=(x, y, c) if arrivals else (px, py, pc), device_id_type=MESH))
        return out

    def start(ins, outs, sems):
        for cp in local_copies(ins, outs, sems) + remote_copies(ins, outs, sems, False):
            cp.start()

    def pass_on(ins, outs, sems):
        pass

    def finish(ins, outs, sems):
        for send in remote_copies(ins, outs, sems, False):
            send.wait_send()
        for arrival in remote_copies(ins, outs, sems, True):
            arrival.wait_recv()
        for cp in local_copies(ins, outs, sems):
            cp.wait()

    return _CommPlan(
        arrays, [jax.ShapeDtypeStruct(a.shape if s else (N_DEV,) + a.shape, a.dtype) for a, s in zip(arrays, scatter)],
        [pltpu.SemaphoreType.DMA((n_arr * n_rel,)), pltpu.SemaphoreType.DMA((n_arr * n_rel,)),
         pltpu.SemaphoreType.DMA((n_arr,))],
        (start, pass_on, finish))


def _chip_exchange_plan(arrays, scatter):
    n_arr = len(arrays)
    n_rel = len(_CHIP_RELATIONS)

    def local_copies(ins, outs, sems):
        me = 2 * lax.axis_index("x") + lax.axis_index("y")
        return [pltpu.make_async_copy(ins[a].at[me] if scatter[a] else ins[a], outs[a].at[me], sems[2].at[a])
                for a in range(n_arr)]

    def remote_copies(ins, outs, sems, arrivals):
        x, y, c = lax.axis_index("x"), lax.axis_index("y"), lax.axis_index("c")
        me = 2 * x + y
        out = []
        for k, (dx, dy, _) in enumerate(_CHIP_RELATIONS):
            px, py = x ^ dx, y ^ dy
            peer = 2 * px + py
            for a in range(n_arr):
                out.append(pltpu.make_async_remote_copy(
                    src_ref=ins[a].at[peer] if scatter[a] else ins[a], dst_ref=outs[a].at[peer if arrivals else me],
                    send_sem=sems[0].at[a * n_rel + k], recv_sem=sems[1].at[a * n_rel + k],
                    device_id=(x, y, c) if arrivals else (px, py, c), device_id_type=MESH))
        return out

    def start(ins, outs, sems):
        for cp in local_copies(ins, outs, sems) + remote_copies(ins, outs, sems, False):
            cp.start()

    def pass_on(ins, outs, sems):
        pass

    def finish(ins, outs, sems):
        for send in remote_copies(ins, outs, sems, False):
            send.wait_send()
        for arrival in remote_copies(ins, outs, sems, True):
            arrival.wait_recv()
        for cp in local_copies(ins, outs, sems):
            cp.wait()

    out_shape = [jax.ShapeDtypeStruct((N_CHIPS,) + (a.shape[1:] if s else a.shape), a.dtype)
                 for a, s in zip(arrays, scatter)]
    return _CommPlan(
        arrays, out_shape,
        [pltpu.SemaphoreType.DMA((n_arr * n_rel,)), pltpu.SemaphoreType.DMA((n_arr * n_rel,)),
         pltpu.SemaphoreType.DMA((n_arr,))],
        (start, pass_on, finish))


def _adamw(w, gslots, m, v, *, name):
    rows, cols = w.shape
    n_slots = gslots.shape[0]
    tr = _row_tile(rows, 128) if rows % 16 == 0 else rows

    def body(w_ref, g_ref, m_ref, v_ref, go_ref, d_ref, mo_ref, vo_ref):
        g = g_ref[0].astype(F32)
        for s in range(1, n_slots):
            g = g + g_ref[s].astype(F32)
        mn = ADAM_B1 * m_ref[...] + (1.0 - ADAM_B1) * g
        vn = ADAM_B2 * v_ref[...] + (1.0 - ADAM_B2) * (g * g)
        go_ref[...] = g
        mo_ref[...] = mn
        vo_ref[...] = vn
        m_hat = mn / (1.0 - ADAM_B1 ** ADAM_STEP)
        v_hat = vn / (1.0 - ADAM_B2 ** ADAM_STEP)
        d_ref[...] = -ADAM_LR * (m_hat / (jnp.sqrt(v_hat) + ADAM_EPS) + ADAM_WD * w_ref[...])

    blk = pl.BlockSpec((tr, cols), lambda i: (i, 0))
    shp = jax.ShapeDtypeStruct((rows, cols), F32)
    return pl.pallas_call(
        body, name=name, grid=(rows // tr,),
        in_specs=[blk, pl.BlockSpec((n_slots, tr, cols), lambda i: (0, i, 0)), blk, blk],
        out_specs=[blk] * 4, out_shape=[shp] * 4,
        compiler_params=_cparams("parallel"))(w, gslots, m, v)


_BIG = ("w_in", "w_ssd_branch", "w_attn_branch", "w_out", "w_ffn_in", "w_ffn_out")
_SMALL_SHARDED = ("meta_tokens", "ssd_conv_w", "gate_b", "ffn_conv_w")
_SMALL_REPLICATED = ("norm_mix_w", "ssd_conv_b", "ssd_dt_bias", "ssd_a_log", "ssd_d", "ssd_norm_w", "attn_sinks",
                     "rel_bias", "norm_ffn_w", "ffn_conv_b", "norm_final_w")
_WEIGHTS = ("meta_tokens", "norm_mix_w", "w_in", "ssd_conv_w", "ssd_conv_b", "ssd_dt_bias", "ssd_a_log", "ssd_d",
            "ssd_norm_w", "w_ssd_branch", "w_attn_branch", "attn_sinks", "rel_bias", "gate_b", "w_out", "norm_ffn_w",
            "w_ffn_in", "ffn_conv_w", "ffn_conv_b", "w_ffn_out", "norm_final_w")
_ROW_SHARDED = ("w_ssd_branch", "w_attn_branch", "w_out", "w_ffn_out")
_COL_SHARDED = ("w_in", "w_ffn_in", "meta_tokens", "ssd_conv_w", "gate_b", "ffn_conv_w")
_IN_SEGS = (("z", SSD_INNER), ("xbc", SSD_XBC), ("dt", SSD_HEADS), ("qkv", ATT_Q + 2 * ATT_KV), ("g", 2 * D_MODEL))


def _pack_rows(flat_parts, width, row_mult):
    flat = jnp.concatenate([p.reshape(-1) for p in flat_parts])
    pad = (-flat.shape[0]) % (width * row_mult)
    if pad:
        flat = jnp.concatenate([flat, jnp.zeros((pad,), flat.dtype)])
    return flat.reshape(-1, width)


def _unpack(flat, shapes):
    out, off = [], 0
    for shp in shapes:
        size = int(np.prod(shp))
        out.append(flat[off:off + size].reshape(shp))
        off += size
    return out


def _gather_full(stack, name, shard_shape):
    if name in _COL_SHARDED:
        return jnp.transpose(stack, (1, 0, 2)).reshape(shard_shape[0], N_DEV * shard_shape[1])
    return stack.reshape(N_DEV * shard_shape[0], shard_shape[1])


_IN_SEG_W = {"z": SSD_INNER, "xbc": SSD_XBC, "dt": DT_W, "qkv": ATT_Q + 2 * ATT_KV, "g": 2 * D_MODEL}
_IN_SHARD_W = (SSD_INNER + SSD_XBC + SSD_HEADS + ATT_Q + 2 * ATT_KV + 2 * D_MODEL) // N_DEV
_FFN_SHARD_W = 2 * D_FF // N_DEV


def _in_seg_runs():
    runs, off = [], 0
    for nm, width in _IN_SEGS:
        if nm == "dt":
            runs.append([(off + SSD_HPG * g, SSD_HPG, LANES * g) for g in range(SSD_GROUPS)])
        else:
            runs.append([(off, width, 0)])
        off += width
    return runs


def _w_in_to_segments(stack):
    pieces = [(seg, None, scol, 0, dev, col, w) for seg, scol, dev, col, w in _shard_pieces(_in_seg_runs(), _IN_SHARD_W)]
    outs = [((D_MODEL, _IN_SEG_W[nm]), stack.dtype) for nm, _ in _IN_SEGS]
    return dict(zip([nm for nm, _ in _IN_SEGS], _col_move([stack], outs, pieces, name="w_in_segments")))


def _segments_to_w_in_shards(seg_grads):
    pieces = [(0, dev, col, seg, None, scol, w) for seg, scol, dev, col, w in _shard_pieces(_in_seg_runs(), _IN_SHARD_W)]
    return _col_move(seg_grads, [((N_DEV, D_MODEL, _IN_SHARD_W), seg_grads[0].dtype)], pieces, name="g_w_in_shards")[0]


def _ffn_in_from_shards(stack):
    pieces = [(0, None, scol, 0, dev, col, w)
              for _, scol, dev, col, w in _shard_pieces([[(0, 2 * D_FF, 0)]], _FFN_SHARD_W)]
    return _col_move([stack], [((D_MODEL, 2 * D_FF), stack.dtype)], pieces, name="w_ffn_in_full")[0]


def _ffn_in_to_shards(g_up, g_gate):
    pieces = [(0, dev, col, seg, None, scol, w)
              for seg, scol, dev, col, w in _shard_pieces([[(0, D_FF, 0)], [(D_FF, D_FF, 0)]], _FFN_SHARD_W)]
    return _col_move([g_up, g_gate], [((N_DEV, D_MODEL, _FFN_SHARD_W), g_up.dtype)], pieces, name="g_w_ffn_in_shards")[0]


def _dt_spread(w_dt):
    k = w_dt.shape[0]
    w4 = w_dt.reshape(k, SSD_GROUPS, SSD_HPG)
    return jnp.pad(w4, ((0, 0), (0, 0), (0, LANES - SSD_HPG))).reshape(k, DT_W)


def _dt_gather(w_wide):
    k = w_wide.shape[0]
    return w_wide.reshape(k, SSD_GROUPS, LANES)[:, :, :SSD_HPG].reshape(k, SSD_HEADS)


class _LateExchanges:
    def __init__(self, two_d, shape2):
        self.two_d, self.shape2 = two_d, shape2
        self.early_grads_received = None
        self.w_in_grads_received = None

    def row_pack(self, tree):
        return jnp.concatenate([tree[k] for k in _ROW_SHARDED], axis=0)

    def late_weights_plan(self):
        return _gather_plan([self.two_d["w_ffn_in"].astype(BF16), self.row_pack(self.two_d).astype(BF16)])

    def late_weights(self, gathered):
        w_ffn_in_all, rows_all = [g.reshape((N_DEV,) + g.shape[2:]) for g in gathered]
        out = {"w_ffn_in": _ffn_in_from_shards(w_ffn_in_all)}
        off = 0
        for k in _ROW_SHARDED:
            r = self.shape2[k][0]
            out[k] = rows_all[:, off:off + r].reshape(N_DEV * r, D_MODEL)
            off += r
        return out

    def early_grads_plan(self, grads):
        rows_send = jnp.concatenate([grads[k].reshape(N_DEV, self.shape2[k][0], D_MODEL) for k in _ROW_SHARDED], axis=1)
        return _all_to_all_plan([_ffn_in_to_shards(*grads["w_ffn_in"]), rows_send])

    def w_in_grads_plan(self, seg_grads):
        return _all_to_all_plan([_segments_to_w_in_shards(seg_grads)])


def _local_step(x, target, w, exchanges=None):
    h0 = jnp.concatenate([jnp.zeros((PAD, D_MODEL), F32), w["meta_tokens"], x], axis=0)
    segs = w["in_segs"]

    dtb = _dt_spread(w["ssd_dt_bias"])
    alog = _dt_spread(w["ssd_a_log"])
    dskip_w = jnp.repeat(w["ssd_d"], SSD_HEADDIM, axis=1)
    sinks = jnp.pad(w["attn_sinks"], ((0, 0), (0, LANES - ATT_HEADS)))
    onehot_t = _onehot_t()
    tables = jnp.transpose(_bias_tables(w["rel_bias"].T, onehot_t).reshape(ATT_HEADS, 3, N_KEYS, BLOCK), (1, 0, 2, 3))

    u = _rms_fwd(h0, w["norm_mix_w"], name="rms_mix_fwd")
    z = _mm(u, segs["z"], name="in_z")
    xbc, pre = _mm_conv_fwd(u, segs["xbc"], w["ssd_conv_w"], w["ssd_conv_b"], name="in_xbc_conv_fwd")
    dt_raw = _mm(u, segs["dt"], name="in_dt")
    qkv = _mm(u, segs["qkv"], out_dtype=BF16, name="in_qkv")
    gates = _mm(u, segs["g"], name="in_g")
    (y, yn, hsave), gathered = _ssd_fwd(pre, dt_raw, z, dtb, alog, dskip_w, w["ssd_norm_w"],
                                        side=None if exchanges is None else exchanges.late_weights_plan())
    if exchanges is not None:
        w = {**w, **exchanges.late_weights(gathered)}
    w_ffn_up, w_ffn_gate = w["w_ffn_in"][:, :D_FF], w["w_ffn_in"][:, D_FF:]
    y_ssd = _mm(yn, w["w_ssd_branch"], out_dtype=BF16, name="ssd_out")
    att = _attn_fwd(qkv, tables, sinks)
    y_att = _mm(att, w["w_attn_branch"], out_dtype=BF16, name="att_out")
    merged, h1 = _merge_out_fwd(gates, y_ssd, y_att, w["gate_b"], w["w_out"], h0)
    u2 = _rms_fwd(h1, w["norm_ffn_w"], name="rms_ffn_fwd")
    x_up, x_gate, hid_up, hid_gate, act = _ffn_in_act_fwd(u2, w["w_ffn_in"], w["ffn_conv_w"], w["ffn_conv_b"])
    h2 = _mm(act, w["w_ffn_out"], c=h1, mask=True, name="ffn_out")
    dh2, dh2_b, loss_row, g_norm_final = _final_loss(h2, w["norm_final_w"], target)

    grads = {"norm_final_w": g_norm_final}
    grads["w_ffn_out"] = _mm(act, dh2_b, ta=True, mask=True, out_dtype=BF16, name="g_w_ffn_out")
    dx_up, dx_gate, dcw_up, dcw_gate, dcb_up, dcb_gate = _ffn_out_act_bwd(
        dh2_b, w["w_ffn_out"], hid_up, hid_gate, x_up, x_gate, w["ffn_conv_w"])
    grads["ffn_conv_w"] = jnp.concatenate([dcw_up, dcw_gate], axis=1)
    grads["ffn_conv_b"] = jnp.concatenate([dcb_up, dcb_gate], axis=1)
    (dh1, grads["norm_ffn_w"]), _ = _mm_rms_bwd([(dx_up, w_ffn_up), (dx_gate, w_ffn_gate)], h1, w["norm_ffn_w"], dh2,
                                                name="d_u2_rms_bwd")
    grads["w_ffn_in"] = (_mm(u2, dx_up, ta=True, out_dtype=BF16, name="g_w_ffn_up"),
                         _mm(u2, dx_gate, ta=True, out_dtype=BF16, name="g_w_ffn_gate"))

    grads["w_out"] = _mm(merged, dh1, ta=True, mask=True, out_dtype=BF16, name="g_w_out")
    dy_ssd, dy_att, dgates, grads["gate_b"] = _merge_out_bwd(dh1, w["w_out"], gates, y_ssd, y_att, w["gate_b"])
    dyn = _mm(dy_ssd, w["w_ssd_branch"], tb=True, name="d_yn")
    grads["w_ssd_branch"] = _mm(yn, dy_ssd, ta=True, out_dtype=BF16, name="g_w_ssd")
    datt = _mm(dy_att, w["w_attn_branch"], tb=True, out_dtype=BF16, name="d_att")
    grads["w_attn_branch"] = _mm(att, dy_att, ta=True, out_dtype=BF16, name="g_w_att")
    (dz, dpxs, dpb, dpc, ddt, grads["ssd_norm_w"], g_dtb, g_alog, g_dskip), received = _ssd_bwd(
        dyn, y, z, pre, dt_raw, hsave, dtb, alog, dskip_w, w["ssd_norm_w"],
        side=None if exchanges is None else exchanges.early_grads_plan(grads))
    if exchanges is not None:
        exchanges.early_grads_received = received
    grads["ssd_dt_bias"] = _dt_gather(g_dtb)
    grads["ssd_a_log"] = _dt_gather(g_alog)
    grads["ssd_d"] = _dt_gather(g_dskip)
    conv_g = _conv_bwd(dpxs, xbc, w["ssd_conv_w"], name="ssd_conv_bwd_x")
    conv_g = _conv_bwd(dpb, xbc, w["ssd_conv_w"], name="ssd_conv_bwd_b", col0=SSD_INNER, into=conv_g)
    dxbc, grads["ssd_conv_w"], grads["ssd_conv_b"] = _conv_bwd(
        dpc, xbc, w["ssd_conv_w"], name="ssd_conv_bwd_c", col0=SSD_INNER + SSD_BC, into=conv_g)
    dqkv, d_tables, d_sinks = _attn_bwd(datt, qkv, tables, sinks)
    grads["attn_sinks"] = d_sinks[:, :ATT_HEADS]
    dtab = jnp.transpose(d_tables, (1, 0, 2, 3)).reshape(ATT_HEADS, NT_ALL)
    grads["rel_bias"] = _bias_grad(dtab, onehot_t).T
    dsegs = {"z": dz, "xbc": dxbc, "dt": ddt, "qkv": dqkv, "g": dgates}
    grads["in_segs"] = [_mm(u, dsegs[nm], ta=True, out_dtype=BF16, name="g_w_in_" + nm) for nm, _ in _IN_SEGS]
    (dh0, grads["norm_mix_w"]), received = _mm_rms_bwd(
        [(dsegs[nm], segs[nm]) for nm, _ in _IN_SEGS], h0, w["norm_mix_w"], dh1, name="d_u_rms_bwd",
        side=None if exchanges is None else exchanges.w_in_grads_plan(grads["in_segs"]))
    if exchanges is not None:
        exchanges.w_in_grads_received = received[0]
    grads["meta_tokens"] = dh0[PAD:BLOCK]
    return loss_row[0, 0], dh0[BLOCK:], grads


def kernel(x, meta_tokens, norm_mix_w, w_in, ssd_conv_w, ssd_conv_b, ssd_dt_bias, ssd_a_log, ssd_d, ssd_norm_w, w_ssd_branch, w_attn_branch, attn_sinks, rel_bias, gate_b, w_out, norm_ffn_w, w_ffn_in, ffn_conv_w, ffn_conv_b, w_ffn_out, norm_final_w, loss_target, m_meta_tokens, m_norm_mix_w, m_w_in, m_ssd_conv_w, m_ssd_conv_b, m_ssd_dt_bias, m_ssd_a_log, m_ssd_d, m_ssd_norm_w, m_w_ssd_branch, m_w_attn_branch, m_attn_sinks, m_rel_bias, m_gate_b, m_w_out, m_norm_ffn_w, m_w_ffn_in, m_ffn_conv_w, m_ffn_conv_b, m_w_ffn_out, m_norm_final_w, v_meta_tokens, v_norm_mix_w, v_w_in, v_ssd_conv_w, v_ssd_conv_b, v_ssd_dt_bias, v_ssd_a_log, v_ssd_d, v_ssd_norm_w, v_w_ssd_branch, v_w_attn_branch, v_attn_sinks, v_rel_bias, v_gate_b, v_w_out, v_norm_ffn_w, v_w_ffn_in, v_ffn_conv_w, v_ffn_conv_b, v_w_ffn_out, v_norm_final_w):
    shard = dict(meta_tokens=meta_tokens, norm_mix_w=norm_mix_w, w_in=w_in, ssd_conv_w=ssd_conv_w,
                 ssd_conv_b=ssd_conv_b, ssd_dt_bias=ssd_dt_bias, ssd_a_log=ssd_a_log, ssd_d=ssd_d,
                 ssd_norm_w=ssd_norm_w, w_ssd_branch=w_ssd_branch, w_attn_branch=w_attn_branch,
                 attn_sinks=attn_sinks, rel_bias=rel_bias, gate_b=gate_b, w_out=w_out, norm_ffn_w=norm_ffn_w,
                 w_ffn_in=w_ffn_in, ffn_conv_w=ffn_conv_w, ffn_conv_b=ffn_conv_b, w_ffn_out=w_ffn_out,
                 norm_final_w=norm_final_w)
    mom_m = dict(zip(_WEIGHTS, (m_meta_tokens, m_norm_mix_w, m_w_in, m_ssd_conv_w, m_ssd_conv_b, m_ssd_dt_bias,
                                m_ssd_a_log, m_ssd_d, m_ssd_norm_w, m_w_ssd_branch, m_w_attn_branch, m_attn_sinks,
                                m_rel_bias, m_gate_b, m_w_out, m_norm_ffn_w, m_w_ffn_in, m_ffn_conv_w, m_ffn_conv_b,
                                m_w_ffn_out, m_norm_final_w)))
    mom_v = dict(zip(_WEIGHTS, (v_meta_tokens, v_norm_mix_w, v_w_in, v_ssd_conv_w, v_ssd_conv_b, v_ssd_dt_bias,
                                v_ssd_a_log, v_ssd_d, v_ssd_norm_w, v_w_ssd_branch, v_w_attn_branch, v_attn_sinks,
                                v_rel_bias, v_gate_b, v_w_out, v_norm_ffn_w, v_w_ffn_in, v_ffn_conv_w, v_ffn_conv_b,
                                v_w_ffn_out, v_norm_final_w)))
    orig_shape = {k: a.shape for k, a in shard.items()}
    two_d = {k: a.reshape(a.shape[-2:]) if a.ndim >= 2 else a.reshape(1, -1) for k, a in shard.items()}
    shape2 = {k: a.shape for k, a in two_d.items()}

    def as2d(tree):
        return {k: tree[k].reshape(shape2[k]) for k in _WEIGHTS}

    mom_m, mom_v = as2d(mom_m), as2d(mom_v)

    exchanges = _LateExchanges(two_d, shape2)
    row_pack = exchanges.row_pack
    small_pack = _pack_rows([two_d[k] for k in _SMALL_SHARDED], LANES, SMALL_ROW_MULT)
    w_in_all, small_all = _gather_two_level([two_d["w_in"].astype(BF16), small_pack], name="gather_weights")
    full = {k: two_d[k] for k in _SMALL_REPLICATED}
    full["in_segs"] = _w_in_to_segments(w_in_all)
    small_flat = small_all.reshape(N_DEV, -1)
    off = 0
    for k in _SMALL_SHARDED:
        size = int(np.prod(shape2[k]))
        full[k] = _gather_full(small_flat[:, off:off + size].reshape((N_DEV,) + shape2[k]), k, shape2[k])
        off += size

    loss_local, grad_x, grads = _local_step(x[0], loss_target[0], full, exchanges)

    small_names = _SMALL_SHARDED + _SMALL_REPLICATED
    small_send = _pack_rows([grads[k] for k in small_names] + [loss_local.reshape(1)], LANES, SMALL_ROW_MULT)
    small_recv, = _run_plan(_all_to_all_plan([small_send], [False]), "exchange_small_grads")
    in_recv = exchanges.w_in_grads_received
    ffn_recv, rows_recv = exchanges.early_grads_received

    big = {"w_in": _adamw(two_d["w_in"], in_recv, mom_m["w_in"], mom_v["w_in"], name="adamw_w_in"),
           "w_ffn_in": _adamw(two_d["w_ffn_in"], ffn_recv, mom_m["w_ffn_in"], mom_v["w_ffn_in"], name="adamw_w_ffn_in")}
    rows_out = _adamw(row_pack(two_d), rows_recv, row_pack(mom_m), row_pack(mom_v), name="adamw_rows")
    off = 0
    for k in _ROW_SHARDED:
        r = shape2[k][0]
        big[k] = [a[off:off + r] for a in rows_out]
        off += r
    me =4 * lax.axis_index("x") + 2 * lax.axis_index("y") + lax.axis_index("c")
    small_full_shapes = [grads[k].shape for k in small_names]
    n_small = sum(int(np.prod(s)) for s in small_full_shapes)

    def packed_small(tree):
        parts = []
        for k in small_names:
            a = tree[k]
            if k in _SMALL_SHARDED:
                fullw = jnp.zeros(grads[k].shape, F32)
                a = lax.dynamic_update_slice(fullw, a, (0, me * a.shape[1]))
            parts.append(a)
        return _pack_rows(parts + [jnp.zeros((1,), F32)], LANES, SMALL_ROW_MULT)

    g_small, d_small, m_small, v_small = _adamw(packed_small(two_d), small_recv, packed_small(mom_m),
                                                packed_small(mom_v), name="adamw_small")

    def unpack_all(which, small):
        out = {k: big[k][which] for k in _BIG}
        flat = small.reshape(-1)
        for k, a in zip(small_names, _unpack(flat, small_full_shapes)):
            if k in _SMALL_SHARDED:
                a = lax.dynamic_slice(a, (0, me * shape2[k][1]), shape2[k])
            out[k] = a
        return out, flat[n_small]

    g_all, loss = unpack_all(0, g_small)
    d_all, _ = unpack_all(1, d_small)
    m_all, _ = unpack_all(2, m_small)
    v_all, _ = unpack_all(3, v_small)

    def final(tree):
        return [tree[k].reshape(orig_shape[k]) for k in _WEIGHTS]

    return (loss, grad_x[None], *final(g_all), *final(d_all), *final(m_all), *final(v_all))
```

```python
import functools
import math

import numpy as np
import jax
import jax.numpy as jnp
from jax import lax
from jax.experimental import pallas as pl
from jax.experimental.pallas import tpu as pltpu

F32 = jnp.float32
BF16 = jnp.bfloat16
HIGHEST = lax.Precision.HIGHEST

D_MODEL = 1024
N_META = 16
BLOCK = 128
PAD = BLOCK - N_META
EPS = 1e-6
NEG = -1e30
SSD_INNER = 2 * D_MODEL
SSD_HEADDIM = 64
SSD_HEADS = SSD_INNER // SSD_HEADDIM
SSD_GROUPS = 4
SSD_HPG = SSD_HEADS // SSD_GROUPS
SSD_STATE = 128
SSD_CONV = 4
SSD_GW = SSD_HPG * SSD_HEADDIM
SSD_BC = SSD_GROUPS * SSD_STATE
SSD_XBC = SSD_INNER + 2 * SSD_BC
ATT_HEADS = 16
ATT_KV_HEADS = 2
ATT_HEADDIM = 64
ATT_GQ = ATT_HEADS // ATT_KV_HEADS
ATT_Q = ATT_HEADS * ATT_HEADDIM
ATT_KV = ATT_KV_HEADS * ATT_HEADDIM
REL_BUCKETS = 32
REL_MAX_DIST = 128
D_FF = 2816
FFN_CONV = 3
ADAM_LR = 0.001
ADAM_B1 = 0.9
ADAM_B2 = 0.999
ADAM_EPS = 1e-08
ADAM_WD = 0.01
ADAM_STEP = 10

N_DEV = 8
LANES = 128
SUBLANES = 8
DT_W = SSD_GROUPS * LANES
VMEM_LIMIT_BYTES = 56 * 1024 * 1024
MESH = pl.DeviceIdType.MESH

SMALL_ROW_MULT = 16

N_KEYS = 3 * BLOCK
NT_ALL = 3 * N_KEYS * BLOCK
NT_TILE = 8192


def _cparams(*sem):
    return pltpu.CompilerParams(dimension_semantics=sem, vmem_limit_bytes=VMEM_LIMIT_BYTES)


def _row_tile(n, cap):
    best = None
    for t in range(16, min(n, cap) + 1, 16):
        if n % t == 0:
            best = t
    return best or n


def _col_tile(n, cap):
    for t in (1408, 1280, 1024, 768, 640, 512, 384, 256, 128):
        if t <= cap and n % t == 0:
            return t
    return n


def _sigmoid(x):
    return 0.5 * jnp.tanh(0.5 * x) + 0.5


def _silu(x):
    return x * _sigmoid(x)


def _softplus(x):
    return jnp.maximum(x, 0.0) + jnp.log(1.0 + jnp.exp(-jnp.abs(x)))


def _dot_nt(a, b):
    return lax.dot_general(a, b, (((1,), (1,)), ((), ())), preferred_element_type=F32)


def _dot_tn(a, b):
    return lax.dot_general(a, b, (((0,), (0,)), ((), ())), preferred_element_type=F32)


def _dot(a, b):
    return jnp.dot(a, b, preferred_element_type=F32)


def _bf16_terms(x, terms):
    out, rest = [], x
    for _ in range(terms):
        part = rest.astype(BF16)
        out.append(part)
        rest = rest - part.astype(F32)
    return out


def _dot_sel(x, sel, terms=3):
    return sum(_dot(part, sel) for part in _bf16_terms(x, terms))


def _sel_dot(sel, x, terms=3):
    return sum(_dot(sel, part) for part in _bf16_terms(x, terms))


def _sum_all(x):
    return jnp.sum(jnp.sum(x, axis=1, keepdims=True), axis=0, keepdims=True)


MM_ROW_CAPS = (2080, 1664, 832, 416)
MM_COL_CAP = 1408
MM_VMEM_BUDGET = 44 * 1024 * 1024


def _mm_tiles(rows, cols, vmem_bytes):
    col_cands = [t for t in (2048, 1536, 1408, 1280, 1024, 768, 640, 512, 384, 256, 128) if cols % t == 0]
    if cols <= 2 * MM_COL_CAP:
        col_cands.append(cols)
    best = None
    for cap in MM_ROW_CAPS:
        tr = _row_tile(rows, cap)
        for tc in col_cands:
            if vmem_bytes(tr, tc) <= MM_VMEM_BUDGET and (best is None or tr * tc > best[0] * best[1]):
                best = (tr, tc)
    assert best is not None, (rows, cols)
    return best


def _mm(a, b, *, name, ta=False, tb=False, c=None, mask=False, out_dtype=F32):
    if not ta:
        m, k = a.shape
        n = b.shape[0] if tb else b.shape[1]
        tm, tn = _mm_tiles(m, n, lambda t_m, t_n: 2 * (t_m * k * a.dtype.itemsize + k * t_n * b.dtype.itemsize
                                                       + t_m * t_n * (jnp.dtype(out_dtype).itemsize
                                                                      + (0 if c is None else c.dtype.itemsize)))
                           + 4 * t_m * t_n)

        def body(*refs):
            if c is None:
                a_ref, b_ref, o_ref = refs
            else:
                a_ref, b_ref, c_ref, o_ref = refs
            acc = (_dot_nt if tb else _dot)(a_ref[...].astype(BF16), b_ref[...].astype(BF16))
            if mask:
                row = pl.program_id(0) * tm + lax.broadcasted_iota(jnp.int32, (tm, 1), 0)
                acc = jnp.where(row >= PAD, acc, 0.0)
            if c is not None:
                acc = acc + c_ref[...]
            o_ref[...] = acc.astype(out_dtype)

        b_spec = pl.BlockSpec((tn, k), lambda i, j: (j, 0)) if tb else pl.BlockSpec((k, tn), lambda i, j: (0, j))
        in_specs = [pl.BlockSpec((tm, k), lambda i, j: (i, 0)), b_spec]
        args = [a, b]
        if c is not None:
            in_specs.append(pl.BlockSpec((tm, tn), lambda i, j: (i, j)))
            args.append(c)
        return pl.pallas_call(
            body, name=name, grid=(m // tm, n // tn), in_specs=in_specs,
            out_specs=pl.BlockSpec((tm, tn), lambda i, j: (i, j)),
            out_shape=jax.ShapeDtypeStruct((m, n), out_dtype),
            compiler_params=_cparams("parallel", "parallel"))(*args)

    kc, m = a.shape
    n = b.shape[1]
    tm = _col_tile(m, MM_COL_CAP)
    tk, tn = _mm_tiles(kc, n, lambda t_k, t_n: 2 * (t_k * tm * a.dtype.itemsize + t_k * t_n * b.dtype.itemsize
                                                    + tm * t_n * jnp.dtype(out_dtype).itemsize) + 8 * tm * t_n)

    n_k = kc // tk

    def body_t(a_ref, b_ref, o_ref, acc_ref):
        kk = pl.program_id(2)
        bb = b_ref[...]
        if mask:
            row = kk * tk + lax.broadcasted_iota(jnp.int32, (tk, 1), 0)
            bb = jnp.where(row >= PAD, bb, jnp.zeros_like(bb))
        p = _dot_tn(a_ref[...].astype(BF16), bb.astype(BF16))

        @pl.when(kk == 0)
        def _():
            acc_ref[...] = p

        @pl.when(kk > 0)
        def _():
            acc_ref[...] += p

        @pl.when(kk == n_k - 1)
        def _():
            o_ref[...] = acc_ref[...].astype(out_dtype)

    return pl.pallas_call(
        body_t, name=name, grid=(m // tm, n // tn, n_k),
        in_specs=[pl.BlockSpec((tk, tm), lambda i, j, kk: (kk, i)), pl.BlockSpec((tk, tn), lambda i, j, kk: (kk, j))],
        out_specs=pl.BlockSpec((tm, tn), lambda i, j, kk: (i, j)),
        out_shape=jax.ShapeDtypeStruct((m, n), out_dtype),
        scratch_shapes=[pltpu.VMEM((tm, tn), F32)],
        compiler_params=_cparams("parallel", "parallel", "arbitrary"))(a, b)


def _mm_rms_bwd(pairs, x, w, dres, *, name, side=None):
    m, d = x.shape
    tm = _row_tile(m, 416)
    n_pairs = len(pairs)

    def body(*refs):
        a_refs, b_refs = refs[:n_pairs], refs[n_pairs:2 * n_pairs]
        x_ref, w_ref, dres_ref, dx_ref, dw_ref = refs[2 * n_pairs:]
        i = pl.program_id(0)
        dyv = None
        for a_ref, b_ref in zip(a_refs, b_refs):
            term = _dot_nt(a_ref[...].astype(BF16), b_ref[...])
            dyv = term if dyv is None else dyv + term
        xv = x_ref[...]
        r = lax.rsqrt(jnp.mean(xv * xv, axis=-1, keepdims=True) + EPS)
        xh = xv * r
        g = dyv * w_ref[...]
        dx_ref[...] = r * (g - xh * jnp.mean(g * xh, axis=-1, keepdims=True)) + dres_ref[...]
        part = jnp.sum(dyv * xh, axis=0, keepdims=True)

        @pl.when(i == 0)
        def _():
            dw_ref[...] = part

        @pl.when(i > 0)
        def _():
            dw_ref[...] += part

    row = pl.BlockSpec((tm, d), lambda i: (i, 0))
    vec = pl.BlockSpec((1, d), lambda i: (0, 0))
    in_specs = ([pl.BlockSpec((tm, a.shape[1]), lambda i: (i, 0)) for a, _ in pairs]
                + [pl.BlockSpec(b.shape, lambda i: (0, 0), pipeline_mode=pl.Buffered(1)) for _, b in pairs]
                + [row, vec, row])
    return _call_with_side(
        body, side, name=name, grid=(m // tm,), in_specs=in_specs, out_specs=[row, vec],
        out_shape=[jax.ShapeDtypeStruct((m, d), F32), jax.ShapeDtypeStruct((1, d), F32)], scratch_shapes=[],
        args=[a for a, _ in pairs] + [b for _, b in pairs] + [x, w, dres], semantics=("arbitrary",))


def _rms_fwd(h, w, *, name):
    n, d = h.shape
    tm = _row_tile(n, 832)

    def body(h_ref, w_ref, o_ref):
        x = h_ref[...]
        r = lax.rsqrt(jnp.mean(x * x, axis=-1, keepdims=True) + EPS)
        o_ref[...] = (x * r * w_ref[...]).astype(BF16)

    return pl.pallas_call(
        body, name=name, grid=(n // tm,),
        in_specs=[pl.BlockSpec((tm, d), lambda i: (i, 0)), pl.BlockSpec((1, d), lambda i: (0, 0))],
        out_specs=pl.BlockSpec((tm, d), lambda i: (i, 0)),
        out_shape=jax.ShapeDtypeStruct((n, d), BF16),
        compiler_params=_cparams("parallel"))(h, w)


def _final_loss(h, w, target):
    n, d = h.shape
    nb = n // BLOCK

    def body(h_ref, w_ref, t_ref, dh_ref, dhb_ref, loss_ref, dw_ref):
        i = pl.program_id(0)
        xv = h_ref[...]
        r = lax.rsqrt(jnp.mean(xv * xv, axis=-1, keepdims=True) + EPS)
        xh = xv * r
        wv = w_ref[...]
        err = jnp.where(i >= 1, xh * wv - t_ref[...], 0.0)
        dyv = err * (1.0 / d)
        g = dyv * wv
        dh = r * (g - xh * jnp.mean(g * xh, axis=-1, keepdims=True))
        dh_ref[...] = dh
        dhb_ref[...] = dh.astype(BF16)
        lpart = jnp.broadcast_to(0.5 * _sum_all(err * err) * (1.0 / d), (1, LANES))
        wpart = jnp.sum(dyv * xh, axis=0, keepdims=True)

        @pl.when(i == 0)
        def _():
            loss_ref[...] = lpart
            dw_ref[...] = wpart

        @pl.when(i > 0)
        def _():
            loss_ref[...] += lpart
            dw_ref[...] += wpart

    row = pl.BlockSpec((BLOCK, d), lambda i: (i, 0))
    vec = pl.BlockSpec((1, d), lambda i: (0, 0))
    return pl.pallas_call(
        body, name="final_loss", grid=(nb,),
        in_specs=[row, vec, pl.BlockSpec((BLOCK, d), lambda i: (jnp.maximum(i - 1, 0), 0))],
        out_specs=[row, row, pl.BlockSpec((1, LANES), lambda i: (0, 0)), vec],
        out_shape=[jax.ShapeDtypeStruct((n, d), F32), jax.ShapeDtypeStruct((n, d), BF16),
                   jax.ShapeDtypeStruct((1, LANES), F32), jax.ShapeDtypeStruct((1, d), F32)],
        compiler_params=_cparams("arbitrary"))(h, w, target)


def _main_spec(tm, cb, off=0):
    return pl.BlockSpec((tm, cb), lambda j, i: (i, j + off))


def _prev_spec(tm, cb, off=0):
    r8 = tm // SUBLANES
    return pl.BlockSpec((SUBLANES, cb), lambda j, i: (jnp.maximum(i * r8 - 1, 0), j + off))


def _next_spec(tm, cb, n_rows, off=0):
    r8 = tm // SUBLANES
    last = n_rows // SUBLANES - 1
    return pl.BlockSpec((SUBLANES, cb), lambda j, i: (jnp.minimum((i + 1) * r8, last), j + off))


def _with_prev(prev_ref, main_ref, i):
    prev = jnp.where(i > 0, prev_ref[...], 0.0)
    return jnp.concatenate([prev, main_ref[...]], axis=0)


def _with_next(main, nxt, i, n_tiles):
    return jnp.concatenate([main, jnp.where(i < n_tiles - 1, nxt, 0.0)], axis=0)


def _back(xx, s, tm):
    if s == 0:
        return xx[SUBLANES:SUBLANES + tm]
    return pltpu.roll(xx, s, 0)[SUBLANES:SUBLANES + tm]


def _ahead(xx, s, tm):
    if s == 0:
        return xx[:tm]
    return pltpu.roll(xx, tm + SUBLANES - s, 0)[:tm]


def _mm_conv_fwd(u, w_in, w, b, *, name):
    n = u.shape[0]
    cdim = w_in.shape[1]
    kw = w.shape[0]
    tm = _row_tile(n, 832)
    cb = _col_tile(cdim, 512)
    nt = n // tm

    def body(u_ref, w_in_ref, w_ref, b_ref, x_ref, o_ref, acc_scr, halo_scr):
        j, i = pl.program_id(0), pl.program_id(1)

        @pl.when((j == 0) & (i == 0))
        def _():
            acc_scr[...] = jnp.zeros_like(acc_scr)
            halo_scr[...] = jnp.zeros_like(halo_scr)

        new = _dot(u_ref[...], w_in_ref[...])
        prev = acc_scr[...]
        xx = jnp.concatenate([jnp.where(i >= 2, halo_scr[...], 0.0), prev], axis=0)
        acc = jnp.broadcast_to(b_ref[...], (tm, cb))
        for k in range(kw):
            acc = acc + w_ref[k:k + 1, :] * _back(xx, kw - 1 - k, tm)
        x_ref[...] = prev.astype(BF16)
        o_ref[...] = acc
        halo_scr[...] = prev[tm - SUBLANES:, :]
        acc_scr[...] = new

    out = pl.BlockSpec((tm, cb), lambda j, i: (jnp.maximum(i - 1, 0), j))
    shp = jax.ShapeDtypeStruct((n, cdim), F32)
    return pl.pallas_call(
        body, name=name, grid=(cdim // cb, nt + 1),
        in_specs=[pl.BlockSpec((tm, u.shape[1]), lambda j, i: (jnp.minimum(i, nt - 1), 0)),
                  pl.BlockSpec((w_in.shape[0], cb), lambda j, i: (0, j)),
                  pl.BlockSpec((kw, cb), lambda j, i: (0, j)), pl.BlockSpec((1, cb), lambda j, i: (0, j))],
        out_specs=[out, out], out_shape=[jax.ShapeDtypeStruct((n, cdim), BF16), shp],
        scratch_shapes=[pltpu.VMEM((tm, cb), F32), pltpu.VMEM((SUBLANES, cb), F32)],
        compiler_params=_cparams("arbitrary", "arbitrary"))(u, w_in, w, b)


def _conv_bwd_core(dpre_ext, x, w_ref, kw, tm):
    dx = None
    dws = []
    for k in range(kw):
        shifted = _ahead(dpre_ext, kw - 1 - k, tm)
        term = w_ref[k:k + 1, :] * shifted
        dx = term if dx is None else dx + term
        dws.append(jnp.sum(shifted * x, axis=0, keepdims=True))
    return dx, dws, jnp.sum(dpre_ext[:tm], axis=0, keepdims=True)


def _acc_rows(i, dw_ref, db_ref, dws, db):
    @pl.when(i == 0)
    def _():
        for k, v in enumerate(dws):
            dw_ref[k:k + 1, :] = v
        db_ref[...] = db

    @pl.when(i > 0)
    def _():
        for k, v in enumerate(dws):
            dw_ref[k:k + 1, :] += v
        db_ref[...] += db


def _conv_bwd(dpre, x, w, *, name, col0=0, into=None):
    n, cdim = x.shape
    kw = w.shape[0]
    tm = _row_tile(n, 832)
    cb = _col_tile(cdim, 512)
    nt = n // tm
    off = col0 // cb
    n_alias = 0 if into is None else 3

    def body(d_ref, dn_ref, x_ref, w_ref, *rest):
        dx_ref, dw_ref, db_ref = rest[n_alias:]
        i = pl.program_id(1)
        dpre_ext = _with_next(d_ref[...], dn_ref[...], i, nt)
        dx, dws, db = _conv_bwd_core(dpre_ext, x_ref[...], w_ref, kw, tm)
        dx_ref[...] = dx.astype(BF16)
        _acc_rows(i, dw_ref, db_ref, dws, db)

    wspec = pl.BlockSpec((kw, cb), lambda j, i: (0, j + off))
    bspec = pl.BlockSpec((1, cb), lambda j, i: (0, j + off))
    return pl.pallas_call(
        body, name=name, grid=(dpre.shape[1] // cb, nt),
        in_specs=[_main_spec(tm, cb), _next_spec(tm, cb, n), _main_spec(tm, cb, off), wspec]
        + [pl.BlockSpec(memory_space=pl.ANY)] * n_alias,
        out_specs=[_main_spec(tm, cb, off), wspec, bspec],
        out_shape=[jax.ShapeDtypeStruct((n, cdim), BF16), jax.ShapeDtypeStruct((kw, cdim), F32),
                   jax.ShapeDtypeStruct((1, cdim), F32)],
        input_output_aliases={4 + k: k for k in range(n_alias)},
        compiler_params=_cparams("parallel", "arbitrary"))(dpre, dpre, x, w, *(into or ()))


def _ffn_in_act_fwd(u, w_in, w, b):
    n = u.shape[0]
    kw = w.shape[0]
    tm = _row_tile(n, 832)
    cb = _col_tile(D_FF, 256)
    nc = D_FF // cb
    nt = n // tm

    def body(u_ref, wu_in_ref, wg_in_ref, wu_ref, wg_ref, bu_ref, bg_ref,
             xu_ref, xg_ref, hu_ref, hg_ref, act_ref, acc_scr, halo_scr):
        j, i = pl.program_id(0), pl.program_id(1)

        @pl.when((j == 0) & (i == 0))
        def _():
            acc_scr[...] = jnp.zeros_like(acc_scr)
            halo_scr[...] = jnp.zeros_like(halo_scr)

        ub = u_ref[...]
        new = [_dot(ub, wu_in_ref[...]), _dot(ub, wg_in_ref[...])]
        hid = []
        for half, (x_ref, w_ref, b_ref) in enumerate(((xu_ref, wu_ref, bu_ref), (xg_ref, wg_ref, bg_ref))):
            prev = acc_scr[half]
            xx = jnp.concatenate([jnp.where(i >= 2, halo_scr[half], 0.0), prev], axis=0)
            acc = jnp.broadcast_to(b_ref[...], (tm, cb))
            for k in range(kw):
                acc = acc + w_ref[k:k + 1, :] * _back(xx, kw - 1 - k, tm)
            x_ref[...] = prev.astype(BF16)
            hid.append(acc)
            halo_scr[half] = prev[tm - SUBLANES:, :]
            acc_scr[half] = new[half]
        hu_ref[...] = hid[0]
        hg_ref[...] = hid[1]
        act_ref[...] = (_silu(hid[1]) * hid[0]).astype(BF16)

    def wspec(off):
        return pl.BlockSpec((kw, cb), lambda j, i: (0, j + off))

    def bspec(off):
        return pl.BlockSpec((1, cb), lambda j, i: (0, j + off))

    def in_w(off):
        return pl.BlockSpec((w_in.shape[0], cb), lambda j, i: (0, j + off))

    out = pl.BlockSpec((tm, cb), lambda j, i: (jnp.maximum(i - 1, 0), j))
    f32_out = jax.ShapeDtypeStruct((n, D_FF), F32)
    bf16_out = jax.ShapeDtypeStruct((n, D_FF), BF16)
    return pl.pallas_call(
        body, name="ffn_in_act_fwd", grid=(nc, nt + 1),
        in_specs=[pl.BlockSpec((tm, u.shape[1]), lambda j, i: (jnp.minimum(i, nt - 1), 0)), in_w(0), in_w(nc),
                  wspec(0), wspec(nc), bspec(0), bspec(nc)],
        out_specs=[out] * 5,
        out_shape=[bf16_out, bf16_out, f32_out, f32_out, bf16_out],
        scratch_shapes=[pltpu.VMEM((2, tm, cb), F32), pltpu.VMEM((2, SUBLANES, cb), F32)],
        compiler_params=_cparams("arbitrary", "arbitrary"))(u, w_in, w_in, w, w, b, b)


def _ffn_out_act_bwd(dh, w_out, hu, hg, x_up, x_gate, w):
    n = x_up.shape[0]
    kw = w.shape[0]
    tm = _row_tile(n, 832)
    cb = _col_tile(D_FF, 256)
    nc = D_FF // cb
    nt = n // tm

    def body(dh_ref, wo_ref, hu_ref, hun_ref, hg_ref, hgn_ref, xu_ref, xg_ref, wu_ref, wg_ref,
             dxu_ref, dxg_ref, dwu_ref, dwg_ref, dbu_ref, dbg_ref, acc_scr, halo_scr):
        j, i = pl.program_id(0), pl.program_id(1)

        @pl.when((j == 0) & (i == 0))
        def _():
            acc_scr[...] = jnp.zeros_like(acc_scr)
            halo_scr[...] = jnp.zeros_like(halo_scr)

        tile = jnp.maximum(nt - 1 - i, 0)
        row = tile * tm + lax.broadcasted_iota(jnp.int32, (tm, 1), 0)
        new = jnp.where(row >= PAD, _dot_nt(dh_ref[...].astype(BF16), wo_ref[...]), 0.0)
        prev = jnp.where(i >= 1, acc_scr[...], 0.0)
        dact_e = jnp.concatenate([prev, jnp.where(i >= 2, halo_scr[...], 0.0)], axis=0)
        last = nt - i >= nt - 1
        up_e = jnp.concatenate([hu_ref[...], jnp.where(last, 0.0, hun_ref[...])], axis=0)
        gate_e = jnp.concatenate([hg_ref[...], jnp.where(last, 0.0, hgn_ref[...])], axis=0)
        halo_scr[...] = prev[:SUBLANES, :]
        acc_scr[...] = new
        sg = _sigmoid(gate_e)
        dup_e = dact_e * (gate_e * sg)
        dgate_e = dact_e * up_e * (sg * (1.0 + gate_e * (1.0 - sg)))
        dx, dws, db = _conv_bwd_core(dup_e, xu_ref[...], wu_ref, kw, tm)
        dxu_ref[...] = dx.astype(BF16)
        _acc_rows(i, dwu_ref, dbu_ref, dws, db)
        dx, dws, db = _conv_bwd_core(dgate_e, xg_ref[...], wg_ref, kw, tm)
        dxg_ref[...] = dx.astype(BF16)
        _acc_rows(i, dwg_ref, dbg_ref, dws, db)

    def done_tile(i):
        return jnp.minimum(nt - i, nt - 1)

    r8 = tm // SUBLANES
    main = pl.BlockSpec((tm, cb), lambda j, i: (done_tile(i), j))
    nxt = pl.BlockSpec((SUBLANES, cb), lambda j, i: (jnp.minimum((done_tile(i) + 1) * r8, n // SUBLANES - 1), j))
    wspec0 = pl.BlockSpec((kw, cb), lambda j, i: (0, j))
    wspec1 = pl.BlockSpec((kw, cb), lambda j, i: (0, j + nc))
    bspec = pl.BlockSpec((1, cb), lambda j, i: (0, j))
    return pl.pallas_call(
        body, name="ffn_out_act_bwd", grid=(nc, nt + 1),
        in_specs=[pl.BlockSpec((tm, dh.shape[1]), lambda j, i: (jnp.maximum(nt - 1 - i, 0), 0)),
                  pl.BlockSpec((cb, w_out.shape[1]), lambda j, i: (j, 0)),
                  main, nxt, main, nxt, main, main, wspec0, wspec1],
        out_specs=[main, main, wspec0, wspec0, bspec, bspec],
        out_shape=[jax.ShapeDtypeStruct((n, D_FF), BF16), jax.ShapeDtypeStruct((n, D_FF), BF16),
                   jax.ShapeDtypeStruct((kw, D_FF), F32), jax.ShapeDtypeStruct((kw, D_FF), F32),
                   jax.ShapeDtypeStruct((1, D_FF), F32), jax.ShapeDtypeStruct((1, D_FF), F32)],
        scratch_shapes=[pltpu.VMEM((tm, cb), F32), pltpu.VMEM((SUBLANES, cb), F32)],
        compiler_params=_cparams("arbitrary", "arbitrary"))(dh, w_out, hu, hu, hg, hg, x_up, x_gate, w, w)


def _ssd_prep(pxs_ref, pb_ref, pc_ref, dtr_ref, dtb_ref, alog_ref, c):
    xs = _silu(pxs_ref[...])
    bm = _silu(pb_ref[...])
    cm = _silu(pc_ref[...])
    return (xs, bm, cm) + _ssd_decay(dtr_ref, dtb_ref, alog_ref, c)


def _ssd_decay(dtr_ref, dtb_ref, alog_ref, c):
    row =lax.broadcasted_iota(jnp.int32, (BLOCK, 1), 0) + c * BLOCK
    valid = (row >= PAD).astype(F32)
    dtr = dtr_ref[...] + dtb_ref[...]
    dt = _softplus(dtr) * valid
    a = -jnp.exp(alog_ref[...])
    lam = dt * a
    ri = lax.broadcasted_iota(jnp.int32, (BLOCK, BLOCK), 0)
    ci = lax.broadcasted_iota(jnp.int32, (BLOCK, BLOCK), 1)
    causal = ci <= ri
    cs = _sel_dot(causal.astype(BF16), lam)
    return valid, dtr, dt, a, lam, cs, causal


def _head_cols(r):
    return slice(SSD_HEADDIM * r, SSD_HEADDIM * (r + 1))


def _ssd_specs(nc, rev):
    def cidx(c):
        return nc - 1 - c if rev else c

    xs = pl.BlockSpec((BLOCK, SSD_GW), lambda g, c: (cidx(c), g))
    bspec = pl.BlockSpec((BLOCK, SSD_STATE), lambda g, c: (cidx(c), SSD_INNER // SSD_STATE + g))
    cspec = pl.BlockSpec((BLOCK, SSD_STATE), lambda g, c: (cidx(c), (SSD_INNER + SSD_BC) // SSD_STATE + g))
    lane = pl.BlockSpec((BLOCK, LANES), lambda g, c: (cidx(c), g))
    vec = pl.BlockSpec((1, LANES), lambda g, c: (0, g))
    wide_vec = pl.BlockSpec((1, SSD_GW), lambda g, c: (0, g))
    hsave = pl.BlockSpec((1, 1, SSD_GW, SSD_STATE), lambda g, c: (cidx(c), g, 0, 0))
    return xs, bspec, cspec, lane, vec, wide_vec, hsave


def _head_spread_matrix():
    r = lax.broadcasted_iota(jnp.int32, (LANES, SSD_GW), 0)
    col = lax.broadcasted_iota(jnp.int32, (LANES, SSD_GW), 1)
    return (col // SSD_HEADDIM == r).astype(BF16)


def _const_spec(shape):
    return pl.BlockSpec(shape, lambda g, c: (0,) * len(shape))


def _spread_heads(per_head, e_ref):
    wide = _dot_sel(jnp.concatenate(per_head, axis=0), e_ref[...])
    return [wide[BLOCK * k:BLOCK * (k + 1)] for k in range(len(per_head))]


def _call_with_side(body, side, *, name, grid, in_specs, out_specs, out_shape, scratch_shapes, args,
                    semantics=("parallel", "arbitrary")):
    if side is None:
        outs = pl.pallas_call(body, name=name, grid=grid, in_specs=in_specs, out_specs=out_specs, out_shape=out_shape,
                              scratch_shapes=scratch_shapes, compiler_params=_cparams(*semantics))(*args)
        return outs, []
    n_in, n_out, n_scr, n_side = len(in_specs), len(out_specs), len(scratch_shapes), len(side.arrays)

    def body_with_side(*refs):
        ins, rest = refs[:n_in + n_side], refs[n_in + n_side:]
        outs, scratch = rest[:n_out + n_side], rest[n_out + n_side:]
        side_refs = (ins[n_in:], outs[n_out:], scratch[n_scr:])
        ids = [pl.program_id(k) for k in range(len(grid))]
        inner_first = functools.reduce(jnp.logical_and, [i == 0 for i in ids[1:]], True)

        @pl.when((ids[0] == 0) & inner_first)
        def _():
            side.phases[0](*side_refs)

        body(*ins[:n_in], *outs[:n_out], *scratch[:n_scr])

        @pl.when((ids[0] == grid[0] // 2) & inner_first)
        def _():
            side.phases[1](*side_refs)

        @pl.when(functools.reduce(jnp.logical_and, [i == n - 1 for i, n in zip(ids, grid)]))
        def _():
            side.phases[2](*side_refs)

    any_spec = pl.BlockSpec(memory_space=pl.ANY)
    outs = pl.pallas_call(
        body_with_side, name=name, grid=grid, in_specs=list(in_specs) + [any_spec] * n_side,
        out_specs=list(out_specs) + [any_spec] * n_side, out_shape=list(out_shape) + list(side.out_shape),
        scratch_shapes=list(scratch_shapes) + list(side.scratch_shapes),
        compiler_params=_cparams(*["arbitrary"] * len(grid)))(*args, *side.arrays)
    return outs[:n_out], outs[n_out:]


def _ssd_fwd(pre, dt_raw, z, dtb, alog, dskip_w, norm_w, side=None):
    n = pre.shape[0]
    nc = n // BLOCK
    xs_s, b_s, c_s, lane_s, vec_s, wide_s, hs_s = _ssd_specs(nc, False)

    def body(pxs_ref, pb_ref, pc_ref, dtr_ref, z_ref, dtb_ref, alog_ref, dskw_ref, nw_ref, e_ref,
             y_ref, yn_ref, hs_ref, h_scr):
        c = pl.program_id(1)

        @pl.when(c == 0)
        def _():
            h_scr[...] = jnp.zeros_like(h_scr)

        xs, bm, cm, _, _, dt, _, _, cs, causal = _ssd_prep(pxs_ref, pb_ref, pc_ref, dtr_ref, dtb_ref, alog_ref, c)
        cst = cs.T
        cs_last = cs[BLOCK - 1:BLOCK, :]
        dt_w, ecs_w, dec_w = _spread_heads([dt, jnp.exp(cs), jnp.exp(cs_last - cs)], e_ref)
        xdt = xs * dt_w
        bmb = bm.astype(BF16)
        cmb = cm.astype(BF16)
        cb = _dot_nt(cmb, bmb)
        hg = h_scr[...]
        hs_ref[0, 0] = hg
        y = _dot_nt(cmb, hg.astype(BF16)) * ecs_w + dskw_ref[...] * xs
        first = lax.broadcasted_iota(jnp.int32, (BLOCK, LANES), 1) < SSD_HEADDIM
        diag = []
        for j in range(SSD_HPG // 2):
            xp = xdt[:, LANES * j:LANES * (j + 1)].astype(BF16)
            res = []
            for r in (2 * j, 2 * j + 1):
                lm = jnp.exp(jnp.where(causal, cs[:, r:r + 1] - cst[r:r + 1, :], NEG))
                res.append(_dot((cb * lm).astype(BF16), xp))
            diag.append(jnp.where(first, res[0], res[1]))
        y = y + jnp.concatenate(diag, axis=1)
        st = _dot_tn((xdt * dec_w).astype(BF16), bmb)
        eh = jnp.exp(cs_last)
        for r in range(SSD_HPG):
            rows = _head_cols(r)
            h_scr[rows, :] = hg[rows, :] * eh[:, r:r + 1] + st[rows, :]
        y_ref[...] = y
        gts = y * _silu(z_ref[...])
        rr = lax.rsqrt(jnp.mean(gts * gts, axis=-1, keepdims=True) + EPS)
        yn_ref[...] = (gts * rr * nw_ref[...]).astype(BF16)

    return _call_with_side(
        body, side, name="ssd_fwd", grid=(SSD_GROUPS, nc),
        in_specs=[xs_s, b_s, c_s, lane_s, xs_s, vec_s, vec_s, wide_s, wide_s, _const_spec((LANES, SSD_GW))],
        out_specs=[xs_s, xs_s, hs_s],
        out_shape=[jax.ShapeDtypeStruct((n, SSD_INNER), F32), jax.ShapeDtypeStruct((n, SSD_INNER), BF16),
                   jax.ShapeDtypeStruct((nc, SSD_GROUPS, SSD_GW, SSD_STATE), F32)],
        scratch_shapes=[pltpu.VMEM((SSD_GW, SSD_STATE), F32)],
        args=(pre, pre, pre, dt_raw, z, dtb, alog, dskip_w, norm_w, _head_spread_matrix()))


def _lane_put(acc, col, r):
    lane = lax.broadcasted_iota(jnp.int32, acc.shape, 1)
    return jnp.where(lane == r, col, acc)


def _ssd_bwd(dyn, y, z, pre, dt_raw, hsave, dtb, alog, dskip_w, norm_w, side=None):
    n = pre.shape[0]
    nc = n // BLOCK
    spread = _head_spread_matrix()
    xs_s, b_s, c_s, lane_s, vec_s, wide_s, hs_s = _ssd_specs(nc, True)
    bc_out =pl.BlockSpec((BLOCK, SSD_STATE), lambda g, c: (nc - 1 - c, g))

    def body(dyn_ref, y_ref, z_ref, pxs_ref, pb_ref, pc_ref, dtr_ref, hs_ref, dtb_ref, alog_ref, dskw_ref, nw_ref,
             e_ref, r_ref,
             dz_ref, dxs_ref, dbm_ref, dcm_ref, ddt_ref, dnw_ref, ddtb_ref, dalog_ref, ddsk_ref, g_scr):
        step = pl.program_id(1)
        c = nc - 1 - step

        @pl.when(step == 0)
        def _():
            g_scr[...] = jnp.zeros_like(g_scr)

        pxs, pb, pc = pxs_ref[...], pb_ref[...], pc_ref[...]
        sx, sb, sc = _sigmoid(pxs), _sigmoid(pb), _sigmoid(pc)
        xs, bm, cm = pxs * sx, pb * sb, pc * sc
        valid, dtr, dt, a, lam, cs, causal = _ssd_decay(dtr_ref, dtb_ref, alog_ref, c)
        cst = cs.T
        cs_last = cs[BLOCK - 1:BLOCK, :]
        bmb = bm.astype(BF16)
        cmb = cm.astype(BF16)
        cb = _dot_nt(cmb, bmb)
        hg = hs_ref[0, 0]
        hgb = hg.astype(BF16)
        yoff = _dot_nt(cmb, hgb)
        gn = g_scr[...]
        gnb = gn.astype(BF16)

        zv = z_ref[...]
        yv = y_ref[...]
        sgz = _sigmoid(zv)
        sz = zv * sgz
        gts = yv * sz
        rr = lax.rsqrt(jnp.mean(gts * gts, axis=-1, keepdims=True) + EPS)
        xh = gts * rr
        dynv = dyn_ref[...]
        gg = dynv * nw_ref[...]
        dgts = rr * (gg - xh * jnp.mean(gg * xh, axis=-1, keepdims=True))
        dnw = jnp.sum(dynv * xh, axis=0, keepdims=True)
        dy = dgts * sz
        dz_ref[...] = (dgts * yv * (sgz * (1.0 + zv * (1.0 - sgz)))).astype(BF16)

        ecs = jnp.exp(cs)
        dec = jnp.exp(cs_last - cs)
        eh = jnp.exp(cs_last)
        dt_w, ecs_w, dec_w = _spread_heads([dt, ecs, dec], e_ref)
        red_m = r_ref[...]

        def head_sums(v):
            return _dot_sel(v, red_m, terms=2)

        xdt = xs * dt_w
        q_all = _dot_nt(bmb, gnb)
        w_all = (dy * ecs_w).astype(BF16)
        e_hl = head_sums(q_all * xdt) * dec
        dcs_col = head_sums(dy * yoff) * ecs - e_hl
        gh = jnp.zeros((1, LANES), F32)
        prod = gn * hg
        for r in range(SSD_HPG):
            gh = _lane_put(gh, _sum_all(prod[_head_cols(r), :]), r)
        dcs_last = jnp.sum(e_hl, axis=0, keepdims=True) + eh * gh
        ddsk = jnp.sum(head_sums(dy * xs), axis=0, keepdims=True)
        cbt = _dot_nt(bmb, cmb)
        lane = lax.broadcasted_iota(jnp.int32, (BLOCK, LANES), 1)
        first = lane < SSD_HEADDIM
        causal_t = lax.broadcasted_iota(jnp.int32, (BLOCK, BLOCK), 1) >= lax.broadcasted_iota(
            jnp.int32, (BLOCK, BLOCK), 0)
        sub = lax.broadcasted_iota(jnp.int32, (SUBLANES, BLOCK), 0)
        dcs_row = jnp.zeros((SUBLANES, BLOCK), F32)
        dcb = jnp.zeros((BLOCK, BLOCK), F32)
        dxdt_pairs = []
        for j in range(SSD_HPG // 2):
            tile = slice(LANES * j, LANES * (j + 1))
            dy_p = dy[:, tile]
            dyb = dy_p.astype(BF16)
            xdtb = xdt[:, tile].astype(BF16)
            res = []
            for half, r in enumerate((2 * j, 2 * j + 1)):
                csc, csr = cs[:, r:r + 1], cst[r:r + 1, :]
                lm = jnp.exp(jnp.where(causal, csc - csr, NEG))
                lmt = jnp.exp(jnp.where(causal_t, csr - csc, NEG))
                keep = first if half == 0 else jnp.logical_not(first)
                gm = _dot_nt(jnp.where(keep, dy_p, 0.0).astype(BF16), xdtb) * lm
                dcb = dcb + gm
                mm_ = gm * cb
                dcs_col = dcs_col + jnp.where(lane == r, jnp.sum(mm_, axis=1, keepdims=True), 0.0)
                dcs_row = jnp.where(sub == r, jnp.sum(mm_, axis=0, keepdims=True), dcs_row)
                res.append(_dot((cbt * lmt).astype(BF16), dyb))
            dxdt_pairs.append(jnp.where(first, res[0], res[1]))
        dxdt = jnp.concatenate(dxdt_pairs, axis=1) + q_all * dec_w
        ddt_x = head_sums(dxdt * xs)
        dxs = dxdt * dt_w + dskw_ref[...] * dy
        dcbb = dcb.astype(BF16)
        dcm = _dot(w_all, hgb) + _dot(dcbb, bmb)
        dbm = _dot((xdt * dec_w).astype(BF16), gnb) + _dot_tn(dcbb, cmb)
        dh_off = _dot_tn(w_all, cmb)
        for r in range(SSD_HPG):
            rows = _head_cols(r)
            g_scr[rows, :] = gn[rows, :] * eh[:, r:r + 1] + dh_off[rows, :]

        pad_rows = jnp.zeros((BLOCK - SUBLANES, BLOCK), F32)
        dcs = dcs_col - jnp.concatenate([dcs_row, pad_rows], axis=0).T
        rsel = lax.broadcasted_iota(jnp.int32, (BLOCK, LANES), 0)
        dcs = dcs + jnp.where(rsel == BLOCK - 1, dcs_last, 0.0)
        ri = lax.broadcasted_iota(jnp.int32, (BLOCK, BLOCK), 0)
        ci = lax.broadcasted_iota(jnp.int32, (BLOCK, BLOCK), 1)
        dlam = _sel_dot((ci >= ri).astype(BF16), dcs)
        head = lane < SSD_HPG
        ddt = dlam * a + ddt_x
        ddtr = jnp.where(head, ddt * _sigmoid(dtr) * valid, 0.0)
        ddt_ref[...] = ddtr.astype(BF16)
        dalog = jnp.sum(jnp.where(head, dlam * lam, 0.0), axis=0, keepdims=True)
        ddtb = jnp.sum(ddtr, axis=0, keepdims=True)

        dxs_ref[...] = dxs * (sx * (1.0 + pxs * (1.0 - sx)))
        dbm_ref[...] = dbm * (sb * (1.0 + pb * (1.0 - sb)))
        dcm_ref[...] = dcm * (sc * (1.0 + pc * (1.0 - sc)))

        @pl.when(step == 0)
        def _():
            dnw_ref[...] = dnw
            ddtb_ref[...] = ddtb
            dalog_ref[...] = dalog
            ddsk_ref[...] = ddsk

        @pl.when(step > 0)
        def _():
            dnw_ref[...] += dnw
            ddtb_ref[...] += ddtb
            dalog_ref[...] += dalog
            ddsk_ref[...] += ddsk

    return _call_with_side(
        body, side, name="ssd_bwd", grid=(SSD_GROUPS, nc),
        in_specs=[xs_s, xs_s, xs_s, xs_s, b_s, c_s, lane_s, hs_s, vec_s, vec_s, wide_s, wide_s,
                  _const_spec((LANES, SSD_GW)), _const_spec((SSD_GW, LANES))],
        out_specs=[xs_s, xs_s, bc_out, bc_out, lane_s, wide_s, vec_s, vec_s, vec_s],
        out_shape=[jax.ShapeDtypeStruct((n, SSD_INNER), BF16), jax.ShapeDtypeStruct((n, SSD_INNER), F32),
                   jax.ShapeDtypeStruct((n, SSD_BC), F32), jax.ShapeDtypeStruct((n, SSD_BC), F32),
                   jax.ShapeDtypeStruct((n, DT_W), BF16), jax.ShapeDtypeStruct((1, SSD_INNER), F32),
                   jax.ShapeDtypeStruct((1, DT_W), F32), jax.ShapeDtypeStruct((1, DT_W), F32),
                   jax.ShapeDtypeStruct((1, DT_W), F32)],
        scratch_shapes=[pltpu.VMEM((SSD_GW, SSD_STATE), F32)],
        args=(dyn, y, z, pre, pre, pre, dt_raw, hsave, dtb, alog, dskip_w, norm_w, spread, spread.T))


def _bucket_table():
    def bucket(dist):
        d = np.maximum(dist, 0)
        half = REL_BUCKETS // 2
        big = half + (np.log(np.maximum(d, half).astype(np.float32) / np.float32(half))
                      / np.float32(math.log(REL_MAX_DIST / half)) * np.float32(REL_BUCKETS - half)).astype(np.int32)
        return np.where(d < half, d, np.minimum(big, REL_BUCKETS - 1)).astype(np.int32)

    l = np.arange(BLOCK)[None, :]
    band = bucket(l + BLOCK - np.arange(2 * BLOCK)[:, None])
    j = np.arange(BLOCK)[:, None]
    tables = [np.concatenate([bucket(v * BLOCK + l - j), band], axis=0) for v in range(3)]
    return np.concatenate([t.reshape(-1) for t in tables])


def _onehot_t():
    buckets = jnp.asarray(_bucket_table())
    return (buckets[None, :] == jnp.arange(REL_BUCKETS, dtype=jnp.int32)[:, None]).astype(F32)


def _bias_tables(rel_t, onehot_t):
    def body(r_ref, oh_ref, o_ref):
        o_ref[...] = jnp.dot(r_ref[...], oh_ref[...], precision=HIGHEST, preferred_element_type=F32)

    return pl.pallas_call(
        body, name="bias_tables", grid=(NT_ALL // NT_TILE,),
        in_specs=[pl.BlockSpec((ATT_HEADS, REL_BUCKETS), lambda i: (0, 0)),
                  pl.BlockSpec((REL_BUCKETS, NT_TILE), lambda i: (0, i))],
        out_specs=pl.BlockSpec((ATT_HEADS, NT_TILE), lambda i: (0, i)),
        out_shape=jax.ShapeDtypeStruct((ATT_HEADS, NT_ALL), F32),
        compiler_params=_cparams("parallel"))(rel_t, onehot_t)


def _bias_grad(dtab, onehot_t):
    def body(d_ref, oh_ref, o_ref):
        i = pl.program_id(0)
        p = lax.dot_general(d_ref[...], oh_ref[...], (((1,), (1,)), ((), ())), precision=HIGHEST,
                            preferred_element_type=F32)

        @pl.when(i == 0)
        def _():
            o_ref[...] = p

        @pl.when(i > 0)
        def _():
            o_ref[...] += p

    return pl.pallas_call(
        body, name="bias_grad", grid=(NT_ALL // NT_TILE,),
        in_specs=[pl.BlockSpec((ATT_HEADS, NT_TILE), lambda i: (0, i)),
                  pl.BlockSpec((REL_BUCKETS, NT_TILE), lambda i: (0, i))],
        out_specs=pl.BlockSpec((ATT_HEADS, REL_BUCKETS), lambda i: (0, 0)),
        out_shape=jax.ShapeDtypeStruct((ATT_HEADS, REL_BUCKETS), F32),
        compiler_params=_cparams("arbitrary"))(dtab, onehot_t)


def _att_mask_t(n, copies):
    far = 4 * BLOCK
    kk = lax.broadcasted_iota(jnp.int32, (N_KEYS, copies * BLOCK), 0)
    li = lax.broadcasted_iota(jnp.int32, (N_KEYS, copies * BLOCK), 1) & (BLOCK - 1)
    meta_ok = (kk >= PAD) & (kk < BLOCK) & (li + jnp.where(n >= 1, far, 0) >= kk)
    prev_ok = (kk >= BLOCK) & (kk < 2 * BLOCK) & (kk - BLOCK > li + jnp.where(n >= 2, 0, far))
    cur_ok = (kk >= 2 * BLOCK) & (kk - 2 * BLOCK <= li - jnp.where(n >= 1, 0, far))
    return meta_ok | prev_ok | cur_ok


def _att_kv(meta_ref, prev_ref, cur_ref):
    kv = jnp.concatenate([meta_ref[...], prev_ref[...], cur_ref[...]], axis=0)
    first = lax.broadcasted_iota(jnp.int32, (N_KEYS, LANES), 1) < ATT_HEADDIM
    out = []
    for pair in (kv[:, :LANES], kv[:, LANES:]):
        swapped = pltpu.roll(pair, ATT_HEADDIM, 1)
        out.append([jnp.where(first, pair, swapped).astype(BF16), jnp.where(first, swapped, pair).astype(BF16)])
    return out[0], out[1]


def _split_heads(x_pair, first):
    return jnp.concatenate([jnp.where(first, x_pair, 0.0), jnp.where(first, 0.0, x_pair)], axis=0).astype(BF16)


def _att_probs_t(qm2, k_dup, t_ref, j, mask2, sink_ref):
    scale = ATT_HEADDIM ** -0.5
    bias2 = jnp.concatenate([t_ref[0, 2 * j], t_ref[0, 2 * j + 1]], axis=1)
    second = lax.broadcasted_iota(jnp.int32, (1, 2 * BLOCK), 1) >= BLOCK
    sink2 = jnp.where(second, sink_ref[0:1, 2 * j + 1:2 * j + 2], sink_ref[0:1, 2 * j:2 * j + 1])
    s_t = jnp.where(mask2, _dot_nt(k_dup, qm2) * scale + bias2, NEG)
    mx = jnp.maximum(jnp.max(s_t, axis=0, keepdims=True), sink2)
    p_t = jnp.exp(s_t - mx)
    p_s = jnp.exp(sink2 - mx)
    inv = 1.0 / (jnp.sum(p_t, axis=0, keepdims=True) + p_s)
    return p_t * inv, p_s * inv


def _att_specs(nb, rev):
    def nidx(i):
        return nb - 1 - i if rev else i

    kvb = ATT_Q // (2 * ATT_KV)
    q_s = pl.BlockSpec((BLOCK, ATT_Q), lambda i: (nidx(i), 0))
    cur = pl.BlockSpec((BLOCK, 2 * ATT_KV), lambda i: (nidx(i), kvb))
    prev = pl.BlockSpec((BLOCK, 2 * ATT_KV), lambda i: (jnp.maximum(nidx(i) - 1, 0), kvb))
    meta = pl.BlockSpec((BLOCK, 2 * ATT_KV), lambda i: (0, kvb))
    table = pl.BlockSpec((1, ATT_HEADS, N_KEYS, BLOCK), lambda i: (jnp.minimum(nidx(i), 2), 0, 0, 0))
    sink = pl.BlockSpec((1, LANES), lambda i: (0, 0))
    return q_s, cur, prev, meta, table, sink


def _attn_fwd(qkv, tables, sinks):
    n = qkv.shape[0]
    nb = n // BLOCK
    q_s, cur_s, prev_s, meta_s, t_s, sink_s = _att_specs(nb, False)

    def body(q_ref, cur_ref, prev_ref, meta_ref, t_ref, sink_ref, o_ref):
        blk = pl.program_id(0)
        mask_t = _att_mask_t(blk, 1)
        k_dup, v_dup = _att_kv(meta_ref, prev_ref, cur_ref)
        v_dup_t = [v.T for v in v_dup]
        first = lax.broadcasted_iota(jnp.int32, (BLOCK, LANES), 1) < ATT_HEADDIM
        top = lax.broadcasted_iota(jnp.int32, (LANES, BLOCK), 0) < ATT_HEADDIM
        scale = ATT_HEADDIM ** -0.5
        for j in range(ATT_HEADS // 2):
            kh = 2 * j // ATT_GQ
            tile = slice(LANES * j, LANES * (j + 1))
            q_p = q_ref[:, tile]
            res = []
            for half, h in enumerate((2 * j, 2 * j + 1)):
                qm = jnp.where(first if half == 0 else jnp.logical_not(first), q_p, 0.0).astype(BF16)
                sink = sink_ref[0:1, h:h + 1]
                s_t = jnp.where(mask_t, _dot_nt(k_dup[kh], qm) * scale + t_ref[0, h], NEG)
                mx = jnp.maximum(jnp.max(s_t, axis=0, keepdims=True), sink)
                p_t = jnp.exp(s_t - mx)
                inv = 1.0 / (jnp.sum(p_t, axis=0, keepdims=True) + jnp.exp(sink - mx))
                res.append(_dot(v_dup_t[kh], (p_t * inv).astype(BF16)))
            o_ref[:, tile] = jnp.where(top, res[0], res[1]).T.astype(BF16)

    return pl.pallas_call(
        body, name="attn_fwd", grid=(nb,),
        in_specs=[q_s, cur_s, prev_s, meta_s, t_s, sink_s],
        out_specs=q_s,
        out_shape=jax.ShapeDtypeStruct((n, ATT_Q), BF16),
        compiler_params=_cparams("parallel"))(qkv, qkv, qkv, qkv, tables, sinks)


def _attn_bwd(datt, qkv, tables, sinks):
    n = qkv.shape[0]
    nb = n // BLOCK
    q_s, cur_s, prev_s, meta_s, t_s, sink_s = _att_specs(nb, True)
    dqkv_s = pl.BlockSpec((BLOCK, ATT_Q + 2 * ATT_KV), lambda i: (nb - 1 - i, 0))
    scale = ATT_HEADDIM ** -0.5

    def body(do_ref, q_ref, cur_ref, prev_ref, meta_ref, t_ref, sink_ref,
             dqkv_ref, dt_ref, dsink_ref, carry_scr, meta_scr):
        step = pl.program_id(0)
        blk = nb - 1 - step
        mask2 = _att_mask_t(blk, 2)
        k_dup, v_dup = _att_kv(meta_ref, prev_ref, cur_ref)
        k_dup_t = [k.T for k in k_dup]

        @pl.when(step == 0)
        def _():
            carry_scr[...] = jnp.zeros_like(carry_scr)
            meta_scr[...] = jnp.zeros_like(meta_scr)
            dsink_ref[...] = jnp.zeros_like(dsink_ref)

        @pl.when((step == 0) | (blk <= 1))
        def _():
            dt_ref[...] = jnp.zeros_like(dt_ref)

        first = lax.broadcasted_iota(jnp.int32, (BLOCK, LANES), 1) < ATT_HEADDIM
        top = lax.broadcasted_iota(jnp.int32, (LANES, BLOCK), 0) < ATT_HEADDIM
        first_k = lax.broadcasted_iota(jnp.int32, (N_KEYS, LANES), 1) < ATT_HEADDIM
        dsink = jnp.zeros((1, LANES), F32)
        dk_acc = [None] * ATT_KV_HEADS
        dv_acc = [None] * ATT_KV_HEADS
        for j in range(ATT_HEADS // 2):
            kh = 2 * j // ATT_GQ
            tile = slice(LANES * j, LANES * (j + 1))
            qm2 = _split_heads(q_ref[:, tile], first)
            dom2 = _split_heads(do_ref[:, tile], first)
            p_t, p_s = _att_probs_t(qm2, k_dup[kh], t_ref, j, mask2, sink_ref)
            dp_t = _dot_nt(v_dup[kh], dom2)
            delta = jnp.sum(p_t * dp_t, axis=0, keepdims=True)
            ds_t = p_t * (dp_t - delta)
            sink_terms = p_s * delta
            for half in range(2):
                cols = slice(BLOCK * half, BLOCK * (half + 1))
                dsink = _lane_put(dsink, -jnp.sum(sink_terms[:, cols], axis=1, keepdims=True), 2 * j + half)
                dt_ref[0, 2 * j + half] += ds_t[:, cols]
            ds_tb = ds_t.astype(BF16)
            dq_t = _dot(k_dup_t[kh], ds_tb)
            dqkv_ref[:, tile] = (jnp.where(top, dq_t[:, :BLOCK], dq_t[:, BLOCK:]).T * scale).astype(BF16)
            dk_part, dv_part = _dot(ds_tb, qm2), _dot(p_t.astype(BF16), dom2)
            dk_acc[kh] = dk_part if dk_acc[kh] is None else dk_acc[kh] + dk_part
            dv_acc[kh] = dv_part if dv_acc[kh] is None else dv_acc[kh] + dv_part
        dsink_ref[...] += dsink
        folded = [a + pltpu.roll(a, ATT_HEADDIM, 1) for a in dk_acc + dv_acc]
        dkv = jnp.concatenate([jnp.where(first_k, folded[0], folded[1]) * scale,
                               jnp.where(first_k, folded[2], folded[3])], axis=1)
        meta_scr[...] += dkv[:BLOCK, :]
        own = dkv[2 * BLOCK:, :] + carry_scr[...]
        carry_scr[...] = dkv[BLOCK:2 * BLOCK, :]

        @pl.when(blk > 0)
        def _():
            dqkv_ref[:, ATT_Q:] = own.astype(BF16)

        @pl.when(blk == 0)
        def _():
            dqkv_ref[:, ATT_Q:] = (own + meta_scr[...]).astype(BF16)

    return pl.pallas_call(
        body, name="attn_bwd", grid=(nb,),
        in_specs=[q_s, q_s, cur_s, prev_s, meta_s, t_s, sink_s],
        out_specs=[dqkv_s, t_s, sink_s],
        out_shape=[jax.ShapeDtypeStruct((n, ATT_Q + 2 * ATT_KV), BF16),
                   jax.ShapeDtypeStruct((3, ATT_HEADS, N_KEYS, BLOCK), F32),
                   jax.ShapeDtypeStruct((1, LANES), F32)],
        scratch_shapes=[pltpu.VMEM((BLOCK, 2 * ATT_KV), F32), pltpu.VMEM((BLOCK, 2 * ATT_KV), F32)],
        compiler_params=_cparams("arbitrary"))(datt, qkv, qkv, qkv, qkv, tables, sinks)


def _merge_out_fwd(gates, y_ssd, y_att, gate_b, w_out, h):
    n = gates.shape[0]
    tm = _row_tile(n, 416)

    def body(gs_ref, ga_ref, ys_ref, ya_ref, gb_ref, w_ref, h_ref, m_ref, o_ref):
        merged = (_sigmoid(gs_ref[...] + gb_ref[0:1, :]) * ys_ref[...]
                  + _sigmoid(ga_ref[...] + gb_ref[1:2, :]) * ya_ref[...]).astype(BF16)
        m_ref[...] = merged
        row = pl.program_id(0) * tm + lax.broadcasted_iota(jnp.int32, (tm, 1), 0)
        o_ref[...] = jnp.where(row >= PAD, _dot(merged, w_ref[...]), 0.0) + h_ref[...]

    row = pl.BlockSpec((tm, D_MODEL), lambda i: (i, 0))
    return pl.pallas_call(
        body, name="merge_out_fwd", grid=(n // tm,),
        in_specs=[row, pl.BlockSpec((tm, D_MODEL), lambda i: (i, 1)), row, row,
                  pl.BlockSpec((2, D_MODEL), lambda i: (0, 0)), pl.BlockSpec((D_MODEL, D_MODEL), lambda i: (0, 0)), row],
        out_specs=[row, row],
        out_shape=[jax.ShapeDtypeStruct((n, D_MODEL), BF16), jax.ShapeDtypeStruct((n, D_MODEL), F32)],
        compiler_params=_cparams("parallel"))(gates, gates, y_ssd, y_att, gate_b, w_out, h)


def _merge_out_bwd(dh, w_out, gates, y_ssd, y_att, gate_b):
    n = gates.shape[0]
    tm = _row_tile(n, 416)

    def body(dh_ref, w_ref, gs_ref, ga_ref, ys_ref, ya_ref, gb_ref, dys_ref, dya_ref, dg_ref, dgb_ref):
        i = pl.program_id(0)
        row = i * tm + lax.broadcasted_iota(jnp.int32, (tm, 1), 0)
        dmv = jnp.where(row >= PAD, _dot_nt(dh_ref[...].astype(BF16), w_ref[...]), 0.0)
        ss =_sigmoid(gs_ref[...] + gb_ref[0:1, :])
        sa = _sigmoid(ga_ref[...] + gb_ref[1:2, :])
        dys_ref[...] = (dmv * ss).astype(BF16)
        dya_ref[...] = (dmv * sa).astype(BF16)
        dgs = dmv * ys_ref[...] * ss * (1.0 - ss)
        dga = dmv * ya_ref[...] * sa * (1.0 - sa)
        dg_ref[:, :D_MODEL] = dgs.astype(BF16)
        dg_ref[:, D_MODEL:] = dga.astype(BF16)
        part = jnp.concatenate([jnp.sum(dgs, axis=0, keepdims=True), jnp.sum(dga, axis=0, keepdims=True)], axis=0)

        @pl.when(i == 0)
        def _():
            dgb_ref[...] = part

        @pl.when(i > 0)
        def _():
            dgb_ref[...] += part

    row = pl.BlockSpec((tm, D_MODEL), lambda i: (i, 0))
    gb = pl.BlockSpec((2, D_MODEL), lambda i: (0, 0))
    return pl.pallas_call(
        body, name="merge_out_bwd", grid=(n // tm,),
        in_specs=[row, pl.BlockSpec((D_MODEL, D_MODEL), lambda i: (0, 0)), row,
                  pl.BlockSpec((tm, D_MODEL), lambda i: (i, 1)), row, row, gb],
        out_specs=[row, row, pl.BlockSpec((tm, 2 * D_MODEL), lambda i: (i, 0)), gb],
        out_shape=[jax.ShapeDtypeStruct((n, D_MODEL), BF16), jax.ShapeDtypeStruct((n, D_MODEL), BF16),
                   jax.ShapeDtypeStruct((n, 2 * D_MODEL), BF16), jax.ShapeDtypeStruct((2, D_MODEL), F32)],
        compiler_params=_cparams("arbitrary"))(dh, w_out, gates, gates, y_ssd, y_att, gate_b)


def _col_move(srcs, outs, pieces, *, name):
    rows = srcs[0].shape[-2]
    tr = _row_tile(rows, 128)
    n_src = len(srcs)
    covered = [sum(p[6] for p in pieces if p[0] == o) for o in range(len(outs))]
    total = [int(np.prod(shp)) // rows for shp, _ in outs]

    def body(*refs):
        in_refs, out_refs = refs[:n_src], refs[n_src:]
        for o, ref in enumerate(out_refs):
            if covered[o] != total[o]:
                ref[...] = jnp.zeros_like(ref)
        for o, ol, oc, s, sl, sc, width in pieces:
            val = in_refs[s][:, sc:sc + width] if sl is None else in_refs[s][sl, :, sc:sc + width]
            val = val.astype(outs[o][1])
            if ol is None:
                out_refs[o][:, oc:oc + width] = val
            else:
                out_refs[o][ol, :, oc:oc + width] = val

    def spec(shape):
        if len(shape) == 2:
            return pl.BlockSpec((tr, shape[1]), lambda i: (i, 0))
        return pl.BlockSpec((shape[0], tr, shape[2]), lambda i: (0, i, 0))

    return pl.pallas_call(
        body, name=name, grid=(rows // tr,),
        in_specs=[spec(a.shape) for a in srcs], out_specs=[spec(shp) for shp, _ in outs],
        out_shape=[jax.ShapeDtypeStruct(shp, dt) for shp, dt in outs],
        compiler_params=_cparams("parallel"))(*srcs)


def _shard_pieces(seg_ranges, shard_w):
    out = []
    for seg, runs in enumerate(seg_ranges):
        for g0, width, s0 in runs:
            done = 0
            while done < width:
                dev, col = divmod(g0 + done, shard_w)
                take = min(width - done, shard_w - col)
                out.append((seg, s0 + done, dev, col, take))
                done += take
    return out


_CHIP_RELATIONS = [(1, 0, 0), (0, 1, 0), (1, 1, 0)]
N_CHIPS = 4


def _gather_two_level(arrays, *, name):
    outs = _run_plan(_gather_plan(arrays), name)
    return [o.reshape((N_DEV,) + a.shape) for o, a in zip(outs, arrays)]


class _CommPlan:
    def __init__(self, arrays, out_shape, scratch_shapes, phases):
        self.arrays, self.out_shape, self.scratch_shapes, self.phases = arrays, out_shape, scratch_shapes, phases


def _run_plan(plan, name):
    n_arr = len(plan.arrays)

    def body(*refs):
        ins, outs, sems = refs[:n_arr], refs[n_arr:2 * n_arr], refs[2 * n_arr:]
        for phase in plan.phases:
            phase(ins, outs, sems)

    any_spec = pl.BlockSpec(memory_space=pl.ANY)
    return pl.pallas_call(
        body, name=name, in_specs=[any_spec] * n_arr, out_specs=[any_spec] * n_arr, out_shape=plan.out_shape,
        scratch_shapes=plan.scratch_shapes)(*plan.arrays)


def _gather_plan(arrays):
    n_arr = len(arrays)
    n_chips = len(_CHIP_RELATIONS)
    n_pair = 1 + 2 * n_chips

    def where():
        x, y, c = lax.axis_index("x"), lax.axis_index("y"), lax.axis_index("c")
        return x, y, c, (x, y, 1 - c), [(x ^ dx, y ^ dy) for dx, dy, _ in _CHIP_RELATIONS]

    def copy(outs, sems, a, k, block, to, src=None):
        slot = outs[a].at[2 * block[0] + block[1], block[2]]
        return pltpu.make_async_remote_copy(
            src_ref=slot if src is None else src, dst_ref=slot, send_sem=sems[0].at[a * n_pair + k],
            recv_sem=sems[1].at[a * n_pair + k], device_id=to, device_id_type=MESH)

    def mine(ins, outs, sems, a, x, y, c):
        return pltpu.make_async_copy(ins[a], outs[a].at[2 * x + y, c], sems[2].at[a])

    def first_copies(ins, outs, sems, a, x, y, c, sibling, chips):
        return ([copy(outs, sems, a, 0, (x, y, c), sibling, src=ins[a])]
                + [copy(outs, sems, a, 1 + j, (x, y, c), (*chip, c), src=ins[a]) for j, chip in enumerate(chips)])

    def start(ins, outs, sems):
        x, y, c, sibling, chips = where()
        for a in range(n_arr):
            mine(ins, outs, sems, a, x, y, c).start()
            for cp in first_copies(ins, outs, sems, a, x, y, c, sibling, chips):
                cp.start()

    def pass_on(ins, outs, sems):
        x, y, c, sibling, chips = where()
        for j, chip in enumerate(chips):
            for a in range(n_arr):
                copy(outs, sems, a, 1 + j, (*chip, c), (x, y, c)).wait_recv()
                copy(outs, sems, a, 1 + n_chips + j, (*chip, c), sibling).start()

    def finish(ins, outs, sems):
        x, y, c, sibling, chips = where()
        for a in range(n_arr):
            copy(outs, sems, a, 0, (x, y, 1 - c), (x, y, c)).wait_recv()
            for j, chip in enumerate(chips):
                copy(outs, sems, a, 1 + n_chips + j, (*chip, 1 - c), (x, y, c)).wait_recv()
        for a in range(n_arr):
            for cp in first_copies(ins, outs, sems, a, x, y, c, sibling, chips):
                cp.wait_send()
            for j, chip in enumerate(chips):
                copy(outs, sems, a, 1 + n_chips + j, (*chip, c), sibling).wait_send()
            mine(ins, outs, sems, a, x, y, c).wait()

    return _CommPlan(
        arrays, [jax.ShapeDtypeStruct((N_CHIPS, 2) + a.shape, a.dtype) for a in arrays],
        [pltpu.SemaphoreType.DMA((n_arr * n_pair,)), pltpu.SemaphoreType.DMA((n_arr * n_pair,)),
         pltpu.SemaphoreType.DMA((n_arr,))],
        (start, pass_on, finish))


_ALL_RELATIONS = [(dx, dy, dc) for dx in (0, 1) for dy in (0, 1) for dc in (0, 1)][1:]


def _all_to_all_plan(arrays, scatter=None):
    n_arr = len(arrays)
    n_rel = len(_ALL_RELATIONS)
    scatter = scatter or [True] * n_arr

    def block(ins, a, p):
        return ins[a].at[p] if scatter[a] else ins[a]

    def local_copies(ins, outs, sems):
        me = 4 * lax.axis_index("x") + 2 * lax.axis_index("y") + lax.axis_index("c")
        return [pltpu.make_async_copy(block(ins, a, me), outs[a].at[me], sems[2].at[a]) for a in range(n_arr)]

    def remote_copies(ins, outs, sems, arrivals):
        x, y, c = lax.axis_index("x"), lax.axis_index("y"), lax.axis_index("c")
        me = 4 * x + 2 * y + c
        out = []
        for k, (dx, dy, dc) in enumerate(_ALL_RELATIONS):
            px, py, pc = x ^ dx, y ^ dy, c ^ dc
            peer = 4 * px + 2 * py + pc
            for a in range(n_arr):
                out.append(pltpu.make_async_remote_copy(
                    src_ref=block(ins, a, peer), dst_ref=outs[a].at[peer if arrivals else me],
                    send_sem=sems[0].at[a * n_rel + k], recv_sem=sems[1].at[a * n_rel + k],
                    device_id=(x, y, c) if arrivals else (px, py, pc), device_id_type=MESH))
        return out

    def start(ins, outs, sems):
        for cp in local_copies(ins, outs, sems) + remote_copies(ins, outs, sems, False):
            cp.start()

    def pass_on(ins, outs, sems):
        pass

    def finish(ins, outs, sems):
        for send in remote_copies(ins, outs, sems, False):
            send.wait_send()
        for arrival in remote_copies(ins, outs, sems, True):
            arrival.wait_recv()
        for cp in local_copies(ins, outs, sems):
            cp.wait()

    return _CommPlan(
        arrays, [jax.ShapeDtypeStruct(a.shape if s else (N_DEV,) + a.shape, a.dtype) for a, s in zip(arrays, scatter)],
        [pltpu.SemaphoreType.DMA((n_arr * n_rel,)), pltpu.SemaphoreType.DMA((n_arr * n_rel,)),
         pltpu.SemaphoreType.DMA((n_arr,))],
        (start, pass_on, finish))


def _adamw(w, gslots, m, v, *, name):
    rows, cols = w.shape
    n_slots = gslots.shape[0]
    tr = _row_tile(rows, 128) if rows % 16 == 0 else rows

    def body(w_ref, g_ref, m_ref, v_ref, go_ref, d_ref, mo_ref, vo_ref):
        g = g_ref[0].astype(F32)
        for s in range(1, n_slots):
            g = g + g_ref[s].astype(F32)
        mn = ADAM_B1 * m_ref[...] + (1.0 - ADAM_B1) * g
        vn = ADAM_B2 * v_ref[...] + (1.0 - ADAM_B2) * (g * g)
        go_ref[...] = g
        mo_ref[...] = mn
        vo_ref[...] = vn
        m_hat = mn / (1.0 - ADAM_B1 ** ADAM_STEP)
        v_hat = vn / (1.0 - ADAM_B2 ** ADAM_STEP)
        d_ref[...] = -ADAM_LR * (m_hat / (jnp.sqrt(v_hat) + ADAM_EPS) + ADAM_WD * w_ref[...])

    blk = pl.BlockSpec((tr, cols), lambda i: (i, 0))
    shp = jax.ShapeDtypeStruct((rows, cols), F32)
    return pl.pallas_call(
        body, name=name, grid=(rows // tr,),
        in_specs=[blk, pl.BlockSpec((n_slots, tr, cols), lambda i: (0, i, 0)), blk, blk],
        out_specs=[blk] * 4, out_shape=[shp] * 4,
        compiler_params=_cparams("parallel"))(w, gslots, m, v)


_BIG = ("w_in", "w_ssd_branch", "w_attn_branch", "w_out", "w_ffn_in", "w_ffn_out")
_SMALL_SHARDED = ("meta_tokens", "ssd_conv_w", "gate_b", "ffn_conv_w")
_SMALL_REPLICATED = ("norm_mix_w", "ssd_conv_b", "ssd_dt_bias", "ssd_a_log", "ssd_d", "ssd_norm_w", "attn_sinks",
                     "rel_bias", "norm_ffn_w", "ffn_conv_b", "norm_final_w")
_WEIGHTS = ("meta_tokens", "norm_mix_w", "w_in", "ssd_conv_w", "ssd_conv_b", "ssd_dt_bias", "ssd_a_log", "ssd_d",
            "ssd_norm_w", "w_ssd_branch", "w_attn_branch", "attn_sinks", "rel_bias", "gate_b", "w_out", "norm_ffn_w",
            "w_ffn_in", "ffn_conv_w", "ffn_conv_b", "w_ffn_out", "norm_final_w")
_ROW_SHARDED = ("w_ssd_branch", "w_attn_branch", "w_out", "w_ffn_out")
_COL_SHARDED = ("w_in", "w_ffn_in", "meta_tokens", "ssd_conv_w", "gate_b", "ffn_conv_w")
_IN_SEGS = (("z", SSD_INNER), ("xbc", SSD_XBC), ("dt", SSD_HEADS), ("qkv", ATT_Q + 2 * ATT_KV), ("g", 2 * D_MODEL))


def _pack_rows(flat_parts, width, row_mult):
    flat = jnp.concatenate([p.reshape(-1) for p in flat_parts])
    pad = (-flat.shape[0]) % (width * row_mult)
    if pad:
        flat = jnp.concatenate([flat, jnp.zeros((pad,), flat.dtype)])
    return flat.reshape(-1, width)


def _unpack(flat, shapes):
    out, off = [], 0
    for shp in shapes:
        size = int(np.prod(shp))
        out.append(flat[off:off + size].reshape(shp))
        off += size
    return out


def _gather_full(stack, name, shard_shape):
    if name in _COL_SHARDED:
        return jnp.transpose(stack, (1, 0, 2)).reshape(shard_shape[0], N_DEV * shard_shape[1])
    return stack.reshape(N_DEV * shard_shape[0], shard_shape[1])


_IN_SEG_W = {"z": SSD_INNER, "xbc": SSD_XBC, "dt": DT_W, "qkv": ATT_Q + 2 * ATT_KV, "g": 2 * D_MODEL}
_IN_SHARD_W = (SSD_INNER + SSD_XBC + SSD_HEADS + ATT_Q + 2 * ATT_KV + 2 * D_MODEL) // N_DEV
_FFN_SHARD_W = 2 * D_FF // N_DEV


def _in_seg_runs():
    runs, off = [], 0
    for nm, width in _IN_SEGS:
        if nm == "dt":
            runs.append([(off + SSD_HPG * g, SSD_HPG, LANES * g) for g in range(SSD_GROUPS)])
        else:
            runs.append([(off, width, 0)])
        off += width
    return runs


def _w_in_to_segments(stack):
    pieces = [(seg, None, scol, 0, dev, col, w) for seg, scol, dev, col, w in _shard_pieces(_in_seg_runs(), _IN_SHARD_W)]
    outs = [((D_MODEL, _IN_SEG_W[nm]), stack.dtype) for nm, _ in _IN_SEGS]
    return dict(zip([nm for nm, _ in _IN_SEGS], _col_move([stack], outs, pieces, name="w_in_segments")))


def _segments_to_w_in_shards(seg_grads):
    pieces = [(0, dev, col, seg, None, scol, w) for seg, scol, dev, col, w in _shard_pieces(_in_seg_runs(), _IN_SHARD_W)]
    return _col_move(seg_grads, [((N_DEV, D_MODEL, _IN_SHARD_W), seg_grads[0].dtype)], pieces, name="g_w_in_shards")[0]


def _ffn_in_from_shards(stack):
    pieces = [(0, None, scol, 0, dev, col, w)
              for _, scol, dev, col, w in _shard_pieces([[(0, 2 * D_FF, 0)]], _FFN_SHARD_W)]
    return _col_move([stack], [((D_MODEL, 2 * D_FF), stack.dtype)], pieces, name="w_ffn_in_full")[0]


def _ffn_in_to_shards(g_up, g_gate):
    pieces = [(0, dev, col, seg, None, scol, w)
              for seg, scol, dev, col, w in _shard_pieces([[(0, D_FF, 0)], [(D_FF, D_FF, 0)]], _FFN_SHARD_W)]
    return _col_move([g_up, g_gate], [((N_DEV, D_MODEL, _FFN_SHARD_W), g_up.dtype)], pieces, name="g_w_ffn_in_shards")[0]


def _dt_spread(w_dt):
    k = w_dt.shape[0]
    w4 = w_dt.reshape(k, SSD_GROUPS, SSD_HPG)
    return jnp.pad(w4, ((0, 0), (0, 0), (0, LANES - SSD_HPG))).reshape(k, DT_W)


def _dt_gather(w_wide):
    k = w_wide.shape[0]
    return w_wide.reshape(k, SSD_GROUPS, LANES)[:, :, :SSD_HPG].reshape(k, SSD_HEADS)


class _LateExchanges:
    def __init__(self, two_d, shape2):
        self.two_d, self.shape2 = two_d, shape2
        self.early_grads_received = None
        self.w_in_grads_received = None

    def row_pack(self, tree):
        return jnp.concatenate([tree[k] for k in _ROW_SHARDED], axis=0)

    def late_weights_plan(self):
        return _gather_plan([self.two_d["w_ffn_in"].astype(BF16), self.row_pack(self.two_d).astype(BF16)])

    def late_weights(self, gathered):
        w_ffn_in_all, rows_all = [g.reshape((N_DEV,) + g.shape[2:]) for g in gathered]
        out = {"w_ffn_in": _ffn_in_from_shards(w_ffn_in_all)}
        off = 0
        for k in _ROW_SHARDED:
            r = self.shape2[k][0]
            out[k] = rows_all[:, off:off + r].reshape(N_DEV * r, D_MODEL)
            off += r
        return out

    def early_grads_plan(self, grads):
        rows_send = jnp.concatenate([grads[k].reshape(N_DEV, self.shape2[k][0], D_MODEL) for k in _ROW_SHARDED], axis=1)
        return _all_to_all_plan([_ffn_in_to_shards(*grads["w_ffn_in"]), rows_send])

    def w_in_grads_plan(self, seg_grads):
        return _all_to_all_plan([_segments_to_w_in_shards(seg_grads)])


def _local_step(x, target, w, exchanges=None):
    h0 = jnp.concatenate([jnp.zeros((PAD, D_MODEL), F32), w["meta_tokens"], x], axis=0)
    segs = w["in_segs"]

    dtb = _dt_spread(w["ssd_dt_bias"])
    alog = _dt_spread(w["ssd_a_log"])
    dskip_w = jnp.repeat(w["ssd_d"], SSD_HEADDIM, axis=1)
    sinks = jnp.pad(w["attn_sinks"], ((0, 0), (0, LANES - ATT_HEADS)))
    onehot_t = _onehot_t()
    tables = jnp.transpose(_bias_tables(w["rel_bias"].T, onehot_t).reshape(ATT_HEADS, 3, N_KEYS, BLOCK), (1, 0, 2, 3))

    u = _rms_fwd(h0, w["norm_mix_w"], name="rms_mix_fwd")
    z = _mm(u, segs["z"], name="in_z")
    xbc, pre = _mm_conv_fwd(u, segs["xbc"], w["ssd_conv_w"], w["ssd_conv_b"], name="in_xbc_conv_fwd")
    dt_raw = _mm(u, segs["dt"], name="in_dt")
    qkv = _mm(u, segs["qkv"], out_dtype=BF16, name="in_qkv")
    gates = _mm(u, segs["g"], name="in_g")
    (y, yn, hsave), gathered = _ssd_fwd(pre, dt_raw, z, dtb, alog, dskip_w, w["ssd_norm_w"],
                                        side=None if exchanges is None else exchanges.late_weights_plan())
    if exchanges is not None:
        w = {**w, **exchanges.late_weights(gathered)}
    w_ffn_up, w_ffn_gate = w["w_ffn_in"][:, :D_FF], w["w_ffn_in"][:, D_FF:]
    y_ssd = _mm(yn, w["w_ssd_branch"], out_dtype=BF16, name="ssd_out")
    att = _attn_fwd(qkv, tables, sinks)
    y_att = _mm(att, w["w_attn_branch"], out_dtype=BF16, name="att_out")
    merged, h1 = _merge_out_fwd(gates, y_ssd, y_att, w["gate_b"], w["w_out"], h0)
    u2 = _rms_fwd(h1, w["norm_ffn_w"], name="rms_ffn_fwd")
    x_up, x_gate, hid_up, hid_gate, act = _ffn_in_act_fwd(u2, w["w_ffn_in"], w["ffn_conv_w"], w["ffn_conv_b"])
    h2 = _mm(act, w["w_ffn_out"], c=h1, mask=True, name="ffn_out")
    dh2, dh2_b, loss_row, g_norm_final = _final_loss(h2, w["norm_final_w"], target)

    grads = {"norm_final_w": g_norm_final}
    grads["w_ffn_out"] = _mm(act, dh2_b, ta=True, mask=True, out_dtype=BF16, name="g_w_ffn_out")
    dx_up, dx_gate, dcw_up, dcw_gate, dcb_up, dcb_gate = _ffn_out_act_bwd(
        dh2_b, w["w_ffn_out"], hid_up, hid_gate, x_up, x_gate, w["ffn_conv_w"])
    grads["ffn_conv_w"] = jnp.concatenate([dcw_up, dcw_gate], axis=1)
    grads["ffn_conv_b"] = jnp.concatenate([dcb_up, dcb_gate], axis=1)
    (dh1, grads["norm_ffn_w"]), _ = _mm_rms_bwd([(dx_up, w_ffn_up), (dx_gate, w_ffn_gate)], h1, w["norm_ffn_w"], dh2,
                                                name="d_u2_rms_bwd")
    grads["w_ffn_in"] = (_mm(u2, dx_up, ta=True, out_dtype=BF16, name="g_w_ffn_up"),
                         _mm(u2, dx_gate, ta=True, out_dtype=BF16, name="g_w_ffn_gate"))

    grads["w_out"] = _mm(merged, dh1, ta=True, mask=True, out_dtype=BF16, name="g_w_out")
    dy_ssd, dy_att, dgates, grads["gate_b"] = _merge_out_bwd(dh1, w["w_out"], gates, y_ssd, y_att, w["gate_b"])
    dyn = _mm(dy_ssd, w["w_ssd_branch"], tb=True, name="d_yn")
    grads["w_ssd_branch"] = _mm(yn, dy_ssd, ta=True, out_dtype=BF16, name="g_w_ssd")
    datt = _mm(dy_att, w["w_attn_branch"], tb=True, out_dtype=BF16, name="d_att")
    grads["w_attn_branch"] = _mm(att, dy_att, ta=True, out_dtype=BF16, name="g_w_att")
    (dz, dpxs, dpb, dpc, ddt, grads["ssd_norm_w"], g_dtb, g_alog, g_dskip), received = _ssd_bwd(
        dyn, y, z, pre, dt_raw, hsave, dtb, alog, dskip_w, w["ssd_norm_w"],
        side=None if exchanges is None else exchanges.early_grads_plan(grads))
    if exchanges is not None:
        exchanges.early_grads_received = received
    grads["ssd_dt_bias"] = _dt_gather(g_dtb)
    grads["ssd_a_log"] = _dt_gather(g_alog)
    grads["ssd_d"] = _dt_gather(g_dskip)
    conv_g = _conv_bwd(dpxs, xbc, w["ssd_conv_w"], name="ssd_conv_bwd_x")
    conv_g = _conv_bwd(dpb, xbc, w["ssd_conv_w"], name="ssd_conv_bwd_b", col0=SSD_INNER, into=conv_g)
    dxbc, grads["ssd_conv_w"], grads["ssd_conv_b"] = _conv_bwd(
        dpc, xbc, w["ssd_conv_w"], name="ssd_conv_bwd_c", col0=SSD_INNER + SSD_BC, into=conv_g)
    dqkv, d_tables, d_sinks = _attn_bwd(datt, qkv, tables, sinks)
    grads["attn_sinks"] = d_sinks[:, :ATT_HEADS]
    dtab = jnp.transpose(d_tables, (1, 0, 2, 3)).reshape(ATT_HEADS, NT_ALL)
    grads["rel_bias"] = _bias_grad(dtab, onehot_t).T
    dsegs = {"z": dz, "xbc": dxbc, "dt": ddt, "qkv": dqkv, "g": dgates}
    grads["in_segs"] = [_mm(u, dsegs[nm], ta=True, out_dtype=BF16, name="g_w_in_" + nm) for nm, _ in _IN_SEGS]
    (dh0, grads["norm_mix_w"]), received = _mm_rms_bwd(
        [(dsegs[nm], segs[nm]) for nm, _ in _IN_SEGS], h0, w["norm_mix_w"], dh1, name="d_u_rms_bwd",
        side=None if exchanges is None else exchanges.w_in_grads_plan(grads["in_segs"]))
    if exchanges is not None:
        exchanges.w_in_grads_received = received[0]
    grads["meta_tokens"] = dh0[PAD:BLOCK]
    return loss_row[0, 0], dh0[BLOCK:], grads


def kernel(x, meta_tokens, norm_mix_w, w_in, ssd_conv_w, ssd_conv_b, ssd_dt_bias, ssd_a_log, ssd_d, ssd_norm_w, w_ssd_branch, w_attn_branch, attn_sinks, rel_bias, gate_b, w_out, norm_ffn_w, w_ffn_in, ffn_conv_w, ffn_conv_b, w_ffn_out, norm_final_w, loss_target, m_meta_tokens, m_norm_mix_w, m_w_in, m_ssd_conv_w, m_ssd_conv_b, m_ssd_dt_bias, m_ssd_a_log, m_ssd_d, m_ssd_norm_w, m_w_ssd_branch, m_w_attn_branch, m_attn_sinks, m_rel_bias, m_gate_b, m_w_out, m_norm_ffn_w, m_w_ffn_in, m_ffn_conv_w, m_ffn_conv_b, m_w_ffn_out, m_norm_final_w, v_meta_tokens, v_norm_mix_w, v_w_in, v_ssd_conv_w, v_ssd_conv_b, v_ssd_dt_bias, v_ssd_a_log, v_ssd_d, v_ssd_norm_w, v_w_ssd_branch, v_w_attn_branch, v_attn_sinks, v_rel_bias, v_gate_b, v_w_out, v_norm_ffn_w, v_w_ffn_in, v_ffn_conv_w, v_ffn_conv_b, v_w_ffn_out, v_norm_final_w):
    shard = dict(meta_tokens=meta_tokens, norm_mix_w=norm_mix_w, w_in=w_in, ssd_conv_w=ssd_conv_w,
                 ssd_conv_b=ssd_conv_b, ssd_dt_bias=ssd_dt_bias, ssd_a_log=ssd_a_log, ssd_d=ssd_d,
                 ssd_norm_w=ssd_norm_w, w_ssd_branch=w_ssd_branch, w_attn_branch=w_attn_branch,
                 attn_sinks=attn_sinks, rel_bias=rel_bias, gate_b=gate_b, w_out=w_out, norm_ffn_w=norm_ffn_w,
                 w_ffn_in=w_ffn_in, ffn_conv_w=ffn_conv_w, ffn_conv_b=ffn_conv_b, w_ffn_out=w_ffn_out,
                 norm_final_w=norm_final_w)
    mom_m = dict(zip(_WEIGHTS, (m_meta_tokens, m_norm_mix_w, m_w_in, m_ssd_conv_w, m_ssd_conv_b, m_ssd_dt_bias,
                                m_ssd_a_log, m_ssd_d, m_ssd_norm_w, m_w_ssd_branch, m_w_attn_branch, m_attn_sinks,
                                m_rel_bias, m_gate_b, m_w_out, m_norm_ffn_w, m_w_ffn_in, m_ffn_conv_w, m_ffn_conv_b,
                                m_w_ffn_out, m_norm_final_w)))
    mom_v = dict(zip(_WEIGHTS, (v_meta_tokens, v_norm_mix_w, v_w_in, v_ssd_conv_w, v_ssd_conv_b, v_ssd_dt_bias,
                                v_ssd_a_log, v_ssd_d, v_ssd_norm_w, v_w_ssd_branch, v_w_attn_branch, v_attn_sinks,
                                v_rel_bias, v_gate_b, v_w_out, v_norm_ffn_w, v_w_ffn_in, v_ffn_conv_w, v_ffn_conv_b,
                                v_w_ffn_out, v_norm_final_w)))
    orig_shape = {k: a.shape for k, a in shard.items()}
    two_d = {k: a.reshape(a.shape[-2:]) if a.ndim >= 2 else a.reshape(1, -1) for k, a in shard.items()}
    shape2 = {k: a.shape for k, a in two_d.items()}

    def as2d(tree):
        return {k: tree[k].reshape(shape2[k]) for k in _WEIGHTS}

    mom_m, mom_v = as2d(mom_m), as2d(mom_v)

    exchanges = _LateExchanges(two_d, shape2)
    row_pack = exchanges.row_pack
    small_pack = _pack_rows([two_d[k] for k in _SMALL_SHARDED], LANES, SMALL_ROW_MULT)
    w_in_all, small_all = _gather_two_level([two_d["w_in"].astype(BF16), small_pack], name="gather_weights")
    full = {k: two_d[k] for k in _SMALL_REPLICATED}
    full["in_segs"] = _w_in_to_segments(w_in_all)
    small_flat = small_all.reshape(N_DEV, -1)
    off = 0
    for k in _SMALL_SHARDED:
        size = int(np.prod(shape2[k]))
        full[k] = _gather_full(small_flat[:, off:off + size].reshape((N_DEV,) + shape2[k]), k, shape2[k])
        off += size

    loss_local, grad_x, grads = _local_step(x[0], loss_target[0], full, exchanges)

    small_names = _SMALL_SHARDED + _SMALL_REPLICATED
    small_send = _pack_rows([grads[k] for k in small_names] + [loss_local.reshape(1)], LANES, SMALL_ROW_MULT)
    small_recv, = _run_plan(_all_to_all_plan([small_send], [False]), "exchange_small_grads")
    in_recv = exchanges.w_in_grads_received
    ffn_recv, rows_recv = exchanges.early_grads_received

    big = {"w_in": _adamw(two_d["w_in"], in_recv, mom_m["w_in"], mom_v["w_in"], name="adamw_w_in"),
           "w_ffn_in": _adamw(two_d["w_ffn_in"], ffn_recv, mom_m["w_ffn_in"], mom_v["w_ffn_in"], name="adamw_w_ffn_in")}
    rows_out = _adamw(row_pack(two_d), rows_recv, row_pack(mom_m), row_pack(mom_v), name="adamw_rows")
    off = 0
    for k in _ROW_SHARDED:
        r = shape2[k][0]
        big[k] = [a[off:off + r] for a in rows_out]
        off += r
    me =4 * lax.axis_index("x") + 2 * lax.axis_index("y") + lax.axis_index("c")
    small_full_shapes = [grads[k].shape for k in small_names]
    n_small = sum(int(np.prod(s)) for s in small_full_shapes)

    def packed_small(tree):
        parts = []
        for k in small_names:
            a = tree[k]
            if k in _SMALL_SHARDED:
                fullw = jnp.zeros(grads[k].shape, F32)
                a = lax.dynamic_update_slice(fullw, a, (0, me * a.shape[1]))
            parts.append(a)
        return _pack_rows(parts + [jnp.zeros((1,), F32)], LANES, SMALL_ROW_MULT)

    g_small, d_small, m_small, v_small = _adamw(packed_small(two_d), small_recv, packed_small(mom_m),
                                                packed_small(mom_v), name="adamw_small")

    def unpack_all(which, small):
        out = {k: big[k][which] for k in _BIG}
        flat = small.reshape(-1)
        for k, a in zip(small_names, _unpack(flat, small_full_shapes)):
            if k in _SMALL_SHARDED:
                a = lax.dynamic_slice(a, (0, me * shape2[k][1]), shape2[k])
            out[k] = a
        return out, flat[n_small]

    g_all, loss = unpack_all(0, g_small)
    d_all, _ = unpack_all(1, d_small)
    m_all, _ = unpack_all(2, m_small)
    v_all, _ = unpack_all(3, v_small)

    def final(tree):
        return [tree[k].reshape(orig_shape[k]) for k in _WEIGHTS]

    return (loss, grad_x[None], *final(g_all), *final(d_all), *final(m_all), *final(v_all))
```

```python
import functools
import math

import numpy as np
import jax
import jax.numpy as jnp
from jax import lax
from jax.experimental import pallas as pl
from jax.experimental.pallas import tpu as pltpu

F32 = jnp.float32
BF16 = jnp.bfloat16
HIGHEST = lax.Precision.HIGHEST

D_MODEL = 1024
N_META = 16
BLOCK = 128
PAD = BLOCK - N_META
EPS = 1e-6
NEG = -1e30
SSD_INNER = 2 * D_MODEL
SSD_HEADDIM = 64
SSD_HEADS = SSD_INNER // SSD_HEADDIM
SSD_GROUPS = 4
SSD_HPG = SSD_HEADS // SSD_GROUPS
SSD_STATE = 128
SSD_CONV = 4
SSD_GW = SSD_HPG * SSD_HEADDIM
SSD_BC = SSD_GROUPS * SSD_STATE
SSD_XBC = SSD_INNER + 2 * SSD_BC
ATT_HEADS = 16
ATT_KV_HEADS = 2
ATT_HEADDIM = 64
ATT_GQ = ATT_HEADS // ATT_KV_HEADS
ATT_Q = ATT_HEADS * ATT_HEADDIM
ATT_KV = ATT_KV_HEADS * ATT_HEADDIM
REL_BUCKETS = 32
REL_MAX_DIST = 128
D_FF = 2816
FFN_CONV = 3
ADAM_LR = 0.001
ADAM_B1 = 0.9
ADAM_B2 = 0.999
ADAM_EPS = 1e-08
ADAM_WD = 0.01
ADAM_STEP = 10

N_DEV = 8
LANES = 128
SUBLANES = 8
BF16_ROWS = 16
DT_W = SSD_GROUPS * LANES
VMEM_LIMIT_BYTES = 56 * 1024 * 1024
MESH = pl.DeviceIdType.MESH

SMALL_ROW_MULT = 16

N_KEYS = 3 * BLOCK
NT_ALL = 3 * N_KEYS * BLOCK
NT_TILE = 8192


def _cparams(*sem):
    return pltpu.CompilerParams(dimension_semantics=sem, vmem_limit_bytes=VMEM_LIMIT_BYTES)


def _row_tile(n, cap):
    best = None
    for t in range(16, min(n, cap) + 1, 16):
        if n % t == 0:
            best = t
    return best or n


def _col_tile(n, cap):
    for t in (1408, 1280, 1024, 768, 640, 512, 384, 256, 128):
        if t <= cap and n % t == 0:
            return t
    return n


def _sigmoid(x):
    return 0.5 * jnp.tanh(0.5 * x) + 0.5


def _silu(x):
    return x * _sigmoid(x)


def _softplus(x):
    return jnp.maximum(x, 0.0) + jnp.log(1.0 + jnp.exp(-jnp.abs(x)))


def _dot_nt(a, b):
    return lax.dot_general(a, b, (((1,), (1,)), ((), ())), preferred_element_type=F32)


def _dot_tn(a, b):
    return lax.dot_general(a, b, (((0,), (0,)), ((), ())), preferred_element_type=F32)


def _dot(a, b):
    return jnp.dot(a, b, preferred_element_type=F32)


def _bf16_terms(x, terms):
    out, rest = [], x
    for _ in range(terms):
        part = rest.astype(BF16)
        out.append(part)
        rest = rest - part.astype(F32)
    return out


def _dot_sel(x, sel, terms=3):
    return sum(_dot(part, sel) for part in _bf16_terms(x, terms))


def _sel_dot(sel, x, terms=3):
    return sum(_dot(sel, part) for part in _bf16_terms(x, terms))


def _sum_all(x):
    return jnp.sum(jnp.sum(x, axis=1, keepdims=True), axis=0, keepdims=True)


MM_ROW_CAPS = (2080, 1664, 832, 416)
MM_COL_CAP = 1408
MM_VMEM_BUDGET = 44 * 1024 * 1024


def _mm_tiles(rows, cols, vmem_bytes):
    col_cands = [t for t in (2048, 1536, 1408, 1280, 1024, 768, 640, 512, 384, 256, 128) if cols % t == 0]
    if cols <= 2 * MM_COL_CAP:
        col_cands.append(cols)
    best = None
    for cap in MM_ROW_CAPS:
        tr = _row_tile(rows, cap)
        for tc in col_cands:
            if vmem_bytes(tr, tc) <= MM_VMEM_BUDGET and (best is None or tr * tc > best[0] * best[1]):
                best = (tr, tc)
    assert best is not None, (rows, cols)
    return best


def _mm(a, b, *, name, ta=False, tb=False, c=None, mask=False, out_dtype=F32):
    if not ta:
        m, k = a.shape
        n = b.shape[0] if tb else b.shape[1]
        tm, tn = _mm_tiles(m, n, lambda t_m, t_n: 2 * (t_m * k * a.dtype.itemsize + k * t_n * b.dtype.itemsize
                                                       + t_m * t_n * (jnp.dtype(out_dtype).itemsize
                                                                      + (0 if c is None else c.dtype.itemsize)))
                           + 4 * t_m * t_n)

        def body(*refs):
            if c is None:
                a_ref, b_ref, o_ref = refs
            else:
                a_ref, b_ref, c_ref, o_ref = refs
            acc = (_dot_nt if tb else _dot)(a_ref[...].astype(BF16), b_ref[...].astype(BF16))
            if mask:
                row = pl.program_id(0) * tm + lax.broadcasted_iota(jnp.int32, (tm, 1), 0)
                acc = jnp.where(row >= PAD, acc, 0.0)
            if c is not None:
                acc = acc + c_ref[...]
            o_ref[...] = acc.astype(out_dtype)

        b_spec = pl.BlockSpec((tn, k), lambda i, j: (j, 0)) if tb else pl.BlockSpec((k, tn), lambda i, j: (0, j))
        in_specs = [pl.BlockSpec((tm, k), lambda i, j: (i, 0)), b_spec]
        args = [a, b]
        if c is not None:
            in_specs.append(pl.BlockSpec((tm, tn), lambda i, j: (i, j)))
            args.append(c)
        return pl.pallas_call(
            body, name=name, grid=(m // tm, n // tn), in_specs=in_specs,
            out_specs=pl.BlockSpec((tm, tn), lambda i, j: (i, j)),
            out_shape=jax.ShapeDtypeStruct((m, n), out_dtype),
            compiler_params=_cparams("parallel", "parallel"))(*args)

    kc, m = a.shape
    n = b.shape[1]
    tm = _col_tile(m, MM_COL_CAP)
    tk, tn = _mm_tiles(kc, n, lambda t_k, t_n: 2 * (t_k * tm * a.dtype.itemsize + t_k * t_n * b.dtype.itemsize
                                                    + tm * t_n * jnp.dtype(out_dtype).itemsize) + 8 * tm * t_n)

    n_k = kc // tk

    def body_t(a_ref, b_ref, o_ref, acc_ref):
        kk = pl.program_id(2)
        bb = b_ref[...]
        if mask:
            row = kk * tk + lax.broadcasted_iota(jnp.int32, (tk, 1), 0)
            bb = jnp.where(row >= PAD, bb, jnp.zeros_like(bb))
        p = _dot_tn(a_ref[...].astype(BF16), bb.astype(BF16))

        @pl.when(kk == 0)
        def _():
            acc_ref[...] = p

        @pl.when(kk > 0)
        def _():
            acc_ref[...] += p

        @pl.when(kk == n_k - 1)
        def _():
            o_ref[...] = acc_ref[...].astype(out_dtype)

    return pl.pallas_call(
        body_t, name=name, grid=(m // tm, n // tn, n_k),
        in_specs=[pl.BlockSpec((tk, tm), lambda i, j, kk: (kk, i)), pl.BlockSpec((tk, tn), lambda i, j, kk: (kk, j))],
        out_specs=pl.BlockSpec((tm, tn), lambda i, j, kk: (i, j)),
        out_shape=jax.ShapeDtypeStruct((m, n), out_dtype),
        scratch_shapes=[pltpu.VMEM((tm, tn), F32)],
        compiler_params=_cparams("parallel", "parallel", "arbitrary"))(a, b)


def _mm_rms_bwd(pairs, x, w, dres, *, name, side=None):
    m, d = x.shape
    tm = _row_tile(m, 416)
    n_pairs = len(pairs)

    def body(*refs):
        a_refs, b_refs = refs[:n_pairs], refs[n_pairs:2 * n_pairs]
        x_ref, w_ref, dres_ref, dx_ref, dw_ref = refs[2 * n_pairs:]
        i = pl.program_id(0)
        dyv = None
        for a_ref, b_ref in zip(a_refs, b_refs):
            term = _dot_nt(a_ref[...].astype(BF16), b_ref[...])
            dyv = term if dyv is None else dyv + term
        xv = x_ref[...]
        r = lax.rsqrt(jnp.mean(xv * xv, axis=-1, keepdims=True) + EPS)
        xh = xv * r
        g = dyv * w_ref[...]
        dx_ref[...] = r * (g - xh * jnp.mean(g * xh, axis=-1, keepdims=True)) + dres_ref[...]
        part = jnp.sum(dyv * xh, axis=0, keepdims=True)

        @pl.when(i == 0)
        def _():
            dw_ref[...] = part

        @pl.when(i > 0)
        def _():
            dw_ref[...] += part

    row = pl.BlockSpec((tm, d), lambda i: (i, 0))
    vec = pl.BlockSpec((1, d), lambda i: (0, 0))
    in_specs = ([pl.BlockSpec((tm, a.shape[1]), lambda i: (i, 0)) for a, _ in pairs]
                + [pl.BlockSpec(b.shape, lambda i: (0, 0), pipeline_mode=pl.Buffered(1)) for _, b in pairs]
                + [row, vec, row])
    return _call_with_side(
        body, side, name=name, grid=(m // tm,), in_specs=in_specs, out_specs=[row, vec],
        out_shape=[jax.ShapeDtypeStruct((m, d), F32), jax.ShapeDtypeStruct((1, d), F32)], scratch_shapes=[],
        args=[a for a, _ in pairs] + [b for _, b in pairs] + [x, w, dres], semantics=("arbitrary",))


def _rms_fwd(h, w, *, name):
    n, d = h.shape
    tm = _row_tile(n, 832)

    def body(h_ref, w_ref, o_ref):
        x = h_ref[...]
        r = lax.rsqrt(jnp.mean(x * x, axis=-1, keepdims=True) + EPS)
        o_ref[...] = (x * r * w_ref[...]).astype(BF16)

    return pl.pallas_call(
        body, name=name, grid=(n // tm,),
        in_specs=[pl.BlockSpec((tm, d), lambda i: (i, 0)), pl.BlockSpec((1, d), lambda i: (0, 0))],
        out_specs=pl.BlockSpec((tm, d), lambda i: (i, 0)),
        out_shape=jax.ShapeDtypeStruct((n, d), BF16),
        compiler_params=_cparams("parallel"))(h, w)


def _final_loss(h, w, target):
    n, d = h.shape
    nb = n // BLOCK

    def body(h_ref, w_ref, t_ref, dh_ref, dhb_ref, loss_ref, dw_ref):
        i = pl.program_id(0)
        xv = h_ref[...]
        r = lax.rsqrt(jnp.mean(xv * xv, axis=-1, keepdims=True) + EPS)
        xh = xv * r
        wv = w_ref[...]
        err = jnp.where(i >= 1, xh * wv - t_ref[...], 0.0)
        dyv = err * (1.0 / d)
        g = dyv * wv
        dh = r * (g - xh * jnp.mean(g * xh, axis=-1, keepdims=True))
        dh_ref[...] = dh
        dhb_ref[...] = dh.astype(BF16)
        lpart = jnp.broadcast_to(0.5 * _sum_all(err * err) * (1.0 / d), (1, LANES))
        wpart = jnp.sum(dyv * xh, axis=0, keepdims=True)

        @pl.when(i == 0)
        def _():
            loss_ref[...] = lpart
            dw_ref[...] = wpart

        @pl.when(i > 0)
        def _():
            loss_ref[...] += lpart
            dw_ref[...] += wpart

    row = pl.BlockSpec((BLOCK, d), lambda i: (i, 0))
    vec = pl.BlockSpec((1, d), lambda i: (0, 0))
    return pl.pallas_call(
        body, name="final_loss", grid=(nb,),
        in_specs=[row, vec, pl.BlockSpec((BLOCK, d), lambda i: (jnp.maximum(i - 1, 0), 0))],
        out_specs=[row, row, pl.BlockSpec((1, LANES), lambda i: (0, 0)), vec],
        out_shape=[jax.ShapeDtypeStruct((n, d), F32), jax.ShapeDtypeStruct((n, d), BF16),
                   jax.ShapeDtypeStruct((1, LANES), F32), jax.ShapeDtypeStruct((1, d), F32)],
        compiler_params=_cparams("arbitrary"))(h, w, target)


def _main_spec(tm, cb, off=0):
    return pl.BlockSpec((tm, cb), lambda j, i: (i, j + off))


def _prev_spec(tm, cb, off=0):
    r8 = tm // SUBLANES
    return pl.BlockSpec((SUBLANES, cb), lambda j, i: (jnp.maximum(i * r8 - 1, 0), j + off))


def _next_spec(tm, cb, n_rows, off=0):
    r8 = tm // SUBLANES
    last = n_rows // SUBLANES - 1
    return pl.BlockSpec((SUBLANES, cb), lambda j, i: (jnp.minimum((i + 1) * r8, last), j + off))


def _with_prev(prev_ref, main_ref, i):
    prev = jnp.where(i > 0, prev_ref[...], 0.0)
    return jnp.concatenate([prev, main_ref[...]], axis=0)


def _with_next(main, nxt, i, n_tiles):
    return jnp.concatenate([main, jnp.where(i < n_tiles - 1, nxt, 0.0)], axis=0)


def _back(xx, s, tm):
    if s == 0:
        return xx[SUBLANES:SUBLANES + tm]
    return pltpu.roll(xx, s, 0)[SUBLANES:SUBLANES + tm]


def _ahead(xx, s, tm):
    if s == 0:
        return xx[:tm]
    return pltpu.roll(xx, xx.shape[0] - s, 0)[:tm]


def _mm_conv_fwd(u, w_in, w, b, *, name):
    n = u.shape[0]
    cdim = w_in.shape[1]
    kw = w.shape[0]
    tm = _row_tile(n, 832)
    cb = _col_tile(cdim, 512)
    nt = n // tm

    def body(u_ref, w_in_ref, w_ref, b_ref, x_ref, o_ref, acc_scr, halo_scr):
        j, i = pl.program_id(0), pl.program_id(1)

        @pl.when((j == 0) & (i == 0))
        def _():
            acc_scr[...] = jnp.zeros_like(acc_scr)
            halo_scr[...] = jnp.zeros_like(halo_scr)

        new = _dot(u_ref[...], w_in_ref[...])
        prev = acc_scr[...]
        xx = jnp.concatenate([jnp.where(i >= 2, halo_scr[...], 0.0), prev], axis=0)
        acc = jnp.broadcast_to(b_ref[...], (tm, cb))
        for k in range(kw):
            acc = acc + w_ref[k:k + 1, :] * _back(xx, kw - 1 - k, tm)
        x_ref[...] = prev.astype(BF16)
        o_ref[...] = acc
        halo_scr[...] = prev[tm - SUBLANES:, :]
        acc_scr[...] = new

    out = pl.BlockSpec((tm, cb), lambda j, i: (jnp.maximum(i - 1, 0), j))
    shp = jax.ShapeDtypeStruct((n, cdim), F32)
    return pl.pallas_call(
        body, name=name, grid=(cdim // cb, nt + 1),
        in_specs=[pl.BlockSpec((tm, u.shape[1]), lambda j, i: (jnp.minimum(i, nt - 1), 0)),
                  pl.BlockSpec((w_in.shape[0], cb), lambda j, i: (0, j)),
                  pl.BlockSpec((kw, cb), lambda j, i: (0, j)), pl.BlockSpec((1, cb), lambda j, i: (0, j))],
        out_specs=[out, out], out_shape=[jax.ShapeDtypeStruct((n, cdim), BF16), shp],
        scratch_shapes=[pltpu.VMEM((tm, cb), F32), pltpu.VMEM((SUBLANES, cb), F32)],
        compiler_params=_cparams("arbitrary", "arbitrary"))(u, w_in, w, b)


def _conv_bwd_core(dpre_ext, x, w_ref, kw, tm):
    dx = None
    dws = []
    for k in range(kw):
        shifted = _ahead(dpre_ext, kw - 1 - k, tm)
        term = w_ref[k:k + 1, :] * shifted
        dx = term if dx is None else dx + term
        dws.append(jnp.sum(shifted * x, axis=0, keepdims=True))
    return dx, dws, jnp.sum(dpre_ext[:tm], axis=0, keepdims=True)


def _acc_rows(i, dw_ref, db_ref, dws, db):
    @pl.when(i == 0)
    def _():
        for k, v in enumerate(dws):
            dw_ref[k:k + 1, :] = v
        db_ref[...] = db

    @pl.when(i > 0)
    def _():
        for k, v in enumerate(dws):
            dw_ref[k:k + 1, :] += v
        db_ref[...] += db


def _conv_bwd(dpre, x, w, *, name, col0=0, into=None):
    n, cdim = x.shape
    kw = w.shape[0]
    tm = _row_tile(n, 832)
    cb = _col_tile(cdim, 512)
    nt = n // tm
    off = col0 // cb
    n_alias = 0 if into is None else 3

    def body(d_ref, dn_ref, x_ref, w_ref, *rest):
        dx_ref, dw_ref, db_ref = rest[n_alias:]
        i = pl.program_id(1)
        dpre_ext = _with_next(d_ref[...], dn_ref[...], i, nt)
        dx, dws, db = _conv_bwd_core(dpre_ext, x_ref[...], w_ref, kw, tm)
        dx_ref[...] = dx.astype(BF16)
        _acc_rows(i, dw_ref, db_ref, dws, db)

    wspec = pl.BlockSpec((kw, cb), lambda j, i: (0, j + off))
    bspec = pl.BlockSpec((1, cb), lambda j, i: (0, j + off))
    return pl.pallas_call(
        body, name=name, grid=(dpre.shape[1] // cb, nt),
        in_specs=[_main_spec(tm, cb), _next_spec(tm, cb, n), _main_spec(tm, cb, off), wspec]
        + [pl.BlockSpec(memory_space=pl.ANY)] * n_alias,
        out_specs=[_main_spec(tm, cb, off), wspec, bspec],
        out_shape=[jax.ShapeDtypeStruct((n, cdim), BF16), jax.ShapeDtypeStruct((kw, cdim), F32),
                   jax.ShapeDtypeStruct((1, cdim), F32)],
        input_output_aliases={4 + k: k for k in range(n_alias)},
        compiler_params=_cparams("parallel", "arbitrary"))(dpre, dpre, x, w, *(into or ()))


def _ffn_in_act_fwd(u, w_in, w, b):
    n = u.shape[0]
    kw = w.shape[0]
    tm = _row_tile(n, 832)
    cb = _col_tile(D_FF, 256)
    nc = D_FF // cb
    nt = n // tm

    def body(u_ref, wu_in_ref, wg_in_ref, wu_ref, wg_ref, bu_ref, bg_ref,
             xu_ref, xg_ref, hu_ref, hg_ref, act_ref, acc_scr, halo_scr):
        j, i = pl.program_id(0), pl.program_id(1)

        @pl.when((j == 0) & (i == 0))
        def _():
            acc_scr[...] = jnp.zeros_like(acc_scr)
            halo_scr[...] = jnp.zeros_like(halo_scr)

        ub = u_ref[...]
        new = [_dot(ub, wu_in_ref[...]), _dot(ub, wg_in_ref[...])]
        hid = []
        for half, (x_ref, w_ref, b_ref) in enumerate(((xu_ref, wu_ref, bu_ref), (xg_ref, wg_ref, bg_ref))):
            prev = acc_scr[half]
            xx = jnp.concatenate([jnp.where(i >= 2, halo_scr[half], 0.0), prev], axis=0)
            acc = jnp.broadcast_to(b_ref[...], (tm, cb))
            for k in range(kw):
                acc = acc + w_ref[k:k + 1, :] * _back(xx, kw - 1 - k, tm)
            x_ref[...] = prev.astype(BF16)
            hid.append(acc)
            halo_scr[half] = prev[tm - SUBLANES:, :]
            acc_scr[half] = new[half]
        hu_ref[...] = hid[0].astype(BF16)
        hg_ref[...] = hid[1].astype(BF16)
        act_ref[...] = (_silu(hid[1]) * hid[0]).astype(BF16)

    def wspec(off):
        return pl.BlockSpec((kw, cb), lambda j, i: (0, j + off))

    def bspec(off):
        return pl.BlockSpec((1, cb), lambda j, i: (0, j + off))

    def in_w(off):
        return pl.BlockSpec((w_in.shape[0], cb), lambda j, i: (0, j + off))

    out = pl.BlockSpec((tm, cb), lambda j, i: (jnp.maximum(i - 1, 0), j))
    bf16_out = jax.ShapeDtypeStruct((n, D_FF), BF16)
    return pl.pallas_call(
        body, name="ffn_in_act_fwd", grid=(nc, nt + 1),
        in_specs=[pl.BlockSpec((tm, u.shape[1]), lambda j, i: (jnp.minimum(i, nt - 1), 0)), in_w(0), in_w(nc),
                  wspec(0), wspec(nc), bspec(0), bspec(nc)],
        out_specs=[out] * 5,
        out_shape=[bf16_out] * 5,
        scratch_shapes=[pltpu.VMEM((2, tm, cb), F32), pltpu.VMEM((2, SUBLANES, cb), F32)],
        compiler_params=_cparams("arbitrary", "arbitrary"))(u, w_in, w_in, w, w, b, b)


def _ffn_out_act_bwd(dh, w_out, hu, hg, x_up, x_gate, w):
    n = x_up.shape[0]
    kw = w.shape[0]
    tm = _row_tile(n, 832)
    cb = _col_tile(D_FF, 256)
    nc = D_FF // cb
    nt = n // tm

    def body(dh_ref, wo_ref, hu_ref, hun_ref, hg_ref, hgn_ref, xu_ref, xg_ref, wu_ref, wg_ref,
             dxu_ref, dxg_ref, dwu_ref, dwg_ref, dbu_ref, dbg_ref, acc_scr, halo_scr):
        j, i = pl.program_id(0), pl.program_id(1)

        @pl.when((j == 0) & (i == 0))
        def _():
            acc_scr[...] = jnp.zeros_like(acc_scr)
            halo_scr[...] = jnp.zeros_like(halo_scr)

        tile = jnp.maximum(nt - 1 - i, 0)
        row = tile * tm + lax.broadcasted_iota(jnp.int32, (tm, 1), 0)
        new = jnp.where(row >= PAD, _dot_nt(dh_ref[...].astype(BF16), wo_ref[...]), 0.0)
        prev = jnp.where(i >= 1, acc_scr[...], 0.0)
        dact_e = jnp.concatenate([prev, jnp.where(i >= 2, halo_scr[...], 0.0)], axis=0)
        last = nt - i >= nt - 1
        up_e = jnp.concatenate([hu_ref[...].astype(F32), jnp.where(last, 0.0, hun_ref[...].astype(F32))], axis=0)
        gate_e = jnp.concatenate([hg_ref[...].astype(F32), jnp.where(last, 0.0, hgn_ref[...].astype(F32))], axis=0)
        halo_scr[...] = prev[:BF16_ROWS, :]
        acc_scr[...] = new
        sg = _sigmoid(gate_e)
        dup_e = dact_e * (gate_e * sg)
        dgate_e = dact_e * up_e * (sg * (1.0 + gate_e * (1.0 - sg)))
        dx, dws, db = _conv_bwd_core(dup_e, xu_ref[...], wu_ref, kw, tm)
        dxu_ref[...] = dx.astype(BF16)
        _acc_rows(i, dwu_ref, dbu_ref, dws, db)
        dx, dws, db = _conv_bwd_core(dgate_e, xg_ref[...], wg_ref, kw, tm)
        dxg_ref[...] = dx.astype(BF16)
        _acc_rows(i, dwg_ref, dbg_ref, dws, db)

    def done_tile(i):
        return jnp.minimum(nt - i, nt - 1)

    halo_blocks = tm // BF16_ROWS
    main = pl.BlockSpec((tm, cb), lambda j, i: (done_tile(i), j))
    nxt = pl.BlockSpec((BF16_ROWS, cb),
                       lambda j, i: (jnp.minimum((done_tile(i) + 1) * halo_blocks, n // BF16_ROWS - 1), j))
    wspec0 = pl.BlockSpec((kw, cb), lambda j, i: (0, j))
    wspec1 = pl.BlockSpec((kw, cb), lambda j, i: (0, j + nc))
    bspec = pl.BlockSpec((1, cb), lambda j, i: (0, j))
    return pl.pallas_call(
        body, name="ffn_out_act_bwd", grid=(nc, nt + 1),
        in_specs=[pl.BlockSpec((tm, dh.shape[1]), lambda j, i: (jnp.maximum(nt - 1 - i, 0), 0)),
                  pl.BlockSpec((cb, w_out.shape[1]), lambda j, i: (j, 0)),
                  main, nxt, main, nxt, main, main, wspec0, wspec1],
        out_specs=[main, main, wspec0, wspec0, bspec, bspec],
        out_shape=[jax.ShapeDtypeStruct((n, D_FF), BF16), jax.ShapeDtypeStruct((n, D_FF), BF16),
                   jax.ShapeDtypeStruct((kw, D_FF), F32), jax.ShapeDtypeStruct((kw, D_FF), F32),
                   jax.ShapeDtypeStruct((1, D_FF), F32), jax.ShapeDtypeStruct((1, D_FF), F32)],
        scratch_shapes=[pltpu.VMEM((tm, cb), F32), pltpu.VMEM((BF16_ROWS, cb), F32)],
        compiler_params=_cparams("arbitrary", "arbitrary"))(dh, w_out, hu, hu, hg, hg, x_up, x_gate, w, w)


def _ssd_prep(pxs_ref, pb_ref, pc_ref, dtr_ref, dtb_ref, alog_ref, c):
    xs = _silu(pxs_ref[...])
    bm = _silu(pb_ref[...])
    cm = _silu(pc_ref[...])
    return (xs, bm, cm) + _ssd_decay(dtr_ref, dtb_ref, alog_ref, c)


def _ssd_decay(dtr_ref, dtb_ref, alog_ref, c):
    row =lax.broadcasted_iota(jnp.int32, (BLOCK, 1), 0) + c * BLOCK
    valid = (row >= PAD).astype(F32)
    dtr = dtr_ref[...] + dtb_ref[...]
    dt = _softplus(dtr) * valid
    a = -jnp.exp(alog_ref[...])
    lam = dt * a
    ri = lax.broadcasted_iota(jnp.int32, (BLOCK, BLOCK), 0)
    ci = lax.broadcasted_iota(jnp.int32, (BLOCK, BLOCK), 1)
    causal = ci <= ri
    cs = _sel_dot(causal.astype(BF16), lam)
    return valid, dtr, dt, a, lam, cs, causal


def _head_cols(r):
    return slice(SSD_HEADDIM * r, SSD_HEADDIM * (r + 1))


def _ssd_specs(nc, rev):
    def cidx(c):
        return nc - 1 - c if rev else c

    xs = pl.BlockSpec((BLOCK, SSD_GW), lambda g, c: (cidx(c), g))
    bspec = pl.BlockSpec((BLOCK, SSD_STATE), lambda g, c: (cidx(c), SSD_INNER // SSD_STATE + g))
    cspec = pl.BlockSpec((BLOCK, SSD_STATE), lambda g, c: (cidx(c), (SSD_INNER + SSD_BC) // SSD_STATE + g))
    lane = pl.BlockSpec((BLOCK, LANES), lambda g, c: (cidx(c), g))
    vec = pl.BlockSpec((1, LANES), lambda g, c: (0, g))
    wide_vec = pl.BlockSpec((1, SSD_GW), lambda g, c: (0, g))
    hsave = pl.BlockSpec((1, 1, SSD_GW, SSD_STATE), lambda g, c: (cidx(c), g, 0, 0))
    return xs, bspec, cspec, lane, vec, wide_vec, hsave


def _head_spread_matrix():
    r = lax.broadcasted_iota(jnp.int32, (LANES, SSD_GW), 0)
    col = lax.broadcasted_iota(jnp.int32, (LANES, SSD_GW), 1)
    return (col // SSD_HEADDIM == r).astype(BF16)


def _const_spec(shape):
    return pl.BlockSpec(shape, lambda g, c: (0,) * len(shape))


def _spread_heads(per_head, e_ref):
    wide = _dot_sel(jnp.concatenate(per_head, axis=0), e_ref[...])
    return [wide[BLOCK * k:BLOCK * (k + 1)] for k in range(len(per_head))]


def _call_with_side(body, side, *, name, grid, in_specs, out_specs, out_shape, scratch_shapes, args,
                    semantics=("parallel", "arbitrary")):
    if side is None:
        outs = pl.pallas_call(body, name=name, grid=grid, in_specs=in_specs, out_specs=out_specs, out_shape=out_shape,
                              scratch_shapes=scratch_shapes, compiler_params=_cparams(*semantics))(*args)
        return outs, []
    n_in, n_out, n_scr, n_side = len(in_specs), len(out_specs), len(scratch_shapes), len(side.arrays)

    def body_with_side(*refs):
        ins, rest = refs[:n_in + n_side], refs[n_in + n_side:]
        outs, scratch = rest[:n_out + n_side], rest[n_out + n_side:]
        side_refs = (ins[n_in:], outs[n_out:], scratch[n_scr:])
        ids = [pl.program_id(k) for k in range(len(grid))]
        inner_first = functools.reduce(jnp.logical_and, [i == 0 for i in ids[1:]], True)

        @pl.when((ids[0] == 0) & inner_first)
        def _():
            side.phases[0](*side_refs)

        body(*ins[:n_in], *outs[:n_out], *scratch[:n_scr])

        @pl.when((ids[0] == grid[0] // 2) & inner_first)
        def _():
            side.phases[1](*side_refs)

        @pl.when(functools.reduce(jnp.logical_and, [i == n - 1 for i, n in zip(ids, grid)]))
        def _():
            side.phases[2](*side_refs)

    any_spec = pl.BlockSpec(memory_space=pl.ANY)
    outs = pl.pallas_call(
        body_with_side, name=name, grid=grid, in_specs=list(in_specs) + [any_spec] * n_side,
        out_specs=list(out_specs) + [any_spec] * n_side, out_shape=list(out_shape) + list(side.out_shape),
        scratch_shapes=list(scratch_shapes) + list(side.scratch_shapes),
        compiler_params=_cparams(*["arbitrary"] * len(grid)))(*args, *side.arrays)
    return outs[:n_out], outs[n_out:]


def _ssd_fwd(pre, dt_raw, z, dtb, alog, dskip_w, norm_w, side=None):
    n = pre.shape[0]
    nc = n // BLOCK
    xs_s, b_s, c_s, lane_s, vec_s, wide_s, hs_s = _ssd_specs(nc, False)

    def body(pxs_ref, pb_ref, pc_ref, dtr_ref, z_ref, dtb_ref, alog_ref, dskw_ref, nw_ref, e_ref,
             y_ref, yn_ref, hs_ref, h_scr):
        c = pl.program_id(1)

        @pl.when(c == 0)
        def _():
            h_scr[...] = jnp.zeros_like(h_scr)

        xs, bm, cm, _, _, dt, _, _, cs, causal = _ssd_prep(pxs_ref, pb_ref, pc_ref, dtr_ref, dtb_ref, alog_ref, c)
        cst = cs.T
        cs_last = cs[BLOCK - 1:BLOCK, :]
        dt_w, ecs_w, dec_w = _spread_heads([dt, jnp.exp(cs), jnp.exp(cs_last - cs)], e_ref)
        xdt = xs * dt_w
        bmb = bm.astype(BF16)
        cmb = cm.astype(BF16)
        cb = _dot_nt(cmb, bmb)
        hg = h_scr[...]
        hs_ref[0, 0] = hg
        y = _dot_nt(cmb, hg.astype(BF16)) * ecs_w + dskw_ref[...] * xs
        first = lax.broadcasted_iota(jnp.int32, (BLOCK, LANES), 1) < SSD_HEADDIM
        diag = []
        for j in range(SSD_HPG // 2):
            xp = xdt[:, LANES * j:LANES * (j + 1)].astype(BF16)
            res = []
            for r in (2 * j, 2 * j + 1):
                lm = jnp.exp(jnp.where(causal, cs[:, r:r + 1] - cst[r:r + 1, :], NEG))
                res.append(_dot((cb * lm).astype(BF16), xp))
            diag.append(jnp.where(first, res[0], res[1]))
        y = y + jnp.concatenate(diag, axis=1)
        st = _dot_tn((xdt * dec_w).astype(BF16), bmb)
        eh = jnp.exp(cs_last)
        for r in range(SSD_HPG):
            rows = _head_cols(r)
            h_scr[rows, :] = hg[rows, :] * eh[:, r:r + 1] + st[rows, :]
        y_ref[...] = y
        gts = y * _silu(z_ref[...])
        rr = lax.rsqrt(jnp.mean(gts * gts, axis=-1, keepdims=True) + EPS)
        yn_ref[...] = (gts * rr * nw_ref[...]).astype(BF16)

    return _call_with_side(
        body, side, name="ssd_fwd", grid=(SSD_GROUPS, nc),
        in_specs=[xs_s, b_s, c_s, lane_s, xs_s, vec_s, vec_s, wide_s, wide_s, _const_spec((LANES, SSD_GW))],
        out_specs=[xs_s, xs_s, hs_s],
        out_shape=[jax.ShapeDtypeStruct((n, SSD_INNER), F32), jax.ShapeDtypeStruct((n, SSD_INNER), BF16),
                   jax.ShapeDtypeStruct((nc, SSD_GROUPS, SSD_GW, SSD_STATE), F32)],
        scratch_shapes=[pltpu.VMEM((SSD_GW, SSD_STATE), F32)],
        args=(pre, pre, pre, dt_raw, z, dtb, alog, dskip_w, norm_w, _head_spread_matrix()))


def _lane_put(acc, col, r):
    lane = lax.broadcasted_iota(jnp.int32, acc.shape, 1)
    return jnp.where(lane == r, col, acc)


def _ssd_bwd(dyn, y, z, pre, dt_raw, hsave, dtb, alog, dskip_w, norm_w, side=None):
    n = pre.shape[0]
    nc = n // BLOCK
    spread = _head_spread_matrix()
    xs_s, b_s, c_s, lane_s, vec_s, wide_s, hs_s = _ssd_specs(nc, True)
    bc_out =pl.BlockSpec((BLOCK, SSD_STATE), lambda g, c: (nc - 1 - c, g))

    def body(dyn_ref, y_ref, z_ref, pxs_ref, pb_ref, pc_ref, dtr_ref, hs_ref, dtb_ref, alog_ref, dskw_ref, nw_ref,
             e_ref, r_ref,
             dz_ref, dxs_ref, dbm_ref, dcm_ref, ddt_ref, dnw_ref, ddtb_ref, dalog_ref, ddsk_ref, g_scr):
        step = pl.program_id(1)
        c = nc - 1 - step

        @pl.when(step == 0)
        def _():
            g_scr[...] = jnp.zeros_like(g_scr)

        pxs, pb, pc = pxs_ref[...], pb_ref[...], pc_ref[...]
        sx, sb, sc = _sigmoid(pxs), _sigmoid(pb), _sigmoid(pc)
        xs, bm, cm = pxs * sx, pb * sb, pc * sc
        valid, dtr, dt, a, lam, cs, causal = _ssd_decay(dtr_ref, dtb_ref, alog_ref, c)
        cst = cs.T
        cs_last = cs[BLOCK - 1:BLOCK, :]
        bmb = bm.astype(BF16)
        cmb = cm.astype(BF16)
        cb = _dot_nt(cmb, bmb)
        hg = hs_ref[0, 0]
        hgb = hg.astype(BF16)
        yoff = _dot_nt(cmb, hgb)
        gn = g_scr[...]
        gnb = gn.astype(BF16)

        zv = z_ref[...]
        yv = y_ref[...]
        sgz = _sigmoid(zv)
        sz = zv * sgz
        gts = yv * sz
        rr = lax.rsqrt(jnp.mean(gts * gts, axis=-1, keepdims=True) + EPS)
        xh = gts * rr
        dynv = dyn_ref[...]
        gg = dynv * nw_ref[...]
        dgts = rr * (gg - xh * jnp.mean(gg * xh, axis=-1, keepdims=True))
        dnw = jnp.sum(dynv * xh, axis=0, keepdims=True)
        dy = dgts * sz
        dz_ref[...] = (dgts * yv * (sgz * (1.0 + zv * (1.0 - sgz)))).astype(BF16)

        ecs = jnp.exp(cs)
        dec = jnp.exp(cs_last - cs)
        eh = jnp.exp(cs_last)
        dt_w, ecs_w, dec_w = _spread_heads([dt, ecs, dec], e_ref)
        red_m = r_ref[...]

        def head_sums(v):
            return _dot_sel(v, red_m, terms=2)

        xdt = xs * dt_w
        q_all = _dot_nt(bmb, gnb)
        w_all = (dy * ecs_w).astype(BF16)
        e_hl = head_sums(q_all * xdt) * dec
        dcs_col = head_sums(dy * yoff) * ecs - e_hl
        gh = jnp.zeros((1, LANES), F32)
        prod = gn * hg
        for r in range(SSD_HPG):
            gh = _lane_put(gh, _sum_all(prod[_head_cols(r), :]), r)
        dcs_last = jnp.sum(e_hl, axis=0, keepdims=True) + eh * gh
        ddsk = jnp.sum(head_sums(dy * xs), axis=0, keepdims=True)
        cbt = _dot_nt(bmb, cmb)
        lane = lax.broadcasted_iota(jnp.int32, (BLOCK, LANES), 1)
        first = lane < SSD_HEADDIM
        causal_t = lax.broadcasted_iota(jnp.int32, (BLOCK, BLOCK), 1) >= lax.broadcasted_iota(
            jnp.int32, (BLOCK, BLOCK), 0)
        sub = lax.broadcasted_iota(jnp.int32, (SUBLANES, BLOCK), 0)
        dcs_row = jnp.zeros((SUBLANES, BLOCK), F32)
        dcb = jnp.zeros((BLOCK, BLOCK), F32)
        dxdt_pairs = []
        for j in range(SSD_HPG // 2):
            tile = slice(LANES * j, LANES * (j + 1))
            dy_p = dy[:, tile]
            dyb = dy_p.astype(BF16)
            xdtb = xdt[:, tile].astype(BF16)
            res = []
            for half, r in enumerate((2 * j, 2 * j + 1)):
                csc, csr = cs[:, r:r + 1], cst[r:r + 1, :]
                lm = jnp.exp(jnp.where(causal, csc - csr, NEG))
                lmt = jnp.exp(jnp.where(causal_t, csr - csc, NEG))
                keep = first if half == 0 else jnp.logical_not(first)
                gm = _dot_nt(jnp.where(keep, dy_p, 0.0).astype(BF16), xdtb) * lm
                dcb = dcb + gm
                mm_ = gm * cb
                dcs_col = dcs_col + jnp.where(lane == r, jnp.sum(mm_, axis=1, keepdims=True), 0.0)
                dcs_row = jnp.where(sub == r, jnp.sum(mm_, axis=0, keepdims=True), dcs_row)
                res.append(_dot((cbt * lmt).astype(BF16), dyb))
            dxdt_pairs.append(jnp.where(first, res[0], res[1]))
        dxdt = jnp.concatenate(dxdt_pairs, axis=1) + q_all * dec_w
        ddt_x = head_sums(dxdt * xs)
        dxs = dxdt * dt_w + dskw_ref[...] * dy
        dcbb = dcb.astype(BF16)
        dcm = _dot(w_all, hgb) + _dot(dcbb, bmb)
        dbm = _dot((xdt * dec_w).astype(BF16), gnb) + _dot_tn(dcbb, cmb)
        dh_off = _dot_tn(w_all, cmb)
        for r in range(SSD_HPG):
            rows = _head_cols(r)
            g_scr[rows, :] = gn[rows, :] * eh[:, r:r + 1] + dh_off[rows, :]

        pad_rows = jnp.zeros((BLOCK - SUBLANES, BLOCK), F32)
        dcs = dcs_col - jnp.concatenate([dcs_row, pad_rows], axis=0).T
        rsel = lax.broadcasted_iota(jnp.int32, (BLOCK, LANES), 0)
        dcs = dcs + jnp.where(rsel == BLOCK - 1, dcs_last, 0.0)
        ri = lax.broadcasted_iota(jnp.int32, (BLOCK, BLOCK), 0)
        ci = lax.broadcasted_iota(jnp.int32, (BLOCK, BLOCK), 1)
        dlam = _sel_dot((ci >= ri).astype(BF16), dcs)
        head = lane < SSD_HPG
        ddt = dlam * a + ddt_x
        ddtr = jnp.where(head, ddt * _sigmoid(dtr) * valid, 0.0)
        ddt_ref[...] = ddtr.astype(BF16)
        dalog = jnp.sum(jnp.where(head, dlam * lam, 0.0), axis=0, keepdims=True)
        ddtb = jnp.sum(ddtr, axis=0, keepdims=True)

        dxs_ref[...] = dxs * (sx * (1.0 + pxs * (1.0 - sx)))
        dbm_ref[...] = dbm * (sb * (1.0 + pb * (1.0 - sb)))
        dcm_ref[...] = dcm * (sc * (1.0 + pc * (1.0 - sc)))

        @pl.when(step == 0)
        def _():
            dnw_ref[...] = dnw
            ddtb_ref[...] = ddtb
            dalog_ref[...] = dalog
            ddsk_ref[...] = ddsk

        @pl.when(step > 0)
        def _():
            dnw_ref[...] += dnw
            ddtb_ref[...] += ddtb
            dalog_ref[...] += dalog
            ddsk_ref[...] += ddsk

    return _call_with_side(
        body, side, name="ssd_bwd", grid=(SSD_GROUPS, nc),
        in_specs=[xs_s, xs_s, xs_s, xs_s, b_s, c_s, lane_s, hs_s, vec_s, vec_s, wide_s, wide_s,
                  _const_spec((LANES, SSD_GW)), _const_spec((SSD_GW, LANES))],
        out_specs=[xs_s, xs_s, bc_out, bc_out, lane_s, wide_s, vec_s, vec_s, vec_s],
        out_shape=[jax.ShapeDtypeStruct((n, SSD_INNER), BF16), jax.ShapeDtypeStruct((n, SSD_INNER), F32),
                   jax.ShapeDtypeStruct((n, SSD_BC), F32), jax.ShapeDtypeStruct((n, SSD_BC), F32),
                   jax.ShapeDtypeStruct((n, DT_W), BF16), jax.ShapeDtypeStruct((1, SSD_INNER), F32),
                   jax.ShapeDtypeStruct((1, DT_W), F32), jax.ShapeDtypeStruct((1, DT_W), F32),
                   jax.ShapeDtypeStruct((1, DT_W), F32)],
        scratch_shapes=[pltpu.VMEM((SSD_GW, SSD_STATE), F32)],
        args=(dyn, y, z, pre, pre, pre, dt_raw, hsave, dtb, alog, dskip_w, norm_w, spread, spread.T))


def _bucket_table():
    def bucket(dist):
        d = np.maximum(dist, 0)
        half = REL_BUCKETS // 2
        big = half + (np.log(np.maximum(d, half).astype(np.float32) / np.float32(half))
                      / np.float32(math.log(REL_MAX_DIST / half)) * np.float32(REL_BUCKETS - half)).astype(np.int32)
        return np.where(d < half, d, np.minimum(big, REL_BUCKETS - 1)).astype(np.int32)

    l = np.arange(BLOCK)[None, :]
    band = bucket(l + BLOCK - np.arange(2 * BLOCK)[:, None])
    j = np.arange(BLOCK)[:, None]
    tables = [np.concatenate([bucket(v * BLOCK + l - j), band], axis=0) for v in range(3)]
    return np.concatenate([t.reshape(-1) for t in tables])


def _onehot_t():
    buckets = jnp.asarray(_bucket_table())
    return (buckets[None, :] == jnp.arange(REL_BUCKETS, dtype=jnp.int32)[:, None]).astype(F32)


def _bias_tables(rel_t, onehot_t):
    def body(r_ref, oh_ref, o_ref):
        o_ref[...] = jnp.dot(r_ref[...], oh_ref[...], precision=HIGHEST, preferred_element_type=F32)

    return pl.pallas_call(
        body, name="bias_tables", grid=(NT_ALL // NT_TILE,),
        in_specs=[pl.BlockSpec((ATT_HEADS, REL_BUCKETS), lambda i: (0, 0)),
                  pl.BlockSpec((REL_BUCKETS, NT_TILE), lambda i: (0, i))],
        out_specs=pl.BlockSpec((ATT_HEADS, NT_TILE), lambda i: (0, i)),
        out_shape=jax.ShapeDtypeStruct((ATT_HEADS, NT_ALL), F32),
        compiler_params=_cparams("parallel"))(rel_t, onehot_t)


def _bias_grad(dtab, onehot_t):
    def body(d_ref, oh_ref, o_ref):
        i = pl.program_id(0)
        p = lax.dot_general(d_ref[...], oh_ref[...], (((1,), (1,)), ((), ())), precision=HIGHEST,
                            preferred_element_type=F32)

        @pl.when(i == 0)
        def _():
            o_ref[...] = p

        @pl.when(i > 0)
        def _():
            o_ref[...] += p

    return pl.pallas_call(
        body, name="bias_grad", grid=(NT_ALL // NT_TILE,),
        in_specs=[pl.BlockSpec((ATT_HEADS, NT_TILE), lambda i: (0, i)),
                  pl.BlockSpec((REL_BUCKETS, NT_TILE), lambda i: (0, i))],
        out_specs=pl.BlockSpec((ATT_HEADS, REL_BUCKETS), lambda i: (0, 0)),
        out_shape=jax.ShapeDtypeStruct((ATT_HEADS, REL_BUCKETS), F32),
        compiler_params=_cparams("arbitrary"))(dtab, onehot_t)


def _att_mask_t(n, copies):
    far = 4 * BLOCK
    kk = lax.broadcasted_iota(jnp.int32, (N_KEYS, copies * BLOCK), 0)
    li = lax.broadcasted_iota(jnp.int32, (N_KEYS, copies * BLOCK), 1) & (BLOCK - 1)
    meta_ok = (kk >= PAD) & (kk < BLOCK) & (li + jnp.where(n >= 1, far, 0) >= kk)
    prev_ok = (kk >= BLOCK) & (kk < 2 * BLOCK) & (kk - BLOCK > li + jnp.where(n >= 2, 0, far))
    cur_ok = (kk >= 2 * BLOCK) & (kk - 2 * BLOCK <= li - jnp.where(n >= 1, 0, far))
    return meta_ok | prev_ok | cur_ok


def _att_kv(meta_ref, prev_ref, cur_ref):
    kv = jnp.concatenate([meta_ref[...], prev_ref[...], cur_ref[...]], axis=0)
    first = lax.broadcasted_iota(jnp.int32, (N_KEYS, LANES), 1) < ATT_HEADDIM
    out = []
    for pair in (kv[:, :LANES], kv[:, LANES:]):
        swapped = pltpu.roll(pair, ATT_HEADDIM, 1)
        out.append([jnp.where(first, pair, swapped).astype(BF16), jnp.where(first, swapped, pair).astype(BF16)])
    return out[0], out[1]


def _split_heads(x_pair, first):
    return jnp.concatenate([jnp.where(first, x_pair, 0.0), jnp.where(first, 0.0, x_pair)], axis=0).astype(BF16)


def _att_probs_t(qm2, k_dup, t_ref, j, mask2, sink_ref):
    scale = ATT_HEADDIM ** -0.5
    bias2 = jnp.concatenate([t_ref[0, 2 * j], t_ref[0, 2 * j + 1]], axis=1)
    second = lax.broadcasted_iota(jnp.int32, (1, 2 * BLOCK), 1) >= BLOCK
    sink2 = jnp.where(second, sink_ref[0:1, 2 * j + 1:2 * j + 2], sink_ref[0:1, 2 * j:2 * j + 1])
    s_t = jnp.where(mask2, _dot_nt(k_dup, qm2) * scale + bias2, NEG)
    mx = jnp.maximum(jnp.max(s_t, axis=0, keepdims=True), sink2)
    p_t = jnp.exp(s_t - mx)
    p_s = jnp.exp(sink2 - mx)
    inv = 1.0 / (jnp.sum(p_t, axis=0, keepdims=True) + p_s)
    return p_t * inv, p_s * inv


def _att_specs(nb, rev):
    def nidx(i):
        return nb - 1 - i if rev else i

    kvb = ATT_Q // (2 * ATT_KV)
    q_s = pl.BlockSpec((BLOCK, ATT_Q), lambda i: (nidx(i), 0))
    cur = pl.BlockSpec((BLOCK, 2 * ATT_KV), lambda i: (nidx(i), kvb))
    prev = pl.BlockSpec((BLOCK, 2 * ATT_KV), lambda i: (jnp.maximum(nidx(i) - 1, 0), kvb))
    meta = pl.BlockSpec((BLOCK, 2 * ATT_KV), lambda i: (0, kvb))
    table = pl.BlockSpec((1, ATT_HEADS, N_KEYS, BLOCK), lambda i: (jnp.minimum(nidx(i), 2), 0, 0, 0))
    sink = pl.BlockSpec((1, LANES), lambda i: (0, 0))
    return q_s, cur, prev, meta, table, sink


def _attn_fwd(qkv, tables, sinks):
    n = qkv.shape[0]
    nb = n // BLOCK
    q_s, cur_s, prev_s, meta_s, t_s, sink_s = _att_specs(nb, False)

    def body(q_ref, cur_ref, prev_ref, meta_ref, t_ref, sink_ref, o_ref):
        blk = pl.program_id(0)
        mask_t = _att_mask_t(blk, 1)
        k_dup, v_dup = _att_kv(meta_ref, prev_ref, cur_ref)
        v_dup_t = [v.T for v in v_dup]
        first = lax.broadcasted_iota(jnp.int32, (BLOCK, LANES), 1) < ATT_HEADDIM
        top = lax.broadcasted_iota(jnp.int32, (LANES, BLOCK), 0) < ATT_HEADDIM
        scale = ATT_HEADDIM ** -0.5
        for j in range(ATT_HEADS // 2):
            kh = 2 * j // ATT_GQ
            tile = slice(LANES * j, LANES * (j + 1))
            q_p = q_ref[:, tile]
            res = []
            for half, h in enumerate((2 * j, 2 * j + 1)):
                qm = jnp.where(first if half == 0 else jnp.logical_not(first), q_p, 0.0).astype(BF16)
                sink = sink_ref[0:1, h:h + 1]
                s_t = jnp.where(mask_t, _dot_nt(k_dup[kh], qm) * scale + t_ref[0, h], NEG)
                mx = jnp.maximum(jnp.max(s_t, axis=0, keepdims=True), sink)
                p_t = jnp.exp(s_t - mx)
                inv = 1.0 / (jnp.sum(p_t, axis=0, keepdims=True) + jnp.exp(sink - mx))
                res.append(_dot(v_dup_t[kh], (p_t * inv).astype(BF16)))
            o_ref[:, tile] = jnp.where(top, res[0], res[1]).T.astype(BF16)

    return pl.pallas_call(
        body, name="attn_fwd", grid=(nb,),
        in_specs=[q_s, cur_s, prev_s, meta_s, t_s, sink_s],
        out_specs=q_s,
        out_shape=jax.ShapeDtypeStruct((n, ATT_Q), BF16),
        compiler_params=_cparams("parallel"))(qkv, qkv, qkv, qkv, tables, sinks)


def _attn_bwd(datt, qkv, tables, sinks):
    n = qkv.shape[0]
    nb = n // BLOCK
    q_s, cur_s, prev_s, meta_s, t_s, sink_s = _att_specs(nb, True)
    dqkv_s = pl.BlockSpec((BLOCK, ATT_Q + 2 * ATT_KV), lambda i: (nb - 1 - i, 0))
    scale = ATT_HEADDIM ** -0.5

    def body(do_ref, q_ref, cur_ref, prev_ref, meta_ref, t_ref, sink_ref,
             dqkv_ref, dt_ref, dsink_ref, carry_scr, meta_scr):
        step = pl.program_id(0)
        blk = nb - 1 - step
        mask2 = _att_mask_t(blk, 2)
        k_dup, v_dup = _att_kv(meta_ref, prev_ref, cur_ref)
        k_dup_t = [k.T for k in k_dup]

        @pl.when(step == 0)
        def _():
            carry_scr[...] = jnp.zeros_like(carry_scr)
            meta_scr[...] = jnp.zeros_like(meta_scr)
            dsink_ref[...] = jnp.zeros_like(dsink_ref)

        @pl.when((step == 0) | (blk <= 1))
        def _():
            dt_ref[...] = jnp.zeros_like(dt_ref)

        first = lax.broadcasted_iota(jnp.int32, (BLOCK, LANES), 1) < ATT_HEADDIM
        top = lax.broadcasted_iota(jnp.int32, (LANES, BLOCK), 0) < ATT_HEADDIM
        first_k = lax.broadcasted_iota(jnp.int32, (N_KEYS, LANES), 1) < ATT_HEADDIM
        dsink = jnp.zeros((1, LANES), F32)
        dk_acc = [None] * ATT_KV_HEADS
        dv_acc = [None] * ATT_KV_HEADS
        for j in range(ATT_HEADS // 2):
            kh = 2 * j // ATT_GQ
            tile = slice(LANES * j, LANES * (j + 1))
            qm2 = _split_heads(q_ref[:, tile], first)
            dom2 = _split_heads(do_ref[:, tile], first)
            p_t, p_s = _att_probs_t(qm2, k_dup[kh], t_ref, j, mask2, sink_ref)
            dp_t = _dot_nt(v_dup[kh], dom2)
            delta = jnp.sum(p_t * dp_t, axis=0, keepdims=True)
            ds_t = p_t * (dp_t - delta)
            sink_terms = p_s * delta
            for half in range(2):
                cols = slice(BLOCK * half, BLOCK * (half + 1))
                dsink = _lane_put(dsink, -jnp.sum(sink_terms[:, cols], axis=1, keepdims=True), 2 * j + half)
                dt_ref[0, 2 * j + half] += ds_t[:, cols]
            ds_tb = ds_t.astype(BF16)
            dq_t = _dot(k_dup_t[kh], ds_tb)
            dqkv_ref[:, tile] = (jnp.where(top, dq_t[:, :BLOCK], dq_t[:, BLOCK:]).T * scale).astype(BF16)
            dk_part, dv_part = _dot(ds_tb, qm2), _dot(p_t.astype(BF16), dom2)
            dk_acc[kh] = dk_part if dk_acc[kh] is None else dk_acc[kh] + dk_part
            dv_acc[kh] = dv_part if dv_acc[kh] is None else dv_acc[kh] + dv_part
        dsink_ref[...] += dsink
        folded = [a + pltpu.roll(a, ATT_HEADDIM, 1) for a in dk_acc + dv_acc]
        dkv = jnp.concatenate([jnp.where(first_k, folded[0], folded[1]) * scale,
                               jnp.where(first_k, folded[2], folded[3])], axis=1)
        meta_scr[...] += dkv[:BLOCK, :]
        own = dkv[2 * BLOCK:, :] + carry_scr[...]
        carry_scr[...] = dkv[BLOCK:2 * BLOCK, :]

        @pl.when(blk > 0)
        def _():
            dqkv_ref[:, ATT_Q:] = own.astype(BF16)

        @pl.when(blk == 0)
        def _():
            dqkv_ref[:, ATT_Q:] = (own + meta_scr[...]).astype(BF16)

    return pl.pallas_call(
        body, name="attn_bwd", grid=(nb,),
        in_specs=[q_s, q_s, cur_s, prev_s, meta_s, t_s, sink_s],
        out_specs=[dqkv_s, t_s, sink_s],
        out_shape=[jax.ShapeDtypeStruct((n, ATT_Q + 2 * ATT_KV), BF16),
                   jax.ShapeDtypeStruct((3, ATT_HEADS, N_KEYS, BLOCK), F32),
                   jax.ShapeDtypeStruct((1, LANES), F32)],
        scratch_shapes=[pltpu.VMEM((BLOCK, 2 * ATT_KV), F32), pltpu.VMEM((BLOCK, 2 * ATT_KV), F32)],
        compiler_params=_cparams("arbitrary"))(datt, qkv, qkv, qkv, qkv, tables, sinks)


def _merge_out_fwd(gates, y_ssd, y_att, gate_b, w_out, h):
    n = gates.shape[0]
    tm = _row_tile(n, 416)

    def body(gs_ref, ga_ref, ys_ref, ya_ref, gb_ref, w_ref, h_ref, m_ref, o_ref):
        merged = (_sigmoid(gs_ref[...] + gb_ref[0:1, :]) * ys_ref[...]
                  + _sigmoid(ga_ref[...] + gb_ref[1:2, :]) * ya_ref[...]).astype(BF16)
        m_ref[...] = merged
        row = pl.program_id(0) * tm + lax.broadcasted_iota(jnp.int32, (tm, 1), 0)
        o_ref[...] = jnp.where(row >= PAD, _dot(merged, w_ref[...]), 0.0) + h_ref[...]

    row = pl.BlockSpec((tm, D_MODEL), lambda i: (i, 0))
    return pl.pallas_call(
        body, name="merge_out_fwd", grid=(n // tm,),
        in_specs=[row, pl.BlockSpec((tm, D_MODEL), lambda i: (i, 1)), row, row,
                  pl.BlockSpec((2, D_MODEL), lambda i: (0, 0)), pl.BlockSpec((D_MODEL, D_MODEL), lambda i: (0, 0)), row],
        out_specs=[row, row],
        out_shape=[jax.ShapeDtypeStruct((n, D_MODEL), BF16), jax.ShapeDtypeStruct((n, D_MODEL), F32)],
        compiler_params=_cparams("parallel"))(gates, gates, y_ssd, y_att, gate_b, w_out, h)


def _merge_out_bwd(dh, w_out, gates, y_ssd, y_att, gate_b):
    n = gates.shape[0]
    tm = _row_tile(n, 416)

    def body(dh_ref, w_ref, gs_ref, ga_ref, ys_ref, ya_ref, gb_ref, dys_ref, dya_ref, dg_ref, dgb_ref):
        i = pl.program_id(0)
        row = i * tm + lax.broadcasted_iota(jnp.int32, (tm, 1), 0)
        dmv = jnp.where(row >= PAD, _dot_nt(dh_ref[...].astype(BF16), w_ref[...]), 0.0)
        ss =_sigmoid(gs_ref[...] + gb_ref[0:1, :])
        sa = _sigmoid(ga_ref[...] + gb_ref[1:2, :])
        dys_ref[...] = (dmv * ss).astype(BF16)
        dya_ref[...] = (dmv * sa).astype(BF16)
        dgs = dmv * ys_ref[...] * ss * (1.0 - ss)
        dga = dmv * ya_ref[...] * sa * (1.0 - sa)
        dg_ref[:, :D_MODEL] = dgs.astype(BF16)
        dg_ref[:, D_MODEL:] = dga.astype(BF16)
        part = jnp.concatenate([jnp.sum(dgs, axis=0, keepdims=True), jnp.sum(dga, axis=0, keepdims=True)], axis=0)

        @pl.when(i == 0)
        def _():
            dgb_ref[...] = part

        @pl.when(i > 0)
        def _():
            dgb_ref[...] += part

    row = pl.BlockSpec((tm, D_MODEL), lambda i: (i, 0))
    gb = pl.BlockSpec((2, D_MODEL), lambda i: (0, 0))
    return pl.pallas_call(
        body, name="merge_out_bwd", grid=(n // tm,),
        in_specs=[row, pl.BlockSpec((D_MODEL, D_MODEL), lambda i: (0, 0)), row,
                  pl.BlockSpec((tm, D_MODEL), lambda i: (i, 1)), row, row, gb],
        out_specs=[row, row, pl.BlockSpec((tm, 2 * D_MODEL), lambda i: (i, 0)), gb],
        out_shape=[jax.ShapeDtypeStruct((n, D_MODEL), BF16), jax.ShapeDtypeStruct((n, D_MODEL), BF16),
                   jax.ShapeDtypeStruct((n, 2 * D_MODEL), BF16), jax.ShapeDtypeStruct((2, D_MODEL), F32)],
        compiler_params=_cparams("arbitrary"))(dh, w_out, gates, gates, y_ssd, y_att, gate_b)


def _col_move(srcs, outs, pieces, *, name):
    rows = srcs[0].shape[-2]
    tr = _row_tile(rows, 128)
    n_src = len(srcs)
    covered = [sum(p[6] for p in pieces if p[0] == o) for o in range(len(outs))]
    total = [int(np.prod(shp)) // rows for shp, _ in outs]

    def body(*refs):
        in_refs, out_refs = refs[:n_src], refs[n_src:]
        for o, ref in enumerate(out_refs):
            if covered[o] != total[o]:
                ref[...] = jnp.zeros_like(ref)
        for o, ol, oc, s, sl, sc, width in pieces:
            val = in_refs[s][:, sc:sc + width] if sl is None else in_refs[s][sl, :, sc:sc + width]
            val = val.astype(outs[o][1])
            if ol is None:
                out_refs[o][:, oc:oc + width] = val
            else:
                out_refs[o][ol, :, oc:oc + width] = val

    def spec(shape):
        if len(shape) == 2:
            return pl.BlockSpec((tr, shape[1]), lambda i: (i, 0))
        return pl.BlockSpec((shape[0], tr, shape[2]), lambda i: (0, i, 0))

    return pl.pallas_call(
        body, name=name, grid=(rows // tr,),
        in_specs=[spec(a.shape) for a in srcs], out_specs=[spec(shp) for shp, _ in outs],
        out_shape=[jax.ShapeDtypeStruct(shp, dt) for shp, dt in outs],
        compiler_params=_cparams("parallel"))(*srcs)


def _shard_pieces(seg_ranges, shard_w):
    out = []
    for seg, runs in enumerate(seg_ranges):
        for g0, width, s0 in runs:
            done = 0
            while done < width:
                dev, col = divmod(g0 + done, shard_w)
                take = min(width - done, shard_w - col)
                out.append((seg, s0 + done, dev, col, take))
                done += take
    return out


_CHIP_RELATIONS = [(1, 0, 0), (0, 1, 0), (1, 1, 0)]
N_CHIPS = 4


def _gather_two_level(arrays, *, name):
    outs = _run_plan(_gather_plan(arrays), name)
    return [o.reshape((N_DEV,) + a.shape) for o, a in zip(outs, arrays)]


class _CommPlan:
    def __init__(self, arrays, out_shape, scratch_shapes, phases):
        self.arrays, self.out_shape, self.scratch_shapes, self.phases = arrays, out_shape, scratch_shapes, phases


def _run_plan(plan, name):
    n_arr = len(plan.arrays)

    def body(*refs):
        ins, outs, sems = refs[:n_arr], refs[n_arr:2 * n_arr], refs[2 * n_arr:]
        for phase in plan.phases:
            phase(ins, outs, sems)

    any_spec = pl.BlockSpec(memory_space=pl.ANY)
    return pl.pallas_call(
        body, name=name, in_specs=[any_spec] * n_arr, out_specs=[any_spec] * n_arr, out_shape=plan.out_shape,
        scratch_shapes=plan.scratch_shapes)(*plan.arrays)


def _gather_plan(arrays):
    n_arr = len(arrays)
    n_chips = len(_CHIP_RELATIONS)
    n_pair = 1 + 2 * n_chips

    def where():
        x, y, c = lax.axis_index("x"), lax.axis_index("y"), lax.axis_index("c")
        return x, y, c, (x, y, 1 - c), [(x ^ dx, y ^ dy) for dx, dy, _ in _CHIP_RELATIONS]

    def copy(outs, sems, a, k, block, to, src=None):
        slot = outs[a].at[2 * block[0] + block[1], block[2]]
        return pltpu.make_async_remote_copy(
            src_ref=slot if src is None else src, dst_ref=slot, send_sem=sems[0].at[a * n_pair + k],
            recv_sem=sems[1].at[a * n_pair + k], device_id=to, device_id_type=MESH)

    def mine(ins, outs, sems, a, x, y, c):
        return pltpu.make_async_copy(ins[a], outs[a].at[2 * x + y, c], sems[2].at[a])

    def first_copies(ins, outs, sems, a, x, y, c, sibling, chips):
        return ([copy(outs, sems, a, 0, (x, y, c), sibling, src=ins[a])]
                + [copy(outs, sems, a, 1 + j, (x, y, c), (*chip, c), src=ins[a]) for j, chip in enumerate(chips)])

    def start(ins, outs, sems):
        x, y, c, sibling, chips = where()
        for a in range(n_arr):
            mine(ins, outs, sems, a, x, y, c).start()
            for cp in first_copies(ins, outs, sems, a, x, y, c, sibling, chips):
                cp.start()

    def pass_on(ins, outs, sems):
        x, y, c, sibling, chips = where()
        for j, chip in enumerate(chips):
            for a in range(n_arr):
                copy(outs, sems, a, 1 + j, (*chip, c), (x, y, c)).wait_recv()
                copy(outs, sems, a, 1 + n_chips + j, (*chip, c), sibling).start()

    def finish(ins, outs, sems):
        x, y, c, sibling, chips = where()
        for a in range(n_arr):
            copy(outs, sems, a, 0, (x, y, 1 - c), (x, y, c)).wait_recv()
            for j, chip in enumerate(chips):
                copy(outs, sems, a, 1 + n_chips + j, (*chip, 1 - c), (x, y, c)).wait_recv()
        for a in range(n_arr):
            for cp in first_copies(ins, outs, sems, a, x, y, c, sibling, chips):
                cp.wait_send()
            for j, chip in enumerate(chips):
                copy(outs, sems, a, 1 + n_chips + j, (*chip, c), sibling).wait_send()
            mine(ins, outs, sems, a, x, y, c).wait()

    return _CommPlan(
        arrays, [jax.ShapeDtypeStruct((N_CHIPS, 2) + a.shape, a.dtype) for a in arrays],
        [pltpu.SemaphoreType.DMA((n_arr * n_pair,)), pltpu.SemaphoreType.DMA((n_arr * n_pair,)),
         pltpu.SemaphoreType.DMA((n_arr,))],
        (start, pass_on, finish))


_ALL_RELATIONS = [(dx, dy, dc) for dx in (0, 1) for dy in (0, 1) for dc in (0, 1)][1:]


def _all_to_all_plan(arrays, scatter=None):
    n_arr = len(arrays)
    n_rel = len(_ALL_RELATIONS)
    scatter = scatter or [True] * n_arr

    def block(ins, a, p):
        return ins[a].at[p] if scatter[a] else ins[a]

    def local_copies(ins, outs, sems):
        me = 4 * lax.axis_index("x") + 2 * lax.axis_index("y") + lax.axis_index("c")
        return [pltpu.make_async_copy(block(ins, a, me), outs[a].at[me], sems[2].at[a]) for a in range(n_arr)]

    def remote_copies(ins, outs, sems, arrivals):
        x, y, c = lax.axis_index("x"), lax.axis_index("y"), lax.axis_index("c")
        me = 4 * x + 2 * y + c
        out = []
        for k, (dx, dy, dc) in enumerate(_ALL_RELATIONS):
            px, py, pc = x ^ dx, y ^ dy, c ^ dc
            peer = 4 * px + 2 * py + pc
            for a in range(n_arr):
                out.append(pltpu.make_async_remote_copy(
                    src_ref=block(ins, a, peer), dst_ref=outs[a].at[peer if arrivals else me],
                    send_sem=sems[0].at[a * n_rel + k], recv_sem=sems[1].at[a * n_rel + k],
                    device_id=(x, y, c) if arrivals else (px, py, pc), device_id_type=MESH))
        return out

    def start(ins, outs, sems):
        for cp in local_copies(ins, outs, sems) + remote_copies(ins, outs, sems, False):
            cp.start()

    def pass_on(ins, outs, sems):
        pass

    def finish(ins, outs, sems):
        for send in remote_copies(ins, outs, sems, False):
            send.wait_send()
        for arrival in remote_copies(ins, outs, sems, True):
            arrival.wait_recv()
        for cp in local_copies(ins, outs, sems):
            cp.wait()

    return _CommPlan(
        arrays, [jax.ShapeDtypeStruct(a.shape if s else (N_DEV,) + a.shape, a.dtype) for a, s in zip(arrays, scatter)],
        [pltpu.SemaphoreType.DMA((n_arr * n_rel,)), pltpu.SemaphoreType.DMA((n_arr * n_rel,)),
         pltpu.SemaphoreType.DMA((n_arr,))],
        (start, pass_on, finish))


def _adamw(w, gslots, m, v, *, name, side=None):
    rows, cols = w.shape
    n_slots = gslots.shape[0]
    tr = _row_tile(rows, 128) if rows % 16 == 0 else rows

    def body(w_ref, g_ref, m_ref, v_ref, go_ref, d_ref, mo_ref, vo_ref):
        g = g_ref[0].astype(F32)
        for s in range(1, n_slots):
            g = g + g_ref[s].astype(F32)
        mn = ADAM_B1 * m_ref[...] + (1.0 - ADAM_B1) * g
        vn = ADAM_B2 * v_ref[...] + (1.0 - ADAM_B2) * (g * g)
        go_ref[...] = g
        mo_ref[...] = mn
        vo_ref[...] = vn
        m_hat = mn / (1.0 - ADAM_B1 ** ADAM_STEP)
        v_hat = vn / (1.0 - ADAM_B2 ** ADAM_STEP)
        d_ref[...] = -ADAM_LR * (m_hat / (jnp.sqrt(v_hat) + ADAM_EPS) + ADAM_WD * w_ref[...])

    blk = pl.BlockSpec((tr, cols), lambda i: (i, 0))
    shp = jax.ShapeDtypeStruct((rows, cols), F32)
    outs, carried = _call_with_side(
        body, side, name=name, grid=(rows // tr,),
        in_specs=[blk, pl.BlockSpec((n_slots, tr, cols), lambda i: (0, i, 0)), blk, blk],
        out_specs=[blk] * 4, out_shape=[shp] * 4, scratch_shapes=[], args=(w, gslots, m, v), semantics=("parallel",))
    return outs if side is None else (outs, carried)


_BIG = ("w_in", "w_ssd_branch", "w_attn_branch", "w_out", "w_ffn_in", "w_ffn_out")
_SMALL_SHARDED = ("meta_tokens", "ssd_conv_w", "gate_b", "ffn_conv_w")
_SMALL_REPLICATED = ("norm_mix_w", "ssd_conv_b", "ssd_dt_bias", "ssd_a_log", "ssd_d", "ssd_norm_w", "attn_sinks",
                     "rel_bias", "norm_ffn_w", "ffn_conv_b", "norm_final_w")
_WEIGHTS = ("meta_tokens", "norm_mix_w", "w_in", "ssd_conv_w", "ssd_conv_b", "ssd_dt_bias", "ssd_a_log", "ssd_d",
            "ssd_norm_w", "w_ssd_branch", "w_attn_branch", "attn_sinks", "rel_bias", "gate_b", "w_out", "norm_ffn_w",
            "w_ffn_in", "ffn_conv_w", "ffn_conv_b", "w_ffn_out", "norm_final_w")
_ROW_SHARDED = ("w_ssd_branch", "w_attn_branch", "w_out", "w_ffn_out")
_COL_SHARDED = ("w_in", "w_ffn_in", "meta_tokens", "ssd_conv_w", "gate_b", "ffn_conv_w")
_IN_SEGS = (("z", SSD_INNER), ("xbc", SSD_XBC), ("dt", SSD_HEADS), ("qkv", ATT_Q + 2 * ATT_KV), ("g", 2 * D_MODEL))


def _pack_rows(flat_parts, width, row_mult):
    flat = jnp.concatenate([p.reshape(-1) for p in flat_parts])
    pad = (-flat.shape[0]) % (width * row_mult)
    if pad:
        flat = jnp.concatenate([flat, jnp.zeros((pad,), flat.dtype)])
    return flat.reshape(-1, width)


def _unpack(flat, shapes):
    out, off = [], 0
    for shp in shapes:
        size = int(np.prod(shp))
        out.append(flat[off:off + size].reshape(shp))
        off += size
    return out


def _gather_full(stack, name, shard_shape):
    if name in _COL_SHARDED:
        return jnp.transpose(stack, (1, 0, 2)).reshape(shard_shape[0], N_DEV * shard_shape[1])
    return stack.reshape(N_DEV * shard_shape[0], shard_shape[1])


_IN_SEG_W = {"z": SSD_INNER, "xbc": SSD_XBC, "dt": DT_W, "qkv": ATT_Q + 2 * ATT_KV, "g": 2 * D_MODEL}
_IN_SHARD_W = (SSD_INNER + SSD_XBC + SSD_HEADS + ATT_Q + 2 * ATT_KV + 2 * D_MODEL) // N_DEV
_FFN_SHARD_W = 2 * D_FF // N_DEV


def _in_seg_runs():
    runs, off = [], 0
    for nm, width in _IN_SEGS:
        if nm == "dt":
            runs.append([(off + SSD_HPG * g, SSD_HPG, LANES * g) for g in range(SSD_GROUPS)])
        else:
            runs.append([(off, width, 0)])
        off += width
    return runs


def _w_in_to_segments(stack):
    pieces = [(seg, None, scol, 0, dev, col, w) for seg, scol, dev, col, w in _shard_pieces(_in_seg_runs(), _IN_SHARD_W)]
    outs = [((D_MODEL, _IN_SEG_W[nm]), stack.dtype) for nm, _ in _IN_SEGS]
    return dict(zip([nm for nm, _ in _IN_SEGS], _col_move([stack], outs, pieces, name="w_in_segments")))


def _segments_to_w_in_shards(seg_grads):
    pieces = [(0, dev, col, seg, None, scol, w) for seg, scol, dev, col, w in _shard_pieces(_in_seg_runs(), _IN_SHARD_W)]
    return _col_move(seg_grads, [((N_DEV, D_MODEL, _IN_SHARD_W), seg_grads[0].dtype)], pieces, name="g_w_in_shards")[0]


def _ffn_in_from_shards(stack):
    pieces = [(0, None, scol, 0, dev, col, w)
              for _, scol, dev, col, w in _shard_pieces([[(0, 2 * D_FF, 0)]], _FFN_SHARD_W)]
    return _col_move([stack], [((D_MODEL, 2 * D_FF), stack.dtype)], pieces, name="w_ffn_in_full")[0]


def _ffn_in_to_shards(g_up, g_gate):
    pieces = [(0, dev, col, seg, None, scol, w)
              for seg, scol, dev, col, w in _shard_pieces([[(0, D_FF, 0)], [(D_FF, D_FF, 0)]], _FFN_SHARD_W)]
    return _col_move([g_up, g_gate], [((N_DEV, D_MODEL, _FFN_SHARD_W), g_up.dtype)], pieces, name="g_w_ffn_in_shards")[0]


def _dt_spread(w_dt):
    k = w_dt.shape[0]
    w4 = w_dt.reshape(k, SSD_GROUPS, SSD_HPG)
    return jnp.pad(w4, ((0, 0), (0, 0), (0, LANES - SSD_HPG))).reshape(k, DT_W)


def _dt_gather(w_wide):
    k = w_wide.shape[0]
    return w_wide.reshape(k, SSD_GROUPS, LANES)[:, :, :SSD_HPG].reshape(k, SSD_HEADS)


class _LateExchanges:
    def __init__(self, two_d, shape2):
        self.two_d, self.shape2 = two_d, shape2
        self.early_grads_received = None
        self.w_in_grads_received = None

    def row_pack(self, tree):
        return jnp.concatenate([tree[k] for k in _ROW_SHARDED], axis=0)

    def late_weights_plan(self):
        return _gather_plan([self.two_d["w_ffn_in"].astype(BF16), self.row_pack(self.two_d).astype(BF16)])

    def late_weights(self, gathered):
        w_ffn_in_all, rows_all = [g.reshape((N_DEV,) + g.shape[2:]) for g in gathered]
        out = {"w_ffn_in": _ffn_in_from_shards(w_ffn_in_all)}
        off = 0
        for k in _ROW_SHARDED:
            r = self.shape2[k][0]
            out[k] = rows_all[:, off:off + r].reshape(N_DEV * r, D_MODEL)
            off += r
        return out

    def early_grads_plan(self, grads):
        rows_send = jnp.concatenate([grads[k].reshape(N_DEV, self.shape2[k][0], D_MODEL) for k in _ROW_SHARDED], axis=1)
        return _all_to_all_plan([_ffn_in_to_shards(*grads["w_ffn_in"]), rows_send])

    def w_in_grads_plan(self, seg_grads):
        return _all_to_all_plan([_segments_to_w_in_shards(seg_grads)])


def _local_step(x, target, w, exchanges=None):
    h0 = jnp.concatenate([jnp.zeros((PAD, D_MODEL), F32), w["meta_tokens"], x], axis=0)
    segs = w["in_segs"]

    dtb = _dt_spread(w["ssd_dt_bias"])
    alog = _dt_spread(w["ssd_a_log"])
    dskip_w = jnp.repeat(w["ssd_d"], SSD_HEADDIM, axis=1)
    sinks = jnp.pad(w["attn_sinks"], ((0, 0), (0, LANES - ATT_HEADS)))
    onehot_t = _onehot_t()
    tables = jnp.transpose(_bias_tables(w["rel_bias"].T, onehot_t).reshape(ATT_HEADS, 3, N_KEYS, BLOCK), (1, 0, 2, 3))

    u = _rms_fwd(h0, w["norm_mix_w"], name="rms_mix_fwd")
    z = _mm(u, segs["z"], name="in_z")
    xbc, pre = _mm_conv_fwd(u, segs["xbc"], w["ssd_conv_w"], w["ssd_conv_b"], name="in_xbc_conv_fwd")
    dt_raw = _mm(u, segs["dt"], name="in_dt")
    qkv = _mm(u, segs["qkv"], out_dtype=BF16, name="in_qkv")
    gates = _mm(u, segs["g"], name="in_g")
    (y, yn, hsave), gathered = _ssd_fwd(pre, dt_raw, z, dtb, alog, dskip_w, w["ssd_norm_w"],
                                        side=None if exchanges is None else exchanges.late_weights_plan())
    if exchanges is not None:
        w = {**w, **exchanges.late_weights(gathered)}
    w_ffn_up, w_ffn_gate = w["w_ffn_in"][:, :D_FF], w["w_ffn_in"][:, D_FF:]
    y_ssd = _mm(yn, w["w_ssd_branch"], out_dtype=BF16, name="ssd_out")
    att = _attn_fwd(qkv, tables, sinks)
    y_att = _mm(att, w["w_attn_branch"], out_dtype=BF16, name="att_out")
    merged, h1 = _merge_out_fwd(gates, y_ssd, y_att, w["gate_b"], w["w_out"], h0)
    u2 = _rms_fwd(h1, w["norm_ffn_w"], name="rms_ffn_fwd")
    x_up, x_gate, hid_up, hid_gate, act = _ffn_in_act_fwd(u2, w["w_ffn_in"], w["ffn_conv_w"], w["ffn_conv_b"])
    h2 = _mm(act, w["w_ffn_out"], c=h1, mask=True, name="ffn_out")
    dh2, dh2_b, loss_row, g_norm_final = _final_loss(h2, w["norm_final_w"], target)

    grads = {"norm_final_w": g_norm_final}
    grads["w_ffn_out"] = _mm(act, dh2_b, ta=True, mask=True, out_dtype=BF16, name="g_w_ffn_out")
    dx_up, dx_gate, dcw_up, dcw_gate, dcb_up, dcb_gate = _ffn_out_act_bwd(
        dh2_b, w["w_ffn_out"], hid_up, hid_gate, x_up, x_gate, w["ffn_conv_w"])
    grads["ffn_conv_w"] = jnp.concatenate([dcw_up, dcw_gate], axis=1)
    grads["ffn_conv_b"] = jnp.concatenate([dcb_up, dcb_gate], axis=1)
    (dh1, grads["norm_ffn_w"]), _ = _mm_rms_bwd([(dx_up, w_ffn_up), (dx_gate, w_ffn_gate)], h1, w["norm_ffn_w"], dh2,
                                                name="d_u2_rms_bwd")
    grads["w_ffn_in"] = (_mm(u2, dx_up, ta=True, out_dtype=BF16, name="g_w_ffn_up"),
                         _mm(u2, dx_gate, ta=True, out_dtype=BF16, name="g_w_ffn_gate"))

    grads["w_out"] = _mm(merged, dh1, ta=True, mask=True, out_dtype=BF16, name="g_w_out")
    dy_ssd, dy_att, dgates, grads["gate_b"] = _merge_out_bwd(dh1, w["w_out"], gates, y_ssd, y_att, w["gate_b"])
    dyn = _mm(dy_ssd, w["w_ssd_branch"], tb=True, name="d_yn")
    grads["w_ssd_branch"] = _mm(yn, dy_ssd, ta=True, out_dtype=BF16, name="g_w_ssd")
    datt = _mm(dy_att, w["w_attn_branch"], tb=True, out_dtype=BF16, name="d_att")
    grads["w_attn_branch"] = _mm(att, dy_att, ta=True, out_dtype=BF16, name="g_w_att")
    (dz, dpxs, dpb, dpc, ddt, grads["ssd_norm_w"], g_dtb, g_alog, g_dskip), received = _ssd_bwd(
        dyn, y, z, pre, dt_raw, hsave, dtb, alog, dskip_w, w["ssd_norm_w"],
        side=None if exchanges is None else exchanges.early_grads_plan(grads))
    if exchanges is not None:
        exchanges.early_grads_received = received
    grads["ssd_dt_bias"] = _dt_gather(g_dtb)
    grads["ssd_a_log"] = _dt_gather(g_alog)
    grads["ssd_d"] = _dt_gather(g_dskip)
    conv_g = _conv_bwd(dpxs, xbc, w["ssd_conv_w"], name="ssd_conv_bwd_x")
    conv_g = _conv_bwd(dpb, xbc, w["ssd_conv_w"], name="ssd_conv_bwd_b", col0=SSD_INNER, into=conv_g)
    dxbc, grads["ssd_conv_w"], grads["ssd_conv_b"] = _conv_bwd(
        dpc, xbc, w["ssd_conv_w"], name="ssd_conv_bwd_c", col0=SSD_INNER + SSD_BC, into=conv_g)
    dqkv, d_tables, d_sinks = _attn_bwd(datt, qkv, tables, sinks)
    grads["attn_sinks"] = d_sinks[:, :ATT_HEADS]
    dtab = jnp.transpose(d_tables, (1, 0, 2, 3)).reshape(ATT_HEADS, NT_ALL)
    grads["rel_bias"] = _bias_grad(dtab, onehot_t).T
    dsegs = {"z": dz, "xbc": dxbc, "dt": ddt, "qkv": dqkv, "g": dgates}
    grads["in_segs"] = [_mm(u, dsegs[nm], ta=True, out_dtype=BF16, name="g_w_in_" + nm) for nm, _ in _IN_SEGS]
    (dh0, grads["norm_mix_w"]), received = _mm_rms_bwd(
        [(dsegs[nm], segs[nm]) for nm, _ in _IN_SEGS], h0, w["norm_mix_w"], dh1, name="d_u_rms_bwd",
        side=None if exchanges is None else exchanges.w_in_grads_plan(grads["in_segs"]))
    if exchanges is not None:
        exchanges.w_in_grads_received = received[0]
    grads["meta_tokens"] = dh0[PAD:BLOCK]
    return loss_row[0, 0], dh0[BLOCK:], grads


def kernel(x, meta_tokens, norm_mix_w, w_in, ssd_conv_w, ssd_conv_b, ssd_dt_bias, ssd_a_log, ssd_d, ssd_norm_w, w_ssd_branch, w_attn_branch, attn_sinks, rel_bias, gate_b, w_out, norm_ffn_w, w_ffn_in, ffn_conv_w, ffn_conv_b, w_ffn_out, norm_final_w, loss_target, m_meta_tokens, m_norm_mix_w, m_w_in, m_ssd_conv_w, m_ssd_conv_b, m_ssd_dt_bias, m_ssd_a_log, m_ssd_d, m_ssd_norm_w, m_w_ssd_branch, m_w_attn_branch, m_attn_sinks, m_rel_bias, m_gate_b, m_w_out, m_norm_ffn_w, m_w_ffn_in, m_ffn_conv_w, m_ffn_conv_b, m_w_ffn_out, m_norm_final_w, v_meta_tokens, v_norm_mix_w, v_w_in, v_ssd_conv_w, v_ssd_conv_b, v_ssd_dt_bias, v_ssd_a_log, v_ssd_d, v_ssd_norm_w, v_w_ssd_branch, v_w_attn_branch, v_attn_sinks, v_rel_bias, v_gate_b, v_w_out, v_norm_ffn_w, v_w_ffn_in, v_ffn_conv_w, v_ffn_conv_b, v_w_ffn_out, v_norm_final_w):
    shard = dict(meta_tokens=meta_tokens, norm_mix_w=norm_mix_w, w_in=w_in, ssd_conv_w=ssd_conv_w,
                 ssd_conv_b=ssd_conv_b, ssd_dt_bias=ssd_dt_bias, ssd_a_log=ssd_a_log, ssd_d=ssd_d,
                 ssd_norm_w=ssd_norm_w, w_ssd_branch=w_ssd_branch, w_attn_branch=w_attn_branch,
                 attn_sinks=attn_sinks, rel_bias=rel_bias, gate_b=gate_b, w_out=w_out, norm_ffn_w=norm_ffn_w,
                 w_ffn_in=w_ffn_in, ffn_conv_w=ffn_conv_w, ffn_conv_b=ffn_conv_b, w_ffn_out=w_ffn_out,
                 norm_final_w=norm_final_w)
    mom_m = dict(zip(_WEIGHTS, (m_meta_tokens, m_norm_mix_w, m_w_in, m_ssd_conv_w, m_ssd_conv_b, m_ssd_dt_bias,
                                m_ssd_a_log, m_ssd_d, m_ssd_norm_w, m_w_ssd_branch, m_w_attn_branch, m_attn_sinks,
                                m_rel_bias, m_gate_b, m_w_out, m_norm_ffn_w, m_w_ffn_in, m_ffn_conv_w, m_ffn_conv_b,
                                m_w_ffn_out, m_norm_final_w)))
    mom_v = dict(zip(_WEIGHTS, (v_meta_tokens, v_norm_mix_w, v_w_in, v_ssd_conv_w, v_ssd_conv_b, v_ssd_dt_bias,
                                v_ssd_a_log, v_ssd_d, v_ssd_norm_w, v_w_ssd_branch, v_w_attn_branch, v_attn_sinks,
                                v_rel_bias, v_gate_b, v_w_out, v_norm_ffn_w, v_w_ffn_in, v_ffn_conv_w, v_ffn_conv_b,
                                v_w_ffn_out, v_norm_final_w)))
    orig_shape = {k: a.shape for k, a in shard.items()}
    two_d = {k: a.reshape(a.shape[-2:]) if a.ndim >= 2 else a.reshape(1, -1) for k, a in shard.items()}
    shape2 = {k: a.shape for k, a in two_d.items()}

    def as2d(tree):
        return {k: tree[k].reshape(shape2[k]) for k in _WEIGHTS}

    mom_m, mom_v = as2d(mom_m), as2d(mom_v)

    exchanges = _LateExchanges(two_d, shape2)
    row_pack = exchanges.row_pack
    small_pack = _pack_rows([two_d[k] for k in _SMALL_SHARDED], LANES, SMALL_ROW_MULT)
    w_in_all, small_all = _gather_two_level([two_d["w_in"].astype(BF16), small_pack], name="gather_weights")
    full = {k: two_d[k] for k in _SMALL_REPLICATED}
    full["in_segs"] = _w_in_to_segments(w_in_all)
    small_flat = small_all.reshape(N_DEV, -1)
    off = 0
    for k in _SMALL_SHARDED:
        size = int(np.prod(shape2[k]))
        full[k] = _gather_full(small_flat[:, off:off + size].reshape((N_DEV,) + shape2[k]), k, shape2[k])
        off += size

    loss_local, grad_x, grads = _local_step(x[0], loss_target[0], full, exchanges)

    small_names = _SMALL_SHARDED + _SMALL_REPLICATED
    small_send = _pack_rows([grads[k] for k in small_names] + [loss_local.reshape(1)], LANES, SMALL_ROW_MULT)
    in_recv = exchanges.w_in_grads_received
    ffn_recv, rows_recv = exchanges.early_grads_received

    w_in_out, (small_recv,) = _adamw(two_d["w_in"], in_recv, mom_m["w_in"], mom_v["w_in"], name="adamw_w_in",
                                     side=_all_to_all_plan([small_send], [False]))
    big = {"w_in": w_in_out,
           "w_ffn_in": _adamw(two_d["w_ffn_in"], ffn_recv, mom_m["w_ffn_in"], mom_v["w_ffn_in"], name="adamw_w_ffn_in")}
    rows_out = _adamw(row_pack(two_d), rows_recv, row_pack(mom_m), row_pack(mom_v), name="adamw_rows")
    off = 0
    for k in _ROW_SHARDED:
        r = shape2[k][0]
        big[k] = [a[off:off + r] for a in rows_out]
        off += r
    me =4 * lax.axis_index("x") + 2 * lax.axis_index("y") + lax.axis_index("c")
    small_full_shapes = [grads[k].shape for k in small_names]
    n_small = sum(int(np.prod(s)) for s in small_full_shapes)

    def packed_small(tree):
        parts = []
        for k in small_names:
            a = tree[k]
            if k in _SMALL_SHARDED:
                fullw = jnp.zeros(grads[k].shape, F32)
                a = lax.dynamic_update_slice(fullw, a, (0, me * a.shape[1]))
            parts.append(a)
        return _pack_rows(parts + [jnp.zeros((1,), F32)], LANES, SMALL_ROW_MULT)

    g_small, d_small, m_small, v_small = _adamw(packed_small(two_d), small_recv, packed_small(mom_m),
                                                packed_small(mom_v), name="adamw_small")

    def unpack_all(which, small):
        out = {k: big[k][which] for k in _BIG}
        flat = small.reshape(-1)
        for k, a in zip(small_names, _unpack(flat, small_full_shapes)):
            if k in _SMALL_SHARDED:
                a = lax.dynamic_slice(a, (0, me * shape2[k][1]), shape2[k])
            out[k] = a
        return out, flat[n_small]

    g_all, loss = unpack_all(0, g_small)
    d_all, _ = unpack_all(1, d_small)
    m_all, _ = unpack_all(2, m_small)
    v_all, _ = unpack_all(3, v_small)

    def final(tree):
        return [tree[k].reshape(orig_shape[k]) for k in _WEIGHTS]

    return (loss, grad_x[None], *final(g_all), *final(d_all), *final(m_all), *final(v_all))
```

```python
import functools
import math

import numpy as np
import jax
import jax.numpy as jnp
from jax import lax
from jax.experimental import pallas as pl
from jax.experimental.pallas import tpu as pltpu

F32 = jnp.float32
BF16 = jnp.bfloat16
HIGHEST = lax.Precision.HIGHEST

D_MODEL = 1024
N_META = 16
BLOCK = 128
PAD = BLOCK - N_META
EPS = 1e-6
NEG = -1e30
SSD_INNER = 2 * D_MODEL
SSD_HEADDIM = 64
SSD_HEADS = SSD_INNER // SSD_HEADDIM
SSD_GROUPS = 4
SSD_HPG = SSD_HEADS // SSD_GROUPS
SSD_STATE = 128
SSD_CONV = 4
SSD_GW = SSD_HPG * SSD_HEADDIM
SSD_BC = SSD_GROUPS * SSD_STATE
SSD_XBC = SSD_INNER + 2 * SSD_BC
ATT_HEADS = 16
ATT_KV_HEADS = 2
ATT_HEADDIM = 64
ATT_GQ = ATT_HEADS // ATT_KV_HEADS
ATT_Q = ATT_HEADS * ATT_HEADDIM
ATT_KV = ATT_KV_HEADS * ATT_HEADDIM
REL_BUCKETS = 32
REL_MAX_DIST = 128
D_FF = 2816
FFN_CONV = 3
ADAM_LR = 0.001
ADAM_B1 = 0.9
ADAM_B2 = 0.999
ADAM_EPS = 1e-08
ADAM_WD = 0.01
ADAM_STEP = 10

N_DEV = 8
LANES = 128
SUBLANES = 8
BF16_ROWS = 16
DT_W = SSD_GROUPS * LANES
VMEM_LIMIT_BYTES = 56 * 1024 * 1024
MESH = pl.DeviceIdType.MESH

SMALL_ROW_MULT = 16

N_KEYS = 3 * BLOCK
NT_ALL = 3 * N_KEYS * BLOCK
NT_TILE = 8192


def _cparams(*sem):
    return pltpu.CompilerParams(dimension_semantics=sem, vmem_limit_bytes=VMEM_LIMIT_BYTES)


def _row_tile(n, cap):
    best = None
    for t in range(16, min(n, cap) + 1, 16):
        if n % t == 0:
            best = t
    return best or n


def _col_tile(n, cap):
    for t in (1408, 1280, 1024, 768, 640, 512, 384, 256, 128):
        if t <= cap and n % t == 0:
            return t
    return n


def _sigmoid(x):
    return 0.5 * jnp.tanh(0.5 * x) + 0.5


def _silu(x):
    return x * _sigmoid(x)


def _softplus(x):
    return jnp.maximum(x, 0.0) + jnp.log(1.0 + jnp.exp(-jnp.abs(x)))


def _dot_nt(a, b):
    return lax.dot_general(a, b, (((1,), (1,)), ((), ())), preferred_element_type=F32)


def _dot_tn(a, b):
    return lax.dot_general(a, b, (((0,), (0,)), ((), ())), preferred_element_type=F32)


def _dot(a, b):
    return jnp.dot(a, b, preferred_element_type=F32)


def _bf16_terms(x, terms):
    out, rest = [], x
    for _ in range(terms):
        part = rest.astype(BF16)
        out.append(part)
        rest = rest - part.astype(F32)
    return out


def _dot_sel(x, sel, terms=3):
    return sum(_dot(part, sel) for part in _bf16_terms(x, terms))


def _sel_dot(sel, x, terms=3):
    return sum(_dot(sel, part) for part in _bf16_terms(x, terms))


def _sum_all(x):
    return jnp.sum(jnp.sum(x, axis=1, keepdims=True), axis=0, keepdims=True)


MM_ROW_CAPS = (2080, 1664, 832, 416)
MM_COL_CAP = 1408
MM_VMEM_BUDGET = 44 * 1024 * 1024


def _mm_tiles(rows, cols, vmem_bytes):
    col_cands = [t for t in (2048, 1536, 1408, 1280, 1024, 768, 640, 512, 384, 256, 128) if cols % t == 0]
    if cols <= 2 * MM_COL_CAP:
        col_cands.append(cols)
    best = None
    for cap in MM_ROW_CAPS:
        tr = _row_tile(rows, cap)
        for tc in col_cands:
            if vmem_bytes(tr, tc) <= MM_VMEM_BUDGET and (best is None or tr * tc > best[0] * best[1]):
                best = (tr, tc)
    assert best is not None, (rows, cols)
    return best


def _mm(a, b, *, name, ta=False, tb=False, c=None, mask=False, out_dtype=F32):
    if not ta:
        m, k = a.shape
        n = b.shape[0] if tb else b.shape[1]
        tm, tn = _mm_tiles(m, n, lambda t_m, t_n: 2 * (t_m * k * a.dtype.itemsize + k * t_n * b.dtype.itemsize
                                                       + t_m * t_n * (jnp.dtype(out_dtype).itemsize
                                                                      + (0 if c is None else c.dtype.itemsize)))
                           + 4 * t_m * t_n)

        def body(*refs):
            if c is None:
                a_ref, b_ref, o_ref = refs
            else:
                a_ref, b_ref, c_ref, o_ref = refs
            acc = (_dot_nt if tb else _dot)(a_ref[...].astype(BF16), b_ref[...].astype(BF16))
            if mask:
                row = pl.program_id(0) * tm + lax.broadcasted_iota(jnp.int32, (tm, 1), 0)
                acc = jnp.where(row >= PAD, acc, 0.0)
            if c is not None:
                acc = acc + c_ref[...]
            o_ref[...] = acc.astype(out_dtype)

        b_spec = pl.BlockSpec((tn, k), lambda i, j: (j, 0)) if tb else pl.BlockSpec((k, tn), lambda i, j: (0, j))
        in_specs = [pl.BlockSpec((tm, k), lambda i, j: (i, 0)), b_spec]
        args = [a, b]
        if c is not None:
            in_specs.append(pl.BlockSpec((tm, tn), lambda i, j: (i, j)))
            args.append(c)
        return pl.pallas_call(
            body, name=name, grid=(m // tm, n // tn), in_specs=in_specs,
            out_specs=pl.BlockSpec((tm, tn), lambda i, j: (i, j)),
            out_shape=jax.ShapeDtypeStruct((m, n), out_dtype),
            compiler_params=_cparams("parallel", "parallel"))(*args)

    kc, m = a.shape
    n = b.shape[1]
    tm = _col_tile(m, MM_COL_CAP)
    tk, tn = _mm_tiles(kc, n, lambda t_k, t_n: 2 * (t_k * tm * a.dtype.itemsize + t_k * t_n * b.dtype.itemsize
                                                    + tm * t_n * jnp.dtype(out_dtype).itemsize) + 8 * tm * t_n)

    n_k = kc // tk

    def body_t(a_ref, b_ref, o_ref, acc_ref):
        kk = pl.program_id(2)
        bb = b_ref[...]
        if mask:
            row = kk * tk + lax.broadcasted_iota(jnp.int32, (tk, 1), 0)
            bb = jnp.where(row >= PAD, bb, jnp.zeros_like(bb))
        p = _dot_tn(a_ref[...].astype(BF16), bb.astype(BF16))

        @pl.when(kk == 0)
        def _():
            acc_ref[...] = p

        @pl.when(kk > 0)
        def _():
            acc_ref[...] += p

        @pl.when(kk == n_k - 1)
        def _():
            o_ref[...] = acc_ref[...].astype(out_dtype)

    return pl.pallas_call(
        body_t, name=name, grid=(m // tm, n // tn, n_k),
        in_specs=[pl.BlockSpec((tk, tm), lambda i, j, kk: (kk, i)), pl.BlockSpec((tk, tn), lambda i, j, kk: (kk, j))],
        out_specs=pl.BlockSpec((tm, tn), lambda i, j, kk: (i, j)),
        out_shape=jax.ShapeDtypeStruct((m, n), out_dtype),
        scratch_shapes=[pltpu.VMEM((tm, tn), F32)],
        compiler_params=_cparams("parallel", "parallel", "arbitrary"))(a, b)


def _mm_rms_bwd(pairs, x, w, dres, *, name, side=None):
    m, d = x.shape
    tm = _row_tile(m, 416)
    n_pairs = len(pairs)

    def body(*refs):
        a_refs, b_refs = refs[:n_pairs], refs[n_pairs:2 * n_pairs]
        x_ref, w_ref, dres_ref, dx_ref, dw_ref = refs[2 * n_pairs:]
        i = pl.program_id(0)
        dyv = None
        for a_ref, b_ref in zip(a_refs, b_refs):
            term = _dot_nt(a_ref[...].astype(BF16), b_ref[...])
            dyv = term if dyv is None else dyv + term
        xv = x_ref[...]
        r = lax.rsqrt(jnp.mean(xv * xv, axis=-1, keepdims=True) + EPS)
        xh = xv * r
        g = dyv * w_ref[...]
        dx_ref[...] = r * (g - xh * jnp.mean(g * xh, axis=-1, keepdims=True)) + dres_ref[...]
        part = jnp.sum(dyv * xh, axis=0, keepdims=True)

        @pl.when(i == 0)
        def _():
            dw_ref[...] = part

        @pl.when(i > 0)
        def _():
            dw_ref[...] += part

    row = pl.BlockSpec((tm, d), lambda i: (i, 0))
    vec = pl.BlockSpec((1, d), lambda i: (0, 0))
    in_specs = ([pl.BlockSpec((tm, a.shape[1]), lambda i: (i, 0)) for a, _ in pairs]
                + [pl.BlockSpec(b.shape, lambda i: (0, 0), pipeline_mode=pl.Buffered(1)) for _, b in pairs]
                + [row, vec, row])
    return _call_with_side(
        body, side, name=name, grid=(m // tm,), in_specs=in_specs, out_specs=[row, vec],
        out_shape=[jax.ShapeDtypeStruct((m, d), F32), jax.ShapeDtypeStruct((1, d), F32)], scratch_shapes=[],
        args=[a for a, _ in pairs] + [b for _, b in pairs] + [x, w, dres], semantics=("arbitrary",))


def _rms_fwd(h, w, *, name):
    n, d = h.shape
    tm = _row_tile(n, 832)

    def body(h_ref, w_ref, o_ref):
        x = h_ref[...]
        r = lax.rsqrt(jnp.mean(x * x, axis=-1, keepdims=True) + EPS)
        o_ref[...] = (x * r * w_ref[...]).astype(BF16)

    return pl.pallas_call(
        body, name=name, grid=(n // tm,),
        in_specs=[pl.BlockSpec((tm, d), lambda i: (i, 0)), pl.BlockSpec((1, d), lambda i: (0, 0))],
        out_specs=pl.BlockSpec((tm, d), lambda i: (i, 0)),
        out_shape=jax.ShapeDtypeStruct((n, d), BF16),
        compiler_params=_cparams("parallel"))(h, w)


def _final_loss(h, w, target):
    n, d = h.shape
    nb = n // BLOCK

    def body(h_ref, w_ref, t_ref, dh_ref, dhb_ref, loss_ref, dw_ref):
        i = pl.program_id(0)
        xv = h_ref[...]
        r = lax.rsqrt(jnp.mean(xv * xv, axis=-1, keepdims=True) + EPS)
        xh = xv * r
        wv = w_ref[...]
        err = jnp.where(i >= 1, xh * wv - t_ref[...], 0.0)
        dyv = err * (1.0 / d)
        g = dyv * wv
        dh = r * (g - xh * jnp.mean(g * xh, axis=-1, keepdims=True))
        dh_ref[...] = dh
        dhb_ref[...] = dh.astype(BF16)
        lpart = jnp.broadcast_to(0.5 * _sum_all(err * err) * (1.0 / d), (1, LANES))
        wpart = jnp.sum(dyv * xh, axis=0, keepdims=True)

        @pl.when(i == 0)
        def _():
            loss_ref[...] = lpart
            dw_ref[...] = wpart

        @pl.when(i > 0)
        def _():
            loss_ref[...] += lpart
            dw_ref[...] += wpart

    row = pl.BlockSpec((BLOCK, d), lambda i: (i, 0))
    vec = pl.BlockSpec((1, d), lambda i: (0, 0))
    return pl.pallas_call(
        body, name="final_loss", grid=(nb,),
        in_specs=[row, vec, pl.BlockSpec((BLOCK, d), lambda i: (jnp.maximum(i - 1, 0), 0))],
        out_specs=[row, row, pl.BlockSpec((1, LANES), lambda i: (0, 0)), vec],
        out_shape=[jax.ShapeDtypeStruct((n, d), F32), jax.ShapeDtypeStruct((n, d), BF16),
                   jax.ShapeDtypeStruct((1, LANES), F32), jax.ShapeDtypeStruct((1, d), F32)],
        compiler_params=_cparams("arbitrary"))(h, w, target)


def _main_spec(tm, cb, off=0):
    return pl.BlockSpec((tm, cb), lambda j, i: (i, j + off))


def _prev_spec(tm, cb, off=0):
    r8 = tm // SUBLANES
    return pl.BlockSpec((SUBLANES, cb), lambda j, i: (jnp.maximum(i * r8 - 1, 0), j + off))


def _next_spec(tm, cb, n_rows, off=0):
    r8 = tm // SUBLANES
    last = n_rows // SUBLANES - 1
    return pl.BlockSpec((SUBLANES, cb), lambda j, i: (jnp.minimum((i + 1) * r8, last), j + off))


def _with_prev(prev_ref, main_ref, i):
    prev = jnp.where(i > 0, prev_ref[...], 0.0)
    return jnp.concatenate([prev, main_ref[...]], axis=0)


def _with_next(main, nxt, i, n_tiles):
    return jnp.concatenate([main, jnp.where(i < n_tiles - 1, nxt, 0.0)], axis=0)


def _back(xx, s, tm):
    if s == 0:
        return xx[SUBLANES:SUBLANES + tm]
    return pltpu.roll(xx, s, 0)[SUBLANES:SUBLANES + tm]


def _ahead(xx, s, tm):
    if s == 0:
        return xx[:tm]
    return pltpu.roll(xx, xx.shape[0] - s, 0)[:tm]


def _mm_conv_fwd(u, w_in, w, b, *, name):
    n = u.shape[0]
    cdim = w_in.shape[1]
    kw = w.shape[0]
    tm = _row_tile(n, 832)
    cb = _col_tile(cdim, 512)
    nt = n // tm

    def body(u_ref, w_in_ref, w_ref, b_ref, x_ref, o_ref, acc_scr, halo_scr):
        j, i = pl.program_id(0), pl.program_id(1)

        @pl.when((j == 0) & (i == 0))
        def _():
            acc_scr[...] = jnp.zeros_like(acc_scr)
            halo_scr[...] = jnp.zeros_like(halo_scr)

        new = _dot(u_ref[...], w_in_ref[...])
        prev = acc_scr[...]
        xx = jnp.concatenate([jnp.where(i >= 2, halo_scr[...], 0.0), prev], axis=0)
        acc = jnp.broadcast_to(b_ref[...], (tm, cb))
        for k in range(kw):
            acc = acc + w_ref[k:k + 1, :] * _back(xx, kw - 1 - k, tm)
        x_ref[...] = prev.astype(BF16)
        o_ref[...] = acc
        halo_scr[...] = prev[tm - SUBLANES:, :]
        acc_scr[...] = new

    out = pl.BlockSpec((tm, cb), lambda j, i: (jnp.maximum(i - 1, 0), j))
    shp = jax.ShapeDtypeStruct((n, cdim), F32)
    return pl.pallas_call(
        body, name=name, grid=(cdim // cb, nt + 1),
        in_specs=[pl.BlockSpec((tm, u.shape[1]), lambda j, i: (jnp.minimum(i, nt - 1), 0)),
                  pl.BlockSpec((w_in.shape[0], cb), lambda j, i: (0, j)),
                  pl.BlockSpec((kw, cb), lambda j, i: (0, j)), pl.BlockSpec((1, cb), lambda j, i: (0, j))],
        out_specs=[out, out], out_shape=[jax.ShapeDtypeStruct((n, cdim), BF16), shp],
        scratch_shapes=[pltpu.VMEM((tm, cb), F32), pltpu.VMEM((SUBLANES, cb), F32)],
        compiler_params=_cparams("arbitrary", "arbitrary"))(u, w_in, w, b)


def _conv_bwd_core(dpre_ext, x, w_ref, kw, tm):
    dx = None
    dws = []
    for k in range(kw):
        shifted = _ahead(dpre_ext, kw - 1 - k, tm)
        term = w_ref[k:k + 1, :] * shifted
        dx = term if dx is None else dx + term
        dws.append(jnp.sum(shifted * x, axis=0, keepdims=True))
    return dx, dws, jnp.sum(dpre_ext[:tm], axis=0, keepdims=True)


def _acc_rows(i, dw_ref, db_ref, dws, db):
    @pl.when(i == 0)
    def _():
        for k, v in enumerate(dws):
            dw_ref[k:k + 1, :] = v
        db_ref[...] = db

    @pl.when(i > 0)
    def _():
        for k, v in enumerate(dws):
            dw_ref[k:k + 1, :] += v
        db_ref[...] += db


def _conv_bwd(dpre, x, w, *, name, col0=0, into=None):
    n, cdim = x.shape
    kw = w.shape[0]
    tm = _row_tile(n, 832)
    cb = _col_tile(cdim, 512)
    nt = n // tm
    off = col0 // cb
    n_alias = 0 if into is None else 3

    def body(d_ref, dn_ref, x_ref, w_ref, *rest):
        dx_ref, dw_ref, db_ref = rest[n_alias:]
        i = pl.program_id(1)
        dpre_ext = _with_next(d_ref[...], dn_ref[...], i, nt)
        dx, dws, db = _conv_bwd_core(dpre_ext, x_ref[...], w_ref, kw, tm)
        dx_ref[...] = dx.astype(BF16)
        _acc_rows(i, dw_ref, db_ref, dws, db)

    wspec = pl.BlockSpec((kw, cb), lambda j, i: (0, j + off))
    bspec = pl.BlockSpec((1, cb), lambda j, i: (0, j + off))
    return pl.pallas_call(
        body, name=name, grid=(dpre.shape[1] // cb, nt),
        in_specs=[_main_spec(tm, cb), _next_spec(tm, cb, n), _main_spec(tm, cb, off), wspec]
        + [pl.BlockSpec(memory_space=pl.ANY)] * n_alias,
        out_specs=[_main_spec(tm, cb, off), wspec, bspec],
        out_shape=[jax.ShapeDtypeStruct((n, cdim), BF16), jax.ShapeDtypeStruct((kw, cdim), F32),
                   jax.ShapeDtypeStruct((1, cdim), F32)],
        input_output_aliases={4 + k: k for k in range(n_alias)},
        compiler_params=_cparams("parallel", "arbitrary"))(dpre, dpre, x, w, *(into or ()))


def _ffn_in_act_fwd(u, w_in, w, b):
    n = u.shape[0]
    kw = w.shape[0]
    tm = _row_tile(n, 832)
    cb = _col_tile(D_FF, 256)
    nc = D_FF // cb
    nt = n // tm

    def body(u_ref, wu_in_ref, wg_in_ref, wu_ref, wg_ref, bu_ref, bg_ref,
             xu_ref, xg_ref, hu_ref, hg_ref, act_ref, acc_scr, halo_scr):
        j, i = pl.program_id(0), pl.program_id(1)

        @pl.when((j == 0) & (i == 0))
        def _():
            acc_scr[...] = jnp.zeros_like(acc_scr)
            halo_scr[...] = jnp.zeros_like(halo_scr)

        ub = u_ref[...]
        new = [_dot(ub, wu_in_ref[...]), _dot(ub, wg_in_ref[...])]
        hid = []
        for half, (x_ref, w_ref, b_ref) in enumerate(((xu_ref, wu_ref, bu_ref), (xg_ref, wg_ref, bg_ref))):
            prev = acc_scr[half]
            xx = jnp.concatenate([jnp.where(i >= 2, halo_scr[half], 0.0), prev], axis=0)
            acc = jnp.broadcast_to(b_ref[...], (tm, cb))
            for k in range(kw):
                acc = acc + w_ref[k:k + 1, :] * _back(xx, kw - 1 - k, tm)
            x_ref[...] = prev.astype(BF16)
            hid.append(acc)
            halo_scr[half] = prev[tm - SUBLANES:, :]
            acc_scr[half] = new[half]
        hu_ref[...] = hid[0].astype(BF16)
        hg_ref[...] = hid[1].astype(BF16)
        act_ref[...] = (_silu(hid[1]) * hid[0]).astype(BF16)

    def wspec(off):
        return pl.BlockSpec((kw, cb), lambda j, i: (0, j + off))

    def bspec(off):
        return pl.BlockSpec((1, cb), lambda j, i: (0, j + off))

    def in_w(off):
        return pl.BlockSpec((w_in.shape[0], cb), lambda j, i: (0, j + off))

    out = pl.BlockSpec((tm, cb), lambda j, i: (jnp.maximum(i - 1, 0), j))
    bf16_out = jax.ShapeDtypeStruct((n, D_FF), BF16)
    return pl.pallas_call(
        body, name="ffn_in_act_fwd", grid=(nc, nt + 1),
        in_specs=[pl.BlockSpec((tm, u.shape[1]), lambda j, i: (jnp.minimum(i, nt - 1), 0)), in_w(0), in_w(nc),
                  wspec(0), wspec(nc), bspec(0), bspec(nc)],
        out_specs=[out] * 5,
        out_shape=[bf16_out] * 5,
        scratch_shapes=[pltpu.VMEM((2, tm, cb), F32), pltpu.VMEM((2, SUBLANES, cb), F32)],
        compiler_params=_cparams("arbitrary", "arbitrary"))(u, w_in, w_in, w, w, b, b)


def _ffn_out_act_bwd(dh, w_out, hu, hg, x_up, x_gate, w):
    n = x_up.shape[0]
    kw = w.shape[0]
    tm = _row_tile(n, 832)
    cb = _col_tile(D_FF, 256)
    nc = D_FF // cb
    nt = n // tm

    def body(dh_ref, wo_ref, hu_ref, hun_ref, hg_ref, hgn_ref, xu_ref, xg_ref, wu_ref, wg_ref,
             dxu_ref, dxg_ref, dwu_ref, dwg_ref, dbu_ref, dbg_ref, acc_scr, halo_scr):
        j, i = pl.program_id(0), pl.program_id(1)

        @pl.when((j == 0) & (i == 0))
        def _():
            acc_scr[...] = jnp.zeros_like(acc_scr)
            halo_scr[...] = jnp.zeros_like(halo_scr)

        tile = jnp.maximum(nt - 1 - i, 0)
        row = tile * tm + lax.broadcasted_iota(jnp.int32, (tm, 1), 0)
        new = jnp.where(row >= PAD, _dot_nt(dh_ref[...].astype(BF16), wo_ref[...]), 0.0)
        prev = jnp.where(i >= 1, acc_scr[...], 0.0)
        dact_e = jnp.concatenate([prev, jnp.where(i >= 2, halo_scr[...], 0.0)], axis=0)
        last = nt - i >= nt - 1
        up_e = jnp.concatenate([hu_ref[...].astype(F32), jnp.where(last, 0.0, hun_ref[...].astype(F32))], axis=0)
        gate_e = jnp.concatenate([hg_ref[...].astype(F32), jnp.where(last, 0.0, hgn_ref[...].astype(F32))], axis=0)
        halo_scr[...] = prev[:BF16_ROWS, :]
        acc_scr[...] = new
        sg = _sigmoid(gate_e)
        dup_e = dact_e * (gate_e * sg)
        dgate_e = dact_e * up_e * (sg * (1.0 + gate_e * (1.0 - sg)))
        dx, dws, db = _conv_bwd_core(dup_e, xu_ref[...], wu_ref, kw, tm)
        dxu_ref[...] = dx.astype(BF16)
        _acc_rows(i, dwu_ref, dbu_ref, dws, db)
        dx, dws, db = _conv_bwd_core(dgate_e, xg_ref[...], wg_ref, kw, tm)
        dxg_ref[...] = dx.astype(BF16)
        _acc_rows(i, dwg_ref, dbg_ref, dws, db)

    def done_tile(i):
        return jnp.minimum(nt - i, nt - 1)

    halo_blocks = tm // BF16_ROWS
    main = pl.BlockSpec((tm, cb), lambda j, i: (done_tile(i), j))
    nxt = pl.BlockSpec((BF16_ROWS, cb),
                       lambda j, i: (jnp.minimum((done_tile(i) + 1) * halo_blocks, n // BF16_ROWS - 1), j))
    wspec0 = pl.BlockSpec((kw, cb), lambda j, i: (0, j))
    wspec1 = pl.BlockSpec((kw, cb), lambda j, i: (0, j + nc))
    bspec = pl.BlockSpec((1, cb), lambda j, i: (0, j))
    return pl.pallas_call(
        body, name="ffn_out_act_bwd", grid=(nc, nt + 1),
        in_specs=[pl.BlockSpec((tm, dh.shape[1]), lambda j, i: (jnp.maximum(nt - 1 - i, 0), 0)),
                  pl.BlockSpec((cb, w_out.shape[1]), lambda j, i: (j, 0)),
                  main, nxt, main, nxt, main, main, wspec0, wspec1],
        out_specs=[main, main, wspec0, wspec0, bspec, bspec],
        out_shape=[jax.ShapeDtypeStruct((n, D_FF), BF16), jax.ShapeDtypeStruct((n, D_FF), BF16),
                   jax.ShapeDtypeStruct((kw, D_FF), F32), jax.ShapeDtypeStruct((kw, D_FF), F32),
                   jax.ShapeDtypeStruct((1, D_FF), F32), jax.ShapeDtypeStruct((1, D_FF), F32)],
        scratch_shapes=[pltpu.VMEM((tm, cb), F32), pltpu.VMEM((BF16_ROWS, cb), F32)],
        compiler_params=_cparams("arbitrary", "arbitrary"))(dh, w_out, hu, hu, hg, hg, x_up, x_gate, w, w)


def _ssd_prep(pxs_ref, pb_ref, pc_ref, dtr_ref, dtb_ref, alog_ref, c):
    xs = _silu(pxs_ref[...])
    bm = _silu(pb_ref[...])
    cm = _silu(pc_ref[...])
    return (xs, bm, cm) + _ssd_decay(dtr_ref, dtb_ref, alog_ref, c)


def _ssd_decay(dtr_ref, dtb_ref, alog_ref, c):
    row =lax.broadcasted_iota(jnp.int32, (BLOCK, 1), 0) + c * BLOCK
    valid = (row >= PAD).astype(F32)
    dtr = dtr_ref[...] + dtb_ref[...]
    dt = _softplus(dtr) * valid
    a = -jnp.exp(alog_ref[...])
    lam = dt * a
    ri = lax.broadcasted_iota(jnp.int32, (BLOCK, BLOCK), 0)
    ci = lax.broadcasted_iota(jnp.int32, (BLOCK, BLOCK), 1)
    causal = ci <= ri
    cs = _sel_dot(causal.astype(BF16), lam)
    return valid, dtr, dt, a, lam, cs, causal


def _head_cols(r):
    return slice(SSD_HEADDIM * r, SSD_HEADDIM * (r + 1))


def _ssd_specs(nc, rev):
    def cidx(c):
        return nc - 1 - c if rev else c

    xs = pl.BlockSpec((BLOCK, SSD_GW), lambda g, c: (cidx(c), g))
    bspec = pl.BlockSpec((BLOCK, SSD_STATE), lambda g, c: (cidx(c), SSD_INNER // SSD_STATE + g))
    cspec = pl.BlockSpec((BLOCK, SSD_STATE), lambda g, c: (cidx(c), (SSD_INNER + SSD_BC) // SSD_STATE + g))
    lane = pl.BlockSpec((BLOCK, LANES), lambda g, c: (cidx(c), g))
    vec = pl.BlockSpec((1, LANES), lambda g, c: (0, g))
    wide_vec = pl.BlockSpec((1, SSD_GW), lambda g, c: (0, g))
    hsave = pl.BlockSpec((1, 1, SSD_GW, SSD_STATE), lambda g, c: (cidx(c), g, 0, 0))
    return xs, bspec, cspec, lane, vec, wide_vec, hsave


def _head_spread_matrix():
    r = lax.broadcasted_iota(jnp.int32, (LANES, SSD_GW), 0)
    col = lax.broadcasted_iota(jnp.int32, (LANES, SSD_GW), 1)
    return (col // SSD_HEADDIM == r).astype(BF16)


def _const_spec(shape):
    return pl.BlockSpec(shape, lambda g, c: (0,) * len(shape))


def _spread_heads(per_head, e_ref):
    wide = _dot_sel(jnp.concatenate(per_head, axis=0), e_ref[...])
    return [wide[BLOCK * k:BLOCK * (k + 1)] for k in range(len(per_head))]


def _call_with_side(body, side, *, name, grid, in_specs, out_specs, out_shape, scratch_shapes, args,
                    semantics=("parallel", "arbitrary")):
    if side is None:
        outs = pl.pallas_call(body, name=name, grid=grid, in_specs=in_specs, out_specs=out_specs, out_shape=out_shape,
                              scratch_shapes=scratch_shapes, compiler_params=_cparams(*semantics))(*args)
        return outs, []
    n_in, n_out, n_scr, n_side = len(in_specs), len(out_specs), len(scratch_shapes), len(side.arrays)

    def body_with_side(*refs):
        ins, rest = refs[:n_in + n_side], refs[n_in + n_side:]
        outs, scratch = rest[:n_out + n_side], rest[n_out + n_side:]
        side_refs = (ins[n_in:], outs[n_out:], scratch[n_scr:])
        ids = [pl.program_id(k) for k in range(len(grid))]
        inner_first = functools.reduce(jnp.logical_and, [i == 0 for i in ids[1:]], True)

        @pl.when((ids[0] == 0) & inner_first)
        def _():
            side.phases[0](*side_refs)

        body(*ins[:n_in], *outs[:n_out], *scratch[:n_scr])

        @pl.when((ids[0] == grid[0] // 2) & inner_first)
        def _():
            side.phases[1](*side_refs)

        @pl.when(functools.reduce(jnp.logical_and, [i == n - 1 for i, n in zip(ids, grid)]))
        def _():
            side.phases[2](*side_refs)

    any_spec = pl.BlockSpec(memory_space=pl.ANY)
    outs = pl.pallas_call(
        body_with_side, name=name, grid=grid, in_specs=list(in_specs) + [any_spec] * n_side,
        out_specs=list(out_specs) + [any_spec] * n_side, out_shape=list(out_shape) + list(side.out_shape),
        scratch_shapes=list(scratch_shapes) + list(side.scratch_shapes),
        compiler_params=_cparams(*["arbitrary"] * len(grid)))(*args, *side.arrays)
    return outs[:n_out], outs[n_out:]


def _ssd_fwd(pre, dt_raw, z, dtb, alog, dskip_w, norm_w, side=None):
    n = pre.shape[0]
    nc = n // BLOCK
    xs_s, b_s, c_s, lane_s, vec_s, wide_s, hs_s = _ssd_specs(nc, False)

    def body(pxs_ref, pb_ref, pc_ref, dtr_ref, z_ref, dtb_ref, alog_ref, dskw_ref, nw_ref, e_ref,
             y_ref, yn_ref, hs_ref, h_scr):
        c = pl.program_id(1)

        @pl.when(c == 0)
        def _():
            h_scr[...] = jnp.zeros_like(h_scr)

        xs, bm, cm, _, _, dt, _, _, cs, causal = _ssd_prep(pxs_ref, pb_ref, pc_ref, dtr_ref, dtb_ref, alog_ref, c)
        cst = cs.T
        cs_last = cs[BLOCK - 1:BLOCK, :]
        dt_w, ecs_w, dec_w = _spread_heads([dt, jnp.exp(cs), jnp.exp(cs_last - cs)], e_ref)
        xdt = xs * dt_w
        bmb = bm.astype(BF16)
        cmb = cm.astype(BF16)
        cb = _dot_nt(cmb, bmb)
        hg = h_scr[...]
        hs_ref[0, 0] = hg
        y = _dot_nt(cmb, hg.astype(BF16)) * ecs_w + dskw_ref[...] * xs
        first = lax.broadcasted_iota(jnp.int32, (BLOCK, LANES), 1) < SSD_HEADDIM
        diag = []
        for j in range(SSD_HPG // 2):
            xp = xdt[:, LANES * j:LANES * (j + 1)].astype(BF16)
            res = []
            for r in (2 * j, 2 * j + 1):
                lm = jnp.exp(jnp.where(causal, cs[:, r:r + 1] - cst[r:r + 1, :], NEG))
                res.append(_dot((cb * lm).astype(BF16), xp))
            diag.append(jnp.where(first, res[0], res[1]))
        y = y + jnp.concatenate(diag, axis=1)
        st = _dot_tn((xdt * dec_w).astype(BF16), bmb)
        eh = jnp.exp(cs_last)
        for r in range(SSD_HPG):
            rows = _head_cols(r)
            h_scr[rows, :] = hg[rows, :] * eh[:, r:r + 1] + st[rows, :]
        y_ref[...] = y
        gts = y * _silu(z_ref[...])
        rr = lax.rsqrt(jnp.mean(gts * gts, axis=-1, keepdims=True) + EPS)
        yn_ref[...] = (gts * rr * nw_ref[...]).astype(BF16)

    return _call_with_side(
        body, side, name="ssd_fwd", grid=(SSD_GROUPS, nc),
        in_specs=[xs_s, b_s, c_s, lane_s, xs_s, vec_s, vec_s, wide_s, wide_s, _const_spec((LANES, SSD_GW))],
        out_specs=[xs_s, xs_s, hs_s],
        out_shape=[jax.ShapeDtypeStruct((n, SSD_INNER), F32), jax.ShapeDtypeStruct((n, SSD_INNER), BF16),
                   jax.ShapeDtypeStruct((nc, SSD_GROUPS, SSD_GW, SSD_STATE), F32)],
        scratch_shapes=[pltpu.VMEM((SSD_GW, SSD_STATE), F32)],
        args=(pre, pre, pre, dt_raw, z, dtb, alog, dskip_w, norm_w, _head_spread_matrix()))


def _lane_put(acc, col, r):
    lane = lax.broadcasted_iota(jnp.int32, acc.shape, 1)
    return jnp.where(lane == r, col, acc)


def _ssd_bwd(dyn, y, z, pre, dt_raw, hsave, dtb, alog, dskip_w, norm_w, side=None):
    n = pre.shape[0]
    nc = n // BLOCK
    spread = _head_spread_matrix()
    xs_s, b_s, c_s, lane_s, vec_s, wide_s, hs_s = _ssd_specs(nc, True)
    bc_out =pl.BlockSpec((BLOCK, SSD_STATE), lambda g, c: (nc - 1 - c, g))

    def body(dyn_ref, y_ref, z_ref, pxs_ref, pb_ref, pc_ref, dtr_ref, hs_ref, dtb_ref, alog_ref, dskw_ref, nw_ref,
             e_ref, r_ref,
             dz_ref, dxs_ref, dbm_ref, dcm_ref, ddt_ref, dnw_ref, ddtb_ref, dalog_ref, ddsk_ref, g_scr):
        step = pl.program_id(1)
        c = nc - 1 - step

        @pl.when(step == 0)
        def _():
            g_scr[...] = jnp.zeros_like(g_scr)

        pxs, pb, pc = pxs_ref[...], pb_ref[...], pc_ref[...]
        sx, sb, sc = _sigmoid(pxs), _sigmoid(pb), _sigmoid(pc)
        xs, bm, cm = pxs * sx, pb * sb, pc * sc
        valid, dtr, dt, a, lam, cs, causal = _ssd_decay(dtr_ref, dtb_ref, alog_ref, c)
        cst = cs.T
        cs_last = cs[BLOCK - 1:BLOCK, :]
        bmb = bm.astype(BF16)
        cmb = cm.astype(BF16)
        cb = _dot_nt(cmb, bmb)
        hg = hs_ref[0, 0]
        hgb = hg.astype(BF16)
        yoff = _dot_nt(cmb, hgb)
        gn = g_scr[...]
        gnb = gn.astype(BF16)

        zv = z_ref[...]
        yv = y_ref[...]
        sgz = _sigmoid(zv)
        sz = zv * sgz
        gts = yv * sz
        rr = lax.rsqrt(jnp.mean(gts * gts, axis=-1, keepdims=True) + EPS)
        xh = gts * rr
        dynv = dyn_ref[...]
        gg = dynv * nw_ref[...]
        dgts = rr * (gg - xh * jnp.mean(gg * xh, axis=-1, keepdims=True))
        dnw = jnp.sum(dynv * xh, axis=0, keepdims=True)
        dy = dgts * sz
        dz_ref[...] = (dgts * yv * (sgz * (1.0 + zv * (1.0 - sgz)))).astype(BF16)

        ecs = jnp.exp(cs)
        dec = jnp.exp(cs_last - cs)
        eh = jnp.exp(cs_last)
        dt_w, ecs_w, dec_w = _spread_heads([dt, ecs, dec], e_ref)
        red_m = r_ref[...]

        def head_sums(v):
            return _dot_sel(v, red_m, terms=2)

        xdt = xs * dt_w
        q_all = _dot_nt(bmb, gnb)
        w_all = (dy * ecs_w).astype(BF16)
        e_hl = head_sums(q_all * xdt) * dec
        dcs_col = head_sums(dy * yoff) * ecs - e_hl
        gh = jnp.zeros((1, LANES), F32)
        prod = gn * hg
        for r in range(SSD_HPG):
            gh = _lane_put(gh, _sum_all(prod[_head_cols(r), :]), r)
        dcs_last = jnp.sum(e_hl, axis=0, keepdims=True) + eh * gh
        ddsk = jnp.sum(head_sums(dy * xs), axis=0, keepdims=True)
        cbt = _dot_nt(bmb, cmb)
        lane = lax.broadcasted_iota(jnp.int32, (BLOCK, LANES), 1)
        first = lane < SSD_HEADDIM
        causal_t = lax.broadcasted_iota(jnp.int32, (BLOCK, BLOCK), 1) >= lax.broadcasted_iota(
            jnp.int32, (BLOCK, BLOCK), 0)
        sub = lax.broadcasted_iota(jnp.int32, (SUBLANES, BLOCK), 0)
        dcs_row = jnp.zeros((SUBLANES, BLOCK), F32)
        dcb = jnp.zeros((BLOCK, BLOCK), F32)
        dxdt_pairs = []
        for j in range(SSD_HPG // 2):
            tile = slice(LANES * j, LANES * (j + 1))
            dy_p = dy[:, tile]
            dyb = dy_p.astype(BF16)
            xdtb = xdt[:, tile].astype(BF16)
            res = []
            for half, r in enumerate((2 * j, 2 * j + 1)):
                csc, csr = cs[:, r:r + 1], cst[r:r + 1, :]
                lm = jnp.exp(jnp.where(causal, csc - csr, NEG))
                lmt = jnp.exp(jnp.where(causal_t, csr - csc, NEG))
                keep = first if half == 0 else jnp.logical_not(first)
                gm = _dot_nt(jnp.where(keep, dy_p, 0.0).astype(BF16), xdtb) * lm
                dcb = dcb + gm
                mm_ = gm * cb
                dcs_col = dcs_col + jnp.where(lane == r, jnp.sum(mm_, axis=1, keepdims=True), 0.0)
                dcs_row = jnp.where(sub == r, jnp.sum(mm_, axis=0, keepdims=True), dcs_row)
                res.append(_dot((cbt * lmt).astype(BF16), dyb))
            dxdt_pairs.append(jnp.where(first, res[0], res[1]))
        dxdt = jnp.concatenate(dxdt_pairs, axis=1) + q_all * dec_w
        ddt_x = head_sums(dxdt * xs)
        dxs = dxdt * dt_w + dskw_ref[...] * dy
        dcbb = dcb.astype(BF16)
        dcm = _dot(w_all, hgb) + _dot(dcbb, bmb)
        dbm = _dot((xdt * dec_w).astype(BF16), gnb) + _dot_tn(dcbb, cmb)
        dh_off = _dot_tn(w_all, cmb)
        for r in range(SSD_HPG):
            rows = _head_cols(r)
            g_scr[rows, :] = gn[rows, :] * eh[:, r:r + 1] + dh_off[rows, :]

        pad_rows = jnp.zeros((BLOCK - SUBLANES, BLOCK), F32)
        dcs = dcs_col - jnp.concatenate([dcs_row, pad_rows], axis=0).T
        rsel = lax.broadcasted_iota(jnp.int32, (BLOCK, LANES), 0)
        dcs = dcs + jnp.where(rsel == BLOCK - 1, dcs_last, 0.0)
        ri = lax.broadcasted_iota(jnp.int32, (BLOCK, BLOCK), 0)
        ci = lax.broadcasted_iota(jnp.int32, (BLOCK, BLOCK), 1)
        dlam = _sel_dot((ci >= ri).astype(BF16), dcs)
        head = lane < SSD_HPG
        ddt = dlam * a + ddt_x
        ddtr = jnp.where(head, ddt * _sigmoid(dtr) * valid, 0.0)
        ddt_ref[...] = ddtr.astype(BF16)
        dalog = jnp.sum(jnp.where(head, dlam * lam, 0.0), axis=0, keepdims=True)
        ddtb = jnp.sum(ddtr, axis=0, keepdims=True)

        dxs_ref[...] = dxs * (sx * (1.0 + pxs * (1.0 - sx)))
        dbm_ref[...] = dbm * (sb * (1.0 + pb * (1.0 - sb)))
        dcm_ref[...] = dcm * (sc * (1.0 + pc * (1.0 - sc)))

        @pl.when(step == 0)
        def _():
            dnw_ref[...] = dnw
            ddtb_ref[...] = ddtb
            dalog_ref[...] = dalog
            ddsk_ref[...] = ddsk

        @pl.when(step > 0)
        def _():
            dnw_ref[...] += dnw
            ddtb_ref[...] += ddtb
            dalog_ref[...] += dalog
            ddsk_ref[...] += ddsk

    return _call_with_side(
        body, side, name="ssd_bwd", grid=(SSD_GROUPS, nc),
        in_specs=[xs_s, xs_s, xs_s, xs_s, b_s, c_s, lane_s, hs_s, vec_s, vec_s, wide_s, wide_s,
                  _const_spec((LANES, SSD_GW)), _const_spec((SSD_GW, LANES))],
        out_specs=[xs_s, xs_s, bc_out, bc_out, lane_s, wide_s, vec_s, vec_s, vec_s],
        out_shape=[jax.ShapeDtypeStruct((n, SSD_INNER), BF16), jax.ShapeDtypeStruct((n, SSD_INNER), F32),
                   jax.ShapeDtypeStruct((n, SSD_BC), F32), jax.ShapeDtypeStruct((n, SSD_BC), F32),
                   jax.ShapeDtypeStruct((n, DT_W), BF16), jax.ShapeDtypeStruct((1, SSD_INNER), F32),
                   jax.ShapeDtypeStruct((1, DT_W), F32), jax.ShapeDtypeStruct((1, DT_W), F32),
                   jax.ShapeDtypeStruct((1, DT_W), F32)],
        scratch_shapes=[pltpu.VMEM((SSD_GW, SSD_STATE), F32)],
        args=(dyn, y, z, pre, pre, pre, dt_raw, hsave, dtb, alog, dskip_w, norm_w, spread, spread.T))


def _bucket_table():
    def bucket(dist):
        d = np.maximum(dist, 0)
        half = REL_BUCKETS // 2
        big = half + (np.log(np.maximum(d, half).astype(np.float32) / np.float32(half))
                      / np.float32(math.log(REL_MAX_DIST / half)) * np.float32(REL_BUCKETS - half)).astype(np.int32)
        return np.where(d < half, d, np.minimum(big, REL_BUCKETS - 1)).astype(np.int32)

    l = np.arange(BLOCK)[None, :]
    band = bucket(l + BLOCK - np.arange(2 * BLOCK)[:, None])
    j = np.arange(BLOCK)[:, None]
    tables = [np.concatenate([bucket(v * BLOCK + l - j), band], axis=0) for v in range(3)]
    return np.concatenate([t.reshape(-1) for t in tables])


def _onehot_t():
    buckets = jnp.asarray(_bucket_table())
    return (buckets[None, :] == jnp.arange(REL_BUCKETS, dtype=jnp.int32)[:, None]).astype(F32)


def _bias_tables(rel_t, onehot_t, side=None):
    def body(r_ref, oh_ref, o_ref):
        o_ref[...] = jnp.dot(r_ref[...], oh_ref[...], precision=HIGHEST, preferred_element_type=F32)

    outs, carried = _call_with_side(
        body, side, name="bias_tables", grid=(NT_ALL // NT_TILE,),
        in_specs=[pl.BlockSpec((ATT_HEADS, REL_BUCKETS), lambda i: (0, 0)),
                  pl.BlockSpec((REL_BUCKETS, NT_TILE), lambda i: (0, i))],
        out_specs=[pl.BlockSpec((ATT_HEADS, NT_TILE), lambda i: (0, i))],
        out_shape=[jax.ShapeDtypeStruct((ATT_HEADS, NT_ALL), F32)], scratch_shapes=[], args=(rel_t, onehot_t),
        semantics=("parallel",))
    return outs[0], carried


def _bias_grad(dtab, onehot_t):
    def body(d_ref, oh_ref, o_ref):
        i = pl.program_id(0)
        p = lax.dot_general(d_ref[...], oh_ref[...], (((1,), (1,)), ((), ())), precision=HIGHEST,
                            preferred_element_type=F32)

        @pl.when(i == 0)
        def _():
            o_ref[...] = p

        @pl.when(i > 0)
        def _():
            o_ref[...] += p

    return pl.pallas_call(
        body, name="bias_grad", grid=(NT_ALL // NT_TILE,),
        in_specs=[pl.BlockSpec((ATT_HEADS, NT_TILE), lambda i: (0, i)),
                  pl.BlockSpec((REL_BUCKETS, NT_TILE), lambda i: (0, i))],
        out_specs=pl.BlockSpec((ATT_HEADS, REL_BUCKETS), lambda i: (0, 0)),
        out_shape=jax.ShapeDtypeStruct((ATT_HEADS, REL_BUCKETS), F32),
        compiler_params=_cparams("arbitrary"))(dtab, onehot_t)


def _att_mask_t(n, copies):
    far = 4 * BLOCK
    kk = lax.broadcasted_iota(jnp.int32, (N_KEYS, copies * BLOCK), 0)
    li = lax.broadcasted_iota(jnp.int32, (N_KEYS, copies * BLOCK), 1) & (BLOCK - 1)
    meta_ok = (kk >= PAD) & (kk < BLOCK) & (li + jnp.where(n >= 1, far, 0) >= kk)
    prev_ok = (kk >= BLOCK) & (kk < 2 * BLOCK) & (kk - BLOCK > li + jnp.where(n >= 2, 0, far))
    cur_ok = (kk >= 2 * BLOCK) & (kk - 2 * BLOCK <= li - jnp.where(n >= 1, 0, far))
    return meta_ok | prev_ok | cur_ok


def _att_kv(meta_ref, prev_ref, cur_ref):
    kv = jnp.concatenate([meta_ref[...], prev_ref[...], cur_ref[...]], axis=0)
    first = lax.broadcasted_iota(jnp.int32, (N_KEYS, LANES), 1) < ATT_HEADDIM
    out = []
    for pair in (kv[:, :LANES], kv[:, LANES:]):
        swapped = pltpu.roll(pair, ATT_HEADDIM, 1)
        out.append([jnp.where(first, pair, swapped).astype(BF16), jnp.where(first, swapped, pair).astype(BF16)])
    return out[0], out[1]


def _split_heads(x_pair, first):
    return jnp.concatenate([jnp.where(first, x_pair, 0.0), jnp.where(first, 0.0, x_pair)], axis=0).astype(BF16)


def _att_probs_t(qm2, k_dup, t_ref, j, mask2, sink_ref):
    scale = ATT_HEADDIM ** -0.5
    bias2 = jnp.concatenate([t_ref[0, 2 * j], t_ref[0, 2 * j + 1]], axis=1)
    second = lax.broadcasted_iota(jnp.int32, (1, 2 * BLOCK), 1) >= BLOCK
    sink2 = jnp.where(second, sink_ref[0:1, 2 * j + 1:2 * j + 2], sink_ref[0:1, 2 * j:2 * j + 1])
    s_t = jnp.where(mask2, _dot_nt(k_dup, qm2) * scale + bias2, NEG)
    mx = jnp.maximum(jnp.max(s_t, axis=0, keepdims=True), sink2)
    p_t = jnp.exp(s_t - mx)
    p_s = jnp.exp(sink2 - mx)
    inv = 1.0 / (jnp.sum(p_t, axis=0, keepdims=True) + p_s)
    return p_t * inv, p_s * inv


def _att_specs(nb, rev):
    def nidx(i):
        return nb - 1 - i if rev else i

    kvb = ATT_Q // (2 * ATT_KV)
    q_s = pl.BlockSpec((BLOCK, ATT_Q), lambda i: (nidx(i), 0))
    cur = pl.BlockSpec((BLOCK, 2 * ATT_KV), lambda i: (nidx(i), kvb))
    prev = pl.BlockSpec((BLOCK, 2 * ATT_KV), lambda i: (jnp.maximum(nidx(i) - 1, 0), kvb))
    meta = pl.BlockSpec((BLOCK, 2 * ATT_KV), lambda i: (0, kvb))
    table = pl.BlockSpec((1, ATT_HEADS, N_KEYS, BLOCK), lambda i: (jnp.minimum(nidx(i), 2), 0, 0, 0))
    sink = pl.BlockSpec((1, LANES), lambda i: (0, 0))
    return q_s, cur, prev, meta, table, sink


def _attn_fwd(qkv, tables, sinks):
    n = qkv.shape[0]
    nb = n // BLOCK
    q_s, cur_s, prev_s, meta_s, t_s, sink_s = _att_specs(nb, False)

    def body(q_ref, cur_ref, prev_ref, meta_ref, t_ref, sink_ref, o_ref):
        blk = pl.program_id(0)
        mask_t = _att_mask_t(blk, 1)
        k_dup, v_dup = _att_kv(meta_ref, prev_ref, cur_ref)
        v_dup_t = [v.T for v in v_dup]
        first = lax.broadcasted_iota(jnp.int32, (BLOCK, LANES), 1) < ATT_HEADDIM
        top = lax.broadcasted_iota(jnp.int32, (LANES, BLOCK), 0) < ATT_HEADDIM
        scale = ATT_HEADDIM ** -0.5
        for j in range(ATT_HEADS // 2):
            kh = 2 * j // ATT_GQ
            tile = slice(LANES * j, LANES * (j + 1))
            q_p = q_ref[:, tile]
            res = []
            for half, h in enumerate((2 * j, 2 * j + 1)):
                qm = jnp.where(first if half == 0 else jnp.logical_not(first), q_p, 0.0).astype(BF16)
                sink = sink_ref[0:1, h:h + 1]
                s_t = jnp.where(mask_t, _dot_nt(k_dup[kh], qm) * scale + t_ref[0, h], NEG)
                mx = jnp.maximum(jnp.max(s_t, axis=0, keepdims=True), sink)
                p_t = jnp.exp(s_t - mx)
                inv = 1.0 / (jnp.sum(p_t, axis=0, keepdims=True) + jnp.exp(sink - mx))
                res.append(_dot(v_dup_t[kh], (p_t * inv).astype(BF16)))
            o_ref[:, tile] = jnp.where(top, res[0], res[1]).T.astype(BF16)

    return pl.pallas_call(
        body, name="attn_fwd", grid=(nb,),
        in_specs=[q_s, cur_s, prev_s, meta_s, t_s, sink_s],
        out_specs=q_s,
        out_shape=jax.ShapeDtypeStruct((n, ATT_Q), BF16),
        compiler_params=_cparams("parallel"))(qkv, qkv, qkv, qkv, tables, sinks)


def _attn_bwd(datt, qkv, tables, sinks):
    n = qkv.shape[0]
    nb = n // BLOCK
    q_s, cur_s, prev_s, meta_s, t_s, sink_s = _att_specs(nb, True)
    dqkv_s = pl.BlockSpec((BLOCK, ATT_Q + 2 * ATT_KV), lambda i: (nb - 1 - i, 0))
    scale = ATT_HEADDIM ** -0.5

    def body(do_ref, q_ref, cur_ref, prev_ref, meta_ref, t_ref, sink_ref,
             dqkv_ref, dt_ref, dsink_ref, carry_scr, meta_scr):
        step = pl.program_id(0)
        blk = nb - 1 - step
        mask2 = _att_mask_t(blk, 2)
        k_dup, v_dup = _att_kv(meta_ref, prev_ref, cur_ref)
        k_dup_t = [k.T for k in k_dup]

        @pl.when(step == 0)
        def _():
            carry_scr[...] = jnp.zeros_like(carry_scr)
            meta_scr[...] = jnp.zeros_like(meta_scr)
            dsink_ref[...] = jnp.zeros_like(dsink_ref)

        @pl.when((step == 0) | (blk <= 1))
        def _():
            dt_ref[...] = jnp.zeros_like(dt_ref)

        first = lax.broadcasted_iota(jnp.int32, (BLOCK, LANES), 1) < ATT_HEADDIM
        top = lax.broadcasted_iota(jnp.int32, (LANES, BLOCK), 0) < ATT_HEADDIM
        first_k = lax.broadcasted_iota(jnp.int32, (N_KEYS, LANES), 1) < ATT_HEADDIM
        dsink = jnp.zeros((1, LANES), F32)
        dk_acc = [None] * ATT_KV_HEADS
        dv_acc = [None] * ATT_KV_HEADS
        for j in range(ATT_HEADS // 2):
            kh = 2 * j // ATT_GQ
            tile = slice(LANES * j, LANES * (j + 1))
            qm2 = _split_heads(q_ref[:, tile], first)
            dom2 = _split_heads(do_ref[:, tile], first)
            p_t, p_s = _att_probs_t(qm2, k_dup[kh], t_ref, j, mask2, sink_ref)
            dp_t = _dot_nt(v_dup[kh], dom2)
            delta = jnp.sum(p_t * dp_t, axis=0, keepdims=True)
            ds_t = p_t * (dp_t - delta)
            sink_terms = p_s * delta
            for half in range(2):
                cols = slice(BLOCK * half, BLOCK * (half + 1))
                dsink = _lane_put(dsink, -jnp.sum(sink_terms[:, cols], axis=1, keepdims=True), 2 * j + half)
                dt_ref[0, 2 * j + half] += ds_t[:, cols]
            ds_tb = ds_t.astype(BF16)
            dq_t = _dot(k_dup_t[kh], ds_tb)
            dqkv_ref[:, tile] = (jnp.where(top, dq_t[:, :BLOCK], dq_t[:, BLOCK:]).T * scale).astype(BF16)
            dk_part, dv_part = _dot(ds_tb, qm2), _dot(p_t.astype(BF16), dom2)
            dk_acc[kh] = dk_part if dk_acc[kh] is None else dk_acc[kh] + dk_part
            dv_acc[kh] = dv_part if dv_acc[kh] is None else dv_acc[kh] + dv_part
        dsink_ref[...] += dsink
        folded = [a + pltpu.roll(a, ATT_HEADDIM, 1) for a in dk_acc + dv_acc]
        dkv = jnp.concatenate([jnp.where(first_k, folded[0], folded[1]) * scale,
                               jnp.where(first_k, folded[2], folded[3])], axis=1)
        meta_scr[...] += dkv[:BLOCK, :]
        own = dkv[2 * BLOCK:, :] + carry_scr[...]
        carry_scr[...] = dkv[BLOCK:2 * BLOCK, :]

        @pl.when(blk > 0)
        def _():
            dqkv_ref[:, ATT_Q:] = own.astype(BF16)

        @pl.when(blk == 0)
        def _():
            dqkv_ref[:, ATT_Q:] = (own + meta_scr[...]).astype(BF16)

    return pl.pallas_call(
        body, name="attn_bwd", grid=(nb,),
        in_specs=[q_s, q_s, cur_s, prev_s, meta_s, t_s, sink_s],
        out_specs=[dqkv_s, t_s, sink_s],
        out_shape=[jax.ShapeDtypeStruct((n, ATT_Q + 2 * ATT_KV), BF16),
                   jax.ShapeDtypeStruct((3, ATT_HEADS, N_KEYS, BLOCK), F32),
                   jax.ShapeDtypeStruct((1, LANES), F32)],
        scratch_shapes=[pltpu.VMEM((BLOCK, 2 * ATT_KV), F32), pltpu.VMEM((BLOCK, 2 * ATT_KV), F32)],
        compiler_params=_cparams("arbitrary"))(datt, qkv, qkv, qkv, qkv, tables, sinks)


def _merge_out_fwd(gates, y_ssd, y_att, gate_b, w_out, h):
    n = gates.shape[0]
    tm = _row_tile(n, 416)

    def body(gs_ref, ga_ref, ys_ref, ya_ref, gb_ref, w_ref, h_ref, m_ref, o_ref):
        merged = (_sigmoid(gs_ref[...] + gb_ref[0:1, :]) * ys_ref[...]
                  + _sigmoid(ga_ref[...] + gb_ref[1:2, :]) * ya_ref[...]).astype(BF16)
        m_ref[...] = merged
        row = pl.program_id(0) * tm + lax.broadcasted_iota(jnp.int32, (tm, 1), 0)
        o_ref[...] = jnp.where(row >= PAD, _dot(merged, w_ref[...]), 0.0) + h_ref[...]

    row = pl.BlockSpec((tm, D_MODEL), lambda i: (i, 0))
    return pl.pallas_call(
        body, name="merge_out_fwd", grid=(n // tm,),
        in_specs=[row, pl.BlockSpec((tm, D_MODEL), lambda i: (i, 1)), row, row,
                  pl.BlockSpec((2, D_MODEL), lambda i: (0, 0)), pl.BlockSpec((D_MODEL, D_MODEL), lambda i: (0, 0)), row],
        out_specs=[row, row],
        out_shape=[jax.ShapeDtypeStruct((n, D_MODEL), BF16), jax.ShapeDtypeStruct((n, D_MODEL), F32)],
        compiler_params=_cparams("parallel"))(gates, gates, y_ssd, y_att, gate_b, w_out, h)


def _merge_out_bwd(dh, w_out, gates, y_ssd, y_att, gate_b):
    n = gates.shape[0]
    tm = _row_tile(n, 416)

    def body(dh_ref, w_ref, gs_ref, ga_ref, ys_ref, ya_ref, gb_ref, dys_ref, dya_ref, dg_ref, dgb_ref):
        i = pl.program_id(0)
        row = i * tm + lax.broadcasted_iota(jnp.int32, (tm, 1), 0)
        dmv = jnp.where(row >= PAD, _dot_nt(dh_ref[...].astype(BF16), w_ref[...]), 0.0)
        ss =_sigmoid(gs_ref[...] + gb_ref[0:1, :])
        sa = _sigmoid(ga_ref[...] + gb_ref[1:2, :])
        dys_ref[...] = (dmv * ss).astype(BF16)
        dya_ref[...] = (dmv * sa).astype(BF16)
        dgs = dmv * ys_ref[...] * ss * (1.0 - ss)
        dga = dmv * ya_ref[...] * sa * (1.0 - sa)
        dg_ref[:, :D_MODEL] = dgs.astype(BF16)
        dg_ref[:, D_MODEL:] = dga.astype(BF16)
        part = jnp.concatenate([jnp.sum(dgs, axis=0, keepdims=True), jnp.sum(dga, axis=0, keepdims=True)], axis=0)

        @pl.when(i == 0)
        def _():
            dgb_ref[...] = part

        @pl.when(i > 0)
        def _():
            dgb_ref[...] += part

    row = pl.BlockSpec((tm, D_MODEL), lambda i: (i, 0))
    gb = pl.BlockSpec((2, D_MODEL), lambda i: (0, 0))
    return pl.pallas_call(
        body, name="merge_out_bwd", grid=(n // tm,),
        in_specs=[row, pl.BlockSpec((D_MODEL, D_MODEL), lambda i: (0, 0)), row,
                  pl.BlockSpec((tm, D_MODEL), lambda i: (i, 1)), row, row, gb],
        out_specs=[row, row, pl.BlockSpec((tm, 2 * D_MODEL), lambda i: (i, 0)), gb],
        out_shape=[jax.ShapeDtypeStruct((n, D_MODEL), BF16), jax.ShapeDtypeStruct((n, D_MODEL), BF16),
                   jax.ShapeDtypeStruct((n, 2 * D_MODEL), BF16), jax.ShapeDtypeStruct((2, D_MODEL), F32)],
        compiler_params=_cparams("arbitrary"))(dh, w_out, gates, gates, y_ssd, y_att, gate_b)


def _col_move(srcs, outs, pieces, *, name):
    rows = srcs[0].shape[-2]
    tr = _row_tile(rows, 128)
    n_src = len(srcs)
    covered = [sum(p[6] for p in pieces if p[0] == o) for o in range(len(outs))]
    total = [int(np.prod(shp)) // rows for shp, _ in outs]

    def body(*refs):
        in_refs, out_refs = refs[:n_src], refs[n_src:]
        for o, ref in enumerate(out_refs):
            if covered[o] != total[o]:
                ref[...] = jnp.zeros_like(ref)
        for o, ol, oc, s, sl, sc, width in pieces:
            val = in_refs[s][:, sc:sc + width] if sl is None else in_refs[s][sl, :, sc:sc + width]
            val = val.astype(outs[o][1])
            if ol is None:
                out_refs[o][:, oc:oc + width] = val
            else:
                out_refs[o][ol, :, oc:oc + width] = val

    def spec(shape):
        if len(shape) == 2:
            return pl.BlockSpec((tr, shape[1]), lambda i: (i, 0))
        return pl.BlockSpec((shape[0], tr, shape[2]), lambda i: (0, i, 0))

    return pl.pallas_call(
        body, name=name, grid=(rows // tr,),
        in_specs=[spec(a.shape) for a in srcs], out_specs=[spec(shp) for shp, _ in outs],
        out_shape=[jax.ShapeDtypeStruct(shp, dt) for shp, dt in outs],
        compiler_params=_cparams("parallel"))(*srcs)


def _shard_pieces(seg_ranges, shard_w):
    out = []
    for seg, runs in enumerate(seg_ranges):
        for g0, width, s0 in runs:
            done = 0
            while done < width:
                dev, col = divmod(g0 + done, shard_w)
                take = min(width - done, shard_w - col)
                out.append((seg, s0 + done, dev, col, take))
                done += take
    return out


_CHIP_RELATIONS = [(1, 0, 0), (0, 1, 0), (1, 1, 0)]
N_CHIPS = 4


class _CommPlan:
    def __init__(self, arrays, out_shape, scratch_shapes, phases):
        self.arrays, self.out_shape, self.scratch_shapes, self.phases = arrays, out_shape, scratch_shapes, phases


def _gather_plan(arrays):
    n_arr = len(arrays)
    n_chips = len(_CHIP_RELATIONS)
    n_pair = 1 + 2 * n_chips

    def where():
        x, y, c = lax.axis_index("x"), lax.axis_index("y"), lax.axis_index("c")
        return x, y, c, (x, y, 1 - c), [(x ^ dx, y ^ dy) for dx, dy, _ in _CHIP_RELATIONS]

    def copy(outs, sems, a, k, block, to, src=None):
        slot = outs[a].at[2 * block[0] + block[1], block[2]]
        return pltpu.make_async_remote_copy(
            src_ref=slot if src is None else src, dst_ref=slot, send_sem=sems[0].at[a * n_pair + k],
            recv_sem=sems[1].at[a * n_pair + k], device_id=to, device_id_type=MESH)

    def mine(ins, outs, sems, a, x, y, c):
        return pltpu.make_async_copy(ins[a], outs[a].at[2 * x + y, c], sems[2].at[a])

    def first_copies(ins, outs, sems, a, x, y, c, sibling, chips):
        return ([copy(outs, sems, a, 0, (x, y, c), sibling, src=ins[a])]
                + [copy(outs, sems, a, 1 + j, (x, y, c), (*chip, c), src=ins[a]) for j, chip in enumerate(chips)])

    def start(ins, outs, sems):
        x, y, c, sibling, chips = where()
        for a in range(n_arr):
            mine(ins, outs, sems, a, x, y, c).start()
            for cp in first_copies(ins, outs, sems, a, x, y, c, sibling, chips):
                cp.start()

    def pass_on(ins, outs, sems):
        x, y, c, sibling, chips = where()
        for j, chip in enumerate(chips):
            for a in range(n_arr):
                copy(outs, sems, a, 1 + j, (*chip, c), (x, y, c)).wait_recv()
                copy(outs, sems, a, 1 + n_chips + j, (*chip, c), sibling).start()

    def finish(ins, outs, sems):
        x, y, c, sibling, chips = where()
        for a in range(n_arr):
            copy(outs, sems, a, 0, (x, y, 1 - c), (x, y, c)).wait_recv()
            for j, chip in enumerate(chips):
                copy(outs, sems, a, 1 + n_chips + j, (*chip, 1 - c), (x, y, c)).wait_recv()
        for a in range(n_arr):
            for cp in first_copies(ins, outs, sems, a, x, y, c, sibling, chips):
                cp.wait_send()
            for j, chip in enumerate(chips):
                copy(outs, sems, a, 1 + n_chips + j, (*chip, c), sibling).wait_send()
            mine(ins, outs, sems, a, x, y, c).wait()

    return _CommPlan(
        arrays, [jax.ShapeDtypeStruct((N_CHIPS, 2) + a.shape, a.dtype) for a in arrays],
        [pltpu.SemaphoreType.DMA((n_arr * n_pair,)), pltpu.SemaphoreType.DMA((n_arr * n_pair,)),
         pltpu.SemaphoreType.DMA((n_arr,))],
        (start, pass_on, finish))


_ALL_RELATIONS = [(dx, dy, dc) for dx in (0, 1) for dy in (0, 1) for dc in (0, 1)][1:]


def _all_to_all_plan(arrays, scatter=None):
    n_arr = len(arrays)
    n_rel = len(_ALL_RELATIONS)
    scatter = scatter or [True] * n_arr

    def block(ins, a, p):
        return ins[a].at[p] if scatter[a] else ins[a]

    def local_copies(ins, outs, sems):
        me = 4 * lax.axis_index("x") + 2 * lax.axis_index("y") + lax.axis_index("c")
        return [pltpu.make_async_copy(block(ins, a, me), outs[a].at[me], sems[2].at[a]) for a in range(n_arr)]

    def remote_copies(ins, outs, sems, arrivals):
        x, y, c = lax.axis_index("x"), lax.axis_index("y"), lax.axis_index("c")
        me = 4 * x + 2 * y + c
        out = []
        for k, (dx, dy, dc) in enumerate(_ALL_RELATIONS):
            px, py, pc = x ^ dx, y ^ dy, c ^ dc
            peer = 4 * px + 2 * py + pc
            for a in range(n_arr):
                out.append(pltpu.make_async_remote_copy(
                    src_ref=block(ins, a, peer), dst_ref=outs[a].at[peer if arrivals else me],
                    send_sem=sems[0].at[a * n_rel + k], recv_sem=sems[1].at[a * n_rel + k],
                    device_id=(x, y, c) if arrivals else (px, py, pc), device_id_type=MESH))
        return out

    def start(ins, outs, sems):
        for cp in local_copies(ins, outs, sems) + remote_copies(ins, outs, sems, False):
            cp.start()

    def pass_on(ins, outs, sems):
        pass

    def finish(ins, outs, sems):
        for send in remote_copies(ins, outs, sems, False):
            send.wait_send()
        for arrival in remote_copies(ins, outs, sems, True):
            arrival.wait_recv()
        for cp in local_copies(ins, outs, sems):
            cp.wait()

    return _CommPlan(
        arrays, [jax.ShapeDtypeStruct(a.shape if s else (N_DEV,) + a.shape, a.dtype) for a, s in zip(arrays, scatter)],
        [pltpu.SemaphoreType.DMA((n_arr * n_rel,)), pltpu.SemaphoreType.DMA((n_arr * n_rel,)),
         pltpu.SemaphoreType.DMA((n_arr,))],
        (start, pass_on, finish))


def _adamw(w, gslots, m, v, *, name, side=None):
    rows, cols = w.shape
    n_slots = gslots.shape[0]
    tr = _row_tile(rows, 128) if rows % 16 == 0 else rows

    def body(w_ref, g_ref, m_ref, v_ref, go_ref, d_ref, mo_ref, vo_ref):
        g = g_ref[0].astype(F32)
        for s in range(1, n_slots):
            g = g + g_ref[s].astype(F32)
        mn = ADAM_B1 * m_ref[...] + (1.0 - ADAM_B1) * g
        vn = ADAM_B2 * v_ref[...] + (1.0 - ADAM_B2) * (g * g)
        go_ref[...] = g
        mo_ref[...] = mn
        vo_ref[...] = vn
        m_hat = mn / (1.0 - ADAM_B1 ** ADAM_STEP)
        v_hat = vn / (1.0 - ADAM_B2 ** ADAM_STEP)
        d_ref[...] = -ADAM_LR * (m_hat / (jnp.sqrt(v_hat) + ADAM_EPS) + ADAM_WD * w_ref[...])

    blk = pl.BlockSpec((tr, cols), lambda i: (i, 0))
    shp = jax.ShapeDtypeStruct((rows, cols), F32)
    outs, carried = _call_with_side(
        body, side, name=name, grid=(rows // tr,),
        in_specs=[blk, pl.BlockSpec((n_slots, tr, cols), lambda i: (0, i, 0)), blk, blk],
        out_specs=[blk] * 4, out_shape=[shp] * 4, scratch_shapes=[], args=(w, gslots, m, v), semantics=("parallel",))
    return outs if side is None else (outs, carried)


_BIG = ("w_in", "w_ssd_branch", "w_attn_branch", "w_out", "w_ffn_in", "w_ffn_out")
_SMALL_SHARDED = ("meta_tokens", "ssd_conv_w", "gate_b", "ffn_conv_w")
_SMALL_REPLICATED = ("norm_mix_w", "ssd_conv_b", "ssd_dt_bias", "ssd_a_log", "ssd_d", "ssd_norm_w", "attn_sinks",
                     "rel_bias", "norm_ffn_w", "ffn_conv_b", "norm_final_w")
_WEIGHTS = ("meta_tokens", "norm_mix_w", "w_in", "ssd_conv_w", "ssd_conv_b", "ssd_dt_bias", "ssd_a_log", "ssd_d",
            "ssd_norm_w", "w_ssd_branch", "w_attn_branch", "attn_sinks", "rel_bias", "gate_b", "w_out", "norm_ffn_w",
            "w_ffn_in", "ffn_conv_w", "ffn_conv_b", "w_ffn_out", "norm_final_w")
_ROW_SHARDED = ("w_ssd_branch", "w_attn_branch", "w_out", "w_ffn_out")
_COL_SHARDED = ("w_in", "w_ffn_in", "meta_tokens", "ssd_conv_w", "gate_b", "ffn_conv_w")
_IN_SEGS = (("z", SSD_INNER), ("xbc", SSD_XBC), ("dt", SSD_HEADS), ("qkv", ATT_Q + 2 * ATT_KV), ("g", 2 * D_MODEL))


def _pack_rows(flat_parts, width, row_mult):
    flat = jnp.concatenate([p.reshape(-1) for p in flat_parts])
    pad = (-flat.shape[0]) % (width * row_mult)
    if pad:
        flat = jnp.concatenate([flat, jnp.zeros((pad,), flat.dtype)])
    return flat.reshape(-1, width)


def _unpack(flat, shapes):
    out, off = [], 0
    for shp in shapes:
        size = int(np.prod(shp))
        out.append(flat[off:off + size].reshape(shp))
        off += size
    return out


def _gather_full(stack, name, shard_shape):
    if name in _COL_SHARDED:
        return jnp.transpose(stack, (1, 0, 2)).reshape(shard_shape[0], N_DEV * shard_shape[1])
    return stack.reshape(N_DEV * shard_shape[0], shard_shape[1])


_IN_SEG_W = {"z": SSD_INNER, "xbc": SSD_XBC, "dt": DT_W, "qkv": ATT_Q + 2 * ATT_KV, "g": 2 * D_MODEL}
_IN_SHARD_W = (SSD_INNER + SSD_XBC + SSD_HEADS + ATT_Q + 2 * ATT_KV + 2 * D_MODEL) // N_DEV
_FFN_SHARD_W = 2 * D_FF // N_DEV


def _in_seg_runs():
    runs, off = [], 0
    for nm, width in _IN_SEGS:
        if nm == "dt":
            runs.append([(off + SSD_HPG * g, SSD_HPG, LANES * g) for g in range(SSD_GROUPS)])
        else:
            runs.append([(off, width, 0)])
        off += width
    return runs


def _w_in_to_segments(stack):
    pieces = [(seg, None, scol, 0, dev, col, w) for seg, scol, dev, col, w in _shard_pieces(_in_seg_runs(), _IN_SHARD_W)]
    outs = [((D_MODEL, _IN_SEG_W[nm]), stack.dtype) for nm, _ in _IN_SEGS]
    return dict(zip([nm for nm, _ in _IN_SEGS], _col_move([stack], outs, pieces, name="w_in_segments")))


def _segments_to_w_in_shards(seg_grads):
    pieces = [(0, dev, col, seg, None, scol, w) for seg, scol, dev, col, w in _shard_pieces(_in_seg_runs(), _IN_SHARD_W)]
    return _col_move(seg_grads, [((N_DEV, D_MODEL, _IN_SHARD_W), seg_grads[0].dtype)], pieces, name="g_w_in_shards")[0]


def _ffn_in_from_shards(stack):
    pieces = [(0, None, scol, 0, dev, col, w)
              for _, scol, dev, col, w in _shard_pieces([[(0, 2 * D_FF, 0)]], _FFN_SHARD_W)]
    return _col_move([stack], [((D_MODEL, 2 * D_FF), stack.dtype)], pieces, name="w_ffn_in_full")[0]


def _ffn_in_to_shards(g_up, g_gate):
    pieces = [(0, dev, col, seg, None, scol, w)
              for seg, scol, dev, col, w in _shard_pieces([[(0, D_FF, 0)], [(D_FF, D_FF, 0)]], _FFN_SHARD_W)]
    return _col_move([g_up, g_gate], [((N_DEV, D_MODEL, _FFN_SHARD_W), g_up.dtype)], pieces, name="g_w_ffn_in_shards")[0]


def _dt_spread(w_dt):
    k = w_dt.shape[0]
    w4 = w_dt.reshape(k, SSD_GROUPS, SSD_HPG)
    return jnp.pad(w4, ((0, 0), (0, 0), (0, LANES - SSD_HPG))).reshape(k, DT_W)


def _dt_gather(w_wide):
    k = w_wide.shape[0]
    return w_wide.reshape(k, SSD_GROUPS, LANES)[:, :, :SSD_HPG].reshape(k, SSD_HEADS)


class _LateExchanges:
    def __init__(self, two_d, shape2):
        self.two_d, self.shape2 = two_d, shape2
        self.early_grads_received = None
        self.w_in_grads_received = None

    def row_pack(self, tree):
        return jnp.concatenate([tree[k] for k in _ROW_SHARDED], axis=0)

    def late_weights_plan(self):
        return _gather_plan([self.two_d["w_ffn_in"].astype(BF16), self.row_pack(self.two_d).astype(BF16)])

    def late_weights(self, gathered):
        w_ffn_in_all, rows_all = [g.reshape((N_DEV,) + g.shape[2:]) for g in gathered]
        out = {"w_ffn_in": _ffn_in_from_shards(w_ffn_in_all)}
        off = 0
        for k in _ROW_SHARDED:
            r = self.shape2[k][0]
            out[k] = rows_all[:, off:off + r].reshape(N_DEV * r, D_MODEL)
            off += r
        return out

    def early_grads_plan(self, grads):
        rows_send = jnp.concatenate([grads[k].reshape(N_DEV, self.shape2[k][0], D_MODEL) for k in _ROW_SHARDED], axis=1)
        return _all_to_all_plan([_ffn_in_to_shards(*grads["w_ffn_in"]), rows_send])

    def w_in_grads_plan(self, seg_grads):
        return _all_to_all_plan([_segments_to_w_in_shards(seg_grads)])


def _local_step(x, target, w, exchanges=None):
    h0 = jnp.concatenate([jnp.zeros((PAD, D_MODEL), F32), w["meta_tokens"], x], axis=0)
    segs = w["in_segs"]

    dtb = _dt_spread(w["ssd_dt_bias"])
    alog = _dt_spread(w["ssd_a_log"])
    dskip_w = jnp.repeat(w["ssd_d"], SSD_HEADDIM, axis=1)
    sinks = jnp.pad(w["attn_sinks"], ((0, 0), (0, LANES - ATT_HEADS)))
    onehot_t = w["onehot_t"] if "onehot_t" in w else _onehot_t()
    flat_tables = w["bias_tables"] if "bias_tables" in w else _bias_tables(w["rel_bias"].T, onehot_t)[0]
    tables = jnp.transpose(flat_tables.reshape(ATT_HEADS, 3, N_KEYS, BLOCK), (1, 0, 2, 3))

    u = _rms_fwd(h0, w["norm_mix_w"], name="rms_mix_fwd")
    z = _mm(u, segs["z"], name="in_z")
    xbc, pre = _mm_conv_fwd(u, segs["xbc"], w["ssd_conv_w"], w["ssd_conv_b"], name="in_xbc_conv_fwd")
    dt_raw = _mm(u, segs["dt"], name="in_dt")
    qkv = _mm(u, segs["qkv"], out_dtype=BF16, name="in_qkv")
    gates = _mm(u, segs["g"], name="in_g")
    (y, yn, hsave), gathered = _ssd_fwd(pre, dt_raw, z, dtb, alog, dskip_w, w["ssd_norm_w"],
                                        side=None if exchanges is None else exchanges.late_weights_plan())
    if exchanges is not None:
        w = {**w, **exchanges.late_weights(gathered)}
    w_ffn_up, w_ffn_gate = w["w_ffn_in"][:, :D_FF], w["w_ffn_in"][:, D_FF:]
    y_ssd = _mm(yn, w["w_ssd_branch"], out_dtype=BF16, name="ssd_out")
    att = _attn_fwd(qkv, tables, sinks)
    y_att = _mm(att, w["w_attn_branch"], out_dtype=BF16, name="att_out")
    merged, h1 = _merge_out_fwd(gates, y_ssd, y_att, w["gate_b"], w["w_out"], h0)
    u2 = _rms_fwd(h1, w["norm_ffn_w"], name="rms_ffn_fwd")
    x_up, x_gate, hid_up, hid_gate, act = _ffn_in_act_fwd(u2, w["w_ffn_in"], w["ffn_conv_w"], w["ffn_conv_b"])
    h2 = _mm(act, w["w_ffn_out"], c=h1, mask=True, name="ffn_out")
    dh2, dh2_b, loss_row, g_norm_final = _final_loss(h2, w["norm_final_w"], target)

    grads = {"norm_final_w": g_norm_final}
    grads["w_ffn_out"] = _mm(act, dh2_b, ta=True, mask=True, out_dtype=BF16, name="g_w_ffn_out")
    dx_up, dx_gate, dcw_up, dcw_gate, dcb_up, dcb_gate = _ffn_out_act_bwd(
        dh2_b, w["w_ffn_out"], hid_up, hid_gate, x_up, x_gate, w["ffn_conv_w"])
    grads["ffn_conv_w"] = jnp.concatenate([dcw_up, dcw_gate], axis=1)
    grads["ffn_conv_b"] = jnp.concatenate([dcb_up, dcb_gate], axis=1)
    (dh1, grads["norm_ffn_w"]), _ = _mm_rms_bwd([(dx_up, w_ffn_up), (dx_gate, w_ffn_gate)], h1, w["norm_ffn_w"], dh2,
                                                name="d_u2_rms_bwd")
    grads["w_ffn_in"] = (_mm(u2, dx_up, ta=True, out_dtype=BF16, name="g_w_ffn_up"),
                         _mm(u2, dx_gate, ta=True, out_dtype=BF16, name="g_w_ffn_gate"))

    grads["w_out"] = _mm(merged, dh1, ta=True, mask=True, out_dtype=BF16, name="g_w_out")
    dy_ssd, dy_att, dgates, grads["gate_b"] = _merge_out_bwd(dh1, w["w_out"], gates, y_ssd, y_att, w["gate_b"])
    dyn = _mm(dy_ssd, w["w_ssd_branch"], tb=True, name="d_yn")
    grads["w_ssd_branch"] = _mm(yn, dy_ssd, ta=True, out_dtype=BF16, name="g_w_ssd")
    datt = _mm(dy_att, w["w_attn_branch"], tb=True, out_dtype=BF16, name="d_att")
    grads["w_attn_branch"] = _mm(att, dy_att, ta=True, out_dtype=BF16, name="g_w_att")
    (dz, dpxs, dpb, dpc, ddt, grads["ssd_norm_w"], g_dtb, g_alog, g_dskip), received = _ssd_bwd(
        dyn, y, z, pre, dt_raw, hsave, dtb, alog, dskip_w, w["ssd_norm_w"],
        side=None if exchanges is None else exchanges.early_grads_plan(grads))
    if exchanges is not None:
        exchanges.early_grads_received = received
    grads["ssd_dt_bias"] = _dt_gather(g_dtb)
    grads["ssd_a_log"] = _dt_gather(g_alog)
    grads["ssd_d"] = _dt_gather(g_dskip)
    conv_g = _conv_bwd(dpxs, xbc, w["ssd_conv_w"], name="ssd_conv_bwd_x")
    conv_g = _conv_bwd(dpb, xbc, w["ssd_conv_w"], name="ssd_conv_bwd_b", col0=SSD_INNER, into=conv_g)
    dxbc, grads["ssd_conv_w"], grads["ssd_conv_b"] = _conv_bwd(
        dpc, xbc, w["ssd_conv_w"], name="ssd_conv_bwd_c", col0=SSD_INNER + SSD_BC, into=conv_g)
    dqkv, d_tables, d_sinks = _attn_bwd(datt, qkv, tables, sinks)
    grads["attn_sinks"] = d_sinks[:, :ATT_HEADS]
    dtab = jnp.transpose(d_tables, (1, 0, 2, 3)).reshape(ATT_HEADS, NT_ALL)
    grads["rel_bias"] = _bias_grad(dtab, onehot_t).T
    dsegs = {"z": dz, "xbc": dxbc, "dt": ddt, "qkv": dqkv, "g": dgates}
    grads["in_segs"] = [_mm(u, dsegs[nm], ta=True, out_dtype=BF16, name="g_w_in_" + nm) for nm, _ in _IN_SEGS]
    (dh0, grads["norm_mix_w"]), received = _mm_rms_bwd(
        [(dsegs[nm], segs[nm]) for nm, _ in _IN_SEGS], h0, w["norm_mix_w"], dh1, name="d_u_rms_bwd",
        side=None if exchanges is None else exchanges.w_in_grads_plan(grads["in_segs"]))
    if exchanges is not None:
        exchanges.w_in_grads_received = received[0]
    grads["meta_tokens"] = dh0[PAD:BLOCK]
    return loss_row[0, 0], dh0[BLOCK:], grads


def kernel(x, meta_tokens, norm_mix_w, w_in, ssd_conv_w, ssd_conv_b, ssd_dt_bias, ssd_a_log, ssd_d, ssd_norm_w, w_ssd_branch, w_attn_branch, attn_sinks, rel_bias, gate_b, w_out, norm_ffn_w, w_ffn_in, ffn_conv_w, ffn_conv_b, w_ffn_out, norm_final_w, loss_target, m_meta_tokens, m_norm_mix_w, m_w_in, m_ssd_conv_w, m_ssd_conv_b, m_ssd_dt_bias, m_ssd_a_log, m_ssd_d, m_ssd_norm_w, m_w_ssd_branch, m_w_attn_branch, m_attn_sinks, m_rel_bias, m_gate_b, m_w_out, m_norm_ffn_w, m_w_ffn_in, m_ffn_conv_w, m_ffn_conv_b, m_w_ffn_out, m_norm_final_w, v_meta_tokens, v_norm_mix_w, v_w_in, v_ssd_conv_w, v_ssd_conv_b, v_ssd_dt_bias, v_ssd_a_log, v_ssd_d, v_ssd_norm_w, v_w_ssd_branch, v_w_attn_branch, v_attn_sinks, v_rel_bias, v_gate_b, v_w_out, v_norm_ffn_w, v_w_ffn_in, v_ffn_conv_w, v_ffn_conv_b, v_w_ffn_out, v_norm_final_w):
    shard = dict(meta_tokens=meta_tokens, norm_mix_w=norm_mix_w, w_in=w_in, ssd_conv_w=ssd_conv_w,
                 ssd_conv_b=ssd_conv_b, ssd_dt_bias=ssd_dt_bias, ssd_a_log=ssd_a_log, ssd_d=ssd_d,
                 ssd_norm_w=ssd_norm_w, w_ssd_branch=w_ssd_branch, w_attn_branch=w_attn_branch,
                 attn_sinks=attn_sinks, rel_bias=rel_bias, gate_b=gate_b, w_out=w_out, norm_ffn_w=norm_ffn_w,
                 w_ffn_in=w_ffn_in, ffn_conv_w=ffn_conv_w, ffn_conv_b=ffn_conv_b, w_ffn_out=w_ffn_out,
                 norm_final_w=norm_final_w)
    mom_m = dict(zip(_WEIGHTS, (m_meta_tokens, m_norm_mix_w, m_w_in, m_ssd_conv_w, m_ssd_conv_b, m_ssd_dt_bias,
                                m_ssd_a_log, m_ssd_d, m_ssd_norm_w, m_w_ssd_branch, m_w_attn_branch, m_attn_sinks,
                                m_rel_bias, m_gate_b, m_w_out, m_norm_ffn_w, m_w_ffn_in, m_ffn_conv_w, m_ffn_conv_b,
                                m_w_ffn_out, m_norm_final_w)))
    mom_v = dict(zip(_WEIGHTS, (v_meta_tokens, v_norm_mix_w, v_w_in, v_ssd_conv_w, v_ssd_conv_b, v_ssd_dt_bias,
                                v_ssd_a_log, v_ssd_d, v_ssd_norm_w, v_w_ssd_branch, v_w_attn_branch, v_attn_sinks,
                                v_rel_bias, v_gate_b, v_w_out, v_norm_ffn_w, v_w_ffn_in, v_ffn_conv_w, v_ffn_conv_b,
                                v_w_ffn_out, v_norm_final_w)))
    orig_shape = {k: a.shape for k, a in shard.items()}
    two_d = {k: a.reshape(a.shape[-2:]) if a.ndim >= 2 else a.reshape(1, -1) for k, a in shard.items()}
    shape2 = {k: a.shape for k, a in two_d.items()}

    def as2d(tree):
        return {k: tree[k].reshape(shape2[k]) for k in _WEIGHTS}

    mom_m, mom_v = as2d(mom_m), as2d(mom_v)

    exchanges = _LateExchanges(two_d, shape2)
    row_pack = exchanges.row_pack
    small_pack = _pack_rows([two_d[k] for k in _SMALL_SHARDED], LANES, SMALL_ROW_MULT)
    onehot_t = _onehot_t()
    gather = [two_d["w_in"].astype(BF16), small_pack]
    flat_tables, gathered = _bias_tables(two_d["rel_bias"].T, onehot_t, side=_gather_plan(gather))
    w_in_all, small_all = [g.reshape((N_DEV,) + a.shape) for g, a in zip(gathered, gather)]
    full = {k: two_d[k] for k in _SMALL_REPLICATED}
    full["onehot_t"], full["bias_tables"] = onehot_t, flat_tables
    full["in_segs"] = _w_in_to_segments(w_in_all)
    small_flat = small_all.reshape(N_DEV, -1)
    off = 0
    for k in _SMALL_SHARDED:
        size = int(np.prod(shape2[k]))
        full[k] = _gather_full(small_flat[:, off:off + size].reshape((N_DEV,) + shape2[k]), k, shape2[k])
        off += size

    loss_local, grad_x, grads = _local_step(x[0], loss_target[0], full, exchanges)

    small_names = _SMALL_SHARDED + _SMALL_REPLICATED
    small_send = _pack_rows([grads[k] for k in small_names] + [loss_local.reshape(1)], LANES, SMALL_ROW_MULT)
    in_recv = exchanges.w_in_grads_received
    ffn_recv, rows_recv = exchanges.early_grads_received

    w_in_out, (small_recv,) = _adamw(two_d["w_in"], in_recv, mom_m["w_in"], mom_v["w_in"], name="adamw_w_in",
                                     side=_all_to_all_plan([small_send], [False]))
    big = {"w_in": w_in_out,
           "w_ffn_in": _adamw(two_d["w_ffn_in"], ffn_recv, mom_m["w_ffn_in"], mom_v["w_ffn_in"], name="adamw_w_ffn_in")}
    rows_out = _adamw(row_pack(two_d), rows_recv, row_pack(mom_m), row_pack(mom_v), name="adamw_rows")
    off = 0
    for k in _ROW_SHARDED:
        r = shape2[k][0]
        big[k] = [a[off:off + r] for a in rows_out]
        off += r
    me =4 * lax.axis_index("x") + 2 * lax.axis_index("y") + lax.axis_index("c")
    small_full_shapes = [grads[k].shape for k in small_names]
    n_small = sum(int(np.prod(s)) for s in small_full_shapes)

    def packed_small(tree):
        parts = []
        for k in small_names:
            a = tree[k]
            if k in _SMALL_SHARDED:
                fullw = jnp.zeros(grads[k].shape, F32)
                a = lax.dynamic_update_slice(fullw, a, (0, me * a.shape[1]))
            parts.append(a)
        return _pack_rows(parts + [jnp.zeros((1,), F32)], LANES, SMALL_ROW_MULT)

    g_small, d_small, m_small, v_small = _adamw(packed_small(two_d), small_recv, packed_small(mom_m),
                                                packed_small(mom_v), name="adamw_small")

    def unpack_all(which, small):
        out = {k: big[k][which] for k in _BIG}
        flat = small.reshape(-1)
        for k, a in zip(small_names, _unpack(flat, small_full_shapes)):
            if k in _SMALL_SHARDED:
                a = lax.dynamic_slice(a, (0, me * shape2[k][1]), shape2[k])
            out[k] = a
        return out, flat[n_small]

    g_all, loss = unpack_all(0, g_small)
    d_all, _ = unpack_all(1, d_small)
    m_all, _ = unpack_all(2, m_small)
    v_all, _ = unpack_all(3, v_small)

    def final(tree):
        return [tree[k].reshape(orig_shape[k]) for k in _WEIGHTS]

    return (loss, grad_x[None], *final(g_all), *final(d_all), *final(m_all), *final(v_all))
```

```python
import functools
import math

import numpy as np
import jax
import jax.numpy as jnp
from jax import lax
from jax.experimental import pallas as pl
from jax.experimental.pallas import tpu as pltpu

F32 = jnp.float32
BF16 = jnp.bfloat16
HIGHEST = lax.Precision.HIGHEST

D_MODEL = 1024
N_META = 16
BLOCK = 128
PAD = BLOCK - N_META
EPS = 1e-6
NEG = -1e30
SSD_INNER = 2 * D_MODEL
SSD_HEADDIM = 64
SSD_HEADS = SSD_INNER // SSD_HEADDIM
SSD_GROUPS = 4
SSD_HPG = SSD_HEADS // SSD_GROUPS
SSD_STATE = 128
SSD_CONV = 4
SSD_GW = SSD_HPG * SSD_HEADDIM
SSD_BC = SSD_GROUPS * SSD_STATE
SSD_XBC = SSD_INNER + 2 * SSD_BC
ATT_HEADS = 16
ATT_KV_HEADS = 2
ATT_HEADDIM = 64
ATT_GQ = ATT_HEADS // ATT_KV_HEADS
ATT_SCALE = ATT_HEADDIM ** -0.5
ATT_Q = ATT_HEADS * ATT_HEADDIM
ATT_KV = ATT_KV_HEADS * ATT_HEADDIM
REL_BUCKETS = 32
REL_MAX_DIST = 128
D_FF = 2816
FFN_CONV = 3
ADAM_LR = 0.001
ADAM_B1 = 0.9
ADAM_B2 = 0.999
ADAM_EPS = 1e-08
ADAM_WD = 0.01
ADAM_STEP = 10

N_DEV = 8
LANES = 128
SUBLANES = 8
BF16_ROWS = 16
DT_W = SSD_GROUPS * LANES
VMEM_LIMIT_BYTES = 56 * 1024 * 1024
MESH = pl.DeviceIdType.MESH

SMALL_ROW_MULT = 16

N_KEYS = 3 * BLOCK
NT_ALL = 3 * N_KEYS * BLOCK
NT_TILE = 8192


def _cparams(*sem):
    return pltpu.CompilerParams(dimension_semantics=sem, vmem_limit_bytes=VMEM_LIMIT_BYTES)


def _row_tile(n, cap):
    best = None
    for t in range(16, min(n, cap) + 1, 16):
        if n % t == 0:
            best = t
    return best or n


def _col_tile(n, cap):
    for t in (1408, 1280, 1024, 768, 640, 512, 384, 256, 128):
        if t <= cap and n % t == 0:
            return t
    return n


def _sigmoid(x):
    return 0.5 * jnp.tanh(0.5 * x) + 0.5


def _silu(x):
    return x * _sigmoid(x)


def _softplus(x):
    return jnp.maximum(x, 0.0) + jnp.log(1.0 + jnp.exp(-jnp.abs(x)))


def _dot_nt(a, b):
    return lax.dot_general(a, b, (((1,), (1,)), ((), ())), preferred_element_type=F32)


def _dot_tn(a, b):
    return lax.dot_general(a, b, (((0,), (0,)), ((), ())), preferred_element_type=F32)


def _dot(a, b):
    return jnp.dot(a, b, preferred_element_type=F32)


def _bf16_terms(x, terms):
    out, rest = [], x
    for _ in range(terms):
        part = rest.astype(BF16)
        out.append(part)
        rest = rest - part.astype(F32)
    return out


def _dot_sel(x, sel, terms=3):
    return sum(_dot(part, sel) for part in _bf16_terms(x, terms))


def _sel_dot(sel, x, terms=3):
    return sum(_dot(sel, part) for part in _bf16_terms(x, terms))


def _sum_all(x):
    return jnp.sum(jnp.sum(x, axis=1, keepdims=True), axis=0, keepdims=True)


MM_ROW_CAPS = (2080, 1664, 832, 416)
MM_COL_CAP = 1408
MM_VMEM_BUDGET = 44 * 1024 * 1024


def _mm_tiles(rows, cols, vmem_bytes):
    col_cands = [t for t in (2048, 1536, 1408, 1280, 1024, 768, 640, 512, 384, 256, 128) if cols % t == 0]
    if cols <= 2 * MM_COL_CAP:
        col_cands.append(cols)
    best = None
    for cap in MM_ROW_CAPS:
        tr = _row_tile(rows, cap)
        for tc in col_cands:
            if vmem_bytes(tr, tc) <= MM_VMEM_BUDGET and (best is None or tr * tc > best[0] * best[1]):
                best = (tr, tc)
    assert best is not None, (rows, cols)
    return best


def _mm(a, b, *, name, ta=False, tb=False, c=None, mask=False, out_dtype=F32):
    if not ta:
        m, k = a.shape
        n = b.shape[0] if tb else b.shape[1]
        tm, tn = _mm_tiles(m, n, lambda t_m, t_n: 2 * (t_m * k * a.dtype.itemsize + k * t_n * b.dtype.itemsize
                                                       + t_m * t_n * (jnp.dtype(out_dtype).itemsize
                                                                      + (0 if c is None else c.dtype.itemsize)))
                           + 4 * t_m * t_n)

        def body(*refs):
            if c is None:
                a_ref, b_ref, o_ref = refs
            else:
                a_ref, b_ref, c_ref, o_ref = refs
            acc = (_dot_nt if tb else _dot)(a_ref[...].astype(BF16), b_ref[...].astype(BF16))
            if mask:
                row = pl.program_id(0) * tm + lax.broadcasted_iota(jnp.int32, (tm, 1), 0)
                acc = jnp.where(row >= PAD, acc, 0.0)
            if c is not None:
                acc = acc + c_ref[...]
            o_ref[...] = acc.astype(out_dtype)

        b_spec = pl.BlockSpec((tn, k), lambda i, j: (j, 0)) if tb else pl.BlockSpec((k, tn), lambda i, j: (0, j))
        in_specs = [pl.BlockSpec((tm, k), lambda i, j: (i, 0)), b_spec]
        args = [a, b]
        if c is not None:
            in_specs.append(pl.BlockSpec((tm, tn), lambda i, j: (i, j)))
            args.append(c)
        return pl.pallas_call(
            body, name=name, grid=(m // tm, n // tn), in_specs=in_specs,
            out_specs=pl.BlockSpec((tm, tn), lambda i, j: (i, j)),
            out_shape=jax.ShapeDtypeStruct((m, n), out_dtype),
            compiler_params=_cparams("parallel", "parallel"))(*args)

    kc, m = a.shape
    n = b.shape[1]
    tm = _col_tile(m, MM_COL_CAP)
    tk, tn = _mm_tiles(kc, n, lambda t_k, t_n: 2 * (t_k * tm * a.dtype.itemsize + t_k * t_n * b.dtype.itemsize
                                                    + tm * t_n * jnp.dtype(out_dtype).itemsize) + 8 * tm * t_n)

    n_k = kc // tk

    def body_t(a_ref, b_ref, o_ref, acc_ref):
        kk = pl.program_id(2)
        bb = b_ref[...]
        if mask:
            row = kk * tk + lax.broadcasted_iota(jnp.int32, (tk, 1), 0)
            bb = jnp.where(row >= PAD, bb, jnp.zeros_like(bb))
        p = _dot_tn(a_ref[...].astype(BF16), bb.astype(BF16))

        @pl.when(kk == 0)
        def _():
            acc_ref[...] = p

        @pl.when(kk > 0)
        def _():
            acc_ref[...] += p

        @pl.when(kk == n_k - 1)
        def _():
            o_ref[...] = acc_ref[...].astype(out_dtype)

    return pl.pallas_call(
        body_t, name=name, grid=(m // tm, n // tn, n_k),
        in_specs=[pl.BlockSpec((tk, tm), lambda i, j, kk: (kk, i)), pl.BlockSpec((tk, tn), lambda i, j, kk: (kk, j))],
        out_specs=pl.BlockSpec((tm, tn), lambda i, j, kk: (i, j)),
        out_shape=jax.ShapeDtypeStruct((m, n), out_dtype),
        scratch_shapes=[pltpu.VMEM((tm, tn), F32)],
        compiler_params=_cparams("parallel", "parallel", "arbitrary"))(a, b)


def _mm_rms_bwd(pairs, x, w, dres, *, name, side=None):
    m, d = x.shape
    tm = _row_tile(m, 416)
    n_pairs = len(pairs)

    def body(*refs):
        a_refs, b_refs = refs[:n_pairs], refs[n_pairs:2 * n_pairs]
        x_ref, w_ref, dres_ref, dx_ref, dw_ref = refs[2 * n_pairs:]
        i = pl.program_id(0)
        dyv = None
        for a_ref, b_ref in zip(a_refs, b_refs):
            term = _dot_nt(a_ref[...].astype(BF16), b_ref[...])
            dyv = term if dyv is None else dyv + term
        xv = x_ref[...]
        r = lax.rsqrt(jnp.mean(xv * xv, axis=-1, keepdims=True) + EPS)
        xh = xv * r
        g = dyv * w_ref[...]
        dx_ref[...] = r * (g - xh * jnp.mean(g * xh, axis=-1, keepdims=True)) + dres_ref[...]
        part = jnp.sum(dyv * xh, axis=0, keepdims=True)

        @pl.when(i == 0)
        def _():
            dw_ref[...] = part

        @pl.when(i > 0)
        def _():
            dw_ref[...] += part

    row = pl.BlockSpec((tm, d), lambda i: (i, 0))
    vec = pl.BlockSpec((1, d), lambda i: (0, 0))
    in_specs = ([pl.BlockSpec((tm, a.shape[1]), lambda i: (i, 0)) for a, _ in pairs]
                + [pl.BlockSpec(b.shape, lambda i: (0, 0), pipeline_mode=pl.Buffered(1)) for _, b in pairs]
                + [row, vec, row])
    return _call_with_side(
        body, side, name=name, grid=(m // tm,), in_specs=in_specs, out_specs=[row, vec],
        out_shape=[jax.ShapeDtypeStruct((m, d), F32), jax.ShapeDtypeStruct((1, d), F32)], scratch_shapes=[],
        args=[a for a, _ in pairs] + [b for _, b in pairs] + [x, w, dres], semantics=("arbitrary",))


def _rms_fwd(h, w, *, name):
    n, d = h.shape
    tm = _row_tile(n, 832)

    def body(h_ref, w_ref, o_ref):
        x = h_ref[...]
        r = lax.rsqrt(jnp.mean(x * x, axis=-1, keepdims=True) + EPS)
        o_ref[...] = (x * r * w_ref[...]).astype(BF16)

    return pl.pallas_call(
        body, name=name, grid=(n // tm,),
        in_specs=[pl.BlockSpec((tm, d), lambda i: (i, 0)), pl.BlockSpec((1, d), lambda i: (0, 0))],
        out_specs=pl.BlockSpec((tm, d), lambda i: (i, 0)),
        out_shape=jax.ShapeDtypeStruct((n, d), BF16),
        compiler_params=_cparams("parallel"))(h, w)


def _final_loss(h, w, target):
    n, d = h.shape
    nb = n // BLOCK

    def body(h_ref, w_ref, t_ref, dh_ref, dhb_ref, loss_ref, dw_ref):
        i = pl.program_id(0)
        xv = h_ref[...]
        r = lax.rsqrt(jnp.mean(xv * xv, axis=-1, keepdims=True) + EPS)
        xh = xv * r
        wv = w_ref[...]
        err = jnp.where(i >= 1, xh * wv - t_ref[...], 0.0)
        dyv = err * (1.0 / d)
        g = dyv * wv
        dh = r * (g - xh * jnp.mean(g * xh, axis=-1, keepdims=True))
        dh_ref[...] = dh
        dhb_ref[...] = dh.astype(BF16)
        lpart = jnp.broadcast_to(0.5 * _sum_all(err * err) * (1.0 / d), (1, LANES))
        wpart = jnp.sum(dyv * xh, axis=0, keepdims=True)

        @pl.when(i == 0)
        def _():
            loss_ref[...] = lpart
            dw_ref[...] = wpart

        @pl.when(i > 0)
        def _():
            loss_ref[...] += lpart
            dw_ref[...] += wpart

    row = pl.BlockSpec((BLOCK, d), lambda i: (i, 0))
    vec = pl.BlockSpec((1, d), lambda i: (0, 0))
    return pl.pallas_call(
        body, name="final_loss", grid=(nb,),
        in_specs=[row, vec, pl.BlockSpec((BLOCK, d), lambda i: (jnp.maximum(i - 1, 0), 0))],
        out_specs=[row, row, pl.BlockSpec((1, LANES), lambda i: (0, 0)), vec],
        out_shape=[jax.ShapeDtypeStruct((n, d), F32), jax.ShapeDtypeStruct((n, d), BF16),
                   jax.ShapeDtypeStruct((1, LANES), F32), jax.ShapeDtypeStruct((1, d), F32)],
        compiler_params=_cparams("arbitrary"))(h, w, target)


def _main_spec(tm, cb, off=0):
    return pl.BlockSpec((tm, cb), lambda j, i: (i, j + off))


def _prev_spec(tm, cb, off=0):
    r8 = tm // SUBLANES
    return pl.BlockSpec((SUBLANES, cb), lambda j, i: (jnp.maximum(i * r8 - 1, 0), j + off))


def _next_spec(tm, cb, n_rows, off=0):
    r8 = tm // SUBLANES
    last = n_rows // SUBLANES - 1
    return pl.BlockSpec((SUBLANES, cb), lambda j, i: (jnp.minimum((i + 1) * r8, last), j + off))


def _with_prev(prev_ref, main_ref, i):
    prev = jnp.where(i > 0, prev_ref[...], 0.0)
    return jnp.concatenate([prev, main_ref[...]], axis=0)


def _with_next(main, nxt, i, n_tiles):
    return jnp.concatenate([main, jnp.where(i < n_tiles - 1, nxt, 0.0)], axis=0)


def _back(xx, s, tm):
    if s == 0:
        return xx[SUBLANES:SUBLANES + tm]
    return pltpu.roll(xx, s, 0)[SUBLANES:SUBLANES + tm]


def _ahead(xx, s, tm):
    if s == 0:
        return xx[:tm]
    return pltpu.roll(xx, xx.shape[0] - s, 0)[:tm]


def _mm_conv_fwd(u, w_in, w, b, *, name):
    n = u.shape[0]
    cdim = w_in.shape[1]
    kw = w.shape[0]
    tm = _row_tile(n, 832)
    cb = _col_tile(cdim, 512)
    nt = n // tm

    def body(u_ref, w_in_ref, w_ref, b_ref, x_ref, o_ref, acc_scr, halo_scr):
        j, i = pl.program_id(0), pl.program_id(1)

        @pl.when((j == 0) & (i == 0))
        def _():
            acc_scr[...] = jnp.zeros_like(acc_scr)
            halo_scr[...] = jnp.zeros_like(halo_scr)

        new = _dot(u_ref[...], w_in_ref[...])
        prev = acc_scr[...]
        xx = jnp.concatenate([jnp.where(i >= 2, halo_scr[...], 0.0), prev], axis=0)
        acc = jnp.broadcast_to(b_ref[...], (tm, cb))
        for k in range(kw):
            acc = acc + w_ref[k:k + 1, :] * _back(xx, kw - 1 - k, tm)
        x_ref[...] = prev.astype(BF16)
        o_ref[...] = acc
        halo_scr[...] = prev[tm - SUBLANES:, :]
        acc_scr[...] = new

    out = pl.BlockSpec((tm, cb), lambda j, i: (jnp.maximum(i - 1, 0), j))
    shp = jax.ShapeDtypeStruct((n, cdim), F32)
    return pl.pallas_call(
        body, name=name, grid=(cdim // cb, nt + 1),
        in_specs=[pl.BlockSpec((tm, u.shape[1]), lambda j, i: (jnp.minimum(i, nt - 1), 0)),
                  pl.BlockSpec((w_in.shape[0], cb), lambda j, i: (0, j)),
                  pl.BlockSpec((kw, cb), lambda j, i: (0, j)), pl.BlockSpec((1, cb), lambda j, i: (0, j))],
        out_specs=[out, out], out_shape=[jax.ShapeDtypeStruct((n, cdim), BF16), shp],
        scratch_shapes=[pltpu.VMEM((tm, cb), F32), pltpu.VMEM((SUBLANES, cb), F32)],
        compiler_params=_cparams("arbitrary", "arbitrary"))(u, w_in, w, b)


def _conv_bwd_core(dpre_ext, x, w_ref, kw, tm):
    dx = None
    dws = []
    for k in range(kw):
        shifted = _ahead(dpre_ext, kw - 1 - k, tm)
        term = w_ref[k:k + 1, :] * shifted
        dx = term if dx is None else dx + term
        dws.append(jnp.sum(shifted * x, axis=0, keepdims=True))
    return dx, dws, jnp.sum(dpre_ext[:tm], axis=0, keepdims=True)


def _acc_rows(i, dw_ref, db_ref, dws, db):
    @pl.when(i == 0)
    def _():
        for k, v in enumerate(dws):
            dw_ref[k:k + 1, :] = v
        db_ref[...] = db

    @pl.when(i > 0)
    def _():
        for k, v in enumerate(dws):
            dw_ref[k:k + 1, :] += v
        db_ref[...] += db


def _conv_bwd(dpre, x, w, *, name, col0=0, into=None):
    n, cdim = x.shape
    kw = w.shape[0]
    tm = _row_tile(n, 832)
    cb = _col_tile(cdim, 512)
    nt = n // tm
    off = col0 // cb
    n_alias = 0 if into is None else 3

    def body(d_ref, dn_ref, x_ref, w_ref, *rest):
        dx_ref, dw_ref, db_ref = rest[n_alias:]
        i = pl.program_id(1)
        dpre_ext = _with_next(d_ref[...], dn_ref[...], i, nt)
        dx, dws, db = _conv_bwd_core(dpre_ext, x_ref[...], w_ref, kw, tm)
        dx_ref[...] = dx.astype(BF16)
        _acc_rows(i, dw_ref, db_ref, dws, db)

    wspec = pl.BlockSpec((kw, cb), lambda j, i: (0, j + off))
    bspec = pl.BlockSpec((1, cb), lambda j, i: (0, j + off))
    return pl.pallas_call(
        body, name=name, grid=(dpre.shape[1] // cb, nt),
        in_specs=[_main_spec(tm, cb), _next_spec(tm, cb, n), _main_spec(tm, cb, off), wspec]
        + [pl.BlockSpec(memory_space=pl.ANY)] * n_alias,
        out_specs=[_main_spec(tm, cb, off), wspec, bspec],
        out_shape=[jax.ShapeDtypeStruct((n, cdim), BF16), jax.ShapeDtypeStruct((kw, cdim), F32),
                   jax.ShapeDtypeStruct((1, cdim), F32)],
        input_output_aliases={4 + k: k for k in range(n_alias)},
        compiler_params=_cparams("parallel", "arbitrary"))(dpre, dpre, x, w, *(into or ()))


def _ffn_in_act_fwd(u, w_in, w, b):
    n = u.shape[0]
    kw = w.shape[0]
    tm = _row_tile(n, 832)
    cb = _col_tile(D_FF, 256)
    nc = D_FF // cb
    nt = n // tm

    def body(u_ref, wu_in_ref, wg_in_ref, wu_ref, wg_ref, bu_ref, bg_ref,
             xu_ref, xg_ref, hu_ref, hg_ref, act_ref, acc_scr, halo_scr):
        j, i = pl.program_id(0), pl.program_id(1)

        @pl.when((j == 0) & (i == 0))
        def _():
            acc_scr[...] = jnp.zeros_like(acc_scr)
            halo_scr[...] = jnp.zeros_like(halo_scr)

        ub = u_ref[...]
        new = [_dot(ub, wu_in_ref[...]), _dot(ub, wg_in_ref[...])]
        hid = []
        for half, (x_ref, w_ref, b_ref) in enumerate(((xu_ref, wu_ref, bu_ref), (xg_ref, wg_ref, bg_ref))):
            prev = acc_scr[half]
            xx = jnp.concatenate([jnp.where(i >= 2, halo_scr[half], 0.0), prev], axis=0)
            acc = jnp.broadcast_to(b_ref[...], (tm, cb))
            for k in range(kw):
                acc = acc + w_ref[k:k + 1, :] * _back(xx, kw - 1 - k, tm)
            x_ref[...] = prev.astype(BF16)
            hid.append(acc)
            halo_scr[half] = prev[tm - SUBLANES:, :]
            acc_scr[half] = new[half]
        hu_ref[...] = hid[0].astype(BF16)
        hg_ref[...] = hid[1].astype(BF16)
        act_ref[...] = (_silu(hid[1]) * hid[0]).astype(BF16)

    def wspec(off):
        return pl.BlockSpec((kw, cb), lambda j, i: (0, j + off))

    def bspec(off):
        return pl.BlockSpec((1, cb), lambda j, i: (0, j + off))

    def in_w(off):
        return pl.BlockSpec((w_in.shape[0], cb), lambda j, i: (0, j + off))

    out = pl.BlockSpec((tm, cb), lambda j, i: (jnp.maximum(i - 1, 0), j))
    bf16_out = jax.ShapeDtypeStruct((n, D_FF), BF16)
    return pl.pallas_call(
        body, name="ffn_in_act_fwd", grid=(nc, nt + 1),
        in_specs=[pl.BlockSpec((tm, u.shape[1]), lambda j, i: (jnp.minimum(i, nt - 1), 0)), in_w(0), in_w(nc),
                  wspec(0), wspec(nc), bspec(0), bspec(nc)],
        out_specs=[out] * 5,
        out_shape=[bf16_out] * 5,
        scratch_shapes=[pltpu.VMEM((2, tm, cb), F32), pltpu.VMEM((2, SUBLANES, cb), F32)],
        compiler_params=_cparams("arbitrary", "arbitrary"))(u, w_in, w_in, w, w, b, b)


def _ffn_out_act_bwd(dh, w_out, hu, hg, x_up, x_gate, w):
    n = x_up.shape[0]
    kw = w.shape[0]
    tm = _row_tile(n, 832)
    cb = _col_tile(D_FF, 256)
    nc = D_FF // cb
    nt = n // tm

    def body(dh_ref, wo_ref, hu_ref, hun_ref, hg_ref, hgn_ref, xu_ref, xg_ref, wu_ref, wg_ref,
             dxu_ref, dxg_ref, dwu_ref, dwg_ref, dbu_ref, dbg_ref, acc_scr, halo_scr):
        j, i = pl.program_id(0), pl.program_id(1)

        @pl.when((j == 0) & (i == 0))
        def _():
            acc_scr[...] = jnp.zeros_like(acc_scr)
            halo_scr[...] = jnp.zeros_like(halo_scr)

        tile = jnp.maximum(nt - 1 - i, 0)
        row = tile * tm + lax.broadcasted_iota(jnp.int32, (tm, 1), 0)
        new = jnp.where(row >= PAD, _dot_nt(dh_ref[...].astype(BF16), wo_ref[...]), 0.0)
        prev = jnp.where(i >= 1, acc_scr[...], 0.0)
        dact_e = jnp.concatenate([prev, jnp.where(i >= 2, halo_scr[...], 0.0)], axis=0)
        last = nt - i >= nt - 1
        up_e = jnp.concatenate([hu_ref[...].astype(F32), jnp.where(last, 0.0, hun_ref[...].astype(F32))], axis=0)
        gate_e = jnp.concatenate([hg_ref[...].astype(F32), jnp.where(last, 0.0, hgn_ref[...].astype(F32))], axis=0)
        halo_scr[...] = prev[:BF16_ROWS, :]
        acc_scr[...] = new
        sg = _sigmoid(gate_e)
        dup_e = dact_e * (gate_e * sg)
        dgate_e = dact_e * up_e * (sg * (1.0 + gate_e * (1.0 - sg)))
        dx, dws, db = _conv_bwd_core(dup_e, xu_ref[...], wu_ref, kw, tm)
        dxu_ref[...] = dx.astype(BF16)
        _acc_rows(i, dwu_ref, dbu_ref, dws, db)
        dx, dws, db = _conv_bwd_core(dgate_e, xg_ref[...], wg_ref, kw, tm)
        dxg_ref[...] = dx.astype(BF16)
        _acc_rows(i, dwg_ref, dbg_ref, dws, db)

    def done_tile(i):
        return jnp.minimum(nt - i, nt - 1)

    halo_blocks = tm // BF16_ROWS
    main = pl.BlockSpec((tm, cb), lambda j, i: (done_tile(i), j))
    nxt = pl.BlockSpec((BF16_ROWS, cb),
                       lambda j, i: (jnp.minimum((done_tile(i) + 1) * halo_blocks, n // BF16_ROWS - 1), j))
    wspec0 = pl.BlockSpec((kw, cb), lambda j, i: (0, j))
    wspec1 = pl.BlockSpec((kw, cb), lambda j, i: (0, j + nc))
    bspec = pl.BlockSpec((1, cb), lambda j, i: (0, j))
    return pl.pallas_call(
        body, name="ffn_out_act_bwd", grid=(nc, nt + 1),
        in_specs=[pl.BlockSpec((tm, dh.shape[1]), lambda j, i: (jnp.maximum(nt - 1 - i, 0), 0)),
                  pl.BlockSpec((cb, w_out.shape[1]), lambda j, i: (j, 0)),
                  main, nxt, main, nxt, main, main, wspec0, wspec1],
        out_specs=[main, main, wspec0, wspec0, bspec, bspec],
        out_shape=[jax.ShapeDtypeStruct((n, D_FF), BF16), jax.ShapeDtypeStruct((n, D_FF), BF16),
                   jax.ShapeDtypeStruct((kw, D_FF), F32), jax.ShapeDtypeStruct((kw, D_FF), F32),
                   jax.ShapeDtypeStruct((1, D_FF), F32), jax.ShapeDtypeStruct((1, D_FF), F32)],
        scratch_shapes=[pltpu.VMEM((tm, cb), F32), pltpu.VMEM((BF16_ROWS, cb), F32)],
        compiler_params=_cparams("arbitrary", "arbitrary"))(dh, w_out, hu, hu, hg, hg, x_up, x_gate, w, w)


def _ssd_prep(pxs_ref, pb_ref, pc_ref, dtr_ref, dtb_ref, alog_ref, c):
    xs = _silu(pxs_ref[...])
    bm = _silu(pb_ref[...])
    cm = _silu(pc_ref[...])
    return (xs, bm, cm) + _ssd_decay(dtr_ref, dtb_ref, alog_ref, c)


def _ssd_decay(dtr_ref, dtb_ref, alog_ref, c):
    row =lax.broadcasted_iota(jnp.int32, (BLOCK, 1), 0) + c * BLOCK
    valid = (row >= PAD).astype(F32)
    dtr = dtr_ref[...] + dtb_ref[...]
    dt = _softplus(dtr) * valid
    a = -jnp.exp(alog_ref[...])
    lam = dt * a
    ri = lax.broadcasted_iota(jnp.int32, (BLOCK, BLOCK), 0)
    ci = lax.broadcasted_iota(jnp.int32, (BLOCK, BLOCK), 1)
    causal = ci <= ri
    cs = _sel_dot(causal.astype(BF16), lam)
    return valid, dtr, dt, a, lam, cs, causal


def _head_cols(r):
    return slice(SSD_HEADDIM * r, SSD_HEADDIM * (r + 1))


def _ssd_specs(nc, rev):
    def cidx(c):
        return nc - 1 - c if rev else c

    xs = pl.BlockSpec((BLOCK, SSD_GW), lambda g, c: (cidx(c), g))
    bspec = pl.BlockSpec((BLOCK, SSD_STATE), lambda g, c: (cidx(c), SSD_INNER // SSD_STATE + g))
    cspec = pl.BlockSpec((BLOCK, SSD_STATE), lambda g, c: (cidx(c), (SSD_INNER + SSD_BC) // SSD_STATE + g))
    lane = pl.BlockSpec((BLOCK, LANES), lambda g, c: (cidx(c), g))
    vec = pl.BlockSpec((1, LANES), lambda g, c: (0, g))
    wide_vec = pl.BlockSpec((1, SSD_GW), lambda g, c: (0, g))
    hsave = pl.BlockSpec((1, 1, SSD_GW, SSD_STATE), lambda g, c: (cidx(c), g, 0, 0))
    return xs, bspec, cspec, lane, vec, wide_vec, hsave


def _head_spread_matrix():
    r = lax.broadcasted_iota(jnp.int32, (LANES, SSD_GW), 0)
    col = lax.broadcasted_iota(jnp.int32, (LANES, SSD_GW), 1)
    return (col // SSD_HEADDIM == r).astype(BF16)


def _const_spec(shape):
    return pl.BlockSpec(shape, lambda g, c: (0,) * len(shape))


def _spread_heads(per_head, e_ref):
    wide = _dot_sel(jnp.concatenate(per_head, axis=0), e_ref[...])
    return [wide[BLOCK * k:BLOCK * (k + 1)] for k in range(len(per_head))]


def _call_with_side(body, side, *, name, grid, in_specs, out_specs, out_shape, scratch_shapes, args,
                    semantics=("parallel", "arbitrary")):
    if side is None:
        outs = pl.pallas_call(body, name=name, grid=grid, in_specs=in_specs, out_specs=out_specs, out_shape=out_shape,
                              scratch_shapes=scratch_shapes, compiler_params=_cparams(*semantics))(*args)
        return outs, []
    n_in, n_out, n_scr, n_side = len(in_specs), len(out_specs), len(scratch_shapes), len(side.arrays)

    def body_with_side(*refs):
        ins, rest = refs[:n_in + n_side], refs[n_in + n_side:]
        outs, scratch = rest[:n_out + n_side], rest[n_out + n_side:]
        side_refs = (ins[n_in:], outs[n_out:], scratch[n_scr:])
        ids = [pl.program_id(k) for k in range(len(grid))]
        inner_first = functools.reduce(jnp.logical_and, [i == 0 for i in ids[1:]], True)

        @pl.when((ids[0] == 0) & inner_first)
        def _():
            side.phases[0](*side_refs)

        body(*ins[:n_in], *outs[:n_out], *scratch[:n_scr])

        @pl.when((ids[0] == grid[0] // 2) & inner_first)
        def _():
            side.phases[1](*side_refs)

        @pl.when(functools.reduce(jnp.logical_and, [i == n - 1 for i, n in zip(ids, grid)]))
        def _():
            side.phases[2](*side_refs)

    any_spec = pl.BlockSpec(memory_space=pl.ANY)
    outs = pl.pallas_call(
        body_with_side, name=name, grid=grid, in_specs=list(in_specs) + [any_spec] * n_side,
        out_specs=list(out_specs) + [any_spec] * n_side, out_shape=list(out_shape) + list(side.out_shape),
        scratch_shapes=list(scratch_shapes) + list(side.scratch_shapes),
        compiler_params=_cparams(*["arbitrary"] * len(grid)))(*args, *side.arrays)
    return outs[:n_out], outs[n_out:]


def _ssd_fwd(pre, dt_raw, z, dtb, alog, dskip_w, norm_w, side=None):
    n = pre.shape[0]
    nc = n // BLOCK
    xs_s, b_s, c_s, lane_s, vec_s, wide_s, hs_s = _ssd_specs(nc, False)

    def body(pxs_ref, pb_ref, pc_ref, dtr_ref, z_ref, dtb_ref, alog_ref, dskw_ref, nw_ref, e_ref,
             y_ref, yn_ref, hs_ref, h_scr):
        c = pl.program_id(1)

        @pl.when(c == 0)
        def _():
            h_scr[...] = jnp.zeros_like(h_scr)

        xs, bm, cm, _, _, dt, _, _, cs, causal = _ssd_prep(pxs_ref, pb_ref, pc_ref, dtr_ref, dtb_ref, alog_ref, c)
        cst = cs.T
        cs_last = cs[BLOCK - 1:BLOCK, :]
        dt_w, ecs_w, dec_w = _spread_heads([dt, jnp.exp(cs), jnp.exp(cs_last - cs)], e_ref)
        xdt = xs * dt_w
        bmb = bm.astype(BF16)
        cmb = cm.astype(BF16)
        cb = _dot_nt(cmb, bmb)
        hg = h_scr[...]
        hs_ref[0, 0] = hg
        y = _dot_nt(cmb, hg.astype(BF16)) * ecs_w + dskw_ref[...] * xs
        first = lax.broadcasted_iota(jnp.int32, (BLOCK, LANES), 1) < SSD_HEADDIM
        diag = []
        for j in range(SSD_HPG // 2):
            xp = xdt[:, LANES * j:LANES * (j + 1)].astype(BF16)
            res = []
            for r in (2 * j, 2 * j + 1):
                lm = jnp.exp(jnp.where(causal, cs[:, r:r + 1] - cst[r:r + 1, :], NEG))
                res.append(_dot((cb * lm).astype(BF16), xp))
            diag.append(jnp.where(first, res[0], res[1]))
        y = y + jnp.concatenate(diag, axis=1)
        st = _dot_tn((xdt * dec_w).astype(BF16), bmb)
        eh = jnp.exp(cs_last)
        for r in range(SSD_HPG):
            rows = _head_cols(r)
            h_scr[rows, :] = hg[rows, :] * eh[:, r:r + 1] + st[rows, :]
        y_ref[...] = y
        gts = y * _silu(z_ref[...])
        rr = lax.rsqrt(jnp.mean(gts * gts, axis=-1, keepdims=True) + EPS)
        yn_ref[...] = (gts * rr * nw_ref[...]).astype(BF16)

    return _call_with_side(
        body, side, name="ssd_fwd", grid=(SSD_GROUPS, nc),
        in_specs=[xs_s, b_s, c_s, lane_s, xs_s, vec_s, vec_s, wide_s, wide_s, _const_spec((LANES, SSD_GW))],
        out_specs=[xs_s, xs_s, hs_s],
        out_shape=[jax.ShapeDtypeStruct((n, SSD_INNER), F32), jax.ShapeDtypeStruct((n, SSD_INNER), BF16),
                   jax.ShapeDtypeStruct((nc, SSD_GROUPS, SSD_GW, SSD_STATE), F32)],
        scratch_shapes=[pltpu.VMEM((SSD_GW, SSD_STATE), F32)],
        args=(pre, pre, pre, dt_raw, z, dtb, alog, dskip_w, norm_w, _head_spread_matrix()))


def _lane_put(acc, col, r):
    lane = lax.broadcasted_iota(jnp.int32, acc.shape, 1)
    return jnp.where(lane == r, col, acc)


def _ssd_bwd(dyn, y, z, pre, dt_raw, hsave, dtb, alog, dskip_w, norm_w, side=None):
    n = pre.shape[0]
    nc = n // BLOCK
    spread = _head_spread_matrix()
    xs_s, b_s, c_s, lane_s, vec_s, wide_s, hs_s = _ssd_specs(nc, True)
    bc_out =pl.BlockSpec((BLOCK, SSD_STATE), lambda g, c: (nc - 1 - c, g))

    def body(dyn_ref, y_ref, z_ref, pxs_ref, pb_ref, pc_ref, dtr_ref, hs_ref, dtb_ref, alog_ref, dskw_ref, nw_ref,
             e_ref, r_ref,
             dz_ref, dxs_ref, dbm_ref, dcm_ref, ddt_ref, dnw_ref, ddtb_ref, dalog_ref, ddsk_ref, g_scr):
        step = pl.program_id(1)
        c = nc - 1 - step

        @pl.when(step == 0)
        def _():
            g_scr[...] = jnp.zeros_like(g_scr)

        pxs, pb, pc = pxs_ref[...], pb_ref[...], pc_ref[...]
        sx, sb, sc = _sigmoid(pxs), _sigmoid(pb), _sigmoid(pc)
        xs, bm, cm = pxs * sx, pb * sb, pc * sc
        valid, dtr, dt, a, lam, cs, causal = _ssd_decay(dtr_ref, dtb_ref, alog_ref, c)
        cst = cs.T
        cs_last = cs[BLOCK - 1:BLOCK, :]
        bmb = bm.astype(BF16)
        cmb = cm.astype(BF16)
        cb = _dot_nt(cmb, bmb)
        hg = hs_ref[0, 0]
        hgb = hg.astype(BF16)
        yoff = _dot_nt(cmb, hgb)
        gn = g_scr[...]
        gnb = gn.astype(BF16)

        zv = z_ref[...]
        yv = y_ref[...]
        sgz = _sigmoid(zv)
        sz = zv * sgz
        gts = yv * sz
        rr = lax.rsqrt(jnp.mean(gts * gts, axis=-1, keepdims=True) + EPS)
        xh = gts * rr
        dynv = dyn_ref[...]
        gg = dynv * nw_ref[...]
        dgts = rr * (gg - xh * jnp.mean(gg * xh, axis=-1, keepdims=True))
        dnw = jnp.sum(dynv * xh, axis=0, keepdims=True)
        dy = dgts * sz
        dz_ref[...] = (dgts * yv * (sgz * (1.0 + zv * (1.0 - sgz)))).astype(BF16)

        ecs = jnp.exp(cs)
        dec = jnp.exp(cs_last - cs)
        eh = jnp.exp(cs_last)
        dt_w, ecs_w, dec_w = _spread_heads([dt, ecs, dec], e_ref)
        red_m = r_ref[...]

        def head_sums(v):
            return _dot_sel(v, red_m, terms=2)

        xdt = xs * dt_w
        q_all = _dot_nt(bmb, gnb)
        w_all = (dy * ecs_w).astype(BF16)
        e_hl = head_sums(q_all * xdt) * dec
        dcs_col = head_sums(dy * yoff) * ecs - e_hl
        gh = jnp.zeros((1, LANES), F32)
        prod = gn * hg
        for r in range(SSD_HPG):
            gh = _lane_put(gh, _sum_all(prod[_head_cols(r), :]), r)
        dcs_last = jnp.sum(e_hl, axis=0, keepdims=True) + eh * gh
        ddsk = jnp.sum(head_sums(dy * xs), axis=0, keepdims=True)
        cbt = _dot_nt(bmb, cmb)
        lane = lax.broadcasted_iota(jnp.int32, (BLOCK, LANES), 1)
        first = lane < SSD_HEADDIM
        causal_t = lax.broadcasted_iota(jnp.int32, (BLOCK, BLOCK), 1) >= lax.broadcasted_iota(
            jnp.int32, (BLOCK, BLOCK), 0)
        sub = lax.broadcasted_iota(jnp.int32, (SUBLANES, BLOCK), 0)
        dcs_row = jnp.zeros((SUBLANES, BLOCK), F32)
        dcb = jnp.zeros((BLOCK, BLOCK), F32)
        dxdt_pairs = []
        for j in range(SSD_HPG // 2):
            tile = slice(LANES * j, LANES * (j + 1))
            dy_p = dy[:, tile]
            dyb = dy_p.astype(BF16)
            xdtb = xdt[:, tile].astype(BF16)
            res = []
            for half, r in enumerate((2 * j, 2 * j + 1)):
                csc, csr = cs[:, r:r + 1], cst[r:r + 1, :]
                lm = jnp.exp(jnp.where(causal, csc - csr, NEG))
                lmt = jnp.exp(jnp.where(causal_t, csr - csc, NEG))
                keep = first if half == 0 else jnp.logical_not(first)
                gm = _dot_nt(jnp.where(keep, dy_p, 0.0).astype(BF16), xdtb) * lm
                dcb = dcb + gm
                mm_ = gm * cb
                dcs_col = dcs_col + jnp.where(lane == r, jnp.sum(mm_, axis=1, keepdims=True), 0.0)
                dcs_row = jnp.where(sub == r, jnp.sum(mm_, axis=0, keepdims=True), dcs_row)
                res.append(_dot((cbt * lmt).astype(BF16), dyb))
            dxdt_pairs.append(jnp.where(first, res[0], res[1]))
        dxdt = jnp.concatenate(dxdt_pairs, axis=1) + q_all * dec_w
        ddt_x = head_sums(dxdt * xs)
        dxs = dxdt * dt_w + dskw_ref[...] * dy
        dcbb = dcb.astype(BF16)
        dcm = _dot(w_all, hgb) + _dot(dcbb, bmb)
        dbm = _dot((xdt * dec_w).astype(BF16), gnb) + _dot_tn(dcbb, cmb)
        dh_off = _dot_tn(w_all, cmb)
        for r in range(SSD_HPG):
            rows = _head_cols(r)
            g_scr[rows, :] = gn[rows, :] * eh[:, r:r + 1] + dh_off[rows, :]

        pad_rows = jnp.zeros((BLOCK - SUBLANES, BLOCK), F32)
        dcs = dcs_col - jnp.concatenate([dcs_row, pad_rows], axis=0).T
        rsel = lax.broadcasted_iota(jnp.int32, (BLOCK, LANES), 0)
        dcs = dcs + jnp.where(rsel == BLOCK - 1, dcs_last, 0.0)
        ri = lax.broadcasted_iota(jnp.int32, (BLOCK, BLOCK), 0)
        ci = lax.broadcasted_iota(jnp.int32, (BLOCK, BLOCK), 1)
        dlam = _sel_dot((ci >= ri).astype(BF16), dcs)
        head = lane < SSD_HPG
        ddt = dlam * a + ddt_x
        ddtr = jnp.where(head, ddt * _sigmoid(dtr) * valid, 0.0)
        ddt_ref[...] = ddtr.astype(BF16)
        dalog = jnp.sum(jnp.where(head, dlam * lam, 0.0), axis=0, keepdims=True)
        ddtb = jnp.sum(ddtr, axis=0, keepdims=True)

        dxs_ref[...] = dxs * (sx * (1.0 + pxs * (1.0 - sx)))
        dbm_ref[...] = dbm * (sb * (1.0 + pb * (1.0 - sb)))
        dcm_ref[...] = dcm * (sc * (1.0 + pc * (1.0 - sc)))

        @pl.when(step == 0)
        def _():
            dnw_ref[...] = dnw
            ddtb_ref[...] = ddtb
            dalog_ref[...] = dalog
            ddsk_ref[...] = ddsk

        @pl.when(step > 0)
        def _():
            dnw_ref[...] += dnw
            ddtb_ref[...] += ddtb
            dalog_ref[...] += dalog
            ddsk_ref[...] += ddsk

    return _call_with_side(
        body, side, name="ssd_bwd", grid=(SSD_GROUPS, nc),
        in_specs=[xs_s, xs_s, xs_s, xs_s, b_s, c_s, lane_s, hs_s, vec_s, vec_s, wide_s, wide_s,
                  _const_spec((LANES, SSD_GW)), _const_spec((SSD_GW, LANES))],
        out_specs=[xs_s, xs_s, bc_out, bc_out, lane_s, wide_s, vec_s, vec_s, vec_s],
        out_shape=[jax.ShapeDtypeStruct((n, SSD_INNER), BF16), jax.ShapeDtypeStruct((n, SSD_INNER), F32),
                   jax.ShapeDtypeStruct((n, SSD_BC), F32), jax.ShapeDtypeStruct((n, SSD_BC), F32),
                   jax.ShapeDtypeStruct((n, DT_W), BF16), jax.ShapeDtypeStruct((1, SSD_INNER), F32),
                   jax.ShapeDtypeStruct((1, DT_W), F32), jax.ShapeDtypeStruct((1, DT_W), F32),
                   jax.ShapeDtypeStruct((1, DT_W), F32)],
        scratch_shapes=[pltpu.VMEM((SSD_GW, SSD_STATE), F32)],
        args=(dyn, y, z, pre, pre, pre, dt_raw, hsave, dtb, alog, dskip_w, norm_w, spread, spread.T))


def _bucket_table():
    def bucket(dist):
        d = np.maximum(dist, 0)
        half = REL_BUCKETS // 2
        big = half + (np.log(np.maximum(d, half).astype(np.float32) / np.float32(half))
                      / np.float32(math.log(REL_MAX_DIST / half)) * np.float32(REL_BUCKETS - half)).astype(np.int32)
        return np.where(d < half, d, np.minimum(big, REL_BUCKETS - 1)).astype(np.int32)

    l = np.arange(BLOCK)[None, :]
    band = bucket(l + BLOCK - np.arange(2 * BLOCK)[:, None])
    j = np.arange(BLOCK)[:, None]
    tables = [np.concatenate([bucket(v * BLOCK + l - j), band], axis=0) for v in range(3)]
    return np.concatenate([t.reshape(-1) for t in tables])


def _onehot_t():
    buckets = jnp.asarray(_bucket_table())
    return (buckets[None, :] == jnp.arange(REL_BUCKETS, dtype=jnp.int32)[:, None]).astype(F32)


def _bias_tables(rel_t, onehot_t, side=None):
    def body(r_ref, oh_ref, o_ref):
        o_ref[...] = jnp.dot(r_ref[...], oh_ref[...], precision=HIGHEST, preferred_element_type=F32)

    outs, carried = _call_with_side(
        body, side, name="bias_tables", grid=(NT_ALL // NT_TILE,),
        in_specs=[pl.BlockSpec((ATT_HEADS, REL_BUCKETS), lambda i: (0, 0)),
                  pl.BlockSpec((REL_BUCKETS, NT_TILE), lambda i: (0, i))],
        out_specs=[pl.BlockSpec((ATT_HEADS, NT_TILE), lambda i: (0, i))],
        out_shape=[jax.ShapeDtypeStruct((ATT_HEADS, NT_ALL), F32)], scratch_shapes=[], args=(rel_t, onehot_t),
        semantics=("parallel",))
    return outs[0], carried


def _bias_grad(dtab, onehot_t):
    def body(d_ref, oh_ref, o_ref):
        i = pl.program_id(0)
        p = lax.dot_general(d_ref[...], oh_ref[...], (((1,), (1,)), ((), ())), precision=HIGHEST,
                            preferred_element_type=F32)

        @pl.when(i == 0)
        def _():
            o_ref[...] = p

        @pl.when(i > 0)
        def _():
            o_ref[...] += p

    return pl.pallas_call(
        body, name="bias_grad", grid=(NT_ALL // NT_TILE,),
        in_specs=[pl.BlockSpec((ATT_HEADS, NT_TILE), lambda i: (0, i)),
                  pl.BlockSpec((REL_BUCKETS, NT_TILE), lambda i: (0, i))],
        out_specs=pl.BlockSpec((ATT_HEADS, REL_BUCKETS), lambda i: (0, 0)),
        out_shape=jax.ShapeDtypeStruct((ATT_HEADS, REL_BUCKETS), F32),
        compiler_params=_cparams("arbitrary"))(dtab, onehot_t)


def _att_mask_t(n, copies):
    far = 4 * BLOCK
    kk = lax.broadcasted_iota(jnp.int32, (N_KEYS, copies * BLOCK), 0)
    li = lax.broadcasted_iota(jnp.int32, (N_KEYS, copies * BLOCK), 1) & (BLOCK - 1)
    meta_ok = (kk >= PAD) & (kk < BLOCK) & (li + jnp.where(n >= 1, far, 0) >= kk)
    prev_ok = (kk >= BLOCK) & (kk < 2 * BLOCK) & (kk - BLOCK > li + jnp.where(n >= 2, 0, far))
    cur_ok = (kk >= 2 * BLOCK) & (kk - 2 * BLOCK <= li - jnp.where(n >= 1, 0, far))
    return meta_ok | prev_ok | cur_ok


def _att_kv(meta_ref, prev_ref, cur_ref):
    kv = jnp.concatenate([meta_ref[...], prev_ref[...], cur_ref[...]], axis=0)
    first = lax.broadcasted_iota(jnp.int32, (N_KEYS, LANES), 1) < ATT_HEADDIM
    out = []
    for pair in (kv[:, :LANES], kv[:, LANES:]):
        swapped = pltpu.roll(pair, ATT_HEADDIM, 1)
        out.append([jnp.where(first, pair, swapped).astype(BF16), jnp.where(first, swapped, pair).astype(BF16)])
    return out[0], out[1]


def _split_heads(x_pair, first):
    return jnp.concatenate([jnp.where(first, x_pair, 0.0), jnp.where(first, 0.0, x_pair)], axis=0).astype(BF16)


def _att_probs_t(qm2, k_dup, t_ref, j, mask2, sink_ref):
    bias2 = jnp.concatenate([t_ref[0, 2 * j], t_ref[0, 2 * j + 1]], axis=1)
    second = lax.broadcasted_iota(jnp.int32, (1, 2 * BLOCK), 1) >= BLOCK
    sink2 = jnp.where(second, sink_ref[0:1, 2 * j + 1:2 * j + 2], sink_ref[0:1, 2 * j:2 * j + 1])
    s_t = jnp.where(mask2, _dot_nt(k_dup, qm2) + bias2, NEG)
    mx = jnp.maximum(jnp.max(s_t, axis=0, keepdims=True), sink2)
    p_t = jnp.exp(s_t - mx)
    p_s = jnp.exp(sink2 - mx)
    inv = 1.0 / (jnp.sum(p_t, axis=0, keepdims=True) + p_s)
    return p_t * inv, p_s * inv


def _att_specs(nb, rev):
    def nidx(i):
        return nb - 1 - i if rev else i

    kvb = ATT_Q // (2 * ATT_KV)
    q_s = pl.BlockSpec((BLOCK, ATT_Q), lambda i: (nidx(i), 0))
    cur = pl.BlockSpec((BLOCK, 2 * ATT_KV), lambda i: (nidx(i), kvb))
    prev = pl.BlockSpec((BLOCK, 2 * ATT_KV), lambda i: (jnp.maximum(nidx(i) - 1, 0), kvb))
    meta = pl.BlockSpec((BLOCK, 2 * ATT_KV), lambda i: (0, kvb))
    table = pl.BlockSpec((1, ATT_HEADS, N_KEYS, BLOCK), lambda i: (jnp.minimum(nidx(i), 2), 0, 0, 0))
    sink = pl.BlockSpec((1, LANES), lambda i: (0, 0))
    return q_s, cur, prev, meta, table, sink


def _attn_fwd(qkv, tables, sinks):
    n = qkv.shape[0]
    nb = n // BLOCK
    q_s, cur_s, prev_s, meta_s, t_s, sink_s = _att_specs(nb, False)

    def body(q_ref, cur_ref, prev_ref, meta_ref, t_ref, sink_ref, o_ref):
        blk = pl.program_id(0)
        mask_t = _att_mask_t(blk, 1)
        k_dup, v_dup = _att_kv(meta_ref, prev_ref, cur_ref)
        v_dup_t = [v.T for v in v_dup]
        first = lax.broadcasted_iota(jnp.int32, (BLOCK, LANES), 1) < ATT_HEADDIM
        top = lax.broadcasted_iota(jnp.int32, (LANES, BLOCK), 0) < ATT_HEADDIM
        for j in range(ATT_HEADS // 2):
            kh = 2 * j // ATT_GQ
            tile = slice(LANES * j, LANES * (j + 1))
            q_p = q_ref[:, tile] * ATT_SCALE
            res = []
            for half, h in enumerate((2 * j, 2 * j + 1)):
                qm = jnp.where(first if half == 0 else jnp.logical_not(first), q_p, 0.0).astype(BF16)
                sink = sink_ref[0:1, h:h + 1]
                s_t = jnp.where(mask_t, _dot_nt(k_dup[kh], qm) + t_ref[0, h], NEG)
                mx = jnp.maximum(jnp.max(s_t, axis=0, keepdims=True), sink)
                p_t = jnp.exp(s_t - mx)
                inv = 1.0 / (jnp.sum(p_t, axis=0, keepdims=True) + jnp.exp(sink - mx))
                res.append(_dot(v_dup_t[kh], (p_t * inv).astype(BF16)))
            o_ref[:, tile] = jnp.where(top, res[0], res[1]).T.astype(BF16)

    return pl.pallas_call(
        body, name="attn_fwd", grid=(nb,),
        in_specs=[q_s, cur_s, prev_s, meta_s, t_s, sink_s],
        out_specs=q_s,
        out_shape=jax.ShapeDtypeStruct((n, ATT_Q), BF16),
        compiler_params=_cparams("parallel"))(qkv, qkv, qkv, qkv, tables, sinks)


def _attn_bwd(datt, qkv, tables, sinks):
    n = qkv.shape[0]
    nb = n // BLOCK
    q_s, cur_s, prev_s, meta_s, t_s, sink_s = _att_specs(nb, True)
    dqkv_s = pl.BlockSpec((BLOCK, ATT_Q + 2 * ATT_KV), lambda i: (nb - 1 - i, 0))
    scale = ATT_SCALE

    def body(do_ref, q_ref, cur_ref, prev_ref, meta_ref, t_ref, sink_ref,
             dqkv_ref, dt_ref, dsink_ref, carry_scr, meta_scr):
        step = pl.program_id(0)
        blk = nb - 1 - step
        mask2 = _att_mask_t(blk, 2)
        k_dup, v_dup = _att_kv(meta_ref, prev_ref, cur_ref)
        k_dup_t = [k.T for k in k_dup]

        @pl.when(step == 0)
        def _():
            carry_scr[...] = jnp.zeros_like(carry_scr)
            meta_scr[...] = jnp.zeros_like(meta_scr)
            dsink_ref[...] = jnp.zeros_like(dsink_ref)

        @pl.when((step == 0) | (blk <= 1))
        def _():
            dt_ref[...] = jnp.zeros_like(dt_ref)

        first = lax.broadcasted_iota(jnp.int32, (BLOCK, LANES), 1) < ATT_HEADDIM
        top = lax.broadcasted_iota(jnp.int32, (LANES, BLOCK), 0) < ATT_HEADDIM
        first_k = lax.broadcasted_iota(jnp.int32, (N_KEYS, LANES), 1) < ATT_HEADDIM
        dsink = jnp.zeros((1, LANES), F32)
        dk_acc = [None] * ATT_KV_HEADS
        dv_acc = [None] * ATT_KV_HEADS
        for j in range(ATT_HEADS // 2):
            kh = 2 * j // ATT_GQ
            tile = slice(LANES * j, LANES * (j + 1))
            qm2 = _split_heads(q_ref[:, tile] * ATT_SCALE, first)
            dom2 = _split_heads(do_ref[:, tile], first)
            p_t, p_s = _att_probs_t(qm2, k_dup[kh], t_ref, j, mask2, sink_ref)
            dp_t = _dot_nt(v_dup[kh], dom2)
            delta = jnp.sum(p_t * dp_t, axis=0, keepdims=True)
            ds_t = p_t * (dp_t - delta)
            sink_terms = p_s * delta
            for half in range(2):
                cols = slice(BLOCK * half, BLOCK * (half + 1))
                dsink = _lane_put(dsink, -jnp.sum(sink_terms[:, cols], axis=1, keepdims=True), 2 * j + half)
                dt_ref[0, 2 * j + half] += ds_t[:, cols]
            ds_tb = ds_t.astype(BF16)
            dq_t = _dot(k_dup_t[kh], ds_tb)
            dqkv_ref[:, tile] = (jnp.where(top, dq_t[:, :BLOCK], dq_t[:, BLOCK:]).T * scale).astype(BF16)
            dk_part, dv_part = _dot(ds_tb, qm2), _dot(p_t.astype(BF16), dom2)
            dk_acc[kh] = dk_part if dk_acc[kh] is None else dk_acc[kh] + dk_part
            dv_acc[kh] = dv_part if dv_acc[kh] is None else dv_acc[kh] + dv_part
        dsink_ref[...] += dsink
        folded = [a + pltpu.roll(a, ATT_HEADDIM, 1) for a in dk_acc + dv_acc]
        dkv = jnp.concatenate([jnp.where(first_k, folded[0], folded[1]),
                               jnp.where(first_k, folded[2], folded[3])], axis=1)
        meta_scr[...] += dkv[:BLOCK, :]
        own = dkv[2 * BLOCK:, :] + carry_scr[...]
        carry_scr[...] = dkv[BLOCK:2 * BLOCK, :]

        @pl.when(blk > 0)
        def _():
            dqkv_ref[:, ATT_Q:] = own.astype(BF16)

        @pl.when(blk == 0)
        def _():
            dqkv_ref[:, ATT_Q:] = (own + meta_scr[...]).astype(BF16)

    return pl.pallas_call(
        body, name="attn_bwd", grid=(nb,),
        in_specs=[q_s, q_s, cur_s, prev_s, meta_s, t_s, sink_s],
        out_specs=[dqkv_s, t_s, sink_s],
        out_shape=[jax.ShapeDtypeStruct((n, ATT_Q + 2 * ATT_KV), BF16),
                   jax.ShapeDtypeStruct((3, ATT_HEADS, N_KEYS, BLOCK), F32),
                   jax.ShapeDtypeStruct((1, LANES), F32)],
        scratch_shapes=[pltpu.VMEM((BLOCK, 2 * ATT_KV), F32), pltpu.VMEM((BLOCK, 2 * ATT_KV), F32)],
        compiler_params=_cparams("arbitrary"))(datt, qkv, qkv, qkv, qkv, tables, sinks)


def _merge_out_fwd(gates, y_ssd, y_att, gate_b, w_out, h):
    n = gates.shape[0]
    tm = _row_tile(n, 416)

    def body(gs_ref, ga_ref, ys_ref, ya_ref, gb_ref, w_ref, h_ref, m_ref, o_ref):
        merged = (_sigmoid(gs_ref[...] + gb_ref[0:1, :]) * ys_ref[...]
                  + _sigmoid(ga_ref[...] + gb_ref[1:2, :]) * ya_ref[...]).astype(BF16)
        m_ref[...] = merged
        row = pl.program_id(0) * tm + lax.broadcasted_iota(jnp.int32, (tm, 1), 0)
        o_ref[...] = jnp.where(row >= PAD, _dot(merged, w_ref[...]), 0.0) + h_ref[...]

    row = pl.BlockSpec((tm, D_MODEL), lambda i: (i, 0))
    return pl.pallas_call(
        body, name="merge_out_fwd", grid=(n // tm,),
        in_specs=[row, pl.BlockSpec((tm, D_MODEL), lambda i: (i, 1)), row, row,
                  pl.BlockSpec((2, D_MODEL), lambda i: (0, 0)), pl.BlockSpec((D_MODEL, D_MODEL), lambda i: (0, 0)), row],
        out_specs=[row, row],
        out_shape=[jax.ShapeDtypeStruct((n, D_MODEL), BF16), jax.ShapeDtypeStruct((n, D_MODEL), F32)],
        compiler_params=_cparams("parallel"))(gates, gates, y_ssd, y_att, gate_b, w_out, h)


def _merge_out_bwd(dh, w_out, gates, y_ssd, y_att, gate_b):
    n = gates.shape[0]
    tm = _row_tile(n, 416)

    def body(dh_ref, w_ref, gs_ref, ga_ref, ys_ref, ya_ref, gb_ref, dys_ref, dya_ref, dg_ref, dgb_ref):
        i = pl.program_id(0)
        row = i * tm + lax.broadcasted_iota(jnp.int32, (tm, 1), 0)
        dmv = jnp.where(row >= PAD, _dot_nt(dh_ref[...].astype(BF16), w_ref[...]), 0.0)
        ss =_sigmoid(gs_ref[...] + gb_ref[0:1, :])
        sa = _sigmoid(ga_ref[...] + gb_ref[1:2, :])
        dys_ref[...] = (dmv * ss).astype(BF16)
        dya_ref[...] = (dmv * sa).astype(BF16)
        dgs = dmv * ys_ref[...] * ss * (1.0 - ss)
        dga = dmv * ya_ref[...] * sa * (1.0 - sa)
        dg_ref[:, :D_MODEL] = dgs.astype(BF16)
        dg_ref[:, D_MODEL:] = dga.astype(BF16)
        part = jnp.concatenate([jnp.sum(dgs, axis=0, keepdims=True), jnp.sum(dga, axis=0, keepdims=True)], axis=0)

        @pl.when(i == 0)
        def _():
            dgb_ref[...] = part

        @pl.when(i > 0)
        def _():
            dgb_ref[...] += part

    row = pl.BlockSpec((tm, D_MODEL), lambda i: (i, 0))
    gb = pl.BlockSpec((2, D_MODEL), lambda i: (0, 0))
    return pl.pallas_call(
        body, name="merge_out_bwd", grid=(n // tm,),
        in_specs=[row, pl.BlockSpec((D_MODEL, D_MODEL), lambda i: (0, 0)), row,
                  pl.BlockSpec((tm, D_MODEL), lambda i: (i, 1)), row, row, gb],
        out_specs=[row, row, pl.BlockSpec((tm, 2 * D_MODEL), lambda i: (i, 0)), gb],
        out_shape=[jax.ShapeDtypeStruct((n, D_MODEL), BF16), jax.ShapeDtypeStruct((n, D_MODEL), BF16),
                   jax.ShapeDtypeStruct((n, 2 * D_MODEL), BF16), jax.ShapeDtypeStruct((2, D_MODEL), F32)],
        compiler_params=_cparams("arbitrary"))(dh, w_out, gates, gates, y_ssd, y_att, gate_b)


def _col_move(srcs, outs, pieces, *, name):
    rows = srcs[0].shape[-2]
    tr = _row_tile(rows, 128)
    n_src = len(srcs)
    covered = [sum(p[6] for p in pieces if p[0] == o) for o in range(len(outs))]
    total = [int(np.prod(shp)) // rows for shp, _ in outs]

    def body(*refs):
        in_refs, out_refs = refs[:n_src], refs[n_src:]
        for o, ref in enumerate(out_refs):
            if covered[o] != total[o]:
                ref[...] = jnp.zeros_like(ref)
        for o, ol, oc, s, sl, sc, width in pieces:
            val = in_refs[s][:, sc:sc + width] if sl is None else in_refs[s][sl, :, sc:sc + width]
            val = val.astype(outs[o][1])
            if ol is None:
                out_refs[o][:, oc:oc + width] = val
            else:
                out_refs[o][ol, :, oc:oc + width] = val

    def spec(shape):
        if len(shape) == 2:
            return pl.BlockSpec((tr, shape[1]), lambda i: (i, 0))
        return pl.BlockSpec((shape[0], tr, shape[2]), lambda i: (0, i, 0))

    return pl.pallas_call(
        body, name=name, grid=(rows // tr,),
        in_specs=[spec(a.shape) for a in srcs], out_specs=[spec(shp) for shp, _ in outs],
        out_shape=[jax.ShapeDtypeStruct(shp, dt) for shp, dt in outs],
        compiler_params=_cparams("parallel"))(*srcs)


def _shard_pieces(seg_ranges, shard_w):
    out = []
    for seg, runs in enumerate(seg_ranges):
        for g0, width, s0 in runs:
            done = 0
            while done < width:
                dev, col = divmod(g0 + done, shard_w)
                take = min(width - done, shard_w - col)
                out.append((seg, s0 + done, dev, col, take))
                done += take
    return out


_CHIP_RELATIONS = [(1, 0, 0), (0, 1, 0), (1, 1, 0)]
N_CHIPS = 4


class _CommPlan:
    def __init__(self, arrays, out_shape, scratch_shapes, phases):
        self.arrays, self.out_shape, self.scratch_shapes, self.phases = arrays, out_shape, scratch_shapes, phases


def _gather_plan(arrays):
    n_arr = len(arrays)
    n_chips = len(_CHIP_RELATIONS)
    n_pair = 1 + 2 * n_chips

    def where():
        x, y, c = lax.axis_index("x"), lax.axis_index("y"), lax.axis_index("c")
        return x, y, c, (x, y, 1 - c), [(x ^ dx, y ^ dy) for dx, dy, _ in _CHIP_RELATIONS]

    def copy(outs, sems, a, k, block, to, src=None):
        slot = outs[a].at[2 * block[0] + block[1], block[2]]
        return pltpu.make_async_remote_copy(
            src_ref=slot if src is None else src, dst_ref=slot, send_sem=sems[0].at[a * n_pair + k],
            recv_sem=sems[1].at[a * n_pair + k], device_id=to, device_id_type=MESH)

    def mine(ins, outs, sems, a, x, y, c):
        return pltpu.make_async_copy(ins[a], outs[a].at[2 * x + y, c], sems[2].at[a])

    def first_copies(ins, outs, sems, a, x, y, c, sibling, chips):
        return ([copy(outs, sems, a, 0, (x, y, c), sibling, src=ins[a])]
                + [copy(outs, sems, a, 1 + j, (x, y, c), (*chip, c), src=ins[a]) for j, chip in enumerate(chips)])

    def start(ins, outs, sems):
        x, y, c, sibling, chips = where()
        for a in range(n_arr):
            mine(ins, outs, sems, a, x, y, c).start()
            for cp in first_copies(ins, outs, sems, a, x, y, c, sibling, chips):
                cp.start()

    def pass_on(ins, outs, sems):
        x, y, c, sibling, chips = where()
        for j, chip in enumerate(chips):
            for a in range(n_arr):
                copy(outs, sems, a, 1 + j, (*chip, c), (x, y, c)).wait_recv()
                copy(outs, sems, a, 1 + n_chips + j, (*chip, c), sibling).start()

    def finish(ins, outs, sems):
        x, y, c, sibling, chips = where()
        for a in range(n_arr):
            copy(outs, sems, a, 0, (x, y, 1 - c), (x, y, c)).wait_recv()
            for j, chip in enumerate(chips):
                copy(outs, sems, a, 1 + n_chips + j, (*chip, 1 - c), (x, y, c)).wait_recv()
        for a in range(n_arr):
            for cp in first_copies(ins, outs, sems, a, x, y, c, sibling, chips):
                cp.wait_send()
            for j, chip in enumerate(chips):
                copy(outs, sems, a, 1 + n_chips + j, (*chip, c), sibling).wait_send()
            mine(ins, outs, sems, a, x, y, c).wait()

    return _CommPlan(
        arrays, [jax.ShapeDtypeStruct((N_CHIPS, 2) + a.shape, a.dtype) for a in arrays],
        [pltpu.SemaphoreType.DMA((n_arr * n_pair,)), pltpu.SemaphoreType.DMA((n_arr * n_pair,)),
         pltpu.SemaphoreType.DMA((n_arr,))],
        (start, pass_on, finish))


_ALL_RELATIONS = [(dx, dy, dc) for dx in (0, 1) for dy in (0, 1) for dc in (0, 1)][1:]


def _all_to_all_plan(arrays, scatter=None):
    n_arr = len(arrays)
    n_rel = len(_ALL_RELATIONS)
    scatter = scatter or [True] * n_arr

    def block(ins, a, p):
        return ins[a].at[p] if scatter[a] else ins[a]

    def local_copies(ins, outs, sems):
        me = 4 * lax.axis_index("x") + 2 * lax.axis_index("y") + lax.axis_index("c")
        return [pltpu.make_async_copy(block(ins, a, me), outs[a].at[me], sems[2].at[a]) for a in range(n_arr)]

    def remote_copies(ins, outs, sems, arrivals):
        x, y, c = lax.axis_index("x"), lax.axis_index("y"), lax.axis_index("c")
        me = 4 * x + 2 * y + c
        out = []
        for k, (dx, dy, dc) in enumerate(_ALL_RELATIONS):
            px, py, pc = x ^ dx, y ^ dy, c ^ dc
            peer = 4 * px + 2 * py + pc
            for a in range(n_arr):
                out.append(pltpu.make_async_remote_copy(
                    src_ref=block(ins, a, peer), dst_ref=outs[a].at[peer if arrivals else me],
                    send_sem=sems[0].at[a * n_rel + k], recv_sem=sems[1].at[a * n_rel + k],
                    device_id=(x, y, c) if arrivals else (px, py, pc), device_id_type=MESH))
        return out

    def start(ins, outs, sems):
        for cp in local_copies(ins, outs, sems) + remote_copies(ins, outs, sems, False):
            cp.start()

    def pass_on(ins, outs, sems):
        pass

    def finish(ins, outs, sems):
        for send in remote_copies(ins, outs, sems, False):
            send.wait_send()
        for arrival in remote_copies(ins, outs, sems, True):
            arrival.wait_recv()
        for cp in local_copies(ins, outs, sems):
            cp.wait()

    return _CommPlan(
        arrays, [jax.ShapeDtypeStruct(a.shape if s else (N_DEV,) + a.shape, a.dtype) for a, s in zip(arrays, scatter)],
        [pltpu.SemaphoreType.DMA((n_arr * n_rel,)), pltpu.SemaphoreType.DMA((n_arr * n_rel,)),
         pltpu.SemaphoreType.DMA((n_arr,))],
        (start, pass_on, finish))


def _adamw(w, gslots, m, v, *, name, side=None):
    rows, cols = w.shape
    n_slots = gslots.shape[0]
    tr = _row_tile(rows, 128) if rows % 16 == 0 else rows

    def body(w_ref, g_ref, m_ref, v_ref, go_ref, d_ref, mo_ref, vo_ref):
        g = g_ref[0].astype(F32)
        for s in range(1, n_slots):
            g = g + g_ref[s].astype(F32)
        mn = ADAM_B1 * m_ref[...] + (1.0 - ADAM_B1) * g
        vn = ADAM_B2 * v_ref[...] + (1.0 - ADAM_B2) * (g * g)
        go_ref[...] = g
        mo_ref[...] = mn
        vo_ref[...] = vn
        m_hat = mn / (1.0 - ADAM_B1 ** ADAM_STEP)
        v_hat = vn / (1.0 - ADAM_B2 ** ADAM_STEP)
        d_ref[...] = -ADAM_LR * (m_hat / (jnp.sqrt(v_hat) + ADAM_EPS) + ADAM_WD * w_ref[...])

    blk = pl.BlockSpec((tr, cols), lambda i: (i, 0))
    shp = jax.ShapeDtypeStruct((rows, cols), F32)
    outs, carried = _call_with_side(
        body, side, name=name, grid=(rows // tr,),
        in_specs=[blk, pl.BlockSpec((n_slots, tr, cols), lambda i: (0, i, 0)), blk, blk],
        out_specs=[blk] * 4, out_shape=[shp] * 4, scratch_shapes=[], args=(w, gslots, m, v), semantics=("parallel",))
    return outs if side is None else (outs, carried)


_BIG = ("w_in", "w_ssd_branch", "w_attn_branch", "w_out", "w_ffn_in", "w_ffn_out")
_SMALL_SHARDED = ("meta_tokens", "ssd_conv_w", "gate_b", "ffn_conv_w")
_SMALL_REPLICATED = ("norm_mix_w", "ssd_conv_b", "ssd_dt_bias", "ssd_a_log", "ssd_d", "ssd_norm_w", "attn_sinks",
                     "rel_bias", "norm_ffn_w", "ffn_conv_b", "norm_final_w")
_WEIGHTS = ("meta_tokens", "norm_mix_w", "w_in", "ssd_conv_w", "ssd_conv_b", "ssd_dt_bias", "ssd_a_log", "ssd_d",
            "ssd_norm_w", "w_ssd_branch", "w_attn_branch", "attn_sinks", "rel_bias", "gate_b", "w_out", "norm_ffn_w",
            "w_ffn_in", "ffn_conv_w", "ffn_conv_b", "w_ffn_out", "norm_final_w")
_ROW_SHARDED = ("w_ssd_branch", "w_attn_branch", "w_out", "w_ffn_out")
_COL_SHARDED = ("w_in", "w_ffn_in", "meta_tokens", "ssd_conv_w", "gate_b", "ffn_conv_w")
_IN_SEGS = (("z", SSD_INNER), ("xbc", SSD_XBC), ("dt", SSD_HEADS), ("qkv", ATT_Q + 2 * ATT_KV), ("g", 2 * D_MODEL))


def _pack_rows(flat_parts, width, row_mult):
    flat = jnp.concatenate([p.reshape(-1) for p in flat_parts])
    pad = (-flat.shape[0]) % (width * row_mult)
    if pad:
        flat = jnp.concatenate([flat, jnp.zeros((pad,), flat.dtype)])
    return flat.reshape(-1, width)


def _unpack(flat, shapes):
    out, off = [], 0
    for shp in shapes:
        size = int(np.prod(shp))
        out.append(flat[off:off + size].reshape(shp))
        off += size
    return out


def _gather_full(stack, name, shard_shape):
    if name in _COL_SHARDED:
        return jnp.transpose(stack, (1, 0, 2)).reshape(shard_shape[0], N_DEV * shard_shape[1])
    return stack.reshape(N_DEV * shard_shape[0], shard_shape[1])


_IN_SEG_W = {"z": SSD_INNER, "xbc": SSD_XBC, "dt": DT_W, "qkv": ATT_Q + 2 * ATT_KV, "g": 2 * D_MODEL}
_IN_SHARD_W = (SSD_INNER + SSD_XBC + SSD_HEADS + ATT_Q + 2 * ATT_KV + 2 * D_MODEL) // N_DEV
_FFN_SHARD_W = 2 * D_FF // N_DEV


def _in_seg_runs():
    runs, off = [], 0
    for nm, width in _IN_SEGS:
        if nm == "dt":
            runs.append([(off + SSD_HPG * g, SSD_HPG, LANES * g) for g in range(SSD_GROUPS)])
        else:
            runs.append([(off, width, 0)])
        off += width
    return runs


def _w_in_to_segments(stack):
    pieces = [(seg, None, scol, 0, dev, col, w) for seg, scol, dev, col, w in _shard_pieces(_in_seg_runs(), _IN_SHARD_W)]
    outs = [((D_MODEL, _IN_SEG_W[nm]), stack.dtype) for nm, _ in _IN_SEGS]
    return dict(zip([nm for nm, _ in _IN_SEGS], _col_move([stack], outs, pieces, name="w_in_segments")))


def _segments_to_w_in_shards(seg_grads):
    pieces = [(0, dev, col, seg, None, scol, w) for seg, scol, dev, col, w in _shard_pieces(_in_seg_runs(), _IN_SHARD_W)]
    return _col_move(seg_grads, [((N_DEV, D_MODEL, _IN_SHARD_W), seg_grads[0].dtype)], pieces, name="g_w_in_shards")[0]


def _ffn_in_from_shards(stack):
    pieces = [(0, None, scol, 0, dev, col, w)
              for _, scol, dev, col, w in _shard_pieces([[(0, 2 * D_FF, 0)]], _FFN_SHARD_W)]
    return _col_move([stack], [((D_MODEL, 2 * D_FF), stack.dtype)], pieces, name="w_ffn_in_full")[0]


def _ffn_in_to_shards(g_up, g_gate):
    pieces = [(0, dev, col, seg, None, scol, w)
              for seg, scol, dev, col, w in _shard_pieces([[(0, D_FF, 0)], [(D_FF, D_FF, 0)]], _FFN_SHARD_W)]
    return _col_move([g_up, g_gate], [((N_DEV, D_MODEL, _FFN_SHARD_W), g_up.dtype)], pieces, name="g_w_ffn_in_shards")[0]


def _dt_spread(w_dt):
    k = w_dt.shape[0]
    w4 = w_dt.reshape(k, SSD_GROUPS, SSD_HPG)
    return jnp.pad(w4, ((0, 0), (0, 0), (0, LANES - SSD_HPG))).reshape(k, DT_W)


def _dt_gather(w_wide):
    k = w_wide.shape[0]
    return w_wide.reshape(k, SSD_GROUPS, LANES)[:, :, :SSD_HPG].reshape(k, SSD_HEADS)


class _LateExchanges:
    def __init__(self, two_d, shape2):
        self.two_d, self.shape2 = two_d, shape2
        self.early_grads_received = None
        self.w_in_grads_received = None

    def row_pack(self, tree):
        return jnp.concatenate([tree[k] for k in _ROW_SHARDED], axis=0)

    def late_weights_plan(self):
        return _gather_plan([self.two_d["w_ffn_in"].astype(BF16), self.row_pack(self.two_d).astype(BF16)])

    def late_weights(self, gathered):
        w_ffn_in_all, rows_all = [g.reshape((N_DEV,) + g.shape[2:]) for g in gathered]
        out = {"w_ffn_in": _ffn_in_from_shards(w_ffn_in_all)}
        off = 0
        for k in _ROW_SHARDED:
            r = self.shape2[k][0]
            out[k] = rows_all[:, off:off + r].reshape(N_DEV * r, D_MODEL)
            off += r
        return out

    def early_grads_plan(self, grads):
        rows_send = jnp.concatenate([grads[k].reshape(N_DEV, self.shape2[k][0], D_MODEL) for k in _ROW_SHARDED], axis=1)
        return _all_to_all_plan([_ffn_in_to_shards(*grads["w_ffn_in"]), rows_send])

    def w_in_grads_plan(self, seg_grads):
        return _all_to_all_plan([_segments_to_w_in_shards(seg_grads)])


def _local_step(x, target, w, exchanges=None):
    h0 = jnp.concatenate([jnp.zeros((PAD, D_MODEL), F32), w["meta_tokens"], x], axis=0)
    segs = w["in_segs"]

    dtb = _dt_spread(w["ssd_dt_bias"])
    alog = _dt_spread(w["ssd_a_log"])
    dskip_w = jnp.repeat(w["ssd_d"], SSD_HEADDIM, axis=1)
    sinks = jnp.pad(w["attn_sinks"], ((0, 0), (0, LANES - ATT_HEADS)))
    onehot_t = w["onehot_t"] if "onehot_t" in w else _onehot_t()
    flat_tables = w["bias_tables"] if "bias_tables" in w else _bias_tables(w["rel_bias"].T, onehot_t)[0]
    tables = jnp.transpose(flat_tables.reshape(ATT_HEADS, 3, N_KEYS, BLOCK), (1, 0, 2, 3))

    u = _rms_fwd(h0, w["norm_mix_w"], name="rms_mix_fwd")
    z = _mm(u, segs["z"], name="in_z")
    xbc, pre = _mm_conv_fwd(u, segs["xbc"], w["ssd_conv_w"], w["ssd_conv_b"], name="in_xbc_conv_fwd")
    dt_raw = _mm(u, segs["dt"], name="in_dt")
    qkv = _mm(u, segs["qkv"], out_dtype=BF16, name="in_qkv")
    gates = _mm(u, segs["g"], name="in_g")
    (y, yn, hsave), gathered = _ssd_fwd(pre, dt_raw, z, dtb, alog, dskip_w, w["ssd_norm_w"],
                                        side=None if exchanges is None else exchanges.late_weights_plan())
    if exchanges is not None:
        w = {**w, **exchanges.late_weights(gathered)}
    w_ffn_up, w_ffn_gate = w["w_ffn_in"][:, :D_FF], w["w_ffn_in"][:, D_FF:]
    y_ssd = _mm(yn, w["w_ssd_branch"], out_dtype=BF16, name="ssd_out")
    att = _attn_fwd(qkv, tables, sinks)
    y_att = _mm(att, w["w_attn_branch"], out_dtype=BF16, name="att_out")
    merged, h1 = _merge_out_fwd(gates, y_ssd, y_att, w["gate_b"], w["w_out"], h0)
    u2 = _rms_fwd(h1, w["norm_ffn_w"], name="rms_ffn_fwd")
    x_up, x_gate, hid_up, hid_gate, act = _ffn_in_act_fwd(u2, w["w_ffn_in"], w["ffn_conv_w"], w["ffn_conv_b"])
    h2 = _mm(act, w["w_ffn_out"], c=h1, mask=True, name="ffn_out")
    dh2, dh2_b, loss_row, g_norm_final = _final_loss(h2, w["norm_final_w"], target)

    grads = {"norm_final_w": g_norm_final}
    grads["w_ffn_out"] = _mm(act, dh2_b, ta=True, mask=True, out_dtype=BF16, name="g_w_ffn_out")
    dx_up, dx_gate, dcw_up, dcw_gate, dcb_up, dcb_gate = _ffn_out_act_bwd(
        dh2_b, w["w_ffn_out"], hid_up, hid_gate, x_up, x_gate, w["ffn_conv_w"])
    grads["ffn_conv_w"] = jnp.concatenate([dcw_up, dcw_gate], axis=1)
    grads["ffn_conv_b"] = jnp.concatenate([dcb_up, dcb_gate], axis=1)
    (dh1, grads["norm_ffn_w"]), _ = _mm_rms_bwd([(dx_up, w_ffn_up), (dx_gate, w_ffn_gate)], h1, w["norm_ffn_w"], dh2,
                                                name="d_u2_rms_bwd")
    grads["w_ffn_in"] = (_mm(u2, dx_up, ta=True, out_dtype=BF16, name="g_w_ffn_up"),
                         _mm(u2, dx_gate, ta=True, out_dtype=BF16, name="g_w_ffn_gate"))

    grads["w_out"] = _mm(merged, dh1, ta=True, mask=True, out_dtype=BF16, name="g_w_out")
    dy_ssd, dy_att, dgates, grads["gate_b"] = _merge_out_bwd(dh1, w["w_out"], gates, y_ssd, y_att, w["gate_b"])
    dyn = _mm(dy_ssd, w["w_ssd_branch"], tb=True, name="d_yn")
    grads["w_ssd_branch"] = _mm(yn, dy_ssd, ta=True, out_dtype=BF16, name="g_w_ssd")
    datt = _mm(dy_att, w["w_attn_branch"], tb=True, out_dtype=BF16, name="d_att")
    grads["w_attn_branch"] = _mm(att, dy_att, ta=True, out_dtype=BF16, name="g_w_att")
    (dz, dpxs, dpb, dpc, ddt, grads["ssd_norm_w"], g_dtb, g_alog, g_dskip), received = _ssd_bwd(
        dyn, y, z, pre, dt_raw, hsave, dtb, alog, dskip_w, w["ssd_norm_w"],
        side=None if exchanges is None else exchanges.early_grads_plan(grads))
    if exchanges is not None:
        exchanges.early_grads_received = received
    grads["ssd_dt_bias"] = _dt_gather(g_dtb)
    grads["ssd_a_log"] = _dt_gather(g_alog)
    grads["ssd_d"] = _dt_gather(g_dskip)
    conv_g = _conv_bwd(dpxs, xbc, w["ssd_conv_w"], name="ssd_conv_bwd_x")
    conv_g = _conv_bwd(dpb, xbc, w["ssd_conv_w"], name="ssd_conv_bwd_b", col0=SSD_INNER, into=conv_g)
    dxbc, grads["ssd_conv_w"], grads["ssd_conv_b"] = _conv_bwd(
        dpc, xbc, w["ssd_conv_w"], name="ssd_conv_bwd_c", col0=SSD_INNER + SSD_BC, into=conv_g)
    dqkv, d_tables, d_sinks = _attn_bwd(datt, qkv, tables, sinks)
    grads["attn_sinks"] = d_sinks[:, :ATT_HEADS]
    dtab = jnp.transpose(d_tables, (1, 0, 2, 3)).reshape(ATT_HEADS, NT_ALL)
    grads["rel_bias"] = _bias_grad(dtab, onehot_t).T
    dsegs = {"z": dz, "xbc": dxbc, "dt": ddt, "qkv": dqkv, "g": dgates}
    grads["in_segs"] = [_mm(u, dsegs[nm], ta=True, out_dtype=BF16, name="g_w_in_" + nm) for nm, _ in _IN_SEGS]
    (dh0, grads["norm_mix_w"]), received = _mm_rms_bwd(
        [(dsegs[nm], segs[nm]) for nm, _ in _IN_SEGS], h0, w["norm_mix_w"], dh1, name="d_u_rms_bwd",
        side=None if exchanges is None else exchanges.w_in_grads_plan(grads["in_segs"]))
    if exchanges is not None:
        exchanges.w_in_grads_received = received[0]
    grads["meta_tokens"] = dh0[PAD:BLOCK]
    return loss_row[0, 0], dh0[BLOCK:], grads


def kernel(x, meta_tokens, norm_mix_w, w_in, ssd_conv_w, ssd_conv_b, ssd_dt_bias, ssd_a_log, ssd_d, ssd_norm_w, w_ssd_branch, w_attn_branch, attn_sinks, rel_bias, gate_b, w_out, norm_ffn_w, w_ffn_in, ffn_conv_w, ffn_conv_b, w_ffn_out, norm_final_w, loss_target, m_meta_tokens, m_norm_mix_w, m_w_in, m_ssd_conv_w, m_ssd_conv_b, m_ssd_dt_bias, m_ssd_a_log, m_ssd_d, m_ssd_norm_w, m_w_ssd_branch, m_w_attn_branch, m_attn_sinks, m_rel_bias, m_gate_b, m_w_out, m_norm_ffn_w, m_w_ffn_in, m_ffn_conv_w, m_ffn_conv_b, m_w_ffn_out, m_norm_final_w, v_meta_tokens, v_norm_mix_w, v_w_in, v_ssd_conv_w, v_ssd_conv_b, v_ssd_dt_bias, v_ssd_a_log, v_ssd_d, v_ssd_norm_w, v_w_ssd_branch, v_w_attn_branch, v_attn_sinks, v_rel_bias, v_gate_b, v_w_out, v_norm_ffn_w, v_w_ffn_in, v_ffn_conv_w, v_ffn_conv_b, v_w_ffn_out, v_norm_final_w):
    shard = dict(meta_tokens=meta_tokens, norm_mix_w=norm_mix_w, w_in=w_in, ssd_conv_w=ssd_conv_w,
                 ssd_conv_b=ssd_conv_b, ssd_dt_bias=ssd_dt_bias, ssd_a_log=ssd_a_log, ssd_d=ssd_d,
                 ssd_norm_w=ssd_norm_w, w_ssd_branch=w_ssd_branch, w_attn_branch=w_attn_branch,
                 attn_sinks=attn_sinks, rel_bias=rel_bias, gate_b=gate_b, w_out=w_out, norm_ffn_w=norm_ffn_w,
                 w_ffn_in=w_ffn_in, ffn_conv_w=ffn_conv_w, ffn_conv_b=ffn_conv_b, w_ffn_out=w_ffn_out,
                 norm_final_w=norm_final_w)
    mom_m = dict(zip(_WEIGHTS, (m_meta_tokens, m_norm_mix_w, m_w_in, m_ssd_conv_w, m_ssd_conv_b, m_ssd_dt_bias,
                                m_ssd_a_log, m_ssd_d, m_ssd_norm_w, m_w_ssd_branch, m_w_attn_branch, m_attn_sinks,
                                m_rel_bias, m_gate_b, m_w_out, m_norm_ffn_w, m_w_ffn_in, m_ffn_conv_w, m_ffn_conv_b,
                                m_w_ffn_out, m_norm_final_w)))
    mom_v = dict(zip(_WEIGHTS, (v_meta_tokens, v_norm_mix_w, v_w_in, v_ssd_conv_w, v_ssd_conv_b, v_ssd_dt_bias,
                                v_ssd_a_log, v_ssd_d, v_ssd_norm_w, v_w_ssd_branch, v_w_attn_branch, v_attn_sinks,
                                v_rel_bias, v_gate_b, v_w_out, v_norm_ffn_w, v_w_ffn_in, v_ffn_conv_w, v_ffn_conv_b,
                                v_w_ffn_out, v_norm_final_w)))
    orig_shape = {k: a.shape for k, a in shard.items()}
    two_d = {k: a.reshape(a.shape[-2:]) if a.ndim >= 2 else a.reshape(1, -1) for k, a in shard.items()}
    shape2 = {k: a.shape for k, a in two_d.items()}

    def as2d(tree):
        return {k: tree[k].reshape(shape2[k]) for k in _WEIGHTS}

    mom_m, mom_v = as2d(mom_m), as2d(mom_v)

    exchanges = _LateExchanges(two_d, shape2)
    row_pack = exchanges.row_pack
    small_pack = _pack_rows([two_d[k] for k in _SMALL_SHARDED], LANES, SMALL_ROW_MULT)
    onehot_t = _onehot_t()
    gather = [two_d["w_in"].astype(BF16), small_pack]
    flat_tables, gathered = _bias_tables(two_d["rel_bias"].T, onehot_t, side=_gather_plan(gather))
    w_in_all, small_all = [g.reshape((N_DEV,) + a.shape) for g, a in zip(gathered, gather)]
    full = {k: two_d[k] for k in _SMALL_REPLICATED}
    full["onehot_t"], full["bias_tables"] = onehot_t, flat_tables
    full["in_segs"] = _w_in_to_segments(w_in_all)
    small_flat = small_all.reshape(N_DEV, -1)
    off = 0
    for k in _SMALL_SHARDED:
        size = int(np.prod(shape2[k]))
        full[k] = _gather_full(small_flat[:, off:off + size].reshape((N_DEV,) + shape2[k]), k, shape2[k])
        off += size

    loss_local, grad_x, grads = _local_step(x[0], loss_target[0], full, exchanges)

    small_names = _SMALL_SHARDED + _SMALL_REPLICATED
    small_send = _pack_rows([grads[k] for k in small_names] + [loss_local.reshape(1)], LANES, SMALL_ROW_MULT)
    in_recv = exchanges.w_in_grads_received
    ffn_recv, rows_recv = exchanges.early_grads_received

    w_in_out, (small_recv,) = _adamw(two_d["w_in"], in_recv, mom_m["w_in"], mom_v["w_in"], name="adamw_w_in",
                                     side=_all_to_all_plan([small_send], [False]))
    big = {"w_in": w_in_out,
           "w_ffn_in": _adamw(two_d["w_ffn_in"], ffn_recv, mom_m["w_ffn_in"], mom_v["w_ffn_in"], name="adamw_w_ffn_in")}
    rows_out = _adamw(row_pack(two_d), rows_recv, row_pack(mom_m), row_pack(mom_v), name="adamw_rows")
    off = 0
    for k in _ROW_SHARDED:
        r = shape2[k][0]
        big[k] = [a[off:off + r] for a in rows_out]
        off += r
    me =4 * lax.axis_index("x") + 2 * lax.axis_index("y") + lax.axis_index("c")
    small_full_shapes = [grads[k].shape for k in small_names]
    n_small = sum(int(np.prod(s)) for s in small_full_shapes)

    def packed_small(tree):
        parts = []
        for k in small_names:
            a = tree[k]
            if k in _SMALL_SHARDED:
                fullw = jnp.zeros(grads[k].shape, F32)
                a = lax.dynamic_update_slice(fullw, a, (0, me * a.shape[1]))
            parts.append(a)
        return _pack_rows(parts + [jnp.zeros((1,), F32)], LANES, SMALL_ROW_MULT)

    g_small, d_small, m_small, v_small = _adamw(packed_small(two_d), small_recv, packed_small(mom_m),
                                                packed_small(mom_v), name="adamw_small")

    def unpack_all(which, small):
        out = {k: big[k][which] for k in _BIG}
        flat = small.reshape(-1)
        for k, a in zip(small_names, _unpack(flat, small_full_shapes)):
            if k in _SMALL_SHARDED:
                a = lax.dynamic_slice(a, (0, me * shape2[k][1]), shape2[k])
            out[k] = a
        return out, flat[n_small]

    g_all, loss = unpack_all(0, g_small)
    d_all, _ = unpack_all(1, d_small)
    m_all, _ = unpack_all(2, m_small)
    v_all, _ = unpack_all(3, v_small)

    def final(tree):
        return [tree[k].reshape(orig_shape[k]) for k in _WEIGHTS]

    return (loss, grad_x[None], *final(g_all), *final(d_all), *final(m_all), *final(v_all))
```

```python
import functools
import math

import numpy as np
import jax
import jax.numpy as jnp
from jax import lax
from jax.experimental import pallas as pl
from jax.experimental.pallas import tpu as pltpu

F32 = jnp.float32
BF16 = jnp.bfloat16
HIGHEST = lax.Precision.HIGHEST

D_MODEL = 1024
N_META = 16
BLOCK = 128
PAD = BLOCK - N_META
EPS = 1e-6
NEG = -1e30
SSD_INNER = 2 * D_MODEL
SSD_HEADDIM = 64
SSD_HEADS = SSD_INNER // SSD_HEADDIM
SSD_GROUPS = 4
SSD_HPG = SSD_HEADS // SSD_GROUPS
SSD_STATE = 128
SSD_CONV = 4
SSD_GW = SSD_HPG * SSD_HEADDIM
SSD_BC = SSD_GROUPS * SSD_STATE
SSD_XBC = SSD_INNER + 2 * SSD_BC
ATT_HEADS = 16
ATT_KV_HEADS = 2
ATT_HEADDIM = 64
ATT_GQ = ATT_HEADS // ATT_KV_HEADS
ATT_Q = ATT_HEADS * ATT_HEADDIM
ATT_KV = ATT_KV_HEADS * ATT_HEADDIM
REL_BUCKETS = 32
REL_MAX_DIST = 128
D_FF = 2816
FFN_CONV = 3
ADAM_LR = 0.001
ADAM_B1 = 0.9
ADAM_B2 = 0.999
ADAM_EPS = 1e-08
ADAM_WD = 0.01
ADAM_STEP = 10

N_DEV = 8
LANES = 128
SUBLANES = 8
BF16_ROWS = 16
DT_W = SSD_GROUPS * LANES
VMEM_LIMIT_BYTES = 56 * 1024 * 1024
MESH = pl.DeviceIdType.MESH

SMALL_ROW_MULT = 16

N_KEYS = 3 * BLOCK
NT_ALL = 3 * N_KEYS * BLOCK
NT_TILE = 8192


def _cparams(*sem):
    return pltpu.CompilerParams(dimension_semantics=sem, vmem_limit_bytes=VMEM_LIMIT_BYTES)


def _row_tile(n, cap):
    best = None
    for t in range(16, min(n, cap) + 1, 16):
        if n % t == 0:
            best = t
    return best or n


def _col_tile(n, cap):
    for t in (1408, 1280, 1024, 768, 640, 512, 384, 256, 128):
        if t <= cap and n % t == 0:
            return t
    return n


def _sigmoid(x):
    return 0.5 * jnp.tanh(0.5 * x) + 0.5


def _silu(x):
    return x * _sigmoid(x)


def _softplus(x):
    return jnp.maximum(x, 0.0) + jnp.log(1.0 + jnp.exp(-jnp.abs(x)))


def _dot_nt(a, b):
    return lax.dot_general(a, b, (((1,), (1,)), ((), ())), preferred_element_type=F32)


def _dot_tn(a, b):
    return lax.dot_general(a, b, (((0,), (0,)), ((), ())), preferred_element_type=F32)


def _dot(a, b):
    return jnp.dot(a, b, preferred_element_type=F32)


def _bf16_terms(x, terms):
    out, rest = [], x
    for _ in range(terms):
        part = rest.astype(BF16)
        out.append(part)
        rest = rest - part.astype(F32)
    return out


def _dot_sel(x, sel, terms=3):
    return sum(_dot(part, sel) for part in _bf16_terms(x, terms))


def _sel_dot(sel, x, terms=3):
    return sum(_dot(sel, part) for part in _bf16_terms(x, terms))


def _sum_all(x):
    return jnp.sum(jnp.sum(x, axis=1, keepdims=True), axis=0, keepdims=True)


MM_ROW_CAPS = (2080, 1664, 832, 416)
MM_COL_CAP = 1408
MM_VMEM_BUDGET = 44 * 1024 * 1024


def _mm_tiles(rows, cols, vmem_bytes):
    col_cands = [t for t in (2048, 1536, 1408, 1280, 1024, 768, 640, 512, 384, 256, 128) if cols % t == 0]
    if cols <= 2 * MM_COL_CAP:
        col_cands.append(cols)
    best = None
    for cap in MM_ROW_CAPS:
        tr = _row_tile(rows, cap)
        for tc in col_cands:
            if vmem_bytes(tr, tc) <= MM_VMEM_BUDGET and (best is None or tr * tc > best[0] * best[1]):
                best = (tr, tc)
    assert best is not None, (rows, cols)
    return best


def _mm(a, b, *, name, ta=False, tb=False, c=None, mask=False, out_dtype=F32):
    if not ta:
        m, k = a.shape
        n = b.shape[0] if tb else b.shape[1]
        tm, tn = _mm_tiles(m, n, lambda t_m, t_n: 2 * (t_m * k * a.dtype.itemsize + k * t_n * b.dtype.itemsize
                                                       + t_m * t_n * (jnp.dtype(out_dtype).itemsize
                                                                      + (0 if c is None else c.dtype.itemsize)))
                           + 4 * t_m * t_n)

        def body(*refs):
            if c is None:
                a_ref, b_ref, o_ref = refs
            else:
                a_ref, b_ref, c_ref, o_ref = refs
            acc = (_dot_nt if tb else _dot)(a_ref[...].astype(BF16), b_ref[...].astype(BF16))
            if mask:
                row = pl.program_id(0) * tm + lax.broadcasted_iota(jnp.int32, (tm, 1), 0)
                acc = jnp.where(row >= PAD, acc, 0.0)
            if c is not None:
                acc = acc + c_ref[...]
            o_ref[...] = acc.astype(out_dtype)

        b_spec = pl.BlockSpec((tn, k), lambda i, j: (j, 0)) if tb else pl.BlockSpec((k, tn), lambda i, j: (0, j))
        in_specs = [pl.BlockSpec((tm, k), lambda i, j: (i, 0)), b_spec]
        args = [a, b]
        if c is not None:
            in_specs.append(pl.BlockSpec((tm, tn), lambda i, j: (i, j)))
            args.append(c)
        return pl.pallas_call(
            body, name=name, grid=(m // tm, n // tn), in_specs=in_specs,
            out_specs=pl.BlockSpec((tm, tn), lambda i, j: (i, j)),
            out_shape=jax.ShapeDtypeStruct((m, n), out_dtype),
            compiler_params=_cparams("parallel", "parallel"))(*args)

    kc, m = a.shape
    n = b.shape[1]
    tm = _col_tile(m, MM_COL_CAP)
    tk, tn = _mm_tiles(kc, n, lambda t_k, t_n: 2 * (t_k * tm * a.dtype.itemsize + t_k * t_n * b.dtype.itemsize
                                                    + tm * t_n * jnp.dtype(out_dtype).itemsize) + 8 * tm * t_n)

    n_k = kc // tk

    def body_t(a_ref, b_ref, o_ref, acc_ref):
        kk = pl.program_id(2)
        bb = b_ref[...]
        if mask:
            row = kk * tk + lax.broadcasted_iota(jnp.int32, (tk, 1), 0)
            bb = jnp.where(row >= PAD, bb, jnp.zeros_like(bb))
        p = _dot_tn(a_ref[...].astype(BF16), bb.astype(BF16))

        @pl.when(kk == 0)
        def _():
            acc_ref[...] = p

        @pl.when(kk > 0)
        def _():
            acc_ref[...] += p

        @pl.when(kk == n_k - 1)
        def _():
            o_ref[...] = acc_ref[...].astype(out_dtype)

    return pl.pallas_call(
        body_t, name=name, grid=(m // tm, n // tn, n_k),
        in_specs=[pl.BlockSpec((tk, tm), lambda i, j, kk: (kk, i)), pl.BlockSpec((tk, tn), lambda i, j, kk: (kk, j))],
        out_specs=pl.BlockSpec((tm, tn), lambda i, j, kk: (i, j)),
        out_shape=jax.ShapeDtypeStruct((m, n), out_dtype),
        scratch_shapes=[pltpu.VMEM((tm, tn), F32)],
        compiler_params=_cparams("parallel", "parallel", "arbitrary"))(a, b)


def _mm_rms_bwd(pairs, x, w, dres, *, name, side=None):
    m, d = x.shape
    tm = _row_tile(m, 416)
    n_pairs = len(pairs)

    def body(*refs):
        a_refs, b_refs = refs[:n_pairs], refs[n_pairs:2 * n_pairs]
        x_ref, w_ref, dres_ref, dx_ref, dw_ref = refs[2 * n_pairs:]
        i = pl.program_id(0)
        dyv = None
        for a_ref, b_ref in zip(a_refs, b_refs):
            term = _dot_nt(a_ref[...].astype(BF16), b_ref[...])
            dyv = term if dyv is None else dyv + term
        xv = x_ref[...]
        r = lax.rsqrt(jnp.mean(xv * xv, axis=-1, keepdims=True) + EPS)
        xh = xv * r
        g = dyv * w_ref[...]
        dx_ref[...] = r * (g - xh * jnp.mean(g * xh, axis=-1, keepdims=True)) + dres_ref[...]
        part = jnp.sum(dyv * xh, axis=0, keepdims=True)

        @pl.when(i == 0)
        def _():
            dw_ref[...] = part

        @pl.when(i > 0)
        def _():
            dw_ref[...] += part

    row = pl.BlockSpec((tm, d), lambda i: (i, 0))
    vec = pl.BlockSpec((1, d), lambda i: (0, 0))
    in_specs = ([pl.BlockSpec((tm, a.shape[1]), lambda i: (i, 0)) for a, _ in pairs]
                + [pl.BlockSpec(b.shape, lambda i: (0, 0), pipeline_mode=pl.Buffered(1)) for _, b in pairs]
                + [row, vec, row])
    return _call_with_side(
        body, side, name=name, grid=(m // tm,), in_specs=in_specs, out_specs=[row, vec],
        out_shape=[jax.ShapeDtypeStruct((m, d), F32), jax.ShapeDtypeStruct((1, d), F32)], scratch_shapes=[],
        args=[a for a, _ in pairs] + [b for _, b in pairs] + [x, w, dres], semantics=("arbitrary",))


def _rms_fwd(h, w, *, name):
    n, d = h.shape
    tm = _row_tile(n, 832)

    def body(h_ref, w_ref, o_ref):
        x = h_ref[...]
        r = lax.rsqrt(jnp.mean(x * x, axis=-1, keepdims=True) + EPS)
        o_ref[...] = (x * r * w_ref[...]).astype(BF16)

    return pl.pallas_call(
        body, name=name, grid=(n // tm,),
        in_specs=[pl.BlockSpec((tm, d), lambda i: (i, 0)), pl.BlockSpec((1, d), lambda i: (0, 0))],
        out_specs=pl.BlockSpec((tm, d), lambda i: (i, 0)),
        out_shape=jax.ShapeDtypeStruct((n, d), BF16),
        compiler_params=_cparams("parallel"))(h, w)


def _final_loss(h, w, target):
    n, d = h.shape
    nb = n // BLOCK

    def body(h_ref, w_ref, t_ref, dh_ref, dhb_ref, loss_ref, dw_ref):
        i = pl.program_id(0)
        xv = h_ref[...]
        r = lax.rsqrt(jnp.mean(xv * xv, axis=-1, keepdims=True) + EPS)
        xh = xv * r
        wv = w_ref[...]
        err = jnp.where(i >= 1, xh * wv - t_ref[...], 0.0)
        dyv = err * (1.0 / d)
        g = dyv * wv
        dh = r * (g - xh * jnp.mean(g * xh, axis=-1, keepdims=True))
        dh_ref[...] = dh
        dhb_ref[...] = dh.astype(BF16)
        lpart = jnp.broadcast_to(0.5 * _sum_all(err * err) * (1.0 / d), (1, LANES))
        wpart = jnp.sum(dyv * xh, axis=0, keepdims=True)

        @pl.when(i == 0)
        def _():
            loss_ref[...] = lpart
            dw_ref[...] = wpart

        @pl.when(i > 0)
        def _():
            loss_ref[...] += lpart
            dw_ref[...] += wpart

    row = pl.BlockSpec((BLOCK, d), lambda i: (i, 0))
    vec = pl.BlockSpec((1, d), lambda i: (0, 0))
    return pl.pallas_call(
        body, name="final_loss", grid=(nb,),
        in_specs=[row, vec, pl.BlockSpec((BLOCK, d), lambda i: (jnp.maximum(i - 1, 0), 0))],
        out_specs=[row, row, pl.BlockSpec((1, LANES), lambda i: (0, 0)), vec],
        out_shape=[jax.ShapeDtypeStruct((n, d), F32), jax.ShapeDtypeStruct((n, d), BF16),
                   jax.ShapeDtypeStruct((1, LANES), F32), jax.ShapeDtypeStruct((1, d), F32)],
        compiler_params=_cparams("arbitrary"))(h, w, target)


def _main_spec(tm, cb, off=0):
    return pl.BlockSpec((tm, cb), lambda j, i: (i, j + off))


def _prev_spec(tm, cb, off=0):
    r8 = tm // SUBLANES
    return pl.BlockSpec((SUBLANES, cb), lambda j, i: (jnp.maximum(i * r8 - 1, 0), j + off))


def _next_spec(tm, cb, n_rows, off=0):
    r8 = tm // SUBLANES
    last = n_rows // SUBLANES - 1
    return pl.BlockSpec((SUBLANES, cb), lambda j, i: (jnp.minimum((i + 1) * r8, last), j + off))


def _with_prev(prev_ref, main_ref, i):
    prev = jnp.where(i > 0, prev_ref[...], 0.0)
    return jnp.concatenate([prev, main_ref[...]], axis=0)


def _with_next(main, nxt, i, n_tiles):
    return jnp.concatenate([main, jnp.where(i < n_tiles - 1, nxt, 0.0)], axis=0)


def _back(xx, s, tm):
    if s == 0:
        return xx[SUBLANES:SUBLANES + tm]
    return pltpu.roll(xx, s, 0)[SUBLANES:SUBLANES + tm]


def _ahead(xx, s, tm):
    if s == 0:
        return xx[:tm]
    return pltpu.roll(xx, xx.shape[0] - s, 0)[:tm]


def _mm_conv_fwd(u, w_in, w, b, *, name):
    n = u.shape[0]
    cdim = w_in.shape[1]
    kw = w.shape[0]
    tm = _row_tile(n, 832)
    cb = _col_tile(cdim, 512)
    nt = n // tm

    def body(u_ref, w_in_ref, w_ref, b_ref, x_ref, o_ref, acc_scr, halo_scr):
        j, i = pl.program_id(0), pl.program_id(1)

        @pl.when((j == 0) & (i == 0))
        def _():
            acc_scr[...] = jnp.zeros_like(acc_scr)
            halo_scr[...] = jnp.zeros_like(halo_scr)

        new = _dot(u_ref[...], w_in_ref[...])
        prev = acc_scr[...]
        xx = jnp.concatenate([jnp.where(i >= 2, halo_scr[...], 0.0), prev], axis=0)
        acc = jnp.broadcast_to(b_ref[...], (tm, cb))
        for k in range(kw):
            acc = acc + w_ref[k:k + 1, :] * _back(xx, kw - 1 - k, tm)
        x_ref[...] = prev.astype(BF16)
        o_ref[...] = acc
        halo_scr[...] = prev[tm - SUBLANES:, :]
        acc_scr[...] = new

    out = pl.BlockSpec((tm, cb), lambda j, i: (jnp.maximum(i - 1, 0), j))
    shp = jax.ShapeDtypeStruct((n, cdim), F32)
    return pl.pallas_call(
        body, name=name, grid=(cdim // cb, nt + 1),
        in_specs=[pl.BlockSpec((tm, u.shape[1]), lambda j, i: (jnp.minimum(i, nt - 1), 0)),
                  pl.BlockSpec((w_in.shape[0], cb), lambda j, i: (0, j)),
                  pl.BlockSpec((kw, cb), lambda j, i: (0, j)), pl.BlockSpec((1, cb), lambda j, i: (0, j))],
        out_specs=[out, out], out_shape=[jax.ShapeDtypeStruct((n, cdim), BF16), shp],
        scratch_shapes=[pltpu.VMEM((tm, cb), F32), pltpu.VMEM((SUBLANES, cb), F32)],
        compiler_params=_cparams("arbitrary", "arbitrary"))(u, w_in, w, b)


def _conv_bwd_core(dpre_ext, x, w_ref, kw, tm):
    dx = None
    dws = []
    for k in range(kw):
        shifted = _ahead(dpre_ext, kw - 1 - k, tm)
        term = w_ref[k:k + 1, :] * shifted
        dx = term if dx is None else dx + term
        dws.append(jnp.sum(shifted * x, axis=0, keepdims=True))
    return dx, dws, jnp.sum(dpre_ext[:tm], axis=0, keepdims=True)


def _acc_rows(i, dw_ref, db_ref, dws, db):
    @pl.when(i == 0)
    def _():
        for k, v in enumerate(dws):
            dw_ref[k:k + 1, :] = v
        db_ref[...] = db

    @pl.when(i > 0)
    def _():
        for k, v in enumerate(dws):
            dw_ref[k:k + 1, :] += v
        db_ref[...] += db


def _conv_bwd(dpre, x, w, *, name, col0=0, into=None):
    n, cdim = x.shape
    kw = w.shape[0]
    tm = _row_tile(n, 832)
    cb = _col_tile(cdim, 512)
    nt = n // tm
    off = col0 // cb
    n_alias = 0 if into is None else 3

    def body(d_ref, dn_ref, x_ref, w_ref, *rest):
        dx_ref, dw_ref, db_ref = rest[n_alias:]
        i = pl.program_id(1)
        dpre_ext = _with_next(d_ref[...], dn_ref[...], i, nt)
        dx, dws, db = _conv_bwd_core(dpre_ext, x_ref[...], w_ref, kw, tm)
        dx_ref[...] = dx.astype(BF16)
        _acc_rows(i, dw_ref, db_ref, dws, db)

    wspec = pl.BlockSpec((kw, cb), lambda j, i: (0, j + off))
    bspec = pl.BlockSpec((1, cb), lambda j, i: (0, j + off))
    return pl.pallas_call(
        body, name=name, grid=(dpre.shape[1] // cb, nt),
        in_specs=[_main_spec(tm, cb), _next_spec(tm, cb, n), _main_spec(tm, cb, off), wspec]
        + [pl.BlockSpec(memory_space=pl.ANY)] * n_alias,
        out_specs=[_main_spec(tm, cb, off), wspec, bspec],
        out_shape=[jax.ShapeDtypeStruct((n, cdim), BF16), jax.ShapeDtypeStruct((kw, cdim), F32),
                   jax.ShapeDtypeStruct((1, cdim), F32)],
        input_output_aliases={4 + k: k for k in range(n_alias)},
        compiler_params=_cparams("parallel", "arbitrary"))(dpre, dpre, x, w, *(into or ()))


def _ffn_in_act_fwd(u, w_in, w, b):
    n = u.shape[0]
    kw = w.shape[0]
    tm = _row_tile(n, 832)
    cb = _col_tile(D_FF, 256)
    nc = D_FF // cb
    nt = n // tm

    def body(u_ref, wu_in_ref, wg_in_ref, wu_ref, wg_ref, bu_ref, bg_ref,
             xu_ref, xg_ref, hu_ref, hg_ref, act_ref, acc_scr, halo_scr):
        j, i = pl.program_id(0), pl.program_id(1)

        @pl.when((j == 0) & (i == 0))
        def _():
            acc_scr[...] = jnp.zeros_like(acc_scr)
            halo_scr[...] = jnp.zeros_like(halo_scr)

        ub = u_ref[...]
        new = [_dot(ub, wu_in_ref[...]), _dot(ub, wg_in_ref[...])]
        hid = []
        for half, (x_ref, w_ref, b_ref) in enumerate(((xu_ref, wu_ref, bu_ref), (xg_ref, wg_ref, bg_ref))):
            prev = acc_scr[half]
            xx = jnp.concatenate([jnp.where(i >= 2, halo_scr[half], 0.0), prev], axis=0)
            acc = jnp.broadcast_to(b_ref[...], (tm, cb))
            for k in range(kw):
                acc = acc + w_ref[k:k + 1, :] * _back(xx, kw - 1 - k, tm)
            x_ref[...] = prev.astype(BF16)
            hid.append(acc)
            halo_scr[half] = prev[tm - SUBLANES:, :]
            acc_scr[half] = new[half]
        hu_ref[...] = hid[0].astype(BF16)
        hg_ref[...] = hid[1].astype(BF16)
        act_ref[...] = (_silu(hid[1]) * hid[0]).astype(BF16)

    def wspec(off):
        return pl.BlockSpec((kw, cb), lambda j, i: (0, j + off))

    def bspec(off):
        return pl.BlockSpec((1, cb), lambda j, i: (0, j + off))

    def in_w(off):
        return pl.BlockSpec((w_in.shape[0], cb), lambda j, i: (0, j + off))

    out = pl.BlockSpec((tm, cb), lambda j, i: (jnp.maximum(i - 1, 0), j))
    bf16_out = jax.ShapeDtypeStruct((n, D_FF), BF16)
    return pl.pallas_call(
        body, name="ffn_in_act_fwd", grid=(nc, nt + 1),
        in_specs=[pl.BlockSpec((tm, u.shape[1]), lambda j, i: (jnp.minimum(i, nt - 1), 0)), in_w(0), in_w(nc),
                  wspec(0), wspec(nc), bspec(0), bspec(nc)],
        out_specs=[out] * 5,
        out_shape=[bf16_out] * 5,
        scratch_shapes=[pltpu.VMEM((2, tm, cb), F32), pltpu.VMEM((2, SUBLANES, cb), F32)],
        compiler_params=_cparams("arbitrary", "arbitrary"))(u, w_in, w_in, w, w, b, b)


def _ffn_out_act_bwd(dh, w_out, hu, hg, x_up, x_gate, w):
    n = x_up.shape[0]
    kw = w.shape[0]
    tm = _row_tile(n, 832)
    cb = _col_tile(D_FF, 256)
    nc = D_FF // cb
    nt = n // tm

    def body(dh_ref, wo_ref, hu_ref, hun_ref, hg_ref, hgn_ref, xu_ref, xg_ref, wu_ref, wg_ref,
             dxu_ref, dxg_ref, dwu_ref, dwg_ref, dbu_ref, dbg_ref, acc_scr, halo_scr):
        j, i = pl.program_id(0), pl.program_id(1)

        @pl.when((j == 0) & (i == 0))
        def _():
            acc_scr[...] = jnp.zeros_like(acc_scr)
            halo_scr[...] = jnp.zeros_like(halo_scr)

        tile = jnp.maximum(nt - 1 - i, 0)
        row = tile * tm + lax.broadcasted_iota(jnp.int32, (tm, 1), 0)
        new = jnp.where(row >= PAD, _dot_nt(dh_ref[...].astype(BF16), wo_ref[...]), 0.0)
        prev = jnp.where(i >= 1, acc_scr[...], 0.0)
        dact_e = jnp.concatenate([prev, jnp.where(i >= 2, halo_scr[...], 0.0)], axis=0)
        last = nt - i >= nt - 1
        up_e = jnp.concatenate([hu_ref[...].astype(F32), jnp.where(last, 0.0, hun_ref[...].astype(F32))], axis=0)
        gate_e = jnp.concatenate([hg_ref[...].astype(F32), jnp.where(last, 0.0, hgn_ref[...].astype(F32))], axis=0)
        halo_scr[...] = prev[:BF16_ROWS, :]
        acc_scr[...] = new
        sg = _sigmoid(gate_e)
        dup_e = dact_e * (gate_e * sg)
        dgate_e = dact_e * up_e * (sg * (1.0 + gate_e * (1.0 - sg)))
        dx, dws, db = _conv_bwd_core(dup_e, xu_ref[...], wu_ref, kw, tm)
        dxu_ref[...] = dx.astype(BF16)
        _acc_rows(i, dwu_ref, dbu_ref, dws, db)
        dx, dws, db = _conv_bwd_core(dgate_e, xg_ref[...], wg_ref, kw, tm)
        dxg_ref[...] = dx.astype(BF16)
        _acc_rows(i, dwg_ref, dbg_ref, dws, db)

    def done_tile(i):
        return jnp.minimum(nt - i, nt - 1)

    halo_blocks = tm // BF16_ROWS
    main = pl.BlockSpec((tm, cb), lambda j, i: (done_tile(i), j))
    nxt = pl.BlockSpec((BF16_ROWS, cb),
                       lambda j, i: (jnp.minimum((done_tile(i) + 1) * halo_blocks, n // BF16_ROWS - 1), j))
    wspec0 = pl.BlockSpec((kw, cb), lambda j, i: (0, j))
    wspec1 = pl.BlockSpec((kw, cb), lambda j, i: (0, j + nc))
    bspec = pl.BlockSpec((1, cb), lambda j, i: (0, j))
    return pl.pallas_call(
        body, name="ffn_out_act_bwd", grid=(nc, nt + 1),
        in_specs=[pl.BlockSpec((tm, dh.shape[1]), lambda j, i: (jnp.maximum(nt - 1 - i, 0), 0)),
                  pl.BlockSpec((cb, w_out.shape[1]), lambda j, i: (j, 0)),
                  main, nxt, main, nxt, main, main, wspec0, wspec1],
        out_specs=[main, main, wspec0, wspec0, bspec, bspec],
        out_shape=[jax.ShapeDtypeStruct((n, D_FF), BF16), jax.ShapeDtypeStruct((n, D_FF), BF16),
                   jax.ShapeDtypeStruct((kw, D_FF), F32), jax.ShapeDtypeStruct((kw, D_FF), F32),
                   jax.ShapeDtypeStruct((1, D_FF), F32), jax.ShapeDtypeStruct((1, D_FF), F32)],
        scratch_shapes=[pltpu.VMEM((tm, cb), F32), pltpu.VMEM((BF16_ROWS, cb), F32)],
        compiler_params=_cparams("arbitrary", "arbitrary"))(dh, w_out, hu, hu, hg, hg, x_up, x_gate, w, w)


def _ssd_prep(pxs_ref, pb_ref, pc_ref, dtr_ref, dtb_ref, alog_ref, c):
    xs = _silu(pxs_ref[...])
    bm = _silu(pb_ref[...])
    cm = _silu(pc_ref[...])
    return (xs, bm, cm) + _ssd_decay(dtr_ref, dtb_ref, alog_ref, c)


def _ssd_decay(dtr_ref, dtb_ref, alog_ref, c):
    row =lax.broadcasted_iota(jnp.int32, (BLOCK, 1), 0) + c * BLOCK
    valid = (row >= PAD).astype(F32)
    dtr = dtr_ref[...] + dtb_ref[...]
    dt = _softplus(dtr) * valid
    a = -jnp.exp(alog_ref[...])
    lam = dt * a
    ri = lax.broadcasted_iota(jnp.int32, (BLOCK, BLOCK), 0)
    ci = lax.broadcasted_iota(jnp.int32, (BLOCK, BLOCK), 1)
    causal = ci <= ri
    cs = _sel_dot(causal.astype(BF16), lam)
    return valid, dtr, dt, a, lam, cs, causal


def _head_cols(r):
    return slice(SSD_HEADDIM * r, SSD_HEADDIM * (r + 1))


def _ssd_specs(nc, rev):
    def cidx(c):
        return nc - 1 - c if rev else c

    xs = pl.BlockSpec((BLOCK, SSD_GW), lambda g, c: (cidx(c), g))
    bspec = pl.BlockSpec((BLOCK, SSD_STATE), lambda g, c: (cidx(c), SSD_INNER // SSD_STATE + g))
    cspec = pl.BlockSpec((BLOCK, SSD_STATE), lambda g, c: (cidx(c), (SSD_INNER + SSD_BC) // SSD_STATE + g))
    lane = pl.BlockSpec((BLOCK, LANES), lambda g, c: (cidx(c), g))
    vec = pl.BlockSpec((1, LANES), lambda g, c: (0, g))
    wide_vec = pl.BlockSpec((1, SSD_GW), lambda g, c: (0, g))
    hsave = pl.BlockSpec((1, 1, SSD_GW, SSD_STATE), lambda g, c: (cidx(c), g, 0, 0))
    return xs, bspec, cspec, lane, vec, wide_vec, hsave


def _head_spread_matrix():
    r = lax.broadcasted_iota(jnp.int32, (LANES, SSD_GW), 0)
    col = lax.broadcasted_iota(jnp.int32, (LANES, SSD_GW), 1)
    return (col // SSD_HEADDIM == r).astype(BF16)


def _const_spec(shape):
    return pl.BlockSpec(shape, lambda g, c: (0,) * len(shape))


def _spread_heads(per_head, e_ref):
    wide = _dot_sel(jnp.concatenate(per_head, axis=0), e_ref[...])
    return [wide[BLOCK * k:BLOCK * (k + 1)] for k in range(len(per_head))]


def _call_with_side(body, side, *, name, grid, in_specs, out_specs, out_shape, scratch_shapes, args,
                    semantics=("parallel", "arbitrary")):
    if side is None:
        outs = pl.pallas_call(body, name=name, grid=grid, in_specs=in_specs, out_specs=out_specs, out_shape=out_shape,
                              scratch_shapes=scratch_shapes, compiler_params=_cparams(*semantics))(*args)
        return outs, []
    n_in, n_out, n_scr, n_side = len(in_specs), len(out_specs), len(scratch_shapes), len(side.arrays)

    def body_with_side(*refs):
        ins, rest = refs[:n_in + n_side], refs[n_in + n_side:]
        outs, scratch = rest[:n_out + n_side], rest[n_out + n_side:]
        side_refs = (ins[n_in:], outs[n_out:], scratch[n_scr:])
        ids = [pl.program_id(k) for k in range(len(grid))]
        inner_first = functools.reduce(jnp.logical_and, [i == 0 for i in ids[1:]], True)

        @pl.when((ids[0] == 0) & inner_first)
        def _():
            side.phases[0](*side_refs)

        body(*ins[:n_in], *outs[:n_out], *scratch[:n_scr])

        @pl.when((ids[0] == grid[0] // 2) & inner_first)
        def _():
            side.phases[1](*side_refs)

        @pl.when(functools.reduce(jnp.logical_and, [i == n - 1 for i, n in zip(ids, grid)]))
        def _():
            side.phases[2](*side_refs)

    any_spec = pl.BlockSpec(memory_space=pl.ANY)
    outs = pl.pallas_call(
        body_with_side, name=name, grid=grid, in_specs=list(in_specs) + [any_spec] * n_side,
        out_specs=list(out_specs) + [any_spec] * n_side, out_shape=list(out_shape) + list(side.out_shape),
        scratch_shapes=list(scratch_shapes) + list(side.scratch_shapes),
        compiler_params=_cparams(*["arbitrary"] * len(grid)))(*args, *side.arrays)
    return outs[:n_out], outs[n_out:]


def _ssd_fwd(pre, dt_raw, z, dtb, alog, dskip_w, norm_w, side=None):
    n = pre.shape[0]
    nc = n // BLOCK
    xs_s, b_s, c_s, lane_s, vec_s, wide_s, hs_s = _ssd_specs(nc, False)

    def body(pxs_ref, pb_ref, pc_ref, dtr_ref, z_ref, dtb_ref, alog_ref, dskw_ref, nw_ref, e_ref,
             y_ref, yn_ref, hs_ref, h_scr):
        c = pl.program_id(1)

        @pl.when(c == 0)
        def _():
            h_scr[...] = jnp.zeros_like(h_scr)

        xs, bm, cm, _, _, dt, _, _, cs, causal = _ssd_prep(pxs_ref, pb_ref, pc_ref, dtr_ref, dtb_ref, alog_ref, c)
        cst = cs.T
        cs_last = cs[BLOCK - 1:BLOCK, :]
        dt_w, ecs_w, dec_w = _spread_heads([dt, jnp.exp(cs), jnp.exp(cs_last - cs)], e_ref)
        xdt = xs * dt_w
        bmb = bm.astype(BF16)
        cmb = cm.astype(BF16)
        cb = _dot_nt(cmb, bmb)
        hg = h_scr[...]
        hs_ref[0, 0] = hg
        y = _dot_nt(cmb, hg.astype(BF16)) * ecs_w + dskw_ref[...] * xs
        first = lax.broadcasted_iota(jnp.int32, (BLOCK, LANES), 1) < SSD_HEADDIM
        diag = []
        for j in range(SSD_HPG // 2):
            xp = xdt[:, LANES * j:LANES * (j + 1)].astype(BF16)
            res = []
            for r in (2 * j, 2 * j + 1):
                lm = jnp.exp(jnp.where(causal, cs[:, r:r + 1] - cst[r:r + 1, :], NEG))
                res.append(_dot((cb * lm).astype(BF16), xp))
            diag.append(jnp.where(first, res[0], res[1]))
        y = y + jnp.concatenate(diag, axis=1)
        st = _dot_tn((xdt * dec_w).astype(BF16), bmb)
        eh = jnp.exp(cs_last)
        for r in range(SSD_HPG):
            rows = _head_cols(r)
            h_scr[rows, :] = hg[rows, :] * eh[:, r:r + 1] + st[rows, :]
        y_ref[...] = y
        gts = y * _silu(z_ref[...])
        rr = lax.rsqrt(jnp.mean(gts * gts, axis=-1, keepdims=True) + EPS)
        yn_ref[...] = (gts * rr * nw_ref[...]).astype(BF16)

    return _call_with_side(
        body, side, name="ssd_fwd", grid=(SSD_GROUPS, nc),
        in_specs=[xs_s, b_s, c_s, lane_s, xs_s, vec_s, vec_s, wide_s, wide_s, _const_spec((LANES, SSD_GW))],
        out_specs=[xs_s, xs_s, hs_s],
        out_shape=[jax.ShapeDtypeStruct((n, SSD_INNER), F32), jax.ShapeDtypeStruct((n, SSD_INNER), BF16),
                   jax.ShapeDtypeStruct((nc, SSD_GROUPS, SSD_GW, SSD_STATE), F32)],
        scratch_shapes=[pltpu.VMEM((SSD_GW, SSD_STATE), F32)],
        args=(pre, pre, pre, dt_raw, z, dtb, alog, dskip_w, norm_w, _head_spread_matrix()))


def _lane_put(acc, col, r):
    lane = lax.broadcasted_iota(jnp.int32, acc.shape, 1)
    return jnp.where(lane == r, col, acc)


def _ssd_bwd(dyn, y, z, pre, dt_raw, hsave, dtb, alog, dskip_w, norm_w, side=None):
    n = pre.shape[0]
    nc = n // BLOCK
    spread = _head_spread_matrix()
    xs_s, b_s, c_s, lane_s, vec_s, wide_s, hs_s = _ssd_specs(nc, True)
    bc_out =pl.BlockSpec((BLOCK, SSD_STATE), lambda g, c: (nc - 1 - c, g))

    def body(dyn_ref, y_ref, z_ref, pxs_ref, pb_ref, pc_ref, dtr_ref, hs_ref, dtb_ref, alog_ref, dskw_ref, nw_ref,
             e_ref, r_ref,
             dz_ref, dxs_ref, dbm_ref, dcm_ref, ddt_ref, dnw_ref, ddtb_ref, dalog_ref, ddsk_ref, g_scr):
        step = pl.program_id(1)
        c = nc - 1 - step

        @pl.when(step == 0)
        def _():
            g_scr[...] = jnp.zeros_like(g_scr)

        pxs, pb, pc = pxs_ref[...], pb_ref[...], pc_ref[...]
        sx, sb, sc = _sigmoid(pxs), _sigmoid(pb), _sigmoid(pc)
        xs, bm, cm = pxs * sx, pb * sb, pc * sc
        valid, dtr, dt, a, lam, cs, causal = _ssd_decay(dtr_ref, dtb_ref, alog_ref, c)
        cst = cs.T
        cs_last = cs[BLOCK - 1:BLOCK, :]
        bmb = bm.astype(BF16)
        cmb = cm.astype(BF16)
        cb = _dot_nt(cmb, bmb)
        hg = hs_ref[0, 0]
        hgb = hg.astype(BF16)
        yoff = _dot_nt(cmb, hgb)
        gn = g_scr[...]
        gnb = gn.astype(BF16)

        zv = z_ref[...]
        yv = y_ref[...]
        sgz = _sigmoid(zv)
        sz = zv * sgz
        gts = yv * sz
        rr = lax.rsqrt(jnp.mean(gts * gts, axis=-1, keepdims=True) + EPS)
        xh = gts * rr
        dynv = dyn_ref[...]
        gg = dynv * nw_ref[...]
        dgts = rr * (gg - xh * jnp.mean(gg * xh, axis=-1, keepdims=True))
        dnw = jnp.sum(dynv * xh, axis=0, keepdims=True)
        dy = dgts * sz
        dz_ref[...] = (dgts * yv * (sgz * (1.0 + zv * (1.0 - sgz)))).astype(BF16)

        ecs = jnp.exp(cs)
        dec = jnp.exp(cs_last - cs)
        eh = jnp.exp(cs_last)
        dt_w, ecs_w, dec_w = _spread_heads([dt, ecs, dec], e_ref)
        red_m = r_ref[...]

        def head_sums(v):
            return _dot_sel(v, red_m, terms=2)

        xdt = xs * dt_w
        q_all = _dot_nt(bmb, gnb)
        w_all = (dy * ecs_w).astype(BF16)
        e_hl = head_sums(q_all * xdt) * dec
        dcs_col = head_sums(dy * yoff) * ecs - e_hl
        gh = jnp.zeros((1, LANES), F32)
        prod = gn * hg
        for r in range(SSD_HPG):
            gh = _lane_put(gh, _sum_all(prod[_head_cols(r), :]), r)
        dcs_last = jnp.sum(e_hl, axis=0, keepdims=True) + eh * gh
        ddsk = jnp.sum(head_sums(dy * xs), axis=0, keepdims=True)
        cbt = _dot_nt(bmb, cmb)
        lane = lax.broadcasted_iota(jnp.int32, (BLOCK, LANES), 1)
        first = lane < SSD_HEADDIM
        causal_t = lax.broadcasted_iota(jnp.int32, (BLOCK, BLOCK), 1) >= lax.broadcasted_iota(
            jnp.int32, (BLOCK, BLOCK), 0)
        sub = lax.broadcasted_iota(jnp.int32, (SUBLANES, BLOCK), 0)
        dcs_row = jnp.zeros((SUBLANES, BLOCK), F32)
        dcb = jnp.zeros((BLOCK, BLOCK), F32)
        dxdt_pairs = []
        for j in range(SSD_HPG // 2):
            tile = slice(LANES * j, LANES * (j + 1))
            dy_p = dy[:, tile]
            dyb = dy_p.astype(BF16)
            xdtb = xdt[:, tile].astype(BF16)
            res = []
            for half, r in enumerate((2 * j, 2 * j + 1)):
                csc, csr = cs[:, r:r + 1], cst[r:r + 1, :]
                lm = jnp.exp(jnp.where(causal, csc - csr, NEG))
                lmt = jnp.exp(jnp.where(causal_t, csr - csc, NEG))
                keep = first if half == 0 else jnp.logical_not(first)
                gm = _dot_nt(jnp.where(keep, dy_p, 0.0).astype(BF16), xdtb) * lm
                dcb = dcb + gm
                mm_ = gm * cb
                dcs_col = dcs_col + jnp.where(lane == r, jnp.sum(mm_, axis=1, keepdims=True), 0.0)
                dcs_row = jnp.where(sub == r, jnp.sum(mm_, axis=0, keepdims=True), dcs_row)
                res.append(_dot((cbt * lmt).astype(BF16), dyb))
            dxdt_pairs.append(jnp.where(first, res[0], res[1]))
        dxdt = jnp.concatenate(dxdt_pairs, axis=1) + q_all * dec_w
        ddt_x = head_sums(dxdt * xs)
        dxs = dxdt * dt_w + dskw_ref[...] * dy
        dcbb = dcb.astype(BF16)
        dcm = _dot(w_all, hgb) + _dot(dcbb, bmb)
        dbm = _dot((xdt * dec_w).astype(BF16), gnb) + _dot_tn(dcbb, cmb)
        dh_off = _dot_tn(w_all, cmb)
        for r in range(SSD_HPG):
            rows = _head_cols(r)
            g_scr[rows, :] = gn[rows, :] * eh[:, r:r + 1] + dh_off[rows, :]

        pad_rows = jnp.zeros((BLOCK - SUBLANES, BLOCK), F32)
        dcs = dcs_col - jnp.concatenate([dcs_row, pad_rows], axis=0).T
        rsel = lax.broadcasted_iota(jnp.int32, (BLOCK, LANES), 0)
        dcs = dcs + jnp.where(rsel == BLOCK - 1, dcs_last, 0.0)
        ri = lax.broadcasted_iota(jnp.int32, (BLOCK, BLOCK), 0)
        ci = lax.broadcasted_iota(jnp.int32, (BLOCK, BLOCK), 1)
        dlam = _sel_dot((ci >= ri).astype(BF16), dcs)
        head = lane < SSD_HPG
        ddt = dlam * a + ddt_x
        ddtr = jnp.where(head, ddt * _sigmoid(dtr) * valid, 0.0)
        ddt_ref[...] = ddtr.astype(BF16)
        dalog = jnp.sum(jnp.where(head, dlam * lam, 0.0), axis=0, keepdims=True)
        ddtb = jnp.sum(ddtr, axis=0, keepdims=True)

        dxs_ref[...] = dxs * (sx * (1.0 + pxs * (1.0 - sx)))
        dbm_ref[...] = dbm * (sb * (1.0 + pb * (1.0 - sb)))
        dcm_ref[...] = dcm * (sc * (1.0 + pc * (1.0 - sc)))

        @pl.when(step == 0)
        def _():
            dnw_ref[...] = dnw
            ddtb_ref[...] = ddtb
            dalog_ref[...] = dalog
            ddsk_ref[...] = ddsk

        @pl.when(step > 0)
        def _():
            dnw_ref[...] += dnw
            ddtb_ref[...] += ddtb
            dalog_ref[...] += dalog
            ddsk_ref[...] += ddsk

    return _call_with_side(
        body, side, name="ssd_bwd", grid=(SSD_GROUPS, nc),
        in_specs=[xs_s, xs_s, xs_s, xs_s, b_s, c_s, lane_s, hs_s, vec_s, vec_s, wide_s, wide_s,
                  _const_spec((LANES, SSD_GW)), _const_spec((SSD_GW, LANES))],
        out_specs=[xs_s, xs_s, bc_out, bc_out, lane_s, wide_s, vec_s, vec_s, vec_s],
        out_shape=[jax.ShapeDtypeStruct((n, SSD_INNER), BF16), jax.ShapeDtypeStruct((n, SSD_INNER), F32),
                   jax.ShapeDtypeStruct((n, SSD_BC), F32), jax.ShapeDtypeStruct((n, SSD_BC), F32),
                   jax.ShapeDtypeStruct((n, DT_W), BF16), jax.ShapeDtypeStruct((1, SSD_INNER), F32),
                   jax.ShapeDtypeStruct((1, DT_W), F32), jax.ShapeDtypeStruct((1, DT_W), F32),
                   jax.ShapeDtypeStruct((1, DT_W), F32)],
        scratch_shapes=[pltpu.VMEM((SSD_GW, SSD_STATE), F32)],
        args=(dyn, y, z, pre, pre, pre, dt_raw, hsave, dtb, alog, dskip_w, norm_w, spread, spread.T))


def _bucket_table():
    def bucket(dist):
        d = np.maximum(dist, 0)
        half = REL_BUCKETS // 2
        big = half + (np.log(np.maximum(d, half).astype(np.float32) / np.float32(half))
                      / np.float32(math.log(REL_MAX_DIST / half)) * np.float32(REL_BUCKETS - half)).astype(np.int32)
        return np.where(d < half, d, np.minimum(big, REL_BUCKETS - 1)).astype(np.int32)

    l = np.arange(BLOCK)[None, :]
    band = bucket(l + BLOCK - np.arange(2 * BLOCK)[:, None])
    j = np.arange(BLOCK)[:, None]
    tables = [np.concatenate([bucket(v * BLOCK + l - j), band], axis=0) for v in range(3)]
    return np.concatenate([t.reshape(-1) for t in tables])


def _onehot_t():
    buckets = jnp.asarray(_bucket_table())
    return (buckets[None, :] == jnp.arange(REL_BUCKETS, dtype=jnp.int32)[:, None]).astype(F32)


def _bias_tables(rel_t, onehot_t, side=None):
    def body(r_ref, oh_ref, o_ref):
        o_ref[...] = jnp.dot(r_ref[...], oh_ref[...], precision=HIGHEST, preferred_element_type=F32)

    outs, carried = _call_with_side(
        body, side, name="bias_tables", grid=(NT_ALL // NT_TILE,),
        in_specs=[pl.BlockSpec((ATT_HEADS, REL_BUCKETS), lambda i: (0, 0)),
                  pl.BlockSpec((REL_BUCKETS, NT_TILE), lambda i: (0, i))],
        out_specs=[pl.BlockSpec((ATT_HEADS, NT_TILE), lambda i: (0, i))],
        out_shape=[jax.ShapeDtypeStruct((ATT_HEADS, NT_ALL), F32)], scratch_shapes=[], args=(rel_t, onehot_t),
        semantics=("parallel",))
    return outs[0], carried


def _bias_grad(dtab, onehot_t):
    def body(d_ref, oh_ref, o_ref):
        i = pl.program_id(0)
        p = lax.dot_general(d_ref[...], oh_ref[...], (((1,), (1,)), ((), ())), precision=HIGHEST,
                            preferred_element_type=F32)

        @pl.when(i == 0)
        def _():
            o_ref[...] = p

        @pl.when(i > 0)
        def _():
            o_ref[...] += p

    return pl.pallas_call(
        body, name="bias_grad", grid=(NT_ALL // NT_TILE,),
        in_specs=[pl.BlockSpec((ATT_HEADS, NT_TILE), lambda i: (0, i)),
                  pl.BlockSpec((REL_BUCKETS, NT_TILE), lambda i: (0, i))],
        out_specs=pl.BlockSpec((ATT_HEADS, REL_BUCKETS), lambda i: (0, 0)),
        out_shape=jax.ShapeDtypeStruct((ATT_HEADS, REL_BUCKETS), F32),
        compiler_params=_cparams("arbitrary"))(dtab, onehot_t)


def _att_mask_t(n, copies):
    far = 4 * BLOCK
    kk = lax.broadcasted_iota(jnp.int32, (N_KEYS, copies * BLOCK), 0)
    li = lax.broadcasted_iota(jnp.int32, (N_KEYS, copies * BLOCK), 1) & (BLOCK - 1)
    meta_ok = (kk >= PAD) & (kk < BLOCK) & (li + jnp.where(n >= 1, far, 0) >= kk)
    prev_ok = (kk >= BLOCK) & (kk < 2 * BLOCK) & (kk - BLOCK > li + jnp.where(n >= 2, 0, far))
    cur_ok = (kk >= 2 * BLOCK) & (kk - 2 * BLOCK <= li - jnp.where(n >= 1, 0, far))
    return meta_ok | prev_ok | cur_ok


def _att_kv(meta_ref, prev_ref, cur_ref):
    kv = jnp.concatenate([meta_ref[...], prev_ref[...], cur_ref[...]], axis=0)
    first = lax.broadcasted_iota(jnp.int32, (N_KEYS, LANES), 1) < ATT_HEADDIM
    out = []
    for pair in (kv[:, :LANES], kv[:, LANES:]):
        swapped = pltpu.roll(pair, ATT_HEADDIM, 1)
        out.append([jnp.where(first, pair, swapped).astype(BF16), jnp.where(first, swapped, pair).astype(BF16)])
    return out[0], out[1]


def _split_heads(x_pair, first):
    return jnp.concatenate([jnp.where(first, x_pair, 0.0), jnp.where(first, 0.0, x_pair)], axis=0).astype(BF16)


def _att_probs_t(qm2, k_dup, t_ref, j, mask2, sink_ref):
    scale = ATT_HEADDIM ** -0.5
    bias2 = jnp.concatenate([t_ref[0, 2 * j], t_ref[0, 2 * j + 1]], axis=1)
    second = lax.broadcasted_iota(jnp.int32, (1, 2 * BLOCK), 1) >= BLOCK
    sink2 = jnp.where(second, sink_ref[0:1, 2 * j + 1:2 * j + 2], sink_ref[0:1, 2 * j:2 * j + 1])
    s_t = jnp.where(mask2, _dot_nt(k_dup, qm2) * scale + bias2, NEG)
    mx = jnp.maximum(jnp.max(s_t, axis=0, keepdims=True), sink2)
    p_t = jnp.exp(s_t - mx)
    p_s = jnp.exp(sink2 - mx)
    inv = 1.0 / (jnp.sum(p_t, axis=0, keepdims=True) + p_s)
    return p_t * inv, p_s * inv


def _att_specs(nb, rev):
    def nidx(i):
        return nb - 1 - i if rev else i

    kvb = ATT_Q // (2 * ATT_KV)
    q_s = pl.BlockSpec((BLOCK, ATT_Q), lambda i: (nidx(i), 0))
    cur = pl.BlockSpec((BLOCK, 2 * ATT_KV), lambda i: (nidx(i), kvb))
    prev = pl.BlockSpec((BLOCK, 2 * ATT_KV), lambda i: (jnp.maximum(nidx(i) - 1, 0), kvb))
    meta = pl.BlockSpec((BLOCK, 2 * ATT_KV), lambda i: (0, kvb))
    table = pl.BlockSpec((1, ATT_HEADS, N_KEYS, BLOCK), lambda i: (jnp.minimum(nidx(i), 2), 0, 0, 0))
    sink = pl.BlockSpec((1, LANES), lambda i: (0, 0))
    return q_s, cur, prev, meta, table, sink


def _attn_fwd(qkv, tables, sinks):
    n = qkv.shape[0]
    nb = n // BLOCK
    q_s, cur_s, prev_s, meta_s, t_s, sink_s = _att_specs(nb, False)

    def body(q_ref, cur_ref, prev_ref, meta_ref, t_ref, sink_ref, o_ref):
        blk = pl.program_id(0)
        mask_t = _att_mask_t(blk, 1)
        k_dup, v_dup = _att_kv(meta_ref, prev_ref, cur_ref)
        v_dup_t = [v.T for v in v_dup]
        first = lax.broadcasted_iota(jnp.int32, (BLOCK, LANES), 1) < ATT_HEADDIM
        top = lax.broadcasted_iota(jnp.int32, (LANES, BLOCK), 0) < ATT_HEADDIM
        scale = ATT_HEADDIM ** -0.5
        for j in range(ATT_HEADS // 2):
            kh = 2 * j // ATT_GQ
            tile = slice(LANES * j, LANES * (j + 1))
            q_p = q_ref[:, tile]
            res = []
            for half, h in enumerate((2 * j, 2 * j + 1)):
                qm = jnp.where(first if half == 0 else jnp.logical_not(first), q_p, 0.0).astype(BF16)
                sink = sink_ref[0:1, h:h + 1]
                s_t = jnp.where(mask_t, _dot_nt(k_dup[kh], qm) * scale + t_ref[0, h], NEG)
                mx = jnp.maximum(jnp.max(s_t, axis=0, keepdims=True), sink)
                p_t = jnp.exp(s_t - mx)
                inv = 1.0 / (jnp.sum(p_t, axis=0, keepdims=True) + jnp.exp(sink - mx))
                res.append(_dot(v_dup_t[kh], (p_t * inv).astype(BF16)))
            o_ref[:, tile] = jnp.where(top, res[0], res[1]).T.astype(BF16)

    return pl.pallas_call(
        body, name="attn_fwd", grid=(nb,),
        in_specs=[q_s, cur_s, prev_s, meta_s, t_s, sink_s],
        out_specs=q_s,
        out_shape=jax.ShapeDtypeStruct((n, ATT_Q), BF16),
        compiler_params=_cparams("parallel"))(qkv, qkv, qkv, qkv, tables, sinks)


def _attn_bwd(datt, qkv, tables, sinks):
    n = qkv.shape[0]
    nb = n // BLOCK
    q_s, cur_s, prev_s, meta_s, t_s, sink_s = _att_specs(nb, True)
    dqkv_s = pl.BlockSpec((BLOCK, ATT_Q + 2 * ATT_KV), lambda i: (nb - 1 - i, 0))
    scale = ATT_HEADDIM ** -0.5

    def body(do_ref, q_ref, cur_ref, prev_ref, meta_ref, t_ref, sink_ref,
             dqkv_ref, dt_ref, dsink_ref, carry_scr, meta_scr):
        step = pl.program_id(0)
        blk = nb - 1 - step
        mask2 = _att_mask_t(blk, 2)
        k_dup, v_dup = _att_kv(meta_ref, prev_ref, cur_ref)
        k_dup_t = [k.T for k in k_dup]

        @pl.when(step == 0)
        def _():
            carry_scr[...] = jnp.zeros_like(carry_scr)
            meta_scr[...] = jnp.zeros_like(meta_scr)
            dsink_ref[...] = jnp.zeros_like(dsink_ref)

        @pl.when((step == 0) | (blk <= 1))
        def _():
            dt_ref[...] = jnp.zeros_like(dt_ref)

        first = lax.broadcasted_iota(jnp.int32, (BLOCK, LANES), 1) < ATT_HEADDIM
        top = lax.broadcasted_iota(jnp.int32, (LANES, BLOCK), 0) < ATT_HEADDIM
        first_k = lax.broadcasted_iota(jnp.int32, (N_KEYS, LANES), 1) < ATT_HEADDIM
        dsink = jnp.zeros((1, LANES), F32)
        dk_acc = [None] * ATT_KV_HEADS
        dv_acc = [None] * ATT_KV_HEADS
        for j in range(ATT_HEADS // 2):
            kh = 2 * j // ATT_GQ
            tile = slice(LANES * j, LANES * (j + 1))
            qm2 = _split_heads(q_ref[:, tile], first)
            dom2 = _split_heads(do_ref[:, tile], first)
            p_t, p_s = _att_probs_t(qm2, k_dup[kh], t_ref, j, mask2, sink_ref)
            dp_t = _dot_nt(v_dup[kh], dom2)
            delta = jnp.sum(p_t * dp_t, axis=0, keepdims=True)
            ds_t = p_t * (dp_t - delta)
            sink_terms = p_s * delta
            for half in range(2):
                cols = slice(BLOCK * half, BLOCK * (half + 1))
                dsink = _lane_put(dsink, -jnp.sum(sink_terms[:, cols], axis=1, keepdims=True), 2 * j + half)
                dt_ref[0, 2 * j + half] += ds_t[:, cols]
            ds_tb = ds_t.astype(BF16)
            dq_t = _dot(k_dup_t[kh], ds_tb)
            dqkv_ref[:, tile] = (jnp.where(top, dq_t[:, :BLOCK], dq_t[:, BLOCK:]).T * scale).astype(BF16)
            dk_part, dv_part = _dot(ds_tb, qm2), _dot(p_t.astype(BF16), dom2)
            dk_acc[kh] = dk_part if dk_acc[kh] is None else dk_acc[kh] + dk_part
            dv_acc[kh] = dv_part if dv_acc[kh] is None else dv_acc[kh] + dv_part
        dsink_ref[...] += dsink
        folded = [a + pltpu.roll(a, ATT_HEADDIM, 1) for a in dk_acc + dv_acc]
        dkv = jnp.concatenate([jnp.where(first_k, folded[0], folded[1]) * scale,
                               jnp.where(first_k, folded[2], folded[3])], axis=1)
        meta_scr[...] += dkv[:BLOCK, :]
        own = dkv[2 * BLOCK:, :] + carry_scr[...]
        carry_scr[...] = dkv[BLOCK:2 * BLOCK, :]

        @pl.when(blk > 0)
        def _():
            dqkv_ref[:, ATT_Q:] = own.astype(BF16)

        @pl.when(blk == 0)
        def _():
            dqkv_ref[:, ATT_Q:] = (own + meta_scr[...]).astype(BF16)

    return pl.pallas_call(
        body, name="attn_bwd", grid=(nb,),
        in_specs=[q_s, q_s, cur_s, prev_s, meta_s, t_s, sink_s],
        out_specs=[dqkv_s, t_s, sink_s],
        out_shape=[jax.ShapeDtypeStruct((n, ATT_Q + 2 * ATT_KV), BF16),
                   jax.ShapeDtypeStruct((3, ATT_HEADS, N_KEYS, BLOCK), F32),
                   jax.ShapeDtypeStruct((1, LANES), F32)],
        scratch_shapes=[pltpu.VMEM((BLOCK, 2 * ATT_KV), F32), pltpu.VMEM((BLOCK, 2 * ATT_KV), F32)],
        compiler_params=_cparams("arbitrary"))(datt, qkv, qkv, qkv, qkv, tables, sinks)


def _merge_out_fwd(gates, y_ssd, y_att, gate_b, w_out, h):
    n = gates.shape[0]
    tm = _row_tile(n, 416)

    def body(gs_ref, ga_ref, ys_ref, ya_ref, gb_ref, w_ref, h_ref, m_ref, o_ref):
        merged = (_sigmoid(gs_ref[...] + gb_ref[0:1, :]) * ys_ref[...]
                  + _sigmoid(ga_ref[...] + gb_ref[1:2, :]) * ya_ref[...]).astype(BF16)
        m_ref[...] = merged
        row = pl.program_id(0) * tm + lax.broadcasted_iota(jnp.int32, (tm, 1), 0)
        o_ref[...] = jnp.where(row >= PAD, _dot(merged, w_ref[...]), 0.0) + h_ref[...]

    row = pl.BlockSpec((tm, D_MODEL), lambda i: (i, 0))
    return pl.pallas_call(
        body, name="merge_out_fwd", grid=(n // tm,),
        in_specs=[row, pl.BlockSpec((tm, D_MODEL), lambda i: (i, 1)), row, row,
                  pl.BlockSpec((2, D_MODEL), lambda i: (0, 0)), pl.BlockSpec((D_MODEL, D_MODEL), lambda i: (0, 0)), row],
        out_specs=[row, row],
        out_shape=[jax.ShapeDtypeStruct((n, D_MODEL), BF16), jax.ShapeDtypeStruct((n, D_MODEL), F32)],
        compiler_params=_cparams("parallel"))(gates, gates, y_ssd, y_att, gate_b, w_out, h)


def _merge_out_bwd(dh, w_out, gates, y_ssd, y_att, gate_b):
    n = gates.shape[0]
    tm = _row_tile(n, 416)

    def body(dh_ref, w_ref, gs_ref, ga_ref, ys_ref, ya_ref, gb_ref, dys_ref, dya_ref, dg_ref, dgb_ref):
        i = pl.program_id(0)
        row = i * tm + lax.broadcasted_iota(jnp.int32, (tm, 1), 0)
        dmv = jnp.where(row >= PAD, _dot_nt(dh_ref[...].astype(BF16), w_ref[...]), 0.0)
        ss =_sigmoid(gs_ref[...] + gb_ref[0:1, :])
        sa = _sigmoid(ga_ref[...] + gb_ref[1:2, :])
        dys_ref[...] = (dmv * ss).astype(BF16)
        dya_ref[...] = (dmv * sa).astype(BF16)
        dgs = dmv * ys_ref[...] * ss * (1.0 - ss)
        dga = dmv * ya_ref[...] * sa * (1.0 - sa)
        dg_ref[:, :D_MODEL] = dgs.astype(BF16)
        dg_ref[:, D_MODEL:] = dga.astype(BF16)
        part = jnp.concatenate([jnp.sum(dgs, axis=0, keepdims=True), jnp.sum(dga, axis=0, keepdims=True)], axis=0)

        @pl.when(i == 0)
        def _():
            dgb_ref[...] = part

        @pl.when(i > 0)
        def _():
            dgb_ref[...] += part

    row = pl.BlockSpec((tm, D_MODEL), lambda i: (i, 0))
    gb = pl.BlockSpec((2, D_MODEL), lambda i: (0, 0))
    return pl.pallas_call(
        body, name="merge_out_bwd", grid=(n // tm,),
        in_specs=[row, pl.BlockSpec((D_MODEL, D_MODEL), lambda i: (0, 0)), row,
                  pl.BlockSpec((tm, D_MODEL), lambda i: (i, 1)), row, row, gb],
        out_specs=[row, row, pl.BlockSpec((tm, 2 * D_MODEL), lambda i: (i, 0)), gb],
        out_shape=[jax.ShapeDtypeStruct((n, D_MODEL), BF16), jax.ShapeDtypeStruct((n, D_MODEL), BF16),
                   jax.ShapeDtypeStruct((n, 2 * D_MODEL), BF16), jax.ShapeDtypeStruct((2, D_MODEL), F32)],
        compiler_params=_cparams("arbitrary"))(dh, w_out, gates, gates, y_ssd, y_att, gate_b)


def _col_move(srcs, outs, pieces, *, name):
    rows = srcs[0].shape[-2]
    tr = _row_tile(rows, 128)
    n_src = len(srcs)
    covered = [sum(p[6] for p in pieces if p[0] == o) for o in range(len(outs))]
    total = [int(np.prod(shp)) // rows for shp, _ in outs]

    def body(*refs):
        in_refs, out_refs = refs[:n_src], refs[n_src:]
        for o, ref in enumerate(out_refs):
            if covered[o] != total[o]:
                ref[...] = jnp.zeros_like(ref)
        for o, ol, oc, s, sl, sc, width in pieces:
            val = in_refs[s][:, sc:sc + width] if sl is None else in_refs[s][sl, :, sc:sc + width]
            val = val.astype(outs[o][1])
            if ol is None:
                out_refs[o][:, oc:oc + width] = val
            else:
                out_refs[o][ol, :, oc:oc + width] = val

    def spec(shape):
        if len(shape) == 2:
            return pl.BlockSpec((tr, shape[1]), lambda i: (i, 0))
        return pl.BlockSpec((shape[0], tr, shape[2]), lambda i: (0, i, 0))

    return pl.pallas_call(
        body, name=name, grid=(rows // tr,),
        in_specs=[spec(a.shape) for a in srcs], out_specs=[spec(shp) for shp, _ in outs],
        out_shape=[jax.ShapeDtypeStruct(shp, dt) for shp, dt in outs],
        compiler_params=_cparams("parallel"))(*srcs)


def _shard_pieces(seg_ranges, shard_w):
    out = []
    for seg, runs in enumerate(seg_ranges):
        for g0, width, s0 in runs:
            done = 0
            while done < width:
                dev, col = divmod(g0 + done, shard_w)
                take = min(width - done, shard_w - col)
                out.append((seg, s0 + done, dev, col, take))
                done += take
    return out


_CHIP_RELATIONS = [(1, 0, 0), (0, 1, 0), (1, 1, 0)]
N_CHIPS = 4


class _CommPlan:
    def __init__(self, arrays, out_shape, scratch_shapes, phases):
        self.arrays, self.out_shape, self.scratch_shapes, self.phases = arrays, out_shape, scratch_shapes, phases


def _gather_plan(arrays):
    n_arr = len(arrays)
    n_chips = len(_CHIP_RELATIONS)
    n_pair = 1 + 2 * n_chips

    def where():
        x, y, c = lax.axis_index("x"), lax.axis_index("y"), lax.axis_index("c")
        return x, y, c, (x, y, 1 - c), [(x ^ dx, y ^ dy) for dx, dy, _ in _CHIP_RELATIONS]

    def copy(outs, sems, a, k, block, to, src=None):
        slot = outs[a].at[2 * block[0] + block[1], block[2]]
        return pltpu.make_async_remote_copy(
            src_ref=slot if src is None else src, dst_ref=slot, send_sem=sems[0].at[a * n_pair + k],
            recv_sem=sems[1].at[a * n_pair + k], device_id=to, device_id_type=MESH)

    def mine(ins, outs, sems, a, x, y, c):
        return pltpu.make_async_copy(ins[a], outs[a].at[2 * x + y, c], sems[2].at[a])

    def first_copies(ins, outs, sems, a, x, y, c, sibling, chips):
        return ([copy(outs, sems, a, 0, (x, y, c), sibling, src=ins[a])]
                + [copy(outs, sems, a, 1 + j, (x, y, c), (*chip, c), src=ins[a]) for j, chip in enumerate(chips)])

    def start(ins, outs, sems):
        x, y, c, sibling, chips = where()
        for a in range(n_arr):
            mine(ins, outs, sems, a, x, y, c).start()
            for cp in first_copies(ins, outs, sems, a, x, y, c, sibling, chips):
                cp.start()

    def pass_on(ins, outs, sems):
        x, y, c, sibling, chips = where()
        for j, chip in enumerate(chips):
            for a in range(n_arr):
                copy(outs, sems, a, 1 + j, (*chip, c), (x, y, c)).wait_recv()
                copy(outs, sems, a, 1 + n_chips + j, (*chip, c), sibling).start()

    def finish(ins, outs, sems):
        x, y, c, sibling, chips = where()
        for a in range(n_arr):
            copy(outs, sems, a, 0, (x, y, 1 - c), (x, y, c)).wait_recv()
            for j, chip in enumerate(chips):
                copy(outs, sems, a, 1 + n_chips + j, (*chip, 1 - c), (x, y, c)).wait_recv()
        for a in range(n_arr):
            for cp in first_copies(ins, outs, sems, a, x, y, c, sibling, chips):
                cp.wait_send()
            for j, chip in enumerate(chips):
                copy(outs, sems, a, 1 + n_chips + j, (*chip, c), sibling).wait_send()
            mine(ins, outs, sems, a, x, y, c).wait()

    return _CommPlan(
        arrays, [jax.ShapeDtypeStruct((N_CHIPS, 2) + a.shape, a.dtype) for a in arrays],
        [pltpu.SemaphoreType.DMA((n_arr * n_pair,)), pltpu.SemaphoreType.DMA((n_arr * n_pair,)),
         pltpu.SemaphoreType.DMA((n_arr,))],
        (start, pass_on, finish))


_ALL_RELATIONS = [(dx, dy, dc) for dx in (0, 1) for dy in (0, 1) for dc in (0, 1)][1:]


def _all_to_all_plan(arrays, scatter=None):
    n_arr = len(arrays)
    n_rel = len(_ALL_RELATIONS)
    scatter = scatter or [True] * n_arr

    def block(ins, a, p):
        return ins[a].at[p] if scatter[a] else ins[a]

    def local_copies(ins, outs, sems):
        me = 4 * lax.axis_index("x") + 2 * lax.axis_index("y") + lax.axis_index("c")
        return [pltpu.make_async_copy(block(ins, a, me), outs[a].at[me], sems[2].at[a]) for a in range(n_arr)]

    def remote_copies(ins, outs, sems, arrivals):
        x, y, c = lax.axis_index("x"), lax.axis_index("y"), lax.axis_index("c")
        me = 4 * x + 2 * y + c
        out = []
        for k, (dx, dy, dc) in enumerate(_ALL_RELATIONS):
            px, py, pc = x ^ dx, y ^ dy, c ^ dc
            peer = 4 * px + 2 * py + pc
            for a in range(n_arr):
                out.append(pltpu.make_async_remote_copy(
                    src_ref=block(ins, a, peer), dst_ref=outs[a].at[peer if arrivals else me],
                    send_sem=sems[0].at[a * n_rel + k], recv_sem=sems[1].at[a * n_rel + k],
                    device_id=(x, y, c) if arrivals else (px, py, pc), device_id_type=MESH))
        return out

    def start(ins, outs, sems):
        for cp in local_copies(ins, outs, sems) + remote_copies(ins, outs, sems, False):
            cp.start()

    def pass_on(ins, outs, sems):
        pass

    def finish(ins, outs, sems):
        for send in remote_copies(ins, outs, sems, False):
            send.wait_send()
        for arrival in remote_copies(ins, outs, sems, True):
            arrival.wait_recv()
        for cp in local_copies(ins, outs, sems):
            cp.wait()

    return _CommPlan(
        arrays, [jax.ShapeDtypeStruct(a.shape if s else (N_DEV,) + a.shape, a.dtype) for a, s in zip(arrays, scatter)],
        [pltpu.SemaphoreType.DMA((n_arr * n_rel,)), pltpu.SemaphoreType.DMA((n_arr * n_rel,)),
         pltpu.SemaphoreType.DMA((n_arr,))],
        (start, pass_on, finish))


def _adamw(w, gslots, m, v, *, name, side=None):
    rows, cols = w.shape
    n_slots = gslots.shape[0]
    tr = _row_tile(rows, 128) if rows % 16 == 0 else rows

    def body(w_ref, g_ref, m_ref, v_ref, go_ref, d_ref, mo_ref, vo_ref):
        g = g_ref[0].astype(F32)
        for s in range(1, n_slots):
            g = g + g_ref[s].astype(F32)
        mn = ADAM_B1 * m_ref[...] + (1.0 - ADAM_B1) * g
        vn = ADAM_B2 * v_ref[...] + (1.0 - ADAM_B2) * (g * g)
        go_ref[...] = g
        mo_ref[...] = mn
        vo_ref[...] = vn
        m_hat = mn / (1.0 - ADAM_B1 ** ADAM_STEP)
        v_hat = vn / (1.0 - ADAM_B2 ** ADAM_STEP)
        d_ref[...] = -ADAM_LR * (m_hat / (jnp.sqrt(v_hat) + ADAM_EPS) + ADAM_WD * w_ref[...])

    blk = pl.BlockSpec((tr, cols), lambda i: (i, 0))
    shp = jax.ShapeDtypeStruct((rows, cols), F32)
    outs, carried = _call_with_side(
        body, side, name=name, grid=(rows // tr,),
        in_specs=[blk, pl.BlockSpec((n_slots, tr, cols), lambda i: (0, i, 0)), blk, blk],
        out_specs=[blk] * 4, out_shape=[shp] * 4, scratch_shapes=[], args=(w, gslots, m, v), semantics=("parallel",))
    return outs if side is None else (outs, carried)


_BIG = ("w_in", "w_ssd_branch", "w_attn_branch", "w_out", "w_ffn_in", "w_ffn_out")
_SMALL_SHARDED = ("meta_tokens", "ssd_conv_w", "gate_b", "ffn_conv_w")
_SMALL_REPLICATED = ("norm_mix_w", "ssd_conv_b", "ssd_dt_bias", "ssd_a_log", "ssd_d", "ssd_norm_w", "attn_sinks",
                     "rel_bias", "norm_ffn_w", "ffn_conv_b", "norm_final_w")
_WEIGHTS = ("meta_tokens", "norm_mix_w", "w_in", "ssd_conv_w", "ssd_conv_b", "ssd_dt_bias", "ssd_a_log", "ssd_d",
            "ssd_norm_w", "w_ssd_branch", "w_attn_branch", "attn_sinks", "rel_bias", "gate_b", "w_out", "norm_ffn_w",
            "w_ffn_in", "ffn_conv_w", "ffn_conv_b", "w_ffn_out", "norm_final_w")
_ROW_SHARDED = ("w_ssd_branch", "w_attn_branch", "w_out", "w_ffn_out")
_COL_SHARDED = ("w_in", "w_ffn_in", "meta_tokens", "ssd_conv_w", "gate_b", "ffn_conv_w")
_IN_SEGS = (("z", SSD_INNER), ("xbc", SSD_XBC), ("dt", SSD_HEADS), ("qkv", ATT_Q + 2 * ATT_KV), ("g", 2 * D_MODEL))


def _pack_rows(flat_parts, width, row_mult):
    flat = jnp.concatenate([p.reshape(-1) for p in flat_parts])
    pad = (-flat.shape[0]) % (width * row_mult)
    if pad:
        flat = jnp.concatenate([flat, jnp.zeros((pad,), flat.dtype)])
    return flat.reshape(-1, width)


def _unpack(flat, shapes):
    out, off = [], 0
    for shp in shapes:
        size = int(np.prod(shp))
        out.append(flat[off:off + size].reshape(shp))
        off += size
    return out


def _gather_full(stack, name, shard_shape):
    if name in _COL_SHARDED:
        return jnp.transpose(stack, (1, 0, 2)).reshape(shard_shape[0], N_DEV * shard_shape[1])
    return stack.reshape(N_DEV * shard_shape[0], shard_shape[1])


_IN_SEG_W = {"z": SSD_INNER, "xbc": SSD_XBC, "dt": DT_W, "qkv": ATT_Q + 2 * ATT_KV, "g": 2 * D_MODEL}
_IN_SHARD_W = (SSD_INNER + SSD_XBC + SSD_HEADS + ATT_Q + 2 * ATT_KV + 2 * D_MODEL) // N_DEV
_FFN_SHARD_W = 2 * D_FF // N_DEV


def _in_seg_runs():
    runs, off = [], 0
    for nm, width in _IN_SEGS:
        if nm == "dt":
            runs.append([(off + SSD_HPG * g, SSD_HPG, LANES * g) for g in range(SSD_GROUPS)])
        else:
            runs.append([(off, width, 0)])
        off += width
    return runs


def _w_in_to_segments(stack):
    pieces = [(seg, None, scol, 0, dev, col, w) for seg, scol, dev, col, w in _shard_pieces(_in_seg_runs(), _IN_SHARD_W)]
    outs = [((D_MODEL, _IN_SEG_W[nm]), stack.dtype) for nm, _ in _IN_SEGS]
    return dict(zip([nm for nm, _ in _IN_SEGS], _col_move([stack], outs, pieces, name="w_in_segments")))


def _segments_to_w_in_shards(seg_grads):
    pieces = [(0, dev, col, seg, None, scol, w) for seg, scol, dev, col, w in _shard_pieces(_in_seg_runs(), _IN_SHARD_W)]
    return _col_move(seg_grads, [((N_DEV, D_MODEL, _IN_SHARD_W), seg_grads[0].dtype)], pieces, name="g_w_in_shards")[0]


def _ffn_in_from_shards(stack):
    pieces = [(0, None, scol, 0, dev, col, w)
              for _, scol, dev, col, w in _shard_pieces([[(0, 2 * D_FF, 0)]], _FFN_SHARD_W)]
    return _col_move([stack], [((D_MODEL, 2 * D_FF), stack.dtype)], pieces, name="w_ffn_in_full")[0]


def _ffn_in_to_shards(g_up, g_gate):
    pieces = [(0, dev, col, seg, None, scol, w)
              for seg, scol, dev, col, w in _shard_pieces([[(0, D_FF, 0)], [(D_FF, D_FF, 0)]], _FFN_SHARD_W)]
    return _col_move([g_up, g_gate], [((N_DEV, D_MODEL, _FFN_SHARD_W), g_up.dtype)], pieces, name="g_w_ffn_in_shards")[0]


def _dt_spread(w_dt):
    k = w_dt.shape[0]
    w4 = w_dt.reshape(k, SSD_GROUPS, SSD_HPG)
    return jnp.pad(w4, ((0, 0), (0, 0), (0, LANES - SSD_HPG))).reshape(k, DT_W)


def _dt_gather(w_wide):
    k = w_wide.shape[0]
    return w_wide.reshape(k, SSD_GROUPS, LANES)[:, :, :SSD_HPG].reshape(k, SSD_HEADS)


class _LateExchanges:
    def __init__(self, two_d, shape2):
        self.two_d, self.shape2 = two_d, shape2
        self.early_grads_received = None
        self.w_in_grads_received = None

    def row_pack(self, tree):
        return jnp.concatenate([tree[k] for k in _ROW_SHARDED], axis=0)

    def late_weights_plan(self):
        return _gather_plan([self.two_d["w_ffn_in"].astype(BF16), self.row_pack(self.two_d).astype(BF16)])

    def late_weights(self, gathered):
        w_ffn_in_all, rows_all = [g.reshape((N_DEV,) + g.shape[2:]) for g in gathered]
        out = {"w_ffn_in": _ffn_in_from_shards(w_ffn_in_all)}
        off = 0
        for k in _ROW_SHARDED:
            r = self.shape2[k][0]
            out[k] = rows_all[:, off:off + r].reshape(N_DEV * r, D_MODEL)
            off += r
        return out

    def early_grads_plan(self, grads):
        rows_send = jnp.concatenate([grads[k].reshape(N_DEV, self.shape2[k][0], D_MODEL) for k in _ROW_SHARDED], axis=1)
        return _all_to_all_plan([_ffn_in_to_shards(*grads["w_ffn_in"]), rows_send])

    def w_in_grads_plan(self, seg_grads):
        return _all_to_all_plan([_segments_to_w_in_shards(seg_grads)])


def _local_step(x, target, w, exchanges=None):
    h0 = jnp.concatenate([jnp.zeros((PAD, D_MODEL), F32), w["meta_tokens"], x], axis=0)
    segs = w["in_segs"]

    dtb = _dt_spread(w["ssd_dt_bias"])
    alog = _dt_spread(w["ssd_a_log"])
    dskip_w = jnp.repeat(w["ssd_d"], SSD_HEADDIM, axis=1)
    sinks = jnp.pad(w["attn_sinks"], ((0, 0), (0, LANES - ATT_HEADS)))
    onehot_t = w["onehot_t"] if "onehot_t" in w else _onehot_t()
    flat_tables = w["bias_tables"] if "bias_tables" in w else _bias_tables(w["rel_bias"].T, onehot_t)[0]
    tables = jnp.transpose(flat_tables.reshape(ATT_HEADS, 3, N_KEYS, BLOCK), (1, 0, 2, 3))

    u = _rms_fwd(h0, w["norm_mix_w"], name="rms_mix_fwd")
    z = _mm(u, segs["z"], name="in_z")
    xbc, pre = _mm_conv_fwd(u, segs["xbc"], w["ssd_conv_w"], w["ssd_conv_b"], name="in_xbc_conv_fwd")
    dt_raw = _mm(u, segs["dt"], name="in_dt")
    qkv = _mm(u, segs["qkv"], out_dtype=BF16, name="in_qkv")
    gates = _mm(u, segs["g"], out_dtype=BF16, name="in_g")
    (y, yn, hsave), gathered = _ssd_fwd(pre, dt_raw, z, dtb, alog, dskip_w, w["ssd_norm_w"],
                                        side=None if exchanges is None else exchanges.late_weights_plan())
    if exchanges is not None:
        w = {**w, **exchanges.late_weights(gathered)}
    w_ffn_up, w_ffn_gate = w["w_ffn_in"][:, :D_FF], w["w_ffn_in"][:, D_FF:]
    y_ssd = _mm(yn, w["w_ssd_branch"], out_dtype=BF16, name="ssd_out")
    att = _attn_fwd(qkv, tables, sinks)
    y_att = _mm(att, w["w_attn_branch"], out_dtype=BF16, name="att_out")
    merged, h1 = _merge_out_fwd(gates, y_ssd, y_att, w["gate_b"], w["w_out"], h0)
    u2 = _rms_fwd(h1, w["norm_ffn_w"], name="rms_ffn_fwd")
    x_up, x_gate, hid_up, hid_gate, act = _ffn_in_act_fwd(u2, w["w_ffn_in"], w["ffn_conv_w"], w["ffn_conv_b"])
    h2 = _mm(act, w["w_ffn_out"], c=h1, mask=True, name="ffn_out")
    dh2, dh2_b, loss_row, g_norm_final = _final_loss(h2, w["norm_final_w"], target)

    grads = {"norm_final_w": g_norm_final}
    grads["w_ffn_out"] = _mm(act, dh2_b, ta=True, mask=True, out_dtype=BF16, name="g_w_ffn_out")
    dx_up, dx_gate, dcw_up, dcw_gate, dcb_up, dcb_gate = _ffn_out_act_bwd(
        dh2_b, w["w_ffn_out"], hid_up, hid_gate, x_up, x_gate, w["ffn_conv_w"])
    grads["ffn_conv_w"] = jnp.concatenate([dcw_up, dcw_gate], axis=1)
    grads["ffn_conv_b"] = jnp.concatenate([dcb_up, dcb_gate], axis=1)
    (dh1, grads["norm_ffn_w"]), _ = _mm_rms_bwd([(dx_up, w_ffn_up), (dx_gate, w_ffn_gate)], h1, w["norm_ffn_w"], dh2,
                                                name="d_u2_rms_bwd")
    grads["w_ffn_in"] = (_mm(u2, dx_up, ta=True, out_dtype=BF16, name="g_w_ffn_up"),
                         _mm(u2, dx_gate, ta=True, out_dtype=BF16, name="g_w_ffn_gate"))

    grads["w_out"] = _mm(merged, dh1, ta=True, mask=True, out_dtype=BF16, name="g_w_out")
    dy_ssd, dy_att, dgates, grads["gate_b"] = _merge_out_bwd(dh1, w["w_out"], gates, y_ssd, y_att, w["gate_b"])
    dyn = _mm(dy_ssd, w["w_ssd_branch"], tb=True, name="d_yn")
    grads["w_ssd_branch"] = _mm(yn, dy_ssd, ta=True, out_dtype=BF16, name="g_w_ssd")
    datt = _mm(dy_att, w["w_attn_branch"], tb=True, out_dtype=BF16, name="d_att")
    grads["w_attn_branch"] = _mm(att, dy_att, ta=True, out_dtype=BF16, name="g_w_att")
    (dz, dpxs, dpb, dpc, ddt, grads["ssd_norm_w"], g_dtb, g_alog, g_dskip), received = _ssd_bwd(
        dyn, y, z, pre, dt_raw, hsave, dtb, alog, dskip_w, w["ssd_norm_w"],
        side=None if exchanges is None else exchanges.early_grads_plan(grads))
    if exchanges is not None:
        exchanges.early_grads_received = received
    grads["ssd_dt_bias"] = _dt_gather(g_dtb)
    grads["ssd_a_log"] = _dt_gather(g_alog)
    grads["ssd_d"] = _dt_gather(g_dskip)
    conv_g = _conv_bwd(dpxs, xbc, w["ssd_conv_w"], name="ssd_conv_bwd_x")
    conv_g = _conv_bwd(dpb, xbc, w["ssd_conv_w"], name="ssd_conv_bwd_b", col0=SSD_INNER, into=conv_g)
    dxbc, grads["ssd_conv_w"], grads["ssd_conv_b"] = _conv_bwd(
        dpc, xbc, w["ssd_conv_w"], name="ssd_conv_bwd_c", col0=SSD_INNER + SSD_BC, into=conv_g)
    dqkv, d_tables, d_sinks = _attn_bwd(datt, qkv, tables, sinks)
    grads["attn_sinks"] = d_sinks[:, :ATT_HEADS]
    dtab = jnp.transpose(d_tables, (1, 0, 2, 3)).reshape(ATT_HEADS, NT_ALL)
    grads["rel_bias"] = _bias_grad(dtab, onehot_t).T
    dsegs = {"z": dz, "xbc": dxbc, "dt": ddt, "qkv": dqkv, "g": dgates}
    grads["in_segs"] = [_mm(u, dsegs[nm], ta=True, out_dtype=BF16, name="g_w_in_" + nm) for nm, _ in _IN_SEGS]
    (dh0, grads["norm_mix_w"]), received = _mm_rms_bwd(
        [(dsegs[nm], segs[nm]) for nm, _ in _IN_SEGS], h0, w["norm_mix_w"], dh1, name="d_u_rms_bwd",
        side=None if exchanges is None else exchanges.w_in_grads_plan(grads["in_segs"]))
    if exchanges is not None:
        exchanges.w_in_grads_received = received[0]
    grads["meta_tokens"] = dh0[PAD:BLOCK]
    return loss_row[0, 0], dh0[BLOCK:], grads


def kernel(x, meta_tokens, norm_mix_w, w_in, ssd_conv_w, ssd_conv_b, ssd_dt_bias, ssd_a_log, ssd_d, ssd_norm_w, w_ssd_branch, w_attn_branch, attn_sinks, rel_bias, gate_b, w_out, norm_ffn_w, w_ffn_in, ffn_conv_w, ffn_conv_b, w_ffn_out, norm_final_w, loss_target, m_meta_tokens, m_norm_mix_w, m_w_in, m_ssd_conv_w, m_ssd_conv_b, m_ssd_dt_bias, m_ssd_a_log, m_ssd_d, m_ssd_norm_w, m_w_ssd_branch, m_w_attn_branch, m_attn_sinks, m_rel_bias, m_gate_b, m_w_out, m_norm_ffn_w, m_w_ffn_in, m_ffn_conv_w, m_ffn_conv_b, m_w_ffn_out, m_norm_final_w, v_meta_tokens, v_norm_mix_w, v_w_in, v_ssd_conv_w, v_ssd_conv_b, v_ssd_dt_bias, v_ssd_a_log, v_ssd_d, v_ssd_norm_w, v_w_ssd_branch, v_w_attn_branch, v_attn_sinks, v_rel_bias, v_gate_b, v_w_out, v_norm_ffn_w, v_w_ffn_in, v_ffn_conv_w, v_ffn_conv_b, v_w_ffn_out, v_norm_final_w):
    shard = dict(meta_tokens=meta_tokens, norm_mix_w=norm_mix_w, w_in=w_in, ssd_conv_w=ssd_conv_w,
                 ssd_conv_b=ssd_conv_b, ssd_dt_bias=ssd_dt_bias, ssd_a_log=ssd_a_log, ssd_d=ssd_d,
                 ssd_norm_w=ssd_norm_w, w_ssd_branch=w_ssd_branch, w_attn_branch=w_attn_branch,
                 attn_sinks=attn_sinks, rel_bias=rel_bias, gate_b=gate_b, w_out=w_out, norm_ffn_w=norm_ffn_w,
                 w_ffn_in=w_ffn_in, ffn_conv_w=ffn_conv_w, ffn_conv_b=ffn_conv_b, w_ffn_out=w_ffn_out,
                 norm_final_w=norm_final_w)
    mom_m = dict(zip(_WEIGHTS, (m_meta_tokens, m_norm_mix_w, m_w_in, m_ssd_conv_w, m_ssd_conv_b, m_ssd_dt_bias,
                                m_ssd_a_log, m_ssd_d, m_ssd_norm_w, m_w_ssd_branch, m_w_attn_branch, m_attn_sinks,
                                m_rel_bias, m_gate_b, m_w_out, m_norm_ffn_w, m_w_ffn_in, m_ffn_conv_w, m_ffn_conv_b,
                                m_w_ffn_out, m_norm_final_w)))
    mom_v = dict(zip(_WEIGHTS, (v_meta_tokens, v_norm_mix_w, v_w_in, v_ssd_conv_w, v_ssd_conv_b, v_ssd_dt_bias,
                                v_ssd_a_log, v_ssd_d, v_ssd_norm_w, v_w_ssd_branch, v_w_attn_branch, v_attn_sinks,
                                v_rel_bias, v_gate_b, v_w_out, v_norm_ffn_w, v_w_ffn_in, v_ffn_conv_w, v_ffn_conv_b,
                                v_w_ffn_out, v_norm_final_w)))
    orig_shape = {k: a.shape for k, a in shard.items()}
    two_d = {k: a.reshape(a.shape[-2:]) if a.ndim >= 2 else a.reshape(1, -1) for k, a in shard.items()}
    shape2 = {k: a.shape for k, a in two_d.items()}

    def as2d(tree):
        return {k: tree[k].reshape(shape2[k]) for k in _WEIGHTS}

    mom_m, mom_v = as2d(mom_m), as2d(mom_v)

    exchanges = _LateExchanges(two_d, shape2)
    row_pack = exchanges.row_pack
    small_pack = _pack_rows([two_d[k] for k in _SMALL_SHARDED], LANES, SMALL_ROW_MULT)
    onehot_t = _onehot_t()
    gather = [two_d["w_in"].astype(BF16), small_pack]
    flat_tables, gathered = _bias_tables(two_d["rel_bias"].T, onehot_t, side=_gather_plan(gather))
    w_in_all, small_all = [g.reshape((N_DEV,) + a.shape) for g, a in zip(gathered, gather)]
    full = {k: two_d[k] for k in _SMALL_REPLICATED}
    full["onehot_t"], full["bias_tables"] = onehot_t, flat_tables
    full["in_segs"] = _w_in_to_segments(w_in_all)
    small_flat = small_all.reshape(N_DEV, -1)
    off = 0
    for k in _SMALL_SHARDED:
        size = int(np.prod(shape2[k]))
        full[k] = _gather_full(small_flat[:, off:off + size].reshape((N_DEV,) + shape2[k]), k, shape2[k])
        off += size

    loss_local, grad_x, grads = _local_step(x[0], loss_target[0], full, exchanges)

    small_names = _SMALL_SHARDED + _SMALL_REPLICATED
    small_send = _pack_rows([grads[k] for k in small_names] + [loss_local.reshape(1)], LANES, SMALL_ROW_MULT)
    in_recv = exchanges.w_in_grads_received
    ffn_recv, rows_recv = exchanges.early_grads_received

    w_in_out, (small_recv,) = _adamw(two_d["w_in"], in_recv, mom_m["w_in"], mom_v["w_in"], name="adamw_w_in",
                                     side=_all_to_all_plan([small_send], [False]))
    big = {"w_in": w_in_out,
           "w_ffn_in": _adamw(two_d["w_ffn_in"], ffn_recv, mom_m["w_ffn_in"], mom_v["w_ffn_in"], name="adamw_w_ffn_in")}
    rows_out = _adamw(row_pack(two_d), rows_recv, row_pack(mom_m), row_pack(mom_v), name="adamw_rows")
    off = 0
    for k in _ROW_SHARDED:
        r = shape2[k][0]
        big[k] = [a[off:off + r] for a in rows_out]
        off += r
    me =4 * lax.axis_index("x") + 2 * lax.axis_index("y") + lax.axis_index("c")
    small_full_shapes = [grads[k].shape for k in small_names]
    n_small = sum(int(np.prod(s)) for s in small_full_shapes)

    def packed_small(tree):
        parts = []
        for k in small_names:
            a = tree[k]
            if k in _SMALL_SHARDED:
                fullw = jnp.zeros(grads[k].shape, F32)
                a = lax.dynamic_update_slice(fullw, a, (0, me * a.shape[1]))
            parts.append(a)
        return _pack_rows(parts + [jnp.zeros((1,), F32)], LANES, SMALL_ROW_MULT)

    g_small, d_small, m_small, v_small = _adamw(packed_small(two_d), small_recv, packed_small(mom_m),
                                                packed_small(mom_v), name="adamw_small")

    def unpack_all(which, small):
        out = {k: big[k][which] for k in _BIG}
        flat = small.reshape(-1)
        for k, a in zip(small_names, _unpack(flat, small_full_shapes)):
            if k in _SMALL_SHARDED:
                a = lax.dynamic_slice(a, (0, me * shape2[k][1]), shape2[k])
            out[k] = a
        return out, flat[n_small]

    g_all, loss = unpack_all(0, g_small)
    d_all, _ = unpack_all(1, d_small)
    m_all, _ = unpack_all(2, m_small)
    v_all, _ = unpack_all(3, v_small)

    def final(tree):
        return [tree[k].reshape(orig_shape[k]) for k in _WEIGHTS]

    return (loss, grad_x[None], *final(g_all), *final(d_all), *final(m_all), *final(v_all))
```

```python
import functools
import math

import numpy as np
import jax
import jax.numpy as jnp
from jax import lax
from jax.experimental import pallas as pl
from jax.experimental.pallas import tpu as pltpu

F32 = jnp.float32
BF16 = jnp.bfloat16
HIGHEST = lax.Precision.HIGHEST

D_MODEL = 1024
N_META = 16
BLOCK = 128
PAD = BLOCK - N_META
EPS = 1e-6
NEG = -1e30
SSD_INNER = 2 * D_MODEL
SSD_HEADDIM = 64
SSD_HEADS = SSD_INNER // SSD_HEADDIM
SSD_GROUPS = 4
SSD_HPG = SSD_HEADS // SSD_GROUPS
SSD_STATE = 128
SSD_CONV = 4
SSD_GW = SSD_HPG * SSD_HEADDIM
SSD_BC = SSD_GROUPS * SSD_STATE
SSD_XBC = SSD_INNER + 2 * SSD_BC
ATT_HEADS = 16
ATT_KV_HEADS = 2
ATT_HEADDIM = 64
ATT_GQ = ATT_HEADS // ATT_KV_HEADS
ATT_Q = ATT_HEADS * ATT_HEADDIM
ATT_KV = ATT_KV_HEADS * ATT_HEADDIM
REL_BUCKETS = 32
REL_MAX_DIST = 128
D_FF = 2816
FFN_CONV = 3
ADAM_LR = 0.001
ADAM_B1 = 0.9
ADAM_B2 = 0.999
ADAM_EPS = 1e-08
ADAM_WD = 0.01
ADAM_STEP = 10

N_DEV = 8
LANES = 128
SUBLANES = 8
BF16_ROWS = 16
DT_W = SSD_GROUPS * LANES
VMEM_LIMIT_BYTES = 56 * 1024 * 1024
MESH = pl.DeviceIdType.MESH

SMALL_ROW_MULT = 16

N_KEYS = 3 * BLOCK
NT_ALL = 3 * N_KEYS * BLOCK
NT_TILE = 8192


def _cparams(*sem):
    return pltpu.CompilerParams(dimension_semantics=sem, vmem_limit_bytes=VMEM_LIMIT_BYTES)


def _row_tile(n, cap):
    best = None
    for t in range(16, min(n, cap) + 1, 16):
        if n % t == 0:
            best = t
    return best or n


def _col_tile(n, cap):
    for t in (1408, 1280, 1024, 768, 640, 512, 384, 256, 128):
        if t <= cap and n % t == 0:
            return t
    return n


def _sigmoid(x):
    return 0.5 * jnp.tanh(0.5 * x) + 0.5


def _silu(x):
    return x * _sigmoid(x)


def _softplus(x):
    return jnp.maximum(x, 0.0) + jnp.log(1.0 + jnp.exp(-jnp.abs(x)))


def _dot_nt(a, b):
    return lax.dot_general(a, b, (((1,), (1,)), ((), ())), preferred_element_type=F32)


def _dot_tn(a, b):
    return lax.dot_general(a, b, (((0,), (0,)), ((), ())), preferred_element_type=F32)


def _dot(a, b):
    return jnp.dot(a, b, preferred_element_type=F32)


def _bf16_terms(x, terms):
    out, rest = [], x
    for _ in range(terms):
        part = rest.astype(BF16)
        out.append(part)
        rest = rest - part.astype(F32)
    return out


def _dot_sel(x, sel, terms=3):
    return sum(_dot(part, sel) for part in _bf16_terms(x, terms))


def _sel_dot(sel, x, terms=3):
    return sum(_dot(sel, part) for part in _bf16_terms(x, terms))


def _sum_all(x):
    return jnp.sum(jnp.sum(x, axis=1, keepdims=True), axis=0, keepdims=True)


MM_ROW_CAPS = (2080, 1664, 832, 416)
MM_COL_CAP = 1408
MM_VMEM_BUDGET = 44 * 1024 * 1024


def _mm_tiles(rows, cols, vmem_bytes):
    col_cands = [t for t in (2048, 1536, 1408, 1280, 1024, 768, 640, 512, 384, 256, 128) if cols % t == 0]
    if cols <= 2 * MM_COL_CAP:
        col_cands.append(cols)
    best = None
    for cap in MM_ROW_CAPS:
        tr = _row_tile(rows, cap)
        for tc in col_cands:
            if vmem_bytes(tr, tc) <= MM_VMEM_BUDGET and (best is None or tr * tc > best[0] * best[1]):
                best = (tr, tc)
    assert best is not None, (rows, cols)
    return best


def _mm(a, b, *, name, ta=False, tb=False, c=None, mask=False, out_dtype=F32):
    if not ta:
        m, k = a.shape
        n = b.shape[0] if tb else b.shape[1]
        tm, tn = _mm_tiles(m, n, lambda t_m, t_n: 2 * (t_m * k * a.dtype.itemsize + k * t_n * b.dtype.itemsize
                                                       + t_m * t_n * (jnp.dtype(out_dtype).itemsize
                                                                      + (0 if c is None else c.dtype.itemsize)))
                           + 4 * t_m * t_n)

        def body(*refs):
            if c is None:
                a_ref, b_ref, o_ref = refs
            else:
                a_ref, b_ref, c_ref, o_ref = refs
            acc = (_dot_nt if tb else _dot)(a_ref[...].astype(BF16), b_ref[...].astype(BF16))
            if mask:
                row = pl.program_id(0) * tm + lax.broadcasted_iota(jnp.int32, (tm, 1), 0)
                acc = jnp.where(row >= PAD, acc, 0.0)
            if c is not None:
                acc = acc + c_ref[...]
            o_ref[...] = acc.astype(out_dtype)

        b_spec = pl.BlockSpec((tn, k), lambda i, j: (j, 0)) if tb else pl.BlockSpec((k, tn), lambda i, j: (0, j))
        in_specs = [pl.BlockSpec((tm, k), lambda i, j: (i, 0)), b_spec]
        args = [a, b]
        if c is not None:
            in_specs.append(pl.BlockSpec((tm, tn), lambda i, j: (i, j)))
            args.append(c)
        return pl.pallas_call(
            body, name=name, grid=(m // tm, n // tn), in_specs=in_specs,
            out_specs=pl.BlockSpec((tm, tn), lambda i, j: (i, j)),
            out_shape=jax.ShapeDtypeStruct((m, n), out_dtype),
            compiler_params=_cparams("parallel", "parallel"))(*args)

    kc, m = a.shape
    n = b.shape[1]
    tm = _col_tile(m, MM_COL_CAP)
    tk, tn = _mm_tiles(kc, n, lambda t_k, t_n: 2 * (t_k * tm * a.dtype.itemsize + t_k * t_n * b.dtype.itemsize
                                                    + tm * t_n * jnp.dtype(out_dtype).itemsize) + 8 * tm * t_n)

    n_k = kc // tk

    def body_t(a_ref, b_ref, o_ref, acc_ref):
        kk = pl.program_id(2)
        bb = b_ref[...]
        if mask:
            row = kk * tk + lax.broadcasted_iota(jnp.int32, (tk, 1), 0)
            bb = jnp.where(row >= PAD, bb, jnp.zeros_like(bb))
        p = _dot_tn(a_ref[...].astype(BF16), bb.astype(BF16))

        @pl.when(kk == 0)
        def _():
            acc_ref[...] = p

        @pl.when(kk > 0)
        def _():
            acc_ref[...] += p

        @pl.when(kk == n_k - 1)
        def _():
            o_ref[...] = acc_ref[...].astype(out_dtype)

    return pl.pallas_call(
        body_t, name=name, grid=(m // tm, n // tn, n_k),
        in_specs=[pl.BlockSpec((tk, tm), lambda i, j, kk: (kk, i)), pl.BlockSpec((tk, tn), lambda i, j, kk: (kk, j))],
        out_specs=pl.BlockSpec((tm, tn), lambda i, j, kk: (i, j)),
        out_shape=jax.ShapeDtypeStruct((m, n), out_dtype),
        scratch_shapes=[pltpu.VMEM((tm, tn), F32)],
        compiler_params=_cparams("parallel", "parallel", "arbitrary"))(a, b)


def _mm_rms_bwd(pairs, x, w, dres, *, name, side=None):
    m, d = x.shape
    tm = _row_tile(m, 416)
    n_pairs = len(pairs)

    def body(*refs):
        a_refs, b_refs = refs[:n_pairs], refs[n_pairs:2 * n_pairs]
        x_ref, w_ref, dres_ref, dx_ref, dw_ref = refs[2 * n_pairs:]
        i = pl.program_id(0)
        dyv = None
        for a_ref, b_ref in zip(a_refs, b_refs):
            term = _dot_nt(a_ref[...].astype(BF16), b_ref[...])
            dyv = term if dyv is None else dyv + term
        xv = x_ref[...]
        r = lax.rsqrt(jnp.mean(xv * xv, axis=-1, keepdims=True) + EPS)
        xh = xv * r
        g = dyv * w_ref[...]
        dx_ref[...] = r * (g - xh * jnp.mean(g * xh, axis=-1, keepdims=True)) + dres_ref[...]
        part = jnp.sum(dyv * xh, axis=0, keepdims=True)

        @pl.when(i == 0)
        def _():
            dw_ref[...] = part

        @pl.when(i > 0)
        def _():
            dw_ref[...] += part

    row = pl.BlockSpec((tm, d), lambda i: (i, 0))
    vec = pl.BlockSpec((1, d), lambda i: (0, 0))
    in_specs = ([pl.BlockSpec((tm, a.shape[1]), lambda i: (i, 0)) for a, _ in pairs]
                + [pl.BlockSpec(b.shape, lambda i: (0, 0), pipeline_mode=pl.Buffered(1)) for _, b in pairs]
                + [row, vec, row])
    return _call_with_side(
        body, side, name=name, grid=(m // tm,), in_specs=in_specs, out_specs=[row, vec],
        out_shape=[jax.ShapeDtypeStruct((m, d), F32), jax.ShapeDtypeStruct((1, d), F32)], scratch_shapes=[],
        args=[a for a, _ in pairs] + [b for _, b in pairs] + [x, w, dres], semantics=("arbitrary",))


def _rms_fwd(h, w, *, name):
    n, d = h.shape
    tm = _row_tile(n, 832)

    def body(h_ref, w_ref, o_ref):
        x = h_ref[...]
        r = lax.rsqrt(jnp.mean(x * x, axis=-1, keepdims=True) + EPS)
        o_ref[...] = (x * r * w_ref[...]).astype(BF16)

    return pl.pallas_call(
        body, name=name, grid=(n // tm,),
        in_specs=[pl.BlockSpec((tm, d), lambda i: (i, 0)), pl.BlockSpec((1, d), lambda i: (0, 0))],
        out_specs=pl.BlockSpec((tm, d), lambda i: (i, 0)),
        out_shape=jax.ShapeDtypeStruct((n, d), BF16),
        compiler_params=_cparams("parallel"))(h, w)


def _ffn_out_loss(act, w_out, h, norm_w, target):
    n, k = act.shape
    d = w_out.shape[1]
    nb = n // BLOCK
    per = max(p for p in (5, 4, 3, 2, 1) if nb % p == 0)
    tm = per * BLOCK

    def body(a_ref, w_ref, h_ref, nw_ref, *rest):
        t_refs, (dh_ref, dhb_ref, loss_ref, dw_ref) = rest[:per], rest[per:]
        i = pl.program_id(0)
        row = i * tm + lax.broadcasted_iota(jnp.int32, (tm, 1), 0)
        xv = jnp.where(row >= PAD, _dot(a_ref[...], w_ref[...]), 0.0) + h_ref[...]
        r = lax.rsqrt(jnp.mean(xv * xv, axis=-1, keepdims=True) + EPS)
        xh = xv * r
        wv = nw_ref[...]
        tgt = jnp.concatenate([t_ref[0] for t_ref in t_refs], axis=0)
        err = jnp.where(row >= BLOCK, xh * wv - tgt, 0.0)
        dyv = err * (1.0 / d)
        g = dyv * wv
        dh = r * (g - xh * jnp.mean(g * xh, axis=-1, keepdims=True))
        dh_ref[...] = dh
        dhb_ref[...] = dh.astype(BF16)
        lpart = jnp.broadcast_to(0.5 * _sum_all(err * err) * (1.0 / d), (1, LANES))
        wpart = jnp.sum(dyv * xh, axis=0, keepdims=True)

        @pl.when(i == 0)
        def _():
            loss_ref[...] = lpart
            dw_ref[...] = wpart

        @pl.when(i > 0)
        def _():
            loss_ref[...] += lpart
            dw_ref[...] += wpart

    def target_spec(j):
        return pl.BlockSpec((1, BLOCK, d), lambda i: (jnp.clip(per * i - 1 + j, 0, nb - 2), 0, 0))

    row = pl.BlockSpec((tm, d), lambda i: (i, 0))
    vec = pl.BlockSpec((1, d), lambda i: (0, 0))
    blocks = target.reshape(nb - 1, BLOCK, d)
    return pl.pallas_call(
        body, name="ffn_out_loss", grid=(nb // per,),
        in_specs=[pl.BlockSpec((tm, k), lambda i: (i, 0)),
                  pl.BlockSpec((k, d), lambda i: (0, 0), pipeline_mode=pl.Buffered(1)), row, vec]
        + [target_spec(j) for j in range(per)],
        out_specs=[row, row, pl.BlockSpec((1, LANES), lambda i: (0, 0)), vec],
        out_shape=[jax.ShapeDtypeStruct((n, d), F32), jax.ShapeDtypeStruct((n, d), BF16),
                   jax.ShapeDtypeStruct((1, LANES), F32), jax.ShapeDtypeStruct((1, d), F32)],
        compiler_params=_cparams("arbitrary"))(act, w_out, h, norm_w, *([blocks] * per))


def _final_loss(h, w, target):
    n, d = h.shape
    nb = n // BLOCK

    def body(h_ref, w_ref, t_ref, dh_ref, dhb_ref, loss_ref, dw_ref):
        i = pl.program_id(0)
        xv = h_ref[...]
        r = lax.rsqrt(jnp.mean(xv * xv, axis=-1, keepdims=True) + EPS)
        xh = xv * r
        wv = w_ref[...]
        err = jnp.where(i >= 1, xh * wv - t_ref[...], 0.0)
        dyv = err * (1.0 / d)
        g = dyv * wv
        dh = r * (g - xh * jnp.mean(g * xh, axis=-1, keepdims=True))
        dh_ref[...] = dh
        dhb_ref[...] = dh.astype(BF16)
        lpart = jnp.broadcast_to(0.5 * _sum_all(err * err) * (1.0 / d), (1, LANES))
        wpart = jnp.sum(dyv * xh, axis=0, keepdims=True)

        @pl.when(i == 0)
        def _():
            loss_ref[...] = lpart
            dw_ref[...] = wpart

        @pl.when(i > 0)
        def _():
            loss_ref[...] += lpart
            dw_ref[...] += wpart

    row = pl.BlockSpec((BLOCK, d), lambda i: (i, 0))
    vec = pl.BlockSpec((1, d), lambda i: (0, 0))
    return pl.pallas_call(
        body, name="final_loss", grid=(nb,),
        in_specs=[row, vec, pl.BlockSpec((BLOCK, d), lambda i: (jnp.maximum(i - 1, 0), 0))],
        out_specs=[row, row, pl.BlockSpec((1, LANES), lambda i: (0, 0)), vec],
        out_shape=[jax.ShapeDtypeStruct((n, d), F32), jax.ShapeDtypeStruct((n, d), BF16),
                   jax.ShapeDtypeStruct((1, LANES), F32), jax.ShapeDtypeStruct((1, d), F32)],
        compiler_params=_cparams("arbitrary"))(h, w, target)


def _main_spec(tm, cb, off=0):
    return pl.BlockSpec((tm, cb), lambda j, i: (i, j + off))


def _prev_spec(tm, cb, off=0):
    r8 = tm // SUBLANES
    return pl.BlockSpec((SUBLANES, cb), lambda j, i: (jnp.maximum(i * r8 - 1, 0), j + off))


def _next_spec(tm, cb, n_rows, off=0):
    r8 = tm // SUBLANES
    last = n_rows // SUBLANES - 1
    return pl.BlockSpec((SUBLANES, cb), lambda j, i: (jnp.minimum((i + 1) * r8, last), j + off))


def _with_prev(prev_ref, main_ref, i):
    prev = jnp.where(i > 0, prev_ref[...], 0.0)
    return jnp.concatenate([prev, main_ref[...]], axis=0)


def _with_next(main, nxt, i, n_tiles):
    return jnp.concatenate([main, jnp.where(i < n_tiles - 1, nxt, 0.0)], axis=0)


def _back(xx, s, tm):
    if s == 0:
        return xx[SUBLANES:SUBLANES + tm]
    return pltpu.roll(xx, s, 0)[SUBLANES:SUBLANES + tm]


def _ahead(xx, s, tm):
    if s == 0:
        return xx[:tm]
    return pltpu.roll(xx, xx.shape[0] - s, 0)[:tm]


def _mm_conv_fwd(u, w_in, w, b, *, name):
    n = u.shape[0]
    cdim = w_in.shape[1]
    kw = w.shape[0]
    tm = _row_tile(n, 832)
    cb = _col_tile(cdim, 512)
    nt = n // tm

    def body(u_ref, w_in_ref, w_ref, b_ref, x_ref, o_ref, acc_scr, halo_scr):
        j, i = pl.program_id(0), pl.program_id(1)

        @pl.when((j == 0) & (i == 0))
        def _():
            acc_scr[...] = jnp.zeros_like(acc_scr)
            halo_scr[...] = jnp.zeros_like(halo_scr)

        new = _dot(u_ref[...], w_in_ref[...])
        prev = acc_scr[...]
        xx = jnp.concatenate([jnp.where(i >= 2, halo_scr[...], 0.0), prev], axis=0)
        acc = jnp.broadcast_to(b_ref[...], (tm, cb))
        for k in range(kw):
            acc = acc + w_ref[k:k + 1, :] * _back(xx, kw - 1 - k, tm)
        x_ref[...] = prev.astype(BF16)
        o_ref[...] = acc
        halo_scr[...] = prev[tm - SUBLANES:, :]
        acc_scr[...] = new

    out = pl.BlockSpec((tm, cb), lambda j, i: (jnp.maximum(i - 1, 0), j))
    shp = jax.ShapeDtypeStruct((n, cdim), F32)
    return pl.pallas_call(
        body, name=name, grid=(cdim // cb, nt + 1),
        in_specs=[pl.BlockSpec((tm, u.shape[1]), lambda j, i: (jnp.minimum(i, nt - 1), 0)),
                  pl.BlockSpec((w_in.shape[0], cb), lambda j, i: (0, j)),
                  pl.BlockSpec((kw, cb), lambda j, i: (0, j)), pl.BlockSpec((1, cb), lambda j, i: (0, j))],
        out_specs=[out, out], out_shape=[jax.ShapeDtypeStruct((n, cdim), BF16), shp],
        scratch_shapes=[pltpu.VMEM((tm, cb), F32), pltpu.VMEM((SUBLANES, cb), F32)],
        compiler_params=_cparams("arbitrary", "arbitrary"))(u, w_in, w, b)


def _conv_bwd_core(dpre_ext, x, w_ref, kw, tm):
    dx = None
    dws = []
    for k in range(kw):
        shifted = _ahead(dpre_ext, kw - 1 - k, tm)
        term = w_ref[k:k + 1, :] * shifted
        dx = term if dx is None else dx + term
        dws.append(jnp.sum(shifted * x, axis=0, keepdims=True))
    return dx, dws, jnp.sum(dpre_ext[:tm], axis=0, keepdims=True)


def _acc_rows(i, dw_ref, db_ref, dws, db):
    @pl.when(i == 0)
    def _():
        for k, v in enumerate(dws):
            dw_ref[k:k + 1, :] = v
        db_ref[...] = db

    @pl.when(i > 0)
    def _():
        for k, v in enumerate(dws):
            dw_ref[k:k + 1, :] += v
        db_ref[...] += db


def _conv_bwd(dpre, x, w, *, name, col0=0, into=None):
    n, cdim = x.shape
    kw = w.shape[0]
    tm = _row_tile(n, 832)
    cb = _col_tile(cdim, 512)
    nt = n // tm
    off = col0 // cb
    n_alias = 0 if into is None else 3

    def body(d_ref, dn_ref, x_ref, w_ref, *rest):
        dx_ref, dw_ref, db_ref = rest[n_alias:]
        i = pl.program_id(1)
        dpre_ext = _with_next(d_ref[...], dn_ref[...], i, nt)
        dx, dws, db = _conv_bwd_core(dpre_ext, x_ref[...], w_ref, kw, tm)
        dx_ref[...] = dx.astype(BF16)
        _acc_rows(i, dw_ref, db_ref, dws, db)

    wspec = pl.BlockSpec((kw, cb), lambda j, i: (0, j + off))
    bspec = pl.BlockSpec((1, cb), lambda j, i: (0, j + off))
    return pl.pallas_call(
        body, name=name, grid=(dpre.shape[1] // cb, nt),
        in_specs=[_main_spec(tm, cb), _next_spec(tm, cb, n), _main_spec(tm, cb, off), wspec]
        + [pl.BlockSpec(memory_space=pl.ANY)] * n_alias,
        out_specs=[_main_spec(tm, cb, off), wspec, bspec],
        out_shape=[jax.ShapeDtypeStruct((n, cdim), BF16), jax.ShapeDtypeStruct((kw, cdim), F32),
                   jax.ShapeDtypeStruct((1, cdim), F32)],
        input_output_aliases={4 + k: k for k in range(n_alias)},
        compiler_params=_cparams("parallel", "arbitrary"))(dpre, dpre, x, w, *(into or ()))


def _ffn_in_act_fwd(u, w_in, w, b):
    n = u.shape[0]
    kw = w.shape[0]
    tm = _row_tile(n, 832)
    cb = _col_tile(D_FF, 256)
    nc = D_FF // cb
    nt = n // tm

    def body(u_ref, wu_in_ref, wg_in_ref, wu_ref, wg_ref, bu_ref, bg_ref,
             xu_ref, xg_ref, hu_ref, hg_ref, act_ref, acc_scr, halo_scr):
        j, i = pl.program_id(0), pl.program_id(1)

        @pl.when((j == 0) & (i == 0))
        def _():
            acc_scr[...] = jnp.zeros_like(acc_scr)
            halo_scr[...] = jnp.zeros_like(halo_scr)

        ub = u_ref[...]
        new = [_dot(ub, wu_in_ref[...]), _dot(ub, wg_in_ref[...])]
        hid = []
        for half, (x_ref, w_ref, b_ref) in enumerate(((xu_ref, wu_ref, bu_ref), (xg_ref, wg_ref, bg_ref))):
            prev = acc_scr[half]
            xx = jnp.concatenate([jnp.where(i >= 2, halo_scr[half], 0.0), prev], axis=0)
            acc = jnp.broadcast_to(b_ref[...], (tm, cb))
            for k in range(kw):
                acc = acc + w_ref[k:k + 1, :] * _back(xx, kw - 1 - k, tm)
            x_ref[...] = prev.astype(BF16)
            hid.append(acc)
            halo_scr[half] = prev[tm - SUBLANES:, :]
            acc_scr[half] = new[half]
        hu_ref[...] = hid[0].astype(BF16)
        hg_ref[...] = hid[1].astype(BF16)
        act_ref[...] = (_silu(hid[1]) * hid[0]).astype(BF16)

    def wspec(off):
        return pl.BlockSpec((kw, cb), lambda j, i: (0, j + off))

    def bspec(off):
        return pl.BlockSpec((1, cb), lambda j, i: (0, j + off))

    def in_w(off):
        return pl.BlockSpec((w_in.shape[0], cb), lambda j, i: (0, j + off))

    out = pl.BlockSpec((tm, cb), lambda j, i: (jnp.maximum(i - 1, 0), j))
    bf16_out = jax.ShapeDtypeStruct((n, D_FF), BF16)
    return pl.pallas_call(
        body, name="ffn_in_act_fwd", grid=(nc, nt + 1),
        in_specs=[pl.BlockSpec((tm, u.shape[1]), lambda j, i: (jnp.minimum(i, nt - 1), 0)), in_w(0), in_w(nc),
                  wspec(0), wspec(nc), bspec(0), bspec(nc)],
        out_specs=[out] * 5,
        out_shape=[bf16_out] * 5,
        scratch_shapes=[pltpu.VMEM((2, tm, cb), F32), pltpu.VMEM((2, SUBLANES, cb), F32)],
        compiler_params=_cparams("arbitrary", "arbitrary"))(u, w_in, w_in, w, w, b, b)


def _ffn_out_act_bwd(dh, w_out, hu, hg, x_up, x_gate, w):
    n = x_up.shape[0]
    kw = w.shape[0]
    tm = _row_tile(n, 832)
    cb = _col_tile(D_FF, 256)
    nc = D_FF // cb
    nt = n // tm

    def body(dh_ref, wo_ref, hu_ref, hun_ref, hg_ref, hgn_ref, xu_ref, xg_ref, wu_ref, wg_ref,
             dxu_ref, dxg_ref, dwu_ref, dwg_ref, dbu_ref, dbg_ref, acc_scr, halo_scr):
        j, i = pl.program_id(0), pl.program_id(1)

        @pl.when((j == 0) & (i == 0))
        def _():
            acc_scr[...] = jnp.zeros_like(acc_scr)
            halo_scr[...] = jnp.zeros_like(halo_scr)

        tile = jnp.maximum(nt - 1 - i, 0)
        row = tile * tm + lax.broadcasted_iota(jnp.int32, (tm, 1), 0)
        new = jnp.where(row >= PAD, _dot_nt(dh_ref[...].astype(BF16), wo_ref[...]), 0.0)
        prev = jnp.where(i >= 1, acc_scr[...], 0.0)
        dact_e = jnp.concatenate([prev, jnp.where(i >= 2, halo_scr[...], 0.0)], axis=0)
        last = nt - i >= nt - 1
        up_e = jnp.concatenate([hu_ref[...].astype(F32), jnp.where(last, 0.0, hun_ref[...].astype(F32))], axis=0)
        gate_e = jnp.concatenate([hg_ref[...].astype(F32), jnp.where(last, 0.0, hgn_ref[...].astype(F32))], axis=0)
        halo_scr[...] = prev[:BF16_ROWS, :]
        acc_scr[...] = new
        sg = _sigmoid(gate_e)
        dup_e = dact_e * (gate_e * sg)
        dgate_e = dact_e * up_e * (sg * (1.0 + gate_e * (1.0 - sg)))
        dx, dws, db = _conv_bwd_core(dup_e, xu_ref[...], wu_ref, kw, tm)
        dxu_ref[...] = dx.astype(BF16)
        _acc_rows(i, dwu_ref, dbu_ref, dws, db)
        dx, dws, db = _conv_bwd_core(dgate_e, xg_ref[...], wg_ref, kw, tm)
        dxg_ref[...] = dx.astype(BF16)
        _acc_rows(i, dwg_ref, dbg_ref, dws, db)

    def done_tile(i):
        return jnp.minimum(nt - i, nt - 1)

    halo_blocks = tm // BF16_ROWS
    main = pl.BlockSpec((tm, cb), lambda j, i: (done_tile(i), j))
    nxt = pl.BlockSpec((BF16_ROWS, cb),
                       lambda j, i: (jnp.minimum((done_tile(i) + 1) * halo_blocks, n // BF16_ROWS - 1), j))
    wspec0 = pl.BlockSpec((kw, cb), lambda j, i: (0, j))
    wspec1 = pl.BlockSpec((kw, cb), lambda j, i: (0, j + nc))
    bspec = pl.BlockSpec((1, cb), lambda j, i: (0, j))
    return pl.pallas_call(
        body, name="ffn_out_act_bwd", grid=(nc, nt + 1),
        in_specs=[pl.BlockSpec((tm, dh.shape[1]), lambda j, i: (jnp.maximum(nt - 1 - i, 0), 0)),
                  pl.BlockSpec((cb, w_out.shape[1]), lambda j, i: (j, 0)),
                  main, nxt, main, nxt, main, main, wspec0, wspec1],
        out_specs=[main, main, wspec0, wspec0, bspec, bspec],
        out_shape=[jax.ShapeDtypeStruct((n, D_FF), BF16), jax.ShapeDtypeStruct((n, D_FF), BF16),
                   jax.ShapeDtypeStruct((kw, D_FF), F32), jax.ShapeDtypeStruct((kw, D_FF), F32),
                   jax.ShapeDtypeStruct((1, D_FF), F32), jax.ShapeDtypeStruct((1, D_FF), F32)],
        scratch_shapes=[pltpu.VMEM((tm, cb), F32), pltpu.VMEM((BF16_ROWS, cb), F32)],
        compiler_params=_cparams("arbitrary", "arbitrary"))(dh, w_out, hu, hu, hg, hg, x_up, x_gate, w, w)


def _ssd_prep(pxs_ref, pb_ref, pc_ref, dtr_ref, dtb_ref, alog_ref, c):
    xs = _silu(pxs_ref[...])
    bm = _silu(pb_ref[...])
    cm = _silu(pc_ref[...])
    return (xs, bm, cm) + _ssd_decay(dtr_ref, dtb_ref, alog_ref, c)


def _ssd_decay(dtr_ref, dtb_ref, alog_ref, c):
    row =lax.broadcasted_iota(jnp.int32, (BLOCK, 1), 0) + c * BLOCK
    valid = (row >= PAD).astype(F32)
    dtr = dtr_ref[...] + dtb_ref[...]
    dt = _softplus(dtr) * valid
    a = -jnp.exp(alog_ref[...])
    lam = dt * a
    ri = lax.broadcasted_iota(jnp.int32, (BLOCK, BLOCK), 0)
    ci = lax.broadcasted_iota(jnp.int32, (BLOCK, BLOCK), 1)
    causal = ci <= ri
    cs = _sel_dot(causal.astype(BF16), lam)
    return valid, dtr, dt, a, lam, cs, causal


def _head_cols(r):
    return slice(SSD_HEADDIM * r, SSD_HEADDIM * (r + 1))


def _ssd_specs(nc, rev):
    def cidx(c):
        return nc - 1 - c if rev else c

    xs = pl.BlockSpec((BLOCK, SSD_GW), lambda g, c: (cidx(c), g))
    bspec = pl.BlockSpec((BLOCK, SSD_STATE), lambda g, c: (cidx(c), SSD_INNER // SSD_STATE + g))
    cspec = pl.BlockSpec((BLOCK, SSD_STATE), lambda g, c: (cidx(c), (SSD_INNER + SSD_BC) // SSD_STATE + g))
    lane = pl.BlockSpec((BLOCK, LANES), lambda g, c: (cidx(c), g))
    vec = pl.BlockSpec((1, LANES), lambda g, c: (0, g))
    wide_vec = pl.BlockSpec((1, SSD_GW), lambda g, c: (0, g))
    hsave = pl.BlockSpec((1, 1, SSD_GW, SSD_STATE), lambda g, c: (cidx(c), g, 0, 0))
    return xs, bspec, cspec, lane, vec, wide_vec, hsave


def _head_spread_matrix():
    r = lax.broadcasted_iota(jnp.int32, (LANES, SSD_GW), 0)
    col = lax.broadcasted_iota(jnp.int32, (LANES, SSD_GW), 1)
    return (col // SSD_HEADDIM == r).astype(BF16)


def _const_spec(shape):
    return pl.BlockSpec(shape, lambda g, c: (0,) * len(shape))


def _spread_heads(per_head, e_ref):
    wide = _dot_sel(jnp.concatenate(per_head, axis=0), e_ref[...])
    return [wide[BLOCK * k:BLOCK * (k + 1)] for k in range(len(per_head))]


def _call_with_side(body, side, *, name, grid, in_specs, out_specs, out_shape, scratch_shapes, args,
                    semantics=("parallel", "arbitrary")):
    if side is None:
        outs = pl.pallas_call(body, name=name, grid=grid, in_specs=in_specs, out_specs=out_specs, out_shape=out_shape,
                              scratch_shapes=scratch_shapes, compiler_params=_cparams(*semantics))(*args)
        return outs, []
    n_in, n_out, n_scr, n_side = len(in_specs), len(out_specs), len(scratch_shapes), len(side.arrays)

    def body_with_side(*refs):
        ins, rest = refs[:n_in + n_side], refs[n_in + n_side:]
        outs, scratch = rest[:n_out + n_side], rest[n_out + n_side:]
        side_refs = (ins[n_in:], outs[n_out:], scratch[n_scr:])
        ids = [pl.program_id(k) for k in range(len(grid))]
        inner_first = functools.reduce(jnp.logical_and, [i == 0 for i in ids[1:]], True)

        @pl.when((ids[0] == 0) & inner_first)
        def _():
            side.phases[0](*side_refs)

        body(*ins[:n_in], *outs[:n_out], *scratch[:n_scr])

        @pl.when((ids[0] == grid[0] // 2) & inner_first)
        def _():
            side.phases[1](*side_refs)

        @pl.when(functools.reduce(jnp.logical_and, [i == n - 1 for i, n in zip(ids, grid)]))
        def _():
            side.phases[2](*side_refs)

    any_spec = pl.BlockSpec(memory_space=pl.ANY)
    outs = pl.pallas_call(
        body_with_side, name=name, grid=grid, in_specs=list(in_specs) + [any_spec] * n_side,
        out_specs=list(out_specs) + [any_spec] * n_side, out_shape=list(out_shape) + list(side.out_shape),
        scratch_shapes=list(scratch_shapes) + list(side.scratch_shapes),
        compiler_params=_cparams(*["arbitrary"] * len(grid)))(*args, *side.arrays)
    return outs[:n_out], outs[n_out:]


def _ssd_fwd(pre, dt_raw, z, dtb, alog, dskip_w, norm_w, side=None):
    n = pre.shape[0]
    nc = n // BLOCK
    xs_s, b_s, c_s, lane_s, vec_s, wide_s, hs_s = _ssd_specs(nc, False)

    def body(pxs_ref, pb_ref, pc_ref, dtr_ref, z_ref, dtb_ref, alog_ref, dskw_ref, nw_ref, e_ref,
             y_ref, yn_ref, hs_ref, h_scr):
        c = pl.program_id(1)

        @pl.when(c == 0)
        def _():
            h_scr[...] = jnp.zeros_like(h_scr)

        xs, bm, cm, _, _, dt, _, _, cs, causal = _ssd_prep(pxs_ref, pb_ref, pc_ref, dtr_ref, dtb_ref, alog_ref, c)
        cst = cs.T
        cs_last = cs[BLOCK - 1:BLOCK, :]
        dt_w, ecs_w, dec_w = _spread_heads([dt, jnp.exp(cs), jnp.exp(cs_last - cs)], e_ref)
        xdt = xs * dt_w
        bmb = bm.astype(BF16)
        cmb = cm.astype(BF16)
        cb = _dot_nt(cmb, bmb)
        hg = h_scr[...]
        hs_ref[0, 0] = hg
        y = _dot_nt(cmb, hg.astype(BF16)) * ecs_w + dskw_ref[...] * xs
        first = lax.broadcasted_iota(jnp.int32, (BLOCK, LANES), 1) < SSD_HEADDIM
        diag = []
        for j in range(SSD_HPG // 2):
            xp = xdt[:, LANES * j:LANES * (j + 1)].astype(BF16)
            res = []
            for r in (2 * j, 2 * j + 1):
                lm = jnp.exp(jnp.where(causal, cs[:, r:r + 1] - cst[r:r + 1, :], NEG))
                res.append(_dot((cb * lm).astype(BF16), xp))
            diag.append(jnp.where(first, res[0], res[1]))
        y = y + jnp.concatenate(diag, axis=1)
        st = _dot_tn((xdt * dec_w).astype(BF16), bmb)
        eh = jnp.exp(cs_last)
        for r in range(SSD_HPG):
            rows = _head_cols(r)
            h_scr[rows, :] = hg[rows, :] * eh[:, r:r + 1] + st[rows, :]
        y_ref[...] = y
        gts = y * _silu(z_ref[...])
        rr = lax.rsqrt(jnp.mean(gts * gts, axis=-1, keepdims=True) + EPS)
        yn_ref[...] = (gts * rr * nw_ref[...]).astype(BF16)

    return _call_with_side(
        body, side, name="ssd_fwd", grid=(SSD_GROUPS, nc),
        in_specs=[xs_s, b_s, c_s, lane_s, xs_s, vec_s, vec_s, wide_s, wide_s, _const_spec((LANES, SSD_GW))],
        out_specs=[xs_s, xs_s, hs_s],
        out_shape=[jax.ShapeDtypeStruct((n, SSD_INNER), F32), jax.ShapeDtypeStruct((n, SSD_INNER), BF16),
                   jax.ShapeDtypeStruct((nc, SSD_GROUPS, SSD_GW, SSD_STATE), F32)],
        scratch_shapes=[pltpu.VMEM((SSD_GW, SSD_STATE), F32)],
        args=(pre, pre, pre, dt_raw, z, dtb, alog, dskip_w, norm_w, _head_spread_matrix()))


def _lane_put(acc, col, r):
    lane = lax.broadcasted_iota(jnp.int32, acc.shape, 1)
    return jnp.where(lane == r, col, acc)


def _ssd_bwd(dyn, y, z, pre, dt_raw, hsave, dtb, alog, dskip_w, norm_w, side=None):
    n = pre.shape[0]
    nc = n // BLOCK
    spread = _head_spread_matrix()
    xs_s, b_s, c_s, lane_s, vec_s, wide_s, hs_s = _ssd_specs(nc, True)
    bc_out =pl.BlockSpec((BLOCK, SSD_STATE), lambda g, c: (nc - 1 - c, g))

    def body(dyn_ref, y_ref, z_ref, pxs_ref, pb_ref, pc_ref, dtr_ref, hs_ref, dtb_ref, alog_ref, dskw_ref, nw_ref,
             e_ref, r_ref,
             dz_ref, dxs_ref, dbm_ref, dcm_ref, ddt_ref, dnw_ref, ddtb_ref, dalog_ref, ddsk_ref, g_scr):
        step = pl.program_id(1)
        c = nc - 1 - step

        @pl.when(step == 0)
        def _():
            g_scr[...] = jnp.zeros_like(g_scr)

        pxs, pb, pc = pxs_ref[...], pb_ref[...], pc_ref[...]
        sx, sb, sc = _sigmoid(pxs), _sigmoid(pb), _sigmoid(pc)
        xs, bm, cm = pxs * sx, pb * sb, pc * sc
        valid, dtr, dt, a, lam, cs, causal = _ssd_decay(dtr_ref, dtb_ref, alog_ref, c)
        cst = cs.T
        cs_last = cs[BLOCK - 1:BLOCK, :]
        bmb = bm.astype(BF16)
        cmb = cm.astype(BF16)
        cb = _dot_nt(cmb, bmb)
        hg = hs_ref[0, 0]
        hgb = hg.astype(BF16)
        yoff = _dot_nt(cmb, hgb)
        gn = g_scr[...]
        gnb = gn.astype(BF16)

        zv = z_ref[...]
        yv = y_ref[...]
        sgz = _sigmoid(zv)
        sz = zv * sgz
        gts = yv * sz
        rr = lax.rsqrt(jnp.mean(gts * gts, axis=-1, keepdims=True) + EPS)
        xh = gts * rr
        dynv = dyn_ref[...]
        gg = dynv * nw_ref[...]
        dgts = rr * (gg - xh * jnp.mean(gg * xh, axis=-1, keepdims=True))
        dnw = jnp.sum(dynv * xh, axis=0, keepdims=True)
        dy = dgts * sz
        dz_ref[...] = (dgts * yv * (sgz * (1.0 + zv * (1.0 - sgz)))).astype(BF16)

        ecs = jnp.exp(cs)
        dec = jnp.exp(cs_last - cs)
        eh = jnp.exp(cs_last)
        dt_w, ecs_w, dec_w = _spread_heads([dt, ecs, dec], e_ref)
        red_m = r_ref[...]

        def head_sums(v):
            return _dot_sel(v, red_m, terms=2)

        xdt = xs * dt_w
        q_all = _dot_nt(bmb, gnb)
        w_all = (dy * ecs_w).astype(BF16)
        e_hl = head_sums(q_all * xdt) * dec
        dcs_col = head_sums(dy * yoff) * ecs - e_hl
        gh = jnp.zeros((1, LANES), F32)
        prod = gn * hg
        for r in range(SSD_HPG):
            gh = _lane_put(gh, _sum_all(prod[_head_cols(r), :]), r)
        dcs_last = jnp.sum(e_hl, axis=0, keepdims=True) + eh * gh
        ddsk = jnp.sum(head_sums(dy * xs), axis=0, keepdims=True)
        cbt = _dot_nt(bmb, cmb)
        lane = lax.broadcasted_iota(jnp.int32, (BLOCK, LANES), 1)
        first = lane < SSD_HEADDIM
        causal_t = lax.broadcasted_iota(jnp.int32, (BLOCK, BLOCK), 1) >= lax.broadcasted_iota(
            jnp.int32, (BLOCK, BLOCK), 0)
        sub = lax.broadcasted_iota(jnp.int32, (SUBLANES, BLOCK), 0)
        dcs_row = jnp.zeros((SUBLANES, BLOCK), F32)
        dcb = jnp.zeros((BLOCK, BLOCK), F32)
        dxdt_pairs = []
        for j in range(SSD_HPG // 2):
            tile = slice(LANES * j, LANES * (j + 1))
            dy_p = dy[:, tile]
            dyb = dy_p.astype(BF16)
            xdtb = xdt[:, tile].astype(BF16)
            res = []
            for half, r in enumerate((2 * j, 2 * j + 1)):
                csc, csr = cs[:, r:r + 1], cst[r:r + 1, :]
                lm = jnp.exp(jnp.where(causal, csc - csr, NEG))
                lmt = jnp.exp(jnp.where(causal_t, csr - csc, NEG))
                keep = first if half == 0 else jnp.logical_not(first)
                gm = _dot_nt(jnp.where(keep, dy_p, 0.0).astype(BF16), xdtb) * lm
                dcb = dcb + gm
                mm_ = gm * cb
                dcs_col = dcs_col + jnp.where(lane == r, jnp.sum(mm_, axis=1, keepdims=True), 0.0)
                dcs_row = jnp.where(sub == r, jnp.sum(mm_, axis=0, keepdims=True), dcs_row)
                res.append(_dot((cbt * lmt).astype(BF16), dyb))
            dxdt_pairs.append(jnp.where(first, res[0], res[1]))
        dxdt = jnp.concatenate(dxdt_pairs, axis=1) + q_all * dec_w
        ddt_x = head_sums(dxdt * xs)
        dxs = dxdt * dt_w + dskw_ref[...] * dy
        dcbb = dcb.astype(BF16)
        dcm = _dot(w_all, hgb) + _dot(dcbb, bmb)
        dbm = _dot((xdt * dec_w).astype(BF16), gnb) + _dot_tn(dcbb, cmb)
        dh_off = _dot_tn(w_all, cmb)
        for r in range(SSD_HPG):
            rows = _head_cols(r)
            g_scr[rows, :] = gn[rows, :] * eh[:, r:r + 1] + dh_off[rows, :]

        pad_rows = jnp.zeros((BLOCK - SUBLANES, BLOCK), F32)
        dcs = dcs_col - jnp.concatenate([dcs_row, pad_rows], axis=0).T
        rsel = lax.broadcasted_iota(jnp.int32, (BLOCK, LANES), 0)
        dcs = dcs + jnp.where(rsel == BLOCK - 1, dcs_last, 0.0)
        ri = lax.broadcasted_iota(jnp.int32, (BLOCK, BLOCK), 0)
        ci = lax.broadcasted_iota(jnp.int32, (BLOCK, BLOCK), 1)
        dlam = _sel_dot((ci >= ri).astype(BF16), dcs)
        head = lane < SSD_HPG
        ddt = dlam * a + ddt_x
        ddtr = jnp.where(head, ddt * _sigmoid(dtr) * valid, 0.0)
        ddt_ref[...] = ddtr.astype(BF16)
        dalog = jnp.sum(jnp.where(head, dlam * lam, 0.0), axis=0, keepdims=True)
        ddtb = jnp.sum(ddtr, axis=0, keepdims=True)

        dxs_ref[...] = dxs * (sx * (1.0 + pxs * (1.0 - sx)))
        dbm_ref[...] = dbm * (sb * (1.0 + pb * (1.0 - sb)))
        dcm_ref[...] = dcm * (sc * (1.0 + pc * (1.0 - sc)))

        @pl.when(step == 0)
        def _():
            dnw_ref[...] = dnw
            ddtb_ref[...] = ddtb
            dalog_ref[...] = dalog
            ddsk_ref[...] = ddsk

        @pl.when(step > 0)
        def _():
            dnw_ref[...] += dnw
            ddtb_ref[...] += ddtb
            dalog_ref[...] += dalog
            ddsk_ref[...] += ddsk

    return _call_with_side(
        body, side, name="ssd_bwd", grid=(SSD_GROUPS, nc),
        in_specs=[xs_s, xs_s, xs_s, xs_s, b_s, c_s, lane_s, hs_s, vec_s, vec_s, wide_s, wide_s,
                  _const_spec((LANES, SSD_GW)), _const_spec((SSD_GW, LANES))],
        out_specs=[xs_s, xs_s, bc_out, bc_out, lane_s, wide_s, vec_s, vec_s, vec_s],
        out_shape=[jax.ShapeDtypeStruct((n, SSD_INNER), BF16), jax.ShapeDtypeStruct((n, SSD_INNER), F32),
                   jax.ShapeDtypeStruct((n, SSD_BC), F32), jax.ShapeDtypeStruct((n, SSD_BC), F32),
                   jax.ShapeDtypeStruct((n, DT_W), BF16), jax.ShapeDtypeStruct((1, SSD_INNER), F32),
                   jax.ShapeDtypeStruct((1, DT_W), F32), jax.ShapeDtypeStruct((1, DT_W), F32),
                   jax.ShapeDtypeStruct((1, DT_W), F32)],
        scratch_shapes=[pltpu.VMEM((SSD_GW, SSD_STATE), F32)],
        args=(dyn, y, z, pre, pre, pre, dt_raw, hsave, dtb, alog, dskip_w, norm_w, spread, spread.T))


def _bucket_table():
    def bucket(dist):
        d = np.maximum(dist, 0)
        half = REL_BUCKETS // 2
        big = half + (np.log(np.maximum(d, half).astype(np.float32) / np.float32(half))
                      / np.float32(math.log(REL_MAX_DIST / half)) * np.float32(REL_BUCKETS - half)).astype(np.int32)
        return np.where(d < half, d, np.minimum(big, REL_BUCKETS - 1)).astype(np.int32)

    l = np.arange(BLOCK)[None, :]
    band = bucket(l + BLOCK - np.arange(2 * BLOCK)[:, None])
    j = np.arange(BLOCK)[:, None]
    tables = [np.concatenate([bucket(v * BLOCK + l - j), band], axis=0) for v in range(3)]
    return np.concatenate([t.reshape(-1) for t in tables])


def _onehot_t():
    buckets = jnp.asarray(_bucket_table())
    return (buckets[None, :] == jnp.arange(REL_BUCKETS, dtype=jnp.int32)[:, None]).astype(F32)


def _bias_tables(rel_t, onehot_t, side=None):
    def body(r_ref, oh_ref, o_ref):
        o_ref[...] = jnp.dot(r_ref[...], oh_ref[...], precision=HIGHEST, preferred_element_type=F32)

    outs, carried = _call_with_side(
        body, side, name="bias_tables", grid=(NT_ALL // NT_TILE,),
        in_specs=[pl.BlockSpec((ATT_HEADS, REL_BUCKETS), lambda i: (0, 0)),
                  pl.BlockSpec((REL_BUCKETS, NT_TILE), lambda i: (0, i))],
        out_specs=[pl.BlockSpec((ATT_HEADS, NT_TILE), lambda i: (0, i))],
        out_shape=[jax.ShapeDtypeStruct((ATT_HEADS, NT_ALL), F32)], scratch_shapes=[], args=(rel_t, onehot_t),
        semantics=("parallel",))
    return outs[0], carried


def _bias_grad(dtab, onehot_t):
    def body(d_ref, oh_ref, o_ref):
        i = pl.program_id(0)
        p = lax.dot_general(d_ref[...], oh_ref[...], (((1,), (1,)), ((), ())), precision=HIGHEST,
                            preferred_element_type=F32)

        @pl.when(i == 0)
        def _():
            o_ref[...] = p

        @pl.when(i > 0)
        def _():
            o_ref[...] += p

    return pl.pallas_call(
        body, name="bias_grad", grid=(NT_ALL // NT_TILE,),
        in_specs=[pl.BlockSpec((ATT_HEADS, NT_TILE), lambda i: (0, i)),
                  pl.BlockSpec((REL_BUCKETS, NT_TILE), lambda i: (0, i))],
        out_specs=pl.BlockSpec((ATT_HEADS, REL_BUCKETS), lambda i: (0, 0)),
        out_shape=jax.ShapeDtypeStruct((ATT_HEADS, REL_BUCKETS), F32),
        compiler_params=_cparams("arbitrary"))(dtab, onehot_t)


def _att_mask_t(n, copies):
    far = 4 * BLOCK
    kk = lax.broadcasted_iota(jnp.int32, (N_KEYS, copies * BLOCK), 0)
    li = lax.broadcasted_iota(jnp.int32, (N_KEYS, copies * BLOCK), 1) & (BLOCK - 1)
    meta_ok = (kk >= PAD) & (kk < BLOCK) & (li + jnp.where(n >= 1, far, 0) >= kk)
    prev_ok = (kk >= BLOCK) & (kk < 2 * BLOCK) & (kk - BLOCK > li + jnp.where(n >= 2, 0, far))
    cur_ok = (kk >= 2 * BLOCK) & (kk - 2 * BLOCK <= li - jnp.where(n >= 1, 0, far))
    return meta_ok | prev_ok | cur_ok


def _att_kv(meta_ref, prev_ref, cur_ref):
    kv = jnp.concatenate([meta_ref[...], prev_ref[...], cur_ref[...]], axis=0)
    first = lax.broadcasted_iota(jnp.int32, (N_KEYS, LANES), 1) < ATT_HEADDIM
    out = []
    for pair in (kv[:, :LANES], kv[:, LANES:]):
        swapped = pltpu.roll(pair, ATT_HEADDIM, 1)
        out.append([jnp.where(first, pair, swapped).astype(BF16), jnp.where(first, swapped, pair).astype(BF16)])
    return out[0], out[1]


def _split_heads(x_pair, first):
    return jnp.concatenate([jnp.where(first, x_pair, 0.0), jnp.where(first, 0.0, x_pair)], axis=0).astype(BF16)


def _att_probs_t(qm2, k_dup, t_ref, j, mask2, sink_ref):
    scale = ATT_HEADDIM ** -0.5
    bias2 = jnp.concatenate([t_ref[0, 2 * j], t_ref[0, 2 * j + 1]], axis=1)
    second = lax.broadcasted_iota(jnp.int32, (1, 2 * BLOCK), 1) >= BLOCK
    sink2 = jnp.where(second, sink_ref[0:1, 2 * j + 1:2 * j + 2], sink_ref[0:1, 2 * j:2 * j + 1])
    s_t = jnp.where(mask2, _dot_nt(k_dup, qm2) * scale + bias2, NEG)
    mx = jnp.maximum(jnp.max(s_t, axis=0, keepdims=True), sink2)
    p_t = jnp.exp(s_t - mx)
    p_s = jnp.exp(sink2 - mx)
    inv = 1.0 / (jnp.sum(p_t, axis=0, keepdims=True) + p_s)
    return p_t * inv, p_s * inv


def _att_specs(nb, rev):
    def nidx(i):
        return nb - 1 - i if rev else i

    kvb = ATT_Q // (2 * ATT_KV)
    q_s = pl.BlockSpec((BLOCK, ATT_Q), lambda i: (nidx(i), 0))
    cur = pl.BlockSpec((BLOCK, 2 * ATT_KV), lambda i: (nidx(i), kvb))
    prev = pl.BlockSpec((BLOCK, 2 * ATT_KV), lambda i: (jnp.maximum(nidx(i) - 1, 0), kvb))
    meta = pl.BlockSpec((BLOCK, 2 * ATT_KV), lambda i: (0, kvb))
    table = pl.BlockSpec((1, ATT_HEADS, N_KEYS, BLOCK), lambda i: (jnp.minimum(nidx(i), 2), 0, 0, 0))
    sink = pl.BlockSpec((1, LANES), lambda i: (0, 0))
    return q_s, cur, prev, meta, table, sink


def _attn_fwd(qkv, tables, sinks):
    n = qkv.shape[0]
    nb = n // BLOCK
    q_s, cur_s, prev_s, meta_s, t_s, sink_s = _att_specs(nb, False)

    def body(q_ref, cur_ref, prev_ref, meta_ref, t_ref, sink_ref, o_ref):
        blk = pl.program_id(0)
        mask_t = _att_mask_t(blk, 1)
        k_dup, v_dup = _att_kv(meta_ref, prev_ref, cur_ref)
        v_dup_t = [v.T for v in v_dup]
        first = lax.broadcasted_iota(jnp.int32, (BLOCK, LANES), 1) < ATT_HEADDIM
        top = lax.broadcasted_iota(jnp.int32, (LANES, BLOCK), 0) < ATT_HEADDIM
        scale = ATT_HEADDIM ** -0.5
        for j in range(ATT_HEADS // 2):
            kh = 2 * j // ATT_GQ
            tile = slice(LANES * j, LANES * (j + 1))
            q_p = q_ref[:, tile]
            res = []
            for half, h in enumerate((2 * j, 2 * j + 1)):
                qm = jnp.where(first if half == 0 else jnp.logical_not(first), q_p, 0.0).astype(BF16)
                sink = sink_ref[0:1, h:h + 1]
                s_t = jnp.where(mask_t, _dot_nt(k_dup[kh], qm) * scale + t_ref[0, h], NEG)
                mx = jnp.maximum(jnp.max(s_t, axis=0, keepdims=True), sink)
                p_t = jnp.exp(s_t - mx)
                inv = 1.0 / (jnp.sum(p_t, axis=0, keepdims=True) + jnp.exp(sink - mx))
                res.append(_dot(v_dup_t[kh], (p_t * inv).astype(BF16)))
            o_ref[:, tile] = jnp.where(top, res[0], res[1]).T.astype(BF16)

    return pl.pallas_call(
        body, name="attn_fwd", grid=(nb,),
        in_specs=[q_s, cur_s, prev_s, meta_s, t_s, sink_s],
        out_specs=q_s,
        out_shape=jax.ShapeDtypeStruct((n, ATT_Q), BF16),
        compiler_params=_cparams("parallel"))(qkv, qkv, qkv, qkv, tables, sinks)


def _attn_bwd(datt, qkv, tables, sinks):
    n = qkv.shape[0]
    nb = n // BLOCK
    q_s, cur_s, prev_s, meta_s, t_s, sink_s = _att_specs(nb, True)
    dqkv_s = pl.BlockSpec((BLOCK, ATT_Q + 2 * ATT_KV), lambda i: (nb - 1 - i, 0))
    scale = ATT_HEADDIM ** -0.5

    def body(do_ref, q_ref, cur_ref, prev_ref, meta_ref, t_ref, sink_ref,
             dqkv_ref, dt_ref, dsink_ref, carry_scr, meta_scr):
        step = pl.program_id(0)
        blk = nb - 1 - step
        mask2 = _att_mask_t(blk, 2)
        k_dup, v_dup = _att_kv(meta_ref, prev_ref, cur_ref)
        k_dup_t = [k.T for k in k_dup]

        @pl.when(step == 0)
        def _():
            carry_scr[...] = jnp.zeros_like(carry_scr)
            meta_scr[...] = jnp.zeros_like(meta_scr)
            dsink_ref[...] = jnp.zeros_like(dsink_ref)

        @pl.when((step == 0) | (blk <= 1))
        def _():
            dt_ref[...] = jnp.zeros_like(dt_ref)

        first = lax.broadcasted_iota(jnp.int32, (BLOCK, LANES), 1) < ATT_HEADDIM
        top = lax.broadcasted_iota(jnp.int32, (LANES, BLOCK), 0) < ATT_HEADDIM
        first_k = lax.broadcasted_iota(jnp.int32, (N_KEYS, LANES), 1) < ATT_HEADDIM
        dsink = jnp.zeros((1, LANES), F32)
        dk_acc = [None] * ATT_KV_HEADS
        dv_acc = [None] * ATT_KV_HEADS
        for j in range(ATT_HEADS // 2):
            kh = 2 * j // ATT_GQ
            tile = slice(LANES * j, LANES * (j + 1))
            qm2 = _split_heads(q_ref[:, tile], first)
            dom2 = _split_heads(do_ref[:, tile], first)
            p_t, p_s = _att_probs_t(qm2, k_dup[kh], t_ref, j, mask2, sink_ref)
            dp_t = _dot_nt(v_dup[kh], dom2)
            delta = jnp.sum(p_t * dp_t, axis=0, keepdims=True)
            ds_t = p_t * (dp_t - delta)
            sink_terms = p_s * delta
            for half in range(2):
                cols = slice(BLOCK * half, BLOCK * (half + 1))
                dsink = _lane_put(dsink, -jnp.sum(sink_terms[:, cols], axis=1, keepdims=True), 2 * j + half)
                dt_ref[0, 2 * j + half] += ds_t[:, cols]
            ds_tb = ds_t.astype(BF16)
            dq_t = _dot(k_dup_t[kh], ds_tb)
            dqkv_ref[:, tile] = (jnp.where(top, dq_t[:, :BLOCK], dq_t[:, BLOCK:]).T * scale).astype(BF16)
            dk_part, dv_part = _dot(ds_tb, qm2), _dot(p_t.astype(BF16), dom2)
            dk_acc[kh] = dk_part if dk_acc[kh] is None else dk_acc[kh] + dk_part
            dv_acc[kh] = dv_part if dv_acc[kh] is None else dv_acc[kh] + dv_part
        dsink_ref[...] += dsink
        folded = [a + pltpu.roll(a, ATT_HEADDIM, 1) for a in dk_acc + dv_acc]
        dkv = jnp.concatenate([jnp.where(first_k, folded[0], folded[1]) * scale,
                               jnp.where(first_k, folded[2], folded[3])], axis=1)
        meta_scr[...] += dkv[:BLOCK, :]
        own = dkv[2 * BLOCK:, :] + carry_scr[...]
        carry_scr[...] = dkv[BLOCK:2 * BLOCK, :]

        @pl.when(blk > 0)
        def _():
            dqkv_ref[:, ATT_Q:] = own.astype(BF16)

        @pl.when(blk == 0)
        def _():
            dqkv_ref[:, ATT_Q:] = (own + meta_scr[...]).astype(BF16)

    return pl.pallas_call(
        body, name="attn_bwd", grid=(nb,),
        in_specs=[q_s, q_s, cur_s, prev_s, meta_s, t_s, sink_s],
        out_specs=[dqkv_s, t_s, sink_s],
        out_shape=[jax.ShapeDtypeStruct((n, ATT_Q + 2 * ATT_KV), BF16),
                   jax.ShapeDtypeStruct((3, ATT_HEADS, N_KEYS, BLOCK), F32),
                   jax.ShapeDtypeStruct((1, LANES), F32)],
        scratch_shapes=[pltpu.VMEM((BLOCK, 2 * ATT_KV), F32), pltpu.VMEM((BLOCK, 2 * ATT_KV), F32)],
        compiler_params=_cparams("arbitrary"))(datt, qkv, qkv, qkv, qkv, tables, sinks)


def _merge_out_fwd(gates, y_ssd, y_att, gate_b, w_out, h):
    n = gates.shape[0]
    tm = _row_tile(n, 416)

    def body(gs_ref, ga_ref, ys_ref, ya_ref, gb_ref, w_ref, h_ref, m_ref, o_ref):
        merged = (_sigmoid(gs_ref[...] + gb_ref[0:1, :]) * ys_ref[...]
                  + _sigmoid(ga_ref[...] + gb_ref[1:2, :]) * ya_ref[...]).astype(BF16)
        m_ref[...] = merged
        row = pl.program_id(0) * tm + lax.broadcasted_iota(jnp.int32, (tm, 1), 0)
        o_ref[...] = jnp.where(row >= PAD, _dot(merged, w_ref[...]), 0.0) + h_ref[...]

    row = pl.BlockSpec((tm, D_MODEL), lambda i: (i, 0))
    return pl.pallas_call(
        body, name="merge_out_fwd", grid=(n // tm,),
        in_specs=[row, pl.BlockSpec((tm, D_MODEL), lambda i: (i, 1)), row, row,
                  pl.BlockSpec((2, D_MODEL), lambda i: (0, 0)), pl.BlockSpec((D_MODEL, D_MODEL), lambda i: (0, 0)), row],
        out_specs=[row, row],
        out_shape=[jax.ShapeDtypeStruct((n, D_MODEL), BF16), jax.ShapeDtypeStruct((n, D_MODEL), F32)],
        compiler_params=_cparams("parallel"))(gates, gates, y_ssd, y_att, gate_b, w_out, h)


def _merge_out_bwd(dh, w_out, gates, y_ssd, y_att, gate_b):
    n = gates.shape[0]
    tm = _row_tile(n, 416)

    def body(dh_ref, w_ref, gs_ref, ga_ref, ys_ref, ya_ref, gb_ref, dys_ref, dya_ref, dg_ref, dgb_ref):
        i = pl.program_id(0)
        row = i * tm + lax.broadcasted_iota(jnp.int32, (tm, 1), 0)
        dmv = jnp.where(row >= PAD, _dot_nt(dh_ref[...].astype(BF16), w_ref[...]), 0.0)
        ss =_sigmoid(gs_ref[...] + gb_ref[0:1, :])
        sa = _sigmoid(ga_ref[...] + gb_ref[1:2, :])
        dys_ref[...] = (dmv * ss).astype(BF16)
        dya_ref[...] = (dmv * sa).astype(BF16)
        dgs = dmv * ys_ref[...] * ss * (1.0 - ss)
        dga = dmv * ya_ref[...] * sa * (1.0 - sa)
        dg_ref[:, :D_MODEL] = dgs.astype(BF16)
        dg_ref[:, D_MODEL:] = dga.astype(BF16)
        part = jnp.concatenate([jnp.sum(dgs, axis=0, keepdims=True), jnp.sum(dga, axis=0, keepdims=True)], axis=0)

        @pl.when(i == 0)
        def _():
            dgb_ref[...] = part

        @pl.when(i > 0)
        def _():
            dgb_ref[...] += part

    row = pl.BlockSpec((tm, D_MODEL), lambda i: (i, 0))
    gb = pl.BlockSpec((2, D_MODEL), lambda i: (0, 0))
    return pl.pallas_call(
        body, name="merge_out_bwd", grid=(n // tm,),
        in_specs=[row, pl.BlockSpec((D_MODEL, D_MODEL), lambda i: (0, 0)), row,
                  pl.BlockSpec((tm, D_MODEL), lambda i: (i, 1)), row, row, gb],
        out_specs=[row, row, pl.BlockSpec((tm, 2 * D_MODEL), lambda i: (i, 0)), gb],
        out_shape=[jax.ShapeDtypeStruct((n, D_MODEL), BF16), jax.ShapeDtypeStruct((n, D_MODEL), BF16),
                   jax.ShapeDtypeStruct((n, 2 * D_MODEL), BF16), jax.ShapeDtypeStruct((2, D_MODEL), F32)],
        compiler_params=_cparams("arbitrary"))(dh, w_out, gates, gates, y_ssd, y_att, gate_b)


def _col_move(srcs, outs, pieces, *, name):
    rows = srcs[0].shape[-2]
    tr = _row_tile(rows, 128)
    n_src = len(srcs)
    covered = [sum(p[6] for p in pieces if p[0] == o) for o in range(len(outs))]
    total = [int(np.prod(shp)) // rows for shp, _ in outs]

    def body(*refs):
        in_refs, out_refs = refs[:n_src], refs[n_src:]
        for o, ref in enumerate(out_refs):
            if covered[o] != total[o]:
                ref[...] = jnp.zeros_like(ref)
        for o, ol, oc, s, sl, sc, width in pieces:
            val = in_refs[s][:, sc:sc + width] if sl is None else in_refs[s][sl, :, sc:sc + width]
            val = val.astype(outs[o][1])
            if ol is None:
                out_refs[o][:, oc:oc + width] = val
            else:
                out_refs[o][ol, :, oc:oc + width] = val

    def spec(shape):
        if len(shape) == 2:
            return pl.BlockSpec((tr, shape[1]), lambda i: (i, 0))
        return pl.BlockSpec((shape[0], tr, shape[2]), lambda i: (0, i, 0))

    return pl.pallas_call(
        body, name=name, grid=(rows // tr,),
        in_specs=[spec(a.shape) for a in srcs], out_specs=[spec(shp) for shp, _ in outs],
        out_shape=[jax.ShapeDtypeStruct(shp, dt) for shp, dt in outs],
        compiler_params=_cparams("parallel"))(*srcs)


def _shard_pieces(seg_ranges, shard_w):
    out = []
    for seg, runs in enumerate(seg_ranges):
        for g0, width, s0 in runs:
            done = 0
            while done < width:
                dev, col = divmod(g0 + done, shard_w)
                take = min(width - done, shard_w - col)
                out.append((seg, s0 + done, dev, col, take))
                done += take
    return out


_CHIP_RELATIONS = [(1, 0, 0), (0, 1, 0), (1, 1, 0)]
N_CHIPS = 4


class _CommPlan:
    def __init__(self, arrays, out_shape, scratch_shapes, phases):
        self.arrays, self.out_shape, self.scratch_shapes, self.phases = arrays, out_shape, scratch_shapes, phases


def _gather_plan(arrays):
    n_arr = len(arrays)
    n_chips = len(_CHIP_RELATIONS)
    n_pair = 1 + 2 * n_chips

    def where():
        x, y, c = lax.axis_index("x"), lax.axis_index("y"), lax.axis_index("c")
        return x, y, c, (x, y, 1 - c), [(x ^ dx, y ^ dy) for dx, dy, _ in _CHIP_RELATIONS]

    def copy(outs, sems, a, k, block, to, src=None):
        slot = outs[a].at[2 * block[0] + block[1], block[2]]
        return pltpu.make_async_remote_copy(
            src_ref=slot if src is None else src, dst_ref=slot, send_sem=sems[0].at[a * n_pair + k],
            recv_sem=sems[1].at[a * n_pair + k], device_id=to, device_id_type=MESH)

    def mine(ins, outs, sems, a, x, y, c):
        return pltpu.make_async_copy(ins[a], outs[a].at[2 * x + y, c], sems[2].at[a])

    def first_copies(ins, outs, sems, a, x, y, c, sibling, chips):
        return ([copy(outs, sems, a, 0, (x, y, c), sibling, src=ins[a])]
                + [copy(outs, sems, a, 1 + j, (x, y, c), (*chip, c), src=ins[a]) for j, chip in enumerate(chips)])

    def start(ins, outs, sems):
        x, y, c, sibling, chips = where()
        for a in range(n_arr):
            mine(ins, outs, sems, a, x, y, c).start()
            for cp in first_copies(ins, outs, sems, a, x, y, c, sibling, chips):
                cp.start()

    def pass_on(ins, outs, sems):
        x, y, c, sibling, chips = where()
        for j, chip in enumerate(chips):
            for a in range(n_arr):
                copy(outs, sems, a, 1 + j, (*chip, c), (x, y, c)).wait_recv()
                copy(outs, sems, a, 1 + n_chips + j, (*chip, c), sibling).start()

    def finish(ins, outs, sems):
        x, y, c, sibling, chips = where()
        for a in range(n_arr):
            copy(outs, sems, a, 0, (x, y, 1 - c), (x, y, c)).wait_recv()
            for j, chip in enumerate(chips):
                copy(outs, sems, a, 1 + n_chips + j, (*chip, 1 - c), (x, y, c)).wait_recv()
        for a in range(n_arr):
            for cp in first_copies(ins, outs, sems, a, x, y, c, sibling, chips):
                cp.wait_send()
            for j, chip in enumerate(chips):
                copy(outs, sems, a, 1 + n_chips + j, (*chip, c), sibling).wait_send()
            mine(ins, outs, sems, a, x, y, c).wait()

    return _CommPlan(
        arrays, [jax.ShapeDtypeStruct((N_CHIPS, 2) + a.shape, a.dtype) for a in arrays],
        [pltpu.SemaphoreType.DMA((n_arr * n_pair,)), pltpu.SemaphoreType.DMA((n_arr * n_pair,)),
         pltpu.SemaphoreType.DMA((n_arr,))],
        (start, pass_on, finish))


_ALL_RELATIONS = [(dx, dy, dc) for dx in (0, 1) for dy in (0, 1) for dc in (0, 1)][1:]


def _all_to_all_plan(arrays, scatter=None):
    n_arr = len(arrays)
    n_rel = len(_ALL_RELATIONS)
    scatter = scatter or [True] * n_arr

    def block(ins, a, p):
        return ins[a].at[p] if scatter[a] else ins[a]

    def local_copies(ins, outs, sems):
        me = 4 * lax.axis_index("x") + 2 * lax.axis_index("y") + lax.axis_index("c")
        return [pltpu.make_async_copy(block(ins, a, me), outs[a].at[me], sems[2].at[a]) for a in range(n_arr)]

    def remote_copies(ins, outs, sems, arrivals):
        x, y, c = lax.axis_index("x"), lax.axis_index("y"), lax.axis_index("c")
        me = 4 * x + 2 * y + c
        out = []
        for k, (dx, dy, dc) in enumerate(_ALL_RELATIONS):
            px, py, pc = x ^ dx, y ^ dy, c ^ dc
            peer = 4 * px + 2 * py + pc
            for a in range(n_arr):
                out.append(pltpu.make_async_remote_copy(
                    src_ref=block(ins, a, peer), dst_ref=outs[a].at[peer if arrivals else me],
                    send_sem=sems[0].at[a * n_rel + k], recv_sem=sems[1].at[a * n_rel + k],
                    device_id=(x, y, c) if arrivals else (px, py, pc), device_id_type=MESH))
        return out

    def start(ins, outs, sems):
        for cp in local_copies(ins, outs, sems) + remote_copies(ins, outs, sems, False):
            cp.start()

    def pass_on(ins, outs, sems):
        pass

    def finish(ins, outs, sems):
        for send in remote_copies(ins, outs, sems, False):
            send.wait_send()
        for arrival in remote_copies(ins, outs, sems, True):
            arrival.wait_recv()
        for cp in local_copies(ins, outs, sems):
            cp.wait()

    return _CommPlan(
        arrays, [jax.ShapeDtypeStruct(a.shape if s else (N_DEV,) + a.shape, a.dtype) for a, s in zip(arrays, scatter)],
        [pltpu.SemaphoreType.DMA((n_arr * n_rel,)), pltpu.SemaphoreType.DMA((n_arr * n_rel,)),
         pltpu.SemaphoreType.DMA((n_arr,))],
        (start, pass_on, finish))


def _adamw(w, gslots, m, v, *, name, side=None):
    rows, cols = w.shape
    n_slots = gslots.shape[0]
    tr = _row_tile(rows, 128) if rows % 16 == 0 else rows

    def body(w_ref, g_ref, m_ref, v_ref, go_ref, d_ref, mo_ref, vo_ref):
        g = g_ref[0].astype(F32)
        for s in range(1, n_slots):
            g = g + g_ref[s].astype(F32)
        mn = ADAM_B1 * m_ref[...] + (1.0 - ADAM_B1) * g
        vn = ADAM_B2 * v_ref[...] + (1.0 - ADAM_B2) * (g * g)
        go_ref[...] = g
        mo_ref[...] = mn
        vo_ref[...] = vn
        m_hat = mn / (1.0 - ADAM_B1 ** ADAM_STEP)
        v_hat = vn / (1.0 - ADAM_B2 ** ADAM_STEP)
        d_ref[...] = -ADAM_LR * (m_hat / (jnp.sqrt(v_hat) + ADAM_EPS) + ADAM_WD * w_ref[...])

    blk = pl.BlockSpec((tr, cols), lambda i: (i, 0))
    shp = jax.ShapeDtypeStruct((rows, cols), F32)
    outs, carried = _call_with_side(
        body, side, name=name, grid=(rows // tr,),
        in_specs=[blk, pl.BlockSpec((n_slots, tr, cols), lambda i: (0, i, 0)), blk, blk],
        out_specs=[blk] * 4, out_shape=[shp] * 4, scratch_shapes=[], args=(w, gslots, m, v), semantics=("parallel",))
    return outs if side is None else (outs, carried)


_BIG = ("w_in", "w_ssd_branch", "w_attn_branch", "w_out", "w_ffn_in", "w_ffn_out")
_SMALL_SHARDED = ("meta_tokens", "ssd_conv_w", "gate_b", "ffn_conv_w")
_SMALL_REPLICATED = ("norm_mix_w", "ssd_conv_b", "ssd_dt_bias", "ssd_a_log", "ssd_d", "ssd_norm_w", "attn_sinks",
                     "rel_bias", "norm_ffn_w", "ffn_conv_b", "norm_final_w")
_WEIGHTS = ("meta_tokens", "norm_mix_w", "w_in", "ssd_conv_w", "ssd_conv_b", "ssd_dt_bias", "ssd_a_log", "ssd_d",
            "ssd_norm_w", "w_ssd_branch", "w_attn_branch", "attn_sinks", "rel_bias", "gate_b", "w_out", "norm_ffn_w",
            "w_ffn_in", "ffn_conv_w", "ffn_conv_b", "w_ffn_out", "norm_final_w")
_ROW_SHARDED = ("w_ssd_branch", "w_attn_branch", "w_out", "w_ffn_out")
_COL_SHARDED = ("w_in", "w_ffn_in", "meta_tokens", "ssd_conv_w", "gate_b", "ffn_conv_w")
_IN_SEGS = (("z", SSD_INNER), ("xbc", SSD_XBC), ("dt", SSD_HEADS), ("qkv", ATT_Q + 2 * ATT_KV), ("g", 2 * D_MODEL))


def _pack_rows(flat_parts, width, row_mult):
    flat = jnp.concatenate([p.reshape(-1) for p in flat_parts])
    pad = (-flat.shape[0]) % (width * row_mult)
    if pad:
        flat = jnp.concatenate([flat, jnp.zeros((pad,), flat.dtype)])
    return flat.reshape(-1, width)


def _unpack(flat, shapes):
    out, off = [], 0
    for shp in shapes:
        size = int(np.prod(shp))
        out.append(flat[off:off + size].reshape(shp))
        off += size
    return out


def _gather_full(stack, name, shard_shape):
    if name in _COL_SHARDED:
        return jnp.transpose(stack, (1, 0, 2)).reshape(shard_shape[0], N_DEV * shard_shape[1])
    return stack.reshape(N_DEV * shard_shape[0], shard_shape[1])


_IN_SEG_W = {"z": SSD_INNER, "xbc": SSD_XBC, "dt": DT_W, "qkv": ATT_Q + 2 * ATT_KV, "g": 2 * D_MODEL}
_IN_SHARD_W = (SSD_INNER + SSD_XBC + SSD_HEADS + ATT_Q + 2 * ATT_KV + 2 * D_MODEL) // N_DEV
_FFN_SHARD_W = 2 * D_FF // N_DEV


def _in_seg_runs():
    runs, off = [], 0
    for nm, width in _IN_SEGS:
        if nm == "dt":
            runs.append([(off + SSD_HPG * g, SSD_HPG, LANES * g) for g in range(SSD_GROUPS)])
        else:
            runs.append([(off, width, 0)])
        off += width
    return runs


def _w_in_to_segments(stack):
    pieces = [(seg, None, scol, 0, dev, col, w) for seg, scol, dev, col, w in _shard_pieces(_in_seg_runs(), _IN_SHARD_W)]
    outs = [((D_MODEL, _IN_SEG_W[nm]), stack.dtype) for nm, _ in _IN_SEGS]
    return dict(zip([nm for nm, _ in _IN_SEGS], _col_move([stack], outs, pieces, name="w_in_segments")))


def _segments_to_w_in_shards(seg_grads):
    pieces = [(0, dev, col, seg, None, scol, w) for seg, scol, dev, col, w in _shard_pieces(_in_seg_runs(), _IN_SHARD_W)]
    return _col_move(seg_grads, [((N_DEV, D_MODEL, _IN_SHARD_W), seg_grads[0].dtype)], pieces, name="g_w_in_shards")[0]


def _ffn_in_from_shards(stack):
    pieces = [(0, None, scol, 0, dev, col, w)
              for _, scol, dev, col, w in _shard_pieces([[(0, 2 * D_FF, 0)]], _FFN_SHARD_W)]
    return _col_move([stack], [((D_MODEL, 2 * D_FF), stack.dtype)], pieces, name="w_ffn_in_full")[0]


def _ffn_in_to_shards(g_up, g_gate):
    pieces = [(0, dev, col, seg, None, scol, w)
              for seg, scol, dev, col, w in _shard_pieces([[(0, D_FF, 0)], [(D_FF, D_FF, 0)]], _FFN_SHARD_W)]
    return _col_move([g_up, g_gate], [((N_DEV, D_MODEL, _FFN_SHARD_W), g_up.dtype)], pieces, name="g_w_ffn_in_shards")[0]


def _dt_spread(w_dt):
    k = w_dt.shape[0]
    w4 = w_dt.reshape(k, SSD_GROUPS, SSD_HPG)
    return jnp.pad(w4, ((0, 0), (0, 0), (0, LANES - SSD_HPG))).reshape(k, DT_W)


def _dt_gather(w_wide):
    k = w_wide.shape[0]
    return w_wide.reshape(k, SSD_GROUPS, LANES)[:, :, :SSD_HPG].reshape(k, SSD_HEADS)


class _LateExchanges:
    def __init__(self, two_d, shape2):
        self.two_d, self.shape2 = two_d, shape2
        self.early_grads_received = None
        self.w_in_grads_received = None

    def row_pack(self, tree):
        return jnp.concatenate([tree[k] for k in _ROW_SHARDED], axis=0)

    def late_weights_plan(self):
        return _gather_plan([self.two_d["w_ffn_in"].astype(BF16), self.row_pack(self.two_d).astype(BF16)])

    def late_weights(self, gathered):
        w_ffn_in_all, rows_all = [g.reshape((N_DEV,) + g.shape[2:]) for g in gathered]
        out = {"w_ffn_in": _ffn_in_from_shards(w_ffn_in_all)}
        off = 0
        for k in _ROW_SHARDED:
            r = self.shape2[k][0]
            out[k] = rows_all[:, off:off + r].reshape(N_DEV * r, D_MODEL)
            off += r
        return out

    def early_grads_plan(self, grads):
        rows_send = jnp.concatenate([grads[k].reshape(N_DEV, self.shape2[k][0], D_MODEL) for k in _ROW_SHARDED], axis=1)
        return _all_to_all_plan([_ffn_in_to_shards(*grads["w_ffn_in"]), rows_send])

    def w_in_grads_plan(self, seg_grads):
        return _all_to_all_plan([_segments_to_w_in_shards(seg_grads)])


def _local_step(x, target, w, exchanges=None):
    h0 = jnp.concatenate([jnp.zeros((PAD, D_MODEL), F32), w["meta_tokens"], x], axis=0)
    segs = w["in_segs"]

    dtb = _dt_spread(w["ssd_dt_bias"])
    alog = _dt_spread(w["ssd_a_log"])
    dskip_w = jnp.repeat(w["ssd_d"], SSD_HEADDIM, axis=1)
    sinks = jnp.pad(w["attn_sinks"], ((0, 0), (0, LANES - ATT_HEADS)))
    onehot_t = w["onehot_t"] if "onehot_t" in w else _onehot_t()
    flat_tables = w["bias_tables"] if "bias_tables" in w else _bias_tables(w["rel_bias"].T, onehot_t)[0]
    tables = jnp.transpose(flat_tables.reshape(ATT_HEADS, 3, N_KEYS, BLOCK), (1, 0, 2, 3))

    u = _rms_fwd(h0, w["norm_mix_w"], name="rms_mix_fwd")
    z = _mm(u, segs["z"], name="in_z")
    xbc, pre = _mm_conv_fwd(u, segs["xbc"], w["ssd_conv_w"], w["ssd_conv_b"], name="in_xbc_conv_fwd")
    dt_raw = _mm(u, segs["dt"], name="in_dt")
    qkv = _mm(u, segs["qkv"], out_dtype=BF16, name="in_qkv")
    gates = _mm(u, segs["g"], name="in_g")
    (y, yn, hsave), gathered = _ssd_fwd(pre, dt_raw, z, dtb, alog, dskip_w, w["ssd_norm_w"],
                                        side=None if exchanges is None else exchanges.late_weights_plan())
    if exchanges is not None:
        w = {**w, **exchanges.late_weights(gathered)}
    w_ffn_up, w_ffn_gate = w["w_ffn_in"][:, :D_FF], w["w_ffn_in"][:, D_FF:]
    y_ssd = _mm(yn, w["w_ssd_branch"], out_dtype=BF16, name="ssd_out")
    att = _attn_fwd(qkv, tables, sinks)
    y_att = _mm(att, w["w_attn_branch"], out_dtype=BF16, name="att_out")
    merged, h1 = _merge_out_fwd(gates, y_ssd, y_att, w["gate_b"], w["w_out"], h0)
    u2 = _rms_fwd(h1, w["norm_ffn_w"], name="rms_ffn_fwd")
    x_up, x_gate, hid_up, hid_gate, act = _ffn_in_act_fwd(u2, w["w_ffn_in"], w["ffn_conv_w"], w["ffn_conv_b"])
    dh2, dh2_b, loss_row, g_norm_final = _ffn_out_loss(act, w["w_ffn_out"], h1, w["norm_final_w"], target)

    grads = {"norm_final_w": g_norm_final}
    grads["w_ffn_out"] = _mm(act, dh2_b, ta=True, mask=True, out_dtype=BF16, name="g_w_ffn_out")
    dx_up, dx_gate, dcw_up, dcw_gate, dcb_up, dcb_gate = _ffn_out_act_bwd(
        dh2_b, w["w_ffn_out"], hid_up, hid_gate, x_up, x_gate, w["ffn_conv_w"])
    grads["ffn_conv_w"] = jnp.concatenate([dcw_up, dcw_gate], axis=1)
    grads["ffn_conv_b"] = jnp.concatenate([dcb_up, dcb_gate], axis=1)
    (dh1, grads["norm_ffn_w"]), _ = _mm_rms_bwd([(dx_up, w_ffn_up), (dx_gate, w_ffn_gate)], h1, w["norm_ffn_w"], dh2,
                                                name="d_u2_rms_bwd")
    grads["w_ffn_in"] = (_mm(u2, dx_up, ta=True, out_dtype=BF16, name="g_w_ffn_up"),
                         _mm(u2, dx_gate, ta=True, out_dtype=BF16, name="g_w_ffn_gate"))

    grads["w_out"] = _mm(merged, dh1, ta=True, mask=True, out_dtype=BF16, name="g_w_out")
    dy_ssd, dy_att, dgates, grads["gate_b"] = _merge_out_bwd(dh1, w["w_out"], gates, y_ssd, y_att, w["gate_b"])
    dyn = _mm(dy_ssd, w["w_ssd_branch"], tb=True, name="d_yn")
    grads["w_ssd_branch"] = _mm(yn, dy_ssd, ta=True, out_dtype=BF16, name="g_w_ssd")
    datt = _mm(dy_att, w["w_attn_branch"], tb=True, out_dtype=BF16, name="d_att")
    grads["w_attn_branch"] = _mm(att, dy_att, ta=True, out_dtype=BF16, name="g_w_att")
    (dz, dpxs, dpb, dpc, ddt, grads["ssd_norm_w"], g_dtb, g_alog, g_dskip), received = _ssd_bwd(
        dyn, y, z, pre, dt_raw, hsave, dtb, alog, dskip_w, w["ssd_norm_w"],
        side=None if exchanges is None else exchanges.early_grads_plan(grads))
    if exchanges is not None:
        exchanges.early_grads_received = received
    grads["ssd_dt_bias"] = _dt_gather(g_dtb)
    grads["ssd_a_log"] = _dt_gather(g_alog)
    grads["ssd_d"] = _dt_gather(g_dskip)
    conv_g = _conv_bwd(dpxs, xbc, w["ssd_conv_w"], name="ssd_conv_bwd_x")
    conv_g = _conv_bwd(dpb, xbc, w["ssd_conv_w"], name="ssd_conv_bwd_b", col0=SSD_INNER, into=conv_g)
    dxbc, grads["ssd_conv_w"], grads["ssd_conv_b"] = _conv_bwd(
        dpc, xbc, w["ssd_conv_w"], name="ssd_conv_bwd_c", col0=SSD_INNER + SSD_BC, into=conv_g)
    dqkv, d_tables, d_sinks = _attn_bwd(datt, qkv, tables, sinks)
    grads["attn_sinks"] = d_sinks[:, :ATT_HEADS]
    dtab = jnp.transpose(d_tables, (1, 0, 2, 3)).reshape(ATT_HEADS, NT_ALL)
    grads["rel_bias"] = _bias_grad(dtab, onehot_t).T
    dsegs = {"z": dz, "xbc": dxbc, "dt": ddt, "qkv": dqkv, "g": dgates}
    grads["in_segs"] = [_mm(u, dsegs[nm], ta=True, out_dtype=BF16, name="g_w_in_" + nm) for nm, _ in _IN_SEGS]
    (dh0, grads["norm_mix_w"]), received = _mm_rms_bwd(
        [(dsegs[nm], segs[nm]) for nm, _ in _IN_SEGS], h0, w["norm_mix_w"], dh1, name="d_u_rms_bwd",
        side=None if exchanges is None else exchanges.w_in_grads_plan(grads["in_segs"]))
    if exchanges is not None:
        exchanges.w_in_grads_received = received[0]
    grads["meta_tokens"] = dh0[PAD:BLOCK]
    return loss_row[0, 0], dh0[BLOCK:], grads


def kernel(x, meta_tokens, norm_mix_w, w_in, ssd_conv_w, ssd_conv_b, ssd_dt_bias, ssd_a_log, ssd_d, ssd_norm_w, w_ssd_branch, w_attn_branch, attn_sinks, rel_bias, gate_b, w_out, norm_ffn_w, w_ffn_in, ffn_conv_w, ffn_conv_b, w_ffn_out, norm_final_w, loss_target, m_meta_tokens, m_norm_mix_w, m_w_in, m_ssd_conv_w, m_ssd_conv_b, m_ssd_dt_bias, m_ssd_a_log, m_ssd_d, m_ssd_norm_w, m_w_ssd_branch, m_w_attn_branch, m_attn_sinks, m_rel_bias, m_gate_b, m_w_out, m_norm_ffn_w, m_w_ffn_in, m_ffn_conv_w, m_ffn_conv_b, m_w_ffn_out, m_norm_final_w, v_meta_tokens, v_norm_mix_w, v_w_in, v_ssd_conv_w, v_ssd_conv_b, v_ssd_dt_bias, v_ssd_a_log, v_ssd_d, v_ssd_norm_w, v_w_ssd_branch, v_w_attn_branch, v_attn_sinks, v_rel_bias, v_gate_b, v_w_out, v_norm_ffn_w, v_w_ffn_in, v_ffn_conv_w, v_ffn_conv_b, v_w_ffn_out, v_norm_final_w):
    shard = dict(meta_tokens=meta_tokens, norm_mix_w=norm_mix_w, w_in=w_in, ssd_conv_w=ssd_conv_w,
                 ssd_conv_b=ssd_conv_b, ssd_dt_bias=ssd_dt_bias, ssd_a_log=ssd_a_log, ssd_d=ssd_d,
                 ssd_norm_w=ssd_norm_w, w_ssd_branch=w_ssd_branch, w_attn_branch=w_attn_branch,
                 attn_sinks=attn_sinks, rel_bias=rel_bias, gate_b=gate_b, w_out=w_out, norm_ffn_w=norm_ffn_w,
                 w_ffn_in=w_ffn_in, ffn_conv_w=ffn_conv_w, ffn_conv_b=ffn_conv_b, w_ffn_out=w_ffn_out,
                 norm_final_w=norm_final_w)
    mom_m = dict(zip(_WEIGHTS, (m_meta_tokens, m_norm_mix_w, m_w_in, m_ssd_conv_w, m_ssd_conv_b, m_ssd_dt_bias,
                                m_ssd_a_log, m_ssd_d, m_ssd_norm_w, m_w_ssd_branch, m_w_attn_branch, m_attn_sinks,
                                m_rel_bias, m_gate_b, m_w_out, m_norm_ffn_w, m_w_ffn_in, m_ffn_conv_w, m_ffn_conv_b,
                                m_w_ffn_out, m_norm_final_w)))
    mom_v = dict(zip(_WEIGHTS, (v_meta_tokens, v_norm_mix_w, v_w_in, v_ssd_conv_w, v_ssd_conv_b, v_ssd_dt_bias,
                                v_ssd_a_log, v_ssd_d, v_ssd_norm_w, v_w_ssd_branch, v_w_attn_branch, v_attn_sinks,
                                v_rel_bias, v_gate_b, v_w_out, v_norm_ffn_w, v_w_ffn_in, v_ffn_conv_w, v_ffn_conv_b,
                                v_w_ffn_out, v_norm_final_w)))
    orig_shape = {k: a.shape for k, a in shard.items()}
    two_d = {k: a.reshape(a.shape[-2:]) if a.ndim >= 2 else a.reshape(1, -1) for k, a in shard.items()}
    shape2 = {k: a.shape for k, a in two_d.items()}

    def as2d(tree):
        return {k: tree[k].reshape(shape2[k]) for k in _WEIGHTS}

    mom_m, mom_v = as2d(mom_m), as2d(mom_v)

    exchanges = _LateExchanges(two_d, shape2)
    row_pack = exchanges.row_pack
    small_pack = _pack_rows([two_d[k] for k in _SMALL_SHARDED], LANES, SMALL_ROW_MULT)
    onehot_t = _onehot_t()
    gather = [two_d["w_in"].astype(BF16), small_pack]
    flat_tables, gathered = _bias_tables(two_d["rel_bias"].T, onehot_t, side=_gather_plan(gather))
    w_in_all, small_all = [g.reshape((N_DEV,) + a.shape) for g, a in zip(gathered, gather)]
    full = {k: two_d[k] for k in _SMALL_REPLICATED}
    full["onehot_t"], full["bias_tables"] = onehot_t, flat_tables
    full["in_segs"] = _w_in_to_segments(w_in_all)
    small_flat = small_all.reshape(N_DEV, -1)
    off = 0
    for k in _SMALL_SHARDED:
        size = int(np.prod(shape2[k]))
        full[k] = _gather_full(small_flat[:, off:off + size].reshape((N_DEV,) + shape2[k]), k, shape2[k])
        off += size

    loss_local, grad_x, grads = _local_step(x[0], loss_target[0], full, exchanges)

    small_names = _SMALL_SHARDED + _SMALL_REPLICATED
    small_send = _pack_rows([grads[k] for k in small_names] + [loss_local.reshape(1)], LANES, SMALL_ROW_MULT)
    in_recv = exchanges.w_in_grads_received
    ffn_recv, rows_recv = exchanges.early_grads_received

    w_in_out, (small_recv,) = _adamw(two_d["w_in"], in_recv, mom_m["w_in"], mom_v["w_in"], name="adamw_w_in",
                                     side=_all_to_all_plan([small_send], [False]))
    big = {"w_in": w_in_out,
           "w_ffn_in": _adamw(two_d["w_ffn_in"], ffn_recv, mom_m["w_ffn_in"], mom_v["w_ffn_in"], name="adamw_w_ffn_in")}
    rows_out = _adamw(row_pack(two_d), rows_recv, row_pack(mom_m), row_pack(mom_v), name="adamw_rows")
    off = 0
    for k in _ROW_SHARDED:
        r = shape2[k][0]
        big[k] = [a[off:off + r] for a in rows_out]
        off += r
    me =4 * lax.axis_index("x") + 2 * lax.axis_index("y") + lax.axis_index("c")
    small_full_shapes = [grads[k].shape for k in small_names]
    n_small = sum(int(np.prod(s)) for s in small_full_shapes)

    def packed_small(tree):
        parts = []
        for k in small_names:
            a = tree[k]
            if k in _SMALL_SHARDED:
                fullw = jnp.zeros(grads[k].shape, F32)
                a = lax.dynamic_update_slice(fullw, a, (0, me * a.shape[1]))
            parts.append(a)
        return _pack_rows(parts + [jnp.zeros((1,), F32)], LANES, SMALL_ROW_MULT)

    g_small, d_small, m_small, v_small = _adamw(packed_small(two_d), small_recv, packed_small(mom_m),
                                                packed_small(mom_v), name="adamw_small")

    def unpack_all(which, small):
        out = {k: big[k][which] for k in _BIG}
        flat = small.reshape(-1)
        for k, a in zip(small_names, _unpack(flat, small_full_shapes)):
            if k in _SMALL_SHARDED:
                a = lax.dynamic_slice(a, (0, me * shape2[k][1]), shape2[k])
            out[k] = a
        return out, flat[n_small]

    g_all, loss = unpack_all(0, g_small)
    d_all, _ = unpack_all(1, d_small)
    m_all, _ = unpack_all(2, m_small)
    v_all, _ = unpack_all(3, v_small)

    def final(tree):
        return [tree[k].reshape(orig_shape[k]) for k in _WEIGHTS]

    return (loss, grad_x[None], *final(g_all), *final(d_all), *final(m_all), *final(v_all))
```

```python
import functools
import math

import numpy as np
import jax
import jax.numpy as jnp
from jax import lax
from jax.experimental import pallas as pl
from jax.experimental.pallas import tpu as pltpu

F32 = jnp.float32
BF16 = jnp.bfloat16
HIGHEST = lax.Precision.HIGHEST

D_MODEL = 1024
N_META = 16
BLOCK = 128
PAD = BLOCK - N_META
EPS = 1e-6
NEG = -1e30
SSD_INNER = 2 * D_MODEL
SSD_HEADDIM = 64
SSD_HEADS = SSD_INNER // SSD_HEADDIM
SSD_GROUPS = 4
SSD_HPG = SSD_HEADS // SSD_GROUPS
SSD_STATE = 128
SSD_CONV = 4
SSD_GW = SSD_HPG * SSD_HEADDIM
SSD_BC = SSD_GROUPS * SSD_STATE
SSD_XBC = SSD_INNER + 2 * SSD_BC
ATT_HEADS = 16
ATT_KV_HEADS = 2
ATT_HEADDIM = 64
ATT_GQ = ATT_HEADS // ATT_KV_HEADS
ATT_Q = ATT_HEADS * ATT_HEADDIM
ATT_KV = ATT_KV_HEADS * ATT_HEADDIM
REL_BUCKETS = 32
REL_MAX_DIST = 128
D_FF = 2816
FFN_CONV = 3
ADAM_LR = 0.001
ADAM_B1 = 0.9
ADAM_B2 = 0.999
ADAM_EPS = 1e-08
ADAM_WD = 0.01
ADAM_STEP = 10

N_DEV = 8
LANES = 128
SUBLANES = 8
BF16_ROWS = 16
DT_W = SSD_GROUPS * LANES
VMEM_LIMIT_BYTES = 56 * 1024 * 1024
MESH = pl.DeviceIdType.MESH

SMALL_ROW_MULT = 16

N_KEYS = 3 * BLOCK
NT_ALL = 3 * N_KEYS * BLOCK
NT_TILE = 8192


def _cparams(*sem):
    return pltpu.CompilerParams(dimension_semantics=sem, vmem_limit_bytes=VMEM_LIMIT_BYTES)


def _row_tile(n, cap):
    best = None
    for t in range(16, min(n, cap) + 1, 16):
        if n % t == 0:
            best = t
    return best or n


def _col_tile(n, cap):
    for t in (1408, 1280, 1024, 768, 640, 512, 384, 256, 128):
        if t <= cap and n % t == 0:
            return t
    return n


def _sigmoid(x):
    return 0.5 * jnp.tanh(0.5 * x) + 0.5


def _silu(x):
    return x * _sigmoid(x)


def _softplus(x):
    return jnp.maximum(x, 0.0) + jnp.log(1.0 + jnp.exp(-jnp.abs(x)))


def _dot_nt(a, b):
    return lax.dot_general(a, b, (((1,), (1,)), ((), ())), preferred_element_type=F32)


def _dot_tn(a, b):
    return lax.dot_general(a, b, (((0,), (0,)), ((), ())), preferred_element_type=F32)


def _dot(a, b):
    return jnp.dot(a, b, preferred_element_type=F32)


def _bf16_terms(x, terms):
    out, rest = [], x
    for _ in range(terms):
        part = rest.astype(BF16)
        out.append(part)
        rest = rest - part.astype(F32)
    return out


def _dot_sel(x, sel, terms=3):
    return sum(_dot(part, sel) for part in _bf16_terms(x, terms))


def _sel_dot(sel, x, terms=3):
    return sum(_dot(sel, part) for part in _bf16_terms(x, terms))


def _sum_all(x):
    return jnp.sum(jnp.sum(x, axis=1, keepdims=True), axis=0, keepdims=True)


MM_ROW_CAPS = (2080, 1664, 832, 416)
MM_COL_CAP = 1408
MM_VMEM_BUDGET = 44 * 1024 * 1024


def _mm_tiles(rows, cols, vmem_bytes):
    col_cands = [t for t in (2048, 1536, 1408, 1280, 1024, 768, 640, 512, 384, 256, 128) if cols % t == 0]
    if cols <= 2 * MM_COL_CAP:
        col_cands.append(cols)
    best = None
    for cap in MM_ROW_CAPS:
        tr = _row_tile(rows, cap)
        for tc in col_cands:
            if vmem_bytes(tr, tc) <= MM_VMEM_BUDGET and (best is None or tr * tc > best[0] * best[1]):
                best = (tr, tc)
    assert best is not None, (rows, cols)
    return best


def _mm(a, b, *, name, ta=False, tb=False, c=None, mask=False, out_dtype=F32):
    if not ta:
        m, k = a.shape
        n = b.shape[0] if tb else b.shape[1]
        tm, tn = _mm_tiles(m, n, lambda t_m, t_n: 2 * (t_m * k * a.dtype.itemsize + k * t_n * b.dtype.itemsize
                                                       + t_m * t_n * (jnp.dtype(out_dtype).itemsize
                                                                      + (0 if c is None else c.dtype.itemsize)))
                           + 4 * t_m * t_n)

        def body(*refs):
            if c is None:
                a_ref, b_ref, o_ref = refs
            else:
                a_ref, b_ref, c_ref, o_ref = refs
            acc = (_dot_nt if tb else _dot)(a_ref[...].astype(BF16), b_ref[...].astype(BF16))
            if mask:
                row = pl.program_id(0) * tm + lax.broadcasted_iota(jnp.int32, (tm, 1), 0)
                acc = jnp.where(row >= PAD, acc, 0.0)
            if c is not None:
                acc = acc + c_ref[...]
            o_ref[...] = acc.astype(out_dtype)

        b_spec = pl.BlockSpec((tn, k), lambda i, j: (j, 0)) if tb else pl.BlockSpec((k, tn), lambda i, j: (0, j))
        in_specs = [pl.BlockSpec((tm, k), lambda i, j: (i, 0)), b_spec]
        args = [a, b]
        if c is not None:
            in_specs.append(pl.BlockSpec((tm, tn), lambda i, j: (i, j)))
            args.append(c)
        return pl.pallas_call(
            body, name=name, grid=(m // tm, n // tn), in_specs=in_specs,
            out_specs=pl.BlockSpec((tm, tn), lambda i, j: (i, j)),
            out_shape=jax.ShapeDtypeStruct((m, n), out_dtype),
            compiler_params=_cparams("parallel", "parallel"))(*args)

    kc, m = a.shape
    n = b.shape[1]
    tm = _col_tile(m, MM_COL_CAP)
    tk, tn = _mm_tiles(kc, n, lambda t_k, t_n: 2 * (t_k * tm * a.dtype.itemsize + t_k * t_n * b.dtype.itemsize
                                                    + tm * t_n * jnp.dtype(out_dtype).itemsize) + 8 * tm * t_n)

    n_k = kc // tk

    def body_t(a_ref, b_ref, o_ref, acc_ref):
        kk = pl.program_id(2)
        bb = b_ref[...]
        if mask:
            row = kk * tk + lax.broadcasted_iota(jnp.int32, (tk, 1), 0)
            bb = jnp.where(row >= PAD, bb, jnp.zeros_like(bb))
        p = _dot_tn(a_ref[...].astype(BF16), bb.astype(BF16))

        @pl.when(kk == 0)
        def _():
            acc_ref[...] = p

        @pl.when(kk > 0)
        def _():
            acc_ref[...] += p

        @pl.when(kk == n_k - 1)
        def _():
            o_ref[...] = acc_ref[...].astype(out_dtype)

    return pl.pallas_call(
        body_t, name=name, grid=(m // tm, n // tn, n_k),
        in_specs=[pl.BlockSpec((tk, tm), lambda i, j, kk: (kk, i)), pl.BlockSpec((tk, tn), lambda i, j, kk: (kk, j))],
        out_specs=pl.BlockSpec((tm, tn), lambda i, j, kk: (i, j)),
        out_shape=jax.ShapeDtypeStruct((m, n), out_dtype),
        scratch_shapes=[pltpu.VMEM((tm, tn), F32)],
        compiler_params=_cparams("parallel", "parallel", "arbitrary"))(a, b)


def _mm_rms_bwd(pairs, x, w, dres, *, name, side=None):
    m, d = x.shape
    tm = _row_tile(m, 416)
    n_pairs = len(pairs)

    def body(*refs):
        a_refs, b_refs = refs[:n_pairs], refs[n_pairs:2 * n_pairs]
        x_ref, w_ref, dres_ref, dx_ref, dw_ref = refs[2 * n_pairs:]
        i = pl.program_id(0)
        dyv = None
        for a_ref, b_ref in zip(a_refs, b_refs):
            term = _dot_nt(a_ref[...].astype(BF16), b_ref[...])
            dyv = term if dyv is None else dyv + term
        xv = x_ref[...]
        r = lax.rsqrt(jnp.mean(xv * xv, axis=-1, keepdims=True) + EPS)
        xh = xv * r
        g = dyv * w_ref[...]
        dx_ref[...] = r * (g - xh * jnp.mean(g * xh, axis=-1, keepdims=True)) + dres_ref[...]
        part = jnp.sum(dyv * xh, axis=0, keepdims=True)

        @pl.when(i == 0)
        def _():
            dw_ref[...] = part

        @pl.when(i > 0)
        def _():
            dw_ref[...] += part

    row = pl.BlockSpec((tm, d), lambda i: (i, 0))
    vec = pl.BlockSpec((1, d), lambda i: (0, 0))
    in_specs = ([pl.BlockSpec((tm, a.shape[1]), lambda i: (i, 0)) for a, _ in pairs]
                + [pl.BlockSpec(b.shape, lambda i: (0, 0), pipeline_mode=pl.Buffered(1)) for _, b in pairs]
                + [row, vec, row])
    return _call_with_side(
        body, side, name=name, grid=(m // tm,), in_specs=in_specs, out_specs=[row, vec],
        out_shape=[jax.ShapeDtypeStruct((m, d), F32), jax.ShapeDtypeStruct((1, d), F32)], scratch_shapes=[],
        args=[a for a, _ in pairs] + [b for _, b in pairs] + [x, w, dres], semantics=("arbitrary",))


def _rms_fwd(h, w, *, name):
    n, d = h.shape
    tm = _row_tile(n, 832)

    def body(h_ref, w_ref, o_ref):
        x = h_ref[...]
        r = lax.rsqrt(jnp.mean(x * x, axis=-1, keepdims=True) + EPS)
        o_ref[...] = (x * r * w_ref[...]).astype(BF16)

    return pl.pallas_call(
        body, name=name, grid=(n // tm,),
        in_specs=[pl.BlockSpec((tm, d), lambda i: (i, 0)), pl.BlockSpec((1, d), lambda i: (0, 0))],
        out_specs=pl.BlockSpec((tm, d), lambda i: (i, 0)),
        out_shape=jax.ShapeDtypeStruct((n, d), BF16),
        compiler_params=_cparams("parallel"))(h, w)


def _ffn_out_loss(act, w_out, h, norm_w, target):
    n, k = act.shape
    d = w_out.shape[1]
    nb = n // BLOCK
    per = max(p for p in (5, 4, 3, 2, 1) if nb % p == 0)
    tm = per * BLOCK

    def body(a_ref, w_ref, h_ref, nw_ref, *rest):
        t_refs, (dh_ref, dhb_ref, loss_ref, dw_ref) = rest[:per], rest[per:]
        i = pl.program_id(0)
        row = i * tm + lax.broadcasted_iota(jnp.int32, (tm, 1), 0)
        xv = jnp.where(row >= PAD, _dot(a_ref[...], w_ref[...]), 0.0) + h_ref[...]
        r = lax.rsqrt(jnp.mean(xv * xv, axis=-1, keepdims=True) + EPS)
        xh = xv * r
        wv = nw_ref[...]
        tgt = jnp.concatenate([t_ref[0] for t_ref in t_refs], axis=0)
        err = jnp.where(row >= BLOCK, xh * wv - tgt, 0.0)
        dyv = err * (1.0 / d)
        g = dyv * wv
        dh = r * (g - xh * jnp.mean(g * xh, axis=-1, keepdims=True))
        dh_ref[...] = dh
        dhb_ref[...] = dh.astype(BF16)
        lpart = jnp.broadcast_to(0.5 * _sum_all(err * err) * (1.0 / d), (1, LANES))
        wpart = jnp.sum(dyv * xh, axis=0, keepdims=True)

        @pl.when(i == 0)
        def _():
            loss_ref[...] = lpart
            dw_ref[...] = wpart

        @pl.when(i > 0)
        def _():
            loss_ref[...] += lpart
            dw_ref[...] += wpart

    def target_spec(j):
        return pl.BlockSpec((1, BLOCK, d), lambda i: (jnp.clip(per * i - 1 + j, 0, nb - 2), 0, 0))

    row = pl.BlockSpec((tm, d), lambda i: (i, 0))
    vec = pl.BlockSpec((1, d), lambda i: (0, 0))
    blocks = target.reshape(nb - 1, BLOCK, d)
    return pl.pallas_call(
        body, name="ffn_out_loss", grid=(nb // per,),
        in_specs=[pl.BlockSpec((tm, k), lambda i: (i, 0)),
                  pl.BlockSpec((k, d), lambda i: (0, 0), pipeline_mode=pl.Buffered(1)), row, vec]
        + [target_spec(j) for j in range(per)],
        out_specs=[row, row, pl.BlockSpec((1, LANES), lambda i: (0, 0)), vec],
        out_shape=[jax.ShapeDtypeStruct((n, d), F32), jax.ShapeDtypeStruct((n, d), BF16),
                   jax.ShapeDtypeStruct((1, LANES), F32), jax.ShapeDtypeStruct((1, d), F32)],
        compiler_params=_cparams("arbitrary"))(act, w_out, h, norm_w, *([blocks] * per))


def _final_loss(h, w, target):
    n, d = h.shape
    nb = n // BLOCK

    def body(h_ref, w_ref, t_ref, dh_ref, dhb_ref, loss_ref, dw_ref):
        i = pl.program_id(0)
        xv = h_ref[...]
        r = lax.rsqrt(jnp.mean(xv * xv, axis=-1, keepdims=True) + EPS)
        xh = xv * r
        wv = w_ref[...]
        err = jnp.where(i >= 1, xh * wv - t_ref[...], 0.0)
        dyv = err * (1.0 / d)
        g = dyv * wv
        dh = r * (g - xh * jnp.mean(g * xh, axis=-1, keepdims=True))
        dh_ref[...] = dh
        dhb_ref[...] = dh.astype(BF16)
        lpart = jnp.broadcast_to(0.5 * _sum_all(err * err) * (1.0 / d), (1, LANES))
        wpart = jnp.sum(dyv * xh, axis=0, keepdims=True)

        @pl.when(i == 0)
        def _():
            loss_ref[...] = lpart
            dw_ref[...] = wpart

        @pl.when(i > 0)
        def _():
            loss_ref[...] += lpart
            dw_ref[...] += wpart

    row = pl.BlockSpec((BLOCK, d), lambda i: (i, 0))
    vec = pl.BlockSpec((1, d), lambda i: (0, 0))
    return pl.pallas_call(
        body, name="final_loss", grid=(nb,),
        in_specs=[row, vec, pl.BlockSpec((BLOCK, d), lambda i: (jnp.maximum(i - 1, 0), 0))],
        out_specs=[row, row, pl.BlockSpec((1, LANES), lambda i: (0, 0)), vec],
        out_shape=[jax.ShapeDtypeStruct((n, d), F32), jax.ShapeDtypeStruct((n, d), BF16),
                   jax.ShapeDtypeStruct((1, LANES), F32), jax.ShapeDtypeStruct((1, d), F32)],
        compiler_params=_cparams("arbitrary"))(h, w, target)


def _main_spec(tm, cb, off=0):
    return pl.BlockSpec((tm, cb), lambda j, i: (i, j + off))


def _prev_spec(tm, cb, off=0):
    r8 = tm // SUBLANES
    return pl.BlockSpec((SUBLANES, cb), lambda j, i: (jnp.maximum(i * r8 - 1, 0), j + off))


def _next_spec(tm, cb, n_rows, off=0):
    r8 = tm // SUBLANES
    last = n_rows // SUBLANES - 1
    return pl.BlockSpec((SUBLANES, cb), lambda j, i: (jnp.minimum((i + 1) * r8, last), j + off))


def _with_prev(prev_ref, main_ref, i):
    prev = jnp.where(i > 0, prev_ref[...], 0.0)
    return jnp.concatenate([prev, main_ref[...]], axis=0)


def _with_next(main, nxt, i, n_tiles):
    return jnp.concatenate([main, jnp.where(i < n_tiles - 1, nxt, 0.0)], axis=0)


def _back(xx, s, tm):
    if s == 0:
        return xx[SUBLANES:SUBLANES + tm]
    return pltpu.roll(xx, s, 0)[SUBLANES:SUBLANES + tm]


def _ahead(xx, s, tm):
    if s == 0:
        return xx[:tm]
    return pltpu.roll(xx, xx.shape[0] - s, 0)[:tm]


def _mm_conv_fwd(u, w_in, w, b, *, name):
    n = u.shape[0]
    cdim = w_in.shape[1]
    kw = w.shape[0]
    tm = _row_tile(n, 832)
    cb = _col_tile(cdim, 512)
    nt = n // tm

    def body(u_ref, w_in_ref, w_ref, b_ref, x_ref, o_ref, acc_scr, halo_scr):
        j, i = pl.program_id(0), pl.program_id(1)

        @pl.when((j == 0) & (i == 0))
        def _():
            acc_scr[...] = jnp.zeros_like(acc_scr)
            halo_scr[...] = jnp.zeros_like(halo_scr)

        new = _dot(u_ref[...], w_in_ref[...])
        prev = acc_scr[...]
        xx = jnp.concatenate([jnp.where(i >= 2, halo_scr[...], 0.0), prev], axis=0)
        acc = jnp.broadcast_to(b_ref[...], (tm, cb))
        for k in range(kw):
            acc = acc + w_ref[k:k + 1, :] * _back(xx, kw - 1 - k, tm)
        x_ref[...] = prev.astype(BF16)
        o_ref[...] = acc
        halo_scr[...] = prev[tm - SUBLANES:, :]
        acc_scr[...] = new

    out = pl.BlockSpec((tm, cb), lambda j, i: (jnp.maximum(i - 1, 0), j))
    shp = jax.ShapeDtypeStruct((n, cdim), F32)
    return pl.pallas_call(
        body, name=name, grid=(cdim // cb, nt + 1),
        in_specs=[pl.BlockSpec((tm, u.shape[1]), lambda j, i: (jnp.minimum(i, nt - 1), 0)),
                  pl.BlockSpec((w_in.shape[0], cb), lambda j, i: (0, j)),
                  pl.BlockSpec((kw, cb), lambda j, i: (0, j)), pl.BlockSpec((1, cb), lambda j, i: (0, j))],
        out_specs=[out, out], out_shape=[jax.ShapeDtypeStruct((n, cdim), BF16), shp],
        scratch_shapes=[pltpu.VMEM((tm, cb), F32), pltpu.VMEM((SUBLANES, cb), F32)],
        compiler_params=_cparams("arbitrary", "arbitrary"))(u, w_in, w, b)


def _conv_bwd_core(dpre_ext, x, w_ref, kw, tm):
    dx = None
    dws = []
    for k in range(kw):
        shifted = _ahead(dpre_ext, kw - 1 - k, tm)
        term = w_ref[k:k + 1, :] * shifted
        dx = term if dx is None else dx + term
        dws.append(jnp.sum(shifted * x, axis=0, keepdims=True))
    return dx, dws, jnp.sum(dpre_ext[:tm], axis=0, keepdims=True)


def _acc_rows(i, dw_ref, db_ref, dws, db):
    @pl.when(i == 0)
    def _():
        for k, v in enumerate(dws):
            dw_ref[k:k + 1, :] = v
        db_ref[...] = db

    @pl.when(i > 0)
    def _():
        for k, v in enumerate(dws):
            dw_ref[k:k + 1, :] += v
        db_ref[...] += db


def _conv_bwd(dpre, x, w, *, name, col0=0, into=None):
    n, cdim = x.shape
    kw = w.shape[0]
    tm = _row_tile(n, 832)
    cb = _col_tile(cdim, 512)
    nt = n // tm
    off = col0 // cb
    n_alias = 0 if into is None else 3

    def body(d_ref, dn_ref, x_ref, w_ref, *rest):
        dx_ref, dw_ref, db_ref = rest[n_alias:]
        i = pl.program_id(1)
        dpre_ext = _with_next(d_ref[...], dn_ref[...], i, nt)
        dx, dws, db = _conv_bwd_core(dpre_ext, x_ref[...], w_ref, kw, tm)
        dx_ref[...] = dx.astype(BF16)
        _acc_rows(i, dw_ref, db_ref, dws, db)

    wspec = pl.BlockSpec((kw, cb), lambda j, i: (0, j + off))
    bspec = pl.BlockSpec((1, cb), lambda j, i: (0, j + off))
    return pl.pallas_call(
        body, name=name, grid=(dpre.shape[1] // cb, nt),
        in_specs=[_main_spec(tm, cb), _next_spec(tm, cb, n), _main_spec(tm, cb, off), wspec]
        + [pl.BlockSpec(memory_space=pl.ANY)] * n_alias,
        out_specs=[_main_spec(tm, cb, off), wspec, bspec],
        out_shape=[jax.ShapeDtypeStruct((n, cdim), BF16), jax.ShapeDtypeStruct((kw, cdim), F32),
                   jax.ShapeDtypeStruct((1, cdim), F32)],
        input_output_aliases={4 + k: k for k in range(n_alias)},
        compiler_params=_cparams("parallel", "arbitrary"))(dpre, dpre, x, w, *(into or ()))


def _ffn_in_act_fwd(u, w_in, w, b):
    n = u.shape[0]
    kw = w.shape[0]
    tm = _row_tile(n, 832)
    cb = _col_tile(D_FF, 256)
    nc = D_FF // cb
    nt = n // tm

    def body(u_ref, wu_in_ref, wg_in_ref, wu_ref, wg_ref, bu_ref, bg_ref,
             xu_ref, xg_ref, hu_ref, hg_ref, act_ref, acc_scr, halo_scr):
        j, i = pl.program_id(0), pl.program_id(1)

        @pl.when((j == 0) & (i == 0))
        def _():
            acc_scr[...] = jnp.zeros_like(acc_scr)
            halo_scr[...] = jnp.zeros_like(halo_scr)

        ub = u_ref[...]
        new = [_dot(ub, wu_in_ref[...]), _dot(ub, wg_in_ref[...])]
        hid = []
        for half, (x_ref, w_ref, b_ref) in enumerate(((xu_ref, wu_ref, bu_ref), (xg_ref, wg_ref, bg_ref))):
            prev = acc_scr[half]
            xx = jnp.concatenate([jnp.where(i >= 2, halo_scr[half], 0.0), prev], axis=0)
            acc = jnp.broadcast_to(b_ref[...], (tm, cb))
            for k in range(kw):
                acc = acc + w_ref[k:k + 1, :] * _back(xx, kw - 1 - k, tm)
            x_ref[...] = prev.astype(BF16)
            hid.append(acc)
            halo_scr[half] = prev[tm - SUBLANES:, :]
            acc_scr[half] = new[half]
        hu_ref[...] = hid[0].astype(BF16)
        hg_ref[...] = hid[1].astype(BF16)
        act_ref[...] = (_silu(hid[1]) * hid[0]).astype(BF16)

    def wspec(off):
        return pl.BlockSpec((kw, cb), lambda j, i: (0, j + off))

    def bspec(off):
        return pl.BlockSpec((1, cb), lambda j, i: (0, j + off))

    def in_w(off):
        return pl.BlockSpec((w_in.shape[0], cb), lambda j, i: (0, j + off))

    out = pl.BlockSpec((tm, cb), lambda j, i: (jnp.maximum(i - 1, 0), j))
    bf16_out = jax.ShapeDtypeStruct((n, D_FF), BF16)
    return pl.pallas_call(
        body, name="ffn_in_act_fwd", grid=(nc, nt + 1),
        in_specs=[pl.BlockSpec((tm, u.shape[1]), lambda j, i: (jnp.minimum(i, nt - 1), 0)), in_w(0), in_w(nc),
                  wspec(0), wspec(nc), bspec(0), bspec(nc)],
        out_specs=[out] * 5,
        out_shape=[bf16_out] * 5,
        scratch_shapes=[pltpu.VMEM((2, tm, cb), F32), pltpu.VMEM((2, SUBLANES, cb), F32)],
        compiler_params=_cparams("arbitrary", "arbitrary"))(u, w_in, w_in, w, w, b, b)


def _ffn_out_act_bwd(dh, w_out, hu, hg, x_up, x_gate, w):
    n = x_up.shape[0]
    kw = w.shape[0]
    tm = _row_tile(n, 832)
    cb = _col_tile(D_FF, 256)
    nc = D_FF // cb
    nt = n // tm

    def body(dh_ref, wo_ref, hu_ref, hun_ref, hg_ref, hgn_ref, xu_ref, xg_ref, wu_ref, wg_ref,
             dxu_ref, dxg_ref, dwu_ref, dwg_ref, dbu_ref, dbg_ref, acc_scr, halo_scr):
        j, i = pl.program_id(0), pl.program_id(1)

        @pl.when((j == 0) & (i == 0))
        def _():
            acc_scr[...] = jnp.zeros_like(acc_scr)
            halo_scr[...] = jnp.zeros_like(halo_scr)

        tile = jnp.maximum(nt - 1 - i, 0)
        row = tile * tm + lax.broadcasted_iota(jnp.int32, (tm, 1), 0)
        new = jnp.where(row >= PAD, _dot_nt(dh_ref[...].astype(BF16), wo_ref[...]), 0.0)
        prev = jnp.where(i >= 1, acc_scr[...], 0.0)
        dact_e = jnp.concatenate([prev, jnp.where(i >= 2, halo_scr[...], 0.0)], axis=0)
        last = nt - i >= nt - 1
        up_e = jnp.concatenate([hu_ref[...].astype(F32), jnp.where(last, 0.0, hun_ref[...].astype(F32))], axis=0)
        gate_e = jnp.concatenate([hg_ref[...].astype(F32), jnp.where(last, 0.0, hgn_ref[...].astype(F32))], axis=0)
        halo_scr[...] = prev[:BF16_ROWS, :]
        acc_scr[...] = new
        sg = _sigmoid(gate_e)
        dup_e = dact_e * (gate_e * sg)
        dgate_e = dact_e * up_e * (sg * (1.0 + gate_e * (1.0 - sg)))
        dx, dws, db = _conv_bwd_core(dup_e, xu_ref[...], wu_ref, kw, tm)
        dxu_ref[...] = dx.astype(BF16)
        _acc_rows(i, dwu_ref, dbu_ref, dws, db)
        dx, dws, db = _conv_bwd_core(dgate_e, xg_ref[...], wg_ref, kw, tm)
        dxg_ref[...] = dx.astype(BF16)
        _acc_rows(i, dwg_ref, dbg_ref, dws, db)

    def done_tile(i):
        return jnp.minimum(nt - i, nt - 1)

    halo_blocks = tm // BF16_ROWS
    main = pl.BlockSpec((tm, cb), lambda j, i: (done_tile(i), j))
    nxt = pl.BlockSpec((BF16_ROWS, cb),
                       lambda j, i: (jnp.minimum((done_tile(i) + 1) * halo_blocks, n // BF16_ROWS - 1), j))
    wspec0 = pl.BlockSpec((kw, cb), lambda j, i: (0, j))
    wspec1 = pl.BlockSpec((kw, cb), lambda j, i: (0, j + nc))
    bspec = pl.BlockSpec((1, cb), lambda j, i: (0, j))
    return pl.pallas_call(
        body, name="ffn_out_act_bwd", grid=(nc, nt + 1),
        in_specs=[pl.BlockSpec((tm, dh.shape[1]), lambda j, i: (jnp.maximum(nt - 1 - i, 0), 0)),
                  pl.BlockSpec((cb, w_out.shape[1]), lambda j, i: (j, 0)),
                  main, nxt, main, nxt, main, main, wspec0, wspec1],
        out_specs=[main, main, wspec0, wspec0, bspec, bspec],
        out_shape=[jax.ShapeDtypeStruct((n, D_FF), BF16), jax.ShapeDtypeStruct((n, D_FF), BF16),
                   jax.ShapeDtypeStruct((kw, D_FF), F32), jax.ShapeDtypeStruct((kw, D_FF), F32),
                   jax.ShapeDtypeStruct((1, D_FF), F32), jax.ShapeDtypeStruct((1, D_FF), F32)],
        scratch_shapes=[pltpu.VMEM((tm, cb), F32), pltpu.VMEM((BF16_ROWS, cb), F32)],
        compiler_params=_cparams("arbitrary", "arbitrary"))(dh, w_out, hu, hu, hg, hg, x_up, x_gate, w, w)


def _ssd_prep(pxs_ref, pb_ref, pc_ref, dtr_ref, dtb_ref, alog_ref, c):
    xs = _silu(pxs_ref[...])
    bm = _silu(pb_ref[...])
    cm = _silu(pc_ref[...])
    return (xs, bm, cm) + _ssd_decay(dtr_ref, dtb_ref, alog_ref, c)


def _ssd_decay(dtr_ref, dtb_ref, alog_ref, c):
    row =lax.broadcasted_iota(jnp.int32, (BLOCK, 1), 0) + c * BLOCK
    valid = (row >= PAD).astype(F32)
    dtr = dtr_ref[...] + dtb_ref[...]
    dt = _softplus(dtr) * valid
    a = -jnp.exp(alog_ref[...])
    lam = dt * a
    ri = lax.broadcasted_iota(jnp.int32, (BLOCK, BLOCK), 0)
    ci = lax.broadcasted_iota(jnp.int32, (BLOCK, BLOCK), 1)
    causal = ci <= ri
    cs = _sel_dot(causal.astype(BF16), lam)
    return valid, dtr, dt, a, lam, cs, causal


def _head_cols(r):
    return slice(SSD_HEADDIM * r, SSD_HEADDIM * (r + 1))


def _ssd_specs(nc, rev):
    def cidx(c):
        return nc - 1 - c if rev else c

    xs = pl.BlockSpec((BLOCK, SSD_GW), lambda g, c: (cidx(c), g))
    bspec = pl.BlockSpec((BLOCK, SSD_STATE), lambda g, c: (cidx(c), SSD_INNER // SSD_STATE + g))
    cspec = pl.BlockSpec((BLOCK, SSD_STATE), lambda g, c: (cidx(c), (SSD_INNER + SSD_BC) // SSD_STATE + g))
    lane = pl.BlockSpec((BLOCK, LANES), lambda g, c: (cidx(c), g))
    vec = pl.BlockSpec((1, LANES), lambda g, c: (0, g))
    wide_vec = pl.BlockSpec((1, SSD_GW), lambda g, c: (0, g))
    hsave = pl.BlockSpec((1, 1, SSD_GW, SSD_STATE), lambda g, c: (cidx(c), g, 0, 0))
    return xs, bspec, cspec, lane, vec, wide_vec, hsave


def _head_spread_matrix():
    r = lax.broadcasted_iota(jnp.int32, (LANES, SSD_GW), 0)
    col = lax.broadcasted_iota(jnp.int32, (LANES, SSD_GW), 1)
    return (col // SSD_HEADDIM == r).astype(BF16)


def _const_spec(shape):
    return pl.BlockSpec(shape, lambda g, c: (0,) * len(shape))


def _spread_heads(per_head, e_ref):
    wide = _dot_sel(jnp.concatenate(per_head, axis=0), e_ref[...])
    return [wide[BLOCK * k:BLOCK * (k + 1)] for k in range(len(per_head))]


def _call_with_side(body, side, *, name, grid, in_specs, out_specs, out_shape, scratch_shapes, args,
                    semantics=("parallel", "arbitrary")):
    if side is None:
        outs = pl.pallas_call(body, name=name, grid=grid, in_specs=in_specs, out_specs=out_specs, out_shape=out_shape,
                              scratch_shapes=scratch_shapes, compiler_params=_cparams(*semantics))(*args)
        return outs, []
    n_in, n_out, n_scr, n_side = len(in_specs), len(out_specs), len(scratch_shapes), len(side.arrays)

    def body_with_side(*refs):
        ins, rest = refs[:n_in + n_side], refs[n_in + n_side:]
        outs, scratch = rest[:n_out + n_side], rest[n_out + n_side:]
        side_refs = (ins[n_in:], outs[n_out:], scratch[n_scr:])
        ids = [pl.program_id(k) for k in range(len(grid))]
        inner_first = functools.reduce(jnp.logical_and, [i == 0 for i in ids[1:]], True)

        @pl.when((ids[0] == 0) & inner_first)
        def _():
            side.phases[0](*side_refs)

        body(*ins[:n_in], *outs[:n_out], *scratch[:n_scr])

        @pl.when((ids[0] == grid[0] // 2) & inner_first)
        def _():
            side.phases[1](*side_refs)

        @pl.when(functools.reduce(jnp.logical_and, [i == n - 1 for i, n in zip(ids, grid)]))
        def _():
            side.phases[2](*side_refs)

    any_spec = pl.BlockSpec(memory_space=pl.ANY)
    outs = pl.pallas_call(
        body_with_side, name=name, grid=grid, in_specs=list(in_specs) + [any_spec] * n_side,
        out_specs=list(out_specs) + [any_spec] * n_side, out_shape=list(out_shape) + list(side.out_shape),
        scratch_shapes=list(scratch_shapes) + list(side.scratch_shapes),
        compiler_params=_cparams(*["arbitrary"] * len(grid)))(*args, *side.arrays)
    return outs[:n_out], outs[n_out:]


def _ssd_fwd(pre, dt_raw, z, dtb, alog, dskip_w, norm_w, side=None):
    n = pre.shape[0]
    nc = n // BLOCK
    xs_s, b_s, c_s, lane_s, vec_s, wide_s, hs_s = _ssd_specs(nc, False)

    def body(pxs_ref, pb_ref, pc_ref, dtr_ref, z_ref, dtb_ref, alog_ref, dskw_ref, nw_ref, e_ref,
             y_ref, yn_ref, hs_ref, h_scr):
        c = pl.program_id(1)

        @pl.when(c == 0)
        def _():
            h_scr[...] = jnp.zeros_like(h_scr)

        xs, bm, cm, _, _, dt, _, _, cs, causal = _ssd_prep(pxs_ref, pb_ref, pc_ref, dtr_ref, dtb_ref, alog_ref, c)
        cst = cs.T
        cs_last = cs[BLOCK - 1:BLOCK, :]
        dt_w, ecs_w, dec_w = _spread_heads([dt, jnp.exp(cs), jnp.exp(cs_last - cs)], e_ref)
        xdt = xs * dt_w
        bmb = bm.astype(BF16)
        cmb = cm.astype(BF16)
        cb = _dot_nt(cmb, bmb)
        hg = h_scr[...]
        hs_ref[0, 0] = hg
        y = _dot_nt(cmb, hg.astype(BF16)) * ecs_w + dskw_ref[...] * xs
        first = lax.broadcasted_iota(jnp.int32, (BLOCK, LANES), 1) < SSD_HEADDIM
        diag = []
        for j in range(SSD_HPG // 2):
            xp = xdt[:, LANES * j:LANES * (j + 1)].astype(BF16)
            res = []
            for r in (2 * j, 2 * j + 1):
                lm = jnp.exp(jnp.where(causal, cs[:, r:r + 1] - cst[r:r + 1, :], NEG))
                res.append(_dot((cb * lm).astype(BF16), xp))
            diag.append(jnp.where(first, res[0], res[1]))
        y = y + jnp.concatenate(diag, axis=1)
        st = _dot_tn((xdt * dec_w).astype(BF16), bmb)
        eh = jnp.exp(cs_last)
        for r in range(SSD_HPG):
            rows = _head_cols(r)
            h_scr[rows, :] = hg[rows, :] * eh[:, r:r + 1] + st[rows, :]
        y_ref[...] = y
        gts = y * _silu(z_ref[...])
        rr = lax.rsqrt(jnp.mean(gts * gts, axis=-1, keepdims=True) + EPS)
        yn_ref[...] = (gts * rr * nw_ref[...]).astype(BF16)

    return _call_with_side(
        body, side, name="ssd_fwd", grid=(SSD_GROUPS, nc),
        in_specs=[xs_s, b_s, c_s, lane_s, xs_s, vec_s, vec_s, wide_s, wide_s, _const_spec((LANES, SSD_GW))],
        out_specs=[xs_s, xs_s, hs_s],
        out_shape=[jax.ShapeDtypeStruct((n, SSD_INNER), F32), jax.ShapeDtypeStruct((n, SSD_INNER), BF16),
                   jax.ShapeDtypeStruct((nc, SSD_GROUPS, SSD_GW, SSD_STATE), F32)],
        scratch_shapes=[pltpu.VMEM((SSD_GW, SSD_STATE), F32)],
        args=(pre, pre, pre, dt_raw, z, dtb, alog, dskip_w, norm_w, _head_spread_matrix()))


def _lane_put(acc, col, r):
    lane = lax.broadcasted_iota(jnp.int32, acc.shape, 1)
    return jnp.where(lane == r, col, acc)


def _ssd_bwd(dyn, y, z, pre, dt_raw, hsave, dtb, alog, dskip_w, norm_w, side=None):
    n = pre.shape[0]
    nc = n // BLOCK
    spread = _head_spread_matrix()
    xs_s, b_s, c_s, lane_s, vec_s, wide_s, hs_s = _ssd_specs(nc, True)
    bc_out =pl.BlockSpec((BLOCK, SSD_STATE), lambda g, c: (nc - 1 - c, g))

    def body(dyn_ref, y_ref, z_ref, pxs_ref, pb_ref, pc_ref, dtr_ref, hs_ref, dtb_ref, alog_ref, dskw_ref, nw_ref,
             e_ref, r_ref,
             dz_ref, dxs_ref, dbm_ref, dcm_ref, ddt_ref, dnw_ref, ddtb_ref, dalog_ref, ddsk_ref, g_scr):
        step = pl.program_id(1)
        c = nc - 1 - step

        @pl.when(step == 0)
        def _():
            g_scr[...] = jnp.zeros_like(g_scr)

        pxs, pb, pc = pxs_ref[...], pb_ref[...], pc_ref[...]
        sx, sb, sc = _sigmoid(pxs), _sigmoid(pb), _sigmoid(pc)
        xs, bm, cm = pxs * sx, pb * sb, pc * sc
        valid, dtr, dt, a, lam, cs, causal = _ssd_decay(dtr_ref, dtb_ref, alog_ref, c)
        cst = cs.T
        cs_last = cs[BLOCK - 1:BLOCK, :]
        bmb = bm.astype(BF16)
        cmb = cm.astype(BF16)
        cb = _dot_nt(cmb, bmb)
        hg = hs_ref[0, 0]
        hgb = hg.astype(BF16)
        yoff = _dot_nt(cmb, hgb)
        gn = g_scr[...]
        gnb = gn.astype(BF16)

        zv = z_ref[...]
        yv = y_ref[...]
        sgz = _sigmoid(zv)
        sz = zv * sgz
        gts = yv * sz
        rr = lax.rsqrt(jnp.mean(gts * gts, axis=-1, keepdims=True) + EPS)
        xh = gts * rr
        dynv = dyn_ref[...]
        gg = dynv * nw_ref[...]
        dgts = rr * (gg - xh * jnp.mean(gg * xh, axis=-1, keepdims=True))
        dnw = jnp.sum(dynv * xh, axis=0, keepdims=True)
        dy = dgts * sz
        dz_ref[...] = (dgts * yv * (sgz * (1.0 + zv * (1.0 - sgz)))).astype(BF16)

        ecs = jnp.exp(cs)
        dec = jnp.exp(cs_last - cs)
        eh = jnp.exp(cs_last)
        dt_w, ecs_w, dec_w = _spread_heads([dt, ecs, dec], e_ref)
        red_m = r_ref[...]

        def head_sums(v):
            return _dot_sel(v, red_m, terms=2)

        xdt = xs * dt_w
        q_all = _dot_nt(bmb, gnb)
        w_all = (dy * ecs_w).astype(BF16)
        e_hl = head_sums(q_all * xdt) * dec
        dcs_col = head_sums(dy * yoff) * ecs - e_hl
        gh = jnp.zeros((1, LANES), F32)
        prod = gn * hg
        for r in range(SSD_HPG):
            gh = _lane_put(gh, _sum_all(prod[_head_cols(r), :]), r)
        dcs_last = jnp.sum(e_hl, axis=0, keepdims=True) + eh * gh
        ddsk = jnp.sum(head_sums(dy * xs), axis=0, keepdims=True)
        cbt = _dot_nt(bmb, cmb)
        lane = lax.broadcasted_iota(jnp.int32, (BLOCK, LANES), 1)
        first = lane < SSD_HEADDIM
        causal_t = lax.broadcasted_iota(jnp.int32, (BLOCK, BLOCK), 1) >= lax.broadcasted_iota(
            jnp.int32, (BLOCK, BLOCK), 0)
        sub = lax.broadcasted_iota(jnp.int32, (SUBLANES, BLOCK), 0)
        dcs_row = jnp.zeros((SUBLANES, BLOCK), F32)
        dcb = jnp.zeros((BLOCK, BLOCK), F32)
        dxdt_pairs = []
        for j in range(SSD_HPG // 2):
            tile = slice(LANES * j, LANES * (j + 1))
            dy_p = dy[:, tile]
            dyb = dy_p.astype(BF16)
            xdtb = xdt[:, tile].astype(BF16)
            res = []
            for half, r in enumerate((2 * j, 2 * j + 1)):
                csc, csr = cs[:, r:r + 1], cst[r:r + 1, :]
                lm = jnp.exp(jnp.where(causal, csc - csr, NEG))
                lmt = jnp.exp(jnp.where(causal_t, csr - csc, NEG))
                keep = first if half == 0 else jnp.logical_not(first)
                gm = _dot_nt(jnp.where(keep, dy_p, 0.0).astype(BF16), xdtb) * lm
                dcb = dcb + gm
                mm_ = gm * cb
                dcs_col = dcs_col + jnp.where(lane == r, jnp.sum(mm_, axis=1, keepdims=True), 0.0)
                dcs_row = jnp.where(sub == r, jnp.sum(mm_, axis=0, keepdims=True), dcs_row)
                res.append(_dot((cbt * lmt).astype(BF16), dyb))
            dxdt_pairs.append(jnp.where(first, res[0], res[1]))
        dxdt = jnp.concatenate(dxdt_pairs, axis=1) + q_all * dec_w
        ddt_x = head_sums(dxdt * xs)
        dxs = dxdt * dt_w + dskw_ref[...] * dy
        dcbb = dcb.astype(BF16)
        dcm = _dot(w_all, hgb) + _dot(dcbb, bmb)
        dbm = _dot((xdt * dec_w).astype(BF16), gnb) + _dot_tn(dcbb, cmb)
        dh_off = _dot_tn(w_all, cmb)
        for r in range(SSD_HPG):
            rows = _head_cols(r)
            g_scr[rows, :] = gn[rows, :] * eh[:, r:r + 1] + dh_off[rows, :]

        pad_rows = jnp.zeros((BLOCK - SUBLANES, BLOCK), F32)
        dcs = dcs_col - jnp.concatenate([dcs_row, pad_rows], axis=0).T
        rsel = lax.broadcasted_iota(jnp.int32, (BLOCK, LANES), 0)
        dcs = dcs + jnp.where(rsel == BLOCK - 1, dcs_last, 0.0)
        ri = lax.broadcasted_iota(jnp.int32, (BLOCK, BLOCK), 0)
        ci = lax.broadcasted_iota(jnp.int32, (BLOCK, BLOCK), 1)
        dlam = _sel_dot((ci >= ri).astype(BF16), dcs)
        head = lane < SSD_HPG
        ddt = dlam * a + ddt_x
        ddtr = jnp.where(head, ddt * _sigmoid(dtr) * valid, 0.0)
        ddt_ref[...] = ddtr.astype(BF16)
        dalog = jnp.sum(jnp.where(head, dlam * lam, 0.0), axis=0, keepdims=True)
        ddtb = jnp.sum(ddtr, axis=0, keepdims=True)

        dxs_ref[...] = dxs * (sx * (1.0 + pxs * (1.0 - sx)))
        dbm_ref[...] = dbm * (sb * (1.0 + pb * (1.0 - sb)))
        dcm_ref[...] = dcm * (sc * (1.0 + pc * (1.0 - sc)))

        @pl.when(step == 0)
        def _():
            dnw_ref[...] = dnw
            ddtb_ref[...] = ddtb
            dalog_ref[...] = dalog
            ddsk_ref[...] = ddsk

        @pl.when(step > 0)
        def _():
            dnw_ref[...] += dnw
            ddtb_ref[...] += ddtb
            dalog_ref[...] += dalog
            ddsk_ref[...] += ddsk

    return _call_with_side(
        body, side, name="ssd_bwd", grid=(SSD_GROUPS, nc),
        in_specs=[xs_s, xs_s, xs_s, xs_s, b_s, c_s, lane_s, hs_s, vec_s, vec_s, wide_s, wide_s,
                  _const_spec((LANES, SSD_GW)), _const_spec((SSD_GW, LANES))],
        out_specs=[xs_s, xs_s, bc_out, bc_out, lane_s, wide_s, vec_s, vec_s, vec_s],
        out_shape=[jax.ShapeDtypeStruct((n, SSD_INNER), BF16), jax.ShapeDtypeStruct((n, SSD_INNER), F32),
                   jax.ShapeDtypeStruct((n, SSD_BC), F32), jax.ShapeDtypeStruct((n, SSD_BC), F32),
                   jax.ShapeDtypeStruct((n, DT_W), BF16), jax.ShapeDtypeStruct((1, SSD_INNER), F32),
                   jax.ShapeDtypeStruct((1, DT_W), F32), jax.ShapeDtypeStruct((1, DT_W), F32),
                   jax.ShapeDtypeStruct((1, DT_W), F32)],
        scratch_shapes=[pltpu.VMEM((SSD_GW, SSD_STATE), F32)],
        args=(dyn, y, z, pre, pre, pre, dt_raw, hsave, dtb, alog, dskip_w, norm_w, spread, spread.T))


def _bucket_table():
    def bucket(dist):
        d = np.maximum(dist, 0)
        half = REL_BUCKETS // 2
        big = half + (np.log(np.maximum(d, half).astype(np.float32) / np.float32(half))
                      / np.float32(math.log(REL_MAX_DIST / half)) * np.float32(REL_BUCKETS - half)).astype(np.int32)
        return np.where(d < half, d, np.minimum(big, REL_BUCKETS - 1)).astype(np.int32)

    l = np.arange(BLOCK)[None, :]
    band = bucket(l + BLOCK - np.arange(2 * BLOCK)[:, None])
    j = np.arange(BLOCK)[:, None]
    tables = [np.concatenate([bucket(v * BLOCK + l - j), band], axis=0) for v in range(3)]
    return np.concatenate([t.reshape(-1) for t in tables])


def _onehot_t():
    buckets = jnp.asarray(_bucket_table())
    return (buckets[None, :] == jnp.arange(REL_BUCKETS, dtype=jnp.int32)[:, None]).astype(F32)


def _bias_tables(rel_t, onehot_t, side=None):
    def body(r_ref, oh_ref, o_ref):
        o_ref[...] = jnp.dot(r_ref[...], oh_ref[...], precision=HIGHEST, preferred_element_type=F32)

    outs, carried = _call_with_side(
        body, side, name="bias_tables", grid=(NT_ALL // NT_TILE,),
        in_specs=[pl.BlockSpec((ATT_HEADS, REL_BUCKETS), lambda i: (0, 0)),
                  pl.BlockSpec((REL_BUCKETS, NT_TILE), lambda i: (0, i))],
        out_specs=[pl.BlockSpec((ATT_HEADS, NT_TILE), lambda i: (0, i))],
        out_shape=[jax.ShapeDtypeStruct((ATT_HEADS, NT_ALL), F32)], scratch_shapes=[], args=(rel_t, onehot_t),
        semantics=("parallel",))
    return outs[0], carried


def _bias_grad(dtab, onehot_t):
    def body(d_ref, oh_ref, o_ref):
        i = pl.program_id(0)
        p = lax.dot_general(d_ref[...], oh_ref[...], (((1,), (1,)), ((), ())), precision=HIGHEST,
                            preferred_element_type=F32)

        @pl.when(i == 0)
        def _():
            o_ref[...] = p

        @pl.when(i > 0)
        def _():
            o_ref[...] += p

    return pl.pallas_call(
        body, name="bias_grad", grid=(NT_ALL // NT_TILE,),
        in_specs=[pl.BlockSpec((ATT_HEADS, NT_TILE), lambda i: (0, i)),
                  pl.BlockSpec((REL_BUCKETS, NT_TILE), lambda i: (0, i))],
        out_specs=pl.BlockSpec((ATT_HEADS, REL_BUCKETS), lambda i: (0, 0)),
        out_shape=jax.ShapeDtypeStruct((ATT_HEADS, REL_BUCKETS), F32),
        compiler_params=_cparams("arbitrary"))(dtab, onehot_t)


def _att_mask_t(n, copies):
    far = 4 * BLOCK
    kk = lax.broadcasted_iota(jnp.int32, (N_KEYS, copies * BLOCK), 0)
    li = lax.broadcasted_iota(jnp.int32, (N_KEYS, copies * BLOCK), 1) & (BLOCK - 1)
    meta_ok = (kk >= PAD) & (kk < BLOCK) & (li + jnp.where(n >= 1, far, 0) >= kk)
    prev_ok = (kk >= BLOCK) & (kk < 2 * BLOCK) & (kk - BLOCK > li + jnp.where(n >= 2, 0, far))
    cur_ok = (kk >= 2 * BLOCK) & (kk - 2 * BLOCK <= li - jnp.where(n >= 1, 0, far))
    return meta_ok | prev_ok | cur_ok


def _att_kv(meta_ref, prev_ref, cur_ref):
    kv = jnp.concatenate([meta_ref[...], prev_ref[...], cur_ref[...]], axis=0)
    first = lax.broadcasted_iota(jnp.int32, (N_KEYS, LANES), 1) < ATT_HEADDIM
    out = []
    for pair in (kv[:, :LANES], kv[:, LANES:]):
        swapped = pltpu.roll(pair, ATT_HEADDIM, 1)
        out.append([jnp.where(first, pair, swapped).astype(BF16), jnp.where(first, swapped, pair).astype(BF16)])
    return out[0], out[1]


def _split_heads(x_pair, first):
    return jnp.concatenate([jnp.where(first, x_pair, 0.0), jnp.where(first, 0.0, x_pair)], axis=0).astype(BF16)


def _att_probs_t(qm2, k_dup, t_ref, j, mask2, sink_ref):
    scale = ATT_HEADDIM ** -0.5
    bias2 = jnp.concatenate([t_ref[0, 2 * j], t_ref[0, 2 * j + 1]], axis=1)
    second = lax.broadcasted_iota(jnp.int32, (1, 2 * BLOCK), 1) >= BLOCK
    sink2 = jnp.where(second, sink_ref[0:1, 2 * j + 1:2 * j + 2], sink_ref[0:1, 2 * j:2 * j + 1])
    s_t = jnp.where(mask2, _dot_nt(k_dup, qm2) * scale + bias2, NEG)
    mx = jnp.maximum(jnp.max(s_t, axis=0, keepdims=True), sink2)
    p_t = jnp.exp(s_t - mx)
    p_s = jnp.exp(sink2 - mx)
    inv = 1.0 / (jnp.sum(p_t, axis=0, keepdims=True) + p_s)
    return p_t * inv, p_s * inv


def _att_specs(nb, rev):
    def nidx(i):
        return nb - 1 - i if rev else i

    kvb = ATT_Q // (2 * ATT_KV)
    q_s = pl.BlockSpec((BLOCK, ATT_Q), lambda i: (nidx(i), 0))
    cur = pl.BlockSpec((BLOCK, 2 * ATT_KV), lambda i: (nidx(i), kvb))
    prev = pl.BlockSpec((BLOCK, 2 * ATT_KV), lambda i: (jnp.maximum(nidx(i) - 1, 0), kvb))
    meta = pl.BlockSpec((BLOCK, 2 * ATT_KV), lambda i: (0, kvb))
    table = pl.BlockSpec((1, ATT_HEADS, N_KEYS, BLOCK), lambda i: (jnp.minimum(nidx(i), 2), 0, 0, 0))
    sink = pl.BlockSpec((1, LANES), lambda i: (0, 0))
    return q_s, cur, prev, meta, table, sink


def _attn_fwd(qkv, tables, sinks):
    n = qkv.shape[0]
    nb = n // BLOCK
    q_s, cur_s, prev_s, meta_s, t_s, sink_s = _att_specs(nb, False)

    def body(q_ref, cur_ref, prev_ref, meta_ref, t_ref, sink_ref, o_ref):
        blk = pl.program_id(0)
        mask_t = _att_mask_t(blk, 1)
        k_dup, v_dup = _att_kv(meta_ref, prev_ref, cur_ref)
        v_dup_t = [v.T for v in v_dup]
        first = lax.broadcasted_iota(jnp.int32, (BLOCK, LANES), 1) < ATT_HEADDIM
        top = lax.broadcasted_iota(jnp.int32, (LANES, BLOCK), 0) < ATT_HEADDIM
        scale = ATT_HEADDIM ** -0.5
        for j in range(ATT_HEADS // 2):
            kh = 2 * j // ATT_GQ
            tile = slice(LANES * j, LANES * (j + 1))
            q_p = q_ref[:, tile]
            res = []
            for half, h in enumerate((2 * j, 2 * j + 1)):
                qm = jnp.where(first if half == 0 else jnp.logical_not(first), q_p, 0.0).astype(BF16)
                sink = sink_ref[0:1, h:h + 1]
                s_t = jnp.where(mask_t, _dot_nt(k_dup[kh], qm) * scale + t_ref[0, h], NEG)
                mx = jnp.maximum(jnp.max(s_t, axis=0, keepdims=True), sink)
                p_t = jnp.exp(s_t - mx)
                inv = 1.0 / (jnp.sum(p_t, axis=0, keepdims=True) + jnp.exp(sink - mx))
                res.append(_dot(v_dup_t[kh], (p_t * inv).astype(BF16)))
            o_ref[:, tile] = jnp.where(top, res[0], res[1]).T.astype(BF16)

    return pl.pallas_call(
        body, name="attn_fwd", grid=(nb,),
        in_specs=[q_s, cur_s, prev_s, meta_s, t_s, sink_s],
        out_specs=q_s,
        out_shape=jax.ShapeDtypeStruct((n, ATT_Q), BF16),
        compiler_params=_cparams("parallel"))(qkv, qkv, qkv, qkv, tables, sinks)


def _attn_bwd(datt, qkv, tables, sinks):
    n = qkv.shape[0]
    nb = n // BLOCK
    q_s, cur_s, prev_s, meta_s, t_s, sink_s = _att_specs(nb, True)
    dqkv_s = pl.BlockSpec((BLOCK, ATT_Q + 2 * ATT_KV), lambda i: (nb - 1 - i, 0))
    scale = ATT_HEADDIM ** -0.5

    def body(do_ref, q_ref, cur_ref, prev_ref, meta_ref, t_ref, sink_ref,
             dqkv_ref, dt_ref, dsink_ref, carry_scr, meta_scr):
        step = pl.program_id(0)
        blk = nb - 1 - step
        mask2 = _att_mask_t(blk, 2)
        k_dup, v_dup = _att_kv(meta_ref, prev_ref, cur_ref)
        k_dup_t = [k.T for k in k_dup]

        @pl.when(step == 0)
        def _():
            carry_scr[...] = jnp.zeros_like(carry_scr)
            meta_scr[...] = jnp.zeros_like(meta_scr)
            dsink_ref[...] = jnp.zeros_like(dsink_ref)

        @pl.when((step == 0) | (blk <= 1))
        def _():
            dt_ref[...] = jnp.zeros_like(dt_ref)

        first = lax.broadcasted_iota(jnp.int32, (BLOCK, LANES), 1) < ATT_HEADDIM
        top = lax.broadcasted_iota(jnp.int32, (LANES, BLOCK), 0) < ATT_HEADDIM
        first_k = lax.broadcasted_iota(jnp.int32, (N_KEYS, LANES), 1) < ATT_HEADDIM
        dsink = jnp.zeros((1, LANES), F32)
        dk_acc = [None] * ATT_KV_HEADS
        dv_acc = [None] * ATT_KV_HEADS
        for j in range(ATT_HEADS // 2):
            kh = 2 * j // ATT_GQ
            tile = slice(LANES * j, LANES * (j + 1))
            qm2 = _split_heads(q_ref[:, tile], first)
            dom2 = _split_heads(do_ref[:, tile], first)
            p_t, p_s = _att_probs_t(qm2, k_dup[kh], t_ref, j, mask2, sink_ref)
            dp_t = _dot_nt(v_dup[kh], dom2)
            delta = jnp.sum(p_t * dp_t, axis=0, keepdims=True)
            ds_t = p_t * (dp_t - delta)
            sink_terms = p_s * delta
            for half in range(2):
                cols = slice(BLOCK * half, BLOCK * (half + 1))
                dsink = _lane_put(dsink, -jnp.sum(sink_terms[:, cols], axis=1, keepdims=True), 2 * j + half)
                dt_ref[0, 2 * j + half] += ds_t[:, cols]
            ds_tb = ds_t.astype(BF16)
            dq_t = _dot(k_dup_t[kh], ds_tb)
            dqkv_ref[:, tile] = (jnp.where(top, dq_t[:, :BLOCK], dq_t[:, BLOCK:]).T * scale).astype(BF16)
            dk_part, dv_part = _dot(ds_tb, qm2), _dot(p_t.astype(BF16), dom2)
            dk_acc[kh] = dk_part if dk_acc[kh] is None else dk_acc[kh] + dk_part
            dv_acc[kh] = dv_part if dv_acc[kh] is None else dv_acc[kh] + dv_part
        dsink_ref[...] += dsink
        folded = [a + pltpu.roll(a, ATT_HEADDIM, 1) for a in dk_acc + dv_acc]
        dkv = jnp.concatenate([jnp.where(first_k, folded[0], folded[1]) * scale,
                               jnp.where(first_k, folded[2], folded[3])], axis=1)
        meta_scr[...] += dkv[:BLOCK, :]
        own = dkv[2 * BLOCK:, :] + carry_scr[...]
        carry_scr[...] = dkv[BLOCK:2 * BLOCK, :]

        @pl.when(blk > 0)
        def _():
            dqkv_ref[:, ATT_Q:] = own.astype(BF16)

        @pl.when(blk == 0)
        def _():
            dqkv_ref[:, ATT_Q:] = (own + meta_scr[...]).astype(BF16)

    return pl.pallas_call(
        body, name="attn_bwd", grid=(nb,),
        in_specs=[q_s, q_s, cur_s, prev_s, meta_s, t_s, sink_s],
        out_specs=[dqkv_s, t_s, sink_s],
        out_shape=[jax.ShapeDtypeStruct((n, ATT_Q + 2 * ATT_KV), BF16),
                   jax.ShapeDtypeStruct((3, ATT_HEADS, N_KEYS, BLOCK), F32),
                   jax.ShapeDtypeStruct((1, LANES), F32)],
        scratch_shapes=[pltpu.VMEM((BLOCK, 2 * ATT_KV), F32), pltpu.VMEM((BLOCK, 2 * ATT_KV), F32)],
        compiler_params=_cparams("arbitrary"))(datt, qkv, qkv, qkv, qkv, tables, sinks)


def _merge_out_fwd(gates, y_ssd, y_att, gate_b, w_out, h, next_norm_w):
    n = gates.shape[0]
    tm = _row_tile(n, 416)

    def body(gs_ref, ga_ref, ys_ref, ya_ref, gb_ref, w_ref, h_ref, nw_ref, m_ref, o_ref, u_ref):
        merged = (_sigmoid(gs_ref[...] + gb_ref[0:1, :]) * ys_ref[...]
                  + _sigmoid(ga_ref[...] + gb_ref[1:2, :]) * ya_ref[...]).astype(BF16)
        m_ref[...] = merged
        row = pl.program_id(0) * tm + lax.broadcasted_iota(jnp.int32, (tm, 1), 0)
        hv = jnp.where(row >= PAD, _dot(merged, w_ref[...]), 0.0) + h_ref[...]
        o_ref[...] = hv
        r = lax.rsqrt(jnp.mean(hv * hv, axis=-1, keepdims=True) + EPS)
        u_ref[...] = (hv * r * nw_ref[...]).astype(BF16)

    row = pl.BlockSpec((tm, D_MODEL), lambda i: (i, 0))
    return pl.pallas_call(
        body, name="merge_out_fwd", grid=(n // tm,),
        in_specs=[row, pl.BlockSpec((tm, D_MODEL), lambda i: (i, 1)), row, row,
                  pl.BlockSpec((2, D_MODEL), lambda i: (0, 0)), pl.BlockSpec((D_MODEL, D_MODEL), lambda i: (0, 0)), row,
                  pl.BlockSpec((1, D_MODEL), lambda i: (0, 0))],
        out_specs=[row, row, row],
        out_shape=[jax.ShapeDtypeStruct((n, D_MODEL), BF16), jax.ShapeDtypeStruct((n, D_MODEL), F32),
                   jax.ShapeDtypeStruct((n, D_MODEL), BF16)],
        compiler_params=_cparams("parallel"))(gates, gates, y_ssd, y_att, gate_b, w_out, h, next_norm_w)


def _merge_out_bwd(dh, w_out, gates, y_ssd, y_att, gate_b):
    n = gates.shape[0]
    tm = _row_tile(n, 416)

    def body(dh_ref, w_ref, gs_ref, ga_ref, ys_ref, ya_ref, gb_ref, dys_ref, dya_ref, dg_ref, dgb_ref):
        i = pl.program_id(0)
        row = i * tm + lax.broadcasted_iota(jnp.int32, (tm, 1), 0)
        dmv = jnp.where(row >= PAD, _dot_nt(dh_ref[...].astype(BF16), w_ref[...]), 0.0)
        ss =_sigmoid(gs_ref[...] + gb_ref[0:1, :])
        sa = _sigmoid(ga_ref[...] + gb_ref[1:2, :])
        dys_ref[...] = (dmv * ss).astype(BF16)
        dya_ref[...] = (dmv * sa).astype(BF16)
        dgs = dmv * ys_ref[...] * ss * (1.0 - ss)
        dga = dmv * ya_ref[...] * sa * (1.0 - sa)
        dg_ref[:, :D_MODEL] = dgs.astype(BF16)
        dg_ref[:, D_MODEL:] = dga.astype(BF16)
        part = jnp.concatenate([jnp.sum(dgs, axis=0, keepdims=True), jnp.sum(dga, axis=0, keepdims=True)], axis=0)

        @pl.when(i == 0)
        def _():
            dgb_ref[...] = part

        @pl.when(i > 0)
        def _():
            dgb_ref[...] += part

    row = pl.BlockSpec((tm, D_MODEL), lambda i: (i, 0))
    gb = pl.BlockSpec((2, D_MODEL), lambda i: (0, 0))
    return pl.pallas_call(
        body, name="merge_out_bwd", grid=(n // tm,),
        in_specs=[row, pl.BlockSpec((D_MODEL, D_MODEL), lambda i: (0, 0)), row,
                  pl.BlockSpec((tm, D_MODEL), lambda i: (i, 1)), row, row, gb],
        out_specs=[row, row, pl.BlockSpec((tm, 2 * D_MODEL), lambda i: (i, 0)), gb],
        out_shape=[jax.ShapeDtypeStruct((n, D_MODEL), BF16), jax.ShapeDtypeStruct((n, D_MODEL), BF16),
                   jax.ShapeDtypeStruct((n, 2 * D_MODEL), BF16), jax.ShapeDtypeStruct((2, D_MODEL), F32)],
        compiler_params=_cparams("arbitrary"))(dh, w_out, gates, gates, y_ssd, y_att, gate_b)


def _col_move(srcs, outs, pieces, *, name):
    rows = srcs[0].shape[-2]
    tr = _row_tile(rows, 128)
    n_src = len(srcs)
    covered = [sum(p[6] for p in pieces if p[0] == o) for o in range(len(outs))]
    total = [int(np.prod(shp)) // rows for shp, _ in outs]

    def body(*refs):
        in_refs, out_refs = refs[:n_src], refs[n_src:]
        for o, ref in enumerate(out_refs):
            if covered[o] != total[o]:
                ref[...] = jnp.zeros_like(ref)
        for o, ol, oc, s, sl, sc, width in pieces:
            val = in_refs[s][:, sc:sc + width] if sl is None else in_refs[s][sl, :, sc:sc + width]
            val = val.astype(outs[o][1])
            if ol is None:
                out_refs[o][:, oc:oc + width] = val
            else:
                out_refs[o][ol, :, oc:oc + width] = val

    def spec(shape):
        if len(shape) == 2:
            return pl.BlockSpec((tr, shape[1]), lambda i: (i, 0))
        return pl.BlockSpec((shape[0], tr, shape[2]), lambda i: (0, i, 0))

    return pl.pallas_call(
        body, name=name, grid=(rows // tr,),
        in_specs=[spec(a.shape) for a in srcs], out_specs=[spec(shp) for shp, _ in outs],
        out_shape=[jax.ShapeDtypeStruct(shp, dt) for shp, dt in outs],
        compiler_params=_cparams("parallel"))(*srcs)


def _shard_pieces(seg_ranges, shard_w):
    out = []
    for seg, runs in enumerate(seg_ranges):
        for g0, width, s0 in runs:
            done = 0
            while done < width:
                dev, col = divmod(g0 + done, shard_w)
                take = min(width - done, shard_w - col)
                out.append((seg, s0 + done, dev, col, take))
                done += take
    return out


_CHIP_RELATIONS = [(1, 0, 0), (0, 1, 0), (1, 1, 0)]
N_CHIPS = 4


class _CommPlan:
    def __init__(self, arrays, out_shape, scratch_shapes, phases):
        self.arrays, self.out_shape, self.scratch_shapes, self.phases = arrays, out_shape, scratch_shapes, phases


def _gather_plan(arrays):
    n_arr = len(arrays)
    n_chips = len(_CHIP_RELATIONS)
    n_pair = 1 + 2 * n_chips

    def where():
        x, y, c = lax.axis_index("x"), lax.axis_index("y"), lax.axis_index("c")
        return x, y, c, (x, y, 1 - c), [(x ^ dx, y ^ dy) for dx, dy, _ in _CHIP_RELATIONS]

    def copy(outs, sems, a, k, block, to, src=None):
        slot = outs[a].at[2 * block[0] + block[1], block[2]]
        return pltpu.make_async_remote_copy(
            src_ref=slot if src is None else src, dst_ref=slot, send_sem=sems[0].at[a * n_pair + k],
            recv_sem=sems[1].at[a * n_pair + k], device_id=to, device_id_type=MESH)

    def mine(ins, outs, sems, a, x, y, c):
        return pltpu.make_async_copy(ins[a], outs[a].at[2 * x + y, c], sems[2].at[a])

    def first_copies(ins, outs, sems, a, x, y, c, sibling, chips):
        return ([copy(outs, sems, a, 0, (x, y, c), sibling, src=ins[a])]
                + [copy(outs, sems, a, 1 + j, (x, y, c), (*chip, c), src=ins[a]) for j, chip in enumerate(chips)])

    def start(ins, outs, sems):
        x, y, c, sibling, chips = where()
        for a in range(n_arr):
            mine(ins, outs, sems, a, x, y, c).start()
            for cp in first_copies(ins, outs, sems, a, x, y, c, sibling, chips):
                cp.start()

    def pass_on(ins, outs, sems):
        x, y, c, sibling, chips = where()
        for j, chip in enumerate(chips):
            for a in range(n_arr):
                copy(outs, sems, a, 1 + j, (*chip, c), (x, y, c)).wait_recv()
                copy(outs, sems, a, 1 + n_chips + j, (*chip, c), sibling).start()

    def finish(ins, outs, sems):
        x, y, c, sibling, chips = where()
        for a in range(n_arr):
            copy(outs, sems, a, 0, (x, y, 1 - c), (x, y, c)).wait_recv()
            for j, chip in enumerate(chips):
                copy(outs, sems, a, 1 + n_chips + j, (*chip, 1 - c), (x, y, c)).wait_recv()
        for a in range(n_arr):
            for cp in first_copies(ins, outs, sems, a, x, y, c, sibling, chips):
                cp.wait_send()
            for j, chip in enumerate(chips):
                copy(outs, sems, a, 1 + n_chips + j, (*chip, c), sibling).wait_send()
            mine(ins, outs, sems, a, x, y, c).wait()

    return _CommPlan(
        arrays, [jax.ShapeDtypeStruct((N_CHIPS, 2) + a.shape, a.dtype) for a in arrays],
        [pltpu.SemaphoreType.DMA((n_arr * n_pair,)), pltpu.SemaphoreType.DMA((n_arr * n_pair,)),
         pltpu.SemaphoreType.DMA((n_arr,))],
        (start, pass_on, finish))


_ALL_RELATIONS = [(dx, dy, dc) for dx in (0, 1) for dy in (0, 1) for dc in (0, 1)][1:]


def _all_to_all_plan(arrays, scatter=None):
    n_arr = len(arrays)
    n_rel = len(_ALL_RELATIONS)
    scatter = scatter or [True] * n_arr

    def block(ins, a, p):
        return ins[a].at[p] if scatter[a] else ins[a]

    def local_copies(ins, outs, sems):
        me = 4 * lax.axis_index("x") + 2 * lax.axis_index("y") + lax.axis_index("c")
        return [pltpu.make_async_copy(block(ins, a, me), outs[a].at[me], sems[2].at[a]) for a in range(n_arr)]

    def remote_copies(ins, outs, sems, arrivals):
        x, y, c = lax.axis_index("x"), lax.axis_index("y"), lax.axis_index("c")
        me = 4 * x + 2 * y + c
        out = []
        for k, (dx, dy, dc) in enumerate(_ALL_RELATIONS):
            px, py, pc = x ^ dx, y ^ dy, c ^ dc
            peer = 4 * px + 2 * py + pc
            for a in range(n_arr):
                out.append(pltpu.make_async_remote_copy(
                    src_ref=block(ins, a, peer), dst_ref=outs[a].at[peer if arrivals else me],
                    send_sem=sems[0].at[a * n_rel + k], recv_sem=sems[1].at[a * n_rel + k],
                    device_id=(x, y, c) if arrivals else (px, py, pc), device_id_type=MESH))
        return out

    def start(ins, outs, sems):
        for cp in local_copies(ins, outs, sems) + remote_copies(ins, outs, sems, False):
            cp.start()

    def pass_on(ins, outs, sems):
        pass

    def finish(ins, outs, sems):
        for send in remote_copies(ins, outs, sems, False):
            send.wait_send()
        for arrival in remote_copies(ins, outs, sems, True):
            arrival.wait_recv()
        for cp in local_copies(ins, outs, sems):
            cp.wait()

    return _CommPlan(
        arrays, [jax.ShapeDtypeStruct(a.shape if s else (N_DEV,) + a.shape, a.dtype) for a, s in zip(arrays, scatter)],
        [pltpu.SemaphoreType.DMA((n_arr * n_rel,)), pltpu.SemaphoreType.DMA((n_arr * n_rel,)),
         pltpu.SemaphoreType.DMA((n_arr,))],
        (start, pass_on, finish))


def _adamw(w, gslots, m, v, *, name, side=None):
    rows, cols = w.shape
    n_slots = gslots.shape[0]
    tr = _row_tile(rows, 128) if rows % 16 == 0 else rows

    def body(w_ref, g_ref, m_ref, v_ref, go_ref, d_ref, mo_ref, vo_ref):
        g = g_ref[0].astype(F32)
        for s in range(1, n_slots):
            g = g + g_ref[s].astype(F32)
        mn = ADAM_B1 * m_ref[...] + (1.0 - ADAM_B1) * g
        vn = ADAM_B2 * v_ref[...] + (1.0 - ADAM_B2) * (g * g)
        go_ref[...] = g
        mo_ref[...] = mn
        vo_ref[...] = vn
        m_hat = mn / (1.0 - ADAM_B1 ** ADAM_STEP)
        v_hat = vn / (1.0 - ADAM_B2 ** ADAM_STEP)
        d_ref[...] = -ADAM_LR * (m_hat / (jnp.sqrt(v_hat) + ADAM_EPS) + ADAM_WD * w_ref[...])

    blk = pl.BlockSpec((tr, cols), lambda i: (i, 0))
    shp = jax.ShapeDtypeStruct((rows, cols), F32)
    outs, carried = _call_with_side(
        body, side, name=name, grid=(rows // tr,),
        in_specs=[blk, pl.BlockSpec((n_slots, tr, cols), lambda i: (0, i, 0)), blk, blk],
        out_specs=[blk] * 4, out_shape=[shp] * 4, scratch_shapes=[], args=(w, gslots, m, v), semantics=("parallel",))
    return outs if side is None else (outs, carried)


_BIG = ("w_in", "w_ssd_branch", "w_attn_branch", "w_out", "w_ffn_in", "w_ffn_out")
_SMALL_SHARDED = ("meta_tokens", "ssd_conv_w", "gate_b", "ffn_conv_w")
_SMALL_REPLICATED = ("norm_mix_w", "ssd_conv_b", "ssd_dt_bias", "ssd_a_log", "ssd_d", "ssd_norm_w", "attn_sinks",
                     "rel_bias", "norm_ffn_w", "ffn_conv_b", "norm_final_w")
_WEIGHTS = ("meta_tokens", "norm_mix_w", "w_in", "ssd_conv_w", "ssd_conv_b", "ssd_dt_bias", "ssd_a_log", "ssd_d",
            "ssd_norm_w", "w_ssd_branch", "w_attn_branch", "attn_sinks", "rel_bias", "gate_b", "w_out", "norm_ffn_w",
            "w_ffn_in", "ffn_conv_w", "ffn_conv_b", "w_ffn_out", "norm_final_w")
_ROW_SHARDED = ("w_ssd_branch", "w_attn_branch", "w_out", "w_ffn_out")
_COL_SHARDED = ("w_in", "w_ffn_in", "meta_tokens", "ssd_conv_w", "gate_b", "ffn_conv_w")
_IN_SEGS = (("z", SSD_INNER), ("xbc", SSD_XBC), ("dt", SSD_HEADS), ("qkv", ATT_Q + 2 * ATT_KV), ("g", 2 * D_MODEL))


def _pack_rows(flat_parts, width, row_mult):
    flat = jnp.concatenate([p.reshape(-1) for p in flat_parts])
    pad = (-flat.shape[0]) % (width * row_mult)
    if pad:
        flat = jnp.concatenate([flat, jnp.zeros((pad,), flat.dtype)])
    return flat.reshape(-1, width)


def _unpack(flat, shapes):
    out, off = [], 0
    for shp in shapes:
        size = int(np.prod(shp))
        out.append(flat[off:off + size].reshape(shp))
        off += size
    return out


def _gather_full(stack, name, shard_shape):
    if name in _COL_SHARDED:
        return jnp.transpose(stack, (1, 0, 2)).reshape(shard_shape[0], N_DEV * shard_shape[1])
    return stack.reshape(N_DEV * shard_shape[0], shard_shape[1])


_IN_SEG_W = {"z": SSD_INNER, "xbc": SSD_XBC, "dt": DT_W, "qkv": ATT_Q + 2 * ATT_KV, "g": 2 * D_MODEL}
_IN_SHARD_W = (SSD_INNER + SSD_XBC + SSD_HEADS + ATT_Q + 2 * ATT_KV + 2 * D_MODEL) // N_DEV
_FFN_SHARD_W = 2 * D_FF // N_DEV


def _in_seg_runs():
    runs, off = [], 0
    for nm, width in _IN_SEGS:
        if nm == "dt":
            runs.append([(off + SSD_HPG * g, SSD_HPG, LANES * g) for g in range(SSD_GROUPS)])
        else:
            runs.append([(off, width, 0)])
        off += width
    return runs


def _w_in_to_segments(stack):
    pieces = [(seg, None, scol, 0, dev, col, w) for seg, scol, dev, col, w in _shard_pieces(_in_seg_runs(), _IN_SHARD_W)]
    outs = [((D_MODEL, _IN_SEG_W[nm]), stack.dtype) for nm, _ in _IN_SEGS]
    return dict(zip([nm for nm, _ in _IN_SEGS], _col_move([stack], outs, pieces, name="w_in_segments")))


def _segments_to_w_in_shards(seg_grads):
    pieces = [(0, dev, col, seg, None, scol, w) for seg, scol, dev, col, w in _shard_pieces(_in_seg_runs(), _IN_SHARD_W)]
    return _col_move(seg_grads, [((N_DEV, D_MODEL, _IN_SHARD_W), seg_grads[0].dtype)], pieces, name="g_w_in_shards")[0]


def _ffn_in_from_shards(stack):
    pieces = [(0, None, scol, 0, dev, col, w)
              for _, scol, dev, col, w in _shard_pieces([[(0, 2 * D_FF, 0)]], _FFN_SHARD_W)]
    return _col_move([stack], [((D_MODEL, 2 * D_FF), stack.dtype)], pieces, name="w_ffn_in_full")[0]


def _ffn_in_to_shards(g_up, g_gate):
    pieces = [(0, dev, col, seg, None, scol, w)
              for seg, scol, dev, col, w in _shard_pieces([[(0, D_FF, 0)], [(D_FF, D_FF, 0)]], _FFN_SHARD_W)]
    return _col_move([g_up, g_gate], [((N_DEV, D_MODEL, _FFN_SHARD_W), g_up.dtype)], pieces, name="g_w_ffn_in_shards")[0]


def _dt_spread(w_dt):
    k = w_dt.shape[0]
    w4 = w_dt.reshape(k, SSD_GROUPS, SSD_HPG)
    return jnp.pad(w4, ((0, 0), (0, 0), (0, LANES - SSD_HPG))).reshape(k, DT_W)


def _dt_gather(w_wide):
    k = w_wide.shape[0]
    return w_wide.reshape(k, SSD_GROUPS, LANES)[:, :, :SSD_HPG].reshape(k, SSD_HEADS)


class _LateExchanges:
    def __init__(self, two_d, shape2):
        self.two_d, self.shape2 = two_d, shape2
        self.early_grads_received = None
        self.w_in_grads_received = None

    def row_pack(self, tree):
        return jnp.concatenate([tree[k] for k in _ROW_SHARDED], axis=0)

    def late_weights_plan(self):
        return _gather_plan([self.two_d["w_ffn_in"].astype(BF16), self.row_pack(self.two_d).astype(BF16)])

    def late_weights(self, gathered):
        w_ffn_in_all, rows_all = [g.reshape((N_DEV,) + g.shape[2:]) for g in gathered]
        out = {"w_ffn_in": _ffn_in_from_shards(w_ffn_in_all)}
        off = 0
        for k in _ROW_SHARDED:
            r = self.shape2[k][0]
            out[k] = rows_all[:, off:off + r].reshape(N_DEV * r, D_MODEL)
            off += r
        return out

    def early_grads_plan(self, grads):
        rows_send = jnp.concatenate([grads[k].reshape(N_DEV, self.shape2[k][0], D_MODEL) for k in _ROW_SHARDED], axis=1)
        return _all_to_all_plan([_ffn_in_to_shards(*grads["w_ffn_in"]), rows_send])

    def w_in_grads_plan(self, seg_grads):
        return _all_to_all_plan([_segments_to_w_in_shards(seg_grads)])


def _local_step(x, target, w, exchanges=None):
    h0 = jnp.concatenate([jnp.zeros((PAD, D_MODEL), F32), w["meta_tokens"], x], axis=0)
    segs = w["in_segs"]

    dtb = _dt_spread(w["ssd_dt_bias"])
    alog = _dt_spread(w["ssd_a_log"])
    dskip_w = jnp.repeat(w["ssd_d"], SSD_HEADDIM, axis=1)
    sinks = jnp.pad(w["attn_sinks"], ((0, 0), (0, LANES - ATT_HEADS)))
    onehot_t = w["onehot_t"] if "onehot_t" in w else _onehot_t()
    flat_tables = w["bias_tables"] if "bias_tables" in w else _bias_tables(w["rel_bias"].T, onehot_t)[0]
    tables = jnp.transpose(flat_tables.reshape(ATT_HEADS, 3, N_KEYS, BLOCK), (1, 0, 2, 3))

    u = _rms_fwd(h0, w["norm_mix_w"], name="rms_mix_fwd")
    z = _mm(u, segs["z"], name="in_z")
    xbc, pre = _mm_conv_fwd(u, segs["xbc"], w["ssd_conv_w"], w["ssd_conv_b"], name="in_xbc_conv_fwd")
    dt_raw = _mm(u, segs["dt"], name="in_dt")
    qkv = _mm(u, segs["qkv"], out_dtype=BF16, name="in_qkv")
    gates = _mm(u, segs["g"], name="in_g")
    (y, yn, hsave), gathered = _ssd_fwd(pre, dt_raw, z, dtb, alog, dskip_w, w["ssd_norm_w"],
                                        side=None if exchanges is None else exchanges.late_weights_plan())
    if exchanges is not None:
        w = {**w, **exchanges.late_weights(gathered)}
    w_ffn_up, w_ffn_gate = w["w_ffn_in"][:, :D_FF], w["w_ffn_in"][:, D_FF:]
    y_ssd = _mm(yn, w["w_ssd_branch"], out_dtype=BF16, name="ssd_out")
    att = _attn_fwd(qkv, tables, sinks)
    y_att = _mm(att, w["w_attn_branch"], out_dtype=BF16, name="att_out")
    merged, h1, u2 = _merge_out_fwd(gates, y_ssd, y_att, w["gate_b"], w["w_out"], h0, w["norm_ffn_w"])
    x_up, x_gate, hid_up, hid_gate, act = _ffn_in_act_fwd(u2, w["w_ffn_in"], w["ffn_conv_w"], w["ffn_conv_b"])
    dh2, dh2_b, loss_row, g_norm_final = _ffn_out_loss(act, w["w_ffn_out"], h1, w["norm_final_w"], target)

    grads = {"norm_final_w": g_norm_final}
    grads["w_ffn_out"] = _mm(act, dh2_b, ta=True, mask=True, out_dtype=BF16, name="g_w_ffn_out")
    dx_up, dx_gate, dcw_up, dcw_gate, dcb_up, dcb_gate = _ffn_out_act_bwd(
        dh2_b, w["w_ffn_out"], hid_up, hid_gate, x_up, x_gate, w["ffn_conv_w"])
    grads["ffn_conv_w"] = jnp.concatenate([dcw_up, dcw_gate], axis=1)
    grads["ffn_conv_b"] = jnp.concatenate([dcb_up, dcb_gate], axis=1)
    (dh1, grads["norm_ffn_w"]), _ = _mm_rms_bwd([(dx_up, w_ffn_up), (dx_gate, w_ffn_gate)], h1, w["norm_ffn_w"], dh2,
                                                name="d_u2_rms_bwd")
    grads["w_ffn_in"] = (_mm(u2, dx_up, ta=True, out_dtype=BF16, name="g_w_ffn_up"),
                         _mm(u2, dx_gate, ta=True, out_dtype=BF16, name="g_w_ffn_gate"))

    grads["w_out"] = _mm(merged, dh1, ta=True, mask=True, out_dtype=BF16, name="g_w_out")
    dy_ssd, dy_att, dgates, grads["gate_b"] = _merge_out_bwd(dh1, w["w_out"], gates, y_ssd, y_att, w["gate_b"])
    dyn = _mm(dy_ssd, w["w_ssd_branch"], tb=True, name="d_yn")
    grads["w_ssd_branch"] = _mm(yn, dy_ssd, ta=True, out_dtype=BF16, name="g_w_ssd")
    datt = _mm(dy_att, w["w_attn_branch"], tb=True, out_dtype=BF16, name="d_att")
    grads["w_attn_branch"] = _mm(att, dy_att, ta=True, out_dtype=BF16, name="g_w_att")
    (dz, dpxs, dpb, dpc, ddt, grads["ssd_norm_w"], g_dtb, g_alog, g_dskip), received = _ssd_bwd(
        dyn, y, z, pre, dt_raw, hsave, dtb, alog, dskip_w, w["ssd_norm_w"],
        side=None if exchanges is None else exchanges.early_grads_plan(grads))
    if exchanges is not None:
        exchanges.early_grads_received = received
    grads["ssd_dt_bias"] = _dt_gather(g_dtb)
    grads["ssd_a_log"] = _dt_gather(g_alog)
    grads["ssd_d"] = _dt_gather(g_dskip)
    conv_g = _conv_bwd(dpxs, xbc, w["ssd_conv_w"], name="ssd_conv_bwd_x")
    conv_g = _conv_bwd(dpb, xbc, w["ssd_conv_w"], name="ssd_conv_bwd_b", col0=SSD_INNER, into=conv_g)
    dxbc, grads["ssd_conv_w"], grads["ssd_conv_b"] = _conv_bwd(
        dpc, xbc, w["ssd_conv_w"], name="ssd_conv_bwd_c", col0=SSD_INNER + SSD_BC, into=conv_g)
    dqkv, d_tables, d_sinks = _attn_bwd(datt, qkv, tables, sinks)
    grads["attn_sinks"] = d_sinks[:, :ATT_HEADS]
    dtab = jnp.transpose(d_tables, (1, 0, 2, 3)).reshape(ATT_HEADS, NT_ALL)
    grads["rel_bias"] = _bias_grad(dtab, onehot_t).T
    dsegs = {"z": dz, "xbc": dxbc, "dt": ddt, "qkv": dqkv, "g": dgates}
    grads["in_segs"] = [_mm(u, dsegs[nm], ta=True, out_dtype=BF16, name="g_w_in_" + nm) for nm, _ in _IN_SEGS]
    (dh0, grads["norm_mix_w"]), received = _mm_rms_bwd(
        [(dsegs[nm], segs[nm]) for nm, _ in _IN_SEGS], h0, w["norm_mix_w"], dh1, name="d_u_rms_bwd",
        side=None if exchanges is None else exchanges.w_in_grads_plan(grads["in_segs"]))
    if exchanges is not None:
        exchanges.w_in_grads_received = received[0]
    grads["meta_tokens"] = dh0[PAD:BLOCK]
    return loss_row[0, 0], dh0[BLOCK:], grads


def kernel(x, meta_tokens, norm_mix_w, w_in, ssd_conv_w, ssd_conv_b, ssd_dt_bias, ssd_a_log, ssd_d, ssd_norm_w, w_ssd_branch, w_attn_branch, attn_sinks, rel_bias, gate_b, w_out, norm_ffn_w, w_ffn_in, ffn_conv_w, ffn_conv_b, w_ffn_out, norm_final_w, loss_target, m_meta_tokens, m_norm_mix_w, m_w_in, m_ssd_conv_w, m_ssd_conv_b, m_ssd_dt_bias, m_ssd_a_log, m_ssd_d, m_ssd_norm_w, m_w_ssd_branch, m_w_attn_branch, m_attn_sinks, m_rel_bias, m_gate_b, m_w_out, m_norm_ffn_w, m_w_ffn_in, m_ffn_conv_w, m_ffn_conv_b, m_w_ffn_out, m_norm_final_w, v_meta_tokens, v_norm_mix_w, v_w_in, v_ssd_conv_w, v_ssd_conv_b, v_ssd_dt_bias, v_ssd_a_log, v_ssd_d, v_ssd_norm_w, v_w_ssd_branch, v_w_attn_branch, v_attn_sinks, v_rel_bias, v_gate_b, v_w_out, v_norm_ffn_w, v_w_ffn_in, v_ffn_conv_w, v_ffn_conv_b, v_w_ffn_out, v_norm_final_w):
    shard = dict(meta_tokens=meta_tokens, norm_mix_w=norm_mix_w, w_in=w_in, ssd_conv_w=ssd_conv_w,
                 ssd_conv_b=ssd_conv_b, ssd_dt_bias=ssd_dt_bias, ssd_a_log=ssd_a_log, ssd_d=ssd_d,
                 ssd_norm_w=ssd_norm_w, w_ssd_branch=w_ssd_branch, w_attn_branch=w_attn_branch,
                 attn_sinks=attn_sinks, rel_bias=rel_bias, gate_b=gate_b, w_out=w_out, norm_ffn_w=norm_ffn_w,
                 w_ffn_in=w_ffn_in, ffn_conv_w=ffn_conv_w, ffn_conv_b=ffn_conv_b, w_ffn_out=w_ffn_out,
                 norm_final_w=norm_final_w)
    mom_m = dict(zip(_WEIGHTS, (m_meta_tokens, m_norm_mix_w, m_w_in, m_ssd_conv_w, m_ssd_conv_b, m_ssd_dt_bias,
                                m_ssd_a_log, m_ssd_d, m_ssd_norm_w, m_w_ssd_branch, m_w_attn_branch, m_attn_sinks,
                                m_rel_bias, m_gate_b, m_w_out, m_norm_ffn_w, m_w_ffn_in, m_ffn_conv_w, m_ffn_conv_b,
                                m_w_ffn_out, m_norm_final_w)))
    mom_v = dict(zip(_WEIGHTS, (v_meta_tokens, v_norm_mix_w, v_w_in, v_ssd_conv_w, v_ssd_conv_b, v_ssd_dt_bias,
                                v_ssd_a_log, v_ssd_d, v_ssd_norm_w, v_w_ssd_branch, v_w_attn_branch, v_attn_sinks,
                                v_rel_bias, v_gate_b, v_w_out, v_norm_ffn_w, v_w_ffn_in, v_ffn_conv_w, v_ffn_conv_b,
                                v_w_ffn_out, v_norm_final_w)))
    orig_shape = {k: a.shape for k, a in shard.items()}
    two_d = {k: a.reshape(a.shape[-2:]) if a.ndim >= 2 else a.reshape(1, -1) for k, a in shard.items()}
    shape2 = {k: a.shape for k, a in two_d.items()}

    def as2d(tree):
        return {k: tree[k].reshape(shape2[k]) for k in _WEIGHTS}

    mom_m, mom_v = as2d(mom_m), as2d(mom_v)

    exchanges = _LateExchanges(two_d, shape2)
    row_pack = exchanges.row_pack
    small_pack = _pack_rows([two_d[k] for k in _SMALL_SHARDED], LANES, SMALL_ROW_MULT)
    onehot_t = _onehot_t()
    gather = [two_d["w_in"].astype(BF16), small_pack]
    flat_tables, gathered = _bias_tables(two_d["rel_bias"].T, onehot_t, side=_gather_plan(gather))
    w_in_all, small_all = [g.reshape((N_DEV,) + a.shape) for g, a in zip(gathered, gather)]
    full = {k: two_d[k] for k in _SMALL_REPLICATED}
    full["onehot_t"], full["bias_tables"] = onehot_t, flat_tables
    full["in_segs"] = _w_in_to_segments(w_in_all)
    small_flat = small_all.reshape(N_DEV, -1)
    off = 0
    for k in _SMALL_SHARDED:
        size = int(np.prod(shape2[k]))
        full[k] = _gather_full(small_flat[:, off:off + size].reshape((N_DEV,) + shape2[k]), k, shape2[k])
        off += size

    loss_local, grad_x, grads = _local_step(x[0], loss_target[0], full, exchanges)

    small_names = _SMALL_SHARDED + _SMALL_REPLICATED
    small_send = _pack_rows([grads[k] for k in small_names] + [loss_local.reshape(1)], LANES, SMALL_ROW_MULT)
    in_recv = exchanges.w_in_grads_received
    ffn_recv, rows_recv = exchanges.early_grads_received

    w_in_out, (small_recv,) = _adamw(two_d["w_in"], in_recv, mom_m["w_in"], mom_v["w_in"], name="adamw_w_in",
                                     side=_all_to_all_plan([small_send], [False]))
    big = {"w_in": w_in_out,
           "w_ffn_in": _adamw(two_d["w_ffn_in"], ffn_recv, mom_m["w_ffn_in"], mom_v["w_ffn_in"], name="adamw_w_ffn_in")}
    rows_out = _adamw(row_pack(two_d), rows_recv, row_pack(mom_m), row_pack(mom_v), name="adamw_rows")
    off = 0
    for k in _ROW_SHARDED:
        r = shape2[k][0]
        big[k] = [a[off:off + r] for a in rows_out]
        off += r
    me =4 * lax.axis_index("x") + 2 * lax.axis_index("y") + lax.axis_index("c")
    small_full_shapes = [grads[k].shape for k in small_names]
    n_small = sum(int(np.prod(s)) for s in small_full_shapes)

    def packed_small(tree):
        parts = []
        for k in small_names:
            a = tree[k]
            if k in _SMALL_SHARDED:
                fullw = jnp.zeros(grads[k].shape, F32)
                a = lax.dynamic_update_slice(fullw, a, (0, me * a.shape[1]))
            parts.append(a)
        return _pack_rows(parts + [jnp.zeros((1,), F32)], LANES, SMALL_ROW_MULT)

    g_small, d_small, m_small, v_small = _adamw(packed_small(two_d), small_recv, packed_small(mom_m),
                                                packed_small(mom_v), name="adamw_small")

    def unpack_all(which, small):
        out = {k: big[k][which] for k in _BIG}
        flat = small.reshape(-1)
        for k, a in zip(small_names, _unpack(flat, small_full_shapes)):
            if k in _SMALL_SHARDED:
                a = lax.dynamic_slice(a, (0, me * shape2[k][1]), shape2[k])
            out[k] = a
        return out, flat[n_small]

    g_all, loss = unpack_all(0, g_small)
    d_all, _ = unpack_all(1, d_small)
    m_all, _ = unpack_all(2, m_small)
    v_all, _ = unpack_all(3, v_small)

    def final(tree):
        return [tree[k].reshape(orig_shape[k]) for k in _WEIGHTS]

    return (loss, grad_x[None], *final(g_all), *final(d_all), *final(m_all), *final(v_all))
```

```python
import functools
import math

import numpy as np
import jax
import jax.numpy as jnp
from jax import lax
from jax.experimental import pallas as pl
from jax.experimental.pallas import tpu as pltpu

F32 = jnp.float32
BF16 = jnp.bfloat16
HIGHEST = lax.Precision.HIGHEST

D_MODEL = 1024
N_META = 16
BLOCK = 128
PAD = BLOCK - N_META
EPS = 1e-6
NEG = -1e30
SSD_INNER = 2 * D_MODEL
SSD_HEADDIM = 64
SSD_HEADS = SSD_INNER // SSD_HEADDIM
SSD_GROUPS = 4
SSD_HPG = SSD_HEADS // SSD_GROUPS
SSD_STATE = 128
SSD_CONV = 4
SSD_GW = SSD_HPG * SSD_HEADDIM
SSD_BC = SSD_GROUPS * SSD_STATE
SSD_XBC = SSD_INNER + 2 * SSD_BC
ATT_HEADS = 16
ATT_KV_HEADS = 2
ATT_HEADDIM = 64
ATT_GQ = ATT_HEADS // ATT_KV_HEADS
ATT_Q = ATT_HEADS * ATT_HEADDIM
ATT_KV = ATT_KV_HEADS * ATT_HEADDIM
REL_BUCKETS = 32
REL_MAX_DIST = 128
D_FF = 2816
FFN_CONV = 3
ADAM_LR = 0.001
ADAM_B1 = 0.9
ADAM_B2 = 0.999
ADAM_EPS = 1e-08
ADAM_WD = 0.01
ADAM_STEP = 10

N_DEV = 8
LANES = 128
SUBLANES = 8
BF16_ROWS = 16
DT_W = SSD_GROUPS * LANES
VMEM_LIMIT_BYTES = 56 * 1024 * 1024
MESH = pl.DeviceIdType.MESH

SMALL_ROW_MULT = 16

N_KEYS = 3 * BLOCK
NT_ALL = 3 * N_KEYS * BLOCK
NT_TILE = 8192


def _cparams(*sem):
    return pltpu.CompilerParams(dimension_semantics=sem, vmem_limit_bytes=VMEM_LIMIT_BYTES)


def _row_tile(n, cap):
    best = None
    for t in range(16, min(n, cap) + 1, 16):
        if n % t == 0:
            best = t
    return best or n


def _col_tile(n, cap):
    for t in (1408, 1280, 1024, 768, 640, 512, 384, 256, 128):
        if t <= cap and n % t == 0:
            return t
    return n


def _sigmoid(x):
    return 0.5 * jnp.tanh(0.5 * x) + 0.5


def _silu(x):
    return x * _sigmoid(x)


def _softplus(x):
    return jnp.maximum(x, 0.0) + jnp.log(1.0 + jnp.exp(-jnp.abs(x)))


def _dot_nt(a, b):
    return lax.dot_general(a, b, (((1,), (1,)), ((), ())), preferred_element_type=F32)


def _dot_tn(a, b):
    return lax.dot_general(a, b, (((0,), (0,)), ((), ())), preferred_element_type=F32)


def _dot(a, b):
    return jnp.dot(a, b, preferred_element_type=F32)


def _bf16_terms(x, terms):
    out, rest = [], x
    for _ in range(terms):
        part = rest.astype(BF16)
        out.append(part)
        rest = rest - part.astype(F32)
    return out


def _dot_sel(x, sel, terms=3):
    return sum(_dot(part, sel) for part in _bf16_terms(x, terms))


def _sel_dot(sel, x, terms=3):
    return sum(_dot(sel, part) for part in _bf16_terms(x, terms))


def _sum_all(x):
    return jnp.sum(jnp.sum(x, axis=1, keepdims=True), axis=0, keepdims=True)


MM_ROW_CAPS = (2080, 1664, 832, 416)
MM_COL_CAP = 1408
MM_VMEM_BUDGET = 50 * 1024 * 1024


def _mm_tiles(rows, cols, vmem_bytes):
    col_cands = [t for t in (2048, 1536, 1408, 1280, 1024, 768, 640, 512, 384, 256, 128) if cols % t == 0]
    if cols <= 2 * MM_COL_CAP:
        col_cands.append(cols)
    best = None
    for cap in MM_ROW_CAPS:
        tr = _row_tile(rows, cap)
        for tc in col_cands:
            if vmem_bytes(tr, tc) <= MM_VMEM_BUDGET and (best is None or tr * tc > best[0] * best[1]):
                best = (tr, tc)
    assert best is not None, (rows, cols)
    return best


def _mm(a, b, *, name, ta=False, tb=False, c=None, mask=False, out_dtype=F32):
    if not ta:
        m, k = a.shape
        n = b.shape[0] if tb else b.shape[1]
        tm, tn = _mm_tiles(m, n, lambda t_m, t_n: 2 * (t_m * k * a.dtype.itemsize + k * t_n * b.dtype.itemsize
                                                       + t_m * t_n * (jnp.dtype(out_dtype).itemsize
                                                                      + (0 if c is None else c.dtype.itemsize)))
                           + 4 * t_m * t_n)

        def body(*refs):
            if c is None:
                a_ref, b_ref, o_ref = refs
            else:
                a_ref, b_ref, c_ref, o_ref = refs
            acc = (_dot_nt if tb else _dot)(a_ref[...].astype(BF16), b_ref[...].astype(BF16))
            if mask:
                row = pl.program_id(0) * tm + lax.broadcasted_iota(jnp.int32, (tm, 1), 0)
                acc = jnp.where(row >= PAD, acc, 0.0)
            if c is not None:
                acc = acc + c_ref[...]
            o_ref[...] = acc.astype(out_dtype)

        b_spec = pl.BlockSpec((tn, k), lambda i, j: (j, 0)) if tb else pl.BlockSpec((k, tn), lambda i, j: (0, j))
        in_specs = [pl.BlockSpec((tm, k), lambda i, j: (i, 0)), b_spec]
        args = [a, b]
        if c is not None:
            in_specs.append(pl.BlockSpec((tm, tn), lambda i, j: (i, j)))
            args.append(c)
        return pl.pallas_call(
            body, name=name, grid=(m // tm, n // tn), in_specs=in_specs,
            out_specs=pl.BlockSpec((tm, tn), lambda i, j: (i, j)),
            out_shape=jax.ShapeDtypeStruct((m, n), out_dtype),
            compiler_params=_cparams("parallel", "parallel"))(*args)

    kc, m = a.shape
    n = b.shape[1]
    tm = _col_tile(m, MM_COL_CAP)
    tk, tn = _mm_tiles(kc, n, lambda t_k, t_n: 2 * (t_k * tm * a.dtype.itemsize + t_k * t_n * b.dtype.itemsize
                                                    + tm * t_n * jnp.dtype(out_dtype).itemsize) + 8 * tm * t_n)

    n_k = kc // tk

    def body_t(a_ref, b_ref, o_ref, acc_ref):
        kk = pl.program_id(2)
        bb = b_ref[...]
        if mask:
            row = kk * tk + lax.broadcasted_iota(jnp.int32, (tk, 1), 0)
            bb = jnp.where(row >= PAD, bb, jnp.zeros_like(bb))
        p = _dot_tn(a_ref[...].astype(BF16), bb.astype(BF16))

        @pl.when(kk == 0)
        def _():
            acc_ref[...] = p

        @pl.when(kk > 0)
        def _():
            acc_ref[...] += p

        @pl.when(kk == n_k - 1)
        def _():
            o_ref[...] = acc_ref[...].astype(out_dtype)

    return pl.pallas_call(
        body_t, name=name, grid=(m // tm, n // tn, n_k),
        in_specs=[pl.BlockSpec((tk, tm), lambda i, j, kk: (kk, i)), pl.BlockSpec((tk, tn), lambda i, j, kk: (kk, j))],
        out_specs=pl.BlockSpec((tm, tn), lambda i, j, kk: (i, j)),
        out_shape=jax.ShapeDtypeStruct((m, n), out_dtype),
        scratch_shapes=[pltpu.VMEM((tm, tn), F32)],
        compiler_params=_cparams("parallel", "parallel", "arbitrary"))(a, b)


def _mm_rms_bwd(pairs, x, w, dres, *, name, side=None):
    m, d = x.shape
    tm = _row_tile(m, 416)
    n_pairs = len(pairs)

    def body(*refs):
        a_refs, b_refs = refs[:n_pairs], refs[n_pairs:2 * n_pairs]
        x_ref, w_ref, dres_ref, dx_ref, dw_ref = refs[2 * n_pairs:]
        i = pl.program_id(0)
        dyv = None
        for a_ref, b_ref in zip(a_refs, b_refs):
            term = _dot_nt(a_ref[...].astype(BF16), b_ref[...])
            dyv = term if dyv is None else dyv + term
        xv = x_ref[...]
        r = lax.rsqrt(jnp.mean(xv * xv, axis=-1, keepdims=True) + EPS)
        xh = xv * r
        g = dyv * w_ref[...]
        dx_ref[...] = r * (g - xh * jnp.mean(g * xh, axis=-1, keepdims=True)) + dres_ref[...]
        part = jnp.sum(dyv * xh, axis=0, keepdims=True)

        @pl.when(i == 0)
        def _():
            dw_ref[...] = part

        @pl.when(i > 0)
        def _():
            dw_ref[...] += part

    row = pl.BlockSpec((tm, d), lambda i: (i, 0))
    vec = pl.BlockSpec((1, d), lambda i: (0, 0))
    in_specs = ([pl.BlockSpec((tm, a.shape[1]), lambda i: (i, 0)) for a, _ in pairs]
                + [pl.BlockSpec(b.shape, lambda i: (0, 0), pipeline_mode=pl.Buffered(1)) for _, b in pairs]
                + [row, vec, row])
    return _call_with_side(
        body, side, name=name, grid=(m // tm,), in_specs=in_specs, out_specs=[row, vec],
        out_shape=[jax.ShapeDtypeStruct((m, d), F32), jax.ShapeDtypeStruct((1, d), F32)], scratch_shapes=[],
        args=[a for a, _ in pairs] + [b for _, b in pairs] + [x, w, dres], semantics=("arbitrary",))


def _rms_fwd(h, w, *, name):
    n, d = h.shape
    tm = _row_tile(n, 832)

    def body(h_ref, w_ref, o_ref):
        x = h_ref[...]
        r = lax.rsqrt(jnp.mean(x * x, axis=-1, keepdims=True) + EPS)
        o_ref[...] = (x * r * w_ref[...]).astype(BF16)

    return pl.pallas_call(
        body, name=name, grid=(n // tm,),
        in_specs=[pl.BlockSpec((tm, d), lambda i: (i, 0)), pl.BlockSpec((1, d), lambda i: (0, 0))],
        out_specs=pl.BlockSpec((tm, d), lambda i: (i, 0)),
        out_shape=jax.ShapeDtypeStruct((n, d), BF16),
        compiler_params=_cparams("parallel"))(h, w)


def _ffn_out_loss(act, w_out, h, norm_w, target):
    n, k = act.shape
    d = w_out.shape[1]
    nb = n // BLOCK
    per = max(p for p in (5, 4, 3, 2, 1) if nb % p == 0)
    tm = per * BLOCK

    def body(a_ref, w_ref, h_ref, nw_ref, *rest):
        t_refs, (dh_ref, dhb_ref, loss_ref, dw_ref) = rest[:per], rest[per:]
        i = pl.program_id(0)
        row = i * tm + lax.broadcasted_iota(jnp.int32, (tm, 1), 0)
        xv = jnp.where(row >= PAD, _dot(a_ref[...], w_ref[...]), 0.0) + h_ref[...]
        r = lax.rsqrt(jnp.mean(xv * xv, axis=-1, keepdims=True) + EPS)
        xh = xv * r
        wv = nw_ref[...]
        tgt = jnp.concatenate([t_ref[0] for t_ref in t_refs], axis=0)
        err = jnp.where(row >= BLOCK, xh * wv - tgt, 0.0)
        dyv = err * (1.0 / d)
        g = dyv * wv
        dh = r * (g - xh * jnp.mean(g * xh, axis=-1, keepdims=True))
        dh_ref[...] = dh
        dhb_ref[...] = dh.astype(BF16)
        lpart = jnp.broadcast_to(0.5 * _sum_all(err * err) * (1.0 / d), (1, LANES))
        wpart = jnp.sum(dyv * xh, axis=0, keepdims=True)

        @pl.when(i == 0)
        def _():
            loss_ref[...] = lpart
            dw_ref[...] = wpart

        @pl.when(i > 0)
        def _():
            loss_ref[...] += lpart
            dw_ref[...] += wpart

    def target_spec(j):
        return pl.BlockSpec((1, BLOCK, d), lambda i: (jnp.clip(per * i - 1 + j, 0, nb - 2), 0, 0))

    row = pl.BlockSpec((tm, d), lambda i: (i, 0))
    vec = pl.BlockSpec((1, d), lambda i: (0, 0))
    blocks = target.reshape(nb - 1, BLOCK, d)
    return pl.pallas_call(
        body, name="ffn_out_loss", grid=(nb // per,),
        in_specs=[pl.BlockSpec((tm, k), lambda i: (i, 0)),
                  pl.BlockSpec((k, d), lambda i: (0, 0), pipeline_mode=pl.Buffered(1)), row, vec]
        + [target_spec(j) for j in range(per)],
        out_specs=[row, row, pl.BlockSpec((1, LANES), lambda i: (0, 0)), vec],
        out_shape=[jax.ShapeDtypeStruct((n, d), F32), jax.ShapeDtypeStruct((n, d), BF16),
                   jax.ShapeDtypeStruct((1, LANES), F32), jax.ShapeDtypeStruct((1, d), F32)],
        compiler_params=_cparams("arbitrary"))(act, w_out, h, norm_w, *([blocks] * per))


def _final_loss(h, w, target):
    n, d = h.shape
    nb = n // BLOCK

    def body(h_ref, w_ref, t_ref, dh_ref, dhb_ref, loss_ref, dw_ref):
        i = pl.program_id(0)
        xv = h_ref[...]
        r = lax.rsqrt(jnp.mean(xv * xv, axis=-1, keepdims=True) + EPS)
        xh = xv * r
        wv = w_ref[...]
        err = jnp.where(i >= 1, xh * wv - t_ref[...], 0.0)
        dyv = err * (1.0 / d)
        g = dyv * wv
        dh = r * (g - xh * jnp.mean(g * xh, axis=-1, keepdims=True))
        dh_ref[...] = dh
        dhb_ref[...] = dh.astype(BF16)
        lpart = jnp.broadcast_to(0.5 * _sum_all(err * err) * (1.0 / d), (1, LANES))
        wpart = jnp.sum(dyv * xh, axis=0, keepdims=True)

        @pl.when(i == 0)
        def _():
            loss_ref[...] = lpart
            dw_ref[...] = wpart

        @pl.when(i > 0)
        def _():
            loss_ref[...] += lpart
            dw_ref[...] += wpart

    row = pl.BlockSpec((BLOCK, d), lambda i: (i, 0))
    vec = pl.BlockSpec((1, d), lambda i: (0, 0))
    return pl.pallas_call(
        body, name="final_loss", grid=(nb,),
        in_specs=[row, vec, pl.BlockSpec((BLOCK, d), lambda i: (jnp.maximum(i - 1, 0), 0))],
        out_specs=[row, row, pl.BlockSpec((1, LANES), lambda i: (0, 0)), vec],
        out_shape=[jax.ShapeDtypeStruct((n, d), F32), jax.ShapeDtypeStruct((n, d), BF16),
                   jax.ShapeDtypeStruct((1, LANES), F32), jax.ShapeDtypeStruct((1, d), F32)],
        compiler_params=_cparams("arbitrary"))(h, w, target)


def _main_spec(tm, cb, off=0):
    return pl.BlockSpec((tm, cb), lambda j, i: (i, j + off))


def _prev_spec(tm, cb, off=0):
    r8 = tm // SUBLANES
    return pl.BlockSpec((SUBLANES, cb), lambda j, i: (jnp.maximum(i * r8 - 1, 0), j + off))


def _next_spec(tm, cb, n_rows, off=0):
    r8 = tm // SUBLANES
    last = n_rows // SUBLANES - 1
    return pl.BlockSpec((SUBLANES, cb), lambda j, i: (jnp.minimum((i + 1) * r8, last), j + off))


def _with_prev(prev_ref, main_ref, i):
    prev = jnp.where(i > 0, prev_ref[...], 0.0)
    return jnp.concatenate([prev, main_ref[...]], axis=0)


def _with_next(main, nxt, i, n_tiles):
    return jnp.concatenate([main, jnp.where(i < n_tiles - 1, nxt, 0.0)], axis=0)


def _back(xx, s, tm):
    if s == 0:
        return xx[SUBLANES:SUBLANES + tm]
    return pltpu.roll(xx, s, 0)[SUBLANES:SUBLANES + tm]


def _ahead(xx, s, tm):
    if s == 0:
        return xx[:tm]
    return pltpu.roll(xx, xx.shape[0] - s, 0)[:tm]


def _mm_conv_fwd(u, w_in, w, b, *, name):
    n = u.shape[0]
    cdim = w_in.shape[1]
    kw = w.shape[0]
    tm = _row_tile(n, 832)
    cb = _col_tile(cdim, 512)
    nt = n // tm

    def body(u_ref, w_in_ref, w_ref, b_ref, x_ref, o_ref, acc_scr, halo_scr):
        j, i = pl.program_id(0), pl.program_id(1)

        @pl.when((j == 0) & (i == 0))
        def _():
            acc_scr[...] = jnp.zeros_like(acc_scr)
            halo_scr[...] = jnp.zeros_like(halo_scr)

        new = _dot(u_ref[...], w_in_ref[...])
        prev = acc_scr[...]
        xx = jnp.concatenate([jnp.where(i >= 2, halo_scr[...], 0.0), prev], axis=0)
        acc = jnp.broadcast_to(b_ref[...], (tm, cb))
        for k in range(kw):
            acc = acc + w_ref[k:k + 1, :] * _back(xx, kw - 1 - k, tm)
        x_ref[...] = prev.astype(BF16)
        o_ref[...] = acc
        halo_scr[...] = prev[tm - SUBLANES:, :]
        acc_scr[...] = new

    out = pl.BlockSpec((tm, cb), lambda j, i: (jnp.maximum(i - 1, 0), j))
    shp = jax.ShapeDtypeStruct((n, cdim), F32)
    return pl.pallas_call(
        body, name=name, grid=(cdim // cb, nt + 1),
        in_specs=[pl.BlockSpec((tm, u.shape[1]), lambda j, i: (jnp.minimum(i, nt - 1), 0)),
                  pl.BlockSpec((w_in.shape[0], cb), lambda j, i: (0, j)),
                  pl.BlockSpec((kw, cb), lambda j, i: (0, j)), pl.BlockSpec((1, cb), lambda j, i: (0, j))],
        out_specs=[out, out], out_shape=[jax.ShapeDtypeStruct((n, cdim), BF16), shp],
        scratch_shapes=[pltpu.VMEM((tm, cb), F32), pltpu.VMEM((SUBLANES, cb), F32)],
        compiler_params=_cparams("arbitrary", "arbitrary"))(u, w_in, w, b)


def _conv_bwd_core(dpre_ext, x, w_ref, kw, tm):
    dx = None
    dws = []
    for k in range(kw):
        shifted = _ahead(dpre_ext, kw - 1 - k, tm)
        term = w_ref[k:k + 1, :] * shifted
        dx = term if dx is None else dx + term
        dws.append(jnp.sum(shifted * x, axis=0, keepdims=True))
    return dx, dws, jnp.sum(dpre_ext[:tm], axis=0, keepdims=True)


def _acc_rows(i, dw_ref, db_ref, dws, db):
    @pl.when(i == 0)
    def _():
        for k, v in enumerate(dws):
            dw_ref[k:k + 1, :] = v
        db_ref[...] = db

    @pl.when(i > 0)
    def _():
        for k, v in enumerate(dws):
            dw_ref[k:k + 1, :] += v
        db_ref[...] += db


def _conv_bwd(dpre, x, w, *, name, col0=0, into=None):
    n, cdim = x.shape
    kw = w.shape[0]
    tm = _row_tile(n, 832)
    cb = _col_tile(cdim, 512)
    nt = n // tm
    off = col0 // cb
    n_alias = 0 if into is None else 3

    def body(d_ref, dn_ref, x_ref, w_ref, *rest):
        dx_ref, dw_ref, db_ref = rest[n_alias:]
        i = pl.program_id(1)
        dpre_ext = _with_next(d_ref[...], dn_ref[...], i, nt)
        dx, dws, db = _conv_bwd_core(dpre_ext, x_ref[...], w_ref, kw, tm)
        dx_ref[...] = dx.astype(BF16)
        _acc_rows(i, dw_ref, db_ref, dws, db)

    wspec = pl.BlockSpec((kw, cb), lambda j, i: (0, j + off))
    bspec = pl.BlockSpec((1, cb), lambda j, i: (0, j + off))
    return pl.pallas_call(
        body, name=name, grid=(dpre.shape[1] // cb, nt),
        in_specs=[_main_spec(tm, cb), _next_spec(tm, cb, n), _main_spec(tm, cb, off), wspec]
        + [pl.BlockSpec(memory_space=pl.ANY)] * n_alias,
        out_specs=[_main_spec(tm, cb, off), wspec, bspec],
        out_shape=[jax.ShapeDtypeStruct((n, cdim), BF16), jax.ShapeDtypeStruct((kw, cdim), F32),
                   jax.ShapeDtypeStruct((1, cdim), F32)],
        input_output_aliases={4 + k: k for k in range(n_alias)},
        compiler_params=_cparams("parallel", "arbitrary"))(dpre, dpre, x, w, *(into or ()))


def _ffn_in_act_fwd(u, w_in, w, b):
    n = u.shape[0]
    kw = w.shape[0]
    tm = _row_tile(n, 832)
    cb = _col_tile(D_FF, 256)
    nc = D_FF // cb
    nt = n // tm

    def body(u_ref, wu_in_ref, wg_in_ref, wu_ref, wg_ref, bu_ref, bg_ref,
             xu_ref, xg_ref, hu_ref, hg_ref, act_ref, acc_scr, halo_scr):
        j, i = pl.program_id(0), pl.program_id(1)

        @pl.when((j == 0) & (i == 0))
        def _():
            acc_scr[...] = jnp.zeros_like(acc_scr)
            halo_scr[...] = jnp.zeros_like(halo_scr)

        ub = u_ref[...]
        new = [_dot(ub, wu_in_ref[...]), _dot(ub, wg_in_ref[...])]
        hid = []
        for half, (x_ref, w_ref, b_ref) in enumerate(((xu_ref, wu_ref, bu_ref), (xg_ref, wg_ref, bg_ref))):
            prev = acc_scr[half]
            xx = jnp.concatenate([jnp.where(i >= 2, halo_scr[half], 0.0), prev], axis=0)
            acc = jnp.broadcast_to(b_ref[...], (tm, cb))
            for k in range(kw):
                acc = acc + w_ref[k:k + 1, :] * _back(xx, kw - 1 - k, tm)
            x_ref[...] = prev.astype(BF16)
            hid.append(acc)
            halo_scr[half] = prev[tm - SUBLANES:, :]
            acc_scr[half] = new[half]
        hu_ref[...] = hid[0].astype(BF16)
        hg_ref[...] = hid[1].astype(BF16)
        act_ref[...] = (_silu(hid[1]) * hid[0]).astype(BF16)

    def wspec(off):
        return pl.BlockSpec((kw, cb), lambda j, i: (0, j + off))

    def bspec(off):
        return pl.BlockSpec((1, cb), lambda j, i: (0, j + off))

    def in_w(off):
        return pl.BlockSpec((w_in.shape[0], cb), lambda j, i: (0, j + off))

    out = pl.BlockSpec((tm, cb), lambda j, i: (jnp.maximum(i - 1, 0), j))
    bf16_out = jax.ShapeDtypeStruct((n, D_FF), BF16)
    return pl.pallas_call(
        body, name="ffn_in_act_fwd", grid=(nc, nt + 1),
        in_specs=[pl.BlockSpec((tm, u.shape[1]), lambda j, i: (jnp.minimum(i, nt - 1), 0)), in_w(0), in_w(nc),
                  wspec(0), wspec(nc), bspec(0), bspec(nc)],
        out_specs=[out] * 5,
        out_shape=[bf16_out] * 5,
        scratch_shapes=[pltpu.VMEM((2, tm, cb), F32), pltpu.VMEM((2, SUBLANES, cb), F32)],
        compiler_params=_cparams("arbitrary", "arbitrary"))(u, w_in, w_in, w, w, b, b)


def _ffn_out_act_bwd(dh, w_out, hu, hg, x_up, x_gate, w):
    n = x_up.shape[0]
    kw = w.shape[0]
    tm = _row_tile(n, 832)
    cb = _col_tile(D_FF, 256)
    nc = D_FF // cb
    nt = n // tm

    def body(dh_ref, wo_ref, hu_ref, hun_ref, hg_ref, hgn_ref, xu_ref, xg_ref, wu_ref, wg_ref,
             dxu_ref, dxg_ref, dwu_ref, dwg_ref, dbu_ref, dbg_ref, acc_scr, halo_scr):
        j, i = pl.program_id(0), pl.program_id(1)

        @pl.when((j == 0) & (i == 0))
        def _():
            acc_scr[...] = jnp.zeros_like(acc_scr)
            halo_scr[...] = jnp.zeros_like(halo_scr)

        tile = jnp.maximum(nt - 1 - i, 0)
        row = tile * tm + lax.broadcasted_iota(jnp.int32, (tm, 1), 0)
        new = jnp.where(row >= PAD, _dot_nt(dh_ref[...].astype(BF16), wo_ref[...]), 0.0)
        prev = jnp.where(i >= 1, acc_scr[...], 0.0)
        dact_e = jnp.concatenate([prev, jnp.where(i >= 2, halo_scr[...], 0.0)], axis=0)
        last = nt - i >= nt - 1
        up_e = jnp.concatenate([hu_ref[...].astype(F32), jnp.where(last, 0.0, hun_ref[...].astype(F32))], axis=0)
        gate_e = jnp.concatenate([hg_ref[...].astype(F32), jnp.where(last, 0.0, hgn_ref[...].astype(F32))], axis=0)
        halo_scr[...] = prev[:BF16_ROWS, :]
        acc_scr[...] = new
        sg = _sigmoid(gate_e)
        dup_e = dact_e * (gate_e * sg)
        dgate_e = dact_e * up_e * (sg * (1.0 + gate_e * (1.0 - sg)))
        dx, dws, db = _conv_bwd_core(dup_e, xu_ref[...], wu_ref, kw, tm)
        dxu_ref[...] = dx.astype(BF16)
        _acc_rows(i, dwu_ref, dbu_ref, dws, db)
        dx, dws, db = _conv_bwd_core(dgate_e, xg_ref[...], wg_ref, kw, tm)
        dxg_ref[...] = dx.astype(BF16)
        _acc_rows(i, dwg_ref, dbg_ref, dws, db)

    def done_tile(i):
        return jnp.minimum(nt - i, nt - 1)

    halo_blocks = tm // BF16_ROWS
    main = pl.BlockSpec((tm, cb), lambda j, i: (done_tile(i), j))
    nxt = pl.BlockSpec((BF16_ROWS, cb),
                       lambda j, i: (jnp.minimum((done_tile(i) + 1) * halo_blocks, n // BF16_ROWS - 1), j))
    wspec0 = pl.BlockSpec((kw, cb), lambda j, i: (0, j))
    wspec1 = pl.BlockSpec((kw, cb), lambda j, i: (0, j + nc))
    bspec = pl.BlockSpec((1, cb), lambda j, i: (0, j))
    return pl.pallas_call(
        body, name="ffn_out_act_bwd", grid=(nc, nt + 1),
        in_specs=[pl.BlockSpec((tm, dh.shape[1]), lambda j, i: (jnp.maximum(nt - 1 - i, 0), 0)),
                  pl.BlockSpec((cb, w_out.shape[1]), lambda j, i: (j, 0)),
                  main, nxt, main, nxt, main, main, wspec0, wspec1],
        out_specs=[main, main, wspec0, wspec0, bspec, bspec],
        out_shape=[jax.ShapeDtypeStruct((n, D_FF), BF16), jax.ShapeDtypeStruct((n, D_FF), BF16),
                   jax.ShapeDtypeStruct((kw, D_FF), F32), jax.ShapeDtypeStruct((kw, D_FF), F32),
                   jax.ShapeDtypeStruct((1, D_FF), F32), jax.ShapeDtypeStruct((1, D_FF), F32)],
        scratch_shapes=[pltpu.VMEM((tm, cb), F32), pltpu.VMEM((BF16_ROWS, cb), F32)],
        compiler_params=_cparams("arbitrary", "arbitrary"))(dh, w_out, hu, hu, hg, hg, x_up, x_gate, w, w)


def _ssd_prep(pxs_ref, pb_ref, pc_ref, dtr_ref, dtb_ref, alog_ref, c):
    xs = _silu(pxs_ref[...])
    bm = _silu(pb_ref[...])
    cm = _silu(pc_ref[...])
    return (xs, bm, cm) + _ssd_decay(dtr_ref, dtb_ref, alog_ref, c)


def _ssd_decay(dtr_ref, dtb_ref, alog_ref, c):
    row =lax.broadcasted_iota(jnp.int32, (BLOCK, 1), 0) + c * BLOCK
    valid = (row >= PAD).astype(F32)
    dtr = dtr_ref[...] + dtb_ref[...]
    dt = _softplus(dtr) * valid
    a = -jnp.exp(alog_ref[...])
    lam = dt * a
    ri = lax.broadcasted_iota(jnp.int32, (BLOCK, BLOCK), 0)
    ci = lax.broadcasted_iota(jnp.int32, (BLOCK, BLOCK), 1)
    causal = ci <= ri
    cs = _sel_dot(causal.astype(BF16), lam)
    return valid, dtr, dt, a, lam, cs, causal


def _head_cols(r):
    return slice(SSD_HEADDIM * r, SSD_HEADDIM * (r + 1))


def _ssd_specs(nc, rev):
    def cidx(c):
        return nc - 1 - c if rev else c

    xs = pl.BlockSpec((BLOCK, SSD_GW), lambda g, c: (cidx(c), g))
    bspec = pl.BlockSpec((BLOCK, SSD_STATE), lambda g, c: (cidx(c), SSD_INNER // SSD_STATE + g))
    cspec = pl.BlockSpec((BLOCK, SSD_STATE), lambda g, c: (cidx(c), (SSD_INNER + SSD_BC) // SSD_STATE + g))
    lane = pl.BlockSpec((BLOCK, LANES), lambda g, c: (cidx(c), g))
    vec = pl.BlockSpec((1, LANES), lambda g, c: (0, g))
    wide_vec = pl.BlockSpec((1, SSD_GW), lambda g, c: (0, g))
    hsave = pl.BlockSpec((1, 1, SSD_GW, SSD_STATE), lambda g, c: (cidx(c), g, 0, 0))
    return xs, bspec, cspec, lane, vec, wide_vec, hsave


def _head_spread_matrix():
    r = lax.broadcasted_iota(jnp.int32, (LANES, SSD_GW), 0)
    col = lax.broadcasted_iota(jnp.int32, (LANES, SSD_GW), 1)
    return (col // SSD_HEADDIM == r).astype(BF16)


def _const_spec(shape):
    return pl.BlockSpec(shape, lambda g, c: (0,) * len(shape))


def _spread_heads(per_head, e_ref):
    wide = _dot_sel(jnp.concatenate(per_head, axis=0), e_ref[...])
    return [wide[BLOCK * k:BLOCK * (k + 1)] for k in range(len(per_head))]


def _call_with_side(body, side, *, name, grid, in_specs, out_specs, out_shape, scratch_shapes, args,
                    semantics=("parallel", "arbitrary")):
    if side is None:
        outs = pl.pallas_call(body, name=name, grid=grid, in_specs=in_specs, out_specs=out_specs, out_shape=out_shape,
                              scratch_shapes=scratch_shapes, compiler_params=_cparams(*semantics))(*args)
        return outs, []
    n_in, n_out, n_scr, n_side = len(in_specs), len(out_specs), len(scratch_shapes), len(side.arrays)

    def body_with_side(*refs):
        ins, rest = refs[:n_in + n_side], refs[n_in + n_side:]
        outs, scratch = rest[:n_out + n_side], rest[n_out + n_side:]
        side_refs = (ins[n_in:], outs[n_out:], scratch[n_scr:])
        ids = [pl.program_id(k) for k in range(len(grid))]
        inner_first = functools.reduce(jnp.logical_and, [i == 0 for i in ids[1:]], True)

        @pl.when((ids[0] == 0) & inner_first)
        def _():
            side.phases[0](*side_refs)

        body(*ins[:n_in], *outs[:n_out], *scratch[:n_scr])

        @pl.when((ids[0] == grid[0] // 2) & inner_first)
        def _():
            side.phases[1](*side_refs)

        @pl.when(functools.reduce(jnp.logical_and, [i == n - 1 for i, n in zip(ids, grid)]))
        def _():
            side.phases[2](*side_refs)

    any_spec = pl.BlockSpec(memory_space=pl.ANY)
    outs = pl.pallas_call(
        body_with_side, name=name, grid=grid, in_specs=list(in_specs) + [any_spec] * n_side,
        out_specs=list(out_specs) + [any_spec] * n_side, out_shape=list(out_shape) + list(side.out_shape),
        scratch_shapes=list(scratch_shapes) + list(side.scratch_shapes),
        compiler_params=_cparams(*["arbitrary"] * len(grid)))(*args, *side.arrays)
    return outs[:n_out], outs[n_out:]


def _ssd_fwd(pre, dt_raw, z, dtb, alog, dskip_w, norm_w, side=None):
    n = pre.shape[0]
    nc = n // BLOCK
    xs_s, b_s, c_s, lane_s, vec_s, wide_s, hs_s = _ssd_specs(nc, False)

    def body(pxs_ref, pb_ref, pc_ref, dtr_ref, z_ref, dtb_ref, alog_ref, dskw_ref, nw_ref, e_ref,
             y_ref, yn_ref, hs_ref, h_scr):
        c = pl.program_id(1)

        @pl.when(c == 0)
        def _():
            h_scr[...] = jnp.zeros_like(h_scr)

        xs, bm, cm, _, _, dt, _, _, cs, causal = _ssd_prep(pxs_ref, pb_ref, pc_ref, dtr_ref, dtb_ref, alog_ref, c)
        cst = cs.T
        cs_last = cs[BLOCK - 1:BLOCK, :]
        dt_w, ecs_w, dec_w = _spread_heads([dt, jnp.exp(cs), jnp.exp(cs_last - cs)], e_ref)
        xdt = xs * dt_w
        bmb = bm.astype(BF16)
        cmb = cm.astype(BF16)
        cb = _dot_nt(cmb, bmb)
        hg = h_scr[...]
        hs_ref[0, 0] = hg
        y = _dot_nt(cmb, hg.astype(BF16)) * ecs_w + dskw_ref[...] * xs
        first = lax.broadcasted_iota(jnp.int32, (BLOCK, LANES), 1) < SSD_HEADDIM
        diag = []
        for j in range(SSD_HPG // 2):
            xp = xdt[:, LANES * j:LANES * (j + 1)].astype(BF16)
            res = []
            for r in (2 * j, 2 * j + 1):
                lm = jnp.exp(jnp.where(causal, cs[:, r:r + 1] - cst[r:r + 1, :], NEG))
                res.append(_dot((cb * lm).astype(BF16), xp))
            diag.append(jnp.where(first, res[0], res[1]))
        y = y + jnp.concatenate(diag, axis=1)
        st = _dot_tn((xdt * dec_w).astype(BF16), bmb)
        eh = jnp.exp(cs_last)
        for r in range(SSD_HPG):
            rows = _head_cols(r)
            h_scr[rows, :] = hg[rows, :] * eh[:, r:r + 1] + st[rows, :]
        y_ref[...] = y
        gts = y * _silu(z_ref[...])
        rr = lax.rsqrt(jnp.mean(gts * gts, axis=-1, keepdims=True) + EPS)
        yn_ref[...] = (gts * rr * nw_ref[...]).astype(BF16)

    return _call_with_side(
        body, side, name="ssd_fwd", grid=(SSD_GROUPS, nc),
        in_specs=[xs_s, b_s, c_s, lane_s, xs_s, vec_s, vec_s, wide_s, wide_s, _const_spec((LANES, SSD_GW))],
        out_specs=[xs_s, xs_s, hs_s],
        out_shape=[jax.ShapeDtypeStruct((n, SSD_INNER), F32), jax.ShapeDtypeStruct((n, SSD_INNER), BF16),
                   jax.ShapeDtypeStruct((nc, SSD_GROUPS, SSD_GW, SSD_STATE), F32)],
        scratch_shapes=[pltpu.VMEM((SSD_GW, SSD_STATE), F32)],
        args=(pre, pre, pre, dt_raw, z, dtb, alog, dskip_w, norm_w, _head_spread_matrix()))


def _lane_put(acc, col, r):
    lane = lax.broadcasted_iota(jnp.int32, acc.shape, 1)
    return jnp.where(lane == r, col, acc)


def _ssd_bwd(dyn, y, z, pre, dt_raw, hsave, dtb, alog, dskip_w, norm_w, side=None):
    n = pre.shape[0]
    nc = n // BLOCK
    spread = _head_spread_matrix()
    xs_s, b_s, c_s, lane_s, vec_s, wide_s, hs_s = _ssd_specs(nc, True)
    bc_out =pl.BlockSpec((BLOCK, SSD_STATE), lambda g, c: (nc - 1 - c, g))

    def body(dyn_ref, y_ref, z_ref, pxs_ref, pb_ref, pc_ref, dtr_ref, hs_ref, dtb_ref, alog_ref, dskw_ref, nw_ref,
             e_ref, r_ref,
             dz_ref, dxs_ref, dbm_ref, dcm_ref, ddt_ref, dnw_ref, ddtb_ref, dalog_ref, ddsk_ref, g_scr):
        step = pl.program_id(1)
        c = nc - 1 - step

        @pl.when(step == 0)
        def _():
            g_scr[...] = jnp.zeros_like(g_scr)

        pxs, pb, pc = pxs_ref[...], pb_ref[...], pc_ref[...]
        sx, sb, sc = _sigmoid(pxs), _sigmoid(pb), _sigmoid(pc)
        xs, bm, cm = pxs * sx, pb * sb, pc * sc
        valid, dtr, dt, a, lam, cs, causal = _ssd_decay(dtr_ref, dtb_ref, alog_ref, c)
        cst = cs.T
        cs_last = cs[BLOCK - 1:BLOCK, :]
        bmb = bm.astype(BF16)
        cmb = cm.astype(BF16)
        cb = _dot_nt(cmb, bmb)
        hg = hs_ref[0, 0]
        hgb = hg.astype(BF16)
        yoff = _dot_nt(cmb, hgb)
        gn = g_scr[...]
        gnb = gn.astype(BF16)

        zv = z_ref[...]
        yv = y_ref[...]
        sgz = _sigmoid(zv)
        sz = zv * sgz
        gts = yv * sz
        rr = lax.rsqrt(jnp.mean(gts * gts, axis=-1, keepdims=True) + EPS)
        xh = gts * rr
        dynv = dyn_ref[...]
        gg = dynv * nw_ref[...]
        dgts = rr * (gg - xh * jnp.mean(gg * xh, axis=-1, keepdims=True))
        dnw = jnp.sum(dynv * xh, axis=0, keepdims=True)
        dy = dgts * sz
        dz_ref[...] = (dgts * yv * (sgz * (1.0 + zv * (1.0 - sgz)))).astype(BF16)

        ecs = jnp.exp(cs)
        dec = jnp.exp(cs_last - cs)
        eh = jnp.exp(cs_last)
        dt_w, ecs_w, dec_w = _spread_heads([dt, ecs, dec], e_ref)
        red_m = r_ref[...]

        def head_sums(v):
            return _dot_sel(v, red_m, terms=2)

        xdt = xs * dt_w
        q_all = _dot_nt(bmb, gnb)
        w_all = (dy * ecs_w).astype(BF16)
        e_hl = head_sums(q_all * xdt) * dec
        dcs_col = head_sums(dy * yoff) * ecs - e_hl
        gh = jnp.zeros((1, LANES), F32)
        prod = gn * hg
        for r in range(SSD_HPG):
            gh = _lane_put(gh, _sum_all(prod[_head_cols(r), :]), r)
        dcs_last = jnp.sum(e_hl, axis=0, keepdims=True) + eh * gh
        ddsk = jnp.sum(head_sums(dy * xs), axis=0, keepdims=True)
        cbt = _dot_nt(bmb, cmb)
        lane = lax.broadcasted_iota(jnp.int32, (BLOCK, LANES), 1)
        first = lane < SSD_HEADDIM
        causal_t = lax.broadcasted_iota(jnp.int32, (BLOCK, BLOCK), 1) >= lax.broadcasted_iota(
            jnp.int32, (BLOCK, BLOCK), 0)
        sub = lax.broadcasted_iota(jnp.int32, (SUBLANES, BLOCK), 0)
        dcs_row = jnp.zeros((SUBLANES, BLOCK), F32)
        dcb = jnp.zeros((BLOCK, BLOCK), F32)
        dxdt_pairs = []
        for j in range(SSD_HPG // 2):
            tile = slice(LANES * j, LANES * (j + 1))
            dy_p = dy[:, tile]
            dyb = dy_p.astype(BF16)
            xdtb = xdt[:, tile].astype(BF16)
            res = []
            for half, r in enumerate((2 * j, 2 * j + 1)):
                csc, csr = cs[:, r:r + 1], cst[r:r + 1, :]
                lm = jnp.exp(jnp.where(causal, csc - csr, NEG))
                lmt = jnp.exp(jnp.where(causal_t, csr - csc, NEG))
                keep = first if half == 0 else jnp.logical_not(first)
                gm = _dot_nt(jnp.where(keep, dy_p, 0.0).astype(BF16), xdtb) * lm
                dcb = dcb + gm
                mm_ = gm * cb
                dcs_col = dcs_col + jnp.where(lane == r, jnp.sum(mm_, axis=1, keepdims=True), 0.0)
                dcs_row = jnp.where(sub == r, jnp.sum(mm_, axis=0, keepdims=True), dcs_row)
                res.append(_dot((cbt * lmt).astype(BF16), dyb))
            dxdt_pairs.append(jnp.where(first, res[0], res[1]))
        dxdt = jnp.concatenate(dxdt_pairs, axis=1) + q_all * dec_w
        ddt_x = head_sums(dxdt * xs)
        dxs = dxdt * dt_w + dskw_ref[...] * dy
        dcbb = dcb.astype(BF16)
        dcm = _dot(w_all, hgb) + _dot(dcbb, bmb)
        dbm = _dot((xdt * dec_w).astype(BF16), gnb) + _dot_tn(dcbb, cmb)
        dh_off = _dot_tn(w_all, cmb)
        for r in range(SSD_HPG):
            rows = _head_cols(r)
            g_scr[rows, :] = gn[rows, :] * eh[:, r:r + 1] + dh_off[rows, :]

        pad_rows = jnp.zeros((BLOCK - SUBLANES, BLOCK), F32)
        dcs = dcs_col - jnp.concatenate([dcs_row, pad_rows], axis=0).T
        rsel = lax.broadcasted_iota(jnp.int32, (BLOCK, LANES), 0)
        dcs = dcs + jnp.where(rsel == BLOCK - 1, dcs_last, 0.0)
        ri = lax.broadcasted_iota(jnp.int32, (BLOCK, BLOCK), 0)
        ci = lax.broadcasted_iota(jnp.int32, (BLOCK, BLOCK), 1)
        dlam = _sel_dot((ci >= ri).astype(BF16), dcs)
        head = lane < SSD_HPG
        ddt = dlam * a + ddt_x
        ddtr = jnp.where(head, ddt * _sigmoid(dtr) * valid, 0.0)
        ddt_ref[...] = ddtr.astype(BF16)
        dalog = jnp.sum(jnp.where(head, dlam * lam, 0.0), axis=0, keepdims=True)
        ddtb = jnp.sum(ddtr, axis=0, keepdims=True)

        dxs_ref[...] = dxs * (sx * (1.0 + pxs * (1.0 - sx)))
        dbm_ref[...] = dbm * (sb * (1.0 + pb * (1.0 - sb)))
        dcm_ref[...] = dcm * (sc * (1.0 + pc * (1.0 - sc)))

        @pl.when(step == 0)
        def _():
            dnw_ref[...] = dnw
            ddtb_ref[...] = ddtb
            dalog_ref[...] = dalog
            ddsk_ref[...] = ddsk

        @pl.when(step > 0)
        def _():
            dnw_ref[...] += dnw
            ddtb_ref[...] += ddtb
            dalog_ref[...] += dalog
            ddsk_ref[...] += ddsk

    return _call_with_side(
        body, side, name="ssd_bwd", grid=(SSD_GROUPS, nc),
        in_specs=[xs_s, xs_s, xs_s, xs_s, b_s, c_s, lane_s, hs_s, vec_s, vec_s, wide_s, wide_s,
                  _const_spec((LANES, SSD_GW)), _const_spec((SSD_GW, LANES))],
        out_specs=[xs_s, xs_s, bc_out, bc_out, lane_s, wide_s, vec_s, vec_s, vec_s],
        out_shape=[jax.ShapeDtypeStruct((n, SSD_INNER), BF16), jax.ShapeDtypeStruct((n, SSD_INNER), F32),
                   jax.ShapeDtypeStruct((n, SSD_BC), F32), jax.ShapeDtypeStruct((n, SSD_BC), F32),
                   jax.ShapeDtypeStruct((n, DT_W), BF16), jax.ShapeDtypeStruct((1, SSD_INNER), F32),
                   jax.ShapeDtypeStruct((1, DT_W), F32), jax.ShapeDtypeStruct((1, DT_W), F32),
                   jax.ShapeDtypeStruct((1, DT_W), F32)],
        scratch_shapes=[pltpu.VMEM((SSD_GW, SSD_STATE), F32)],
        args=(dyn, y, z, pre, pre, pre, dt_raw, hsave, dtb, alog, dskip_w, norm_w, spread, spread.T))


def _bucket_table():
    def bucket(dist):
        d = np.maximum(dist, 0)
        half = REL_BUCKETS // 2
        big = half + (np.log(np.maximum(d, half).astype(np.float32) / np.float32(half))
                      / np.float32(math.log(REL_MAX_DIST / half)) * np.float32(REL_BUCKETS - half)).astype(np.int32)
        return np.where(d < half, d, np.minimum(big, REL_BUCKETS - 1)).astype(np.int32)

    l = np.arange(BLOCK)[None, :]
    band = bucket(l + BLOCK - np.arange(2 * BLOCK)[:, None])
    j = np.arange(BLOCK)[:, None]
    tables = [np.concatenate([bucket(v * BLOCK + l - j), band], axis=0) for v in range(3)]
    return np.concatenate([t.reshape(-1) for t in tables])


def _onehot_t():
    buckets = jnp.asarray(_bucket_table())
    return (buckets[None, :] == jnp.arange(REL_BUCKETS, dtype=jnp.int32)[:, None]).astype(F32)


def _bias_tables(rel_t, onehot_t, side=None):
    def body(r_ref, oh_ref, o_ref):
        o_ref[...] = jnp.dot(r_ref[...], oh_ref[...], precision=HIGHEST, preferred_element_type=F32)

    outs, carried = _call_with_side(
        body, side, name="bias_tables", grid=(NT_ALL // NT_TILE,),
        in_specs=[pl.BlockSpec((ATT_HEADS, REL_BUCKETS), lambda i: (0, 0)),
                  pl.BlockSpec((REL_BUCKETS, NT_TILE), lambda i: (0, i))],
        out_specs=[pl.BlockSpec((ATT_HEADS, NT_TILE), lambda i: (0, i))],
        out_shape=[jax.ShapeDtypeStruct((ATT_HEADS, NT_ALL), F32)], scratch_shapes=[], args=(rel_t, onehot_t),
        semantics=("parallel",))
    return outs[0], carried


def _bias_grad(dtab, onehot_t):
    def body(d_ref, oh_ref, o_ref):
        i = pl.program_id(0)
        p = lax.dot_general(d_ref[...], oh_ref[...], (((1,), (1,)), ((), ())), precision=HIGHEST,
                            preferred_element_type=F32)

        @pl.when(i == 0)
        def _():
            o_ref[...] = p

        @pl.when(i > 0)
        def _():
            o_ref[...] += p

    return pl.pallas_call(
        body, name="bias_grad", grid=(NT_ALL // NT_TILE,),
        in_specs=[pl.BlockSpec((ATT_HEADS, NT_TILE), lambda i: (0, i)),
                  pl.BlockSpec((REL_BUCKETS, NT_TILE), lambda i: (0, i))],
        out_specs=pl.BlockSpec((ATT_HEADS, REL_BUCKETS), lambda i: (0, 0)),
        out_shape=jax.ShapeDtypeStruct((ATT_HEADS, REL_BUCKETS), F32),
        compiler_params=_cparams("arbitrary"))(dtab, onehot_t)


def _att_mask_t(n, copies):
    far = 4 * BLOCK
    kk = lax.broadcasted_iota(jnp.int32, (N_KEYS, copies * BLOCK), 0)
    li = lax.broadcasted_iota(jnp.int32, (N_KEYS, copies * BLOCK), 1) & (BLOCK - 1)
    meta_ok = (kk >= PAD) & (kk < BLOCK) & (li + jnp.where(n >= 1, far, 0) >= kk)
    prev_ok = (kk >= BLOCK) & (kk < 2 * BLOCK) & (kk - BLOCK > li + jnp.where(n >= 2, 0, far))
    cur_ok = (kk >= 2 * BLOCK) & (kk - 2 * BLOCK <= li - jnp.where(n >= 1, 0, far))
    return meta_ok | prev_ok | cur_ok


def _att_kv(meta_ref, prev_ref, cur_ref):
    kv = jnp.concatenate([meta_ref[...], prev_ref[...], cur_ref[...]], axis=0)
    first = lax.broadcasted_iota(jnp.int32, (N_KEYS, LANES), 1) < ATT_HEADDIM
    out = []
    for pair in (kv[:, :LANES], kv[:, LANES:]):
        swapped = pltpu.roll(pair, ATT_HEADDIM, 1)
        out.append([jnp.where(first, pair, swapped).astype(BF16), jnp.where(first, swapped, pair).astype(BF16)])
    return out[0], out[1]


def _split_heads(x_pair, first):
    return jnp.concatenate([jnp.where(first, x_pair, 0.0), jnp.where(first, 0.0, x_pair)], axis=0).astype(BF16)


def _att_probs_t(qm2, k_dup, t_ref, j, mask2, sink_ref):
    scale = ATT_HEADDIM ** -0.5
    bias2 = jnp.concatenate([t_ref[0, 2 * j], t_ref[0, 2 * j + 1]], axis=1)
    second = lax.broadcasted_iota(jnp.int32, (1, 2 * BLOCK), 1) >= BLOCK
    sink2 = jnp.where(second, sink_ref[0:1, 2 * j + 1:2 * j + 2], sink_ref[0:1, 2 * j:2 * j + 1])
    s_t = jnp.where(mask2, _dot_nt(k_dup, qm2) * scale + bias2, NEG)
    mx = jnp.maximum(jnp.max(s_t, axis=0, keepdims=True), sink2)
    p_t = jnp.exp(s_t - mx)
    p_s = jnp.exp(sink2 - mx)
    inv = 1.0 / (jnp.sum(p_t, axis=0, keepdims=True) + p_s)
    return p_t * inv, p_s * inv


def _att_specs(nb, rev):
    def nidx(i):
        return nb - 1 - i if rev else i

    kvb = ATT_Q // (2 * ATT_KV)
    q_s = pl.BlockSpec((BLOCK, ATT_Q), lambda i: (nidx(i), 0))
    cur = pl.BlockSpec((BLOCK, 2 * ATT_KV), lambda i: (nidx(i), kvb))
    prev = pl.BlockSpec((BLOCK, 2 * ATT_KV), lambda i: (jnp.maximum(nidx(i) - 1, 0), kvb))
    meta = pl.BlockSpec((BLOCK, 2 * ATT_KV), lambda i: (0, kvb))
    table = pl.BlockSpec((1, ATT_HEADS, N_KEYS, BLOCK), lambda i: (jnp.minimum(nidx(i), 2), 0, 0, 0))
    sink = pl.BlockSpec((1, LANES), lambda i: (0, 0))
    return q_s, cur, prev, meta, table, sink


def _attn_fwd(qkv, tables, sinks):
    n = qkv.shape[0]
    nb = n // BLOCK
    q_s, cur_s, prev_s, meta_s, t_s, sink_s = _att_specs(nb, False)

    def body(q_ref, cur_ref, prev_ref, meta_ref, t_ref, sink_ref, o_ref):
        blk = pl.program_id(0)
        mask_t = _att_mask_t(blk, 1)
        k_dup, v_dup = _att_kv(meta_ref, prev_ref, cur_ref)
        v_dup_t = [v.T for v in v_dup]
        first = lax.broadcasted_iota(jnp.int32, (BLOCK, LANES), 1) < ATT_HEADDIM
        top = lax.broadcasted_iota(jnp.int32, (LANES, BLOCK), 0) < ATT_HEADDIM
        scale = ATT_HEADDIM ** -0.5
        for j in range(ATT_HEADS // 2):
            kh = 2 * j // ATT_GQ
            tile = slice(LANES * j, LANES * (j + 1))
            q_p = q_ref[:, tile]
            res = []
            for half, h in enumerate((2 * j, 2 * j + 1)):
                qm = jnp.where(first if half == 0 else jnp.logical_not(first), q_p, 0.0).astype(BF16)
                sink = sink_ref[0:1, h:h + 1]
                s_t = jnp.where(mask_t, _dot_nt(k_dup[kh], qm) * scale + t_ref[0, h], NEG)
                mx = jnp.maximum(jnp.max(s_t, axis=0, keepdims=True), sink)
                p_t = jnp.exp(s_t - mx)
                inv = 1.0 / (jnp.sum(p_t, axis=0, keepdims=True) + jnp.exp(sink - mx))
                res.append(_dot(v_dup_t[kh], (p_t * inv).astype(BF16)))
            o_ref[:, tile] = jnp.where(top, res[0], res[1]).T.astype(BF16)

    return pl.pallas_call(
        body, name="attn_fwd", grid=(nb,),
        in_specs=[q_s, cur_s, prev_s, meta_s, t_s, sink_s],
        out_specs=q_s,
        out_shape=jax.ShapeDtypeStruct((n, ATT_Q), BF16),
        compiler_params=_cparams("parallel"))(qkv, qkv, qkv, qkv, tables, sinks)


def _attn_bwd(datt, qkv, tables, sinks):
    n = qkv.shape[0]
    nb = n // BLOCK
    q_s, cur_s, prev_s, meta_s, t_s, sink_s = _att_specs(nb, True)
    dqkv_s = pl.BlockSpec((BLOCK, ATT_Q + 2 * ATT_KV), lambda i: (nb - 1 - i, 0))
    scale = ATT_HEADDIM ** -0.5

    def body(do_ref, q_ref, cur_ref, prev_ref, meta_ref, t_ref, sink_ref,
             dqkv_ref, dt_ref, dsink_ref, carry_scr, meta_scr):
        step = pl.program_id(0)
        blk = nb - 1 - step
        mask2 = _att_mask_t(blk, 2)
        k_dup, v_dup = _att_kv(meta_ref, prev_ref, cur_ref)
        k_dup_t = [k.T for k in k_dup]

        @pl.when(step == 0)
        def _():
            carry_scr[...] = jnp.zeros_like(carry_scr)
            meta_scr[...] = jnp.zeros_like(meta_scr)
            dsink_ref[...] = jnp.zeros_like(dsink_ref)

        @pl.when((step == 0) | (blk <= 1))
        def _():
            dt_ref[...] = jnp.zeros_like(dt_ref)

        first = lax.broadcasted_iota(jnp.int32, (BLOCK, LANES), 1) < ATT_HEADDIM
        top = lax.broadcasted_iota(jnp.int32, (LANES, BLOCK), 0) < ATT_HEADDIM
        first_k = lax.broadcasted_iota(jnp.int32, (N_KEYS, LANES), 1) < ATT_HEADDIM
        dsink = jnp.zeros((1, LANES), F32)
        dk_acc = [None] * ATT_KV_HEADS
        dv_acc = [None] * ATT_KV_HEADS
        for j in range(ATT_HEADS // 2):
            kh = 2 * j // ATT_GQ
            tile = slice(LANES * j, LANES * (j + 1))
            qm2 = _split_heads(q_ref[:, tile], first)
            dom2 = _split_heads(do_ref[:, tile], first)
            p_t, p_s = _att_probs_t(qm2, k_dup[kh], t_ref, j, mask2, sink_ref)
            dp_t = _dot_nt(v_dup[kh], dom2)
            delta = jnp.sum(p_t * dp_t, axis=0, keepdims=True)
            ds_t = p_t * (dp_t - delta)
            sink_terms = p_s * delta
            for half in range(2):
                cols = slice(BLOCK * half, BLOCK * (half + 1))
                dsink = _lane_put(dsink, -jnp.sum(sink_terms[:, cols], axis=1, keepdims=True), 2 * j + half)
                dt_ref[0, 2 * j + half] += ds_t[:, cols]
            ds_tb = ds_t.astype(BF16)
            dq_t = _dot(k_dup_t[kh], ds_tb)
            dqkv_ref[:, tile] = (jnp.where(top, dq_t[:, :BLOCK], dq_t[:, BLOCK:]).T * scale).astype(BF16)
            dk_part, dv_part = _dot(ds_tb, qm2), _dot(p_t.astype(BF16), dom2)
            dk_acc[kh] = dk_part if dk_acc[kh] is None else dk_acc[kh] + dk_part
            dv_acc[kh] = dv_part if dv_acc[kh] is None else dv_acc[kh] + dv_part
        dsink_ref[...] += dsink
        folded = [a + pltpu.roll(a, ATT_HEADDIM, 1) for a in dk_acc + dv_acc]
        dkv = jnp.concatenate([jnp.where(first_k, folded[0], folded[1]) * scale,
                               jnp.where(first_k, folded[2], folded[3])], axis=1)
        meta_scr[...] += dkv[:BLOCK, :]
        own = dkv[2 * BLOCK:, :] + carry_scr[...]
        carry_scr[...] = dkv[BLOCK:2 * BLOCK, :]

        @pl.when(blk > 0)
        def _():
            dqkv_ref[:, ATT_Q:] = own.astype(BF16)

        @pl.when(blk == 0)
        def _():
            dqkv_ref[:, ATT_Q:] = (own + meta_scr[...]).astype(BF16)

    return pl.pallas_call(
        body, name="attn_bwd", grid=(nb,),
        in_specs=[q_s, q_s, cur_s, prev_s, meta_s, t_s, sink_s],
        out_specs=[dqkv_s, t_s, sink_s],
        out_shape=[jax.ShapeDtypeStruct((n, ATT_Q + 2 * ATT_KV), BF16),
                   jax.ShapeDtypeStruct((3, ATT_HEADS, N_KEYS, BLOCK), F32),
                   jax.ShapeDtypeStruct((1, LANES), F32)],
        scratch_shapes=[pltpu.VMEM((BLOCK, 2 * ATT_KV), F32), pltpu.VMEM((BLOCK, 2 * ATT_KV), F32)],
        compiler_params=_cparams("arbitrary"))(datt, qkv, qkv, qkv, qkv, tables, sinks)


def _merge_out_fwd(gates, y_ssd, y_att, gate_b, w_out, h, next_norm_w):
    n = gates.shape[0]
    tm = _row_tile(n, 416)

    def body(gs_ref, ga_ref, ys_ref, ya_ref, gb_ref, w_ref, h_ref, nw_ref, m_ref, o_ref, u_ref):
        merged = (_sigmoid(gs_ref[...] + gb_ref[0:1, :]) * ys_ref[...]
                  + _sigmoid(ga_ref[...] + gb_ref[1:2, :]) * ya_ref[...]).astype(BF16)
        m_ref[...] = merged
        row = pl.program_id(0) * tm + lax.broadcasted_iota(jnp.int32, (tm, 1), 0)
        hv = jnp.where(row >= PAD, _dot(merged, w_ref[...]), 0.0) + h_ref[...]
        o_ref[...] = hv
        r = lax.rsqrt(jnp.mean(hv * hv, axis=-1, keepdims=True) + EPS)
        u_ref[...] = (hv * r * nw_ref[...]).astype(BF16)

    row = pl.BlockSpec((tm, D_MODEL), lambda i: (i, 0))
    return pl.pallas_call(
        body, name="merge_out_fwd", grid=(n // tm,),
        in_specs=[row, pl.BlockSpec((tm, D_MODEL), lambda i: (i, 1)), row, row,
                  pl.BlockSpec((2, D_MODEL), lambda i: (0, 0)), pl.BlockSpec((D_MODEL, D_MODEL), lambda i: (0, 0)), row,
                  pl.BlockSpec((1, D_MODEL), lambda i: (0, 0))],
        out_specs=[row, row, row],
        out_shape=[jax.ShapeDtypeStruct((n, D_MODEL), BF16), jax.ShapeDtypeStruct((n, D_MODEL), F32),
                   jax.ShapeDtypeStruct((n, D_MODEL), BF16)],
        compiler_params=_cparams("parallel"))(gates, gates, y_ssd, y_att, gate_b, w_out, h, next_norm_w)


def _merge_out_bwd(dh, w_out, gates, y_ssd, y_att, gate_b):
    n = gates.shape[0]
    tm = _row_tile(n, 416)

    def body(dh_ref, w_ref, gs_ref, ga_ref, ys_ref, ya_ref, gb_ref, dys_ref, dya_ref, dg_ref, dgb_ref):
        i = pl.program_id(0)
        row = i * tm + lax.broadcasted_iota(jnp.int32, (tm, 1), 0)
        dmv = jnp.where(row >= PAD, _dot_nt(dh_ref[...].astype(BF16), w_ref[...]), 0.0)
        ss =_sigmoid(gs_ref[...] + gb_ref[0:1, :])
        sa = _sigmoid(ga_ref[...] + gb_ref[1:2, :])
        dys_ref[...] = (dmv * ss).astype(BF16)
        dya_ref[...] = (dmv * sa).astype(BF16)
        dgs = dmv * ys_ref[...] * ss * (1.0 - ss)
        dga = dmv * ya_ref[...] * sa * (1.0 - sa)
        dg_ref[:, :D_MODEL] = dgs.astype(BF16)
        dg_ref[:, D_MODEL:] = dga.astype(BF16)
        part = jnp.concatenate([jnp.sum(dgs, axis=0, keepdims=True), jnp.sum(dga, axis=0, keepdims=True)], axis=0)

        @pl.when(i == 0)
        def _():
            dgb_ref[...] = part

        @pl.when(i > 0)
        def _():
            dgb_ref[...] += part

    row = pl.BlockSpec((tm, D_MODEL), lambda i: (i, 0))
    gb = pl.BlockSpec((2, D_MODEL), lambda i: (0, 0))
    return pl.pallas_call(
        body, name="merge_out_bwd", grid=(n // tm,),
        in_specs=[row, pl.BlockSpec((D_MODEL, D_MODEL), lambda i: (0, 0)), row,
                  pl.BlockSpec((tm, D_MODEL), lambda i: (i, 1)), row, row, gb],
        out_specs=[row, row, pl.BlockSpec((tm, 2 * D_MODEL), lambda i: (i, 0)), gb],
        out_shape=[jax.ShapeDtypeStruct((n, D_MODEL), BF16), jax.ShapeDtypeStruct((n, D_MODEL), BF16),
                   jax.ShapeDtypeStruct((n, 2 * D_MODEL), BF16), jax.ShapeDtypeStruct((2, D_MODEL), F32)],
        compiler_params=_cparams("arbitrary"))(dh, w_out, gates, gates, y_ssd, y_att, gate_b)


def _col_move(srcs, outs, pieces, *, name):
    rows = srcs[0].shape[-2]
    tr = _row_tile(rows, 128)
    n_src = len(srcs)
    covered = [sum(p[6] for p in pieces if p[0] == o) for o in range(len(outs))]
    total = [int(np.prod(shp)) // rows for shp, _ in outs]

    def body(*refs):
        in_refs, out_refs = refs[:n_src], refs[n_src:]
        for o, ref in enumerate(out_refs):
            if covered[o] != total[o]:
                ref[...] = jnp.zeros_like(ref)
        for o, ol, oc, s, sl, sc, width in pieces:
            val = in_refs[s][:, sc:sc + width] if sl is None else in_refs[s][sl, :, sc:sc + width]
            val = val.astype(outs[o][1])
            if ol is None:
                out_refs[o][:, oc:oc + width] = val
            else:
                out_refs[o][ol, :, oc:oc + width] = val

    def spec(shape):
        if len(shape) == 2:
            return pl.BlockSpec((tr, shape[1]), lambda i: (i, 0))
        return pl.BlockSpec((shape[0], tr, shape[2]), lambda i: (0, i, 0))

    return pl.pallas_call(
        body, name=name, grid=(rows // tr,),
        in_specs=[spec(a.shape) for a in srcs], out_specs=[spec(shp) for shp, _ in outs],
        out_shape=[jax.ShapeDtypeStruct(shp, dt) for shp, dt in outs],
        compiler_params=_cparams("parallel"))(*srcs)


def _shard_pieces(seg_ranges, shard_w):
    out = []
    for seg, runs in enumerate(seg_ranges):
        for g0, width, s0 in runs:
            done = 0
            while done < width:
                dev, col = divmod(g0 + done, shard_w)
                take = min(width - done, shard_w - col)
                out.append((seg, s0 + done, dev, col, take))
                done += take
    return out


_CHIP_RELATIONS = [(1, 0, 0), (0, 1, 0), (1, 1, 0)]
N_CHIPS = 4


class _CommPlan:
    def __init__(self, arrays, out_shape, scratch_shapes, phases):
        self.arrays, self.out_shape, self.scratch_shapes, self.phases = arrays, out_shape, scratch_shapes, phases


def _gather_plan(arrays):
    n_arr = len(arrays)
    n_chips = len(_CHIP_RELATIONS)
    n_pair = 1 + 2 * n_chips

    def where():
        x, y, c = lax.axis_index("x"), lax.axis_index("y"), lax.axis_index("c")
        return x, y, c, (x, y, 1 - c), [(x ^ dx, y ^ dy) for dx, dy, _ in _CHIP_RELATIONS]

    def copy(outs, sems, a, k, block, to, src=None):
        slot = outs[a].at[2 * block[0] + block[1], block[2]]
        return pltpu.make_async_remote_copy(
            src_ref=slot if src is None else src, dst_ref=slot, send_sem=sems[0].at[a * n_pair + k],
            recv_sem=sems[1].at[a * n_pair + k], device_id=to, device_id_type=MESH)

    def mine(ins, outs, sems, a, x, y, c):
        return pltpu.make_async_copy(ins[a], outs[a].at[2 * x + y, c], sems[2].at[a])

    def first_copies(ins, outs, sems, a, x, y, c, sibling, chips):
        return ([copy(outs, sems, a, 0, (x, y, c), sibling, src=ins[a])]
                + [copy(outs, sems, a, 1 + j, (x, y, c), (*chip, c), src=ins[a]) for j, chip in enumerate(chips)])

    def start(ins, outs, sems):
        x, y, c, sibling, chips = where()
        for a in range(n_arr):
            mine(ins, outs, sems, a, x, y, c).start()
            for cp in first_copies(ins, outs, sems, a, x, y, c, sibling, chips):
                cp.start()

    def pass_on(ins, outs, sems):
        x, y, c, sibling, chips = where()
        for j, chip in enumerate(chips):
            for a in range(n_arr):
                copy(outs, sems, a, 1 + j, (*chip, c), (x, y, c)).wait_recv()
                copy(outs, sems, a, 1 + n_chips + j, (*chip, c), sibling).start()

    def finish(ins, outs, sems):
        x, y, c, sibling, chips = where()
        for a in range(n_arr):
            copy(outs, sems, a, 0, (x, y, 1 - c), (x, y, c)).wait_recv()
            for j, chip in enumerate(chips):
                copy(outs, sems, a, 1 + n_chips + j, (*chip, 1 - c), (x, y, c)).wait_recv()
        for a in range(n_arr):
            for cp in first_copies(ins, outs, sems, a, x, y, c, sibling, chips):
                cp.wait_send()
            for j, chip in enumerate(chips):
                copy(outs, sems, a, 1 + n_chips + j, (*chip, c), sibling).wait_send()
            mine(ins, outs, sems, a, x, y, c).wait()

    return _CommPlan(
        arrays, [jax.ShapeDtypeStruct((N_CHIPS, 2) + a.shape, a.dtype) for a in arrays],
        [pltpu.SemaphoreType.DMA((n_arr * n_pair,)), pltpu.SemaphoreType.DMA((n_arr * n_pair,)),
         pltpu.SemaphoreType.DMA((n_arr,))],
        (start, pass_on, finish))


_ALL_RELATIONS = [(dx, dy, dc) for dx in (0, 1) for dy in (0, 1) for dc in (0, 1)][1:]


def _all_to_all_plan(arrays, scatter=None):
    n_arr = len(arrays)
    n_rel = len(_ALL_RELATIONS)
    scatter = scatter or [True] * n_arr

    def block(ins, a, p):
        return ins[a].at[p] if scatter[a] else ins[a]

    def local_copies(ins, outs, sems):
        me = 4 * lax.axis_index("x") + 2 * lax.axis_index("y") + lax.axis_index("c")
        return [pltpu.make_async_copy(block(ins, a, me), outs[a].at[me], sems[2].at[a]) for a in range(n_arr)]

    def remote_copies(ins, outs, sems, arrivals):
        x, y, c = lax.axis_index("x"), lax.axis_index("y"), lax.axis_index("c")
        me = 4 * x + 2 * y + c
        out = []
        for k, (dx, dy, dc) in enumerate(_ALL_RELATIONS):
            px, py, pc = x ^ dx, y ^ dy, c ^ dc
            peer = 4 * px + 2 * py + pc
            for a in range(n_arr):
                out.append(pltpu.make_async_remote_copy(
                    src_ref=block(ins, a, peer), dst_ref=outs[a].at[peer if arrivals else me],
                    send_sem=sems[0].at[a * n_rel + k], recv_sem=sems[1].at[a * n_rel + k],
                    device_id=(x, y, c) if arrivals else (px, py, pc), device_id_type=MESH))
        return out

    def start(ins, outs, sems):
        for cp in local_copies(ins, outs, sems) + remote_copies(ins, outs, sems, False):
            cp.start()

    def pass_on(ins, outs, sems):
        pass

    def finish(ins, outs, sems):
        for send in remote_copies(ins, outs, sems, False):
            send.wait_send()
        for arrival in remote_copies(ins, outs, sems, True):
            arrival.wait_recv()
        for cp in local_copies(ins, outs, sems):
            cp.wait()

    return _CommPlan(
        arrays, [jax.ShapeDtypeStruct(a.shape if s else (N_DEV,) + a.shape, a.dtype) for a, s in zip(arrays, scatter)],
        [pltpu.SemaphoreType.DMA((n_arr * n_rel,)), pltpu.SemaphoreType.DMA((n_arr * n_rel,)),
         pltpu.SemaphoreType.DMA((n_arr,))],
        (start, pass_on, finish))


def _adamw(w, gslots, m, v, *, name, side=None):
    rows, cols = w.shape
    n_slots = gslots.shape[0]
    tr = _row_tile(rows, 128) if rows % 16 == 0 else rows

    def body(w_ref, g_ref, m_ref, v_ref, go_ref, d_ref, mo_ref, vo_ref):
        g = g_ref[0].astype(F32)
        for s in range(1, n_slots):
            g = g + g_ref[s].astype(F32)
        mn = ADAM_B1 * m_ref[...] + (1.0 - ADAM_B1) * g
        vn = ADAM_B2 * v_ref[...] + (1.0 - ADAM_B2) * (g * g)
        go_ref[...] = g
        mo_ref[...] = mn
        vo_ref[...] = vn
        m_hat = mn / (1.0 - ADAM_B1 ** ADAM_STEP)
        v_hat = vn / (1.0 - ADAM_B2 ** ADAM_STEP)
        d_ref[...] = -ADAM_LR * (m_hat / (jnp.sqrt(v_hat) + ADAM_EPS) + ADAM_WD * w_ref[...])

    blk = pl.BlockSpec((tr, cols), lambda i: (i, 0))
    shp = jax.ShapeDtypeStruct((rows, cols), F32)
    outs, carried = _call_with_side(
        body, side, name=name, grid=(rows // tr,),
        in_specs=[blk, pl.BlockSpec((n_slots, tr, cols), lambda i: (0, i, 0)), blk, blk],
        out_specs=[blk] * 4, out_shape=[shp] * 4, scratch_shapes=[], args=(w, gslots, m, v), semantics=("parallel",))
    return outs if side is None else (outs, carried)


_BIG = ("w_in", "w_ssd_branch", "w_attn_branch", "w_out", "w_ffn_in", "w_ffn_out")
_SMALL_SHARDED = ("meta_tokens", "ssd_conv_w", "gate_b", "ffn_conv_w")
_SMALL_REPLICATED = ("norm_mix_w", "ssd_conv_b", "ssd_dt_bias", "ssd_a_log", "ssd_d", "ssd_norm_w", "attn_sinks",
                     "rel_bias", "norm_ffn_w", "ffn_conv_b", "norm_final_w")
_WEIGHTS = ("meta_tokens", "norm_mix_w", "w_in", "ssd_conv_w", "ssd_conv_b", "ssd_dt_bias", "ssd_a_log", "ssd_d",
            "ssd_norm_w", "w_ssd_branch", "w_attn_branch", "attn_sinks", "rel_bias", "gate_b", "w_out", "norm_ffn_w",
            "w_ffn_in", "ffn_conv_w", "ffn_conv_b", "w_ffn_out", "norm_final_w")
_ROW_SHARDED = ("w_ssd_branch", "w_attn_branch", "w_out", "w_ffn_out")
_COL_SHARDED = ("w_in", "w_ffn_in", "meta_tokens", "ssd_conv_w", "gate_b", "ffn_conv_w")
_IN_SEGS = (("z", SSD_INNER), ("xbc", SSD_XBC), ("dt", SSD_HEADS), ("qkv", ATT_Q + 2 * ATT_KV), ("g", 2 * D_MODEL))


def _pack_rows(flat_parts, width, row_mult):
    flat = jnp.concatenate([p.reshape(-1) for p in flat_parts])
    pad = (-flat.shape[0]) % (width * row_mult)
    if pad:
        flat = jnp.concatenate([flat, jnp.zeros((pad,), flat.dtype)])
    return flat.reshape(-1, width)


def _unpack(flat, shapes):
    out, off = [], 0
    for shp in shapes:
        size = int(np.prod(shp))
        out.append(flat[off:off + size].reshape(shp))
        off += size
    return out


def _gather_full(stack, name, shard_shape):
    if name in _COL_SHARDED:
        return jnp.transpose(stack, (1, 0, 2)).reshape(shard_shape[0], N_DEV * shard_shape[1])
    return stack.reshape(N_DEV * shard_shape[0], shard_shape[1])


_IN_SEG_W = {"z": SSD_INNER, "xbc": SSD_XBC, "dt": DT_W, "qkv": ATT_Q + 2 * ATT_KV, "g": 2 * D_MODEL}
_IN_SHARD_W = (SSD_INNER + SSD_XBC + SSD_HEADS + ATT_Q + 2 * ATT_KV + 2 * D_MODEL) // N_DEV
_FFN_SHARD_W = 2 * D_FF // N_DEV


def _in_seg_runs():
    runs, off = [], 0
    for nm, width in _IN_SEGS:
        if nm == "dt":
            runs.append([(off + SSD_HPG * g, SSD_HPG, LANES * g) for g in range(SSD_GROUPS)])
        else:
            runs.append([(off, width, 0)])
        off += width
    return runs


def _w_in_to_segments(stack):
    pieces = [(seg, None, scol, 0, dev, col, w) for seg, scol, dev, col, w in _shard_pieces(_in_seg_runs(), _IN_SHARD_W)]
    outs = [((D_MODEL, _IN_SEG_W[nm]), stack.dtype) for nm, _ in _IN_SEGS]
    return dict(zip([nm for nm, _ in _IN_SEGS], _col_move([stack], outs, pieces, name="w_in_segments")))


def _segments_to_w_in_shards(seg_grads):
    pieces = [(0, dev, col, seg, None, scol, w) for seg, scol, dev, col, w in _shard_pieces(_in_seg_runs(), _IN_SHARD_W)]
    return _col_move(seg_grads, [((N_DEV, D_MODEL, _IN_SHARD_W), seg_grads[0].dtype)], pieces, name="g_w_in_shards")[0]


def _ffn_in_from_shards(stack):
    pieces = [(0, None, scol, 0, dev, col, w)
              for _, scol, dev, col, w in _shard_pieces([[(0, 2 * D_FF, 0)]], _FFN_SHARD_W)]
    return _col_move([stack], [((D_MODEL, 2 * D_FF), stack.dtype)], pieces, name="w_ffn_in_full")[0]


def _ffn_in_to_shards(g_up, g_gate):
    pieces = [(0, dev, col, seg, None, scol, w)
              for seg, scol, dev, col, w in _shard_pieces([[(0, D_FF, 0)], [(D_FF, D_FF, 0)]], _FFN_SHARD_W)]
    return _col_move([g_up, g_gate], [((N_DEV, D_MODEL, _FFN_SHARD_W), g_up.dtype)], pieces, name="g_w_ffn_in_shards")[0]


def _dt_spread(w_dt):
    k = w_dt.shape[0]
    w4 = w_dt.reshape(k, SSD_GROUPS, SSD_HPG)
    return jnp.pad(w4, ((0, 0), (0, 0), (0, LANES - SSD_HPG))).reshape(k, DT_W)


def _dt_gather(w_wide):
    k = w_wide.shape[0]
    return w_wide.reshape(k, SSD_GROUPS, LANES)[:, :, :SSD_HPG].reshape(k, SSD_HEADS)


class _LateExchanges:
    def __init__(self, two_d, shape2):
        self.two_d, self.shape2 = two_d, shape2
        self.early_grads_received = None
        self.w_in_grads_received = None

    def row_pack(self, tree):
        return jnp.concatenate([tree[k] for k in _ROW_SHARDED], axis=0)

    def late_weights_plan(self):
        return _gather_plan([self.two_d["w_ffn_in"].astype(BF16), self.row_pack(self.two_d).astype(BF16)])

    def late_weights(self, gathered):
        w_ffn_in_all, rows_all = [g.reshape((N_DEV,) + g.shape[2:]) for g in gathered]
        out = {"w_ffn_in": _ffn_in_from_shards(w_ffn_in_all)}
        off = 0
        for k in _ROW_SHARDED:
            r = self.shape2[k][0]
            out[k] = rows_all[:, off:off + r].reshape(N_DEV * r, D_MODEL)
            off += r
        return out

    def early_grads_plan(self, grads):
        rows_send = jnp.concatenate([grads[k].reshape(N_DEV, self.shape2[k][0], D_MODEL) for k in _ROW_SHARDED], axis=1)
        return _all_to_all_plan([_ffn_in_to_shards(*grads["w_ffn_in"]), rows_send])

    def w_in_grads_plan(self, seg_grads):
        return _all_to_all_plan([_segments_to_w_in_shards(seg_grads)])


def _local_step(x, target, w, exchanges=None):
    h0 = jnp.concatenate([jnp.zeros((PAD, D_MODEL), F32), w["meta_tokens"], x], axis=0)
    segs = w["in_segs"]

    dtb = _dt_spread(w["ssd_dt_bias"])
    alog = _dt_spread(w["ssd_a_log"])
    dskip_w = jnp.repeat(w["ssd_d"], SSD_HEADDIM, axis=1)
    sinks = jnp.pad(w["attn_sinks"], ((0, 0), (0, LANES - ATT_HEADS)))
    onehot_t = w["onehot_t"] if "onehot_t" in w else _onehot_t()
    flat_tables = w["bias_tables"] if "bias_tables" in w else _bias_tables(w["rel_bias"].T, onehot_t)[0]
    tables = jnp.transpose(flat_tables.reshape(ATT_HEADS, 3, N_KEYS, BLOCK), (1, 0, 2, 3))

    u = _rms_fwd(h0, w["norm_mix_w"], name="rms_mix_fwd")
    z = _mm(u, segs["z"], name="in_z")
    xbc, pre = _mm_conv_fwd(u, segs["xbc"], w["ssd_conv_w"], w["ssd_conv_b"], name="in_xbc_conv_fwd")
    dt_raw = _mm(u, segs["dt"], name="in_dt")
    qkv = _mm(u, segs["qkv"], out_dtype=BF16, name="in_qkv")
    gates = _mm(u, segs["g"], name="in_g")
    (y, yn, hsave), gathered = _ssd_fwd(pre, dt_raw, z, dtb, alog, dskip_w, w["ssd_norm_w"],
                                        side=None if exchanges is None else exchanges.late_weights_plan())
    if exchanges is not None:
        w = {**w, **exchanges.late_weights(gathered)}
    w_ffn_up, w_ffn_gate = w["w_ffn_in"][:, :D_FF], w["w_ffn_in"][:, D_FF:]
    y_ssd = _mm(yn, w["w_ssd_branch"], out_dtype=BF16, name="ssd_out")
    att = _attn_fwd(qkv, tables, sinks)
    y_att = _mm(att, w["w_attn_branch"], out_dtype=BF16, name="att_out")
    merged, h1, u2 = _merge_out_fwd(gates, y_ssd, y_att, w["gate_b"], w["w_out"], h0, w["norm_ffn_w"])
    x_up, x_gate, hid_up, hid_gate, act = _ffn_in_act_fwd(u2, w["w_ffn_in"], w["ffn_conv_w"], w["ffn_conv_b"])
    dh2, dh2_b, loss_row, g_norm_final = _ffn_out_loss(act, w["w_ffn_out"], h1, w["norm_final_w"], target)

    grads = {"norm_final_w": g_norm_final}
    grads["w_ffn_out"] = _mm(act, dh2_b, ta=True, mask=True, out_dtype=BF16, name="g_w_ffn_out")
    dx_up, dx_gate, dcw_up, dcw_gate, dcb_up, dcb_gate = _ffn_out_act_bwd(
        dh2_b, w["w_ffn_out"], hid_up, hid_gate, x_up, x_gate, w["ffn_conv_w"])
    grads["ffn_conv_w"] = jnp.concatenate([dcw_up, dcw_gate], axis=1)
    grads["ffn_conv_b"] = jnp.concatenate([dcb_up, dcb_gate], axis=1)
    (dh1, grads["norm_ffn_w"]), _ = _mm_rms_bwd([(dx_up, w_ffn_up), (dx_gate, w_ffn_gate)], h1, w["norm_ffn_w"], dh2,
                                                name="d_u2_rms_bwd")
    grads["w_ffn_in"] = (_mm(u2, dx_up, ta=True, out_dtype=BF16, name="g_w_ffn_up"),
                         _mm(u2, dx_gate, ta=True, out_dtype=BF16, name="g_w_ffn_gate"))

    grads["w_out"] = _mm(merged, dh1, ta=True, mask=True, out_dtype=BF16, name="g_w_out")
    dy_ssd, dy_att, dgates, grads["gate_b"] = _merge_out_bwd(dh1, w["w_out"], gates, y_ssd, y_att, w["gate_b"])
    dyn = _mm(dy_ssd, w["w_ssd_branch"], tb=True, name="d_yn")
    grads["w_ssd_branch"] = _mm(yn, dy_ssd, ta=True, out_dtype=BF16, name="g_w_ssd")
    datt = _mm(dy_att, w["w_attn_branch"], tb=True, out_dtype=BF16, name="d_att")
    grads["w_attn_branch"] = _mm(att, dy_att, ta=True, out_dtype=BF16, name="g_w_att")
    (dz, dpxs, dpb, dpc, ddt, grads["ssd_norm_w"], g_dtb, g_alog, g_dskip), received = _ssd_bwd(
        dyn, y, z, pre, dt_raw, hsave, dtb, alog, dskip_w, w["ssd_norm_w"],
        side=None if exchanges is None else exchanges.early_grads_plan(grads))
    if exchanges is not None:
        exchanges.early_grads_received = received
    grads["ssd_dt_bias"] = _dt_gather(g_dtb)
    grads["ssd_a_log"] = _dt_gather(g_alog)
    grads["ssd_d"] = _dt_gather(g_dskip)
    conv_g = _conv_bwd(dpxs, xbc, w["ssd_conv_w"], name="ssd_conv_bwd_x")
    conv_g = _conv_bwd(dpb, xbc, w["ssd_conv_w"], name="ssd_conv_bwd_b", col0=SSD_INNER, into=conv_g)
    dxbc, grads["ssd_conv_w"], grads["ssd_conv_b"] = _conv_bwd(
        dpc, xbc, w["ssd_conv_w"], name="ssd_conv_bwd_c", col0=SSD_INNER + SSD_BC, into=conv_g)
    dqkv, d_tables, d_sinks = _attn_bwd(datt, qkv, tables, sinks)
    grads["attn_sinks"] = d_sinks[:, :ATT_HEADS]
    dtab = jnp.transpose(d_tables, (1, 0, 2, 3)).reshape(ATT_HEADS, NT_ALL)
    grads["rel_bias"] = _bias_grad(dtab, onehot_t).T
    dsegs = {"z": dz, "xbc": dxbc, "dt": ddt, "qkv": dqkv, "g": dgates}
    grads["in_segs"] = [_mm(u, dsegs[nm], ta=True, out_dtype=BF16, name="g_w_in_" + nm) for nm, _ in _IN_SEGS]
    (dh0, grads["norm_mix_w"]), received = _mm_rms_bwd(
        [(dsegs[nm], segs[nm]) for nm, _ in _IN_SEGS], h0, w["norm_mix_w"], dh1, name="d_u_rms_bwd",
        side=None if exchanges is None else exchanges.w_in_grads_plan(grads["in_segs"]))
    if exchanges is not None:
        exchanges.w_in_grads_received = received[0]
    grads["meta_tokens"] = dh0[PAD:BLOCK]
    return loss_row[0, 0], dh0[BLOCK:], grads


def kernel(x, meta_tokens, norm_mix_w, w_in, ssd_conv_w, ssd_conv_b, ssd_dt_bias, ssd_a_log, ssd_d, ssd_norm_w, w_ssd_branch, w_attn_branch, attn_sinks, rel_bias, gate_b, w_out, norm_ffn_w, w_ffn_in, ffn_conv_w, ffn_conv_b, w_ffn_out, norm_final_w, loss_target, m_meta_tokens, m_norm_mix_w, m_w_in, m_ssd_conv_w, m_ssd_conv_b, m_ssd_dt_bias, m_ssd_a_log, m_ssd_d, m_ssd_norm_w, m_w_ssd_branch, m_w_attn_branch, m_attn_sinks, m_rel_bias, m_gate_b, m_w_out, m_norm_ffn_w, m_w_ffn_in, m_ffn_conv_w, m_ffn_conv_b, m_w_ffn_out, m_norm_final_w, v_meta_tokens, v_norm_mix_w, v_w_in, v_ssd_conv_w, v_ssd_conv_b, v_ssd_dt_bias, v_ssd_a_log, v_ssd_d, v_ssd_norm_w, v_w_ssd_branch, v_w_attn_branch, v_attn_sinks, v_rel_bias, v_gate_b, v_w_out, v_norm_ffn_w, v_w_ffn_in, v_ffn_conv_w, v_ffn_conv_b, v_w_ffn_out, v_norm_final_w):
    shard = dict(meta_tokens=meta_tokens, norm_mix_w=norm_mix_w, w_in=w_in, ssd_conv_w=ssd_conv_w,
                 ssd_conv_b=ssd_conv_b, ssd_dt_bias=ssd_dt_bias, ssd_a_log=ssd_a_log, ssd_d=ssd_d,
                 ssd_norm_w=ssd_norm_w, w_ssd_branch=w_ssd_branch, w_attn_branch=w_attn_branch,
                 attn_sinks=attn_sinks, rel_bias=rel_bias, gate_b=gate_b, w_out=w_out, norm_ffn_w=norm_ffn_w,
                 w_ffn_in=w_ffn_in, ffn_conv_w=ffn_conv_w, ffn_conv_b=ffn_conv_b, w_ffn_out=w_ffn_out,
                 norm_final_w=norm_final_w)
    mom_m = dict(zip(_WEIGHTS, (m_meta_tokens, m_norm_mix_w, m_w_in, m_ssd_conv_w, m_ssd_conv_b, m_ssd_dt_bias,
                                m_ssd_a_log, m_ssd_d, m_ssd_norm_w, m_w_ssd_branch, m_w_attn_branch, m_attn_sinks,
                                m_rel_bias, m_gate_b, m_w_out, m_norm_ffn_w, m_w_ffn_in, m_ffn_conv_w, m_ffn_conv_b,
                                m_w_ffn_out, m_norm_final_w)))
    mom_v = dict(zip(_WEIGHTS, (v_meta_tokens, v_norm_mix_w, v_w_in, v_ssd_conv_w, v_ssd_conv_b, v_ssd_dt_bias,
                                v_ssd_a_log, v_ssd_d, v_ssd_norm_w, v_w_ssd_branch, v_w_attn_branch, v_attn_sinks,
                                v_rel_bias, v_gate_b, v_w_out, v_norm_ffn_w, v_w_ffn_in, v_ffn_conv_w, v_ffn_conv_b,
                                v_w_ffn_out, v_norm_final_w)))
    orig_shape = {k: a.shape for k, a in shard.items()}
    two_d = {k: a.reshape(a.shape[-2:]) if a.ndim >= 2 else a.reshape(1, -1) for k, a in shard.items()}
    shape2 = {k: a.shape for k, a in two_d.items()}

    def as2d(tree):
        return {k: tree[k].reshape(shape2[k]) for k in _WEIGHTS}

    mom_m, mom_v = as2d(mom_m), as2d(mom_v)

    exchanges = _LateExchanges(two_d, shape2)
    row_pack = exchanges.row_pack
    small_pack = _pack_rows([two_d[k] for k in _SMALL_SHARDED], LANES, SMALL_ROW_MULT)
    onehot_t = _onehot_t()
    gather = [two_d["w_in"].astype(BF16), small_pack]
    flat_tables, gathered = _bias_tables(two_d["rel_bias"].T, onehot_t, side=_gather_plan(gather))
    w_in_all, small_all = [g.reshape((N_DEV,) + a.shape) for g, a in zip(gathered, gather)]
    full = {k: two_d[k] for k in _SMALL_REPLICATED}
    full["onehot_t"], full["bias_tables"] = onehot_t, flat_tables
    full["in_segs"] = _w_in_to_segments(w_in_all)
    small_flat = small_all.reshape(N_DEV, -1)
    off = 0
    for k in _SMALL_SHARDED:
        size = int(np.prod(shape2[k]))
        full[k] = _gather_full(small_flat[:, off:off + size].reshape((N_DEV,) + shape2[k]), k, shape2[k])
        off += size

    loss_local, grad_x, grads = _local_step(x[0], loss_target[0], full, exchanges)

    small_names = _SMALL_SHARDED + _SMALL_REPLICATED
    small_send = _pack_rows([grads[k] for k in small_names] + [loss_local.reshape(1)], LANES, SMALL_ROW_MULT)
    in_recv = exchanges.w_in_grads_received
    ffn_recv, rows_recv = exchanges.early_grads_received

    w_in_out, (small_recv,) = _adamw(two_d["w_in"], in_recv, mom_m["w_in"], mom_v["w_in"], name="adamw_w_in",
                                     side=_all_to_all_plan([small_send], [False]))
    big = {"w_in": w_in_out,
           "w_ffn_in": _adamw(two_d["w_ffn_in"], ffn_recv, mom_m["w_ffn_in"], mom_v["w_ffn_in"], name="adamw_w_ffn_in")}
    rows_out = _adamw(row_pack(two_d), rows_recv, row_pack(mom_m), row_pack(mom_v), name="adamw_rows")
    off = 0
    for k in _ROW_SHARDED:
        r = shape2[k][0]
        big[k] = [a[off:off + r] for a in rows_out]
        off += r
    me =4 * lax.axis_index("x") + 2 * lax.axis_index("y") + lax.axis_index("c")
    small_full_shapes = [grads[k].shape for k in small_names]
    n_small = sum(int(np.prod(s)) for s in small_full_shapes)

    def packed_small(tree):
        parts = []
        for k in small_names:
            a = tree[k]
            if k in _SMALL_SHARDED:
                fullw = jnp.zeros(grads[k].shape, F32)
                a = lax.dynamic_update_slice(fullw, a, (0, me * a.shape[1]))
            parts.append(a)
        return _pack_rows(parts + [jnp.zeros((1,), F32)], LANES, SMALL_ROW_MULT)

    g_small, d_small, m_small, v_small = _adamw(packed_small(two_d), small_recv, packed_small(mom_m),
                                                packed_small(mom_v), name="adamw_small")

    def unpack_all(which, small):
        out = {k: big[k][which] for k in _BIG}
        flat = small.reshape(-1)
        for k, a in zip(small_names, _unpack(flat, small_full_shapes)):
            if k in _SMALL_SHARDED:
                a = lax.dynamic_slice(a, (0, me * shape2[k][1]), shape2[k])
            out[k] = a
        return out, flat[n_small]

    g_all, loss = unpack_all(0, g_small)
    d_all, _ = unpack_all(1, d_small)
    m_all, _ = unpack_all(2, m_small)
    v_all, _ = unpack_all(3, v_small)

    def final(tree):
        return [tree[k].reshape(orig_shape[k]) for k in _WEIGHTS]

    return (loss, grad_x[None], *final(g_all), *final(d_all), *final(m_all), *final(v_all))
```
